```python
import jax, jax.numpy as jnp
from jax import lax
import numpy as np

D_MODEL = 1024
BATCH = 8
SEQ = 8192
DEPTH = 2

HEAD_DIM = 64
POOL_WIDTH = D_MODEL // 4
POOL_WINDOWS = (2, 4, 8, 16)
N_POOL_GROUPS = len(POOL_WINDOWS)
POOL_GC = POOL_WIDTH // N_POOL_GROUPS
ATTN_WIDTH = D_MODEL - POOL_WIDTH
N_ATTN_HEADS = ATTN_WIDTH // HEAD_DIM
DIL_PATTERNS = ((128, 1), (512, 4), (2048, 16))
HEADS_PER_PATTERN = N_ATTN_HEADS // len(DIL_PATTERNS)
ROT_DIM = HEAD_DIM // 4
ROPE_THETA = 500000.0
BLK = 128
D_FF = 4 * D_MODEL
PLE_DIM = 256
EPS = 1e-6

kernel_name = "hybrid_pool_dilated_attn_block"


def rmsnorm(x, g):
    xf = x.astype(jnp.float32)
    y = xf * lax.rsqrt(jnp.mean(xf * xf, axis=-1, keepdims=True) + EPS)
    return (y * g.astype(jnp.float32)).astype(x.dtype)


def rotary_tables(positions, dtype):
    inv_freq = ROPE_THETA ** (-jnp.arange(0, ROT_DIM, 2, dtype=jnp.float32) / ROT_DIM)
    ang = positions.astype(jnp.float32)[..., None] * inv_freq
    return jnp.cos(ang)[:, :, None, :].astype(dtype), jnp.sin(ang)[:, :, None, :].astype(dtype)


def apply_partial_rotary(x, cos, sin):
    half = ROT_DIM // 2
    x1 = x[..., :half]
    x2 = x[..., half:ROT_DIM]
    rot = jnp.concatenate([x1 * cos - x2 * sin, x2 * cos + x1 * sin], axis=-1)
    return jnp.concatenate([rot, x[..., ROT_DIM:]], axis=-1)


def pool_mixer(u, w, scale):
    B, S, _ = u.shape
    ug = u.reshape(B, S, N_POOL_GROUPS, POOL_GC).astype(jnp.float32)
    c = lax.cumsum(ug, axis=1)
    c0 = jnp.pad(c, ((0, 0), (1, 0), (0, 0), (0, 0)))
    t = jnp.arange(S, dtype=jnp.int32)
    win = jnp.array(POOL_WINDOWS, dtype=jnp.int32)
    lo = jnp.maximum(t[:, None] + 1 - win[None, :], 0)
    c_lo = jnp.take_along_axis(c0, lo[None, :, :, None], axis=1)
    cnt = (t[:, None] + 1 - lo).astype(jnp.float32)
    y = ((c - c_lo) / cnt[None, :, :, None] - ug).astype(u.dtype)
    y = jnp.einsum('bsgc,gcd->bsgd', y, w).reshape(B, S, POOL_WIDTH)
    return y * scale


def dilated_window_attention(q, k, v, window, dil):
    B, S, H, Dh = q.shape
    steps = window // dil
    L = -(-S // (dil * BLK)) * BLK
    pad = L * dil - S
    nb = L // BLK

    def to_strided(a):
        a = jnp.pad(a, ((0, 0), (0, pad), (0, 0), (0, 0)))
        a = a.reshape(B, L, dil, H, Dh).transpose(0, 2, 1, 3, 4)
        return a.reshape(B, dil, nb, BLK, H, Dh)

    def with_prev(a):
        prev = jnp.pad(a[:, :, :-1], ((0, 0), (0, 0), (1, 0), (0, 0), (0, 0), (0, 0)))
        return jnp.concatenate([prev, a], axis=3)

    qs = to_strided(q)
    kb = with_prev(to_strided(k))
    vb = with_prev(to_strided(v))
    s = jnp.einsum('brnqhd,brnkhd->brnhqk', qs, kb,
                   preferred_element_type=jnp.float32) * (HEAD_DIM ** -0.5)
    qi = jnp.arange(nb)[:, None] * BLK + jnp.arange(BLK)[None, :]
    ki = jnp.arange(nb)[:, None] * BLK - BLK + jnp.arange(2 * BLK)[None, :]
    dist = qi[:, :, None] - ki[:, None, :]
    valid = (dist >= 0) & (dist <= steps) & (ki[:, None, :] >= 0)
    s = jnp.where(valid[None, None, :, None], s, -jnp.inf)
    m = jnp.max(s, axis=-1, keepdims=True)
    e = jnp.exp(s - m)
    l = jnp.sum(e, axis=-1, keepdims=True)
    o = jnp.einsum('brnhqk,brnkhd->brnqhd', e / l, vb.astype(jnp.float32))
    lse = (m + jnp.log(l))[..., 0]
    o = o.reshape(B, dil, L, H, Dh).transpose(0, 2, 1, 3, 4).reshape(B, L * dil, H, Dh)[:, :S]
    lse = lse.transpose(0, 1, 2, 4, 3).reshape(B, dil, L, H).transpose(0, 2, 1, 3)
    lse = lse.reshape(B, L * dil, H)[:, :S]
    return o, lse


def dilated_mixer(q, k, v):
    outs, lses = [], []
    for g, (window, dil) in enumerate(DIL_PATTERNS):
        sl = slice(g * HEADS_PER_PATTERN, (g + 1) * HEADS_PER_PATTERN)
        o, lse = dilated_window_attention(q[:, :, sl], k[:, :, sl], v[:, :, sl], window, dil)
        outs.append(o)
        lses.append(lse)
    wts = jax.nn.softmax(jnp.stack(lses, axis=0), axis=0)
    o = jnp.concatenate([outs[g] * wts[g][..., None] for g in range(len(DIL_PATTERNS))], axis=2)
    B, S = q.shape[0], q.shape[1]
    return o.reshape(B, S, ATTN_WIDTH).astype(q.dtype)


def _fwd_setup_inputs(seed: int = 0) -> dict:
    key = jax.random.key(seed)
    ks = jax.random.split(key, 16)
    f32 = jnp.float32
    n_in = POOL_WIDTH + 3 * ATTN_WIDTH
    return {
        "x": jax.random.normal(ks[0], (BATCH, SEQ, D_MODEL), f32),
        "p": jax.random.normal(ks[1], (DEPTH, BATCH, SEQ, PLE_DIM), f32),
        "positions": jnp.broadcast_to(jnp.arange(SEQ, dtype=jnp.int32), (BATCH, SEQ)),
        "norm1": 1.0 + 0.02 * jax.random.normal(ks[2], (DEPTH, D_MODEL), f32),
        "w_in": jax.random.normal(ks[3], (DEPTH, D_MODEL, n_in), f32) * D_MODEL ** -0.5,
        "pool_w": jax.random.normal(ks[4], (DEPTH, N_POOL_GROUPS, POOL_GC, POOL_GC), f32) * POOL_GC ** -0.5,
        "pool_scale": 1.0 + 0.02 * jax.random.normal(ks[5], (DEPTH, POOL_WIDTH), f32),
        "w_out": jax.random.normal(ks[6], (DEPTH, POOL_WIDTH + ATTN_WIDTH, D_MODEL), f32) * (POOL_WIDTH + ATTN_WIDTH) ** -0.5,
        "norm2": 1.0 + 0.02 * jax.random.normal(ks[7], (DEPTH, D_MODEL), f32),
        "w_up": jax.random.normal(ks[8], (DEPTH, D_MODEL, D_FF), f32) * D_MODEL ** -0.5,
        "w_down": jax.random.normal(ks[9], (DEPTH, D_FF, D_MODEL), f32) * D_FF ** -0.5,
        "norm3": 1.0 + 0.02 * jax.random.normal(ks[10], (DEPTH, D_MODEL), f32),
        "w_gate": jax.random.normal(ks[11], (DEPTH, D_MODEL, D_MODEL), f32) * D_MODEL ** -0.5,
        "w_ple": jax.random.normal(ks[12], (DEPTH, PLE_DIM, D_MODEL), f32) * PLE_DIM ** -0.5,
        "final_norm": 1.0 + 0.02 * jax.random.normal(ks[13], (D_MODEL,), f32),
    }


def _fwd_reference(x, p, positions, norm1, w_in, pool_w, pool_scale, w_out, norm2, w_up, w_down,
              norm3, w_gate, w_ple, final_norm):
    B, S, _ = x.shape
    cos, sin = rotary_tables(positions, x.dtype)
    h = x
    for i in range(DEPTH):
        hn = rmsnorm(h, norm1[i])
        z = hn @ w_in[i]
        u = z[..., :POOL_WIDTH]
        q, k, v = jnp.split(z[..., POOL_WIDTH:], 3, axis=-1)
        q = apply_partial_rotary(q.reshape(B, S, N_ATTN_HEADS, HEAD_DIM), cos, sin)
        k = apply_partial_rotary(k.reshape(B, S, N_ATTN_HEADS, HEAD_DIM), cos, sin)
        v = v.reshape(B, S, N_ATTN_HEADS, HEAD_DIM)
        pool_out = pool_mixer(u, pool_w[i], pool_scale[i])
        attn_out = dilated_mixer(q, k, v)
        h = h + jnp.concatenate([pool_out, attn_out], axis=-1) @ w_out[i]
        hn = rmsnorm(h, norm2[i])
        h = h + jnp.square(jax.nn.relu(hn @ w_up[i])) @ w_down[i]
        gate = jax.nn.sigmoid(rmsnorm(h, norm3[i]) @ w_gate[i])
        h = h + gate * (p[i] @ w_ple[i])
    return rmsnorm(h, final_norm)


import jax as _jax
import jax.numpy as _jnp

TWIN_FORMAT = 'train_step'
FWD_PARAMS = ['x', 'p', 'positions', 'norm1', 'w_in', 'pool_w', 'pool_scale', 'w_out', 'norm2', 'w_up', 'w_down', 'norm3', 'w_gate', 'w_ple', 'final_norm']
TWIN_WEIGHTS = ['norm1', 'w_in', 'pool_w', 'pool_scale', 'w_out', 'norm2', 'w_up', 'w_down', 'norm3', 'w_gate', 'w_ple', 'final_norm']
TWIN_DIFF_INPUT = 'x'
TWIN_INPUTS = ['x', 'p', 'positions', 'norm1', 'w_in', 'pool_w', 'pool_scale', 'w_out', 'norm2', 'w_up', 'w_down', 'norm3', 'w_gate', 'w_ple', 'final_norm', 'loss_target', 'm_norm1', 'm_w_in', 'm_pool_w', 'm_pool_scale', 'm_w_out', 'm_norm2', 'm_w_up', 'm_w_down', 'm_norm3', 'm_w_gate', 'm_w_ple', 'm_final_norm', 'v_norm1', 'v_w_in', 'v_pool_w', 'v_pool_scale', 'v_w_out', 'v_norm2', 'v_w_up', 'v_w_down', 'v_norm3', 'v_w_gate', 'v_w_ple', 'v_final_norm']
TWIN_OUTPUTS = ['loss', 'grad_x', 'grad_norm1', 'grad_w_in', 'grad_pool_w', 'grad_pool_scale', 'grad_w_out', 'grad_norm2', 'grad_w_up', 'grad_w_down', 'grad_norm3', 'grad_w_gate', 'grad_w_ple', 'grad_final_norm', 'delta_norm1', 'delta_w_in', 'delta_pool_w', 'delta_pool_scale', 'delta_w_out', 'delta_norm2', 'delta_w_up', 'delta_w_down', 'delta_norm3', 'delta_w_gate', 'delta_w_ple', 'delta_final_norm', 'new_m_norm1', 'new_m_w_in', 'new_m_pool_w', 'new_m_pool_scale', 'new_m_w_out', 'new_m_norm2', 'new_m_w_up', 'new_m_w_down', 'new_m_norm3', 'new_m_w_gate', 'new_m_w_ple', 'new_m_final_norm', 'new_v_norm1', 'new_v_w_in', 'new_v_pool_w', 'new_v_pool_scale', 'new_v_w_out', 'new_v_norm2', 'new_v_w_up', 'new_v_w_down', 'new_v_norm3', 'new_v_w_gate', 'new_v_w_ple', 'new_v_final_norm']
TWIN_LEAF_KINDS = {'loss': 'loss', 'grad_x': 'grad_x', 'grad_norm1': 'grad_w', 'grad_w_in': 'grad_w', 'grad_pool_w': 'grad_w', 'grad_pool_scale': 'grad_w', 'grad_w_out': 'grad_w', 'grad_norm2': 'grad_w', 'grad_w_up': 'grad_w', 'grad_w_down': 'grad_w', 'grad_norm3': 'grad_w', 'grad_w_gate': 'grad_w', 'grad_w_ple': 'grad_w', 'grad_final_norm': 'grad_w', 'delta_norm1': 'delta_w', 'delta_w_in': 'delta_w', 'delta_pool_w': 'delta_w', 'delta_pool_scale': 'delta_w', 'delta_w_out': 'delta_w', 'delta_norm2': 'delta_w', 'delta_w_up': 'delta_w', 'delta_w_down': 'delta_w', 'delta_norm3': 'delta_w', 'delta_w_gate': 'delta_w', 'delta_w_ple': 'delta_w', 'delta_final_norm': 'delta_w', 'new_m_norm1': 'new_m', 'new_m_w_in': 'new_m', 'new_m_pool_w': 'new_m', 'new_m_pool_scale': 'new_m', 'new_m_w_out': 'new_m', 'new_m_norm2': 'new_m', 'new_m_w_up': 'new_m', 'new_m_w_down': 'new_m', 'new_m_norm3': 'new_m', 'new_m_w_gate': 'new_m', 'new_m_w_ple': 'new_m', 'new_m_final_norm': 'new_m', 'new_v_norm1': 'new_v', 'new_v_w_in': 'new_v', 'new_v_pool_w': 'new_v', 'new_v_pool_scale': 'new_v', 'new_v_w_out': 'new_v', 'new_v_norm2': 'new_v', 'new_v_w_up': 'new_v', 'new_v_w_down': 'new_v', 'new_v_norm3': 'new_v', 'new_v_w_gate': 'new_v', 'new_v_w_ple': 'new_v', 'new_v_final_norm': 'new_v'}


def _forward(args):
    return _fwd_reference(*[args[k] for k in FWD_PARAMS])


def _output_shape():
    def fwd():
        inp = _fwd_setup_inputs(0)
        return _fwd_reference(*[inp[k] for k in FWD_PARAMS])
    out = _jax.eval_shape(fwd)
    return out.shape, out.dtype

N_MICROBATCH = 1
ADAM_LR = 0.001
ADAM_B1 = 0.9
ADAM_B2 = 0.999
ADAM_EPS = 1e-08
ADAM_WD = 0.01
ADAM_STEP = 10
PER_EXAMPLE_BATCH_AXIS = {'x': 0, 'p': 1, 'positions': 0, 'loss_target': 0}
SHARED_INPUTS = []
_WEIGHT_DTYPES = {'norm1': _jnp.float32, 'w_in': _jnp.float32, 'pool_w': _jnp.float32, 'pool_scale': _jnp.float32, 'w_out': _jnp.float32, 'norm2': _jnp.float32, 'w_up': _jnp.float32, 'w_down': _jnp.float32, 'norm3': _jnp.float32, 'w_gate': _jnp.float32, 'w_ple': _jnp.float32, 'final_norm': _jnp.float32}
MOMENT_SCALE = {'norm1': 9.495852e-02, 'w_in': 5.876835e-02, 'pool_w': 1.787005e-01, 'pool_scale': 1.928137e-01, 'w_out': 8.998696e-02, 'norm2': 1.914469e-01, 'w_up': 9.580857e-02, 'w_down': 2.063273e-01, 'norm3': 2.827731e-02, 'w_gate': 2.921951e-02, 'w_ple': 7.155831e-02, 'final_norm': 6.483060e+01}


def _to_microbatches(a, axis):
    t = _jnp.moveaxis(a, axis, 0)
    t = t.reshape((N_MICROBATCH, t.shape[0] // N_MICROBATCH) + t.shape[1:])
    return _jnp.moveaxis(t, 1, axis + 1)


def setup_inputs(seed: int = 0) -> dict:
    inp = _fwd_setup_inputs(seed)
    key = _jax.random.fold_in(_jax.random.key(seed), 7919)
    shape, _ = _output_shape()
    out = dict(inp)
    out["loss_target"] = _jax.random.normal(_jax.random.fold_in(key, 0), shape, _jnp.float32)
    for i, name in enumerate(TWIN_WEIGHTS):
        w = inp[name].astype(_jnp.float32)
        if MOMENT_SCALE is None:
            s = _jnp.sqrt(_jnp.mean(_jnp.square(w)) + 1e-30)
        else:
            s = MOMENT_SCALE[name]
        km, kv = _jax.random.split(_jax.random.fold_in(key, i + 1))
        out[name] = w
        out["m_" + name] = s * _jax.random.normal(km, w.shape, _jnp.float32)
        out["v_" + name] = (s * s) * _jax.random.uniform(kv, w.shape, _jnp.float32, 0.5, 1.5)
    if N_MICROBATCH > 1:
        for name, axis in PER_EXAMPLE_BATCH_AXIS.items():
            out[name] = _to_microbatches(out[name], axis)
    return {'x': out['x'], 'p': out['p'], 'positions': out['positions'], 'norm1': out['norm1'], 'w_in': out['w_in'], 'pool_w': out['pool_w'], 'pool_scale': out['pool_scale'], 'w_out': out['w_out'], 'norm2': out['norm2'], 'w_up': out['w_up'], 'w_down': out['w_down'], 'norm3': out['norm3'], 'w_gate': out['w_gate'], 'w_ple': out['w_ple'], 'final_norm': out['final_norm'], 'loss_target': out['loss_target'], 'm_norm1': out['m_norm1'], 'm_w_in': out['m_w_in'], 'm_pool_w': out['m_pool_w'], 'm_pool_scale': out['m_pool_scale'], 'm_w_out': out['m_w_out'], 'm_norm2': out['m_norm2'], 'm_w_up': out['m_w_up'], 'm_w_down': out['m_w_down'], 'm_norm3': out['m_norm3'], 'm_w_gate': out['m_w_gate'], 'm_w_ple': out['m_w_ple'], 'm_final_norm': out['m_final_norm'], 'v_norm1': out['v_norm1'], 'v_w_in': out['v_w_in'], 'v_pool_w': out['v_pool_w'], 'v_pool_scale': out['v_pool_scale'], 'v_w_out': out['v_w_out'], 'v_norm2': out['v_norm2'], 'v_w_up': out['v_w_up'], 'v_w_down': out['v_w_down'], 'v_norm3': out['v_norm3'], 'v_w_gate': out['v_w_gate'], 'v_w_ple': out['v_w_ple'], 'v_final_norm': out['v_final_norm']}


def _loss(weights, diff, rest, loss_target):
    with _jax.named_scope("forward"):
        args = {**rest, TWIN_DIFF_INPUT: diff, **{k: w.astype(_WEIGHT_DTYPES[k]) for k, w in weights.items()}}
        y = _forward(args)
    with _jax.named_scope("loss_head"):
        err = _jnp.square(y.astype(_jnp.float32) - loss_target)
        return 0.5 * _jnp.sum(_jnp.mean(err, axis=-1)) if err.ndim else 0.5 * err


def _adamw(w, g, m, v):
    m = ADAM_B1 * m + (1.0 - ADAM_B1) * g
    v = ADAM_B2 * v + (1.0 - ADAM_B2) * _jnp.square(g)
    m_hat = m / (1.0 - ADAM_B1 ** ADAM_STEP)
    v_hat = v / (1.0 - ADAM_B2 ** ADAM_STEP)
    delta = -ADAM_LR * (m_hat / (_jnp.sqrt(v_hat) + ADAM_EPS) + ADAM_WD * w)
    return delta, m, v


def reference(x, p, positions, norm1, w_in, pool_w, pool_scale, w_out, norm2, w_up, w_down, norm3, w_gate, w_ple, final_norm, loss_target, m_norm1, m_w_in, m_pool_w, m_pool_scale, m_w_out, m_norm2, m_w_up, m_w_down, m_norm3, m_w_gate, m_w_ple, m_final_norm, v_norm1, v_w_in, v_pool_w, v_pool_scale, v_w_out, v_norm2, v_w_up, v_w_down, v_norm3, v_w_gate, v_w_ple, v_final_norm):
    given = dict(x=x, p=p, positions=positions, norm1=norm1, w_in=w_in, pool_w=pool_w, pool_scale=pool_scale, w_out=w_out, norm2=norm2, w_up=w_up, w_down=w_down, norm3=norm3, w_gate=w_gate, w_ple=w_ple, final_norm=final_norm, loss_target=loss_target, m_norm1=m_norm1, m_w_in=m_w_in, m_pool_w=m_pool_w, m_pool_scale=m_pool_scale, m_w_out=m_w_out, m_norm2=m_norm2, m_w_up=m_w_up, m_w_down=m_w_down, m_norm3=m_norm3, m_w_gate=m_w_gate, m_w_ple=m_w_ple, m_final_norm=m_final_norm, v_norm1=v_norm1, v_w_in=v_w_in, v_pool_w=v_pool_w, v_pool_scale=v_pool_scale, v_w_out=v_w_out, v_norm2=v_norm2, v_w_up=v_w_up, v_w_down=v_w_down, v_norm3=v_norm3, v_w_gate=v_w_gate, v_w_ple=v_w_ple, v_final_norm=v_final_norm)
    weights = {n: given[n] for n in TWIN_WEIGHTS}
    shared = {n: given[n] for n in SHARED_INPUTS}
    per_example = {n: given[n] for n in ['x', 'p', 'positions']}
    grad_fn = _jax.value_and_grad(_loss, argnums=(0, 1))

    def one_microbatch(ex, loss_target):
        ex = dict(ex)
        diff = ex.pop(TWIN_DIFF_INPUT)
        return grad_fn(weights, diff, {**shared, **ex}, loss_target)

    if N_MICROBATCH == 1:
        loss, (grad_w, grad_x) = one_microbatch(per_example, given["loss_target"])
    else:
        def body(carry, xs):
            loss_sum, grad_sum = carry
            l_k, (gw_k, gx_k) = one_microbatch(xs[0], xs[1])
            with _jax.named_scope("update"):
                return (loss_sum + l_k, _jax.tree.map(_jnp.add, grad_sum, gw_k)), gx_k

        init = (_jnp.zeros((), _jnp.float32), _jax.tree.map(_jnp.zeros_like, weights))
        (loss, grad_w), grad_x = _jax.lax.scan(body, init, (per_example, given["loss_target"]))
    with _jax.named_scope("update"):
        delta_w, new_m, new_v = {}, {}, {}
        for n in TWIN_WEIGHTS:
            delta_w[n], new_m[n], new_v[n] = _adamw(weights[n], grad_w[n], given["m_" + n], given["v_" + n])
    return (loss, grad_x, *[grad_w[n] for n in TWIN_WEIGHTS], *[delta_w[n] for n in TWIN_WEIGHTS],
            *[new_m[n] for n in TWIN_WEIGHTS], *[new_v[n] for n in TWIN_WEIGHTS])
```

```python
import functools

import jax
import jax.numpy as jnp
import numpy as np
from jax import lax
from jax.experimental import pallas as pl
from jax.experimental.pallas import tpu as pltpu

f32 = jnp.float32
MXU_DTYPE = jnp.bfloat16
COMM_DTYPE = jnp.bfloat16

D_MODEL = 1024
POOL_WIDTH = 256
POOL_GC = 64
ATTN_WIDTH = 768
HEAD_DIM = 64
N_IN = POOL_WIDTH + 3 * ATTN_WIDTH
D_FF = 4096
PLE_DIM = 256
BLK = 128
DILATIONS = (1, 4, 16)
ROT_DIM = 16
ROPE_THETA = 500000.0
EPS = 1e-6
ATTN_SCALE = HEAD_DIM ** -0.5
NEG_BIG = -1e30

ADAM_LR, ADAM_B1, ADAM_B2, ADAM_EPS, ADAM_WD, ADAM_STEP = 0.001, 0.9, 0.999, 1e-08, 0.01, 10

TM = 512
HALO = 16
VMEM_LIMIT = 48 * 1024 * 1024
N_CHIPS = 4
MESH = pl.DeviceIdType.MESH

BIG = ("w_in", "w_out", "w_up", "w_down", "w_gate", "w_ple")
FULL_SHAPE = {"w_in": (D_MODEL, N_IN), "w_out": (D_MODEL, D_MODEL), "w_up": (D_MODEL, D_FF),
              "w_down": (D_FF, D_MODEL), "w_gate": (D_MODEL, D_MODEL), "w_ple": (PLE_DIM, D_MODEL)}
COL_SHARDED = {"w_in": True, "w_out": False, "w_up": True, "w_down": False, "w_gate": False, "w_ple": True}


def _shard_shape(name):
    k, n = FULL_SHAPE[name]
    return (k, n // N_CHIPS) if COL_SHARDED[name] else (k // N_CHIPS, n)


def _cparams(sem=None, vmem=VMEM_LIMIT):
    return pltpu.CompilerParams(dimension_semantics=sem, vmem_limit_bytes=vmem)


def _mx(x):
    return x.astype(MXU_DTYPE)


def _dot(a, b):
    return jnp.dot(a, b, preferred_element_type=f32)


def _dot_nt(a, b):
    return lax.dot_general(a, b, (((1,), (1,)), ((), ())), preferred_element_type=f32)


def _dot_tn(a, b):
    return lax.dot_general(a, b, (((0,), (0,)), ((), ())), preferred_element_type=f32)


def _sigmoid(x):
    return 1.0 / (1.0 + jnp.exp(-x))


def _rope_apply(y, c, s1, s2, width):
    return y * c + pltpu.roll(y, width - 8, axis=1) * s1 + pltpu.roll(y, 8, axis=1) * s2


def _rope_transpose(dy, c, s1, s2, width):
    return dy * c + pltpu.roll(dy * s1, 8, axis=1) + pltpu.roll(dy * s2, width - 8, axis=1)


def _norm_matmul(h, g, w, layer, tn, name, rope=None):
    s_len, d = h.shape
    n = w.shape[2]
    nj = n // tn

    def body(*refs):
        if rope is None:
            h_ref, g_ref, w_ref, y_ref, hn_ref = refs
        else:
            h_ref, g_ref, w_ref, c_ref, s1_ref, s2_ref, y_ref, hn_ref = refs
        j = pl.program_id(1)

        @pl.when(j == 0)
        def _():
            x = h_ref[...]
            r = lax.rsqrt(jnp.mean(x * x, axis=-1, keepdims=True) + EPS)
            hn_ref[...] = ((x * r) * g_ref[...]).astype(hn_ref.dtype)

        y = _dot(hn_ref[...], w_ref[...])
        if rope is not None:
            reps = tn // 256
            c = jnp.concatenate([c_ref[...]] * reps, axis=1)
            s1 = jnp.concatenate([s1_ref[...]] * reps, axis=1)
            s2 = jnp.concatenate([s2_ref[...]] * reps, axis=1)
            col = j * tn + lax.broadcasted_iota(jnp.int32, y.shape, 1)
            is_qk = (col >= POOL_WIDTH) & (col < POOL_WIDTH + 2 * ATTN_WIDTH)
            y = jnp.where(is_qk, _rope_apply(y, c, s1, s2, tn), y)
        y_ref[...] = y

    in_specs = [pl.BlockSpec((TM, d), lambda i, j: (i, 0)),
                pl.BlockSpec((1, d), lambda i, j: (0, 0)),
                pl.BlockSpec((None, d, tn), lambda i, j: (layer, 0, j))]
    args = [h, g, w]
    if rope is not None:
        in_specs += [pl.BlockSpec((TM, 256), lambda i, j: (i, 0))] * 3
        args += list(rope)
    return pl.pallas_call(
        body, name=name, grid=(s_len // TM, nj), in_specs=in_specs,
        out_specs=[pl.BlockSpec((TM, tn), lambda i, j: (i, j)), pl.BlockSpec((TM, d), lambda i, j: (i, 0))],
        out_shape=[jax.ShapeDtypeStruct((s_len, n), f32), jax.ShapeDtypeStruct((s_len, d), MXU_DTYPE)],
        compiler_params=_cparams(("parallel", "arbitrary")),
    )(*args)


def _matmul_residual(a, w, layer, res, name, act=False, tk=1024):
    s_len, k_dim = a.shape
    n = w.shape[2]
    nk = k_dim // tk

    def body(a_ref, w_ref, res_ref, o_ref):
        k = pl.program_id(1)
        x = a_ref[...]
        if act:
            r = jnp.maximum(x, 0.0)
            x = r * r
        part = _dot(_mx(x), w_ref[...])

        @pl.when(k == 0)
        def _():
            o_ref[...] = res_ref[...] + part

        @pl.when(k > 0)
        def _():
            o_ref[...] += part

    return pl.pallas_call(
        body, name=name, grid=(s_len // TM, nk),
        in_specs=[pl.BlockSpec((TM, tk), lambda i, k: (i, k)),
                  pl.BlockSpec((None, tk, n), lambda i, k: (layer, k, 0)),
                  pl.BlockSpec((TM, n), lambda i, k: (i, 0))],
        out_specs=pl.BlockSpec((TM, n), lambda i, k: (i, 0)),
        out_shape=jax.ShapeDtypeStruct((s_len, n), f32),
        compiler_params=_cparams(("parallel", "arbitrary")),
    )(a, w, res)


def _gate_ple_fwd(h2, g, w_gate, w_ple, layer, p, name):
    s_len, d = h2.shape

    def body(h_ref, g_ref, wg_ref, p_ref, wp_ref, h3_ref, gl_ref, hn_ref):
        x = h_ref[...]
        r = lax.rsqrt(jnp.mean(x * x, axis=-1, keepdims=True) + EPS)
        hn = ((x * r) * g_ref[...]).astype(hn_ref.dtype)
        hn_ref[...] = hn
        gl = _dot(hn, wg_ref[...])
        gl_ref[...] = gl
        e = _dot(_mx(p_ref[...]), wp_ref[...])
        h3_ref[...] = x + _sigmoid(gl) * e

    row = lambda i: (i, 0)
    return pl.pallas_call(
        body, name=name, grid=(s_len // TM,),
        in_specs=[pl.BlockSpec((TM, d), row), pl.BlockSpec((1, d), lambda i: (0, 0)),
                  pl.BlockSpec((None, d, d), lambda i: (layer, 0, 0)), pl.BlockSpec((TM, PLE_DIM), row),
                  pl.BlockSpec((None, PLE_DIM, d), lambda i: (layer, 0, 0))],
        out_specs=[pl.BlockSpec((TM, d), row)] * 3,
        out_shape=[jax.ShapeDtypeStruct((s_len, d), f32), jax.ShapeDtypeStruct((s_len, d), f32),
                   jax.ShapeDtypeStruct((s_len, d), MXU_DTYPE)],
        compiler_params=_cparams(("parallel",)),
    )(h2, g, w_gate, p, w_ple)


def _gate_ple_bwd(dh3, gl, p, w_ple, layer, name):
    s_len, d = dh3.shape

    def body(dh_ref, gl_ref, p_ref, wp_ref, de_ref, dgl_ref):
        dh = dh_ref[...]
        gate = _sigmoid(gl_ref[...])
        e = _dot(_mx(p_ref[...]), wp_ref[...])
        de_ref[...] = (dh * gate).astype(de_ref.dtype)
        dgl_ref[...] = ((dh * e) * (gate * (1.0 - gate))).astype(dgl_ref.dtype)

    row = lambda i: (i, 0)
    return pl.pallas_call(
        body, name=name, grid=(s_len // TM,),
        in_specs=[pl.BlockSpec((TM, d), row), pl.BlockSpec((TM, d), row), pl.BlockSpec((TM, PLE_DIM), row),
                  pl.BlockSpec((None, PLE_DIM, d), lambda i: (layer, 0, 0))],
        out_specs=[pl.BlockSpec((TM, d), row)] * 2,
        out_shape=[jax.ShapeDtypeStruct((s_len, d), MXU_DTYPE)] * 2,
        compiler_params=_cparams(("parallel",)),
    )(dh3, gl, p, w_ple)


def _rmsnorm_bwd(dhn, x, g):
    r = lax.rsqrt(jnp.mean(x * x, axis=-1, keepdims=True) + EPS)
    xh = x * r
    dxh = dhn * g
    dx = r * (dxh - xh * jnp.mean(dxh * xh, axis=-1, keepdims=True))
    return dx, dhn * xh


def _matmul_nt_norm_bwd(dy, w, layer, h_prev, g, dres, name, tk=1024):
    s_len, k_dim = dy.shape
    d = h_prev.shape[1]
    nk = k_dim // tk

    def body(dy_ref, w_ref, h_ref, g_ref, dres_ref, dh_ref, dg_ref, acc_ref):
        i, k = pl.program_id(0), pl.program_id(1)
        part = _dot_nt(_mx(dy_ref[...]), w_ref[...])

        @pl.when(k == 0)
        def _():
            acc_ref[...] = part

        @pl.when(k > 0)
        def _():
            acc_ref[...] += part

        @pl.when(k == nk - 1)
        def _():
            dx, dgrow = _rmsnorm_bwd(acc_ref[...], h_ref[...], g_ref[...])
            dh_ref[...] = dres_ref[...] + dx
            dgsum = jnp.sum(dgrow, axis=0, keepdims=True)

            @pl.when(i == 0)
            def _():
                dg_ref[...] = dgsum

            @pl.when(i > 0)
            def _():
                dg_ref[...] += dgsum

    return pl.pallas_call(
        body, name=name, grid=(s_len // TM, nk),
        in_specs=[pl.BlockSpec((TM, tk), lambda i, k: (i, k)),
                  pl.BlockSpec((None, d, tk), lambda i, k: (layer, 0, k)),
                  pl.BlockSpec((TM, d), lambda i, k: (i, 0)),
                  pl.BlockSpec((1, d), lambda i, k: (0, 0)),
                  pl.BlockSpec((TM, d), lambda i, k: (i, 0))],
        out_specs=[pl.BlockSpec((TM, d), lambda i, k: (i, 0)), pl.BlockSpec((1, d), lambda i, k: (0, 0))],
        out_shape=[jax.ShapeDtypeStruct((s_len, d), f32), jax.ShapeDtypeStruct((1, d), f32)],
        scratch_shapes=[pltpu.VMEM((TM, d), f32)],
        compiler_params=_cparams(("arbitrary", "arbitrary")),
    )(dy, w, h_prev, g, dres)


def _down_bwd(dh2, w_down, layer, a, name, tf=1024):
    s_len, d = dh2.shape
    ff = a.shape[1]

    def body(dh_ref, w_ref, a_ref, act_ref, da_ref, dhb_ref):
        j = pl.program_id(1)

        @pl.when(j == 0)
        def _():
            dhb_ref[...] = _mx(dh_ref[...])

        dact = _dot_nt(dhb_ref[...], w_ref[...])
        r = jnp.maximum(a_ref[...], 0.0)
        act_ref[...] = (r * r).astype(act_ref.dtype)
        da_ref[...] = (dact * (2.0 * r)).astype(da_ref.dtype)

    return pl.pallas_call(
        body, name=name, grid=(s_len // TM, ff // tf),
        in_specs=[pl.BlockSpec((TM, d), lambda i, j: (i, 0)),
                  pl.BlockSpec((None, tf, d), lambda i, j: (layer, j, 0)),
                  pl.BlockSpec((TM, tf), lambda i, j: (i, j))],
        out_specs=[pl.BlockSpec((TM, tf), lambda i, j: (i, j))] * 2,
        out_shape=[jax.ShapeDtypeStruct((s_len, ff), MXU_DTYPE)] * 2,
        scratch_shapes=[pltpu.VMEM((TM, d), MXU_DTYPE)],
        compiler_params=_cparams(("parallel", "arbitrary")),
    )(dh2, w_down, a)


def _matmul_nt(dy, w, layer, name):
    s_len, n = dy.shape
    k_dim = w.shape[1]

    def body(dy_ref, w_ref, o_ref):
        o_ref[...] = _dot_nt(_mx(dy_ref[...]), w_ref[...])

    return pl.pallas_call(
        body, name=name, grid=(s_len // TM,),
        in_specs=[pl.BlockSpec((TM, n), lambda i: (i, 0)), pl.BlockSpec((None, k_dim, n), lambda i: (layer, 0, 0))],
        out_specs=pl.BlockSpec((TM, k_dim), lambda i: (i, 0)),
        out_shape=jax.ShapeDtypeStruct((s_len, k_dim), f32),
        compiler_params=_cparams(("parallel",)),
    )(dy, w)


def _weight_grad(a, b, layer, prev, name):
    s_len, k_dim = a.shape
    n = b.shape[1]
    tka = min(k_dim, 1024)
    tnb = n if n <= 1024 else (1024 if n % 1024 == 0 else 640)
    ns = s_len // TM

    def body(*refs):
        a_ref, b_ref = refs[0], refs[1]
        o_ref = refs[-1]
        s = pl.program_id(2)
        part = _dot_tn(_mx(a_ref[...]), _mx(b_ref[...]))

        @pl.when(s == 0)
        def _():
            o_ref[...] = part

        @pl.when(s > 0)
        def _():
            o_ref[...] += part

    in_specs = [pl.BlockSpec((TM, tka), lambda i, j, s: (s, i)), pl.BlockSpec((TM, tnb), lambda i, j, s: (s, j))]
    args = [a, b]
    aliases = {}
    if prev is not None:
        in_specs.append(pl.BlockSpec(memory_space=pl.ANY))
        args.append(prev)
        aliases = {2: 0}
    return pl.pallas_call(
        body, name=name, grid=(k_dim // tka, n // tnb, ns), in_specs=in_specs,
        out_specs=pl.BlockSpec((None, tka, tnb), lambda i, j, s: (layer, i, j)),
        out_shape=jax.ShapeDtypeStruct((2, k_dim, n), f32),
        input_output_aliases=aliases,
        compiler_params=_cparams(("parallel", "parallel", "arbitrary")),
    )(*args)


def _group_select(lane, x2, x4, x8, x16):
    grp = lane // POOL_GC
    return jnp.where(grp == 0, x2, jnp.where(grp == 1, x4, jnp.where(grp == 2, x8, x16)))


def _pool_window(lane):
    grp = lane // POOL_GC
    return jnp.where(grp == 0, 2, jnp.where(grp == 1, 4, jnp.where(grp == 2, 8, 16)))


def _pool_y(u, halo, i):
    xs = jnp.concatenate([jnp.where(i > 0, halo, 0.0), u], axis=0)
    s2 = xs + pltpu.roll(xs, 1, axis=0)
    s4 = s2 + pltpu.roll(s2, 2, axis=0)
    s8 = s4 + pltpu.roll(s4, 4, axis=0)
    s16 = s8 + pltpu.roll(s8, 8, axis=0)
    lane = lax.broadcasted_iota(jnp.int32, xs.shape, 1)
    sel = _group_select(lane, s2, s4, s8, s16)[HALO:, :]
    t = i * TM + lax.broadcasted_iota(jnp.int32, u.shape, 0)
    cnt = jnp.minimum(_pool_window(lax.broadcasted_iota(jnp.int32, u.shape, 1)), t + 1).astype(f32)
    return sel / cnt - u


def _group_weights(l0, l1, l2):
    mx = jnp.maximum(jnp.maximum(l0, l1), l2)
    e0, e1, e2 = jnp.exp(l0 - mx), jnp.exp(l1 - mx), jnp.exp(l2 - mx)
    den = e0 + e1 + e2
    return e0 / den, e1 / den, e2 / den


def _mixer_merge(z, wbd, scale, outs, lses, name):
    s_len = z.shape[0]

    def body(u_ref, halo_ref, wbd_ref, sc_ref, o0, o1, o2, l0, l1, l2, m_ref):
        i = pl.program_id(0)
        y = _pool_y(u_ref[...], halo_ref[...], i)
        pool = _dot(_mx(y), wbd_ref[...]) * sc_ref[...]
        w0, w1, w2 = _group_weights(l0[...], l1[...], l2[...])
        m_ref[...] = jnp.concatenate([pool, o0[...] * w0, o1[...] * w1, o2[...] * w2], axis=1).astype(m_ref.dtype)

    row = lambda i: (i, 0)
    blk = pl.BlockSpec((TM, 256), row)
    return pl.pallas_call(
        body, name=name, grid=(s_len // TM,),
        in_specs=[blk, pl.BlockSpec((HALO, 256), lambda i: (jnp.maximum(i * (TM // HALO) - 1, 0), 0)),
                  pl.BlockSpec((256, 256), lambda i: (0, 0)), pl.BlockSpec((1, 256), lambda i: (0, 0))] + [blk] * 6,
        out_specs=pl.BlockSpec((TM, D_MODEL), row),
        out_shape=jax.ShapeDtypeStruct((s_len, D_MODEL), MXU_DTYPE),
        compiler_params=_cparams(("parallel",)),
    )(z, z, wbd, scale, *outs, *lses)


def _head_sums(x):
    r = lax.broadcasted_iota(jnp.int32, (256, 256), 0) // HEAD_DIM
    c = lax.broadcasted_iota(jnp.int32, (256, 256), 1) // HEAD_DIM
    ones = jnp.where(r == c, 1.0, 0.0).astype(jnp.bfloat16)
    hi = x.astype(jnp.bfloat16)
    lo = (x - hi.astype(f32)).astype(jnp.bfloat16)
    return _dot(hi, ones) + _dot(lo, ones)


def _combine_bwd(dm, outs, lses, name):
    s_len = dm.shape[0]

    def body(d0, d1, d2, o0, o1, o2, l0, l1, l2, do_ref, dl_ref):
        w = _group_weights(l0[...], l1[...], l2[...])
        da = (d0[...], d1[...], d2[...])
        o = (o0[...], o1[...], o2[...])
        dw = [_head_sums(da[g] * o[g]) for g in range(3)]
        t = w[0] * dw[0] + w[1] * dw[1] + w[2] * dw[2]
        do_ref[...] = jnp.concatenate([da[g] * w[g] for g in range(3)], axis=1)
        dl_ref[...] = jnp.concatenate([w[g] * t for g in range(3)], axis=1)

    blk = pl.BlockSpec((TM, 256), lambda i: (i, 0))
    return pl.pallas_call(
        body, name=name, grid=(s_len // TM,),
        in_specs=[pl.BlockSpec((TM, 256), lambda i: (i, 1)), pl.BlockSpec((TM, 256), lambda i: (i, 2)),
                  pl.BlockSpec((TM, 256), lambda i: (i, 3))] + [blk] * 6,
        out_specs=[pl.BlockSpec((TM, ATTN_WIDTH), lambda i: (i, 0))] * 2,
        out_shape=[jax.ShapeDtypeStruct((s_len, ATTN_WIDTH), f32)] * 2,
        compiler_params=_cparams(("parallel",)),
    )(dm, dm, dm, *outs, *lses)


def _pool_bwd(z, dm, wbd, scale, name):
    s_len = z.shape[0]
    n_halo = s_len // HALO

    def body(u_ref, uh_ref, d_ref, dh_ref, wbd_ref, sc_ref, du_ref, dw_ref, dsc_ref):
        i = pl.program_id(0)
        last = pl.num_programs(0) - 1
        y = _pool_y(u_ref[...], uh_ref[...], i)
        yb = _mx(y)
        dpo = d_ref[...]
        sc = sc_ref[...]
        dsc = jnp.sum(dpo * _dot(yb, wbd_ref[...]), axis=0, keepdims=True)
        dwp = _dot_tn(yb, _mx(dpo * sc))

        @pl.when(i == 0)
        def _():
            dsc_ref[...] = dsc
            dw_ref[...] = dwp

        @pl.when(i > 0)
        def _():
            dsc_ref[...] += dsc
            dw_ref[...] += dwp

        ext = jnp.concatenate([dpo, jnp.where(i < last, dh_ref[...], 0.0)], axis=0)
        dy = _dot_nt(_mx(ext * sc), wbd_ref[...])
        t = i * TM + lax.broadcasted_iota(jnp.int32, ext.shape, 0)
        lane = lax.broadcasted_iota(jnp.int32, ext.shape, 1)
        e = dy / jnp.minimum(_pool_window(lane), t + 1).astype(f32)
        rows = ext.shape[0]
        f2 = e + pltpu.roll(e, rows - 1, axis=0)
        f4 = f2 + pltpu.roll(f2, rows - 2, axis=0)
        f8 = f4 + pltpu.roll(f4, rows - 4, axis=0)
        f16 = f8 + pltpu.roll(f8, rows - 8, axis=0)
        du_ref[...] = (_group_select(lane, f2, f4, f8, f16) - dy)[:TM, :]

    row = lambda i: (i, 0)
    blk = pl.BlockSpec((TM, 256), row)
    return pl.pallas_call(
        body, name=name, grid=(s_len // TM,),
        in_specs=[blk, pl.BlockSpec((HALO, 256), lambda i: (jnp.maximum(i * (TM // HALO) - 1, 0), 0)),
                  blk, pl.BlockSpec((HALO, 256), lambda i: (jnp.minimum((i + 1) * (TM // HALO), n_halo - 1), 0)),
                  pl.BlockSpec((256, 256), lambda i: (0, 0)), pl.BlockSpec((1, 256), lambda i: (0, 0))],
        out_specs=[blk, pl.BlockSpec((256, 256), lambda i: (0, 0)), pl.BlockSpec((1, 256), lambda i: (0, 0))],
        out_shape=[jax.ShapeDtypeStruct((s_len, 256), f32), jax.ShapeDtypeStruct((256, 256), f32),
                   jax.ShapeDtypeStruct((1, 256), f32)],
        compiler_params=_cparams(("arbitrary",)),
    )(z, z, dm, dm, wbd, scale)


def _to_strided(x, dil):
    if dil == 1:
        return x
    s_len, c = x.shape
    return x.reshape(s_len // (BLK * dil), BLK, dil, c).transpose(0, 2, 1, 3).reshape(s_len, c)


def _from_strided(x, dil):
    if dil == 1:
        return x
    s_len, c = x.shape
    return x.reshape(s_len // (BLK * dil), dil, BLK, c).transpose(0, 2, 1, 3).reshape(s_len, c)


def _tri_masks():
    qi = lax.broadcasted_iota(jnp.int32, (BLK, BLK), 0)
    ki = lax.broadcasted_iota(jnp.int32, (BLK, BLK), 1)
    return qi >= ki, ki >= qi


def _attn_fwd(q, k, v, dil, name):
    s_len = q.shape[0]
    nblk = s_len // BLK

    def body(q_ref, kc_ref, kp_ref, vc_ref, vp_ref, o_ref, l_ref):
        b = pl.program_id(0)
        has_prev = b >= dil
        low, up = _tri_masks()
        valid = jnp.concatenate([up & has_prev, low], axis=1)
        outs, lses = [], []
        for hh in range(2):
            sl = slice(hh * HEAD_DIM, (hh + 1) * HEAD_DIM)
            qh = _mx(q_ref[:, sl])
            k2 = jnp.concatenate([_mx(kp_ref[:, sl]), _mx(kc_ref[:, sl])], axis=0)
            v2 = jnp.concatenate([_mx(vp_ref[:, sl]), _mx(vc_ref[:, sl])], axis=0)
            s = jnp.where(valid, _dot_nt(qh, k2) * ATTN_SCALE, NEG_BIG)
            m = jnp.max(s, axis=-1, keepdims=True)
            e = jnp.exp(s - m)
            l = jnp.sum(e, axis=-1, keepdims=True)
            outs.append(_dot(_mx(e / l), v2))
            lses.append(jnp.broadcast_to(m + jnp.log(l), (BLK, HEAD_DIM)))
        o_ref[...] = jnp.concatenate(outs, axis=1)
        l_ref[...] = jnp.concatenate(lses, axis=1)

    cur = pl.BlockSpec((BLK, 128), lambda b, hp: (b, hp))
    prev = pl.BlockSpec((BLK, 128), lambda b, hp: (jnp.maximum(b - dil, 0), hp))
    return pl.pallas_call(
        body, name=name, grid=(nblk, 2), in_specs=[cur, cur, prev, cur, prev], out_specs=[cur, cur],
        out_shape=[jax.ShapeDtypeStruct((s_len, 256), f32)] * 2,
        compiler_params=_cparams(("parallel", "parallel")),
    )(q, k, k, v, v)


def _attn_bwd(q, k, v, do, lse, dlt, tabs, dil, name):
    s_len = q.shape[0]
    nblk = s_len // BLK

    def body(q_ref, qn_ref, kc_ref, kp_ref, vc_ref, vp_ref, do_ref, don_ref, l_ref, ln_ref, d_ref, dn_ref,
             c_ref, s1_ref, s2_ref, dq_ref, dk_ref, dv_ref):
        b = pl.program_id(0)
        has_prev = b >= dil
        has_next = b + dil < nblk
        low, up = _tri_masks()
        dqs, dks, dvs = [], [], []
        for hh in range(2):
            sl = slice(hh * HEAD_DIM, (hh + 1) * HEAD_DIM)
            one = slice(hh * HEAD_DIM, hh * HEAD_DIM + 1)
            qc, qn = _mx(q_ref[:, sl]), _mx(qn_ref[:, sl])
            kc, kp = _mx(kc_ref[:, sl]), _mx(kp_ref[:, sl])
            vc, vp = _mx(vc_ref[:, sl]), _mx(vp_ref[:, sl])
            doc, don = _mx(do_ref[:, sl]), _mx(don_ref[:, sl])
            lc, ln = l_ref[:, one], ln_ref[:, one]
            dc, dn = d_ref[:, one], dn_ref[:, one]
            p_a = jnp.where(low, jnp.exp(_dot_nt(qc, kc) * ATTN_SCALE - lc), 0.0)
            ds_a = _mx(p_a * (_dot_nt(doc, vc) - dc) * ATTN_SCALE)
            p_b = jnp.where(up & has_prev, jnp.exp(_dot_nt(qc, kp) * ATTN_SCALE - lc), 0.0)
            ds_b = _mx(p_b * (_dot_nt(doc, vp) - dc) * ATTN_SCALE)
            p_c = jnp.where(up & has_next, jnp.exp(_dot_nt(qn, kc) * ATTN_SCALE - ln), 0.0)
            ds_c = _mx(p_c * (_dot_nt(don, vc) - dn) * ATTN_SCALE)
            dqs.append(_dot(ds_a, kc) + _dot(ds_b, kp))
            dks.append(_dot_tn(ds_a, qc) + _dot_tn(ds_c, qn))
            dvs.append(_dot_tn(_mx(p_a), doc) + _dot_tn(_mx(p_c), don))
        c, s1, s2 = c_ref[...], s1_ref[...], s2_ref[...]
        dq_ref[...] = _rope_transpose(jnp.concatenate(dqs, axis=1), c, s1, s2, 128)
        dk_ref[...] = _rope_transpose(jnp.concatenate(dks, axis=1), c, s1, s2, 128)
        dv_ref[...] = jnp.concatenate(dvs, axis=1)

    cur = pl.BlockSpec((BLK, 128), lambda b, hp: (b, hp))
    prev = pl.BlockSpec((BLK, 128), lambda b, hp: (jnp.maximum(b - dil, 0), hp))
    nxt = pl.BlockSpec((BLK, 128), lambda b, hp: (jnp.minimum(b + dil, nblk - 1), hp))
    tab = pl.BlockSpec((BLK, 128), lambda b, hp: (b, 0))
    return pl.pallas_call(
        body, name=name, grid=(nblk, 2),
        in_specs=[cur, nxt, cur, prev, cur, prev, cur, nxt, cur, nxt, cur, nxt, tab, tab, tab],
        out_specs=[cur, cur, cur], out_shape=[jax.ShapeDtypeStruct((s_len, 256), f32)] * 3,
        compiler_params=_cparams(("parallel", "parallel")),
    )(q, q, k, k, v, v, do, do, lse, lse, dlt, dlt, *tabs)


def _loss_head(h, g, target, name):
    s_len, d = h.shape

    def body(h_ref, g_ref, t_ref, loss_ref, dh_ref, dg_ref):
        i = pl.program_id(0)
        x = h_ref[...]
        gv = g_ref[...]
        r = lax.rsqrt(jnp.mean(x * x, axis=-1, keepdims=True) + EPS)
        xh = x * r
        diff = xh * gv - t_ref[...]
        part = 0.5 * jnp.sum(jnp.mean(diff * diff, axis=-1, keepdims=True), axis=0, keepdims=True)
        dy = diff * (1.0 / d)
        dxh = dy * gv
        dh_ref[...] = r * (dxh - xh * jnp.mean(dxh * xh, axis=-1, keepdims=True))
        dgsum = jnp.sum(dy * xh, axis=0, keepdims=True)
        lossb = jnp.broadcast_to(part, (8, 128))

        @pl.when(i == 0)
        def _():
            loss_ref[...] = lossb
            dg_ref[...] = dgsum

        @pl.when(i > 0)
        def _():
            loss_ref[...] += lossb
            dg_ref[...] += dgsum

    row = lambda i: (i, 0)
    return pl.pallas_call(
        body, name=name, grid=(s_len // TM,),
        in_specs=[pl.BlockSpec((TM, d), row), pl.BlockSpec((1, d), lambda i: (0, 0)), pl.BlockSpec((TM, d), row)],
        out_specs=[pl.BlockSpec((8, 128), lambda i: (0, 0)), pl.BlockSpec((TM, d), row),
                   pl.BlockSpec((1, d), lambda i: (0, 0))],
        out_shape=[jax.ShapeDtypeStruct((8, 128), f32), jax.ShapeDtypeStruct((s_len, d), f32),
                   jax.ShapeDtypeStruct((1, d), f32)],
        compiler_params=_cparams(("arbitrary",)),
    )(h, g, target)


def _rope_tables(positions):
    inv_freq = ROPE_THETA ** (-jnp.arange(0, ROT_DIM, 2, dtype=f32) / ROT_DIM)
    ang = positions.astype(f32)[:, None] * inv_freq
    cos, sin = jnp.cos(ang), jnp.sin(ang)
    s_len = positions.shape[0]
    zero8, rest = jnp.zeros((s_len, 8), f32), jnp.zeros((s_len, HEAD_DIM - ROT_DIM), f32)
    c = jnp.concatenate([cos, cos, jnp.ones((s_len, HEAD_DIM - ROT_DIM), f32)], axis=1)
    s1 = jnp.concatenate([-sin, zero8, rest], axis=1)
    s2 = jnp.concatenate([zero8, sin, rest], axis=1)
    return c, s1, s2


def _block_diag(pool_w):
    out = jnp.zeros((POOL_WIDTH, POOL_WIDTH), pool_w.dtype)
    for g in range(4):
        out = lax.dynamic_update_slice(out, pool_w[g], (g * POOL_GC, g * POOL_GC))
    return out


def _layer_fwd(h, p_l, wts, small, layer, tabs256):
    nm = f"l{layer}_"
    z, hn1 = _norm_matmul(h, small["norm1"][layer][None], wts["w_in"], layer, 512, nm + "in_proj", rope=tabs256)
    outs, lses = [], []
    for g, dil in enumerate(DILATIONS):
        c0 = POOL_WIDTH + 256 * g
        qg = _to_strided(z[:, c0:c0 + 256], dil)
        kg = _to_strided(z[:, c0 + ATTN_WIDTH:c0 + ATTN_WIDTH + 256], dil)
        vg = _to_strided(z[:, c0 + 2 * ATTN_WIDTH:c0 + 2 * ATTN_WIDTH + 256], dil)
        o, lse = _attn_fwd(qg, kg, vg, dil, nm + f"attn_fwd{g}")
        outs.append(_from_strided(o, dil))
        lses.append(_from_strided(lse, dil))
    wbd = _mx(_block_diag(small["pool_w"][layer]))
    scale = small["pool_scale"][layer][None]
    m = _mixer_merge(z, wbd, scale, outs, lses, nm + "mixer_merge")
    h1 = _matmul_residual(m, wts["w_out"], layer, h, nm + "out_proj")
    a, hn2 = _norm_matmul(h1, small["norm2"][layer][None], wts["w_up"], layer, 1024, nm + "up_proj")
    h2 = _matmul_residual(a, wts["w_down"], layer, h1, nm + "down_proj", act=True)
    h3, gl, hn3 = _gate_ple_fwd(h2, small["norm3"][layer][None], wts["w_gate"], wts["w_ple"], layer, p_l, nm + "gate_ple")
    saved = dict(h=h, z=z, hn1=hn1, outs=outs, lses=lses, wbd=wbd, scale=scale, m=m, h1=h1, a=a, hn2=hn2, h2=h2,
                 gl=gl, hn3=hn3)
    return h3, saved


def _layer_bwd(dh3, sv, p_l, wts, small, layer, tabs128, grads):
    nm = f"l{layer}_"
    gb = {}
    de, dgl = _gate_ple_bwd(dh3, sv["gl"], p_l, wts["w_ple"], layer, nm + "gate_ple_bwd")
    gb["w_gate"] = _weight_grad(sv["hn3"], dgl, layer, grads.get("w_gate"), nm + "dw_gate")
    gb["w_ple"] = _weight_grad(p_l, de, layer, grads.get("w_ple"), nm + "dw_ple")
    dh2, dg3 = _matmul_nt_norm_bwd(dgl, wts["w_gate"], layer, sv["h2"], small["norm3"][layer][None], dh3, nm + "gate_bwd")
    act, da = _down_bwd(dh2, wts["w_down"], layer, sv["a"], nm + "down_bwd")
    gb["w_down"] = _weight_grad(act, dh2, layer, grads.get("w_down"), nm + "dw_down")
    gb["w_up"] = _weight_grad(sv["hn2"], da, layer, grads.get("w_up"), nm + "dw_up")
    dh1, dg2 = _matmul_nt_norm_bwd(da, wts["w_up"], layer, sv["h1"], small["norm2"][layer][None], dh2, nm + "up_bwd")
    dm = _matmul_nt(dh1, wts["w_out"], layer, nm + "out_bwd")
    gb["w_out"] = _weight_grad(sv["m"], dh1, layer, grads.get("w_out"), nm + "dw_out")
    do, dlt = _combine_bwd(dm, sv["outs"], sv["lses"], nm + "combine_bwd")
    du, dwbd, dscale = _pool_bwd(sv["z"], dm, sv["wbd"], sv["scale"], nm + "pool_bwd")
    z = sv["z"]
    dqs, dks, dvs = [], [], []
    for g, dil in enumerate(DILATIONS):
        c0 = POOL_WIDTH + 256 * g
        sl = slice(256 * g, 256 * (g + 1))
        qg = _to_strided(z[:, c0:c0 + 256], dil)
        kg = _to_strided(z[:, c0 + ATTN_WIDTH:c0 + ATTN_WIDTH + 256], dil)
        vg = _to_strided(z[:, c0 + 2 * ATTN_WIDTH:c0 + 2 * ATTN_WIDTH + 256], dil)
        dog = _to_strided(do[:, sl], dil)
        dlg = _to_strided(dlt[:, sl], dil)
        lg = _to_strided(sv["lses"][g], dil)
        dq, dk, dv = _attn_bwd(qg, kg, vg, dog, lg, dlg, tabs128[g], dil, nm + f"attn_bwd{g}")
        dqs.append(_from_strided(dq, dil))
        dks.append(_from_strided(dk, dil))
        dvs.append(_from_strided(dv, dil))
    dz = jnp.concatenate([du] + dqs + dks + dvs, axis=1)
    gb["w_in"] = _weight_grad(sv["hn1"], dz, layer, grads.get("w_in"), nm + "dw_in")
    dh0, dg1 = _matmul_nt_norm_bwd(dz, wts["w_in"], layer, sv["h"], small["norm1"][layer][None], dh1, nm + "in_bwd",
                                   tk=512)
    dpool_w = jnp.stack([dwbd[g * POOL_GC:(g + 1) * POOL_GC, g * POOL_GC:(g + 1) * POOL_GC] for g in range(4)])
    sg = dict(norm1=dg1[0], norm2=dg2[0], norm3=dg3[0], pool_w=dpool_w, pool_scale=dscale[0])
    return dh0, gb, sg


def _local_step(x, p, positions, wts, small, target):
    c64, s164, s264 = _rope_tables(positions)
    tabs256 = tuple(jnp.tile(t, (1, 4)) for t in (c64, s164, s264))
    tabs128 = [tuple(_to_strided(jnp.tile(t, (1, 2)), dil) for t in (c64, s164, s264)) for dil in DILATIONS]
    h = x
    saved = []
    for layer in range(2):
        h, sv = _layer_fwd(h, p[layer], wts, small, layer, tabs256)
        saved.append(sv)
    loss, dh, dgf = _loss_head(h, small["final_norm"][None], target, "loss_head")
    grads = {}
    sgs = [None, None]
    for layer in (1, 0):
        dh, grads, sgs[layer] = _layer_bwd(dh, saved[layer], p[layer], wts, small, layer, tabs128[:], grads)
    small_grads = {k: jnp.stack([sgs[0][k], sgs[1][k]]) for k in sgs[0]}
    small_grads["final_norm"] = dgf[0]
    return loss, dh, grads, small_grads


HBM = pl.BlockSpec(memory_space=pltpu.HBM)


def _my_place():
    return lax.axis_index("x"), lax.axis_index("y"), lax.axis_index("c")


def _other_chips(x, y):
    return [(1 - x, y), (x, 1 - y), (1 - x, 1 - y)]


def _window(ref, name, chip):
    k, n = _shard_shape(name)
    if COL_SHARDED[name]:
        return ref.at[:, pl.ds(pl.multiple_of(chip * n, 128), n)]
    return ref.at[pl.ds(pl.multiple_of(chip * k, 128), k), :]


def _gather_weights(shards):
    names = list(BIG)

    def body(*refs):
        ins = refs[:len(names)]
        outs = refs[len(names):2 * len(names)]
        send_ici, recv_ici, send_d2d, recv_d2d, local_sems = refs[2 * len(names):]
        x, y, c = _my_place()
        me = 2 * x + y
        sibling = (x, y, 1 - c)
        chips = _other_chips(x, y)
        started = []
        for t, name in enumerate(names):
            for layer in range(2):
                cp = pltpu.make_async_copy(ins[t].at[layer], _window(outs[t].at[layer], name, me), local_sems.at[2 * t + layer])
                cp.start()
                started.append(cp)
        ici = []
        for t, name in enumerate(names):
            for j, (cx, cy) in enumerate(chips):
                cp = pltpu.make_async_remote_copy(
                    src_ref=ins[t].at[c], dst_ref=_window(outs[t].at[c], name, me),
                    send_sem=send_ici.at[3 * t + j], recv_sem=recv_ici.at[3 * t + j],
                    device_id=(cx, cy, c), device_id_type=MESH)
                cp.start()
                ici.append(cp)
        fwd = []
        for t, name in enumerate(names):
            for j, (cx, cy) in enumerate(chips):
                land = _window(outs[t].at[c], name, 2 * cx + cy)
                pltpu.make_async_remote_copy(
                    src_ref=land, dst_ref=land, send_sem=send_ici.at[3 * t + j], recv_sem=recv_ici.at[3 * t + j],
                    device_id=(cx, cy, c), device_id_type=MESH).wait_recv()
                cp = pltpu.make_async_remote_copy(
                    src_ref=land, dst_ref=land, send_sem=send_d2d.at[3 * t + j], recv_sem=recv_d2d.at[3 * t + j],
                    device_id=sibling, device_id_type=MESH)
                cp.start()
                fwd.append(cp)
        for t, name in enumerate(names):
            for j, (cx, cy) in enumerate(chips):
                land = _window(outs[t].at[1 - c], name, 2 * cx + cy)
                pltpu.make_async_remote_copy(
                    src_ref=land, dst_ref=land, send_sem=send_d2d.at[3 * t + j], recv_sem=recv_d2d.at[3 * t + j],
                    device_id=sibling, device_id_type=MESH).wait_recv()
        for cp in ici + fwd:
            cp.wait_send()
        for cp in started:
            cp.wait()

    nsem = 3 * len(names)
    outs = pl.pallas_call(
        body, name="gather_weights",
        in_specs=[HBM] * len(names), out_specs=[HBM] * len(names),
        out_shape=[jax.ShapeDtypeStruct((2,) + FULL_SHAPE[n], shards[n].dtype) for n in names],
        scratch_shapes=[pltpu.SemaphoreType.DMA((nsem,)), pltpu.SemaphoreType.DMA((nsem,)),
                        pltpu.SemaphoreType.DMA((nsem,)), pltpu.SemaphoreType.DMA((nsem,)),
                        pltpu.SemaphoreType.DMA((2 * len(names),))],
    )(*[shards[n] for n in names])
    return dict(zip(names, outs))


def _swap_layers(grads):
    names = list(BIG)

    def body(*refs):
        ins = refs[:len(names)]
        outs = refs[len(names):2 * len(names)]
        send_sems, recv_sems = refs[2 * len(names):]
        x, y, c = _my_place()
        sibling = (x, y, 1 - c)
        cps = []
        for t in range(len(names)):
            cp = pltpu.make_async_remote_copy(
                src_ref=ins[t].at[1 - c], dst_ref=outs[t], send_sem=send_sems.at[t], recv_sem=recv_sems.at[t],
                device_id=sibling, device_id_type=MESH)
            cp.start()
            cps.append(cp)
        for cp in cps:
            cp.wait()

    outs = pl.pallas_call(
        body, name="swap_layers", in_specs=[HBM] * len(names), out_specs=[HBM] * len(names),
        out_shape=[jax.ShapeDtypeStruct(FULL_SHAPE[n], f32) for n in names],
        scratch_shapes=[pltpu.SemaphoreType.DMA((len(names),)), pltpu.SemaphoreType.DMA((len(names),))],
    )(*[grads[n] for n in names])
    return dict(zip(names, outs))


def _chip_sum(grad, other, name):
    k, n = FULL_SHAPE[name]
    tr = min(k, 512)
    c = lax.axis_index("c")

    def body(c_ref, g_ref, o_ref, out_ref):
        out_ref[...] = (g_ref[...] + o_ref[...]).astype(out_ref.dtype)

    return pl.pallas_call(
        body, name="chip_sum_" + name,
        grid_spec=pltpu.PrefetchScalarGridSpec(
            num_scalar_prefetch=1, grid=(k // tr,),
            in_specs=[pl.BlockSpec((None, tr, n), lambda i, c_ref: (c_ref[0], i, 0)),
                      pl.BlockSpec((tr, n), lambda i, c_ref: (i, 0))],
            out_specs=pl.BlockSpec((tr, n), lambda i, c_ref: (i, 0))),
        out_shape=jax.ShapeDtypeStruct((k, n), COMM_DTYPE),
        compiler_params=_cparams(("parallel",)),
    )(jnp.reshape(c, (1,)).astype(jnp.int32), grad, other)


def _scatter_shards(sums):
    names = list(BIG)

    def body(*refs):
        ins = refs[:len(names)]
        outs = refs[len(names):2 * len(names)]
        send_sems, recv_sems, local_sems = refs[2 * len(names):]
        x, y, c = _my_place()
        me = 2 * x + y
        chips = _other_chips(x, y)
        cps, local = [], []
        for t, name in enumerate(names):
            cp = pltpu.make_async_copy(_window(ins[t], name, me), outs[t].at[me], local_sems.at[t])
            cp.start()
            local.append(cp)
            for j, (cx, cy) in enumerate(chips):
                cp = pltpu.make_async_remote_copy(
                    src_ref=_window(ins[t], name, 2 * cx + cy), dst_ref=outs[t].at[me],
                    send_sem=send_sems.at[3 * t + j], recv_sem=recv_sems.at[3 * t + j],
                    device_id=(cx, cy, c), device_id_type=MESH)
                cp.start()
                cps.append(cp)
        for t, name in enumerate(names):
            for j, (cx, cy) in enumerate(chips):
                land = outs[t].at[2 * cx + cy]
                pltpu.make_async_remote_copy(
                    src_ref=land, dst_ref=land, send_sem=send_sems.at[3 * t + j], recv_sem=recv_sems.at[3 * t + j],
                    device_id=(cx, cy, c), device_id_type=MESH).wait_recv()
        for cp in cps:
            cp.wait_send()
        for cp in local:
            cp.wait()

    nsem = 3 * len(names)
    outs = pl.pallas_call(
        body, name="scatter_shards", in_specs=[HBM] * len(names), out_specs=[HBM] * len(names),
        out_shape=[jax.ShapeDtypeStruct((N_CHIPS,) + _shard_shape(n), sums[n].dtype) for n in names],
        scratch_shapes=[pltpu.SemaphoreType.DMA((nsem,)), pltpu.SemaphoreType.DMA((nsem,)),
                        pltpu.SemaphoreType.DMA((len(names),))],
    )(*[sums[n] for n in names])
    return dict(zip(names, outs))


def _sum_slots(slots, name):
    k, n = _shard_shape(name)
    tr = min(k, 512)

    def body(s_ref, out_ref):
        acc = s_ref[0].astype(f32)
        for s in range(1, N_CHIPS):
            acc = acc + s_ref[s].astype(f32)
        out_ref[...] = acc

    return pl.pallas_call(
        body, name="sum_slots_" + name, grid=(k // tr,),
        in_specs=[pl.BlockSpec((N_CHIPS, tr, n), lambda i: (0, i, 0))],
        out_specs=pl.BlockSpec((tr, n), lambda i: (i, 0)),
        out_shape=jax.ShapeDtypeStruct((k, n), f32),
        compiler_params=_cparams(("parallel",)),
    )(slots)


def _pair_layers(mine):
    names = list(BIG)

    def body(*refs):
        ins = refs[:len(names)]
        outs = refs[len(names):2 * len(names)]
        send_sems, recv_sems, local_sems = refs[2 * len(names):]
        x, y, c = _my_place()
        sibling = (x, y, 1 - c)
        cps = []
        for t in range(len(names)):
            lc = pltpu.make_async_copy(ins[t], outs[t].at[c], local_sems.at[t])
            lc.start()
            cp = pltpu.make_async_remote_copy(
                src_ref=ins[t], dst_ref=outs[t].at[c], send_sem=send_sems.at[t], recv_sem=recv_sems.at[t],
                device_id=sibling, device_id_type=MESH)
            cp.start()
            cps.append((lc, cp))
        for t in range(len(names)):
            lc, cp = cps[t]
            cp.wait_send()
            land = outs[t].at[1 - c]
            pltpu.make_async_remote_copy(
                src_ref=land, dst_ref=land, send_sem=send_sems.at[t], recv_sem=recv_sems.at[t],
                device_id=sibling, device_id_type=MESH).wait_recv()
            lc.wait()

    outs = pl.pallas_call(
        body, name="pair_layers", in_specs=[HBM] * len(names), out_specs=[HBM] * len(names),
        out_shape=[jax.ShapeDtypeStruct((2,) + _shard_shape(n), f32) for n in names],
        scratch_shapes=[pltpu.SemaphoreType.DMA((len(names),)), pltpu.SemaphoreType.DMA((len(names),)),
                        pltpu.SemaphoreType.DMA((len(names),))],
    )(*[mine[n] for n in names])
    return dict(zip(names, outs))


SMALL_ROWS = 320


def _allreduce_small(vec):
    n_dev = 8

    def body(v_ref, out_ref, buf_ref, send_sems, recv_sems):
        x, y, c = _my_place()
        me = 4 * x + 2 * y + c
        buf_ref[me] = v_ref[...]
        cps = []
        for k in range(1, n_dev):
            dx, dy, dc = (k >> 2) & 1, (k >> 1) & 1, k & 1
            peer = (x ^ dx, y ^ dy, c ^ dc)
            cp = pltpu.make_async_remote_copy(
                src_ref=v_ref, dst_ref=buf_ref.at[me], send_sem=send_sems.at[k - 1], recv_sem=recv_sems.at[k - 1],
                device_id=peer, device_id_type=MESH)
            cp.start()
            cps.append(cp)
        for k in range(1, n_dev):
            dx, dy, dc = (k >> 2) & 1, (k >> 1) & 1, k & 1
            src = 4 * (x ^ dx) + 2 * (y ^ dy) + (c ^ dc)
            land = buf_ref.at[src]
            pltpu.make_async_remote_copy(
                src_ref=land, dst_ref=land, send_sem=send_sems.at[k - 1], recv_sem=recv_sems.at[k - 1],
                device_id=(x ^ dx, y ^ dy, c ^ dc), device_id_type=MESH).wait_recv()
        for cp in cps:
            cp.wait_send()
        acc = buf_ref[0]
        for s in range(1, n_dev):
            acc = acc + buf_ref[s]
        out_ref[...] = acc

    return pl.pallas_call(
        body, name="allreduce_small",
        in_specs=[pl.BlockSpec(memory_space=pltpu.VMEM)], out_specs=pl.BlockSpec(memory_space=pltpu.VMEM),
        out_shape=jax.ShapeDtypeStruct((SMALL_ROWS, 128), f32),
        scratch_shapes=[pltpu.VMEM((n_dev, SMALL_ROWS, 128), f32), pltpu.SemaphoreType.DMA((n_dev - 1,)),
                        pltpu.SemaphoreType.DMA((n_dev - 1,))],
    )(vec)


def _adamw(w, g, m, v, name):
    rows, cols = w.shape
    tr = rows
    for cand in (512, 256, 128, 64, 32, 16, 8):
        if rows % cand == 0 and cand * cols * 4 <= 2 * 1024 * 1024:
            tr = cand
            break
    c1 = np.float32(1.0 - ADAM_B1 ** ADAM_STEP)
    c2 = np.float32(1.0 - ADAM_B2 ** ADAM_STEP)

    def body(w_ref, g_ref, m_ref, v_ref, d_ref, mo_ref, vo_ref):
        gv = g_ref[...]
        mn = ADAM_B1 * m_ref[...] + (1.0 - ADAM_B1) * gv
        vn = ADAM_B2 * v_ref[...] + (1.0 - ADAM_B2) * (gv * gv)
        mo_ref[...] = mn
        vo_ref[...] = vn
        d_ref[...] = -ADAM_LR * ((mn / c1) / (jnp.sqrt(vn / c2) + ADAM_EPS) + ADAM_WD * w_ref[...])

    blk = pl.BlockSpec((tr, cols), lambda i: (i, 0))
    return pl.pallas_call(
        body, name="adamw_" + name, grid=(rows // tr,), in_specs=[blk] * 4, out_specs=[blk] * 3,
        out_shape=[jax.ShapeDtypeStruct((rows, cols), f32)] * 3,
        compiler_params=_cparams(("parallel",)),
    )(w, g, m, v)


SMALL = ("norm1", "pool_w", "pool_scale", "norm2", "norm3", "final_norm")
ORDER = ("norm1", "w_in", "pool_w", "pool_scale", "w_out", "norm2", "w_up", "w_down", "norm3", "w_gate", "w_ple",
         "final_norm")


def _pack_small(tree, extra=None):
    parts = [tree[n].reshape(-1) for n in SMALL]
    if extra is not None:
        parts.append(extra.reshape(-1))
    flat = jnp.concatenate(parts)
    return jnp.pad(flat, (0, SMALL_ROWS * 128 - flat.shape[0])).reshape(SMALL_ROWS, 128)


def _unpack_small(packed, like):
    flat = packed.reshape(-1)
    out, off = {}, 0
    for n in SMALL:
        size = int(np.prod(like[n].shape))
        out[n] = flat[off:off + size].reshape(like[n].shape)
        off += size
    return out, flat[off]


def kernel(x, p, positions, norm1, w_in, pool_w, pool_scale, w_out, norm2, w_up, w_down, norm3, w_gate, w_ple, final_norm, loss_target, m_norm1, m_w_in, m_pool_w, m_pool_scale, m_w_out, m_norm2, m_w_up, m_w_down, m_norm3, m_w_gate, m_w_ple, m_final_norm, v_norm1, v_w_in, v_pool_w, v_pool_scale, v_w_out, v_norm2, v_w_up, v_w_down, v_norm3, v_w_gate, v_w_ple, v_final_norm):
    w = dict(norm1=norm1, w_in=w_in, pool_w=pool_w, pool_scale=pool_scale, w_out=w_out, norm2=norm2, w_up=w_up,
             w_down=w_down, norm3=norm3, w_gate=w_gate, w_ple=w_ple, final_norm=final_norm)
    m = dict(norm1=m_norm1, w_in=m_w_in, pool_w=m_pool_w, pool_scale=m_pool_scale, w_out=m_w_out, norm2=m_norm2,
             w_up=m_w_up, w_down=m_w_down, norm3=m_norm3, w_gate=m_w_gate, w_ple=m_w_ple, final_norm=m_final_norm)
    v = dict(norm1=v_norm1, w_in=v_w_in, pool_w=v_pool_w, pool_scale=v_pool_scale, w_out=v_w_out, norm2=v_norm2,
             w_up=v_w_up, w_down=v_w_down, norm3=v_norm3, w_gate=v_w_gate, w_ple=v_w_ple, final_norm=v_final_norm)
    small = {n: w[n] for n in SMALL}

    full = _gather_weights({n: _mx(w[n]) for n in BIG})
    loss8, dx, grads, small_grads = _local_step(x[0], p[:, 0], positions[0], full, small, loss_target[0])

    other = _swap_layers(grads)
    sums = {n: _chip_sum(grads[n], other[n], n) for n in BIG}
    slots = _scatter_shards(sums)
    mine = {n: _sum_slots(slots[n], n) for n in BIG}
    gsh = _pair_layers(mine)

    red = _allreduce_small(_pack_small(small_grads, loss8[0, 0]))
    g_small, loss = _unpack_small(red, small)

    g_out, d_out, m_out, v_out = {}, {}, {}, {}
    for n in BIG:
        shp = w[n].shape
        two = lambda a: a.reshape(shp[0] * shp[1], shp[2])
        d2, m2, v2 = _adamw(two(w[n]), two(gsh[n]), two(m[n]), two(v[n]), n)
        g_out[n], d_out[n], m_out[n], v_out[n] = gsh[n], d2.reshape(shp), m2.reshape(shp), v2.reshape(shp)
    d2, m2, v2 = _adamw(_pack_small(small), red, _pack_small({n: m[n] for n in SMALL}),
                        _pack_small({n: v[n] for n in SMALL}), "small")
    for tree, packed in ((d_out, d2), (m_out, m2), (v_out, v2)):
        tree.update(_unpack_small(packed, small)[0])
    g_out.update(g_small)

    return (loss, dx[None], *[g_out[n] for n in ORDER], *[d_out[n] for n in ORDER], *[m_out[n] for n in ORDER],
            *[v_out[n] for n in ORDER])
```

```python
import functools

import jax
import jax.numpy as jnp
import numpy as np
from jax import lax
from jax.experimental import pallas as pl
from jax.experimental.pallas import tpu as pltpu

f32 = jnp.float32
MXU_DTYPE = jnp.bfloat16
COMM_DTYPE = jnp.bfloat16

D_MODEL = 1024
POOL_WIDTH = 256
POOL_GC = 64
ATTN_WIDTH = 768
HEAD_DIM = 64
N_IN = POOL_WIDTH + 3 * ATTN_WIDTH
D_FF = 4096
PLE_DIM = 256
BLK = 128
DILATIONS = (1, 4, 16)
ROT_DIM = 16
ROPE_THETA = 500000.0
EPS = 1e-6
ATTN_SCALE = HEAD_DIM ** -0.5
NEG_BIG = -1e30

ADAM_LR, ADAM_B1, ADAM_B2, ADAM_EPS, ADAM_WD, ADAM_STEP = 0.001, 0.9, 0.999, 1e-08, 0.01, 10

TM = 512
HALO = 16
VMEM_LIMIT = 48 * 1024 * 1024
N_CHIPS = 4
MESH = pl.DeviceIdType.MESH

BIG = ("w_in", "w_out", "w_up", "w_down", "w_gate", "w_ple")
FULL_SHAPE = {"w_in": (D_MODEL, N_IN), "w_out": (D_MODEL, D_MODEL), "w_up": (D_MODEL, D_FF),
              "w_down": (D_FF, D_MODEL), "w_gate": (D_MODEL, D_MODEL), "w_ple": (PLE_DIM, D_MODEL)}
COL_SHARDED = {"w_in": True, "w_out": False, "w_up": True, "w_down": False, "w_gate": False, "w_ple": True}


def _shard_shape(name):
    k, n = FULL_SHAPE[name]
    return (k, n // N_CHIPS) if COL_SHARDED[name] else (k // N_CHIPS, n)


def _cparams(sem=None, vmem=VMEM_LIMIT):
    return pltpu.CompilerParams(dimension_semantics=sem, vmem_limit_bytes=vmem)


def _mx(x):
    return x.astype(MXU_DTYPE)


def _dot(a, b):
    return jnp.dot(a, b, preferred_element_type=f32)


def _dot_nt(a, b):
    return lax.dot_general(a, b, (((1,), (1,)), ((), ())), preferred_element_type=f32)


def _dot_tn(a, b):
    return lax.dot_general(a, b, (((0,), (0,)), ((), ())), preferred_element_type=f32)


def _sigmoid(x):
    return 1.0 / (1.0 + jnp.exp(-x))


def _rope_apply(y, c, s1, s2, width):
    return y * c + pltpu.roll(y, width - 8, axis=1) * s1 + pltpu.roll(y, 8, axis=1) * s2


def _rope_transpose(dy, c, s1, s2, width):
    return dy * c + pltpu.roll(dy * s1, 8, axis=1) + pltpu.roll(dy * s2, width - 8, axis=1)


def _norm_matmul(h, g, w, layer, tn, name, rope=None):
    s_len, d = h.shape
    n = w.shape[2]
    nj = n // tn

    def body(*refs):
        if rope is None:
            h_ref, g_ref, w_ref, y_ref, hn_ref = refs
        else:
            h_ref, g_ref, w_ref, c_ref, s1_ref, s2_ref, y_ref, hn_ref = refs
        j = pl.program_id(1)

        @pl.when(j == 0)
        def _():
            x = h_ref[...]
            r = lax.rsqrt(jnp.mean(x * x, axis=-1, keepdims=True) + EPS)
            hn_ref[...] = ((x * r) * g_ref[...]).astype(hn_ref.dtype)

        y = _dot(hn_ref[...], w_ref[...])
        if rope is not None:
            reps = tn // 128
            c = jnp.concatenate([c_ref[...]] * reps, axis=1)
            s1 = jnp.concatenate([s1_ref[...]] * reps, axis=1)
            s2 = jnp.concatenate([s2_ref[...]] * reps, axis=1)
            col = j * tn + lax.broadcasted_iota(jnp.int32, y.shape, 1)
            is_qk = (col >= POOL_WIDTH) & (col < POOL_WIDTH + 2 * ATTN_WIDTH)
            y = jnp.where(is_qk, _rope_apply(y, c, s1, s2, tn), y)
        y_ref[...] = y

    in_specs = [pl.BlockSpec((TM, d), lambda i, j: (i, 0)),
                pl.BlockSpec((1, d), lambda i, j: (0, 0)),
                pl.BlockSpec((None, d, tn), lambda i, j: (layer, 0, j))]
    args = [h, g, w]
    if rope is not None:
        in_specs += [pl.BlockSpec((TM, 128), lambda i, j: (i, 0))] * 3
        args += list(rope)
    return pl.pallas_call(
        body, name=name, grid=(s_len // TM, nj), in_specs=in_specs,
        out_specs=[pl.BlockSpec((TM, tn), lambda i, j: (i, j)), pl.BlockSpec((TM, d), lambda i, j: (i, 0))],
        out_shape=[jax.ShapeDtypeStruct((s_len, n), f32), jax.ShapeDtypeStruct((s_len, d), MXU_DTYPE)],
        compiler_params=_cparams(("parallel", "arbitrary")),
    )(*args)


def _matmul_residual(a, w, layer, res, name, act=False, tk=1024):
    s_len, k_dim = a.shape
    n = w.shape[2]
    nk = k_dim // tk

    def body(a_ref, w_ref, res_ref, o_ref):
        k = pl.program_id(1)
        x = a_ref[...]
        if act:
            r = jnp.maximum(x, 0.0)
            x = r * r
        part = _dot(_mx(x), w_ref[...])

        @pl.when(k == 0)
        def _():
            o_ref[...] = res_ref[...] + part

        @pl.when(k > 0)
        def _():
            o_ref[...] += part

    return pl.pallas_call(
        body, name=name, grid=(s_len // TM, nk),
        in_specs=[pl.BlockSpec((TM, tk), lambda i, k: (i, k)),
                  pl.BlockSpec((None, tk, n), lambda i, k: (layer, k, 0)),
                  pl.BlockSpec((TM, n), lambda i, k: (i, 0))],
        out_specs=pl.BlockSpec((TM, n), lambda i, k: (i, 0)),
        out_shape=jax.ShapeDtypeStruct((s_len, n), f32),
        compiler_params=_cparams(("parallel", "arbitrary")),
    )(a, w, res)


def _gate_ple_fwd(h2, g, w_gate, w_ple, layer, p, name):
    s_len, d = h2.shape

    def body(h_ref, g_ref, wg_ref, p_ref, wp_ref, h3_ref, gl_ref, hn_ref):
        x = h_ref[...]
        r = lax.rsqrt(jnp.mean(x * x, axis=-1, keepdims=True) + EPS)
        hn = ((x * r) * g_ref[...]).astype(hn_ref.dtype)
        hn_ref[...] = hn
        gl = _dot(hn, wg_ref[...])
        gl_ref[...] = gl
        e = _dot(_mx(p_ref[...]), wp_ref[...])
        h3_ref[...] = x + _sigmoid(gl) * e

    row = lambda i: (i, 0)
    return pl.pallas_call(
        body, name=name, grid=(s_len // TM,),
        in_specs=[pl.BlockSpec((TM, d), row), pl.BlockSpec((1, d), lambda i: (0, 0)),
                  pl.BlockSpec((None, d, d), lambda i: (layer, 0, 0)), pl.BlockSpec((TM, PLE_DIM), row),
                  pl.BlockSpec((None, PLE_DIM, d), lambda i: (layer, 0, 0))],
        out_specs=[pl.BlockSpec((TM, d), row)] * 3,
        out_shape=[jax.ShapeDtypeStruct((s_len, d), f32), jax.ShapeDtypeStruct((s_len, d), f32),
                   jax.ShapeDtypeStruct((s_len, d), MXU_DTYPE)],
        compiler_params=_cparams(("parallel",)),
    )(h2, g, w_gate, p, w_ple)


def _gate_ple_bwd(dh3, gl, p, w_ple, layer, name):
    s_len, d = dh3.shape

    def body(dh_ref, gl_ref, p_ref, wp_ref, de_ref, dgl_ref):
        dh = dh_ref[...]
        gate = _sigmoid(gl_ref[...])
        e = _dot(_mx(p_ref[...]), wp_ref[...])
        de_ref[...] = (dh * gate).astype(de_ref.dtype)
        dgl_ref[...] = ((dh * e) * (gate * (1.0 - gate))).astype(dgl_ref.dtype)

    row = lambda i: (i, 0)
    return pl.pallas_call(
        body, name=name, grid=(s_len // TM,),
        in_specs=[pl.BlockSpec((TM, d), row), pl.BlockSpec((TM, d), row), pl.BlockSpec((TM, PLE_DIM), row),
                  pl.BlockSpec((None, PLE_DIM, d), lambda i: (layer, 0, 0))],
        out_specs=[pl.BlockSpec((TM, d), row)] * 2,
        out_shape=[jax.ShapeDtypeStruct((s_len, d), MXU_DTYPE)] * 2,
        compiler_params=_cparams(("parallel",)),
    )(dh3, gl, p, w_ple)


def _rmsnorm_bwd(dhn, x, g):
    r = lax.rsqrt(jnp.mean(x * x, axis=-1, keepdims=True) + EPS)
    xh = x * r
    dxh = dhn * g
    dx = r * (dxh - xh * jnp.mean(dxh * xh, axis=-1, keepdims=True))
    return dx, dhn * xh


def _matmul_nt_norm_bwd(dy, w, layer, h_prev, g, dres, name, tk=1024):
    s_len, k_dim = dy.shape
    d = h_prev.shape[1]
    nk = k_dim // tk

    def body(dy_ref, w_ref, h_ref, g_ref, dres_ref, dh_ref, dg_ref, acc_ref):
        i, k = pl.program_id(0), pl.program_id(1)
        part = _dot_nt(_mx(dy_ref[...]), w_ref[...])

        @pl.when(k == 0)
        def _():
            acc_ref[...] = part

        @pl.when(k > 0)
        def _():
            acc_ref[...] += part

        @pl.when(k == nk - 1)
        def _():
            dx, dgrow = _rmsnorm_bwd(acc_ref[...], h_ref[...], g_ref[...])
            dh_ref[...] = dres_ref[...] + dx
            dgsum = jnp.sum(dgrow, axis=0, keepdims=True)

            @pl.when(i == 0)
            def _():
                dg_ref[...] = dgsum

            @pl.when(i > 0)
            def _():
                dg_ref[...] += dgsum

    return pl.pallas_call(
        body, name=name, grid=(s_len // TM, nk),
        in_specs=[pl.BlockSpec((TM, tk), lambda i, k: (i, k)),
                  pl.BlockSpec((None, d, tk), lambda i, k: (layer, 0, k)),
                  pl.BlockSpec((TM, d), lambda i, k: (i, 0)),
                  pl.BlockSpec((1, d), lambda i, k: (0, 0)),
                  pl.BlockSpec((TM, d), lambda i, k: (i, 0))],
        out_specs=[pl.BlockSpec((TM, d), lambda i, k: (i, 0)), pl.BlockSpec((1, d), lambda i, k: (0, 0))],
        out_shape=[jax.ShapeDtypeStruct((s_len, d), f32), jax.ShapeDtypeStruct((1, d), f32)],
        scratch_shapes=[pltpu.VMEM((TM, d), f32)],
        compiler_params=_cparams(("arbitrary", "arbitrary")),
    )(dy, w, h_prev, g, dres)


def _down_bwd(dh2, w_down, layer, a, name, tf=1024):
    s_len, d = dh2.shape
    ff = a.shape[1]

    def body(dh_ref, w_ref, a_ref, act_ref, da_ref, dhb_ref):
        j = pl.program_id(1)

        @pl.when(j == 0)
        def _():
            dhb_ref[...] = _mx(dh_ref[...])

        dact = _dot_nt(dhb_ref[...], w_ref[...])
        r = jnp.maximum(a_ref[...], 0.0)
        act_ref[...] = (r * r).astype(act_ref.dtype)
        da_ref[...] = (dact * (2.0 * r)).astype(da_ref.dtype)

    return pl.pallas_call(
        body, name=name, grid=(s_len // TM, ff // tf),
        in_specs=[pl.BlockSpec((TM, d), lambda i, j: (i, 0)),
                  pl.BlockSpec((None, tf, d), lambda i, j: (layer, j, 0)),
                  pl.BlockSpec((TM, tf), lambda i, j: (i, j))],
        out_specs=[pl.BlockSpec((TM, tf), lambda i, j: (i, j))] * 2,
        out_shape=[jax.ShapeDtypeStruct((s_len, ff), MXU_DTYPE)] * 2,
        scratch_shapes=[pltpu.VMEM((TM, d), MXU_DTYPE)],
        compiler_params=_cparams(("parallel", "arbitrary")),
    )(dh2, w_down, a)


def _matmul_nt(dy, w, layer, name):
    s_len, n = dy.shape
    k_dim = w.shape[1]

    def body(dy_ref, w_ref, o_ref):
        o_ref[...] = _dot_nt(_mx(dy_ref[...]), w_ref[...])

    return pl.pallas_call(
        body, name=name, grid=(s_len // TM,),
        in_specs=[pl.BlockSpec((TM, n), lambda i: (i, 0)), pl.BlockSpec((None, k_dim, n), lambda i: (layer, 0, 0))],
        out_specs=pl.BlockSpec((TM, k_dim), lambda i: (i, 0)),
        out_shape=jax.ShapeDtypeStruct((s_len, k_dim), f32),
        compiler_params=_cparams(("parallel",)),
    )(dy, w)


def _weight_grad(a, b, layer, prev, name):
    s_len, k_dim = a.shape
    n = b.shape[1]
    tka = min(k_dim, 1024)
    tnb = n if n <= 1024 else (1024 if n % 1024 == 0 else 640)
    ns = s_len // TM

    def body(*refs):
        a_ref, b_ref = refs[0], refs[1]
        o_ref = refs[-1]
        s = pl.program_id(2)
        part = _dot_tn(_mx(a_ref[...]), _mx(b_ref[...]))

        @pl.when(s == 0)
        def _():
            o_ref[...] = part

        @pl.when(s > 0)
        def _():
            o_ref[...] += part

    in_specs = [pl.BlockSpec((TM, tka), lambda i, j, s: (s, i)), pl.BlockSpec((TM, tnb), lambda i, j, s: (s, j))]
    args = [a, b]
    aliases = {}
    if prev is not None:
        in_specs.append(pl.BlockSpec(memory_space=pl.ANY))
        args.append(prev)
        aliases = {2: 0}
    return pl.pallas_call(
        body, name=name, grid=(k_dim // tka, n // tnb, ns), in_specs=in_specs,
        out_specs=pl.BlockSpec((None, tka, tnb), lambda i, j, s: (layer, i, j)),
        out_shape=jax.ShapeDtypeStruct((2, k_dim, n), f32),
        input_output_aliases=aliases,
        compiler_params=_cparams(("parallel", "parallel", "arbitrary")),
    )(*args)


def _group_select(lane, x2, x4, x8, x16):
    grp = lane // POOL_GC
    return jnp.where(grp == 0, x2, jnp.where(grp == 1, x4, jnp.where(grp == 2, x8, x16)))


def _pool_window(lane):
    grp = lane // POOL_GC
    return jnp.where(grp == 0, 2, jnp.where(grp == 1, 4, jnp.where(grp == 2, 8, 16)))


def _pool_y(u, halo, i):
    xs = jnp.concatenate([jnp.where(i > 0, halo, 0.0), u], axis=0)
    s2 = xs + pltpu.roll(xs, 1, axis=0)
    s4 = s2 + pltpu.roll(s2, 2, axis=0)
    s8 = s4 + pltpu.roll(s4, 4, axis=0)
    s16 = s8 + pltpu.roll(s8, 8, axis=0)
    lane = lax.broadcasted_iota(jnp.int32, xs.shape, 1)
    sel = _group_select(lane, s2, s4, s8, s16)[HALO:, :]
    t = i * TM + lax.broadcasted_iota(jnp.int32, u.shape, 0)
    cnt = jnp.minimum(_pool_window(lax.broadcasted_iota(jnp.int32, u.shape, 1)), t + 1).astype(f32)
    return sel / cnt - u


def _group_weights(l0, l1, l2):
    mx = jnp.maximum(jnp.maximum(l0, l1), l2)
    e0, e1, e2 = jnp.exp(l0 - mx), jnp.exp(l1 - mx), jnp.exp(l2 - mx)
    den = e0 + e1 + e2
    return e0 / den, e1 / den, e2 / den


def _mixer_merge(z, wbd, scale, outs, lses, name):
    s_len = z.shape[0]

    def body(u_ref, halo_ref, wbd_ref, sc_ref, o0, o1, o2, l0, l1, l2, m_ref):
        i = pl.program_id(0)
        y = _pool_y(u_ref[...], halo_ref[...], i)
        pool = _dot(_mx(y), wbd_ref[...]) * sc_ref[...]
        w0, w1, w2 = _group_weights(l0[...], l1[...], l2[...])
        m_ref[...] = jnp.concatenate([pool, o0[...] * w0, o1[...] * w1, o2[...] * w2], axis=1).astype(m_ref.dtype)

    row = lambda i: (i, 0)
    blk = pl.BlockSpec((TM, 256), row)
    grp = [pl.BlockSpec((TM, 256), lambda i, g=g: (i, g)) for g in range(3)]
    return pl.pallas_call(
        body, name=name, grid=(s_len // TM,),
        in_specs=[blk, pl.BlockSpec((HALO, 256), lambda i: (jnp.maximum(i * (TM // HALO) - 1, 0), 0)),
                  pl.BlockSpec((256, 256), lambda i: (0, 0)), pl.BlockSpec((1, 256), lambda i: (0, 0))] + grp + grp,
        out_specs=pl.BlockSpec((TM, D_MODEL), row),
        out_shape=jax.ShapeDtypeStruct((s_len, D_MODEL), MXU_DTYPE),
        compiler_params=_cparams(("parallel",)),
    )(z, z, wbd, scale, outs, outs, outs, lses, lses, lses)


def _head_sums(x):
    r = lax.broadcasted_iota(jnp.int32, (256, 256), 0) // HEAD_DIM
    c = lax.broadcasted_iota(jnp.int32, (256, 256), 1) // HEAD_DIM
    ones = jnp.where(r == c, 1.0, 0.0).astype(jnp.bfloat16)
    hi = x.astype(jnp.bfloat16)
    lo = (x - hi.astype(f32)).astype(jnp.bfloat16)
    return _dot(hi, ones) + _dot(lo, ones)


def _combine_bwd(dm, outs, lses, name):
    s_len = dm.shape[0]

    def body(d0, d1, d2, o0, o1, o2, l0, l1, l2, do_ref, dl_ref):
        w = _group_weights(l0[...], l1[...], l2[...])
        da = (d0[...], d1[...], d2[...])
        o = (o0[...], o1[...], o2[...])
        dw = [_head_sums(da[g] * o[g]) for g in range(3)]
        t = w[0] * dw[0] + w[1] * dw[1] + w[2] * dw[2]
        do_ref[...] = jnp.concatenate([da[g] * w[g] for g in range(3)], axis=1)
        dl_ref[...] = jnp.concatenate([w[g] * t for g in range(3)], axis=1)

    grp = [pl.BlockSpec((TM, 256), lambda i, g=g: (i, g)) for g in range(3)]
    return pl.pallas_call(
        body, name=name, grid=(s_len // TM,),
        in_specs=[pl.BlockSpec((TM, 256), lambda i: (i, 1)), pl.BlockSpec((TM, 256), lambda i: (i, 2)),
                  pl.BlockSpec((TM, 256), lambda i: (i, 3))] + grp + grp,
        out_specs=[pl.BlockSpec((TM, ATTN_WIDTH), lambda i: (i, 0))] * 2,
        out_shape=[jax.ShapeDtypeStruct((s_len, ATTN_WIDTH), f32)] * 2,
        compiler_params=_cparams(("parallel",)),
    )(dm, dm, dm, outs, outs, outs, lses, lses, lses)


def _pool_bwd(z, dm, wbd, scale, name):
    s_len = z.shape[0]
    n_halo = s_len // HALO

    def body(u_ref, uh_ref, d_ref, dh_ref, wbd_ref, sc_ref, du_ref, dw_ref, dsc_ref):
        i = pl.program_id(0)
        last = pl.num_programs(0) - 1
        y = _pool_y(u_ref[...], uh_ref[...], i)
        yb = _mx(y)
        dpo = d_ref[...]
        sc = sc_ref[...]
        dsc = jnp.sum(dpo * _dot(yb, wbd_ref[...]), axis=0, keepdims=True)
        dwp = _dot_tn(yb, _mx(dpo * sc))

        @pl.when(i == 0)
        def _():
            dsc_ref[...] = dsc
            dw_ref[...] = dwp

        @pl.when(i > 0)
        def _():
            dsc_ref[...] += dsc
            dw_ref[...] += dwp

        ext = jnp.concatenate([dpo, jnp.where(i < last, dh_ref[...], 0.0)], axis=0)
        dy = _dot_nt(_mx(ext * sc), wbd_ref[...])
        t = i * TM + lax.broadcasted_iota(jnp.int32, ext.shape, 0)
        lane = lax.broadcasted_iota(jnp.int32, ext.shape, 1)
        e = dy / jnp.minimum(_pool_window(lane), t + 1).astype(f32)
        rows = ext.shape[0]
        f2 = e + pltpu.roll(e, rows - 1, axis=0)
        f4 = f2 + pltpu.roll(f2, rows - 2, axis=0)
        f8 = f4 + pltpu.roll(f4, rows - 4, axis=0)
        f16 = f8 + pltpu.roll(f8, rows - 8, axis=0)
        du_ref[...] = (_group_select(lane, f2, f4, f8, f16) - dy)[:TM, :]

    row = lambda i: (i, 0)
    blk = pl.BlockSpec((TM, 256), row)
    return pl.pallas_call(
        body, name=name, grid=(s_len // TM,),
        in_specs=[blk, pl.BlockSpec((HALO, 256), lambda i: (jnp.maximum(i * (TM // HALO) - 1, 0), 0)),
                  blk, pl.BlockSpec((HALO, 256), lambda i: (jnp.minimum((i + 1) * (TM // HALO), n_halo - 1), 0)),
                  pl.BlockSpec((256, 256), lambda i: (0, 0)), pl.BlockSpec((1, 256), lambda i: (0, 0))],
        out_specs=[blk, pl.BlockSpec((256, 256), lambda i: (0, 0)), pl.BlockSpec((1, 256), lambda i: (0, 0))],
        out_shape=[jax.ShapeDtypeStruct((s_len, N_IN), f32), jax.ShapeDtypeStruct((256, 256), f32),
                   jax.ShapeDtypeStruct((1, 256), f32)],
        compiler_params=_cparams(("arbitrary",)),
    )(z, z, dm, dm, wbd, scale)


def _to_strided(x, dil):
    if dil == 1:
        return x
    s_len, c = x.shape
    return x.reshape(s_len // (BLK * dil), BLK, dil, c).transpose(0, 2, 1, 3).reshape(s_len, c)


def _from_strided(x, dil):
    if dil == 1:
        return x
    s_len, c = x.shape
    return x.reshape(s_len // (BLK * dil), dil, BLK, c).transpose(0, 2, 1, 3).reshape(s_len, c)


def _tri_masks():
    qi = lax.broadcasted_iota(jnp.int32, (BLK, BLK), 0)
    ki = lax.broadcasted_iota(jnp.int32, (BLK, BLK), 1)
    return qi >= ki, ki >= qi


ATTN_SUPER_PER_STEP = (8, 2, 1)
Q_COL, K_COL, V_COL = POOL_WIDTH // 128, (POOL_WIDTH + ATTN_WIDTH) // 128, (POOL_WIDTH + 2 * ATTN_WIDTH) // 128


def _rows(ref, start, dil):
    if dil == 1:
        return ref[pl.ds(start, BLK), :]
    return ref[pl.ds(start, BLK, stride=dil), :]


def _set_rows(ref, start, dil, val):
    if dil == 1:
        ref[pl.ds(start, BLK), :] = val
    else:
        ref[pl.ds(start, BLK, stride=dil), :] = val


def _attn_fwd(z, g, prev, name):
    s_len = z.shape[0]
    dil, m = DILATIONS[g], ATTN_SUPER_PER_STEP[g]
    sbr = BLK * dil
    rows = sbr * m

    def body(*refs):
        q_ref, kc_ref, kp_ref, vc_ref, vp_ref = refs[:5]
        o_ref, l_ref = refs[-2:]
        st = pl.program_id(0)
        low, up = _tri_masks()
        for sb in range(m):
            valid = jnp.concatenate([up & (st > 0) if sb == 0 else up, low], axis=1)
            for r in range(dil):
                base = sb * sbr + r
                q = _rows(q_ref, base, dil)
                kc, vc = _rows(kc_ref, base, dil), _rows(vc_ref, base, dil)
                if sb == 0:
                    kp, vp = _rows(kp_ref, r, dil), _rows(vp_ref, r, dil)
                else:
                    kp, vp = _rows(kc_ref, base - sbr, dil), _rows(vc_ref, base - sbr, dil)
                outs, lses = [], []
                for hh in range(2):
                    sl = slice(hh * HEAD_DIM, (hh + 1) * HEAD_DIM)
                    k2 = jnp.concatenate([_mx(kp[:, sl]), _mx(kc[:, sl])], axis=0)
                    v2 = jnp.concatenate([_mx(vp[:, sl]), _mx(vc[:, sl])], axis=0)
                    s = jnp.where(valid, _dot_nt(_mx(q[:, sl]), k2) * ATTN_SCALE, NEG_BIG)
                    mx = jnp.max(s, axis=-1, keepdims=True)
                    e = jnp.exp(s - mx)
                    l = jnp.sum(e, axis=-1, keepdims=True)
                    outs.append(_dot(_mx(e / l), v2))
                    lses.append(jnp.broadcast_to(mx + jnp.log(l), (BLK, HEAD_DIM)))
                _set_rows(o_ref, base, dil, jnp.concatenate(outs, axis=1))
                _set_rows(l_ref, base, dil, jnp.concatenate(lses, axis=1))

    def cur(col):
        return pl.BlockSpec((rows, 128), lambda st, hp: (st, col + 2 * g + hp))

    def before(col):
        return pl.BlockSpec((sbr, 128), lambda st, hp: (jnp.maximum(st * m - 1, 0), col + 2 * g + hp))

    in_specs = [cur(Q_COL), cur(K_COL), before(K_COL), cur(V_COL), before(V_COL)]
    args = [z, z, z, z, z]
    aliases = {}
    if prev is not None:
        in_specs += [pl.BlockSpec(memory_space=pl.ANY)] * 2
        args += list(prev)
        aliases = {5: 0, 6: 1}
    return pl.pallas_call(
        body, name=name, grid=(s_len // rows, 2), in_specs=in_specs, out_specs=[cur(0), cur(0)],
        out_shape=[jax.ShapeDtypeStruct((s_len, ATTN_WIDTH), f32)] * 2, input_output_aliases=aliases,
        compiler_params=_cparams(("parallel", "parallel")),
    )(*args)


def _attn_bwd(z, do, lse, dlt, tabs, dz, g, name):
    s_len = z.shape[0]
    dil, m = DILATIONS[g], ATTN_SUPER_PER_STEP[g]
    sbr = BLK * dil
    rows = sbr * m
    nsteps = s_len // rows

    def body(q_ref, qn_ref, kc_ref, kp_ref, vc_ref, vp_ref, do_ref, don_ref, l_ref, ln_ref, d_ref, dn_ref,
             c_ref, s1_ref, s2_ref, dz_in, dz_ref, dq_buf, dk_buf, dv_buf, sems):
        del dz_in
        st, hp = pl.program_id(0), pl.program_id(1)
        low, up = _tri_masks()
        for sb in range(m):
            up_prev = up & (st > 0) if sb == 0 else up
            up_next = up & (st < nsteps - 1) if sb == m - 1 else up
            for r in range(dil):
                base = sb * sbr + r
                q, k, v = _rows(q_ref, base, dil), _rows(kc_ref, base, dil), _rows(vc_ref, base, dil)
                do_c, l_c, d_c = _rows(do_ref, base, dil), _rows(l_ref, base, dil), _rows(d_ref, base, dil)
                if sb == 0:
                    kp, vp = _rows(kp_ref, r, dil), _rows(vp_ref, r, dil)
                else:
                    kp, vp = _rows(kc_ref, base - sbr, dil), _rows(vc_ref, base - sbr, dil)
                if sb == m - 1:
                    qn, do_n = _rows(qn_ref, r, dil), _rows(don_ref, r, dil)
                    l_n, d_n = _rows(ln_ref, r, dil), _rows(dn_ref, r, dil)
                else:
                    qn, do_n = _rows(q_ref, base + sbr, dil), _rows(do_ref, base + sbr, dil)
                    l_n, d_n = _rows(l_ref, base + sbr, dil), _rows(d_ref, base + sbr, dil)
                dqs, dks, dvs = [], [], []
                for hh in range(2):
                    sl = slice(hh * HEAD_DIM, (hh + 1) * HEAD_DIM)
                    one = slice(hh * HEAD_DIM, hh * HEAD_DIM + 1)
                    qc, qx = _mx(q[:, sl]), _mx(qn[:, sl])
                    kc, kb = _mx(k[:, sl]), _mx(kp[:, sl])
                    vc, vb = _mx(v[:, sl]), _mx(vp[:, sl])
                    doc, dox = _mx(do_c[:, sl]), _mx(do_n[:, sl])
                    lc, lx, dc, dx = l_c[:, one], l_n[:, one], d_c[:, one], d_n[:, one]
                    p_a = jnp.where(low, jnp.exp(_dot_nt(qc, kc) * ATTN_SCALE - lc), 0.0)
                    ds_a = _mx(p_a * (_dot_nt(doc, vc) - dc) * ATTN_SCALE)
                    p_b = jnp.where(up_prev, jnp.exp(_dot_nt(qc, kb) * ATTN_SCALE - lc), 0.0)
                    ds_b = _mx(p_b * (_dot_nt(doc, vb) - dc) * ATTN_SCALE)
                    p_c = jnp.where(up_next, jnp.exp(_dot_nt(qx, kc) * ATTN_SCALE - lx), 0.0)
                    ds_c = _mx(p_c * (_dot_nt(dox, vc) - dx) * ATTN_SCALE)
                    dqs.append(_dot(ds_a, kc) + _dot(ds_b, kb))
                    dks.append(_dot_tn(ds_a, qc) + _dot_tn(ds_c, qx))
                    dvs.append(_dot_tn(_mx(p_a), doc) + _dot_tn(_mx(p_c), dox))
                c, s1, s2 = _rows(c_ref, base, dil), _rows(s1_ref, base, dil), _rows(s2_ref, base, dil)
                _set_rows(dq_buf, base, dil, _rope_transpose(jnp.concatenate(dqs, axis=1), c, s1, s2, 128))
                _set_rows(dk_buf, base, dil, _rope_transpose(jnp.concatenate(dks, axis=1), c, s1, s2, 128))
                _set_rows(dv_buf, base, dil, jnp.concatenate(dvs, axis=1))
        copies = []
        for t, (buf, col) in enumerate(((dq_buf, Q_COL), (dk_buf, K_COL), (dv_buf, V_COL))):
            lane0 = pl.multiple_of((col + 2 * g + hp) * 128, 128)
            dst = dz_ref.at[pl.ds(pl.multiple_of(st * rows, rows), rows), pl.ds(lane0, 128)]
            cp = pltpu.make_async_copy(buf, dst, sems.at[t])
            cp.start()
            copies.append(cp)
        for cp in copies:
            cp.wait()

    def cur(col):
        return pl.BlockSpec((rows, 128), lambda st, hp: (st, col + 2 * g + hp))

    def before(col):
        return pl.BlockSpec((sbr, 128), lambda st, hp: (jnp.maximum(st * m - 1, 0), col + 2 * g + hp))

    def after(col):
        return pl.BlockSpec((sbr, 128), lambda st, hp: (jnp.minimum((st + 1) * m, s_len // sbr - 1), col + 2 * g + hp))

    tab = pl.BlockSpec((rows, 128), lambda st, hp: (st, 0))
    return pl.pallas_call(
        body, name=name, grid=(nsteps, 2),
        in_specs=[cur(Q_COL), after(Q_COL), cur(K_COL), before(K_COL), cur(V_COL), before(V_COL),
                  cur(0), after(0), cur(0), after(0), cur(0), after(0), tab, tab, tab,
                  pl.BlockSpec(memory_space=pl.ANY)],
        out_specs=pl.BlockSpec(memory_space=pl.ANY),
        out_shape=jax.ShapeDtypeStruct(dz.shape, dz.dtype), input_output_aliases={15: 0},
        scratch_shapes=[pltpu.VMEM((rows, 128), f32)] * 3 + [pltpu.SemaphoreType.DMA((3,))],
        compiler_params=_cparams(("arbitrary", "arbitrary")),
    )(z, z, z, z, z, z, do, do, lse, lse, dlt, dlt, *tabs, dz)


def _attn_fwd_old(q, k, v, dil, name):
    s_len = q.shape[0]
    nblk = s_len // BLK

    def body(q_ref, kc_ref, kp_ref, vc_ref, vp_ref, o_ref, l_ref):
        b = pl.program_id(0)
        has_prev = b >= dil
        low, up = _tri_masks()
        valid = jnp.concatenate([up & has_prev, low], axis=1)
        outs, lses = [], []
        for hh in range(2):
            sl = slice(hh * HEAD_DIM, (hh + 1) * HEAD_DIM)
            qh = _mx(q_ref[:, sl])
            k2 = jnp.concatenate([_mx(kp_ref[:, sl]), _mx(kc_ref[:, sl])], axis=0)
            v2 = jnp.concatenate([_mx(vp_ref[:, sl]), _mx(vc_ref[:, sl])], axis=0)
            s = jnp.where(valid, _dot_nt(qh, k2) * ATTN_SCALE, NEG_BIG)
            m = jnp.max(s, axis=-1, keepdims=True)
            e = jnp.exp(s - m)
            l = jnp.sum(e, axis=-1, keepdims=True)
            outs.append(_dot(_mx(e / l), v2))
            lses.append(jnp.broadcast_to(m + jnp.log(l), (BLK, HEAD_DIM)))
        o_ref[...] = jnp.concatenate(outs, axis=1)
        l_ref[...] = jnp.concatenate(lses, axis=1)

    cur = pl.BlockSpec((BLK, 128), lambda b, hp: (b, hp))
    prev = pl.BlockSpec((BLK, 128), lambda b, hp: (jnp.maximum(b - dil, 0), hp))
    return pl.pallas_call(
        body, name=name, grid=(nblk, 2), in_specs=[cur, cur, prev, cur, prev], out_specs=[cur, cur],
        out_shape=[jax.ShapeDtypeStruct((s_len, 256), f32)] * 2,
        compiler_params=_cparams(("parallel", "parallel")),
    )(q, k, k, v, v)


def _attn_bwd_old(q, k, v, do, lse, dlt, tabs, dil, name):
    s_len = q.shape[0]
    nblk = s_len // BLK

    def body(q_ref, qn_ref, kc_ref, kp_ref, vc_ref, vp_ref, do_ref, don_ref, l_ref, ln_ref, d_ref, dn_ref,
             c_ref, s1_ref, s2_ref, dq_ref, dk_ref, dv_ref):
        b = pl.program_id(0)
        has_prev = b >= dil
        has_next = b + dil < nblk
        low, up = _tri_masks()
        dqs, dks, dvs = [], [], []
        for hh in range(2):
            sl = slice(hh * HEAD_DIM, (hh + 1) * HEAD_DIM)
            one = slice(hh * HEAD_DIM, hh * HEAD_DIM + 1)
            qc, qn = _mx(q_ref[:, sl]), _mx(qn_ref[:, sl])
            kc, kp = _mx(kc_ref[:, sl]), _mx(kp_ref[:, sl])
            vc, vp = _mx(vc_ref[:, sl]), _mx(vp_ref[:, sl])
            doc, don = _mx(do_ref[:, sl]), _mx(don_ref[:, sl])
            lc, ln = l_ref[:, one], ln_ref[:, one]
            dc, dn = d_ref[:, one], dn_ref[:, one]
            p_a = jnp.where(low, jnp.exp(_dot_nt(qc, kc) * ATTN_SCALE - lc), 0.0)
            ds_a = _mx(p_a * (_dot_nt(doc, vc) - dc) * ATTN_SCALE)
            p_b = jnp.where(up & has_prev, jnp.exp(_dot_nt(qc, kp) * ATTN_SCALE - lc), 0.0)
            ds_b = _mx(p_b * (_dot_nt(doc, vp) - dc) * ATTN_SCALE)
            p_c = jnp.where(up & has_next, jnp.exp(_dot_nt(qn, kc) * ATTN_SCALE - ln), 0.0)
            ds_c = _mx(p_c * (_dot_nt(don, vc) - dn) * ATTN_SCALE)
            dqs.append(_dot(ds_a, kc) + _dot(ds_b, kp))
            dks.append(_dot_tn(ds_a, qc) + _dot_tn(ds_c, qn))
            dvs.append(_dot_tn(_mx(p_a), doc) + _dot_tn(_mx(p_c), don))
        c, s1, s2 = c_ref[...], s1_ref[...], s2_ref[...]
        dq_ref[...] = _rope_transpose(jnp.concatenate(dqs, axis=1), c, s1, s2, 128)
        dk_ref[...] = _rope_transpose(jnp.concatenate(dks, axis=1), c, s1, s2, 128)
        dv_ref[...] = jnp.concatenate(dvs, axis=1)

    cur = pl.BlockSpec((BLK, 128), lambda b, hp: (b, hp))
    prev = pl.BlockSpec((BLK, 128), lambda b, hp: (jnp.maximum(b - dil, 0), hp))
    nxt = pl.BlockSpec((BLK, 128), lambda b, hp: (jnp.minimum(b + dil, nblk - 1), hp))
    tab = pl.BlockSpec((BLK, 128), lambda b, hp: (b, 0))
    return pl.pallas_call(
        body, name=name, grid=(nblk, 2),
        in_specs=[cur, nxt, cur, prev, cur, prev, cur, nxt, cur, nxt, cur, nxt, tab, tab, tab],
        out_specs=[cur, cur, cur], out_shape=[jax.ShapeDtypeStruct((s_len, 256), f32)] * 3,
        compiler_params=_cparams(("parallel", "parallel")),
    )(q, q, k, k, v, v, do, do, lse, lse, dlt, dlt, *tabs)


def _loss_head(h, g, target, name):
    s_len, d = h.shape

    def body(h_ref, g_ref, t_ref, loss_ref, dh_ref, dg_ref):
        i = pl.program_id(0)
        x = h_ref[...]
        gv = g_ref[...]
        r = lax.rsqrt(jnp.mean(x * x, axis=-1, keepdims=True) + EPS)
        xh = x * r
        diff = xh * gv - t_ref[...]
        part = 0.5 * jnp.sum(jnp.mean(diff * diff, axis=-1, keepdims=True), axis=0, keepdims=True)
        dy = diff * (1.0 / d)
        dxh = dy * gv
        dh_ref[...] = r * (dxh - xh * jnp.mean(dxh * xh, axis=-1, keepdims=True))
        dgsum = jnp.sum(dy * xh, axis=0, keepdims=True)
        lossb = jnp.broadcast_to(part, (8, 128))

        @pl.when(i == 0)
        def _():
            loss_ref[...] = lossb
            dg_ref[...] = dgsum

        @pl.when(i > 0)
        def _():
            loss_ref[...] += lossb
            dg_ref[...] += dgsum

    row = lambda i: (i, 0)
    return pl.pallas_call(
        body, name=name, grid=(s_len // TM,),
        in_specs=[pl.BlockSpec((TM, d), row), pl.BlockSpec((1, d), lambda i: (0, 0)), pl.BlockSpec((TM, d), row)],
        out_specs=[pl.BlockSpec((8, 128), lambda i: (0, 0)), pl.BlockSpec((TM, d), row),
                   pl.BlockSpec((1, d), lambda i: (0, 0))],
        out_shape=[jax.ShapeDtypeStruct((8, 128), f32), jax.ShapeDtypeStruct((s_len, d), f32),
                   jax.ShapeDtypeStruct((1, d), f32)],
        compiler_params=_cparams(("arbitrary",)),
    )(h, g, target)


def _rope_tables(positions):
    inv_freq = ROPE_THETA ** (-jnp.arange(0, ROT_DIM, 2, dtype=f32) / ROT_DIM)
    ang = positions.astype(f32)[:, None] * inv_freq
    cos, sin = jnp.cos(ang), jnp.sin(ang)
    s_len = positions.shape[0]
    zero8, rest = jnp.zeros((s_len, 8), f32), jnp.zeros((s_len, HEAD_DIM - ROT_DIM), f32)
    c = jnp.concatenate([cos, cos, jnp.ones((s_len, HEAD_DIM - ROT_DIM), f32)], axis=1)
    s1 = jnp.concatenate([-sin, zero8, rest], axis=1)
    s2 = jnp.concatenate([zero8, sin, rest], axis=1)
    return c, s1, s2


def _block_diag(pool_w):
    out = jnp.zeros((POOL_WIDTH, POOL_WIDTH), pool_w.dtype)
    for g in range(4):
        out = lax.dynamic_update_slice(out, pool_w[g], (g * POOL_GC, g * POOL_GC))
    return out


def _layer_fwd(h, p_l, wts, small, layer, tabs):
    nm = f"l{layer}_"
    z, hn1 = _norm_matmul(h, small["norm1"][layer][None], wts["w_in"], layer, 512, nm + "in_proj", rope=tabs)
    ol = None
    for g in range(3):
        ol = _attn_fwd(z, g, ol, nm + f"attn_fwd{g}")
    outs, lses = ol
    wbd = _mx(_block_diag(small["pool_w"][layer]))
    scale = small["pool_scale"][layer][None]
    m = _mixer_merge(z, wbd, scale, outs, lses, nm + "mixer_merge")
    h1 = _matmul_residual(m, wts["w_out"], layer, h, nm + "out_proj")
    a, hn2 = _norm_matmul(h1, small["norm2"][layer][None], wts["w_up"], layer, 1024, nm + "up_proj")
    h2 = _matmul_residual(a, wts["w_down"], layer, h1, nm + "down_proj", act=True)
    h3, gl, hn3 = _gate_ple_fwd(h2, small["norm3"][layer][None], wts["w_gate"], wts["w_ple"], layer, p_l, nm + "gate_ple")
    saved = dict(h=h, z=z, hn1=hn1, outs=outs, lses=lses, wbd=wbd, scale=scale, m=m, h1=h1, a=a, hn2=hn2, h2=h2,
                 gl=gl, hn3=hn3)
    return h3, saved


def _layer_bwd(dh3, sv, p_l, wts, small, layer, tabs128, grads):
    nm = f"l{layer}_"
    gb = {}
    de, dgl = _gate_ple_bwd(dh3, sv["gl"], p_l, wts["w_ple"], layer, nm + "gate_ple_bwd")
    gb["w_gate"] = _weight_grad(sv["hn3"], dgl, layer, grads.get("w_gate"), nm + "dw_gate")
    gb["w_ple"] = _weight_grad(p_l, de, layer, grads.get("w_ple"), nm + "dw_ple")
    dh2, dg3 = _matmul_nt_norm_bwd(dgl, wts["w_gate"], layer, sv["h2"], small["norm3"][layer][None], dh3, nm + "gate_bwd")
    act, da = _down_bwd(dh2, wts["w_down"], layer, sv["a"], nm + "down_bwd")
    gb["w_down"] = _weight_grad(act, dh2, layer, grads.get("w_down"), nm + "dw_down")
    gb["w_up"] = _weight_grad(sv["hn2"], da, layer, grads.get("w_up"), nm + "dw_up")
    dh1, dg2 = _matmul_nt_norm_bwd(da, wts["w_up"], layer, sv["h1"], small["norm2"][layer][None], dh2, nm + "up_bwd")
    dm = _matmul_nt(dh1, wts["w_out"], layer, nm + "out_bwd")
    gb["w_out"] = _weight_grad(sv["m"], dh1, layer, grads.get("w_out"), nm + "dw_out")
    do, dlt = _combine_bwd(dm, sv["outs"], sv["lses"], nm + "combine_bwd")
    dz, dwbd, dscale = _pool_bwd(sv["z"], dm, sv["wbd"], sv["scale"], nm + "pool_bwd")
    for g in range(3):
        dz = _attn_bwd(sv["z"], do, sv["lses"], dlt, tabs128, dz, g, nm + f"attn_bwd{g}")
    gb["w_in"] = _weight_grad(sv["hn1"], dz, layer, grads.get("w_in"), nm + "dw_in")
    dh0, dg1 = _matmul_nt_norm_bwd(dz, wts["w_in"], layer, sv["h"], small["norm1"][layer][None], dh1, nm + "in_bwd",
                                   tk=512)
    dpool_w = jnp.stack([dwbd[g * POOL_GC:(g + 1) * POOL_GC, g * POOL_GC:(g + 1) * POOL_GC] for g in range(4)])
    sg = dict(norm1=dg1[0], norm2=dg2[0], norm3=dg3[0], pool_w=dpool_w, pool_scale=dscale[0])
    return dh0, gb, sg


def _local_step(x, p, positions, wts, small, target):
    tabs128 = tuple(jnp.tile(t, (1, 2)) for t in _rope_tables(positions))
    h = x
    saved = []
    for layer in range(2):
        h, sv = _layer_fwd(h, p[layer], wts, small, layer, tabs128)
        saved.append(sv)
    loss, dh, dgf = _loss_head(h, small["final_norm"][None], target, "loss_head")
    grads = {}
    sgs = [None, None]
    for layer in (1, 0):
        dh, grads, sgs[layer] = _layer_bwd(dh, saved[layer], p[layer], wts, small, layer, tabs128, grads)
    small_grads = {k: jnp.stack([sgs[0][k], sgs[1][k]]) for k in sgs[0]}
    small_grads["final_norm"] = dgf[0]
    return loss, dh, grads, small_grads


HBM = pl.BlockSpec(memory_space=pltpu.HBM)


def _my_place():
    return lax.axis_index("x"), lax.axis_index("y"), lax.axis_index("c")


def _other_chips(x, y):
    return [(1 - x, y), (x, 1 - y), (1 - x, 1 - y)]


def _window(ref, name, chip):
    k, n = _shard_shape(name)
    if COL_SHARDED[name]:
        return ref.at[:, pl.ds(pl.multiple_of(chip * n, 128), n)]
    return ref.at[pl.ds(pl.multiple_of(chip * k, 128), k), :]


def _chip_index():
    return jnp.reshape(2 * lax.axis_index("x") + lax.axis_index("y"), (1,)).astype(jnp.int32)


def _shard_block(name, tr):
    ks, ns = _shard_shape(name)
    if COL_SHARDED[name]:
        return (tr, ns), lambda i, me: (i, me[0])
    return (tr, ns), lambda i, me: (me[0] * (ks // tr) + i, 0)


def _place_shard(w, name):
    ks, ns = _shard_shape(name)
    tr = min(ks, 256)
    shape, index = _shard_block(name, tr)

    def body(me_ref, w_ref, o_ref):
        o_ref[...] = w_ref[...].astype(o_ref.dtype)

    return pl.pallas_call(
        body, name="place_" + name,
        grid_spec=pltpu.PrefetchScalarGridSpec(
            num_scalar_prefetch=1, grid=(2, ks // tr),
            in_specs=[pl.BlockSpec((None, tr, ns), lambda l, i, me: (l, i, 0))],
            out_specs=pl.BlockSpec((None,) + shape, lambda l, i, me: (l,) + index(i, me))),
        out_shape=jax.ShapeDtypeStruct((2,) + FULL_SHAPE[name], MXU_DTYPE),
        compiler_params=_cparams(("parallel", "parallel")),
    )(_chip_index(), w)


def _gather_weights(full):
    names = list(BIG)

    def body(*refs):
        ins = refs[:len(names)]
        outs = refs[len(names):2 * len(names)]
        send_ici, recv_ici, send_d2d, recv_d2d = refs[2 * len(names):]
        x, y, c = _my_place()
        me = 2 * x + y
        sibling = (x, y, 1 - c)
        chips = _other_chips(x, y)
        ici = []
        for t, name in enumerate(names):
            for j, (cx, cy) in enumerate(chips):
                cp = pltpu.make_async_remote_copy(
                    src_ref=_window(ins[t].at[c], name, me), dst_ref=_window(outs[t].at[c], name, me),
                    send_sem=send_ici.at[3 * t + j], recv_sem=recv_ici.at[3 * t + j],
                    device_id=(cx, cy, c), device_id_type=MESH)
                cp.start()
                ici.append(cp)
        fwd = []
        for t, name in enumerate(names):
            for j, (cx, cy) in enumerate(chips):
                land = _window(outs[t].at[c], name, 2 * cx + cy)
                pltpu.make_async_remote_copy(
                    src_ref=land, dst_ref=land, send_sem=send_ici.at[3 * t + j], recv_sem=recv_ici.at[3 * t + j],
                    device_id=(cx, cy, c), device_id_type=MESH).wait_recv()
                cp = pltpu.make_async_remote_copy(
                    src_ref=land, dst_ref=land, send_sem=send_d2d.at[3 * t + j], recv_sem=recv_d2d.at[3 * t + j],
                    device_id=sibling, device_id_type=MESH)
                cp.start()
                fwd.append(cp)
        for t, name in enumerate(names):
            for j, (cx, cy) in enumerate(chips):
                land = _window(outs[t].at[1 - c], name, 2 * cx + cy)
                pltpu.make_async_remote_copy(
                    src_ref=land, dst_ref=land, send_sem=send_d2d.at[3 * t + j], recv_sem=recv_d2d.at[3 * t + j],
                    device_id=sibling, device_id_type=MESH).wait_recv()
        for cp in ici + fwd:
            cp.wait_send()

    nsem = 3 * len(names)
    outs = pl.pallas_call(
        body, name="gather_weights",
        in_specs=[HBM] * len(names), out_specs=[HBM] * len(names),
        out_shape=[jax.ShapeDtypeStruct(full[n].shape, full[n].dtype) for n in names],
        input_output_aliases={t: t for t in range(len(names))},
        scratch_shapes=[pltpu.SemaphoreType.DMA((nsem,)), pltpu.SemaphoreType.DMA((nsem,)),
                        pltpu.SemaphoreType.DMA((nsem,)), pltpu.SemaphoreType.DMA((nsem,))],
    )(*[full[n] for n in names])
    return dict(zip(names, outs))


def _swap_layers(grads):
    names = list(BIG)

    def body(*refs):
        ins = refs[:len(names)]
        outs = refs[len(names):2 * len(names)]
        send_sems, recv_sems = refs[2 * len(names):]
        x, y, c = _my_place()
        sibling = (x, y, 1 - c)
        cps = []
        for t in range(len(names)):
            cp = pltpu.make_async_remote_copy(
                src_ref=ins[t].at[1 - c], dst_ref=outs[t], send_sem=send_sems.at[t], recv_sem=recv_sems.at[t],
                device_id=sibling, device_id_type=MESH)
            cp.start()
            cps.append(cp)
        for cp in cps:
            cp.wait()

    outs = pl.pallas_call(
        body, name="swap_layers", in_specs=[HBM] * len(names), out_specs=[HBM] * len(names),
        out_shape=[jax.ShapeDtypeStruct(FULL_SHAPE[n], f32) for n in names],
        scratch_shapes=[pltpu.SemaphoreType.DMA((len(names),)), pltpu.SemaphoreType.DMA((len(names),))],
    )(*[grads[n] for n in names])
    return dict(zip(names, outs))


def _chip_sum(grad, other, name):
    k, n = FULL_SHAPE[name]
    tr = min(k, 512)
    c = lax.axis_index("c")

    def body(c_ref, g_ref, o_ref, out_ref):
        out_ref[...] = (g_ref[...] + o_ref[...]).astype(out_ref.dtype)

    return pl.pallas_call(
        body, name="chip_sum_" + name,
        grid_spec=pltpu.PrefetchScalarGridSpec(
            num_scalar_prefetch=1, grid=(k // tr,),
            in_specs=[pl.BlockSpec((None, tr, n), lambda i, c_ref: (c_ref[0], i, 0)),
                      pl.BlockSpec((tr, n), lambda i, c_ref: (i, 0))],
            out_specs=pl.BlockSpec((tr, n), lambda i, c_ref: (i, 0))),
        out_shape=jax.ShapeDtypeStruct((k, n), COMM_DTYPE),
        compiler_params=_cparams(("parallel",)),
    )(jnp.reshape(c, (1,)).astype(jnp.int32), grad, other)


def _scatter_shards(sums):
    names = list(BIG)

    def body(*refs):
        ins = refs[:len(names)]
        outs = refs[len(names):2 * len(names)]
        send_sems, recv_sems = refs[2 * len(names):]
        x, y, c = _my_place()
        me = 2 * x + y
        chips = _other_chips(x, y)
        cps = []
        for t, name in enumerate(names):
            for j, (cx, cy) in enumerate(chips):
                cp = pltpu.make_async_remote_copy(
                    src_ref=_window(ins[t], name, 2 * cx + cy), dst_ref=outs[t].at[me],
                    send_sem=send_sems.at[3 * t + j], recv_sem=recv_sems.at[3 * t + j],
                    device_id=(cx, cy, c), device_id_type=MESH)
                cp.start()
                cps.append(cp)
        for t, name in enumerate(names):
            for j, (cx, cy) in enumerate(chips):
                land = outs[t].at[2 * cx + cy]
                pltpu.make_async_remote_copy(
                    src_ref=land, dst_ref=land, send_sem=send_sems.at[3 * t + j], recv_sem=recv_sems.at[3 * t + j],
                    device_id=(cx, cy, c), device_id_type=MESH).wait_recv()
        for cp in cps:
            cp.wait_send()

    nsem = 3 * len(names)
    outs = pl.pallas_call(
        body, name="scatter_shards", in_specs=[HBM] * len(names), out_specs=[HBM] * len(names),
        out_shape=[jax.ShapeDtypeStruct((N_CHIPS,) + _shard_shape(n), sums[n].dtype) for n in names],
        scratch_shapes=[pltpu.SemaphoreType.DMA((nsem,)), pltpu.SemaphoreType.DMA((nsem,))],
    )(*[sums[n] for n in names])
    return dict(zip(names, outs))


def _sum_slots(slots, own, name):
    ks, ns = _shard_shape(name)
    tr = min(ks, 256)
    shape, index = _shard_block(name, tr)

    def body(me_ref, c_ref, s_ref, own_ref, out_ref):
        me = me_ref[0]
        acc = None
        for s in range(N_CHIPS):
            term = jnp.where(me == s, own_ref[...], s_ref[s]).astype(f32)
            acc = term if acc is None else acc + term
        out_ref[...] = acc

    return pl.pallas_call(
        body, name="sum_slots_" + name,
        grid_spec=pltpu.PrefetchScalarGridSpec(
            num_scalar_prefetch=2, grid=(ks // tr,),
            in_specs=[pl.BlockSpec((N_CHIPS, tr, ns), lambda i, me, c: (0, i, 0)),
                      pl.BlockSpec(shape, lambda i, me, c: index(i, me))],
            out_specs=pl.BlockSpec((None, tr, ns), lambda i, me, c: (c[0], i, 0))),
        out_shape=jax.ShapeDtypeStruct((2, ks, ns), f32),
        compiler_params=_cparams(("parallel",)),
    )(_chip_index(), jnp.reshape(lax.axis_index("c"), (1,)).astype(jnp.int32), slots, own)


def _pair_layers(mine):
    names = list(BIG)

    def body(*refs):
        ins = refs[:len(names)]
        outs = refs[len(names):2 * len(names)]
        send_sems, recv_sems = refs[2 * len(names):]
        x, y, c = _my_place()
        sibling = (x, y, 1 - c)
        cps = []
        for t in range(len(names)):
            cp = pltpu.make_async_remote_copy(
                src_ref=ins[t].at[c], dst_ref=outs[t].at[c], send_sem=send_sems.at[t], recv_sem=recv_sems.at[t],
                device_id=sibling, device_id_type=MESH)
            cp.start()
            cps.append(cp)
        for t in range(len(names)):
            cps[t].wait_send()
            land = outs[t].at[1 - c]
            pltpu.make_async_remote_copy(
                src_ref=land, dst_ref=land, send_sem=send_sems.at[t], recv_sem=recv_sems.at[t],
                device_id=sibling, device_id_type=MESH).wait_recv()

    outs = pl.pallas_call(
        body, name="pair_layers", in_specs=[HBM] * len(names), out_specs=[HBM] * len(names),
        out_shape=[jax.ShapeDtypeStruct((2,) + _shard_shape(n), f32) for n in names],
        input_output_aliases={t: t for t in range(len(names))},
        scratch_shapes=[pltpu.SemaphoreType.DMA((len(names),)), pltpu.SemaphoreType.DMA((len(names),))],
    )(*[mine[n] for n in names])
    return dict(zip(names, outs))


SMALL_ROWS = 320


def _allreduce_small(vec):
    n_dev = 8

    def body(v_ref, out_ref, buf_ref, send_sems, recv_sems):
        x, y, c = _my_place()
        me = 4 * x + 2 * y + c
        buf_ref[me] = v_ref[...]
        cps = []
        for k in range(1, n_dev):
            dx, dy, dc = (k >> 2) & 1, (k >> 1) & 1, k & 1
            peer = (x ^ dx, y ^ dy, c ^ dc)
            cp = pltpu.make_async_remote_copy(
                src_ref=v_ref, dst_ref=buf_ref.at[me], send_sem=send_sems.at[k - 1], recv_sem=recv_sems.at[k - 1],
                device_id=peer, device_id_type=MESH)
            cp.start()
            cps.append(cp)
        for k in range(1, n_dev):
            dx, dy, dc = (k >> 2) & 1, (k >> 1) & 1, k & 1
            src = 4 * (x ^ dx) + 2 * (y ^ dy) + (c ^ dc)
            land = buf_ref.at[src]
            pltpu.make_async_remote_copy(
                src_ref=land, dst_ref=land, send_sem=send_sems.at[k - 1], recv_sem=recv_sems.at[k - 1],
                device_id=(x ^ dx, y ^ dy, c ^ dc), device_id_type=MESH).wait_recv()
        for cp in cps:
            cp.wait_send()
        acc = buf_ref[0]
        for s in range(1, n_dev):
            acc = acc + buf_ref[s]
        out_ref[...] = acc

    return pl.pallas_call(
        body, name="allreduce_small",
        in_specs=[pl.BlockSpec(memory_space=pltpu.VMEM)], out_specs=pl.BlockSpec(memory_space=pltpu.VMEM),
        out_shape=jax.ShapeDtypeStruct((SMALL_ROWS, 128), f32),
        scratch_shapes=[pltpu.VMEM((n_dev, SMALL_ROWS, 128), f32), pltpu.SemaphoreType.DMA((n_dev - 1,)),
                        pltpu.SemaphoreType.DMA((n_dev - 1,))],
    )(vec)


def _adamw(w, g, m, v, name):
    rows, cols = w.shape
    tr = rows
    for cand in (512, 256, 128, 64, 32, 16, 8):
        if rows % cand == 0 and cand * cols * 4 <= 2 * 1024 * 1024:
            tr = cand
            break
    c1 = np.float32(1.0 - ADAM_B1 ** ADAM_STEP)
    c2 = np.float32(1.0 - ADAM_B2 ** ADAM_STEP)

    def body(w_ref, g_ref, m_ref, v_ref, d_ref, mo_ref, vo_ref):
        gv = g_ref[...]
        mn = ADAM_B1 * m_ref[...] + (1.0 - ADAM_B1) * gv
        vn = ADAM_B2 * v_ref[...] + (1.0 - ADAM_B2) * (gv * gv)
        mo_ref[...] = mn
        vo_ref[...] = vn
        d_ref[...] = -ADAM_LR * ((mn / c1) / (jnp.sqrt(vn / c2) + ADAM_EPS) + ADAM_WD * w_ref[...])

    blk = pl.BlockSpec((tr, cols), lambda i: (i, 0))
    return pl.pallas_call(
        body, name="adamw_" + name, grid=(rows // tr,), in_specs=[blk] * 4, out_specs=[blk] * 3,
        out_shape=[jax.ShapeDtypeStruct((rows, cols), f32)] * 3,
        compiler_params=_cparams(("parallel",)),
    )(w, g, m, v)


SMALL = ("norm1", "pool_w", "pool_scale", "norm2", "norm3", "final_norm")
ORDER = ("norm1", "w_in", "pool_w", "pool_scale", "w_out", "norm2", "w_up", "w_down", "norm3", "w_gate", "w_ple",
         "final_norm")


def _pack_small(tree, extra=None):
    parts = [tree[n].reshape(-1) for n in SMALL]
    if extra is not None:
        parts.append(extra.reshape(-1))
    flat = jnp.concatenate(parts)
    return jnp.pad(flat, (0, SMALL_ROWS * 128 - flat.shape[0])).reshape(SMALL_ROWS, 128)


def _unpack_small(packed, like):
    flat = packed.reshape(-1)
    out, off = {}, 0
    for n in SMALL:
        size = int(np.prod(like[n].shape))
        out[n] = flat[off:off + size].reshape(like[n].shape)
        off += size
    return out, flat[off]


def kernel(x, p, positions, norm1, w_in, pool_w, pool_scale, w_out, norm2, w_up, w_down, norm3, w_gate, w_ple, final_norm, loss_target, m_norm1, m_w_in, m_pool_w, m_pool_scale, m_w_out, m_norm2, m_w_up, m_w_down, m_norm3, m_w_gate, m_w_ple, m_final_norm, v_norm1, v_w_in, v_pool_w, v_pool_scale, v_w_out, v_norm2, v_w_up, v_w_down, v_norm3, v_w_gate, v_w_ple, v_final_norm):
    w = dict(norm1=norm1, w_in=w_in, pool_w=pool_w, pool_scale=pool_scale, w_out=w_out, norm2=norm2, w_up=w_up,
             w_down=w_down, norm3=norm3, w_gate=w_gate, w_ple=w_ple, final_norm=final_norm)
    m = dict(norm1=m_norm1, w_in=m_w_in, pool_w=m_pool_w, pool_scale=m_pool_scale, w_out=m_w_out, norm2=m_norm2,
             w_up=m_w_up, w_down=m_w_down, norm3=m_norm3, w_gate=m_w_gate, w_ple=m_w_ple, final_norm=m_final_norm)
    v = dict(norm1=v_norm1, w_in=v_w_in, pool_w=v_pool_w, pool_scale=v_pool_scale, w_out=v_w_out, norm2=v_norm2,
             w_up=v_w_up, w_down=v_w_down, norm3=v_norm3, w_gate=v_w_gate, w_ple=v_w_ple, final_norm=v_final_norm)
    small = {n: w[n] for n in SMALL}

    full = _gather_weights({n: _place_shard(w[n], n) for n in BIG})
    loss8, dx, grads, small_grads = _local_step(x[0], p[:, 0], positions[0], full, small, loss_target[0])

    other = _swap_layers(grads)
    sums = {n: _chip_sum(grads[n], other[n], n) for n in BIG}
    slots = _scatter_shards(sums)
    mine = {n: _sum_slots(slots[n], sums[n], n) for n in BIG}
    gsh = _pair_layers(mine)

    red = _allreduce_small(_pack_small(small_grads, loss8[0, 0]))
    g_small, loss = _unpack_small(red, small)

    g_out, d_out, m_out, v_out = {}, {}, {}, {}
    for n in BIG:
        shp = w[n].shape
        two = lambda a: a.reshape(shp[0] * shp[1], shp[2])
        d2, m2, v2 = _adamw(two(w[n]), two(gsh[n]), two(m[n]), two(v[n]), n)
        g_out[n], d_out[n], m_out[n], v_out[n] = gsh[n], d2.reshape(shp), m2.reshape(shp), v2.reshape(shp)
    d2, m2, v2 = _adamw(_pack_small(small), red, _pack_small({n: m[n] for n in SMALL}),
                        _pack_small({n: v[n] for n in SMALL}), "small")
    for tree, packed in ((d_out, d2), (m_out, m2), (v_out, v2)):
        tree.update(_unpack_small(packed, small)[0])
    g_out.update(g_small)

    return (loss, dx[None], *[g_out[n] for n in ORDER], *[d_out[n] for n in ORDER], *[m_out[n] for n in ORDER],
            *[v_out[n] for n in ORDER])
```

```python
import functools

import jax
import jax.numpy as jnp
import numpy as np
from jax import lax
from jax.experimental import pallas as pl
from jax.experimental.pallas import tpu as pltpu

f32 = jnp.float32
MXU_DTYPE = jnp.bfloat16
COMM_DTYPE = jnp.bfloat16

D_MODEL = 1024
POOL_WIDTH = 256
POOL_GC = 64
ATTN_WIDTH = 768
HEAD_DIM = 64
N_IN = POOL_WIDTH + 3 * ATTN_WIDTH
D_FF = 4096
PLE_DIM = 256
BLK = 128
DILATIONS = (1, 4, 16)
ROT_DIM = 16
ROPE_THETA = 500000.0
EPS = 1e-6
ATTN_SCALE = HEAD_DIM ** -0.5
NEG_BIG = -1e30

ADAM_LR, ADAM_B1, ADAM_B2, ADAM_EPS, ADAM_WD, ADAM_STEP = 0.001, 0.9, 0.999, 1e-08, 0.01, 10

TM = 512
HALO = 16
VMEM_LIMIT = 48 * 1024 * 1024
N_CHIPS = 4
MESH = pl.DeviceIdType.MESH

BIG = ("w_in", "w_out", "w_up", "w_down", "w_gate", "w_ple")
FULL_SHAPE = {"w_in": (D_MODEL, N_IN), "w_out": (D_MODEL, D_MODEL), "w_up": (D_MODEL, D_FF),
              "w_down": (D_FF, D_MODEL), "w_gate": (D_MODEL, D_MODEL), "w_ple": (PLE_DIM, D_MODEL)}
COL_SHARDED = {"w_in": True, "w_out": False, "w_up": True, "w_down": False, "w_gate": False, "w_ple": True}


def _shard_shape(name):
    k, n = FULL_SHAPE[name]
    return (k, n // N_CHIPS) if COL_SHARDED[name] else (k // N_CHIPS, n)


def _cparams(sem=None, vmem=VMEM_LIMIT):
    return pltpu.CompilerParams(dimension_semantics=sem, vmem_limit_bytes=vmem)


def _resident(block_shape, index_map):
    return pl.BlockSpec(block_shape, index_map, pipeline_mode=pl.Buffered(1))


def _mx(x):
    return x.astype(MXU_DTYPE)


def _dot(a, b):
    return jnp.dot(a, b, preferred_element_type=f32)


def _dot_nt(a, b):
    return lax.dot_general(a, b, (((1,), (1,)), ((), ())), preferred_element_type=f32)


def _dot_tn(a, b):
    return lax.dot_general(a, b, (((0,), (0,)), ((), ())), preferred_element_type=f32)


def _sigmoid(x):
    return 1.0 / (1.0 + jnp.exp(-x))


def _rope_apply(y, c, s1, s2, width):
    return y * c + pltpu.roll(y, width - 8, axis=1) * s1 + pltpu.roll(y, 8, axis=1) * s2


def _rope_transpose(dy, c, s1, s2, width):
    return dy * c + pltpu.roll(dy * s1, 8, axis=1) + pltpu.roll(dy * s2, width - 8, axis=1)


def _norm_matmul(h, g, w, layer, tn, name, rope=None):
    s_len, d = h.shape
    n = w.shape[2]

    def body(*refs):
        if rope is None:
            h_ref, g_ref, w_ref, y_ref, hn_ref = refs
        else:
            h_ref, g_ref, w_ref, c_ref, s1_ref, s2_ref, y_ref, hn_ref = refs
            reps = tn // 128
            c = jnp.concatenate([c_ref[...]] * reps, axis=1)
            s1 = jnp.concatenate([s1_ref[...]] * reps, axis=1)
            s2 = jnp.concatenate([s2_ref[...]] * reps, axis=1)
        x = h_ref[...]
        r = lax.rsqrt(jnp.mean(x * x, axis=-1, keepdims=True) + EPS)
        hn = ((x * r) * g_ref[...]).astype(hn_ref.dtype)
        hn_ref[...] = hn
        for j in range(n // tn):
            y = _dot(hn, w_ref[:, j * tn:(j + 1) * tn])
            if rope is not None and POOL_WIDTH <= j * tn < POOL_WIDTH + 2 * ATTN_WIDTH:
                y = _rope_apply(y, c, s1, s2, tn)
            y_ref[:, j * tn:(j + 1) * tn] = y

    in_specs = [pl.BlockSpec((TM, d), lambda i: (i, 0)),
                pl.BlockSpec((1, d), lambda i: (0, 0)),
                _resident((None, d, n), lambda i: (layer, 0, 0))]
    args = [h, g, w]
    if rope is not None:
        assert POOL_WIDTH % tn == 0 and (2 * ATTN_WIDTH) % tn == 0
        in_specs += [pl.BlockSpec((TM, 128), lambda i: (i, 0))] * 3
        args += list(rope)
    return pl.pallas_call(
        body, name=name, grid=(s_len // TM,), in_specs=in_specs,
        out_specs=[pl.BlockSpec((TM, n), lambda i: (i, 0)), pl.BlockSpec((TM, d), lambda i: (i, 0))],
        out_shape=[jax.ShapeDtypeStruct((s_len, n), f32), jax.ShapeDtypeStruct((s_len, d), MXU_DTYPE)],
        compiler_params=_cparams(("parallel",)),
    )(*args)


def _matmul_residual(a, w, layer, res, name, act=False, tk=1024):
    s_len, k_dim = a.shape
    n = w.shape[2]

    def body(a_ref, w_ref, res_ref, o_ref):
        acc = res_ref[...]
        for k in range(k_dim // tk):
            x = a_ref[:, k * tk:(k + 1) * tk]
            if act:
                r = jnp.maximum(x, 0.0)
                x = r * r
            acc = acc + _dot(_mx(x), w_ref[k * tk:(k + 1) * tk, :])
        o_ref[...] = acc

    return pl.pallas_call(
        body, name=name, grid=(s_len // TM,),
        in_specs=[pl.BlockSpec((TM, k_dim), lambda i: (i, 0)),
                  _resident((None, k_dim, n), lambda i: (layer, 0, 0)),
                  pl.BlockSpec((TM, n), lambda i: (i, 0))],
        out_specs=pl.BlockSpec((TM, n), lambda i: (i, 0)),
        out_shape=jax.ShapeDtypeStruct((s_len, n), f32),
        compiler_params=_cparams(("parallel",)),
    )(a, w, res)


def _gate_ple_fwd(h2, g, w_gate, w_ple, layer, p, name):
    s_len, d = h2.shape

    def body(h_ref, g_ref, wg_ref, p_ref, wp_ref, h3_ref, gl_ref, hn_ref):
        x = h_ref[...]
        r = lax.rsqrt(jnp.mean(x * x, axis=-1, keepdims=True) + EPS)
        hn = ((x * r) * g_ref[...]).astype(hn_ref.dtype)
        hn_ref[...] = hn
        gl = _dot(hn, wg_ref[...])
        gl_ref[...] = gl
        e = _dot(_mx(p_ref[...]), wp_ref[...])
        h3_ref[...] = x + _sigmoid(gl) * e

    row = lambda i: (i, 0)
    return pl.pallas_call(
        body, name=name, grid=(s_len // TM,),
        in_specs=[pl.BlockSpec((TM, d), row), pl.BlockSpec((1, d), lambda i: (0, 0)),
                  pl.BlockSpec((None, d, d), lambda i: (layer, 0, 0)), pl.BlockSpec((TM, PLE_DIM), row),
                  pl.BlockSpec((None, PLE_DIM, d), lambda i: (layer, 0, 0))],
        out_specs=[pl.BlockSpec((TM, d), row)] * 3,
        out_shape=[jax.ShapeDtypeStruct((s_len, d), f32), jax.ShapeDtypeStruct((s_len, d), f32),
                   jax.ShapeDtypeStruct((s_len, d), MXU_DTYPE)],
        compiler_params=_cparams(("parallel",)),
    )(h2, g, w_gate, p, w_ple)


def _gate_ple_bwd(dh3, gl, p, w_ple, layer, name):
    s_len, d = dh3.shape

    def body(dh_ref, gl_ref, p_ref, wp_ref, de_ref, dgl_ref):
        dh = dh_ref[...]
        gate = _sigmoid(gl_ref[...])
        e = _dot(_mx(p_ref[...]), wp_ref[...])
        de_ref[...] = (dh * gate).astype(de_ref.dtype)
        dgl_ref[...] = ((dh * e) * (gate * (1.0 - gate))).astype(dgl_ref.dtype)

    row = lambda i: (i, 0)
    return pl.pallas_call(
        body, name=name, grid=(s_len // TM,),
        in_specs=[pl.BlockSpec((TM, d), row), pl.BlockSpec((TM, d), row), pl.BlockSpec((TM, PLE_DIM), row),
                  pl.BlockSpec((None, PLE_DIM, d), lambda i: (layer, 0, 0))],
        out_specs=[pl.BlockSpec((TM, d), row)] * 2,
        out_shape=[jax.ShapeDtypeStruct((s_len, d), MXU_DTYPE)] * 2,
        compiler_params=_cparams(("parallel",)),
    )(dh3, gl, p, w_ple)


def _rmsnorm_bwd(dhn, x, g):
    r = lax.rsqrt(jnp.mean(x * x, axis=-1, keepdims=True) + EPS)
    xh = x * r
    dxh = dhn * g
    dx = r * (dxh - xh * jnp.mean(dxh * xh, axis=-1, keepdims=True))
    return dx, dhn * xh


def _matmul_nt_norm_bwd(dy, w, layer, h_prev, g, dres, name, tk=1024):
    s_len, k_dim = dy.shape
    d = h_prev.shape[1]

    def body(dy_ref, w_ref, h_ref, g_ref, dres_ref, dh_ref, dg_ref):
        i = pl.program_id(0)
        acc = None
        for k in range(k_dim // tk):
            part = _dot_nt(_mx(dy_ref[:, k * tk:(k + 1) * tk]), w_ref[:, k * tk:(k + 1) * tk])
            acc = part if acc is None else acc + part
        dx, dgrow = _rmsnorm_bwd(acc, h_ref[...], g_ref[...])
        dh_ref[...] = dres_ref[...] + dx
        dgsum = jnp.sum(dgrow, axis=0, keepdims=True)

        @pl.when(i == 0)
        def _():
            dg_ref[...] = dgsum

        @pl.when(i > 0)
        def _():
            dg_ref[...] += dgsum

    return pl.pallas_call(
        body, name=name, grid=(s_len // TM,),
        in_specs=[pl.BlockSpec((TM, k_dim), lambda i: (i, 0)),
                  _resident((None, d, k_dim), lambda i: (layer, 0, 0)),
                  pl.BlockSpec((TM, d), lambda i: (i, 0)),
                  pl.BlockSpec((1, d), lambda i: (0, 0)),
                  pl.BlockSpec((TM, d), lambda i: (i, 0))],
        out_specs=[pl.BlockSpec((TM, d), lambda i: (i, 0)), pl.BlockSpec((1, d), lambda i: (0, 0))],
        out_shape=[jax.ShapeDtypeStruct((s_len, d), f32), jax.ShapeDtypeStruct((1, d), f32)],
        compiler_params=_cparams(("arbitrary",)),
    )(dy, w, h_prev, g, dres)


def _down_bwd(dh2, w_down, layer, a, name, tf=1024):
    s_len, d = dh2.shape
    ff = a.shape[1]

    def body(dh_ref, w_ref, a_ref, act_ref, da_ref, dhb_ref):
        j = pl.program_id(1)

        @pl.when(j == 0)
        def _():
            dhb_ref[...] = _mx(dh_ref[...])

        dact = _dot_nt(dhb_ref[...], w_ref[pl.ds(pl.multiple_of(j * tf, tf), tf), :])
        r = jnp.maximum(a_ref[...], 0.0)
        act_ref[...] = (r * r).astype(act_ref.dtype)
        da_ref[...] = (dact * (2.0 * r)).astype(da_ref.dtype)

    return pl.pallas_call(
        body, name=name, grid=(s_len // TM, ff // tf),
        in_specs=[pl.BlockSpec((TM, d), lambda i, j: (i, 0)),
                  _resident((None, ff, d), lambda i, j: (layer, 0, 0)),
                  pl.BlockSpec((TM, tf), lambda i, j: (i, j))],
        out_specs=[pl.BlockSpec((TM, tf), lambda i, j: (i, j))] * 2,
        out_shape=[jax.ShapeDtypeStruct((s_len, ff), MXU_DTYPE)] * 2,
        scratch_shapes=[pltpu.VMEM((TM, d), MXU_DTYPE)],
        compiler_params=_cparams(("parallel", "arbitrary")),
    )(dh2, w_down, a)


def _matmul_nt(dy, w, layer, name):
    s_len, n = dy.shape
    k_dim = w.shape[1]

    def body(dy_ref, w_ref, o_ref):
        o_ref[...] = _dot_nt(_mx(dy_ref[...]), w_ref[...])

    return pl.pallas_call(
        body, name=name, grid=(s_len // TM,),
        in_specs=[pl.BlockSpec((TM, n), lambda i: (i, 0)), pl.BlockSpec((None, k_dim, n), lambda i: (layer, 0, 0))],
        out_specs=pl.BlockSpec((TM, k_dim), lambda i: (i, 0)),
        out_shape=jax.ShapeDtypeStruct((s_len, k_dim), f32),
        compiler_params=_cparams(("parallel",)),
    )(dy, w)


def _weight_grad(a, b, layer, prev, name):
    s_len, k_dim = a.shape
    n = b.shape[1]
    tka = min(k_dim, 2048)
    tnb = n if n <= 1024 else (2048 if n % 2048 == 0 else 640)
    ns = s_len // TM

    def body(*refs):
        a_ref, b_ref = refs[0], refs[1]
        o_ref = refs[-1]
        s = pl.program_id(2)
        part = _dot_tn(_mx(a_ref[...]), _mx(b_ref[...]))

        @pl.when(s == 0)
        def _():
            o_ref[...] = part

        @pl.when(s > 0)
        def _():
            o_ref[...] += part

    in_specs = [pl.BlockSpec((TM, tka), lambda i, j, s: (s, i)), pl.BlockSpec((TM, tnb), lambda i, j, s: (s, j))]
    args = [a, b]
    aliases = {}
    if prev is not None:
        in_specs.append(pl.BlockSpec(memory_space=pl.ANY))
        args.append(prev)
        aliases = {2: 0}
    return pl.pallas_call(
        body, name=name, grid=(k_dim // tka, n // tnb, ns), in_specs=in_specs,
        out_specs=pl.BlockSpec((None, tka, tnb), lambda i, j, s: (layer, i, j)),
        out_shape=jax.ShapeDtypeStruct((2, k_dim, n), f32),
        input_output_aliases=aliases,
        compiler_params=_cparams(("parallel", "parallel", "arbitrary")),
    )(*args)


def _group_select(lane, x2, x4, x8, x16):
    grp = lane // POOL_GC
    return jnp.where(grp == 0, x2, jnp.where(grp == 1, x4, jnp.where(grp == 2, x8, x16)))


def _pool_window(lane):
    grp = lane // POOL_GC
    return jnp.where(grp == 0, 2, jnp.where(grp == 1, 4, jnp.where(grp == 2, 8, 16)))


def _pool_y(u, halo, i):
    xs = jnp.concatenate([jnp.where(i > 0, halo, 0.0), u], axis=0)
    s2 = xs + pltpu.roll(xs, 1, axis=0)
    s4 = s2 + pltpu.roll(s2, 2, axis=0)
    s8 = s4 + pltpu.roll(s4, 4, axis=0)
    s16 = s8 + pltpu.roll(s8, 8, axis=0)
    lane = lax.broadcasted_iota(jnp.int32, xs.shape, 1)
    sel = _group_select(lane, s2, s4, s8, s16)[HALO:, :]
    t = i * TM + lax.broadcasted_iota(jnp.int32, u.shape, 0)
    cnt = jnp.minimum(_pool_window(lax.broadcasted_iota(jnp.int32, u.shape, 1)), t + 1).astype(f32)
    return sel / cnt - u


def _group_weights(l0, l1, l2):
    mx = jnp.maximum(jnp.maximum(l0, l1), l2)
    e0, e1, e2 = jnp.exp(l0 - mx), jnp.exp(l1 - mx), jnp.exp(l2 - mx)
    den = e0 + e1 + e2
    return e0 / den, e1 / den, e2 / den


def _mixer_merge(z, wbd, scale, outs, lses, name):
    s_len = z.shape[0]

    def body(u_ref, halo_ref, wbd_ref, sc_ref, o0, o1, o2, l0, l1, l2, m_ref):
        i = pl.program_id(0)
        y = _pool_y(u_ref[...], halo_ref[...], i)
        pool = _dot(_mx(y), wbd_ref[...]) * sc_ref[...]
        w0, w1, w2 = _group_weights(l0[...], l1[...], l2[...])
        m_ref[...] = jnp.concatenate([pool, o0[...] * w0, o1[...] * w1, o2[...] * w2], axis=1).astype(m_ref.dtype)

    row = lambda i: (i, 0)
    blk = pl.BlockSpec((TM, 256), row)
    grp = [pl.BlockSpec((TM, 256), lambda i, g=g: (i, g)) for g in range(3)]
    return pl.pallas_call(
        body, name=name, grid=(s_len // TM,),
        in_specs=[blk, pl.BlockSpec((HALO, 256), lambda i: (jnp.maximum(i * (TM // HALO) - 1, 0), 0)),
                  pl.BlockSpec((256, 256), lambda i: (0, 0)), pl.BlockSpec((1, 256), lambda i: (0, 0))] + grp + grp,
        out_specs=pl.BlockSpec((TM, D_MODEL), row),
        out_shape=jax.ShapeDtypeStruct((s_len, D_MODEL), MXU_DTYPE),
        compiler_params=_cparams(("parallel",)),
    )(z, z, wbd, scale, outs, outs, outs, lses, lses, lses)


def _head_sums(x):
    r = lax.broadcasted_iota(jnp.int32, (256, 256), 0) // HEAD_DIM
    c = lax.broadcasted_iota(jnp.int32, (256, 256), 1) // HEAD_DIM
    ones = jnp.where(r == c, 1.0, 0.0).astype(jnp.bfloat16)
    hi = x.astype(jnp.bfloat16)
    lo = (x - hi.astype(f32)).astype(jnp.bfloat16)
    return _dot(hi, ones) + _dot(lo, ones)


def _combine_bwd(dm, outs, lses, name):
    s_len = dm.shape[0]

    def body(d0, d1, d2, o0, o1, o2, l0, l1, l2, do_ref, dl_ref):
        w = _group_weights(l0[...], l1[...], l2[...])
        da = (d0[...], d1[...], d2[...])
        o = (o0[...], o1[...], o2[...])
        dw = [_head_sums(da[g] * o[g]) for g in range(3)]
        t = w[0] * dw[0] + w[1] * dw[1] + w[2] * dw[2]
        do_ref[...] = jnp.concatenate([da[g] * w[g] for g in range(3)], axis=1)
        dl_ref[...] = jnp.concatenate([w[g] * t for g in range(3)], axis=1)

    grp = [pl.BlockSpec((TM, 256), lambda i, g=g: (i, g)) for g in range(3)]
    return pl.pallas_call(
        body, name=name, grid=(s_len // TM,),
        in_specs=[pl.BlockSpec((TM, 256), lambda i: (i, 1)), pl.BlockSpec((TM, 256), lambda i: (i, 2)),
                  pl.BlockSpec((TM, 256), lambda i: (i, 3))] + grp + grp,
        out_specs=[pl.BlockSpec((TM, ATTN_WIDTH), lambda i: (i, 0))] * 2,
        out_shape=[jax.ShapeDtypeStruct((s_len, ATTN_WIDTH), f32)] * 2,
        compiler_params=_cparams(("parallel",)),
    )(dm, dm, dm, outs, outs, outs, lses, lses, lses)


def _pool_bwd(z, dm, wbd, scale, name):
    s_len = z.shape[0]
    n_halo = s_len // HALO

    def body(u_ref, uh_ref, d_ref, dh_ref, wbd_ref, sc_ref, du_ref, dw_ref, dsc_ref):
        i = pl.program_id(0)
        last = pl.num_programs(0) - 1
        y = _pool_y(u_ref[...], uh_ref[...], i)
        yb = _mx(y)
        dpo = d_ref[...]
        sc = sc_ref[...]
        dsc = jnp.sum(dpo * _dot(yb, wbd_ref[...]), axis=0, keepdims=True)
        dwp = _dot_tn(yb, _mx(dpo * sc))

        @pl.when(i == 0)
        def _():
            dsc_ref[...] = dsc
            dw_ref[...] = dwp

        @pl.when(i > 0)
        def _():
            dsc_ref[...] += dsc
            dw_ref[...] += dwp

        ext = jnp.concatenate([dpo, jnp.where(i < last, dh_ref[...], 0.0)], axis=0)
        dy = _dot_nt(_mx(ext * sc), wbd_ref[...])
        t = i * TM + lax.broadcasted_iota(jnp.int32, ext.shape, 0)
        lane = lax.broadcasted_iota(jnp.int32, ext.shape, 1)
        e = dy / jnp.minimum(_pool_window(lane), t + 1).astype(f32)
        rows = ext.shape[0]
        f2 = e + pltpu.roll(e, rows - 1, axis=0)
        f4 = f2 + pltpu.roll(f2, rows - 2, axis=0)
        f8 = f4 + pltpu.roll(f4, rows - 4, axis=0)
        f16 = f8 + pltpu.roll(f8, rows - 8, axis=0)
        du_ref[...] = (_group_select(lane, f2, f4, f8, f16) - dy)[:TM, :]

    row = lambda i: (i, 0)
    blk = pl.BlockSpec((TM, 256), row)
    return pl.pallas_call(
        body, name=name, grid=(s_len // TM,),
        in_specs=[blk, pl.BlockSpec((HALO, 256), lambda i: (jnp.maximum(i * (TM // HALO) - 1, 0), 0)),
                  blk, pl.BlockSpec((HALO, 256), lambda i: (jnp.minimum((i + 1) * (TM // HALO), n_halo - 1), 0)),
                  pl.BlockSpec((256, 256), lambda i: (0, 0)), pl.BlockSpec((1, 256), lambda i: (0, 0))],
        out_specs=[blk, pl.BlockSpec((256, 256), lambda i: (0, 0)), pl.BlockSpec((1, 256), lambda i: (0, 0))],
        out_shape=[jax.ShapeDtypeStruct((s_len, N_IN), f32), jax.ShapeDtypeStruct((256, 256), f32),
                   jax.ShapeDtypeStruct((1, 256), f32)],
        compiler_params=_cparams(("arbitrary",)),
    )(z, z, dm, dm, wbd, scale)


def _to_strided(x, dil):
    if dil == 1:
        return x
    s_len, c = x.shape
    return x.reshape(s_len // (BLK * dil), BLK, dil, c).transpose(0, 2, 1, 3).reshape(s_len, c)


def _from_strided(x, dil):
    if dil == 1:
        return x
    s_len, c = x.shape
    return x.reshape(s_len // (BLK * dil), dil, BLK, c).transpose(0, 2, 1, 3).reshape(s_len, c)


def _tri_masks():
    qi = lax.broadcasted_iota(jnp.int32, (BLK, BLK), 0)
    ki = lax.broadcasted_iota(jnp.int32, (BLK, BLK), 1)
    return qi >= ki, ki >= qi


ATTN_SUPER_PER_STEP = (8, 2, 1)
Q_COL, K_COL, V_COL = POOL_WIDTH // 128, (POOL_WIDTH + ATTN_WIDTH) // 128, (POOL_WIDTH + 2 * ATTN_WIDTH) // 128


def _rows(ref, start, dil):
    if dil == 1:
        return ref[pl.ds(start, BLK), :]
    return ref[pl.ds(start, BLK, stride=dil), :]


def _set_rows(ref, start, dil, val):
    if dil == 1:
        ref[pl.ds(start, BLK), :] = val
    else:
        ref[pl.ds(start, BLK, stride=dil), :] = val


def _attn_fwd(z, g, prev, name):
    s_len = z.shape[0]
    dil, m = DILATIONS[g], ATTN_SUPER_PER_STEP[g]
    sbr = BLK * dil
    rows = sbr * m

    def body(*refs):
        q_ref, kc_ref, kp_ref, vc_ref, vp_ref = refs[:5]
        o_ref, l_ref = refs[-2:]
        st = pl.program_id(0)
        low, up = _tri_masks()
        for sb in range(m):
            valid = jnp.concatenate([up & (st > 0) if sb == 0 else up, low], axis=1)
            for r in range(dil):
                base = sb * sbr + r
                q = _rows(q_ref, base, dil)
                kc, vc = _rows(kc_ref, base, dil), _rows(vc_ref, base, dil)
                if sb == 0:
                    kp, vp = _rows(kp_ref, r, dil), _rows(vp_ref, r, dil)
                else:
                    kp, vp = _rows(kc_ref, base - sbr, dil), _rows(vc_ref, base - sbr, dil)
                outs, lses = [], []
                for hh in range(2):
                    sl = slice(hh * HEAD_DIM, (hh + 1) * HEAD_DIM)
                    k2 = jnp.concatenate([_mx(kp[:, sl]), _mx(kc[:, sl])], axis=0)
                    v2 = jnp.concatenate([_mx(vp[:, sl]), _mx(vc[:, sl])], axis=0)
                    s = jnp.where(valid, _dot_nt(_mx(q[:, sl]), k2) * ATTN_SCALE, NEG_BIG)
                    mx = jnp.max(s, axis=-1, keepdims=True)
                    e = jnp.exp(s - mx)
                    l = jnp.sum(e, axis=-1, keepdims=True)
                    outs.append(_dot(_mx(e / l), v2))
                    lses.append(jnp.broadcast_to(mx + jnp.log(l), (BLK, HEAD_DIM)))
                _set_rows(o_ref, base, dil, jnp.concatenate(outs, axis=1))
                _set_rows(l_ref, base, dil, jnp.concatenate(lses, axis=1))

    def cur(col):
        return pl.BlockSpec((rows, 128), lambda st, hp: (st, col + 2 * g + hp))

    def before(col):
        return pl.BlockSpec((sbr, 128), lambda st, hp: (jnp.maximum(st * m - 1, 0), col + 2 * g + hp))

    in_specs = [cur(Q_COL), cur(K_COL), before(K_COL), cur(V_COL), before(V_COL)]
    args = [z, z, z, z, z]
    aliases = {}
    if prev is not None:
        in_specs += [pl.BlockSpec(memory_space=pl.ANY)] * 2
        args += list(prev)
        aliases = {5: 0, 6: 1}
    return pl.pallas_call(
        body, name=name, grid=(s_len // rows, 2), in_specs=in_specs, out_specs=[cur(0), cur(0)],
        out_shape=[jax.ShapeDtypeStruct((s_len, ATTN_WIDTH), f32)] * 2, input_output_aliases=aliases,
        compiler_params=_cparams(("parallel", "parallel")),
    )(*args)


def _attn_bwd(z, do, lse, dlt, tabs, dz, g, name):
    s_len = z.shape[0]
    dil, m = DILATIONS[g], ATTN_SUPER_PER_STEP[g]
    sbr = BLK * dil
    rows = sbr * m
    nsteps = s_len // rows

    def body(q_ref, qn_ref, kc_ref, kp_ref, vc_ref, vp_ref, do_ref, don_ref, l_ref, ln_ref, d_ref, dn_ref,
             c_ref, s1_ref, s2_ref, dz_in, dz_ref, dq_buf, dk_buf, dv_buf, sems):
        del dz_in
        st, hp = pl.program_id(0), pl.program_id(1)
        low, up = _tri_masks()
        for sb in range(m):
            up_prev = up & (st > 0) if sb == 0 else up
            up_next = up & (st < nsteps - 1) if sb == m - 1 else up
            for r in range(dil):
                base = sb * sbr + r
                q, k, v = _rows(q_ref, base, dil), _rows(kc_ref, base, dil), _rows(vc_ref, base, dil)
                do_c, l_c, d_c = _rows(do_ref, base, dil), _rows(l_ref, base, dil), _rows(d_ref, base, dil)
                if sb == 0:
                    kp, vp = _rows(kp_ref, r, dil), _rows(vp_ref, r, dil)
                else:
                    kp, vp = _rows(kc_ref, base - sbr, dil), _rows(vc_ref, base - sbr, dil)
                if sb == m - 1:
                    qn, do_n = _rows(qn_ref, r, dil), _rows(don_ref, r, dil)
                    l_n, d_n = _rows(ln_ref, r, dil), _rows(dn_ref, r, dil)
                else:
                    qn, do_n = _rows(q_ref, base + sbr, dil), _rows(do_ref, base + sbr, dil)
                    l_n, d_n = _rows(l_ref, base + sbr, dil), _rows(d_ref, base + sbr, dil)
                dqs, dks, dvs = [], [], []
                for hh in range(2):
                    sl = slice(hh * HEAD_DIM, (hh + 1) * HEAD_DIM)
                    one = slice(hh * HEAD_DIM, hh * HEAD_DIM + 1)
                    qc, qx = _mx(q[:, sl]), _mx(qn[:, sl])
                    kc, kb = _mx(k[:, sl]), _mx(kp[:, sl])
                    vc, vb = _mx(v[:, sl]), _mx(vp[:, sl])
                    doc, dox = _mx(do_c[:, sl]), _mx(do_n[:, sl])
                    lc, lx, dc, dx = l_c[:, one], l_n[:, one], d_c[:, one], d_n[:, one]
                    p_a = jnp.where(low, jnp.exp(_dot_nt(qc, kc) * ATTN_SCALE - lc), 0.0)
                    ds_a = _mx(p_a * (_dot_nt(doc, vc) - dc) * ATTN_SCALE)
                    p_b = jnp.where(up_prev, jnp.exp(_dot_nt(qc, kb) * ATTN_SCALE - lc), 0.0)
                    ds_b = _mx(p_b * (_dot_nt(doc, vb) - dc) * ATTN_SCALE)
                    p_c = jnp.where(up_next, jnp.exp(_dot_nt(qx, kc) * ATTN_SCALE - lx), 0.0)
                    ds_c = _mx(p_c * (_dot_nt(dox, vc) - dx) * ATTN_SCALE)
                    dqs.append(_dot(ds_a, kc) + _dot(ds_b, kb))
                    dks.append(_dot_tn(ds_a, qc) + _dot_tn(ds_c, qx))
                    dvs.append(_dot_tn(_mx(p_a), doc) + _dot_tn(_mx(p_c), dox))
                c, s1, s2 = _rows(c_ref, base, dil), _rows(s1_ref, base, dil), _rows(s2_ref, base, dil)
                _set_rows(dq_buf, base, dil, _rope_transpose(jnp.concatenate(dqs, axis=1), c, s1, s2, 128))
                _set_rows(dk_buf, base, dil, _rope_transpose(jnp.concatenate(dks, axis=1), c, s1, s2, 128))
                _set_rows(dv_buf, base, dil, jnp.concatenate(dvs, axis=1))
        copies = []
        for t, (buf, col) in enumerate(((dq_buf, Q_COL), (dk_buf, K_COL), (dv_buf, V_COL))):
            lane0 = pl.multiple_of((col + 2 * g + hp) * 128, 128)
            dst = dz_ref.at[pl.ds(pl.multiple_of(st * rows, rows), rows), pl.ds(lane0, 128)]
            cp = pltpu.make_async_copy(buf, dst, sems.at[t])
            cp.start()
            copies.append(cp)
        for cp in copies:
            cp.wait()

    def cur(col):
        return pl.BlockSpec((rows, 128), lambda st, hp: (st, col + 2 * g + hp))

    def before(col):
        return pl.BlockSpec((sbr, 128), lambda st, hp: (jnp.maximum(st * m - 1, 0), col + 2 * g + hp))

    def after(col):
        return pl.BlockSpec((sbr, 128), lambda st, hp: (jnp.minimum((st + 1) * m, s_len // sbr - 1), col + 2 * g + hp))

    tab = pl.BlockSpec((rows, 128), lambda st, hp: (st, 0))
    return pl.pallas_call(
        body, name=name, grid=(nsteps, 2),
        in_specs=[cur(Q_COL), after(Q_COL), cur(K_COL), before(K_COL), cur(V_COL), before(V_COL),
                  cur(0), after(0), cur(0), after(0), cur(0), after(0), tab, tab, tab,
                  pl.BlockSpec(memory_space=pl.ANY)],
        out_specs=pl.BlockSpec(memory_space=pl.ANY),
        out_shape=jax.ShapeDtypeStruct(dz.shape, dz.dtype), input_output_aliases={15: 0},
        scratch_shapes=[pltpu.VMEM((rows, 128), f32)] * 3 + [pltpu.SemaphoreType.DMA((3,))],
        compiler_params=_cparams(("arbitrary", "arbitrary")),
    )(z, z, z, z, z, z, do, do, lse, lse, dlt, dlt, *tabs, dz)


def _attn_fwd_old(q, k, v, dil, name):
    s_len = q.shape[0]
    nblk = s_len // BLK

    def body(q_ref, kc_ref, kp_ref, vc_ref, vp_ref, o_ref, l_ref):
        b = pl.program_id(0)
        has_prev = b >= dil
        low, up = _tri_masks()
        valid = jnp.concatenate([up & has_prev, low], axis=1)
        outs, lses = [], []
        for hh in range(2):
            sl = slice(hh * HEAD_DIM, (hh + 1) * HEAD_DIM)
            qh = _mx(q_ref[:, sl])
            k2 = jnp.concatenate([_mx(kp_ref[:, sl]), _mx(kc_ref[:, sl])], axis=0)
            v2 = jnp.concatenate([_mx(vp_ref[:, sl]), _mx(vc_ref[:, sl])], axis=0)
            s = jnp.where(valid, _dot_nt(qh, k2) * ATTN_SCALE, NEG_BIG)
            m = jnp.max(s, axis=-1, keepdims=True)
            e = jnp.exp(s - m)
            l = jnp.sum(e, axis=-1, keepdims=True)
            outs.append(_dot(_mx(e / l), v2))
            lses.append(jnp.broadcast_to(m + jnp.log(l), (BLK, HEAD_DIM)))
        o_ref[...] = jnp.concatenate(outs, axis=1)
        l_ref[...] = jnp.concatenate(lses, axis=1)

    cur = pl.BlockSpec((BLK, 128), lambda b, hp: (b, hp))
    prev = pl.BlockSpec((BLK, 128), lambda b, hp: (jnp.maximum(b - dil, 0), hp))
    return pl.pallas_call(
        body, name=name, grid=(nblk, 2), in_specs=[cur, cur, prev, cur, prev], out_specs=[cur, cur],
        out_shape=[jax.ShapeDtypeStruct((s_len, 256), f32)] * 2,
        compiler_params=_cparams(("parallel", "parallel")),
    )(q, k, k, v, v)


def _attn_bwd_old(q, k, v, do, lse, dlt, tabs, dil, name):
    s_len = q.shape[0]
    nblk = s_len // BLK

    def body(q_ref, qn_ref, kc_ref, kp_ref, vc_ref, vp_ref, do_ref, don_ref, l_ref, ln_ref, d_ref, dn_ref,
             c_ref, s1_ref, s2_ref, dq_ref, dk_ref, dv_ref):
        b = pl.program_id(0)
        has_prev = b >= dil
        has_next = b + dil < nblk
        low, up = _tri_masks()
        dqs, dks, dvs = [], [], []
        for hh in range(2):
            sl = slice(hh * HEAD_DIM, (hh + 1) * HEAD_DIM)
            one = slice(hh * HEAD_DIM, hh * HEAD_DIM + 1)
            qc, qn = _mx(q_ref[:, sl]), _mx(qn_ref[:, sl])
            kc, kp = _mx(kc_ref[:, sl]), _mx(kp_ref[:, sl])
            vc, vp = _mx(vc_ref[:, sl]), _mx(vp_ref[:, sl])
            doc, don = _mx(do_ref[:, sl]), _mx(don_ref[:, sl])
            lc, ln = l_ref[:, one], ln_ref[:, one]
            dc, dn = d_ref[:, one], dn_ref[:, one]
            p_a = jnp.where(low, jnp.exp(_dot_nt(qc, kc) * ATTN_SCALE - lc), 0.0)
            ds_a = _mx(p_a * (_dot_nt(doc, vc) - dc) * ATTN_SCALE)
            p_b = jnp.where(up & has_prev, jnp.exp(_dot_nt(qc, kp) * ATTN_SCALE - lc), 0.0)
            ds_b = _mx(p_b * (_dot_nt(doc, vp) - dc) * ATTN_SCALE)
            p_c = jnp.where(up & has_next, jnp.exp(_dot_nt(qn, kc) * ATTN_SCALE - ln), 0.0)
            ds_c = _mx(p_c * (_dot_nt(don, vc) - dn) * ATTN_SCALE)
            dqs.append(_dot(ds_a, kc) + _dot(ds_b, kp))
            dks.append(_dot_tn(ds_a, qc) + _dot_tn(ds_c, qn))
            dvs.append(_dot_tn(_mx(p_a), doc) + _dot_tn(_mx(p_c), don))
        c, s1, s2 = c_ref[...], s1_ref[...], s2_ref[...]
        dq_ref[...] = _rope_transpose(jnp.concatenate(dqs, axis=1), c, s1, s2, 128)
        dk_ref[...] = _rope_transpose(jnp.concatenate(dks, axis=1), c, s1, s2, 128)
        dv_ref[...] = jnp.concatenate(dvs, axis=1)

    cur = pl.BlockSpec((BLK, 128), lambda b, hp: (b, hp))
    prev = pl.BlockSpec((BLK, 128), lambda b, hp: (jnp.maximum(b - dil, 0), hp))
    nxt = pl.BlockSpec((BLK, 128), lambda b, hp: (jnp.minimum(b + dil, nblk - 1), hp))
    tab = pl.BlockSpec((BLK, 128), lambda b, hp: (b, 0))
    return pl.pallas_call(
        body, name=name, grid=(nblk, 2),
        in_specs=[cur, nxt, cur, prev, cur, prev, cur, nxt, cur, nxt, cur, nxt, tab, tab, tab],
        out_specs=[cur, cur, cur], out_shape=[jax.ShapeDtypeStruct((s_len, 256), f32)] * 3,
        compiler_params=_cparams(("parallel", "parallel")),
    )(q, q, k, k, v, v, do, do, lse, lse, dlt, dlt, *tabs)


def _loss_head(h, g, target, name):
    s_len, d = h.shape

    def body(h_ref, g_ref, t_ref, loss_ref, dh_ref, dg_ref):
        i = pl.program_id(0)
        x = h_ref[...]
        gv = g_ref[...]
        r = lax.rsqrt(jnp.mean(x * x, axis=-1, keepdims=True) + EPS)
        xh = x * r
        diff = xh * gv - t_ref[...]
        part = 0.5 * jnp.sum(jnp.mean(diff * diff, axis=-1, keepdims=True), axis=0, keepdims=True)
        dy = diff * (1.0 / d)
        dxh = dy * gv
        dh_ref[...] = r * (dxh - xh * jnp.mean(dxh * xh, axis=-1, keepdims=True))
        dgsum = jnp.sum(dy * xh, axis=0, keepdims=True)
        lossb = jnp.broadcast_to(part, (8, 128))

        @pl.when(i == 0)
        def _():
            loss_ref[...] = lossb
            dg_ref[...] = dgsum

        @pl.when(i > 0)
        def _():
            loss_ref[...] += lossb
            dg_ref[...] += dgsum

    row = lambda i: (i, 0)
    return pl.pallas_call(
        body, name=name, grid=(s_len // TM,),
        in_specs=[pl.BlockSpec((TM, d), row), pl.BlockSpec((1, d), lambda i: (0, 0)), pl.BlockSpec((TM, d), row)],
        out_specs=[pl.BlockSpec((8, 128), lambda i: (0, 0)), pl.BlockSpec((TM, d), row),
                   pl.BlockSpec((1, d), lambda i: (0, 0))],
        out_shape=[jax.ShapeDtypeStruct((8, 128), f32), jax.ShapeDtypeStruct((s_len, d), f32),
                   jax.ShapeDtypeStruct((1, d), f32)],
        compiler_params=_cparams(("arbitrary",)),
    )(h, g, target)


def _rope_tables(positions):
    inv_freq = ROPE_THETA ** (-jnp.arange(0, ROT_DIM, 2, dtype=f32) / ROT_DIM)
    ang = positions.astype(f32)[:, None] * inv_freq
    cos, sin = jnp.cos(ang), jnp.sin(ang)
    s_len = positions.shape[0]
    zero8, rest = jnp.zeros((s_len, 8), f32), jnp.zeros((s_len, HEAD_DIM - ROT_DIM), f32)
    c = jnp.concatenate([cos, cos, jnp.ones((s_len, HEAD_DIM - ROT_DIM), f32)], axis=1)
    s1 = jnp.concatenate([-sin, zero8, rest], axis=1)
    s2 = jnp.concatenate([zero8, sin, rest], axis=1)
    return c, s1, s2


def _block_diag(pool_w):
    out = jnp.zeros((POOL_WIDTH, POOL_WIDTH), pool_w.dtype)
    for g in range(4):
        out = lax.dynamic_update_slice(out, pool_w[g], (g * POOL_GC, g * POOL_GC))
    return out


def _layer_fwd(h, p_l, wts, small, layer, tabs):
    nm = f"l{layer}_"
    z, hn1 = _norm_matmul(h, small["norm1"][layer][None], wts["w_in"], layer, 256, nm + "in_proj", rope=tabs)
    ol = None
    for g in range(3):
        ol = _attn_fwd(z, g, ol, nm + f"attn_fwd{g}")
    outs, lses = ol
    wbd = _mx(_block_diag(small["pool_w"][layer]))
    scale = small["pool_scale"][layer][None]
    m = _mixer_merge(z, wbd, scale, outs, lses, nm + "mixer_merge")
    h1 = _matmul_residual(m, wts["w_out"], layer, h, nm + "out_proj")
    a, hn2 = _norm_matmul(h1, small["norm2"][layer][None], wts["w_up"], layer, 1024, nm + "up_proj")
    h2 = _matmul_residual(a, wts["w_down"], layer, h1, nm + "down_proj", act=True)
    h3, gl, hn3 = _gate_ple_fwd(h2, small["norm3"][layer][None], wts["w_gate"], wts["w_ple"], layer, p_l, nm + "gate_ple")
    saved = dict(h=h, z=z, hn1=hn1, outs=outs, lses=lses, wbd=wbd, scale=scale, m=m, h1=h1, a=a, hn2=hn2, h2=h2,
                 gl=gl, hn3=hn3)
    return h3, saved


def _layer_bwd(dh3, sv, p_l, wts, small, layer, tabs128, grads):
    nm = f"l{layer}_"
    gb = {}
    de, dgl = _gate_ple_bwd(dh3, sv["gl"], p_l, wts["w_ple"], layer, nm + "gate_ple_bwd")
    gb["w_gate"] = _weight_grad(sv["hn3"], dgl, layer, grads.get("w_gate"), nm + "dw_gate")
    gb["w_ple"] = _weight_grad(p_l, de, layer, grads.get("w_ple"), nm + "dw_ple")
    dh2, dg3 = _matmul_nt_norm_bwd(dgl, wts["w_gate"], layer, sv["h2"], small["norm3"][layer][None], dh3, nm + "gate_bwd")
    act, da = _down_bwd(dh2, wts["w_down"], layer, sv["a"], nm + "down_bwd")
    gb["w_down"] = _weight_grad(act, dh2, layer, grads.get("w_down"), nm + "dw_down")
    gb["w_up"] = _weight_grad(sv["hn2"], da, layer, grads.get("w_up"), nm + "dw_up")
    dh1, dg2 = _matmul_nt_norm_bwd(da, wts["w_up"], layer, sv["h1"], small["norm2"][layer][None], dh2, nm + "up_bwd")
    dm = _matmul_nt(dh1, wts["w_out"], layer, nm + "out_bwd")
    gb["w_out"] = _weight_grad(sv["m"], dh1, layer, grads.get("w_out"), nm + "dw_out")
    do, dlt = _combine_bwd(dm, sv["outs"], sv["lses"], nm + "combine_bwd")
    dz, dwbd, dscale = _pool_bwd(sv["z"], dm, sv["wbd"], sv["scale"], nm + "pool_bwd")
    for g in range(3):
        dz = _attn_bwd(sv["z"], do, sv["lses"], dlt, tabs128, dz, g, nm + f"attn_bwd{g}")
    gb["w_in"] = _weight_grad(sv["hn1"], dz, layer, grads.get("w_in"), nm + "dw_in")
    dh0, dg1 = _matmul_nt_norm_bwd(dz, wts["w_in"], layer, sv["h"], small["norm1"][layer][None], dh1, nm + "in_bwd",
                                   tk=512)
    dpool_w = jnp.stack([dwbd[g * POOL_GC:(g + 1) * POOL_GC, g * POOL_GC:(g + 1) * POOL_GC] for g in range(4)])
    sg = dict(norm1=dg1[0], norm2=dg2[0], norm3=dg3[0], pool_w=dpool_w, pool_scale=dscale[0])
    return dh0, gb, sg


def _local_step(x, p, positions, wts, small, target):
    tabs128 = tuple(jnp.tile(t, (1, 2)) for t in _rope_tables(positions))
    h = x
    saved = []
    for layer in range(2):
        h, sv = _layer_fwd(h, p[layer], wts, small, layer, tabs128)
        saved.append(sv)
    loss, dh, dgf = _loss_head(h, small["final_norm"][None], target, "loss_head")
    grads = {}
    sgs = [None, None]
    for layer in (1, 0):
        dh, grads, sgs[layer] = _layer_bwd(dh, saved[layer], p[layer], wts, small, layer, tabs128, grads)
    small_grads = {k: jnp.stack([sgs[0][k], sgs[1][k]]) for k in sgs[0]}
    small_grads["final_norm"] = dgf[0]
    return loss, dh, grads, small_grads


HBM = pl.BlockSpec(memory_space=pltpu.HBM)


def _my_place():
    return lax.axis_index("x"), lax.axis_index("y"), lax.axis_index("c")


def _other_chips(x, y):
    return [(1 - x, y), (x, 1 - y), (1 - x, 1 - y)]


def _window(ref, name, chip):
    k, n = _shard_shape(name)
    if COL_SHARDED[name]:
        return ref.at[:, pl.ds(pl.multiple_of(chip * n, 128), n)]
    return ref.at[pl.ds(pl.multiple_of(chip * k, 128), k), :]


def _chip_index():
    return jnp.reshape(2 * lax.axis_index("x") + lax.axis_index("y"), (1,)).astype(jnp.int32)


def _shard_block(name, tr):
    ks, ns = _shard_shape(name)
    if COL_SHARDED[name]:
        return (tr, ns), lambda i, me: (i, me[0])
    return (tr, ns), lambda i, me: (me[0] * (ks // tr) + i, 0)


def _place_shard(w, name):
    ks, ns = _shard_shape(name)
    tr = min(ks, 256)
    shape, index = _shard_block(name, tr)

    def body(me_ref, w_ref, o_ref):
        o_ref[...] = w_ref[...].astype(o_ref.dtype)

    return pl.pallas_call(
        body, name="place_" + name,
        grid_spec=pltpu.PrefetchScalarGridSpec(
            num_scalar_prefetch=1, grid=(2, ks // tr),
            in_specs=[pl.BlockSpec((None, tr, ns), lambda l, i, me: (l, i, 0))],
            out_specs=pl.BlockSpec((None,) + shape, lambda l, i, me: (l,) + index(i, me))),
        out_shape=jax.ShapeDtypeStruct((2,) + FULL_SHAPE[name], MXU_DTYPE),
        compiler_params=_cparams(("parallel", "parallel")),
    )(_chip_index(), w)


def _gather_weights(full):
    names = list(BIG)

    def body(*refs):
        ins = refs[:len(names)]
        outs = refs[len(names):2 * len(names)]
        send_ici, recv_ici, send_d2d, recv_d2d = refs[2 * len(names):]
        x, y, c = _my_place()
        me = 2 * x + y
        sibling = (x, y, 1 - c)
        chips = _other_chips(x, y)
        ici = []
        for t, name in enumerate(names):
            for j, (cx, cy) in enumerate(chips):
                cp = pltpu.make_async_remote_copy(
                    src_ref=_window(ins[t].at[c], name, me), dst_ref=_window(outs[t].at[c], name, me),
                    send_sem=send_ici.at[3 * t + j], recv_sem=recv_ici.at[3 * t + j],
                    device_id=(cx, cy, c), device_id_type=MESH)
                cp.start()
                ici.append(cp)
        fwd = []
        for t, name in enumerate(names):
            for j, (cx, cy) in enumerate(chips):
                land = _window(outs[t].at[c], name, 2 * cx + cy)
                pltpu.make_async_remote_copy(
                    src_ref=land, dst_ref=land, send_sem=send_ici.at[3 * t + j], recv_sem=recv_ici.at[3 * t + j],
                    device_id=(cx, cy, c), device_id_type=MESH).wait_recv()
                cp = pltpu.make_async_remote_copy(
                    src_ref=land, dst_ref=land, send_sem=send_d2d.at[3 * t + j], recv_sem=recv_d2d.at[3 * t + j],
                    device_id=sibling, device_id_type=MESH)
                cp.start()
                fwd.append(cp)
        for t, name in enumerate(names):
            for j, (cx, cy) in enumerate(chips):
                land = _window(outs[t].at[1 - c], name, 2 * cx + cy)
                pltpu.make_async_remote_copy(
                    src_ref=land, dst_ref=land, send_sem=send_d2d.at[3 * t + j], recv_sem=recv_d2d.at[3 * t + j],
                    device_id=sibling, device_id_type=MESH).wait_recv()
        for cp in ici + fwd:
            cp.wait_send()

    nsem = 3 * len(names)
    outs = pl.pallas_call(
        body, name="gather_weights",
        in_specs=[HBM] * len(names), out_specs=[HBM] * len(names),
        out_shape=[jax.ShapeDtypeStruct(full[n].shape, full[n].dtype) for n in names],
        input_output_aliases={t: t for t in range(len(names))},
        scratch_shapes=[pltpu.SemaphoreType.DMA((nsem,)), pltpu.SemaphoreType.DMA((nsem,)),
                        pltpu.SemaphoreType.DMA((nsem,)), pltpu.SemaphoreType.DMA((nsem,))],
    )(*[full[n] for n in names])
    return dict(zip(names, outs))


def _swap_layers(grads):
    names = list(BIG)

    def body(*refs):
        ins = refs[:len(names)]
        outs = refs[len(names):2 * len(names)]
        send_sems, recv_sems = refs[2 * len(names):]
        x, y, c = _my_place()
        sibling = (x, y, 1 - c)
        cps = []
        for t in range(len(names)):
            cp = pltpu.make_async_remote_copy(
                src_ref=ins[t].at[1 - c], dst_ref=outs[t], send_sem=send_sems.at[t], recv_sem=recv_sems.at[t],
                device_id=sibling, device_id_type=MESH)
            cp.start()
            cps.append(cp)
        for cp in cps:
            cp.wait()

    outs = pl.pallas_call(
        body, name="swap_layers", in_specs=[HBM] * len(names), out_specs=[HBM] * len(names),
        out_shape=[jax.ShapeDtypeStruct(FULL_SHAPE[n], f32) for n in names],
        scratch_shapes=[pltpu.SemaphoreType.DMA((len(names),)), pltpu.SemaphoreType.DMA((len(names),))],
    )(*[grads[n] for n in names])
    return dict(zip(names, outs))


def _chip_sum(grad, other, name):
    k, n = FULL_SHAPE[name]
    tr = min(k, 512)
    c = lax.axis_index("c")

    def body(c_ref, g_ref, o_ref, out_ref):
        out_ref[...] = (g_ref[...] + o_ref[...]).astype(out_ref.dtype)

    return pl.pallas_call(
        body, name="chip_sum_" + name,
        grid_spec=pltpu.PrefetchScalarGridSpec(
            num_scalar_prefetch=1, grid=(k // tr,),
            in_specs=[pl.BlockSpec((None, tr, n), lambda i, c_ref: (c_ref[0], i, 0)),
                      pl.BlockSpec((tr, n), lambda i, c_ref: (i, 0))],
            out_specs=pl.BlockSpec((tr, n), lambda i, c_ref: (i, 0))),
        out_shape=jax.ShapeDtypeStruct((k, n), COMM_DTYPE),
        compiler_params=_cparams(("parallel",)),
    )(jnp.reshape(c, (1,)).astype(jnp.int32), grad, other)


def _scatter_shards(sums):
    names = list(BIG)

    def body(*refs):
        ins = refs[:len(names)]
        outs = refs[len(names):2 * len(names)]
        send_sems, recv_sems = refs[2 * len(names):]
        x, y, c = _my_place()
        me = 2 * x + y
        chips = _other_chips(x, y)
        cps = []
        for t, name in enumerate(names):
            for j, (cx, cy) in enumerate(chips):
                cp = pltpu.make_async_remote_copy(
                    src_ref=_window(ins[t], name, 2 * cx + cy), dst_ref=outs[t].at[me],
                    send_sem=send_sems.at[3 * t + j], recv_sem=recv_sems.at[3 * t + j],
                    device_id=(cx, cy, c), device_id_type=MESH)
                cp.start()
                cps.append(cp)
        for t, name in enumerate(names):
            for j, (cx, cy) in enumerate(chips):
                land = outs[t].at[2 * cx + cy]
                pltpu.make_async_remote_copy(
                    src_ref=land, dst_ref=land, send_sem=send_sems.at[3 * t + j], recv_sem=recv_sems.at[3 * t + j],
                    device_id=(cx, cy, c), device_id_type=MESH).wait_recv()
        for cp in cps:
            cp.wait_send()

    nsem = 3 * len(names)
    outs = pl.pallas_call(
        body, name="scatter_shards", in_specs=[HBM] * len(names), out_specs=[HBM] * len(names),
        out_shape=[jax.ShapeDtypeStruct((N_CHIPS,) + _shard_shape(n), sums[n].dtype) for n in names],
        scratch_shapes=[pltpu.SemaphoreType.DMA((nsem,)), pltpu.SemaphoreType.DMA((nsem,))],
    )(*[sums[n] for n in names])
    return dict(zip(names, outs))


def _sum_slots(slots, own, name):
    ks, ns = _shard_shape(name)
    tr = min(ks, 256)
    shape, index = _shard_block(name, tr)

    def body(me_ref, c_ref, s_ref, own_ref, out_ref):
        me = me_ref[0]
        acc = None
        for s in range(N_CHIPS):
            term = jnp.where(me == s, own_ref[...], s_ref[s]).astype(f32)
            acc = term if acc is None else acc + term
        out_ref[...] = acc

    return pl.pallas_call(
        body, name="sum_slots_" + name,
        grid_spec=pltpu.PrefetchScalarGridSpec(
            num_scalar_prefetch=2, grid=(ks // tr,),
            in_specs=[pl.BlockSpec((N_CHIPS, tr, ns), lambda i, me, c: (0, i, 0)),
                      pl.BlockSpec(shape, lambda i, me, c: index(i, me))],
            out_specs=pl.BlockSpec((None, tr, ns), lambda i, me, c: (c[0], i, 0))),
        out_shape=jax.ShapeDtypeStruct((2, ks, ns), f32),
        compiler_params=_cparams(("parallel",)),
    )(_chip_index(), jnp.reshape(lax.axis_index("c"), (1,)).astype(jnp.int32), slots, own)


def _pair_layers(mine):
    names = list(BIG)

    def body(*refs):
        ins = refs[:len(names)]
        outs = refs[len(names):2 * len(names)]
        send_sems, recv_sems = refs[2 * len(names):]
        x, y, c = _my_place()
        sibling = (x, y, 1 - c)
        cps = []
        for t in range(len(names)):
            cp = pltpu.make_async_remote_copy(
                src_ref=ins[t].at[c], dst_ref=outs[t].at[c], send_sem=send_sems.at[t], recv_sem=recv_sems.at[t],
                device_id=sibling, device_id_type=MESH)
            cp.start()
            cps.append(cp)
        for t in range(len(names)):
            cps[t].wait_send()
            land = outs[t].at[1 - c]
            pltpu.make_async_remote_copy(
                src_ref=land, dst_ref=land, send_sem=send_sems.at[t], recv_sem=recv_sems.at[t],
                device_id=sibling, device_id_type=MESH).wait_recv()

    outs = pl.pallas_call(
        body, name="pair_layers", in_specs=[HBM] * len(names), out_specs=[HBM] * len(names),
        out_shape=[jax.ShapeDtypeStruct((2,) + _shard_shape(n), f32) for n in names],
        input_output_aliases={t: t for t in range(len(names))},
        scratch_shapes=[pltpu.SemaphoreType.DMA((len(names),)), pltpu.SemaphoreType.DMA((len(names),))],
    )(*[mine[n] for n in names])
    return dict(zip(names, outs))


SMALL_ROWS = 320


def _allreduce_small(vec):
    n_dev = 8

    def body(v_ref, out_ref, buf_ref, send_sems, recv_sems):
        x, y, c = _my_place()
        me = 4 * x + 2 * y + c
        buf_ref[me] = v_ref[...]
        cps = []
        for k in range(1, n_dev):
            dx, dy, dc = (k >> 2) & 1, (k >> 1) & 1, k & 1
            peer = (x ^ dx, y ^ dy, c ^ dc)
            cp = pltpu.make_async_remote_copy(
                src_ref=v_ref, dst_ref=buf_ref.at[me], send_sem=send_sems.at[k - 1], recv_sem=recv_sems.at[k - 1],
                device_id=peer, device_id_type=MESH)
            cp.start()
            cps.append(cp)
        for k in range(1, n_dev):
            dx, dy, dc = (k >> 2) & 1, (k >> 1) & 1, k & 1
            src = 4 * (x ^ dx) + 2 * (y ^ dy) + (c ^ dc)
            land = buf_ref.at[src]
            pltpu.make_async_remote_copy(
                src_ref=land, dst_ref=land, send_sem=send_sems.at[k - 1], recv_sem=recv_sems.at[k - 1],
                device_id=(x ^ dx, y ^ dy, c ^ dc), device_id_type=MESH).wait_recv()
        for cp in cps:
            cp.wait_send()
        acc = buf_ref[0]
        for s in range(1, n_dev):
            acc = acc + buf_ref[s]
        out_ref[...] = acc

    return pl.pallas_call(
        body, name="allreduce_small",
        in_specs=[pl.BlockSpec(memory_space=pltpu.VMEM)], out_specs=pl.BlockSpec(memory_space=pltpu.VMEM),
        out_shape=jax.ShapeDtypeStruct((SMALL_ROWS, 128), f32),
        scratch_shapes=[pltpu.VMEM((n_dev, SMALL_ROWS, 128), f32), pltpu.SemaphoreType.DMA((n_dev - 1,)),
                        pltpu.SemaphoreType.DMA((n_dev - 1,))],
    )(vec)


def _adamw(w, g, m, v, name):
    rows, cols = w.shape
    tr = rows
    for cand in (512, 256, 128, 64, 32, 16, 8):
        if rows % cand == 0 and cand * cols * 4 <= 2 * 1024 * 1024:
            tr = cand
            break
    c1 = np.float32(1.0 - ADAM_B1 ** ADAM_STEP)
    c2 = np.float32(1.0 - ADAM_B2 ** ADAM_STEP)

    def body(w_ref, g_ref, m_ref, v_ref, d_ref, mo_ref, vo_ref):
        gv = g_ref[...]
        mn = ADAM_B1 * m_ref[...] + (1.0 - ADAM_B1) * gv
        vn = ADAM_B2 * v_ref[...] + (1.0 - ADAM_B2) * (gv * gv)
        mo_ref[...] = mn
        vo_ref[...] = vn
        d_ref[...] = -ADAM_LR * ((mn / c1) / (jnp.sqrt(vn / c2) + ADAM_EPS) + ADAM_WD * w_ref[...])

    blk = pl.BlockSpec((tr, cols), lambda i: (i, 0))
    return pl.pallas_call(
        body, name="adamw_" + name, grid=(rows // tr,), in_specs=[blk] * 4, out_specs=[blk] * 3,
        out_shape=[jax.ShapeDtypeStruct((rows, cols), f32)] * 3,
        compiler_params=_cparams(("parallel",)),
    )(w, g, m, v)


SMALL = ("norm1", "pool_w", "pool_scale", "norm2", "norm3", "final_norm")
ORDER = ("norm1", "w_in", "pool_w", "pool_scale", "w_out", "norm2", "w_up", "w_down", "norm3", "w_gate", "w_ple",
         "final_norm")


def _pack_small(tree, extra=None):
    parts = [tree[n].reshape(-1) for n in SMALL]
    if extra is not None:
        parts.append(extra.reshape(-1))
    flat = jnp.concatenate(parts)
    return jnp.pad(flat, (0, SMALL_ROWS * 128 - flat.shape[0])).reshape(SMALL_ROWS, 128)


def _unpack_small(packed, like):
    flat = packed.reshape(-1)
    out, off = {}, 0
    for n in SMALL:
        size = int(np.prod(like[n].shape))
        out[n] = flat[off:off + size].reshape(like[n].shape)
        off += size
    return out, flat[off]


def kernel(x, p, positions, norm1, w_in, pool_w, pool_scale, w_out, norm2, w_up, w_down, norm3, w_gate, w_ple, final_norm, loss_target, m_norm1, m_w_in, m_pool_w, m_pool_scale, m_w_out, m_norm2, m_w_up, m_w_down, m_norm3, m_w_gate, m_w_ple, m_final_norm, v_norm1, v_w_in, v_pool_w, v_pool_scale, v_w_out, v_norm2, v_w_up, v_w_down, v_norm3, v_w_gate, v_w_ple, v_final_norm):
    w = dict(norm1=norm1, w_in=w_in, pool_w=pool_w, pool_scale=pool_scale, w_out=w_out, norm2=norm2, w_up=w_up,
             w_down=w_down, norm3=norm3, w_gate=w_gate, w_ple=w_ple, final_norm=final_norm)
    m = dict(norm1=m_norm1, w_in=m_w_in, pool_w=m_pool_w, pool_scale=m_pool_scale, w_out=m_w_out, norm2=m_norm2,
             w_up=m_w_up, w_down=m_w_down, norm3=m_norm3, w_gate=m_w_gate, w_ple=m_w_ple, final_norm=m_final_norm)
    v = dict(norm1=v_norm1, w_in=v_w_in, pool_w=v_pool_w, pool_scale=v_pool_scale, w_out=v_w_out, norm2=v_norm2,
             w_up=v_w_up, w_down=v_w_down, norm3=v_norm3, w_gate=v_w_gate, w_ple=v_w_ple, final_norm=v_final_norm)
    small = {n: w[n] for n in SMALL}

    full = _gather_weights({n: _place_shard(w[n], n) for n in BIG})
    loss8, dx, grads, small_grads = _local_step(x[0], p[:, 0], positions[0], full, small, loss_target[0])

    other = _swap_layers(grads)
    sums = {n: _chip_sum(grads[n], other[n], n) for n in BIG}
    slots = _scatter_shards(sums)
    mine = {n: _sum_slots(slots[n], sums[n], n) for n in BIG}
    gsh = _pair_layers(mine)

    red = _allreduce_small(_pack_small(small_grads, loss8[0, 0]))
    g_small, loss = _unpack_small(red, small)

    g_out, d_out, m_out, v_out = {}, {}, {}, {}
    for n in BIG:
        shp = w[n].shape
        two = lambda a: a.reshape(shp[0] * shp[1], shp[2])
        d2, m2, v2 = _adamw(two(w[n]), two(gsh[n]), two(m[n]), two(v[n]), n)
        g_out[n], d_out[n], m_out[n], v_out[n] = gsh[n], d2.reshape(shp), m2.reshape(shp), v2.reshape(shp)
    d2, m2, v2 = _adamw(_pack_small(small), red, _pack_small({n: m[n] for n in SMALL}),
                        _pack_small({n: v[n] for n in SMALL}), "small")
    for tree, packed in ((d_out, d2), (m_out, m2), (v_out, v2)):
        tree.update(_unpack_small(packed, small)[0])
    g_out.update(g_small)

    return (loss, dx[None], *[g_out[n] for n in ORDER], *[d_out[n] for n in ORDER], *[m_out[n] for n in ORDER],
            *[v_out[n] for n in ORDER])
```

```python
import functools

import jax
import jax.numpy as jnp
import numpy as np
from jax import lax
from jax.experimental import pallas as pl
from jax.experimental.pallas import tpu as pltpu

f32 = jnp.float32
MXU_DTYPE = jnp.bfloat16
COMM_DTYPE = jnp.bfloat16

D_MODEL = 1024
POOL_WIDTH = 256
POOL_GC = 64
ATTN_WIDTH = 768
HEAD_DIM = 64
N_IN = POOL_WIDTH + 3 * ATTN_WIDTH
D_FF = 4096
PLE_DIM = 256
BLK = 128
DILATIONS = (1, 4, 16)
ROT_DIM = 16
ROPE_THETA = 500000.0
EPS = 1e-6
ATTN_SCALE = HEAD_DIM ** -0.5
NEG_BIG = -1e30

ADAM_LR, ADAM_B1, ADAM_B2, ADAM_EPS, ADAM_WD, ADAM_STEP = 0.001, 0.9, 0.999, 1e-08, 0.01, 10

TM = 512
HALO = 16
VMEM_LIMIT = 48 * 1024 * 1024
N_CHIPS = 4
MESH = pl.DeviceIdType.MESH

BIG = ("w_in", "w_out", "w_up", "w_down", "w_gate", "w_ple")
FULL_SHAPE = {"w_in": (D_MODEL, N_IN), "w_out": (D_MODEL, D_MODEL), "w_up": (D_MODEL, D_FF),
              "w_down": (D_FF, D_MODEL), "w_gate": (D_MODEL, D_MODEL), "w_ple": (PLE_DIM, D_MODEL)}
COL_SHARDED = {"w_in": True, "w_out": False, "w_up": True, "w_down": False, "w_gate": False, "w_ple": True}


def _shard_shape(name):
    k, n = FULL_SHAPE[name]
    return (k, n // N_CHIPS) if COL_SHARDED[name] else (k // N_CHIPS, n)


def _cparams(sem=None, vmem=VMEM_LIMIT):
    return pltpu.CompilerParams(dimension_semantics=sem, vmem_limit_bytes=vmem)


def _resident(block_shape, index_map):
    return pl.BlockSpec(block_shape, index_map, pipeline_mode=pl.Buffered(1))


def _mx(x):
    return x.astype(MXU_DTYPE)


def _dot(a, b):
    return jnp.dot(a, b, preferred_element_type=f32)


def _dot_nt(a, b):
    return lax.dot_general(a, b, (((1,), (1,)), ((), ())), preferred_element_type=f32)


def _dot_tn(a, b):
    return lax.dot_general(a, b, (((0,), (0,)), ((), ())), preferred_element_type=f32)


def _sigmoid(x):
    return 1.0 / (1.0 + jnp.exp(-x))


def _rope_apply(y, c, s1, s2, width):
    return y * c + pltpu.roll(y, width - 8, axis=1) * s1 + pltpu.roll(y, 8, axis=1) * s2


def _rope_transpose(dy, c, s1, s2, width):
    return dy * c + pltpu.roll(dy * s1, 8, axis=1) + pltpu.roll(dy * s2, width - 8, axis=1)


def _norm_matmul(h, g, w, layer, tn, name, rope=None):
    s_len, d = h.shape
    n = w.shape[2]

    def body(*refs):
        if rope is None:
            h_ref, g_ref, w_ref, y_ref, hn_ref = refs
        else:
            h_ref, g_ref, w_ref, c_ref, s1_ref, s2_ref, y_ref, hn_ref = refs
            reps = tn // 128
            c = jnp.concatenate([c_ref[...]] * reps, axis=1)
            s1 = jnp.concatenate([s1_ref[...]] * reps, axis=1)
            s2 = jnp.concatenate([s2_ref[...]] * reps, axis=1)
        x = h_ref[...]
        r = lax.rsqrt(jnp.mean(x * x, axis=-1, keepdims=True) + EPS)
        hn = ((x * r) * g_ref[...]).astype(hn_ref.dtype)
        hn_ref[...] = hn
        for j in range(n // tn):
            y = _dot(hn, w_ref[:, j * tn:(j + 1) * tn])
            if rope is not None and POOL_WIDTH <= j * tn < POOL_WIDTH + 2 * ATTN_WIDTH:
                y = _rope_apply(y, c, s1, s2, tn)
            y_ref[:, j * tn:(j + 1) * tn] = y

    in_specs = [pl.BlockSpec((TM, d), lambda i: (i, 0)),
                pl.BlockSpec((1, d), lambda i: (0, 0)),
                _resident((None, d, n), lambda i: (layer, 0, 0))]
    args = [h, g, w]
    if rope is not None:
        assert POOL_WIDTH % tn == 0 and (2 * ATTN_WIDTH) % tn == 0
        in_specs += [pl.BlockSpec((TM, 128), lambda i: (i, 0))] * 3
        args += list(rope)
    return pl.pallas_call(
        body, name=name, grid=(s_len // TM,), in_specs=in_specs,
        out_specs=[pl.BlockSpec((TM, n), lambda i: (i, 0)), pl.BlockSpec((TM, d), lambda i: (i, 0))],
        out_shape=[jax.ShapeDtypeStruct((s_len, n), f32), jax.ShapeDtypeStruct((s_len, d), MXU_DTYPE)],
        compiler_params=_cparams(("parallel",)),
    )(*args)


def _matmul_residual(a, w, layer, res, name, act=False, tk=1024):
    s_len, k_dim = a.shape
    n = w.shape[2]

    def body(a_ref, w_ref, res_ref, o_ref):
        acc = res_ref[...]
        for k in range(k_dim // tk):
            x = a_ref[:, k * tk:(k + 1) * tk]
            if act:
                r = jnp.maximum(x, 0.0)
                x = r * r
            acc = acc + _dot(_mx(x), w_ref[k * tk:(k + 1) * tk, :])
        o_ref[...] = acc

    return pl.pallas_call(
        body, name=name, grid=(s_len // TM,),
        in_specs=[pl.BlockSpec((TM, k_dim), lambda i: (i, 0)),
                  _resident((None, k_dim, n), lambda i: (layer, 0, 0)),
                  pl.BlockSpec((TM, n), lambda i: (i, 0))],
        out_specs=pl.BlockSpec((TM, n), lambda i: (i, 0)),
        out_shape=jax.ShapeDtypeStruct((s_len, n), f32),
        compiler_params=_cparams(("parallel",)),
    )(a, w, res)


def _gate_ple_fwd(h2, g, w_gate, w_ple, layer, p, name):
    s_len, d = h2.shape

    def body(h_ref, g_ref, wg_ref, p_ref, wp_ref, h3_ref, gl_ref, hn_ref):
        x = h_ref[...]
        r = lax.rsqrt(jnp.mean(x * x, axis=-1, keepdims=True) + EPS)
        hn = ((x * r) * g_ref[...]).astype(hn_ref.dtype)
        hn_ref[...] = hn
        gl = _dot(hn, wg_ref[...])
        gl_ref[...] = gl
        e = _dot(_mx(p_ref[...]), wp_ref[...])
        h3_ref[...] = x + _sigmoid(gl) * e

    row = lambda i: (i, 0)
    return pl.pallas_call(
        body, name=name, grid=(s_len // TM,),
        in_specs=[pl.BlockSpec((TM, d), row), pl.BlockSpec((1, d), lambda i: (0, 0)),
                  pl.BlockSpec((None, d, d), lambda i: (layer, 0, 0)), pl.BlockSpec((TM, PLE_DIM), row),
                  pl.BlockSpec((None, PLE_DIM, d), lambda i: (layer, 0, 0))],
        out_specs=[pl.BlockSpec((TM, d), row)] * 3,
        out_shape=[jax.ShapeDtypeStruct((s_len, d), f32), jax.ShapeDtypeStruct((s_len, d), f32),
                   jax.ShapeDtypeStruct((s_len, d), MXU_DTYPE)],
        compiler_params=_cparams(("parallel",)),
    )(h2, g, w_gate, p, w_ple)


def _gate_ple_bwd(dh3, gl, p, w_ple, layer, name):
    s_len, d = dh3.shape

    def body(dh_ref, gl_ref, p_ref, wp_ref, de_ref, dgl_ref):
        dh = dh_ref[...]
        gate = _sigmoid(gl_ref[...])
        e = _dot(_mx(p_ref[...]), wp_ref[...])
        de_ref[...] = (dh * gate).astype(de_ref.dtype)
        dgl_ref[...] = ((dh * e) * (gate * (1.0 - gate))).astype(dgl_ref.dtype)

    row = lambda i: (i, 0)
    return pl.pallas_call(
        body, name=name, grid=(s_len // TM,),
        in_specs=[pl.BlockSpec((TM, d), row), pl.BlockSpec((TM, d), row), pl.BlockSpec((TM, PLE_DIM), row),
                  pl.BlockSpec((None, PLE_DIM, d), lambda i: (layer, 0, 0))],
        out_specs=[pl.BlockSpec((TM, d), row)] * 2,
        out_shape=[jax.ShapeDtypeStruct((s_len, d), MXU_DTYPE)] * 2,
        compiler_params=_cparams(("parallel",)),
    )(dh3, gl, p, w_ple)


def _rmsnorm_bwd(dhn, x, g):
    r = lax.rsqrt(jnp.mean(x * x, axis=-1, keepdims=True) + EPS)
    xh = x * r
    dxh = dhn * g
    dx = r * (dxh - xh * jnp.mean(dxh * xh, axis=-1, keepdims=True))
    return dx, dhn * xh


def _matmul_nt_norm_bwd(dy, w, layer, h_prev, g, dres, name, tk=1024):
    s_len, k_dim = dy.shape
    d = h_prev.shape[1]

    def body(dy_ref, w_ref, h_ref, g_ref, dres_ref, dh_ref, dg_ref):
        i = pl.program_id(0)
        acc = None
        for k in range(k_dim // tk):
            part = _dot_nt(_mx(dy_ref[:, k * tk:(k + 1) * tk]), w_ref[:, k * tk:(k + 1) * tk])
            acc = part if acc is None else acc + part
        dx, dgrow = _rmsnorm_bwd(acc, h_ref[...], g_ref[...])
        dh_ref[...] = dres_ref[...] + dx
        dgsum = jnp.sum(dgrow, axis=0, keepdims=True)

        @pl.when(i == 0)
        def _():
            dg_ref[...] = dgsum

        @pl.when(i > 0)
        def _():
            dg_ref[...] += dgsum

    return pl.pallas_call(
        body, name=name, grid=(s_len // TM,),
        in_specs=[pl.BlockSpec((TM, k_dim), lambda i: (i, 0)),
                  _resident((None, d, k_dim), lambda i: (layer, 0, 0)),
                  pl.BlockSpec((TM, d), lambda i: (i, 0)),
                  pl.BlockSpec((1, d), lambda i: (0, 0)),
                  pl.BlockSpec((TM, d), lambda i: (i, 0))],
        out_specs=[pl.BlockSpec((TM, d), lambda i: (i, 0)), pl.BlockSpec((1, d), lambda i: (0, 0))],
        out_shape=[jax.ShapeDtypeStruct((s_len, d), f32), jax.ShapeDtypeStruct((1, d), f32)],
        compiler_params=_cparams(("arbitrary",)),
    )(dy, w, h_prev, g, dres)


def _down_bwd(dh2, w_down, layer, a, name, tf=1024):
    s_len, d = dh2.shape
    ff = a.shape[1]

    def body(dh_ref, w_ref, a_ref, act_ref, da_ref, dhb_ref):
        j = pl.program_id(1)

        @pl.when(j == 0)
        def _():
            dhb_ref[...] = _mx(dh_ref[...])

        dact = _dot_nt(dhb_ref[...], w_ref[pl.ds(pl.multiple_of(j * tf, tf), tf), :])
        r = jnp.maximum(a_ref[...], 0.0)
        act_ref[...] = (r * r).astype(act_ref.dtype)
        da_ref[...] = (dact * (2.0 * r)).astype(da_ref.dtype)

    return pl.pallas_call(
        body, name=name, grid=(s_len // TM, ff // tf),
        in_specs=[pl.BlockSpec((TM, d), lambda i, j: (i, 0)),
                  _resident((None, ff, d), lambda i, j: (layer, 0, 0)),
                  pl.BlockSpec((TM, tf), lambda i, j: (i, j))],
        out_specs=[pl.BlockSpec((TM, tf), lambda i, j: (i, j))] * 2,
        out_shape=[jax.ShapeDtypeStruct((s_len, ff), MXU_DTYPE)] * 2,
        scratch_shapes=[pltpu.VMEM((TM, d), MXU_DTYPE)],
        compiler_params=_cparams(("parallel", "arbitrary")),
    )(dh2, w_down, a)


def _matmul_nt(dy, w, layer, name):
    s_len, n = dy.shape
    k_dim = w.shape[1]

    def body(dy_ref, w_ref, o_ref):
        o_ref[...] = _dot_nt(_mx(dy_ref[...]), w_ref[...])

    return pl.pallas_call(
        body, name=name, grid=(s_len // TM,),
        in_specs=[pl.BlockSpec((TM, n), lambda i: (i, 0)), pl.BlockSpec((None, k_dim, n), lambda i: (layer, 0, 0))],
        out_specs=pl.BlockSpec((TM, k_dim), lambda i: (i, 0)),
        out_shape=jax.ShapeDtypeStruct((s_len, k_dim), f32),
        compiler_params=_cparams(("parallel",)),
    )(dy, w)


def _weight_grad(a, b, layer, prev, name):
    s_len, k_dim = a.shape
    n = b.shape[1]
    tka = min(k_dim, 2048)
    tnb = n if n <= 1024 else (2048 if n % 2048 == 0 else 640)
    ns = s_len // TM

    def body(*refs):
        a_ref, b_ref = refs[0], refs[1]
        o_ref = refs[-1]
        s = pl.program_id(2)
        part = _dot_tn(_mx(a_ref[...]), _mx(b_ref[...]))

        @pl.when(s == 0)
        def _():
            o_ref[...] = part

        @pl.when(s > 0)
        def _():
            o_ref[...] += part

    in_specs = [pl.BlockSpec((TM, tka), lambda i, j, s: (s, i)), pl.BlockSpec((TM, tnb), lambda i, j, s: (s, j))]
    args = [a, b]
    aliases = {}
    if prev is not None:
        in_specs.append(pl.BlockSpec(memory_space=pl.ANY))
        args.append(prev)
        aliases = {2: 0}
    return pl.pallas_call(
        body, name=name, grid=(k_dim // tka, n // tnb, ns), in_specs=in_specs,
        out_specs=pl.BlockSpec((None, tka, tnb), lambda i, j, s: (layer, i, j)),
        out_shape=jax.ShapeDtypeStruct((2, k_dim, n), f32),
        input_output_aliases=aliases,
        compiler_params=_cparams(("parallel", "parallel", "arbitrary")),
    )(*args)


def _group_select(lane, x2, x4, x8, x16):
    grp = lane // POOL_GC
    return jnp.where(grp == 0, x2, jnp.where(grp == 1, x4, jnp.where(grp == 2, x8, x16)))


def _pool_window(lane):
    grp = lane // POOL_GC
    return jnp.where(grp == 0, 2, jnp.where(grp == 1, 4, jnp.where(grp == 2, 8, 16)))


def _pool_y(u, halo, i):
    xs = jnp.concatenate([jnp.where(i > 0, halo, 0.0), u], axis=0)
    s2 = xs + pltpu.roll(xs, 1, axis=0)
    s4 = s2 + pltpu.roll(s2, 2, axis=0)
    s8 = s4 + pltpu.roll(s4, 4, axis=0)
    s16 = s8 + pltpu.roll(s8, 8, axis=0)
    lane = lax.broadcasted_iota(jnp.int32, xs.shape, 1)
    sel = _group_select(lane, s2, s4, s8, s16)[HALO:, :]
    t = i * TM + lax.broadcasted_iota(jnp.int32, u.shape, 0)
    cnt = jnp.minimum(_pool_window(lax.broadcasted_iota(jnp.int32, u.shape, 1)), t + 1).astype(f32)
    return sel / cnt - u


def _group_weights(l0, l1, l2):
    mx = jnp.maximum(jnp.maximum(l0, l1), l2)
    e0, e1, e2 = jnp.exp(l0 - mx), jnp.exp(l1 - mx), jnp.exp(l2 - mx)
    den = e0 + e1 + e2
    return e0 / den, e1 / den, e2 / den


def _mixer_merge(z, wbd, scale, outs, lses, name):
    s_len = z.shape[0]

    def body(u_ref, halo_ref, wbd_ref, sc_ref, o0, o1, o2, l0, l1, l2, m_ref):
        i = pl.program_id(0)
        y = _pool_y(u_ref[...], halo_ref[...], i)
        pool = _dot(_mx(y), wbd_ref[...]) * sc_ref[...]
        w0, w1, w2 = _group_weights(l0[...], l1[...], l2[...])
        m_ref[...] = jnp.concatenate([pool, o0[...] * w0, o1[...] * w1, o2[...] * w2], axis=1).astype(m_ref.dtype)

    row = lambda i: (i, 0)
    blk = pl.BlockSpec((TM, 256), row)
    grp = [pl.BlockSpec((TM, 256), lambda i, g=g: (i, g)) for g in range(3)]
    return pl.pallas_call(
        body, name=name, grid=(s_len // TM,),
        in_specs=[blk, pl.BlockSpec((HALO, 256), lambda i: (jnp.maximum(i * (TM // HALO) - 1, 0), 0)),
                  pl.BlockSpec((256, 256), lambda i: (0, 0)), pl.BlockSpec((1, 256), lambda i: (0, 0))] + grp + grp,
        out_specs=pl.BlockSpec((TM, D_MODEL), row),
        out_shape=jax.ShapeDtypeStruct((s_len, D_MODEL), MXU_DTYPE),
        compiler_params=_cparams(("parallel",)),
    )(z, z, wbd, scale, outs, outs, outs, lses, lses, lses)


def _head_sums(x):
    r = lax.broadcasted_iota(jnp.int32, (256, 256), 0) // HEAD_DIM
    c = lax.broadcasted_iota(jnp.int32, (256, 256), 1) // HEAD_DIM
    ones = jnp.where(r == c, 1.0, 0.0).astype(jnp.bfloat16)
    hi = x.astype(jnp.bfloat16)
    lo = (x - hi.astype(f32)).astype(jnp.bfloat16)
    return _dot(hi, ones) + _dot(lo, ones)


def _combine_bwd(dm, outs, lses, name):
    s_len = dm.shape[0]

    def body(d0, d1, d2, o0, o1, o2, l0, l1, l2, do_ref, dl_ref):
        w = _group_weights(l0[...], l1[...], l2[...])
        da = (d0[...], d1[...], d2[...])
        o = (o0[...], o1[...], o2[...])
        dw = [_head_sums(da[g] * o[g]) for g in range(3)]
        t = w[0] * dw[0] + w[1] * dw[1] + w[2] * dw[2]
        do_ref[...] = jnp.concatenate([da[g] * w[g] for g in range(3)], axis=1)
        dl_ref[...] = jnp.concatenate([w[g] * t for g in range(3)], axis=1)

    grp = [pl.BlockSpec((TM, 256), lambda i, g=g: (i, g)) for g in range(3)]
    return pl.pallas_call(
        body, name=name, grid=(s_len // TM,),
        in_specs=[pl.BlockSpec((TM, 256), lambda i: (i, 1)), pl.BlockSpec((TM, 256), lambda i: (i, 2)),
                  pl.BlockSpec((TM, 256), lambda i: (i, 3))] + grp + grp,
        out_specs=[pl.BlockSpec((TM, ATTN_WIDTH), lambda i: (i, 0))] * 2,
        out_shape=[jax.ShapeDtypeStruct((s_len, ATTN_WIDTH), f32)] * 2,
        compiler_params=_cparams(("parallel",)),
    )(dm, dm, dm, outs, outs, outs, lses, lses, lses)


def _pool_bwd(z, dm, wbd, scale, name):
    s_len = z.shape[0]
    n_halo = s_len // HALO

    def body(u_ref, uh_ref, d_ref, dh_ref, wbd_ref, sc_ref, du_ref, dw_ref, dsc_ref):
        i = pl.program_id(0)
        last = pl.num_programs(0) - 1
        y = _pool_y(u_ref[...], uh_ref[...], i)
        yb = _mx(y)
        dpo = d_ref[...]
        sc = sc_ref[...]
        dsc = jnp.sum(dpo * _dot(yb, wbd_ref[...]), axis=0, keepdims=True)
        dwp = _dot_tn(yb, _mx(dpo * sc))

        @pl.when(i == 0)
        def _():
            dsc_ref[...] = dsc
            dw_ref[...] = dwp

        @pl.when(i > 0)
        def _():
            dsc_ref[...] += dsc
            dw_ref[...] += dwp

        ext = jnp.concatenate([dpo, jnp.where(i < last, dh_ref[...], 0.0)], axis=0)
        dy = _dot_nt(_mx(ext * sc), wbd_ref[...])
        t = i * TM + lax.broadcasted_iota(jnp.int32, ext.shape, 0)
        lane = lax.broadcasted_iota(jnp.int32, ext.shape, 1)
        e = dy / jnp.minimum(_pool_window(lane), t + 1).astype(f32)
        rows = ext.shape[0]
        f2 = e + pltpu.roll(e, rows - 1, axis=0)
        f4 = f2 + pltpu.roll(f2, rows - 2, axis=0)
        f8 = f4 + pltpu.roll(f4, rows - 4, axis=0)
        f16 = f8 + pltpu.roll(f8, rows - 8, axis=0)
        du_ref[...] = (_group_select(lane, f2, f4, f8, f16) - dy)[:TM, :]

    row = lambda i: (i, 0)
    blk = pl.BlockSpec((TM, 256), row)
    return pl.pallas_call(
        body, name=name, grid=(s_len // TM,),
        in_specs=[blk, pl.BlockSpec((HALO, 256), lambda i: (jnp.maximum(i * (TM // HALO) - 1, 0), 0)),
                  blk, pl.BlockSpec((HALO, 256), lambda i: (jnp.minimum((i + 1) * (TM // HALO), n_halo - 1), 0)),
                  pl.BlockSpec((256, 256), lambda i: (0, 0)), pl.BlockSpec((1, 256), lambda i: (0, 0))],
        out_specs=[blk, pl.BlockSpec((256, 256), lambda i: (0, 0)), pl.BlockSpec((1, 256), lambda i: (0, 0))],
        out_shape=[jax.ShapeDtypeStruct((s_len, N_IN), f32), jax.ShapeDtypeStruct((256, 256), f32),
                   jax.ShapeDtypeStruct((1, 256), f32)],
        compiler_params=_cparams(("arbitrary",)),
    )(z, z, dm, dm, wbd, scale)


def _to_strided(x, dil):
    if dil == 1:
        return x
    s_len, c = x.shape
    return x.reshape(s_len // (BLK * dil), BLK, dil, c).transpose(0, 2, 1, 3).reshape(s_len, c)


def _from_strided(x, dil):
    if dil == 1:
        return x
    s_len, c = x.shape
    return x.reshape(s_len // (BLK * dil), dil, BLK, c).transpose(0, 2, 1, 3).reshape(s_len, c)


def _tri_masks():
    qi = lax.broadcasted_iota(jnp.int32, (BLK, BLK), 0)
    ki = lax.broadcasted_iota(jnp.int32, (BLK, BLK), 1)
    return qi >= ki, ki >= qi


ATTN_SUPER_PER_STEP = (8, 2, 1)
Q_COL, K_COL, V_COL = POOL_WIDTH // 128, (POOL_WIDTH + ATTN_WIDTH) // 128, (POOL_WIDTH + 2 * ATTN_WIDTH) // 128


def _rows(ref, start, dil):
    if dil == 1:
        return ref[pl.ds(start, BLK), :]
    return ref[pl.ds(start, BLK, stride=dil), :]


def _set_rows(ref, start, dil, val):
    if dil == 1:
        ref[pl.ds(start, BLK), :] = val
    else:
        ref[pl.ds(start, BLK, stride=dil), :] = val


def _attn_fwd(z, g, prev, name):
    s_len = z.shape[0]
    dil, m = DILATIONS[g], ATTN_SUPER_PER_STEP[g]
    sbr = BLK * dil
    rows = sbr * m

    def body(*refs):
        q_ref, kc_ref, kp_ref, vc_ref, vp_ref = refs[:5]
        o_ref, l_ref = refs[-2:]
        st = pl.program_id(0)
        low, up = _tri_masks()
        for sb in range(m):
            valid = jnp.concatenate([up & (st > 0) if sb == 0 else up, low], axis=1)
            for r in range(dil):
                base = sb * sbr + r
                q = _rows(q_ref, base, dil)
                kc, vc = _rows(kc_ref, base, dil), _rows(vc_ref, base, dil)
                if sb == 0:
                    kp, vp = _rows(kp_ref, r, dil), _rows(vp_ref, r, dil)
                else:
                    kp, vp = _rows(kc_ref, base - sbr, dil), _rows(vc_ref, base - sbr, dil)
                outs, lses = [], []
                for hh in range(2):
                    sl = slice(hh * HEAD_DIM, (hh + 1) * HEAD_DIM)
                    k2 = jnp.concatenate([_mx(kp[:, sl]), _mx(kc[:, sl])], axis=0)
                    v2 = jnp.concatenate([_mx(vp[:, sl]), _mx(vc[:, sl])], axis=0)
                    s = jnp.where(valid, _dot_nt(_mx(q[:, sl]), k2) * ATTN_SCALE, NEG_BIG)
                    mx = jnp.max(s, axis=-1, keepdims=True)
                    e = jnp.exp(s - mx)
                    l = jnp.sum(e, axis=-1, keepdims=True)
                    outs.append(_dot(_mx(e / l), v2))
                    lses.append(jnp.broadcast_to(mx + jnp.log(l), (BLK, HEAD_DIM)))
                _set_rows(o_ref, base, dil, jnp.concatenate(outs, axis=1))
                _set_rows(l_ref, base, dil, jnp.concatenate(lses, axis=1))

    def cur(col):
        return pl.BlockSpec((rows, 128), lambda st, hp: (st, col + 2 * g + hp))

    def before(col):
        return pl.BlockSpec((sbr, 128), lambda st, hp: (jnp.maximum(st * m - 1, 0), col + 2 * g + hp))

    in_specs = [cur(Q_COL), cur(K_COL), before(K_COL), cur(V_COL), before(V_COL)]
    args = [z, z, z, z, z]
    aliases = {}
    if prev is not None:
        in_specs += [pl.BlockSpec(memory_space=pl.ANY)] * 2
        args += list(prev)
        aliases = {5: 0, 6: 1}
    return pl.pallas_call(
        body, name=name, grid=(s_len // rows, 2), in_specs=in_specs, out_specs=[cur(0), cur(0)],
        out_shape=[jax.ShapeDtypeStruct((s_len, ATTN_WIDTH), f32)] * 2, input_output_aliases=aliases,
        compiler_params=_cparams(("parallel", "parallel")),
    )(*args)


def _attn_bwd(z, do, lse, dlt, tabs, dz, g, name):
    s_len = z.shape[0]
    dil, m = DILATIONS[g], ATTN_SUPER_PER_STEP[g]
    sbr = BLK * dil
    rows = sbr * m
    nsteps = s_len // rows

    def body(q_ref, qn_ref, kc_ref, kp_ref, vc_ref, vp_ref, do_ref, don_ref, l_ref, ln_ref, d_ref, dn_ref,
             c_ref, s1_ref, s2_ref, dz_in, dz_ref, dq_buf, dk_buf, dv_buf, sems):
        del dz_in
        st, hp = pl.program_id(0), pl.program_id(1)
        low, up = _tri_masks()
        for sb in range(m):
            up_prev = up & (st > 0) if sb == 0 else up
            up_next = up & (st < nsteps - 1) if sb == m - 1 else up
            for r in range(dil):
                base = sb * sbr + r
                q, k, v = _rows(q_ref, base, dil), _rows(kc_ref, base, dil), _rows(vc_ref, base, dil)
                do_c, l_c, d_c = _rows(do_ref, base, dil), _rows(l_ref, base, dil), _rows(d_ref, base, dil)
                if sb == 0:
                    kp, vp = _rows(kp_ref, r, dil), _rows(vp_ref, r, dil)
                else:
                    kp, vp = _rows(kc_ref, base - sbr, dil), _rows(vc_ref, base - sbr, dil)
                if sb == m - 1:
                    qn, do_n = _rows(qn_ref, r, dil), _rows(don_ref, r, dil)
                    l_n, d_n = _rows(ln_ref, r, dil), _rows(dn_ref, r, dil)
                else:
                    qn, do_n = _rows(q_ref, base + sbr, dil), _rows(do_ref, base + sbr, dil)
                    l_n, d_n = _rows(l_ref, base + sbr, dil), _rows(d_ref, base + sbr, dil)
                dqs, dks, dvs = [], [], []
                for hh in range(2):
                    sl = slice(hh * HEAD_DIM, (hh + 1) * HEAD_DIM)
                    one = slice(hh * HEAD_DIM, hh * HEAD_DIM + 1)
                    qc, qx = _mx(q[:, sl]), _mx(qn[:, sl])
                    kc, kb = _mx(k[:, sl]), _mx(kp[:, sl])
                    vc, vb = _mx(v[:, sl]), _mx(vp[:, sl])
                    doc, dox = _mx(do_c[:, sl]), _mx(do_n[:, sl])
                    lc, lx, dc, dx = l_c[:, one], l_n[:, one], d_c[:, one], d_n[:, one]
                    p_a = jnp.where(low, jnp.exp(_dot_nt(qc, kc) * ATTN_SCALE - lc), 0.0)
                    ds_a = _mx(p_a * (_dot_nt(doc, vc) - dc) * ATTN_SCALE)
                    p_b = jnp.where(up_prev, jnp.exp(_dot_nt(qc, kb) * ATTN_SCALE - lc), 0.0)
                    ds_b = _mx(p_b * (_dot_nt(doc, vb) - dc) * ATTN_SCALE)
                    p_c = jnp.where(up_next, jnp.exp(_dot_nt(qx, kc) * ATTN_SCALE - lx), 0.0)
                    ds_c = _mx(p_c * (_dot_nt(dox, vc) - dx) * ATTN_SCALE)
                    dqs.append(_dot(ds_a, kc) + _dot(ds_b, kb))
                    dks.append(_dot_tn(ds_a, qc) + _dot_tn(ds_c, qx))
                    dvs.append(_dot_tn(_mx(p_a), doc) + _dot_tn(_mx(p_c), dox))
                c, s1, s2 = _rows(c_ref, base, dil), _rows(s1_ref, base, dil), _rows(s2_ref, base, dil)
                _set_rows(dq_buf, base, dil, _rope_transpose(jnp.concatenate(dqs, axis=1), c, s1, s2, 128))
                _set_rows(dk_buf, base, dil, _rope_transpose(jnp.concatenate(dks, axis=1), c, s1, s2, 128))
                _set_rows(dv_buf, base, dil, jnp.concatenate(dvs, axis=1))
        copies = []
        for t, (buf, col) in enumerate(((dq_buf, Q_COL), (dk_buf, K_COL), (dv_buf, V_COL))):
            lane0 = pl.multiple_of((col + 2 * g + hp) * 128, 128)
            dst = dz_ref.at[pl.ds(pl.multiple_of(st * rows, rows), rows), pl.ds(lane0, 128)]
            cp = pltpu.make_async_copy(buf, dst, sems.at[t])
            cp.start()
            copies.append(cp)
        for cp in copies:
            cp.wait()

    def cur(col):
        return pl.BlockSpec((rows, 128), lambda st, hp: (st, col + 2 * g + hp))

    def before(col):
        return pl.BlockSpec((sbr, 128), lambda st, hp: (jnp.maximum(st * m - 1, 0), col + 2 * g + hp))

    def after(col):
        return pl.BlockSpec((sbr, 128), lambda st, hp: (jnp.minimum((st + 1) * m, s_len // sbr - 1), col + 2 * g + hp))

    tab = pl.BlockSpec((rows, 128), lambda st, hp: (st, 0))
    return pl.pallas_call(
        body, name=name, grid=(nsteps, 2),
        in_specs=[cur(Q_COL), after(Q_COL), cur(K_COL), before(K_COL), cur(V_COL), before(V_COL),
                  cur(0), after(0), cur(0), after(0), cur(0), after(0), tab, tab, tab,
                  pl.BlockSpec(memory_space=pl.ANY)],
        out_specs=pl.BlockSpec(memory_space=pl.ANY),
        out_shape=jax.ShapeDtypeStruct(dz.shape, dz.dtype), input_output_aliases={15: 0},
        scratch_shapes=[pltpu.VMEM((rows, 128), f32)] * 3 + [pltpu.SemaphoreType.DMA((3,))],
        compiler_params=_cparams(("arbitrary", "arbitrary")),
    )(z, z, z, z, z, z, do, do, lse, lse, dlt, dlt, *tabs, dz)


def _attn_fwd_old(q, k, v, dil, name):
    s_len = q.shape[0]
    nblk = s_len // BLK

    def body(q_ref, kc_ref, kp_ref, vc_ref, vp_ref, o_ref, l_ref):
        b = pl.program_id(0)
        has_prev = b >= dil
        low, up = _tri_masks()
        valid = jnp.concatenate([up & has_prev, low], axis=1)
        outs, lses = [], []
        for hh in range(2):
            sl = slice(hh * HEAD_DIM, (hh + 1) * HEAD_DIM)
            qh = _mx(q_ref[:, sl])
            k2 = jnp.concatenate([_mx(kp_ref[:, sl]), _mx(kc_ref[:, sl])], axis=0)
            v2 = jnp.concatenate([_mx(vp_ref[:, sl]), _mx(vc_ref[:, sl])], axis=0)
            s = jnp.where(valid, _dot_nt(qh, k2) * ATTN_SCALE, NEG_BIG)
            m = jnp.max(s, axis=-1, keepdims=True)
            e = jnp.exp(s - m)
            l = jnp.sum(e, axis=-1, keepdims=True)
            outs.append(_dot(_mx(e / l), v2))
            lses.append(jnp.broadcast_to(m + jnp.log(l), (BLK, HEAD_DIM)))
        o_ref[...] = jnp.concatenate(outs, axis=1)
        l_ref[...] = jnp.concatenate(lses, axis=1)

    cur = pl.BlockSpec((BLK, 128), lambda b, hp: (b, hp))
    prev = pl.BlockSpec((BLK, 128), lambda b, hp: (jnp.maximum(b - dil, 0), hp))
    return pl.pallas_call(
        body, name=name, grid=(nblk, 2), in_specs=[cur, cur, prev, cur, prev], out_specs=[cur, cur],
        out_shape=[jax.ShapeDtypeStruct((s_len, 256), f32)] * 2,
        compiler_params=_cparams(("parallel", "parallel")),
    )(q, k, k, v, v)


def _attn_bwd_old(q, k, v, do, lse, dlt, tabs, dil, name):
    s_len = q.shape[0]
    nblk = s_len // BLK

    def body(q_ref, qn_ref, kc_ref, kp_ref, vc_ref, vp_ref, do_ref, don_ref, l_ref, ln_ref, d_ref, dn_ref,
             c_ref, s1_ref, s2_ref, dq_ref, dk_ref, dv_ref):
        b = pl.program_id(0)
        has_prev = b >= dil
        has_next = b + dil < nblk
        low, up = _tri_masks()
        dqs, dks, dvs = [], [], []
        for hh in range(2):
            sl = slice(hh * HEAD_DIM, (hh + 1) * HEAD_DIM)
            one = slice(hh * HEAD_DIM, hh * HEAD_DIM + 1)
            qc, qn = _mx(q_ref[:, sl]), _mx(qn_ref[:, sl])
            kc, kp = _mx(kc_ref[:, sl]), _mx(kp_ref[:, sl])
            vc, vp = _mx(vc_ref[:, sl]), _mx(vp_ref[:, sl])
            doc, don = _mx(do_ref[:, sl]), _mx(don_ref[:, sl])
            lc, ln = l_ref[:, one], ln_ref[:, one]
            dc, dn = d_ref[:, one], dn_ref[:, one]
            p_a = jnp.where(low, jnp.exp(_dot_nt(qc, kc) * ATTN_SCALE - lc), 0.0)
            ds_a = _mx(p_a * (_dot_nt(doc, vc) - dc) * ATTN_SCALE)
            p_b = jnp.where(up & has_prev, jnp.exp(_dot_nt(qc, kp) * ATTN_SCALE - lc), 0.0)
            ds_b = _mx(p_b * (_dot_nt(doc, vp) - dc) * ATTN_SCALE)
            p_c = jnp.where(up & has_next, jnp.exp(_dot_nt(qn, kc) * ATTN_SCALE - ln), 0.0)
            ds_c = _mx(p_c * (_dot_nt(don, vc) - dn) * ATTN_SCALE)
            dqs.append(_dot(ds_a, kc) + _dot(ds_b, kp))
            dks.append(_dot_tn(ds_a, qc) + _dot_tn(ds_c, qn))
            dvs.append(_dot_tn(_mx(p_a), doc) + _dot_tn(_mx(p_c), don))
        c, s1, s2 = c_ref[...], s1_ref[...], s2_ref[...]
        dq_ref[...] = _rope_transpose(jnp.concatenate(dqs, axis=1), c, s1, s2, 128)
        dk_ref[...] = _rope_transpose(jnp.concatenate(dks, axis=1), c, s1, s2, 128)
        dv_ref[...] = jnp.concatenate(dvs, axis=1)

    cur = pl.BlockSpec((BLK, 128), lambda b, hp: (b, hp))
    prev = pl.BlockSpec((BLK, 128), lambda b, hp: (jnp.maximum(b - dil, 0), hp))
    nxt = pl.BlockSpec((BLK, 128), lambda b, hp: (jnp.minimum(b + dil, nblk - 1), hp))
    tab = pl.BlockSpec((BLK, 128), lambda b, hp: (b, 0))
    return pl.pallas_call(
        body, name=name, grid=(nblk, 2),
        in_specs=[cur, nxt, cur, prev, cur, prev, cur, nxt, cur, nxt, cur, nxt, tab, tab, tab],
        out_specs=[cur, cur, cur], out_shape=[jax.ShapeDtypeStruct((s_len, 256), f32)] * 3,
        compiler_params=_cparams(("parallel", "parallel")),
    )(q, q, k, k, v, v, do, do, lse, lse, dlt, dlt, *tabs)


def _loss_head(h, g, target, name):
    s_len, d = h.shape

    def body(h_ref, g_ref, t_ref, loss_ref, dh_ref, dg_ref):
        i = pl.program_id(0)
        x = h_ref[...]
        gv = g_ref[...]
        r = lax.rsqrt(jnp.mean(x * x, axis=-1, keepdims=True) + EPS)
        xh = x * r
        diff = xh * gv - t_ref[...]
        part = 0.5 * jnp.sum(jnp.mean(diff * diff, axis=-1, keepdims=True), axis=0, keepdims=True)
        dy = diff * (1.0 / d)
        dxh = dy * gv
        dh_ref[...] = r * (dxh - xh * jnp.mean(dxh * xh, axis=-1, keepdims=True))
        dgsum = jnp.sum(dy * xh, axis=0, keepdims=True)
        lossb = jnp.broadcast_to(part, (8, 128))

        @pl.when(i == 0)
        def _():
            loss_ref[...] = lossb
            dg_ref[...] = dgsum

        @pl.when(i > 0)
        def _():
            loss_ref[...] += lossb
            dg_ref[...] += dgsum

    row = lambda i: (i, 0)
    return pl.pallas_call(
        body, name=name, grid=(s_len // TM,),
        in_specs=[pl.BlockSpec((TM, d), row), pl.BlockSpec((1, d), lambda i: (0, 0)), pl.BlockSpec((TM, d), row)],
        out_specs=[pl.BlockSpec((8, 128), lambda i: (0, 0)), pl.BlockSpec((TM, d), row),
                   pl.BlockSpec((1, d), lambda i: (0, 0))],
        out_shape=[jax.ShapeDtypeStruct((8, 128), f32), jax.ShapeDtypeStruct((s_len, d), f32),
                   jax.ShapeDtypeStruct((1, d), f32)],
        compiler_params=_cparams(("arbitrary",)),
    )(h, g, target)


def _rope_tables(positions):
    inv_freq = ROPE_THETA ** (-jnp.arange(0, ROT_DIM, 2, dtype=f32) / ROT_DIM)
    ang = positions.astype(f32)[:, None] * inv_freq
    cos, sin = jnp.cos(ang), jnp.sin(ang)
    s_len = positions.shape[0]
    zero8, rest = jnp.zeros((s_len, 8), f32), jnp.zeros((s_len, HEAD_DIM - ROT_DIM), f32)
    c = jnp.concatenate([cos, cos, jnp.ones((s_len, HEAD_DIM - ROT_DIM), f32)], axis=1)
    s1 = jnp.concatenate([-sin, zero8, rest], axis=1)
    s2 = jnp.concatenate([zero8, sin, rest], axis=1)
    return c, s1, s2


def _block_diag(pool_w):
    out = jnp.zeros((POOL_WIDTH, POOL_WIDTH), pool_w.dtype)
    for g in range(4):
        out = lax.dynamic_update_slice(out, pool_w[g], (g * POOL_GC, g * POOL_GC))
    return out


class _ReadyWeights:
    def __init__(self, full):
        self.full = full

    def take(self, layer, names, after):
        del after
        return {n: self.full[n] for n in names}, layer


def _layer_fwd(h, p_l, wsrc, small, layer, tabs):
    nm = f"l{layer}_"
    wts, wl = wsrc.take(layer, ("w_in",), h if layer else None)
    z, hn1 = _norm_matmul(h, small["norm1"][layer][None], wts["w_in"], wl, 256, nm + "in_proj", rope=tabs)
    ol = None
    for g in range(3):
        ol = _attn_fwd(z, g, ol, nm + f"attn_fwd{g}")
    outs, lses = ol
    wbd = _mx(_block_diag(small["pool_w"][layer]))
    scale = small["pool_scale"][layer][None]
    m = _mixer_merge(z, wbd, scale, outs, lses, nm + "mixer_merge")
    rest, _ = wsrc.take(layer, ("w_out", "w_up", "w_down", "w_gate", "w_ple"), m)
    wts = {**wts, **rest}
    h1 = _matmul_residual(m, wts["w_out"], wl, h, nm + "out_proj")
    a, hn2 = _norm_matmul(h1, small["norm2"][layer][None], wts["w_up"], wl, 1024, nm + "up_proj")
    h2 = _matmul_residual(a, wts["w_down"], wl, h1, nm + "down_proj", act=True)
    h3, gl, hn3 = _gate_ple_fwd(h2, small["norm3"][layer][None], wts["w_gate"], wts["w_ple"], wl, p_l, nm + "gate_ple")
    saved = dict(h=h, z=z, hn1=hn1, outs=outs, lses=lses, wbd=wbd, scale=scale, m=m, h1=h1, a=a, hn2=hn2, h2=h2,
                 gl=gl, hn3=hn3, wts=wts, wl=wl)
    return h3, saved


def _layer_bwd(dh3, sv, p_l, small, layer, tabs128, grads):
    nm = f"l{layer}_"
    gb = {}
    wts, wl = sv["wts"], sv["wl"]
    de, dgl = _gate_ple_bwd(dh3, sv["gl"], p_l, wts["w_ple"], wl, nm + "gate_ple_bwd")
    gb["w_gate"] = _weight_grad(sv["hn3"], dgl, layer, grads.get("w_gate"), nm + "dw_gate")
    gb["w_ple"] = _weight_grad(p_l, de, layer, grads.get("w_ple"), nm + "dw_ple")
    dh2, dg3 = _matmul_nt_norm_bwd(dgl, wts["w_gate"], wl, sv["h2"], small["norm3"][layer][None], dh3, nm + "gate_bwd")
    act, da = _down_bwd(dh2, wts["w_down"], wl, sv["a"], nm + "down_bwd")
    gb["w_down"] = _weight_grad(act, dh2, layer, grads.get("w_down"), nm + "dw_down")
    gb["w_up"] = _weight_grad(sv["hn2"], da, layer, grads.get("w_up"), nm + "dw_up")
    dh1, dg2 = _matmul_nt_norm_bwd(da, wts["w_up"], wl, sv["h1"], small["norm2"][layer][None], dh2, nm + "up_bwd")
    dm = _matmul_nt(dh1, wts["w_out"], wl, nm + "out_bwd")
    gb["w_out"] = _weight_grad(sv["m"], dh1, layer, grads.get("w_out"), nm + "dw_out")
    do, dlt = _combine_bwd(dm, sv["outs"], sv["lses"], nm + "combine_bwd")
    dz, dwbd, dscale = _pool_bwd(sv["z"], dm, sv["wbd"], sv["scale"], nm + "pool_bwd")
    for g in range(3):
        dz = _attn_bwd(sv["z"], do, sv["lses"], dlt, tabs128, dz, g, nm + f"attn_bwd{g}")
    gb["w_in"] = _weight_grad(sv["hn1"], dz, layer, grads.get("w_in"), nm + "dw_in")
    dh0, dg1 = _matmul_nt_norm_bwd(dz, wts["w_in"], wl, sv["h"], small["norm1"][layer][None], dh1, nm + "in_bwd",
                                   tk=512)
    dpool_w = jnp.stack([dwbd[g * POOL_GC:(g + 1) * POOL_GC, g * POOL_GC:(g + 1) * POOL_GC] for g in range(4)])
    sg = dict(norm1=dg1[0], norm2=dg2[0], norm3=dg3[0], pool_w=dpool_w, pool_scale=dscale[0])
    return dh0, gb, sg


def _local_step(x, p, positions, wsrc, small, target):
    tabs128 = tuple(jnp.tile(t, (1, 2)) for t in _rope_tables(positions))
    h = x
    saved = []
    for layer in range(2):
        h, sv = _layer_fwd(h, p[layer], wsrc, small, layer, tabs128)
        saved.append(sv)
    loss, dh, dgf = _loss_head(h, small["final_norm"][None], target, "loss_head")
    grads = {}
    sgs = [None, None]
    for layer in (1, 0):
        dh, grads, sgs[layer] = _layer_bwd(dh, saved[layer], p[layer], small, layer, tabs128, grads)
    small_grads = {k: jnp.stack([sgs[0][k], sgs[1][k]]) for k in sgs[0]}
    small_grads["final_norm"] = dgf[0]
    return loss, dh, grads, small_grads


HBM = pl.BlockSpec(memory_space=pltpu.HBM)


def _my_place():
    return lax.axis_index("x"), lax.axis_index("y"), lax.axis_index("c")


def _other_chips(x, y):
    return [(1 - x, y), (x, 1 - y), (1 - x, 1 - y)]


def _window(ref, name, chip):
    k, n = _shard_shape(name)
    if COL_SHARDED[name]:
        return ref.at[:, pl.ds(pl.multiple_of(chip * n, 128), n)]
    return ref.at[pl.ds(pl.multiple_of(chip * k, 128), k), :]


def _chip_index():
    return jnp.reshape(2 * lax.axis_index("x") + lax.axis_index("y"), (1,)).astype(jnp.int32)


def _shard_block(name, tr):
    ks, ns = _shard_shape(name)
    if COL_SHARDED[name]:
        return (tr, ns), lambda i, me: (i, me[0])
    return (tr, ns), lambda i, me: (me[0] * (ks // tr) + i, 0)


def _place_shard(w, name, layer):
    ks, ns = _shard_shape(name)
    tr = min(ks, 256)
    shape, index = _shard_block(name, tr)

    def body(me_ref, w_ref, o_ref):
        o_ref[...] = w_ref[...].astype(o_ref.dtype)

    return pl.pallas_call(
        body, name=f"place_{name}{layer}",
        grid_spec=pltpu.PrefetchScalarGridSpec(
            num_scalar_prefetch=1, grid=(ks // tr,),
            in_specs=[pl.BlockSpec((None, tr, ns), lambda i, me: (layer, i, 0))],
            out_specs=pl.BlockSpec((None,) + shape, lambda i, me: (0,) + index(i, me))),
        out_shape=jax.ShapeDtypeStruct((1,) + FULL_SHAPE[name], MXU_DTYPE),
        compiler_params=_cparams(("parallel",)),
    )(_chip_index(), w)


GATHER_ORDER = [("w_in", 0), ("w_out", 0), ("w_up", 0), ("w_down", 0), ("w_gate", 0), ("w_ple", 0),
                ("w_in", 1), ("w_out", 1), ("w_up", 1), ("w_down", 1), ("w_gate", 1), ("w_ple", 1)]
SEM = pl.BlockSpec(memory_space=pltpu.SEMAPHORE)
EFFECT = pltpu.SideEffectType.DATAFLOW_SIDE_EFFECTING


def _gather_copy(src_ref, dst_ref, name, idx, j, chip, send_sems, recv_sems, c):
    cx, cy = chip
    return pltpu.make_async_remote_copy(
        src_ref=src_ref, dst_ref=dst_ref, send_sem=send_sems.at[3 * idx + j], recv_sem=recv_sems.at[3 * idx + j],
        device_id=(cx, cy, c), device_id_type=MESH)


def _gather_start(placed):
    n = len(GATHER_ORDER)

    def body(*refs):
        ins = refs[:n]
        send_sems, recv_sems = refs[n], refs[n + 1]
        outs = refs[n + 2:]
        x, y, c = _my_place()
        me = 2 * x + y
        for idx, (name, _) in enumerate(GATHER_ORDER):
            for j, chip in enumerate(_other_chips(x, y)):
                _gather_copy(_window(ins[idx].at[0], name, me), _window(outs[idx].at[0], name, me), name, idx, j, chip,
                             send_sems, recv_sems, c).start()

    res = pl.pallas_call(
        body, name="gather_start",
        out_shape=(pltpu.SemaphoreType.DMA((3 * n,)), pltpu.SemaphoreType.DMA((3 * n,)))
        + tuple(pltpu.HBM(a.shape, a.dtype) for a in placed),
        in_specs=[HBM] * n, out_specs=(SEM, SEM) + (HBM,) * n,
        input_output_aliases={i: i + 2 for i in range(n)},
        compiler_params=pltpu.CompilerParams(has_side_effects=EFFECT),
    )(*[pltpu.with_memory_space_constraint(a, pltpu.HBM) for a in placed])
    return res[0], res[1], list(res[2:])


def _gather_wait(send_sems, recv_sems, arrays, idxs, after, name):
    n = len(idxs)

    def body(*refs):
        ins = refs[:n]
        send_ref, recv_ref = refs[n], refs[n + 1]
        x, y, c = _my_place()
        me = 2 * x + y
        for k, idx in enumerate(idxs):
            wname = GATHER_ORDER[idx][0]
            for j, chip in enumerate(_other_chips(x, y)):
                cx, cy = chip
                mine = _window(ins[k].at[0], wname, me)
                land = _window(ins[k].at[0], wname, 2 * cx + cy)
                _gather_copy(mine, mine, wname, idx, j, chip, send_ref, recv_ref, c).wait_send()
                _gather_copy(land, land, wname, idx, j, chip, send_ref, recv_ref, c).wait_recv()

    operands = list(arrays) + [send_sems, recv_sems]
    in_specs = [HBM] * n + [SEM, SEM]
    if after is not None:
        operands.append(after)
        in_specs.append(pl.BlockSpec(memory_space=pl.ANY))
    res = pl.pallas_call(
        body, name=name, out_shape=tuple(pltpu.HBM(a.shape, a.dtype) for a in arrays),
        in_specs=in_specs, out_specs=(HBM,) * n, input_output_aliases={i: i for i in range(n)},
        compiler_params=pltpu.CompilerParams(has_side_effects=EFFECT),
    )(*operands)
    return list(res)


class _GatheredWeights:
    def __init__(self, shards):
        placed = [_place_shard(shards[name], name, layer) for name, layer in GATHER_ORDER]
        self.send, self.recv, self.arrays = _gather_start(placed)
        self.ready = {}

    def take(self, layer, names, after):
        idxs = [GATHER_ORDER.index((n, layer)) for n in names]
        got = _gather_wait(self.send, self.recv, [self.arrays[i] for i in idxs], idxs, after,
                           f"gather_wait{layer}_{names[0]}")
        return dict(zip(names, got)), 0


def _gather_weights(full):
    names = list(BIG)

    def body(*refs):
        ins = refs[:len(names)]
        outs = refs[len(names):2 * len(names)]
        send_ici, recv_ici, send_d2d, recv_d2d = refs[2 * len(names):]
        x, y, c = _my_place()
        me = 2 * x + y
        sibling = (x, y, 1 - c)
        chips = _other_chips(x, y)
        ici = []
        for t, name in enumerate(names):
            for j, (cx, cy) in enumerate(chips):
                cp = pltpu.make_async_remote_copy(
                    src_ref=_window(ins[t].at[c], name, me), dst_ref=_window(outs[t].at[c], name, me),
                    send_sem=send_ici.at[3 * t + j], recv_sem=recv_ici.at[3 * t + j],
                    device_id=(cx, cy, c), device_id_type=MESH)
                cp.start()
                ici.append(cp)
        fwd = []
        for t, name in enumerate(names):
            for j, (cx, cy) in enumerate(chips):
                land = _window(outs[t].at[c], name, 2 * cx + cy)
                pltpu.make_async_remote_copy(
                    src_ref=land, dst_ref=land, send_sem=send_ici.at[3 * t + j], recv_sem=recv_ici.at[3 * t + j],
                    device_id=(cx, cy, c), device_id_type=MESH).wait_recv()
                cp = pltpu.make_async_remote_copy(
                    src_ref=land, dst_ref=land, send_sem=send_d2d.at[3 * t + j], recv_sem=recv_d2d.at[3 * t + j],
                    device_id=sibling, device_id_type=MESH)
                cp.start()
                fwd.append(cp)
        for t, name in enumerate(names):
            for j, (cx, cy) in enumerate(chips):
                land = _window(outs[t].at[1 - c], name, 2 * cx + cy)
                pltpu.make_async_remote_copy(
                    src_ref=land, dst_ref=land, send_sem=send_d2d.at[3 * t + j], recv_sem=recv_d2d.at[3 * t + j],
                    device_id=sibling, device_id_type=MESH).wait_recv()
        for cp in ici + fwd:
            cp.wait_send()

    nsem = 3 * len(names)
    outs = pl.pallas_call(
        body, name="gather_weights",
        in_specs=[HBM] * len(names), out_specs=[HBM] * len(names),
        out_shape=[jax.ShapeDtypeStruct(full[n].shape, full[n].dtype) for n in names],
        input_output_aliases={t: t for t in range(len(names))},
        scratch_shapes=[pltpu.SemaphoreType.DMA((nsem,)), pltpu.SemaphoreType.DMA((nsem,)),
                        pltpu.SemaphoreType.DMA((nsem,)), pltpu.SemaphoreType.DMA((nsem,))],
    )(*[full[n] for n in names])
    return dict(zip(names, outs))


def _swap_layers(grads):
    names = list(BIG)

    def body(*refs):
        ins = refs[:len(names)]
        outs = refs[len(names):2 * len(names)]
        send_sems, recv_sems = refs[2 * len(names):]
        x, y, c = _my_place()
        sibling = (x, y, 1 - c)
        cps = []
        for t in range(len(names)):
            cp = pltpu.make_async_remote_copy(
                src_ref=ins[t].at[1 - c], dst_ref=outs[t], send_sem=send_sems.at[t], recv_sem=recv_sems.at[t],
                device_id=sibling, device_id_type=MESH)
            cp.start()
            cps.append(cp)
        for cp in cps:
            cp.wait()

    outs = pl.pallas_call(
        body, name="swap_layers", in_specs=[HBM] * len(names), out_specs=[HBM] * len(names),
        out_shape=[jax.ShapeDtypeStruct(FULL_SHAPE[n], f32) for n in names],
        scratch_shapes=[pltpu.SemaphoreType.DMA((len(names),)), pltpu.SemaphoreType.DMA((len(names),))],
    )(*[grads[n] for n in names])
    return dict(zip(names, outs))


def _chip_sum(grad, other, name):
    k, n = FULL_SHAPE[name]
    tr = min(k, 512)
    c = lax.axis_index("c")

    def body(c_ref, g_ref, o_ref, out_ref):
        out_ref[...] = (g_ref[...] + o_ref[...]).astype(out_ref.dtype)

    return pl.pallas_call(
        body, name="chip_sum_" + name,
        grid_spec=pltpu.PrefetchScalarGridSpec(
            num_scalar_prefetch=1, grid=(k // tr,),
            in_specs=[pl.BlockSpec((None, tr, n), lambda i, c_ref: (c_ref[0], i, 0)),
                      pl.BlockSpec((tr, n), lambda i, c_ref: (i, 0))],
            out_specs=pl.BlockSpec((tr, n), lambda i, c_ref: (i, 0))),
        out_shape=jax.ShapeDtypeStruct((k, n), COMM_DTYPE),
        compiler_params=_cparams(("parallel",)),
    )(jnp.reshape(c, (1,)).astype(jnp.int32), grad, other)


def _scatter_shards(sums):
    names = list(BIG)

    def body(*refs):
        ins = refs[:len(names)]
        outs = refs[len(names):2 * len(names)]
        send_sems, recv_sems = refs[2 * len(names):]
        x, y, c = _my_place()
        me = 2 * x + y
        chips = _other_chips(x, y)
        cps = []
        for t, name in enumerate(names):
            for j, (cx, cy) in enumerate(chips):
                cp = pltpu.make_async_remote_copy(
                    src_ref=_window(ins[t], name, 2 * cx + cy), dst_ref=outs[t].at[me],
                    send_sem=send_sems.at[3 * t + j], recv_sem=recv_sems.at[3 * t + j],
                    device_id=(cx, cy, c), device_id_type=MESH)
                cp.start()
                cps.append(cp)
        for t, name in enumerate(names):
            for j, (cx, cy) in enumerate(chips):
                land = outs[t].at[2 * cx + cy]
                pltpu.make_async_remote_copy(
                    src_ref=land, dst_ref=land, send_sem=send_sems.at[3 * t + j], recv_sem=recv_sems.at[3 * t + j],
                    device_id=(cx, cy, c), device_id_type=MESH).wait_recv()
        for cp in cps:
            cp.wait_send()

    nsem = 3 * len(names)
    outs = pl.pallas_call(
        body, name="scatter_shards", in_specs=[HBM] * len(names), out_specs=[HBM] * len(names),
        out_shape=[jax.ShapeDtypeStruct((N_CHIPS,) + _shard_shape(n), sums[n].dtype) for n in names],
        scratch_shapes=[pltpu.SemaphoreType.DMA((nsem,)), pltpu.SemaphoreType.DMA((nsem,))],
    )(*[sums[n] for n in names])
    return dict(zip(names, outs))


def _sum_slots(slots, own, name):
    ks, ns = _shard_shape(name)
    tr = min(ks, 256)
    shape, index = _shard_block(name, tr)

    def body(me_ref, c_ref, s_ref, own_ref, out_ref):
        me = me_ref[0]
        acc = None
        for s in range(N_CHIPS):
            term = jnp.where(me == s, own_ref[...], s_ref[s]).astype(f32)
            acc = term if acc is None else acc + term
        out_ref[...] = acc

    return pl.pallas_call(
        body, name="sum_slots_" + name,
        grid_spec=pltpu.PrefetchScalarGridSpec(
            num_scalar_prefetch=2, grid=(ks // tr,),
            in_specs=[pl.BlockSpec((N_CHIPS, tr, ns), lambda i, me, c: (0, i, 0)),
                      pl.BlockSpec(shape, lambda i, me, c: index(i, me))],
            out_specs=pl.BlockSpec((None, tr, ns), lambda i, me, c: (c[0], i, 0))),
        out_shape=jax.ShapeDtypeStruct((2, ks, ns), f32),
        compiler_params=_cparams(("parallel",)),
    )(_chip_index(), jnp.reshape(lax.axis_index("c"), (1,)).astype(jnp.int32), slots, own)


def _pair_layers(mine):
    names = list(BIG)

    def body(*refs):
        ins = refs[:len(names)]
        outs = refs[len(names):2 * len(names)]
        send_sems, recv_sems = refs[2 * len(names):]
        x, y, c = _my_place()
        sibling = (x, y, 1 - c)
        cps = []
        for t in range(len(names)):
            cp = pltpu.make_async_remote_copy(
                src_ref=ins[t].at[c], dst_ref=outs[t].at[c], send_sem=send_sems.at[t], recv_sem=recv_sems.at[t],
                device_id=sibling, device_id_type=MESH)
            cp.start()
            cps.append(cp)
        for t in range(len(names)):
            cps[t].wait_send()
            land = outs[t].at[1 - c]
            pltpu.make_async_remote_copy(
                src_ref=land, dst_ref=land, send_sem=send_sems.at[t], recv_sem=recv_sems.at[t],
                device_id=sibling, device_id_type=MESH).wait_recv()

    outs = pl.pallas_call(
        body, name="pair_layers", in_specs=[HBM] * len(names), out_specs=[HBM] * len(names),
        out_shape=[jax.ShapeDtypeStruct((2,) + _shard_shape(n), f32) for n in names],
        input_output_aliases={t: t for t in range(len(names))},
        scratch_shapes=[pltpu.SemaphoreType.DMA((len(names),)), pltpu.SemaphoreType.DMA((len(names),))],
    )(*[mine[n] for n in names])
    return dict(zip(names, outs))


SMALL_ROWS = 320


def _allreduce_small(vec):
    n_dev = 8

    def body(v_ref, out_ref, buf_ref, send_sems, recv_sems):
        x, y, c = _my_place()
        me = 4 * x + 2 * y + c
        buf_ref[me] = v_ref[...]
        cps = []
        for k in range(1, n_dev):
            dx, dy, dc = (k >> 2) & 1, (k >> 1) & 1, k & 1
            peer = (x ^ dx, y ^ dy, c ^ dc)
            cp = pltpu.make_async_remote_copy(
                src_ref=v_ref, dst_ref=buf_ref.at[me], send_sem=send_sems.at[k - 1], recv_sem=recv_sems.at[k - 1],
                device_id=peer, device_id_type=MESH)
            cp.start()
            cps.append(cp)
        for k in range(1, n_dev):
            dx, dy, dc = (k >> 2) & 1, (k >> 1) & 1, k & 1
            src = 4 * (x ^ dx) + 2 * (y ^ dy) + (c ^ dc)
            land = buf_ref.at[src]
            pltpu.make_async_remote_copy(
                src_ref=land, dst_ref=land, send_sem=send_sems.at[k - 1], recv_sem=recv_sems.at[k - 1],
                device_id=(x ^ dx, y ^ dy, c ^ dc), device_id_type=MESH).wait_recv()
        for cp in cps:
            cp.wait_send()
        acc = buf_ref[0]
        for s in range(1, n_dev):
            acc = acc + buf_ref[s]
        out_ref[...] = acc

    return pl.pallas_call(
        body, name="allreduce_small",
        in_specs=[pl.BlockSpec(memory_space=pltpu.VMEM)], out_specs=pl.BlockSpec(memory_space=pltpu.VMEM),
        out_shape=jax.ShapeDtypeStruct((SMALL_ROWS, 128), f32),
        scratch_shapes=[pltpu.VMEM((n_dev, SMALL_ROWS, 128), f32), pltpu.SemaphoreType.DMA((n_dev - 1,)),
                        pltpu.SemaphoreType.DMA((n_dev - 1,))],
    )(vec)


def _adamw(w, g, m, v, name):
    rows, cols = w.shape
    tr = rows
    for cand in (512, 256, 128, 64, 32, 16, 8):
        if rows % cand == 0 and cand * cols * 4 <= 2 * 1024 * 1024:
            tr = cand
            break
    c1 = np.float32(1.0 - ADAM_B1 ** ADAM_STEP)
    c2 = np.float32(1.0 - ADAM_B2 ** ADAM_STEP)

    def body(w_ref, g_ref, m_ref, v_ref, d_ref, mo_ref, vo_ref):
        gv = g_ref[...]
        mn = ADAM_B1 * m_ref[...] + (1.0 - ADAM_B1) * gv
        vn = ADAM_B2 * v_ref[...] + (1.0 - ADAM_B2) * (gv * gv)
        mo_ref[...] = mn
        vo_ref[...] = vn
        d_ref[...] = -ADAM_LR * ((mn / c1) / (jnp.sqrt(vn / c2) + ADAM_EPS) + ADAM_WD * w_ref[...])

    blk = pl.BlockSpec((tr, cols), lambda i: (i, 0))
    return pl.pallas_call(
        body, name="adamw_" + name, grid=(rows // tr,), in_specs=[blk] * 4, out_specs=[blk] * 3,
        out_shape=[jax.ShapeDtypeStruct((rows, cols), f32)] * 3,
        compiler_params=_cparams(("parallel",)),
    )(w, g, m, v)


SMALL = ("norm1", "pool_w", "pool_scale", "norm2", "norm3", "final_norm")
ORDER = ("norm1", "w_in", "pool_w", "pool_scale", "w_out", "norm2", "w_up", "w_down", "norm3", "w_gate", "w_ple",
         "final_norm")


def _pack_small(tree, extra=None):
    parts = [tree[n].reshape(-1) for n in SMALL]
    if extra is not None:
        parts.append(extra.reshape(-1))
    flat = jnp.concatenate(parts)
    return jnp.pad(flat, (0, SMALL_ROWS * 128 - flat.shape[0])).reshape(SMALL_ROWS, 128)


def _unpack_small(packed, like):
    flat = packed.reshape(-1)
    out, off = {}, 0
    for n in SMALL:
        size = int(np.prod(like[n].shape))
        out[n] = flat[off:off + size].reshape(like[n].shape)
        off += size
    return out, flat[off]


def kernel(x, p, positions, norm1, w_in, pool_w, pool_scale, w_out, norm2, w_up, w_down, norm3, w_gate, w_ple, final_norm, loss_target, m_norm1, m_w_in, m_pool_w, m_pool_scale, m_w_out, m_norm2, m_w_up, m_w_down, m_norm3, m_w_gate, m_w_ple, m_final_norm, v_norm1, v_w_in, v_pool_w, v_pool_scale, v_w_out, v_norm2, v_w_up, v_w_down, v_norm3, v_w_gate, v_w_ple, v_final_norm):
    w = dict(norm1=norm1, w_in=w_in, pool_w=pool_w, pool_scale=pool_scale, w_out=w_out, norm2=norm2, w_up=w_up,
             w_down=w_down, norm3=norm3, w_gate=w_gate, w_ple=w_ple, final_norm=final_norm)
    m = dict(norm1=m_norm1, w_in=m_w_in, pool_w=m_pool_w, pool_scale=m_pool_scale, w_out=m_w_out, norm2=m_norm2,
             w_up=m_w_up, w_down=m_w_down, norm3=m_norm3, w_gate=m_w_gate, w_ple=m_w_ple, final_norm=m_final_norm)
    v = dict(norm1=v_norm1, w_in=v_w_in, pool_w=v_pool_w, pool_scale=v_pool_scale, w_out=v_w_out, norm2=v_norm2,
             w_up=v_w_up, w_down=v_w_down, norm3=v_norm3, w_gate=v_w_gate, w_ple=v_w_ple, final_norm=v_final_norm)
    small = {n: w[n] for n in SMALL}

    wsrc = _GatheredWeights({n: w[n] for n in BIG})
    loss8, dx, grads, small_grads = _local_step(x[0], p[:, 0], positions[0], wsrc, small, loss_target[0])

    other = _swap_layers(grads)
    sums = {n: _chip_sum(grads[n], other[n], n) for n in BIG}
    slots = _scatter_shards(sums)
    mine = {n: _sum_slots(slots[n], sums[n], n) for n in BIG}
    gsh = _pair_layers(mine)

    red = _allreduce_small(_pack_small(small_grads, loss8[0, 0]))
    g_small, loss = _unpack_small(red, small)

    g_out, d_out, m_out, v_out = {}, {}, {}, {}
    for n in BIG:
        shp = w[n].shape
        two = lambda a: a.reshape(shp[0] * shp[1], shp[2])
        d2, m2, v2 = _adamw(two(w[n]), two(gsh[n]), two(m[n]), two(v[n]), n)
        g_out[n], d_out[n], m_out[n], v_out[n] = gsh[n], d2.reshape(shp), m2.reshape(shp), v2.reshape(shp)
    d2, m2, v2 = _adamw(_pack_small(small), red, _pack_small({n: m[n] for n in SMALL}),
                        _pack_small({n: v[n] for n in SMALL}), "small")
    for tree, packed in ((d_out, d2), (m_out, m2), (v_out, v2)):
        tree.update(_unpack_small(packed, small)[0])
    g_out.update(g_small)

    return (loss, dx[None], *[g_out[n] for n in ORDER], *[d_out[n] for n in ORDER], *[m_out[n] for n in ORDER],
            *[v_out[n] for n in ORDER])
```

```python
import functools

import jax
import jax.numpy as jnp
import numpy as np
from jax import lax
from jax.experimental import pallas as pl
from jax.experimental.pallas import tpu as pltpu

f32 = jnp.float32
MXU_DTYPE = jnp.bfloat16
COMM_DTYPE = jnp.bfloat16

D_MODEL = 1024
POOL_WIDTH = 256
POOL_GC = 64
ATTN_WIDTH = 768
HEAD_DIM = 64
N_IN = POOL_WIDTH + 3 * ATTN_WIDTH
D_FF = 4096
PLE_DIM = 256
BLK = 128
DILATIONS = (1, 4, 16)
ROT_DIM = 16
ROPE_THETA = 500000.0
EPS = 1e-6
ATTN_SCALE = HEAD_DIM ** -0.5
NEG_BIG = -1e30

ADAM_LR, ADAM_B1, ADAM_B2, ADAM_EPS, ADAM_WD, ADAM_STEP = 0.001, 0.9, 0.999, 1e-08, 0.01, 10

TM = 512
HALO = 16
VMEM_LIMIT = 48 * 1024 * 1024
N_CHIPS = 4
MESH = pl.DeviceIdType.MESH

BIG = ("w_in", "w_out", "w_up", "w_down", "w_gate", "w_ple")
FULL_SHAPE = {"w_in": (D_MODEL, N_IN), "w_out": (D_MODEL, D_MODEL), "w_up": (D_MODEL, D_FF),
              "w_down": (D_FF, D_MODEL), "w_gate": (D_MODEL, D_MODEL), "w_ple": (PLE_DIM, D_MODEL)}
COL_SHARDED = {"w_in": True, "w_out": False, "w_up": True, "w_down": False, "w_gate": False, "w_ple": True}


def _shard_shape(name):
    k, n = FULL_SHAPE[name]
    return (k, n // N_CHIPS) if COL_SHARDED[name] else (k // N_CHIPS, n)


def _cparams(sem=None, vmem=VMEM_LIMIT):
    return pltpu.CompilerParams(dimension_semantics=sem, vmem_limit_bytes=vmem)


def _resident(block_shape, index_map):
    return pl.BlockSpec(block_shape, index_map, pipeline_mode=pl.Buffered(1))


def _mx(x):
    return x.astype(MXU_DTYPE)


def _dot(a, b):
    return jnp.dot(a, b, preferred_element_type=f32)


def _dot_nt(a, b):
    return lax.dot_general(a, b, (((1,), (1,)), ((), ())), preferred_element_type=f32)


def _dot_tn(a, b):
    return lax.dot_general(a, b, (((0,), (0,)), ((), ())), preferred_element_type=f32)


def _sigmoid(x):
    return 1.0 / (1.0 + jnp.exp(-x))


def _rope_apply(y, c, s1, s2, width):
    return y * c + pltpu.roll(y, width - 8, axis=1) * s1 + pltpu.roll(y, 8, axis=1) * s2


def _rope_transpose(dy, c, s1, s2, width):
    return dy * c + pltpu.roll(dy * s1, 8, axis=1) + pltpu.roll(dy * s2, width - 8, axis=1)


def _norm_matmul(h, g, w, layer, tn, name, rope=None):
    s_len, d = h.shape
    n = w.shape[2]

    def body(*refs):
        if rope is None:
            h_ref, g_ref, w_ref, y_ref, hn_ref = refs
        else:
            h_ref, g_ref, w_ref, c_ref, s1_ref, s2_ref, y_ref, hn_ref = refs
            reps = tn // 128
            c = jnp.concatenate([c_ref[...]] * reps, axis=1)
            s1 = jnp.concatenate([s1_ref[...]] * reps, axis=1)
            s2 = jnp.concatenate([s2_ref[...]] * reps, axis=1)
        x = h_ref[...]
        r = lax.rsqrt(jnp.mean(x * x, axis=-1, keepdims=True) + EPS)
        hn = ((x * r) * g_ref[...]).astype(hn_ref.dtype)
        hn_ref[...] = hn
        for j in range(n // tn):
            y = _dot(hn, w_ref[:, j * tn:(j + 1) * tn])
            if rope is not None and POOL_WIDTH <= j * tn < POOL_WIDTH + 2 * ATTN_WIDTH:
                y = _rope_apply(y, c, s1, s2, tn)
            y_ref[:, j * tn:(j + 1) * tn] = y

    in_specs = [pl.BlockSpec((TM, d), lambda i: (i, 0)),
                pl.BlockSpec((1, d), lambda i: (0, 0)),
                _resident((None, d, n), lambda i: (layer, 0, 0))]
    args = [h, g, w]
    if rope is not None:
        assert POOL_WIDTH % tn == 0 and (2 * ATTN_WIDTH) % tn == 0
        in_specs += [pl.BlockSpec((TM, 128), lambda i: (i, 0))] * 3
        args += list(rope)
    return pl.pallas_call(
        body, name=name, grid=(s_len // TM,), in_specs=in_specs,
        out_specs=[pl.BlockSpec((TM, n), lambda i: (i, 0)), pl.BlockSpec((TM, d), lambda i: (i, 0))],
        out_shape=[jax.ShapeDtypeStruct((s_len, n), f32), jax.ShapeDtypeStruct((s_len, d), MXU_DTYPE)],
        compiler_params=_cparams(("parallel",)),
    )(*args)


def _matmul_residual(a, w, layer, res, name, act=False, tk=1024):
    s_len, k_dim = a.shape
    n = w.shape[2]

    def body(a_ref, w_ref, res_ref, o_ref):
        acc = res_ref[...]
        for k in range(k_dim // tk):
            x = a_ref[:, k * tk:(k + 1) * tk]
            if act:
                r = jnp.maximum(x, 0.0)
                x = r * r
            acc = acc + _dot(_mx(x), w_ref[k * tk:(k + 1) * tk, :])
        o_ref[...] = acc

    return pl.pallas_call(
        body, name=name, grid=(s_len // TM,),
        in_specs=[pl.BlockSpec((TM, k_dim), lambda i: (i, 0)),
                  _resident((None, k_dim, n), lambda i: (layer, 0, 0)),
                  pl.BlockSpec((TM, n), lambda i: (i, 0))],
        out_specs=pl.BlockSpec((TM, n), lambda i: (i, 0)),
        out_shape=jax.ShapeDtypeStruct((s_len, n), f32),
        compiler_params=_cparams(("parallel",)),
    )(a, w, res)


def _gate_ple_fwd(h2, g, w_gate, w_ple, layer, p, name):
    s_len, d = h2.shape

    def body(h_ref, g_ref, wg_ref, p_ref, wp_ref, h3_ref, gl_ref, hn_ref):
        x = h_ref[...]
        r = lax.rsqrt(jnp.mean(x * x, axis=-1, keepdims=True) + EPS)
        hn = ((x * r) * g_ref[...]).astype(hn_ref.dtype)
        hn_ref[...] = hn
        gl = _dot(hn, wg_ref[...])
        gl_ref[...] = gl
        e = _dot(_mx(p_ref[...]), wp_ref[...])
        h3_ref[...] = x + _sigmoid(gl) * e

    row = lambda i: (i, 0)
    return pl.pallas_call(
        body, name=name, grid=(s_len // TM,),
        in_specs=[pl.BlockSpec((TM, d), row), pl.BlockSpec((1, d), lambda i: (0, 0)),
                  pl.BlockSpec((None, d, d), lambda i: (layer, 0, 0)), pl.BlockSpec((TM, PLE_DIM), row),
                  pl.BlockSpec((None, PLE_DIM, d), lambda i: (layer, 0, 0))],
        out_specs=[pl.BlockSpec((TM, d), row)] * 3,
        out_shape=[jax.ShapeDtypeStruct((s_len, d), f32), jax.ShapeDtypeStruct((s_len, d), f32),
                   jax.ShapeDtypeStruct((s_len, d), MXU_DTYPE)],
        compiler_params=_cparams(("parallel",)),
    )(h2, g, w_gate, p, w_ple)


def _gate_ple_bwd(dh3, gl, p, w_ple, layer, name):
    s_len, d = dh3.shape

    def body(dh_ref, gl_ref, p_ref, wp_ref, de_ref, dgl_ref):
        dh = dh_ref[...]
        gate = _sigmoid(gl_ref[...])
        e = _dot(_mx(p_ref[...]), wp_ref[...])
        de_ref[...] = (dh * gate).astype(de_ref.dtype)
        dgl_ref[...] = ((dh * e) * (gate * (1.0 - gate))).astype(dgl_ref.dtype)

    row = lambda i: (i, 0)
    return pl.pallas_call(
        body, name=name, grid=(s_len // TM,),
        in_specs=[pl.BlockSpec((TM, d), row), pl.BlockSpec((TM, d), row), pl.BlockSpec((TM, PLE_DIM), row),
                  pl.BlockSpec((None, PLE_DIM, d), lambda i: (layer, 0, 0))],
        out_specs=[pl.BlockSpec((TM, d), row)] * 2,
        out_shape=[jax.ShapeDtypeStruct((s_len, d), MXU_DTYPE)] * 2,
        compiler_params=_cparams(("parallel",)),
    )(dh3, gl, p, w_ple)


def _rmsnorm_bwd(dhn, x, g):
    r = lax.rsqrt(jnp.mean(x * x, axis=-1, keepdims=True) + EPS)
    xh = x * r
    dxh = dhn * g
    dx = r * (dxh - xh * jnp.mean(dxh * xh, axis=-1, keepdims=True))
    return dx, dhn * xh


def _matmul_nt_norm_bwd(dy, w, layer, h_prev, g, dres, name, tk=1024):
    s_len, k_dim = dy.shape
    d = h_prev.shape[1]

    def body(dy_ref, w_ref, h_ref, g_ref, dres_ref, dh_ref, dg_ref):
        i = pl.program_id(0)
        acc = None
        for k in range(k_dim // tk):
            part = _dot_nt(_mx(dy_ref[:, k * tk:(k + 1) * tk]), w_ref[:, k * tk:(k + 1) * tk])
            acc = part if acc is None else acc + part
        dx, dgrow = _rmsnorm_bwd(acc, h_ref[...], g_ref[...])
        dh_ref[...] = dres_ref[...] + dx
        dgsum = jnp.sum(dgrow, axis=0, keepdims=True)

        @pl.when(i == 0)
        def _():
            dg_ref[...] = dgsum

        @pl.when(i > 0)
        def _():
            dg_ref[...] += dgsum

    return pl.pallas_call(
        body, name=name, grid=(s_len // TM,),
        in_specs=[pl.BlockSpec((TM, k_dim), lambda i: (i, 0)),
                  _resident((None, d, k_dim), lambda i: (layer, 0, 0)),
                  pl.BlockSpec((TM, d), lambda i: (i, 0)),
                  pl.BlockSpec((1, d), lambda i: (0, 0)),
                  pl.BlockSpec((TM, d), lambda i: (i, 0))],
        out_specs=[pl.BlockSpec((TM, d), lambda i: (i, 0)), pl.BlockSpec((1, d), lambda i: (0, 0))],
        out_shape=[jax.ShapeDtypeStruct((s_len, d), f32), jax.ShapeDtypeStruct((1, d), f32)],
        compiler_params=_cparams(("arbitrary",)),
    )(dy, w, h_prev, g, dres)


def _down_bwd(dh2, w_down, layer, a, name, tf=1024):
    s_len, d = dh2.shape
    ff = a.shape[1]

    def body(dh_ref, w_ref, a_ref, act_ref, da_ref, dhb_ref):
        j = pl.program_id(1)

        @pl.when(j == 0)
        def _():
            dhb_ref[...] = _mx(dh_ref[...])

        dact = _dot_nt(dhb_ref[...], w_ref[pl.ds(pl.multiple_of(j * tf, tf), tf), :])
        r = jnp.maximum(a_ref[...], 0.0)
        act_ref[...] = (r * r).astype(act_ref.dtype)
        da_ref[...] = (dact * (2.0 * r)).astype(da_ref.dtype)

    return pl.pallas_call(
        body, name=name, grid=(s_len // TM, ff // tf),
        in_specs=[pl.BlockSpec((TM, d), lambda i, j: (i, 0)),
                  _resident((None, ff, d), lambda i, j: (layer, 0, 0)),
                  pl.BlockSpec((TM, tf), lambda i, j: (i, j))],
        out_specs=[pl.BlockSpec((TM, tf), lambda i, j: (i, j))] * 2,
        out_shape=[jax.ShapeDtypeStruct((s_len, ff), MXU_DTYPE)] * 2,
        scratch_shapes=[pltpu.VMEM((TM, d), MXU_DTYPE)],
        compiler_params=_cparams(("parallel", "arbitrary")),
    )(dh2, w_down, a)


def _matmul_nt(dy, w, layer, name):
    s_len, n = dy.shape
    k_dim = w.shape[1]

    def body(dy_ref, w_ref, o_ref):
        o_ref[...] = _dot_nt(_mx(dy_ref[...]), w_ref[...])

    return pl.pallas_call(
        body, name=name, grid=(s_len // TM,),
        in_specs=[pl.BlockSpec((TM, n), lambda i: (i, 0)), pl.BlockSpec((None, k_dim, n), lambda i: (layer, 0, 0))],
        out_specs=pl.BlockSpec((TM, k_dim), lambda i: (i, 0)),
        out_shape=jax.ShapeDtypeStruct((s_len, k_dim), f32),
        compiler_params=_cparams(("parallel",)),
    )(dy, w)


def _weight_grad(a, b, name):
    s_len, k_dim = a.shape
    n = b.shape[1]
    tka = min(k_dim, 2048)
    tnb = n if n <= 1024 else (2048 if n % 2048 == 0 else 640)
    ns = s_len // TM

    def body(a_ref, b_ref, o_ref, acc_ref):
        s = pl.program_id(2)
        part = _dot_tn(_mx(a_ref[...]), _mx(b_ref[...]))

        @pl.when(s == 0)
        def _():
            acc_ref[...] = part

        @pl.when(s > 0)
        def _():
            acc_ref[...] += part

        @pl.when(s == ns - 1)
        def _():
            o_ref[...] = acc_ref[...].astype(o_ref.dtype)

    return pl.pallas_call(
        body, name=name, grid=(k_dim // tka, n // tnb, ns),
        in_specs=[pl.BlockSpec((TM, tka), lambda i, j, s: (s, i)), pl.BlockSpec((TM, tnb), lambda i, j, s: (s, j))],
        out_specs=pl.BlockSpec((tka, tnb), lambda i, j, s: (i, j)),
        out_shape=jax.ShapeDtypeStruct((k_dim, n), COMM_DTYPE),
        scratch_shapes=[pltpu.VMEM((tka, tnb), f32)],
        compiler_params=_cparams(("parallel", "parallel", "arbitrary")),
    )(a, b)


def _group_select(lane, x2, x4, x8, x16):
    grp = lane // POOL_GC
    return jnp.where(grp == 0, x2, jnp.where(grp == 1, x4, jnp.where(grp == 2, x8, x16)))


def _pool_window(lane):
    grp = lane // POOL_GC
    return jnp.where(grp == 0, 2, jnp.where(grp == 1, 4, jnp.where(grp == 2, 8, 16)))


def _pool_y(u, halo, i):
    xs = jnp.concatenate([jnp.where(i > 0, halo, 0.0), u], axis=0)
    s2 = xs + pltpu.roll(xs, 1, axis=0)
    s4 = s2 + pltpu.roll(s2, 2, axis=0)
    s8 = s4 + pltpu.roll(s4, 4, axis=0)
    s16 = s8 + pltpu.roll(s8, 8, axis=0)
    lane = lax.broadcasted_iota(jnp.int32, xs.shape, 1)
    sel = _group_select(lane, s2, s4, s8, s16)[HALO:, :]
    t = i * TM + lax.broadcasted_iota(jnp.int32, u.shape, 0)
    cnt = jnp.minimum(_pool_window(lax.broadcasted_iota(jnp.int32, u.shape, 1)), t + 1).astype(f32)
    return sel / cnt - u


def _group_weights(l0, l1, l2):
    mx = jnp.maximum(jnp.maximum(l0, l1), l2)
    e0, e1, e2 = jnp.exp(l0 - mx), jnp.exp(l1 - mx), jnp.exp(l2 - mx)
    den = e0 + e1 + e2
    return e0 / den, e1 / den, e2 / den


def _mixer_merge(z, wbd, scale, outs, lses, name):
    s_len = z.shape[0]

    def body(u_ref, halo_ref, wbd_ref, sc_ref, o0, o1, o2, l0, l1, l2, m_ref):
        i = pl.program_id(0)
        y = _pool_y(u_ref[...], halo_ref[...], i)
        pool = _dot(_mx(y), wbd_ref[...]) * sc_ref[...]
        w0, w1, w2 = _group_weights(l0[...], l1[...], l2[...])
        m_ref[...] = jnp.concatenate([pool, o0[...] * w0, o1[...] * w1, o2[...] * w2], axis=1).astype(m_ref.dtype)

    row = lambda i: (i, 0)
    blk = pl.BlockSpec((TM, 256), row)
    grp = [pl.BlockSpec((TM, 256), lambda i, g=g: (i, g)) for g in range(3)]
    return pl.pallas_call(
        body, name=name, grid=(s_len // TM,),
        in_specs=[blk, pl.BlockSpec((HALO, 256), lambda i: (jnp.maximum(i * (TM // HALO) - 1, 0), 0)),
                  pl.BlockSpec((256, 256), lambda i: (0, 0)), pl.BlockSpec((1, 256), lambda i: (0, 0))] + grp + grp,
        out_specs=pl.BlockSpec((TM, D_MODEL), row),
        out_shape=jax.ShapeDtypeStruct((s_len, D_MODEL), MXU_DTYPE),
        compiler_params=_cparams(("parallel",)),
    )(z, z, wbd, scale, outs, outs, outs, lses, lses, lses)


def _head_sums(x):
    r = lax.broadcasted_iota(jnp.int32, (256, 256), 0) // HEAD_DIM
    c = lax.broadcasted_iota(jnp.int32, (256, 256), 1) // HEAD_DIM
    ones = jnp.where(r == c, 1.0, 0.0).astype(jnp.bfloat16)
    hi = x.astype(jnp.bfloat16)
    lo = (x - hi.astype(f32)).astype(jnp.bfloat16)
    return _dot(hi, ones) + _dot(lo, ones)


def _combine_bwd(dm, outs, lses, name):
    s_len = dm.shape[0]

    def body(d0, d1, d2, o0, o1, o2, l0, l1, l2, do_ref, dl_ref):
        w = _group_weights(l0[...], l1[...], l2[...])
        da = (d0[...], d1[...], d2[...])
        o = (o0[...], o1[...], o2[...])
        dw = [_head_sums(da[g] * o[g]) for g in range(3)]
        t = w[0] * dw[0] + w[1] * dw[1] + w[2] * dw[2]
        do_ref[...] = jnp.concatenate([da[g] * w[g] for g in range(3)], axis=1)
        dl_ref[...] = jnp.concatenate([w[g] * t for g in range(3)], axis=1)

    grp = [pl.BlockSpec((TM, 256), lambda i, g=g: (i, g)) for g in range(3)]
    return pl.pallas_call(
        body, name=name, grid=(s_len // TM,),
        in_specs=[pl.BlockSpec((TM, 256), lambda i: (i, 1)), pl.BlockSpec((TM, 256), lambda i: (i, 2)),
                  pl.BlockSpec((TM, 256), lambda i: (i, 3))] + grp + grp,
        out_specs=[pl.BlockSpec((TM, ATTN_WIDTH), lambda i: (i, 0))] * 2,
        out_shape=[jax.ShapeDtypeStruct((s_len, ATTN_WIDTH), f32)] * 2,
        compiler_params=_cparams(("parallel",)),
    )(dm, dm, dm, outs, outs, outs, lses, lses, lses)


def _pool_bwd(z, dm, wbd, scale, name):
    s_len = z.shape[0]
    n_halo = s_len // HALO

    def body(u_ref, uh_ref, d_ref, dh_ref, wbd_ref, sc_ref, du_ref, dw_ref, dsc_ref):
        i = pl.program_id(0)
        last = pl.num_programs(0) - 1
        y = _pool_y(u_ref[...], uh_ref[...], i)
        yb = _mx(y)
        dpo = d_ref[...]
        sc = sc_ref[...]
        dsc = jnp.sum(dpo * _dot(yb, wbd_ref[...]), axis=0, keepdims=True)
        dwp = _dot_tn(yb, _mx(dpo * sc))

        @pl.when(i == 0)
        def _():
            dsc_ref[...] = dsc
            dw_ref[...] = dwp

        @pl.when(i > 0)
        def _():
            dsc_ref[...] += dsc
            dw_ref[...] += dwp

        ext = jnp.concatenate([dpo, jnp.where(i < last, dh_ref[...], 0.0)], axis=0)
        dy = _dot_nt(_mx(ext * sc), wbd_ref[...])
        t = i * TM + lax.broadcasted_iota(jnp.int32, ext.shape, 0)
        lane = lax.broadcasted_iota(jnp.int32, ext.shape, 1)
        e = dy / jnp.minimum(_pool_window(lane), t + 1).astype(f32)
        rows = ext.shape[0]
        f2 = e + pltpu.roll(e, rows - 1, axis=0)
        f4 = f2 + pltpu.roll(f2, rows - 2, axis=0)
        f8 = f4 + pltpu.roll(f4, rows - 4, axis=0)
        f16 = f8 + pltpu.roll(f8, rows - 8, axis=0)
        du_ref[...] = (_group_select(lane, f2, f4, f8, f16) - dy)[:TM, :]

    row = lambda i: (i, 0)
    blk = pl.BlockSpec((TM, 256), row)
    return pl.pallas_call(
        body, name=name, grid=(s_len // TM,),
        in_specs=[blk, pl.BlockSpec((HALO, 256), lambda i: (jnp.maximum(i * (TM // HALO) - 1, 0), 0)),
                  blk, pl.BlockSpec((HALO, 256), lambda i: (jnp.minimum((i + 1) * (TM // HALO), n_halo - 1), 0)),
                  pl.BlockSpec((256, 256), lambda i: (0, 0)), pl.BlockSpec((1, 256), lambda i: (0, 0))],
        out_specs=[blk, pl.BlockSpec((256, 256), lambda i: (0, 0)), pl.BlockSpec((1, 256), lambda i: (0, 0))],
        out_shape=[jax.ShapeDtypeStruct((s_len, N_IN), f32), jax.ShapeDtypeStruct((256, 256), f32),
                   jax.ShapeDtypeStruct((1, 256), f32)],
        compiler_params=_cparams(("arbitrary",)),
    )(z, z, dm, dm, wbd, scale)


def _to_strided(x, dil):
    if dil == 1:
        return x
    s_len, c = x.shape
    return x.reshape(s_len // (BLK * dil), BLK, dil, c).transpose(0, 2, 1, 3).reshape(s_len, c)


def _from_strided(x, dil):
    if dil == 1:
        return x
    s_len, c = x.shape
    return x.reshape(s_len // (BLK * dil), dil, BLK, c).transpose(0, 2, 1, 3).reshape(s_len, c)


def _tri_masks():
    qi = lax.broadcasted_iota(jnp.int32, (BLK, BLK), 0)
    ki = lax.broadcasted_iota(jnp.int32, (BLK, BLK), 1)
    return qi >= ki, ki >= qi


ATTN_SUPER_PER_STEP = (8, 2, 1)
Q_COL, K_COL, V_COL = POOL_WIDTH // 128, (POOL_WIDTH + ATTN_WIDTH) // 128, (POOL_WIDTH + 2 * ATTN_WIDTH) // 128


def _rows(ref, start, dil):
    if dil == 1:
        return ref[pl.ds(start, BLK), :]
    return ref[pl.ds(start, BLK, stride=dil), :]


def _set_rows(ref, start, dil, val):
    if dil == 1:
        ref[pl.ds(start, BLK), :] = val
    else:
        ref[pl.ds(start, BLK, stride=dil), :] = val


def _attn_fwd(z, g, prev, name):
    s_len = z.shape[0]
    dil, m = DILATIONS[g], ATTN_SUPER_PER_STEP[g]
    sbr = BLK * dil
    rows = sbr * m

    def body(*refs):
        q_ref, kc_ref, kp_ref, vc_ref, vp_ref = refs[:5]
        o_ref, l_ref = refs[-2:]
        st = pl.program_id(0)
        low, up = _tri_masks()
        for sb in range(m):
            valid = jnp.concatenate([up & (st > 0) if sb == 0 else up, low], axis=1)
            for r in range(dil):
                base = sb * sbr + r
                q = _rows(q_ref, base, dil)
                kc, vc = _rows(kc_ref, base, dil), _rows(vc_ref, base, dil)
                if sb == 0:
                    kp, vp = _rows(kp_ref, r, dil), _rows(vp_ref, r, dil)
                else:
                    kp, vp = _rows(kc_ref, base - sbr, dil), _rows(vc_ref, base - sbr, dil)
                outs, lses = [], []
                for hh in range(2):
                    sl = slice(hh * HEAD_DIM, (hh + 1) * HEAD_DIM)
                    k2 = jnp.concatenate([_mx(kp[:, sl]), _mx(kc[:, sl])], axis=0)
                    v2 = jnp.concatenate([_mx(vp[:, sl]), _mx(vc[:, sl])], axis=0)
                    s = jnp.where(valid, _dot_nt(_mx(q[:, sl]), k2) * ATTN_SCALE, NEG_BIG)
                    mx = jnp.max(s, axis=-1, keepdims=True)
                    e = jnp.exp(s - mx)
                    l = jnp.sum(e, axis=-1, keepdims=True)
                    outs.append(_dot(_mx(e / l), v2))
                    lses.append(jnp.broadcast_to(mx + jnp.log(l), (BLK, HEAD_DIM)))
                _set_rows(o_ref, base, dil, jnp.concatenate(outs, axis=1))
                _set_rows(l_ref, base, dil, jnp.concatenate(lses, axis=1))

    def cur(col):
        return pl.BlockSpec((rows, 128), lambda st, hp: (st, col + 2 * g + hp))

    def before(col):
        return pl.BlockSpec((sbr, 128), lambda st, hp: (jnp.maximum(st * m - 1, 0), col + 2 * g + hp))

    in_specs = [cur(Q_COL), cur(K_COL), before(K_COL), cur(V_COL), before(V_COL)]
    args = [z, z, z, z, z]
    aliases = {}
    if prev is not None:
        in_specs += [pl.BlockSpec(memory_space=pl.ANY)] * 2
        args += list(prev)
        aliases = {5: 0, 6: 1}
    return pl.pallas_call(
        body, name=name, grid=(s_len // rows, 2), in_specs=in_specs, out_specs=[cur(0), cur(0)],
        out_shape=[jax.ShapeDtypeStruct((s_len, ATTN_WIDTH), f32)] * 2, input_output_aliases=aliases,
        compiler_params=_cparams(("parallel", "parallel")),
    )(*args)


def _attn_bwd(z, do, lse, dlt, tabs, dz, g, name):
    s_len = z.shape[0]
    dil, m = DILATIONS[g], ATTN_SUPER_PER_STEP[g]
    sbr = BLK * dil
    rows = sbr * m
    nsteps = s_len // rows

    def body(q_ref, qn_ref, kc_ref, kp_ref, vc_ref, vp_ref, do_ref, don_ref, l_ref, ln_ref, d_ref, dn_ref,
             c_ref, s1_ref, s2_ref, dz_in, dz_ref, dq_buf, dk_buf, dv_buf, sems):
        del dz_in
        st, hp = pl.program_id(0), pl.program_id(1)
        low, up = _tri_masks()
        for sb in range(m):
            up_prev = up & (st > 0) if sb == 0 else up
            up_next = up & (st < nsteps - 1) if sb == m - 1 else up
            for r in range(dil):
                base = sb * sbr + r
                q, k, v = _rows(q_ref, base, dil), _rows(kc_ref, base, dil), _rows(vc_ref, base, dil)
                do_c, l_c, d_c = _rows(do_ref, base, dil), _rows(l_ref, base, dil), _rows(d_ref, base, dil)
                if sb == 0:
                    kp, vp = _rows(kp_ref, r, dil), _rows(vp_ref, r, dil)
                else:
                    kp, vp = _rows(kc_ref, base - sbr, dil), _rows(vc_ref, base - sbr, dil)
                if sb == m - 1:
                    qn, do_n = _rows(qn_ref, r, dil), _rows(don_ref, r, dil)
                    l_n, d_n = _rows(ln_ref, r, dil), _rows(dn_ref, r, dil)
                else:
                    qn, do_n = _rows(q_ref, base + sbr, dil), _rows(do_ref, base + sbr, dil)
                    l_n, d_n = _rows(l_ref, base + sbr, dil), _rows(d_ref, base + sbr, dil)
                dqs, dks, dvs = [], [], []
                for hh in range(2):
                    sl = slice(hh * HEAD_DIM, (hh + 1) * HEAD_DIM)
                    one = slice(hh * HEAD_DIM, hh * HEAD_DIM + 1)
                    qc, qx = _mx(q[:, sl]), _mx(qn[:, sl])
                    kc, kb = _mx(k[:, sl]), _mx(kp[:, sl])
                    vc, vb = _mx(v[:, sl]), _mx(vp[:, sl])
                    doc, dox = _mx(do_c[:, sl]), _mx(do_n[:, sl])
                    lc, lx, dc, dx = l_c[:, one], l_n[:, one], d_c[:, one], d_n[:, one]
                    p_a = jnp.where(low, jnp.exp(_dot_nt(qc, kc) * ATTN_SCALE - lc), 0.0)
                    ds_a = _mx(p_a * (_dot_nt(doc, vc) - dc) * ATTN_SCALE)
                    p_b = jnp.where(up_prev, jnp.exp(_dot_nt(qc, kb) * ATTN_SCALE - lc), 0.0)
                    ds_b = _mx(p_b * (_dot_nt(doc, vb) - dc) * ATTN_SCALE)
                    p_c = jnp.where(up_next, jnp.exp(_dot_nt(qx, kc) * ATTN_SCALE - lx), 0.0)
                    ds_c = _mx(p_c * (_dot_nt(dox, vc) - dx) * ATTN_SCALE)
                    dqs.append(_dot(ds_a, kc) + _dot(ds_b, kb))
                    dks.append(_dot_tn(ds_a, qc) + _dot_tn(ds_c, qx))
                    dvs.append(_dot_tn(_mx(p_a), doc) + _dot_tn(_mx(p_c), dox))
                c, s1, s2 = _rows(c_ref, base, dil), _rows(s1_ref, base, dil), _rows(s2_ref, base, dil)
                _set_rows(dq_buf, base, dil, _rope_transpose(jnp.concatenate(dqs, axis=1), c, s1, s2, 128))
                _set_rows(dk_buf, base, dil, _rope_transpose(jnp.concatenate(dks, axis=1), c, s1, s2, 128))
                _set_rows(dv_buf, base, dil, jnp.concatenate(dvs, axis=1))
        copies = []
        for t, (buf, col) in enumerate(((dq_buf, Q_COL), (dk_buf, K_COL), (dv_buf, V_COL))):
            lane0 = pl.multiple_of((col + 2 * g + hp) * 128, 128)
            dst = dz_ref.at[pl.ds(pl.multiple_of(st * rows, rows), rows), pl.ds(lane0, 128)]
            cp = pltpu.make_async_copy(buf, dst, sems.at[t])
            cp.start()
            copies.append(cp)
        for cp in copies:
            cp.wait()

    def cur(col):
        return pl.BlockSpec((rows, 128), lambda st, hp: (st, col + 2 * g + hp))

    def before(col):
        return pl.BlockSpec((sbr, 128), lambda st, hp: (jnp.maximum(st * m - 1, 0), col + 2 * g + hp))

    def after(col):
        return pl.BlockSpec((sbr, 128), lambda st, hp: (jnp.minimum((st + 1) * m, s_len // sbr - 1), col + 2 * g + hp))

    tab = pl.BlockSpec((rows, 128), lambda st, hp: (st, 0))
    return pl.pallas_call(
        body, name=name, grid=(nsteps, 2),
        in_specs=[cur(Q_COL), after(Q_COL), cur(K_COL), before(K_COL), cur(V_COL), before(V_COL),
                  cur(0), after(0), cur(0), after(0), cur(0), after(0), tab, tab, tab,
                  pl.BlockSpec(memory_space=pl.ANY)],
        out_specs=pl.BlockSpec(memory_space=pl.ANY),
        out_shape=jax.ShapeDtypeStruct(dz.shape, dz.dtype), input_output_aliases={15: 0},
        scratch_shapes=[pltpu.VMEM((rows, 128), f32)] * 3 + [pltpu.SemaphoreType.DMA((3,))],
        compiler_params=_cparams(("arbitrary", "arbitrary")),
    )(z, z, z, z, z, z, do, do, lse, lse, dlt, dlt, *tabs, dz)


def _attn_fwd_old(q, k, v, dil, name):
    s_len = q.shape[0]
    nblk = s_len // BLK

    def body(q_ref, kc_ref, kp_ref, vc_ref, vp_ref, o_ref, l_ref):
        b = pl.program_id(0)
        has_prev = b >= dil
        low, up = _tri_masks()
        valid = jnp.concatenate([up & has_prev, low], axis=1)
        outs, lses = [], []
        for hh in range(2):
            sl = slice(hh * HEAD_DIM, (hh + 1) * HEAD_DIM)
            qh = _mx(q_ref[:, sl])
            k2 = jnp.concatenate([_mx(kp_ref[:, sl]), _mx(kc_ref[:, sl])], axis=0)
            v2 = jnp.concatenate([_mx(vp_ref[:, sl]), _mx(vc_ref[:, sl])], axis=0)
            s = jnp.where(valid, _dot_nt(qh, k2) * ATTN_SCALE, NEG_BIG)
            m = jnp.max(s, axis=-1, keepdims=True)
            e = jnp.exp(s - m)
            l = jnp.sum(e, axis=-1, keepdims=True)
            outs.append(_dot(_mx(e / l), v2))
            lses.append(jnp.broadcast_to(m + jnp.log(l), (BLK, HEAD_DIM)))
        o_ref[...] = jnp.concatenate(outs, axis=1)
        l_ref[...] = jnp.concatenate(lses, axis=1)

    cur = pl.BlockSpec((BLK, 128), lambda b, hp: (b, hp))
    prev = pl.BlockSpec((BLK, 128), lambda b, hp: (jnp.maximum(b - dil, 0), hp))
    return pl.pallas_call(
        body, name=name, grid=(nblk, 2), in_specs=[cur, cur, prev, cur, prev], out_specs=[cur, cur],
        out_shape=[jax.ShapeDtypeStruct((s_len, 256), f32)] * 2,
        compiler_params=_cparams(("parallel", "parallel")),
    )(q, k, k, v, v)


def _attn_bwd_old(q, k, v, do, lse, dlt, tabs, dil, name):
    s_len = q.shape[0]
    nblk = s_len // BLK

    def body(q_ref, qn_ref, kc_ref, kp_ref, vc_ref, vp_ref, do_ref, don_ref, l_ref, ln_ref, d_ref, dn_ref,
             c_ref, s1_ref, s2_ref, dq_ref, dk_ref, dv_ref):
        b = pl.program_id(0)
        has_prev = b >= dil
        has_next = b + dil < nblk
        low, up = _tri_masks()
        dqs, dks, dvs = [], [], []
        for hh in range(2):
            sl = slice(hh * HEAD_DIM, (hh + 1) * HEAD_DIM)
            one = slice(hh * HEAD_DIM, hh * HEAD_DIM + 1)
            qc, qn = _mx(q_ref[:, sl]), _mx(qn_ref[:, sl])
            kc, kp = _mx(kc_ref[:, sl]), _mx(kp_ref[:, sl])
            vc, vp = _mx(vc_ref[:, sl]), _mx(vp_ref[:, sl])
            doc, don = _mx(do_ref[:, sl]), _mx(don_ref[:, sl])
            lc, ln = l_ref[:, one], ln_ref[:, one]
            dc, dn = d_ref[:, one], dn_ref[:, one]
            p_a = jnp.where(low, jnp.exp(_dot_nt(qc, kc) * ATTN_SCALE - lc), 0.0)
            ds_a = _mx(p_a * (_dot_nt(doc, vc) - dc) * ATTN_SCALE)
            p_b = jnp.where(up & has_prev, jnp.exp(_dot_nt(qc, kp) * ATTN_SCALE - lc), 0.0)
            ds_b = _mx(p_b * (_dot_nt(doc, vp) - dc) * ATTN_SCALE)
            p_c = jnp.where(up & has_next, jnp.exp(_dot_nt(qn, kc) * ATTN_SCALE - ln), 0.0)
            ds_c = _mx(p_c * (_dot_nt(don, vc) - dn) * ATTN_SCALE)
            dqs.append(_dot(ds_a, kc) + _dot(ds_b, kp))
            dks.append(_dot_tn(ds_a, qc) + _dot_tn(ds_c, qn))
            dvs.append(_dot_tn(_mx(p_a), doc) + _dot_tn(_mx(p_c), don))
        c, s1, s2 = c_ref[...], s1_ref[...], s2_ref[...]
        dq_ref[...] = _rope_transpose(jnp.concatenate(dqs, axis=1), c, s1, s2, 128)
        dk_ref[...] = _rope_transpose(jnp.concatenate(dks, axis=1), c, s1, s2, 128)
        dv_ref[...] = jnp.concatenate(dvs, axis=1)

    cur = pl.BlockSpec((BLK, 128), lambda b, hp: (b, hp))
    prev = pl.BlockSpec((BLK, 128), lambda b, hp: (jnp.maximum(b - dil, 0), hp))
    nxt = pl.BlockSpec((BLK, 128), lambda b, hp: (jnp.minimum(b + dil, nblk - 1), hp))
    tab = pl.BlockSpec((BLK, 128), lambda b, hp: (b, 0))
    return pl.pallas_call(
        body, name=name, grid=(nblk, 2),
        in_specs=[cur, nxt, cur, prev, cur, prev, cur, nxt, cur, nxt, cur, nxt, tab, tab, tab],
        out_specs=[cur, cur, cur], out_shape=[jax.ShapeDtypeStruct((s_len, 256), f32)] * 3,
        compiler_params=_cparams(("parallel", "parallel")),
    )(q, q, k, k, v, v, do, do, lse, lse, dlt, dlt, *tabs)


def _loss_head(h, g, target, name):
    s_len, d = h.shape

    def body(h_ref, g_ref, t_ref, loss_ref, dh_ref, dg_ref):
        i = pl.program_id(0)
        x = h_ref[...]
        gv = g_ref[...]
        r = lax.rsqrt(jnp.mean(x * x, axis=-1, keepdims=True) + EPS)
        xh = x * r
        diff = xh * gv - t_ref[...]
        part = 0.5 * jnp.sum(jnp.mean(diff * diff, axis=-1, keepdims=True), axis=0, keepdims=True)
        dy = diff * (1.0 / d)
        dxh = dy * gv
        dh_ref[...] = r * (dxh - xh * jnp.mean(dxh * xh, axis=-1, keepdims=True))
        dgsum = jnp.sum(dy * xh, axis=0, keepdims=True)
        lossb = jnp.broadcast_to(part, (8, 128))

        @pl.when(i == 0)
        def _():
            loss_ref[...] = lossb
            dg_ref[...] = dgsum

        @pl.when(i > 0)
        def _():
            loss_ref[...] += lossb
            dg_ref[...] += dgsum

    row = lambda i: (i, 0)
    return pl.pallas_call(
        body, name=name, grid=(s_len // TM,),
        in_specs=[pl.BlockSpec((TM, d), row), pl.BlockSpec((1, d), lambda i: (0, 0)), pl.BlockSpec((TM, d), row)],
        out_specs=[pl.BlockSpec((8, 128), lambda i: (0, 0)), pl.BlockSpec((TM, d), row),
                   pl.BlockSpec((1, d), lambda i: (0, 0))],
        out_shape=[jax.ShapeDtypeStruct((8, 128), f32), jax.ShapeDtypeStruct((s_len, d), f32),
                   jax.ShapeDtypeStruct((1, d), f32)],
        compiler_params=_cparams(("arbitrary",)),
    )(h, g, target)


def _rope_tables(positions):
    inv_freq = ROPE_THETA ** (-jnp.arange(0, ROT_DIM, 2, dtype=f32) / ROT_DIM)
    ang = positions.astype(f32)[:, None] * inv_freq
    cos, sin = jnp.cos(ang), jnp.sin(ang)
    s_len = positions.shape[0]
    zero8, rest = jnp.zeros((s_len, 8), f32), jnp.zeros((s_len, HEAD_DIM - ROT_DIM), f32)
    c = jnp.concatenate([cos, cos, jnp.ones((s_len, HEAD_DIM - ROT_DIM), f32)], axis=1)
    s1 = jnp.concatenate([-sin, zero8, rest], axis=1)
    s2 = jnp.concatenate([zero8, sin, rest], axis=1)
    return c, s1, s2


def _block_diag(pool_w):
    out = jnp.zeros((POOL_WIDTH, POOL_WIDTH), pool_w.dtype)
    for g in range(4):
        out = lax.dynamic_update_slice(out, pool_w[g], (g * POOL_GC, g * POOL_GC))
    return out


class _ReadyWeights:
    def __init__(self, full):
        self.full = full

    def take(self, layer, names, after):
        del after
        return {n: self.full[n] for n in names}, layer


def _layer_fwd(h, p_l, wsrc, small, layer, tabs):
    nm = f"l{layer}_"
    wts, wl = wsrc.take(layer, ("w_in",), h if layer else None)
    z, hn1 = _norm_matmul(h, small["norm1"][layer][None], wts["w_in"], wl, 256, nm + "in_proj", rope=tabs)
    ol = None
    for g in range(3):
        ol = _attn_fwd(z, g, ol, nm + f"attn_fwd{g}")
    outs, lses = ol
    wbd = _mx(_block_diag(small["pool_w"][layer]))
    scale = small["pool_scale"][layer][None]
    m = _mixer_merge(z, wbd, scale, outs, lses, nm + "mixer_merge")
    rest, _ = wsrc.take(layer, ("w_out", "w_up", "w_down", "w_gate", "w_ple"), m)
    wts = {**wts, **rest}
    h1 = _matmul_residual(m, wts["w_out"], wl, h, nm + "out_proj")
    a, hn2 = _norm_matmul(h1, small["norm2"][layer][None], wts["w_up"], wl, 1024, nm + "up_proj")
    h2 = _matmul_residual(a, wts["w_down"], wl, h1, nm + "down_proj", act=True)
    h3, gl, hn3 = _gate_ple_fwd(h2, small["norm3"][layer][None], wts["w_gate"], wts["w_ple"], wl, p_l, nm + "gate_ple")
    saved = dict(h=h, z=z, hn1=hn1, outs=outs, lses=lses, wbd=wbd, scale=scale, m=m, h1=h1, a=a, hn2=hn2, h2=h2,
                 gl=gl, hn3=hn3, wts=wts, wl=wl)
    return h3, saved


def _layer_bwd(dh3, sv, p_l, small, layer, tabs128, reducer):
    nm = f"l{layer}_"
    wts, wl = sv["wts"], sv["wl"]
    de, dgl = _gate_ple_bwd(dh3, sv["gl"], p_l, wts["w_ple"], wl, nm + "gate_ple_bwd")
    reducer.add("w_gate", layer, _weight_grad(sv["hn3"], dgl, nm + "dw_gate"))
    reducer.add("w_ple", layer, _weight_grad(p_l, de, nm + "dw_ple"))
    dh2, dg3 = _matmul_nt_norm_bwd(dgl, wts["w_gate"], wl, sv["h2"], small["norm3"][layer][None], dh3, nm + "gate_bwd")
    act, da = _down_bwd(dh2, wts["w_down"], wl, sv["a"], nm + "down_bwd")
    reducer.add("w_down", layer, _weight_grad(act, dh2, nm + "dw_down"))
    reducer.add("w_up", layer, _weight_grad(sv["hn2"], da, nm + "dw_up"))
    dh1, dg2 = _matmul_nt_norm_bwd(da, wts["w_up"], wl, sv["h1"], small["norm2"][layer][None], dh2, nm + "up_bwd")
    dm = _matmul_nt(dh1, wts["w_out"], wl, nm + "out_bwd")
    reducer.add("w_out", layer, _weight_grad(sv["m"], dh1, nm + "dw_out"))
    do, dlt = _combine_bwd(dm, sv["outs"], sv["lses"], nm + "combine_bwd")
    dz, dwbd, dscale = _pool_bwd(sv["z"], dm, sv["wbd"], sv["scale"], nm + "pool_bwd")
    for g in range(3):
        dz = _attn_bwd(sv["z"], do, sv["lses"], dlt, tabs128, dz, g, nm + f"attn_bwd{g}")
    reducer.add("w_in", layer, _weight_grad(sv["hn1"], dz, nm + "dw_in"))
    dh0, dg1 = _matmul_nt_norm_bwd(dz, wts["w_in"], wl, sv["h"], small["norm1"][layer][None], dh1, nm + "in_bwd",
                                   tk=512)
    dpool_w = jnp.stack([dwbd[g * POOL_GC:(g + 1) * POOL_GC, g * POOL_GC:(g + 1) * POOL_GC] for g in range(4)])
    sg = dict(norm1=dg1[0], norm2=dg2[0], norm3=dg3[0], pool_w=dpool_w, pool_scale=dscale[0])
    return dh0, sg


class _CollectGrads:
    def __init__(self):
        self.grads = {}

    def add(self, name, layer, dw):
        self.grads[(name, layer)] = dw


def _local_step(x, p, positions, wsrc, small, target, reducer):
    tabs128 = tuple(jnp.tile(t, (1, 2)) for t in _rope_tables(positions))
    h = x
    saved = []
    for layer in range(2):
        h, sv = _layer_fwd(h, p[layer], wsrc, small, layer, tabs128)
        saved.append(sv)
    loss, dh, dgf = _loss_head(h, small["final_norm"][None], target, "loss_head")
    sgs = [None, None]
    for layer in (1, 0):
        dh, sgs[layer] = _layer_bwd(dh, saved[layer], p[layer], small, layer, tabs128, reducer)
    small_grads = {k: jnp.stack([sgs[0][k], sgs[1][k]]) for k in sgs[0]}
    small_grads["final_norm"] = dgf[0]
    return loss, dh, small_grads


HBM = pl.BlockSpec(memory_space=pltpu.HBM)


def _my_place():
    return lax.axis_index("x"), lax.axis_index("y"), lax.axis_index("c")


def _other_chips(x, y):
    return [(1 - x, y), (x, 1 - y), (1 - x, 1 - y)]


def _window(ref, name, chip):
    k, n = _shard_shape(name)
    if COL_SHARDED[name]:
        return ref.at[:, pl.ds(pl.multiple_of(chip * n, 128), n)]
    return ref.at[pl.ds(pl.multiple_of(chip * k, 128), k), :]


def _chip_index():
    return jnp.reshape(2 * lax.axis_index("x") + lax.axis_index("y"), (1,)).astype(jnp.int32)


def _shard_block(name, tr):
    ks, ns = _shard_shape(name)
    if COL_SHARDED[name]:
        return (tr, ns), lambda i, me: (i, me[0])
    return (tr, ns), lambda i, me: (me[0] * (ks // tr) + i, 0)


def _place_shard(w, name, layer):
    ks, ns = _shard_shape(name)
    tr = min(ks, 256)
    shape, index = _shard_block(name, tr)

    def body(me_ref, w_ref, o_ref):
        o_ref[...] = w_ref[...].astype(o_ref.dtype)

    return pl.pallas_call(
        body, name=f"place_{name}{layer}",
        grid_spec=pltpu.PrefetchScalarGridSpec(
            num_scalar_prefetch=1, grid=(ks // tr,),
            in_specs=[pl.BlockSpec((None, tr, ns), lambda i, me: (layer, i, 0))],
            out_specs=pl.BlockSpec((None,) + shape, lambda i, me: (0,) + index(i, me))),
        out_shape=jax.ShapeDtypeStruct((1,) + FULL_SHAPE[name], MXU_DTYPE),
        compiler_params=_cparams(("parallel",)),
    )(_chip_index(), w)


GATHER_ORDER = [("w_in", 0), ("w_out", 0), ("w_up", 0), ("w_down", 0), ("w_gate", 0), ("w_ple", 0),
                ("w_in", 1), ("w_out", 1), ("w_up", 1), ("w_down", 1), ("w_gate", 1), ("w_ple", 1)]
SEM = pl.BlockSpec(memory_space=pltpu.SEMAPHORE)
EFFECT = pltpu.SideEffectType.DATAFLOW_SIDE_EFFECTING


def _gather_copy(src_ref, dst_ref, name, idx, j, chip, send_sems, recv_sems, c):
    cx, cy = chip
    return pltpu.make_async_remote_copy(
        src_ref=src_ref, dst_ref=dst_ref, send_sem=send_sems.at[3 * idx + j], recv_sem=recv_sems.at[3 * idx + j],
        device_id=(cx, cy, c), device_id_type=MESH)


def _gather_start(placed):
    n = len(GATHER_ORDER)

    def body(*refs):
        ins = refs[:n]
        send_sems, recv_sems = refs[n], refs[n + 1]
        outs = refs[n + 2:]
        x, y, c = _my_place()
        me = 2 * x + y
        for idx, (name, _) in enumerate(GATHER_ORDER):
            for j, chip in enumerate(_other_chips(x, y)):
                _gather_copy(_window(ins[idx].at[0], name, me), _window(outs[idx].at[0], name, me), name, idx, j, chip,
                             send_sems, recv_sems, c).start()

    res = pl.pallas_call(
        body, name="gather_start",
        out_shape=(pltpu.SemaphoreType.DMA((3 * n,)), pltpu.SemaphoreType.DMA((3 * n,)))
        + tuple(pltpu.HBM(a.shape, a.dtype) for a in placed),
        in_specs=[HBM] * n, out_specs=(SEM, SEM) + (HBM,) * n,
        input_output_aliases={i: i + 2 for i in range(n)},
        compiler_params=pltpu.CompilerParams(has_side_effects=EFFECT),
    )(*[pltpu.with_memory_space_constraint(a, pltpu.HBM) for a in placed])
    return res[0], res[1], list(res[2:])


def _gather_wait(send_sems, recv_sems, arrays, idxs, after, name):
    n = len(idxs)

    def body(*refs):
        ins = refs[:n]
        send_ref, recv_ref = refs[n], refs[n + 1]
        x, y, c = _my_place()
        me = 2 * x + y
        for k, idx in enumerate(idxs):
            wname = GATHER_ORDER[idx][0]
            for j, chip in enumerate(_other_chips(x, y)):
                cx, cy = chip
                mine = _window(ins[k].at[0], wname, me)
                land = _window(ins[k].at[0], wname, 2 * cx + cy)
                _gather_copy(mine, mine, wname, idx, j, chip, send_ref, recv_ref, c).wait_send()
                _gather_copy(land, land, wname, idx, j, chip, send_ref, recv_ref, c).wait_recv()

    operands = list(arrays) + [send_sems, recv_sems]
    in_specs = [HBM] * n + [SEM, SEM]
    if after is not None:
        operands.append(after)
        in_specs.append(pl.BlockSpec(memory_space=pl.ANY))
    res = pl.pallas_call(
        body, name=name, out_shape=tuple(pltpu.HBM(a.shape, a.dtype) for a in arrays),
        in_specs=in_specs, out_specs=(HBM,) * n, input_output_aliases={i: i for i in range(n)},
        compiler_params=pltpu.CompilerParams(has_side_effects=EFFECT),
    )(*operands)
    return list(res)


class _GatheredWeights:
    def __init__(self, shards):
        placed = [_place_shard(shards[name], name, layer) for name, layer in GATHER_ORDER]
        self.send, self.recv, self.arrays = _gather_start(placed)
        self.ready = {}

    def take(self, layer, names, after):
        idxs = [GATHER_ORDER.index((n, layer)) for n in names]
        got = _gather_wait(self.send, self.recv, [self.arrays[i] for i in idxs], idxs, after,
                           f"gather_wait{layer}_{names[0]}")
        return dict(zip(names, got)), 0


def _gather_weights(full):
    names = list(BIG)

    def body(*refs):
        ins = refs[:len(names)]
        outs = refs[len(names):2 * len(names)]
        send_ici, recv_ici, send_d2d, recv_d2d = refs[2 * len(names):]
        x, y, c = _my_place()
        me = 2 * x + y
        sibling = (x, y, 1 - c)
        chips = _other_chips(x, y)
        ici = []
        for t, name in enumerate(names):
            for j, (cx, cy) in enumerate(chips):
                cp = pltpu.make_async_remote_copy(
                    src_ref=_window(ins[t].at[c], name, me), dst_ref=_window(outs[t].at[c], name, me),
                    send_sem=send_ici.at[3 * t + j], recv_sem=recv_ici.at[3 * t + j],
                    device_id=(cx, cy, c), device_id_type=MESH)
                cp.start()
                ici.append(cp)
        fwd = []
        for t, name in enumerate(names):
            for j, (cx, cy) in enumerate(chips):
                land = _window(outs[t].at[c], name, 2 * cx + cy)
                pltpu.make_async_remote_copy(
                    src_ref=land, dst_ref=land, send_sem=send_ici.at[3 * t + j], recv_sem=recv_ici.at[3 * t + j],
                    device_id=(cx, cy, c), device_id_type=MESH).wait_recv()
                cp = pltpu.make_async_remote_copy(
                    src_ref=land, dst_ref=land, send_sem=send_d2d.at[3 * t + j], recv_sem=recv_d2d.at[3 * t + j],
                    device_id=sibling, device_id_type=MESH)
                cp.start()
                fwd.append(cp)
        for t, name in enumerate(names):
            for j, (cx, cy) in enumerate(chips):
                land = _window(outs[t].at[1 - c], name, 2 * cx + cy)
                pltpu.make_async_remote_copy(
                    src_ref=land, dst_ref=land, send_sem=send_d2d.at[3 * t + j], recv_sem=recv_d2d.at[3 * t + j],
                    device_id=sibling, device_id_type=MESH).wait_recv()
        for cp in ici + fwd:
            cp.wait_send()

    nsem = 3 * len(names)
    outs = pl.pallas_call(
        body, name="gather_weights",
        in_specs=[HBM] * len(names), out_specs=[HBM] * len(names),
        out_shape=[jax.ShapeDtypeStruct(full[n].shape, full[n].dtype) for n in names],
        input_output_aliases={t: t for t in range(len(names))},
        scratch_shapes=[pltpu.SemaphoreType.DMA((nsem,)), pltpu.SemaphoreType.DMA((nsem,)),
                        pltpu.SemaphoreType.DMA((nsem,)), pltpu.SemaphoreType.DMA((nsem,))],
    )(*[full[n] for n in names])
    return dict(zip(names, outs))


def _swap_layers(grads):
    names = list(BIG)

    def body(*refs):
        ins = refs[:len(names)]
        outs = refs[len(names):2 * len(names)]
        send_sems, recv_sems = refs[2 * len(names):]
        x, y, c = _my_place()
        sibling = (x, y, 1 - c)
        cps = []
        for t in range(len(names)):
            cp = pltpu.make_async_remote_copy(
                src_ref=ins[t].at[1 - c], dst_ref=outs[t], send_sem=send_sems.at[t], recv_sem=recv_sems.at[t],
                device_id=sibling, device_id_type=MESH)
            cp.start()
            cps.append(cp)
        for cp in cps:
            cp.wait()

    outs = pl.pallas_call(
        body, name="swap_layers", in_specs=[HBM] * len(names), out_specs=[HBM] * len(names),
        out_shape=[jax.ShapeDtypeStruct(FULL_SHAPE[n], f32) for n in names],
        scratch_shapes=[pltpu.SemaphoreType.DMA((len(names),)), pltpu.SemaphoreType.DMA((len(names),))],
    )(*[grads[n] for n in names])
    return dict(zip(names, outs))


def _chip_sum(grad, other, name):
    k, n = FULL_SHAPE[name]
    tr = min(k, 512)
    c = lax.axis_index("c")

    def body(c_ref, g_ref, o_ref, out_ref):
        out_ref[...] = (g_ref[...] + o_ref[...]).astype(out_ref.dtype)

    return pl.pallas_call(
        body, name="chip_sum_" + name,
        grid_spec=pltpu.PrefetchScalarGridSpec(
            num_scalar_prefetch=1, grid=(k // tr,),
            in_specs=[pl.BlockSpec((None, tr, n), lambda i, c_ref: (c_ref[0], i, 0)),
                      pl.BlockSpec((tr, n), lambda i, c_ref: (i, 0))],
            out_specs=pl.BlockSpec((tr, n), lambda i, c_ref: (i, 0))),
        out_shape=jax.ShapeDtypeStruct((k, n), COMM_DTYPE),
        compiler_params=_cparams(("parallel",)),
    )(jnp.reshape(c, (1,)).astype(jnp.int32), grad, other)


def _scatter_shards(sums):
    names = list(BIG)

    def body(*refs):
        ins = refs[:len(names)]
        outs = refs[len(names):2 * len(names)]
        send_sems, recv_sems = refs[2 * len(names):]
        x, y, c = _my_place()
        me = 2 * x + y
        chips = _other_chips(x, y)
        cps = []
        for t, name in enumerate(names):
            for j, (cx, cy) in enumerate(chips):
                cp = pltpu.make_async_remote_copy(
                    src_ref=_window(ins[t], name, 2 * cx + cy), dst_ref=outs[t].at[me],
                    send_sem=send_sems.at[3 * t + j], recv_sem=recv_sems.at[3 * t + j],
                    device_id=(cx, cy, c), device_id_type=MESH)
                cp.start()
                cps.append(cp)
        for t, name in enumerate(names):
            for j, (cx, cy) in enumerate(chips):
                land = outs[t].at[2 * cx + cy]
                pltpu.make_async_remote_copy(
                    src_ref=land, dst_ref=land, send_sem=send_sems.at[3 * t + j], recv_sem=recv_sems.at[3 * t + j],
                    device_id=(cx, cy, c), device_id_type=MESH).wait_recv()
        for cp in cps:
            cp.wait_send()

    nsem = 3 * len(names)
    outs = pl.pallas_call(
        body, name="scatter_shards", in_specs=[HBM] * len(names), out_specs=[HBM] * len(names),
        out_shape=[jax.ShapeDtypeStruct((N_CHIPS,) + _shard_shape(n), sums[n].dtype) for n in names],
        scratch_shapes=[pltpu.SemaphoreType.DMA((nsem,)), pltpu.SemaphoreType.DMA((nsem,))],
    )(*[sums[n] for n in names])
    return dict(zip(names, outs))


def _sum_slots(slots, own, name):
    ks, ns = _shard_shape(name)
    tr = min(ks, 256)
    shape, index = _shard_block(name, tr)

    def body(me_ref, c_ref, s_ref, own_ref, out_ref):
        me = me_ref[0]
        acc = None
        for s in range(N_CHIPS):
            term = jnp.where(me == s, own_ref[...], s_ref[s]).astype(f32)
            acc = term if acc is None else acc + term
        out_ref[...] = acc

    return pl.pallas_call(
        body, name="sum_slots_" + name,
        grid_spec=pltpu.PrefetchScalarGridSpec(
            num_scalar_prefetch=2, grid=(ks // tr,),
            in_specs=[pl.BlockSpec((N_CHIPS, tr, ns), lambda i, me, c: (0, i, 0)),
                      pl.BlockSpec(shape, lambda i, me, c: index(i, me))],
            out_specs=pl.BlockSpec((None, tr, ns), lambda i, me, c: (c[0], i, 0))),
        out_shape=jax.ShapeDtypeStruct((2, ks, ns), f32),
        compiler_params=_cparams(("parallel",)),
    )(_chip_index(), jnp.reshape(lax.axis_index("c"), (1,)).astype(jnp.int32), slots, own)


N_DEV = 8


def _reduce_copies(dws, lands, names, layer, send_sems, recv_sems):
    x, y, c = _my_place()
    me, my_dev = 2 * x + y, 4 * x + 2 * y + c
    out = []
    for t, name in enumerate(names):
        for j, (cx, cy) in enumerate(_other_chips(x, y)):
            out.append((pltpu.make_async_remote_copy(
                src_ref=_window(dws[t], name, 2 * cx + cy), dst_ref=lands[t].at[my_dev],
                send_sem=send_sems.at[4 * t + j], recv_sem=recv_sems.at[N_DEV * t + my_dev],
                device_id=(cx, cy, layer), device_id_type=MESH), False))
        out.append((pltpu.make_async_remote_copy(
            src_ref=_window(dws[t], name, me), dst_ref=lands[t].at[my_dev],
            send_sem=send_sems.at[4 * t + 3], recv_sem=recv_sems.at[N_DEV * t + my_dev],
            device_id=(x, y, layer), device_id_type=MESH), True))
    return out


def _reduce_start(dws, names, layer, tag):
    n = len(names)
    lands = [lax.empty((N_DEV,) + _shard_shape(nm), dws[0].dtype) for nm in names]

    def body(*refs):
        ins = refs[:n]
        send_sems, recv_sems = refs[2 * n], refs[2 * n + 1]
        land_out = refs[3 * n + 2:]
        c = lax.axis_index("c")
        for cp, non_owner_only in _reduce_copies(ins, land_out, names, layer, send_sems, recv_sems):
            if non_owner_only:
                @pl.when(c != layer)
                def _():
                    cp.start()
            else:
                cp.start()

    res = pl.pallas_call(
        body, name="reduce_start" + tag,
        out_shape=(pltpu.SemaphoreType.DMA((4 * n,)), pltpu.SemaphoreType.DMA((N_DEV * n,)))
        + tuple(pltpu.HBM(a.shape, a.dtype) for a in dws) + tuple(pltpu.HBM(a.shape, a.dtype) for a in lands),
        in_specs=[HBM] * (2 * n), out_specs=(SEM, SEM) + (HBM,) * (2 * n),
        input_output_aliases={i: i + 2 for i in range(2 * n)},
        compiler_params=pltpu.CompilerParams(has_side_effects=EFFECT),
    )(*[pltpu.with_memory_space_constraint(a, pltpu.HBM) for a in list(dws) + lands])
    return res[0], res[1], list(res[2:2 + n]), list(res[2 + n:])


def _reduce_wait(send_sems, recv_sems, dws, lands, names, layer, after, tag):
    n = len(names)

    def body(*refs):
        ins, land_in = refs[:n], refs[n:2 * n]
        send_ref, recv_ref = refs[2 * n], refs[2 * n + 1]
        x, y, c = _my_place()
        for cp, non_owner_only in _reduce_copies(ins, land_in, names, layer, send_ref, recv_ref):
            if non_owner_only:
                @pl.when(c != layer)
                def _():
                    cp.wait_send()
            else:
                cp.wait_send()

        @pl.when(c == layer)
        def _():
            for t in range(n):
                for k in range(1, N_DEV):
                    px, py, pc = x ^ ((k >> 2) & 1), y ^ ((k >> 1) & 1), c ^ (k & 1)
                    dev = 4 * px + 2 * py + pc
                    land = land_in[t].at[dev]
                    pltpu.make_async_remote_copy(
                        src_ref=land, dst_ref=land, send_sem=send_ref.at[4 * t], recv_sem=recv_ref.at[N_DEV * t + dev],
                        device_id=(px, py, pc), device_id_type=MESH).wait_recv()

    res = pl.pallas_call(
        body, name="reduce_wait" + tag,
        out_shape=tuple(pltpu.HBM(a.shape, a.dtype) for a in list(dws) + list(lands)),
        in_specs=[HBM] * (2 * n) + [SEM, SEM, pl.BlockSpec(memory_space=pl.ANY)], out_specs=(HBM,) * (2 * n),
        input_output_aliases={i: i for i in range(2 * n)},
        compiler_params=pltpu.CompilerParams(has_side_effects=EFFECT),
    )(*dws, *lands, send_sems, recv_sems, after)
    return list(res[:n]), list(res[n:])


def _sum_devices(land, own, name, layer, prev):
    ks, ns = _shard_shape(name)
    tr = min(ks, 256)
    shape, index = _shard_block(name, tr)

    def body(me_ref, dev_ref, *refs):
        s_ref, own_ref, out_ref = refs[0], refs[1], refs[-1]
        dev = dev_ref[0]
        acc = None
        for s in range(N_DEV):
            term = jnp.where(dev == s, own_ref[...], s_ref[s]).astype(f32)
            acc = term if acc is None else acc + term
        out_ref[...] = acc

    in_specs = [pl.BlockSpec((N_DEV, tr, ns), lambda i, me, dev: (0, i, 0)),
                pl.BlockSpec(shape, lambda i, me, dev: index(i, me))]
    args = [land, own]
    aliases = {}
    if prev is not None:
        in_specs.append(pl.BlockSpec(memory_space=pl.ANY))
        args.append(prev)
        aliases = {4: 0}
    x, y, c = _my_place()
    return pl.pallas_call(
        body, name=f"sum_devices_{name}{layer}",
        grid_spec=pltpu.PrefetchScalarGridSpec(
            num_scalar_prefetch=2, grid=(ks // tr,), in_specs=in_specs,
            out_specs=pl.BlockSpec((None, tr, ns), lambda i, me, dev: (layer, i, 0))),
        out_shape=jax.ShapeDtypeStruct((2, ks, ns), f32), input_output_aliases=aliases,
        compiler_params=_cparams(("parallel",)),
    )(_chip_index(), jnp.reshape(4 * x + 2 * y + c, (1,)).astype(jnp.int32), *args)


class _GradReducer:
    GROUPS = (("1", 1, ("w_gate", "w_ple", "w_down", "w_up", "w_out", "w_in")),
              ("0a", 0, ("w_gate", "w_ple", "w_down", "w_up")),
              ("0b", 0, ("w_out", "w_in")))

    def __init__(self):
        self.grads = {}
        self.started = {}

    def add(self, name, layer, dw):
        self.grads[(name, layer)] = dw
        for tag, glayer, names in self.GROUPS:
            if tag not in self.started and all((nm, glayer) in self.grads for nm in names):
                self.started[tag] = _reduce_start([self.grads[(nm, glayer)] for nm in names], names, glayer, tag)

    def finish(self, after):
        mine = {}
        for tag, layer, names in self.GROUPS:
            send, recv, dws, lands = self.started[tag]
            dws, lands = _reduce_wait(send, recv, dws, lands, names, layer, after, tag)
            for nm, dw, land in zip(names, dws, lands):
                mine[nm] = _sum_devices(land, dw, nm, layer, mine.get(nm))
        return _pair_layers(mine)


def _pair_layers(mine):
    names = list(BIG)

    def body(*refs):
        ins = refs[:len(names)]
        outs = refs[len(names):2 * len(names)]
        send_sems, recv_sems = refs[2 * len(names):]
        x, y, c = _my_place()
        sibling = (x, y, 1 - c)
        cps = []
        for t in range(len(names)):
            cp = pltpu.make_async_remote_copy(
                src_ref=ins[t].at[c], dst_ref=outs[t].at[c], send_sem=send_sems.at[t], recv_sem=recv_sems.at[t],
                device_id=sibling, device_id_type=MESH)
            cp.start()
            cps.append(cp)
        for t in range(len(names)):
            cps[t].wait_send()
            land = outs[t].at[1 - c]
            pltpu.make_async_remote_copy(
                src_ref=land, dst_ref=land, send_sem=send_sems.at[t], recv_sem=recv_sems.at[t],
                device_id=sibling, device_id_type=MESH).wait_recv()

    outs = pl.pallas_call(
        body, name="pair_layers", in_specs=[HBM] * len(names), out_specs=[HBM] * len(names),
        out_shape=[jax.ShapeDtypeStruct((2,) + _shard_shape(n), f32) for n in names],
        input_output_aliases={t: t for t in range(len(names))},
        scratch_shapes=[pltpu.SemaphoreType.DMA((len(names),)), pltpu.SemaphoreType.DMA((len(names),))],
    )(*[mine[n] for n in names])
    return dict(zip(names, outs))


SMALL_ROWS = 320


def _allreduce_small(vec):
    n_dev = 8

    def body(v_ref, out_ref, buf_ref, send_sems, recv_sems):
        x, y, c = _my_place()
        me = 4 * x + 2 * y + c
        buf_ref[me] = v_ref[...]
        cps = []
        for k in range(1, n_dev):
            dx, dy, dc = (k >> 2) & 1, (k >> 1) & 1, k & 1
            peer = (x ^ dx, y ^ dy, c ^ dc)
            cp = pltpu.make_async_remote_copy(
                src_ref=v_ref, dst_ref=buf_ref.at[me], send_sem=send_sems.at[k - 1], recv_sem=recv_sems.at[k - 1],
                device_id=peer, device_id_type=MESH)
            cp.start()
            cps.append(cp)
        for k in range(1, n_dev):
            dx, dy, dc = (k >> 2) & 1, (k >> 1) & 1, k & 1
            src = 4 * (x ^ dx) + 2 * (y ^ dy) + (c ^ dc)
            land = buf_ref.at[src]
            pltpu.make_async_remote_copy(
                src_ref=land, dst_ref=land, send_sem=send_sems.at[k - 1], recv_sem=recv_sems.at[k - 1],
                device_id=(x ^ dx, y ^ dy, c ^ dc), device_id_type=MESH).wait_recv()
        for cp in cps:
            cp.wait_send()
        acc = buf_ref[0]
        for s in range(1, n_dev):
            acc = acc + buf_ref[s]
        out_ref[...] = acc

    return pl.pallas_call(
        body, name="allreduce_small",
        in_specs=[pl.BlockSpec(memory_space=pltpu.VMEM)], out_specs=pl.BlockSpec(memory_space=pltpu.VMEM),
        out_shape=jax.ShapeDtypeStruct((SMALL_ROWS, 128), f32),
        scratch_shapes=[pltpu.VMEM((n_dev, SMALL_ROWS, 128), f32), pltpu.SemaphoreType.DMA((n_dev - 1,)),
                        pltpu.SemaphoreType.DMA((n_dev - 1,))],
    )(vec)


def _adamw(w, g, m, v, name):
    rows, cols = w.shape
    tr = rows
    for cand in (512, 256, 128, 64, 32, 16, 8):
        if rows % cand == 0 and cand * cols * 4 <= 2 * 1024 * 1024:
            tr = cand
            break
    c1 = np.float32(1.0 - ADAM_B1 ** ADAM_STEP)
    c2 = np.float32(1.0 - ADAM_B2 ** ADAM_STEP)

    def body(w_ref, g_ref, m_ref, v_ref, d_ref, mo_ref, vo_ref):
        gv = g_ref[...]
        mn = ADAM_B1 * m_ref[...] + (1.0 - ADAM_B1) * gv
        vn = ADAM_B2 * v_ref[...] + (1.0 - ADAM_B2) * (gv * gv)
        mo_ref[...] = mn
        vo_ref[...] = vn
        d_ref[...] = -ADAM_LR * ((mn / c1) / (jnp.sqrt(vn / c2) + ADAM_EPS) + ADAM_WD * w_ref[...])

    blk = pl.BlockSpec((tr, cols), lambda i: (i, 0))
    return pl.pallas_call(
        body, name="adamw_" + name, grid=(rows // tr,), in_specs=[blk] * 4, out_specs=[blk] * 3,
        out_shape=[jax.ShapeDtypeStruct((rows, cols), f32)] * 3,
        compiler_params=_cparams(("parallel",)),
    )(w, g, m, v)


SMALL = ("norm1", "pool_w", "pool_scale", "norm2", "norm3", "final_norm")
ORDER = ("norm1", "w_in", "pool_w", "pool_scale", "w_out", "norm2", "w_up", "w_down", "norm3", "w_gate", "w_ple",
         "final_norm")


def _pack_small(tree, extra=None):
    parts = [tree[n].reshape(-1) for n in SMALL]
    if extra is not None:
        parts.append(extra.reshape(-1))
    flat = jnp.concatenate(parts)
    return jnp.pad(flat, (0, SMALL_ROWS * 128 - flat.shape[0])).reshape(SMALL_ROWS, 128)


def _unpack_small(packed, like):
    flat = packed.reshape(-1)
    out, off = {}, 0
    for n in SMALL:
        size = int(np.prod(like[n].shape))
        out[n] = flat[off:off + size].reshape(like[n].shape)
        off += size
    return out, flat[off]


def kernel(x, p, positions, norm1, w_in, pool_w, pool_scale, w_out, norm2, w_up, w_down, norm3, w_gate, w_ple, final_norm, loss_target, m_norm1, m_w_in, m_pool_w, m_pool_scale, m_w_out, m_norm2, m_w_up, m_w_down, m_norm3, m_w_gate, m_w_ple, m_final_norm, v_norm1, v_w_in, v_pool_w, v_pool_scale, v_w_out, v_norm2, v_w_up, v_w_down, v_norm3, v_w_gate, v_w_ple, v_final_norm):
    w = dict(norm1=norm1, w_in=w_in, pool_w=pool_w, pool_scale=pool_scale, w_out=w_out, norm2=norm2, w_up=w_up,
             w_down=w_down, norm3=norm3, w_gate=w_gate, w_ple=w_ple, final_norm=final_norm)
    m = dict(norm1=m_norm1, w_in=m_w_in, pool_w=m_pool_w, pool_scale=m_pool_scale, w_out=m_w_out, norm2=m_norm2,
             w_up=m_w_up, w_down=m_w_down, norm3=m_norm3, w_gate=m_w_gate, w_ple=m_w_ple, final_norm=m_final_norm)
    v = dict(norm1=v_norm1, w_in=v_w_in, pool_w=v_pool_w, pool_scale=v_pool_scale, w_out=v_w_out, norm2=v_norm2,
             w_up=v_w_up, w_down=v_w_down, norm3=v_norm3, w_gate=v_w_gate, w_ple=v_w_ple, final_norm=v_final_norm)
    small = {n: w[n] for n in SMALL}

    wsrc = _GatheredWeights({n: w[n] for n in BIG})
    reducer = _GradReducer()
    loss8, dx, small_grads = _local_step(x[0], p[:, 0], positions[0], wsrc, small, loss_target[0], reducer)
    gsh = reducer.finish(dx)

    red = _allreduce_small(_pack_small(small_grads, loss8[0, 0]))
    g_small, loss = _unpack_small(red, small)

    g_out, d_out, m_out, v_out = {}, {}, {}, {}
    for n in BIG:
        shp = w[n].shape
        two = lambda a: a.reshape(shp[0] * shp[1], shp[2])
        d2, m2, v2 = _adamw(two(w[n]), two(gsh[n]), two(m[n]), two(v[n]), n)
        g_out[n], d_out[n], m_out[n], v_out[n] = gsh[n], d2.reshape(shp), m2.reshape(shp), v2.reshape(shp)
    d2, m2, v2 = _adamw(_pack_small(small), red, _pack_small({n: m[n] for n in SMALL}),
                        _pack_small({n: v[n] for n in SMALL}), "small")
    for tree, packed in ((d_out, d2), (m_out, m2), (v_out, v2)):
        tree.update(_unpack_small(packed, small)[0])
    g_out.update(g_small)

    return (loss, dx[None], *[g_out[n] for n in ORDER], *[d_out[n] for n in ORDER], *[m_out[n] for n in ORDER],
            *[v_out[n] for n in ORDER])
```

```python
import functools

import jax
import jax.numpy as jnp
import numpy as np
from jax import lax
from jax.experimental import pallas as pl
from jax.experimental.pallas import tpu as pltpu

f32 = jnp.float32
MXU_DTYPE = jnp.bfloat16
COMM_DTYPE = jnp.bfloat16

D_MODEL = 1024
POOL_WIDTH = 256
POOL_GC = 64
ATTN_WIDTH = 768
HEAD_DIM = 64
N_IN = POOL_WIDTH + 3 * ATTN_WIDTH
D_FF = 4096
PLE_DIM = 256
BLK = 128
DILATIONS = (1, 4, 16)
ROT_DIM = 16
ROPE_THETA = 500000.0
EPS = 1e-6
ATTN_SCALE = HEAD_DIM ** -0.5
NEG_BIG = -1e30

ADAM_LR, ADAM_B1, ADAM_B2, ADAM_EPS, ADAM_WD, ADAM_STEP = 0.001, 0.9, 0.999, 1e-08, 0.01, 10

TM = 512
HALO = 16
VMEM_LIMIT = 48 * 1024 * 1024
N_CHIPS = 4
MESH = pl.DeviceIdType.MESH

BIG = ("w_in", "w_out", "w_up", "w_down", "w_gate", "w_ple")
FULL_SHAPE = {"w_in": (D_MODEL, N_IN), "w_out": (D_MODEL, D_MODEL), "w_up": (D_MODEL, D_FF),
              "w_down": (D_FF, D_MODEL), "w_gate": (D_MODEL, D_MODEL), "w_ple": (PLE_DIM, D_MODEL)}
COL_SHARDED = {"w_in": True, "w_out": False, "w_up": True, "w_down": False, "w_gate": False, "w_ple": True}


def _shard_shape(name):
    k, n = FULL_SHAPE[name]
    return (k, n // N_CHIPS) if COL_SHARDED[name] else (k // N_CHIPS, n)


def _cparams(sem=None, vmem=VMEM_LIMIT):
    return pltpu.CompilerParams(dimension_semantics=sem, vmem_limit_bytes=vmem)


def _resident(block_shape, index_map):
    return pl.BlockSpec(block_shape, index_map, pipeline_mode=pl.Buffered(1))


def _mx(x):
    return x.astype(MXU_DTYPE)


def _dot(a, b):
    return jnp.dot(a, b, preferred_element_type=f32)


def _dot_nt(a, b):
    return lax.dot_general(a, b, (((1,), (1,)), ((), ())), preferred_element_type=f32)


def _dot_tn(a, b):
    return lax.dot_general(a, b, (((0,), (0,)), ((), ())), preferred_element_type=f32)


def _sigmoid(x):
    return 1.0 / (1.0 + jnp.exp(-x))


def _rope_apply(y, c, s1, s2, width):
    return y * c + pltpu.roll(y, width - 8, axis=1) * s1 + pltpu.roll(y, 8, axis=1) * s2


def _rope_transpose(dy, c, s1, s2, width):
    return dy * c + pltpu.roll(dy * s1, 8, axis=1) + pltpu.roll(dy * s2, width - 8, axis=1)


def _norm_matmul(h, g, w, layer, tn, name, rope=None):
    s_len, d = h.shape
    n = w.shape[2]

    def body(*refs):
        if rope is None:
            h_ref, g_ref, w_ref, y_ref, hn_ref = refs
        else:
            h_ref, g_ref, w_ref, c_ref, s1_ref, s2_ref, y_ref, hn_ref = refs
            reps = tn // 128
            c = jnp.concatenate([c_ref[...]] * reps, axis=1)
            s1 = jnp.concatenate([s1_ref[...]] * reps, axis=1)
            s2 = jnp.concatenate([s2_ref[...]] * reps, axis=1)
        x = h_ref[...]
        r = lax.rsqrt(jnp.mean(x * x, axis=-1, keepdims=True) + EPS)
        hn = ((x * r) * g_ref[...]).astype(hn_ref.dtype)
        hn_ref[...] = hn
        for j in range(n // tn):
            y = _dot(hn, w_ref[:, j * tn:(j + 1) * tn])
            if rope is not None and POOL_WIDTH <= j * tn < POOL_WIDTH + 2 * ATTN_WIDTH:
                y = _rope_apply(y, c, s1, s2, tn)
            y_ref[:, j * tn:(j + 1) * tn] = y

    in_specs = [pl.BlockSpec((TM, d), lambda i: (i, 0)),
                pl.BlockSpec((1, d), lambda i: (0, 0)),
                _resident((None, d, n), lambda i: (layer, 0, 0))]
    args = [h, g, w]
    if rope is not None:
        assert POOL_WIDTH % tn == 0 and (2 * ATTN_WIDTH) % tn == 0
        in_specs += [pl.BlockSpec((TM, 128), lambda i: (i, 0))] * 3
        args += list(rope)
    return pl.pallas_call(
        body, name=name, grid=(s_len // TM,), in_specs=in_specs,
        out_specs=[pl.BlockSpec((TM, n), lambda i: (i, 0)), pl.BlockSpec((TM, d), lambda i: (i, 0))],
        out_shape=[jax.ShapeDtypeStruct((s_len, n), f32), jax.ShapeDtypeStruct((s_len, d), MXU_DTYPE)],
        compiler_params=_cparams(("parallel",)),
    )(*args)


def _matmul_residual(a, w, layer, res, name, act=False, tk=1024):
    s_len, k_dim = a.shape
    n = w.shape[2]

    def body(a_ref, w_ref, res_ref, o_ref):
        acc = res_ref[...]
        for k in range(k_dim // tk):
            x = a_ref[:, k * tk:(k + 1) * tk]
            if act:
                r = jnp.maximum(x, 0.0)
                x = r * r
            acc = acc + _dot(_mx(x), w_ref[k * tk:(k + 1) * tk, :])
        o_ref[...] = acc

    return pl.pallas_call(
        body, name=name, grid=(s_len // TM,),
        in_specs=[pl.BlockSpec((TM, k_dim), lambda i: (i, 0)),
                  _resident((None, k_dim, n), lambda i: (layer, 0, 0)),
                  pl.BlockSpec((TM, n), lambda i: (i, 0))],
        out_specs=pl.BlockSpec((TM, n), lambda i: (i, 0)),
        out_shape=jax.ShapeDtypeStruct((s_len, n), f32),
        compiler_params=_cparams(("parallel",)),
    )(a, w, res)


def _gate_ple_fwd(h2, g, w_gate, w_ple, layer, p, name):
    s_len, d = h2.shape

    def body(h_ref, g_ref, wg_ref, p_ref, wp_ref, h3_ref, gl_ref, hn_ref):
        x = h_ref[...]
        r = lax.rsqrt(jnp.mean(x * x, axis=-1, keepdims=True) + EPS)
        hn = ((x * r) * g_ref[...]).astype(hn_ref.dtype)
        hn_ref[...] = hn
        gl = _dot(hn, wg_ref[...])
        gl_ref[...] = gl
        e = _dot(_mx(p_ref[...]), wp_ref[...])
        h3_ref[...] = x + _sigmoid(gl) * e

    row = lambda i: (i, 0)
    return pl.pallas_call(
        body, name=name, grid=(s_len // TM,),
        in_specs=[pl.BlockSpec((TM, d), row), pl.BlockSpec((1, d), lambda i: (0, 0)),
                  pl.BlockSpec((None, d, d), lambda i: (layer, 0, 0)), pl.BlockSpec((TM, PLE_DIM), row),
                  pl.BlockSpec((None, PLE_DIM, d), lambda i: (layer, 0, 0))],
        out_specs=[pl.BlockSpec((TM, d), row)] * 3,
        out_shape=[jax.ShapeDtypeStruct((s_len, d), f32), jax.ShapeDtypeStruct((s_len, d), f32),
                   jax.ShapeDtypeStruct((s_len, d), MXU_DTYPE)],
        compiler_params=_cparams(("parallel",)),
    )(h2, g, w_gate, p, w_ple)


def _gate_ple_bwd(dh3, gl, p, w_ple, layer, name):
    s_len, d = dh3.shape

    def body(dh_ref, gl_ref, p_ref, wp_ref, de_ref, dgl_ref):
        dh = dh_ref[...]
        gate = _sigmoid(gl_ref[...])
        e = _dot(_mx(p_ref[...]), wp_ref[...])
        de_ref[...] = (dh * gate).astype(de_ref.dtype)
        dgl_ref[...] = ((dh * e) * (gate * (1.0 - gate))).astype(dgl_ref.dtype)

    row = lambda i: (i, 0)
    return pl.pallas_call(
        body, name=name, grid=(s_len // TM,),
        in_specs=[pl.BlockSpec((TM, d), row), pl.BlockSpec((TM, d), row), pl.BlockSpec((TM, PLE_DIM), row),
                  pl.BlockSpec((None, PLE_DIM, d), lambda i: (layer, 0, 0))],
        out_specs=[pl.BlockSpec((TM, d), row)] * 2,
        out_shape=[jax.ShapeDtypeStruct((s_len, d), MXU_DTYPE)] * 2,
        compiler_params=_cparams(("parallel",)),
    )(dh3, gl, p, w_ple)


def _rmsnorm_bwd(dhn, x, g):
    r = lax.rsqrt(jnp.mean(x * x, axis=-1, keepdims=True) + EPS)
    xh = x * r
    dxh = dhn * g
    dx = r * (dxh - xh * jnp.mean(dxh * xh, axis=-1, keepdims=True))
    return dx, dhn * xh


def _matmul_nt_norm_bwd(dy, w, layer, h_prev, g, dres, name, tk=1024, after=None):
    s_len, k_dim = dy.shape
    d = h_prev.shape[1]

    def body(dy_ref, w_ref, h_ref, g_ref, dres_ref, *rest):
        dh_ref, dg_ref = rest[-2:]
        i = pl.program_id(0)
        acc = None
        for k in range(k_dim // tk):
            part = _dot_nt(_mx(dy_ref[:, k * tk:(k + 1) * tk]), w_ref[:, k * tk:(k + 1) * tk])
            acc = part if acc is None else acc + part
        dx, dgrow = _rmsnorm_bwd(acc, h_ref[...], g_ref[...])
        dh_ref[...] = dres_ref[...] + dx
        dgsum = jnp.sum(dgrow, axis=0, keepdims=True)

        @pl.when(i == 0)
        def _():
            dg_ref[...] = dgsum

        @pl.when(i > 0)
        def _():
            dg_ref[...] += dgsum

    in_specs = [pl.BlockSpec((TM, k_dim), lambda i: (i, 0)),
                _resident((None, d, k_dim), lambda i: (layer, 0, 0)),
                pl.BlockSpec((TM, d), lambda i: (i, 0)),
                pl.BlockSpec((1, d), lambda i: (0, 0)),
                pl.BlockSpec((TM, d), lambda i: (i, 0))]
    args = [dy, w, h_prev, g, dres]
    if after is not None:
        in_specs.append(pl.BlockSpec(memory_space=pl.ANY))
        args.append(after)
    return pl.pallas_call(
        body, name=name, grid=(s_len // TM,), in_specs=in_specs,
        out_specs=[pl.BlockSpec((TM, d), lambda i: (i, 0)), pl.BlockSpec((1, d), lambda i: (0, 0))],
        out_shape=[jax.ShapeDtypeStruct((s_len, d), f32), jax.ShapeDtypeStruct((1, d), f32)],
        compiler_params=_cparams(("arbitrary",)),
    )(*args)


def _down_bwd(dh2, w_down, layer, a, name, tf=1024):
    s_len, d = dh2.shape
    ff = a.shape[1]

    def body(dh_ref, w_ref, a_ref, act_ref, da_ref, dhb_ref):
        j = pl.program_id(1)

        @pl.when(j == 0)
        def _():
            dhb_ref[...] = _mx(dh_ref[...])

        dact = _dot_nt(dhb_ref[...], w_ref[pl.ds(pl.multiple_of(j * tf, tf), tf), :])
        r = jnp.maximum(a_ref[...], 0.0)
        act_ref[...] = (r * r).astype(act_ref.dtype)
        da_ref[...] = (dact * (2.0 * r)).astype(da_ref.dtype)

    return pl.pallas_call(
        body, name=name, grid=(s_len // TM, ff // tf),
        in_specs=[pl.BlockSpec((TM, d), lambda i, j: (i, 0)),
                  _resident((None, ff, d), lambda i, j: (layer, 0, 0)),
                  pl.BlockSpec((TM, tf), lambda i, j: (i, j))],
        out_specs=[pl.BlockSpec((TM, tf), lambda i, j: (i, j))] * 2,
        out_shape=[jax.ShapeDtypeStruct((s_len, ff), MXU_DTYPE)] * 2,
        scratch_shapes=[pltpu.VMEM((TM, d), MXU_DTYPE)],
        compiler_params=_cparams(("parallel", "arbitrary")),
    )(dh2, w_down, a)


def _matmul_nt(dy, w, layer, name):
    s_len, n = dy.shape
    k_dim = w.shape[1]

    def body(dy_ref, w_ref, o_ref):
        o_ref[...] = _dot_nt(_mx(dy_ref[...]), w_ref[...])

    return pl.pallas_call(
        body, name=name, grid=(s_len // TM,),
        in_specs=[pl.BlockSpec((TM, n), lambda i: (i, 0)), pl.BlockSpec((None, k_dim, n), lambda i: (layer, 0, 0))],
        out_specs=pl.BlockSpec((TM, k_dim), lambda i: (i, 0)),
        out_shape=jax.ShapeDtypeStruct((s_len, k_dim), f32),
        compiler_params=_cparams(("parallel",)),
    )(dy, w)


def _weight_grad(a, b, name):
    s_len, k_dim = a.shape
    n = b.shape[1]
    tka = min(k_dim, 2048)
    tnb = n if n <= 1024 else (2048 if n % 2048 == 0 else 640)
    ns = s_len // TM

    def body(a_ref, b_ref, o_ref, acc_ref):
        s = pl.program_id(2)
        part = _dot_tn(_mx(a_ref[...]), _mx(b_ref[...]))

        @pl.when(s == 0)
        def _():
            acc_ref[...] = part

        @pl.when(s > 0)
        def _():
            acc_ref[...] += part

        @pl.when(s == ns - 1)
        def _():
            o_ref[...] = acc_ref[...].astype(o_ref.dtype)

    return pl.pallas_call(
        body, name=name, grid=(k_dim // tka, n // tnb, ns),
        in_specs=[pl.BlockSpec((TM, tka), lambda i, j, s: (s, i)), pl.BlockSpec((TM, tnb), lambda i, j, s: (s, j))],
        out_specs=pl.BlockSpec((tka, tnb), lambda i, j, s: (i, j)),
        out_shape=jax.ShapeDtypeStruct((k_dim, n), COMM_DTYPE),
        scratch_shapes=[pltpu.VMEM((tka, tnb), f32)],
        compiler_params=_cparams(("parallel", "parallel", "arbitrary")),
    )(a, b)


def _group_select(lane, x2, x4, x8, x16):
    grp = lane // POOL_GC
    return jnp.where(grp == 0, x2, jnp.where(grp == 1, x4, jnp.where(grp == 2, x8, x16)))


def _pool_window(lane):
    grp = lane // POOL_GC
    return jnp.where(grp == 0, 2, jnp.where(grp == 1, 4, jnp.where(grp == 2, 8, 16)))


def _pool_y(u, halo, i):
    xs = jnp.concatenate([jnp.where(i > 0, halo, 0.0), u], axis=0)
    s2 = xs + pltpu.roll(xs, 1, axis=0)
    s4 = s2 + pltpu.roll(s2, 2, axis=0)
    s8 = s4 + pltpu.roll(s4, 4, axis=0)
    s16 = s8 + pltpu.roll(s8, 8, axis=0)
    lane = lax.broadcasted_iota(jnp.int32, xs.shape, 1)
    sel = _group_select(lane, s2, s4, s8, s16)[HALO:, :]
    t = i * TM + lax.broadcasted_iota(jnp.int32, u.shape, 0)
    cnt = jnp.minimum(_pool_window(lax.broadcasted_iota(jnp.int32, u.shape, 1)), t + 1).astype(f32)
    return sel / cnt - u


def _group_weights(l0, l1, l2):
    mx = jnp.maximum(jnp.maximum(l0, l1), l2)
    e0, e1, e2 = jnp.exp(l0 - mx), jnp.exp(l1 - mx), jnp.exp(l2 - mx)
    den = e0 + e1 + e2
    return e0 / den, e1 / den, e2 / den


def _mixer_merge(z, wbd, scale, outs, lses, name):
    s_len = z.shape[0]

    def body(u_ref, halo_ref, wbd_ref, sc_ref, o0, o1, o2, l0, l1, l2, m_ref):
        i = pl.program_id(0)
        y = _pool_y(u_ref[...], halo_ref[...], i)
        pool = _dot(_mx(y), wbd_ref[...]) * sc_ref[...]
        w0, w1, w2 = _group_weights(l0[...], l1[...], l2[...])
        m_ref[...] = jnp.concatenate([pool, o0[...] * w0, o1[...] * w1, o2[...] * w2], axis=1).astype(m_ref.dtype)

    row = lambda i: (i, 0)
    blk = pl.BlockSpec((TM, 256), row)
    grp = [pl.BlockSpec((TM, 256), lambda i, g=g: (i, g)) for g in range(3)]
    return pl.pallas_call(
        body, name=name, grid=(s_len // TM,),
        in_specs=[blk, pl.BlockSpec((HALO, 256), lambda i: (jnp.maximum(i * (TM // HALO) - 1, 0), 0)),
                  pl.BlockSpec((256, 256), lambda i: (0, 0)), pl.BlockSpec((1, 256), lambda i: (0, 0))] + grp + grp,
        out_specs=pl.BlockSpec((TM, D_MODEL), row),
        out_shape=jax.ShapeDtypeStruct((s_len, D_MODEL), MXU_DTYPE),
        compiler_params=_cparams(("parallel",)),
    )(z, z, wbd, scale, outs, outs, outs, lses, lses, lses)


def _head_sums(x):
    r = lax.broadcasted_iota(jnp.int32, (256, 256), 0) // HEAD_DIM
    c = lax.broadcasted_iota(jnp.int32, (256, 256), 1) // HEAD_DIM
    ones = jnp.where(r == c, 1.0, 0.0).astype(jnp.bfloat16)
    hi = x.astype(jnp.bfloat16)
    lo = (x - hi.astype(f32)).astype(jnp.bfloat16)
    return _dot(hi, ones) + _dot(lo, ones)


def _combine_bwd(dm, outs, lses, name):
    s_len = dm.shape[0]

    def body(d0, d1, d2, o0, o1, o2, l0, l1, l2, do_ref, dl_ref):
        w = _group_weights(l0[...], l1[...], l2[...])
        da = (d0[...], d1[...], d2[...])
        o = (o0[...], o1[...], o2[...])
        dw = [_head_sums(da[g] * o[g]) for g in range(3)]
        t = w[0] * dw[0] + w[1] * dw[1] + w[2] * dw[2]
        do_ref[...] = jnp.concatenate([da[g] * w[g] for g in range(3)], axis=1)
        dl_ref[...] = jnp.concatenate([w[g] * t for g in range(3)], axis=1)

    grp = [pl.BlockSpec((TM, 256), lambda i, g=g: (i, g)) for g in range(3)]
    return pl.pallas_call(
        body, name=name, grid=(s_len // TM,),
        in_specs=[pl.BlockSpec((TM, 256), lambda i: (i, 1)), pl.BlockSpec((TM, 256), lambda i: (i, 2)),
                  pl.BlockSpec((TM, 256), lambda i: (i, 3))] + grp + grp,
        out_specs=[pl.BlockSpec((TM, ATTN_WIDTH), lambda i: (i, 0))] * 2,
        out_shape=[jax.ShapeDtypeStruct((s_len, ATTN_WIDTH), f32)] * 2,
        compiler_params=_cparams(("parallel",)),
    )(dm, dm, dm, outs, outs, outs, lses, lses, lses)


def _pool_bwd(z, dm, wbd, scale, name):
    s_len = z.shape[0]
    n_halo = s_len // HALO

    def body(u_ref, uh_ref, d_ref, dh_ref, wbd_ref, sc_ref, du_ref, dw_ref, dsc_ref):
        i = pl.program_id(0)
        last = pl.num_programs(0) - 1
        y = _pool_y(u_ref[...], uh_ref[...], i)
        yb = _mx(y)
        dpo = d_ref[...]
        sc = sc_ref[...]
        dsc = jnp.sum(dpo * _dot(yb, wbd_ref[...]), axis=0, keepdims=True)
        dwp = _dot_tn(yb, _mx(dpo * sc))

        @pl.when(i == 0)
        def _():
            dsc_ref[...] = dsc
            dw_ref[...] = dwp

        @pl.when(i > 0)
        def _():
            dsc_ref[...] += dsc
            dw_ref[...] += dwp

        ext = jnp.concatenate([dpo, jnp.where(i < last, dh_ref[...], 0.0)], axis=0)
        dy = _dot_nt(_mx(ext * sc), wbd_ref[...])
        t = i * TM + lax.broadcasted_iota(jnp.int32, ext.shape, 0)
        lane = lax.broadcasted_iota(jnp.int32, ext.shape, 1)
        e = dy / jnp.minimum(_pool_window(lane), t + 1).astype(f32)
        rows = ext.shape[0]
        f2 = e + pltpu.roll(e, rows - 1, axis=0)
        f4 = f2 + pltpu.roll(f2, rows - 2, axis=0)
        f8 = f4 + pltpu.roll(f4, rows - 4, axis=0)
        f16 = f8 + pltpu.roll(f8, rows - 8, axis=0)
        du_ref[...] = (_group_select(lane, f2, f4, f8, f16) - dy)[:TM, :]

    row = lambda i: (i, 0)
    blk = pl.BlockSpec((TM, 256), row)
    return pl.pallas_call(
        body, name=name, grid=(s_len // TM,),
        in_specs=[blk, pl.BlockSpec((HALO, 256), lambda i: (jnp.maximum(i * (TM // HALO) - 1, 0), 0)),
                  blk, pl.BlockSpec((HALO, 256), lambda i: (jnp.minimum((i + 1) * (TM // HALO), n_halo - 1), 0)),
                  pl.BlockSpec((256, 256), lambda i: (0, 0)), pl.BlockSpec((1, 256), lambda i: (0, 0))],
        out_specs=[blk, pl.BlockSpec((256, 256), lambda i: (0, 0)), pl.BlockSpec((1, 256), lambda i: (0, 0))],
        out_shape=[jax.ShapeDtypeStruct((s_len, N_IN), f32), jax.ShapeDtypeStruct((256, 256), f32),
                   jax.ShapeDtypeStruct((1, 256), f32)],
        compiler_params=_cparams(("arbitrary",)),
    )(z, z, dm, dm, wbd, scale)


def _to_strided(x, dil):
    if dil == 1:
        return x
    s_len, c = x.shape
    return x.reshape(s_len // (BLK * dil), BLK, dil, c).transpose(0, 2, 1, 3).reshape(s_len, c)


def _from_strided(x, dil):
    if dil == 1:
        return x
    s_len, c = x.shape
    return x.reshape(s_len // (BLK * dil), dil, BLK, c).transpose(0, 2, 1, 3).reshape(s_len, c)


def _tri_masks():
    qi = lax.broadcasted_iota(jnp.int32, (BLK, BLK), 0)
    ki = lax.broadcasted_iota(jnp.int32, (BLK, BLK), 1)
    return qi >= ki, ki >= qi


ATTN_SUPER_PER_STEP = (8, 2, 1)
Q_COL, K_COL, V_COL = POOL_WIDTH // 128, (POOL_WIDTH + ATTN_WIDTH) // 128, (POOL_WIDTH + 2 * ATTN_WIDTH) // 128


def _rows(ref, start, dil):
    if dil == 1:
        return ref[pl.ds(start, BLK), :]
    return ref[pl.ds(start, BLK, stride=dil), :]


def _set_rows(ref, start, dil, val):
    if dil == 1:
        ref[pl.ds(start, BLK), :] = val
    else:
        ref[pl.ds(start, BLK, stride=dil), :] = val


def _attn_fwd(z, g, prev, name):
    s_len = z.shape[0]
    dil, m = DILATIONS[g], ATTN_SUPER_PER_STEP[g]
    sbr = BLK * dil
    rows = sbr * m

    def body(*refs):
        q_ref, kc_ref, kp_ref, vc_ref, vp_ref = refs[:5]
        o_ref, l_ref = refs[-2:]
        st = pl.program_id(0)
        low, up = _tri_masks()
        for sb in range(m):
            valid = jnp.concatenate([up & (st > 0) if sb == 0 else up, low], axis=1)
            for r in range(dil):
                base = sb * sbr + r
                q = _rows(q_ref, base, dil)
                kc, vc = _rows(kc_ref, base, dil), _rows(vc_ref, base, dil)
                if sb == 0:
                    kp, vp = _rows(kp_ref, r, dil), _rows(vp_ref, r, dil)
                else:
                    kp, vp = _rows(kc_ref, base - sbr, dil), _rows(vc_ref, base - sbr, dil)
                outs, lses = [], []
                for hh in range(2):
                    sl = slice(hh * HEAD_DIM, (hh + 1) * HEAD_DIM)
                    k2 = jnp.concatenate([_mx(kp[:, sl]), _mx(kc[:, sl])], axis=0)
                    v2 = jnp.concatenate([_mx(vp[:, sl]), _mx(vc[:, sl])], axis=0)
                    s = jnp.where(valid, _dot_nt(_mx(q[:, sl]), k2) * ATTN_SCALE, NEG_BIG)
                    mx = jnp.max(s, axis=-1, keepdims=True)
                    e = jnp.exp(s - mx)
                    l = jnp.sum(e, axis=-1, keepdims=True)
                    outs.append(_dot(_mx(e / l), v2))
                    lses.append(jnp.broadcast_to(mx + jnp.log(l), (BLK, HEAD_DIM)))
                _set_rows(o_ref, base, dil, jnp.concatenate(outs, axis=1))
                _set_rows(l_ref, base, dil, jnp.concatenate(lses, axis=1))

    def cur(col):
        return pl.BlockSpec((rows, 128), lambda st, hp: (st, col + 2 * g + hp))

    def before(col):
        return pl.BlockSpec((sbr, 128), lambda st, hp: (jnp.maximum(st * m - 1, 0), col + 2 * g + hp))

    in_specs = [cur(Q_COL), cur(K_COL), before(K_COL), cur(V_COL), before(V_COL)]
    args = [z, z, z, z, z]
    aliases = {}
    if prev is not None:
        in_specs += [pl.BlockSpec(memory_space=pl.ANY)] * 2
        args += list(prev)
        aliases = {5: 0, 6: 1}
    return pl.pallas_call(
        body, name=name, grid=(s_len // rows, 2), in_specs=in_specs, out_specs=[cur(0), cur(0)],
        out_shape=[jax.ShapeDtypeStruct((s_len, ATTN_WIDTH), f32)] * 2, input_output_aliases=aliases,
        compiler_params=_cparams(("parallel", "parallel")),
    )(*args)


def _attn_bwd(z, do, lse, dlt, tabs, dz, g, name):
    s_len = z.shape[0]
    dil, m = DILATIONS[g], ATTN_SUPER_PER_STEP[g]
    sbr = BLK * dil
    rows = sbr * m
    nsteps = s_len // rows

    def body(q_ref, qn_ref, kc_ref, kp_ref, vc_ref, vp_ref, do_ref, don_ref, l_ref, ln_ref, d_ref, dn_ref,
             c_ref, s1_ref, s2_ref, dz_in, dz_ref, dq_buf, dk_buf, dv_buf, sems):
        del dz_in
        st, hp = pl.program_id(0), pl.program_id(1)
        low, up = _tri_masks()
        for sb in range(m):
            up_prev = up & (st > 0) if sb == 0 else up
            up_next = up & (st < nsteps - 1) if sb == m - 1 else up
            for r in range(dil):
                base = sb * sbr + r
                q, k, v = _rows(q_ref, base, dil), _rows(kc_ref, base, dil), _rows(vc_ref, base, dil)
                do_c, l_c, d_c = _rows(do_ref, base, dil), _rows(l_ref, base, dil), _rows(d_ref, base, dil)
                if sb == 0:
                    kp, vp = _rows(kp_ref, r, dil), _rows(vp_ref, r, dil)
                else:
                    kp, vp = _rows(kc_ref, base - sbr, dil), _rows(vc_ref, base - sbr, dil)
                if sb == m - 1:
                    qn, do_n = _rows(qn_ref, r, dil), _rows(don_ref, r, dil)
                    l_n, d_n = _rows(ln_ref, r, dil), _rows(dn_ref, r, dil)
                else:
                    qn, do_n = _rows(q_ref, base + sbr, dil), _rows(do_ref, base + sbr, dil)
                    l_n, d_n = _rows(l_ref, base + sbr, dil), _rows(d_ref, base + sbr, dil)
                dqs, dks, dvs = [], [], []
                for hh in range(2):
                    sl = slice(hh * HEAD_DIM, (hh + 1) * HEAD_DIM)
                    one = slice(hh * HEAD_DIM, hh * HEAD_DIM + 1)
                    qc, qx = _mx(q[:, sl]), _mx(qn[:, sl])
                    kc, kb = _mx(k[:, sl]), _mx(kp[:, sl])
                    vc, vb = _mx(v[:, sl]), _mx(vp[:, sl])
                    doc, dox = _mx(do_c[:, sl]), _mx(do_n[:, sl])
                    lc, lx, dc, dx = l_c[:, one], l_n[:, one], d_c[:, one], d_n[:, one]
                    p_a = jnp.where(low, jnp.exp(_dot_nt(qc, kc) * ATTN_SCALE - lc), 0.0)
                    ds_a = _mx(p_a * (_dot_nt(doc, vc) - dc) * ATTN_SCALE)
                    p_b = jnp.where(up_prev, jnp.exp(_dot_nt(qc, kb) * ATTN_SCALE - lc), 0.0)
                    ds_b = _mx(p_b * (_dot_nt(doc, vb) - dc) * ATTN_SCALE)
                    p_c = jnp.where(up_next, jnp.exp(_dot_nt(qx, kc) * ATTN_SCALE - lx), 0.0)
                    ds_c = _mx(p_c * (_dot_nt(dox, vc) - dx) * ATTN_SCALE)
                    dqs.append(_dot(ds_a, kc) + _dot(ds_b, kb))
                    dks.append(_dot_tn(ds_a, qc) + _dot_tn(ds_c, qx))
                    dvs.append(_dot_tn(_mx(p_a), doc) + _dot_tn(_mx(p_c), dox))
                c, s1, s2 = _rows(c_ref, base, dil), _rows(s1_ref, base, dil), _rows(s2_ref, base, dil)
                _set_rows(dq_buf, base, dil, _rope_transpose(jnp.concatenate(dqs, axis=1), c, s1, s2, 128))
                _set_rows(dk_buf, base, dil, _rope_transpose(jnp.concatenate(dks, axis=1), c, s1, s2, 128))
                _set_rows(dv_buf, base, dil, jnp.concatenate(dvs, axis=1))
        copies = []
        for t, (buf, col) in enumerate(((dq_buf, Q_COL), (dk_buf, K_COL), (dv_buf, V_COL))):
            lane0 = pl.multiple_of((col + 2 * g + hp) * 128, 128)
            dst = dz_ref.at[pl.ds(pl.multiple_of(st * rows, rows), rows), pl.ds(lane0, 128)]
            cp = pltpu.make_async_copy(buf, dst, sems.at[t])
            cp.start()
            copies.append(cp)
        for cp in copies:
            cp.wait()

    def cur(col):
        return pl.BlockSpec((rows, 128), lambda st, hp: (st, col + 2 * g + hp))

    def before(col):
        return pl.BlockSpec((sbr, 128), lambda st, hp: (jnp.maximum(st * m - 1, 0), col + 2 * g + hp))

    def after(col):
        return pl.BlockSpec((sbr, 128), lambda st, hp: (jnp.minimum((st + 1) * m, s_len // sbr - 1), col + 2 * g + hp))

    tab = pl.BlockSpec((rows, 128), lambda st, hp: (st, 0))
    return pl.pallas_call(
        body, name=name, grid=(nsteps, 2),
        in_specs=[cur(Q_COL), after(Q_COL), cur(K_COL), before(K_COL), cur(V_COL), before(V_COL),
                  cur(0), after(0), cur(0), after(0), cur(0), after(0), tab, tab, tab,
                  pl.BlockSpec(memory_space=pl.ANY)],
        out_specs=pl.BlockSpec(memory_space=pl.ANY),
        out_shape=jax.ShapeDtypeStruct(dz.shape, dz.dtype), input_output_aliases={15: 0},
        scratch_shapes=[pltpu.VMEM((rows, 128), f32)] * 3 + [pltpu.SemaphoreType.DMA((3,))],
        compiler_params=_cparams(("arbitrary", "arbitrary")),
    )(z, z, z, z, z, z, do, do, lse, lse, dlt, dlt, *tabs, dz)


def _attn_fwd_old(q, k, v, dil, name):
    s_len = q.shape[0]
    nblk = s_len // BLK

    def body(q_ref, kc_ref, kp_ref, vc_ref, vp_ref, o_ref, l_ref):
        b = pl.program_id(0)
        has_prev = b >= dil
        low, up = _tri_masks()
        valid = jnp.concatenate([up & has_prev, low], axis=1)
        outs, lses = [], []
        for hh in range(2):
            sl = slice(hh * HEAD_DIM, (hh + 1) * HEAD_DIM)
            qh = _mx(q_ref[:, sl])
            k2 = jnp.concatenate([_mx(kp_ref[:, sl]), _mx(kc_ref[:, sl])], axis=0)
            v2 = jnp.concatenate([_mx(vp_ref[:, sl]), _mx(vc_ref[:, sl])], axis=0)
            s = jnp.where(valid, _dot_nt(qh, k2) * ATTN_SCALE, NEG_BIG)
            m = jnp.max(s, axis=-1, keepdims=True)
            e = jnp.exp(s - m)
            l = jnp.sum(e, axis=-1, keepdims=True)
            outs.append(_dot(_mx(e / l), v2))
            lses.append(jnp.broadcast_to(m + jnp.log(l), (BLK, HEAD_DIM)))
        o_ref[...] = jnp.concatenate(outs, axis=1)
        l_ref[...] = jnp.concatenate(lses, axis=1)

    cur = pl.BlockSpec((BLK, 128), lambda b, hp: (b, hp))
    prev = pl.BlockSpec((BLK, 128), lambda b, hp: (jnp.maximum(b - dil, 0), hp))
    return pl.pallas_call(
        body, name=name, grid=(nblk, 2), in_specs=[cur, cur, prev, cur, prev], out_specs=[cur, cur],
        out_shape=[jax.ShapeDtypeStruct((s_len, 256), f32)] * 2,
        compiler_params=_cparams(("parallel", "parallel")),
    )(q, k, k, v, v)


def _attn_bwd_old(q, k, v, do, lse, dlt, tabs, dil, name):
    s_len = q.shape[0]
    nblk = s_len // BLK

    def body(q_ref, qn_ref, kc_ref, kp_ref, vc_ref, vp_ref, do_ref, don_ref, l_ref, ln_ref, d_ref, dn_ref,
             c_ref, s1_ref, s2_ref, dq_ref, dk_ref, dv_ref):
        b = pl.program_id(0)
        has_prev = b >= dil
        has_next = b + dil < nblk
        low, up = _tri_masks()
        dqs, dks, dvs = [], [], []
        for hh in range(2):
            sl = slice(hh * HEAD_DIM, (hh + 1) * HEAD_DIM)
            one = slice(hh * HEAD_DIM, hh * HEAD_DIM + 1)
            qc, qn = _mx(q_ref[:, sl]), _mx(qn_ref[:, sl])
            kc, kp = _mx(kc_ref[:, sl]), _mx(kp_ref[:, sl])
            vc, vp = _mx(vc_ref[:, sl]), _mx(vp_ref[:, sl])
            doc, don = _mx(do_ref[:, sl]), _mx(don_ref[:, sl])
            lc, ln = l_ref[:, one], ln_ref[:, one]
            dc, dn = d_ref[:, one], dn_ref[:, one]
            p_a = jnp.where(low, jnp.exp(_dot_nt(qc, kc) * ATTN_SCALE - lc), 0.0)
            ds_a = _mx(p_a * (_dot_nt(doc, vc) - dc) * ATTN_SCALE)
            p_b = jnp.where(up & has_prev, jnp.exp(_dot_nt(qc, kp) * ATTN_SCALE - lc), 0.0)
            ds_b = _mx(p_b * (_dot_nt(doc, vp) - dc) * ATTN_SCALE)
            p_c = jnp.where(up & has_next, jnp.exp(_dot_nt(qn, kc) * ATTN_SCALE - ln), 0.0)
            ds_c = _mx(p_c * (_dot_nt(don, vc) - dn) * ATTN_SCALE)
            dqs.append(_dot(ds_a, kc) + _dot(ds_b, kp))
            dks.append(_dot_tn(ds_a, qc) + _dot_tn(ds_c, qn))
            dvs.append(_dot_tn(_mx(p_a), doc) + _dot_tn(_mx(p_c), don))
        c, s1, s2 = c_ref[...], s1_ref[...], s2_ref[...]
        dq_ref[...] = _rope_transpose(jnp.concatenate(dqs, axis=1), c, s1, s2, 128)
        dk_ref[...] = _rope_transpose(jnp.concatenate(dks, axis=1), c, s1, s2, 128)
        dv_ref[...] = jnp.concatenate(dvs, axis=1)

    cur = pl.BlockSpec((BLK, 128), lambda b, hp: (b, hp))
    prev = pl.BlockSpec((BLK, 128), lambda b, hp: (jnp.maximum(b - dil, 0), hp))
    nxt = pl.BlockSpec((BLK, 128), lambda b, hp: (jnp.minimum(b + dil, nblk - 1), hp))
    tab = pl.BlockSpec((BLK, 128), lambda b, hp: (b, 0))
    return pl.pallas_call(
        body, name=name, grid=(nblk, 2),
        in_specs=[cur, nxt, cur, prev, cur, prev, cur, nxt, cur, nxt, cur, nxt, tab, tab, tab],
        out_specs=[cur, cur, cur], out_shape=[jax.ShapeDtypeStruct((s_len, 256), f32)] * 3,
        compiler_params=_cparams(("parallel", "parallel")),
    )(q, q, k, k, v, v, do, do, lse, lse, dlt, dlt, *tabs)


def _loss_head(h, g, target, name):
    s_len, d = h.shape

    def body(h_ref, g_ref, t_ref, loss_ref, dh_ref, dg_ref):
        i = pl.program_id(0)
        x = h_ref[...]
        gv = g_ref[...]
        r = lax.rsqrt(jnp.mean(x * x, axis=-1, keepdims=True) + EPS)
        xh = x * r
        diff = xh * gv - t_ref[...]
        part = 0.5 * jnp.sum(jnp.mean(diff * diff, axis=-1, keepdims=True), axis=0, keepdims=True)
        dy = diff * (1.0 / d)
        dxh = dy * gv
        dh_ref[...] = r * (dxh - xh * jnp.mean(dxh * xh, axis=-1, keepdims=True))
        dgsum = jnp.sum(dy * xh, axis=0, keepdims=True)
        lossb = jnp.broadcast_to(part, (8, 128))

        @pl.when(i == 0)
        def _():
            loss_ref[...] = lossb
            dg_ref[...] = dgsum

        @pl.when(i > 0)
        def _():
            loss_ref[...] += lossb
            dg_ref[...] += dgsum

    row = lambda i: (i, 0)
    return pl.pallas_call(
        body, name=name, grid=(s_len // TM,),
        in_specs=[pl.BlockSpec((TM, d), row), pl.BlockSpec((1, d), lambda i: (0, 0)), pl.BlockSpec((TM, d), row)],
        out_specs=[pl.BlockSpec((8, 128), lambda i: (0, 0)), pl.BlockSpec((TM, d), row),
                   pl.BlockSpec((1, d), lambda i: (0, 0))],
        out_shape=[jax.ShapeDtypeStruct((8, 128), f32), jax.ShapeDtypeStruct((s_len, d), f32),
                   jax.ShapeDtypeStruct((1, d), f32)],
        compiler_params=_cparams(("arbitrary",)),
    )(h, g, target)


def _rope_tables(positions):
    inv_freq = ROPE_THETA ** (-jnp.arange(0, ROT_DIM, 2, dtype=f32) / ROT_DIM)
    ang = positions.astype(f32)[:, None] * inv_freq
    cos, sin = jnp.cos(ang), jnp.sin(ang)
    s_len = positions.shape[0]
    zero8, rest = jnp.zeros((s_len, 8), f32), jnp.zeros((s_len, HEAD_DIM - ROT_DIM), f32)
    c = jnp.concatenate([cos, cos, jnp.ones((s_len, HEAD_DIM - ROT_DIM), f32)], axis=1)
    s1 = jnp.concatenate([-sin, zero8, rest], axis=1)
    s2 = jnp.concatenate([zero8, sin, rest], axis=1)
    return c, s1, s2


def _block_diag(pool_w):
    out = jnp.zeros((POOL_WIDTH, POOL_WIDTH), pool_w.dtype)
    for g in range(4):
        out = lax.dynamic_update_slice(out, pool_w[g], (g * POOL_GC, g * POOL_GC))
    return out


class _ReadyWeights:
    def __init__(self, full):
        self.full = full

    def take(self, layer, names, after):
        del after
        return {n: self.full[n] for n in names}, layer


def _layer_fwd(h, p_l, wsrc, small, layer, tabs):
    nm = f"l{layer}_"
    wts, wl = wsrc.take(layer, ("w_in",), h if layer else None)
    z, hn1 = _norm_matmul(h, small["norm1"][layer][None], wts["w_in"], wl, 256, nm + "in_proj", rope=tabs)
    ol = None
    for g in range(3):
        ol = _attn_fwd(z, g, ol, nm + f"attn_fwd{g}")
    outs, lses = ol
    wbd = _mx(_block_diag(small["pool_w"][layer]))
    scale = small["pool_scale"][layer][None]
    m = _mixer_merge(z, wbd, scale, outs, lses, nm + "mixer_merge")
    rest, _ = wsrc.take(layer, ("w_out", "w_up", "w_down", "w_gate", "w_ple"), m)
    wts = {**wts, **rest}
    h1 = _matmul_residual(m, wts["w_out"], wl, h, nm + "out_proj")
    a, hn2 = _norm_matmul(h1, small["norm2"][layer][None], wts["w_up"], wl, 1024, nm + "up_proj")
    h2 = _matmul_residual(a, wts["w_down"], wl, h1, nm + "down_proj", act=True)
    h3, gl, hn3 = _gate_ple_fwd(h2, small["norm3"][layer][None], wts["w_gate"], wts["w_ple"], wl, p_l, nm + "gate_ple")
    saved = dict(h=h, z=z, hn1=hn1, outs=outs, lses=lses, wbd=wbd, scale=scale, m=m, h1=h1, a=a, hn2=hn2, h2=h2,
                 gl=gl, hn3=hn3, wts=wts, wl=wl)
    return h3, saved


def _layer_bwd(dh3, sv, p_l, small, layer, tabs128, reducer):
    nm = f"l{layer}_"
    wts, wl = sv["wts"], sv["wl"]
    de, dgl = _gate_ple_bwd(dh3, sv["gl"], p_l, wts["w_ple"], wl, nm + "gate_ple_bwd")
    reducer.add("w_gate", layer, _weight_grad(sv["hn3"], dgl, nm + "dw_gate"))
    reducer.add("w_ple", layer, _weight_grad(p_l, de, nm + "dw_ple"))
    dh2, dg3 = _matmul_nt_norm_bwd(dgl, wts["w_gate"], wl, sv["h2"], small["norm3"][layer][None], dh3, nm + "gate_bwd")
    act, da = _down_bwd(dh2, wts["w_down"], wl, sv["a"], nm + "down_bwd")
    reducer.add("w_down", layer, _weight_grad(act, dh2, nm + "dw_down"))
    started = reducer.add("w_up", layer, _weight_grad(sv["hn2"], da, nm + "dw_up"))
    dh1, dg2 = _matmul_nt_norm_bwd(da, wts["w_up"], wl, sv["h1"], small["norm2"][layer][None], dh2, nm + "up_bwd",
                                   after=started)
    dm = _matmul_nt(dh1, wts["w_out"], wl, nm + "out_bwd")
    reducer.add("w_out", layer, _weight_grad(sv["m"], dh1, nm + "dw_out"))
    do, dlt = _combine_bwd(dm, sv["outs"], sv["lses"], nm + "combine_bwd")
    dz, dwbd, dscale = _pool_bwd(sv["z"], dm, sv["wbd"], sv["scale"], nm + "pool_bwd")
    for g in range(3):
        dz = _attn_bwd(sv["z"], do, sv["lses"], dlt, tabs128, dz, g, nm + f"attn_bwd{g}")
    started = reducer.add("w_in", layer, _weight_grad(sv["hn1"], dz, nm + "dw_in"))
    dh0, dg1 = _matmul_nt_norm_bwd(dz, wts["w_in"], wl, sv["h"], small["norm1"][layer][None], dh1, nm + "in_bwd",
                                   tk=512, after=started)
    dpool_w = jnp.stack([dwbd[g * POOL_GC:(g + 1) * POOL_GC, g * POOL_GC:(g + 1) * POOL_GC] for g in range(4)])
    sg = dict(norm1=dg1[0], norm2=dg2[0], norm3=dg3[0], pool_w=dpool_w, pool_scale=dscale[0])
    return dh0, sg


class _CollectGrads:
    def __init__(self):
        self.grads = {}

    def add(self, name, layer, dw):
        self.grads[(name, layer)] = dw


def _local_step(x, p, positions, wsrc, small, target, reducer):
    tabs128 = tuple(jnp.tile(t, (1, 2)) for t in _rope_tables(positions))
    h = x
    saved = []
    for layer in range(2):
        h, sv = _layer_fwd(h, p[layer], wsrc, small, layer, tabs128)
        saved.append(sv)
    loss, dh, dgf = _loss_head(h, small["final_norm"][None], target, "loss_head")
    sgs = [None, None]
    for layer in (1, 0):
        dh, sgs[layer] = _layer_bwd(dh, saved[layer], p[layer], small, layer, tabs128, reducer)
    small_grads = {k: jnp.stack([sgs[0][k], sgs[1][k]]) for k in sgs[0]}
    small_grads["final_norm"] = dgf[0]
    return loss, dh, small_grads


HBM = pl.BlockSpec(memory_space=pltpu.HBM)


def _my_place():
    return lax.axis_index("x"), lax.axis_index("y"), lax.axis_index("c")


def _other_chips(x, y):
    return [(1 - x, y), (x, 1 - y), (1 - x, 1 - y)]


def _window(ref, name, chip):
    k, n = _shard_shape(name)
    if COL_SHARDED[name]:
        return ref.at[:, pl.ds(pl.multiple_of(chip * n, 128), n)]
    return ref.at[pl.ds(pl.multiple_of(chip * k, 128), k), :]


def _chip_index():
    return jnp.reshape(2 * lax.axis_index("x") + lax.axis_index("y"), (1,)).astype(jnp.int32)


def _shard_block(name, tr):
    ks, ns = _shard_shape(name)
    if COL_SHARDED[name]:
        return (tr, ns), lambda i, me: (i, me[0])
    return (tr, ns), lambda i, me: (me[0] * (ks // tr) + i, 0)


def _place_shard(w, name, layer):
    ks, ns = _shard_shape(name)
    tr = min(ks, 256)
    shape, index = _shard_block(name, tr)

    def body(me_ref, w_ref, o_ref):
        o_ref[...] = w_ref[...].astype(o_ref.dtype)

    return pl.pallas_call(
        body, name=f"place_{name}{layer}",
        grid_spec=pltpu.PrefetchScalarGridSpec(
            num_scalar_prefetch=1, grid=(ks // tr,),
            in_specs=[pl.BlockSpec((None, tr, ns), lambda i, me: (layer, i, 0))],
            out_specs=pl.BlockSpec((None,) + shape, lambda i, me: (0,) + index(i, me))),
        out_shape=jax.ShapeDtypeStruct((1,) + FULL_SHAPE[name], MXU_DTYPE),
        compiler_params=_cparams(("parallel",)),
    )(_chip_index(), w)


GATHER_ORDER = [("w_in", 0), ("w_out", 0), ("w_up", 0), ("w_down", 0), ("w_gate", 0), ("w_ple", 0),
                ("w_in", 1), ("w_out", 1), ("w_up", 1), ("w_down", 1), ("w_gate", 1), ("w_ple", 1)]
SEM = pl.BlockSpec(memory_space=pltpu.SEMAPHORE)
EFFECT = pltpu.SideEffectType.DATAFLOW_SIDE_EFFECTING


def _gather_copy(src_ref, dst_ref, name, idx, j, chip, send_sems, recv_sems, c):
    cx, cy = chip
    return pltpu.make_async_remote_copy(
        src_ref=src_ref, dst_ref=dst_ref, send_sem=send_sems.at[3 * idx + j], recv_sem=recv_sems.at[3 * idx + j],
        device_id=(cx, cy, c), device_id_type=MESH)


def _gather_start(placed, order, tag):
    n = len(order)

    def body(*refs):
        ins = refs[:n]
        send_sems, recv_sems = refs[n], refs[n + 1]
        outs = refs[n + 2:2 * n + 2]
        token = refs[-1]
        x, y, c = _my_place()
        me = 2 * x + y
        for idx, (name, _) in enumerate(order):
            for j, chip in enumerate(_other_chips(x, y)):
                _gather_copy(_window(ins[idx].at[0], name, me), _window(outs[idx].at[0], name, me), name, idx, j, chip,
                             send_sems, recv_sems, c).start()
        token[...] = jnp.zeros_like(token)

    res = pl.pallas_call(
        body, name="gather_start" + tag,
        out_shape=(pltpu.SemaphoreType.DMA((3 * n,)), pltpu.SemaphoreType.DMA((3 * n,)))
        + tuple(pltpu.HBM(a.shape, a.dtype) for a in placed) + (jax.ShapeDtypeStruct((8, 128), f32),),
        in_specs=[HBM] * n, out_specs=(SEM, SEM) + (HBM,) * n + (pl.BlockSpec(memory_space=pltpu.VMEM),),
        input_output_aliases={i: i + 2 for i in range(n)},
        compiler_params=pltpu.CompilerParams(has_side_effects=EFFECT),
    )(*[pltpu.with_memory_space_constraint(a, pltpu.HBM) for a in placed])
    return res[0], res[1], list(res[2:2 + n]), res[-1]


def _gather_wait(send_sems, recv_sems, arrays, order, idxs, after, name):
    n = len(idxs)

    def body(*refs):
        ins = refs[:n]
        send_ref, recv_ref = refs[n], refs[n + 1]
        x, y, c = _my_place()
        me = 2 * x + y
        for k, idx in enumerate(idxs):
            wname = order[idx][0]
            for j, chip in enumerate(_other_chips(x, y)):
                cx, cy = chip
                mine = _window(ins[k].at[0], wname, me)
                land = _window(ins[k].at[0], wname, 2 * cx + cy)
                _gather_copy(mine, mine, wname, idx, j, chip, send_ref, recv_ref, c).wait_send()
                _gather_copy(land, land, wname, idx, j, chip, send_ref, recv_ref, c).wait_recv()

    operands = list(arrays) + [send_sems, recv_sems]
    in_specs = [HBM] * n + [SEM, SEM]
    if after is not None:
        operands.append(after)
        in_specs.append(pl.BlockSpec(memory_space=pl.ANY))
    res = pl.pallas_call(
        body, name=name, out_shape=tuple(pltpu.HBM(a.shape, a.dtype) for a in arrays),
        in_specs=in_specs, out_specs=(HBM,) * n, input_output_aliases={i: i for i in range(n)},
        compiler_params=pltpu.CompilerParams(has_side_effects=EFFECT),
    )(*operands)
    return list(res)


class _GatheredWeights:
    def __init__(self, shards):
        self.starts = []
        for tag, order in (("_first", GATHER_ORDER[:1]), ("_rest", GATHER_ORDER[1:])):
            placed = [_place_shard(shards[name], name, layer) for name, layer in order]
            self.starts.append((order,) + _gather_start(placed, order, tag))

    def take(self, layer, names, after):
        order, send, recv, arrays, _ = next(s for s in self.starts if (names[0], layer) in s[0])
        if after is None:
            after = self.starts[-1][-1]
        idxs = [order.index((n, layer)) for n in names]
        got = _gather_wait(send, recv, [arrays[i] for i in idxs], order, idxs, after, f"gather_wait{layer}_{names[0]}")
        return dict(zip(names, got)), 0


def _gather_weights(full):
    names = list(BIG)

    def body(*refs):
        ins = refs[:len(names)]
        outs = refs[len(names):2 * len(names)]
        send_ici, recv_ici, send_d2d, recv_d2d = refs[2 * len(names):]
        x, y, c = _my_place()
        me = 2 * x + y
        sibling = (x, y, 1 - c)
        chips = _other_chips(x, y)
        ici = []
        for t, name in enumerate(names):
            for j, (cx, cy) in enumerate(chips):
                cp = pltpu.make_async_remote_copy(
                    src_ref=_window(ins[t].at[c], name, me), dst_ref=_window(outs[t].at[c], name, me),
                    send_sem=send_ici.at[3 * t + j], recv_sem=recv_ici.at[3 * t + j],
                    device_id=(cx, cy, c), device_id_type=MESH)
                cp.start()
                ici.append(cp)
        fwd = []
        for t, name in enumerate(names):
            for j, (cx, cy) in enumerate(chips):
                land = _window(outs[t].at[c], name, 2 * cx + cy)
                pltpu.make_async_remote_copy(
                    src_ref=land, dst_ref=land, send_sem=send_ici.at[3 * t + j], recv_sem=recv_ici.at[3 * t + j],
                    device_id=(cx, cy, c), device_id_type=MESH).wait_recv()
                cp = pltpu.make_async_remote_copy(
                    src_ref=land, dst_ref=land, send_sem=send_d2d.at[3 * t + j], recv_sem=recv_d2d.at[3 * t + j],
                    device_id=sibling, device_id_type=MESH)
                cp.start()
                fwd.append(cp)
        for t, name in enumerate(names):
            for j, (cx, cy) in enumerate(chips):
                land = _window(outs[t].at[1 - c], name, 2 * cx + cy)
                pltpu.make_async_remote_copy(
                    src_ref=land, dst_ref=land, send_sem=send_d2d.at[3 * t + j], recv_sem=recv_d2d.at[3 * t + j],
                    device_id=sibling, device_id_type=MESH).wait_recv()
        for cp in ici + fwd:
            cp.wait_send()

    nsem = 3 * len(names)
    outs = pl.pallas_call(
        body, name="gather_weights",
        in_specs=[HBM] * len(names), out_specs=[HBM] * len(names),
        out_shape=[jax.ShapeDtypeStruct(full[n].shape, full[n].dtype) for n in names],
        input_output_aliases={t: t for t in range(len(names))},
        scratch_shapes=[pltpu.SemaphoreType.DMA((nsem,)), pltpu.SemaphoreType.DMA((nsem,)),
                        pltpu.SemaphoreType.DMA((nsem,)), pltpu.SemaphoreType.DMA((nsem,))],
    )(*[full[n] for n in names])
    return dict(zip(names, outs))


def _swap_layers(grads):
    names = list(BIG)

    def body(*refs):
        ins = refs[:len(names)]
        outs = refs[len(names):2 * len(names)]
        send_sems, recv_sems = refs[2 * len(names):]
        x, y, c = _my_place()
        sibling = (x, y, 1 - c)
        cps = []
        for t in range(len(names)):
            cp = pltpu.make_async_remote_copy(
                src_ref=ins[t].at[1 - c], dst_ref=outs[t], send_sem=send_sems.at[t], recv_sem=recv_sems.at[t],
                device_id=sibling, device_id_type=MESH)
            cp.start()
            cps.append(cp)
        for cp in cps:
            cp.wait()

    outs = pl.pallas_call(
        body, name="swap_layers", in_specs=[HBM] * len(names), out_specs=[HBM] * len(names),
        out_shape=[jax.ShapeDtypeStruct(FULL_SHAPE[n], f32) for n in names],
        scratch_shapes=[pltpu.SemaphoreType.DMA((len(names),)), pltpu.SemaphoreType.DMA((len(names),))],
    )(*[grads[n] for n in names])
    return dict(zip(names, outs))


def _chip_sum(grad, other, name):
    k, n = FULL_SHAPE[name]
    tr = min(k, 512)
    c = lax.axis_index("c")

    def body(c_ref, g_ref, o_ref, out_ref):
        out_ref[...] = (g_ref[...] + o_ref[...]).astype(out_ref.dtype)

    return pl.pallas_call(
        body, name="chip_sum_" + name,
        grid_spec=pltpu.PrefetchScalarGridSpec(
            num_scalar_prefetch=1, grid=(k // tr,),
            in_specs=[pl.BlockSpec((None, tr, n), lambda i, c_ref: (c_ref[0], i, 0)),
                      pl.BlockSpec((tr, n), lambda i, c_ref: (i, 0))],
            out_specs=pl.BlockSpec((tr, n), lambda i, c_ref: (i, 0))),
        out_shape=jax.ShapeDtypeStruct((k, n), COMM_DTYPE),
        compiler_params=_cparams(("parallel",)),
    )(jnp.reshape(c, (1,)).astype(jnp.int32), grad, other)


def _scatter_shards(sums):
    names = list(BIG)

    def body(*refs):
        ins = refs[:len(names)]
        outs = refs[len(names):2 * len(names)]
        send_sems, recv_sems = refs[2 * len(names):]
        x, y, c = _my_place()
        me = 2 * x + y
        chips = _other_chips(x, y)
        cps = []
        for t, name in enumerate(names):
            for j, (cx, cy) in enumerate(chips):
                cp = pltpu.make_async_remote_copy(
                    src_ref=_window(ins[t], name, 2 * cx + cy), dst_ref=outs[t].at[me],
                    send_sem=send_sems.at[3 * t + j], recv_sem=recv_sems.at[3 * t + j],
                    device_id=(cx, cy, c), device_id_type=MESH)
                cp.start()
                cps.append(cp)
        for t, name in enumerate(names):
            for j, (cx, cy) in enumerate(chips):
                land = outs[t].at[2 * cx + cy]
                pltpu.make_async_remote_copy(
                    src_ref=land, dst_ref=land, send_sem=send_sems.at[3 * t + j], recv_sem=recv_sems.at[3 * t + j],
                    device_id=(cx, cy, c), device_id_type=MESH).wait_recv()
        for cp in cps:
            cp.wait_send()

    nsem = 3 * len(names)
    outs = pl.pallas_call(
        body, name="scatter_shards", in_specs=[HBM] * len(names), out_specs=[HBM] * len(names),
        out_shape=[jax.ShapeDtypeStruct((N_CHIPS,) + _shard_shape(n), sums[n].dtype) for n in names],
        scratch_shapes=[pltpu.SemaphoreType.DMA((nsem,)), pltpu.SemaphoreType.DMA((nsem,))],
    )(*[sums[n] for n in names])
    return dict(zip(names, outs))


def _sum_slots(slots, own, name):
    ks, ns = _shard_shape(name)
    tr = min(ks, 256)
    shape, index = _shard_block(name, tr)

    def body(me_ref, c_ref, s_ref, own_ref, out_ref):
        me = me_ref[0]
        acc = None
        for s in range(N_CHIPS):
            term = jnp.where(me == s, own_ref[...], s_ref[s]).astype(f32)
            acc = term if acc is None else acc + term
        out_ref[...] = acc

    return pl.pallas_call(
        body, name="sum_slots_" + name,
        grid_spec=pltpu.PrefetchScalarGridSpec(
            num_scalar_prefetch=2, grid=(ks // tr,),
            in_specs=[pl.BlockSpec((N_CHIPS, tr, ns), lambda i, me, c: (0, i, 0)),
                      pl.BlockSpec(shape, lambda i, me, c: index(i, me))],
            out_specs=pl.BlockSpec((None, tr, ns), lambda i, me, c: (c[0], i, 0))),
        out_shape=jax.ShapeDtypeStruct((2, ks, ns), f32),
        compiler_params=_cparams(("parallel",)),
    )(_chip_index(), jnp.reshape(lax.axis_index("c"), (1,)).astype(jnp.int32), slots, own)


N_DEV = 8


def _reduce_copies(dws, lands, names, layer, send_sems, recv_sems):
    x, y, c = _my_place()
    me, my_dev = 2 * x + y, 4 * x + 2 * y + c
    out = []
    for t, name in enumerate(names):
        for j, (cx, cy) in enumerate(_other_chips(x, y)):
            out.append((pltpu.make_async_remote_copy(
                src_ref=_window(dws[t], name, 2 * cx + cy), dst_ref=lands[t].at[my_dev],
                send_sem=send_sems.at[4 * t + j], recv_sem=recv_sems.at[N_DEV * t + my_dev],
                device_id=(cx, cy, layer), device_id_type=MESH), False))
        out.append((pltpu.make_async_remote_copy(
            src_ref=_window(dws[t], name, me), dst_ref=lands[t].at[my_dev],
            send_sem=send_sems.at[4 * t + 3], recv_sem=recv_sems.at[N_DEV * t + my_dev],
            device_id=(x, y, layer), device_id_type=MESH), True))
    return out


def _reduce_start(dws, names, layer, tag):
    n = len(names)
    lands = [lax.empty((N_DEV,) + _shard_shape(nm), dws[0].dtype) for nm in names]

    def body(*refs):
        ins = refs[:n]
        send_sems, recv_sems = refs[2 * n], refs[2 * n + 1]
        land_out = refs[3 * n + 2:4 * n + 2]
        token = refs[-1]
        c = lax.axis_index("c")
        for cp, non_owner_only in _reduce_copies(ins, land_out, names, layer, send_sems, recv_sems):
            if non_owner_only:
                @pl.when(c != layer)
                def _():
                    cp.start()
            else:
                cp.start()
        token[...] = jnp.zeros_like(token)

    res = pl.pallas_call(
        body, name="reduce_start" + tag,
        out_shape=(pltpu.SemaphoreType.DMA((4 * n,)), pltpu.SemaphoreType.DMA((N_DEV * n,)))
        + tuple(pltpu.HBM(a.shape, a.dtype) for a in dws) + tuple(pltpu.HBM(a.shape, a.dtype) for a in lands)
        + (jax.ShapeDtypeStruct((8, 128), f32),),
        in_specs=[HBM] * (2 * n),
        out_specs=(SEM, SEM) + (HBM,) * (2 * n) + (pl.BlockSpec(memory_space=pltpu.VMEM),),
        input_output_aliases={i: i + 2 for i in range(2 * n)},
        compiler_params=pltpu.CompilerParams(has_side_effects=EFFECT),
    )(*[pltpu.with_memory_space_constraint(a, pltpu.HBM) for a in list(dws) + lands])
    return res[0], res[1], list(res[2:2 + n]), list(res[2 + n:2 + 2 * n]), res[-1]


def _reduce_wait(send_sems, recv_sems, dws, lands, names, layer, after, tag):
    n = len(names)

    def body(*refs):
        ins, land_in = refs[:n], refs[n:2 * n]
        send_ref, recv_ref = refs[2 * n], refs[2 * n + 1]
        x, y, c = _my_place()
        for cp, non_owner_only in _reduce_copies(ins, land_in, names, layer, send_ref, recv_ref):
            if non_owner_only:
                @pl.when(c != layer)
                def _():
                    cp.wait_send()
            else:
                cp.wait_send()

        @pl.when(c == layer)
        def _():
            for t in range(n):
                for k in range(1, N_DEV):
                    px, py, pc = x ^ ((k >> 2) & 1), y ^ ((k >> 1) & 1), c ^ (k & 1)
                    dev = 4 * px + 2 * py + pc
                    land = land_in[t].at[dev]
                    pltpu.make_async_remote_copy(
                        src_ref=land, dst_ref=land, send_sem=send_ref.at[4 * t], recv_sem=recv_ref.at[N_DEV * t + dev],
                        device_id=(px, py, pc), device_id_type=MESH).wait_recv()

    res = pl.pallas_call(
        body, name="reduce_wait" + tag,
        out_shape=tuple(pltpu.HBM(a.shape, a.dtype) for a in list(dws) + list(lands)),
        in_specs=[HBM] * (2 * n) + [SEM, SEM, pl.BlockSpec(memory_space=pl.ANY)], out_specs=(HBM,) * (2 * n),
        input_output_aliases={i: i for i in range(2 * n)},
        compiler_params=pltpu.CompilerParams(has_side_effects=EFFECT),
    )(*dws, *lands, send_sems, recv_sems, after)
    return list(res[:n]), list(res[n:])


def _sum_devices(land, own, name, layer, prev):
    ks, ns = _shard_shape(name)
    tr = min(ks, 256)
    shape, index = _shard_block(name, tr)

    def body(me_ref, dev_ref, *refs):
        s_ref, own_ref, out_ref = refs[0], refs[1], refs[-1]
        dev = dev_ref[0]
        acc = None
        for s in range(N_DEV):
            term = jnp.where(dev == s, own_ref[...], s_ref[s]).astype(f32)
            acc = term if acc is None else acc + term
        out_ref[...] = acc

    in_specs = [pl.BlockSpec((N_DEV, tr, ns), lambda i, me, dev: (0, i, 0)),
                pl.BlockSpec(shape, lambda i, me, dev: index(i, me))]
    args = [land, own]
    aliases = {}
    if prev is not None:
        in_specs.append(pl.BlockSpec(memory_space=pl.ANY))
        args.append(prev)
        aliases = {4: 0}
    x, y, c = _my_place()
    return pl.pallas_call(
        body, name=f"sum_devices_{name}{layer}",
        grid_spec=pltpu.PrefetchScalarGridSpec(
            num_scalar_prefetch=2, grid=(ks // tr,), in_specs=in_specs,
            out_specs=pl.BlockSpec((None, tr, ns), lambda i, me, dev: (layer, i, 0))),
        out_shape=jax.ShapeDtypeStruct((2, ks, ns), f32), input_output_aliases=aliases,
        compiler_params=_cparams(("parallel",)),
    )(_chip_index(), jnp.reshape(4 * x + 2 * y + c, (1,)).astype(jnp.int32), *args)


class _GradReducer:
    GROUPS = (("1", 1, ("w_gate", "w_ple", "w_down", "w_up", "w_out", "w_in")),
              ("0a", 0, ("w_gate", "w_ple", "w_down", "w_up")),
              ("0b", 0, ("w_out", "w_in")))

    def __init__(self):
        self.grads = {}
        self.started = {}

    def add(self, name, layer, dw):
        self.grads[(name, layer)] = dw
        token = None
        for tag, glayer, names in self.GROUPS:
            if tag not in self.started and all((nm, glayer) in self.grads for nm in names):
                *self.started[tag], token = _reduce_start([self.grads[(nm, glayer)] for nm in names], names, glayer, tag)
        return token

    def finish(self, after):
        mine = {}
        for tag, layer, names in self.GROUPS:
            send, recv, dws, lands = self.started[tag]
            dws, lands = _reduce_wait(send, recv, dws, lands, names, layer, after, tag)
            for nm, dw, land in zip(names, dws, lands):
                mine[nm] = _sum_devices(land, dw, nm, layer, mine.get(nm))
        return _pair_layers(mine)


def _pair_layers(mine):
    names = list(BIG)

    def body(*refs):
        ins = refs[:len(names)]
        outs = refs[len(names):2 * len(names)]
        send_sems, recv_sems = refs[2 * len(names):]
        x, y, c = _my_place()
        sibling = (x, y, 1 - c)
        cps = []
        for t in range(len(names)):
            cp = pltpu.make_async_remote_copy(
                src_ref=ins[t].at[c], dst_ref=outs[t].at[c], send_sem=send_sems.at[t], recv_sem=recv_sems.at[t],
                device_id=sibling, device_id_type=MESH)
            cp.start()
            cps.append(cp)
        for t in range(len(names)):
            cps[t].wait_send()
            land = outs[t].at[1 - c]
            pltpu.make_async_remote_copy(
                src_ref=land, dst_ref=land, send_sem=send_sems.at[t], recv_sem=recv_sems.at[t],
                device_id=sibling, device_id_type=MESH).wait_recv()

    outs = pl.pallas_call(
        body, name="pair_layers", in_specs=[HBM] * len(names), out_specs=[HBM] * len(names),
        out_shape=[jax.ShapeDtypeStruct((2,) + _shard_shape(n), f32) for n in names],
        input_output_aliases={t: t for t in range(len(names))},
        scratch_shapes=[pltpu.SemaphoreType.DMA((len(names),)), pltpu.SemaphoreType.DMA((len(names),))],
    )(*[mine[n] for n in names])
    return dict(zip(names, outs))


SMALL_ROWS = 320


def _allreduce_small(vec):
    n_dev = 8

    def body(v_ref, out_ref, buf_ref, send_sems, recv_sems):
        x, y, c = _my_place()
        me = 4 * x + 2 * y + c
        buf_ref[me] = v_ref[...]
        cps = []
        for k in range(1, n_dev):
            dx, dy, dc = (k >> 2) & 1, (k >> 1) & 1, k & 1
            peer = (x ^ dx, y ^ dy, c ^ dc)
            cp = pltpu.make_async_remote_copy(
                src_ref=v_ref, dst_ref=buf_ref.at[me], send_sem=send_sems.at[k - 1], recv_sem=recv_sems.at[k - 1],
                device_id=peer, device_id_type=MESH)
            cp.start()
            cps.append(cp)
        for k in range(1, n_dev):
            dx, dy, dc = (k >> 2) & 1, (k >> 1) & 1, k & 1
            src = 4 * (x ^ dx) + 2 * (y ^ dy) + (c ^ dc)
            land = buf_ref.at[src]
            pltpu.make_async_remote_copy(
                src_ref=land, dst_ref=land, send_sem=send_sems.at[k - 1], recv_sem=recv_sems.at[k - 1],
                device_id=(x ^ dx, y ^ dy, c ^ dc), device_id_type=MESH).wait_recv()
        for cp in cps:
            cp.wait_send()
        acc = buf_ref[0]
        for s in range(1, n_dev):
            acc = acc + buf_ref[s]
        out_ref[...] = acc

    return pl.pallas_call(
        body, name="allreduce_small",
        in_specs=[pl.BlockSpec(memory_space=pltpu.VMEM)], out_specs=pl.BlockSpec(memory_space=pltpu.VMEM),
        out_shape=jax.ShapeDtypeStruct((SMALL_ROWS, 128), f32),
        scratch_shapes=[pltpu.VMEM((n_dev, SMALL_ROWS, 128), f32), pltpu.SemaphoreType.DMA((n_dev - 1,)),
                        pltpu.SemaphoreType.DMA((n_dev - 1,))],
    )(vec)


def _adamw(w, g, m, v, name):
    rows, cols = w.shape
    tr = rows
    for cand in (512, 256, 128, 64, 32, 16, 8):
        if rows % cand == 0 and cand * cols * 4 <= 2 * 1024 * 1024:
            tr = cand
            break
    c1 = np.float32(1.0 - ADAM_B1 ** ADAM_STEP)
    c2 = np.float32(1.0 - ADAM_B2 ** ADAM_STEP)

    def body(w_ref, g_ref, m_ref, v_ref, go_ref, d_ref, mo_ref, vo_ref):
        gv = g_ref[...]
        go_ref[...] = gv
        mn = ADAM_B1 * m_ref[...] + (1.0 - ADAM_B1) * gv
        vn = ADAM_B2 * v_ref[...] + (1.0 - ADAM_B2) * (gv * gv)
        mo_ref[...] = mn
        vo_ref[...] = vn
        d_ref[...] = -ADAM_LR * ((mn / c1) / (jnp.sqrt(vn / c2) + ADAM_EPS) + ADAM_WD * w_ref[...])

    blk = pl.BlockSpec((tr, cols), lambda i: (i, 0))
    return pl.pallas_call(
        body, name="adamw_" + name, grid=(rows // tr,), in_specs=[blk] * 4, out_specs=[blk] * 4,
        out_shape=[jax.ShapeDtypeStruct((rows, cols), f32)] * 4,
        compiler_params=_cparams(("parallel",)),
    )(w, g, m, v)


SMALL = ("norm1", "pool_w", "pool_scale", "norm2", "norm3", "final_norm")
ORDER = ("norm1", "w_in", "pool_w", "pool_scale", "w_out", "norm2", "w_up", "w_down", "norm3", "w_gate", "w_ple",
         "final_norm")


def _pack_small(tree, extra=None):
    parts = [tree[n].reshape(-1) for n in SMALL]
    if extra is not None:
        parts.append(extra.reshape(-1))
    flat = jnp.concatenate(parts)
    return jnp.pad(flat, (0, SMALL_ROWS * 128 - flat.shape[0])).reshape(SMALL_ROWS, 128)


def _unpack_small(packed, like):
    flat = packed.reshape(-1)
    out, off = {}, 0
    for n in SMALL:
        size = int(np.prod(like[n].shape))
        out[n] = flat[off:off + size].reshape(like[n].shape)
        off += size
    return out, flat[off]


def kernel(x, p, positions, norm1, w_in, pool_w, pool_scale, w_out, norm2, w_up, w_down, norm3, w_gate, w_ple, final_norm, loss_target, m_norm1, m_w_in, m_pool_w, m_pool_scale, m_w_out, m_norm2, m_w_up, m_w_down, m_norm3, m_w_gate, m_w_ple, m_final_norm, v_norm1, v_w_in, v_pool_w, v_pool_scale, v_w_out, v_norm2, v_w_up, v_w_down, v_norm3, v_w_gate, v_w_ple, v_final_norm):
    w = dict(norm1=norm1, w_in=w_in, pool_w=pool_w, pool_scale=pool_scale, w_out=w_out, norm2=norm2, w_up=w_up,
             w_down=w_down, norm3=norm3, w_gate=w_gate, w_ple=w_ple, final_norm=final_norm)
    m = dict(norm1=m_norm1, w_in=m_w_in, pool_w=m_pool_w, pool_scale=m_pool_scale, w_out=m_w_out, norm2=m_norm2,
             w_up=m_w_up, w_down=m_w_down, norm3=m_norm3, w_gate=m_w_gate, w_ple=m_w_ple, final_norm=m_final_norm)
    v = dict(norm1=v_norm1, w_in=v_w_in, pool_w=v_pool_w, pool_scale=v_pool_scale, w_out=v_w_out, norm2=v_norm2,
             w_up=v_w_up, w_down=v_w_down, norm3=v_norm3, w_gate=v_w_gate, w_ple=v_w_ple, final_norm=v_final_norm)
    small = {n: w[n] for n in SMALL}

    wsrc = _GatheredWeights({n: w[n] for n in BIG})
    reducer = _GradReducer()
    loss8, dx, small_grads = _local_step(x[0], p[:, 0], positions[0], wsrc, small, loss_target[0], reducer)
    gsh = reducer.finish(dx)

    red = _allreduce_small(_pack_small(small_grads, loss8[0, 0]))
    g_small, loss = _unpack_small(red, small)

    g_out, d_out, m_out, v_out = {}, {}, {}, {}
    for n in BIG:
        shp = w[n].shape
        two = lambda a: a.reshape(shp[0] * shp[1], shp[2])
        g2, d2, m2, v2 = _adamw(two(w[n]), two(gsh[n]), two(m[n]), two(v[n]), n)
        g_out[n], d_out[n], m_out[n], v_out[n] = g2.reshape(shp), d2.reshape(shp), m2.reshape(shp), v2.reshape(shp)
    _, d2, m2, v2 = _adamw(_pack_small(small), red, _pack_small({n: m[n] for n in SMALL}),
                           _pack_small({n: v[n] for n in SMALL}), "small")
    for tree, packed in ((d_out, d2), (m_out, m2), (v_out, v2)):
        tree.update(_unpack_small(packed, small)[0])
    g_out.update(g_small)

    return (loss, dx[None], *[g_out[n] for n in ORDER], *[d_out[n] for n in ORDER], *[m_out[n] for n in ORDER],
            *[v_out[n] for n in ORDER])
```

```python
import functools

import jax
import jax.numpy as jnp
import numpy as np
from jax import lax
from jax.experimental import pallas as pl
from jax.experimental.pallas import tpu as pltpu

f32 = jnp.float32
MXU_DTYPE = jnp.bfloat16
COMM_DTYPE = jnp.bfloat16

D_MODEL = 1024
POOL_WIDTH = 256
POOL_GC = 64
ATTN_WIDTH = 768
HEAD_DIM = 64
N_IN = POOL_WIDTH + 3 * ATTN_WIDTH
D_FF = 4096
PLE_DIM = 256
BLK = 128
DILATIONS = (1, 4, 16)
ROT_DIM = 16
ROPE_THETA = 500000.0
EPS = 1e-6
ATTN_SCALE = HEAD_DIM ** -0.5
NEG_BIG = -1e30

ADAM_LR, ADAM_B1, ADAM_B2, ADAM_EPS, ADAM_WD, ADAM_STEP = 0.001, 0.9, 0.999, 1e-08, 0.01, 10

TM = 512
HALO = 16
VMEM_LIMIT = 48 * 1024 * 1024
N_CHIPS = 4
MESH = pl.DeviceIdType.MESH

BIG = ("w_in", "w_out", "w_up", "w_down", "w_gate", "w_ple")
FULL_SHAPE = {"w_in": (D_MODEL, N_IN), "w_out": (D_MODEL, D_MODEL), "w_up": (D_MODEL, D_FF),
              "w_down": (D_FF, D_MODEL), "w_gate": (D_MODEL, D_MODEL), "w_ple": (PLE_DIM, D_MODEL)}
COL_SHARDED = {"w_in": True, "w_out": False, "w_up": True, "w_down": False, "w_gate": False, "w_ple": True}


def _shard_shape(name):
    k, n = FULL_SHAPE[name]
    return (k, n // N_CHIPS) if COL_SHARDED[name] else (k // N_CHIPS, n)


def _cparams(sem=None, vmem=VMEM_LIMIT):
    return pltpu.CompilerParams(dimension_semantics=sem, vmem_limit_bytes=vmem)


def _resident(block_shape, index_map):
    return pl.BlockSpec(block_shape, index_map, pipeline_mode=pl.Buffered(1))


def _mx(x):
    return x.astype(MXU_DTYPE)


def _dot(a, b):
    return jnp.dot(a, b, preferred_element_type=f32)


def _dot_nt(a, b):
    return lax.dot_general(a, b, (((1,), (1,)), ((), ())), preferred_element_type=f32)


def _dot_tn(a, b):
    return lax.dot_general(a, b, (((0,), (0,)), ((), ())), preferred_element_type=f32)


def _sigmoid(x):
    return 1.0 / (1.0 + jnp.exp(-x))


def _rope_apply(y, c, s1, s2, width):
    return y * c + pltpu.roll(y, width - 8, axis=1) * s1 + pltpu.roll(y, 8, axis=1) * s2


def _rope_transpose(dy, c, s1, s2, width):
    return dy * c + pltpu.roll(dy * s1, 8, axis=1) + pltpu.roll(dy * s2, width - 8, axis=1)


def _norm_matmul(h, g, w, layer, tn, name, rope=None):
    s_len, d = h.shape
    n = w.shape[2]

    def body(*refs):
        if rope is None:
            h_ref, g_ref, w_ref, y_ref, hn_ref = refs
        else:
            h_ref, g_ref, w_ref, c_ref, s1_ref, s2_ref, y_ref, hn_ref = refs
            reps = tn // 128
            c = jnp.concatenate([c_ref[...]] * reps, axis=1)
            s1 = jnp.concatenate([s1_ref[...]] * reps, axis=1)
            s2 = jnp.concatenate([s2_ref[...]] * reps, axis=1)
        x = h_ref[...]
        r = lax.rsqrt(jnp.mean(x * x, axis=-1, keepdims=True) + EPS)
        hn = ((x * r) * g_ref[...]).astype(hn_ref.dtype)
        hn_ref[...] = hn
        for j in range(n // tn):
            y = _dot(hn, w_ref[:, j * tn:(j + 1) * tn])
            if rope is not None and POOL_WIDTH <= j * tn < POOL_WIDTH + 2 * ATTN_WIDTH:
                y = _rope_apply(y, c, s1, s2, tn)
            y_ref[:, j * tn:(j + 1) * tn] = y

    in_specs = [pl.BlockSpec((TM, d), lambda i: (i, 0)),
                pl.BlockSpec((1, d), lambda i: (0, 0)),
                _resident((None, d, n), lambda i: (layer, 0, 0))]
    args = [h, g, w]
    if rope is not None:
        assert POOL_WIDTH % tn == 0 and (2 * ATTN_WIDTH) % tn == 0
        in_specs += [pl.BlockSpec((TM, 128), lambda i: (i, 0))] * 3
        args += list(rope)
    return pl.pallas_call(
        body, name=name, grid=(s_len // TM,), in_specs=in_specs,
        out_specs=[pl.BlockSpec((TM, n), lambda i: (i, 0)), pl.BlockSpec((TM, d), lambda i: (i, 0))],
        out_shape=[jax.ShapeDtypeStruct((s_len, n), f32), jax.ShapeDtypeStruct((s_len, d), MXU_DTYPE)],
        compiler_params=_cparams(("parallel",)),
    )(*args)


def _matmul_residual(a, w, layer, res, name, act=False, tk=1024):
    s_len, k_dim = a.shape
    n = w.shape[2]

    def body(a_ref, w_ref, res_ref, o_ref):
        acc = res_ref[...]
        for k in range(k_dim // tk):
            x = a_ref[:, k * tk:(k + 1) * tk]
            if act:
                r = jnp.maximum(x, 0.0)
                x = r * r
            acc = acc + _dot(_mx(x), w_ref[k * tk:(k + 1) * tk, :])
        o_ref[...] = acc

    return pl.pallas_call(
        body, name=name, grid=(s_len // TM,),
        in_specs=[pl.BlockSpec((TM, k_dim), lambda i: (i, 0)),
                  _resident((None, k_dim, n), lambda i: (layer, 0, 0)),
                  pl.BlockSpec((TM, n), lambda i: (i, 0))],
        out_specs=pl.BlockSpec((TM, n), lambda i: (i, 0)),
        out_shape=jax.ShapeDtypeStruct((s_len, n), f32),
        compiler_params=_cparams(("parallel",)),
    )(a, w, res)


def _gate_ple_fwd(h2, g, w_gate, w_ple, layer, p, name):
    s_len, d = h2.shape

    def body(h_ref, g_ref, wg_ref, p_ref, wp_ref, h3_ref, gl_ref, hn_ref):
        x = h_ref[...]
        r = lax.rsqrt(jnp.mean(x * x, axis=-1, keepdims=True) + EPS)
        hn = ((x * r) * g_ref[...]).astype(hn_ref.dtype)
        hn_ref[...] = hn
        gl = _dot(hn, wg_ref[...])
        gl_ref[...] = gl
        e = _dot(_mx(p_ref[...]), wp_ref[...])
        h3_ref[...] = x + _sigmoid(gl) * e

    row = lambda i: (i, 0)
    return pl.pallas_call(
        body, name=name, grid=(s_len // TM,),
        in_specs=[pl.BlockSpec((TM, d), row), pl.BlockSpec((1, d), lambda i: (0, 0)),
                  pl.BlockSpec((None, d, d), lambda i: (layer, 0, 0)), pl.BlockSpec((TM, PLE_DIM), row),
                  pl.BlockSpec((None, PLE_DIM, d), lambda i: (layer, 0, 0))],
        out_specs=[pl.BlockSpec((TM, d), row)] * 3,
        out_shape=[jax.ShapeDtypeStruct((s_len, d), f32), jax.ShapeDtypeStruct((s_len, d), f32),
                   jax.ShapeDtypeStruct((s_len, d), MXU_DTYPE)],
        compiler_params=_cparams(("parallel",)),
    )(h2, g, w_gate, p, w_ple)


def _gate_ple_bwd(dh3, gl, p, w_ple, layer, name):
    s_len, d = dh3.shape

    def body(dh_ref, gl_ref, p_ref, wp_ref, de_ref, dgl_ref):
        dh = dh_ref[...]
        gate = _sigmoid(gl_ref[...])
        e = _dot(_mx(p_ref[...]), wp_ref[...])
        de_ref[...] = (dh * gate).astype(de_ref.dtype)
        dgl_ref[...] = ((dh * e) * (gate * (1.0 - gate))).astype(dgl_ref.dtype)

    row = lambda i: (i, 0)
    return pl.pallas_call(
        body, name=name, grid=(s_len // TM,),
        in_specs=[pl.BlockSpec((TM, d), row), pl.BlockSpec((TM, d), row), pl.BlockSpec((TM, PLE_DIM), row),
                  pl.BlockSpec((None, PLE_DIM, d), lambda i: (layer, 0, 0))],
        out_specs=[pl.BlockSpec((TM, d), row)] * 2,
        out_shape=[jax.ShapeDtypeStruct((s_len, d), MXU_DTYPE)] * 2,
        compiler_params=_cparams(("parallel",)),
    )(dh3, gl, p, w_ple)


def _rmsnorm_bwd(dhn, x, g):
    r = lax.rsqrt(jnp.mean(x * x, axis=-1, keepdims=True) + EPS)
    xh = x * r
    dxh = dhn * g
    dx = r * (dxh - xh * jnp.mean(dxh * xh, axis=-1, keepdims=True))
    return dx, dhn * xh


def _matmul_nt_norm_bwd(dy, w, layer, h_prev, g, dres, name, tk=1024, after=None):
    s_len, k_dim = dy.shape
    d = h_prev.shape[1]

    def body(dy_ref, w_ref, h_ref, g_ref, dres_ref, *rest):
        dh_ref, dg_ref = rest[-2:]
        i = pl.program_id(0)
        acc = None
        for k in range(k_dim // tk):
            part = _dot_nt(_mx(dy_ref[:, k * tk:(k + 1) * tk]), w_ref[:, k * tk:(k + 1) * tk])
            acc = part if acc is None else acc + part
        dx, dgrow = _rmsnorm_bwd(acc, h_ref[...], g_ref[...])
        dh_ref[...] = dres_ref[...] + dx
        dgsum = jnp.sum(dgrow, axis=0, keepdims=True)

        @pl.when(i == 0)
        def _():
            dg_ref[...] = dgsum

        @pl.when(i > 0)
        def _():
            dg_ref[...] += dgsum

    in_specs = [pl.BlockSpec((TM, k_dim), lambda i: (i, 0)),
                _resident((None, d, k_dim), lambda i: (layer, 0, 0)),
                pl.BlockSpec((TM, d), lambda i: (i, 0)),
                pl.BlockSpec((1, d), lambda i: (0, 0)),
                pl.BlockSpec((TM, d), lambda i: (i, 0))]
    args = [dy, w, h_prev, g, dres]
    if after is not None:
        in_specs.append(pl.BlockSpec(memory_space=pl.ANY))
        args.append(after)
    return pl.pallas_call(
        body, name=name, grid=(s_len // TM,), in_specs=in_specs,
        out_specs=[pl.BlockSpec((TM, d), lambda i: (i, 0)), pl.BlockSpec((1, d), lambda i: (0, 0))],
        out_shape=[jax.ShapeDtypeStruct((s_len, d), f32), jax.ShapeDtypeStruct((1, d), f32)],
        compiler_params=_cparams(("arbitrary",)),
    )(*args)


def _down_bwd(dh2, w_down, layer, a, name, tf=1024):
    s_len, d = dh2.shape
    ff = a.shape[1]

    def body(dh_ref, w_ref, a_ref, act_ref, da_ref, dhb_ref):
        j = pl.program_id(1)

        @pl.when(j == 0)
        def _():
            dhb_ref[...] = _mx(dh_ref[...])

        dact = _dot_nt(dhb_ref[...], w_ref[pl.ds(pl.multiple_of(j * tf, tf), tf), :])
        r = jnp.maximum(a_ref[...], 0.0)
        act_ref[...] = (r * r).astype(act_ref.dtype)
        da_ref[...] = (dact * (2.0 * r)).astype(da_ref.dtype)

    return pl.pallas_call(
        body, name=name, grid=(s_len // TM, ff // tf),
        in_specs=[pl.BlockSpec((TM, d), lambda i, j: (i, 0)),
                  _resident((None, ff, d), lambda i, j: (layer, 0, 0)),
                  pl.BlockSpec((TM, tf), lambda i, j: (i, j))],
        out_specs=[pl.BlockSpec((TM, tf), lambda i, j: (i, j))] * 2,
        out_shape=[jax.ShapeDtypeStruct((s_len, ff), MXU_DTYPE)] * 2,
        scratch_shapes=[pltpu.VMEM((TM, d), MXU_DTYPE)],
        compiler_params=_cparams(("parallel", "arbitrary")),
    )(dh2, w_down, a)


def _matmul_nt(dy, w, layer, name):
    s_len, n = dy.shape
    k_dim = w.shape[1]

    def body(dy_ref, w_ref, o_ref):
        o_ref[...] = _dot_nt(_mx(dy_ref[...]), w_ref[...])

    return pl.pallas_call(
        body, name=name, grid=(s_len // TM,),
        in_specs=[pl.BlockSpec((TM, n), lambda i: (i, 0)), pl.BlockSpec((None, k_dim, n), lambda i: (layer, 0, 0))],
        out_specs=pl.BlockSpec((TM, k_dim), lambda i: (i, 0)),
        out_shape=jax.ShapeDtypeStruct((s_len, k_dim), f32),
        compiler_params=_cparams(("parallel",)),
    )(dy, w)


def _weight_grad(a, b, name):
    s_len, k_dim = a.shape
    n = b.shape[1]
    tka = min(k_dim, 2048)
    tnb = n if n <= 1024 else (2048 if n % 2048 == 0 else 640)
    ns = s_len // TM

    def body(a_ref, b_ref, o_ref, acc_ref):
        s = pl.program_id(2)
        part = _dot_tn(_mx(a_ref[...]), _mx(b_ref[...]))

        @pl.when(s == 0)
        def _():
            acc_ref[...] = part

        @pl.when(s > 0)
        def _():
            acc_ref[...] += part

        @pl.when(s == ns - 1)
        def _():
            o_ref[...] = acc_ref[...].astype(o_ref.dtype)

    return pl.pallas_call(
        body, name=name, grid=(k_dim // tka, n // tnb, ns),
        in_specs=[pl.BlockSpec((TM, tka), lambda i, j, s: (s, i)), pl.BlockSpec((TM, tnb), lambda i, j, s: (s, j))],
        out_specs=pl.BlockSpec((tka, tnb), lambda i, j, s: (i, j)),
        out_shape=jax.ShapeDtypeStruct((k_dim, n), COMM_DTYPE),
        scratch_shapes=[pltpu.VMEM((tka, tnb), f32)],
        compiler_params=_cparams(("parallel", "parallel", "arbitrary")),
    )(a, b)


def _group_select(lane, x2, x4, x8, x16):
    grp = lane // POOL_GC
    return jnp.where(grp == 0, x2, jnp.where(grp == 1, x4, jnp.where(grp == 2, x8, x16)))


def _pool_window(lane):
    grp = lane // POOL_GC
    return jnp.where(grp == 0, 2, jnp.where(grp == 1, 4, jnp.where(grp == 2, 8, 16)))


def _pool_y(u, halo, i):
    xs = jnp.concatenate([jnp.where(i > 0, halo, 0.0), u], axis=0)
    s2 = xs + pltpu.roll(xs, 1, axis=0)
    s4 = s2 + pltpu.roll(s2, 2, axis=0)
    s8 = s4 + pltpu.roll(s4, 4, axis=0)
    s16 = s8 + pltpu.roll(s8, 8, axis=0)
    lane = lax.broadcasted_iota(jnp.int32, xs.shape, 1)
    sel = _group_select(lane, s2, s4, s8, s16)[HALO:, :]
    t = i * TM + lax.broadcasted_iota(jnp.int32, u.shape, 0)
    cnt = jnp.minimum(_pool_window(lax.broadcasted_iota(jnp.int32, u.shape, 1)), t + 1).astype(f32)
    return sel / cnt - u


def _group_weights(l0, l1, l2):
    mx = jnp.maximum(jnp.maximum(l0, l1), l2)
    e0, e1, e2 = jnp.exp(l0 - mx), jnp.exp(l1 - mx), jnp.exp(l2 - mx)
    den = e0 + e1 + e2
    return e0 / den, e1 / den, e2 / den


def _mixer_merge(z, wbd, scale, outs, lses, name):
    s_len = z.shape[0]

    def body(u_ref, halo_ref, wbd_ref, sc_ref, o0, o1, o2, l0, l1, l2, m_ref):
        i = pl.program_id(0)
        y = _pool_y(u_ref[...], halo_ref[...], i)
        pool = _dot(_mx(y), wbd_ref[...]) * sc_ref[...]
        w0, w1, w2 = _group_weights(l0[...], l1[...], l2[...])
        m_ref[...] = jnp.concatenate([pool, o0[...] * w0, o1[...] * w1, o2[...] * w2], axis=1).astype(m_ref.dtype)

    row = lambda i: (i, 0)
    blk = pl.BlockSpec((TM, 256), row)
    grp = [pl.BlockSpec((TM, 256), lambda i, g=g: (i, g)) for g in range(3)]
    return pl.pallas_call(
        body, name=name, grid=(s_len // TM,),
        in_specs=[blk, pl.BlockSpec((HALO, 256), lambda i: (jnp.maximum(i * (TM // HALO) - 1, 0), 0)),
                  pl.BlockSpec((256, 256), lambda i: (0, 0)), pl.BlockSpec((1, 256), lambda i: (0, 0))] + grp + grp,
        out_specs=pl.BlockSpec((TM, D_MODEL), row),
        out_shape=jax.ShapeDtypeStruct((s_len, D_MODEL), MXU_DTYPE),
        compiler_params=_cparams(("parallel",)),
    )(z, z, wbd, scale, outs, outs, outs, lses, lses, lses)


def _head_sums(x):
    r = lax.broadcasted_iota(jnp.int32, (256, 256), 0) // HEAD_DIM
    c = lax.broadcasted_iota(jnp.int32, (256, 256), 1) // HEAD_DIM
    ones = jnp.where(r == c, 1.0, 0.0).astype(jnp.bfloat16)
    hi = x.astype(jnp.bfloat16)
    lo = (x - hi.astype(f32)).astype(jnp.bfloat16)
    return _dot(hi, ones) + _dot(lo, ones)


def _combine_bwd(dm, outs, lses, name):
    s_len = dm.shape[0]

    def body(d0, d1, d2, o0, o1, o2, l0, l1, l2, do_ref, dl_ref):
        w = _group_weights(l0[...], l1[...], l2[...])
        da = (d0[...], d1[...], d2[...])
        o = (o0[...], o1[...], o2[...])
        dw = [_head_sums(da[g] * o[g]) for g in range(3)]
        t = w[0] * dw[0] + w[1] * dw[1] + w[2] * dw[2]
        do_ref[...] = jnp.concatenate([da[g] * w[g] for g in range(3)], axis=1)
        dl_ref[...] = jnp.concatenate([w[g] * t for g in range(3)], axis=1)

    grp = [pl.BlockSpec((TM, 256), lambda i, g=g: (i, g)) for g in range(3)]
    return pl.pallas_call(
        body, name=name, grid=(s_len // TM,),
        in_specs=[pl.BlockSpec((TM, 256), lambda i: (i, 1)), pl.BlockSpec((TM, 256), lambda i: (i, 2)),
                  pl.BlockSpec((TM, 256), lambda i: (i, 3))] + grp + grp,
        out_specs=[pl.BlockSpec((TM, ATTN_WIDTH), lambda i: (i, 0))] * 2,
        out_shape=[jax.ShapeDtypeStruct((s_len, ATTN_WIDTH), f32)] * 2,
        compiler_params=_cparams(("parallel",)),
    )(dm, dm, dm, outs, outs, outs, lses, lses, lses)


def _pool_bwd(z, dm, wbd, scale, name):
    s_len = z.shape[0]
    n_halo = s_len // HALO

    def body(u_ref, uh_ref, d_ref, dh_ref, wbd_ref, sc_ref, du_ref, dw_ref, dsc_ref):
        i = pl.program_id(0)
        last = pl.num_programs(0) - 1
        y = _pool_y(u_ref[...], uh_ref[...], i)
        yb = _mx(y)
        dpo = d_ref[...]
        sc = sc_ref[...]
        dsc = jnp.sum(dpo * _dot(yb, wbd_ref[...]), axis=0, keepdims=True)
        dwp = _dot_tn(yb, _mx(dpo * sc))

        @pl.when(i == 0)
        def _():
            dsc_ref[...] = dsc
            dw_ref[...] = dwp

        @pl.when(i > 0)
        def _():
            dsc_ref[...] += dsc
            dw_ref[...] += dwp

        ext = jnp.concatenate([dpo, jnp.where(i < last, dh_ref[...], 0.0)], axis=0)
        dy = _dot_nt(_mx(ext * sc), wbd_ref[...])
        t = i * TM + lax.broadcasted_iota(jnp.int32, ext.shape, 0)
        lane = lax.broadcasted_iota(jnp.int32, ext.shape, 1)
        e = dy / jnp.minimum(_pool_window(lane), t + 1).astype(f32)
        rows = ext.shape[0]
        f2 = e + pltpu.roll(e, rows - 1, axis=0)
        f4 = f2 + pltpu.roll(f2, rows - 2, axis=0)
        f8 = f4 + pltpu.roll(f4, rows - 4, axis=0)
        f16 = f8 + pltpu.roll(f8, rows - 8, axis=0)
        du_ref[...] = (_group_select(lane, f2, f4, f8, f16) - dy)[:TM, :]

    row = lambda i: (i, 0)
    blk = pl.BlockSpec((TM, 256), row)
    return pl.pallas_call(
        body, name=name, grid=(s_len // TM,),
        in_specs=[blk, pl.BlockSpec((HALO, 256), lambda i: (jnp.maximum(i * (TM // HALO) - 1, 0), 0)),
                  blk, pl.BlockSpec((HALO, 256), lambda i: (jnp.minimum((i + 1) * (TM // HALO), n_halo - 1), 0)),
                  pl.BlockSpec((256, 256), lambda i: (0, 0)), pl.BlockSpec((1, 256), lambda i: (0, 0))],
        out_specs=[blk, pl.BlockSpec((256, 256), lambda i: (0, 0)), pl.BlockSpec((1, 256), lambda i: (0, 0))],
        out_shape=[jax.ShapeDtypeStruct((s_len, N_IN), f32), jax.ShapeDtypeStruct((256, 256), f32),
                   jax.ShapeDtypeStruct((1, 256), f32)],
        compiler_params=_cparams(("arbitrary",)),
    )(z, z, dm, dm, wbd, scale)


def _to_strided(x, dil):
    if dil == 1:
        return x
    s_len, c = x.shape
    return x.reshape(s_len // (BLK * dil), BLK, dil, c).transpose(0, 2, 1, 3).reshape(s_len, c)


def _from_strided(x, dil):
    if dil == 1:
        return x
    s_len, c = x.shape
    return x.reshape(s_len // (BLK * dil), dil, BLK, c).transpose(0, 2, 1, 3).reshape(s_len, c)


def _tri_masks():
    qi = lax.broadcasted_iota(jnp.int32, (BLK, BLK), 0)
    ki = lax.broadcasted_iota(jnp.int32, (BLK, BLK), 1)
    return qi >= ki, ki >= qi


ATTN_SUPER_PER_STEP = (8, 2, 1)
Q_COL, K_COL, V_COL = POOL_WIDTH // 128, (POOL_WIDTH + ATTN_WIDTH) // 128, (POOL_WIDTH + 2 * ATTN_WIDTH) // 128


def _rows(ref, start, dil):
    if dil == 1:
        return ref[pl.ds(start, BLK), :]
    return ref[pl.ds(start, BLK, stride=dil), :]


def _set_rows(ref, start, dil, val):
    if dil == 1:
        ref[pl.ds(start, BLK), :] = val
    else:
        ref[pl.ds(start, BLK, stride=dil), :] = val


def _attn_fwd(z, g, prev, name):
    s_len = z.shape[0]
    dil, m = DILATIONS[g], ATTN_SUPER_PER_STEP[g]
    sbr = BLK * dil
    rows = sbr * m

    def body(*refs):
        q_ref, kc_ref, kp_ref, vc_ref, vp_ref = refs[:5]
        o_ref, l_ref = refs[-2:]
        st = pl.program_id(0)
        low, up = _tri_masks()
        for sb in range(m):
            valid = jnp.concatenate([up & (st > 0) if sb == 0 else up, low], axis=1)
            for r in range(dil):
                base = sb * sbr + r
                q = _rows(q_ref, base, dil)
                kc, vc = _rows(kc_ref, base, dil), _rows(vc_ref, base, dil)
                if sb == 0:
                    kp, vp = _rows(kp_ref, r, dil), _rows(vp_ref, r, dil)
                else:
                    kp, vp = _rows(kc_ref, base - sbr, dil), _rows(vc_ref, base - sbr, dil)
                outs, lses = [], []
                for hh in range(2):
                    sl = slice(hh * HEAD_DIM, (hh + 1) * HEAD_DIM)
                    k2 = jnp.concatenate([_mx(kp[:, sl]), _mx(kc[:, sl])], axis=0)
                    v2 = jnp.concatenate([_mx(vp[:, sl]), _mx(vc[:, sl])], axis=0)
                    s = jnp.where(valid, _dot_nt(_mx(q[:, sl]), k2) * ATTN_SCALE, NEG_BIG)
                    mx = jnp.max(s, axis=-1, keepdims=True)
                    e = jnp.exp(s - mx)
                    l = jnp.sum(e, axis=-1, keepdims=True)
                    outs.append(_dot(_mx(e / l), v2))
                    lses.append(jnp.broadcast_to(mx + jnp.log(l), (BLK, HEAD_DIM)))
                _set_rows(o_ref, base, dil, jnp.concatenate(outs, axis=1))
                _set_rows(l_ref, base, dil, jnp.concatenate(lses, axis=1))

    def cur(col):
        return pl.BlockSpec((rows, 128), lambda st, hp: (st, col + 2 * g + hp))

    def before(col):
        return pl.BlockSpec((sbr, 128), lambda st, hp: (jnp.maximum(st * m - 1, 0), col + 2 * g + hp))

    in_specs = [cur(Q_COL), cur(K_COL), before(K_COL), cur(V_COL), before(V_COL)]
    args = [z, z, z, z, z]
    aliases = {}
    if prev is not None:
        in_specs += [pl.BlockSpec(memory_space=pl.ANY)] * 2
        args += list(prev)
        aliases = {5: 0, 6: 1}
    return pl.pallas_call(
        body, name=name, grid=(s_len // rows, 2), in_specs=in_specs, out_specs=[cur(0), cur(0)],
        out_shape=[jax.ShapeDtypeStruct((s_len, ATTN_WIDTH), f32)] * 2, input_output_aliases=aliases,
        compiler_params=_cparams(("parallel", "parallel")),
    )(*args)


def _attn_bwd(z, do, lse, dlt, tabs, dz, g, name):
    s_len = z.shape[0]
    dil, m = DILATIONS[g], ATTN_SUPER_PER_STEP[g]
    sbr = BLK * dil
    rows = sbr * m
    nsteps = s_len // rows

    def body(q_ref, qn_ref, kc_ref, kp_ref, vc_ref, vp_ref, do_ref, don_ref, l_ref, ln_ref, d_ref, dn_ref,
             c_ref, s1_ref, s2_ref, dz_in, dz_ref, dq_buf, dk_buf, dv_buf, sems):
        del dz_in
        st, hp = pl.program_id(0), pl.program_id(1)
        low, up = _tri_masks()
        for sb in range(m):
            up_prev = up & (st > 0) if sb == 0 else up
            up_next = up & (st < nsteps - 1) if sb == m - 1 else up
            for r in range(dil):
                base = sb * sbr + r
                q, k, v = _rows(q_ref, base, dil), _rows(kc_ref, base, dil), _rows(vc_ref, base, dil)
                do_c, l_c, d_c = _rows(do_ref, base, dil), _rows(l_ref, base, dil), _rows(d_ref, base, dil)
                if sb == 0:
                    kp, vp = _rows(kp_ref, r, dil), _rows(vp_ref, r, dil)
                else:
                    kp, vp = _rows(kc_ref, base - sbr, dil), _rows(vc_ref, base - sbr, dil)
                if sb == m - 1:
                    qn, do_n = _rows(qn_ref, r, dil), _rows(don_ref, r, dil)
                    l_n, d_n = _rows(ln_ref, r, dil), _rows(dn_ref, r, dil)
                else:
                    qn, do_n = _rows(q_ref, base + sbr, dil), _rows(do_ref, base + sbr, dil)
                    l_n, d_n = _rows(l_ref, base + sbr, dil), _rows(d_ref, base + sbr, dil)
                dqs, dks, dvs = [], [], []
                for hh in range(2):
                    sl = slice(hh * HEAD_DIM, (hh + 1) * HEAD_DIM)
                    one = slice(hh * HEAD_DIM, hh * HEAD_DIM + 1)
                    qc, qx = _mx(q[:, sl]), _mx(qn[:, sl])
                    kc, kb = _mx(k[:, sl]), _mx(kp[:, sl])
                    vc, vb = _mx(v[:, sl]), _mx(vp[:, sl])
                    doc, dox = _mx(do_c[:, sl]), _mx(do_n[:, sl])
                    lc, lx, dc, dx = l_c[:, one], l_n[:, one], d_c[:, one], d_n[:, one]
                    p_a = jnp.where(low, jnp.exp(_dot_nt(qc, kc) * ATTN_SCALE - lc), 0.0)
                    ds_a = _mx(p_a * (_dot_nt(doc, vc) - dc) * ATTN_SCALE)
                    p_b = jnp.where(up_prev, jnp.exp(_dot_nt(qc, kb) * ATTN_SCALE - lc), 0.0)
                    ds_b = _mx(p_b * (_dot_nt(doc, vb) - dc) * ATTN_SCALE)
                    p_c = jnp.where(up_next, jnp.exp(_dot_nt(qx, kc) * ATTN_SCALE - lx), 0.0)
                    ds_c = _mx(p_c * (_dot_nt(dox, vc) - dx) * ATTN_SCALE)
                    dqs.append(_dot(ds_a, kc) + _dot(ds_b, kb))
                    dks.append(_dot_tn(ds_a, qc) + _dot_tn(ds_c, qx))
                    dvs.append(_dot_tn(_mx(p_a), doc) + _dot_tn(_mx(p_c), dox))
                c, s1, s2 = _rows(c_ref, base, dil), _rows(s1_ref, base, dil), _rows(s2_ref, base, dil)
                _set_rows(dq_buf, base, dil, _rope_transpose(jnp.concatenate(dqs, axis=1), c, s1, s2, 128))
                _set_rows(dk_buf, base, dil, _rope_transpose(jnp.concatenate(dks, axis=1), c, s1, s2, 128))
                _set_rows(dv_buf, base, dil, jnp.concatenate(dvs, axis=1))
        copies = []
        for t, (buf, col) in enumerate(((dq_buf, Q_COL), (dk_buf, K_COL), (dv_buf, V_COL))):
            lane0 = pl.multiple_of((col + 2 * g + hp) * 128, 128)
            dst = dz_ref.at[pl.ds(pl.multiple_of(st * rows, rows), rows), pl.ds(lane0, 128)]
            cp = pltpu.make_async_copy(buf, dst, sems.at[t])
            cp.start()
            copies.append(cp)
        for cp in copies:
            cp.wait()

    def cur(col):
        return pl.BlockSpec((rows, 128), lambda st, hp: (st, col + 2 * g + hp))

    def before(col):
        return pl.BlockSpec((sbr, 128), lambda st, hp: (jnp.maximum(st * m - 1, 0), col + 2 * g + hp))

    def after(col):
        return pl.BlockSpec((sbr, 128), lambda st, hp: (jnp.minimum((st + 1) * m, s_len // sbr - 1), col + 2 * g + hp))

    tab = pl.BlockSpec((rows, 128), lambda st, hp: (st, 0))
    return pl.pallas_call(
        body, name=name, grid=(nsteps, 2),
        in_specs=[cur(Q_COL), after(Q_COL), cur(K_COL), before(K_COL), cur(V_COL), before(V_COL),
                  cur(0), after(0), cur(0), after(0), cur(0), after(0), tab, tab, tab,
                  pl.BlockSpec(memory_space=pl.ANY)],
        out_specs=pl.BlockSpec(memory_space=pl.ANY),
        out_shape=jax.ShapeDtypeStruct(dz.shape, dz.dtype), input_output_aliases={15: 0},
        scratch_shapes=[pltpu.VMEM((rows, 128), f32)] * 3 + [pltpu.SemaphoreType.DMA((3,))],
        compiler_params=_cparams(("arbitrary", "arbitrary")),
    )(z, z, z, z, z, z, do, do, lse, lse, dlt, dlt, *tabs, dz)


def _attn_fwd_old(q, k, v, dil, name):
    s_len = q.shape[0]
    nblk = s_len // BLK

    def body(q_ref, kc_ref, kp_ref, vc_ref, vp_ref, o_ref, l_ref):
        b = pl.program_id(0)
        has_prev = b >= dil
        low, up = _tri_masks()
        valid = jnp.concatenate([up & has_prev, low], axis=1)
        outs, lses = [], []
        for hh in range(2):
            sl = slice(hh * HEAD_DIM, (hh + 1) * HEAD_DIM)
            qh = _mx(q_ref[:, sl])
            k2 = jnp.concatenate([_mx(kp_ref[:, sl]), _mx(kc_ref[:, sl])], axis=0)
            v2 = jnp.concatenate([_mx(vp_ref[:, sl]), _mx(vc_ref[:, sl])], axis=0)
            s = jnp.where(valid, _dot_nt(qh, k2) * ATTN_SCALE, NEG_BIG)
            m = jnp.max(s, axis=-1, keepdims=True)
            e = jnp.exp(s - m)
            l = jnp.sum(e, axis=-1, keepdims=True)
            outs.append(_dot(_mx(e / l), v2))
            lses.append(jnp.broadcast_to(m + jnp.log(l), (BLK, HEAD_DIM)))
        o_ref[...] = jnp.concatenate(outs, axis=1)
        l_ref[...] = jnp.concatenate(lses, axis=1)

    cur = pl.BlockSpec((BLK, 128), lambda b, hp: (b, hp))
    prev = pl.BlockSpec((BLK, 128), lambda b, hp: (jnp.maximum(b - dil, 0), hp))
    return pl.pallas_call(
        body, name=name, grid=(nblk, 2), in_specs=[cur, cur, prev, cur, prev], out_specs=[cur, cur],
        out_shape=[jax.ShapeDtypeStruct((s_len, 256), f32)] * 2,
        compiler_params=_cparams(("parallel", "parallel")),
    )(q, k, k, v, v)


def _attn_bwd_old(q, k, v, do, lse, dlt, tabs, dil, name):
    s_len = q.shape[0]
    nblk = s_len // BLK

    def body(q_ref, qn_ref, kc_ref, kp_ref, vc_ref, vp_ref, do_ref, don_ref, l_ref, ln_ref, d_ref, dn_ref,
             c_ref, s1_ref, s2_ref, dq_ref, dk_ref, dv_ref):
        b = pl.program_id(0)
        has_prev = b >= dil
        has_next = b + dil < nblk
        low, up = _tri_masks()
        dqs, dks, dvs = [], [], []
        for hh in range(2):
            sl = slice(hh * HEAD_DIM, (hh + 1) * HEAD_DIM)
            one = slice(hh * HEAD_DIM, hh * HEAD_DIM + 1)
            qc, qn = _mx(q_ref[:, sl]), _mx(qn_ref[:, sl])
            kc, kp = _mx(kc_ref[:, sl]), _mx(kp_ref[:, sl])
            vc, vp = _mx(vc_ref[:, sl]), _mx(vp_ref[:, sl])
            doc, don = _mx(do_ref[:, sl]), _mx(don_ref[:, sl])
            lc, ln = l_ref[:, one], ln_ref[:, one]
            dc, dn = d_ref[:, one], dn_ref[:, one]
            p_a = jnp.where(low, jnp.exp(_dot_nt(qc, kc) * ATTN_SCALE - lc), 0.0)
            ds_a = _mx(p_a * (_dot_nt(doc, vc) - dc) * ATTN_SCALE)
            p_b = jnp.where(up & has_prev, jnp.exp(_dot_nt(qc, kp) * ATTN_SCALE - lc), 0.0)
            ds_b = _mx(p_b * (_dot_nt(doc, vp) - dc) * ATTN_SCALE)
            p_c = jnp.where(up & has_next, jnp.exp(_dot_nt(qn, kc) * ATTN_SCALE - ln), 0.0)
            ds_c = _mx(p_c * (_dot_nt(don, vc) - dn) * ATTN_SCALE)
            dqs.append(_dot(ds_a, kc) + _dot(ds_b, kp))
            dks.append(_dot_tn(ds_a, qc) + _dot_tn(ds_c, qn))
            dvs.append(_dot_tn(_mx(p_a), doc) + _dot_tn(_mx(p_c), don))
        c, s1, s2 = c_ref[...], s1_ref[...], s2_ref[...]
        dq_ref[...] = _rope_transpose(jnp.concatenate(dqs, axis=1), c, s1, s2, 128)
        dk_ref[...] = _rope_transpose(jnp.concatenate(dks, axis=1), c, s1, s2, 128)
        dv_ref[...] = jnp.concatenate(dvs, axis=1)

    cur = pl.BlockSpec((BLK, 128), lambda b, hp: (b, hp))
    prev = pl.BlockSpec((BLK, 128), lambda b, hp: (jnp.maximum(b - dil, 0), hp))
    nxt = pl.BlockSpec((BLK, 128), lambda b, hp: (jnp.minimum(b + dil, nblk - 1), hp))
    tab = pl.BlockSpec((BLK, 128), lambda b, hp: (b, 0))
    return pl.pallas_call(
        body, name=name, grid=(nblk, 2),
        in_specs=[cur, nxt, cur, prev, cur, prev, cur, nxt, cur, nxt, cur, nxt, tab, tab, tab],
        out_specs=[cur, cur, cur], out_shape=[jax.ShapeDtypeStruct((s_len, 256), f32)] * 3,
        compiler_params=_cparams(("parallel", "parallel")),
    )(q, q, k, k, v, v, do, do, lse, lse, dlt, dlt, *tabs)


def _loss_head(h, g, target, name):
    s_len, d = h.shape

    def body(h_ref, g_ref, t_ref, loss_ref, dh_ref, dg_ref):
        i = pl.program_id(0)
        x = h_ref[...]
        gv = g_ref[...]
        r = lax.rsqrt(jnp.mean(x * x, axis=-1, keepdims=True) + EPS)
        xh = x * r
        diff = xh * gv - t_ref[...]
        part = 0.5 * jnp.sum(jnp.mean(diff * diff, axis=-1, keepdims=True), axis=0, keepdims=True)
        dy = diff * (1.0 / d)
        dxh = dy * gv
        dh_ref[...] = r * (dxh - xh * jnp.mean(dxh * xh, axis=-1, keepdims=True))
        dgsum = jnp.sum(dy * xh, axis=0, keepdims=True)
        lossb = jnp.broadcast_to(part, (8, 128))

        @pl.when(i == 0)
        def _():
            loss_ref[...] = lossb
            dg_ref[...] = dgsum

        @pl.when(i > 0)
        def _():
            loss_ref[...] += lossb
            dg_ref[...] += dgsum

    row = lambda i: (i, 0)
    return pl.pallas_call(
        body, name=name, grid=(s_len // TM,),
        in_specs=[pl.BlockSpec((TM, d), row), pl.BlockSpec((1, d), lambda i: (0, 0)), pl.BlockSpec((TM, d), row)],
        out_specs=[pl.BlockSpec((8, 128), lambda i: (0, 0)), pl.BlockSpec((TM, d), row),
                   pl.BlockSpec((1, d), lambda i: (0, 0))],
        out_shape=[jax.ShapeDtypeStruct((8, 128), f32), jax.ShapeDtypeStruct((s_len, d), f32),
                   jax.ShapeDtypeStruct((1, d), f32)],
        compiler_params=_cparams(("arbitrary",)),
    )(h, g, target)


def _rope_tables(positions):
    inv_freq = ROPE_THETA ** (-jnp.arange(0, ROT_DIM, 2, dtype=f32) / ROT_DIM)
    ang = positions.astype(f32)[:, None] * inv_freq
    cos, sin = jnp.cos(ang), jnp.sin(ang)
    s_len = positions.shape[0]
    zero8, rest = jnp.zeros((s_len, 8), f32), jnp.zeros((s_len, HEAD_DIM - ROT_DIM), f32)
    c = jnp.concatenate([cos, cos, jnp.ones((s_len, HEAD_DIM - ROT_DIM), f32)], axis=1)
    s1 = jnp.concatenate([-sin, zero8, rest], axis=1)
    s2 = jnp.concatenate([zero8, sin, rest], axis=1)
    return c, s1, s2


def _block_diag(pool_w):
    out = jnp.zeros((POOL_WIDTH, POOL_WIDTH), pool_w.dtype)
    for g in range(4):
        out = lax.dynamic_update_slice(out, pool_w[g], (g * POOL_GC, g * POOL_GC))
    return out


class _ReadyWeights:
    def __init__(self, full):
        self.full = full

    def take(self, layer, names, after):
        del after
        return {n: self.full[n] for n in names}, layer


def _layer_fwd(h, p_l, wsrc, small, layer, tabs):
    nm = f"l{layer}_"
    wts, wl = wsrc.take(layer, ("w_in",), h if layer else None)
    z, hn1 = _norm_matmul(h, small["norm1"][layer][None], wts["w_in"], wl, 256, nm + "in_proj", rope=tabs)
    ol = None
    for g in range(3):
        ol = _attn_fwd(z, g, ol, nm + f"attn_fwd{g}")
    outs, lses = ol
    wbd = _mx(_block_diag(small["pool_w"][layer]))
    scale = small["pool_scale"][layer][None]
    m = _mixer_merge(z, wbd, scale, outs, lses, nm + "mixer_merge")
    rest, _ = wsrc.take(layer, ("w_out", "w_up", "w_down", "w_gate", "w_ple"), m)
    wts = {**wts, **rest}
    h1 = _matmul_residual(m, wts["w_out"], wl, h, nm + "out_proj")
    a, hn2 = _norm_matmul(h1, small["norm2"][layer][None], wts["w_up"], wl, 1024, nm + "up_proj")
    h2 = _matmul_residual(a, wts["w_down"], wl, h1, nm + "down_proj", act=True)
    h3, gl, hn3 = _gate_ple_fwd(h2, small["norm3"][layer][None], wts["w_gate"], wts["w_ple"], wl, p_l, nm + "gate_ple")
    saved = dict(h=h, z=z, hn1=hn1, outs=outs, lses=lses, wbd=wbd, scale=scale, m=m, h1=h1, a=a, hn2=hn2, h2=h2,
                 gl=gl, hn3=hn3, wts=wts, wl=wl)
    return h3, saved


def _layer_bwd(dh3, sv, p_l, small, layer, tabs128, reducer):
    nm = f"l{layer}_"
    wts, wl = sv["wts"], sv["wl"]
    de, dgl = _gate_ple_bwd(dh3, sv["gl"], p_l, wts["w_ple"], wl, nm + "gate_ple_bwd")
    reducer.add("w_gate", layer, _weight_grad(sv["hn3"], dgl, nm + "dw_gate"))
    reducer.add("w_ple", layer, _weight_grad(p_l, de, nm + "dw_ple"))
    dh2, dg3 = _matmul_nt_norm_bwd(dgl, wts["w_gate"], wl, sv["h2"], small["norm3"][layer][None], dh3, nm + "gate_bwd")
    act, da = _down_bwd(dh2, wts["w_down"], wl, sv["a"], nm + "down_bwd")
    reducer.add("w_down", layer, _weight_grad(act, dh2, nm + "dw_down"))
    started = reducer.add("w_up", layer, _weight_grad(sv["hn2"], da, nm + "dw_up"))
    dh1, dg2 = _matmul_nt_norm_bwd(da, wts["w_up"], wl, sv["h1"], small["norm2"][layer][None], dh2, nm + "up_bwd",
                                   after=started)
    dm = _matmul_nt(dh1, wts["w_out"], wl, nm + "out_bwd")
    reducer.add("w_out", layer, _weight_grad(sv["m"], dh1, nm + "dw_out"))
    do, dlt = _combine_bwd(dm, sv["outs"], sv["lses"], nm + "combine_bwd")
    dz, dwbd, dscale = _pool_bwd(sv["z"], dm, sv["wbd"], sv["scale"], nm + "pool_bwd")
    for g in range(3):
        dz = _attn_bwd(sv["z"], do, sv["lses"], dlt, tabs128, dz, g, nm + f"attn_bwd{g}")
    started = reducer.add("w_in", layer, _weight_grad(sv["hn1"], dz, nm + "dw_in"))
    dh0, dg1 = _matmul_nt_norm_bwd(dz, wts["w_in"], wl, sv["h"], small["norm1"][layer][None], dh1, nm + "in_bwd",
                                   tk=512, after=started)
    dpool_w = jnp.stack([dwbd[g * POOL_GC:(g + 1) * POOL_GC, g * POOL_GC:(g + 1) * POOL_GC] for g in range(4)])
    sg = dict(norm1=dg1[0], norm2=dg2[0], norm3=dg3[0], pool_w=dpool_w, pool_scale=dscale[0])
    return dh0, sg


class _CollectGrads:
    def __init__(self):
        self.grads = {}

    def add(self, name, layer, dw):
        self.grads[(name, layer)] = dw


def _local_step(x, p, positions, wsrc, small, target, reducer):
    tabs128 = tuple(jnp.tile(t, (1, 2)) for t in _rope_tables(positions))
    h = x
    saved = []
    for layer in range(2):
        h, sv = _layer_fwd(h, p[layer], wsrc, small, layer, tabs128)
        saved.append(sv)
    loss, dh, dgf = _loss_head(h, small["final_norm"][None], target, "loss_head")
    sgs = [None, None]
    for layer in (1, 0):
        dh, sgs[layer] = _layer_bwd(dh, saved[layer], p[layer], small, layer, tabs128, reducer)
    small_grads = {k: jnp.stack([sgs[0][k], sgs[1][k]]) for k in sgs[0]}
    small_grads["final_norm"] = dgf[0]
    return loss, dh, small_grads


HBM = pl.BlockSpec(memory_space=pltpu.HBM)


def _my_place():
    return lax.axis_index("x"), lax.axis_index("y"), lax.axis_index("c")


def _other_chips(x, y):
    return [(1 - x, y), (x, 1 - y), (1 - x, 1 - y)]


def _window(ref, name, chip):
    k, n = _shard_shape(name)
    if COL_SHARDED[name]:
        return ref.at[:, pl.ds(pl.multiple_of(chip * n, 128), n)]
    return ref.at[pl.ds(pl.multiple_of(chip * k, 128), k), :]


def _chip_index():
    return jnp.reshape(2 * lax.axis_index("x") + lax.axis_index("y"), (1,)).astype(jnp.int32)


def _shard_block(name, tr):
    ks, ns = _shard_shape(name)
    if COL_SHARDED[name]:
        return (tr, ns), lambda i, me: (i, me[0])
    return (tr, ns), lambda i, me: (me[0] * (ks // tr) + i, 0)


def _place_shard(w, name, layer):
    ks, ns = _shard_shape(name)
    tr = min(ks, 256)
    shape, index = _shard_block(name, tr)

    def body(me_ref, w_ref, o_ref):
        o_ref[...] = w_ref[...].astype(o_ref.dtype)

    return pl.pallas_call(
        body, name=f"place_{name}{layer}",
        grid_spec=pltpu.PrefetchScalarGridSpec(
            num_scalar_prefetch=1, grid=(ks // tr,),
            in_specs=[pl.BlockSpec((None, tr, ns), lambda i, me: (layer, i, 0))],
            out_specs=pl.BlockSpec((None,) + shape, lambda i, me: (0,) + index(i, me))),
        out_shape=jax.ShapeDtypeStruct((1,) + FULL_SHAPE[name], MXU_DTYPE),
        compiler_params=_cparams(("parallel",)),
    )(_chip_index(), w)


GATHER_ORDER = [("w_in", 0), ("w_out", 0), ("w_up", 0), ("w_down", 0), ("w_gate", 0), ("w_ple", 0),
                ("w_in", 1), ("w_out", 1), ("w_up", 1), ("w_down", 1), ("w_gate", 1), ("w_ple", 1)]
SEM = pl.BlockSpec(memory_space=pltpu.SEMAPHORE)
EFFECT = pltpu.SideEffectType.DATAFLOW_SIDE_EFFECTING


def _gather_copy(src_ref, dst_ref, name, idx, j, chip, send_sems, recv_sems, c):
    cx, cy = chip
    return pltpu.make_async_remote_copy(
        src_ref=src_ref, dst_ref=dst_ref, send_sem=send_sems.at[3 * idx + j], recv_sem=recv_sems.at[3 * idx + j],
        device_id=(cx, cy, c), device_id_type=MESH)


def _gather_start(placed, order, tag, after=None):
    n = len(order)
    extra = [] if after is None else [after]

    def body(*refs):
        ins = refs[:n]
        k = n + len(extra)
        send_sems, recv_sems = refs[k], refs[k + 1]
        outs = refs[k + 2:k + 2 + n]
        token = refs[-1]
        x, y, c = _my_place()
        me = 2 * x + y
        for idx, (name, _) in enumerate(order):
            for j, chip in enumerate(_other_chips(x, y)):
                _gather_copy(_window(ins[idx].at[0], name, me), _window(outs[idx].at[0], name, me), name, idx, j, chip,
                             send_sems, recv_sems, c).start()
        token[...] = jnp.zeros_like(token)

    res = pl.pallas_call(
        body, name="gather_start" + tag,
        out_shape=(pltpu.SemaphoreType.DMA((3 * n,)), pltpu.SemaphoreType.DMA((3 * n,)))
        + tuple(pltpu.HBM(a.shape, a.dtype) for a in placed) + (jax.ShapeDtypeStruct((8, 128), f32),),
        in_specs=[HBM] * n + [pl.BlockSpec(memory_space=pl.ANY)] * len(extra),
        out_specs=(SEM, SEM) + (HBM,) * n + (pl.BlockSpec(memory_space=pltpu.VMEM),),
        input_output_aliases={i: i + 2 for i in range(n)},
        compiler_params=pltpu.CompilerParams(has_side_effects=EFFECT),
    )(*[pltpu.with_memory_space_constraint(a, pltpu.HBM) for a in placed], *extra)
    return res[0], res[1], list(res[2:2 + n]), res[-1]


def _gather_wait(send_sems, recv_sems, arrays, order, idxs, after, name):
    n = len(idxs)

    def body(*refs):
        ins = refs[:n]
        send_ref, recv_ref = refs[n], refs[n + 1]
        x, y, c = _my_place()
        me = 2 * x + y
        for k, idx in enumerate(idxs):
            wname = order[idx][0]
            for j, chip in enumerate(_other_chips(x, y)):
                cx, cy = chip
                mine = _window(ins[k].at[0], wname, me)
                land = _window(ins[k].at[0], wname, 2 * cx + cy)
                _gather_copy(mine, mine, wname, idx, j, chip, send_ref, recv_ref, c).wait_send()
                _gather_copy(land, land, wname, idx, j, chip, send_ref, recv_ref, c).wait_recv()

    operands = list(arrays) + [send_sems, recv_sems]
    in_specs = [HBM] * n + [SEM, SEM]
    if after is not None:
        operands.append(after)
        in_specs.append(pl.BlockSpec(memory_space=pl.ANY))
    res = pl.pallas_call(
        body, name=name, out_shape=tuple(pltpu.HBM(a.shape, a.dtype) for a in arrays),
        in_specs=in_specs, out_specs=(HBM,) * n, input_output_aliases={i: i for i in range(n)},
        compiler_params=pltpu.CompilerParams(has_side_effects=EFFECT),
    )(*operands)
    return list(res)


class _GatheredWeights:
    def __init__(self, shards):
        self.starts = []
        token = None
        for tag, order in (("_first", GATHER_ORDER[:1]), ("_rest", GATHER_ORDER[1:])):
            placed = [_place_shard(shards[name], name, layer) for name, layer in order]
            self.starts.append((order,) + _gather_start(placed, order, tag, token))
            token = self.starts[-1][-1]

    def take(self, layer, names, after):
        order, send, recv, arrays, _ = next(s for s in self.starts if (names[0], layer) in s[0])
        if after is None:
            after = self.starts[-1][-1]
        idxs = [order.index((n, layer)) for n in names]
        got = _gather_wait(send, recv, [arrays[i] for i in idxs], order, idxs, after, f"gather_wait{layer}_{names[0]}")
        return dict(zip(names, got)), 0


def _gather_weights(full):
    names = list(BIG)

    def body(*refs):
        ins = refs[:len(names)]
        outs = refs[len(names):2 * len(names)]
        send_ici, recv_ici, send_d2d, recv_d2d = refs[2 * len(names):]
        x, y, c = _my_place()
        me = 2 * x + y
        sibling = (x, y, 1 - c)
        chips = _other_chips(x, y)
        ici = []
        for t, name in enumerate(names):
            for j, (cx, cy) in enumerate(chips):
                cp = pltpu.make_async_remote_copy(
                    src_ref=_window(ins[t].at[c], name, me), dst_ref=_window(outs[t].at[c], name, me),
                    send_sem=send_ici.at[3 * t + j], recv_sem=recv_ici.at[3 * t + j],
                    device_id=(cx, cy, c), device_id_type=MESH)
                cp.start()
                ici.append(cp)
        fwd = []
        for t, name in enumerate(names):
            for j, (cx, cy) in enumerate(chips):
                land = _window(outs[t].at[c], name, 2 * cx + cy)
                pltpu.make_async_remote_copy(
                    src_ref=land, dst_ref=land, send_sem=send_ici.at[3 * t + j], recv_sem=recv_ici.at[3 * t + j],
                    device_id=(cx, cy, c), device_id_type=MESH).wait_recv()
                cp = pltpu.make_async_remote_copy(
                    src_ref=land, dst_ref=land, send_sem=send_d2d.at[3 * t + j], recv_sem=recv_d2d.at[3 * t + j],
                    device_id=sibling, device_id_type=MESH)
                cp.start()
                fwd.append(cp)
        for t, name in enumerate(names):
            for j, (cx, cy) in enumerate(chips):
                land = _window(outs[t].at[1 - c], name, 2 * cx + cy)
                pltpu.make_async_remote_copy(
                    src_ref=land, dst_ref=land, send_sem=send_d2d.at[3 * t + j], recv_sem=recv_d2d.at[3 * t + j],
                    device_id=sibling, device_id_type=MESH).wait_recv()
        for cp in ici + fwd:
            cp.wait_send()

    nsem = 3 * len(names)
    outs = pl.pallas_call(
        body, name="gather_weights",
        in_specs=[HBM] * len(names), out_specs=[HBM] * len(names),
        out_shape=[jax.ShapeDtypeStruct(full[n].shape, full[n].dtype) for n in names],
        input_output_aliases={t: t for t in range(len(names))},
        scratch_shapes=[pltpu.SemaphoreType.DMA((nsem,)), pltpu.SemaphoreType.DMA((nsem,)),
                        pltpu.SemaphoreType.DMA((nsem,)), pltpu.SemaphoreType.DMA((nsem,))],
    )(*[full[n] for n in names])
    return dict(zip(names, outs))


def _swap_layers(grads):
    names = list(BIG)

    def body(*refs):
        ins = refs[:len(names)]
        outs = refs[len(names):2 * len(names)]
        send_sems, recv_sems = refs[2 * len(names):]
        x, y, c = _my_place()
        sibling = (x, y, 1 - c)
        cps = []
        for t in range(len(names)):
            cp = pltpu.make_async_remote_copy(
                src_ref=ins[t].at[1 - c], dst_ref=outs[t], send_sem=send_sems.at[t], recv_sem=recv_sems.at[t],
                device_id=sibling, device_id_type=MESH)
            cp.start()
            cps.append(cp)
        for cp in cps:
            cp.wait()

    outs = pl.pallas_call(
        body, name="swap_layers", in_specs=[HBM] * len(names), out_specs=[HBM] * len(names),
        out_shape=[jax.ShapeDtypeStruct(FULL_SHAPE[n], f32) for n in names],
        scratch_shapes=[pltpu.SemaphoreType.DMA((len(names),)), pltpu.SemaphoreType.DMA((len(names),))],
    )(*[grads[n] for n in names])
    return dict(zip(names, outs))


def _chip_sum(grad, other, name):
    k, n = FULL_SHAPE[name]
    tr = min(k, 512)
    c = lax.axis_index("c")

    def body(c_ref, g_ref, o_ref, out_ref):
        out_ref[...] = (g_ref[...] + o_ref[...]).astype(out_ref.dtype)

    return pl.pallas_call(
        body, name="chip_sum_" + name,
        grid_spec=pltpu.PrefetchScalarGridSpec(
            num_scalar_prefetch=1, grid=(k // tr,),
            in_specs=[pl.BlockSpec((None, tr, n), lambda i, c_ref: (c_ref[0], i, 0)),
                      pl.BlockSpec((tr, n), lambda i, c_ref: (i, 0))],
            out_specs=pl.BlockSpec((tr, n), lambda i, c_ref: (i, 0))),
        out_shape=jax.ShapeDtypeStruct((k, n), COMM_DTYPE),
        compiler_params=_cparams(("parallel",)),
    )(jnp.reshape(c, (1,)).astype(jnp.int32), grad, other)


def _scatter_shards(sums):
    names = list(BIG)

    def body(*refs):
        ins = refs[:len(names)]
        outs = refs[len(names):2 * len(names)]
        send_sems, recv_sems = refs[2 * len(names):]
        x, y, c = _my_place()
        me = 2 * x + y
        chips = _other_chips(x, y)
        cps = []
        for t, name in enumerate(names):
            for j, (cx, cy) in enumerate(chips):
                cp = pltpu.make_async_remote_copy(
                    src_ref=_window(ins[t], name, 2 * cx + cy), dst_ref=outs[t].at[me],
                    send_sem=send_sems.at[3 * t + j], recv_sem=recv_sems.at[3 * t + j],
                    device_id=(cx, cy, c), device_id_type=MESH)
                cp.start()
                cps.append(cp)
        for t, name in enumerate(names):
            for j, (cx, cy) in enumerate(chips):
                land = outs[t].at[2 * cx + cy]
                pltpu.make_async_remote_copy(
                    src_ref=land, dst_ref=land, send_sem=send_sems.at[3 * t + j], recv_sem=recv_sems.at[3 * t + j],
                    device_id=(cx, cy, c), device_id_type=MESH).wait_recv()
        for cp in cps:
            cp.wait_send()

    nsem = 3 * len(names)
    outs = pl.pallas_call(
        body, name="scatter_shards", in_specs=[HBM] * len(names), out_specs=[HBM] * len(names),
        out_shape=[jax.ShapeDtypeStruct((N_CHIPS,) + _shard_shape(n), sums[n].dtype) for n in names],
        scratch_shapes=[pltpu.SemaphoreType.DMA((nsem,)), pltpu.SemaphoreType.DMA((nsem,))],
    )(*[sums[n] for n in names])
    return dict(zip(names, outs))


def _sum_slots(slots, own, name):
    ks, ns = _shard_shape(name)
    tr = min(ks, 256)
    shape, index = _shard_block(name, tr)

    def body(me_ref, c_ref, s_ref, own_ref, out_ref):
        me = me_ref[0]
        acc = None
        for s in range(N_CHIPS):
            term = jnp.where(me == s, own_ref[...], s_ref[s]).astype(f32)
            acc = term if acc is None else acc + term
        out_ref[...] = acc

    return pl.pallas_call(
        body, name="sum_slots_" + name,
        grid_spec=pltpu.PrefetchScalarGridSpec(
            num_scalar_prefetch=2, grid=(ks // tr,),
            in_specs=[pl.BlockSpec((N_CHIPS, tr, ns), lambda i, me, c: (0, i, 0)),
                      pl.BlockSpec(shape, lambda i, me, c: index(i, me))],
            out_specs=pl.BlockSpec((None, tr, ns), lambda i, me, c: (c[0], i, 0))),
        out_shape=jax.ShapeDtypeStruct((2, ks, ns), f32),
        compiler_params=_cparams(("parallel",)),
    )(_chip_index(), jnp.reshape(lax.axis_index("c"), (1,)).astype(jnp.int32), slots, own)


N_DEV = 8


def _reduce_copies(dws, lands, names, layer, send_sems, recv_sems):
    x, y, c = _my_place()
    me, my_dev = 2 * x + y, 4 * x + 2 * y + c
    out = []
    for t, name in enumerate(names):
        for j, (cx, cy) in enumerate(_other_chips(x, y)):
            out.append((pltpu.make_async_remote_copy(
                src_ref=_window(dws[t], name, 2 * cx + cy), dst_ref=lands[t].at[my_dev],
                send_sem=send_sems.at[4 * t + j], recv_sem=recv_sems.at[N_DEV * t + my_dev],
                device_id=(cx, cy, layer), device_id_type=MESH), False))
        out.append((pltpu.make_async_remote_copy(
            src_ref=_window(dws[t], name, me), dst_ref=lands[t].at[my_dev],
            send_sem=send_sems.at[4 * t + 3], recv_sem=recv_sems.at[N_DEV * t + my_dev],
            device_id=(x, y, layer), device_id_type=MESH), True))
    return out


def _reduce_start(dws, names, layer, tag):
    n = len(names)
    lands = [lax.empty((N_DEV,) + _shard_shape(nm), dws[0].dtype) for nm in names]

    def body(*refs):
        ins = refs[:n]
        send_sems, recv_sems = refs[2 * n], refs[2 * n + 1]
        land_out = refs[3 * n + 2:4 * n + 2]
        token = refs[-1]
        c = lax.axis_index("c")
        for cp, non_owner_only in _reduce_copies(ins, land_out, names, layer, send_sems, recv_sems):
            if non_owner_only:
                @pl.when(c != layer)
                def _():
                    cp.start()
            else:
                cp.start()
        token[...] = jnp.zeros_like(token)

    res = pl.pallas_call(
        body, name="reduce_start" + tag,
        out_shape=(pltpu.SemaphoreType.DMA((4 * n,)), pltpu.SemaphoreType.DMA((N_DEV * n,)))
        + tuple(pltpu.HBM(a.shape, a.dtype) for a in dws) + tuple(pltpu.HBM(a.shape, a.dtype) for a in lands)
        + (jax.ShapeDtypeStruct((8, 128), f32),),
        in_specs=[HBM] * (2 * n),
        out_specs=(SEM, SEM) + (HBM,) * (2 * n) + (pl.BlockSpec(memory_space=pltpu.VMEM),),
        input_output_aliases={i: i + 2 for i in range(2 * n)},
        compiler_params=pltpu.CompilerParams(has_side_effects=EFFECT),
    )(*[pltpu.with_memory_space_constraint(a, pltpu.HBM) for a in list(dws) + lands])
    return res[0], res[1], list(res[2:2 + n]), list(res[2 + n:2 + 2 * n]), res[-1]


def _reduce_wait(send_sems, recv_sems, dws, lands, names, layer, after, tag):
    n = len(names)

    def body(*refs):
        ins, land_in = refs[:n], refs[n:2 * n]
        send_ref, recv_ref = refs[2 * n], refs[2 * n + 1]
        x, y, c = _my_place()
        for cp, non_owner_only in _reduce_copies(ins, land_in, names, layer, send_ref, recv_ref):
            if non_owner_only:
                @pl.when(c != layer)
                def _():
                    cp.wait_send()
            else:
                cp.wait_send()

        @pl.when(c == layer)
        def _():
            for t in range(n):
                for k in range(1, N_DEV):
                    px, py, pc = x ^ ((k >> 2) & 1), y ^ ((k >> 1) & 1), c ^ (k & 1)
                    dev = 4 * px + 2 * py + pc
                    land = land_in[t].at[dev]
                    pltpu.make_async_remote_copy(
                        src_ref=land, dst_ref=land, send_sem=send_ref.at[4 * t], recv_sem=recv_ref.at[N_DEV * t + dev],
                        device_id=(px, py, pc), device_id_type=MESH).wait_recv()

    res = pl.pallas_call(
        body, name="reduce_wait" + tag,
        out_shape=tuple(pltpu.HBM(a.shape, a.dtype) for a in list(dws) + list(lands)),
        in_specs=[HBM] * (2 * n) + [SEM, SEM, pl.BlockSpec(memory_space=pl.ANY)], out_specs=(HBM,) * (2 * n),
        input_output_aliases={i: i for i in range(2 * n)},
        compiler_params=pltpu.CompilerParams(has_side_effects=EFFECT),
    )(*dws, *lands, send_sems, recv_sems, after)
    return list(res[:n]), list(res[n:])


def _sum_devices(land, own, name, layer, prev):
    ks, ns = _shard_shape(name)
    tr = min(ks, 256)
    shape, index = _shard_block(name, tr)

    def body(me_ref, dev_ref, *refs):
        s_ref, own_ref, out_ref = refs[0], refs[1], refs[-1]
        dev = dev_ref[0]
        acc = None
        for s in range(N_DEV):
            term = jnp.where(dev == s, own_ref[...], s_ref[s]).astype(f32)
            acc = term if acc is None else acc + term
        out_ref[...] = acc

    in_specs = [pl.BlockSpec((N_DEV, tr, ns), lambda i, me, dev: (0, i, 0)),
                pl.BlockSpec(shape, lambda i, me, dev: index(i, me))]
    args = [land, own]
    aliases = {}
    if prev is not None:
        in_specs.append(pl.BlockSpec(memory_space=pl.ANY))
        args.append(prev)
        aliases = {4: 0}
    x, y, c = _my_place()
    return pl.pallas_call(
        body, name=f"sum_devices_{name}{layer}",
        grid_spec=pltpu.PrefetchScalarGridSpec(
            num_scalar_prefetch=2, grid=(ks // tr,), in_specs=in_specs,
            out_specs=pl.BlockSpec((None, tr, ns), lambda i, me, dev: (layer, i, 0))),
        out_shape=jax.ShapeDtypeStruct((2, ks, ns), f32), input_output_aliases=aliases,
        compiler_params=_cparams(("parallel",)),
    )(_chip_index(), jnp.reshape(4 * x + 2 * y + c, (1,)).astype(jnp.int32), *args)


class _GradReducer:
    GROUPS = (("1", 1, ("w_gate", "w_ple", "w_down", "w_up", "w_out", "w_in")),
              ("0a", 0, ("w_gate", "w_ple", "w_down", "w_up")),
              ("0b", 0, ("w_out", "w_in")))

    def __init__(self):
        self.grads = {}
        self.started = {}

    def add(self, name, layer, dw):
        self.grads[(name, layer)] = dw
        token = None
        for tag, glayer, names in self.GROUPS:
            if tag not in self.started and all((nm, glayer) in self.grads for nm in names):
                *self.started[tag], token = _reduce_start([self.grads[(nm, glayer)] for nm in names], names, glayer, tag)
        return token

    def finish(self, after):
        mine = {}
        for tag, layer, names in self.GROUPS:
            send, recv, dws, lands = self.started[tag]
            dws, lands = _reduce_wait(send, recv, dws, lands, names, layer, after, tag)
            for nm, dw, land in zip(names, dws, lands):
                mine[nm] = _sum_devices(land, dw, nm, layer, mine.get(nm))
        return _pair_layers(mine)


def _pair_layers(mine):
    names = list(BIG)

    def body(*refs):
        ins = refs[:len(names)]
        outs = refs[len(names):2 * len(names)]
        send_sems, recv_sems = refs[2 * len(names):]
        x, y, c = _my_place()
        sibling = (x, y, 1 - c)
        cps = []
        for t in range(len(names)):
            cp = pltpu.make_async_remote_copy(
                src_ref=ins[t].at[c], dst_ref=outs[t].at[c], send_sem=send_sems.at[t], recv_sem=recv_sems.at[t],
                device_id=sibling, device_id_type=MESH)
            cp.start()
            cps.append(cp)
        for t in range(len(names)):
            cps[t].wait_send()
            land = outs[t].at[1 - c]
            pltpu.make_async_remote_copy(
                src_ref=land, dst_ref=land, send_sem=send_sems.at[t], recv_sem=recv_sems.at[t],
                device_id=sibling, device_id_type=MESH).wait_recv()

    outs = pl.pallas_call(
        body, name="pair_layers", in_specs=[HBM] * len(names), out_specs=[HBM] * len(names),
        out_shape=[jax.ShapeDtypeStruct((2,) + _shard_shape(n), f32) for n in names],
        input_output_aliases={t: t for t in range(len(names))},
        scratch_shapes=[pltpu.SemaphoreType.DMA((len(names),)), pltpu.SemaphoreType.DMA((len(names),))],
    )(*[mine[n] for n in names])
    return dict(zip(names, outs))


SMALL_ROWS = 320


def _allreduce_small(vec):
    n_dev = 8

    def body(v_ref, out_ref, buf_ref, send_sems, recv_sems):
        x, y, c = _my_place()
        me = 4 * x + 2 * y + c
        buf_ref[me] = v_ref[...]
        cps = []
        for k in range(1, n_dev):
            dx, dy, dc = (k >> 2) & 1, (k >> 1) & 1, k & 1
            peer = (x ^ dx, y ^ dy, c ^ dc)
            cp = pltpu.make_async_remote_copy(
                src_ref=v_ref, dst_ref=buf_ref.at[me], send_sem=send_sems.at[k - 1], recv_sem=recv_sems.at[k - 1],
                device_id=peer, device_id_type=MESH)
            cp.start()
            cps.append(cp)
        for k in range(1, n_dev):
            dx, dy, dc = (k >> 2) & 1, (k >> 1) & 1, k & 1
            src = 4 * (x ^ dx) + 2 * (y ^ dy) + (c ^ dc)
            land = buf_ref.at[src]
            pltpu.make_async_remote_copy(
                src_ref=land, dst_ref=land, send_sem=send_sems.at[k - 1], recv_sem=recv_sems.at[k - 1],
                device_id=(x ^ dx, y ^ dy, c ^ dc), device_id_type=MESH).wait_recv()
        for cp in cps:
            cp.wait_send()
        acc = buf_ref[0]
        for s in range(1, n_dev):
            acc = acc + buf_ref[s]
        out_ref[...] = acc

    return pl.pallas_call(
        body, name="allreduce_small",
        in_specs=[pl.BlockSpec(memory_space=pltpu.VMEM)], out_specs=pl.BlockSpec(memory_space=pltpu.VMEM),
        out_shape=jax.ShapeDtypeStruct((SMALL_ROWS, 128), f32),
        scratch_shapes=[pltpu.VMEM((n_dev, SMALL_ROWS, 128), f32), pltpu.SemaphoreType.DMA((n_dev - 1,)),
                        pltpu.SemaphoreType.DMA((n_dev - 1,))],
    )(vec)


def _adamw(w, g, m, v, name):
    rows, cols = w.shape
    tr = rows
    for cand in (512, 256, 128, 64, 32, 16, 8):
        if rows % cand == 0 and cand * cols * 4 <= 2 * 1024 * 1024:
            tr = cand
            break
    c1 = np.float32(1.0 - ADAM_B1 ** ADAM_STEP)
    c2 = np.float32(1.0 - ADAM_B2 ** ADAM_STEP)

    def body(w_ref, g_ref, m_ref, v_ref, go_ref, d_ref, mo_ref, vo_ref):
        gv = g_ref[...]
        go_ref[...] = gv
        mn = ADAM_B1 * m_ref[...] + (1.0 - ADAM_B1) * gv
        vn = ADAM_B2 * v_ref[...] + (1.0 - ADAM_B2) * (gv * gv)
        mo_ref[...] = mn
        vo_ref[...] = vn
        d_ref[...] = -ADAM_LR * ((mn / c1) / (jnp.sqrt(vn / c2) + ADAM_EPS) + ADAM_WD * w_ref[...])

    blk = pl.BlockSpec((tr, cols), lambda i: (i, 0))
    return pl.pallas_call(
        body, name="adamw_" + name, grid=(rows // tr,), in_specs=[blk] * 4, out_specs=[blk] * 4,
        out_shape=[jax.ShapeDtypeStruct((rows, cols), f32)] * 4,
        compiler_params=_cparams(("parallel",)),
    )(w, g, m, v)


SMALL = ("norm1", "pool_w", "pool_scale", "norm2", "norm3", "final_norm")
ORDER = ("norm1", "w_in", "pool_w", "pool_scale", "w_out", "norm2", "w_up", "w_down", "norm3", "w_gate", "w_ple",
         "final_norm")


def _pack_small(tree, extra=None):
    parts = [tree[n].reshape(-1) for n in SMALL]
    if extra is not None:
        parts.append(extra.reshape(-1))
    flat = jnp.concatenate(parts)
    return jnp.pad(flat, (0, SMALL_ROWS * 128 - flat.shape[0])).reshape(SMALL_ROWS, 128)


def _unpack_small(packed, like):
    flat = packed.reshape(-1)
    out, off = {}, 0
    for n in SMALL:
        size = int(np.prod(like[n].shape))
        out[n] = flat[off:off + size].reshape(like[n].shape)
        off += size
    return out, flat[off]


def kernel(x, p, positions, norm1, w_in, pool_w, pool_scale, w_out, norm2, w_up, w_down, norm3, w_gate, w_ple, final_norm, loss_target, m_norm1, m_w_in, m_pool_w, m_pool_scale, m_w_out, m_norm2, m_w_up, m_w_down, m_norm3, m_w_gate, m_w_ple, m_final_norm, v_norm1, v_w_in, v_pool_w, v_pool_scale, v_w_out, v_norm2, v_w_up, v_w_down, v_norm3, v_w_gate, v_w_ple, v_final_norm):
    w = dict(norm1=norm1, w_in=w_in, pool_w=pool_w, pool_scale=pool_scale, w_out=w_out, norm2=norm2, w_up=w_up,
             w_down=w_down, norm3=norm3, w_gate=w_gate, w_ple=w_ple, final_norm=final_norm)
    m = dict(norm1=m_norm1, w_in=m_w_in, pool_w=m_pool_w, pool_scale=m_pool_scale, w_out=m_w_out, norm2=m_norm2,
             w_up=m_w_up, w_down=m_w_down, norm3=m_norm3, w_gate=m_w_gate, w_ple=m_w_ple, final_norm=m_final_norm)
    v = dict(norm1=v_norm1, w_in=v_w_in, pool_w=v_pool_w, pool_scale=v_pool_scale, w_out=v_w_out, norm2=v_norm2,
             w_up=v_w_up, w_down=v_w_down, norm3=v_norm3, w_gate=v_w_gate, w_ple=v_w_ple, final_norm=v_final_norm)
    small = {n: w[n] for n in SMALL}

    wsrc = _GatheredWeights({n: w[n] for n in BIG})
    reducer = _GradReducer()
    loss8, dx, small_grads = _local_step(x[0], p[:, 0], positions[0], wsrc, small, loss_target[0], reducer)
    gsh = reducer.finish(dx)

    red = _allreduce_small(_pack_small(small_grads, loss8[0, 0]))
    g_small, loss = _unpack_small(red, small)

    g_out, d_out, m_out, v_out = {}, {}, {}, {}
    for n in BIG:
        shp = w[n].shape
        two = lambda a: a.reshape(shp[0] * shp[1], shp[2])
        g2, d2, m2, v2 = _adamw(two(w[n]), two(gsh[n]), two(m[n]), two(v[n]), n)
        g_out[n], d_out[n], m_out[n], v_out[n] = g2.reshape(shp), d2.reshape(shp), m2.reshape(shp), v2.reshape(shp)
    _, d2, m2, v2 = _adamw(_pack_small(small), red, _pack_small({n: m[n] for n in SMALL}),
                           _pack_small({n: v[n] for n in SMALL}), "small")
    for tree, packed in ((d_out, d2), (m_out, m2), (v_out, v2)):
        tree.update(_unpack_small(packed, small)[0])
    g_out.update(g_small)

    return (loss, dx[None], *[g_out[n] for n in ORDER], *[d_out[n] for n in ORDER], *[m_out[n] for n in ORDER],
            *[v_out[n] for n in ORDER])
```

```python
import functools

import jax
import jax.numpy as jnp
import numpy as np
from jax import lax
from jax.experimental import pallas as pl
from jax.experimental.pallas import tpu as pltpu

f32 = jnp.float32
MXU_DTYPE = jnp.bfloat16
COMM_DTYPE = jnp.bfloat16

D_MODEL = 1024
POOL_WIDTH = 256
POOL_GC = 64
ATTN_WIDTH = 768
HEAD_DIM = 64
N_IN = POOL_WIDTH + 3 * ATTN_WIDTH
D_FF = 4096
PLE_DIM = 256
BLK = 128
DILATIONS = (1, 4, 16)
ROT_DIM = 16
ROPE_THETA = 500000.0
EPS = 1e-6
ATTN_SCALE = HEAD_DIM ** -0.5
NEG_BIG = -1e30

ADAM_LR, ADAM_B1, ADAM_B2, ADAM_EPS, ADAM_WD, ADAM_STEP = 0.001, 0.9, 0.999, 1e-08, 0.01, 10

TM = 512
TM_WGRAD = 1024
HALO = 16
VMEM_LIMIT = 48 * 1024 * 1024
N_CHIPS = 4
MESH = pl.DeviceIdType.MESH

BIG = ("w_in", "w_out", "w_up", "w_down", "w_gate", "w_ple")
FULL_SHAPE = {"w_in": (D_MODEL, N_IN), "w_out": (D_MODEL, D_MODEL), "w_up": (D_MODEL, D_FF),
              "w_down": (D_FF, D_MODEL), "w_gate": (D_MODEL, D_MODEL), "w_ple": (PLE_DIM, D_MODEL)}
COL_SHARDED = {"w_in": True, "w_out": False, "w_up": True, "w_down": False, "w_gate": False, "w_ple": True}


def _shard_shape(name):
    k, n = FULL_SHAPE[name]
    return (k, n // N_CHIPS) if COL_SHARDED[name] else (k // N_CHIPS, n)


def _cparams(sem=None, vmem=VMEM_LIMIT):
    return pltpu.CompilerParams(dimension_semantics=sem, vmem_limit_bytes=vmem)


def _resident(block_shape, index_map):
    return pl.BlockSpec(block_shape, index_map, pipeline_mode=pl.Buffered(1))


def _mx(x):
    return x.astype(MXU_DTYPE)


def _dot(a, b):
    return jnp.dot(a, b, preferred_element_type=f32)


def _dot_nt(a, b):
    return lax.dot_general(a, b, (((1,), (1,)), ((), ())), preferred_element_type=f32)


def _dot_tn(a, b):
    return lax.dot_general(a, b, (((0,), (0,)), ((), ())), preferred_element_type=f32)


def _sigmoid(x):
    return 1.0 / (1.0 + jnp.exp(-x))


def _rope_apply(y, c, s1, s2, width):
    return y * c + pltpu.roll(y, width - 8, axis=1) * s1 + pltpu.roll(y, 8, axis=1) * s2


def _rope_transpose(dy, c, s1, s2, width):
    return dy * c + pltpu.roll(dy * s1, 8, axis=1) + pltpu.roll(dy * s2, width - 8, axis=1)


def _norm_matmul(h, g, w, layer, tn, name, rope=None):
    s_len, d = h.shape
    n = w.shape[2]

    def body(*refs):
        if rope is None:
            h_ref, g_ref, w_ref, y_ref, hn_ref = refs
        else:
            h_ref, g_ref, w_ref, c_ref, s1_ref, s2_ref, y_ref, hn_ref = refs
            reps = tn // 128
            c = jnp.concatenate([c_ref[...]] * reps, axis=1)
            s1 = jnp.concatenate([s1_ref[...]] * reps, axis=1)
            s2 = jnp.concatenate([s2_ref[...]] * reps, axis=1)
        x = h_ref[...]
        r = lax.rsqrt(jnp.mean(x * x, axis=-1, keepdims=True) + EPS)
        hn = ((x * r) * g_ref[...]).astype(hn_ref.dtype)
        hn_ref[...] = hn
        for j in range(n // tn):
            y = _dot(hn, w_ref[:, j * tn:(j + 1) * tn])
            if rope is not None and POOL_WIDTH <= j * tn < POOL_WIDTH + 2 * ATTN_WIDTH:
                y = _rope_apply(y, c, s1, s2, tn)
            y_ref[:, j * tn:(j + 1) * tn] = y

    in_specs = [pl.BlockSpec((TM, d), lambda i: (i, 0)),
                pl.BlockSpec((1, d), lambda i: (0, 0)),
                _resident((None, d, n), lambda i: (layer, 0, 0))]
    args = [h, g, w]
    if rope is not None:
        assert POOL_WIDTH % tn == 0 and (2 * ATTN_WIDTH) % tn == 0
        in_specs += [pl.BlockSpec((TM, 128), lambda i: (i, 0))] * 3
        args += list(rope)
    return pl.pallas_call(
        body, name=name, grid=(s_len // TM,), in_specs=in_specs,
        out_specs=[pl.BlockSpec((TM, n), lambda i: (i, 0)), pl.BlockSpec((TM, d), lambda i: (i, 0))],
        out_shape=[jax.ShapeDtypeStruct((s_len, n), f32), jax.ShapeDtypeStruct((s_len, d), MXU_DTYPE)],
        compiler_params=_cparams(("parallel",)),
    )(*args)


def _matmul_residual(a, w, layer, res, name, act=False, tk=1024):
    s_len, k_dim = a.shape
    n = w.shape[2]

    def body(a_ref, w_ref, res_ref, o_ref):
        acc = res_ref[...]
        for k in range(k_dim // tk):
            x = a_ref[:, k * tk:(k + 1) * tk]
            if act:
                r = jnp.maximum(x, 0.0)
                x = r * r
            acc = acc + _dot(_mx(x), w_ref[k * tk:(k + 1) * tk, :])
        o_ref[...] = acc

    return pl.pallas_call(
        body, name=name, grid=(s_len // TM,),
        in_specs=[pl.BlockSpec((TM, k_dim), lambda i: (i, 0)),
                  _resident((None, k_dim, n), lambda i: (layer, 0, 0)),
                  pl.BlockSpec((TM, n), lambda i: (i, 0))],
        out_specs=pl.BlockSpec((TM, n), lambda i: (i, 0)),
        out_shape=jax.ShapeDtypeStruct((s_len, n), f32),
        compiler_params=_cparams(("parallel",)),
    )(a, w, res)


def _gate_ple_fwd(h2, g, w_gate, w_ple, layer, p, name):
    s_len, d = h2.shape

    def body(h_ref, g_ref, wg_ref, p_ref, wp_ref, h3_ref, gl_ref, hn_ref):
        x = h_ref[...]
        r = lax.rsqrt(jnp.mean(x * x, axis=-1, keepdims=True) + EPS)
        hn = ((x * r) * g_ref[...]).astype(hn_ref.dtype)
        hn_ref[...] = hn
        gl = _dot(hn, wg_ref[...])
        gl_ref[...] = gl
        e = _dot(_mx(p_ref[...]), wp_ref[...])
        h3_ref[...] = x + _sigmoid(gl) * e

    row = lambda i: (i, 0)
    return pl.pallas_call(
        body, name=name, grid=(s_len // TM,),
        in_specs=[pl.BlockSpec((TM, d), row), pl.BlockSpec((1, d), lambda i: (0, 0)),
                  pl.BlockSpec((None, d, d), lambda i: (layer, 0, 0)), pl.BlockSpec((TM, PLE_DIM), row),
                  pl.BlockSpec((None, PLE_DIM, d), lambda i: (layer, 0, 0))],
        out_specs=[pl.BlockSpec((TM, d), row)] * 3,
        out_shape=[jax.ShapeDtypeStruct((s_len, d), f32), jax.ShapeDtypeStruct((s_len, d), f32),
                   jax.ShapeDtypeStruct((s_len, d), MXU_DTYPE)],
        compiler_params=_cparams(("parallel",)),
    )(h2, g, w_gate, p, w_ple)


def _gate_ple_bwd(dh3, gl, p, w_ple, layer, name):
    s_len, d = dh3.shape

    def body(dh_ref, gl_ref, p_ref, wp_ref, de_ref, dgl_ref):
        dh = dh_ref[...]
        gate = _sigmoid(gl_ref[...])
        e = _dot(_mx(p_ref[...]), wp_ref[...])
        de_ref[...] = (dh * gate).astype(de_ref.dtype)
        dgl_ref[...] = ((dh * e) * (gate * (1.0 - gate))).astype(dgl_ref.dtype)

    row = lambda i: (i, 0)
    return pl.pallas_call(
        body, name=name, grid=(s_len // TM,),
        in_specs=[pl.BlockSpec((TM, d), row), pl.BlockSpec((TM, d), row), pl.BlockSpec((TM, PLE_DIM), row),
                  pl.BlockSpec((None, PLE_DIM, d), lambda i: (layer, 0, 0))],
        out_specs=[pl.BlockSpec((TM, d), row)] * 2,
        out_shape=[jax.ShapeDtypeStruct((s_len, d), MXU_DTYPE)] * 2,
        compiler_params=_cparams(("parallel",)),
    )(dh3, gl, p, w_ple)


def _rmsnorm_bwd(dhn, x, g):
    r = lax.rsqrt(jnp.mean(x * x, axis=-1, keepdims=True) + EPS)
    xh = x * r
    dxh = dhn * g
    dx = r * (dxh - xh * jnp.mean(dxh * xh, axis=-1, keepdims=True))
    return dx, dhn * xh


def _matmul_nt_norm_bwd(dy, w, layer, h_prev, g, dres, name, tk=1024, after=None):
    s_len, k_dim = dy.shape
    d = h_prev.shape[1]

    def body(dy_ref, w_ref, h_ref, g_ref, dres_ref, *rest):
        dh_ref, dg_ref = rest[-2:]
        i = pl.program_id(0)
        acc = None
        for k in range(k_dim // tk):
            part = _dot_nt(_mx(dy_ref[:, k * tk:(k + 1) * tk]), w_ref[:, k * tk:(k + 1) * tk])
            acc = part if acc is None else acc + part
        dx, dgrow = _rmsnorm_bwd(acc, h_ref[...], g_ref[...])
        dh_ref[...] = dres_ref[...] + dx
        dgsum = jnp.sum(dgrow, axis=0, keepdims=True)

        @pl.when(i == 0)
        def _():
            dg_ref[...] = dgsum

        @pl.when(i > 0)
        def _():
            dg_ref[...] += dgsum

    in_specs = [pl.BlockSpec((TM, k_dim), lambda i: (i, 0)),
                _resident((None, d, k_dim), lambda i: (layer, 0, 0)),
                pl.BlockSpec((TM, d), lambda i: (i, 0)),
                pl.BlockSpec((1, d), lambda i: (0, 0)),
                pl.BlockSpec((TM, d), lambda i: (i, 0))]
    args = [dy, w, h_prev, g, dres]
    if after is not None:
        in_specs.append(pl.BlockSpec(memory_space=pl.ANY))
        args.append(after)
    return pl.pallas_call(
        body, name=name, grid=(s_len // TM,), in_specs=in_specs,
        out_specs=[pl.BlockSpec((TM, d), lambda i: (i, 0)), pl.BlockSpec((1, d), lambda i: (0, 0))],
        out_shape=[jax.ShapeDtypeStruct((s_len, d), f32), jax.ShapeDtypeStruct((1, d), f32)],
        compiler_params=_cparams(("arbitrary",)),
    )(*args)


def _down_bwd(dh2, w_down, layer, a, name, tf=1024):
    s_len, d = dh2.shape
    ff = a.shape[1]

    def body(dh_ref, w_ref, a_ref, act_ref, da_ref, dhb_ref):
        j = pl.program_id(1)

        @pl.when(j == 0)
        def _():
            dhb_ref[...] = _mx(dh_ref[...])

        dact = _dot_nt(dhb_ref[...], w_ref[pl.ds(pl.multiple_of(j * tf, tf), tf), :])
        r = jnp.maximum(a_ref[...], 0.0)
        act_ref[...] = (r * r).astype(act_ref.dtype)
        da_ref[...] = (dact * (2.0 * r)).astype(da_ref.dtype)

    return pl.pallas_call(
        body, name=name, grid=(s_len // TM, ff // tf),
        in_specs=[pl.BlockSpec((TM, d), lambda i, j: (i, 0)),
                  _resident((None, ff, d), lambda i, j: (layer, 0, 0)),
                  pl.BlockSpec((TM, tf), lambda i, j: (i, j))],
        out_specs=[pl.BlockSpec((TM, tf), lambda i, j: (i, j))] * 2,
        out_shape=[jax.ShapeDtypeStruct((s_len, ff), MXU_DTYPE)] * 2,
        scratch_shapes=[pltpu.VMEM((TM, d), MXU_DTYPE)],
        compiler_params=_cparams(("parallel", "arbitrary")),
    )(dh2, w_down, a)


def _matmul_nt(dy, w, layer, name):
    s_len, n = dy.shape
    k_dim = w.shape[1]

    def body(dy_ref, w_ref, o_ref):
        o_ref[...] = _dot_nt(_mx(dy_ref[...]), w_ref[...])

    return pl.pallas_call(
        body, name=name, grid=(s_len // TM,),
        in_specs=[pl.BlockSpec((TM, n), lambda i: (i, 0)), pl.BlockSpec((None, k_dim, n), lambda i: (layer, 0, 0))],
        out_specs=pl.BlockSpec((TM, k_dim), lambda i: (i, 0)),
        out_shape=jax.ShapeDtypeStruct((s_len, k_dim), f32),
        compiler_params=_cparams(("parallel",)),
    )(dy, w)


def _weight_grad(a, b, name):
    s_len, k_dim = a.shape
    n = b.shape[1]
    tka = min(k_dim, 2048)
    tnb = n if n <= 1024 else (2048 if n % 2048 == 0 else 640)
    ns = s_len // TM_WGRAD

    def body(a_ref, b_ref, o_ref, acc_ref):
        s = pl.program_id(2)
        part = _dot_tn(_mx(a_ref[...]), _mx(b_ref[...]))

        @pl.when(s == 0)
        def _():
            acc_ref[...] = part

        @pl.when(s > 0)
        def _():
            acc_ref[...] += part

        @pl.when(s == ns - 1)
        def _():
            o_ref[...] = acc_ref[...].astype(o_ref.dtype)

    return pl.pallas_call(
        body, name=name, grid=(k_dim // tka, n // tnb, ns),
        in_specs=[pl.BlockSpec((TM_WGRAD, tka), lambda i, j, s: (s, i)),
                  pl.BlockSpec((TM_WGRAD, tnb), lambda i, j, s: (s, j))],
        out_specs=pl.BlockSpec((tka, tnb), lambda i, j, s: (i, j)),
        out_shape=jax.ShapeDtypeStruct((k_dim, n), COMM_DTYPE),
        scratch_shapes=[pltpu.VMEM((tka, tnb), f32)],
        compiler_params=_cparams(("parallel", "parallel", "arbitrary")),
    )(a, b)


def _group_select(lane, x2, x4, x8, x16):
    grp = lane // POOL_GC
    return jnp.where(grp == 0, x2, jnp.where(grp == 1, x4, jnp.where(grp == 2, x8, x16)))


def _pool_window(lane):
    grp = lane // POOL_GC
    return jnp.where(grp == 0, 2, jnp.where(grp == 1, 4, jnp.where(grp == 2, 8, 16)))


def _pool_y(u, halo, i):
    xs = jnp.concatenate([jnp.where(i > 0, halo, 0.0), u], axis=0)
    s2 = xs + pltpu.roll(xs, 1, axis=0)
    s4 = s2 + pltpu.roll(s2, 2, axis=0)
    s8 = s4 + pltpu.roll(s4, 4, axis=0)
    s16 = s8 + pltpu.roll(s8, 8, axis=0)
    lane = lax.broadcasted_iota(jnp.int32, xs.shape, 1)
    sel = _group_select(lane, s2, s4, s8, s16)[HALO:, :]
    t = i * TM + lax.broadcasted_iota(jnp.int32, u.shape, 0)
    cnt = jnp.minimum(_pool_window(lax.broadcasted_iota(jnp.int32, u.shape, 1)), t + 1).astype(f32)
    return sel / cnt - u


def _group_weights(l0, l1, l2):
    mx = jnp.maximum(jnp.maximum(l0, l1), l2)
    e0, e1, e2 = jnp.exp(l0 - mx), jnp.exp(l1 - mx), jnp.exp(l2 - mx)
    den = e0 + e1 + e2
    return e0 / den, e1 / den, e2 / den


def _mixer_merge(z, wbd, scale, outs, lses, name):
    s_len = z.shape[0]

    def body(u_ref, halo_ref, wbd_ref, sc_ref, o0, o1, o2, l0, l1, l2, m_ref):
        i = pl.program_id(0)
        y = _pool_y(u_ref[...], halo_ref[...], i)
        pool = _dot(_mx(y), wbd_ref[...]) * sc_ref[...]
        w0, w1, w2 = _group_weights(l0[...], l1[...], l2[...])
        m_ref[...] = jnp.concatenate([pool, o0[...] * w0, o1[...] * w1, o2[...] * w2], axis=1).astype(m_ref.dtype)

    row = lambda i: (i, 0)
    blk = pl.BlockSpec((TM, 256), row)
    grp = [pl.BlockSpec((TM, 256), lambda i, g=g: (i, g)) for g in range(3)]
    return pl.pallas_call(
        body, name=name, grid=(s_len // TM,),
        in_specs=[blk, pl.BlockSpec((HALO, 256), lambda i: (jnp.maximum(i * (TM // HALO) - 1, 0), 0)),
                  pl.BlockSpec((256, 256), lambda i: (0, 0)), pl.BlockSpec((1, 256), lambda i: (0, 0))] + grp + grp,
        out_specs=pl.BlockSpec((TM, D_MODEL), row),
        out_shape=jax.ShapeDtypeStruct((s_len, D_MODEL), MXU_DTYPE),
        compiler_params=_cparams(("parallel",)),
    )(z, z, wbd, scale, outs, outs, outs, lses, lses, lses)


def _head_sums(x):
    r = lax.broadcasted_iota(jnp.int32, (256, 256), 0) // HEAD_DIM
    c = lax.broadcasted_iota(jnp.int32, (256, 256), 1) // HEAD_DIM
    ones = jnp.where(r == c, 1.0, 0.0).astype(jnp.bfloat16)
    hi = x.astype(jnp.bfloat16)
    lo = (x - hi.astype(f32)).astype(jnp.bfloat16)
    return _dot(hi, ones) + _dot(lo, ones)


def _combine_bwd(dm, outs, lses, name):
    s_len = dm.shape[0]

    def body(d0, d1, d2, o0, o1, o2, l0, l1, l2, do_ref, dl_ref):
        w = _group_weights(l0[...], l1[...], l2[...])
        da = (d0[...], d1[...], d2[...])
        o = (o0[...], o1[...], o2[...])
        dw = [_head_sums(da[g] * o[g]) for g in range(3)]
        t = w[0] * dw[0] + w[1] * dw[1] + w[2] * dw[2]
        do_ref[...] = jnp.concatenate([da[g] * w[g] for g in range(3)], axis=1)
        dl_ref[...] = jnp.concatenate([w[g] * t for g in range(3)], axis=1)

    grp = [pl.BlockSpec((TM, 256), lambda i, g=g: (i, g)) for g in range(3)]
    return pl.pallas_call(
        body, name=name, grid=(s_len // TM,),
        in_specs=[pl.BlockSpec((TM, 256), lambda i: (i, 1)), pl.BlockSpec((TM, 256), lambda i: (i, 2)),
                  pl.BlockSpec((TM, 256), lambda i: (i, 3))] + grp + grp,
        out_specs=[pl.BlockSpec((TM, ATTN_WIDTH), lambda i: (i, 0))] * 2,
        out_shape=[jax.ShapeDtypeStruct((s_len, ATTN_WIDTH), f32)] * 2,
        compiler_params=_cparams(("parallel",)),
    )(dm, dm, dm, outs, outs, outs, lses, lses, lses)


def _pool_bwd(z, dm, wbd, scale, name):
    s_len = z.shape[0]
    n_halo = s_len // HALO

    def body(u_ref, uh_ref, d_ref, dh_ref, wbd_ref, sc_ref, du_ref, dw_ref, dsc_ref):
        i = pl.program_id(0)
        last = pl.num_programs(0) - 1
        y = _pool_y(u_ref[...], uh_ref[...], i)
        yb = _mx(y)
        dpo = d_ref[...]
        sc = sc_ref[...]
        dsc = jnp.sum(dpo * _dot(yb, wbd_ref[...]), axis=0, keepdims=True)
        dwp = _dot_tn(yb, _mx(dpo * sc))

        @pl.when(i == 0)
        def _():
            dsc_ref[...] = dsc
            dw_ref[...] = dwp

        @pl.when(i > 0)
        def _():
            dsc_ref[...] += dsc
            dw_ref[...] += dwp

        ext = jnp.concatenate([dpo, jnp.where(i < last, dh_ref[...], 0.0)], axis=0)
        dy = _dot_nt(_mx(ext * sc), wbd_ref[...])
        t = i * TM + lax.broadcasted_iota(jnp.int32, ext.shape, 0)
        lane = lax.broadcasted_iota(jnp.int32, ext.shape, 1)
        e = dy / jnp.minimum(_pool_window(lane), t + 1).astype(f32)
        rows = ext.shape[0]
        f2 = e + pltpu.roll(e, rows - 1, axis=0)
        f4 = f2 + pltpu.roll(f2, rows - 2, axis=0)
        f8 = f4 + pltpu.roll(f4, rows - 4, axis=0)
        f16 = f8 + pltpu.roll(f8, rows - 8, axis=0)
        du_ref[...] = (_group_select(lane, f2, f4, f8, f16) - dy)[:TM, :]

    row = lambda i: (i, 0)
    blk = pl.BlockSpec((TM, 256), row)
    return pl.pallas_call(
        body, name=name, grid=(s_len // TM,),
        in_specs=[blk, pl.BlockSpec((HALO, 256), lambda i: (jnp.maximum(i * (TM // HALO) - 1, 0), 0)),
                  blk, pl.BlockSpec((HALO, 256), lambda i: (jnp.minimum((i + 1) * (TM // HALO), n_halo - 1), 0)),
                  pl.BlockSpec((256, 256), lambda i: (0, 0)), pl.BlockSpec((1, 256), lambda i: (0, 0))],
        out_specs=[blk, pl.BlockSpec((256, 256), lambda i: (0, 0)), pl.BlockSpec((1, 256), lambda i: (0, 0))],
        out_shape=[jax.ShapeDtypeStruct((s_len, N_IN), f32), jax.ShapeDtypeStruct((256, 256), f32),
                   jax.ShapeDtypeStruct((1, 256), f32)],
        compiler_params=_cparams(("arbitrary",)),
    )(z, z, dm, dm, wbd, scale)


def _to_strided(x, dil):
    if dil == 1:
        return x
    s_len, c = x.shape
    return x.reshape(s_len // (BLK * dil), BLK, dil, c).transpose(0, 2, 1, 3).reshape(s_len, c)


def _from_strided(x, dil):
    if dil == 1:
        return x
    s_len, c = x.shape
    return x.reshape(s_len // (BLK * dil), dil, BLK, c).transpose(0, 2, 1, 3).reshape(s_len, c)


def _tri_masks():
    qi = lax.broadcasted_iota(jnp.int32, (BLK, BLK), 0)
    ki = lax.broadcasted_iota(jnp.int32, (BLK, BLK), 1)
    return qi >= ki, ki >= qi


ATTN_SUPER_PER_STEP = (8, 2, 1)
Q_COL, K_COL, V_COL = POOL_WIDTH // 128, (POOL_WIDTH + ATTN_WIDTH) // 128, (POOL_WIDTH + 2 * ATTN_WIDTH) // 128


def _rows(ref, start, dil):
    if dil == 1:
        return ref[pl.ds(start, BLK), :]
    return ref[pl.ds(start, BLK, stride=dil), :]


def _set_rows(ref, start, dil, val):
    if dil == 1:
        ref[pl.ds(start, BLK), :] = val
    else:
        ref[pl.ds(start, BLK, stride=dil), :] = val


def _attn_fwd(z, g, prev, name):
    s_len = z.shape[0]
    dil, m = DILATIONS[g], ATTN_SUPER_PER_STEP[g]
    sbr = BLK * dil
    rows = sbr * m

    def body(*refs):
        q_ref, kc_ref, kp_ref, vc_ref, vp_ref = refs[:5]
        o_ref, l_ref = refs[-2:]
        st = pl.program_id(0)
        low, up = _tri_masks()
        head0 = lax.broadcasted_iota(jnp.int32, (BLK, 128), 1) < HEAD_DIM
        for sb in range(m):
            valid = jnp.concatenate([up & (st > 0) if sb == 0 else up, low], axis=1)
            for r in range(dil):
                base = sb * sbr + r
                q = _rows(q_ref, base, dil)
                kc, vc = _rows(kc_ref, base, dil), _rows(vc_ref, base, dil)
                if sb == 0:
                    kp, vp = _rows(kp_ref, r, dil), _rows(vp_ref, r, dil)
                else:
                    kp, vp = _rows(kc_ref, base - sbr, dil), _rows(vc_ref, base - sbr, dil)
                k2 = jnp.concatenate([_mx(kp), _mx(kc)], axis=0)
                v2 = jnp.concatenate([_mx(vp), _mx(vc)], axis=0)
                outs, lses = [], []
                for hh in range(2):
                    s = jnp.where(valid, _dot_nt(_mx(jnp.where(head0 == (hh == 0), q, 0.0)), k2) * ATTN_SCALE, NEG_BIG)
                    mx = jnp.max(s, axis=-1, keepdims=True)
                    e = jnp.exp(s - mx)
                    l = jnp.sum(e, axis=-1, keepdims=True)
                    outs.append(_dot(_mx(e / l), v2))
                    lses.append(jnp.broadcast_to(mx + jnp.log(l), (BLK, 128)))
                _set_rows(o_ref, base, dil, jnp.where(head0, outs[0], outs[1]))
                _set_rows(l_ref, base, dil, jnp.where(head0, lses[0], lses[1]))

    def cur(col):
        return pl.BlockSpec((rows, 128), lambda st, hp: (st, col + 2 * g + hp))

    def before(col):
        return pl.BlockSpec((sbr, 128), lambda st, hp: (jnp.maximum(st * m - 1, 0), col + 2 * g + hp))

    in_specs = [cur(Q_COL), cur(K_COL), before(K_COL), cur(V_COL), before(V_COL)]
    args = [z, z, z, z, z]
    aliases = {}
    if prev is not None:
        in_specs += [pl.BlockSpec(memory_space=pl.ANY)] * 2
        args += list(prev)
        aliases = {5: 0, 6: 1}
    return pl.pallas_call(
        body, name=name, grid=(s_len // rows, 2), in_specs=in_specs, out_specs=[cur(0), cur(0)],
        out_shape=[jax.ShapeDtypeStruct((s_len, ATTN_WIDTH), f32)] * 2, input_output_aliases=aliases,
        compiler_params=_cparams(("parallel", "parallel")),
    )(*args)


def _attn_bwd(z, do, lse, dlt, tabs, dz, g, name):
    s_len = z.shape[0]
    dil, m = DILATIONS[g], ATTN_SUPER_PER_STEP[g]
    sbr = BLK * dil
    rows = sbr * m
    nsteps = s_len // rows

    def body(q_ref, qn_ref, kc_ref, kp_ref, vc_ref, vp_ref, do_ref, don_ref, l_ref, ln_ref, d_ref, dn_ref,
             c_ref, s1_ref, s2_ref, dz_in, dz_ref, dq_buf, dk_buf, dv_buf, sems):
        del dz_in
        st, hp = pl.program_id(0), pl.program_id(1)
        low, up = _tri_masks()
        head0 = lax.broadcasted_iota(jnp.int32, (BLK, 128), 1) < HEAD_DIM
        for sb in range(m):
            up_prev = up & (st > 0) if sb == 0 else up
            up_next = up & (st < nsteps - 1) if sb == m - 1 else up
            for r in range(dil):
                base = sb * sbr + r
                q, k, v = _rows(q_ref, base, dil), _rows(kc_ref, base, dil), _rows(vc_ref, base, dil)
                do_c, l_c, d_c = _rows(do_ref, base, dil), _rows(l_ref, base, dil), _rows(d_ref, base, dil)
                if sb == 0:
                    kp, vp = _rows(kp_ref, r, dil), _rows(vp_ref, r, dil)
                else:
                    kp, vp = _rows(kc_ref, base - sbr, dil), _rows(vc_ref, base - sbr, dil)
                if sb == m - 1:
                    qn, do_n = _rows(qn_ref, r, dil), _rows(don_ref, r, dil)
                    l_n, d_n = _rows(ln_ref, r, dil), _rows(dn_ref, r, dil)
                else:
                    qn, do_n = _rows(q_ref, base + sbr, dil), _rows(do_ref, base + sbr, dil)
                    l_n, d_n = _rows(l_ref, base + sbr, dil), _rows(d_ref, base + sbr, dil)
                kc, kb, vc, vb = _mx(k), _mx(kp), _mx(v), _mx(vp)
                dqs, dks, dvs = [], [], []
                for hh in range(2):
                    mine = head0 == (hh == 0)
                    one = slice(hh * HEAD_DIM, hh * HEAD_DIM + 1)
                    qc, qx = _mx(jnp.where(mine, q, 0.0)), _mx(jnp.where(mine, qn, 0.0))
                    doc, dox = _mx(jnp.where(mine, do_c, 0.0)), _mx(jnp.where(mine, do_n, 0.0))
                    lc, lx, dc, dx = l_c[:, one], l_n[:, one], d_c[:, one], d_n[:, one]
                    p_a = jnp.where(low, jnp.exp(_dot_nt(qc, kc) * ATTN_SCALE - lc), 0.0)
                    ds_a = _mx(p_a * (_dot_nt(doc, vc) - dc) * ATTN_SCALE)
                    p_b = jnp.where(up_prev, jnp.exp(_dot_nt(qc, kb) * ATTN_SCALE - lc), 0.0)
                    ds_b = _mx(p_b * (_dot_nt(doc, vb) - dc) * ATTN_SCALE)
                    p_c = jnp.where(up_next, jnp.exp(_dot_nt(qx, kc) * ATTN_SCALE - lx), 0.0)
                    ds_c = _mx(p_c * (_dot_nt(dox, vc) - dx) * ATTN_SCALE)
                    dqs.append(_dot(ds_a, kc) + _dot(ds_b, kb))
                    dks.append(_dot_tn(ds_a, qc) + _dot_tn(ds_c, qx))
                    dvs.append(_dot_tn(_mx(p_a), doc) + _dot_tn(_mx(p_c), dox))
                c, s1, s2 = _rows(c_ref, base, dil), _rows(s1_ref, base, dil), _rows(s2_ref, base, dil)
                _set_rows(dq_buf, base, dil, _rope_transpose(jnp.where(head0, dqs[0], dqs[1]), c, s1, s2, 128))
                _set_rows(dk_buf, base, dil, _rope_transpose(dks[0] + dks[1], c, s1, s2, 128))
                _set_rows(dv_buf, base, dil, dvs[0] + dvs[1])
        copies = []
        for t, (buf, col) in enumerate(((dq_buf, Q_COL), (dk_buf, K_COL), (dv_buf, V_COL))):
            lane0 = pl.multiple_of((col + 2 * g + hp) * 128, 128)
            dst = dz_ref.at[pl.ds(pl.multiple_of(st * rows, rows), rows), pl.ds(lane0, 128)]
            cp = pltpu.make_async_copy(buf, dst, sems.at[t])
            cp.start()
            copies.append(cp)
        for cp in copies:
            cp.wait()

    def cur(col):
        return pl.BlockSpec((rows, 128), lambda st, hp: (st, col + 2 * g + hp))

    def before(col):
        return pl.BlockSpec((sbr, 128), lambda st, hp: (jnp.maximum(st * m - 1, 0), col + 2 * g + hp))

    def after(col):
        return pl.BlockSpec((sbr, 128), lambda st, hp: (jnp.minimum((st + 1) * m, s_len // sbr - 1), col + 2 * g + hp))

    tab = pl.BlockSpec((rows, 128), lambda st, hp: (st, 0))
    return pl.pallas_call(
        body, name=name, grid=(nsteps, 2),
        in_specs=[cur(Q_COL), after(Q_COL), cur(K_COL), before(K_COL), cur(V_COL), before(V_COL),
                  cur(0), after(0), cur(0), after(0), cur(0), after(0), tab, tab, tab,
                  pl.BlockSpec(memory_space=pl.ANY)],
        out_specs=pl.BlockSpec(memory_space=pl.ANY),
        out_shape=jax.ShapeDtypeStruct(dz.shape, dz.dtype), input_output_aliases={15: 0},
        scratch_shapes=[pltpu.VMEM((rows, 128), f32)] * 3 + [pltpu.SemaphoreType.DMA((3,))],
        compiler_params=_cparams(("arbitrary", "arbitrary")),
    )(z, z, z, z, z, z, do, do, lse, lse, dlt, dlt, *tabs, dz)


def _attn_fwd_old(q, k, v, dil, name):
    s_len = q.shape[0]
    nblk = s_len // BLK

    def body(q_ref, kc_ref, kp_ref, vc_ref, vp_ref, o_ref, l_ref):
        b = pl.program_id(0)
        has_prev = b >= dil
        low, up = _tri_masks()
        valid = jnp.concatenate([up & has_prev, low], axis=1)
        outs, lses = [], []
        for hh in range(2):
            sl = slice(hh * HEAD_DIM, (hh + 1) * HEAD_DIM)
            qh = _mx(q_ref[:, sl])
            k2 = jnp.concatenate([_mx(kp_ref[:, sl]), _mx(kc_ref[:, sl])], axis=0)
            v2 = jnp.concatenate([_mx(vp_ref[:, sl]), _mx(vc_ref[:, sl])], axis=0)
            s = jnp.where(valid, _dot_nt(qh, k2) * ATTN_SCALE, NEG_BIG)
            m = jnp.max(s, axis=-1, keepdims=True)
            e = jnp.exp(s - m)
            l = jnp.sum(e, axis=-1, keepdims=True)
            outs.append(_dot(_mx(e / l), v2))
            lses.append(jnp.broadcast_to(m + jnp.log(l), (BLK, HEAD_DIM)))
        o_ref[...] = jnp.concatenate(outs, axis=1)
        l_ref[...] = jnp.concatenate(lses, axis=1)

    cur = pl.BlockSpec((BLK, 128), lambda b, hp: (b, hp))
    prev = pl.BlockSpec((BLK, 128), lambda b, hp: (jnp.maximum(b - dil, 0), hp))
    return pl.pallas_call(
        body, name=name, grid=(nblk, 2), in_specs=[cur, cur, prev, cur, prev], out_specs=[cur, cur],
        out_shape=[jax.ShapeDtypeStruct((s_len, 256), f32)] * 2,
        compiler_params=_cparams(("parallel", "parallel")),
    )(q, k, k, v, v)


def _attn_bwd_old(q, k, v, do, lse, dlt, tabs, dil, name):
    s_len = q.shape[0]
    nblk = s_len // BLK

    def body(q_ref, qn_ref, kc_ref, kp_ref, vc_ref, vp_ref, do_ref, don_ref, l_ref, ln_ref, d_ref, dn_ref,
             c_ref, s1_ref, s2_ref, dq_ref, dk_ref, dv_ref):
        b = pl.program_id(0)
        has_prev = b >= dil
        has_next = b + dil < nblk
        low, up = _tri_masks()
        dqs, dks, dvs = [], [], []
        for hh in range(2):
            sl = slice(hh * HEAD_DIM, (hh + 1) * HEAD_DIM)
            one = slice(hh * HEAD_DIM, hh * HEAD_DIM + 1)
            qc, qn = _mx(q_ref[:, sl]), _mx(qn_ref[:, sl])
            kc, kp = _mx(kc_ref[:, sl]), _mx(kp_ref[:, sl])
            vc, vp = _mx(vc_ref[:, sl]), _mx(vp_ref[:, sl])
            doc, don = _mx(do_ref[:, sl]), _mx(don_ref[:, sl])
            lc, ln = l_ref[:, one], ln_ref[:, one]
            dc, dn = d_ref[:, one], dn_ref[:, one]
            p_a = jnp.where(low, jnp.exp(_dot_nt(qc, kc) * ATTN_SCALE - lc), 0.0)
            ds_a = _mx(p_a * (_dot_nt(doc, vc) - dc) * ATTN_SCALE)
            p_b = jnp.where(up & has_prev, jnp.exp(_dot_nt(qc, kp) * ATTN_SCALE - lc), 0.0)
            ds_b = _mx(p_b * (_dot_nt(doc, vp) - dc) * ATTN_SCALE)
            p_c = jnp.where(up & has_next, jnp.exp(_dot_nt(qn, kc) * ATTN_SCALE - ln), 0.0)
            ds_c = _mx(p_c * (_dot_nt(don, vc) - dn) * ATTN_SCALE)
            dqs.append(_dot(ds_a, kc) + _dot(ds_b, kp))
            dks.append(_dot_tn(ds_a, qc) + _dot_tn(ds_c, qn))
            dvs.append(_dot_tn(_mx(p_a), doc) + _dot_tn(_mx(p_c), don))
        c, s1, s2 = c_ref[...], s1_ref[...], s2_ref[...]
        dq_ref[...] = _rope_transpose(jnp.concatenate(dqs, axis=1), c, s1, s2, 128)
        dk_ref[...] = _rope_transpose(jnp.concatenate(dks, axis=1), c, s1, s2, 128)
        dv_ref[...] = jnp.concatenate(dvs, axis=1)

    cur = pl.BlockSpec((BLK, 128), lambda b, hp: (b, hp))
    prev = pl.BlockSpec((BLK, 128), lambda b, hp: (jnp.maximum(b - dil, 0), hp))
    nxt = pl.BlockSpec((BLK, 128), lambda b, hp: (jnp.minimum(b + dil, nblk - 1), hp))
    tab = pl.BlockSpec((BLK, 128), lambda b, hp: (b, 0))
    return pl.pallas_call(
        body, name=name, grid=(nblk, 2),
        in_specs=[cur, nxt, cur, prev, cur, prev, cur, nxt, cur, nxt, cur, nxt, tab, tab, tab],
        out_specs=[cur, cur, cur], out_shape=[jax.ShapeDtypeStruct((s_len, 256), f32)] * 3,
        compiler_params=_cparams(("parallel", "parallel")),
    )(q, q, k, k, v, v, do, do, lse, lse, dlt, dlt, *tabs)


def _loss_head(h, g, target, name):
    s_len, d = h.shape

    def body(h_ref, g_ref, t_ref, loss_ref, dh_ref, dg_ref):
        i = pl.program_id(0)
        x = h_ref[...]
        gv = g_ref[...]
        r = lax.rsqrt(jnp.mean(x * x, axis=-1, keepdims=True) + EPS)
        xh = x * r
        diff = xh * gv - t_ref[...]
        part = 0.5 * jnp.sum(jnp.mean(diff * diff, axis=-1, keepdims=True), axis=0, keepdims=True)
        dy = diff * (1.0 / d)
        dxh = dy * gv
        dh_ref[...] = r * (dxh - xh * jnp.mean(dxh * xh, axis=-1, keepdims=True))
        dgsum = jnp.sum(dy * xh, axis=0, keepdims=True)
        lossb = jnp.broadcast_to(part, (8, 128))

        @pl.when(i == 0)
        def _():
            loss_ref[...] = lossb
            dg_ref[...] = dgsum

        @pl.when(i > 0)
        def _():
            loss_ref[...] += lossb
            dg_ref[...] += dgsum

    row = lambda i: (i, 0)
    return pl.pallas_call(
        body, name=name, grid=(s_len // TM,),
        in_specs=[pl.BlockSpec((TM, d), row), pl.BlockSpec((1, d), lambda i: (0, 0)), pl.BlockSpec((TM, d), row)],
        out_specs=[pl.BlockSpec((8, 128), lambda i: (0, 0)), pl.BlockSpec((TM, d), row),
                   pl.BlockSpec((1, d), lambda i: (0, 0))],
        out_shape=[jax.ShapeDtypeStruct((8, 128), f32), jax.ShapeDtypeStruct((s_len, d), f32),
                   jax.ShapeDtypeStruct((1, d), f32)],
        compiler_params=_cparams(("arbitrary",)),
    )(h, g, target)


def _rope_tables(positions):
    inv_freq = ROPE_THETA ** (-jnp.arange(0, ROT_DIM, 2, dtype=f32) / ROT_DIM)
    ang = positions.astype(f32)[:, None] * inv_freq
    cos, sin = jnp.cos(ang), jnp.sin(ang)
    s_len = positions.shape[0]
    zero8, rest = jnp.zeros((s_len, 8), f32), jnp.zeros((s_len, HEAD_DIM - ROT_DIM), f32)
    c = jnp.concatenate([cos, cos, jnp.ones((s_len, HEAD_DIM - ROT_DIM), f32)], axis=1)
    s1 = jnp.concatenate([-sin, zero8, rest], axis=1)
    s2 = jnp.concatenate([zero8, sin, rest], axis=1)
    return c, s1, s2


def _block_diag(pool_w):
    out = jnp.zeros((POOL_WIDTH, POOL_WIDTH), pool_w.dtype)
    for g in range(4):
        out = lax.dynamic_update_slice(out, pool_w[g], (g * POOL_GC, g * POOL_GC))
    return out


class _ReadyWeights:
    def __init__(self, full):
        self.full = full

    def take(self, layer, names, after):
        del after
        return {n: self.full[n] for n in names}, layer


def _layer_fwd(h, p_l, wsrc, small, layer, tabs):
    nm = f"l{layer}_"
    wts, wl = wsrc.take(layer, ("w_in",), h if layer else None)
    z, hn1 = _norm_matmul(h, small["norm1"][layer][None], wts["w_in"], wl, 256, nm + "in_proj", rope=tabs)
    ol = None
    for g in range(3):
        ol = _attn_fwd(z, g, ol, nm + f"attn_fwd{g}")
    outs, lses = ol
    wbd = _mx(_block_diag(small["pool_w"][layer]))
    scale = small["pool_scale"][layer][None]
    m = _mixer_merge(z, wbd, scale, outs, lses, nm + "mixer_merge")
    rest, _ = wsrc.take(layer, ("w_out", "w_up", "w_down", "w_gate", "w_ple"), m)
    wts = {**wts, **rest}
    h1 = _matmul_residual(m, wts["w_out"], wl, h, nm + "out_proj")
    a, hn2 = _norm_matmul(h1, small["norm2"][layer][None], wts["w_up"], wl, 1024, nm + "up_proj")
    h2 = _matmul_residual(a, wts["w_down"], wl, h1, nm + "down_proj", act=True)
    h3, gl, hn3 = _gate_ple_fwd(h2, small["norm3"][layer][None], wts["w_gate"], wts["w_ple"], wl, p_l, nm + "gate_ple")
    saved = dict(h=h, z=z, hn1=hn1, outs=outs, lses=lses, wbd=wbd, scale=scale, m=m, h1=h1, a=a, hn2=hn2, h2=h2,
                 gl=gl, hn3=hn3, wts=wts, wl=wl)
    return h3, saved


def _layer_bwd(dh3, sv, p_l, small, layer, tabs128, reducer):
    nm = f"l{layer}_"
    wts, wl = sv["wts"], sv["wl"]
    de, dgl = _gate_ple_bwd(dh3, sv["gl"], p_l, wts["w_ple"], wl, nm + "gate_ple_bwd")
    reducer.add("w_gate", layer, _weight_grad(sv["hn3"], dgl, nm + "dw_gate"))
    reducer.add("w_ple", layer, _weight_grad(p_l, de, nm + "dw_ple"))
    dh2, dg3 = _matmul_nt_norm_bwd(dgl, wts["w_gate"], wl, sv["h2"], small["norm3"][layer][None], dh3, nm + "gate_bwd")
    act, da = _down_bwd(dh2, wts["w_down"], wl, sv["a"], nm + "down_bwd")
    reducer.add("w_down", layer, _weight_grad(act, dh2, nm + "dw_down"))
    started = reducer.add("w_up", layer, _weight_grad(sv["hn2"], da, nm + "dw_up"))
    dh1, dg2 = _matmul_nt_norm_bwd(da, wts["w_up"], wl, sv["h1"], small["norm2"][layer][None], dh2, nm + "up_bwd",
                                   after=started)
    dm = _matmul_nt(dh1, wts["w_out"], wl, nm + "out_bwd")
    reducer.add("w_out", layer, _weight_grad(sv["m"], dh1, nm + "dw_out"))
    do, dlt = _combine_bwd(dm, sv["outs"], sv["lses"], nm + "combine_bwd")
    dz, dwbd, dscale = _pool_bwd(sv["z"], dm, sv["wbd"], sv["scale"], nm + "pool_bwd")
    for g in range(3):
        dz = _attn_bwd(sv["z"], do, sv["lses"], dlt, tabs128, dz, g, nm + f"attn_bwd{g}")
    started = reducer.add("w_in", layer, _weight_grad(sv["hn1"], dz, nm + "dw_in"))
    dh0, dg1 = _matmul_nt_norm_bwd(dz, wts["w_in"], wl, sv["h"], small["norm1"][layer][None], dh1, nm + "in_bwd",
                                   tk=512, after=started)
    dpool_w = jnp.stack([dwbd[g * POOL_GC:(g + 1) * POOL_GC, g * POOL_GC:(g + 1) * POOL_GC] for g in range(4)])
    sg = dict(norm1=dg1[0], norm2=dg2[0], norm3=dg3[0], pool_w=dpool_w, pool_scale=dscale[0])
    return dh0, sg


class _CollectGrads:
    def __init__(self):
        self.grads = {}

    def add(self, name, layer, dw):
        self.grads[(name, layer)] = dw


def _local_step(x, p, positions, wsrc, small, target, reducer):
    tabs128 = tuple(jnp.tile(t, (1, 2)) for t in _rope_tables(positions))
    h = x
    saved = []
    for layer in range(2):
        h, sv = _layer_fwd(h, p[layer], wsrc, small, layer, tabs128)
        saved.append(sv)
    loss, dh, dgf = _loss_head(h, small["final_norm"][None], target, "loss_head")
    sgs = [None, None]
    for layer in (1, 0):
        dh, sgs[layer] = _layer_bwd(dh, saved[layer], p[layer], small, layer, tabs128, reducer)
    small_grads = {k: jnp.stack([sgs[0][k], sgs[1][k]]) for k in sgs[0]}
    small_grads["final_norm"] = dgf[0]
    return loss, dh, small_grads


HBM = pl.BlockSpec(memory_space=pltpu.HBM)


def _my_place():
    return lax.axis_index("x"), lax.axis_index("y"), lax.axis_index("c")


def _other_chips(x, y):
    return [(1 - x, y), (x, 1 - y), (1 - x, 1 - y)]


def _window(ref, name, chip):
    k, n = _shard_shape(name)
    if COL_SHARDED[name]:
        return ref.at[:, pl.ds(pl.multiple_of(chip * n, 128), n)]
    return ref.at[pl.ds(pl.multiple_of(chip * k, 128), k), :]


def _chip_index():
    return jnp.reshape(2 * lax.axis_index("x") + lax.axis_index("y"), (1,)).astype(jnp.int32)


def _shard_block(name, tr):
    ks, ns = _shard_shape(name)
    if COL_SHARDED[name]:
        return (tr, ns), lambda i, me: (i, me[0])
    return (tr, ns), lambda i, me: (me[0] * (ks // tr) + i, 0)


def _place_shard(w, name, layer):
    ks, ns = _shard_shape(name)
    tr = min(ks, 256)
    shape, index = _shard_block(name, tr)

    def body(me_ref, w_ref, o_ref):
        o_ref[...] = w_ref[...].astype(o_ref.dtype)

    return pl.pallas_call(
        body, name=f"place_{name}{layer}",
        grid_spec=pltpu.PrefetchScalarGridSpec(
            num_scalar_prefetch=1, grid=(ks // tr,),
            in_specs=[pl.BlockSpec((None, tr, ns), lambda i, me: (layer, i, 0))],
            out_specs=pl.BlockSpec((None,) + shape, lambda i, me: (0,) + index(i, me))),
        out_shape=jax.ShapeDtypeStruct((1,) + FULL_SHAPE[name], MXU_DTYPE),
        compiler_params=_cparams(("parallel",)),
    )(_chip_index(), w)


GATHER_ORDER = [("w_in", 0), ("w_out", 0), ("w_up", 0), ("w_down", 0), ("w_gate", 0), ("w_ple", 0),
                ("w_in", 1), ("w_out", 1), ("w_up", 1), ("w_down", 1), ("w_gate", 1), ("w_ple", 1)]
SEM = pl.BlockSpec(memory_space=pltpu.SEMAPHORE)
EFFECT = pltpu.SideEffectType.DATAFLOW_SIDE_EFFECTING


def _gather_copy(src_ref, dst_ref, name, idx, j, chip, send_sems, recv_sems, c):
    cx, cy = chip
    return pltpu.make_async_remote_copy(
        src_ref=src_ref, dst_ref=dst_ref, send_sem=send_sems.at[3 * idx + j], recv_sem=recv_sems.at[3 * idx + j],
        device_id=(cx, cy, c), device_id_type=MESH)


def _gather_start(placed, order, tag, after=None):
    n = len(order)
    extra = [] if after is None else [after]

    def body(*refs):
        ins = refs[:n]
        k = n + len(extra)
        send_sems, recv_sems = refs[k], refs[k + 1]
        outs = refs[k + 2:k + 2 + n]
        token = refs[-1]
        x, y, c = _my_place()
        me = 2 * x + y
        for idx, (name, _) in enumerate(order):
            for j, chip in enumerate(_other_chips(x, y)):
                _gather_copy(_window(ins[idx].at[0], name, me), _window(outs[idx].at[0], name, me), name, idx, j, chip,
                             send_sems, recv_sems, c).start()
        token[...] = jnp.zeros_like(token)

    res = pl.pallas_call(
        body, name="gather_start" + tag,
        out_shape=(pltpu.SemaphoreType.DMA((3 * n,)), pltpu.SemaphoreType.DMA((3 * n,)))
        + tuple(pltpu.HBM(a.shape, a.dtype) for a in placed) + (jax.ShapeDtypeStruct((8, 128), f32),),
        in_specs=[HBM] * n + [pl.BlockSpec(memory_space=pl.ANY)] * len(extra),
        out_specs=(SEM, SEM) + (HBM,) * n + (pl.BlockSpec(memory_space=pltpu.VMEM),),
        input_output_aliases={i: i + 2 for i in range(n)},
        compiler_params=pltpu.CompilerParams(has_side_effects=EFFECT),
    )(*[pltpu.with_memory_space_constraint(a, pltpu.HBM) for a in placed], *extra)
    return res[0], res[1], list(res[2:2 + n]), res[-1]


def _gather_wait(send_sems, recv_sems, arrays, order, idxs, after, name):
    n = len(idxs)

    def body(*refs):
        ins = refs[:n]
        send_ref, recv_ref = refs[n], refs[n + 1]
        x, y, c = _my_place()
        me = 2 * x + y
        for k, idx in enumerate(idxs):
            wname = order[idx][0]
            for j, chip in enumerate(_other_chips(x, y)):
                cx, cy = chip
                mine = _window(ins[k].at[0], wname, me)
                land = _window(ins[k].at[0], wname, 2 * cx + cy)
                _gather_copy(mine, mine, wname, idx, j, chip, send_ref, recv_ref, c).wait_send()
                _gather_copy(land, land, wname, idx, j, chip, send_ref, recv_ref, c).wait_recv()

    operands = list(arrays) + [send_sems, recv_sems]
    in_specs = [HBM] * n + [SEM, SEM]
    if after is not None:
        operands.append(after)
        in_specs.append(pl.BlockSpec(memory_space=pl.ANY))
    res = pl.pallas_call(
        body, name=name, out_shape=tuple(pltpu.HBM(a.shape, a.dtype) for a in arrays),
        in_specs=in_specs, out_specs=(HBM,) * n, input_output_aliases={i: i for i in range(n)},
        compiler_params=pltpu.CompilerParams(has_side_effects=EFFECT),
    )(*operands)
    return list(res)


class _GatheredWeights:
    def __init__(self, shards):
        self.starts = []
        token = None
        for tag, order in (("_first", GATHER_ORDER[:1]), ("_rest", GATHER_ORDER[1:])):
            placed = [_place_shard(shards[name], name, layer) for name, layer in order]
            self.starts.append((order,) + _gather_start(placed, order, tag, token))
            token = self.starts[-1][-1]

    def take(self, layer, names, after):
        order, send, recv, arrays, _ = next(s for s in self.starts if (names[0], layer) in s[0])
        if after is None:
            after = self.starts[-1][-1]
        idxs = [order.index((n, layer)) for n in names]
        got = _gather_wait(send, recv, [arrays[i] for i in idxs], order, idxs, after, f"gather_wait{layer}_{names[0]}")
        return dict(zip(names, got)), 0


def _gather_weights(full):
    names = list(BIG)

    def body(*refs):
        ins = refs[:len(names)]
        outs = refs[len(names):2 * len(names)]
        send_ici, recv_ici, send_d2d, recv_d2d = refs[2 * len(names):]
        x, y, c = _my_place()
        me = 2 * x + y
        sibling = (x, y, 1 - c)
        chips = _other_chips(x, y)
        ici = []
        for t, name in enumerate(names):
            for j, (cx, cy) in enumerate(chips):
                cp = pltpu.make_async_remote_copy(
                    src_ref=_window(ins[t].at[c], name, me), dst_ref=_window(outs[t].at[c], name, me),
                    send_sem=send_ici.at[3 * t + j], recv_sem=recv_ici.at[3 * t + j],
                    device_id=(cx, cy, c), device_id_type=MESH)
                cp.start()
                ici.append(cp)
        fwd = []
        for t, name in enumerate(names):
            for j, (cx, cy) in enumerate(chips):
                land = _window(outs[t].at[c], name, 2 * cx + cy)
                pltpu.make_async_remote_copy(
                    src_ref=land, dst_ref=land, send_sem=send_ici.at[3 * t + j], recv_sem=recv_ici.at[3 * t + j],
                    device_id=(cx, cy, c), device_id_type=MESH).wait_recv()
                cp = pltpu.make_async_remote_copy(
                    src_ref=land, dst_ref=land, send_sem=send_d2d.at[3 * t + j], recv_sem=recv_d2d.at[3 * t + j],
                    device_id=sibling, device_id_type=MESH)
                cp.start()
                fwd.append(cp)
        for t, name in enumerate(names):
            for j, (cx, cy) in enumerate(chips):
                land = _window(outs[t].at[1 - c], name, 2 * cx + cy)
                pltpu.make_async_remote_copy(
                    src_ref=land, dst_ref=land, send_sem=send_d2d.at[3 * t + j], recv_sem=recv_d2d.at[3 * t + j],
                    device_id=sibling, device_id_type=MESH).wait_recv()
        for cp in ici + fwd:
            cp.wait_send()

    nsem = 3 * len(names)
    outs = pl.pallas_call(
        body, name="gather_weights",
        in_specs=[HBM] * len(names), out_specs=[HBM] * len(names),
        out_shape=[jax.ShapeDtypeStruct(full[n].shape, full[n].dtype) for n in names],
        input_output_aliases={t: t for t in range(len(names))},
        scratch_shapes=[pltpu.SemaphoreType.DMA((nsem,)), pltpu.SemaphoreType.DMA((nsem,)),
                        pltpu.SemaphoreType.DMA((nsem,)), pltpu.SemaphoreType.DMA((nsem,))],
    )(*[full[n] for n in names])
    return dict(zip(names, outs))


def _swap_layers(grads):
    names = list(BIG)

    def body(*refs):
        ins = refs[:len(names)]
        outs = refs[len(names):2 * len(names)]
        send_sems, recv_sems = refs[2 * len(names):]
        x, y, c = _my_place()
        sibling = (x, y, 1 - c)
        cps = []
        for t in range(len(names)):
            cp = pltpu.make_async_remote_copy(
                src_ref=ins[t].at[1 - c], dst_ref=outs[t], send_sem=send_sems.at[t], recv_sem=recv_sems.at[t],
                device_id=sibling, device_id_type=MESH)
            cp.start()
            cps.append(cp)
        for cp in cps:
            cp.wait()

    outs = pl.pallas_call(
        body, name="swap_layers", in_specs=[HBM] * len(names), out_specs=[HBM] * len(names),
        out_shape=[jax.ShapeDtypeStruct(FULL_SHAPE[n], f32) for n in names],
        scratch_shapes=[pltpu.SemaphoreType.DMA((len(names),)), pltpu.SemaphoreType.DMA((len(names),))],
    )(*[grads[n] for n in names])
    return dict(zip(names, outs))


def _chip_sum(grad, other, name):
    k, n = FULL_SHAPE[name]
    tr = min(k, 512)
    c = lax.axis_index("c")

    def body(c_ref, g_ref, o_ref, out_ref):
        out_ref[...] = (g_ref[...] + o_ref[...]).astype(out_ref.dtype)

    return pl.pallas_call(
        body, name="chip_sum_" + name,
        grid_spec=pltpu.PrefetchScalarGridSpec(
            num_scalar_prefetch=1, grid=(k // tr,),
            in_specs=[pl.BlockSpec((None, tr, n), lambda i, c_ref: (c_ref[0], i, 0)),
                      pl.BlockSpec((tr, n), lambda i, c_ref: (i, 0))],
            out_specs=pl.BlockSpec((tr, n), lambda i, c_ref: (i, 0))),
        out_shape=jax.ShapeDtypeStruct((k, n), COMM_DTYPE),
        compiler_params=_cparams(("parallel",)),
    )(jnp.reshape(c, (1,)).astype(jnp.int32), grad, other)


def _scatter_shards(sums):
    names = list(BIG)

    def body(*refs):
        ins = refs[:len(names)]
        outs = refs[len(names):2 * len(names)]
        send_sems, recv_sems = refs[2 * len(names):]
        x, y, c = _my_place()
        me = 2 * x + y
        chips = _other_chips(x, y)
        cps = []
        for t, name in enumerate(names):
            for j, (cx, cy) in enumerate(chips):
                cp = pltpu.make_async_remote_copy(
                    src_ref=_window(ins[t], name, 2 * cx + cy), dst_ref=outs[t].at[me],
                    send_sem=send_sems.at[3 * t + j], recv_sem=recv_sems.at[3 * t + j],
                    device_id=(cx, cy, c), device_id_type=MESH)
                cp.start()
                cps.append(cp)
        for t, name in enumerate(names):
            for j, (cx, cy) in enumerate(chips):
                land = outs[t].at[2 * cx + cy]
                pltpu.make_async_remote_copy(
                    src_ref=land, dst_ref=land, send_sem=send_sems.at[3 * t + j], recv_sem=recv_sems.at[3 * t + j],
                    device_id=(cx, cy, c), device_id_type=MESH).wait_recv()
        for cp in cps:
            cp.wait_send()

    nsem = 3 * len(names)
    outs = pl.pallas_call(
        body, name="scatter_shards", in_specs=[HBM] * len(names), out_specs=[HBM] * len(names),
        out_shape=[jax.ShapeDtypeStruct((N_CHIPS,) + _shard_shape(n), sums[n].dtype) for n in names],
        scratch_shapes=[pltpu.SemaphoreType.DMA((nsem,)), pltpu.SemaphoreType.DMA((nsem,))],
    )(*[sums[n] for n in names])
    return dict(zip(names, outs))


def _sum_slots(slots, own, name):
    ks, ns = _shard_shape(name)
    tr = min(ks, 256)
    shape, index = _shard_block(name, tr)

    def body(me_ref, c_ref, s_ref, own_ref, out_ref):
        me = me_ref[0]
        acc = None
        for s in range(N_CHIPS):
            term = jnp.where(me == s, own_ref[...], s_ref[s]).astype(f32)
            acc = term if acc is None else acc + term
        out_ref[...] = acc

    return pl.pallas_call(
        body, name="sum_slots_" + name,
        grid_spec=pltpu.PrefetchScalarGridSpec(
            num_scalar_prefetch=2, grid=(ks // tr,),
            in_specs=[pl.BlockSpec((N_CHIPS, tr, ns), lambda i, me, c: (0, i, 0)),
                      pl.BlockSpec(shape, lambda i, me, c: index(i, me))],
            out_specs=pl.BlockSpec((None, tr, ns), lambda i, me, c: (c[0], i, 0))),
        out_shape=jax.ShapeDtypeStruct((2, ks, ns), f32),
        compiler_params=_cparams(("parallel",)),
    )(_chip_index(), jnp.reshape(lax.axis_index("c"), (1,)).astype(jnp.int32), slots, own)


N_DEV = 8


def _reduce_copies(dws, lands, names, layer, send_sems, recv_sems):
    x, y, c = _my_place()
    me, my_dev = 2 * x + y, 4 * x + 2 * y + c
    out = []
    for t, name in enumerate(names):
        for j, (cx, cy) in enumerate(_other_chips(x, y)):
            out.append((pltpu.make_async_remote_copy(
                src_ref=_window(dws[t], name, 2 * cx + cy), dst_ref=lands[t].at[my_dev],
                send_sem=send_sems.at[4 * t + j], recv_sem=recv_sems.at[N_DEV * t + my_dev],
                device_id=(cx, cy, layer), device_id_type=MESH), False))
        out.append((pltpu.make_async_remote_copy(
            src_ref=_window(dws[t], name, me), dst_ref=lands[t].at[my_dev],
            send_sem=send_sems.at[4 * t + 3], recv_sem=recv_sems.at[N_DEV * t + my_dev],
            device_id=(x, y, layer), device_id_type=MESH), True))
    return out


def _reduce_start(dws, names, layer, tag):
    n = len(names)
    lands = [lax.empty((N_DEV,) + _shard_shape(nm), dws[0].dtype) for nm in names]

    def body(*refs):
        ins = refs[:n]
        send_sems, recv_sems = refs[2 * n], refs[2 * n + 1]
        land_out = refs[3 * n + 2:4 * n + 2]
        token = refs[-1]
        c = lax.axis_index("c")
        for cp, non_owner_only in _reduce_copies(ins, land_out, names, layer, send_sems, recv_sems):
            if non_owner_only:
                @pl.when(c != layer)
                def _():
                    cp.start()
            else:
                cp.start()
        token[...] = jnp.zeros_like(token)

    res = pl.pallas_call(
        body, name="reduce_start" + tag,
        out_shape=(pltpu.SemaphoreType.DMA((4 * n,)), pltpu.SemaphoreType.DMA((N_DEV * n,)))
        + tuple(pltpu.HBM(a.shape, a.dtype) for a in dws) + tuple(pltpu.HBM(a.shape, a.dtype) for a in lands)
        + (jax.ShapeDtypeStruct((8, 128), f32),),
        in_specs=[HBM] * (2 * n),
        out_specs=(SEM, SEM) + (HBM,) * (2 * n) + (pl.BlockSpec(memory_space=pltpu.VMEM),),
        input_output_aliases={i: i + 2 for i in range(2 * n)},
        compiler_params=pltpu.CompilerParams(has_side_effects=EFFECT),
    )(*[pltpu.with_memory_space_constraint(a, pltpu.HBM) for a in list(dws) + lands])
    return res[0], res[1], list(res[2:2 + n]), list(res[2 + n:2 + 2 * n]), res[-1]


def _reduce_wait(send_sems, recv_sems, dws, lands, names, layer, after, tag):
    n = len(names)

    def body(*refs):
        ins, land_in = refs[:n], refs[n:2 * n]
        send_ref, recv_ref = refs[2 * n], refs[2 * n + 1]
        x, y, c = _my_place()
        for cp, non_owner_only in _reduce_copies(ins, land_in, names, layer, send_ref, recv_ref):
            if non_owner_only:
                @pl.when(c != layer)
                def _():
                    cp.wait_send()
            else:
                cp.wait_send()

        @pl.when(c == layer)
        def _():
            for t in range(n):
                for k in range(1, N_DEV):
                    px, py, pc = x ^ ((k >> 2) & 1), y ^ ((k >> 1) & 1), c ^ (k & 1)
                    dev = 4 * px + 2 * py + pc
                    land = land_in[t].at[dev]
                    pltpu.make_async_remote_copy(
                        src_ref=land, dst_ref=land, send_sem=send_ref.at[4 * t], recv_sem=recv_ref.at[N_DEV * t + dev],
                        device_id=(px, py, pc), device_id_type=MESH).wait_recv()

    res = pl.pallas_call(
        body, name="reduce_wait" + tag,
        out_shape=tuple(pltpu.HBM(a.shape, a.dtype) for a in list(dws) + list(lands)),
        in_specs=[HBM] * (2 * n) + [SEM, SEM, pl.BlockSpec(memory_space=pl.ANY)], out_specs=(HBM,) * (2 * n),
        input_output_aliases={i: i for i in range(2 * n)},
        compiler_params=pltpu.CompilerParams(has_side_effects=EFFECT),
    )(*dws, *lands, send_sems, recv_sems, after)
    return list(res[:n]), list(res[n:])


def _sum_devices(land, own, name, layer, prev):
    ks, ns = _shard_shape(name)
    tr = min(ks, 256)
    shape, index = _shard_block(name, tr)

    def body(me_ref, dev_ref, *refs):
        s_ref, own_ref, out_ref = refs[0], refs[1], refs[-1]
        dev = dev_ref[0]
        acc = None
        for s in range(N_DEV):
            term = jnp.where(dev == s, own_ref[...], s_ref[s]).astype(f32)
            acc = term if acc is None else acc + term
        out_ref[...] = acc

    in_specs = [pl.BlockSpec((N_DEV, tr, ns), lambda i, me, dev: (0, i, 0)),
                pl.BlockSpec(shape, lambda i, me, dev: index(i, me))]
    args = [land, own]
    aliases = {}
    if prev is not None:
        in_specs.append(pl.BlockSpec(memory_space=pl.ANY))
        args.append(prev)
        aliases = {4: 0}
    x, y, c = _my_place()
    return pl.pallas_call(
        body, name=f"sum_devices_{name}{layer}",
        grid_spec=pltpu.PrefetchScalarGridSpec(
            num_scalar_prefetch=2, grid=(ks // tr,), in_specs=in_specs,
            out_specs=pl.BlockSpec((None, tr, ns), lambda i, me, dev: (layer, i, 0))),
        out_shape=jax.ShapeDtypeStruct((2, ks, ns), f32), input_output_aliases=aliases,
        compiler_params=_cparams(("parallel",)),
    )(_chip_index(), jnp.reshape(4 * x + 2 * y + c, (1,)).astype(jnp.int32), *args)


class _GradReducer:
    GROUPS = (("1", 1, ("w_gate", "w_ple", "w_down", "w_up", "w_out", "w_in")),
              ("0a", 0, ("w_gate", "w_ple", "w_down", "w_up")),
              ("0b", 0, ("w_out", "w_in")))

    def __init__(self):
        self.grads = {}
        self.started = {}

    def add(self, name, layer, dw):
        self.grads[(name, layer)] = dw
        token = None
        for tag, glayer, names in self.GROUPS:
            if tag not in self.started and all((nm, glayer) in self.grads for nm in names):
                *self.started[tag], token = _reduce_start([self.grads[(nm, glayer)] for nm in names], names, glayer, tag)
        return token

    def finish(self, after):
        mine = {}
        for tag, layer, names in self.GROUPS:
            send, recv, dws, lands = self.started[tag]
            dws, lands = _reduce_wait(send, recv, dws, lands, names, layer, after, tag)
            for nm, dw, land in zip(names, dws, lands):
                mine[nm] = _sum_devices(land, dw, nm, layer, mine.get(nm))
        return _pair_layers(mine)


def _pair_layers(mine):
    names = list(BIG)

    def body(*refs):
        ins = refs[:len(names)]
        outs = refs[len(names):2 * len(names)]
        send_sems, recv_sems = refs[2 * len(names):]
        x, y, c = _my_place()
        sibling = (x, y, 1 - c)
        cps = []
        for t in range(len(names)):
            cp = pltpu.make_async_remote_copy(
                src_ref=ins[t].at[c], dst_ref=outs[t].at[c], send_sem=send_sems.at[t], recv_sem=recv_sems.at[t],
                device_id=sibling, device_id_type=MESH)
            cp.start()
            cps.append(cp)
        for t in range(len(names)):
            cps[t].wait_send()
            land = outs[t].at[1 - c]
            pltpu.make_async_remote_copy(
                src_ref=land, dst_ref=land, send_sem=send_sems.at[t], recv_sem=recv_sems.at[t],
                device_id=sibling, device_id_type=MESH).wait_recv()

    outs = pl.pallas_call(
        body, name="pair_layers", in_specs=[HBM] * len(names), out_specs=[HBM] * len(names),
        out_shape=[jax.ShapeDtypeStruct((2,) + _shard_shape(n), f32) for n in names],
        input_output_aliases={t: t for t in range(len(names))},
        scratch_shapes=[pltpu.SemaphoreType.DMA((len(names),)), pltpu.SemaphoreType.DMA((len(names),))],
    )(*[mine[n] for n in names])
    return dict(zip(names, outs))


SMALL_ROWS = 320


def _allreduce_small(vec):
    n_dev = 8

    def body(v_ref, out_ref, buf_ref, send_sems, recv_sems):
        x, y, c = _my_place()
        me = 4 * x + 2 * y + c
        buf_ref[me] = v_ref[...]
        cps = []
        for k in range(1, n_dev):
            dx, dy, dc = (k >> 2) & 1, (k >> 1) & 1, k & 1
            peer = (x ^ dx, y ^ dy, c ^ dc)
            cp = pltpu.make_async_remote_copy(
                src_ref=v_ref, dst_ref=buf_ref.at[me], send_sem=send_sems.at[k - 1], recv_sem=recv_sems.at[k - 1],
                device_id=peer, device_id_type=MESH)
            cp.start()
            cps.append(cp)
        for k in range(1, n_dev):
            dx, dy, dc = (k >> 2) & 1, (k >> 1) & 1, k & 1
            src = 4 * (x ^ dx) + 2 * (y ^ dy) + (c ^ dc)
            land = buf_ref.at[src]
            pltpu.make_async_remote_copy(
                src_ref=land, dst_ref=land, send_sem=send_sems.at[k - 1], recv_sem=recv_sems.at[k - 1],
                device_id=(x ^ dx, y ^ dy, c ^ dc), device_id_type=MESH).wait_recv()
        for cp in cps:
            cp.wait_send()
        acc = buf_ref[0]
        for s in range(1, n_dev):
            acc = acc + buf_ref[s]
        out_ref[...] = acc

    return pl.pallas_call(
        body, name="allreduce_small",
        in_specs=[pl.BlockSpec(memory_space=pltpu.VMEM)], out_specs=pl.BlockSpec(memory_space=pltpu.VMEM),
        out_shape=jax.ShapeDtypeStruct((SMALL_ROWS, 128), f32),
        scratch_shapes=[pltpu.VMEM((n_dev, SMALL_ROWS, 128), f32), pltpu.SemaphoreType.DMA((n_dev - 1,)),
                        pltpu.SemaphoreType.DMA((n_dev - 1,))],
    )(vec)


def _adamw(w, g, m, v, name):
    rows, cols = w.shape
    tr = rows
    for cand in (512, 256, 128, 64, 32, 16, 8):
        if rows % cand == 0 and cand * cols * 4 <= 2 * 1024 * 1024:
            tr = cand
            break
    c1 = np.float32(1.0 - ADAM_B1 ** ADAM_STEP)
    c2 = np.float32(1.0 - ADAM_B2 ** ADAM_STEP)

    def body(w_ref, g_ref, m_ref, v_ref, go_ref, d_ref, mo_ref, vo_ref):
        gv = g_ref[...]
        go_ref[...] = gv
        mn = ADAM_B1 * m_ref[...] + (1.0 - ADAM_B1) * gv
        vn = ADAM_B2 * v_ref[...] + (1.0 - ADAM_B2) * (gv * gv)
        mo_ref[...] = mn
        vo_ref[...] = vn
        d_ref[...] = -ADAM_LR * ((mn / c1) / (jnp.sqrt(vn / c2) + ADAM_EPS) + ADAM_WD * w_ref[...])

    blk = pl.BlockSpec((tr, cols), lambda i: (i, 0))
    return pl.pallas_call(
        body, name="adamw_" + name, grid=(rows // tr,), in_specs=[blk] * 4, out_specs=[blk] * 4,
        out_shape=[jax.ShapeDtypeStruct((rows, cols), f32)] * 4,
        compiler_params=_cparams(("parallel",)),
    )(w, g, m, v)


SMALL = ("norm1", "pool_w", "pool_scale", "norm2", "norm3", "final_norm")
ORDER = ("norm1", "w_in", "pool_w", "pool_scale", "w_out", "norm2", "w_up", "w_down", "norm3", "w_gate", "w_ple",
         "final_norm")


def _pack_small(tree, extra=None):
    parts = [tree[n].reshape(-1) for n in SMALL]
    if extra is not None:
        parts.append(extra.reshape(-1))
    flat = jnp.concatenate(parts)
    return jnp.pad(flat, (0, SMALL_ROWS * 128 - flat.shape[0])).reshape(SMALL_ROWS, 128)


def _unpack_small(packed, like):
    flat = packed.reshape(-1)
    out, off = {}, 0
    for n in SMALL:
        size = int(np.prod(like[n].shape))
        out[n] = flat[off:off + size].reshape(like[n].shape)
        off += size
    return out, flat[off]


def kernel(x, p, positions, norm1, w_in, pool_w, pool_scale, w_out, norm2, w_up, w_down, norm3, w_gate, w_ple, final_norm, loss_target, m_norm1, m_w_in, m_pool_w, m_pool_scale, m_w_out, m_norm2, m_w_up, m_w_down, m_norm3, m_w_gate, m_w_ple, m_final_norm, v_norm1, v_w_in, v_pool_w, v_pool_scale, v_w_out, v_norm2, v_w_up, v_w_down, v_norm3, v_w_gate, v_w_ple, v_final_norm):
    w = dict(norm1=norm1, w_in=w_in, pool_w=pool_w, pool_scale=pool_scale, w_out=w_out, norm2=norm2, w_up=w_up,
             w_down=w_down, norm3=norm3, w_gate=w_gate, w_ple=w_ple, final_norm=final_norm)
    m = dict(norm1=m_norm1, w_in=m_w_in, pool_w=m_pool_w, pool_scale=m_pool_scale, w_out=m_w_out, norm2=m_norm2,
             w_up=m_w_up, w_down=m_w_down, norm3=m_norm3, w_gate=m_w_gate, w_ple=m_w_ple, final_norm=m_final_norm)
    v = dict(norm1=v_norm1, w_in=v_w_in, pool_w=v_pool_w, pool_scale=v_pool_scale, w_out=v_w_out, norm2=v_norm2,
             w_up=v_w_up, w_down=v_w_down, norm3=v_norm3, w_gate=v_w_gate, w_ple=v_w_ple, final_norm=v_final_norm)
    small = {n: w[n] for n in SMALL}

    wsrc = _GatheredWeights({n: w[n] for n in BIG})
    reducer = _GradReducer()
    loss8, dx, small_grads = _local_step(x[0], p[:, 0], positions[0], wsrc, small, loss_target[0], reducer)
    gsh = reducer.finish(dx)

    red = _allreduce_small(_pack_small(small_grads, loss8[0, 0]))
    g_small, loss = _unpack_small(red, small)

    g_out, d_out, m_out, v_out = {}, {}, {}, {}
    for n in BIG:
        shp = w[n].shape
        two = lambda a: a.reshape(shp[0] * shp[1], shp[2])
        g2, d2, m2, v2 = _adamw(two(w[n]), two(gsh[n]), two(m[n]), two(v[n]), n)
        g_out[n], d_out[n], m_out[n], v_out[n] = g2.reshape(shp), d2.reshape(shp), m2.reshape(shp), v2.reshape(shp)
    _, d2, m2, v2 = _adamw(_pack_small(small), red, _pack_small({n: m[n] for n in SMALL}),
                           _pack_small({n: v[n] for n in SMALL}), "small")
    for tree, packed in ((d_out, d2), (m_out, m2), (v_out, v2)):
        tree.update(_unpack_small(packed, small)[0])
    g_out.update(g_small)

    return (loss, dx[None], *[g_out[n] for n in ORDER], *[d_out[n] for n in ORDER], *[m_out[n] for n in ORDER],
            *[v_out[n] for n in ORDER])
```

```python
import functools

import jax
import jax.numpy as jnp
import numpy as np
from jax import lax
from jax.experimental import pallas as pl
from jax.experimental.pallas import tpu as pltpu

f32 = jnp.float32
MXU_DTYPE = jnp.bfloat16
COMM_DTYPE = jnp.bfloat16

D_MODEL = 1024
POOL_WIDTH = 256
POOL_GC = 64
ATTN_WIDTH = 768
HEAD_DIM = 64
N_IN = POOL_WIDTH + 3 * ATTN_WIDTH
D_FF = 4096
PLE_DIM = 256
BLK = 128
DILATIONS = (1, 4, 16)
ROT_DIM = 16
ROPE_THETA = 500000.0
EPS = 1e-6
ATTN_SCALE = HEAD_DIM ** -0.5
NEG_BIG = -1e30

ADAM_LR, ADAM_B1, ADAM_B2, ADAM_EPS, ADAM_WD, ADAM_STEP = 0.001, 0.9, 0.999, 1e-08, 0.01, 10

TM = 512
TM_WGRAD = 1024
HALO = 16
VMEM_LIMIT = 48 * 1024 * 1024
N_CHIPS = 4
MESH = pl.DeviceIdType.MESH

BIG = ("w_in", "w_out", "w_up", "w_down", "w_gate", "w_ple")
FULL_SHAPE = {"w_in": (D_MODEL, N_IN), "w_out": (D_MODEL, D_MODEL), "w_up": (D_MODEL, D_FF),
              "w_down": (D_FF, D_MODEL), "w_gate": (D_MODEL, D_MODEL), "w_ple": (PLE_DIM, D_MODEL)}
COL_SHARDED = {"w_in": True, "w_out": False, "w_up": True, "w_down": False, "w_gate": False, "w_ple": True}


def _shard_shape(name):
    k, n = FULL_SHAPE[name]
    return (k, n // N_CHIPS) if COL_SHARDED[name] else (k // N_CHIPS, n)


def _cparams(sem=None, vmem=VMEM_LIMIT):
    return pltpu.CompilerParams(dimension_semantics=sem, vmem_limit_bytes=vmem)


def _resident(block_shape, index_map):
    return pl.BlockSpec(block_shape, index_map, pipeline_mode=pl.Buffered(1))


def _mx(x):
    return x.astype(MXU_DTYPE)


def _dot(a, b):
    return jnp.dot(a, b, preferred_element_type=f32)


def _dot_nt(a, b):
    return lax.dot_general(a, b, (((1,), (1,)), ((), ())), preferred_element_type=f32)


def _dot_tn(a, b):
    return lax.dot_general(a, b, (((0,), (0,)), ((), ())), preferred_element_type=f32)


def _sigmoid(x):
    return 1.0 / (1.0 + jnp.exp(-x))


def _rope_apply(y, c, s1, s2, width):
    return y * c + pltpu.roll(y, width - 8, axis=1) * s1 + pltpu.roll(y, 8, axis=1) * s2


def _rope_transpose(dy, c, s1, s2, width):
    return dy * c + pltpu.roll(dy * s1, 8, axis=1) + pltpu.roll(dy * s2, width - 8, axis=1)


def _norm_matmul(h, g, w, layer, tn, name, rope=None):
    s_len, d = h.shape
    n = w.shape[2]

    def body(*refs):
        if rope is None:
            h_ref, g_ref, w_ref, y_ref, hn_ref = refs
        else:
            h_ref, g_ref, w_ref, c_ref, s1_ref, s2_ref, y_ref, hn_ref = refs
            reps = tn // 128
            c = jnp.concatenate([c_ref[...]] * reps, axis=1)
            s1 = jnp.concatenate([s1_ref[...]] * reps, axis=1)
            s2 = jnp.concatenate([s2_ref[...]] * reps, axis=1)
        x = h_ref[...]
        r = lax.rsqrt(jnp.mean(x * x, axis=-1, keepdims=True) + EPS)
        hn = ((x * r) * g_ref[...]).astype(hn_ref.dtype)
        hn_ref[...] = hn
        for j in range(n // tn):
            y = _dot(hn, w_ref[:, j * tn:(j + 1) * tn])
            if rope is not None and POOL_WIDTH <= j * tn < POOL_WIDTH + 2 * ATTN_WIDTH:
                y = _rope_apply(y, c, s1, s2, tn)
            y_ref[:, j * tn:(j + 1) * tn] = y

    in_specs = [pl.BlockSpec((TM, d), lambda i: (i, 0)),
                pl.BlockSpec((1, d), lambda i: (0, 0)),
                _resident((None, d, n), lambda i: (layer, 0, 0))]
    args = [h, g, w]
    if rope is not None:
        assert POOL_WIDTH % tn == 0 and (2 * ATTN_WIDTH) % tn == 0
        in_specs += [pl.BlockSpec((TM, 128), lambda i: (i, 0))] * 3
        args += list(rope)
    return pl.pallas_call(
        body, name=name, grid=(s_len // TM,), in_specs=in_specs,
        out_specs=[pl.BlockSpec((TM, n), lambda i: (i, 0)), pl.BlockSpec((TM, d), lambda i: (i, 0))],
        out_shape=[jax.ShapeDtypeStruct((s_len, n), f32), jax.ShapeDtypeStruct((s_len, d), MXU_DTYPE)],
        compiler_params=_cparams(("parallel",)),
    )(*args)


def _matmul_residual(a, w, layer, res, name, act=False, tk=1024):
    s_len, k_dim = a.shape
    n = w.shape[2]

    def body(a_ref, w_ref, res_ref, o_ref):
        acc = res_ref[...]
        for k in range(k_dim // tk):
            x = a_ref[:, k * tk:(k + 1) * tk]
            if act:
                r = jnp.maximum(x, 0.0)
                x = r * r
            acc = acc + _dot(_mx(x), w_ref[k * tk:(k + 1) * tk, :])
        o_ref[...] = acc

    return pl.pallas_call(
        body, name=name, grid=(s_len // TM,),
        in_specs=[pl.BlockSpec((TM, k_dim), lambda i: (i, 0)),
                  _resident((None, k_dim, n), lambda i: (layer, 0, 0)),
                  pl.BlockSpec((TM, n), lambda i: (i, 0))],
        out_specs=pl.BlockSpec((TM, n), lambda i: (i, 0)),
        out_shape=jax.ShapeDtypeStruct((s_len, n), f32),
        compiler_params=_cparams(("parallel",)),
    )(a, w, res)


def _gate_ple_fwd(h2, g, w_gate, w_ple, layer, p, name):
    s_len, d = h2.shape

    def body(h_ref, g_ref, wg_ref, p_ref, wp_ref, h3_ref, gl_ref, hn_ref):
        x = h_ref[...]
        r = lax.rsqrt(jnp.mean(x * x, axis=-1, keepdims=True) + EPS)
        hn = ((x * r) * g_ref[...]).astype(hn_ref.dtype)
        hn_ref[...] = hn
        gl = _dot(hn, wg_ref[...])
        gl_ref[...] = gl
        e = _dot(_mx(p_ref[...]), wp_ref[...])
        h3_ref[...] = x + _sigmoid(gl) * e

    row = lambda i: (i, 0)
    return pl.pallas_call(
        body, name=name, grid=(s_len // TM,),
        in_specs=[pl.BlockSpec((TM, d), row), pl.BlockSpec((1, d), lambda i: (0, 0)),
                  pl.BlockSpec((None, d, d), lambda i: (layer, 0, 0)), pl.BlockSpec((TM, PLE_DIM), row),
                  pl.BlockSpec((None, PLE_DIM, d), lambda i: (layer, 0, 0))],
        out_specs=[pl.BlockSpec((TM, d), row)] * 3,
        out_shape=[jax.ShapeDtypeStruct((s_len, d), f32), jax.ShapeDtypeStruct((s_len, d), f32),
                   jax.ShapeDtypeStruct((s_len, d), MXU_DTYPE)],
        compiler_params=_cparams(("parallel",)),
    )(h2, g, w_gate, p, w_ple)


def _gate_ple_bwd(dh3, gl, p, w_ple, layer, name):
    s_len, d = dh3.shape

    def body(dh_ref, gl_ref, p_ref, wp_ref, de_ref, dgl_ref):
        dh = dh_ref[...]
        gate = _sigmoid(gl_ref[...])
        e = _dot(_mx(p_ref[...]), wp_ref[...])
        de_ref[...] = (dh * gate).astype(de_ref.dtype)
        dgl_ref[...] = ((dh * e) * (gate * (1.0 - gate))).astype(dgl_ref.dtype)

    row = lambda i: (i, 0)
    return pl.pallas_call(
        body, name=name, grid=(s_len // TM,),
        in_specs=[pl.BlockSpec((TM, d), row), pl.BlockSpec((TM, d), row), pl.BlockSpec((TM, PLE_DIM), row),
                  pl.BlockSpec((None, PLE_DIM, d), lambda i: (layer, 0, 0))],
        out_specs=[pl.BlockSpec((TM, d), row)] * 2,
        out_shape=[jax.ShapeDtypeStruct((s_len, d), MXU_DTYPE)] * 2,
        compiler_params=_cparams(("parallel",)),
    )(dh3, gl, p, w_ple)


def _rmsnorm_bwd(dhn, x, g):
    r = lax.rsqrt(jnp.mean(x * x, axis=-1, keepdims=True) + EPS)
    xh = x * r
    dxh = dhn * g
    dx = r * (dxh - xh * jnp.mean(dxh * xh, axis=-1, keepdims=True))
    return dx, dhn * xh


def _matmul_nt_norm_bwd(dy, w, layer, h_prev, g, dres, name, tk=1024, after=None):
    s_len, k_dim = dy.shape
    d = h_prev.shape[1]

    def body(dy_ref, w_ref, h_ref, g_ref, dres_ref, *rest):
        dh_ref, dg_ref = rest[-2:]
        i = pl.program_id(0)
        acc = None
        for k in range(k_dim // tk):
            part = _dot_nt(_mx(dy_ref[:, k * tk:(k + 1) * tk]), w_ref[:, k * tk:(k + 1) * tk])
            acc = part if acc is None else acc + part
        dx, dgrow = _rmsnorm_bwd(acc, h_ref[...], g_ref[...])
        dh_ref[...] = dres_ref[...] + dx
        dgsum = jnp.sum(dgrow, axis=0, keepdims=True)

        @pl.when(i == 0)
        def _():
            dg_ref[...] = dgsum

        @pl.when(i > 0)
        def _():
            dg_ref[...] += dgsum

    in_specs = [pl.BlockSpec((TM, k_dim), lambda i: (i, 0)),
                _resident((None, d, k_dim), lambda i: (layer, 0, 0)),
                pl.BlockSpec((TM, d), lambda i: (i, 0)),
                pl.BlockSpec((1, d), lambda i: (0, 0)),
                pl.BlockSpec((TM, d), lambda i: (i, 0))]
    args = [dy, w, h_prev, g, dres]
    if after is not None:
        in_specs.append(pl.BlockSpec(memory_space=pl.ANY))
        args.append(after)
    return pl.pallas_call(
        body, name=name, grid=(s_len // TM,), in_specs=in_specs,
        out_specs=[pl.BlockSpec((TM, d), lambda i: (i, 0)), pl.BlockSpec((1, d), lambda i: (0, 0))],
        out_shape=[jax.ShapeDtypeStruct((s_len, d), f32), jax.ShapeDtypeStruct((1, d), f32)],
        compiler_params=_cparams(("arbitrary",)),
    )(*args)


def _mlp_fwd(h1, g, w_up, w_down, layer, name, tf=1024):
    s_len, d = h1.shape
    ff = w_up.shape[2]

    def body(h_ref, g_ref, wu_ref, wd_ref, h2_ref, a_ref, hn_ref):
        x = h_ref[...]
        r = lax.rsqrt(jnp.mean(x * x, axis=-1, keepdims=True) + EPS)
        hn = ((x * r) * g_ref[...]).astype(hn_ref.dtype)
        hn_ref[...] = hn
        acc = x
        for j in range(ff // tf):
            a = _dot(hn, wu_ref[:, j * tf:(j + 1) * tf])
            a_ref[:, j * tf:(j + 1) * tf] = a.astype(a_ref.dtype)
            relu = jnp.maximum(a, 0.0)
            acc = acc + _dot(_mx(relu * relu), wd_ref[j * tf:(j + 1) * tf, :])
        h2_ref[...] = acc

    row = lambda i: (i, 0)
    return pl.pallas_call(
        body, name=name, grid=(s_len // TM,),
        in_specs=[pl.BlockSpec((TM, d), row), pl.BlockSpec((1, d), lambda i: (0, 0)),
                  _resident((None, d, ff), lambda i: (layer, 0, 0)), _resident((None, ff, d), lambda i: (layer, 0, 0))],
        out_specs=[pl.BlockSpec((TM, d), row), pl.BlockSpec((TM, ff), row), pl.BlockSpec((TM, d), row)],
        out_shape=[jax.ShapeDtypeStruct((s_len, d), f32), jax.ShapeDtypeStruct((s_len, ff), MXU_DTYPE),
                   jax.ShapeDtypeStruct((s_len, d), MXU_DTYPE)],
        compiler_params=_cparams(("parallel",)),
    )(h1, g, w_up, w_down)


def _down_bwd(dh2, w_down, layer, a, name, tf=1024):
    s_len, d = dh2.shape
    ff = a.shape[1]

    def body(dh_ref, w_ref, a_ref, da_ref, dhb_ref):
        j = pl.program_id(1)

        @pl.when(j == 0)
        def _():
            dhb_ref[...] = _mx(dh_ref[...])

        dact = _dot_nt(dhb_ref[...], w_ref[pl.ds(pl.multiple_of(j * tf, tf), tf), :])
        da_ref[...] = (dact * (2.0 * jnp.maximum(a_ref[...].astype(f32), 0.0))).astype(da_ref.dtype)

    return pl.pallas_call(
        body, name=name, grid=(s_len // TM, ff // tf),
        in_specs=[pl.BlockSpec((TM, d), lambda i, j: (i, 0)),
                  _resident((None, ff, d), lambda i, j: (layer, 0, 0)),
                  pl.BlockSpec((TM, tf), lambda i, j: (i, j))],
        out_specs=pl.BlockSpec((TM, tf), lambda i, j: (i, j)),
        out_shape=jax.ShapeDtypeStruct((s_len, ff), MXU_DTYPE),
        scratch_shapes=[pltpu.VMEM((TM, d), MXU_DTYPE)],
        compiler_params=_cparams(("parallel", "arbitrary")),
    )(dh2, w_down, a)


def _matmul_nt(dy, w, layer, name):
    s_len, n = dy.shape
    k_dim = w.shape[1]

    def body(dy_ref, w_ref, o_ref):
        o_ref[...] = _dot_nt(_mx(dy_ref[...]), w_ref[...])

    return pl.pallas_call(
        body, name=name, grid=(s_len // TM,),
        in_specs=[pl.BlockSpec((TM, n), lambda i: (i, 0)), pl.BlockSpec((None, k_dim, n), lambda i: (layer, 0, 0))],
        out_specs=pl.BlockSpec((TM, k_dim), lambda i: (i, 0)),
        out_shape=jax.ShapeDtypeStruct((s_len, k_dim), f32),
        compiler_params=_cparams(("parallel",)),
    )(dy, w)


def _weight_grad(a, b, name, act=False):
    s_len, k_dim = a.shape
    n = b.shape[1]
    tka = min(k_dim, 2048)
    tnb = n if n <= 1024 else (2048 if n % 2048 == 0 else 640)
    ns = s_len // TM_WGRAD

    def body(a_ref, b_ref, o_ref, acc_ref):
        s = pl.program_id(2)
        x = a_ref[...]
        if act:
            relu = jnp.maximum(x.astype(f32), 0.0)
            x = relu * relu
        part = _dot_tn(_mx(x), _mx(b_ref[...]))

        @pl.when(s == 0)
        def _():
            acc_ref[...] = part

        @pl.when(s > 0)
        def _():
            acc_ref[...] += part

        @pl.when(s == ns - 1)
        def _():
            o_ref[...] = acc_ref[...].astype(o_ref.dtype)

    return pl.pallas_call(
        body, name=name, grid=(k_dim // tka, n // tnb, ns),
        in_specs=[pl.BlockSpec((TM_WGRAD, tka), lambda i, j, s: (s, i)),
                  pl.BlockSpec((TM_WGRAD, tnb), lambda i, j, s: (s, j))],
        out_specs=pl.BlockSpec((tka, tnb), lambda i, j, s: (i, j)),
        out_shape=jax.ShapeDtypeStruct((k_dim, n), COMM_DTYPE),
        scratch_shapes=[pltpu.VMEM((tka, tnb), f32)],
        compiler_params=_cparams(("parallel", "parallel", "arbitrary")),
    )(a, b)


def _group_select(lane, x2, x4, x8, x16):
    grp = lane // POOL_GC
    return jnp.where(grp == 0, x2, jnp.where(grp == 1, x4, jnp.where(grp == 2, x8, x16)))


def _pool_window(lane):
    grp = lane // POOL_GC
    return jnp.where(grp == 0, 2, jnp.where(grp == 1, 4, jnp.where(grp == 2, 8, 16)))


def _pool_y(u, halo, i):
    xs = jnp.concatenate([jnp.where(i > 0, halo, 0.0), u], axis=0)
    s2 = xs + pltpu.roll(xs, 1, axis=0)
    s4 = s2 + pltpu.roll(s2, 2, axis=0)
    s8 = s4 + pltpu.roll(s4, 4, axis=0)
    s16 = s8 + pltpu.roll(s8, 8, axis=0)
    lane = lax.broadcasted_iota(jnp.int32, xs.shape, 1)
    sel = _group_select(lane, s2, s4, s8, s16)[HALO:, :]
    t = i * TM + lax.broadcasted_iota(jnp.int32, u.shape, 0)
    cnt = jnp.minimum(_pool_window(lax.broadcasted_iota(jnp.int32, u.shape, 1)), t + 1).astype(f32)
    return sel / cnt - u


def _group_weights(l0, l1, l2):
    mx = jnp.maximum(jnp.maximum(l0, l1), l2)
    e0, e1, e2 = jnp.exp(l0 - mx), jnp.exp(l1 - mx), jnp.exp(l2 - mx)
    den = e0 + e1 + e2
    return e0 / den, e1 / den, e2 / den


def _mixer_merge(z, wbd, scale, outs, lses, name):
    s_len = z.shape[0]

    def body(u_ref, halo_ref, wbd_ref, sc_ref, o0, o1, o2, l0, l1, l2, m_ref):
        i = pl.program_id(0)
        y = _pool_y(u_ref[...], halo_ref[...], i)
        pool = _dot(_mx(y), wbd_ref[...]) * sc_ref[...]
        w0, w1, w2 = _group_weights(l0[...], l1[...], l2[...])
        m_ref[...] = jnp.concatenate([pool, o0[...] * w0, o1[...] * w1, o2[...] * w2], axis=1).astype(m_ref.dtype)

    row = lambda i: (i, 0)
    blk = pl.BlockSpec((TM, 256), row)
    grp = [pl.BlockSpec((TM, 256), lambda i, g=g: (i, g)) for g in range(3)]
    return pl.pallas_call(
        body, name=name, grid=(s_len // TM,),
        in_specs=[blk, pl.BlockSpec((HALO, 256), lambda i: (jnp.maximum(i * (TM // HALO) - 1, 0), 0)),
                  pl.BlockSpec((256, 256), lambda i: (0, 0)), pl.BlockSpec((1, 256), lambda i: (0, 0))] + grp + grp,
        out_specs=pl.BlockSpec((TM, D_MODEL), row),
        out_shape=jax.ShapeDtypeStruct((s_len, D_MODEL), MXU_DTYPE),
        compiler_params=_cparams(("parallel",)),
    )(z, z, wbd, scale, outs, outs, outs, lses, lses, lses)


def _head_sums(x):
    r = lax.broadcasted_iota(jnp.int32, (256, 256), 0) // HEAD_DIM
    c = lax.broadcasted_iota(jnp.int32, (256, 256), 1) // HEAD_DIM
    ones = jnp.where(r == c, 1.0, 0.0).astype(jnp.bfloat16)
    hi = x.astype(jnp.bfloat16)
    lo = (x - hi.astype(f32)).astype(jnp.bfloat16)
    return _dot(hi, ones) + _dot(lo, ones)


def _combine_bwd(dm, outs, lses, name):
    s_len = dm.shape[0]

    def body(d0, d1, d2, o0, o1, o2, l0, l1, l2, do_ref, dl_ref):
        w = _group_weights(l0[...], l1[...], l2[...])
        da = (d0[...], d1[...], d2[...])
        o = (o0[...], o1[...], o2[...])
        dw = [_head_sums(da[g] * o[g]) for g in range(3)]
        t = w[0] * dw[0] + w[1] * dw[1] + w[2] * dw[2]
        do_ref[...] = jnp.concatenate([da[g] * w[g] for g in range(3)], axis=1)
        dl_ref[...] = jnp.concatenate([w[g] * t for g in range(3)], axis=1)

    grp = [pl.BlockSpec((TM, 256), lambda i, g=g: (i, g)) for g in range(3)]
    return pl.pallas_call(
        body, name=name, grid=(s_len // TM,),
        in_specs=[pl.BlockSpec((TM, 256), lambda i: (i, 1)), pl.BlockSpec((TM, 256), lambda i: (i, 2)),
                  pl.BlockSpec((TM, 256), lambda i: (i, 3))] + grp + grp,
        out_specs=[pl.BlockSpec((TM, ATTN_WIDTH), lambda i: (i, 0))] * 2,
        out_shape=[jax.ShapeDtypeStruct((s_len, ATTN_WIDTH), f32)] * 2,
        compiler_params=_cparams(("parallel",)),
    )(dm, dm, dm, outs, outs, outs, lses, lses, lses)


def _pool_bwd(z, dm, wbd, scale, name):
    s_len = z.shape[0]
    n_halo = s_len // HALO

    def body(u_ref, uh_ref, d_ref, dh_ref, wbd_ref, sc_ref, du_ref, dw_ref, dsc_ref):
        i = pl.program_id(0)
        last = pl.num_programs(0) - 1
        y = _pool_y(u_ref[...], uh_ref[...], i)
        yb = _mx(y)
        dpo = d_ref[...]
        sc = sc_ref[...]
        dsc = jnp.sum(dpo * _dot(yb, wbd_ref[...]), axis=0, keepdims=True)
        dwp = _dot_tn(yb, _mx(dpo * sc))

        @pl.when(i == 0)
        def _():
            dsc_ref[...] = dsc
            dw_ref[...] = dwp

        @pl.when(i > 0)
        def _():
            dsc_ref[...] += dsc
            dw_ref[...] += dwp

        ext = jnp.concatenate([dpo, jnp.where(i < last, dh_ref[...], 0.0)], axis=0)
        dy = _dot_nt(_mx(ext * sc), wbd_ref[...])
        t = i * TM + lax.broadcasted_iota(jnp.int32, ext.shape, 0)
        lane = lax.broadcasted_iota(jnp.int32, ext.shape, 1)
        e = dy / jnp.minimum(_pool_window(lane), t + 1).astype(f32)
        rows = ext.shape[0]
        f2 = e + pltpu.roll(e, rows - 1, axis=0)
        f4 = f2 + pltpu.roll(f2, rows - 2, axis=0)
        f8 = f4 + pltpu.roll(f4, rows - 4, axis=0)
        f16 = f8 + pltpu.roll(f8, rows - 8, axis=0)
        du_ref[...] = (_group_select(lane, f2, f4, f8, f16) - dy)[:TM, :]

    row = lambda i: (i, 0)
    blk = pl.BlockSpec((TM, 256), row)
    return pl.pallas_call(
        body, name=name, grid=(s_len // TM,),
        in_specs=[blk, pl.BlockSpec((HALO, 256), lambda i: (jnp.maximum(i * (TM // HALO) - 1, 0), 0)),
                  blk, pl.BlockSpec((HALO, 256), lambda i: (jnp.minimum((i + 1) * (TM // HALO), n_halo - 1), 0)),
                  pl.BlockSpec((256, 256), lambda i: (0, 0)), pl.BlockSpec((1, 256), lambda i: (0, 0))],
        out_specs=[blk, pl.BlockSpec((256, 256), lambda i: (0, 0)), pl.BlockSpec((1, 256), lambda i: (0, 0))],
        out_shape=[jax.ShapeDtypeStruct((s_len, N_IN), f32), jax.ShapeDtypeStruct((256, 256), f32),
                   jax.ShapeDtypeStruct((1, 256), f32)],
        compiler_params=_cparams(("arbitrary",)),
    )(z, z, dm, dm, wbd, scale)


def _to_strided(x, dil):
    if dil == 1:
        return x
    s_len, c = x.shape
    return x.reshape(s_len // (BLK * dil), BLK, dil, c).transpose(0, 2, 1, 3).reshape(s_len, c)


def _from_strided(x, dil):
    if dil == 1:
        return x
    s_len, c = x.shape
    return x.reshape(s_len // (BLK * dil), dil, BLK, c).transpose(0, 2, 1, 3).reshape(s_len, c)


def _tri_masks():
    qi = lax.broadcasted_iota(jnp.int32, (BLK, BLK), 0)
    ki = lax.broadcasted_iota(jnp.int32, (BLK, BLK), 1)
    return qi >= ki, ki >= qi


ATTN_SUPER_PER_STEP = (8, 2, 1)
Q_COL, K_COL, V_COL = POOL_WIDTH // 128, (POOL_WIDTH + ATTN_WIDTH) // 128, (POOL_WIDTH + 2 * ATTN_WIDTH) // 128


def _rows(ref, start, dil):
    if dil == 1:
        return ref[pl.ds(start, BLK), :]
    return ref[pl.ds(start, BLK, stride=dil), :]


def _set_rows(ref, start, dil, val):
    if dil == 1:
        ref[pl.ds(start, BLK), :] = val
    else:
        ref[pl.ds(start, BLK, stride=dil), :] = val


def _attn_fwd(z, g, prev, name):
    s_len = z.shape[0]
    dil, m = DILATIONS[g], ATTN_SUPER_PER_STEP[g]
    sbr = BLK * dil
    rows = sbr * m

    def body(*refs):
        q_ref, kc_ref, kp_ref, vc_ref, vp_ref = refs[:5]
        o_ref, l_ref = refs[-2:]
        st = pl.program_id(0)
        low, up = _tri_masks()
        head0 = lax.broadcasted_iota(jnp.int32, (BLK, 128), 1) < HEAD_DIM
        for sb in range(m):
            valid = jnp.concatenate([up & (st > 0) if sb == 0 else up, low], axis=1)
            for r in range(dil):
                base = sb * sbr + r
                q = _rows(q_ref, base, dil)
                kc, vc = _rows(kc_ref, base, dil), _rows(vc_ref, base, dil)
                if sb == 0:
                    kp, vp = _rows(kp_ref, r, dil), _rows(vp_ref, r, dil)
                else:
                    kp, vp = _rows(kc_ref, base - sbr, dil), _rows(vc_ref, base - sbr, dil)
                k2 = jnp.concatenate([_mx(kp), _mx(kc)], axis=0)
                v2 = jnp.concatenate([_mx(vp), _mx(vc)], axis=0)
                outs, lses = [], []
                for hh in range(2):
                    s = jnp.where(valid, _dot_nt(_mx(jnp.where(head0 == (hh == 0), q, 0.0)), k2) * ATTN_SCALE, NEG_BIG)
                    mx = jnp.max(s, axis=-1, keepdims=True)
                    e = jnp.exp(s - mx)
                    l = jnp.sum(e, axis=-1, keepdims=True)
                    outs.append(_dot(_mx(e / l), v2))
                    lses.append(jnp.broadcast_to(mx + jnp.log(l), (BLK, 128)))
                _set_rows(o_ref, base, dil, jnp.where(head0, outs[0], outs[1]))
                _set_rows(l_ref, base, dil, jnp.where(head0, lses[0], lses[1]))

    def cur(col):
        return pl.BlockSpec((rows, 128), lambda st, hp: (st, col + 2 * g + hp))

    def before(col):
        return pl.BlockSpec((sbr, 128), lambda st, hp: (jnp.maximum(st * m - 1, 0), col + 2 * g + hp))

    in_specs = [cur(Q_COL), cur(K_COL), before(K_COL), cur(V_COL), before(V_COL)]
    args = [z, z, z, z, z]
    aliases = {}
    if prev is not None:
        in_specs += [pl.BlockSpec(memory_space=pl.ANY)] * 2
        args += list(prev)
        aliases = {5: 0, 6: 1}
    return pl.pallas_call(
        body, name=name, grid=(s_len // rows, 2), in_specs=in_specs, out_specs=[cur(0), cur(0)],
        out_shape=[jax.ShapeDtypeStruct((s_len, ATTN_WIDTH), f32)] * 2, input_output_aliases=aliases,
        compiler_params=_cparams(("parallel", "parallel")),
    )(*args)


def _attn_bwd(z, do, lse, dlt, tabs, dz, g, name):
    s_len = z.shape[0]
    dil, m = DILATIONS[g], ATTN_SUPER_PER_STEP[g]
    sbr = BLK * dil
    rows = sbr * m
    nsteps = s_len // rows

    def body(q_ref, qn_ref, kc_ref, kp_ref, vc_ref, vp_ref, do_ref, don_ref, l_ref, ln_ref, d_ref, dn_ref,
             c_ref, s1_ref, s2_ref, dz_in, dz_ref, dq_buf, dk_buf, dv_buf, sems):
        del dz_in
        st, hp = pl.program_id(0), pl.program_id(1)
        low, up = _tri_masks()
        head0 = lax.broadcasted_iota(jnp.int32, (BLK, 128), 1) < HEAD_DIM
        for sb in range(m):
            up_prev = up & (st > 0) if sb == 0 else up
            up_next = up & (st < nsteps - 1) if sb == m - 1 else up
            for r in range(dil):
                base = sb * sbr + r
                q, k, v = _rows(q_ref, base, dil), _rows(kc_ref, base, dil), _rows(vc_ref, base, dil)
                do_c, l_c, d_c = _rows(do_ref, base, dil), _rows(l_ref, base, dil), _rows(d_ref, base, dil)
                if sb == 0:
                    kp, vp = _rows(kp_ref, r, dil), _rows(vp_ref, r, dil)
                else:
                    kp, vp = _rows(kc_ref, base - sbr, dil), _rows(vc_ref, base - sbr, dil)
                if sb == m - 1:
                    qn, do_n = _rows(qn_ref, r, dil), _rows(don_ref, r, dil)
                    l_n, d_n = _rows(ln_ref, r, dil), _rows(dn_ref, r, dil)
                else:
                    qn, do_n = _rows(q_ref, base + sbr, dil), _rows(do_ref, base + sbr, dil)
                    l_n, d_n = _rows(l_ref, base + sbr, dil), _rows(d_ref, base + sbr, dil)
                kc, kb, vc, vb = _mx(k), _mx(kp), _mx(v), _mx(vp)
                dqs, dks, dvs = [], [], []
                for hh in range(2):
                    mine = head0 == (hh == 0)
                    one = slice(hh * HEAD_DIM, hh * HEAD_DIM + 1)
                    qc, qx = _mx(jnp.where(mine, q, 0.0)), _mx(jnp.where(mine, qn, 0.0))
                    doc, dox = _mx(jnp.where(mine, do_c, 0.0)), _mx(jnp.where(mine, do_n, 0.0))
                    lc, lx, dc, dx = l_c[:, one], l_n[:, one], d_c[:, one], d_n[:, one]
                    p_a = jnp.where(low, jnp.exp(_dot_nt(qc, kc) * ATTN_SCALE - lc), 0.0)
                    ds_a = _mx(p_a * (_dot_nt(doc, vc) - dc) * ATTN_SCALE)
                    p_b = jnp.where(up_prev, jnp.exp(_dot_nt(qc, kb) * ATTN_SCALE - lc), 0.0)
                    ds_b = _mx(p_b * (_dot_nt(doc, vb) - dc) * ATTN_SCALE)
                    p_c = jnp.where(up_next, jnp.exp(_dot_nt(qx, kc) * ATTN_SCALE - lx), 0.0)
                    ds_c = _mx(p_c * (_dot_nt(dox, vc) - dx) * ATTN_SCALE)
                    dqs.append(_dot(ds_a, kc) + _dot(ds_b, kb))
                    dks.append(_dot_tn(ds_a, qc) + _dot_tn(ds_c, qx))
                    dvs.append(_dot_tn(_mx(p_a), doc) + _dot_tn(_mx(p_c), dox))
                c, s1, s2 = _rows(c_ref, base, dil), _rows(s1_ref, base, dil), _rows(s2_ref, base, dil)
                _set_rows(dq_buf, base, dil, _rope_transpose(jnp.where(head0, dqs[0], dqs[1]), c, s1, s2, 128))
                _set_rows(dk_buf, base, dil, _rope_transpose(dks[0] + dks[1], c, s1, s2, 128))
                _set_rows(dv_buf, base, dil, dvs[0] + dvs[1])
        copies = []
        for t, (buf, col) in enumerate(((dq_buf, Q_COL), (dk_buf, K_COL), (dv_buf, V_COL))):
            lane0 = pl.multiple_of((col + 2 * g + hp) * 128, 128)
            dst = dz_ref.at[pl.ds(pl.multiple_of(st * rows, rows), rows), pl.ds(lane0, 128)]
            cp = pltpu.make_async_copy(buf, dst, sems.at[t])
            cp.start()
            copies.append(cp)
        for cp in copies:
            cp.wait()

    def cur(col):
        return pl.BlockSpec((rows, 128), lambda st, hp: (st, col + 2 * g + hp))

    def before(col):
        return pl.BlockSpec((sbr, 128), lambda st, hp: (jnp.maximum(st * m - 1, 0), col + 2 * g + hp))

    def after(col):
        return pl.BlockSpec((sbr, 128), lambda st, hp: (jnp.minimum((st + 1) * m, s_len // sbr - 1), col + 2 * g + hp))

    tab = pl.BlockSpec((rows, 128), lambda st, hp: (st, 0))
    return pl.pallas_call(
        body, name=name, grid=(nsteps, 2),
        in_specs=[cur(Q_COL), after(Q_COL), cur(K_COL), before(K_COL), cur(V_COL), before(V_COL),
                  cur(0), after(0), cur(0), after(0), cur(0), after(0), tab, tab, tab,
                  pl.BlockSpec(memory_space=pl.ANY)],
        out_specs=pl.BlockSpec(memory_space=pl.ANY),
        out_shape=jax.ShapeDtypeStruct(dz.shape, dz.dtype), input_output_aliases={15: 0},
        scratch_shapes=[pltpu.VMEM((rows, 128), f32)] * 3 + [pltpu.SemaphoreType.DMA((3,))],
        compiler_params=_cparams(("arbitrary", "arbitrary")),
    )(z, z, z, z, z, z, do, do, lse, lse, dlt, dlt, *tabs, dz)


def _attn_fwd_old(q, k, v, dil, name):
    s_len = q.shape[0]
    nblk = s_len // BLK

    def body(q_ref, kc_ref, kp_ref, vc_ref, vp_ref, o_ref, l_ref):
        b = pl.program_id(0)
        has_prev = b >= dil
        low, up = _tri_masks()
        valid = jnp.concatenate([up & has_prev, low], axis=1)
        outs, lses = [], []
        for hh in range(2):
            sl = slice(hh * HEAD_DIM, (hh + 1) * HEAD_DIM)
            qh = _mx(q_ref[:, sl])
            k2 = jnp.concatenate([_mx(kp_ref[:, sl]), _mx(kc_ref[:, sl])], axis=0)
            v2 = jnp.concatenate([_mx(vp_ref[:, sl]), _mx(vc_ref[:, sl])], axis=0)
            s = jnp.where(valid, _dot_nt(qh, k2) * ATTN_SCALE, NEG_BIG)
            m = jnp.max(s, axis=-1, keepdims=True)
            e = jnp.exp(s - m)
            l = jnp.sum(e, axis=-1, keepdims=True)
            outs.append(_dot(_mx(e / l), v2))
            lses.append(jnp.broadcast_to(m + jnp.log(l), (BLK, HEAD_DIM)))
        o_ref[...] = jnp.concatenate(outs, axis=1)
        l_ref[...] = jnp.concatenate(lses, axis=1)

    cur = pl.BlockSpec((BLK, 128), lambda b, hp: (b, hp))
    prev = pl.BlockSpec((BLK, 128), lambda b, hp: (jnp.maximum(b - dil, 0), hp))
    return pl.pallas_call(
        body, name=name, grid=(nblk, 2), in_specs=[cur, cur, prev, cur, prev], out_specs=[cur, cur],
        out_shape=[jax.ShapeDtypeStruct((s_len, 256), f32)] * 2,
        compiler_params=_cparams(("parallel", "parallel")),
    )(q, k, k, v, v)


def _attn_bwd_old(q, k, v, do, lse, dlt, tabs, dil, name):
    s_len = q.shape[0]
    nblk = s_len // BLK

    def body(q_ref, qn_ref, kc_ref, kp_ref, vc_ref, vp_ref, do_ref, don_ref, l_ref, ln_ref, d_ref, dn_ref,
             c_ref, s1_ref, s2_ref, dq_ref, dk_ref, dv_ref):
        b = pl.program_id(0)
        has_prev = b >= dil
        has_next = b + dil < nblk
        low, up = _tri_masks()
        dqs, dks, dvs = [], [], []
        for hh in range(2):
            sl = slice(hh * HEAD_DIM, (hh + 1) * HEAD_DIM)
            one = slice(hh * HEAD_DIM, hh * HEAD_DIM + 1)
            qc, qn = _mx(q_ref[:, sl]), _mx(qn_ref[:, sl])
            kc, kp = _mx(kc_ref[:, sl]), _mx(kp_ref[:, sl])
            vc, vp = _mx(vc_ref[:, sl]), _mx(vp_ref[:, sl])
            doc, don = _mx(do_ref[:, sl]), _mx(don_ref[:, sl])
            lc, ln = l_ref[:, one], ln_ref[:, one]
            dc, dn = d_ref[:, one], dn_ref[:, one]
            p_a = jnp.where(low, jnp.exp(_dot_nt(qc, kc) * ATTN_SCALE - lc), 0.0)
            ds_a = _mx(p_a * (_dot_nt(doc, vc) - dc) * ATTN_SCALE)
            p_b = jnp.where(up & has_prev, jnp.exp(_dot_nt(qc, kp) * ATTN_SCALE - lc), 0.0)
            ds_b = _mx(p_b * (_dot_nt(doc, vp) - dc) * ATTN_SCALE)
            p_c = jnp.where(up & has_next, jnp.exp(_dot_nt(qn, kc) * ATTN_SCALE - ln), 0.0)
            ds_c = _mx(p_c * (_dot_nt(don, vc) - dn) * ATTN_SCALE)
            dqs.append(_dot(ds_a, kc) + _dot(ds_b, kp))
            dks.append(_dot_tn(ds_a, qc) + _dot_tn(ds_c, qn))
            dvs.append(_dot_tn(_mx(p_a), doc) + _dot_tn(_mx(p_c), don))
        c, s1, s2 = c_ref[...], s1_ref[...], s2_ref[...]
        dq_ref[...] = _rope_transpose(jnp.concatenate(dqs, axis=1), c, s1, s2, 128)
        dk_ref[...] = _rope_transpose(jnp.concatenate(dks, axis=1), c, s1, s2, 128)
        dv_ref[...] = jnp.concatenate(dvs, axis=1)

    cur = pl.BlockSpec((BLK, 128), lambda b, hp: (b, hp))
    prev = pl.BlockSpec((BLK, 128), lambda b, hp: (jnp.maximum(b - dil, 0), hp))
    nxt = pl.BlockSpec((BLK, 128), lambda b, hp: (jnp.minimum(b + dil, nblk - 1), hp))
    tab = pl.BlockSpec((BLK, 128), lambda b, hp: (b, 0))
    return pl.pallas_call(
        body, name=name, grid=(nblk, 2),
        in_specs=[cur, nxt, cur, prev, cur, prev, cur, nxt, cur, nxt, cur, nxt, tab, tab, tab],
        out_specs=[cur, cur, cur], out_shape=[jax.ShapeDtypeStruct((s_len, 256), f32)] * 3,
        compiler_params=_cparams(("parallel", "parallel")),
    )(q, q, k, k, v, v, do, do, lse, lse, dlt, dlt, *tabs)


def _loss_head(h, g, target, name):
    s_len, d = h.shape

    def body(h_ref, g_ref, t_ref, loss_ref, dh_ref, dg_ref):
        i = pl.program_id(0)
        x = h_ref[...]
        gv = g_ref[...]
        r = lax.rsqrt(jnp.mean(x * x, axis=-1, keepdims=True) + EPS)
        xh = x * r
        diff = xh * gv - t_ref[...]
        part = 0.5 * jnp.sum(jnp.mean(diff * diff, axis=-1, keepdims=True), axis=0, keepdims=True)
        dy = diff * (1.0 / d)
        dxh = dy * gv
        dh_ref[...] = r * (dxh - xh * jnp.mean(dxh * xh, axis=-1, keepdims=True))
        dgsum = jnp.sum(dy * xh, axis=0, keepdims=True)
        lossb = jnp.broadcast_to(part, (8, 128))

        @pl.when(i == 0)
        def _():
            loss_ref[...] = lossb
            dg_ref[...] = dgsum

        @pl.when(i > 0)
        def _():
            loss_ref[...] += lossb
            dg_ref[...] += dgsum

    row = lambda i: (i, 0)
    return pl.pallas_call(
        body, name=name, grid=(s_len // TM,),
        in_specs=[pl.BlockSpec((TM, d), row), pl.BlockSpec((1, d), lambda i: (0, 0)), pl.BlockSpec((TM, d), row)],
        out_specs=[pl.BlockSpec((8, 128), lambda i: (0, 0)), pl.BlockSpec((TM, d), row),
                   pl.BlockSpec((1, d), lambda i: (0, 0))],
        out_shape=[jax.ShapeDtypeStruct((8, 128), f32), jax.ShapeDtypeStruct((s_len, d), f32),
                   jax.ShapeDtypeStruct((1, d), f32)],
        compiler_params=_cparams(("arbitrary",)),
    )(h, g, target)


def _rope_tables(positions):
    inv_freq = ROPE_THETA ** (-jnp.arange(0, ROT_DIM, 2, dtype=f32) / ROT_DIM)
    ang = positions.astype(f32)[:, None] * inv_freq
    cos, sin = jnp.cos(ang), jnp.sin(ang)
    s_len = positions.shape[0]
    zero8, rest = jnp.zeros((s_len, 8), f32), jnp.zeros((s_len, HEAD_DIM - ROT_DIM), f32)
    c = jnp.concatenate([cos, cos, jnp.ones((s_len, HEAD_DIM - ROT_DIM), f32)], axis=1)
    s1 = jnp.concatenate([-sin, zero8, rest], axis=1)
    s2 = jnp.concatenate([zero8, sin, rest], axis=1)
    return c, s1, s2


def _block_diag(pool_w):
    out = jnp.zeros((POOL_WIDTH, POOL_WIDTH), pool_w.dtype)
    for g in range(4):
        out = lax.dynamic_update_slice(out, pool_w[g], (g * POOL_GC, g * POOL_GC))
    return out


class _ReadyWeights:
    def __init__(self, full):
        self.full = full

    def take(self, layer, names, after):
        del after
        return {n: self.full[n] for n in names}, layer


def _layer_fwd(h, p_l, wsrc, small, layer, tabs):
    nm = f"l{layer}_"
    wts, wl = wsrc.take(layer, ("w_in",), h if layer else None)
    z, hn1 = _norm_matmul(h, small["norm1"][layer][None], wts["w_in"], wl, 256, nm + "in_proj", rope=tabs)
    ol = None
    for g in range(3):
        ol = _attn_fwd(z, g, ol, nm + f"attn_fwd{g}")
    outs, lses = ol
    wbd = _mx(_block_diag(small["pool_w"][layer]))
    scale = small["pool_scale"][layer][None]
    m = _mixer_merge(z, wbd, scale, outs, lses, nm + "mixer_merge")
    rest, _ = wsrc.take(layer, ("w_out", "w_up", "w_down", "w_gate", "w_ple"), m)
    wts = {**wts, **rest}
    h1 = _matmul_residual(m, wts["w_out"], wl, h, nm + "out_proj")
    h2, a, hn2 = _mlp_fwd(h1, small["norm2"][layer][None], wts["w_up"], wts["w_down"], wl, nm + "mlp")
    h3, gl, hn3 = _gate_ple_fwd(h2, small["norm3"][layer][None], wts["w_gate"], wts["w_ple"], wl, p_l, nm + "gate_ple")
    saved = dict(h=h, z=z, hn1=hn1, outs=outs, lses=lses, wbd=wbd, scale=scale, m=m, h1=h1, a=a, hn2=hn2, h2=h2,
                 gl=gl, hn3=hn3, wts=wts, wl=wl)
    return h3, saved


def _layer_bwd(dh3, sv, p_l, small, layer, tabs128, reducer):
    nm = f"l{layer}_"
    wts, wl = sv["wts"], sv["wl"]
    de, dgl = _gate_ple_bwd(dh3, sv["gl"], p_l, wts["w_ple"], wl, nm + "gate_ple_bwd")
    reducer.add("w_gate", layer, _weight_grad(sv["hn3"], dgl, nm + "dw_gate"))
    reducer.add("w_ple", layer, _weight_grad(p_l, de, nm + "dw_ple"))
    dh2, dg3 = _matmul_nt_norm_bwd(dgl, wts["w_gate"], wl, sv["h2"], small["norm3"][layer][None], dh3, nm + "gate_bwd")
    da = _down_bwd(dh2, wts["w_down"], wl, sv["a"], nm + "down_bwd")
    reducer.add("w_down", layer, _weight_grad(sv["a"], dh2, nm + "dw_down", act=True))
    started = reducer.add("w_up", layer, _weight_grad(sv["hn2"], da, nm + "dw_up"))
    dh1, dg2 = _matmul_nt_norm_bwd(da, wts["w_up"], wl, sv["h1"], small["norm2"][layer][None], dh2, nm + "up_bwd",
                                   after=started)
    dm = _matmul_nt(dh1, wts["w_out"], wl, nm + "out_bwd")
    reducer.add("w_out", layer, _weight_grad(sv["m"], dh1, nm + "dw_out"))
    do, dlt = _combine_bwd(dm, sv["outs"], sv["lses"], nm + "combine_bwd")
    dz, dwbd, dscale = _pool_bwd(sv["z"], dm, sv["wbd"], sv["scale"], nm + "pool_bwd")
    for g in range(3):
        dz = _attn_bwd(sv["z"], do, sv["lses"], dlt, tabs128, dz, g, nm + f"attn_bwd{g}")
    started = reducer.add("w_in", layer, _weight_grad(sv["hn1"], dz, nm + "dw_in"))
    dh0, dg1 = _matmul_nt_norm_bwd(dz, wts["w_in"], wl, sv["h"], small["norm1"][layer][None], dh1, nm + "in_bwd",
                                   tk=512, after=started)
    dpool_w = jnp.stack([dwbd[g * POOL_GC:(g + 1) * POOL_GC, g * POOL_GC:(g + 1) * POOL_GC] for g in range(4)])
    sg = dict(norm1=dg1[0], norm2=dg2[0], norm3=dg3[0], pool_w=dpool_w, pool_scale=dscale[0])
    return dh0, sg


class _CollectGrads:
    def __init__(self):
        self.grads = {}

    def add(self, name, layer, dw):
        self.grads[(name, layer)] = dw


def _local_step(x, p, positions, wsrc, small, target, reducer):
    tabs128 = tuple(jnp.tile(t, (1, 2)) for t in _rope_tables(positions))
    h = x
    saved = []
    for layer in range(2):
        h, sv = _layer_fwd(h, p[layer], wsrc, small, layer, tabs128)
        saved.append(sv)
    loss, dh, dgf = _loss_head(h, small["final_norm"][None], target, "loss_head")
    sgs = [None, None]
    for layer in (1, 0):
        dh, sgs[layer] = _layer_bwd(dh, saved[layer], p[layer], small, layer, tabs128, reducer)
    small_grads = {k: jnp.stack([sgs[0][k], sgs[1][k]]) for k in sgs[0]}
    small_grads["final_norm"] = dgf[0]
    return loss, dh, small_grads


HBM = pl.BlockSpec(memory_space=pltpu.HBM)


def _my_place():
    return lax.axis_index("x"), lax.axis_index("y"), lax.axis_index("c")


def _other_chips(x, y):
    return [(1 - x, y), (x, 1 - y), (1 - x, 1 - y)]


def _window(ref, name, chip):
    k, n = _shard_shape(name)
    if COL_SHARDED[name]:
        return ref.at[:, pl.ds(pl.multiple_of(chip * n, 128), n)]
    return ref.at[pl.ds(pl.multiple_of(chip * k, 128), k), :]


def _chip_index():
    return jnp.reshape(2 * lax.axis_index("x") + lax.axis_index("y"), (1,)).astype(jnp.int32)


def _shard_block(name, tr):
    ks, ns = _shard_shape(name)
    if COL_SHARDED[name]:
        return (tr, ns), lambda i, me: (i, me[0])
    return (tr, ns), lambda i, me: (me[0] * (ks // tr) + i, 0)


def _place_shard(w, name, layer):
    ks, ns = _shard_shape(name)
    tr = min(ks, 256)
    shape, index = _shard_block(name, tr)

    def body(me_ref, w_ref, o_ref):
        o_ref[...] = w_ref[...].astype(o_ref.dtype)

    return pl.pallas_call(
        body, name=f"place_{name}{layer}",
        grid_spec=pltpu.PrefetchScalarGridSpec(
            num_scalar_prefetch=1, grid=(ks // tr,),
            in_specs=[pl.BlockSpec((None, tr, ns), lambda i, me: (layer, i, 0))],
            out_specs=pl.BlockSpec((None,) + shape, lambda i, me: (0,) + index(i, me))),
        out_shape=jax.ShapeDtypeStruct((1,) + FULL_SHAPE[name], MXU_DTYPE),
        compiler_params=_cparams(("parallel",)),
    )(_chip_index(), w)


GATHER_ORDER = [("w_in", 0), ("w_out", 0), ("w_up", 0), ("w_down", 0), ("w_gate", 0), ("w_ple", 0),
                ("w_in", 1), ("w_out", 1), ("w_up", 1), ("w_down", 1), ("w_gate", 1), ("w_ple", 1)]
SEM = pl.BlockSpec(memory_space=pltpu.SEMAPHORE)
EFFECT = pltpu.SideEffectType.DATAFLOW_SIDE_EFFECTING


def _gather_copy(src_ref, dst_ref, name, idx, j, chip, send_sems, recv_sems, c):
    cx, cy = chip
    return pltpu.make_async_remote_copy(
        src_ref=src_ref, dst_ref=dst_ref, send_sem=send_sems.at[3 * idx + j], recv_sem=recv_sems.at[3 * idx + j],
        device_id=(cx, cy, c), device_id_type=MESH)


def _gather_start(placed, order, tag, after=None):
    n = len(order)
    extra = [] if after is None else [after]

    def body(*refs):
        ins = refs[:n]
        k = n + len(extra)
        send_sems, recv_sems = refs[k], refs[k + 1]
        outs = refs[k + 2:k + 2 + n]
        token = refs[-1]
        x, y, c = _my_place()
        me = 2 * x + y
        for idx, (name, _) in enumerate(order):
            for j, chip in enumerate(_other_chips(x, y)):
                _gather_copy(_window(ins[idx].at[0], name, me), _window(outs[idx].at[0], name, me), name, idx, j, chip,
                             send_sems, recv_sems, c).start()
        token[...] = jnp.zeros_like(token)

    res = pl.pallas_call(
        body, name="gather_start" + tag,
        out_shape=(pltpu.SemaphoreType.DMA((3 * n,)), pltpu.SemaphoreType.DMA((3 * n,)))
        + tuple(pltpu.HBM(a.shape, a.dtype) for a in placed) + (jax.ShapeDtypeStruct((8, 128), f32),),
        in_specs=[HBM] * n + [pl.BlockSpec(memory_space=pl.ANY)] * len(extra),
        out_specs=(SEM, SEM) + (HBM,) * n + (pl.BlockSpec(memory_space=pltpu.VMEM),),
        input_output_aliases={i: i + 2 for i in range(n)},
        compiler_params=pltpu.CompilerParams(has_side_effects=EFFECT),
    )(*[pltpu.with_memory_space_constraint(a, pltpu.HBM) for a in placed], *extra)
    return res[0], res[1], list(res[2:2 + n]), res[-1]


def _gather_wait(send_sems, recv_sems, arrays, order, idxs, after, name):
    n = len(idxs)

    def body(*refs):
        ins = refs[:n]
        send_ref, recv_ref = refs[n], refs[n + 1]
        x, y, c = _my_place()
        me = 2 * x + y
        for k, idx in enumerate(idxs):
            wname = order[idx][0]
            for j, chip in enumerate(_other_chips(x, y)):
                cx, cy = chip
                mine = _window(ins[k].at[0], wname, me)
                land = _window(ins[k].at[0], wname, 2 * cx + cy)
                _gather_copy(mine, mine, wname, idx, j, chip, send_ref, recv_ref, c).wait_send()
                _gather_copy(land, land, wname, idx, j, chip, send_ref, recv_ref, c).wait_recv()

    operands = list(arrays) + [send_sems, recv_sems]
    in_specs = [HBM] * n + [SEM, SEM]
    if after is not None:
        operands.append(after)
        in_specs.append(pl.BlockSpec(memory_space=pl.ANY))
    res = pl.pallas_call(
        body, name=name, out_shape=tuple(pltpu.HBM(a.shape, a.dtype) for a in arrays),
        in_specs=in_specs, out_specs=(HBM,) * n, input_output_aliases={i: i for i in range(n)},
        compiler_params=pltpu.CompilerParams(has_side_effects=EFFECT),
    )(*operands)
    return list(res)


class _GatheredWeights:
    def __init__(self, shards):
        self.starts = []
        token = None
        for tag, order in (("_first", GATHER_ORDER[:1]), ("_rest", GATHER_ORDER[1:])):
            placed = [_place_shard(shards[name], name, layer) for name, layer in order]
            self.starts.append((order,) + _gather_start(placed, order, tag, token))
            token = self.starts[-1][-1]

    def take(self, layer, names, after):
        order, send, recv, arrays, _ = next(s for s in self.starts if (names[0], layer) in s[0])
        if after is None:
            after = self.starts[-1][-1]
        idxs = [order.index((n, layer)) for n in names]
        got = _gather_wait(send, recv, [arrays[i] for i in idxs], order, idxs, after, f"gather_wait{layer}_{names[0]}")
        return dict(zip(names, got)), 0


def _gather_weights(full):
    names = list(BIG)

    def body(*refs):
        ins = refs[:len(names)]
        outs = refs[len(names):2 * len(names)]
        send_ici, recv_ici, send_d2d, recv_d2d = refs[2 * len(names):]
        x, y, c = _my_place()
        me = 2 * x + y
        sibling = (x, y, 1 - c)
        chips = _other_chips(x, y)
        ici = []
        for t, name in enumerate(names):
            for j, (cx, cy) in enumerate(chips):
                cp = pltpu.make_async_remote_copy(
                    src_ref=_window(ins[t].at[c], name, me), dst_ref=_window(outs[t].at[c], name, me),
                    send_sem=send_ici.at[3 * t + j], recv_sem=recv_ici.at[3 * t + j],
                    device_id=(cx, cy, c), device_id_type=MESH)
                cp.start()
                ici.append(cp)
        fwd = []
        for t, name in enumerate(names):
            for j, (cx, cy) in enumerate(chips):
                land = _window(outs[t].at[c], name, 2 * cx + cy)
                pltpu.make_async_remote_copy(
                    src_ref=land, dst_ref=land, send_sem=send_ici.at[3 * t + j], recv_sem=recv_ici.at[3 * t + j],
                    device_id=(cx, cy, c), device_id_type=MESH).wait_recv()
                cp = pltpu.make_async_remote_copy(
                    src_ref=land, dst_ref=land, send_sem=send_d2d.at[3 * t + j], recv_sem=recv_d2d.at[3 * t + j],
                    device_id=sibling, device_id_type=MESH)
                cp.start()
                fwd.append(cp)
        for t, name in enumerate(names):
            for j, (cx, cy) in enumerate(chips):
                land = _window(outs[t].at[1 - c], name, 2 * cx + cy)
                pltpu.make_async_remote_copy(
                    src_ref=land, dst_ref=land, send_sem=send_d2d.at[3 * t + j], recv_sem=recv_d2d.at[3 * t + j],
                    device_id=sibling, device_id_type=MESH).wait_recv()
        for cp in ici + fwd:
            cp.wait_send()

    nsem = 3 * len(names)
    outs = pl.pallas_call(
        body, name="gather_weights",
        in_specs=[HBM] * len(names), out_specs=[HBM] * len(names),
        out_shape=[jax.ShapeDtypeStruct(full[n].shape, full[n].dtype) for n in names],
        input_output_aliases={t: t for t in range(len(names))},
        scratch_shapes=[pltpu.SemaphoreType.DMA((nsem,)), pltpu.SemaphoreType.DMA((nsem,)),
                        pltpu.SemaphoreType.DMA((nsem,)), pltpu.SemaphoreType.DMA((nsem,))],
    )(*[full[n] for n in names])
    return dict(zip(names, outs))


def _swap_layers(grads):
    names = list(BIG)

    def body(*refs):
        ins = refs[:len(names)]
        outs = refs[len(names):2 * len(names)]
        send_sems, recv_sems = refs[2 * len(names):]
        x, y, c = _my_place()
        sibling = (x, y, 1 - c)
        cps = []
        for t in range(len(names)):
            cp = pltpu.make_async_remote_copy(
                src_ref=ins[t].at[1 - c], dst_ref=outs[t], send_sem=send_sems.at[t], recv_sem=recv_sems.at[t],
                device_id=sibling, device_id_type=MESH)
            cp.start()
            cps.append(cp)
        for cp in cps:
            cp.wait()

    outs = pl.pallas_call(
        body, name="swap_layers", in_specs=[HBM] * len(names), out_specs=[HBM] * len(names),
        out_shape=[jax.ShapeDtypeStruct(FULL_SHAPE[n], f32) for n in names],
        scratch_shapes=[pltpu.SemaphoreType.DMA((len(names),)), pltpu.SemaphoreType.DMA((len(names),))],
    )(*[grads[n] for n in names])
    return dict(zip(names, outs))


def _chip_sum(grad, other, name):
    k, n = FULL_SHAPE[name]
    tr = min(k, 512)
    c = lax.axis_index("c")

    def body(c_ref, g_ref, o_ref, out_ref):
        out_ref[...] = (g_ref[...] + o_ref[...]).astype(out_ref.dtype)

    return pl.pallas_call(
        body, name="chip_sum_" + name,
        grid_spec=pltpu.PrefetchScalarGridSpec(
            num_scalar_prefetch=1, grid=(k // tr,),
            in_specs=[pl.BlockSpec((None, tr, n), lambda i, c_ref: (c_ref[0], i, 0)),
                      pl.BlockSpec((tr, n), lambda i, c_ref: (i, 0))],
            out_specs=pl.BlockSpec((tr, n), lambda i, c_ref: (i, 0))),
        out_shape=jax.ShapeDtypeStruct((k, n), COMM_DTYPE),
        compiler_params=_cparams(("parallel",)),
    )(jnp.reshape(c, (1,)).astype(jnp.int32), grad, other)


def _scatter_shards(sums):
    names = list(BIG)

    def body(*refs):
        ins = refs[:len(names)]
        outs = refs[len(names):2 * len(names)]
        send_sems, recv_sems = refs[2 * len(names):]
        x, y, c = _my_place()
        me = 2 * x + y
        chips = _other_chips(x, y)
        cps = []
        for t, name in enumerate(names):
            for j, (cx, cy) in enumerate(chips):
                cp = pltpu.make_async_remote_copy(
                    src_ref=_window(ins[t], name, 2 * cx + cy), dst_ref=outs[t].at[me],
                    send_sem=send_sems.at[3 * t + j], recv_sem=recv_sems.at[3 * t + j],
                    device_id=(cx, cy, c), device_id_type=MESH)
                cp.start()
                cps.append(cp)
        for t, name in enumerate(names):
            for j, (cx, cy) in enumerate(chips):
                land = outs[t].at[2 * cx + cy]
                pltpu.make_async_remote_copy(
                    src_ref=land, dst_ref=land, send_sem=send_sems.at[3 * t + j], recv_sem=recv_sems.at[3 * t + j],
                    device_id=(cx, cy, c), device_id_type=MESH).wait_recv()
        for cp in cps:
            cp.wait_send()

    nsem = 3 * len(names)
    outs = pl.pallas_call(
        body, name="scatter_shards", in_specs=[HBM] * len(names), out_specs=[HBM] * len(names),
        out_shape=[jax.ShapeDtypeStruct((N_CHIPS,) + _shard_shape(n), sums[n].dtype) for n in names],
        scratch_shapes=[pltpu.SemaphoreType.DMA((nsem,)), pltpu.SemaphoreType.DMA((nsem,))],
    )(*[sums[n] for n in names])
    return dict(zip(names, outs))


def _sum_slots(slots, own, name):
    ks, ns = _shard_shape(name)
    tr = min(ks, 256)
    shape, index = _shard_block(name, tr)

    def body(me_ref, c_ref, s_ref, own_ref, out_ref):
        me = me_ref[0]
        acc = None
        for s in range(N_CHIPS):
            term = jnp.where(me == s, own_ref[...], s_ref[s]).astype(f32)
            acc = term if acc is None else acc + term
        out_ref[...] = acc

    return pl.pallas_call(
        body, name="sum_slots_" + name,
        grid_spec=pltpu.PrefetchScalarGridSpec(
            num_scalar_prefetch=2, grid=(ks // tr,),
            in_specs=[pl.BlockSpec((N_CHIPS, tr, ns), lambda i, me, c: (0, i, 0)),
                      pl.BlockSpec(shape, lambda i, me, c: index(i, me))],
            out_specs=pl.BlockSpec((None, tr, ns), lambda i, me, c: (c[0], i, 0))),
        out_shape=jax.ShapeDtypeStruct((2, ks, ns), f32),
        compiler_params=_cparams(("parallel",)),
    )(_chip_index(), jnp.reshape(lax.axis_index("c"), (1,)).astype(jnp.int32), slots, own)


N_DEV = 8


def _reduce_copies(dws, lands, names, layer, send_sems, recv_sems):
    x, y, c = _my_place()
    me, my_dev = 2 * x + y, 4 * x + 2 * y + c
    out = []
    for t, name in enumerate(names):
        for j, (cx, cy) in enumerate(_other_chips(x, y)):
            out.append((pltpu.make_async_remote_copy(
                src_ref=_window(dws[t], name, 2 * cx + cy), dst_ref=lands[t].at[my_dev],
                send_sem=send_sems.at[4 * t + j], recv_sem=recv_sems.at[N_DEV * t + my_dev],
                device_id=(cx, cy, layer), device_id_type=MESH), False))
        out.append((pltpu.make_async_remote_copy(
            src_ref=_window(dws[t], name, me), dst_ref=lands[t].at[my_dev],
            send_sem=send_sems.at[4 * t + 3], recv_sem=recv_sems.at[N_DEV * t + my_dev],
            device_id=(x, y, layer), device_id_type=MESH), True))
    return out


def _reduce_start(dws, names, layer, tag):
    n = len(names)
    lands = [lax.empty((N_DEV,) + _shard_shape(nm), dws[0].dtype) for nm in names]

    def body(*refs):
        ins = refs[:n]
        send_sems, recv_sems = refs[2 * n], refs[2 * n + 1]
        land_out = refs[3 * n + 2:4 * n + 2]
        token = refs[-1]
        c = lax.axis_index("c")
        for cp, non_owner_only in _reduce_copies(ins, land_out, names, layer, send_sems, recv_sems):
            if non_owner_only:
                @pl.when(c != layer)
                def _():
                    cp.start()
            else:
                cp.start()
        token[...] = jnp.zeros_like(token)

    res = pl.pallas_call(
        body, name="reduce_start" + tag,
        out_shape=(pltpu.SemaphoreType.DMA((4 * n,)), pltpu.SemaphoreType.DMA((N_DEV * n,)))
        + tuple(pltpu.HBM(a.shape, a.dtype) for a in dws) + tuple(pltpu.HBM(a.shape, a.dtype) for a in lands)
        + (jax.ShapeDtypeStruct((8, 128), f32),),
        in_specs=[HBM] * (2 * n),
        out_specs=(SEM, SEM) + (HBM,) * (2 * n) + (pl.BlockSpec(memory_space=pltpu.VMEM),),
        input_output_aliases={i: i + 2 for i in range(2 * n)},
        compiler_params=pltpu.CompilerParams(has_side_effects=EFFECT),
    )(*[pltpu.with_memory_space_constraint(a, pltpu.HBM) for a in list(dws) + lands])
    return res[0], res[1], list(res[2:2 + n]), list(res[2 + n:2 + 2 * n]), res[-1]


def _reduce_wait(send_sems, recv_sems, dws, lands, names, layer, after, tag):
    n = len(names)

    def body(*refs):
        ins, land_in = refs[:n], refs[n:2 * n]
        send_ref, recv_ref = refs[2 * n], refs[2 * n + 1]
        x, y, c = _my_place()
        for cp, non_owner_only in _reduce_copies(ins, land_in, names, layer, send_ref, recv_ref):
            if non_owner_only:
                @pl.when(c != layer)
                def _():
                    cp.wait_send()
            else:
                cp.wait_send()

        @pl.when(c == layer)
        def _():
            for t in range(n):
                for k in range(1, N_DEV):
                    px, py, pc = x ^ ((k >> 2) & 1), y ^ ((k >> 1) & 1), c ^ (k & 1)
                    dev = 4 * px + 2 * py + pc
                    land = land_in[t].at[dev]
                    pltpu.make_async_remote_copy(
                        src_ref=land, dst_ref=land, send_sem=send_ref.at[4 * t], recv_sem=recv_ref.at[N_DEV * t + dev],
                        device_id=(px, py, pc), device_id_type=MESH).wait_recv()

    res = pl.pallas_call(
        body, name="reduce_wait" + tag,
        out_shape=tuple(pltpu.HBM(a.shape, a.dtype) for a in list(dws) + list(lands)),
        in_specs=[HBM] * (2 * n) + [SEM, SEM, pl.BlockSpec(memory_space=pl.ANY)], out_specs=(HBM,) * (2 * n),
        input_output_aliases={i: i for i in range(2 * n)},
        compiler_params=pltpu.CompilerParams(has_side_effects=EFFECT),
    )(*dws, *lands, send_sems, recv_sems, after)
    return list(res[:n]), list(res[n:])


def _sum_devices(land, own, name, layer, prev):
    ks, ns = _shard_shape(name)
    tr = min(ks, 256)
    shape, index = _shard_block(name, tr)

    def body(me_ref, dev_ref, *refs):
        s_ref, own_ref, out_ref = refs[0], refs[1], refs[-1]
        dev = dev_ref[0]
        acc = None
        for s in range(N_DEV):
            term = jnp.where(dev == s, own_ref[...], s_ref[s]).astype(f32)
            acc = term if acc is None else acc + term
        out_ref[...] = acc

    in_specs = [pl.BlockSpec((N_DEV, tr, ns), lambda i, me, dev: (0, i, 0)),
                pl.BlockSpec(shape, lambda i, me, dev: index(i, me))]
    args = [land, own]
    aliases = {}
    if prev is not None:
        in_specs.append(pl.BlockSpec(memory_space=pl.ANY))
        args.append(prev)
        aliases = {4: 0}
    x, y, c = _my_place()
    return pl.pallas_call(
        body, name=f"sum_devices_{name}{layer}",
        grid_spec=pltpu.PrefetchScalarGridSpec(
            num_scalar_prefetch=2, grid=(ks // tr,), in_specs=in_specs,
            out_specs=pl.BlockSpec((None, tr, ns), lambda i, me, dev: (layer, i, 0))),
        out_shape=jax.ShapeDtypeStruct((2, ks, ns), f32), input_output_aliases=aliases,
        compiler_params=_cparams(("parallel",)),
    )(_chip_index(), jnp.reshape(4 * x + 2 * y + c, (1,)).astype(jnp.int32), *args)


class _GradReducer:
    GROUPS = (("1", 1, ("w_gate", "w_ple", "w_down", "w_up", "w_out", "w_in")),
              ("0a", 0, ("w_gate", "w_ple", "w_down", "w_up")),
              ("0b", 0, ("w_out", "w_in")))

    def __init__(self):
        self.grads = {}
        self.started = {}

    def add(self, name, layer, dw):
        self.grads[(name, layer)] = dw
        token = None
        for tag, glayer, names in self.GROUPS:
            if tag not in self.started and all((nm, glayer) in self.grads for nm in names):
                *self.started[tag], token = _reduce_start([self.grads[(nm, glayer)] for nm in names], names, glayer, tag)
        return token

    def finish(self, after):
        mine = {}
        for tag, layer, names in self.GROUPS:
            send, recv, dws, lands = self.started[tag]
            dws, lands = _reduce_wait(send, recv, dws, lands, names, layer, after, tag)
            for nm, dw, land in zip(names, dws, lands):
                mine[nm] = _sum_devices(land, dw, nm, layer, mine.get(nm))
        return _pair_layers(mine)


def _pair_layers(mine):
    names = list(BIG)

    def body(*refs):
        ins = refs[:len(names)]
        outs = refs[len(names):2 * len(names)]
        send_sems, recv_sems = refs[2 * len(names):]
        x, y, c = _my_place()
        sibling = (x, y, 1 - c)
        cps = []
        for t in range(len(names)):
            cp = pltpu.make_async_remote_copy(
                src_ref=ins[t].at[c], dst_ref=outs[t].at[c], send_sem=send_sems.at[t], recv_sem=recv_sems.at[t],
                device_id=sibling, device_id_type=MESH)
            cp.start()
            cps.append(cp)
        for t in range(len(names)):
            cps[t].wait_send()
            land = outs[t].at[1 - c]
            pltpu.make_async_remote_copy(
                src_ref=land, dst_ref=land, send_sem=send_sems.at[t], recv_sem=recv_sems.at[t],
                device_id=sibling, device_id_type=MESH).wait_recv()

    outs = pl.pallas_call(
        body, name="pair_layers", in_specs=[HBM] * len(names), out_specs=[HBM] * len(names),
        out_shape=[jax.ShapeDtypeStruct((2,) + _shard_shape(n), f32) for n in names],
        input_output_aliases={t: t for t in range(len(names))},
        scratch_shapes=[pltpu.SemaphoreType.DMA((len(names),)), pltpu.SemaphoreType.DMA((len(names),))],
    )(*[mine[n] for n in names])
    return dict(zip(names, outs))


SMALL_ROWS = 320


def _allreduce_small(vec):
    n_dev = 8

    def body(v_ref, out_ref, buf_ref, send_sems, recv_sems):
        x, y, c = _my_place()
        me = 4 * x + 2 * y + c
        buf_ref[me] = v_ref[...]
        cps = []
        for k in range(1, n_dev):
            dx, dy, dc = (k >> 2) & 1, (k >> 1) & 1, k & 1
            peer = (x ^ dx, y ^ dy, c ^ dc)
            cp = pltpu.make_async_remote_copy(
                src_ref=v_ref, dst_ref=buf_ref.at[me], send_sem=send_sems.at[k - 1], recv_sem=recv_sems.at[k - 1],
                device_id=peer, device_id_type=MESH)
            cp.start()
            cps.append(cp)
        for k in range(1, n_dev):
            dx, dy, dc = (k >> 2) & 1, (k >> 1) & 1, k & 1
            src = 4 * (x ^ dx) + 2 * (y ^ dy) + (c ^ dc)
            land = buf_ref.at[src]
            pltpu.make_async_remote_copy(
                src_ref=land, dst_ref=land, send_sem=send_sems.at[k - 1], recv_sem=recv_sems.at[k - 1],
                device_id=(x ^ dx, y ^ dy, c ^ dc), device_id_type=MESH).wait_recv()
        for cp in cps:
            cp.wait_send()
        acc = buf_ref[0]
        for s in range(1, n_dev):
            acc = acc + buf_ref[s]
        out_ref[...] = acc

    return pl.pallas_call(
        body, name="allreduce_small",
        in_specs=[pl.BlockSpec(memory_space=pltpu.VMEM)], out_specs=pl.BlockSpec(memory_space=pltpu.VMEM),
        out_shape=jax.ShapeDtypeStruct((SMALL_ROWS, 128), f32),
        scratch_shapes=[pltpu.VMEM((n_dev, SMALL_ROWS, 128), f32), pltpu.SemaphoreType.DMA((n_dev - 1,)),
                        pltpu.SemaphoreType.DMA((n_dev - 1,))],
    )(vec)


def _adamw(w, g, m, v, name):
    rows, cols = w.shape
    tr = rows
    for cand in (512, 256, 128, 64, 32, 16, 8):
        if rows % cand == 0 and cand * cols * 4 <= 2 * 1024 * 1024:
            tr = cand
            break
    c1 = np.float32(1.0 - ADAM_B1 ** ADAM_STEP)
    c2 = np.float32(1.0 - ADAM_B2 ** ADAM_STEP)

    def body(w_ref, g_ref, m_ref, v_ref, go_ref, d_ref, mo_ref, vo_ref):
        gv = g_ref[...]
        go_ref[...] = gv
        mn = ADAM_B1 * m_ref[...] + (1.0 - ADAM_B1) * gv
        vn = ADAM_B2 * v_ref[...] + (1.0 - ADAM_B2) * (gv * gv)
        mo_ref[...] = mn
        vo_ref[...] = vn
        d_ref[...] = -ADAM_LR * ((mn / c1) / (jnp.sqrt(vn / c2) + ADAM_EPS) + ADAM_WD * w_ref[...])

    blk = pl.BlockSpec((tr, cols), lambda i: (i, 0))
    return pl.pallas_call(
        body, name="adamw_" + name, grid=(rows // tr,), in_specs=[blk] * 4, out_specs=[blk] * 4,
        out_shape=[jax.ShapeDtypeStruct((rows, cols), f32)] * 4,
        compiler_params=_cparams(("parallel",)),
    )(w, g, m, v)


SMALL = ("norm1", "pool_w", "pool_scale", "norm2", "norm3", "final_norm")
ORDER = ("norm1", "w_in", "pool_w", "pool_scale", "w_out", "norm2", "w_up", "w_down", "norm3", "w_gate", "w_ple",
         "final_norm")


def _pack_small(tree, extra=None):
    parts = [tree[n].reshape(-1) for n in SMALL]
    if extra is not None:
        parts.append(extra.reshape(-1))
    flat = jnp.concatenate(parts)
    return jnp.pad(flat, (0, SMALL_ROWS * 128 - flat.shape[0])).reshape(SMALL_ROWS, 128)


def _unpack_small(packed, like):
    flat = packed.reshape(-1)
    out, off = {}, 0
    for n in SMALL:
        size = int(np.prod(like[n].shape))
        out[n] = flat[off:off + size].reshape(like[n].shape)
        off += size
    return out, flat[off]


def kernel(x, p, positions, norm1, w_in, pool_w, pool_scale, w_out, norm2, w_up, w_down, norm3, w_gate, w_ple, final_norm, loss_target, m_norm1, m_w_in, m_pool_w, m_pool_scale, m_w_out, m_norm2, m_w_up, m_w_down, m_norm3, m_w_gate, m_w_ple, m_final_norm, v_norm1, v_w_in, v_pool_w, v_pool_scale, v_w_out, v_norm2, v_w_up, v_w_down, v_norm3, v_w_gate, v_w_ple, v_final_norm):
    w = dict(norm1=norm1, w_in=w_in, pool_w=pool_w, pool_scale=pool_scale, w_out=w_out, norm2=norm2, w_up=w_up,
             w_down=w_down, norm3=norm3, w_gate=w_gate, w_ple=w_ple, final_norm=final_norm)
    m = dict(norm1=m_norm1, w_in=m_w_in, pool_w=m_pool_w, pool_scale=m_pool_scale, w_out=m_w_out, norm2=m_norm2,
             w_up=m_w_up, w_down=m_w_down, norm3=m_norm3, w_gate=m_w_gate, w_ple=m_w_ple, final_norm=m_final_norm)
    v = dict(norm1=v_norm1, w_in=v_w_in, pool_w=v_pool_w, pool_scale=v_pool_scale, w_out=v_w_out, norm2=v_norm2,
             w_up=v_w_up, w_down=v_w_down, norm3=v_norm3, w_gate=v_w_gate, w_ple=v_w_ple, final_norm=v_final_norm)
    small = {n: w[n] for n in SMALL}

    wsrc = _GatheredWeights({n: w[n] for n in BIG})
    reducer = _GradReducer()
    loss8, dx, small_grads = _local_step(x[0], p[:, 0], positions[0], wsrc, small, loss_target[0], reducer)
    gsh = reducer.finish(dx)

    red = _allreduce_small(_pack_small(small_grads, loss8[0, 0]))
    g_small, loss = _unpack_small(red, small)

    g_out, d_out, m_out, v_out = {}, {}, {}, {}
    for n in BIG:
        shp = w[n].shape
        two = lambda a: a.reshape(shp[0] * shp[1], shp[2])
        g2, d2, m2, v2 = _adamw(two(w[n]), two(gsh[n]), two(m[n]), two(v[n]), n)
        g_out[n], d_out[n], m_out[n], v_out[n] = g2.reshape(shp), d2.reshape(shp), m2.reshape(shp), v2.reshape(shp)
    _, d2, m2, v2 = _adamw(_pack_small(small), red, _pack_small({n: m[n] for n in SMALL}),
                           _pack_small({n: v[n] for n in SMALL}), "small")
    for tree, packed in ((d_out, d2), (m_out, m2), (v_out, v2)):
        tree.update(_unpack_small(packed, small)[0])
    g_out.update(g_small)

    return (loss, dx[None], *[g_out[n] for n in ORDER], *[d_out[n] for n in ORDER], *[m_out[n] for n in ORDER],
            *[v_out[n] for n in ORDER])
```

```python
import functools

import jax
import jax.numpy as jnp
import numpy as np
from jax import lax
from jax.experimental import pallas as pl
from jax.experimental.pallas import tpu as pltpu

f32 = jnp.float32
MXU_DTYPE = jnp.bfloat16
COMM_DTYPE = jnp.bfloat16

D_MODEL = 1024
POOL_WIDTH = 256
POOL_GC = 64
ATTN_WIDTH = 768
HEAD_DIM = 64
N_IN = POOL_WIDTH + 3 * ATTN_WIDTH
D_FF = 4096
PLE_DIM = 256
BLK = 128
DILATIONS = (1, 4, 16)
ROT_DIM = 16
ROPE_THETA = 500000.0
EPS = 1e-6
ATTN_SCALE = HEAD_DIM ** -0.5
NEG_BIG = -1e30

ADAM_LR, ADAM_B1, ADAM_B2, ADAM_EPS, ADAM_WD, ADAM_STEP = 0.001, 0.9, 0.999, 1e-08, 0.01, 10

TM = 512
TM_WGRAD = 1024
HALO = 16
VMEM_LIMIT = 48 * 1024 * 1024
N_CHIPS = 4
MESH = pl.DeviceIdType.MESH

BIG = ("w_in", "w_out", "w_up", "w_down", "w_gate", "w_ple")
FULL_SHAPE = {"w_in": (D_MODEL, N_IN), "w_out": (D_MODEL, D_MODEL), "w_up": (D_MODEL, D_FF),
              "w_down": (D_FF, D_MODEL), "w_gate": (D_MODEL, D_MODEL), "w_ple": (PLE_DIM, D_MODEL)}
COL_SHARDED = {"w_in": True, "w_out": False, "w_up": True, "w_down": False, "w_gate": False, "w_ple": True}


def _shard_shape(name):
    k, n = FULL_SHAPE[name]
    return (k, n // N_CHIPS) if COL_SHARDED[name] else (k // N_CHIPS, n)


def _cparams(sem=None, vmem=VMEM_LIMIT):
    return pltpu.CompilerParams(dimension_semantics=sem, vmem_limit_bytes=vmem)


def _resident(block_shape, index_map):
    return pl.BlockSpec(block_shape, index_map, pipeline_mode=pl.Buffered(1))


def _mx(x):
    return x.astype(MXU_DTYPE)


def _dot(a, b):
    return jnp.dot(a, b, preferred_element_type=f32)


def _dot_nt(a, b):
    return lax.dot_general(a, b, (((1,), (1,)), ((), ())), preferred_element_type=f32)


def _dot_tn(a, b):
    return lax.dot_general(a, b, (((0,), (0,)), ((), ())), preferred_element_type=f32)


def _sigmoid(x):
    return 1.0 / (1.0 + jnp.exp(-x))


def _rope_apply(y, c, s1, s2, width):
    return y * c + pltpu.roll(y, width - 8, axis=1) * s1 + pltpu.roll(y, 8, axis=1) * s2


def _rope_transpose(dy, c, s1, s2, width):
    return dy * c + pltpu.roll(dy * s1, 8, axis=1) + pltpu.roll(dy * s2, width - 8, axis=1)


def _norm_matmul(h, g, w, layer, tn, name, rope=None):
    s_len, d = h.shape
    n = w.shape[2]

    def body(*refs):
        if rope is None:
            h_ref, g_ref, w_ref, y_ref, hn_ref = refs
        else:
            h_ref, g_ref, w_ref, c_ref, s1_ref, s2_ref, y_ref, hn_ref = refs
            reps = tn // 128
            c = jnp.concatenate([c_ref[...]] * reps, axis=1)
            s1 = jnp.concatenate([s1_ref[...]] * reps, axis=1)
            s2 = jnp.concatenate([s2_ref[...]] * reps, axis=1)
        x = h_ref[...]
        r = lax.rsqrt(jnp.mean(x * x, axis=-1, keepdims=True) + EPS)
        hn = ((x * r) * g_ref[...]).astype(hn_ref.dtype)
        hn_ref[...] = hn
        for j in range(n // tn):
            y = _dot(hn, w_ref[:, j * tn:(j + 1) * tn])
            if rope is not None and POOL_WIDTH <= j * tn < POOL_WIDTH + 2 * ATTN_WIDTH:
                y = _rope_apply(y, c, s1, s2, tn)
            y_ref[:, j * tn:(j + 1) * tn] = y

    in_specs = [pl.BlockSpec((TM, d), lambda i: (i, 0)),
                pl.BlockSpec((1, d), lambda i: (0, 0)),
                _resident((None, d, n), lambda i: (layer, 0, 0))]
    args = [h, g, w]
    if rope is not None:
        assert POOL_WIDTH % tn == 0 and (2 * ATTN_WIDTH) % tn == 0
        in_specs += [pl.BlockSpec((TM, 128), lambda i: (i, 0))] * 3
        args += list(rope)
    return pl.pallas_call(
        body, name=name, grid=(s_len // TM,), in_specs=in_specs,
        out_specs=[pl.BlockSpec((TM, n), lambda i: (i, 0)), pl.BlockSpec((TM, d), lambda i: (i, 0))],
        out_shape=[jax.ShapeDtypeStruct((s_len, n), f32), jax.ShapeDtypeStruct((s_len, d), MXU_DTYPE)],
        compiler_params=_cparams(("parallel",)),
    )(*args)


def _matmul_residual(a, w, layer, res, name, act=False, tk=1024):
    s_len, k_dim = a.shape
    n = w.shape[2]

    def body(a_ref, w_ref, res_ref, o_ref):
        acc = res_ref[...]
        for k in range(k_dim // tk):
            x = a_ref[:, k * tk:(k + 1) * tk]
            if act:
                r = jnp.maximum(x, 0.0)
                x = r * r
            acc = acc + _dot(_mx(x), w_ref[k * tk:(k + 1) * tk, :])
        o_ref[...] = acc

    return pl.pallas_call(
        body, name=name, grid=(s_len // TM,),
        in_specs=[pl.BlockSpec((TM, k_dim), lambda i: (i, 0)),
                  _resident((None, k_dim, n), lambda i: (layer, 0, 0)),
                  pl.BlockSpec((TM, n), lambda i: (i, 0))],
        out_specs=pl.BlockSpec((TM, n), lambda i: (i, 0)),
        out_shape=jax.ShapeDtypeStruct((s_len, n), f32),
        compiler_params=_cparams(("parallel",)),
    )(a, w, res)


def _gate_ple_fwd(h2, g, w_gate, w_ple, layer, p, name):
    s_len, d = h2.shape

    def body(h_ref, g_ref, wg_ref, p_ref, wp_ref, h3_ref, gl_ref, hn_ref):
        x = h_ref[...]
        r = lax.rsqrt(jnp.mean(x * x, axis=-1, keepdims=True) + EPS)
        hn = ((x * r) * g_ref[...]).astype(hn_ref.dtype)
        hn_ref[...] = hn
        gl = _dot(hn, wg_ref[...])
        gl_ref[...] = gl
        e = _dot(_mx(p_ref[...]), wp_ref[...])
        h3_ref[...] = x + _sigmoid(gl) * e

    row = lambda i: (i, 0)
    return pl.pallas_call(
        body, name=name, grid=(s_len // TM,),
        in_specs=[pl.BlockSpec((TM, d), row), pl.BlockSpec((1, d), lambda i: (0, 0)),
                  pl.BlockSpec((None, d, d), lambda i: (layer, 0, 0)), pl.BlockSpec((TM, PLE_DIM), row),
                  pl.BlockSpec((None, PLE_DIM, d), lambda i: (layer, 0, 0))],
        out_specs=[pl.BlockSpec((TM, d), row)] * 3,
        out_shape=[jax.ShapeDtypeStruct((s_len, d), f32), jax.ShapeDtypeStruct((s_len, d), f32),
                   jax.ShapeDtypeStruct((s_len, d), MXU_DTYPE)],
        compiler_params=_cparams(("parallel",)),
    )(h2, g, w_gate, p, w_ple)


def _gate_ple_bwd(dh3, gl, p, w_ple, layer, name):
    s_len, d = dh3.shape

    def body(dh_ref, gl_ref, p_ref, wp_ref, de_ref, dgl_ref):
        dh = dh_ref[...]
        gate = _sigmoid(gl_ref[...])
        e = _dot(_mx(p_ref[...]), wp_ref[...])
        de_ref[...] = (dh * gate).astype(de_ref.dtype)
        dgl_ref[...] = ((dh * e) * (gate * (1.0 - gate))).astype(dgl_ref.dtype)

    row = lambda i: (i, 0)
    return pl.pallas_call(
        body, name=name, grid=(s_len // TM,),
        in_specs=[pl.BlockSpec((TM, d), row), pl.BlockSpec((TM, d), row), pl.BlockSpec((TM, PLE_DIM), row),
                  pl.BlockSpec((None, PLE_DIM, d), lambda i: (layer, 0, 0))],
        out_specs=[pl.BlockSpec((TM, d), row)] * 2,
        out_shape=[jax.ShapeDtypeStruct((s_len, d), MXU_DTYPE)] * 2,
        compiler_params=_cparams(("parallel",)),
    )(dh3, gl, p, w_ple)


def _rmsnorm_bwd(dhn, x, g):
    r = lax.rsqrt(jnp.mean(x * x, axis=-1, keepdims=True) + EPS)
    xh = x * r
    dxh = dhn * g
    dx = r * (dxh - xh * jnp.mean(dxh * xh, axis=-1, keepdims=True))
    return dx, dhn * xh


def _matmul_nt_norm_bwd(dy, w, layer, h_prev, g, dres, name, tk=1024, after=None):
    s_len, k_dim = dy.shape
    d = h_prev.shape[1]

    def body(dy_ref, w_ref, h_ref, g_ref, dres_ref, *rest):
        dh_ref, dg_ref = rest[-2:]
        i = pl.program_id(0)
        acc = None
        for k in range(k_dim // tk):
            part = _dot_nt(_mx(dy_ref[:, k * tk:(k + 1) * tk]), w_ref[:, k * tk:(k + 1) * tk])
            acc = part if acc is None else acc + part
        dx, dgrow = _rmsnorm_bwd(acc, h_ref[...], g_ref[...])
        dh_ref[...] = dres_ref[...] + dx
        dgsum = jnp.sum(dgrow, axis=0, keepdims=True)

        @pl.when(i == 0)
        def _():
            dg_ref[...] = dgsum

        @pl.when(i > 0)
        def _():
            dg_ref[...] += dgsum

    in_specs = [pl.BlockSpec((TM, k_dim), lambda i: (i, 0)),
                _resident((None, d, k_dim), lambda i: (layer, 0, 0)),
                pl.BlockSpec((TM, d), lambda i: (i, 0)),
                pl.BlockSpec((1, d), lambda i: (0, 0)),
                pl.BlockSpec((TM, d), lambda i: (i, 0))]
    args = [dy, w, h_prev, g, dres]
    if after is not None:
        in_specs.append(pl.BlockSpec(memory_space=pl.ANY))
        args.append(after)
    return pl.pallas_call(
        body, name=name, grid=(s_len // TM,), in_specs=in_specs,
        out_specs=[pl.BlockSpec((TM, d), lambda i: (i, 0)), pl.BlockSpec((1, d), lambda i: (0, 0))],
        out_shape=[jax.ShapeDtypeStruct((s_len, d), f32), jax.ShapeDtypeStruct((1, d), f32)],
        compiler_params=_cparams(("arbitrary",)),
    )(*args)


def _mlp_fwd(h1, g, w_up, w_down, layer, name, tf=1024):
    s_len, d = h1.shape
    ff = w_up.shape[2]

    def body(h_ref, g_ref, wu_ref, wd_ref, h2_ref, a_ref, hn_ref):
        x = h_ref[...]
        r = lax.rsqrt(jnp.mean(x * x, axis=-1, keepdims=True) + EPS)
        hn = ((x * r) * g_ref[...]).astype(hn_ref.dtype)
        hn_ref[...] = hn
        acc = x
        for j in range(ff // tf):
            a = _dot(hn, wu_ref[:, j * tf:(j + 1) * tf])
            a_ref[:, j * tf:(j + 1) * tf] = a.astype(a_ref.dtype)
            relu = jnp.maximum(a, 0.0)
            acc = acc + _dot(_mx(relu * relu), wd_ref[j * tf:(j + 1) * tf, :])
        h2_ref[...] = acc

    row = lambda i: (i, 0)
    return pl.pallas_call(
        body, name=name, grid=(s_len // TM,),
        in_specs=[pl.BlockSpec((TM, d), row), pl.BlockSpec((1, d), lambda i: (0, 0)),
                  _resident((None, d, ff), lambda i: (layer, 0, 0)), _resident((None, ff, d), lambda i: (layer, 0, 0))],
        out_specs=[pl.BlockSpec((TM, d), row), pl.BlockSpec((TM, ff), row), pl.BlockSpec((TM, d), row)],
        out_shape=[jax.ShapeDtypeStruct((s_len, d), f32), jax.ShapeDtypeStruct((s_len, ff), MXU_DTYPE),
                   jax.ShapeDtypeStruct((s_len, d), MXU_DTYPE)],
        compiler_params=_cparams(("parallel",)),
    )(h1, g, w_up, w_down)


def _down_bwd(dh2, w_down, layer, a, name, tf=1024):
    s_len, d = dh2.shape
    ff = a.shape[1]

    def body(dh_ref, w_ref, a_ref, da_ref, dhb_ref):
        j = pl.program_id(1)

        @pl.when(j == 0)
        def _():
            dhb_ref[...] = _mx(dh_ref[...])

        dact = _dot_nt(dhb_ref[...], w_ref[pl.ds(pl.multiple_of(j * tf, tf), tf), :])
        da_ref[...] = (dact * (2.0 * jnp.maximum(a_ref[...].astype(f32), 0.0))).astype(da_ref.dtype)

    return pl.pallas_call(
        body, name=name, grid=(s_len // TM, ff // tf),
        in_specs=[pl.BlockSpec((TM, d), lambda i, j: (i, 0)),
                  _resident((None, ff, d), lambda i, j: (layer, 0, 0)),
                  pl.BlockSpec((TM, tf), lambda i, j: (i, j))],
        out_specs=pl.BlockSpec((TM, tf), lambda i, j: (i, j)),
        out_shape=jax.ShapeDtypeStruct((s_len, ff), MXU_DTYPE),
        scratch_shapes=[pltpu.VMEM((TM, d), MXU_DTYPE)],
        compiler_params=_cparams(("parallel", "arbitrary")),
    )(dh2, w_down, a)


def _matmul_nt(dy, w, layer, name):
    s_len, n = dy.shape
    k_dim = w.shape[1]

    def body(dy_ref, w_ref, o_ref):
        o_ref[...] = _dot_nt(_mx(dy_ref[...]), w_ref[...])

    return pl.pallas_call(
        body, name=name, grid=(s_len // TM,),
        in_specs=[pl.BlockSpec((TM, n), lambda i: (i, 0)), pl.BlockSpec((None, k_dim, n), lambda i: (layer, 0, 0))],
        out_specs=pl.BlockSpec((TM, k_dim), lambda i: (i, 0)),
        out_shape=jax.ShapeDtypeStruct((s_len, k_dim), f32),
        compiler_params=_cparams(("parallel",)),
    )(dy, w)


def _weight_grad(a, b, name, act=False):
    s_len, k_dim = a.shape
    n = b.shape[1]
    tka = min(k_dim, 2048)
    tnb = n if n <= 1024 else (2048 if n % 2048 == 0 else 640)
    ns = s_len // TM_WGRAD

    def body(a_ref, b_ref, o_ref, acc_ref):
        s = pl.program_id(2)
        x = a_ref[...]
        if act:
            relu = jnp.maximum(x.astype(f32), 0.0)
            x = relu * relu
        part = _dot_tn(_mx(x), _mx(b_ref[...]))

        @pl.when(s == 0)
        def _():
            acc_ref[...] = part

        @pl.when(s > 0)
        def _():
            acc_ref[...] += part

        @pl.when(s == ns - 1)
        def _():
            o_ref[...] = acc_ref[...].astype(o_ref.dtype)

    return pl.pallas_call(
        body, name=name, grid=(k_dim // tka, n // tnb, ns),
        in_specs=[pl.BlockSpec((TM_WGRAD, tka), lambda i, j, s: (s, i)),
                  pl.BlockSpec((TM_WGRAD, tnb), lambda i, j, s: (s, j))],
        out_specs=pl.BlockSpec((tka, tnb), lambda i, j, s: (i, j)),
        out_shape=jax.ShapeDtypeStruct((k_dim, n), COMM_DTYPE),
        scratch_shapes=[pltpu.VMEM((tka, tnb), f32)],
        compiler_params=_cparams(("parallel", "parallel", "arbitrary")),
    )(a, b)


def _group_select(lane, x2, x4, x8, x16):
    grp = lane // POOL_GC
    return jnp.where(grp == 0, x2, jnp.where(grp == 1, x4, jnp.where(grp == 2, x8, x16)))


def _pool_window(lane):
    grp = lane // POOL_GC
    return jnp.where(grp == 0, 2, jnp.where(grp == 1, 4, jnp.where(grp == 2, 8, 16)))


def _pool_y(u, halo, i):
    xs = jnp.concatenate([jnp.where(i > 0, halo, 0.0), u], axis=0)
    s2 = xs + pltpu.roll(xs, 1, axis=0)
    s4 = s2 + pltpu.roll(s2, 2, axis=0)
    s8 = s4 + pltpu.roll(s4, 4, axis=0)
    s16 = s8 + pltpu.roll(s8, 8, axis=0)
    lane = lax.broadcasted_iota(jnp.int32, xs.shape, 1)
    sel = _group_select(lane, s2, s4, s8, s16)[HALO:, :]
    t = i * TM + lax.broadcasted_iota(jnp.int32, u.shape, 0)
    cnt = jnp.minimum(_pool_window(lax.broadcasted_iota(jnp.int32, u.shape, 1)), t + 1).astype(f32)
    return sel / cnt - u


def _group_weights(l0, l1, l2):
    mx = jnp.maximum(jnp.maximum(l0, l1), l2)
    e0, e1, e2 = jnp.exp(l0 - mx), jnp.exp(l1 - mx), jnp.exp(l2 - mx)
    den = e0 + e1 + e2
    return e0 / den, e1 / den, e2 / den


def _mixer_merge(z, wbd, scale, outs, lses, name):
    s_len = z.shape[0]

    def body(u_ref, halo_ref, wbd_ref, sc_ref, o0, o1, o2, l0, l1, l2, m_ref):
        i = pl.program_id(0)
        y = _pool_y(u_ref[...], halo_ref[...], i)
        pool = _dot(_mx(y), wbd_ref[...]) * sc_ref[...]
        w0, w1, w2 = _group_weights(l0[...], l1[...], l2[...])
        m_ref[...] = jnp.concatenate([pool, o0[...] * w0, o1[...] * w1, o2[...] * w2], axis=1).astype(m_ref.dtype)

    row = lambda i: (i, 0)
    blk = pl.BlockSpec((TM, 256), row)
    grp = [pl.BlockSpec((TM, 256), lambda i, g=g: (i, g)) for g in range(3)]
    return pl.pallas_call(
        body, name=name, grid=(s_len // TM,),
        in_specs=[blk, pl.BlockSpec((HALO, 256), lambda i: (jnp.maximum(i * (TM // HALO) - 1, 0), 0)),
                  pl.BlockSpec((256, 256), lambda i: (0, 0)), pl.BlockSpec((1, 256), lambda i: (0, 0))] + grp + grp,
        out_specs=pl.BlockSpec((TM, D_MODEL), row),
        out_shape=jax.ShapeDtypeStruct((s_len, D_MODEL), MXU_DTYPE),
        compiler_params=_cparams(("parallel",)),
    )(z, z, wbd, scale, outs, outs, outs, lses, lses, lses)


def _head_sums(x):
    r = lax.broadcasted_iota(jnp.int32, (256, 256), 0) // HEAD_DIM
    c = lax.broadcasted_iota(jnp.int32, (256, 256), 1) // HEAD_DIM
    ones = jnp.where(r == c, 1.0, 0.0).astype(jnp.bfloat16)
    hi = x.astype(jnp.bfloat16)
    lo = (x - hi.astype(f32)).astype(jnp.bfloat16)
    return _dot(hi, ones) + _dot(lo, ones)


def _combine_bwd(dm, outs, lses, name):
    s_len = dm.shape[0]

    def body(d0, d1, d2, o0, o1, o2, l0, l1, l2, do_ref, dl_ref):
        w = _group_weights(l0[...], l1[...], l2[...])
        da = (d0[...], d1[...], d2[...])
        o = (o0[...], o1[...], o2[...])
        dw = [_head_sums(da[g] * o[g]) for g in range(3)]
        t = w[0] * dw[0] + w[1] * dw[1] + w[2] * dw[2]
        do_ref[...] = jnp.concatenate([da[g] * w[g] for g in range(3)], axis=1)
        dl_ref[...] = jnp.concatenate([w[g] * t for g in range(3)], axis=1)

    grp = [pl.BlockSpec((TM, 256), lambda i, g=g: (i, g)) for g in range(3)]
    return pl.pallas_call(
        body, name=name, grid=(s_len // TM,),
        in_specs=[pl.BlockSpec((TM, 256), lambda i: (i, 1)), pl.BlockSpec((TM, 256), lambda i: (i, 2)),
                  pl.BlockSpec((TM, 256), lambda i: (i, 3))] + grp + grp,
        out_specs=[pl.BlockSpec((TM, ATTN_WIDTH), lambda i: (i, 0))] * 2,
        out_shape=[jax.ShapeDtypeStruct((s_len, ATTN_WIDTH), f32)] * 2,
        compiler_params=_cparams(("parallel",)),
    )(dm, dm, dm, outs, outs, outs, lses, lses, lses)


def _pool_bwd(z, dm, wbd, scale, name):
    s_len = z.shape[0]
    n_halo = s_len // HALO

    def body(u_ref, uh_ref, d_ref, dh_ref, wbd_ref, sc_ref, du_ref, dw_ref, dsc_ref):
        i = pl.program_id(0)
        last = pl.num_programs(0) - 1
        y = _pool_y(u_ref[...], uh_ref[...], i)
        yb = _mx(y)
        dpo = d_ref[...]
        sc = sc_ref[...]
        dsc = jnp.sum(dpo * _dot(yb, wbd_ref[...]), axis=0, keepdims=True)
        dwp = _dot_tn(yb, _mx(dpo * sc))

        @pl.when(i == 0)
        def _():
            dsc_ref[...] = dsc
            dw_ref[...] = dwp

        @pl.when(i > 0)
        def _():
            dsc_ref[...] += dsc
            dw_ref[...] += dwp

        ext = jnp.concatenate([dpo, jnp.where(i < last, dh_ref[...], 0.0)], axis=0)
        dy = _dot_nt(_mx(ext * sc), wbd_ref[...])
        t = i * TM + lax.broadcasted_iota(jnp.int32, ext.shape, 0)
        lane = lax.broadcasted_iota(jnp.int32, ext.shape, 1)
        e = dy / jnp.minimum(_pool_window(lane), t + 1).astype(f32)
        rows = ext.shape[0]
        f2 = e + pltpu.roll(e, rows - 1, axis=0)
        f4 = f2 + pltpu.roll(f2, rows - 2, axis=0)
        f8 = f4 + pltpu.roll(f4, rows - 4, axis=0)
        f16 = f8 + pltpu.roll(f8, rows - 8, axis=0)
        du_ref[...] = (_group_select(lane, f2, f4, f8, f16) - dy)[:TM, :]

    row = lambda i: (i, 0)
    blk = pl.BlockSpec((TM, 256), row)
    return pl.pallas_call(
        body, name=name, grid=(s_len // TM,),
        in_specs=[blk, pl.BlockSpec((HALO, 256), lambda i: (jnp.maximum(i * (TM // HALO) - 1, 0), 0)),
                  blk, pl.BlockSpec((HALO, 256), lambda i: (jnp.minimum((i + 1) * (TM // HALO), n_halo - 1), 0)),
                  pl.BlockSpec((256, 256), lambda i: (0, 0)), pl.BlockSpec((1, 256), lambda i: (0, 0))],
        out_specs=[blk, pl.BlockSpec((256, 256), lambda i: (0, 0)), pl.BlockSpec((1, 256), lambda i: (0, 0))],
        out_shape=[jax.ShapeDtypeStruct((s_len, N_IN), f32), jax.ShapeDtypeStruct((256, 256), f32),
                   jax.ShapeDtypeStruct((1, 256), f32)],
        compiler_params=_cparams(("arbitrary",)),
    )(z, z, dm, dm, wbd, scale)


def _to_strided(x, dil):
    if dil == 1:
        return x
    s_len, c = x.shape
    return x.reshape(s_len // (BLK * dil), BLK, dil, c).transpose(0, 2, 1, 3).reshape(s_len, c)


def _from_strided(x, dil):
    if dil == 1:
        return x
    s_len, c = x.shape
    return x.reshape(s_len // (BLK * dil), dil, BLK, c).transpose(0, 2, 1, 3).reshape(s_len, c)


def _tri_masks():
    qi = lax.broadcasted_iota(jnp.int32, (BLK, BLK), 0)
    ki = lax.broadcasted_iota(jnp.int32, (BLK, BLK), 1)
    return qi >= ki, ki >= qi


ATTN_SUPER_PER_STEP = (8, 2, 1)
Q_COL, K_COL, V_COL = POOL_WIDTH // 128, (POOL_WIDTH + ATTN_WIDTH) // 128, (POOL_WIDTH + 2 * ATTN_WIDTH) // 128


def _rows(ref, start, dil):
    if dil == 1:
        return ref[pl.ds(start, BLK), :]
    return ref[pl.ds(start, BLK, stride=dil), :]


RESIDUE_UNROLL = 4


def _for_residues(dil, fn):
    if dil <= RESIDUE_UNROLL:
        for r in range(dil):
            fn(r, 0)
    else:
        lax.fori_loop(0, dil, fn, 0, unroll=RESIDUE_UNROLL)


def _set_rows(ref, start, dil, val):
    if dil == 1:
        ref[pl.ds(start, BLK), :] = val
    else:
        ref[pl.ds(start, BLK, stride=dil), :] = val


def _attn_fwd(z, g, prev, name):
    s_len = z.shape[0]
    dil, m = DILATIONS[g], ATTN_SUPER_PER_STEP[g]
    sbr = BLK * dil
    rows = sbr * m

    def body(*refs):
        q_ref, kc_ref, kp_ref, vc_ref, vp_ref = refs[:5]
        o_ref, l_ref = refs[-2:]
        st = pl.program_id(0)
        low, up = _tri_masks()
        head0 = lax.broadcasted_iota(jnp.int32, (BLK, 128), 1) < HEAD_DIM
        for sb in range(m):
            valid = jnp.concatenate([up & (st > 0) if sb == 0 else up, low], axis=1)

            def one_residue(r, carry, sb=sb, valid=valid):
                base = sb * sbr + r
                q = _rows(q_ref, base, dil)
                kc, vc = _rows(kc_ref, base, dil), _rows(vc_ref, base, dil)
                if sb == 0:
                    kp, vp = _rows(kp_ref, r, dil), _rows(vp_ref, r, dil)
                else:
                    kp, vp = _rows(kc_ref, base - sbr, dil), _rows(vc_ref, base - sbr, dil)
                k2 = jnp.concatenate([_mx(kp), _mx(kc)], axis=0)
                v2 = jnp.concatenate([_mx(vp), _mx(vc)], axis=0)
                outs, lses = [], []
                for hh in range(2):
                    s = jnp.where(valid, _dot_nt(_mx(jnp.where(head0 == (hh == 0), q, 0.0)), k2) * ATTN_SCALE, NEG_BIG)
                    mx = jnp.max(s, axis=-1, keepdims=True)
                    e = jnp.exp(s - mx)
                    l = jnp.sum(e, axis=-1, keepdims=True)
                    outs.append(_dot(_mx(e / l), v2))
                    lses.append(jnp.broadcast_to(mx + jnp.log(l), (BLK, 128)))
                _set_rows(o_ref, base, dil, jnp.where(head0, outs[0], outs[1]))
                _set_rows(l_ref, base, dil, jnp.where(head0, lses[0], lses[1]))
                return carry

            _for_residues(dil, one_residue)

    def cur(col):
        return pl.BlockSpec((rows, 128), lambda st, hp: (st, col + 2 * g + hp))

    def before(col):
        return pl.BlockSpec((sbr, 128), lambda st, hp: (jnp.maximum(st * m - 1, 0), col + 2 * g + hp))

    in_specs = [cur(Q_COL), cur(K_COL), before(K_COL), cur(V_COL), before(V_COL)]
    args = [z, z, z, z, z]
    aliases = {}
    if prev is not None:
        in_specs += [pl.BlockSpec(memory_space=pl.ANY)] * 2
        args += list(prev)
        aliases = {5: 0, 6: 1}
    return pl.pallas_call(
        body, name=name, grid=(s_len // rows, 2), in_specs=in_specs, out_specs=[cur(0), cur(0)],
        out_shape=[jax.ShapeDtypeStruct((s_len, ATTN_WIDTH), f32)] * 2, input_output_aliases=aliases,
        compiler_params=_cparams(("parallel", "parallel")),
    )(*args)


def _stack_heads(x, head0):
    return jnp.concatenate([_mx(jnp.where(head0, x, 0.0)), _mx(jnp.where(head0, 0.0, x))], axis=0)


def _head_rows(x):
    xt = x.T
    return jnp.concatenate([jnp.broadcast_to(xt[0:1, :], (BLK, BLK)),
                            jnp.broadcast_to(xt[HEAD_DIM:HEAD_DIM + 1, :], (BLK, BLK))], axis=0)


def _attn_bwd(z, do, lse, dlt, tabs, dz, g, name):
    s_len = z.shape[0]
    dil, m = DILATIONS[g], ATTN_SUPER_PER_STEP[g]
    sbr = BLK * dil
    rows = sbr * m
    nsteps = s_len // rows

    def body(q_ref, qn_ref, kc_ref, kp_ref, vc_ref, vp_ref, do_ref, don_ref, l_ref, ln_ref, d_ref, dn_ref,
             c_ref, s1_ref, s2_ref, dz_in, dz_ref, dq_buf, dk_buf, dv_buf, sems):
        del dz_in
        st, hp = pl.program_id(0), pl.program_id(1)
        head0 = lax.broadcasted_iota(jnp.int32, (BLK, 128), 1) < HEAD_DIM
        key_i = lax.broadcasted_iota(jnp.int32, (2 * BLK, BLK), 0) & (BLK - 1)
        query_i = lax.broadcasted_iota(jnp.int32, (2 * BLK, BLK), 1)
        same_t, cross_t = query_i >= key_i, key_i >= query_i
        for sb in range(m):
            prev_t = cross_t & (st > 0) if sb == 0 else cross_t
            next_t = cross_t & (st < nsteps - 1) if sb == m - 1 else cross_t

            def one_residue(r, carry, sb=sb, prev_t=prev_t, next_t=next_t):
                base = sb * sbr + r
                q, k, v = _rows(q_ref, base, dil), _rows(kc_ref, base, dil), _rows(vc_ref, base, dil)
                do_c, l_c, d_c = _rows(do_ref, base, dil), _rows(l_ref, base, dil), _rows(d_ref, base, dil)
                if sb == 0:
                    kp, vp = _rows(kp_ref, r, dil), _rows(vp_ref, r, dil)
                else:
                    kp, vp = _rows(kc_ref, base - sbr, dil), _rows(vc_ref, base - sbr, dil)
                if sb == m - 1:
                    qn, do_n = _rows(qn_ref, r, dil), _rows(don_ref, r, dil)
                    l_n, d_n = _rows(ln_ref, r, dil), _rows(dn_ref, r, dil)
                else:
                    qn, do_n = _rows(q_ref, base + sbr, dil), _rows(do_ref, base + sbr, dil)
                    l_n, d_n = _rows(l_ref, base + sbr, dil), _rows(d_ref, base + sbr, dil)
                k2, kp2, v2, vp2 = _stack_heads(k, head0), _stack_heads(kp, head0), _stack_heads(v, head0), _stack_heads(vp, head0)
                qb, qnb, dob, donb = _mx(q), _mx(qn), _mx(do_c), _mx(do_n)
                lse2, dlt2, lsen2, dltn2 = _head_rows(l_c), _head_rows(d_c), _head_rows(l_n), _head_rows(d_n)

                def pair(keys, vals, qs, dos, lse_rows, dlt_rows, valid):
                    p = jnp.where(valid, jnp.exp(_dot_nt(keys, qs) * ATTN_SCALE - lse_rows), 0.0)
                    ds = _mx(p * (_dot_nt(vals, dos) - dlt_rows) * ATTN_SCALE)
                    return _mx(p), ds

                p_a, ds_a = pair(k2, v2, qb, dob, lse2, dlt2, same_t)
                _, ds_b = pair(kp2, vp2, qb, dob, lse2, dlt2, prev_t)
                p_c, ds_c = pair(k2, v2, qnb, donb, lsen2, dltn2, next_t)
                dq = _dot_tn(ds_a, k2) + _dot_tn(ds_b, kp2)
                dk2 = _dot(ds_a, qb) + _dot(ds_c, qnb)
                dv2 = _dot(p_a, dob) + _dot(p_c, donb)
                c, s1, s2 = _rows(c_ref, base, dil), _rows(s1_ref, base, dil), _rows(s2_ref, base, dil)
                _set_rows(dq_buf, base, dil, _rope_transpose(dq, c, s1, s2, 128))
                _set_rows(dk_buf, base, dil, _rope_transpose(jnp.where(head0, dk2[:BLK], dk2[BLK:]), c, s1, s2, 128))
                _set_rows(dv_buf, base, dil, jnp.where(head0, dv2[:BLK], dv2[BLK:]))
                return carry

            _for_residues(dil, one_residue)
        copies = []
        for t, (buf, col) in enumerate(((dq_buf, Q_COL), (dk_buf, K_COL), (dv_buf, V_COL))):
            lane0 = pl.multiple_of((col + 2 * g + hp) * 128, 128)
            dst = dz_ref.at[pl.ds(pl.multiple_of(st * rows, rows), rows), pl.ds(lane0, 128)]
            cp = pltpu.make_async_copy(buf, dst, sems.at[t])
            cp.start()
            copies.append(cp)
        for cp in copies:
            cp.wait()

    def cur(col):
        return pl.BlockSpec((rows, 128), lambda st, hp: (st, col + 2 * g + hp))

    def before(col):
        return pl.BlockSpec((sbr, 128), lambda st, hp: (jnp.maximum(st * m - 1, 0), col + 2 * g + hp))

    def after(col):
        return pl.BlockSpec((sbr, 128), lambda st, hp: (jnp.minimum((st + 1) * m, s_len // sbr - 1), col + 2 * g + hp))

    tab = pl.BlockSpec((rows, 128), lambda st, hp: (st, 0))
    return pl.pallas_call(
        body, name=name, grid=(nsteps, 2),
        in_specs=[cur(Q_COL), after(Q_COL), cur(K_COL), before(K_COL), cur(V_COL), before(V_COL),
                  cur(0), after(0), cur(0), after(0), cur(0), after(0), tab, tab, tab,
                  pl.BlockSpec(memory_space=pl.ANY)],
        out_specs=pl.BlockSpec(memory_space=pl.ANY),
        out_shape=jax.ShapeDtypeStruct(dz.shape, dz.dtype), input_output_aliases={15: 0},
        scratch_shapes=[pltpu.VMEM((rows, 128), f32)] * 3 + [pltpu.SemaphoreType.DMA((3,))],
        compiler_params=_cparams(("arbitrary", "arbitrary")),
    )(z, z, z, z, z, z, do, do, lse, lse, dlt, dlt, *tabs, dz)


def _attn_fwd_old(q, k, v, dil, name):
    s_len = q.shape[0]
    nblk = s_len // BLK

    def body(q_ref, kc_ref, kp_ref, vc_ref, vp_ref, o_ref, l_ref):
        b = pl.program_id(0)
        has_prev = b >= dil
        low, up = _tri_masks()
        valid = jnp.concatenate([up & has_prev, low], axis=1)
        outs, lses = [], []
        for hh in range(2):
            sl = slice(hh * HEAD_DIM, (hh + 1) * HEAD_DIM)
            qh = _mx(q_ref[:, sl])
            k2 = jnp.concatenate([_mx(kp_ref[:, sl]), _mx(kc_ref[:, sl])], axis=0)
            v2 = jnp.concatenate([_mx(vp_ref[:, sl]), _mx(vc_ref[:, sl])], axis=0)
            s = jnp.where(valid, _dot_nt(qh, k2) * ATTN_SCALE, NEG_BIG)
            m = jnp.max(s, axis=-1, keepdims=True)
            e = jnp.exp(s - m)
            l = jnp.sum(e, axis=-1, keepdims=True)
            outs.append(_dot(_mx(e / l), v2))
            lses.append(jnp.broadcast_to(m + jnp.log(l), (BLK, HEAD_DIM)))
        o_ref[...] = jnp.concatenate(outs, axis=1)
        l_ref[...] = jnp.concatenate(lses, axis=1)

    cur = pl.BlockSpec((BLK, 128), lambda b, hp: (b, hp))
    prev = pl.BlockSpec((BLK, 128), lambda b, hp: (jnp.maximum(b - dil, 0), hp))
    return pl.pallas_call(
        body, name=name, grid=(nblk, 2), in_specs=[cur, cur, prev, cur, prev], out_specs=[cur, cur],
        out_shape=[jax.ShapeDtypeStruct((s_len, 256), f32)] * 2,
        compiler_params=_cparams(("parallel", "parallel")),
    )(q, k, k, v, v)


def _attn_bwd_old(q, k, v, do, lse, dlt, tabs, dil, name):
    s_len = q.shape[0]
    nblk = s_len // BLK

    def body(q_ref, qn_ref, kc_ref, kp_ref, vc_ref, vp_ref, do_ref, don_ref, l_ref, ln_ref, d_ref, dn_ref,
             c_ref, s1_ref, s2_ref, dq_ref, dk_ref, dv_ref):
        b = pl.program_id(0)
        has_prev = b >= dil
        has_next = b + dil < nblk
        low, up = _tri_masks()
        dqs, dks, dvs = [], [], []
        for hh in range(2):
            sl = slice(hh * HEAD_DIM, (hh + 1) * HEAD_DIM)
            one = slice(hh * HEAD_DIM, hh * HEAD_DIM + 1)
            qc, qn = _mx(q_ref[:, sl]), _mx(qn_ref[:, sl])
            kc, kp = _mx(kc_ref[:, sl]), _mx(kp_ref[:, sl])
            vc, vp = _mx(vc_ref[:, sl]), _mx(vp_ref[:, sl])
            doc, don = _mx(do_ref[:, sl]), _mx(don_ref[:, sl])
            lc, ln = l_ref[:, one], ln_ref[:, one]
            dc, dn = d_ref[:, one], dn_ref[:, one]
            p_a = jnp.where(low, jnp.exp(_dot_nt(qc, kc) * ATTN_SCALE - lc), 0.0)
            ds_a = _mx(p_a * (_dot_nt(doc, vc) - dc) * ATTN_SCALE)
            p_b = jnp.where(up & has_prev, jnp.exp(_dot_nt(qc, kp) * ATTN_SCALE - lc), 0.0)
            ds_b = _mx(p_b * (_dot_nt(doc, vp) - dc) * ATTN_SCALE)
            p_c = jnp.where(up & has_next, jnp.exp(_dot_nt(qn, kc) * ATTN_SCALE - ln), 0.0)
            ds_c = _mx(p_c * (_dot_nt(don, vc) - dn) * ATTN_SCALE)
            dqs.append(_dot(ds_a, kc) + _dot(ds_b, kp))
            dks.append(_dot_tn(ds_a, qc) + _dot_tn(ds_c, qn))
            dvs.append(_dot_tn(_mx(p_a), doc) + _dot_tn(_mx(p_c), don))
        c, s1, s2 = c_ref[...], s1_ref[...], s2_ref[...]
        dq_ref[...] = _rope_transpose(jnp.concatenate(dqs, axis=1), c, s1, s2, 128)
        dk_ref[...] = _rope_transpose(jnp.concatenate(dks, axis=1), c, s1, s2, 128)
        dv_ref[...] = jnp.concatenate(dvs, axis=1)

    cur = pl.BlockSpec((BLK, 128), lambda b, hp: (b, hp))
    prev = pl.BlockSpec((BLK, 128), lambda b, hp: (jnp.maximum(b - dil, 0), hp))
    nxt = pl.BlockSpec((BLK, 128), lambda b, hp: (jnp.minimum(b + dil, nblk - 1), hp))
    tab = pl.BlockSpec((BLK, 128), lambda b, hp: (b, 0))
    return pl.pallas_call(
        body, name=name, grid=(nblk, 2),
        in_specs=[cur, nxt, cur, prev, cur, prev, cur, nxt, cur, nxt, cur, nxt, tab, tab, tab],
        out_specs=[cur, cur, cur], out_shape=[jax.ShapeDtypeStruct((s_len, 256), f32)] * 3,
        compiler_params=_cparams(("parallel", "parallel")),
    )(q, q, k, k, v, v, do, do, lse, lse, dlt, dlt, *tabs)


def _loss_head(h, g, target, name):
    s_len, d = h.shape

    def body(h_ref, g_ref, t_ref, loss_ref, dh_ref, dg_ref):
        i = pl.program_id(0)
        x = h_ref[...]
        gv = g_ref[...]
        r = lax.rsqrt(jnp.mean(x * x, axis=-1, keepdims=True) + EPS)
        xh = x * r
        diff = xh * gv - t_ref[...]
        part = 0.5 * jnp.sum(jnp.mean(diff * diff, axis=-1, keepdims=True), axis=0, keepdims=True)
        dy = diff * (1.0 / d)
        dxh = dy * gv
        dh_ref[...] = r * (dxh - xh * jnp.mean(dxh * xh, axis=-1, keepdims=True))
        dgsum = jnp.sum(dy * xh, axis=0, keepdims=True)
        lossb = jnp.broadcast_to(part, (8, 128))

        @pl.when(i == 0)
        def _():
            loss_ref[...] = lossb
            dg_ref[...] = dgsum

        @pl.when(i > 0)
        def _():
            loss_ref[...] += lossb
            dg_ref[...] += dgsum

    row = lambda i: (i, 0)
    return pl.pallas_call(
        body, name=name, grid=(s_len // TM,),
        in_specs=[pl.BlockSpec((TM, d), row), pl.BlockSpec((1, d), lambda i: (0, 0)), pl.BlockSpec((TM, d), row)],
        out_specs=[pl.BlockSpec((8, 128), lambda i: (0, 0)), pl.BlockSpec((TM, d), row),
                   pl.BlockSpec((1, d), lambda i: (0, 0))],
        out_shape=[jax.ShapeDtypeStruct((8, 128), f32), jax.ShapeDtypeStruct((s_len, d), f32),
                   jax.ShapeDtypeStruct((1, d), f32)],
        compiler_params=_cparams(("arbitrary",)),
    )(h, g, target)


def _rope_tables(positions):
    inv_freq = ROPE_THETA ** (-jnp.arange(0, ROT_DIM, 2, dtype=f32) / ROT_DIM)
    ang = positions.astype(f32)[:, None] * inv_freq
    cos, sin = jnp.cos(ang), jnp.sin(ang)
    s_len = positions.shape[0]
    zero8, rest = jnp.zeros((s_len, 8), f32), jnp.zeros((s_len, HEAD_DIM - ROT_DIM), f32)
    c = jnp.concatenate([cos, cos, jnp.ones((s_len, HEAD_DIM - ROT_DIM), f32)], axis=1)
    s1 = jnp.concatenate([-sin, zero8, rest], axis=1)
    s2 = jnp.concatenate([zero8, sin, rest], axis=1)
    return c, s1, s2


def _block_diag(pool_w):
    out = jnp.zeros((POOL_WIDTH, POOL_WIDTH), pool_w.dtype)
    for g in range(4):
        out = lax.dynamic_update_slice(out, pool_w[g], (g * POOL_GC, g * POOL_GC))
    return out


class _ReadyWeights:
    def __init__(self, full):
        self.full = full

    def take(self, layer, names, after):
        del after
        return {n: self.full[n] for n in names}, layer


def _layer_fwd(h, p_l, wsrc, small, layer, tabs):
    nm = f"l{layer}_"
    wts, wl = wsrc.take(layer, ("w_in",), h if layer else None)
    z, hn1 = _norm_matmul(h, small["norm1"][layer][None], wts["w_in"], wl, 256, nm + "in_proj", rope=tabs)
    ol = None
    for g in range(3):
        ol = _attn_fwd(z, g, ol, nm + f"attn_fwd{g}")
    outs, lses = ol
    wbd = _mx(_block_diag(small["pool_w"][layer]))
    scale = small["pool_scale"][layer][None]
    m = _mixer_merge(z, wbd, scale, outs, lses, nm + "mixer_merge")
    rest, _ = wsrc.take(layer, ("w_out", "w_up", "w_down", "w_gate", "w_ple"), m)
    wts = {**wts, **rest}
    h1 = _matmul_residual(m, wts["w_out"], wl, h, nm + "out_proj")
    h2, a, hn2 = _mlp_fwd(h1, small["norm2"][layer][None], wts["w_up"], wts["w_down"], wl, nm + "mlp")
    h3, gl, hn3 = _gate_ple_fwd(h2, small["norm3"][layer][None], wts["w_gate"], wts["w_ple"], wl, p_l, nm + "gate_ple")
    saved = dict(h=h, z=z, hn1=hn1, outs=outs, lses=lses, wbd=wbd, scale=scale, m=m, h1=h1, a=a, hn2=hn2, h2=h2,
                 gl=gl, hn3=hn3, wts=wts, wl=wl)
    return h3, saved


def _layer_bwd(dh3, sv, p_l, small, layer, tabs128, reducer):
    nm = f"l{layer}_"
    wts, wl = sv["wts"], sv["wl"]
    de, dgl = _gate_ple_bwd(dh3, sv["gl"], p_l, wts["w_ple"], wl, nm + "gate_ple_bwd")
    reducer.add("w_gate", layer, _weight_grad(sv["hn3"], dgl, nm + "dw_gate"))
    reducer.add("w_ple", layer, _weight_grad(p_l, de, nm + "dw_ple"))
    dh2, dg3 = _matmul_nt_norm_bwd(dgl, wts["w_gate"], wl, sv["h2"], small["norm3"][layer][None], dh3, nm + "gate_bwd")
    da = _down_bwd(dh2, wts["w_down"], wl, sv["a"], nm + "down_bwd")
    reducer.add("w_down", layer, _weight_grad(sv["a"], dh2, nm + "dw_down", act=True))
    started = reducer.add("w_up", layer, _weight_grad(sv["hn2"], da, nm + "dw_up"))
    dh1, dg2 = _matmul_nt_norm_bwd(da, wts["w_up"], wl, sv["h1"], small["norm2"][layer][None], dh2, nm + "up_bwd",
                                   after=started)
    dm = _matmul_nt(dh1, wts["w_out"], wl, nm + "out_bwd")
    reducer.add("w_out", layer, _weight_grad(sv["m"], dh1, nm + "dw_out"))
    do, dlt = _combine_bwd(dm, sv["outs"], sv["lses"], nm + "combine_bwd")
    dz, dwbd, dscale = _pool_bwd(sv["z"], dm, sv["wbd"], sv["scale"], nm + "pool_bwd")
    for g in range(3):
        dz = _attn_bwd(sv["z"], do, sv["lses"], dlt, tabs128, dz, g, nm + f"attn_bwd{g}")
    started = reducer.add("w_in", layer, _weight_grad(sv["hn1"], dz, nm + "dw_in"))
    dh0, dg1 = _matmul_nt_norm_bwd(dz, wts["w_in"], wl, sv["h"], small["norm1"][layer][None], dh1, nm + "in_bwd",
                                   tk=512, after=started)
    dpool_w = jnp.stack([dwbd[g * POOL_GC:(g + 1) * POOL_GC, g * POOL_GC:(g + 1) * POOL_GC] for g in range(4)])
    sg = dict(norm1=dg1[0], norm2=dg2[0], norm3=dg3[0], pool_w=dpool_w, pool_scale=dscale[0])
    return dh0, sg


class _CollectGrads:
    def __init__(self):
        self.grads = {}

    def add(self, name, layer, dw):
        self.grads[(name, layer)] = dw


def _local_step(x, p, positions, wsrc, small, target, reducer):
    tabs128 = tuple(jnp.tile(t, (1, 2)) for t in _rope_tables(positions))
    h = x
    saved = []
    for layer in range(2):
        h, sv = _layer_fwd(h, p[layer], wsrc, small, layer, tabs128)
        saved.append(sv)
    loss, dh, dgf = _loss_head(h, small["final_norm"][None], target, "loss_head")
    sgs = [None, None]
    for layer in (1, 0):
        dh, sgs[layer] = _layer_bwd(dh, saved[layer], p[layer], small, layer, tabs128, reducer)
    small_grads = {k: jnp.stack([sgs[0][k], sgs[1][k]]) for k in sgs[0]}
    small_grads["final_norm"] = dgf[0]
    return loss, dh, small_grads


HBM = pl.BlockSpec(memory_space=pltpu.HBM)


def _my_place():
    return lax.axis_index("x"), lax.axis_index("y"), lax.axis_index("c")


def _other_chips(x, y):
    return [(1 - x, y), (x, 1 - y), (1 - x, 1 - y)]


def _window(ref, name, chip):
    k, n = _shard_shape(name)
    if COL_SHARDED[name]:
        return ref.at[:, pl.ds(pl.multiple_of(chip * n, 128), n)]
    return ref.at[pl.ds(pl.multiple_of(chip * k, 128), k), :]


def _chip_index():
    return jnp.reshape(2 * lax.axis_index("x") + lax.axis_index("y"), (1,)).astype(jnp.int32)


def _shard_block(name, tr):
    ks, ns = _shard_shape(name)
    if COL_SHARDED[name]:
        return (tr, ns), lambda i, me: (i, me[0])
    return (tr, ns), lambda i, me: (me[0] * (ks // tr) + i, 0)


def _place_shard(w, name, layer):
    ks, ns = _shard_shape(name)
    tr = min(ks, 256)
    shape, index = _shard_block(name, tr)

    def body(me_ref, w_ref, o_ref):
        o_ref[...] = w_ref[...].astype(o_ref.dtype)

    return pl.pallas_call(
        body, name=f"place_{name}{layer}",
        grid_spec=pltpu.PrefetchScalarGridSpec(
            num_scalar_prefetch=1, grid=(ks // tr,),
            in_specs=[pl.BlockSpec((None, tr, ns), lambda i, me: (layer, i, 0))],
            out_specs=pl.BlockSpec((None,) + shape, lambda i, me: (0,) + index(i, me))),
        out_shape=jax.ShapeDtypeStruct((1,) + FULL_SHAPE[name], MXU_DTYPE),
        compiler_params=_cparams(("parallel",)),
    )(_chip_index(), w)


GATHER_ORDER = [("w_in", 0), ("w_out", 0), ("w_up", 0), ("w_down", 0), ("w_gate", 0), ("w_ple", 0),
                ("w_in", 1), ("w_out", 1), ("w_up", 1), ("w_down", 1), ("w_gate", 1), ("w_ple", 1)]
SEM = pl.BlockSpec(memory_space=pltpu.SEMAPHORE)
EFFECT = pltpu.SideEffectType.DATAFLOW_SIDE_EFFECTING


def _gather_copy(src_ref, dst_ref, name, idx, j, chip, send_sems, recv_sems, c):
    cx, cy = chip
    return pltpu.make_async_remote_copy(
        src_ref=src_ref, dst_ref=dst_ref, send_sem=send_sems.at[3 * idx + j], recv_sem=recv_sems.at[3 * idx + j],
        device_id=(cx, cy, c), device_id_type=MESH)


def _gather_start(placed, order, tag, after=None):
    n = len(order)
    extra = [] if after is None else [after]

    def body(*refs):
        ins = refs[:n]
        k = n + len(extra)
        send_sems, recv_sems = refs[k], refs[k + 1]
        outs = refs[k + 2:k + 2 + n]
        token = refs[-1]
        x, y, c = _my_place()
        me = 2 * x + y
        for idx, (name, _) in enumerate(order):
            for j, chip in enumerate(_other_chips(x, y)):
                _gather_copy(_window(ins[idx].at[0], name, me), _window(outs[idx].at[0], name, me), name, idx, j, chip,
                             send_sems, recv_sems, c).start()
        token[...] = jnp.zeros_like(token)

    res = pl.pallas_call(
        body, name="gather_start" + tag,
        out_shape=(pltpu.SemaphoreType.DMA((3 * n,)), pltpu.SemaphoreType.DMA((3 * n,)))
        + tuple(pltpu.HBM(a.shape, a.dtype) for a in placed) + (jax.ShapeDtypeStruct((8, 128), f32),),
        in_specs=[HBM] * n + [pl.BlockSpec(memory_space=pl.ANY)] * len(extra),
        out_specs=(SEM, SEM) + (HBM,) * n + (pl.BlockSpec(memory_space=pltpu.VMEM),),
        input_output_aliases={i: i + 2 for i in range(n)},
        compiler_params=pltpu.CompilerParams(has_side_effects=EFFECT),
    )(*[pltpu.with_memory_space_constraint(a, pltpu.HBM) for a in placed], *extra)
    return res[0], res[1], list(res[2:2 + n]), res[-1]


def _gather_wait(send_sems, recv_sems, arrays, order, idxs, after, name):
    n = len(idxs)

    def body(*refs):
        ins = refs[:n]
        send_ref, recv_ref = refs[n], refs[n + 1]
        x, y, c = _my_place()
        me = 2 * x + y
        for k, idx in enumerate(idxs):
            wname = order[idx][0]
            for j, chip in enumerate(_other_chips(x, y)):
                cx, cy = chip
                mine = _window(ins[k].at[0], wname, me)
                land = _window(ins[k].at[0], wname, 2 * cx + cy)
                _gather_copy(mine, mine, wname, idx, j, chip, send_ref, recv_ref, c).wait_send()
                _gather_copy(land, land, wname, idx, j, chip, send_ref, recv_ref, c).wait_recv()

    operands = list(arrays) + [send_sems, recv_sems]
    in_specs = [HBM] * n + [SEM, SEM]
    if after is not None:
        operands.append(after)
        in_specs.append(pl.BlockSpec(memory_space=pl.ANY))
    res = pl.pallas_call(
        body, name=name, out_shape=tuple(pltpu.HBM(a.shape, a.dtype) for a in arrays),
        in_specs=in_specs, out_specs=(HBM,) * n, input_output_aliases={i: i for i in range(n)},
        compiler_params=pltpu.CompilerParams(has_side_effects=EFFECT),
    )(*operands)
    return list(res)


class _GatheredWeights:
    def __init__(self, shards):
        self.starts = []
        token = None
        for tag, order in (("_first", GATHER_ORDER[:1]), ("_rest", GATHER_ORDER[1:])):
            placed = [_place_shard(shards[name], name, layer) for name, layer in order]
            self.starts.append((order,) + _gather_start(placed, order, tag, token))
            token = self.starts[-1][-1]

    def take(self, layer, names, after):
        order, send, recv, arrays, _ = next(s for s in self.starts if (names[0], layer) in s[0])
        if after is None:
            after = self.starts[-1][-1]
        idxs = [order.index((n, layer)) for n in names]
        got = _gather_wait(send, recv, [arrays[i] for i in idxs], order, idxs, after, f"gather_wait{layer}_{names[0]}")
        return dict(zip(names, got)), 0


def _gather_weights(full):
    names = list(BIG)

    def body(*refs):
        ins = refs[:len(names)]
        outs = refs[len(names):2 * len(names)]
        send_ici, recv_ici, send_d2d, recv_d2d = refs[2 * len(names):]
        x, y, c = _my_place()
        me = 2 * x + y
        sibling = (x, y, 1 - c)
        chips = _other_chips(x, y)
        ici = []
        for t, name in enumerate(names):
            for j, (cx, cy) in enumerate(chips):
                cp = pltpu.make_async_remote_copy(
                    src_ref=_window(ins[t].at[c], name, me), dst_ref=_window(outs[t].at[c], name, me),
                    send_sem=send_ici.at[3 * t + j], recv_sem=recv_ici.at[3 * t + j],
                    device_id=(cx, cy, c), device_id_type=MESH)
                cp.start()
                ici.append(cp)
        fwd = []
        for t, name in enumerate(names):
            for j, (cx, cy) in enumerate(chips):
                land = _window(outs[t].at[c], name, 2 * cx + cy)
                pltpu.make_async_remote_copy(
                    src_ref=land, dst_ref=land, send_sem=send_ici.at[3 * t + j], recv_sem=recv_ici.at[3 * t + j],
                    device_id=(cx, cy, c), device_id_type=MESH).wait_recv()
                cp = pltpu.make_async_remote_copy(
                    src_ref=land, dst_ref=land, send_sem=send_d2d.at[3 * t + j], recv_sem=recv_d2d.at[3 * t + j],
                    device_id=sibling, device_id_type=MESH)
                cp.start()
                fwd.append(cp)
        for t, name in enumerate(names):
            for j, (cx, cy) in enumerate(chips):
                land = _window(outs[t].at[1 - c], name, 2 * cx + cy)
                pltpu.make_async_remote_copy(
                    src_ref=land, dst_ref=land, send_sem=send_d2d.at[3 * t + j], recv_sem=recv_d2d.at[3 * t + j],
                    device_id=sibling, device_id_type=MESH).wait_recv()
        for cp in ici + fwd:
            cp.wait_send()

    nsem = 3 * len(names)
    outs = pl.pallas_call(
        body, name="gather_weights",
        in_specs=[HBM] * len(names), out_specs=[HBM] * len(names),
        out_shape=[jax.ShapeDtypeStruct(full[n].shape, full[n].dtype) for n in names],
        input_output_aliases={t: t for t in range(len(names))},
        scratch_shapes=[pltpu.SemaphoreType.DMA((nsem,)), pltpu.SemaphoreType.DMA((nsem,)),
                        pltpu.SemaphoreType.DMA((nsem,)), pltpu.SemaphoreType.DMA((nsem,))],
    )(*[full[n] for n in names])
    return dict(zip(names, outs))


def _swap_layers(grads):
    names = list(BIG)

    def body(*refs):
        ins = refs[:len(names)]
        outs = refs[len(names):2 * len(names)]
        send_sems, recv_sems = refs[2 * len(names):]
        x, y, c = _my_place()
        sibling = (x, y, 1 - c)
        cps = []
        for t in range(len(names)):
            cp = pltpu.make_async_remote_copy(
                src_ref=ins[t].at[1 - c], dst_ref=outs[t], send_sem=send_sems.at[t], recv_sem=recv_sems.at[t],
                device_id=sibling, device_id_type=MESH)
            cp.start()
            cps.append(cp)
        for cp in cps:
            cp.wait()

    outs = pl.pallas_call(
        body, name="swap_layers", in_specs=[HBM] * len(names), out_specs=[HBM] * len(names),
        out_shape=[jax.ShapeDtypeStruct(FULL_SHAPE[n], f32) for n in names],
        scratch_shapes=[pltpu.SemaphoreType.DMA((len(names),)), pltpu.SemaphoreType.DMA((len(names),))],
    )(*[grads[n] for n in names])
    return dict(zip(names, outs))


def _chip_sum(grad, other, name):
    k, n = FULL_SHAPE[name]
    tr = min(k, 512)
    c = lax.axis_index("c")

    def body(c_ref, g_ref, o_ref, out_ref):
        out_ref[...] = (g_ref[...] + o_ref[...]).astype(out_ref.dtype)

    return pl.pallas_call(
        body, name="chip_sum_" + name,
        grid_spec=pltpu.PrefetchScalarGridSpec(
            num_scalar_prefetch=1, grid=(k // tr,),
            in_specs=[pl.BlockSpec((None, tr, n), lambda i, c_ref: (c_ref[0], i, 0)),
                      pl.BlockSpec((tr, n), lambda i, c_ref: (i, 0))],
            out_specs=pl.BlockSpec((tr, n), lambda i, c_ref: (i, 0))),
        out_shape=jax.ShapeDtypeStruct((k, n), COMM_DTYPE),
        compiler_params=_cparams(("parallel",)),
    )(jnp.reshape(c, (1,)).astype(jnp.int32), grad, other)


def _scatter_shards(sums):
    names = list(BIG)

    def body(*refs):
        ins = refs[:len(names)]
        outs = refs[len(names):2 * len(names)]
        send_sems, recv_sems = refs[2 * len(names):]
        x, y, c = _my_place()
        me = 2 * x + y
        chips = _other_chips(x, y)
        cps = []
        for t, name in enumerate(names):
            for j, (cx, cy) in enumerate(chips):
                cp = pltpu.make_async_remote_copy(
                    src_ref=_window(ins[t], name, 2 * cx + cy), dst_ref=outs[t].at[me],
                    send_sem=send_sems.at[3 * t + j], recv_sem=recv_sems.at[3 * t + j],
                    device_id=(cx, cy, c), device_id_type=MESH)
                cp.start()
                cps.append(cp)
        for t, name in enumerate(names):
            for j, (cx, cy) in enumerate(chips):
                land = outs[t].at[2 * cx + cy]
                pltpu.make_async_remote_copy(
                    src_ref=land, dst_ref=land, send_sem=send_sems.at[3 * t + j], recv_sem=recv_sems.at[3 * t + j],
                    device_id=(cx, cy, c), device_id_type=MESH).wait_recv()
        for cp in cps:
            cp.wait_send()

    nsem = 3 * len(names)
    outs = pl.pallas_call(
        body, name="scatter_shards", in_specs=[HBM] * len(names), out_specs=[HBM] * len(names),
        out_shape=[jax.ShapeDtypeStruct((N_CHIPS,) + _shard_shape(n), sums[n].dtype) for n in names],
        scratch_shapes=[pltpu.SemaphoreType.DMA((nsem,)), pltpu.SemaphoreType.DMA((nsem,))],
    )(*[sums[n] for n in names])
    return dict(zip(names, outs))


def _sum_slots(slots, own, name):
    ks, ns = _shard_shape(name)
    tr = min(ks, 256)
    shape, index = _shard_block(name, tr)

    def body(me_ref, c_ref, s_ref, own_ref, out_ref):
        me = me_ref[0]
        acc = None
        for s in range(N_CHIPS):
            term = jnp.where(me == s, own_ref[...], s_ref[s]).astype(f32)
            acc = term if acc is None else acc + term
        out_ref[...] = acc

    return pl.pallas_call(
        body, name="sum_slots_" + name,
        grid_spec=pltpu.PrefetchScalarGridSpec(
            num_scalar_prefetch=2, grid=(ks // tr,),
            in_specs=[pl.BlockSpec((N_CHIPS, tr, ns), lambda i, me, c: (0, i, 0)),
                      pl.BlockSpec(shape, lambda i, me, c: index(i, me))],
            out_specs=pl.BlockSpec((None, tr, ns), lambda i, me, c: (c[0], i, 0))),
        out_shape=jax.ShapeDtypeStruct((2, ks, ns), f32),
        compiler_params=_cparams(("parallel",)),
    )(_chip_index(), jnp.reshape(lax.axis_index("c"), (1,)).astype(jnp.int32), slots, own)


N_DEV = 8


def _reduce_copies(dws, lands, names, layer, send_sems, recv_sems):
    x, y, c = _my_place()
    me, my_dev = 2 * x + y, 4 * x + 2 * y + c
    out = []
    for t, name in enumerate(names):
        for j, (cx, cy) in enumerate(_other_chips(x, y)):
            out.append((pltpu.make_async_remote_copy(
                src_ref=_window(dws[t], name, 2 * cx + cy), dst_ref=lands[t].at[my_dev],
                send_sem=send_sems.at[4 * t + j], recv_sem=recv_sems.at[N_DEV * t + my_dev],
                device_id=(cx, cy, layer), device_id_type=MESH), False))
        out.append((pltpu.make_async_remote_copy(
            src_ref=_window(dws[t], name, me), dst_ref=lands[t].at[my_dev],
            send_sem=send_sems.at[4 * t + 3], recv_sem=recv_sems.at[N_DEV * t + my_dev],
            device_id=(x, y, layer), device_id_type=MESH), True))
    return out


def _reduce_start(dws, names, layer, tag):
    n = len(names)
    lands = [lax.empty((N_DEV,) + _shard_shape(nm), dws[0].dtype) for nm in names]

    def body(*refs):
        ins = refs[:n]
        send_sems, recv_sems = refs[2 * n], refs[2 * n + 1]
        land_out = refs[3 * n + 2:4 * n + 2]
        token = refs[-1]
        c = lax.axis_index("c")
        for cp, non_owner_only in _reduce_copies(ins, land_out, names, layer, send_sems, recv_sems):
            if non_owner_only:
                @pl.when(c != layer)
                def _():
                    cp.start()
            else:
                cp.start()
        token[...] = jnp.zeros_like(token)

    res = pl.pallas_call(
        body, name="reduce_start" + tag,
        out_shape=(pltpu.SemaphoreType.DMA((4 * n,)), pltpu.SemaphoreType.DMA((N_DEV * n,)))
        + tuple(pltpu.HBM(a.shape, a.dtype) for a in dws) + tuple(pltpu.HBM(a.shape, a.dtype) for a in lands)
        + (jax.ShapeDtypeStruct((8, 128), f32),),
        in_specs=[HBM] * (2 * n),
        out_specs=(SEM, SEM) + (HBM,) * (2 * n) + (pl.BlockSpec(memory_space=pltpu.VMEM),),
        input_output_aliases={i: i + 2 for i in range(2 * n)},
        compiler_params=pltpu.CompilerParams(has_side_effects=EFFECT),
    )(*[pltpu.with_memory_space_constraint(a, pltpu.HBM) for a in list(dws) + lands])
    return res[0], res[1], list(res[2:2 + n]), list(res[2 + n:2 + 2 * n]), res[-1]


def _reduce_wait(send_sems, recv_sems, dws, lands, names, layer, after, tag):
    n = len(names)

    def body(*refs):
        ins, land_in = refs[:n], refs[n:2 * n]
        send_ref, recv_ref = refs[2 * n], refs[2 * n + 1]
        x, y, c = _my_place()
        for cp, non_owner_only in _reduce_copies(ins, land_in, names, layer, send_ref, recv_ref):
            if non_owner_only:
                @pl.when(c != layer)
                def _():
                    cp.wait_send()
            else:
                cp.wait_send()

        @pl.when(c == layer)
        def _():
            for t in range(n):
                for k in range(1, N_DEV):
                    px, py, pc = x ^ ((k >> 2) & 1), y ^ ((k >> 1) & 1), c ^ (k & 1)
                    dev = 4 * px + 2 * py + pc
                    land = land_in[t].at[dev]
                    pltpu.make_async_remote_copy(
                        src_ref=land, dst_ref=land, send_sem=send_ref.at[4 * t], recv_sem=recv_ref.at[N_DEV * t + dev],
                        device_id=(px, py, pc), device_id_type=MESH).wait_recv()

    res = pl.pallas_call(
        body, name="reduce_wait" + tag,
        out_shape=tuple(pltpu.HBM(a.shape, a.dtype) for a in list(dws) + list(lands)),
        in_specs=[HBM] * (2 * n) + [SEM, SEM, pl.BlockSpec(memory_space=pl.ANY)], out_specs=(HBM,) * (2 * n),
        input_output_aliases={i: i for i in range(2 * n)},
        compiler_params=pltpu.CompilerParams(has_side_effects=EFFECT),
    )(*dws, *lands, send_sems, recv_sems, after)
    return list(res[:n]), list(res[n:])


def _sum_devices(land, own, name, layer, prev):
    ks, ns = _shard_shape(name)
    tr = min(ks, 256)
    shape, index = _shard_block(name, tr)

    def body(me_ref, dev_ref, *refs):
        s_ref, own_ref, out_ref = refs[0], refs[1], refs[-1]
        dev = dev_ref[0]
        acc = None
        for s in range(N_DEV):
            term = jnp.where(dev == s, own_ref[...], s_ref[s]).astype(f32)
            acc = term if acc is None else acc + term
        out_ref[...] = acc

    in_specs = [pl.BlockSpec((N_DEV, tr, ns), lambda i, me, dev: (0, i, 0)),
                pl.BlockSpec(shape, lambda i, me, dev: index(i, me))]
    args = [land, own]
    aliases = {}
    if prev is not None:
        in_specs.append(pl.BlockSpec(memory_space=pl.ANY))
        args.append(prev)
        aliases = {4: 0}
    x, y, c = _my_place()
    return pl.pallas_call(
        body, name=f"sum_devices_{name}{layer}",
        grid_spec=pltpu.PrefetchScalarGridSpec(
            num_scalar_prefetch=2, grid=(ks // tr,), in_specs=in_specs,
            out_specs=pl.BlockSpec((None, tr, ns), lambda i, me, dev: (layer, i, 0))),
        out_shape=jax.ShapeDtypeStruct((2, ks, ns), f32), input_output_aliases=aliases,
        compiler_params=_cparams(("parallel",)),
    )(_chip_index(), jnp.reshape(4 * x + 2 * y + c, (1,)).astype(jnp.int32), *args)


class _GradReducer:
    GROUPS = (("1", 1, ("w_gate", "w_ple", "w_down", "w_up", "w_out", "w_in")),
              ("0a", 0, ("w_gate", "w_ple", "w_down", "w_up")),
              ("0b", 0, ("w_out", "w_in")))

    def __init__(self):
        self.grads = {}
        self.started = {}

    def add(self, name, layer, dw):
        self.grads[(name, layer)] = dw
        token = None
        for tag, glayer, names in self.GROUPS:
            if tag not in self.started and all((nm, glayer) in self.grads for nm in names):
                *self.started[tag], token = _reduce_start([self.grads[(nm, glayer)] for nm in names], names, glayer, tag)
        return token

    def finish(self, after):
        mine = {}
        for tag, layer, names in self.GROUPS:
            send, recv, dws, lands = self.started[tag]
            dws, lands = _reduce_wait(send, recv, dws, lands, names, layer, after, tag)
            for nm, dw, land in zip(names, dws, lands):
                mine[nm] = _sum_devices(land, dw, nm, layer, mine.get(nm))
        return _pair_layers(mine)


def _pair_layers(mine):
    names = list(BIG)

    def body(*refs):
        ins = refs[:len(names)]
        outs = refs[len(names):2 * len(names)]
        send_sems, recv_sems = refs[2 * len(names):]
        x, y, c = _my_place()
        sibling = (x, y, 1 - c)
        cps = []
        for t in range(len(names)):
            cp = pltpu.make_async_remote_copy(
                src_ref=ins[t].at[c], dst_ref=outs[t].at[c], send_sem=send_sems.at[t], recv_sem=recv_sems.at[t],
                device_id=sibling, device_id_type=MESH)
            cp.start()
            cps.append(cp)
        for t in range(len(names)):
            cps[t].wait_send()
            land = outs[t].at[1 - c]
            pltpu.make_async_remote_copy(
                src_ref=land, dst_ref=land, send_sem=send_sems.at[t], recv_sem=recv_sems.at[t],
                device_id=sibling, device_id_type=MESH).wait_recv()

    outs = pl.pallas_call(
        body, name="pair_layers", in_specs=[HBM] * len(names), out_specs=[HBM] * len(names),
        out_shape=[jax.ShapeDtypeStruct((2,) + _shard_shape(n), f32) for n in names],
        input_output_aliases={t: t for t in range(len(names))},
        scratch_shapes=[pltpu.SemaphoreType.DMA((len(names),)), pltpu.SemaphoreType.DMA((len(names),))],
    )(*[mine[n] for n in names])
    return dict(zip(names, outs))


SMALL_ROWS = 320


def _allreduce_small(vec):
    n_dev = 8

    def body(v_ref, out_ref, buf_ref, send_sems, recv_sems):
        x, y, c = _my_place()
        me = 4 * x + 2 * y + c
        buf_ref[me] = v_ref[...]
        cps = []
        for k in range(1, n_dev):
            dx, dy, dc = (k >> 2) & 1, (k >> 1) & 1, k & 1
            peer = (x ^ dx, y ^ dy, c ^ dc)
            cp = pltpu.make_async_remote_copy(
                src_ref=v_ref, dst_ref=buf_ref.at[me], send_sem=send_sems.at[k - 1], recv_sem=recv_sems.at[k - 1],
                device_id=peer, device_id_type=MESH)
            cp.start()
            cps.append(cp)
        for k in range(1, n_dev):
            dx, dy, dc = (k >> 2) & 1, (k >> 1) & 1, k & 1
            src = 4 * (x ^ dx) + 2 * (y ^ dy) + (c ^ dc)
            land = buf_ref.at[src]
            pltpu.make_async_remote_copy(
                src_ref=land, dst_ref=land, send_sem=send_sems.at[k - 1], recv_sem=recv_sems.at[k - 1],
                device_id=(x ^ dx, y ^ dy, c ^ dc), device_id_type=MESH).wait_recv()
        for cp in cps:
            cp.wait_send()
        acc = buf_ref[0]
        for s in range(1, n_dev):
            acc = acc + buf_ref[s]
        out_ref[...] = acc

    return pl.pallas_call(
        body, name="allreduce_small",
        in_specs=[pl.BlockSpec(memory_space=pltpu.VMEM)], out_specs=pl.BlockSpec(memory_space=pltpu.VMEM),
        out_shape=jax.ShapeDtypeStruct((SMALL_ROWS, 128), f32),
        scratch_shapes=[pltpu.VMEM((n_dev, SMALL_ROWS, 128), f32), pltpu.SemaphoreType.DMA((n_dev - 1,)),
                        pltpu.SemaphoreType.DMA((n_dev - 1,))],
    )(vec)


def _adamw(w, g, m, v, name):
    rows, cols = w.shape
    tr = rows
    for cand in (512, 256, 128, 64, 32, 16, 8):
        if rows % cand == 0 and cand * cols * 4 <= 2 * 1024 * 1024:
            tr = cand
            break
    c1 = np.float32(1.0 - ADAM_B1 ** ADAM_STEP)
    c2 = np.float32(1.0 - ADAM_B2 ** ADAM_STEP)

    def body(w_ref, g_ref, m_ref, v_ref, go_ref, d_ref, mo_ref, vo_ref):
        gv = g_ref[...]
        go_ref[...] = gv
        mn = ADAM_B1 * m_ref[...] + (1.0 - ADAM_B1) * gv
        vn = ADAM_B2 * v_ref[...] + (1.0 - ADAM_B2) * (gv * gv)
        mo_ref[...] = mn
        vo_ref[...] = vn
        d_ref[...] = -ADAM_LR * ((mn / c1) / (jnp.sqrt(vn / c2) + ADAM_EPS) + ADAM_WD * w_ref[...])

    blk = pl.BlockSpec((tr, cols), lambda i: (i, 0))
    return pl.pallas_call(
        body, name="adamw_" + name, grid=(rows // tr,), in_specs=[blk] * 4, out_specs=[blk] * 4,
        out_shape=[jax.ShapeDtypeStruct((rows, cols), f32)] * 4,
        compiler_params=_cparams(("parallel",)),
    )(w, g, m, v)


SMALL = ("norm1", "pool_w", "pool_scale", "norm2", "norm3", "final_norm")
ORDER = ("norm1", "w_in", "pool_w", "pool_scale", "w_out", "norm2", "w_up", "w_down", "norm3", "w_gate", "w_ple",
         "final_norm")


def _pack_small(tree, extra=None):
    parts = [tree[n].reshape(-1) for n in SMALL]
    if extra is not None:
        parts.append(extra.reshape(-1))
    flat = jnp.concatenate(parts)
    return jnp.pad(flat, (0, SMALL_ROWS * 128 - flat.shape[0])).reshape(SMALL_ROWS, 128)


def _unpack_small(packed, like):
    flat = packed.reshape(-1)
    out, off = {}, 0
    for n in SMALL:
        size = int(np.prod(like[n].shape))
        out[n] = flat[off:off + size].reshape(like[n].shape)
        off += size
    return out, flat[off]


def kernel(x, p, positions, norm1, w_in, pool_w, pool_scale, w_out, norm2, w_up, w_down, norm3, w_gate, w_ple, final_norm, loss_target, m_norm1, m_w_in, m_pool_w, m_pool_scale, m_w_out, m_norm2, m_w_up, m_w_down, m_norm3, m_w_gate, m_w_ple, m_final_norm, v_norm1, v_w_in, v_pool_w, v_pool_scale, v_w_out, v_norm2, v_w_up, v_w_down, v_norm3, v_w_gate, v_w_ple, v_final_norm):
    w = dict(norm1=norm1, w_in=w_in, pool_w=pool_w, pool_scale=pool_scale, w_out=w_out, norm2=norm2, w_up=w_up,
             w_down=w_down, norm3=norm3, w_gate=w_gate, w_ple=w_ple, final_norm=final_norm)
    m = dict(norm1=m_norm1, w_in=m_w_in, pool_w=m_pool_w, pool_scale=m_pool_scale, w_out=m_w_out, norm2=m_norm2,
             w_up=m_w_up, w_down=m_w_down, norm3=m_norm3, w_gate=m_w_gate, w_ple=m_w_ple, final_norm=m_final_norm)
    v = dict(norm1=v_norm1, w_in=v_w_in, pool_w=v_pool_w, pool_scale=v_pool_scale, w_out=v_w_out, norm2=v_norm2,
             w_up=v_w_up, w_down=v_w_down, norm3=v_norm3, w_gate=v_w_gate, w_ple=v_w_ple, final_norm=v_final_norm)
    small = {n: w[n] for n in SMALL}

    wsrc = _GatheredWeights({n: w[n] for n in BIG})
    reducer = _GradReducer()
    loss8, dx, small_grads = _local_step(x[0], p[:, 0], positions[0], wsrc, small, loss_target[0], reducer)
    gsh = reducer.finish(dx)

    red = _allreduce_small(_pack_small(small_grads, loss8[0, 0]))
    g_small, loss = _unpack_small(red, small)

    g_out, d_out, m_out, v_out = {}, {}, {}, {}
    for n in BIG:
        shp = w[n].shape
        two = lambda a: a.reshape(shp[0] * shp[1], shp[2])
        g2, d2, m2, v2 = _adamw(two(w[n]), two(gsh[n]), two(m[n]), two(v[n]), n)
        g_out[n], d_out[n], m_out[n], v_out[n] = g2.reshape(shp), d2.reshape(shp), m2.reshape(shp), v2.reshape(shp)
    _, d2, m2, v2 = _adamw(_pack_small(small), red, _pack_small({n: m[n] for n in SMALL}),
                           _pack_small({n: v[n] for n in SMALL}), "small")
    for tree, packed in ((d_out, d2), (m_out, m2), (v_out, v2)):
        tree.update(_unpack_small(packed, small)[0])
    g_out.update(g_small)

    return (loss, dx[None], *[g_out[n] for n in ORDER], *[d_out[n] for n in ORDER], *[m_out[n] for n in ORDER],
            *[v_out[n] for n in ORDER])
```

```python
import functools

import jax
import jax.numpy as jnp
import numpy as np
from jax import lax
from jax.experimental import pallas as pl
from jax.experimental.pallas import tpu as pltpu

f32 = jnp.float32
MXU_DTYPE = jnp.bfloat16
COMM_DTYPE = jnp.bfloat16

D_MODEL = 1024
POOL_WIDTH = 256
POOL_GC = 64
ATTN_WIDTH = 768
HEAD_DIM = 64
N_IN = POOL_WIDTH + 3 * ATTN_WIDTH
D_FF = 4096
PLE_DIM = 256
BLK = 128
DILATIONS = (1, 4, 16)
ROT_DIM = 16
ROPE_THETA = 500000.0
EPS = 1e-6
ATTN_SCALE = HEAD_DIM ** -0.5
NEG_BIG = -1e30

ADAM_LR, ADAM_B1, ADAM_B2, ADAM_EPS, ADAM_WD, ADAM_STEP = 0.001, 0.9, 0.999, 1e-08, 0.01, 10

TM = 512
TM_WGRAD = 1024
HALO = 16
VMEM_LIMIT = 48 * 1024 * 1024
N_CHIPS = 4
MESH = pl.DeviceIdType.MESH

BIG = ("w_in", "w_out", "w_up", "w_down", "w_gate", "w_ple")
FULL_SHAPE = {"w_in": (D_MODEL, N_IN), "w_out": (D_MODEL, D_MODEL), "w_up": (D_MODEL, D_FF),
              "w_down": (D_FF, D_MODEL), "w_gate": (D_MODEL, D_MODEL), "w_ple": (PLE_DIM, D_MODEL)}
COL_SHARDED = {"w_in": True, "w_out": False, "w_up": True, "w_down": False, "w_gate": False, "w_ple": True}


def _shard_shape(name):
    k, n = FULL_SHAPE[name]
    return (k, n // N_CHIPS) if COL_SHARDED[name] else (k // N_CHIPS, n)


def _cparams(sem=None, vmem=VMEM_LIMIT):
    return pltpu.CompilerParams(dimension_semantics=sem, vmem_limit_bytes=vmem)


def _resident(block_shape, index_map):
    return pl.BlockSpec(block_shape, index_map, pipeline_mode=pl.Buffered(1))


def _mx(x):
    return x.astype(MXU_DTYPE)


def _dot(a, b):
    return jnp.dot(a, b, preferred_element_type=f32)


def _dot_nt(a, b):
    return lax.dot_general(a, b, (((1,), (1,)), ((), ())), preferred_element_type=f32)


def _dot_tn(a, b):
    return lax.dot_general(a, b, (((0,), (0,)), ((), ())), preferred_element_type=f32)


def _sigmoid(x):
    return 1.0 / (1.0 + jnp.exp(-x))


def _rope_apply(y, c, s1, s2, width):
    return y * c + pltpu.roll(y, width - 8, axis=1) * s1 + pltpu.roll(y, 8, axis=1) * s2


def _rope_transpose(dy, c, s1, s2, width):
    return dy * c + pltpu.roll(dy * s1, 8, axis=1) + pltpu.roll(dy * s2, width - 8, axis=1)


def _norm_matmul(h, g, w, layer, tn, name, rope=None):
    s_len, d = h.shape
    n = w.shape[2]

    def body(*refs):
        if rope is None:
            h_ref, g_ref, w_ref, y_ref, hn_ref = refs
        else:
            h_ref, g_ref, w_ref, c_ref, s1_ref, s2_ref, y_ref, hn_ref = refs
            reps = tn // 128
            c = jnp.concatenate([c_ref[...]] * reps, axis=1)
            s1 = jnp.concatenate([s1_ref[...]] * reps, axis=1)
            s2 = jnp.concatenate([s2_ref[...]] * reps, axis=1)
        x = h_ref[...]
        r = lax.rsqrt(jnp.mean(x * x, axis=-1, keepdims=True) + EPS)
        hn = ((x * r) * g_ref[...]).astype(hn_ref.dtype)
        hn_ref[...] = hn
        for j in range(n // tn):
            y = _dot(hn, w_ref[:, j * tn:(j + 1) * tn])
            if rope is not None and POOL_WIDTH <= j * tn < POOL_WIDTH + 2 * ATTN_WIDTH:
                y = _rope_apply(y, c, s1, s2, tn)
            y_ref[:, j * tn:(j + 1) * tn] = y

    in_specs = [pl.BlockSpec((TM, d), lambda i: (i, 0)),
                pl.BlockSpec((1, d), lambda i: (0, 0)),
                _resident((None, d, n), lambda i: (layer, 0, 0))]
    args = [h, g, w]
    if rope is not None:
        assert POOL_WIDTH % tn == 0 and (2 * ATTN_WIDTH) % tn == 0
        in_specs += [pl.BlockSpec((TM, 128), lambda i: (i, 0))] * 3
        args += list(rope)
    return pl.pallas_call(
        body, name=name, grid=(s_len // TM,), in_specs=in_specs,
        out_specs=[pl.BlockSpec((TM, n), lambda i: (i, 0)), pl.BlockSpec((TM, d), lambda i: (i, 0))],
        out_shape=[jax.ShapeDtypeStruct((s_len, n), f32), jax.ShapeDtypeStruct((s_len, d), MXU_DTYPE)],
        compiler_params=_cparams(("parallel",)),
    )(*args)


def _matmul_residual(a, w, layer, res, name, act=False, tk=1024):
    s_len, k_dim = a.shape
    n = w.shape[2]

    def body(a_ref, w_ref, res_ref, o_ref):
        acc = res_ref[...]
        for k in range(k_dim // tk):
            x = a_ref[:, k * tk:(k + 1) * tk]
            if act:
                r = jnp.maximum(x, 0.0)
                x = r * r
            acc = acc + _dot(_mx(x), w_ref[k * tk:(k + 1) * tk, :])
        o_ref[...] = acc

    return pl.pallas_call(
        body, name=name, grid=(s_len // TM,),
        in_specs=[pl.BlockSpec((TM, k_dim), lambda i: (i, 0)),
                  _resident((None, k_dim, n), lambda i: (layer, 0, 0)),
                  pl.BlockSpec((TM, n), lambda i: (i, 0))],
        out_specs=pl.BlockSpec((TM, n), lambda i: (i, 0)),
        out_shape=jax.ShapeDtypeStruct((s_len, n), f32),
        compiler_params=_cparams(("parallel",)),
    )(a, w, res)


def _gate_ple_fwd(h2, g, w_gate, w_ple, layer, p, name):
    s_len, d = h2.shape

    def body(h_ref, g_ref, wg_ref, p_ref, wp_ref, h3_ref, gl_ref, hn_ref):
        x = h_ref[...]
        r = lax.rsqrt(jnp.mean(x * x, axis=-1, keepdims=True) + EPS)
        hn = ((x * r) * g_ref[...]).astype(hn_ref.dtype)
        hn_ref[...] = hn
        gl = _dot(hn, wg_ref[...])
        gl_ref[...] = gl
        e = _dot(_mx(p_ref[...]), wp_ref[...])
        h3_ref[...] = x + _sigmoid(gl) * e

    row = lambda i: (i, 0)
    return pl.pallas_call(
        body, name=name, grid=(s_len // TM,),
        in_specs=[pl.BlockSpec((TM, d), row), pl.BlockSpec((1, d), lambda i: (0, 0)),
                  pl.BlockSpec((None, d, d), lambda i: (layer, 0, 0)), pl.BlockSpec((TM, PLE_DIM), row),
                  pl.BlockSpec((None, PLE_DIM, d), lambda i: (layer, 0, 0))],
        out_specs=[pl.BlockSpec((TM, d), row)] * 3,
        out_shape=[jax.ShapeDtypeStruct((s_len, d), f32), jax.ShapeDtypeStruct((s_len, d), f32),
                   jax.ShapeDtypeStruct((s_len, d), MXU_DTYPE)],
        compiler_params=_cparams(("parallel",)),
    )(h2, g, w_gate, p, w_ple)


def _gate_ple_bwd(dh3, gl, p, w_ple, layer, name):
    s_len, d = dh3.shape

    def body(dh_ref, gl_ref, p_ref, wp_ref, de_ref, dgl_ref):
        dh = dh_ref[...]
        gate = _sigmoid(gl_ref[...])
        e = _dot(_mx(p_ref[...]), wp_ref[...])
        de_ref[...] = (dh * gate).astype(de_ref.dtype)
        dgl_ref[...] = ((dh * e) * (gate * (1.0 - gate))).astype(dgl_ref.dtype)

    row = lambda i: (i, 0)
    return pl.pallas_call(
        body, name=name, grid=(s_len // TM,),
        in_specs=[pl.BlockSpec((TM, d), row), pl.BlockSpec((TM, d), row), pl.BlockSpec((TM, PLE_DIM), row),
                  pl.BlockSpec((None, PLE_DIM, d), lambda i: (layer, 0, 0))],
        out_specs=[pl.BlockSpec((TM, d), row)] * 2,
        out_shape=[jax.ShapeDtypeStruct((s_len, d), MXU_DTYPE)] * 2,
        compiler_params=_cparams(("parallel",)),
    )(dh3, gl, p, w_ple)


def _gate_bwd(dh3, gl, p, w_ple, w_gate, layer, h2, g, name):
    s_len, d = dh3.shape

    def body(dh_ref, gl_ref, p_ref, wp_ref, wg_ref, h_ref, g_ref, dh2_ref, dg_ref, de_ref, dgl_ref):
        i = pl.program_id(0)
        dh = dh_ref[...]
        gate = _sigmoid(gl_ref[...])
        e = _dot(_mx(p_ref[...]), wp_ref[...])
        de_ref[...] = (dh * gate).astype(de_ref.dtype)
        dgl = ((dh * e) * (gate * (1.0 - gate))).astype(dgl_ref.dtype)
        dgl_ref[...] = dgl
        dx, dgrow = _rmsnorm_bwd(_dot_nt(dgl, wg_ref[...]), h_ref[...], g_ref[...])
        dh2_ref[...] = dh + dx
        dgsum = jnp.sum(dgrow, axis=0, keepdims=True)

        @pl.when(i == 0)
        def _():
            dg_ref[...] = dgsum

        @pl.when(i > 0)
        def _():
            dg_ref[...] += dgsum

    row = lambda i: (i, 0)
    blk = pl.BlockSpec((TM, d), row)
    return pl.pallas_call(
        body, name=name, grid=(s_len // TM,),
        in_specs=[blk, blk, pl.BlockSpec((TM, PLE_DIM), row), _resident((None, PLE_DIM, d), lambda i: (layer, 0, 0)),
                  _resident((None, d, d), lambda i: (layer, 0, 0)), blk, pl.BlockSpec((1, d), lambda i: (0, 0))],
        out_specs=[blk, pl.BlockSpec((1, d), lambda i: (0, 0)), blk, blk],
        out_shape=[jax.ShapeDtypeStruct((s_len, d), f32), jax.ShapeDtypeStruct((1, d), f32),
                   jax.ShapeDtypeStruct((s_len, d), MXU_DTYPE), jax.ShapeDtypeStruct((s_len, d), MXU_DTYPE)],
        compiler_params=_cparams(("arbitrary",)),
    )(dh3, gl, p, w_ple, w_gate, h2, g)


def _rmsnorm_bwd(dhn, x, g):
    r = lax.rsqrt(jnp.mean(x * x, axis=-1, keepdims=True) + EPS)
    xh = x * r
    dxh = dhn * g
    dx = r * (dxh - xh * jnp.mean(dxh * xh, axis=-1, keepdims=True))
    return dx, dhn * xh


def _matmul_nt_norm_bwd(dy, w, layer, h_prev, g, dres, name, tk=1024, after=None):
    s_len, k_dim = dy.shape
    d = h_prev.shape[1]

    def body(dy_ref, w_ref, h_ref, g_ref, dres_ref, *rest):
        dh_ref, dg_ref = rest[-2:]
        i = pl.program_id(0)
        acc = None
        for k in range(k_dim // tk):
            part = _dot_nt(_mx(dy_ref[:, k * tk:(k + 1) * tk]), w_ref[:, k * tk:(k + 1) * tk])
            acc = part if acc is None else acc + part
        dx, dgrow = _rmsnorm_bwd(acc, h_ref[...], g_ref[...])
        dh_ref[...] = dres_ref[...] + dx
        dgsum = jnp.sum(dgrow, axis=0, keepdims=True)

        @pl.when(i == 0)
        def _():
            dg_ref[...] = dgsum

        @pl.when(i > 0)
        def _():
            dg_ref[...] += dgsum

    in_specs = [pl.BlockSpec((TM, k_dim), lambda i: (i, 0)),
                _resident((None, d, k_dim), lambda i: (layer, 0, 0)),
                pl.BlockSpec((TM, d), lambda i: (i, 0)),
                pl.BlockSpec((1, d), lambda i: (0, 0)),
                pl.BlockSpec((TM, d), lambda i: (i, 0))]
    args = [dy, w, h_prev, g, dres]
    if after is not None:
        in_specs.append(pl.BlockSpec(memory_space=pl.ANY))
        args.append(after)
    return pl.pallas_call(
        body, name=name, grid=(s_len // TM,), in_specs=in_specs,
        out_specs=[pl.BlockSpec((TM, d), lambda i: (i, 0)), pl.BlockSpec((1, d), lambda i: (0, 0))],
        out_shape=[jax.ShapeDtypeStruct((s_len, d), f32), jax.ShapeDtypeStruct((1, d), f32)],
        compiler_params=_cparams(("arbitrary",)),
    )(*args)


def _mlp_fwd(h1, g, w_up, w_down, layer, name, tf=1024):
    s_len, d = h1.shape
    ff = w_up.shape[2]

    def body(h_ref, g_ref, wu_ref, wd_ref, h2_ref, a_ref, hn_ref):
        x = h_ref[...]
        r = lax.rsqrt(jnp.mean(x * x, axis=-1, keepdims=True) + EPS)
        hn = ((x * r) * g_ref[...]).astype(hn_ref.dtype)
        hn_ref[...] = hn
        acc = x
        for j in range(ff // tf):
            a = _dot(hn, wu_ref[:, j * tf:(j + 1) * tf])
            a_ref[:, j * tf:(j + 1) * tf] = a.astype(a_ref.dtype)
            relu = jnp.maximum(a, 0.0)
            acc = acc + _dot(_mx(relu * relu), wd_ref[j * tf:(j + 1) * tf, :])
        h2_ref[...] = acc

    row = lambda i: (i, 0)
    return pl.pallas_call(
        body, name=name, grid=(s_len // TM,),
        in_specs=[pl.BlockSpec((TM, d), row), pl.BlockSpec((1, d), lambda i: (0, 0)),
                  _resident((None, d, ff), lambda i: (layer, 0, 0)), _resident((None, ff, d), lambda i: (layer, 0, 0))],
        out_specs=[pl.BlockSpec((TM, d), row), pl.BlockSpec((TM, ff), row), pl.BlockSpec((TM, d), row)],
        out_shape=[jax.ShapeDtypeStruct((s_len, d), f32), jax.ShapeDtypeStruct((s_len, ff), MXU_DTYPE),
                   jax.ShapeDtypeStruct((s_len, d), MXU_DTYPE)],
        compiler_params=_cparams(("parallel",)),
    )(h1, g, w_up, w_down)


def _down_bwd(dh2, w_down, layer, a, name, tf=1024):
    s_len, d = dh2.shape
    ff = a.shape[1]

    def body(dh_ref, w_ref, a_ref, da_ref):
        dhb = _mx(dh_ref[...])
        for j in range(ff // tf):
            cols = slice(j * tf, (j + 1) * tf)
            dact = _dot_nt(dhb, w_ref[cols, :])
            da_ref[:, cols] = (dact * (2.0 * jnp.maximum(a_ref[:, cols].astype(f32), 0.0))).astype(da_ref.dtype)

    return pl.pallas_call(
        body, name=name, grid=(s_len // TM,),
        in_specs=[pl.BlockSpec((TM, d), lambda i: (i, 0)),
                  _resident((None, ff, d), lambda i: (layer, 0, 0)),
                  pl.BlockSpec((TM, ff), lambda i: (i, 0))],
        out_specs=pl.BlockSpec((TM, ff), lambda i: (i, 0)),
        out_shape=jax.ShapeDtypeStruct((s_len, ff), MXU_DTYPE),
        compiler_params=_cparams(("parallel",)),
    )(dh2, w_down, a)


def _matmul_nt(dy, w, layer, name):
    s_len, n = dy.shape
    k_dim = w.shape[1]

    def body(dy_ref, w_ref, o_ref):
        o_ref[...] = _dot_nt(_mx(dy_ref[...]), w_ref[...])

    return pl.pallas_call(
        body, name=name, grid=(s_len // TM,),
        in_specs=[pl.BlockSpec((TM, n), lambda i: (i, 0)), pl.BlockSpec((None, k_dim, n), lambda i: (layer, 0, 0))],
        out_specs=pl.BlockSpec((TM, k_dim), lambda i: (i, 0)),
        out_shape=jax.ShapeDtypeStruct((s_len, k_dim), f32),
        compiler_params=_cparams(("parallel",)),
    )(dy, w)


def _weight_grad(a, b, name, act=False):
    s_len, k_dim = a.shape
    n = b.shape[1]
    tka = min(k_dim, 2048)
    tnb = n if n <= 1024 else (2048 if n % 2048 == 0 else 640)
    ns = s_len // TM_WGRAD

    def body(a_ref, b_ref, o_ref, acc_ref):
        s = pl.program_id(2)
        x = a_ref[...]
        if act:
            relu = jnp.maximum(x.astype(f32), 0.0)
            x = relu * relu
        part = _dot_tn(_mx(x), _mx(b_ref[...]))

        @pl.when(s == 0)
        def _():
            acc_ref[...] = part

        @pl.when(s > 0)
        def _():
            acc_ref[...] += part

        @pl.when(s == ns - 1)
        def _():
            o_ref[...] = acc_ref[...].astype(o_ref.dtype)

    return pl.pallas_call(
        body, name=name, grid=(k_dim // tka, n // tnb, ns),
        in_specs=[pl.BlockSpec((TM_WGRAD, tka), lambda i, j, s: (s, i)),
                  pl.BlockSpec((TM_WGRAD, tnb), lambda i, j, s: (s, j))],
        out_specs=pl.BlockSpec((tka, tnb), lambda i, j, s: (i, j)),
        out_shape=jax.ShapeDtypeStruct((k_dim, n), COMM_DTYPE),
        scratch_shapes=[pltpu.VMEM((tka, tnb), f32)],
        compiler_params=_cparams(("parallel", "parallel", "arbitrary")),
    )(a, b)


def _group_select(lane, x2, x4, x8, x16):
    grp = lane // POOL_GC
    return jnp.where(grp == 0, x2, jnp.where(grp == 1, x4, jnp.where(grp == 2, x8, x16)))


def _pool_window(lane):
    grp = lane // POOL_GC
    return jnp.where(grp == 0, 2, jnp.where(grp == 1, 4, jnp.where(grp == 2, 8, 16)))


def _pool_y(u, halo, i):
    xs = jnp.concatenate([jnp.where(i > 0, halo, 0.0), u], axis=0)
    s2 = xs + pltpu.roll(xs, 1, axis=0)
    s4 = s2 + pltpu.roll(s2, 2, axis=0)
    s8 = s4 + pltpu.roll(s4, 4, axis=0)
    s16 = s8 + pltpu.roll(s8, 8, axis=0)
    lane = lax.broadcasted_iota(jnp.int32, xs.shape, 1)
    sel = _group_select(lane, s2, s4, s8, s16)[HALO:, :]
    t = i * TM + lax.broadcasted_iota(jnp.int32, u.shape, 0)
    cnt = jnp.minimum(_pool_window(lax.broadcasted_iota(jnp.int32, u.shape, 1)), t + 1).astype(f32)
    return sel / cnt - u


def _group_weights(l0, l1, l2):
    mx = jnp.maximum(jnp.maximum(l0, l1), l2)
    e0, e1, e2 = jnp.exp(l0 - mx), jnp.exp(l1 - mx), jnp.exp(l2 - mx)
    den = e0 + e1 + e2
    return e0 / den, e1 / den, e2 / den


def _mixer_merge(z, wbd, scale, outs, lses, name):
    s_len = z.shape[0]

    def body(u_ref, halo_ref, wbd_ref, sc_ref, o0, o1, o2, l0, l1, l2, m_ref):
        i = pl.program_id(0)
        y = _pool_y(u_ref[...], halo_ref[...], i)
        pool = _dot(_mx(y), wbd_ref[...]) * sc_ref[...]
        w0, w1, w2 = _group_weights(l0[...], l1[...], l2[...])
        m_ref[...] = jnp.concatenate([pool, o0[...] * w0, o1[...] * w1, o2[...] * w2], axis=1).astype(m_ref.dtype)

    row = lambda i: (i, 0)
    blk = pl.BlockSpec((TM, 256), row)
    grp = [pl.BlockSpec((TM, 256), lambda i, g=g: (i, g)) for g in range(3)]
    return pl.pallas_call(
        body, name=name, grid=(s_len // TM,),
        in_specs=[blk, pl.BlockSpec((HALO, 256), lambda i: (jnp.maximum(i * (TM // HALO) - 1, 0), 0)),
                  pl.BlockSpec((256, 256), lambda i: (0, 0)), pl.BlockSpec((1, 256), lambda i: (0, 0))] + grp + grp,
        out_specs=pl.BlockSpec((TM, D_MODEL), row),
        out_shape=jax.ShapeDtypeStruct((s_len, D_MODEL), MXU_DTYPE),
        compiler_params=_cparams(("parallel",)),
    )(z, z, wbd, scale, outs, outs, outs, lses, lses, lses)


def _head_sums(x):
    r = lax.broadcasted_iota(jnp.int32, (256, 256), 0) // HEAD_DIM
    c = lax.broadcasted_iota(jnp.int32, (256, 256), 1) // HEAD_DIM
    ones = jnp.where(r == c, 1.0, 0.0).astype(jnp.bfloat16)
    hi = x.astype(jnp.bfloat16)
    lo = (x - hi.astype(f32)).astype(jnp.bfloat16)
    return _dot(hi, ones) + _dot(lo, ones)


def _combine_bwd(dm, outs, lses, name):
    s_len = dm.shape[0]

    def body(d0, d1, d2, o0, o1, o2, l0, l1, l2, do_ref, dl_ref):
        w = _group_weights(l0[...], l1[...], l2[...])
        da = (d0[...], d1[...], d2[...])
        o = (o0[...], o1[...], o2[...])
        dw = [_head_sums(da[g] * o[g]) for g in range(3)]
        t = w[0] * dw[0] + w[1] * dw[1] + w[2] * dw[2]
        do_ref[...] = jnp.concatenate([da[g] * w[g] for g in range(3)], axis=1)
        dl_ref[...] = jnp.concatenate([w[g] * t for g in range(3)], axis=1)

    grp = [pl.BlockSpec((TM, 256), lambda i, g=g: (i, g)) for g in range(3)]
    return pl.pallas_call(
        body, name=name, grid=(s_len // TM,),
        in_specs=[pl.BlockSpec((TM, 256), lambda i: (i, 1)), pl.BlockSpec((TM, 256), lambda i: (i, 2)),
                  pl.BlockSpec((TM, 256), lambda i: (i, 3))] + grp + grp,
        out_specs=[pl.BlockSpec((TM, ATTN_WIDTH), lambda i: (i, 0))] * 2,
        out_shape=[jax.ShapeDtypeStruct((s_len, ATTN_WIDTH), f32)] * 2,
        compiler_params=_cparams(("parallel",)),
    )(dm, dm, dm, outs, outs, outs, lses, lses, lses)


def _pool_bwd(z, dm, wbd, scale, name):
    s_len = z.shape[0]
    n_halo = s_len // HALO

    def body(u_ref, uh_ref, d_ref, dh_ref, wbd_ref, sc_ref, du_ref, dw_ref, dsc_ref):
        i = pl.program_id(0)
        last = pl.num_programs(0) - 1
        y = _pool_y(u_ref[...], uh_ref[...], i)
        yb = _mx(y)
        dpo = d_ref[...]
        sc = sc_ref[...]
        dsc = jnp.sum(dpo * _dot(yb, wbd_ref[...]), axis=0, keepdims=True)
        dwp = _dot_tn(yb, _mx(dpo * sc))

        @pl.when(i == 0)
        def _():
            dsc_ref[...] = dsc
            dw_ref[...] = dwp

        @pl.when(i > 0)
        def _():
            dsc_ref[...] += dsc
            dw_ref[...] += dwp

        ext = jnp.concatenate([dpo, jnp.where(i < last, dh_ref[...], 0.0)], axis=0)
        dy = _dot_nt(_mx(ext * sc), wbd_ref[...])
        t = i * TM + lax.broadcasted_iota(jnp.int32, ext.shape, 0)
        lane = lax.broadcasted_iota(jnp.int32, ext.shape, 1)
        e = dy / jnp.minimum(_pool_window(lane), t + 1).astype(f32)
        rows = ext.shape[0]
        f2 = e + pltpu.roll(e, rows - 1, axis=0)
        f4 = f2 + pltpu.roll(f2, rows - 2, axis=0)
        f8 = f4 + pltpu.roll(f4, rows - 4, axis=0)
        f16 = f8 + pltpu.roll(f8, rows - 8, axis=0)
        du_ref[...] = (_group_select(lane, f2, f4, f8, f16) - dy)[:TM, :].astype(du_ref.dtype)

    row = lambda i: (i, 0)
    blk = pl.BlockSpec((TM, 256), row)
    return pl.pallas_call(
        body, name=name, grid=(s_len // TM,),
        in_specs=[blk, pl.BlockSpec((HALO, 256), lambda i: (jnp.maximum(i * (TM // HALO) - 1, 0), 0)),
                  blk, pl.BlockSpec((HALO, 256), lambda i: (jnp.minimum((i + 1) * (TM // HALO), n_halo - 1), 0)),
                  pl.BlockSpec((256, 256), lambda i: (0, 0)), pl.BlockSpec((1, 256), lambda i: (0, 0))],
        out_specs=[blk, pl.BlockSpec((256, 256), lambda i: (0, 0)), pl.BlockSpec((1, 256), lambda i: (0, 0))],
        out_shape=[jax.ShapeDtypeStruct((s_len, N_IN), MXU_DTYPE), jax.ShapeDtypeStruct((256, 256), f32),
                   jax.ShapeDtypeStruct((1, 256), f32)],
        compiler_params=_cparams(("arbitrary",)),
    )(z, z, dm, dm, wbd, scale)


def _to_strided(x, dil):
    if dil == 1:
        return x
    s_len, c = x.shape
    return x.reshape(s_len // (BLK * dil), BLK, dil, c).transpose(0, 2, 1, 3).reshape(s_len, c)


def _from_strided(x, dil):
    if dil == 1:
        return x
    s_len, c = x.shape
    return x.reshape(s_len // (BLK * dil), dil, BLK, c).transpose(0, 2, 1, 3).reshape(s_len, c)


def _tri_masks():
    qi = lax.broadcasted_iota(jnp.int32, (BLK, BLK), 0)
    ki = lax.broadcasted_iota(jnp.int32, (BLK, BLK), 1)
    return qi >= ki, ki >= qi


ATTN_SUPER_PER_STEP = (8, 2, 1)
Q_COL, K_COL, V_COL = POOL_WIDTH // 128, (POOL_WIDTH + ATTN_WIDTH) // 128, (POOL_WIDTH + 2 * ATTN_WIDTH) // 128


def _rows(ref, start, dil):
    if dil == 1:
        return ref[pl.ds(start, BLK), :]
    return ref[pl.ds(start, BLK, stride=dil), :]


RESIDUE_UNROLL = 4


def _for_residues(dil, fn, loop=True):
    if dil <= RESIDUE_UNROLL or not loop:
        for r in range(dil):
            fn(r, 0)
    else:
        lax.fori_loop(0, dil, fn, 0, unroll=RESIDUE_UNROLL)


def _set_rows(ref, start, dil, val):
    if dil == 1:
        ref[pl.ds(start, BLK), :] = val
    else:
        ref[pl.ds(start, BLK, stride=dil), :] = val


def _attn_fwd(z, g, prev, name):
    s_len = z.shape[0]
    dil, m = DILATIONS[g], ATTN_SUPER_PER_STEP[g]
    sbr = BLK * dil
    rows = sbr * m

    def body(*refs):
        q_ref, kc_ref, kp_ref, vc_ref, vp_ref = refs[:5]
        o_ref, l_ref = refs[-2:]
        st = pl.program_id(0)
        low, up = _tri_masks()
        head0 = lax.broadcasted_iota(jnp.int32, (BLK, 128), 1) < HEAD_DIM
        for sb in range(m):
            valid = jnp.concatenate([up & (st > 0) if sb == 0 else up, low], axis=1)

            def one_residue(r, carry, sb=sb, valid=valid):
                base = sb * sbr + r
                q = _rows(q_ref, base, dil)
                kc, vc = _rows(kc_ref, base, dil), _rows(vc_ref, base, dil)
                if sb == 0:
                    kp, vp = _rows(kp_ref, r, dil), _rows(vp_ref, r, dil)
                else:
                    kp, vp = _rows(kc_ref, base - sbr, dil), _rows(vc_ref, base - sbr, dil)
                k2 = jnp.concatenate([_mx(kp), _mx(kc)], axis=0)
                v2 = jnp.concatenate([_mx(vp), _mx(vc)], axis=0)
                outs, lses = [], []
                for hh in range(2):
                    s = jnp.where(valid, _dot_nt(_mx(jnp.where(head0 == (hh == 0), q, 0.0)), k2) * ATTN_SCALE, NEG_BIG)
                    mx = jnp.max(s, axis=-1, keepdims=True)
                    e = jnp.exp(s - mx)
                    l = jnp.sum(e, axis=-1, keepdims=True)
                    outs.append(_dot(_mx(e / l), v2))
                    lses.append(jnp.broadcast_to(mx + jnp.log(l), (BLK, 128)))
                _set_rows(o_ref, base, dil, jnp.where(head0, outs[0], outs[1]))
                _set_rows(l_ref, base, dil, jnp.where(head0, lses[0], lses[1]))
                return carry

            _for_residues(dil, one_residue, loop=False)

    def cur(col):
        return pl.BlockSpec((rows, 128), lambda st, hp: (st, col + 2 * g + hp))

    def before(col):
        return pl.BlockSpec((sbr, 128), lambda st, hp: (jnp.maximum(st * m - 1, 0), col + 2 * g + hp))

    in_specs = [cur(Q_COL), cur(K_COL), before(K_COL), cur(V_COL), before(V_COL)]
    args = [z, z, z, z, z]
    aliases = {}
    if prev is not None:
        in_specs += [pl.BlockSpec(memory_space=pl.ANY)] * 2
        args += list(prev)
        aliases = {5: 0, 6: 1}
    return pl.pallas_call(
        body, name=name, grid=(s_len // rows, 2), in_specs=in_specs, out_specs=[cur(0), cur(0)],
        out_shape=[jax.ShapeDtypeStruct((s_len, ATTN_WIDTH), f32)] * 2, input_output_aliases=aliases,
        compiler_params=_cparams(("parallel", "parallel")),
    )(*args)


def _stack_heads(x, head0):
    return jnp.concatenate([_mx(jnp.where(head0, x, 0.0)), _mx(jnp.where(head0, 0.0, x))], axis=0)


def _head_rows(x):
    xt = x.T
    return jnp.concatenate([jnp.broadcast_to(xt[0:1, :], (BLK, BLK)),
                            jnp.broadcast_to(xt[HEAD_DIM:HEAD_DIM + 1, :], (BLK, BLK))], axis=0)


def _attn_bwd(z, do, lse, dlt, tabs, dz, g, name):
    s_len = z.shape[0]
    dil, m = DILATIONS[g], ATTN_SUPER_PER_STEP[g]
    sbr = BLK * dil
    rows = sbr * m
    nsteps = s_len // rows

    def body(q_ref, qn_ref, kc_ref, kp_ref, vc_ref, vp_ref, do_ref, don_ref, l_ref, ln_ref, d_ref, dn_ref,
             c_ref, s1_ref, s2_ref, dz_in, dz_ref, dq_buf, dk_buf, dv_buf, out_buf, sems):
        del dz_in
        st, hp = pl.program_id(0), pl.program_id(1)
        head0 = lax.broadcasted_iota(jnp.int32, (BLK, 128), 1) < HEAD_DIM
        key_i = lax.broadcasted_iota(jnp.int32, (2 * BLK, BLK), 0) & (BLK - 1)
        query_i = lax.broadcasted_iota(jnp.int32, (2 * BLK, BLK), 1)
        same_t, cross_t = query_i >= key_i, key_i >= query_i
        for sb in range(m):
            prev_t = cross_t & (st > 0) if sb == 0 else cross_t
            next_t = cross_t & (st < nsteps - 1) if sb == m - 1 else cross_t

            def one_residue(r, carry, sb=sb, prev_t=prev_t, next_t=next_t):
                base = sb * sbr + r
                q, k, v = _rows(q_ref, base, dil), _rows(kc_ref, base, dil), _rows(vc_ref, base, dil)
                do_c, l_c, d_c = _rows(do_ref, base, dil), _rows(l_ref, base, dil), _rows(d_ref, base, dil)
                if sb == 0:
                    kp, vp = _rows(kp_ref, r, dil), _rows(vp_ref, r, dil)
                else:
                    kp, vp = _rows(kc_ref, base - sbr, dil), _rows(vc_ref, base - sbr, dil)
                if sb == m - 1:
                    qn, do_n = _rows(qn_ref, r, dil), _rows(don_ref, r, dil)
                    l_n, d_n = _rows(ln_ref, r, dil), _rows(dn_ref, r, dil)
                else:
                    qn, do_n = _rows(q_ref, base + sbr, dil), _rows(do_ref, base + sbr, dil)
                    l_n, d_n = _rows(l_ref, base + sbr, dil), _rows(d_ref, base + sbr, dil)
                k2, kp2, v2, vp2 = _stack_heads(k, head0), _stack_heads(kp, head0), _stack_heads(v, head0), _stack_heads(vp, head0)
                qb, qnb, dob, donb = _mx(q), _mx(qn), _mx(do_c), _mx(do_n)
                lse2, dlt2, lsen2, dltn2 = _head_rows(l_c), _head_rows(d_c), _head_rows(l_n), _head_rows(d_n)

                def pair(keys, vals, qs, dos, lse_rows, dlt_rows, valid):
                    p = jnp.where(valid, jnp.exp(_dot_nt(keys, qs) * ATTN_SCALE - lse_rows), 0.0)
                    ds = _mx(p * (_dot_nt(vals, dos) - dlt_rows) * ATTN_SCALE)
                    return _mx(p), ds

                p_a, ds_a = pair(k2, v2, qb, dob, lse2, dlt2, same_t)
                _, ds_b = pair(kp2, vp2, qb, dob, lse2, dlt2, prev_t)
                p_c, ds_c = pair(k2, v2, qnb, donb, lsen2, dltn2, next_t)
                dq = _dot_tn(ds_a, k2) + _dot_tn(ds_b, kp2)
                dk2 = _dot(ds_a, qb) + _dot(ds_c, qnb)
                dv2 = _dot(p_a, dob) + _dot(p_c, donb)
                c, s1, s2 = _rows(c_ref, base, dil), _rows(s1_ref, base, dil), _rows(s2_ref, base, dil)
                _set_rows(dq_buf, base, dil, _rope_transpose(dq, c, s1, s2, 128))
                _set_rows(dk_buf, base, dil, _rope_transpose(jnp.where(head0, dk2[:BLK], dk2[BLK:]), c, s1, s2, 128))
                _set_rows(dv_buf, base, dil, jnp.where(head0, dv2[:BLK], dv2[BLK:]))
                return carry

            _for_residues(dil, one_residue)
        copies = []
        for t, (buf, col) in enumerate(((dq_buf, Q_COL), (dk_buf, K_COL), (dv_buf, V_COL))):
            out_buf[t] = buf[...].astype(out_buf.dtype)
            lane0 = pl.multiple_of((col + 2 * g + hp) * 128, 128)
            dst = dz_ref.at[pl.ds(pl.multiple_of(st * rows, rows), rows), pl.ds(lane0, 128)]
            cp = pltpu.make_async_copy(out_buf.at[t], dst, sems.at[t])
            cp.start()
            copies.append(cp)
        for cp in copies:
            cp.wait()

    def cur(col):
        return pl.BlockSpec((rows, 128), lambda st, hp: (st, col + 2 * g + hp))

    def before(col):
        return pl.BlockSpec((sbr, 128), lambda st, hp: (jnp.maximum(st * m - 1, 0), col + 2 * g + hp))

    def after(col):
        return pl.BlockSpec((sbr, 128), lambda st, hp: (jnp.minimum((st + 1) * m, s_len // sbr - 1), col + 2 * g + hp))

    tab = pl.BlockSpec((rows, 128), lambda st, hp: (st, 0))
    return pl.pallas_call(
        body, name=name, grid=(nsteps, 2),
        in_specs=[cur(Q_COL), after(Q_COL), cur(K_COL), before(K_COL), cur(V_COL), before(V_COL),
                  cur(0), after(0), cur(0), after(0), cur(0), after(0), tab, tab, tab,
                  pl.BlockSpec(memory_space=pl.ANY)],
        out_specs=pl.BlockSpec(memory_space=pl.ANY),
        out_shape=jax.ShapeDtypeStruct(dz.shape, dz.dtype), input_output_aliases={15: 0},
        scratch_shapes=[pltpu.VMEM((rows, 128), f32)] * 3 + [pltpu.VMEM((3, rows, 128), dz.dtype),
                                                            pltpu.SemaphoreType.DMA((3,))],
        compiler_params=_cparams(("arbitrary", "arbitrary")),
    )(z, z, z, z, z, z, do, do, lse, lse, dlt, dlt, *tabs, dz)


def _attn_fwd_old(q, k, v, dil, name):
    s_len = q.shape[0]
    nblk = s_len // BLK

    def body(q_ref, kc_ref, kp_ref, vc_ref, vp_ref, o_ref, l_ref):
        b = pl.program_id(0)
        has_prev = b >= dil
        low, up = _tri_masks()
        valid = jnp.concatenate([up & has_prev, low], axis=1)
        outs, lses = [], []
        for hh in range(2):
            sl = slice(hh * HEAD_DIM, (hh + 1) * HEAD_DIM)
            qh = _mx(q_ref[:, sl])
            k2 = jnp.concatenate([_mx(kp_ref[:, sl]), _mx(kc_ref[:, sl])], axis=0)
            v2 = jnp.concatenate([_mx(vp_ref[:, sl]), _mx(vc_ref[:, sl])], axis=0)
            s = jnp.where(valid, _dot_nt(qh, k2) * ATTN_SCALE, NEG_BIG)
            m = jnp.max(s, axis=-1, keepdims=True)
            e = jnp.exp(s - m)
            l = jnp.sum(e, axis=-1, keepdims=True)
            outs.append(_dot(_mx(e / l), v2))
            lses.append(jnp.broadcast_to(m + jnp.log(l), (BLK, HEAD_DIM)))
        o_ref[...] = jnp.concatenate(outs, axis=1)
        l_ref[...] = jnp.concatenate(lses, axis=1)

    cur = pl.BlockSpec((BLK, 128), lambda b, hp: (b, hp))
    prev = pl.BlockSpec((BLK, 128), lambda b, hp: (jnp.maximum(b - dil, 0), hp))
    return pl.pallas_call(
        body, name=name, grid=(nblk, 2), in_specs=[cur, cur, prev, cur, prev], out_specs=[cur, cur],
        out_shape=[jax.ShapeDtypeStruct((s_len, 256), f32)] * 2,
        compiler_params=_cparams(("parallel", "parallel")),
    )(q, k, k, v, v)


def _attn_bwd_old(q, k, v, do, lse, dlt, tabs, dil, name):
    s_len = q.shape[0]
    nblk = s_len // BLK

    def body(q_ref, qn_ref, kc_ref, kp_ref, vc_ref, vp_ref, do_ref, don_ref, l_ref, ln_ref, d_ref, dn_ref,
             c_ref, s1_ref, s2_ref, dq_ref, dk_ref, dv_ref):
        b = pl.program_id(0)
        has_prev = b >= dil
        has_next = b + dil < nblk
        low, up = _tri_masks()
        dqs, dks, dvs = [], [], []
        for hh in range(2):
            sl = slice(hh * HEAD_DIM, (hh + 1) * HEAD_DIM)
            one = slice(hh * HEAD_DIM, hh * HEAD_DIM + 1)
            qc, qn = _mx(q_ref[:, sl]), _mx(qn_ref[:, sl])
            kc, kp = _mx(kc_ref[:, sl]), _mx(kp_ref[:, sl])
            vc, vp = _mx(vc_ref[:, sl]), _mx(vp_ref[:, sl])
            doc, don = _mx(do_ref[:, sl]), _mx(don_ref[:, sl])
            lc, ln = l_ref[:, one], ln_ref[:, one]
            dc, dn = d_ref[:, one], dn_ref[:, one]
            p_a = jnp.where(low, jnp.exp(_dot_nt(qc, kc) * ATTN_SCALE - lc), 0.0)
            ds_a = _mx(p_a * (_dot_nt(doc, vc) - dc) * ATTN_SCALE)
            p_b = jnp.where(up & has_prev, jnp.exp(_dot_nt(qc, kp) * ATTN_SCALE - lc), 0.0)
            ds_b = _mx(p_b * (_dot_nt(doc, vp) - dc) * ATTN_SCALE)
            p_c = jnp.where(up & has_next, jnp.exp(_dot_nt(qn, kc) * ATTN_SCALE - ln), 0.0)
            ds_c = _mx(p_c * (_dot_nt(don, vc) - dn) * ATTN_SCALE)
            dqs.append(_dot(ds_a, kc) + _dot(ds_b, kp))
            dks.append(_dot_tn(ds_a, qc) + _dot_tn(ds_c, qn))
            dvs.append(_dot_tn(_mx(p_a), doc) + _dot_tn(_mx(p_c), don))
        c, s1, s2 = c_ref[...], s1_ref[...], s2_ref[...]
        dq_ref[...] = _rope_transpose(jnp.concatenate(dqs, axis=1), c, s1, s2, 128)
        dk_ref[...] = _rope_transpose(jnp.concatenate(dks, axis=1), c, s1, s2, 128)
        dv_ref[...] = jnp.concatenate(dvs, axis=1)

    cur = pl.BlockSpec((BLK, 128), lambda b, hp: (b, hp))
    prev = pl.BlockSpec((BLK, 128), lambda b, hp: (jnp.maximum(b - dil, 0), hp))
    nxt = pl.BlockSpec((BLK, 128), lambda b, hp: (jnp.minimum(b + dil, nblk - 1), hp))
    tab = pl.BlockSpec((BLK, 128), lambda b, hp: (b, 0))
    return pl.pallas_call(
        body, name=name, grid=(nblk, 2),
        in_specs=[cur, nxt, cur, prev, cur, prev, cur, nxt, cur, nxt, cur, nxt, tab, tab, tab],
        out_specs=[cur, cur, cur], out_shape=[jax.ShapeDtypeStruct((s_len, 256), f32)] * 3,
        compiler_params=_cparams(("parallel", "parallel")),
    )(q, q, k, k, v, v, do, do, lse, lse, dlt, dlt, *tabs)


def _loss_head(h, g, target, name):
    s_len, d = h.shape

    def body(h_ref, g_ref, t_ref, loss_ref, dh_ref, dg_ref):
        i = pl.program_id(0)
        x = h_ref[...]
        gv = g_ref[...]
        r = lax.rsqrt(jnp.mean(x * x, axis=-1, keepdims=True) + EPS)
        xh = x * r
        diff = xh * gv - t_ref[...]
        part = 0.5 * jnp.sum(jnp.mean(diff * diff, axis=-1, keepdims=True), axis=0, keepdims=True)
        dy = diff * (1.0 / d)
        dxh = dy * gv
        dh_ref[...] = r * (dxh - xh * jnp.mean(dxh * xh, axis=-1, keepdims=True))
        dgsum = jnp.sum(dy * xh, axis=0, keepdims=True)
        lossb = jnp.broadcast_to(part, (8, 128))

        @pl.when(i == 0)
        def _():
            loss_ref[...] = lossb
            dg_ref[...] = dgsum

        @pl.when(i > 0)
        def _():
            loss_ref[...] += lossb
            dg_ref[...] += dgsum

    row = lambda i: (i, 0)
    return pl.pallas_call(
        body, name=name, grid=(s_len // TM,),
        in_specs=[pl.BlockSpec((TM, d), row), pl.BlockSpec((1, d), lambda i: (0, 0)), pl.BlockSpec((TM, d), row)],
        out_specs=[pl.BlockSpec((8, 128), lambda i: (0, 0)), pl.BlockSpec((TM, d), row),
                   pl.BlockSpec((1, d), lambda i: (0, 0))],
        out_shape=[jax.ShapeDtypeStruct((8, 128), f32), jax.ShapeDtypeStruct((s_len, d), f32),
                   jax.ShapeDtypeStruct((1, d), f32)],
        compiler_params=_cparams(("arbitrary",)),
    )(h, g, target)


def _rope_tables(positions):
    inv_freq = ROPE_THETA ** (-jnp.arange(0, ROT_DIM, 2, dtype=f32) / ROT_DIM)
    ang = positions.astype(f32)[:, None] * inv_freq
    cos, sin = jnp.cos(ang), jnp.sin(ang)
    s_len = positions.shape[0]
    zero8, rest = jnp.zeros((s_len, 8), f32), jnp.zeros((s_len, HEAD_DIM - ROT_DIM), f32)
    c = jnp.concatenate([cos, cos, jnp.ones((s_len, HEAD_DIM - ROT_DIM), f32)], axis=1)
    s1 = jnp.concatenate([-sin, zero8, rest], axis=1)
    s2 = jnp.concatenate([zero8, sin, rest], axis=1)
    return c, s1, s2


def _block_diag(pool_w):
    out = jnp.zeros((POOL_WIDTH, POOL_WIDTH), pool_w.dtype)
    for g in range(4):
        out = lax.dynamic_update_slice(out, pool_w[g], (g * POOL_GC, g * POOL_GC))
    return out


class _ReadyWeights:
    def __init__(self, full):
        self.full = full

    def take(self, layer, names, after):
        del after
        return {n: self.full[n] for n in names}, layer


def _layer_fwd(h, p_l, wsrc, small, layer, tabs):
    nm = f"l{layer}_"
    wts, wl = wsrc.take(layer, ("w_in",), (h,) if layer else tuple(tabs))
    z, hn1 = _norm_matmul(h, small["norm1"][layer][None], wts["w_in"], wl, 256, nm + "in_proj", rope=tabs)
    ol = None
    for g in range(3):
        ol = _attn_fwd(z, g, ol, nm + f"attn_fwd{g}")
    outs, lses = ol
    wbd = _mx(_block_diag(small["pool_w"][layer]))
    scale = small["pool_scale"][layer][None]
    m = _mixer_merge(z, wbd, scale, outs, lses, nm + "mixer_merge")
    rest, _ = wsrc.take(layer, ("w_out", "w_up", "w_down", "w_gate", "w_ple"), (m,))
    wts = {**wts, **rest}
    h1 = _matmul_residual(m, wts["w_out"], wl, h, nm + "out_proj")
    h2, a, hn2 = _mlp_fwd(h1, small["norm2"][layer][None], wts["w_up"], wts["w_down"], wl, nm + "mlp")
    h3, gl, hn3 = _gate_ple_fwd(h2, small["norm3"][layer][None], wts["w_gate"], wts["w_ple"], wl, p_l, nm + "gate_ple")
    saved = dict(h=h, z=z, hn1=hn1, outs=outs, lses=lses, wbd=wbd, scale=scale, m=m, h1=h1, a=a, hn2=hn2, h2=h2,
                 gl=gl, hn3=hn3, wts=wts, wl=wl)
    return h3, saved


def _layer_bwd(dh3, sv, p_l, small, layer, tabs128, reducer):
    nm = f"l{layer}_"
    wts, wl = sv["wts"], sv["wl"]
    dh2, dg3, de, dgl = _gate_bwd(dh3, sv["gl"], p_l, wts["w_ple"], wts["w_gate"], wl, sv["h2"],
                                  small["norm3"][layer][None], nm + "gate_bwd")
    reducer.add("w_gate", layer, _weight_grad(sv["hn3"], dgl, nm + "dw_gate"))
    reducer.add("w_ple", layer, _weight_grad(p_l, de, nm + "dw_ple"))
    da = _down_bwd(dh2, wts["w_down"], wl, sv["a"], nm + "down_bwd")
    reducer.add("w_down", layer, _weight_grad(sv["a"], dh2, nm + "dw_down", act=True))
    started = reducer.add("w_up", layer, _weight_grad(sv["hn2"], da, nm + "dw_up"))
    dh1, dg2 = _matmul_nt_norm_bwd(da, wts["w_up"], wl, sv["h1"], small["norm2"][layer][None], dh2, nm + "up_bwd",
                                   after=started)
    dm = _matmul_nt(dh1, wts["w_out"], wl, nm + "out_bwd")
    reducer.add("w_out", layer, _weight_grad(sv["m"], dh1, nm + "dw_out"))
    do, dlt = _combine_bwd(dm, sv["outs"], sv["lses"], nm + "combine_bwd")
    dz, dwbd, dscale = _pool_bwd(sv["z"], dm, sv["wbd"], sv["scale"], nm + "pool_bwd")
    for g in range(3):
        dz = _attn_bwd(sv["z"], do, sv["lses"], dlt, tabs128, dz, g, nm + f"attn_bwd{g}")
    started = reducer.add("w_in", layer, _weight_grad(sv["hn1"], dz, nm + "dw_in"))
    dh0, dg1 = _matmul_nt_norm_bwd(dz, wts["w_in"], wl, sv["h"], small["norm1"][layer][None], dh1, nm + "in_bwd",
                                   tk=512, after=started)
    dpool_w = jnp.stack([dwbd[g * POOL_GC:(g + 1) * POOL_GC, g * POOL_GC:(g + 1) * POOL_GC] for g in range(4)])
    sg = dict(norm1=dg1[0], norm2=dg2[0], norm3=dg3[0], pool_w=dpool_w, pool_scale=dscale[0])
    return dh0, sg


class _CollectGrads:
    def __init__(self):
        self.grads = {}

    def add(self, name, layer, dw):
        self.grads[(name, layer)] = dw


def _local_step(x, p, positions, wsrc, small, target, reducer):
    tabs128 = tuple(jnp.tile(t, (1, 2)) for t in _rope_tables(positions))
    h = x
    saved = []
    for layer in range(2):
        h, sv = _layer_fwd(h, p[layer], wsrc, small, layer, tabs128)
        saved.append(sv)
    loss, dh, dgf = _loss_head(h, small["final_norm"][None], target, "loss_head")
    sgs = [None, None]
    for layer in (1, 0):
        dh, sgs[layer] = _layer_bwd(dh, saved[layer], p[layer], small, layer, tabs128, reducer)
    small_grads = {k: jnp.stack([sgs[0][k], sgs[1][k]]) for k in sgs[0]}
    small_grads["final_norm"] = dgf[0]
    return loss, dh, small_grads


HBM = pl.BlockSpec(memory_space=pltpu.HBM)


def _my_place():
    return lax.axis_index("x"), lax.axis_index("y"), lax.axis_index("c")


def _other_chips(x, y):
    return [(1 - x, y), (x, 1 - y), (1 - x, 1 - y)]


def _window(ref, name, chip):
    k, n = _shard_shape(name)
    if COL_SHARDED[name]:
        return ref.at[:, pl.ds(pl.multiple_of(chip * n, 128), n)]
    return ref.at[pl.ds(pl.multiple_of(chip * k, 128), k), :]


def _chip_index():
    return jnp.reshape(2 * lax.axis_index("x") + lax.axis_index("y"), (1,)).astype(jnp.int32)


def _shard_block(name, tr):
    ks, ns = _shard_shape(name)
    if COL_SHARDED[name]:
        return (tr, ns), lambda i, me: (i, me[0])
    return (tr, ns), lambda i, me: (me[0] * (ks // tr) + i, 0)


def _place_shard(w, name, layer):
    ks, ns = _shard_shape(name)
    tr = min(ks, 256)
    shape, index = _shard_block(name, tr)

    def body(me_ref, w_ref, o_ref):
        o_ref[...] = w_ref[...].astype(o_ref.dtype)

    return pl.pallas_call(
        body, name=f"place_{name}{layer}",
        grid_spec=pltpu.PrefetchScalarGridSpec(
            num_scalar_prefetch=1, grid=(ks // tr,),
            in_specs=[pl.BlockSpec((None, tr, ns), lambda i, me: (layer, i, 0))],
            out_specs=pl.BlockSpec((None,) + shape, lambda i, me: (0,) + index(i, me))),
        out_shape=jax.ShapeDtypeStruct((1,) + FULL_SHAPE[name], MXU_DTYPE),
        compiler_params=_cparams(("parallel",)),
    )(_chip_index(), w)


GATHER_ORDER = [("w_in", 0), ("w_out", 0), ("w_up", 0), ("w_down", 0), ("w_gate", 0), ("w_ple", 0),
                ("w_in", 1), ("w_out", 1), ("w_up", 1), ("w_down", 1), ("w_gate", 1), ("w_ple", 1)]
SEM = pl.BlockSpec(memory_space=pltpu.SEMAPHORE)
EFFECT = pltpu.SideEffectType.DATAFLOW_SIDE_EFFECTING


def _gather_copy(src_ref, dst_ref, name, idx, j, chip, send_sems, recv_sems, c):
    cx, cy = chip
    return pltpu.make_async_remote_copy(
        src_ref=src_ref, dst_ref=dst_ref, send_sem=send_sems.at[3 * idx + j], recv_sem=recv_sems.at[3 * idx + j],
        device_id=(cx, cy, c), device_id_type=MESH)


def _gather_start(placed, order, tag, after=None):
    n = len(order)
    extra = [] if after is None else [after]

    def body(*refs):
        ins = refs[:n]
        k = n + len(extra)
        send_sems, recv_sems = refs[k], refs[k + 1]
        outs = refs[k + 2:k + 2 + n]
        token = refs[-1]
        x, y, c = _my_place()
        me = 2 * x + y
        for idx, (name, _) in enumerate(order):
            for j, chip in enumerate(_other_chips(x, y)):
                _gather_copy(_window(ins[idx].at[0], name, me), _window(outs[idx].at[0], name, me), name, idx, j, chip,
                             send_sems, recv_sems, c).start()
        token[...] = jnp.zeros_like(token)

    res = pl.pallas_call(
        body, name="gather_start" + tag,
        out_shape=(pltpu.SemaphoreType.DMA((3 * n,)), pltpu.SemaphoreType.DMA((3 * n,)))
        + tuple(pltpu.HBM(a.shape, a.dtype) for a in placed) + (jax.ShapeDtypeStruct((8, 128), f32),),
        in_specs=[HBM] * n + [pl.BlockSpec(memory_space=pl.ANY)] * len(extra),
        out_specs=(SEM, SEM) + (HBM,) * n + (pl.BlockSpec(memory_space=pltpu.VMEM),),
        input_output_aliases={i: i + 2 for i in range(n)},
        compiler_params=pltpu.CompilerParams(has_side_effects=EFFECT),
    )(*[pltpu.with_memory_space_constraint(a, pltpu.HBM) for a in placed], *extra)
    return res[0], res[1], list(res[2:2 + n]), res[-1]


def _gather_wait(send_sems, recv_sems, arrays, order, idxs, after, name):
    n = len(idxs)

    def body(*refs):
        ins = refs[:n]
        send_ref, recv_ref = refs[n], refs[n + 1]
        x, y, c = _my_place()
        me = 2 * x + y
        for k, idx in enumerate(idxs):
            wname = order[idx][0]
            for j, chip in enumerate(_other_chips(x, y)):
                cx, cy = chip
                mine = _window(ins[k].at[0], wname, me)
                land = _window(ins[k].at[0], wname, 2 * cx + cy)
                _gather_copy(mine, mine, wname, idx, j, chip, send_ref, recv_ref, c).wait_send()
                _gather_copy(land, land, wname, idx, j, chip, send_ref, recv_ref, c).wait_recv()

    operands = list(arrays) + [send_sems, recv_sems] + list(after)
    in_specs = [HBM] * n + [SEM, SEM] + [pl.BlockSpec(memory_space=pl.ANY)] * len(after)
    res = pl.pallas_call(
        body, name=name, out_shape=tuple(pltpu.HBM(a.shape, a.dtype) for a in arrays),
        in_specs=in_specs, out_specs=(HBM,) * n, input_output_aliases={i: i for i in range(n)},
        compiler_params=pltpu.CompilerParams(has_side_effects=EFFECT),
    )(*operands)
    return list(res)


class _GatheredWeights:
    def __init__(self, shards):
        self.starts = []
        token = None
        for tag, order in (("_first", GATHER_ORDER[:1]), ("_rest", GATHER_ORDER[1:])):
            placed = [_place_shard(shards[name], name, layer) for name, layer in order]
            self.starts.append((order,) + _gather_start(placed, order, tag, token))
            token = self.starts[-1][-1]

    def take(self, layer, names, after):
        order, send, recv, arrays, _ = next(s for s in self.starts if (names[0], layer) in s[0])
        after = list(after)
        if order is self.starts[0][0]:
            after.append(self.starts[-1][-1])
        idxs = [order.index((n, layer)) for n in names]
        got = _gather_wait(send, recv, [arrays[i] for i in idxs], order, idxs, after, f"gather_wait{layer}_{names[0]}")
        return dict(zip(names, got)), 0


def _gather_weights(full):
    names = list(BIG)

    def body(*refs):
        ins = refs[:len(names)]
        outs = refs[len(names):2 * len(names)]
        send_ici, recv_ici, send_d2d, recv_d2d = refs[2 * len(names):]
        x, y, c = _my_place()
        me = 2 * x + y
        sibling = (x, y, 1 - c)
        chips = _other_chips(x, y)
        ici = []
        for t, name in enumerate(names):
            for j, (cx, cy) in enumerate(chips):
                cp = pltpu.make_async_remote_copy(
                    src_ref=_window(ins[t].at[c], name, me), dst_ref=_window(outs[t].at[c], name, me),
                    send_sem=send_ici.at[3 * t + j], recv_sem=recv_ici.at[3 * t + j],
                    device_id=(cx, cy, c), device_id_type=MESH)
                cp.start()
                ici.append(cp)
        fwd = []
        for t, name in enumerate(names):
            for j, (cx, cy) in enumerate(chips):
                land = _window(outs[t].at[c], name, 2 * cx + cy)
                pltpu.make_async_remote_copy(
                    src_ref=land, dst_ref=land, send_sem=send_ici.at[3 * t + j], recv_sem=recv_ici.at[3 * t + j],
                    device_id=(cx, cy, c), device_id_type=MESH).wait_recv()
                cp = pltpu.make_async_remote_copy(
                    src_ref=land, dst_ref=land, send_sem=send_d2d.at[3 * t + j], recv_sem=recv_d2d.at[3 * t + j],
                    device_id=sibling, device_id_type=MESH)
                cp.start()
                fwd.append(cp)
        for t, name in enumerate(names):
            for j, (cx, cy) in enumerate(chips):
                land = _window(outs[t].at[1 - c], name, 2 * cx + cy)
                pltpu.make_async_remote_copy(
                    src_ref=land, dst_ref=land, send_sem=send_d2d.at[3 * t + j], recv_sem=recv_d2d.at[3 * t + j],
                    device_id=sibling, device_id_type=MESH).wait_recv()
        for cp in ici + fwd:
            cp.wait_send()

    nsem = 3 * len(names)
    outs = pl.pallas_call(
        body, name="gather_weights",
        in_specs=[HBM] * len(names), out_specs=[HBM] * len(names),
        out_shape=[jax.ShapeDtypeStruct(full[n].shape, full[n].dtype) for n in names],
        input_output_aliases={t: t for t in range(len(names))},
        scratch_shapes=[pltpu.SemaphoreType.DMA((nsem,)), pltpu.SemaphoreType.DMA((nsem,)),
                        pltpu.SemaphoreType.DMA((nsem,)), pltpu.SemaphoreType.DMA((nsem,))],
    )(*[full[n] for n in names])
    return dict(zip(names, outs))


def _swap_layers(grads):
    names = list(BIG)

    def body(*refs):
        ins = refs[:len(names)]
        outs = refs[len(names):2 * len(names)]
        send_sems, recv_sems = refs[2 * len(names):]
        x, y, c = _my_place()
        sibling = (x, y, 1 - c)
        cps = []
        for t in range(len(names)):
            cp = pltpu.make_async_remote_copy(
                src_ref=ins[t].at[1 - c], dst_ref=outs[t], send_sem=send_sems.at[t], recv_sem=recv_sems.at[t],
                device_id=sibling, device_id_type=MESH)
            cp.start()
            cps.append(cp)
        for cp in cps:
            cp.wait()

    outs = pl.pallas_call(
        body, name="swap_layers", in_specs=[HBM] * len(names), out_specs=[HBM] * len(names),
        out_shape=[jax.ShapeDtypeStruct(FULL_SHAPE[n], f32) for n in names],
        scratch_shapes=[pltpu.SemaphoreType.DMA((len(names),)), pltpu.SemaphoreType.DMA((len(names),))],
    )(*[grads[n] for n in names])
    return dict(zip(names, outs))


def _chip_sum(grad, other, name):
    k, n = FULL_SHAPE[name]
    tr = min(k, 512)
    c = lax.axis_index("c")

    def body(c_ref, g_ref, o_ref, out_ref):
        out_ref[...] = (g_ref[...] + o_ref[...]).astype(out_ref.dtype)

    return pl.pallas_call(
        body, name="chip_sum_" + name,
        grid_spec=pltpu.PrefetchScalarGridSpec(
            num_scalar_prefetch=1, grid=(k // tr,),
            in_specs=[pl.BlockSpec((None, tr, n), lambda i, c_ref: (c_ref[0], i, 0)),
                      pl.BlockSpec((tr, n), lambda i, c_ref: (i, 0))],
            out_specs=pl.BlockSpec((tr, n), lambda i, c_ref: (i, 0))),
        out_shape=jax.ShapeDtypeStruct((k, n), COMM_DTYPE),
        compiler_params=_cparams(("parallel",)),
    )(jnp.reshape(c, (1,)).astype(jnp.int32), grad, other)


def _scatter_shards(sums):
    names = list(BIG)

    def body(*refs):
        ins = refs[:len(names)]
        outs = refs[len(names):2 * len(names)]
        send_sems, recv_sems = refs[2 * len(names):]
        x, y, c = _my_place()
        me = 2 * x + y
        chips = _other_chips(x, y)
        cps = []
        for t, name in enumerate(names):
            for j, (cx, cy) in enumerate(chips):
                cp = pltpu.make_async_remote_copy(
                    src_ref=_window(ins[t], name, 2 * cx + cy), dst_ref=outs[t].at[me],
                    send_sem=send_sems.at[3 * t + j], recv_sem=recv_sems.at[3 * t + j],
                    device_id=(cx, cy, c), device_id_type=MESH)
                cp.start()
                cps.append(cp)
        for t, name in enumerate(names):
            for j, (cx, cy) in enumerate(chips):
                land = outs[t].at[2 * cx + cy]
                pltpu.make_async_remote_copy(
                    src_ref=land, dst_ref=land, send_sem=send_sems.at[3 * t + j], recv_sem=recv_sems.at[3 * t + j],
                    device_id=(cx, cy, c), device_id_type=MESH).wait_recv()
        for cp in cps:
            cp.wait_send()

    nsem = 3 * len(names)
    outs = pl.pallas_call(
        body, name="scatter_shards", in_specs=[HBM] * len(names), out_specs=[HBM] * len(names),
        out_shape=[jax.ShapeDtypeStruct((N_CHIPS,) + _shard_shape(n), sums[n].dtype) for n in names],
        scratch_shapes=[pltpu.SemaphoreType.DMA((nsem,)), pltpu.SemaphoreType.DMA((nsem,))],
    )(*[sums[n] for n in names])
    return dict(zip(names, outs))


def _sum_slots(slots, own, name):
    ks, ns = _shard_shape(name)
    tr = min(ks, 256)
    shape, index = _shard_block(name, tr)

    def body(me_ref, c_ref, s_ref, own_ref, out_ref):
        me = me_ref[0]
        acc = None
        for s in range(N_CHIPS):
            term = jnp.where(me == s, own_ref[...], s_ref[s]).astype(f32)
            acc = term if acc is None else acc + term
        out_ref[...] = acc

    return pl.pallas_call(
        body, name="sum_slots_" + name,
        grid_spec=pltpu.PrefetchScalarGridSpec(
            num_scalar_prefetch=2, grid=(ks // tr,),
            in_specs=[pl.BlockSpec((N_CHIPS, tr, ns), lambda i, me, c: (0, i, 0)),
                      pl.BlockSpec(shape, lambda i, me, c: index(i, me))],
            out_specs=pl.BlockSpec((None, tr, ns), lambda i, me, c: (c[0], i, 0))),
        out_shape=jax.ShapeDtypeStruct((2, ks, ns), f32),
        compiler_params=_cparams(("parallel",)),
    )(_chip_index(), jnp.reshape(lax.axis_index("c"), (1,)).astype(jnp.int32), slots, own)


N_DEV = 8


def _reduce_copies(dws, lands, names, layer, send_sems, recv_sems):
    x, y, c = _my_place()
    me, my_dev = 2 * x + y, 4 * x + 2 * y + c
    out = []
    for t, name in enumerate(names):
        for j, (cx, cy) in enumerate(_other_chips(x, y)):
            out.append((pltpu.make_async_remote_copy(
                src_ref=_window(dws[t], name, 2 * cx + cy), dst_ref=lands[t].at[my_dev],
                send_sem=send_sems.at[4 * t + j], recv_sem=recv_sems.at[N_DEV * t + my_dev],
                device_id=(cx, cy, layer), device_id_type=MESH), False))
        out.append((pltpu.make_async_remote_copy(
            src_ref=_window(dws[t], name, me), dst_ref=lands[t].at[my_dev],
            send_sem=send_sems.at[4 * t + 3], recv_sem=recv_sems.at[N_DEV * t + my_dev],
            device_id=(x, y, layer), device_id_type=MESH), True))
    return out


def _reduce_start(dws, names, layer, tag):
    n = len(names)
    lands = [lax.empty((N_DEV,) + _shard_shape(nm), dws[0].dtype) for nm in names]

    def body(*refs):
        ins = refs[:n]
        send_sems, recv_sems = refs[2 * n], refs[2 * n + 1]
        land_out = refs[3 * n + 2:4 * n + 2]
        token = refs[-1]
        c = lax.axis_index("c")
        for cp, non_owner_only in _reduce_copies(ins, land_out, names, layer, send_sems, recv_sems):
            if non_owner_only:
                @pl.when(c != layer)
                def _():
                    cp.start()
            else:
                cp.start()
        token[...] = jnp.zeros_like(token)

    res = pl.pallas_call(
        body, name="reduce_start" + tag,
        out_shape=(pltpu.SemaphoreType.DMA((4 * n,)), pltpu.SemaphoreType.DMA((N_DEV * n,)))
        + tuple(pltpu.HBM(a.shape, a.dtype) for a in dws) + tuple(pltpu.HBM(a.shape, a.dtype) for a in lands)
        + (jax.ShapeDtypeStruct((8, 128), f32),),
        in_specs=[HBM] * (2 * n),
        out_specs=(SEM, SEM) + (HBM,) * (2 * n) + (pl.BlockSpec(memory_space=pltpu.VMEM),),
        input_output_aliases={i: i + 2 for i in range(2 * n)},
        compiler_params=pltpu.CompilerParams(has_side_effects=EFFECT),
    )(*[pltpu.with_memory_space_constraint(a, pltpu.HBM) for a in list(dws) + lands])
    return res[0], res[1], list(res[2:2 + n]), list(res[2 + n:2 + 2 * n]), res[-1]


def _reduce_wait(send_sems, recv_sems, dws, lands, names, layer, after, tag):
    n = len(names)

    def body(*refs):
        ins, land_in = refs[:n], refs[n:2 * n]
        send_ref, recv_ref = refs[2 * n], refs[2 * n + 1]
        x, y, c = _my_place()
        for cp, non_owner_only in _reduce_copies(ins, land_in, names, layer, send_ref, recv_ref):
            if non_owner_only:
                @pl.when(c != layer)
                def _():
                    cp.wait_send()
            else:
                cp.wait_send()

        @pl.when(c == layer)
        def _():
            for t in range(n):
                for k in range(1, N_DEV):
                    px, py, pc = x ^ ((k >> 2) & 1), y ^ ((k >> 1) & 1), c ^ (k & 1)
                    dev = 4 * px + 2 * py + pc
                    land = land_in[t].at[dev]
                    pltpu.make_async_remote_copy(
                        src_ref=land, dst_ref=land, send_sem=send_ref.at[4 * t], recv_sem=recv_ref.at[N_DEV * t + dev],
                        device_id=(px, py, pc), device_id_type=MESH).wait_recv()

    res = pl.pallas_call(
        body, name="reduce_wait" + tag,
        out_shape=tuple(pltpu.HBM(a.shape, a.dtype) for a in list(dws) + list(lands)),
        in_specs=[HBM] * (2 * n) + [SEM, SEM, pl.BlockSpec(memory_space=pl.ANY)], out_specs=(HBM,) * (2 * n),
        input_output_aliases={i: i for i in range(2 * n)},
        compiler_params=pltpu.CompilerParams(has_side_effects=EFFECT),
    )(*dws, *lands, send_sems, recv_sems, after)
    return list(res[:n]), list(res[n:])


def _sum_devices(land, own, name, layer, prev):
    ks, ns = _shard_shape(name)
    tr = min(ks, 256)
    shape, index = _shard_block(name, tr)

    def body(me_ref, dev_ref, *refs):
        s_ref, own_ref, out_ref = refs[0], refs[1], refs[-1]
        dev = dev_ref[0]
        acc = None
        for s in range(N_DEV):
            term = jnp.where(dev == s, own_ref[...], s_ref[s]).astype(f32)
            acc = term if acc is None else acc + term
        out_ref[...] = acc

    in_specs = [pl.BlockSpec((N_DEV, tr, ns), lambda i, me, dev: (0, i, 0)),
                pl.BlockSpec(shape, lambda i, me, dev: index(i, me))]
    args = [land, own]
    aliases = {}
    if prev is not None:
        in_specs.append(pl.BlockSpec(memory_space=pl.ANY))
        args.append(prev)
        aliases = {4: 0}
    x, y, c = _my_place()
    return pl.pallas_call(
        body, name=f"sum_devices_{name}{layer}",
        grid_spec=pltpu.PrefetchScalarGridSpec(
            num_scalar_prefetch=2, grid=(ks // tr,), in_specs=in_specs,
            out_specs=pl.BlockSpec((None, tr, ns), lambda i, me, dev: (layer, i, 0))),
        out_shape=jax.ShapeDtypeStruct((2, ks, ns), f32), input_output_aliases=aliases,
        compiler_params=_cparams(("parallel",)),
    )(_chip_index(), jnp.reshape(4 * x + 2 * y + c, (1,)).astype(jnp.int32), *args)


class _GradReducer:
    GROUPS = (("1", 1, ("w_gate", "w_ple", "w_down", "w_up", "w_out", "w_in")),
              ("0a", 0, ("w_gate", "w_ple", "w_down", "w_up")),
              ("0b", 0, ("w_out", "w_in")))

    def __init__(self):
        self.grads = {}
        self.started = {}

    def add(self, name, layer, dw):
        self.grads[(name, layer)] = dw
        token = None
        for tag, glayer, names in self.GROUPS:
            if tag not in self.started and all((nm, glayer) in self.grads for nm in names):
                *self.started[tag], token = _reduce_start([self.grads[(nm, glayer)] for nm in names], names, glayer, tag)
        return token

    def finish(self, after):
        mine = {}
        for tag, layer, names in self.GROUPS:
            send, recv, dws, lands = self.started[tag]
            dws, lands = _reduce_wait(send, recv, dws, lands, names, layer, after, tag)
            for nm, dw, land in zip(names, dws, lands):
                mine[nm] = _sum_devices(land, dw, nm, layer, mine.get(nm))
        return _pair_layers(mine)


def _pair_layers(mine):
    names = list(BIG)

    def body(*refs):
        ins = refs[:len(names)]
        outs = refs[len(names):2 * len(names)]
        send_sems, recv_sems = refs[2 * len(names):]
        x, y, c = _my_place()
        sibling = (x, y, 1 - c)
        cps = []
        for t in range(len(names)):
            cp = pltpu.make_async_remote_copy(
                src_ref=ins[t].at[c], dst_ref=outs[t].at[c], send_sem=send_sems.at[t], recv_sem=recv_sems.at[t],
                device_id=sibling, device_id_type=MESH)
            cp.start()
            cps.append(cp)
        for t in range(len(names)):
            cps[t].wait_send()
            land = outs[t].at[1 - c]
            pltpu.make_async_remote_copy(
                src_ref=land, dst_ref=land, send_sem=send_sems.at[t], recv_sem=recv_sems.at[t],
                device_id=sibling, device_id_type=MESH).wait_recv()

    outs = pl.pallas_call(
        body, name="pair_layers", in_specs=[HBM] * len(names), out_specs=[HBM] * len(names),
        out_shape=[jax.ShapeDtypeStruct((2,) + _shard_shape(n), f32) for n in names],
        input_output_aliases={t: t for t in range(len(names))},
        scratch_shapes=[pltpu.SemaphoreType.DMA((len(names),)), pltpu.SemaphoreType.DMA((len(names),))],
    )(*[mine[n] for n in names])
    return dict(zip(names, outs))


SMALL_ROWS = 320


def _allreduce_small(vec):
    n_dev = 8

    def body(v_ref, out_ref, buf_ref, send_sems, recv_sems):
        x, y, c = _my_place()
        me = 4 * x + 2 * y + c
        buf_ref[me] = v_ref[...]
        cps = []
        for k in range(1, n_dev):
            dx, dy, dc = (k >> 2) & 1, (k >> 1) & 1, k & 1
            peer = (x ^ dx, y ^ dy, c ^ dc)
            cp = pltpu.make_async_remote_copy(
                src_ref=v_ref, dst_ref=buf_ref.at[me], send_sem=send_sems.at[k - 1], recv_sem=recv_sems.at[k - 1],
                device_id=peer, device_id_type=MESH)
            cp.start()
            cps.append(cp)
        for k in range(1, n_dev):
            dx, dy, dc = (k >> 2) & 1, (k >> 1) & 1, k & 1
            src = 4 * (x ^ dx) + 2 * (y ^ dy) + (c ^ dc)
            land = buf_ref.at[src]
            pltpu.make_async_remote_copy(
                src_ref=land, dst_ref=land, send_sem=send_sems.at[k - 1], recv_sem=recv_sems.at[k - 1],
                device_id=(x ^ dx, y ^ dy, c ^ dc), device_id_type=MESH).wait_recv()
        for cp in cps:
            cp.wait_send()
        acc = buf_ref[0]
        for s in range(1, n_dev):
            acc = acc + buf_ref[s]
        out_ref[...] = acc

    return pl.pallas_call(
        body, name="allreduce_small",
        in_specs=[pl.BlockSpec(memory_space=pltpu.VMEM)], out_specs=pl.BlockSpec(memory_space=pltpu.VMEM),
        out_shape=jax.ShapeDtypeStruct((SMALL_ROWS, 128), f32),
        scratch_shapes=[pltpu.VMEM((n_dev, SMALL_ROWS, 128), f32), pltpu.SemaphoreType.DMA((n_dev - 1,)),
                        pltpu.SemaphoreType.DMA((n_dev - 1,))],
    )(vec)


def _adamw(w, g, m, v, name):
    rows, cols = w.shape
    tr = rows
    for cand in (512, 256, 128, 64, 32, 16, 8):
        if rows % cand == 0 and cand * cols * 4 <= 2 * 1024 * 1024:
            tr = cand
            break
    c1 = np.float32(1.0 - ADAM_B1 ** ADAM_STEP)
    c2 = np.float32(1.0 - ADAM_B2 ** ADAM_STEP)

    def body(w_ref, g_ref, m_ref, v_ref, go_ref, d_ref, mo_ref, vo_ref):
        gv = g_ref[...]
        go_ref[...] = gv
        mn = ADAM_B1 * m_ref[...] + (1.0 - ADAM_B1) * gv
        vn = ADAM_B2 * v_ref[...] + (1.0 - ADAM_B2) * (gv * gv)
        mo_ref[...] = mn
        vo_ref[...] = vn
        d_ref[...] = -ADAM_LR * ((mn / c1) / (jnp.sqrt(vn / c2) + ADAM_EPS) + ADAM_WD * w_ref[...])

    blk = pl.BlockSpec((tr, cols), lambda i: (i, 0))
    return pl.pallas_call(
        body, name="adamw_" + name, grid=(rows // tr,), in_specs=[blk] * 4, out_specs=[blk] * 4,
        out_shape=[jax.ShapeDtypeStruct((rows, cols), f32)] * 4,
        compiler_params=_cparams(("parallel",)),
    )(w, g, m, v)


SMALL = ("norm1", "pool_w", "pool_scale", "norm2", "norm3", "final_norm")
ORDER = ("norm1", "w_in", "pool_w", "pool_scale", "w_out", "norm2", "w_up", "w_down", "norm3", "w_gate", "w_ple",
         "final_norm")


def _pack_small(tree, extra=None):
    parts = [tree[n].reshape(-1) for n in SMALL]
    if extra is not None:
        parts.append(extra.reshape(-1))
    flat = jnp.concatenate(parts)
    return jnp.pad(flat, (0, SMALL_ROWS * 128 - flat.shape[0])).reshape(SMALL_ROWS, 128)


def _unpack_small(packed, like):
    flat = packed.reshape(-1)
    out, off = {}, 0
    for n in SMALL:
        size = int(np.prod(like[n].shape))
        out[n] = flat[off:off + size].reshape(like[n].shape)
        off += size
    return out, flat[off]


def kernel(x, p, positions, norm1, w_in, pool_w, pool_scale, w_out, norm2, w_up, w_down, norm3, w_gate, w_ple, final_norm, loss_target, m_norm1, m_w_in, m_pool_w, m_pool_scale, m_w_out, m_norm2, m_w_up, m_w_down, m_norm3, m_w_gate, m_w_ple, m_final_norm, v_norm1, v_w_in, v_pool_w, v_pool_scale, v_w_out, v_norm2, v_w_up, v_w_down, v_norm3, v_w_gate, v_w_ple, v_final_norm):
    w = dict(norm1=norm1, w_in=w_in, pool_w=pool_w, pool_scale=pool_scale, w_out=w_out, norm2=norm2, w_up=w_up,
             w_down=w_down, norm3=norm3, w_gate=w_gate, w_ple=w_ple, final_norm=final_norm)
    m = dict(norm1=m_norm1, w_in=m_w_in, pool_w=m_pool_w, pool_scale=m_pool_scale, w_out=m_w_out, norm2=m_norm2,
             w_up=m_w_up, w_down=m_w_down, norm3=m_norm3, w_gate=m_w_gate, w_ple=m_w_ple, final_norm=m_final_norm)
    v = dict(norm1=v_norm1, w_in=v_w_in, pool_w=v_pool_w, pool_scale=v_pool_scale, w_out=v_w_out, norm2=v_norm2,
             w_up=v_w_up, w_down=v_w_down, norm3=v_norm3, w_gate=v_w_gate, w_ple=v_w_ple, final_norm=v_final_norm)
    small = {n: w[n] for n in SMALL}

    wsrc = _GatheredWeights({n: w[n] for n in BIG})
    reducer = _GradReducer()
    loss8, dx, small_grads = _local_step(x[0], p[:, 0], positions[0], wsrc, small, loss_target[0], reducer)
    gsh = reducer.finish(dx)

    red = _allreduce_small(_pack_small(small_grads, loss8[0, 0]))
    g_small, loss = _unpack_small(red, small)

    g_out, d_out, m_out, v_out = {}, {}, {}, {}
    for n in BIG:
        shp = w[n].shape
        two = lambda a: a.reshape(shp[0] * shp[1], shp[2])
        g2, d2, m2, v2 = _adamw(two(w[n]), two(gsh[n]), two(m[n]), two(v[n]), n)
        g_out[n], d_out[n], m_out[n], v_out[n] = g2.reshape(shp), d2.reshape(shp), m2.reshape(shp), v2.reshape(shp)
    _, d2, m2, v2 = _adamw(_pack_small(small), red, _pack_small({n: m[n] for n in SMALL}),
                           _pack_small({n: v[n] for n in SMALL}), "small")
    for tree, packed in ((d_out, d2), (m_out, m2), (v_out, v2)):
        tree.update(_unpack_small(packed, small)[0])
    g_out.update(g_small)

    return (loss, dx[None], *[g_out[n] for n in ORDER], *[d_out[n] for n in ORDER], *[m_out[n] for n in ORDER],
            *[v_out[n] for n in ORDER])
```

```python
import functools

import jax
import jax.numpy as jnp
import numpy as np
from jax import lax
from jax.experimental import pallas as pl
from jax.experimental.pallas import tpu as pltpu

f32 = jnp.float32
MXU_DTYPE = jnp.bfloat16
COMM_DTYPE = jnp.bfloat16

D_MODEL = 1024
POOL_WIDTH = 256
POOL_GC = 64
ATTN_WIDTH = 768
HEAD_DIM = 64
N_IN = POOL_WIDTH + 3 * ATTN_WIDTH
D_FF = 4096
PLE_DIM = 256
BLK = 128
DILATIONS = (1, 4, 16)
ROT_DIM = 16
ROPE_THETA = 500000.0
EPS = 1e-6
ATTN_SCALE = HEAD_DIM ** -0.5
NEG_BIG = -1e30

ADAM_LR, ADAM_B1, ADAM_B2, ADAM_EPS, ADAM_WD, ADAM_STEP = 0.001, 0.9, 0.999, 1e-08, 0.01, 10

TM = 512
TM_WGRAD = 1024
HALO = 16
VMEM_LIMIT = 48 * 1024 * 1024
N_CHIPS = 4
MESH = pl.DeviceIdType.MESH

BIG = ("w_in", "w_out", "w_up", "w_down", "w_gate", "w_ple")
FULL_SHAPE = {"w_in": (D_MODEL, N_IN), "w_out": (D_MODEL, D_MODEL), "w_up": (D_MODEL, D_FF),
              "w_down": (D_FF, D_MODEL), "w_gate": (D_MODEL, D_MODEL), "w_ple": (PLE_DIM, D_MODEL)}
COL_SHARDED = {"w_in": True, "w_out": False, "w_up": True, "w_down": False, "w_gate": False, "w_ple": True}


def _shard_shape(name):
    k, n = FULL_SHAPE[name]
    return (k, n // N_CHIPS) if COL_SHARDED[name] else (k // N_CHIPS, n)


def _cparams(sem=None, vmem=VMEM_LIMIT):
    return pltpu.CompilerParams(dimension_semantics=sem, vmem_limit_bytes=vmem)


def _resident(block_shape, index_map):
    return pl.BlockSpec(block_shape, index_map, pipeline_mode=pl.Buffered(1))


def _mx(x):
    return x.astype(MXU_DTYPE)


def _dot(a, b):
    return jnp.dot(a, b, preferred_element_type=f32)


def _dot_nt(a, b):
    return lax.dot_general(a, b, (((1,), (1,)), ((), ())), preferred_element_type=f32)


def _dot_tn(a, b):
    return lax.dot_general(a, b, (((0,), (0,)), ((), ())), preferred_element_type=f32)


def _sigmoid(x):
    return 1.0 / (1.0 + jnp.exp(-x))


def _rope_apply(y, c, s1, s2, width):
    return y * c + pltpu.roll(y, width - 8, axis=1) * s1 + pltpu.roll(y, 8, axis=1) * s2


def _rope_transpose(dy, c, s1, s2, width):
    return dy * c + pltpu.roll(dy * s1, 8, axis=1) + pltpu.roll(dy * s2, width - 8, axis=1)


def _norm_matmul(h, g, w, layer, tn, name, rope=None):
    s_len, d = h.shape
    n = w.shape[2]

    def body(*refs):
        if rope is None:
            h_ref, g_ref, w_ref, y_ref, hn_ref = refs
        else:
            h_ref, g_ref, w_ref, c_ref, s1_ref, s2_ref, y_ref, hn_ref = refs
            reps = tn // 128
            c = jnp.concatenate([c_ref[...]] * reps, axis=1)
            s1 = jnp.concatenate([s1_ref[...]] * reps, axis=1)
            s2 = jnp.concatenate([s2_ref[...]] * reps, axis=1)
        x = h_ref[...]
        r = lax.rsqrt(jnp.mean(x * x, axis=-1, keepdims=True) + EPS)
        hn = ((x * r) * g_ref[...]).astype(hn_ref.dtype)
        hn_ref[...] = hn
        for j in range(n // tn):
            y = _dot(hn, w_ref[:, j * tn:(j + 1) * tn])
            if rope is not None and POOL_WIDTH <= j * tn < POOL_WIDTH + 2 * ATTN_WIDTH:
                y = _rope_apply(y, c, s1, s2, tn)
            y_ref[:, j * tn:(j + 1) * tn] = y

    in_specs = [pl.BlockSpec((TM, d), lambda i: (i, 0)),
                pl.BlockSpec((1, d), lambda i: (0, 0)),
                _resident((None, d, n), lambda i: (layer, 0, 0))]
    args = [h, g, w]
    if rope is not None:
        assert POOL_WIDTH % tn == 0 and (2 * ATTN_WIDTH) % tn == 0
        in_specs += [pl.BlockSpec((TM, 128), lambda i: (i, 0))] * 3
        args += list(rope)
    return pl.pallas_call(
        body, name=name, grid=(s_len // TM,), in_specs=in_specs,
        out_specs=[pl.BlockSpec((TM, n), lambda i: (i, 0)), pl.BlockSpec((TM, d), lambda i: (i, 0))],
        out_shape=[jax.ShapeDtypeStruct((s_len, n), f32), jax.ShapeDtypeStruct((s_len, d), MXU_DTYPE)],
        compiler_params=_cparams(("parallel",)),
    )(*args)


def _matmul_residual(a, w, layer, res, name, act=False, tk=1024):
    s_len, k_dim = a.shape
    n = w.shape[2]

    def body(a_ref, w_ref, res_ref, o_ref):
        acc = res_ref[...]
        for k in range(k_dim // tk):
            x = a_ref[:, k * tk:(k + 1) * tk]
            if act:
                r = jnp.maximum(x, 0.0)
                x = r * r
            acc = acc + _dot(_mx(x), w_ref[k * tk:(k + 1) * tk, :])
        o_ref[...] = acc

    return pl.pallas_call(
        body, name=name, grid=(s_len // TM,),
        in_specs=[pl.BlockSpec((TM, k_dim), lambda i: (i, 0)),
                  _resident((None, k_dim, n), lambda i: (layer, 0, 0)),
                  pl.BlockSpec((TM, n), lambda i: (i, 0))],
        out_specs=pl.BlockSpec((TM, n), lambda i: (i, 0)),
        out_shape=jax.ShapeDtypeStruct((s_len, n), f32),
        compiler_params=_cparams(("parallel",)),
    )(a, w, res)


def _gate_ple_fwd(h2, g, w_gate, w_ple, layer, p, name):
    s_len, d = h2.shape

    def body(h_ref, g_ref, wg_ref, p_ref, wp_ref, h3_ref, gl_ref, hn_ref):
        x = h_ref[...]
        r = lax.rsqrt(jnp.mean(x * x, axis=-1, keepdims=True) + EPS)
        hn = ((x * r) * g_ref[...]).astype(hn_ref.dtype)
        hn_ref[...] = hn
        gl = _dot(hn, wg_ref[...])
        gl_ref[...] = gl
        e = _dot(_mx(p_ref[...]), wp_ref[...])
        h3_ref[...] = x + _sigmoid(gl) * e

    row = lambda i: (i, 0)
    return pl.pallas_call(
        body, name=name, grid=(s_len // TM,),
        in_specs=[pl.BlockSpec((TM, d), row), pl.BlockSpec((1, d), lambda i: (0, 0)),
                  pl.BlockSpec((None, d, d), lambda i: (layer, 0, 0)), pl.BlockSpec((TM, PLE_DIM), row),
                  pl.BlockSpec((None, PLE_DIM, d), lambda i: (layer, 0, 0))],
        out_specs=[pl.BlockSpec((TM, d), row)] * 3,
        out_shape=[jax.ShapeDtypeStruct((s_len, d), f32), jax.ShapeDtypeStruct((s_len, d), f32),
                   jax.ShapeDtypeStruct((s_len, d), MXU_DTYPE)],
        compiler_params=_cparams(("parallel",)),
    )(h2, g, w_gate, p, w_ple)


def _gate_ple_bwd(dh3, gl, p, w_ple, layer, name):
    s_len, d = dh3.shape

    def body(dh_ref, gl_ref, p_ref, wp_ref, de_ref, dgl_ref):
        dh = dh_ref[...]
        gate = _sigmoid(gl_ref[...])
        e = _dot(_mx(p_ref[...]), wp_ref[...])
        de_ref[...] = (dh * gate).astype(de_ref.dtype)
        dgl_ref[...] = ((dh * e) * (gate * (1.0 - gate))).astype(dgl_ref.dtype)

    row = lambda i: (i, 0)
    return pl.pallas_call(
        body, name=name, grid=(s_len // TM,),
        in_specs=[pl.BlockSpec((TM, d), row), pl.BlockSpec((TM, d), row), pl.BlockSpec((TM, PLE_DIM), row),
                  pl.BlockSpec((None, PLE_DIM, d), lambda i: (layer, 0, 0))],
        out_specs=[pl.BlockSpec((TM, d), row)] * 2,
        out_shape=[jax.ShapeDtypeStruct((s_len, d), MXU_DTYPE)] * 2,
        compiler_params=_cparams(("parallel",)),
    )(dh3, gl, p, w_ple)


def _gate_bwd(dh3, gl, p, w_ple, w_gate, layer, h2, g, name):
    s_len, d = dh3.shape

    def body(dh_ref, gl_ref, p_ref, wp_ref, wg_ref, h_ref, g_ref, dh2_ref, dg_ref, de_ref, dgl_ref):
        i = pl.program_id(0)
        dh = dh_ref[...]
        gate = _sigmoid(gl_ref[...])
        e = _dot(_mx(p_ref[...]), wp_ref[...])
        de_ref[...] = (dh * gate).astype(de_ref.dtype)
        dgl = ((dh * e) * (gate * (1.0 - gate))).astype(dgl_ref.dtype)
        dgl_ref[...] = dgl
        dx, dgrow = _rmsnorm_bwd(_dot_nt(dgl, wg_ref[...]), h_ref[...], g_ref[...])
        dh2_ref[...] = dh + dx
        dgsum = jnp.sum(dgrow, axis=0, keepdims=True)

        @pl.when(i == 0)
        def _():
            dg_ref[...] = dgsum

        @pl.when(i > 0)
        def _():
            dg_ref[...] += dgsum

    row = lambda i: (i, 0)
    blk = pl.BlockSpec((TM, d), row)
    return pl.pallas_call(
        body, name=name, grid=(s_len // TM,),
        in_specs=[blk, blk, pl.BlockSpec((TM, PLE_DIM), row), _resident((None, PLE_DIM, d), lambda i: (layer, 0, 0)),
                  _resident((None, d, d), lambda i: (layer, 0, 0)), blk, pl.BlockSpec((1, d), lambda i: (0, 0))],
        out_specs=[blk, pl.BlockSpec((1, d), lambda i: (0, 0)), blk, blk],
        out_shape=[jax.ShapeDtypeStruct((s_len, d), f32), jax.ShapeDtypeStruct((1, d), f32),
                   jax.ShapeDtypeStruct((s_len, d), MXU_DTYPE), jax.ShapeDtypeStruct((s_len, d), MXU_DTYPE)],
        compiler_params=_cparams(("arbitrary",)),
    )(dh3, gl, p, w_ple, w_gate, h2, g)


def _rmsnorm_bwd(dhn, x, g):
    r = lax.rsqrt(jnp.mean(x * x, axis=-1, keepdims=True) + EPS)
    xh = x * r
    dxh = dhn * g
    dx = r * (dxh - xh * jnp.mean(dxh * xh, axis=-1, keepdims=True))
    return dx, dhn * xh


def _matmul_nt_norm_bwd(dy, w, layer, h_prev, g, dres, name, tk=1024, after=None):
    s_len, k_dim = dy.shape
    d = h_prev.shape[1]

    def body(dy_ref, w_ref, h_ref, g_ref, dres_ref, *rest):
        dh_ref, dg_ref = rest[-2:]
        i = pl.program_id(0)
        acc = None
        for k in range(k_dim // tk):
            part = _dot_nt(_mx(dy_ref[:, k * tk:(k + 1) * tk]), w_ref[:, k * tk:(k + 1) * tk])
            acc = part if acc is None else acc + part
        dx, dgrow = _rmsnorm_bwd(acc, h_ref[...], g_ref[...])
        dh_ref[...] = dres_ref[...] + dx
        dgsum = jnp.sum(dgrow, axis=0, keepdims=True)

        @pl.when(i == 0)
        def _():
            dg_ref[...] = dgsum

        @pl.when(i > 0)
        def _():
            dg_ref[...] += dgsum

    in_specs = [pl.BlockSpec((TM, k_dim), lambda i: (i, 0)),
                _resident((None, d, k_dim), lambda i: (layer, 0, 0)),
                pl.BlockSpec((TM, d), lambda i: (i, 0)),
                pl.BlockSpec((1, d), lambda i: (0, 0)),
                pl.BlockSpec((TM, d), lambda i: (i, 0))]
    args = [dy, w, h_prev, g, dres]
    if after is not None:
        in_specs.append(pl.BlockSpec(memory_space=pl.ANY))
        args.append(after)
    return pl.pallas_call(
        body, name=name, grid=(s_len // TM,), in_specs=in_specs,
        out_specs=[pl.BlockSpec((TM, d), lambda i: (i, 0)), pl.BlockSpec((1, d), lambda i: (0, 0))],
        out_shape=[jax.ShapeDtypeStruct((s_len, d), f32), jax.ShapeDtypeStruct((1, d), f32)],
        compiler_params=_cparams(("arbitrary",)),
    )(*args)


def _mlp_fwd(h1, g, w_up, w_down, layer, name, tf=1024):
    s_len, d = h1.shape
    ff = w_up.shape[2]

    def body(h_ref, g_ref, wu_ref, wd_ref, h2_ref, a_ref, hn_ref):
        x = h_ref[...]
        r = lax.rsqrt(jnp.mean(x * x, axis=-1, keepdims=True) + EPS)
        hn = ((x * r) * g_ref[...]).astype(hn_ref.dtype)
        hn_ref[...] = hn
        acc = x
        for j in range(ff // tf):
            a = _dot(hn, wu_ref[:, j * tf:(j + 1) * tf])
            a_ref[:, j * tf:(j + 1) * tf] = a.astype(a_ref.dtype)
            relu = jnp.maximum(a, 0.0)
            acc = acc + _dot(_mx(relu * relu), wd_ref[j * tf:(j + 1) * tf, :])
        h2_ref[...] = acc

    row = lambda i: (i, 0)
    return pl.pallas_call(
        body, name=name, grid=(s_len // TM,),
        in_specs=[pl.BlockSpec((TM, d), row), pl.BlockSpec((1, d), lambda i: (0, 0)),
                  _resident((None, d, ff), lambda i: (layer, 0, 0)), _resident((None, ff, d), lambda i: (layer, 0, 0))],
        out_specs=[pl.BlockSpec((TM, d), row), pl.BlockSpec((TM, ff), row), pl.BlockSpec((TM, d), row)],
        out_shape=[jax.ShapeDtypeStruct((s_len, d), f32), jax.ShapeDtypeStruct((s_len, ff), MXU_DTYPE),
                   jax.ShapeDtypeStruct((s_len, d), MXU_DTYPE)],
        compiler_params=_cparams(("parallel",)),
    )(h1, g, w_up, w_down)


def _down_bwd(dh2, w_down, layer, a, name, tf=1024):
    s_len, d = dh2.shape
    ff = a.shape[1]

    def body(dh_ref, w_ref, a_ref, da_ref):
        dhb = _mx(dh_ref[...])
        for j in range(ff // tf):
            cols = slice(j * tf, (j + 1) * tf)
            dact = _dot_nt(dhb, w_ref[cols, :])
            da_ref[:, cols] = (dact * (2.0 * jnp.maximum(a_ref[:, cols].astype(f32), 0.0))).astype(da_ref.dtype)

    return pl.pallas_call(
        body, name=name, grid=(s_len // TM,),
        in_specs=[pl.BlockSpec((TM, d), lambda i: (i, 0)),
                  _resident((None, ff, d), lambda i: (layer, 0, 0)),
                  pl.BlockSpec((TM, ff), lambda i: (i, 0))],
        out_specs=pl.BlockSpec((TM, ff), lambda i: (i, 0)),
        out_shape=jax.ShapeDtypeStruct((s_len, ff), MXU_DTYPE),
        compiler_params=_cparams(("parallel",)),
    )(dh2, w_down, a)


def _matmul_nt(dy, w, layer, name):
    s_len, n = dy.shape
    k_dim = w.shape[1]

    def body(dy_ref, w_ref, o_ref):
        o_ref[...] = _dot_nt(_mx(dy_ref[...]), w_ref[...])

    return pl.pallas_call(
        body, name=name, grid=(s_len // TM,),
        in_specs=[pl.BlockSpec((TM, n), lambda i: (i, 0)), pl.BlockSpec((None, k_dim, n), lambda i: (layer, 0, 0))],
        out_specs=pl.BlockSpec((TM, k_dim), lambda i: (i, 0)),
        out_shape=jax.ShapeDtypeStruct((s_len, k_dim), f32),
        compiler_params=_cparams(("parallel",)),
    )(dy, w)


def _weight_grad(a, b, name, act=False):
    s_len, k_dim = a.shape
    n = b.shape[1]
    tka = min(k_dim, 2048)
    tnb = n if n <= 1024 else (2048 if n % 2048 == 0 else 640)
    ns = s_len // TM_WGRAD

    def body(a_ref, b_ref, o_ref, acc_ref):
        s = pl.program_id(2)
        x = a_ref[...]
        if act:
            relu = jnp.maximum(x.astype(f32), 0.0)
            x = relu * relu
        part = _dot_tn(_mx(x), _mx(b_ref[...]))

        @pl.when(s == 0)
        def _():
            acc_ref[...] = part

        @pl.when(s > 0)
        def _():
            acc_ref[...] += part

        @pl.when(s == ns - 1)
        def _():
            o_ref[...] = acc_ref[...].astype(o_ref.dtype)

    return pl.pallas_call(
        body, name=name, grid=(k_dim // tka, n // tnb, ns),
        in_specs=[pl.BlockSpec((TM_WGRAD, tka), lambda i, j, s: (s, i)),
                  pl.BlockSpec((TM_WGRAD, tnb), lambda i, j, s: (s, j))],
        out_specs=pl.BlockSpec((tka, tnb), lambda i, j, s: (i, j)),
        out_shape=jax.ShapeDtypeStruct((k_dim, n), COMM_DTYPE),
        scratch_shapes=[pltpu.VMEM((tka, tnb), f32)],
        compiler_params=_cparams(("parallel", "parallel", "arbitrary")),
    )(a, b)


def _group_select(lane, x2, x4, x8, x16):
    grp = lane // POOL_GC
    return jnp.where(grp == 0, x2, jnp.where(grp == 1, x4, jnp.where(grp == 2, x8, x16)))


def _pool_window(lane):
    grp = lane // POOL_GC
    return jnp.where(grp == 0, 2, jnp.where(grp == 1, 4, jnp.where(grp == 2, 8, 16)))


def _pool_y(u, halo, i):
    xs = jnp.concatenate([jnp.where(i > 0, halo, 0.0), u], axis=0)
    s2 = xs + pltpu.roll(xs, 1, axis=0)
    s4 = s2 + pltpu.roll(s2, 2, axis=0)
    s8 = s4 + pltpu.roll(s4, 4, axis=0)
    s16 = s8 + pltpu.roll(s8, 8, axis=0)
    lane = lax.broadcasted_iota(jnp.int32, xs.shape, 1)
    sel = _group_select(lane, s2, s4, s8, s16)[HALO:, :]
    t = i * TM + lax.broadcasted_iota(jnp.int32, u.shape, 0)
    cnt = jnp.minimum(_pool_window(lax.broadcasted_iota(jnp.int32, u.shape, 1)), t + 1).astype(f32)
    return sel / cnt - u


def _group_weights(l0, l1, l2):
    mx = jnp.maximum(jnp.maximum(l0, l1), l2)
    e0, e1, e2 = jnp.exp(l0 - mx), jnp.exp(l1 - mx), jnp.exp(l2 - mx)
    den = e0 + e1 + e2
    return e0 / den, e1 / den, e2 / den


def _mixer_merge(z, wbd, scale, outs, lses, name):
    s_len = z.shape[0]

    def body(u_ref, halo_ref, wbd_ref, sc_ref, o0, o1, o2, l0, l1, l2, m_ref):
        i = pl.program_id(0)
        y = _pool_y(u_ref[...], halo_ref[...], i)
        pool = _dot(_mx(y), wbd_ref[...]) * sc_ref[...]
        w0, w1, w2 = _group_weights(l0[...], l1[...], l2[...])
        m_ref[...] = jnp.concatenate([pool, o0[...] * w0, o1[...] * w1, o2[...] * w2], axis=1).astype(m_ref.dtype)

    row = lambda i: (i, 0)
    blk = pl.BlockSpec((TM, 256), row)
    grp = [pl.BlockSpec((TM, 256), lambda i, g=g: (i, g)) for g in range(3)]
    return pl.pallas_call(
        body, name=name, grid=(s_len // TM,),
        in_specs=[blk, pl.BlockSpec((HALO, 256), lambda i: (jnp.maximum(i * (TM // HALO) - 1, 0), 0)),
                  pl.BlockSpec((256, 256), lambda i: (0, 0)), pl.BlockSpec((1, 256), lambda i: (0, 0))] + grp + grp,
        out_specs=pl.BlockSpec((TM, D_MODEL), row),
        out_shape=jax.ShapeDtypeStruct((s_len, D_MODEL), MXU_DTYPE),
        compiler_params=_cparams(("parallel",)),
    )(z, z, wbd, scale, outs, outs, outs, lses, lses, lses)


def _head_sums(x):
    r = lax.broadcasted_iota(jnp.int32, (256, 256), 0) // HEAD_DIM
    c = lax.broadcasted_iota(jnp.int32, (256, 256), 1) // HEAD_DIM
    ones = jnp.where(r == c, 1.0, 0.0).astype(jnp.bfloat16)
    hi = x.astype(jnp.bfloat16)
    lo = (x - hi.astype(f32)).astype(jnp.bfloat16)
    return _dot(hi, ones) + _dot(lo, ones)


def _combine_bwd(dm, outs, lses, name):
    s_len = dm.shape[0]

    def body(d0, d1, d2, o0, o1, o2, l0, l1, l2, do_ref, dl_ref):
        w = _group_weights(l0[...], l1[...], l2[...])
        da = (d0[...], d1[...], d2[...])
        o = (o0[...], o1[...], o2[...])
        dw = [_head_sums(da[g] * o[g]) for g in range(3)]
        t = w[0] * dw[0] + w[1] * dw[1] + w[2] * dw[2]
        do_ref[...] = jnp.concatenate([da[g] * w[g] for g in range(3)], axis=1)
        dl_ref[...] = jnp.concatenate([w[g] * t for g in range(3)], axis=1)

    grp = [pl.BlockSpec((TM, 256), lambda i, g=g: (i, g)) for g in range(3)]
    return pl.pallas_call(
        body, name=name, grid=(s_len // TM,),
        in_specs=[pl.BlockSpec((TM, 256), lambda i: (i, 1)), pl.BlockSpec((TM, 256), lambda i: (i, 2)),
                  pl.BlockSpec((TM, 256), lambda i: (i, 3))] + grp + grp,
        out_specs=[pl.BlockSpec((TM, ATTN_WIDTH), lambda i: (i, 0))] * 2,
        out_shape=[jax.ShapeDtypeStruct((s_len, ATTN_WIDTH), f32)] * 2,
        compiler_params=_cparams(("parallel",)),
    )(dm, dm, dm, outs, outs, outs, lses, lses, lses)


def _pool_bwd(z, dm, wbd, scale, name):
    s_len = z.shape[0]
    n_halo = s_len // HALO

    def body(u_ref, uh_ref, d_ref, dh_ref, wbd_ref, sc_ref, du_ref, dw_ref, dsc_ref):
        i = pl.program_id(0)
        last = pl.num_programs(0) - 1
        y = _pool_y(u_ref[...], uh_ref[...], i)
        yb = _mx(y)
        dpo = d_ref[...]
        sc = sc_ref[...]
        dsc = jnp.sum(dpo * _dot(yb, wbd_ref[...]), axis=0, keepdims=True)
        dwp = _dot_tn(yb, _mx(dpo * sc))

        @pl.when(i == 0)
        def _():
            dsc_ref[...] = dsc
            dw_ref[...] = dwp

        @pl.when(i > 0)
        def _():
            dsc_ref[...] += dsc
            dw_ref[...] += dwp

        ext = jnp.concatenate([dpo, jnp.where(i < last, dh_ref[...], 0.0)], axis=0)
        dy = _dot_nt(_mx(ext * sc), wbd_ref[...])
        t = i * TM + lax.broadcasted_iota(jnp.int32, ext.shape, 0)
        lane = lax.broadcasted_iota(jnp.int32, ext.shape, 1)
        e = dy / jnp.minimum(_pool_window(lane), t + 1).astype(f32)
        rows = ext.shape[0]
        f2 = e + pltpu.roll(e, rows - 1, axis=0)
        f4 = f2 + pltpu.roll(f2, rows - 2, axis=0)
        f8 = f4 + pltpu.roll(f4, rows - 4, axis=0)
        f16 = f8 + pltpu.roll(f8, rows - 8, axis=0)
        du_ref[...] = (_group_select(lane, f2, f4, f8, f16) - dy)[:TM, :].astype(du_ref.dtype)

    row = lambda i: (i, 0)
    blk = pl.BlockSpec((TM, 256), row)
    return pl.pallas_call(
        body, name=name, grid=(s_len // TM,),
        in_specs=[blk, pl.BlockSpec((HALO, 256), lambda i: (jnp.maximum(i * (TM // HALO) - 1, 0), 0)),
                  blk, pl.BlockSpec((HALO, 256), lambda i: (jnp.minimum((i + 1) * (TM // HALO), n_halo - 1), 0)),
                  pl.BlockSpec((256, 256), lambda i: (0, 0)), pl.BlockSpec((1, 256), lambda i: (0, 0))],
        out_specs=[blk, pl.BlockSpec((256, 256), lambda i: (0, 0)), pl.BlockSpec((1, 256), lambda i: (0, 0))],
        out_shape=[jax.ShapeDtypeStruct((s_len, N_IN), MXU_DTYPE), jax.ShapeDtypeStruct((256, 256), f32),
                   jax.ShapeDtypeStruct((1, 256), f32)],
        compiler_params=_cparams(("arbitrary",)),
    )(z, z, dm, dm, wbd, scale)


def _to_strided(x, dil):
    if dil == 1:
        return x
    s_len, c = x.shape
    return x.reshape(s_len // (BLK * dil), BLK, dil, c).transpose(0, 2, 1, 3).reshape(s_len, c)


def _from_strided(x, dil):
    if dil == 1:
        return x
    s_len, c = x.shape
    return x.reshape(s_len // (BLK * dil), dil, BLK, c).transpose(0, 2, 1, 3).reshape(s_len, c)


def _tri_masks():
    qi = lax.broadcasted_iota(jnp.int32, (BLK, BLK), 0)
    ki = lax.broadcasted_iota(jnp.int32, (BLK, BLK), 1)
    return qi >= ki, ki >= qi


ATTN_SUPER_PER_STEP = (8, 2, 1)
Q_COL, K_COL, V_COL = POOL_WIDTH // 128, (POOL_WIDTH + ATTN_WIDTH) // 128, (POOL_WIDTH + 2 * ATTN_WIDTH) // 128


def _rows(ref, start, dil):
    if dil == 1:
        return ref[pl.ds(start, BLK), :]
    return ref[pl.ds(start, BLK, stride=dil), :]


RESIDUE_UNROLL = 4


def _for_residues(dil, fn, loop=True):
    if dil <= RESIDUE_UNROLL or not loop:
        for r in range(dil):
            fn(r, 0)
    else:
        lax.fori_loop(0, dil, fn, 0, unroll=RESIDUE_UNROLL)


def _set_rows(ref, start, dil, val):
    if dil == 1:
        ref[pl.ds(start, BLK), :] = val
    else:
        ref[pl.ds(start, BLK, stride=dil), :] = val


def _attn_fwd(z, g, prev, name):
    s_len = z.shape[0]
    dil, m = DILATIONS[g], ATTN_SUPER_PER_STEP[g]
    sbr = BLK * dil
    rows = sbr * m

    def body(*refs):
        q_ref, kc_ref, kp_ref, vc_ref, vp_ref = refs[:5]
        o_ref, l_ref = refs[-2:]
        st = pl.program_id(0)
        low, up = _tri_masks()
        head0 = lax.broadcasted_iota(jnp.int32, (BLK, 128), 1) < HEAD_DIM
        for sb in range(m):
            valid = jnp.concatenate([up & (st > 0) if sb == 0 else up, low], axis=1)

            def one_residue(r, carry, sb=sb, valid=valid):
                base = sb * sbr + r
                q = _rows(q_ref, base, dil)
                kc, vc = _rows(kc_ref, base, dil), _rows(vc_ref, base, dil)
                if sb == 0:
                    kp, vp = _rows(kp_ref, r, dil), _rows(vp_ref, r, dil)
                else:
                    kp, vp = _rows(kc_ref, base - sbr, dil), _rows(vc_ref, base - sbr, dil)
                k2 = jnp.concatenate([_mx(kp), _mx(kc)], axis=0)
                v2 = jnp.concatenate([_mx(vp), _mx(vc)], axis=0)
                qs = q * ATTN_SCALE
                outs, lses = [], []
                for hh in range(2):
                    s = jnp.where(valid, _dot_nt(_mx(jnp.where(head0 == (hh == 0), qs, 0.0)), k2), NEG_BIG)
                    mx = jnp.max(s, axis=-1, keepdims=True)
                    e = jnp.exp(s - mx)
                    l = jnp.sum(e, axis=-1, keepdims=True)
                    outs.append(_dot(_mx(e / l), v2))
                    lses.append(jnp.broadcast_to(mx + jnp.log(l), (BLK, 128)))
                _set_rows(o_ref, base, dil, jnp.where(head0, outs[0], outs[1]))
                _set_rows(l_ref, base, dil, jnp.where(head0, lses[0], lses[1]))
                return carry

            _for_residues(dil, one_residue, loop=False)

    def cur(col):
        return pl.BlockSpec((rows, 128), lambda st, hp: (st, col + 2 * g + hp))

    def before(col):
        return pl.BlockSpec((sbr, 128), lambda st, hp: (jnp.maximum(st * m - 1, 0), col + 2 * g + hp))

    in_specs = [cur(Q_COL), cur(K_COL), before(K_COL), cur(V_COL), before(V_COL)]
    args = [z, z, z, z, z]
    aliases = {}
    if prev is not None:
        in_specs += [pl.BlockSpec(memory_space=pl.ANY)] * 2
        args += list(prev)
        aliases = {5: 0, 6: 1}
    return pl.pallas_call(
        body, name=name, grid=(s_len // rows, 2), in_specs=in_specs, out_specs=[cur(0), cur(0)],
        out_shape=[jax.ShapeDtypeStruct((s_len, ATTN_WIDTH), f32)] * 2, input_output_aliases=aliases,
        compiler_params=_cparams(("parallel", "parallel")),
    )(*args)


def _stack_heads(x, head0):
    return jnp.concatenate([_mx(jnp.where(head0, x, 0.0)), _mx(jnp.where(head0, 0.0, x))], axis=0)


def _head_rows(x):
    xt = x.T
    return jnp.concatenate([jnp.broadcast_to(xt[0:1, :], (BLK, BLK)),
                            jnp.broadcast_to(xt[HEAD_DIM:HEAD_DIM + 1, :], (BLK, BLK))], axis=0)


def _attn_bwd(z, do, lse, dlt, tabs, dz, g, name):
    s_len = z.shape[0]
    dil, m = DILATIONS[g], ATTN_SUPER_PER_STEP[g]
    sbr = BLK * dil
    rows = sbr * m
    nsteps = s_len // rows

    def body(q_ref, qn_ref, kc_ref, kp_ref, vc_ref, vp_ref, do_ref, don_ref, l_ref, ln_ref, d_ref, dn_ref,
             c_ref, s1_ref, s2_ref, dz_in, dz_ref, dq_buf, dk_buf, dv_buf, out_buf, sems):
        del dz_in
        st, hp = pl.program_id(0), pl.program_id(1)
        head0 = lax.broadcasted_iota(jnp.int32, (BLK, 128), 1) < HEAD_DIM
        key_i = lax.broadcasted_iota(jnp.int32, (2 * BLK, BLK), 0) & (BLK - 1)
        query_i = lax.broadcasted_iota(jnp.int32, (2 * BLK, BLK), 1)
        same_t, cross_t = query_i >= key_i, key_i >= query_i
        for sb in range(m):
            prev_t = cross_t & (st > 0) if sb == 0 else cross_t
            next_t = cross_t & (st < nsteps - 1) if sb == m - 1 else cross_t

            def one_residue(r, carry, sb=sb, prev_t=prev_t, next_t=next_t):
                base = sb * sbr + r
                q, k, v = _rows(q_ref, base, dil), _rows(kc_ref, base, dil), _rows(vc_ref, base, dil)
                do_c, l_c, d_c = _rows(do_ref, base, dil), _rows(l_ref, base, dil), _rows(d_ref, base, dil)
                if sb == 0:
                    kp, vp = _rows(kp_ref, r, dil), _rows(vp_ref, r, dil)
                else:
                    kp, vp = _rows(kc_ref, base - sbr, dil), _rows(vc_ref, base - sbr, dil)
                if sb == m - 1:
                    qn, do_n = _rows(qn_ref, r, dil), _rows(don_ref, r, dil)
                    l_n, d_n = _rows(ln_ref, r, dil), _rows(dn_ref, r, dil)
                else:
                    qn, do_n = _rows(q_ref, base + sbr, dil), _rows(do_ref, base + sbr, dil)
                    l_n, d_n = _rows(l_ref, base + sbr, dil), _rows(d_ref, base + sbr, dil)
                k2, kp2, v2, vp2 = _stack_heads(k, head0), _stack_heads(kp, head0), _stack_heads(v, head0), _stack_heads(vp, head0)
                qb, qnb, dob, donb = _mx(q), _mx(qn), _mx(do_c), _mx(do_n)
                lse2, dlt2, lsen2, dltn2 = _head_rows(l_c), _head_rows(d_c), _head_rows(l_n), _head_rows(d_n)

                def pair(keys, vals, qs, dos, lse_rows, dlt_rows, valid):
                    p = jnp.where(valid, jnp.exp(_dot_nt(keys, qs) * ATTN_SCALE - lse_rows), 0.0)
                    ds = _mx(p * (_dot_nt(vals, dos) - dlt_rows) * ATTN_SCALE)
                    return _mx(p), ds

                p_a, ds_a = pair(k2, v2, qb, dob, lse2, dlt2, same_t)
                _, ds_b = pair(kp2, vp2, qb, dob, lse2, dlt2, prev_t)
                p_c, ds_c = pair(k2, v2, qnb, donb, lsen2, dltn2, next_t)
                dq = _dot_tn(ds_a, k2) + _dot_tn(ds_b, kp2)
                dk2 = _dot(ds_a, qb) + _dot(ds_c, qnb)
                dv2 = _dot(p_a, dob) + _dot(p_c, donb)
                c, s1, s2 = _rows(c_ref, base, dil), _rows(s1_ref, base, dil), _rows(s2_ref, base, dil)
                _set_rows(dq_buf, base, dil, _rope_transpose(dq, c, s1, s2, 128))
                _set_rows(dk_buf, base, dil, _rope_transpose(jnp.where(head0, dk2[:BLK], dk2[BLK:]), c, s1, s2, 128))
                _set_rows(dv_buf, base, dil, jnp.where(head0, dv2[:BLK], dv2[BLK:]))
                return carry

            _for_residues(dil, one_residue)
        copies = []
        for t, (buf, col) in enumerate(((dq_buf, Q_COL), (dk_buf, K_COL), (dv_buf, V_COL))):
            out_buf[t] = buf[...].astype(out_buf.dtype)
            lane0 = pl.multiple_of((col + 2 * g + hp) * 128, 128)
            dst = dz_ref.at[pl.ds(pl.multiple_of(st * rows, rows), rows), pl.ds(lane0, 128)]
            cp = pltpu.make_async_copy(out_buf.at[t], dst, sems.at[t])
            cp.start()
            copies.append(cp)
        for cp in copies:
            cp.wait()

    def cur(col):
        return pl.BlockSpec((rows, 128), lambda st, hp: (st, col + 2 * g + hp))

    def before(col):
        return pl.BlockSpec((sbr, 128), lambda st, hp: (jnp.maximum(st * m - 1, 0), col + 2 * g + hp))

    def after(col):
        return pl.BlockSpec((sbr, 128), lambda st, hp: (jnp.minimum((st + 1) * m, s_len // sbr - 1), col + 2 * g + hp))

    tab = pl.BlockSpec((rows, 128), lambda st, hp: (st, 0))
    return pl.pallas_call(
        body, name=name, grid=(nsteps, 2),
        in_specs=[cur(Q_COL), after(Q_COL), cur(K_COL), before(K_COL), cur(V_COL), before(V_COL),
                  cur(0), after(0), cur(0), after(0), cur(0), after(0), tab, tab, tab,
                  pl.BlockSpec(memory_space=pl.ANY)],
        out_specs=pl.BlockSpec(memory_space=pl.ANY),
        out_shape=jax.ShapeDtypeStruct(dz.shape, dz.dtype), input_output_aliases={15: 0},
        scratch_shapes=[pltpu.VMEM((rows, 128), f32)] * 3 + [pltpu.VMEM((3, rows, 128), dz.dtype),
                                                            pltpu.SemaphoreType.DMA((3,))],
        compiler_params=_cparams(("arbitrary", "arbitrary")),
    )(z, z, z, z, z, z, do, do, lse, lse, dlt, dlt, *tabs, dz)


def _attn_fwd_old(q, k, v, dil, name):
    s_len = q.shape[0]
    nblk = s_len // BLK

    def body(q_ref, kc_ref, kp_ref, vc_ref, vp_ref, o_ref, l_ref):
        b = pl.program_id(0)
        has_prev = b >= dil
        low, up = _tri_masks()
        valid = jnp.concatenate([up & has_prev, low], axis=1)
        outs, lses = [], []
        for hh in range(2):
            sl = slice(hh * HEAD_DIM, (hh + 1) * HEAD_DIM)
            qh = _mx(q_ref[:, sl])
            k2 = jnp.concatenate([_mx(kp_ref[:, sl]), _mx(kc_ref[:, sl])], axis=0)
            v2 = jnp.concatenate([_mx(vp_ref[:, sl]), _mx(vc_ref[:, sl])], axis=0)
            s = jnp.where(valid, _dot_nt(qh, k2) * ATTN_SCALE, NEG_BIG)
            m = jnp.max(s, axis=-1, keepdims=True)
            e = jnp.exp(s - m)
            l = jnp.sum(e, axis=-1, keepdims=True)
            outs.append(_dot(_mx(e / l), v2))
            lses.append(jnp.broadcast_to(m + jnp.log(l), (BLK, HEAD_DIM)))
        o_ref[...] = jnp.concatenate(outs, axis=1)
        l_ref[...] = jnp.concatenate(lses, axis=1)

    cur = pl.BlockSpec((BLK, 128), lambda b, hp: (b, hp))
    prev = pl.BlockSpec((BLK, 128), lambda b, hp: (jnp.maximum(b - dil, 0), hp))
    return pl.pallas_call(
        body, name=name, grid=(nblk, 2), in_specs=[cur, cur, prev, cur, prev], out_specs=[cur, cur],
        out_shape=[jax.ShapeDtypeStruct((s_len, 256), f32)] * 2,
        compiler_params=_cparams(("parallel", "parallel")),
    )(q, k, k, v, v)


def _attn_bwd_old(q, k, v, do, lse, dlt, tabs, dil, name):
    s_len = q.shape[0]
    nblk = s_len // BLK

    def body(q_ref, qn_ref, kc_ref, kp_ref, vc_ref, vp_ref, do_ref, don_ref, l_ref, ln_ref, d_ref, dn_ref,
             c_ref, s1_ref, s2_ref, dq_ref, dk_ref, dv_ref):
        b = pl.program_id(0)
        has_prev = b >= dil
        has_next = b + dil < nblk
        low, up = _tri_masks()
        dqs, dks, dvs = [], [], []
        for hh in range(2):
            sl = slice(hh * HEAD_DIM, (hh + 1) * HEAD_DIM)
            one = slice(hh * HEAD_DIM, hh * HEAD_DIM + 1)
            qc, qn = _mx(q_ref[:, sl]), _mx(qn_ref[:, sl])
            kc, kp = _mx(kc_ref[:, sl]), _mx(kp_ref[:, sl])
            vc, vp = _mx(vc_ref[:, sl]), _mx(vp_ref[:, sl])
            doc, don = _mx(do_ref[:, sl]), _mx(don_ref[:, sl])
            lc, ln = l_ref[:, one], ln_ref[:, one]
            dc, dn = d_ref[:, one], dn_ref[:, one]
            p_a = jnp.where(low, jnp.exp(_dot_nt(qc, kc) * ATTN_SCALE - lc), 0.0)
            ds_a = _mx(p_a * (_dot_nt(doc, vc) - dc) * ATTN_SCALE)
            p_b = jnp.where(up & has_prev, jnp.exp(_dot_nt(qc, kp) * ATTN_SCALE - lc), 0.0)
            ds_b = _mx(p_b * (_dot_nt(doc, vp) - dc) * ATTN_SCALE)
            p_c = jnp.where(up & has_next, jnp.exp(_dot_nt(qn, kc) * ATTN_SCALE - ln), 0.0)
            ds_c = _mx(p_c * (_dot_nt(don, vc) - dn) * ATTN_SCALE)
            dqs.append(_dot(ds_a, kc) + _dot(ds_b, kp))
            dks.append(_dot_tn(ds_a, qc) + _dot_tn(ds_c, qn))
            dvs.append(_dot_tn(_mx(p_a), doc) + _dot_tn(_mx(p_c), don))
        c, s1, s2 = c_ref[...], s1_ref[...], s2_ref[...]
        dq_ref[...] = _rope_transpose(jnp.concatenate(dqs, axis=1), c, s1, s2, 128)
        dk_ref[...] = _rope_transpose(jnp.concatenate(dks, axis=1), c, s1, s2, 128)
        dv_ref[...] = jnp.concatenate(dvs, axis=1)

    cur = pl.BlockSpec((BLK, 128), lambda b, hp: (b, hp))
    prev = pl.BlockSpec((BLK, 128), lambda b, hp: (jnp.maximum(b - dil, 0), hp))
    nxt = pl.BlockSpec((BLK, 128), lambda b, hp: (jnp.minimum(b + dil, nblk - 1), hp))
    tab = pl.BlockSpec((BLK, 128), lambda b, hp: (b, 0))
    return pl.pallas_call(
        body, name=name, grid=(nblk, 2),
        in_specs=[cur, nxt, cur, prev, cur, prev, cur, nxt, cur, nxt, cur, nxt, tab, tab, tab],
        out_specs=[cur, cur, cur], out_shape=[jax.ShapeDtypeStruct((s_len, 256), f32)] * 3,
        compiler_params=_cparams(("parallel", "parallel")),
    )(q, q, k, k, v, v, do, do, lse, lse, dlt, dlt, *tabs)


def _loss_head(h, g, target, name):
    s_len, d = h.shape

    def body(h_ref, g_ref, t_ref, loss_ref, dh_ref, dg_ref):
        i = pl.program_id(0)
        x = h_ref[...]
        gv = g_ref[...]
        r = lax.rsqrt(jnp.mean(x * x, axis=-1, keepdims=True) + EPS)
        xh = x * r
        diff = xh * gv - t_ref[...]
        part = 0.5 * jnp.sum(jnp.mean(diff * diff, axis=-1, keepdims=True), axis=0, keepdims=True)
        dy = diff * (1.0 / d)
        dxh = dy * gv
        dh_ref[...] = r * (dxh - xh * jnp.mean(dxh * xh, axis=-1, keepdims=True))
        dgsum = jnp.sum(dy * xh, axis=0, keepdims=True)
        lossb = jnp.broadcast_to(part, (8, 128))

        @pl.when(i == 0)
        def _():
            loss_ref[...] = lossb
            dg_ref[...] = dgsum

        @pl.when(i > 0)
        def _():
            loss_ref[...] += lossb
            dg_ref[...] += dgsum

    row = lambda i: (i, 0)
    return pl.pallas_call(
        body, name=name, grid=(s_len // TM,),
        in_specs=[pl.BlockSpec((TM, d), row), pl.BlockSpec((1, d), lambda i: (0, 0)), pl.BlockSpec((TM, d), row)],
        out_specs=[pl.BlockSpec((8, 128), lambda i: (0, 0)), pl.BlockSpec((TM, d), row),
                   pl.BlockSpec((1, d), lambda i: (0, 0))],
        out_shape=[jax.ShapeDtypeStruct((8, 128), f32), jax.ShapeDtypeStruct((s_len, d), f32),
                   jax.ShapeDtypeStruct((1, d), f32)],
        compiler_params=_cparams(("arbitrary",)),
    )(h, g, target)


def _rope_tables(positions):
    inv_freq = ROPE_THETA ** (-jnp.arange(0, ROT_DIM, 2, dtype=f32) / ROT_DIM)
    ang = positions.astype(f32)[:, None] * inv_freq
    cos, sin = jnp.cos(ang), jnp.sin(ang)
    s_len = positions.shape[0]
    zero8, rest = jnp.zeros((s_len, 8), f32), jnp.zeros((s_len, HEAD_DIM - ROT_DIM), f32)
    c = jnp.concatenate([cos, cos, jnp.ones((s_len, HEAD_DIM - ROT_DIM), f32)], axis=1)
    s1 = jnp.concatenate([-sin, zero8, rest], axis=1)
    s2 = jnp.concatenate([zero8, sin, rest], axis=1)
    return c, s1, s2


def _block_diag(pool_w):
    out = jnp.zeros((POOL_WIDTH, POOL_WIDTH), pool_w.dtype)
    for g in range(4):
        out = lax.dynamic_update_slice(out, pool_w[g], (g * POOL_GC, g * POOL_GC))
    return out


class _ReadyWeights:
    def __init__(self, full):
        self.full = full

    def take(self, layer, names, after):
        del after
        return {n: self.full[n] for n in names}, layer


def _layer_fwd(h, p_l, wsrc, small, layer, tabs):
    nm = f"l{layer}_"
    wts, wl = wsrc.take(layer, ("w_in",), (h,) if layer else tuple(tabs))
    z, hn1 = _norm_matmul(h, small["norm1"][layer][None], wts["w_in"], wl, 256, nm + "in_proj", rope=tabs)
    ol = None
    for g in range(3):
        ol = _attn_fwd(z, g, ol, nm + f"attn_fwd{g}")
    outs, lses = ol
    wbd = _mx(_block_diag(small["pool_w"][layer]))
    scale = small["pool_scale"][layer][None]
    m = _mixer_merge(z, wbd, scale, outs, lses, nm + "mixer_merge")
    wts.update(wsrc.take(layer, ("w_out",), (m,))[0])
    h1 = _matmul_residual(m, wts["w_out"], wl, h, nm + "out_proj")
    wts.update(wsrc.take(layer, ("w_up", "w_down"), (h1,))[0])
    h2, a, hn2 = _mlp_fwd(h1, small["norm2"][layer][None], wts["w_up"], wts["w_down"], wl, nm + "mlp")
    wts.update(wsrc.take(layer, ("w_gate", "w_ple"), (h2,))[0])
    h3, gl, hn3 = _gate_ple_fwd(h2, small["norm3"][layer][None], wts["w_gate"], wts["w_ple"], wl, p_l, nm + "gate_ple")
    saved = dict(h=h, z=z, hn1=hn1, outs=outs, lses=lses, wbd=wbd, scale=scale, m=m, h1=h1, a=a, hn2=hn2, h2=h2,
                 gl=gl, hn3=hn3, wts=wts, wl=wl)
    return h3, saved


def _layer_bwd(dh3, sv, p_l, small, layer, tabs128, reducer):
    nm = f"l{layer}_"
    wts, wl = sv["wts"], sv["wl"]
    dh2, dg3, de, dgl = _gate_bwd(dh3, sv["gl"], p_l, wts["w_ple"], wts["w_gate"], wl, sv["h2"],
                                  small["norm3"][layer][None], nm + "gate_bwd")
    reducer.add("w_gate", layer, _weight_grad(sv["hn3"], dgl, nm + "dw_gate"))
    reducer.add("w_ple", layer, _weight_grad(p_l, de, nm + "dw_ple"))
    da = _down_bwd(dh2, wts["w_down"], wl, sv["a"], nm + "down_bwd")
    reducer.add("w_down", layer, _weight_grad(sv["a"], dh2, nm + "dw_down", act=True))
    started = reducer.add("w_up", layer, _weight_grad(sv["hn2"], da, nm + "dw_up"))
    dh1, dg2 = _matmul_nt_norm_bwd(da, wts["w_up"], wl, sv["h1"], small["norm2"][layer][None], dh2, nm + "up_bwd",
                                   after=started)
    dm = _matmul_nt(dh1, wts["w_out"], wl, nm + "out_bwd")
    reducer.add("w_out", layer, _weight_grad(sv["m"], dh1, nm + "dw_out"))
    do, dlt = _combine_bwd(dm, sv["outs"], sv["lses"], nm + "combine_bwd")
    dz, dwbd, dscale = _pool_bwd(sv["z"], dm, sv["wbd"], sv["scale"], nm + "pool_bwd")
    for g in range(3):
        dz = _attn_bwd(sv["z"], do, sv["lses"], dlt, tabs128, dz, g, nm + f"attn_bwd{g}")
    started = reducer.add("w_in", layer, _weight_grad(sv["hn1"], dz, nm + "dw_in"))
    dh0, dg1 = _matmul_nt_norm_bwd(dz, wts["w_in"], wl, sv["h"], small["norm1"][layer][None], dh1, nm + "in_bwd",
                                   tk=512, after=started)
    dpool_w = jnp.stack([dwbd[g * POOL_GC:(g + 1) * POOL_GC, g * POOL_GC:(g + 1) * POOL_GC] for g in range(4)])
    sg = dict(norm1=dg1[0], norm2=dg2[0], norm3=dg3[0], pool_w=dpool_w, pool_scale=dscale[0])
    return dh0, sg


class _CollectGrads:
    def __init__(self):
        self.grads = {}

    def add(self, name, layer, dw):
        self.grads[(name, layer)] = dw


def _local_step(x, p, positions, wsrc, small, target, reducer):
    tabs128 = tuple(jnp.tile(t, (1, 2)) for t in _rope_tables(positions))
    h = x
    saved = []
    for layer in range(2):
        h, sv = _layer_fwd(h, p[layer], wsrc, small, layer, tabs128)
        saved.append(sv)
    loss, dh, dgf = _loss_head(h, small["final_norm"][None], target, "loss_head")
    sgs = [None, None]
    for layer in (1, 0):
        dh, sgs[layer] = _layer_bwd(dh, saved[layer], p[layer], small, layer, tabs128, reducer)
    small_grads = {k: jnp.stack([sgs[0][k], sgs[1][k]]) for k in sgs[0]}
    small_grads["final_norm"] = dgf[0]
    return loss, dh, small_grads


HBM = pl.BlockSpec(memory_space=pltpu.HBM)


def _my_place():
    return lax.axis_index("x"), lax.axis_index("y"), lax.axis_index("c")


def _other_chips(x, y):
    return [(1 - x, y), (x, 1 - y), (1 - x, 1 - y)]


def _window(ref, name, chip):
    k, n = _shard_shape(name)
    if COL_SHARDED[name]:
        return ref.at[:, pl.ds(pl.multiple_of(chip * n, 128), n)]
    return ref.at[pl.ds(pl.multiple_of(chip * k, 128), k), :]


def _chip_index():
    return jnp.reshape(2 * lax.axis_index("x") + lax.axis_index("y"), (1,)).astype(jnp.int32)


def _shard_block(name, tr):
    ks, ns = _shard_shape(name)
    if COL_SHARDED[name]:
        return (tr, ns), lambda i, me: (i, me[0])
    return (tr, ns), lambda i, me: (me[0] * (ks // tr) + i, 0)


def _place_shard(w, name, layer):
    ks, ns = _shard_shape(name)
    tr = min(ks, 256)
    shape, index = _shard_block(name, tr)

    def body(me_ref, w_ref, o_ref):
        o_ref[...] = w_ref[...].astype(o_ref.dtype)

    return pl.pallas_call(
        body, name=f"place_{name}{layer}",
        grid_spec=pltpu.PrefetchScalarGridSpec(
            num_scalar_prefetch=1, grid=(ks // tr,),
            in_specs=[pl.BlockSpec((None, tr, ns), lambda i, me: (layer, i, 0))],
            out_specs=pl.BlockSpec((None,) + shape, lambda i, me: (0,) + index(i, me))),
        out_shape=jax.ShapeDtypeStruct((1,) + FULL_SHAPE[name], MXU_DTYPE),
        compiler_params=_cparams(("parallel",)),
    )(_chip_index(), w)


GATHER_ORDER = [("w_in", 0), ("w_out", 0), ("w_up", 0), ("w_down", 0), ("w_gate", 0), ("w_ple", 0),
                ("w_in", 1), ("w_out", 1), ("w_up", 1), ("w_down", 1), ("w_gate", 1), ("w_ple", 1)]
SEM = pl.BlockSpec(memory_space=pltpu.SEMAPHORE)
EFFECT = pltpu.SideEffectType.DATAFLOW_SIDE_EFFECTING


def _gather_copy(src_ref, dst_ref, name, idx, j, chip, send_sems, recv_sems, c):
    cx, cy = chip
    return pltpu.make_async_remote_copy(
        src_ref=src_ref, dst_ref=dst_ref, send_sem=send_sems.at[3 * idx + j], recv_sem=recv_sems.at[3 * idx + j],
        device_id=(cx, cy, c), device_id_type=MESH)


def _gather_start(placed, order, tag, after=None):
    n = len(order)
    extra = [] if after is None else [after]

    def body(*refs):
        ins = refs[:n]
        k = n + len(extra)
        send_sems, recv_sems = refs[k], refs[k + 1]
        outs = refs[k + 2:k + 2 + n]
        token = refs[-1]
        x, y, c = _my_place()
        me = 2 * x + y
        for idx, (name, _) in enumerate(order):
            for j, chip in enumerate(_other_chips(x, y)):
                _gather_copy(_window(ins[idx].at[0], name, me), _window(outs[idx].at[0], name, me), name, idx, j, chip,
                             send_sems, recv_sems, c).start()
        token[...] = jnp.zeros_like(token)

    res = pl.pallas_call(
        body, name="gather_start" + tag,
        out_shape=(pltpu.SemaphoreType.DMA((3 * n,)), pltpu.SemaphoreType.DMA((3 * n,)))
        + tuple(pltpu.HBM(a.shape, a.dtype) for a in placed) + (jax.ShapeDtypeStruct((8, 128), f32),),
        in_specs=[HBM] * n + [pl.BlockSpec(memory_space=pl.ANY)] * len(extra),
        out_specs=(SEM, SEM) + (HBM,) * n + (pl.BlockSpec(memory_space=pltpu.VMEM),),
        input_output_aliases={i: i + 2 for i in range(n)},
        compiler_params=pltpu.CompilerParams(has_side_effects=EFFECT),
    )(*[pltpu.with_memory_space_constraint(a, pltpu.HBM) for a in placed], *extra)
    return res[0], res[1], list(res[2:2 + n]), res[-1]


def _gather_wait(send_sems, recv_sems, arrays, order, idxs, after, name):
    n = len(idxs)

    def body(*refs):
        ins = refs[:n]
        send_ref, recv_ref = refs[n], refs[n + 1]
        x, y, c = _my_place()
        me = 2 * x + y
        for k, idx in enumerate(idxs):
            wname = order[idx][0]
            for j, chip in enumerate(_other_chips(x, y)):
                cx, cy = chip
                mine = _window(ins[k].at[0], wname, me)
                land = _window(ins[k].at[0], wname, 2 * cx + cy)
                _gather_copy(mine, mine, wname, idx, j, chip, send_ref, recv_ref, c).wait_send()
                _gather_copy(land, land, wname, idx, j, chip, send_ref, recv_ref, c).wait_recv()

    operands = list(arrays) + [send_sems, recv_sems] + list(after)
    in_specs = [HBM] * n + [SEM, SEM] + [pl.BlockSpec(memory_space=pl.ANY)] * len(after)
    res = pl.pallas_call(
        body, name=name, out_shape=tuple(pltpu.HBM(a.shape, a.dtype) for a in arrays),
        in_specs=in_specs, out_specs=(HBM,) * n, input_output_aliases={i: i for i in range(n)},
        compiler_params=pltpu.CompilerParams(has_side_effects=EFFECT),
    )(*operands)
    return list(res)


class _GatheredWeights:
    def __init__(self, shards):
        self.starts = []
        token = None
        for tag, order in (("_first", GATHER_ORDER[:1]), ("_rest", GATHER_ORDER[1:])):
            placed = [_place_shard(shards[name], name, layer) for name, layer in order]
            self.starts.append((order,) + _gather_start(placed, order, tag, token))
            token = self.starts[-1][-1]

    def take(self, layer, names, after):
        order, send, recv, arrays, _ = next(s for s in self.starts if (names[0], layer) in s[0])
        after = list(after)
        if order is self.starts[0][0]:
            after.append(self.starts[-1][-1])
        idxs = [order.index((n, layer)) for n in names]
        got = _gather_wait(send, recv, [arrays[i] for i in idxs], order, idxs, after, f"gather_wait{layer}_{names[0]}")
        return dict(zip(names, got)), 0


def _gather_weights(full):
    names = list(BIG)

    def body(*refs):
        ins = refs[:len(names)]
        outs = refs[len(names):2 * len(names)]
        send_ici, recv_ici, send_d2d, recv_d2d = refs[2 * len(names):]
        x, y, c = _my_place()
        me = 2 * x + y
        sibling = (x, y, 1 - c)
        chips = _other_chips(x, y)
        ici = []
        for t, name in enumerate(names):
            for j, (cx, cy) in enumerate(chips):
                cp = pltpu.make_async_remote_copy(
                    src_ref=_window(ins[t].at[c], name, me), dst_ref=_window(outs[t].at[c], name, me),
                    send_sem=send_ici.at[3 * t + j], recv_sem=recv_ici.at[3 * t + j],
                    device_id=(cx, cy, c), device_id_type=MESH)
                cp.start()
                ici.append(cp)
        fwd = []
        for t, name in enumerate(names):
            for j, (cx, cy) in enumerate(chips):
                land = _window(outs[t].at[c], name, 2 * cx + cy)
                pltpu.make_async_remote_copy(
                    src_ref=land, dst_ref=land, send_sem=send_ici.at[3 * t + j], recv_sem=recv_ici.at[3 * t + j],
                    device_id=(cx, cy, c), device_id_type=MESH).wait_recv()
                cp = pltpu.make_async_remote_copy(
                    src_ref=land, dst_ref=land, send_sem=send_d2d.at[3 * t + j], recv_sem=recv_d2d.at[3 * t + j],
                    device_id=sibling, device_id_type=MESH)
                cp.start()
                fwd.append(cp)
        for t, name in enumerate(names):
            for j, (cx, cy) in enumerate(chips):
                land = _window(outs[t].at[1 - c], name, 2 * cx + cy)
                pltpu.make_async_remote_copy(
                    src_ref=land, dst_ref=land, send_sem=send_d2d.at[3 * t + j], recv_sem=recv_d2d.at[3 * t + j],
                    device_id=sibling, device_id_type=MESH).wait_recv()
        for cp in ici + fwd:
            cp.wait_send()

    nsem = 3 * len(names)
    outs = pl.pallas_call(
        body, name="gather_weights",
        in_specs=[HBM] * len(names), out_specs=[HBM] * len(names),
        out_shape=[jax.ShapeDtypeStruct(full[n].shape, full[n].dtype) for n in names],
        input_output_aliases={t: t for t in range(len(names))},
        scratch_shapes=[pltpu.SemaphoreType.DMA((nsem,)), pltpu.SemaphoreType.DMA((nsem,)),
                        pltpu.SemaphoreType.DMA((nsem,)), pltpu.SemaphoreType.DMA((nsem,))],
    )(*[full[n] for n in names])
    return dict(zip(names, outs))


def _swap_layers(grads):
    names = list(BIG)

    def body(*refs):
        ins = refs[:len(names)]
        outs = refs[len(names):2 * len(names)]
        send_sems, recv_sems = refs[2 * len(names):]
        x, y, c = _my_place()
        sibling = (x, y, 1 - c)
        cps = []
        for t in range(len(names)):
            cp = pltpu.make_async_remote_copy(
                src_ref=ins[t].at[1 - c], dst_ref=outs[t], send_sem=send_sems.at[t], recv_sem=recv_sems.at[t],
                device_id=sibling, device_id_type=MESH)
            cp.start()
            cps.append(cp)
        for cp in cps:
            cp.wait()

    outs = pl.pallas_call(
        body, name="swap_layers", in_specs=[HBM] * len(names), out_specs=[HBM] * len(names),
        out_shape=[jax.ShapeDtypeStruct(FULL_SHAPE[n], f32) for n in names],
        scratch_shapes=[pltpu.SemaphoreType.DMA((len(names),)), pltpu.SemaphoreType.DMA((len(names),))],
    )(*[grads[n] for n in names])
    return dict(zip(names, outs))


def _chip_sum(grad, other, name):
    k, n = FULL_SHAPE[name]
    tr = min(k, 512)
    c = lax.axis_index("c")

    def body(c_ref, g_ref, o_ref, out_ref):
        out_ref[...] = (g_ref[...] + o_ref[...]).astype(out_ref.dtype)

    return pl.pallas_call(
        body, name="chip_sum_" + name,
        grid_spec=pltpu.PrefetchScalarGridSpec(
            num_scalar_prefetch=1, grid=(k // tr,),
            in_specs=[pl.BlockSpec((None, tr, n), lambda i, c_ref: (c_ref[0], i, 0)),
                      pl.BlockSpec((tr, n), lambda i, c_ref: (i, 0))],
            out_specs=pl.BlockSpec((tr, n), lambda i, c_ref: (i, 0))),
        out_shape=jax.ShapeDtypeStruct((k, n), COMM_DTYPE),
        compiler_params=_cparams(("parallel",)),
    )(jnp.reshape(c, (1,)).astype(jnp.int32), grad, other)


def _scatter_shards(sums):
    names = list(BIG)

    def body(*refs):
        ins = refs[:len(names)]
        outs = refs[len(names):2 * len(names)]
        send_sems, recv_sems = refs[2 * len(names):]
        x, y, c = _my_place()
        me = 2 * x + y
        chips = _other_chips(x, y)
        cps = []
        for t, name in enumerate(names):
            for j, (cx, cy) in enumerate(chips):
                cp = pltpu.make_async_remote_copy(
                    src_ref=_window(ins[t], name, 2 * cx + cy), dst_ref=outs[t].at[me],
                    send_sem=send_sems.at[3 * t + j], recv_sem=recv_sems.at[3 * t + j],
                    device_id=(cx, cy, c), device_id_type=MESH)
                cp.start()
                cps.append(cp)
        for t, name in enumerate(names):
            for j, (cx, cy) in enumerate(chips):
                land = outs[t].at[2 * cx + cy]
                pltpu.make_async_remote_copy(
                    src_ref=land, dst_ref=land, send_sem=send_sems.at[3 * t + j], recv_sem=recv_sems.at[3 * t + j],
                    device_id=(cx, cy, c), device_id_type=MESH).wait_recv()
        for cp in cps:
            cp.wait_send()

    nsem = 3 * len(names)
    outs = pl.pallas_call(
        body, name="scatter_shards", in_specs=[HBM] * len(names), out_specs=[HBM] * len(names),
        out_shape=[jax.ShapeDtypeStruct((N_CHIPS,) + _shard_shape(n), sums[n].dtype) for n in names],
        scratch_shapes=[pltpu.SemaphoreType.DMA((nsem,)), pltpu.SemaphoreType.DMA((nsem,))],
    )(*[sums[n] for n in names])
    return dict(zip(names, outs))


def _sum_slots(slots, own, name):
    ks, ns = _shard_shape(name)
    tr = min(ks, 256)
    shape, index = _shard_block(name, tr)

    def body(me_ref, c_ref, s_ref, own_ref, out_ref):
        me = me_ref[0]
        acc = None
        for s in range(N_CHIPS):
            term = jnp.where(me == s, own_ref[...], s_ref[s]).astype(f32)
            acc = term if acc is None else acc + term
        out_ref[...] = acc

    return pl.pallas_call(
        body, name="sum_slots_" + name,
        grid_spec=pltpu.PrefetchScalarGridSpec(
            num_scalar_prefetch=2, grid=(ks // tr,),
            in_specs=[pl.BlockSpec((N_CHIPS, tr, ns), lambda i, me, c: (0, i, 0)),
                      pl.BlockSpec(shape, lambda i, me, c: index(i, me))],
            out_specs=pl.BlockSpec((None, tr, ns), lambda i, me, c: (c[0], i, 0))),
        out_shape=jax.ShapeDtypeStruct((2, ks, ns), f32),
        compiler_params=_cparams(("parallel",)),
    )(_chip_index(), jnp.reshape(lax.axis_index("c"), (1,)).astype(jnp.int32), slots, own)


N_DEV = 8


def _reduce_copies(dws, lands, names, layer, send_sems, recv_sems):
    x, y, c = _my_place()
    me, my_dev = 2 * x + y, 4 * x + 2 * y + c
    out = []
    for t, name in enumerate(names):
        for j, (cx, cy) in enumerate(_other_chips(x, y)):
            out.append((pltpu.make_async_remote_copy(
                src_ref=_window(dws[t], name, 2 * cx + cy), dst_ref=lands[t].at[my_dev],
                send_sem=send_sems.at[4 * t + j], recv_sem=recv_sems.at[N_DEV * t + my_dev],
                device_id=(cx, cy, layer), device_id_type=MESH), False))
        out.append((pltpu.make_async_remote_copy(
            src_ref=_window(dws[t], name, me), dst_ref=lands[t].at[my_dev],
            send_sem=send_sems.at[4 * t + 3], recv_sem=recv_sems.at[N_DEV * t + my_dev],
            device_id=(x, y, layer), device_id_type=MESH), True))
    return out


def _reduce_start(dws, names, layer, tag):
    n = len(names)
    lands = [lax.empty((N_DEV,) + _shard_shape(nm), dws[0].dtype) for nm in names]

    def body(*refs):
        ins = refs[:n]
        send_sems, recv_sems = refs[2 * n], refs[2 * n + 1]
        land_out = refs[3 * n + 2:4 * n + 2]
        token = refs[-1]
        c = lax.axis_index("c")
        for cp, non_owner_only in _reduce_copies(ins, land_out, names, layer, send_sems, recv_sems):
            if non_owner_only:
                @pl.when(c != layer)
                def _():
                    cp.start()
            else:
                cp.start()
        token[...] = jnp.zeros_like(token)

    res = pl.pallas_call(
        body, name="reduce_start" + tag,
        out_shape=(pltpu.SemaphoreType.DMA((4 * n,)), pltpu.SemaphoreType.DMA((N_DEV * n,)))
        + tuple(pltpu.HBM(a.shape, a.dtype) for a in dws) + tuple(pltpu.HBM(a.shape, a.dtype) for a in lands)
        + (jax.ShapeDtypeStruct((8, 128), f32),),
        in_specs=[HBM] * (2 * n),
        out_specs=(SEM, SEM) + (HBM,) * (2 * n) + (pl.BlockSpec(memory_space=pltpu.VMEM),),
        input_output_aliases={i: i + 2 for i in range(2 * n)},
        compiler_params=pltpu.CompilerParams(has_side_effects=EFFECT),
    )(*[pltpu.with_memory_space_constraint(a, pltpu.HBM) for a in list(dws) + lands])
    return res[0], res[1], list(res[2:2 + n]), list(res[2 + n:2 + 2 * n]), res[-1]


def _reduce_wait(send_sems, recv_sems, dws, lands, names, layer, after, tag):
    n = len(names)

    def body(*refs):
        ins, land_in = refs[:n], refs[n:2 * n]
        send_ref, recv_ref = refs[2 * n], refs[2 * n + 1]
        x, y, c = _my_place()
        for cp, non_owner_only in _reduce_copies(ins, land_in, names, layer, send_ref, recv_ref):
            if non_owner_only:
                @pl.when(c != layer)
                def _():
                    cp.wait_send()
            else:
                cp.wait_send()

        @pl.when(c == layer)
        def _():
            for t in range(n):
                for k in range(1, N_DEV):
                    px, py, pc = x ^ ((k >> 2) & 1), y ^ ((k >> 1) & 1), c ^ (k & 1)
                    dev = 4 * px + 2 * py + pc
                    land = land_in[t].at[dev]
                    pltpu.make_async_remote_copy(
                        src_ref=land, dst_ref=land, send_sem=send_ref.at[4 * t], recv_sem=recv_ref.at[N_DEV * t + dev],
                        device_id=(px, py, pc), device_id_type=MESH).wait_recv()

    res = pl.pallas_call(
        body, name="reduce_wait" + tag,
        out_shape=tuple(pltpu.HBM(a.shape, a.dtype) for a in list(dws) + list(lands)),
        in_specs=[HBM] * (2 * n) + [SEM, SEM, pl.BlockSpec(memory_space=pl.ANY)], out_specs=(HBM,) * (2 * n),
        input_output_aliases={i: i for i in range(2 * n)},
        compiler_params=pltpu.CompilerParams(has_side_effects=EFFECT),
    )(*dws, *lands, send_sems, recv_sems, after)
    return list(res[:n]), list(res[n:])


def _sum_devices(land, own, name, layer, prev):
    ks, ns = _shard_shape(name)
    tr = min(ks, 256)
    shape, index = _shard_block(name, tr)

    def body(me_ref, dev_ref, *refs):
        s_ref, own_ref, out_ref = refs[0], refs[1], refs[-1]
        dev = dev_ref[0]
        acc = None
        for s in range(N_DEV):
            term = jnp.where(dev == s, own_ref[...], s_ref[s]).astype(f32)
            acc = term if acc is None else acc + term
        out_ref[...] = acc

    def mine(i, dev):
        return i * jnp.where((dev[0] & 1) == layer, 1, 0)

    in_specs = [pl.BlockSpec((N_DEV, tr, ns), lambda i, me, dev: (0, mine(i, dev), 0)),
                pl.BlockSpec(shape, lambda i, me, dev: index(mine(i, dev), me))]
    args = [land, own]
    aliases = {}
    if prev is not None:
        in_specs.append(pl.BlockSpec(memory_space=pl.ANY))
        args.append(prev)
        aliases = {4: 0}
    x, y, c = _my_place()
    return pl.pallas_call(
        body, name=f"sum_devices_{name}{layer}",
        grid_spec=pltpu.PrefetchScalarGridSpec(
            num_scalar_prefetch=2, grid=(ks // tr,), in_specs=in_specs,
            out_specs=pl.BlockSpec((None, tr, ns), lambda i, me, dev: (layer, mine(i, dev), 0))),
        out_shape=jax.ShapeDtypeStruct((2, ks, ns), f32), input_output_aliases=aliases,
        compiler_params=_cparams(("arbitrary",)),
    )(_chip_index(), jnp.reshape(4 * x + 2 * y + c, (1,)).astype(jnp.int32), *args)


class _GradReducer:
    GROUPS = (("1", 1, ("w_gate", "w_ple", "w_down", "w_up", "w_out", "w_in")),
              ("0a", 0, ("w_gate", "w_ple", "w_down", "w_up")),
              ("0b", 0, ("w_out", "w_in")))

    def __init__(self):
        self.grads = {}
        self.started = {}

    def add(self, name, layer, dw):
        self.grads[(name, layer)] = dw
        token = None
        for tag, glayer, names in self.GROUPS:
            if tag not in self.started and all((nm, glayer) in self.grads for nm in names):
                *self.started[tag], token = _reduce_start([self.grads[(nm, glayer)] for nm in names], names, glayer, tag)
        return token

    def finish(self, after):
        mine = {}
        for tag, layer, names in self.GROUPS:
            send, recv, dws, lands = self.started[tag]
            dws, lands = _reduce_wait(send, recv, dws, lands, names, layer, after, tag)
            for nm, dw, land in zip(names, dws, lands):
                mine[nm] = _sum_devices(land, dw, nm, layer, mine.get(nm))
        return _pair_layers(mine)


def _pair_layers(mine):
    names = list(BIG)

    def body(*refs):
        ins = refs[:len(names)]
        outs = refs[len(names):2 * len(names)]
        send_sems, recv_sems = refs[2 * len(names):]
        x, y, c = _my_place()
        sibling = (x, y, 1 - c)
        cps = []
        for t in range(len(names)):
            cp = pltpu.make_async_remote_copy(
                src_ref=ins[t].at[c], dst_ref=outs[t].at[c], send_sem=send_sems.at[t], recv_sem=recv_sems.at[t],
                device_id=sibling, device_id_type=MESH)
            cp.start()
            cps.append(cp)
        for t in range(len(names)):
            cps[t].wait_send()
            land = outs[t].at[1 - c]
            pltpu.make_async_remote_copy(
                src_ref=land, dst_ref=land, send_sem=send_sems.at[t], recv_sem=recv_sems.at[t],
                device_id=sibling, device_id_type=MESH).wait_recv()

    outs = pl.pallas_call(
        body, name="pair_layers", in_specs=[HBM] * len(names), out_specs=[HBM] * len(names),
        out_shape=[jax.ShapeDtypeStruct((2,) + _shard_shape(n), f32) for n in names],
        input_output_aliases={t: t for t in range(len(names))},
        scratch_shapes=[pltpu.SemaphoreType.DMA((len(names),)), pltpu.SemaphoreType.DMA((len(names),))],
    )(*[mine[n] for n in names])
    return dict(zip(names, outs))


SMALL_ROWS = 320


def _allreduce_small(vec):
    n_dev = 8

    def body(v_ref, out_ref, buf_ref, send_sems, recv_sems):
        x, y, c = _my_place()
        me = 4 * x + 2 * y + c
        buf_ref[me] = v_ref[...]
        cps = []
        for k in range(1, n_dev):
            dx, dy, dc = (k >> 2) & 1, (k >> 1) & 1, k & 1
            peer = (x ^ dx, y ^ dy, c ^ dc)
            cp = pltpu.make_async_remote_copy(
                src_ref=v_ref, dst_ref=buf_ref.at[me], send_sem=send_sems.at[k - 1], recv_sem=recv_sems.at[k - 1],
                device_id=peer, device_id_type=MESH)
            cp.start()
            cps.append(cp)
        for k in range(1, n_dev):
            dx, dy, dc = (k >> 2) & 1, (k >> 1) & 1, k & 1
            src = 4 * (x ^ dx) + 2 * (y ^ dy) + (c ^ dc)
            land = buf_ref.at[src]
            pltpu.make_async_remote_copy(
                src_ref=land, dst_ref=land, send_sem=send_sems.at[k - 1], recv_sem=recv_sems.at[k - 1],
                device_id=(x ^ dx, y ^ dy, c ^ dc), device_id_type=MESH).wait_recv()
        for cp in cps:
            cp.wait_send()
        acc = buf_ref[0]
        for s in range(1, n_dev):
            acc = acc + buf_ref[s]
        out_ref[...] = acc

    return pl.pallas_call(
        body, name="allreduce_small",
        in_specs=[pl.BlockSpec(memory_space=pltpu.VMEM)], out_specs=pl.BlockSpec(memory_space=pltpu.VMEM),
        out_shape=jax.ShapeDtypeStruct((SMALL_ROWS, 128), f32),
        scratch_shapes=[pltpu.VMEM((n_dev, SMALL_ROWS, 128), f32), pltpu.SemaphoreType.DMA((n_dev - 1,)),
                        pltpu.SemaphoreType.DMA((n_dev - 1,))],
    )(vec)


def _adamw(w, g, m, v, name):
    rows, cols = w.shape
    tr = rows
    for cand in (512, 256, 128, 64, 32, 16, 8):
        if rows % cand == 0 and cand * cols * 4 <= 2 * 1024 * 1024:
            tr = cand
            break
    c1 = np.float32(1.0 - ADAM_B1 ** ADAM_STEP)
    c2 = np.float32(1.0 - ADAM_B2 ** ADAM_STEP)

    def body(w_ref, g_ref, m_ref, v_ref, go_ref, d_ref, mo_ref, vo_ref):
        gv = g_ref[...]
        go_ref[...] = gv
        mn = ADAM_B1 * m_ref[...] + (1.0 - ADAM_B1) * gv
        vn = ADAM_B2 * v_ref[...] + (1.0 - ADAM_B2) * (gv * gv)
        mo_ref[...] = mn
        vo_ref[...] = vn
        d_ref[...] = -ADAM_LR * ((mn / c1) / (jnp.sqrt(vn / c2) + ADAM_EPS) + ADAM_WD * w_ref[...])

    blk = pl.BlockSpec((tr, cols), lambda i: (i, 0))
    return pl.pallas_call(
        body, name="adamw_" + name, grid=(rows // tr,), in_specs=[blk] * 4, out_specs=[blk] * 4,
        out_shape=[jax.ShapeDtypeStruct((rows, cols), f32)] * 4,
        compiler_params=_cparams(("parallel",)),
    )(w, g, m, v)


SMALL = ("norm1", "pool_w", "pool_scale", "norm2", "norm3", "final_norm")
ORDER = ("norm1", "w_in", "pool_w", "pool_scale", "w_out", "norm2", "w_up", "w_down", "norm3", "w_gate", "w_ple",
         "final_norm")


def _pack_small(tree, extra=None):
    parts = [tree[n].reshape(-1) for n in SMALL]
    if extra is not None:
        parts.append(extra.reshape(-1))
    flat = jnp.concatenate(parts)
    return jnp.pad(flat, (0, SMALL_ROWS * 128 - flat.shape[0])).reshape(SMALL_ROWS, 128)


def _unpack_small(packed, like):
    flat = packed.reshape(-1)
    out, off = {}, 0
    for n in SMALL:
        size = int(np.prod(like[n].shape))
        out[n] = flat[off:off + size].reshape(like[n].shape)
        off += size
    return out, flat[off]


def kernel(x, p, positions, norm1, w_in, pool_w, pool_scale, w_out, norm2, w_up, w_down, norm3, w_gate, w_ple, final_norm, loss_target, m_norm1, m_w_in, m_pool_w, m_pool_scale, m_w_out, m_norm2, m_w_up, m_w_down, m_norm3, m_w_gate, m_w_ple, m_final_norm, v_norm1, v_w_in, v_pool_w, v_pool_scale, v_w_out, v_norm2, v_w_up, v_w_down, v_norm3, v_w_gate, v_w_ple, v_final_norm):
    w = dict(norm1=norm1, w_in=w_in, pool_w=pool_w, pool_scale=pool_scale, w_out=w_out, norm2=norm2, w_up=w_up,
             w_down=w_down, norm3=norm3, w_gate=w_gate, w_ple=w_ple, final_norm=final_norm)
    m = dict(norm1=m_norm1, w_in=m_w_in, pool_w=m_pool_w, pool_scale=m_pool_scale, w_out=m_w_out, norm2=m_norm2,
             w_up=m_w_up, w_down=m_w_down, norm3=m_norm3, w_gate=m_w_gate, w_ple=m_w_ple, final_norm=m_final_norm)
    v = dict(norm1=v_norm1, w_in=v_w_in, pool_w=v_pool_w, pool_scale=v_pool_scale, w_out=v_w_out, norm2=v_norm2,
             w_up=v_w_up, w_down=v_w_down, norm3=v_norm3, w_gate=v_w_gate, w_ple=v_w_ple, final_norm=v_final_norm)
    small = {n: w[n] for n in SMALL}

    wsrc = _GatheredWeights({n: w[n] for n in BIG})
    reducer = _GradReducer()
    loss8, dx, small_grads = _local_step(x[0], p[:, 0], positions[0], wsrc, small, loss_target[0], reducer)
    gsh = reducer.finish(dx)

    red = _allreduce_small(_pack_small(small_grads, loss8[0, 0]))
    g_small, loss = _unpack_small(red, small)

    g_out, d_out, m_out, v_out = {}, {}, {}, {}
    for n in BIG:
        shp = w[n].shape
        two = lambda a: a.reshape(shp[0] * shp[1], shp[2])
        g2, d2, m2, v2 = _adamw(two(w[n]), two(gsh[n]), two(m[n]), two(v[n]), n)
        g_out[n], d_out[n], m_out[n], v_out[n] = g2.reshape(shp), d2.reshape(shp), m2.reshape(shp), v2.reshape(shp)
    _, d2, m2, v2 = _adamw(_pack_small(small), red, _pack_small({n: m[n] for n in SMALL}),
                           _pack_small({n: v[n] for n in SMALL}), "small")
    for tree, packed in ((d_out, d2), (m_out, m2), (v_out, v2)):
        tree.update(_unpack_small(packed, small)[0])
    g_out.update(g_small)

    return (loss, dx[None], *[g_out[n] for n in ORDER], *[d_out[n] for n in ORDER], *[m_out[n] for n in ORDER],
            *[v_out[n] for n in ORDER])
```

```python
import jax
import jax.numpy as jnp
import numpy as np
from jax import lax
from jax.experimental import pallas as pl
from jax.experimental.pallas import tpu as pltpu

f32 = jnp.float32
MXU_DTYPE = jnp.bfloat16
COMM_DTYPE = jnp.bfloat16

D_MODEL = 1024
POOL_WIDTH = 256
POOL_GC = 64
ATTN_WIDTH = 768
HEAD_DIM = 64
N_IN = POOL_WIDTH + 3 * ATTN_WIDTH
D_FF = 4096
PLE_DIM = 256
BLK = 128
DILATIONS = (1, 4, 16)
ROT_DIM = 16
ROPE_THETA = 500000.0
EPS = 1e-6
ATTN_SCALE = HEAD_DIM ** -0.5
NEG_BIG = -1e30

ADAM_LR, ADAM_B1, ADAM_B2, ADAM_EPS, ADAM_WD, ADAM_STEP = 0.001, 0.9, 0.999, 1e-08, 0.01, 10

TM = 512
TM_WGRAD = 1024
HALO = 16
VMEM_LIMIT = 48 * 1024 * 1024
N_CHIPS = 4
MESH = pl.DeviceIdType.MESH

BIG = ("w_in", "w_out", "w_up", "w_down", "w_gate", "w_ple")
FULL_SHAPE = {"w_in": (D_MODEL, N_IN), "w_out": (D_MODEL, D_MODEL), "w_up": (D_MODEL, D_FF),
              "w_down": (D_FF, D_MODEL), "w_gate": (D_MODEL, D_MODEL), "w_ple": (PLE_DIM, D_MODEL)}
COL_SHARDED = {"w_in": True, "w_out": False, "w_up": True, "w_down": False, "w_gate": False, "w_ple": True}


def _shard_shape(name):
    k, n = FULL_SHAPE[name]
    return (k, n // N_CHIPS) if COL_SHARDED[name] else (k // N_CHIPS, n)


def _cparams(sem=None, vmem=VMEM_LIMIT):
    return pltpu.CompilerParams(dimension_semantics=sem, vmem_limit_bytes=vmem)


def _resident(block_shape, index_map):
    return pl.BlockSpec(block_shape, index_map, pipeline_mode=pl.Buffered(1))


def _mx(x):
    return x.astype(MXU_DTYPE)


def _dot(a, b):
    return jnp.dot(a, b, preferred_element_type=f32)


def _dot_nt(a, b):
    return lax.dot_general(a, b, (((1,), (1,)), ((), ())), preferred_element_type=f32)


def _dot_tn(a, b):
    return lax.dot_general(a, b, (((0,), (0,)), ((), ())), preferred_element_type=f32)


def _sigmoid(x):
    return 1.0 / (1.0 + jnp.exp(-x))


def _rope_apply(y, c, s1, s2, width):
    return y * c + pltpu.roll(y, width - 8, axis=1) * s1 + pltpu.roll(y, 8, axis=1) * s2


def _rope_transpose(dy, c, s1, s2, width):
    return dy * c + pltpu.roll(dy * s1, 8, axis=1) + pltpu.roll(dy * s2, width - 8, axis=1)


def _norm_matmul(h, g, w, layer, tn, name, rope=None):
    s_len, d = h.shape
    n = w.shape[2]

    def body(*refs):
        if rope is None:
            h_ref, g_ref, w_ref, y_ref, hn_ref = refs
        else:
            h_ref, g_ref, w_ref, c_ref, s1_ref, s2_ref, y_ref, hn_ref = refs
            reps = tn // 128
            c = jnp.concatenate([c_ref[...]] * reps, axis=1)
            s1 = jnp.concatenate([s1_ref[...]] * reps, axis=1)
            s2 = jnp.concatenate([s2_ref[...]] * reps, axis=1)
        x = h_ref[...]
        r = lax.rsqrt(jnp.mean(x * x, axis=-1, keepdims=True) + EPS)
        hn = ((x * r) * g_ref[...]).astype(hn_ref.dtype)
        hn_ref[...] = hn
        for j in range(n // tn):
            y = _dot(hn, w_ref[:, j * tn:(j + 1) * tn])
            if rope is not None and POOL_WIDTH <= j * tn < POOL_WIDTH + 2 * ATTN_WIDTH:
                y = _rope_apply(y, c, s1, s2, tn)
            y_ref[:, j * tn:(j + 1) * tn] = y

    in_specs = [pl.BlockSpec((TM, d), lambda i: (i, 0)),
                pl.BlockSpec((1, d), lambda i: (0, 0)),
                _resident((None, d, n), lambda i: (layer, 0, 0))]
    args = [h, g, w]
    if rope is not None:
        assert POOL_WIDTH % tn == 0 and (2 * ATTN_WIDTH) % tn == 0
        in_specs += [pl.BlockSpec((TM, 128), lambda i: (i, 0))] * 3
        args += list(rope)
    return pl.pallas_call(
        body, name=name, grid=(s_len // TM,), in_specs=in_specs,
        out_specs=[pl.BlockSpec((TM, n), lambda i: (i, 0)), pl.BlockSpec((TM, d), lambda i: (i, 0))],
        out_shape=[jax.ShapeDtypeStruct((s_len, n), f32), jax.ShapeDtypeStruct((s_len, d), MXU_DTYPE)],
        compiler_params=_cparams(("parallel",)),
    )(*args)


def _matmul_residual(a, w, layer, res, name):
    s_len, k_dim = a.shape
    n = w.shape[2]

    def body(a_ref, w_ref, res_ref, o_ref):
        o_ref[...] = res_ref[...] + _dot(_mx(a_ref[...]), w_ref[...])

    return pl.pallas_call(
        body, name=name, grid=(s_len // TM,),
        in_specs=[pl.BlockSpec((TM, k_dim), lambda i: (i, 0)),
                  _resident((None, k_dim, n), lambda i: (layer, 0, 0)),
                  pl.BlockSpec((TM, n), lambda i: (i, 0))],
        out_specs=pl.BlockSpec((TM, n), lambda i: (i, 0)),
        out_shape=jax.ShapeDtypeStruct((s_len, n), f32),
        compiler_params=_cparams(("parallel",)),
    )(a, w, res)


def _gate_ple_fwd(h2, g, w_gate, w_ple, layer, p, name):
    s_len, d = h2.shape

    def body(h_ref, g_ref, wg_ref, p_ref, wp_ref, h3_ref, gl_ref, hn_ref):
        x = h_ref[...]
        r = lax.rsqrt(jnp.mean(x * x, axis=-1, keepdims=True) + EPS)
        hn = ((x * r) * g_ref[...]).astype(hn_ref.dtype)
        hn_ref[...] = hn
        gl = _dot(hn, wg_ref[...])
        gl_ref[...] = gl
        e = _dot(_mx(p_ref[...]), wp_ref[...])
        h3_ref[...] = x + _sigmoid(gl) * e

    row = lambda i: (i, 0)
    return pl.pallas_call(
        body, name=name, grid=(s_len // TM,),
        in_specs=[pl.BlockSpec((TM, d), row), pl.BlockSpec((1, d), lambda i: (0, 0)),
                  pl.BlockSpec((None, d, d), lambda i: (layer, 0, 0)), pl.BlockSpec((TM, PLE_DIM), row),
                  pl.BlockSpec((None, PLE_DIM, d), lambda i: (layer, 0, 0))],
        out_specs=[pl.BlockSpec((TM, d), row)] * 3,
        out_shape=[jax.ShapeDtypeStruct((s_len, d), f32), jax.ShapeDtypeStruct((s_len, d), f32),
                   jax.ShapeDtypeStruct((s_len, d), MXU_DTYPE)],
        compiler_params=_cparams(("parallel",)),
    )(h2, g, w_gate, p, w_ple)


def _gate_bwd(dh3, gl, p, w_ple, w_gate, layer, h2, g, name):
    s_len, d = dh3.shape

    def body(dh_ref, gl_ref, p_ref, wp_ref, wg_ref, h_ref, g_ref, dh2_ref, dg_ref, de_ref, dgl_ref):
        i = pl.program_id(0)
        dh = dh_ref[...]
        gate = _sigmoid(gl_ref[...])
        e = _dot(_mx(p_ref[...]), wp_ref[...])
        de_ref[...] = (dh * gate).astype(de_ref.dtype)
        dgl = ((dh * e) * (gate * (1.0 - gate))).astype(dgl_ref.dtype)
        dgl_ref[...] = dgl
        dx, dgrow = _rmsnorm_bwd(_dot_nt(dgl, wg_ref[...]), h_ref[...], g_ref[...])
        dh2_ref[...] = dh + dx
        dgsum = jnp.sum(dgrow, axis=0, keepdims=True)

        @pl.when(i == 0)
        def _():
            dg_ref[...] = dgsum

        @pl.when(i > 0)
        def _():
            dg_ref[...] += dgsum

    row = lambda i: (i, 0)
    blk = pl.BlockSpec((TM, d), row)
    return pl.pallas_call(
        body, name=name, grid=(s_len // TM,),
        in_specs=[blk, blk, pl.BlockSpec((TM, PLE_DIM), row), _resident((None, PLE_DIM, d), lambda i: (layer, 0, 0)),
                  _resident((None, d, d), lambda i: (layer, 0, 0)), blk, pl.BlockSpec((1, d), lambda i: (0, 0))],
        out_specs=[blk, pl.BlockSpec((1, d), lambda i: (0, 0)), blk, blk],
        out_shape=[jax.ShapeDtypeStruct((s_len, d), f32), jax.ShapeDtypeStruct((1, d), f32),
                   jax.ShapeDtypeStruct((s_len, d), MXU_DTYPE), jax.ShapeDtypeStruct((s_len, d), MXU_DTYPE)],
        compiler_params=_cparams(("arbitrary",)),
    )(dh3, gl, p, w_ple, w_gate, h2, g)


def _rmsnorm_bwd(dhn, x, g):
    r = lax.rsqrt(jnp.mean(x * x, axis=-1, keepdims=True) + EPS)
    xh = x * r
    dxh = dhn * g
    dx = r * (dxh - xh * jnp.mean(dxh * xh, axis=-1, keepdims=True))
    return dx, dhn * xh


def _matmul_nt_norm_bwd(dy, w, layer, h_prev, g, dres, name, tk=1024, after=None):
    s_len, k_dim = dy.shape
    d = h_prev.shape[1]

    def body(dy_ref, w_ref, h_ref, g_ref, dres_ref, *rest):
        dh_ref, dg_ref = rest[-2:]
        i = pl.program_id(0)
        acc = None
        for k in range(k_dim // tk):
            part = _dot_nt(_mx(dy_ref[:, k * tk:(k + 1) * tk]), w_ref[:, k * tk:(k + 1) * tk])
            acc = part if acc is None else acc + part
        dx, dgrow = _rmsnorm_bwd(acc, h_ref[...], g_ref[...])
        dh_ref[...] = dres_ref[...] + dx
        dgsum = jnp.sum(dgrow, axis=0, keepdims=True)

        @pl.when(i == 0)
        def _():
            dg_ref[...] = dgsum

        @pl.when(i > 0)
        def _():
            dg_ref[...] += dgsum

    in_specs = [pl.BlockSpec((TM, k_dim), lambda i: (i, 0)),
                _resident((None, d, k_dim), lambda i: (layer, 0, 0)),
                pl.BlockSpec((TM, d), lambda i: (i, 0)),
                pl.BlockSpec((1, d), lambda i: (0, 0)),
                pl.BlockSpec((TM, d), lambda i: (i, 0))]
    args = [dy, w, h_prev, g, dres]
    if after is not None:
        in_specs.append(pl.BlockSpec(memory_space=pl.ANY))
        args.append(after)
    return pl.pallas_call(
        body, name=name, grid=(s_len // TM,), in_specs=in_specs,
        out_specs=[pl.BlockSpec((TM, d), lambda i: (i, 0)), pl.BlockSpec((1, d), lambda i: (0, 0))],
        out_shape=[jax.ShapeDtypeStruct((s_len, d), f32), jax.ShapeDtypeStruct((1, d), f32)],
        compiler_params=_cparams(("arbitrary",)),
    )(*args)


def _mlp_fwd(h1, g, w_up, w_down, layer, name, tf=1024):
    s_len, d = h1.shape
    ff = w_up.shape[2]

    def body(h_ref, g_ref, wu_ref, wd_ref, h2_ref, a_ref, hn_ref):
        x = h_ref[...]
        r = lax.rsqrt(jnp.mean(x * x, axis=-1, keepdims=True) + EPS)
        hn = ((x * r) * g_ref[...]).astype(hn_ref.dtype)
        hn_ref[...] = hn
        acc = x
        for j in range(ff // tf):
            a = _dot(hn, wu_ref[:, j * tf:(j + 1) * tf])
            a_ref[:, j * tf:(j + 1) * tf] = a.astype(a_ref.dtype)
            relu = jnp.maximum(a, 0.0)
            acc = acc + _dot(_mx(relu * relu), wd_ref[j * tf:(j + 1) * tf, :])
        h2_ref[...] = acc

    row = lambda i: (i, 0)
    return pl.pallas_call(
        body, name=name, grid=(s_len // TM,),
        in_specs=[pl.BlockSpec((TM, d), row), pl.BlockSpec((1, d), lambda i: (0, 0)),
                  _resident((None, d, ff), lambda i: (layer, 0, 0)), _resident((None, ff, d), lambda i: (layer, 0, 0))],
        out_specs=[pl.BlockSpec((TM, d), row), pl.BlockSpec((TM, ff), row), pl.BlockSpec((TM, d), row)],
        out_shape=[jax.ShapeDtypeStruct((s_len, d), f32), jax.ShapeDtypeStruct((s_len, ff), MXU_DTYPE),
                   jax.ShapeDtypeStruct((s_len, d), MXU_DTYPE)],
        compiler_params=_cparams(("parallel",)),
    )(h1, g, w_up, w_down)


def _down_bwd(dh2, w_down, layer, a, name, tf=1024):
    s_len, d = dh2.shape
    ff = a.shape[1]

    def body(dh_ref, w_ref, a_ref, da_ref):
        dhb = _mx(dh_ref[...])
        for j in range(ff // tf):
            cols = slice(j * tf, (j + 1) * tf)
            dact = _dot_nt(dhb, w_ref[cols, :])
            da_ref[:, cols] = (dact * (2.0 * jnp.maximum(a_ref[:, cols].astype(f32), 0.0))).astype(da_ref.dtype)

    return pl.pallas_call(
        body, name=name, grid=(s_len // TM,),
        in_specs=[pl.BlockSpec((TM, d), lambda i: (i, 0)),
                  _resident((None, ff, d), lambda i: (layer, 0, 0)),
                  pl.BlockSpec((TM, ff), lambda i: (i, 0))],
        out_specs=pl.BlockSpec((TM, ff), lambda i: (i, 0)),
        out_shape=jax.ShapeDtypeStruct((s_len, ff), MXU_DTYPE),
        compiler_params=_cparams(("parallel",)),
    )(dh2, w_down, a)


def _matmul_nt(dy, w, layer, name):
    s_len, n = dy.shape
    k_dim = w.shape[1]

    def body(dy_ref, w_ref, o_ref):
        o_ref[...] = _dot_nt(_mx(dy_ref[...]), w_ref[...])

    return pl.pallas_call(
        body, name=name, grid=(s_len // TM,),
        in_specs=[pl.BlockSpec((TM, n), lambda i: (i, 0)), pl.BlockSpec((None, k_dim, n), lambda i: (layer, 0, 0))],
        out_specs=pl.BlockSpec((TM, k_dim), lambda i: (i, 0)),
        out_shape=jax.ShapeDtypeStruct((s_len, k_dim), f32),
        compiler_params=_cparams(("parallel",)),
    )(dy, w)


def _weight_grad(a, b, name, act=False):
    s_len, k_dim = a.shape
    n = b.shape[1]
    tka = min(k_dim, 2048)
    tnb = n if n <= 1024 else (2048 if n % 2048 == 0 else 640)
    ns = s_len // TM_WGRAD

    def body(a_ref, b_ref, o_ref, acc_ref):
        s = pl.program_id(2)
        x = a_ref[...]
        if act:
            relu = jnp.maximum(x.astype(f32), 0.0)
            x = relu * relu
        part = _dot_tn(_mx(x), _mx(b_ref[...]))

        @pl.when(s == 0)
        def _():
            acc_ref[...] = part

        @pl.when(s > 0)
        def _():
            acc_ref[...] += part

        @pl.when(s == ns - 1)
        def _():
            o_ref[...] = acc_ref[...].astype(o_ref.dtype)

    return pl.pallas_call(
        body, name=name, grid=(k_dim // tka, n // tnb, ns),
        in_specs=[pl.BlockSpec((TM_WGRAD, tka), lambda i, j, s: (s, i)),
                  pl.BlockSpec((TM_WGRAD, tnb), lambda i, j, s: (s, j))],
        out_specs=pl.BlockSpec((tka, tnb), lambda i, j, s: (i, j)),
        out_shape=jax.ShapeDtypeStruct((k_dim, n), COMM_DTYPE),
        scratch_shapes=[pltpu.VMEM((tka, tnb), f32)],
        compiler_params=_cparams(("parallel", "parallel", "arbitrary")),
    )(a, b)


def _group_select(lane, x2, x4, x8, x16):
    grp = lane // POOL_GC
    return jnp.where(grp == 0, x2, jnp.where(grp == 1, x4, jnp.where(grp == 2, x8, x16)))


def _pool_window(lane):
    grp = lane // POOL_GC
    return jnp.where(grp == 0, 2, jnp.where(grp == 1, 4, jnp.where(grp == 2, 8, 16)))


def _pool_y(u, halo, i):
    xs = jnp.concatenate([jnp.where(i > 0, halo, 0.0), u], axis=0)
    s2 = xs + pltpu.roll(xs, 1, axis=0)
    s4 = s2 + pltpu.roll(s2, 2, axis=0)
    s8 = s4 + pltpu.roll(s4, 4, axis=0)
    s16 = s8 + pltpu.roll(s8, 8, axis=0)
    lane = lax.broadcasted_iota(jnp.int32, xs.shape, 1)
    sel = _group_select(lane, s2, s4, s8, s16)[HALO:, :]
    t = i * TM + lax.broadcasted_iota(jnp.int32, u.shape, 0)
    cnt = jnp.minimum(_pool_window(lax.broadcasted_iota(jnp.int32, u.shape, 1)), t + 1).astype(f32)
    return sel / cnt - u


def _group_weights(l0, l1, l2):
    mx = jnp.maximum(jnp.maximum(l0, l1), l2)
    e0, e1, e2 = jnp.exp(l0 - mx), jnp.exp(l1 - mx), jnp.exp(l2 - mx)
    den = e0 + e1 + e2
    return e0 / den, e1 / den, e2 / den


def _mixer_merge(z, wbd, scale, outs, lses, name):
    s_len = z.shape[0]

    def body(u_ref, halo_ref, wbd_ref, sc_ref, o0, o1, o2, l0, l1, l2, m_ref):
        i = pl.program_id(0)
        y = _pool_y(u_ref[...], halo_ref[...], i)
        pool = _dot(_mx(y), wbd_ref[...]) * sc_ref[...]
        w0, w1, w2 = _group_weights(l0[...], l1[...], l2[...])
        m_ref[...] = jnp.concatenate([pool, o0[...] * w0, o1[...] * w1, o2[...] * w2], axis=1).astype(m_ref.dtype)

    row = lambda i: (i, 0)
    blk = pl.BlockSpec((TM, 256), row)
    grp = [pl.BlockSpec((TM, 256), lambda i, g=g: (i, g)) for g in range(3)]
    return pl.pallas_call(
        body, name=name, grid=(s_len // TM,),
        in_specs=[blk, pl.BlockSpec((HALO, 256), lambda i: (jnp.maximum(i * (TM // HALO) - 1, 0), 0)),
                  pl.BlockSpec((256, 256), lambda i: (0, 0)), pl.BlockSpec((1, 256), lambda i: (0, 0))] + grp + grp,
        out_specs=pl.BlockSpec((TM, D_MODEL), row),
        out_shape=jax.ShapeDtypeStruct((s_len, D_MODEL), MXU_DTYPE),
        compiler_params=_cparams(("parallel",)),
    )(z, z, wbd, scale, outs, outs, outs, lses, lses, lses)


def _head_sums(x):
    r = lax.broadcasted_iota(jnp.int32, (256, 256), 0) // HEAD_DIM
    c = lax.broadcasted_iota(jnp.int32, (256, 256), 1) // HEAD_DIM
    ones = jnp.where(r == c, 1.0, 0.0).astype(jnp.bfloat16)
    hi = x.astype(jnp.bfloat16)
    lo = (x - hi.astype(f32)).astype(jnp.bfloat16)
    return _dot(hi, ones) + _dot(lo, ones)


def _combine_bwd(dm, outs, lses, name):
    s_len = dm.shape[0]

    def body(d0, d1, d2, o0, o1, o2, l0, l1, l2, do_ref, dl_ref):
        w = _group_weights(l0[...], l1[...], l2[...])
        da = (d0[...], d1[...], d2[...])
        o = (o0[...], o1[...], o2[...])
        dw = [_head_sums(da[g] * o[g]) for g in range(3)]
        t = w[0] * dw[0] + w[1] * dw[1] + w[2] * dw[2]
        do_ref[...] = jnp.concatenate([da[g] * w[g] for g in range(3)], axis=1)
        dl_ref[...] = jnp.concatenate([w[g] * t for g in range(3)], axis=1)

    grp = [pl.BlockSpec((TM, 256), lambda i, g=g: (i, g)) for g in range(3)]
    return pl.pallas_call(
        body, name=name, grid=(s_len // TM,),
        in_specs=[pl.BlockSpec((TM, 256), lambda i: (i, 1)), pl.BlockSpec((TM, 256), lambda i: (i, 2)),
                  pl.BlockSpec((TM, 256), lambda i: (i, 3))] + grp + grp,
        out_specs=[pl.BlockSpec((TM, ATTN_WIDTH), lambda i: (i, 0))] * 2,
        out_shape=[jax.ShapeDtypeStruct((s_len, ATTN_WIDTH), f32)] * 2,
        compiler_params=_cparams(("parallel",)),
    )(dm, dm, dm, outs, outs, outs, lses, lses, lses)


def _pool_bwd(z, dm, wbd, scale, name):
    s_len = z.shape[0]
    n_halo = s_len // HALO

    def body(u_ref, uh_ref, d_ref, dh_ref, wbd_ref, sc_ref, du_ref, dw_ref, dsc_ref):
        i = pl.program_id(0)
        last = pl.num_programs(0) - 1
        y = _pool_y(u_ref[...], uh_ref[...], i)
        yb = _mx(y)
        dpo = d_ref[...]
        sc = sc_ref[...]
        dsc = jnp.sum(dpo * _dot(yb, wbd_ref[...]), axis=0, keepdims=True)
        dwp = _dot_tn(yb, _mx(dpo * sc))

        @pl.when(i == 0)
        def _():
            dsc_ref[...] = dsc
            dw_ref[...] = dwp

        @pl.when(i > 0)
        def _():
            dsc_ref[...] += dsc
            dw_ref[...] += dwp

        ext = jnp.concatenate([dpo, jnp.where(i < last, dh_ref[...], 0.0)], axis=0)
        dy = _dot_nt(_mx(ext * sc), wbd_ref[...])
        t = i * TM + lax.broadcasted_iota(jnp.int32, ext.shape, 0)
        lane = lax.broadcasted_iota(jnp.int32, ext.shape, 1)
        e = dy / jnp.minimum(_pool_window(lane), t + 1).astype(f32)
        rows = ext.shape[0]
        f2 = e + pltpu.roll(e, rows - 1, axis=0)
        f4 = f2 + pltpu.roll(f2, rows - 2, axis=0)
        f8 = f4 + pltpu.roll(f4, rows - 4, axis=0)
        f16 = f8 + pltpu.roll(f8, rows - 8, axis=0)
        du_ref[...] = (_group_select(lane, f2, f4, f8, f16) - dy)[:TM, :].astype(du_ref.dtype)

    row = lambda i: (i, 0)
    blk = pl.BlockSpec((TM, 256), row)
    return pl.pallas_call(
        body, name=name, grid=(s_len // TM,),
        in_specs=[blk, pl.BlockSpec((HALO, 256), lambda i: (jnp.maximum(i * (TM // HALO) - 1, 0), 0)),
                  blk, pl.BlockSpec((HALO, 256), lambda i: (jnp.minimum((i + 1) * (TM // HALO), n_halo - 1), 0)),
                  pl.BlockSpec((256, 256), lambda i: (0, 0)), pl.BlockSpec((1, 256), lambda i: (0, 0))],
        out_specs=[blk, pl.BlockSpec((256, 256), lambda i: (0, 0)), pl.BlockSpec((1, 256), lambda i: (0, 0))],
        out_shape=[jax.ShapeDtypeStruct((s_len, N_IN), MXU_DTYPE), jax.ShapeDtypeStruct((256, 256), f32),
                   jax.ShapeDtypeStruct((1, 256), f32)],
        compiler_params=_cparams(("arbitrary",)),
    )(z, z, dm, dm, wbd, scale)


def _tri_masks():
    qi = lax.broadcasted_iota(jnp.int32, (BLK, BLK), 0)
    ki = lax.broadcasted_iota(jnp.int32, (BLK, BLK), 1)
    return qi >= ki, ki >= qi


ATTN_SUPER_PER_STEP = (8, 2, 1)
Q_COL, K_COL, V_COL = POOL_WIDTH // 128, (POOL_WIDTH + ATTN_WIDTH) // 128, (POOL_WIDTH + 2 * ATTN_WIDTH) // 128


def _rows(ref, start, dil):
    if dil == 1:
        return ref[pl.ds(start, BLK), :]
    return ref[pl.ds(start, BLK, stride=dil), :]


RESIDUE_UNROLL = 4


def _for_residues(dil, fn, loop=True):
    if dil <= RESIDUE_UNROLL or not loop:
        for r in range(dil):
            fn(r, 0)
    else:
        lax.fori_loop(0, dil, fn, 0, unroll=RESIDUE_UNROLL)


def _set_rows(ref, start, dil, val):
    if dil == 1:
        ref[pl.ds(start, BLK), :] = val
    else:
        ref[pl.ds(start, BLK, stride=dil), :] = val


def _attn_fwd(z, g, prev, name):
    s_len = z.shape[0]
    dil, m = DILATIONS[g], ATTN_SUPER_PER_STEP[g]
    sbr = BLK * dil
    rows = sbr * m

    def body(*refs):
        q_ref, kc_ref, kp_ref, vc_ref, vp_ref = refs[:5]
        o_ref, l_ref = refs[-2:]
        st = pl.program_id(0)
        low, up = _tri_masks()
        head0 = lax.broadcasted_iota(jnp.int32, (BLK, 128), 1) < HEAD_DIM
        for sb in range(m):
            valid = jnp.concatenate([up & (st > 0) if sb == 0 else up, low], axis=1)

            def one_residue(r, carry, sb=sb, valid=valid):
                base = sb * sbr + r
                q = _rows(q_ref, base, dil)
                kc, vc = _rows(kc_ref, base, dil), _rows(vc_ref, base, dil)
                if sb == 0:
                    kp, vp = _rows(kp_ref, r, dil), _rows(vp_ref, r, dil)
                else:
                    kp, vp = _rows(kc_ref, base - sbr, dil), _rows(vc_ref, base - sbr, dil)
                k2 = jnp.concatenate([_mx(kp), _mx(kc)], axis=0)
                v2 = jnp.concatenate([_mx(vp), _mx(vc)], axis=0)
                qs = q * ATTN_SCALE
                outs, lses = [], []
                for hh in range(2):
                    s = jnp.where(valid, _dot_nt(_mx(jnp.where(head0 == (hh == 0), qs, 0.0)), k2), NEG_BIG)
                    mx = jnp.max(s, axis=-1, keepdims=True)
                    e = jnp.exp(s - mx)
                    l = jnp.sum(e, axis=-1, keepdims=True)
                    outs.append(_dot(_mx(e / l), v2))
                    lses.append(jnp.broadcast_to(mx + jnp.log(l), (BLK, 128)))
                _set_rows(o_ref, base, dil, jnp.where(head0, outs[0], outs[1]))
                _set_rows(l_ref, base, dil, jnp.where(head0, lses[0], lses[1]))
                return carry

            _for_residues(dil, one_residue, loop=False)

    def cur(col):
        return pl.BlockSpec((rows, 128), lambda st, hp: (st, col + 2 * g + hp))

    def before(col):
        return pl.BlockSpec((sbr, 128), lambda st, hp: (jnp.maximum(st * m - 1, 0), col + 2 * g + hp))

    in_specs = [cur(Q_COL), cur(K_COL), before(K_COL), cur(V_COL), before(V_COL)]
    args = [z, z, z, z, z]
    aliases = {}
    if prev is not None:
        in_specs += [pl.BlockSpec(memory_space=pl.ANY)] * 2
        args += list(prev)
        aliases = {5: 0, 6: 1}
    return pl.pallas_call(
        body, name=name, grid=(s_len // rows, 2), in_specs=in_specs, out_specs=[cur(0), cur(0)],
        out_shape=[jax.ShapeDtypeStruct((s_len, ATTN_WIDTH), f32)] * 2, input_output_aliases=aliases,
        compiler_params=_cparams(("parallel", "parallel")),
    )(*args)


def _stack_heads(x, head0):
    return jnp.concatenate([_mx(jnp.where(head0, x, 0.0)), _mx(jnp.where(head0, 0.0, x))], axis=0)


def _head_rows(x):
    xt = x.T
    return jnp.concatenate([jnp.broadcast_to(xt[0:1, :], (BLK, BLK)),
                            jnp.broadcast_to(xt[HEAD_DIM:HEAD_DIM + 1, :], (BLK, BLK))], axis=0)


def _attn_bwd(z, do, lse, dlt, tabs, dz, g, name):
    s_len = z.shape[0]
    dil, m = DILATIONS[g], ATTN_SUPER_PER_STEP[g]
    sbr = BLK * dil
    rows = sbr * m
    nsteps = s_len // rows

    def body(q_ref, qn_ref, kc_ref, kp_ref, vc_ref, vp_ref, do_ref, don_ref, l_ref, ln_ref, d_ref, dn_ref,
             c_ref, s1_ref, s2_ref, dz_in, dz_ref, dq_buf, dk_buf, dv_buf, out_buf, sems):
        del dz_in
        st, hp = pl.program_id(0), pl.program_id(1)
        head0 = lax.broadcasted_iota(jnp.int32, (BLK, 128), 1) < HEAD_DIM
        key_i = lax.broadcasted_iota(jnp.int32, (2 * BLK, BLK), 0) & (BLK - 1)
        query_i = lax.broadcasted_iota(jnp.int32, (2 * BLK, BLK), 1)
        same_t, cross_t = query_i >= key_i, key_i >= query_i
        for sb in range(m):
            prev_t = cross_t & (st > 0) if sb == 0 else cross_t
            next_t = cross_t & (st < nsteps - 1) if sb == m - 1 else cross_t

            def one_residue(r, carry, sb=sb, prev_t=prev_t, next_t=next_t):
                base = sb * sbr + r
                q, k, v = _rows(q_ref, base, dil), _rows(kc_ref, base, dil), _rows(vc_ref, base, dil)
                do_c, l_c, d_c = _rows(do_ref, base, dil), _rows(l_ref, base, dil), _rows(d_ref, base, dil)
                if sb == 0:
                    kp, vp = _rows(kp_ref, r, dil), _rows(vp_ref, r, dil)
                else:
                    kp, vp = _rows(kc_ref, base - sbr, dil), _rows(vc_ref, base - sbr, dil)
                if sb == m - 1:
                    qn, do_n = _rows(qn_ref, r, dil), _rows(don_ref, r, dil)
                    l_n, d_n = _rows(ln_ref, r, dil), _rows(dn_ref, r, dil)
                else:
                    qn, do_n = _rows(q_ref, base + sbr, dil), _rows(do_ref, base + sbr, dil)
                    l_n, d_n = _rows(l_ref, base + sbr, dil), _rows(d_ref, base + sbr, dil)
                k2, kp2, v2, vp2 = _stack_heads(k, head0), _stack_heads(kp, head0), _stack_heads(v, head0), _stack_heads(vp, head0)
                qb, qnb, dob, donb = _mx(q), _mx(qn), _mx(do_c), _mx(do_n)
                lse2, dlt2, lsen2, dltn2 = _head_rows(l_c), _head_rows(d_c), _head_rows(l_n), _head_rows(d_n)

                def pair(keys, vals, qs, dos, lse_rows, dlt_rows, valid):
                    p = jnp.where(valid, jnp.exp(_dot_nt(keys, qs) * ATTN_SCALE - lse_rows), 0.0)
                    ds = _mx(p * (_dot_nt(vals, dos) - dlt_rows) * ATTN_SCALE)
                    return _mx(p), ds

                p_a, ds_a = pair(k2, v2, qb, dob, lse2, dlt2, same_t)
                _, ds_b = pair(kp2, vp2, qb, dob, lse2, dlt2, prev_t)
                p_c, ds_c = pair(k2, v2, qnb, donb, lsen2, dltn2, next_t)
                dq = _dot_tn(ds_a, k2) + _dot_tn(ds_b, kp2)
                dk2 = _dot(ds_a, qb) + _dot(ds_c, qnb)
                dv2 = _dot(p_a, dob) + _dot(p_c, donb)
                c, s1, s2 = _rows(c_ref, base, dil), _rows(s1_ref, base, dil), _rows(s2_ref, base, dil)
                _set_rows(dq_buf, base, dil, _rope_transpose(dq, c, s1, s2, 128))
                _set_rows(dk_buf, base, dil, _rope_transpose(jnp.where(head0, dk2[:BLK], dk2[BLK:]), c, s1, s2, 128))
                _set_rows(dv_buf, base, dil, jnp.where(head0, dv2[:BLK], dv2[BLK:]))
                return carry

            _for_residues(dil, one_residue)
        copies = []
        for t, (buf, col) in enumerate(((dq_buf, Q_COL), (dk_buf, K_COL), (dv_buf, V_COL))):
            out_buf[t] = buf[...].astype(out_buf.dtype)
            lane0 = pl.multiple_of((col + 2 * g + hp) * 128, 128)
            dst = dz_ref.at[pl.ds(pl.multiple_of(st * rows, rows), rows), pl.ds(lane0, 128)]
            cp = pltpu.make_async_copy(out_buf.at[t], dst, sems.at[t])
            cp.start()
            copies.append(cp)
        for cp in copies:
            cp.wait()

    def cur(col):
        return pl.BlockSpec((rows, 128), lambda st, hp: (st, col + 2 * g + hp))

    def before(col):
        return pl.BlockSpec((sbr, 128), lambda st, hp: (jnp.maximum(st * m - 1, 0), col + 2 * g + hp))

    def after(col):
        return pl.BlockSpec((sbr, 128), lambda st, hp: (jnp.minimum((st + 1) * m, s_len // sbr - 1), col + 2 * g + hp))

    tab = pl.BlockSpec((rows, 128), lambda st, hp: (st, 0))
    return pl.pallas_call(
        body, name=name, grid=(nsteps, 2),
        in_specs=[cur(Q_COL), after(Q_COL), cur(K_COL), before(K_COL), cur(V_COL), before(V_COL),
                  cur(0), after(0), cur(0), after(0), cur(0), after(0), tab, tab, tab,
                  pl.BlockSpec(memory_space=pl.ANY)],
        out_specs=pl.BlockSpec(memory_space=pl.ANY),
        out_shape=jax.ShapeDtypeStruct(dz.shape, dz.dtype), input_output_aliases={15: 0},
        scratch_shapes=[pltpu.VMEM((rows, 128), f32)] * 3 + [pltpu.VMEM((3, rows, 128), dz.dtype),
                                                            pltpu.SemaphoreType.DMA((3,))],
        compiler_params=_cparams(("arbitrary", "arbitrary")),
    )(z, z, z, z, z, z, do, do, lse, lse, dlt, dlt, *tabs, dz)


def _loss_head(h, g, target, name):
    s_len, d = h.shape

    def body(h_ref, g_ref, t_ref, loss_ref, dh_ref, dg_ref):
        i = pl.program_id(0)
        x = h_ref[...]
        gv = g_ref[...]
        r = lax.rsqrt(jnp.mean(x * x, axis=-1, keepdims=True) + EPS)
        xh = x * r
        diff = xh * gv - t_ref[...]
        part = 0.5 * jnp.sum(jnp.mean(diff * diff, axis=-1, keepdims=True), axis=0, keepdims=True)
        dy = diff * (1.0 / d)
        dxh = dy * gv
        dh_ref[...] = r * (dxh - xh * jnp.mean(dxh * xh, axis=-1, keepdims=True))
        dgsum = jnp.sum(dy * xh, axis=0, keepdims=True)
        lossb = jnp.broadcast_to(part, (8, 128))

        @pl.when(i == 0)
        def _():
            loss_ref[...] = lossb
            dg_ref[...] = dgsum

        @pl.when(i > 0)
        def _():
            loss_ref[...] += lossb
            dg_ref[...] += dgsum

    row = lambda i: (i, 0)
    return pl.pallas_call(
        body, name=name, grid=(s_len // TM,),
        in_specs=[pl.BlockSpec((TM, d), row), pl.BlockSpec((1, d), lambda i: (0, 0)), pl.BlockSpec((TM, d), row)],
        out_specs=[pl.BlockSpec((8, 128), lambda i: (0, 0)), pl.BlockSpec((TM, d), row),
                   pl.BlockSpec((1, d), lambda i: (0, 0))],
        out_shape=[jax.ShapeDtypeStruct((8, 128), f32), jax.ShapeDtypeStruct((s_len, d), f32),
                   jax.ShapeDtypeStruct((1, d), f32)],
        compiler_params=_cparams(("arbitrary",)),
    )(h, g, target)


def _rope_tables(positions):
    inv_freq = ROPE_THETA ** (-jnp.arange(0, ROT_DIM, 2, dtype=f32) / ROT_DIM)
    ang = positions.astype(f32)[:, None] * inv_freq
    cos, sin = jnp.cos(ang), jnp.sin(ang)
    s_len = positions.shape[0]
    zero8, rest = jnp.zeros((s_len, 8), f32), jnp.zeros((s_len, HEAD_DIM - ROT_DIM), f32)
    c = jnp.concatenate([cos, cos, jnp.ones((s_len, HEAD_DIM - ROT_DIM), f32)], axis=1)
    s1 = jnp.concatenate([-sin, zero8, rest], axis=1)
    s2 = jnp.concatenate([zero8, sin, rest], axis=1)
    return c, s1, s2


def _block_diag(pool_w):
    out = jnp.zeros((POOL_WIDTH, POOL_WIDTH), pool_w.dtype)
    for g in range(4):
        out = lax.dynamic_update_slice(out, pool_w[g], (g * POOL_GC, g * POOL_GC))
    return out


def _layer_fwd(h, p_l, wsrc, small, layer, tabs):
    nm = f"l{layer}_"
    wts, wl = wsrc.take(layer, ("w_in",), (h,) if layer else tuple(tabs))
    z, hn1 = _norm_matmul(h, small["norm1"][layer][None], wts["w_in"], wl, 256, nm + "in_proj", rope=tabs)
    ol = None
    for g in range(3):
        ol = _attn_fwd(z, g, ol, nm + f"attn_fwd{g}")
    outs, lses = ol
    wbd = _mx(_block_diag(small["pool_w"][layer]))
    scale = small["pool_scale"][layer][None]
    m = _mixer_merge(z, wbd, scale, outs, lses, nm + "mixer_merge")
    wts.update(wsrc.take(layer, ("w_out",), (m,))[0])
    h1 = _matmul_residual(m, wts["w_out"], wl, h, nm + "out_proj")
    wts.update(wsrc.take(layer, ("w_up", "w_down"), (h1,))[0])
    h2, a, hn2 = _mlp_fwd(h1, small["norm2"][layer][None], wts["w_up"], wts["w_down"], wl, nm + "mlp")
    wts.update(wsrc.take(layer, ("w_gate", "w_ple"), (h2,))[0])
    h3, gl, hn3 = _gate_ple_fwd(h2, small["norm3"][layer][None], wts["w_gate"], wts["w_ple"], wl, p_l, nm + "gate_ple")
    saved = dict(h=h, z=z, hn1=hn1, outs=outs, lses=lses, wbd=wbd, scale=scale, m=m, h1=h1, a=a, hn2=hn2, h2=h2,
                 gl=gl, hn3=hn3, wts=wts, wl=wl)
    return h3, saved


def _layer_bwd(dh3, sv, p_l, small, layer, tabs128, reducer):
    nm = f"l{layer}_"
    wts, wl = sv["wts"], sv["wl"]
    dh2, dg3, de, dgl = _gate_bwd(dh3, sv["gl"], p_l, wts["w_ple"], wts["w_gate"], wl, sv["h2"],
                                  small["norm3"][layer][None], nm + "gate_bwd")
    reducer.add("w_gate", layer, _weight_grad(sv["hn3"], dgl, nm + "dw_gate"))
    reducer.add("w_ple", layer, _weight_grad(p_l, de, nm + "dw_ple"))
    da = _down_bwd(dh2, wts["w_down"], wl, sv["a"], nm + "down_bwd")
    reducer.add("w_down", layer, _weight_grad(sv["a"], dh2, nm + "dw_down", act=True))
    started = reducer.add("w_up", layer, _weight_grad(sv["hn2"], da, nm + "dw_up"))
    dh1, dg2 = _matmul_nt_norm_bwd(da, wts["w_up"], wl, sv["h1"], small["norm2"][layer][None], dh2, nm + "up_bwd",
                                   after=started)
    dm = _matmul_nt(dh1, wts["w_out"], wl, nm + "out_bwd")
    reducer.add("w_out", layer, _weight_grad(sv["m"], dh1, nm + "dw_out"))
    do, dlt = _combine_bwd(dm, sv["outs"], sv["lses"], nm + "combine_bwd")
    dz, dwbd, dscale = _pool_bwd(sv["z"], dm, sv["wbd"], sv["scale"], nm + "pool_bwd")
    for g in range(3):
        dz = _attn_bwd(sv["z"], do, sv["lses"], dlt, tabs128, dz, g, nm + f"attn_bwd{g}")
    started = reducer.add("w_in", layer, _weight_grad(sv["hn1"], dz, nm + "dw_in"))
    dh0, dg1 = _matmul_nt_norm_bwd(dz, wts["w_in"], wl, sv["h"], small["norm1"][layer][None], dh1, nm + "in_bwd",
                                   tk=512, after=started)
    dpool_w = jnp.stack([dwbd[g * POOL_GC:(g + 1) * POOL_GC, g * POOL_GC:(g + 1) * POOL_GC] for g in range(4)])
    sg = dict(norm1=dg1[0], norm2=dg2[0], norm3=dg3[0], pool_w=dpool_w, pool_scale=dscale[0])
    return dh0, sg


def _local_step(x, p, positions, wsrc, small, target, reducer):
    tabs128 = tuple(jnp.tile(t, (1, 2)) for t in _rope_tables(positions))
    h = x
    saved = []
    for layer in range(2):
        h, sv = _layer_fwd(h, p[layer], wsrc, small, layer, tabs128)
        saved.append(sv)
    loss, dh, dgf = _loss_head(h, small["final_norm"][None], target, "loss_head")
    sgs = [None, None]
    for layer in (1, 0):
        dh, sgs[layer] = _layer_bwd(dh, saved[layer], p[layer], small, layer, tabs128, reducer)
    small_grads = {k: jnp.stack([sgs[0][k], sgs[1][k]]) for k in sgs[0]}
    small_grads["final_norm"] = dgf[0]
    return loss, dh, small_grads


HBM = pl.BlockSpec(memory_space=pltpu.HBM)


def _my_place():
    return lax.axis_index("x"), lax.axis_index("y"), lax.axis_index("c")


def _other_chips(x, y):
    return [(1 - x, y), (x, 1 - y), (1 - x, 1 - y)]


def _window(ref, name, chip):
    k, n = _shard_shape(name)
    if COL_SHARDED[name]:
        return ref.at[:, pl.ds(pl.multiple_of(chip * n, 128), n)]
    return ref.at[pl.ds(pl.multiple_of(chip * k, 128), k), :]


def _chip_index():
    return jnp.reshape(2 * lax.axis_index("x") + lax.axis_index("y"), (1,)).astype(jnp.int32)


def _shard_block(name, tr):
    ks, ns = _shard_shape(name)
    if COL_SHARDED[name]:
        return (tr, ns), lambda i, me: (i, me[0])
    return (tr, ns), lambda i, me: (me[0] * (ks // tr) + i, 0)


def _place_shard(w, name, layer):
    ks, ns = _shard_shape(name)
    tr = min(ks, 256)
    shape, index = _shard_block(name, tr)

    def body(me_ref, w_ref, o_ref):
        o_ref[...] = w_ref[...].astype(o_ref.dtype)

    return pl.pallas_call(
        body, name=f"place_{name}{layer}",
        grid_spec=pltpu.PrefetchScalarGridSpec(
            num_scalar_prefetch=1, grid=(ks // tr,),
            in_specs=[pl.BlockSpec((None, tr, ns), lambda i, me: (layer, i, 0))],
            out_specs=pl.BlockSpec((None,) + shape, lambda i, me: (0,) + index(i, me))),
        out_shape=jax.ShapeDtypeStruct((1,) + FULL_SHAPE[name], MXU_DTYPE),
        compiler_params=_cparams(("parallel",)),
    )(_chip_index(), w)


GATHER_ORDER = [("w_in", 0), ("w_out", 0), ("w_up", 0), ("w_down", 0), ("w_gate", 0), ("w_ple", 0),
                ("w_in", 1), ("w_out", 1), ("w_up", 1), ("w_down", 1), ("w_gate", 1), ("w_ple", 1)]
SEM = pl.BlockSpec(memory_space=pltpu.SEMAPHORE)
EFFECT = pltpu.SideEffectType.DATAFLOW_SIDE_EFFECTING


def _gather_copy(src_ref, dst_ref, name, idx, j, chip, send_sems, recv_sems, c):
    cx, cy = chip
    return pltpu.make_async_remote_copy(
        src_ref=src_ref, dst_ref=dst_ref, send_sem=send_sems.at[3 * idx + j], recv_sem=recv_sems.at[3 * idx + j],
        device_id=(cx, cy, c), device_id_type=MESH)


def _gather_start(placed, order, tag, after=None):
    n = len(order)
    extra = [] if after is None else [after]

    def body(*refs):
        ins = refs[:n]
        k = n + len(extra)
        send_sems, recv_sems = refs[k], refs[k + 1]
        outs = refs[k + 2:k + 2 + n]
        token = refs[-1]
        x, y, c = _my_place()
        me = 2 * x + y
        for idx, (name, _) in enumerate(order):
            for j, chip in enumerate(_other_chips(x, y)):
                _gather_copy(_window(ins[idx].at[0], name, me), _window(outs[idx].at[0], name, me), name, idx, j, chip,
                             send_sems, recv_sems, c).start()
        token[...] = jnp.zeros_like(token)

    res = pl.pallas_call(
        body, name="gather_start" + tag,
        out_shape=(pltpu.SemaphoreType.DMA((3 * n,)), pltpu.SemaphoreType.DMA((3 * n,)))
        + tuple(pltpu.HBM(a.shape, a.dtype) for a in placed) + (jax.ShapeDtypeStruct((8, 128), f32),),
        in_specs=[HBM] * n + [pl.BlockSpec(memory_space=pl.ANY)] * len(extra),
        out_specs=(SEM, SEM) + (HBM,) * n + (pl.BlockSpec(memory_space=pltpu.VMEM),),
        input_output_aliases={i: i + 2 for i in range(n)},
        compiler_params=pltpu.CompilerParams(has_side_effects=EFFECT),
    )(*[pltpu.with_memory_space_constraint(a, pltpu.HBM) for a in placed], *extra)
    return res[0], res[1], list(res[2:2 + n]), res[-1]


def _gather_wait(send_sems, recv_sems, arrays, order, idxs, after, name):
    n = len(idxs)

    def body(*refs):
        ins = refs[:n]
        send_ref, recv_ref = refs[n], refs[n + 1]
        x, y, c = _my_place()
        me = 2 * x + y
        for k, idx in enumerate(idxs):
            wname = order[idx][0]
            for j, chip in enumerate(_other_chips(x, y)):
                cx, cy = chip
                mine = _window(ins[k].at[0], wname, me)
                land = _window(ins[k].at[0], wname, 2 * cx + cy)
                _gather_copy(mine, mine, wname, idx, j, chip, send_ref, recv_ref, c).wait_send()
                _gather_copy(land, land, wname, idx, j, chip, send_ref, recv_ref, c).wait_recv()

    operands = list(arrays) + [send_sems, recv_sems] + list(after)
    in_specs = [HBM] * n + [SEM, SEM] + [pl.BlockSpec(memory_space=pl.ANY)] * len(after)
    res = pl.pallas_call(
        body, name=name, out_shape=tuple(pltpu.HBM(a.shape, a.dtype) for a in arrays),
        in_specs=in_specs, out_specs=(HBM,) * n, input_output_aliases={i: i for i in range(n)},
        compiler_params=pltpu.CompilerParams(has_side_effects=EFFECT),
    )(*operands)
    return list(res)


class _GatheredWeights:
    def __init__(self, shards):
        self.starts = []
        token = None
        for tag, order in (("_first", GATHER_ORDER[:1]), ("_rest", GATHER_ORDER[1:])):
            placed = [_place_shard(shards[name], name, layer) for name, layer in order]
            self.starts.append((order,) + _gather_start(placed, order, tag, token))
            token = self.starts[-1][-1]

    def take(self, layer, names, after):
        order, send, recv, arrays, _ = next(s for s in self.starts if (names[0], layer) in s[0])
        after = list(after)
        if order is self.starts[0][0]:
            after.append(self.starts[-1][-1])
        idxs = [order.index((n, layer)) for n in names]
        got = _gather_wait(send, recv, [arrays[i] for i in idxs], order, idxs, after, f"gather_wait{layer}_{names[0]}")
        return dict(zip(names, got)), 0


N_DEV = 8


def _reduce_copies(dws, lands, names, layer, send_sems, recv_sems):
    x, y, c = _my_place()
    me, my_dev = 2 * x + y, 4 * x + 2 * y + c
    out = []
    for t, name in enumerate(names):
        for j, (cx, cy) in enumerate(_other_chips(x, y)):
            out.append((pltpu.make_async_remote_copy(
                src_ref=_window(dws[t], name, 2 * cx + cy), dst_ref=lands[t].at[my_dev],
                send_sem=send_sems.at[4 * t + j], recv_sem=recv_sems.at[N_DEV * t + my_dev],
                device_id=(cx, cy, layer), device_id_type=MESH), False))
        out.append((pltpu.make_async_remote_copy(
            src_ref=_window(dws[t], name, me), dst_ref=lands[t].at[my_dev],
            send_sem=send_sems.at[4 * t + 3], recv_sem=recv_sems.at[N_DEV * t + my_dev],
            device_id=(x, y, layer), device_id_type=MESH), True))
    return out


def _reduce_start(dws, names, layer, tag):
    n = len(names)
    lands = [lax.empty((N_DEV,) + _shard_shape(nm), dws[0].dtype) for nm in names]

    def body(*refs):
        ins = refs[:n]
        send_sems, recv_sems = refs[2 * n], refs[2 * n + 1]
        land_out = refs[3 * n + 2:4 * n + 2]
        token = refs[-1]
        c = lax.axis_index("c")
        for cp, non_owner_only in _reduce_copies(ins, land_out, names, layer, send_sems, recv_sems):
            if non_owner_only:
                @pl.when(c != layer)
                def _():
                    cp.start()
            else:
                cp.start()
        token[...] = jnp.zeros_like(token)

    res = pl.pallas_call(
        body, name="reduce_start" + tag,
        out_shape=(pltpu.SemaphoreType.DMA((4 * n,)), pltpu.SemaphoreType.DMA((N_DEV * n,)))
        + tuple(pltpu.HBM(a.shape, a.dtype) for a in dws) + tuple(pltpu.HBM(a.shape, a.dtype) for a in lands)
        + (jax.ShapeDtypeStruct((8, 128), f32),),
        in_specs=[HBM] * (2 * n),
        out_specs=(SEM, SEM) + (HBM,) * (2 * n) + (pl.BlockSpec(memory_space=pltpu.VMEM),),
        input_output_aliases={i: i + 2 for i in range(2 * n)},
        compiler_params=pltpu.CompilerParams(has_side_effects=EFFECT),
    )(*[pltpu.with_memory_space_constraint(a, pltpu.HBM) for a in list(dws) + lands])
    return res[0], res[1], list(res[2:2 + n]), list(res[2 + n:2 + 2 * n]), res[-1]


def _reduce_wait(send_sems, recv_sems, dws, lands, names, layer, after, tag):
    n = len(names)

    def body(*refs):
        ins, land_in = refs[:n], refs[n:2 * n]
        send_ref, recv_ref = refs[2 * n], refs[2 * n + 1]
        x, y, c = _my_place()
        for cp, non_owner_only in _reduce_copies(ins, land_in, names, layer, send_ref, recv_ref):
            if non_owner_only:
                @pl.when(c != layer)
                def _():
                    cp.wait_send()
            else:
                cp.wait_send()

        @pl.when(c == layer)
        def _():
            for t in range(n):
                for k in range(1, N_DEV):
                    px, py, pc = x ^ ((k >> 2) & 1), y ^ ((k >> 1) & 1), c ^ (k & 1)
                    dev = 4 * px + 2 * py + pc
                    land = land_in[t].at[dev]
                    pltpu.make_async_remote_copy(
                        src_ref=land, dst_ref=land, send_sem=send_ref.at[4 * t], recv_sem=recv_ref.at[N_DEV * t + dev],
                        device_id=(px, py, pc), device_id_type=MESH).wait_recv()

    res = pl.pallas_call(
        body, name="reduce_wait" + tag,
        out_shape=tuple(pltpu.HBM(a.shape, a.dtype) for a in list(dws) + list(lands)),
        in_specs=[HBM] * (2 * n) + [SEM, SEM, pl.BlockSpec(memory_space=pl.ANY)], out_specs=(HBM,) * (2 * n),
        input_output_aliases={i: i for i in range(2 * n)},
        compiler_params=pltpu.CompilerParams(has_side_effects=EFFECT),
    )(*dws, *lands, send_sems, recv_sems, after)
    return list(res[:n]), list(res[n:])


def _sum_devices(land, own, name, layer, prev):
    ks, ns = _shard_shape(name)
    tr = min(ks, 256)
    shape, index = _shard_block(name, tr)

    def body(me_ref, dev_ref, *refs):
        s_ref, own_ref, out_ref = refs[0], refs[1], refs[-1]
        dev = dev_ref[0]
        acc = None
        for s in range(N_DEV):
            term = jnp.where(dev == s, own_ref[...], s_ref[s]).astype(f32)
            acc = term if acc is None else acc + term
        out_ref[...] = acc

    def mine(i, dev):
        return i * jnp.where((dev[0] & 1) == layer, 1, 0)

    in_specs = [pl.BlockSpec((N_DEV, tr, ns), lambda i, me, dev: (0, mine(i, dev), 0)),
                pl.BlockSpec(shape, lambda i, me, dev: index(mine(i, dev), me))]
    args = [land, own]
    aliases = {}
    if prev is not None:
        in_specs.append(pl.BlockSpec(memory_space=pl.ANY))
        args.append(prev)
        aliases = {4: 0}
    x, y, c = _my_place()
    return pl.pallas_call(
        body, name=f"sum_devices_{name}{layer}",
        grid_spec=pltpu.PrefetchScalarGridSpec(
            num_scalar_prefetch=2, grid=(ks // tr,), in_specs=in_specs,
            out_specs=pl.BlockSpec((None, tr, ns), lambda i, me, dev: (layer, mine(i, dev), 0))),
        out_shape=jax.ShapeDtypeStruct((2, ks, ns), f32), input_output_aliases=aliases,
        compiler_params=_cparams(("arbitrary",)),
    )(_chip_index(), jnp.reshape(4 * x + 2 * y + c, (1,)).astype(jnp.int32), *args)


class _GradReducer:
    GROUPS = (("1", 1, ("w_gate", "w_ple", "w_down", "w_up", "w_out", "w_in")),
              ("0a", 0, ("w_gate", "w_ple", "w_down", "w_up")),
              ("0b", 0, ("w_out", "w_in")))

    def __init__(self):
        self.grads = {}
        self.started = {}

    def add(self, name, layer, dw):
        self.grads[(name, layer)] = dw
        token = None
        for tag, glayer, names in self.GROUPS:
            if tag not in self.started and all((nm, glayer) in self.grads for nm in names):
                *self.started[tag], token = _reduce_start([self.grads[(nm, glayer)] for nm in names], names, glayer, tag)
        return token

    def finish(self, after):
        mine = {}
        for tag, layer, names in self.GROUPS:
            send, recv, dws, lands = self.started[tag]
            dws, lands = _reduce_wait(send, recv, dws, lands, names, layer, after, tag)
            for nm, dw, land in zip(names, dws, lands):
                mine[nm] = _sum_devices(land, dw, nm, layer, mine.get(nm))
        return _pair_layers(mine)


def _pair_layers(mine):
    names = list(BIG)

    def body(*refs):
        ins = refs[:len(names)]
        outs = refs[len(names):2 * len(names)]
        send_sems, recv_sems = refs[2 * len(names):]
        x, y, c = _my_place()
        sibling = (x, y, 1 - c)
        cps = []
        for t in range(len(names)):
            cp = pltpu.make_async_remote_copy(
                src_ref=ins[t].at[c], dst_ref=outs[t].at[c], send_sem=send_sems.at[t], recv_sem=recv_sems.at[t],
                device_id=sibling, device_id_type=MESH)
            cp.start()
            cps.append(cp)
        for t in range(len(names)):
            cps[t].wait_send()
            land = outs[t].at[1 - c]
            pltpu.make_async_remote_copy(
                src_ref=land, dst_ref=land, send_sem=send_sems.at[t], recv_sem=recv_sems.at[t],
                device_id=sibling, device_id_type=MESH).wait_recv()

    outs = pl.pallas_call(
        body, name="pair_layers", in_specs=[HBM] * len(names), out_specs=[HBM] * len(names),
        out_shape=[jax.ShapeDtypeStruct((2,) + _shard_shape(n), f32) for n in names],
        input_output_aliases={t: t for t in range(len(names))},
        scratch_shapes=[pltpu.SemaphoreType.DMA((len(names),)), pltpu.SemaphoreType.DMA((len(names),))],
    )(*[mine[n] for n in names])
    return dict(zip(names, outs))


SMALL_ROWS = 320


def _small_copies(vec_ref, land_ref, send_sems, recv_sems):
    x, y, c = _my_place()
    me = 4 * x + 2 * y + c
    out = []
    for k in range(1, N_DEV):
        peer = (x ^ ((k >> 2) & 1), y ^ ((k >> 1) & 1), c ^ (k & 1))
        src_dev = 4 * peer[0] + 2 * peer[1] + peer[2]
        send = pltpu.make_async_remote_copy(
            src_ref=vec_ref, dst_ref=land_ref.at[me], send_sem=send_sems.at[k - 1], recv_sem=recv_sems.at[k - 1],
            device_id=peer, device_id_type=MESH)
        arrival = pltpu.make_async_remote_copy(
            src_ref=land_ref.at[src_dev], dst_ref=land_ref.at[src_dev], send_sem=send_sems.at[k - 1],
            recv_sem=recv_sems.at[k - 1], device_id=peer, device_id_type=MESH)
        out.append((send, arrival))
    return out


def _small_start(vec):
    land = lax.empty((N_DEV,) + vec.shape, vec.dtype)

    def body(v_ref, land_in, send_sems, recv_sems, v_out, land_out):
        del land_in, v_out
        for send, _ in _small_copies(v_ref, land_out, send_sems, recv_sems):
            send.start()

    return pl.pallas_call(
        body, name="small_start",
        out_shape=(pltpu.SemaphoreType.DMA((N_DEV - 1,)), pltpu.SemaphoreType.DMA((N_DEV - 1,)),
                   pltpu.HBM(vec.shape, vec.dtype), pltpu.HBM(land.shape, land.dtype)),
        in_specs=[HBM, HBM], out_specs=(SEM, SEM, HBM, HBM), input_output_aliases={0: 2, 1: 3},
        compiler_params=pltpu.CompilerParams(has_side_effects=EFFECT),
    )(pltpu.with_memory_space_constraint(vec, pltpu.HBM), pltpu.with_memory_space_constraint(land, pltpu.HBM))


def _small_wait(send_sems, recv_sems, vec, land, after):
    def body(v_ref, land_ref, send_ref, recv_ref, after_ref, v_out, land_out):
        del after_ref, v_out, land_out
        for send, arrival in _small_copies(v_ref, land_ref, send_ref, recv_ref):
            send.wait_send()
            arrival.wait_recv()

    return pl.pallas_call(
        body, name="small_wait", out_shape=(pltpu.HBM(vec.shape, vec.dtype), pltpu.HBM(land.shape, land.dtype)),
        in_specs=[HBM, HBM, SEM, SEM, pl.BlockSpec(memory_space=pl.ANY)], out_specs=(HBM, HBM),
        input_output_aliases={0: 0, 1: 1}, compiler_params=pltpu.CompilerParams(has_side_effects=EFFECT),
    )(vec, land, send_sems, recv_sems, after)


def _small_sum(vec, land):
    x, y, c = _my_place()

    def body(dev_ref, v_ref, land_ref, out_ref):
        acc = None
        for s in range(N_DEV):
            term = jnp.where(dev_ref[0] == s, v_ref[...], land_ref[s])
            acc = term if acc is None else acc + term
        out_ref[...] = acc

    return pl.pallas_call(
        body, name="small_sum",
        grid_spec=pltpu.PrefetchScalarGridSpec(
            num_scalar_prefetch=1, grid=(1,),
            in_specs=[pl.BlockSpec(vec.shape, lambda i, dev: (0, 0)), pl.BlockSpec(land.shape, lambda i, dev: (0, 0, 0))],
            out_specs=pl.BlockSpec(vec.shape, lambda i, dev: (0, 0))),
        out_shape=jax.ShapeDtypeStruct(vec.shape, vec.dtype),
        compiler_params=_cparams(("arbitrary",)),
    )(jnp.reshape(4 * x + 2 * y + c, (1,)).astype(jnp.int32), vec, land)


def _adamw(w, g, m, v, name):
    rows, cols = w.shape
    tr = rows
    for cand in (512, 256, 128, 64, 32, 16, 8):
        if rows % cand == 0 and cand * cols * 4 <= 2 * 1024 * 1024:
            tr = cand
            break
    c1 = np.float32(1.0 - ADAM_B1 ** ADAM_STEP)
    c2 = np.float32(1.0 - ADAM_B2 ** ADAM_STEP)

    def body(w_ref, g_ref, m_ref, v_ref, go_ref, d_ref, mo_ref, vo_ref):
        gv = g_ref[...]
        go_ref[...] = gv
        mn = ADAM_B1 * m_ref[...] + (1.0 - ADAM_B1) * gv
        vn = ADAM_B2 * v_ref[...] + (1.0 - ADAM_B2) * (gv * gv)
        mo_ref[...] = mn
        vo_ref[...] = vn
        d_ref[...] = -ADAM_LR * ((mn / c1) / (jnp.sqrt(vn / c2) + ADAM_EPS) + ADAM_WD * w_ref[...])

    blk = pl.BlockSpec((tr, cols), lambda i: (i, 0))
    return pl.pallas_call(
        body, name="adamw_" + name, grid=(rows // tr,), in_specs=[blk] * 4, out_specs=[blk] * 4,
        out_shape=[jax.ShapeDtypeStruct((rows, cols), f32)] * 4,
        compiler_params=_cparams(("parallel",)),
    )(w, g, m, v)


SMALL = ("norm1", "pool_w", "pool_scale", "norm2", "norm3", "final_norm")
ORDER = ("norm1", "w_in", "pool_w", "pool_scale", "w_out", "norm2", "w_up", "w_down", "norm3", "w_gate", "w_ple",
         "final_norm")


def _pack_small(tree, extra=None):
    parts = [tree[n].reshape(-1) for n in SMALL]
    if extra is not None:
        parts.append(extra.reshape(-1))
    flat = jnp.concatenate(parts)
    return jnp.pad(flat, (0, SMALL_ROWS * 128 - flat.shape[0])).reshape(SMALL_ROWS, 128)


def _unpack_small(packed, like):
    flat = packed.reshape(-1)
    out, off = {}, 0
    for n in SMALL:
        size = int(np.prod(like[n].shape))
        out[n] = flat[off:off + size].reshape(like[n].shape)
        off += size
    return out, flat[off]


def kernel(x, p, positions, norm1, w_in, pool_w, pool_scale, w_out, norm2, w_up, w_down, norm3, w_gate, w_ple, final_norm, loss_target, m_norm1, m_w_in, m_pool_w, m_pool_scale, m_w_out, m_norm2, m_w_up, m_w_down, m_norm3, m_w_gate, m_w_ple, m_final_norm, v_norm1, v_w_in, v_pool_w, v_pool_scale, v_w_out, v_norm2, v_w_up, v_w_down, v_norm3, v_w_gate, v_w_ple, v_final_norm):
    w = dict(norm1=norm1, w_in=w_in, pool_w=pool_w, pool_scale=pool_scale, w_out=w_out, norm2=norm2, w_up=w_up,
             w_down=w_down, norm3=norm3, w_gate=w_gate, w_ple=w_ple, final_norm=final_norm)
    m = dict(norm1=m_norm1, w_in=m_w_in, pool_w=m_pool_w, pool_scale=m_pool_scale, w_out=m_w_out, norm2=m_norm2,
             w_up=m_w_up, w_down=m_w_down, norm3=m_norm3, w_gate=m_w_gate, w_ple=m_w_ple, final_norm=m_final_norm)
    v = dict(norm1=v_norm1, w_in=v_w_in, pool_w=v_pool_w, pool_scale=v_pool_scale, w_out=v_w_out, norm2=v_norm2,
             w_up=v_w_up, w_down=v_w_down, norm3=v_norm3, w_gate=v_w_gate, w_ple=v_w_ple, final_norm=v_final_norm)
    small = {n: w[n] for n in SMALL}

    wsrc = _GatheredWeights({n: w[n] for n in BIG})
    reducer = _GradReducer()
    loss8, dx, small_grads = _local_step(x[0], p[:, 0], positions[0], wsrc, small, loss_target[0], reducer)
    s_send, s_recv, s_vec, s_land = _small_start(_pack_small(small_grads, loss8[0, 0]))
    gsh = reducer.finish(s_vec)

    g_out, d_out, m_out, v_out = {}, {}, {}, {}
    for n in BIG:
        shp = w[n].shape
        two = lambda a: a.reshape(shp[0] * shp[1], shp[2])
        g2, d2, m2, v2 = _adamw(two(w[n]), two(gsh[n]), two(m[n]), two(v[n]), n)
        g_out[n], d_out[n], m_out[n], v_out[n] = g2.reshape(shp), d2.reshape(shp), m2.reshape(shp), v2.reshape(shp)
    red = _small_sum(*_small_wait(s_send, s_recv, s_vec, s_land, d2))
    g_small, loss = _unpack_small(red, small)
    _, d2, m2, v2 = _adamw(_pack_small(small), red, _pack_small({n: m[n] for n in SMALL}),
                           _pack_small({n: v[n] for n in SMALL}), "small")
    for tree, packed in ((d_out, d2), (m_out, m2), (v_out, v2)):
        tree.update(_unpack_small(packed, small)[0])
    g_out.update(g_small)

    return (loss, dx[None], *[g_out[n] for n in ORDER], *[d_out[n] for n in ORDER], *[m_out[n] for n in ORDER],
            *[v_out[n] for n in ORDER])
```

```python
import jax
import jax.numpy as jnp
import numpy as np
from jax import lax
from jax.experimental import pallas as pl
from jax.experimental.pallas import tpu as pltpu

f32 = jnp.float32
MXU_DTYPE = jnp.bfloat16
COMM_DTYPE = jnp.bfloat16

D_MODEL = 1024
POOL_WIDTH = 256
POOL_GC = 64
ATTN_WIDTH = 768
HEAD_DIM = 64
N_IN = POOL_WIDTH + 3 * ATTN_WIDTH
D_FF = 4096
PLE_DIM = 256
BLK = 128
DILATIONS = (1, 4, 16)
ROT_DIM = 16
ROPE_THETA = 500000.0
EPS = 1e-6
ATTN_SCALE = HEAD_DIM ** -0.5
NEG_BIG = -1e30

ADAM_LR, ADAM_B1, ADAM_B2, ADAM_EPS, ADAM_WD, ADAM_STEP = 0.001, 0.9, 0.999, 1e-08, 0.01, 10

TM = 512
TM_WGRAD = 1024
HALO = 16
VMEM_LIMIT = 48 * 1024 * 1024
VMEM_LIMIT_LARGE = 58 * 1024 * 1024
N_CHIPS = 4
MESH = pl.DeviceIdType.MESH

BIG = ("w_in", "w_out", "w_up", "w_down", "w_gate", "w_ple")
FULL_SHAPE = {"w_in": (D_MODEL, N_IN), "w_out": (D_MODEL, D_MODEL), "w_up": (D_MODEL, D_FF),
              "w_down": (D_FF, D_MODEL), "w_gate": (D_MODEL, D_MODEL), "w_ple": (PLE_DIM, D_MODEL)}
COL_SHARDED = {"w_in": True, "w_out": False, "w_up": True, "w_down": False, "w_gate": False, "w_ple": True}


def _shard_shape(name):
    k, n = FULL_SHAPE[name]
    return (k, n // N_CHIPS) if COL_SHARDED[name] else (k // N_CHIPS, n)


def _cparams(sem=None, vmem=VMEM_LIMIT):
    return pltpu.CompilerParams(dimension_semantics=sem, vmem_limit_bytes=vmem)


def _resident(block_shape, index_map):
    return pl.BlockSpec(block_shape, index_map, pipeline_mode=pl.Buffered(1))


def _mx(x):
    return x.astype(MXU_DTYPE)


def _dot(a, b):
    return jnp.dot(a, b, preferred_element_type=f32)


def _dot_nt(a, b):
    return lax.dot_general(a, b, (((1,), (1,)), ((), ())), preferred_element_type=f32)


def _dot_tn(a, b):
    return lax.dot_general(a, b, (((0,), (0,)), ((), ())), preferred_element_type=f32)


def _sigmoid(x):
    return 1.0 / (1.0 + jnp.exp(-x))


def _rope_apply(y, c, s1, s2, width):
    return y * c + pltpu.roll(y, width - 8, axis=1) * s1 + pltpu.roll(y, 8, axis=1) * s2


def _rope_transpose(dy, c, s1, s2, width):
    return dy * c + pltpu.roll(dy * s1, 8, axis=1) + pltpu.roll(dy * s2, width - 8, axis=1)


def _norm_matmul(h, g, w, layer, tn, name, rope=None):
    s_len, d = h.shape
    n = w.shape[2]

    def body(*refs):
        if rope is None:
            h_ref, g_ref, w_ref, y_ref, hn_ref = refs
        else:
            h_ref, g_ref, w_ref, c_ref, s1_ref, s2_ref, y_ref, hn_ref = refs
            reps = tn // 128
            c = jnp.concatenate([c_ref[...]] * reps, axis=1)
            s1 = jnp.concatenate([s1_ref[...]] * reps, axis=1)
            s2 = jnp.concatenate([s2_ref[...]] * reps, axis=1)
        x = h_ref[...]
        r = lax.rsqrt(jnp.mean(x * x, axis=-1, keepdims=True) + EPS)
        hn = ((x * r) * g_ref[...]).astype(hn_ref.dtype)
        hn_ref[...] = hn
        for j in range(n // tn):
            y = _dot(hn, w_ref[:, j * tn:(j + 1) * tn])
            if rope is not None and POOL_WIDTH <= j * tn < POOL_WIDTH + 2 * ATTN_WIDTH:
                y = _rope_apply(y, c, s1, s2, tn)
            y_ref[:, j * tn:(j + 1) * tn] = y

    in_specs = [pl.BlockSpec((TM, d), lambda i: (i, 0)),
                pl.BlockSpec((1, d), lambda i: (0, 0)),
                _resident((None, d, n), lambda i: (layer, 0, 0))]
    args = [h, g, w]
    if rope is not None:
        assert POOL_WIDTH % tn == 0 and (2 * ATTN_WIDTH) % tn == 0
        in_specs += [pl.BlockSpec((TM, 128), lambda i: (i, 0))] * 3
        args += list(rope)
    return pl.pallas_call(
        body, name=name, grid=(s_len // TM,), in_specs=in_specs,
        out_specs=[pl.BlockSpec((TM, n), lambda i: (i, 0)), pl.BlockSpec((TM, d), lambda i: (i, 0))],
        out_shape=[jax.ShapeDtypeStruct((s_len, n), f32), jax.ShapeDtypeStruct((s_len, d), MXU_DTYPE)],
        compiler_params=_cparams(("parallel",)),
    )(*args)


def _matmul_residual(a, w, layer, res, name):
    s_len, k_dim = a.shape
    n = w.shape[2]

    def body(a_ref, w_ref, res_ref, o_ref):
        o_ref[...] = res_ref[...] + _dot(_mx(a_ref[...]), w_ref[...])

    return pl.pallas_call(
        body, name=name, grid=(s_len // TM,),
        in_specs=[pl.BlockSpec((TM, k_dim), lambda i: (i, 0)),
                  _resident((None, k_dim, n), lambda i: (layer, 0, 0)),
                  pl.BlockSpec((TM, n), lambda i: (i, 0))],
        out_specs=pl.BlockSpec((TM, n), lambda i: (i, 0)),
        out_shape=jax.ShapeDtypeStruct((s_len, n), f32),
        compiler_params=_cparams(("parallel",)),
    )(a, w, res)


def _gate_ple_fwd(h2, g, w_gate, w_ple, layer, p, name):
    s_len, d = h2.shape

    def body(h_ref, g_ref, wg_ref, p_ref, wp_ref, h3_ref, gl_ref, hn_ref):
        x = h_ref[...]
        r = lax.rsqrt(jnp.mean(x * x, axis=-1, keepdims=True) + EPS)
        hn = ((x * r) * g_ref[...]).astype(hn_ref.dtype)
        hn_ref[...] = hn
        gl = _dot(hn, wg_ref[...])
        gl_ref[...] = gl.astype(gl_ref.dtype)
        e = _dot(_mx(p_ref[...]), wp_ref[...])
        h3_ref[...] = x + _sigmoid(gl) * e

    row = lambda i: (i, 0)
    return pl.pallas_call(
        body, name=name, grid=(s_len // TM,),
        in_specs=[pl.BlockSpec((TM, d), row), pl.BlockSpec((1, d), lambda i: (0, 0)),
                  pl.BlockSpec((None, d, d), lambda i: (layer, 0, 0)), pl.BlockSpec((TM, PLE_DIM), row),
                  pl.BlockSpec((None, PLE_DIM, d), lambda i: (layer, 0, 0))],
        out_specs=[pl.BlockSpec((TM, d), row)] * 3,
        out_shape=[jax.ShapeDtypeStruct((s_len, d), f32), jax.ShapeDtypeStruct((s_len, d), MXU_DTYPE),
                   jax.ShapeDtypeStruct((s_len, d), MXU_DTYPE)],
        compiler_params=_cparams(("parallel",)),
    )(h2, g, w_gate, p, w_ple)


def _gate_bwd(dh3, gl, p, w_ple, w_gate, layer, h2, g, name):
    s_len, d = dh3.shape

    def body(dh_ref, gl_ref, p_ref, wp_ref, wg_ref, h_ref, g_ref, dh2_ref, dg_ref, de_ref, dgl_ref):
        i = pl.program_id(0)
        dh = dh_ref[...]
        gate = _sigmoid(gl_ref[...].astype(f32))
        e = _dot(_mx(p_ref[...]), wp_ref[...])
        de_ref[...] = (dh * gate).astype(de_ref.dtype)
        dgl = ((dh * e) * (gate * (1.0 - gate))).astype(dgl_ref.dtype)
        dgl_ref[...] = dgl
        dx, dgrow = _rmsnorm_bwd(_dot_nt(dgl, wg_ref[...]), h_ref[...], g_ref[...])
        dh2_ref[...] = dh + dx
        dgsum = jnp.sum(dgrow, axis=0, keepdims=True)

        @pl.when(i == 0)
        def _():
            dg_ref[...] = dgsum

        @pl.when(i > 0)
        def _():
            dg_ref[...] += dgsum

    row = lambda i: (i, 0)
    blk = pl.BlockSpec((TM, d), row)
    return pl.pallas_call(
        body, name=name, grid=(s_len // TM,),
        in_specs=[blk, blk, pl.BlockSpec((TM, PLE_DIM), row), _resident((None, PLE_DIM, d), lambda i: (layer, 0, 0)),
                  _resident((None, d, d), lambda i: (layer, 0, 0)), blk, pl.BlockSpec((1, d), lambda i: (0, 0))],
        out_specs=[blk, pl.BlockSpec((1, d), lambda i: (0, 0)), blk, blk],
        out_shape=[jax.ShapeDtypeStruct((s_len, d), f32), jax.ShapeDtypeStruct((1, d), f32),
                   jax.ShapeDtypeStruct((s_len, d), MXU_DTYPE), jax.ShapeDtypeStruct((s_len, d), MXU_DTYPE)],
        compiler_params=_cparams(("arbitrary",)),
    )(dh3, gl, p, w_ple, w_gate, h2, g)


def _rmsnorm_bwd(dhn, x, g):
    r = lax.rsqrt(jnp.mean(x * x, axis=-1, keepdims=True) + EPS)
    xh = x * r
    dxh = dhn * g
    dx = r * (dxh - xh * jnp.mean(dxh * xh, axis=-1, keepdims=True))
    return dx, dhn * xh


def _matmul_nt_norm_bwd(dy, w, layer, h_prev, g, dres, name, tk=1024, after=None):
    s_len, k_dim = dy.shape
    d = h_prev.shape[1]

    def body(dy_ref, w_ref, h_ref, g_ref, dres_ref, *rest):
        dh_ref, dg_ref = rest[-2:]
        i = pl.program_id(0)
        acc = None
        for k in range(k_dim // tk):
            part = _dot_nt(_mx(dy_ref[:, k * tk:(k + 1) * tk]), w_ref[:, k * tk:(k + 1) * tk])
            acc = part if acc is None else acc + part
        dx, dgrow = _rmsnorm_bwd(acc, h_ref[...], g_ref[...])
        dh_ref[...] = dres_ref[...] + dx
        dgsum = jnp.sum(dgrow, axis=0, keepdims=True)

        @pl.when(i == 0)
        def _():
            dg_ref[...] = dgsum

        @pl.when(i > 0)
        def _():
            dg_ref[...] += dgsum

    in_specs = [pl.BlockSpec((TM, k_dim), lambda i: (i, 0)),
                _resident((None, d, k_dim), lambda i: (layer, 0, 0)),
                pl.BlockSpec((TM, d), lambda i: (i, 0)),
                pl.BlockSpec((1, d), lambda i: (0, 0)),
                pl.BlockSpec((TM, d), lambda i: (i, 0))]
    args = [dy, w, h_prev, g, dres]
    if after is not None:
        in_specs.append(pl.BlockSpec(memory_space=pl.ANY))
        args.append(after)
    return pl.pallas_call(
        body, name=name, grid=(s_len // TM,), in_specs=in_specs,
        out_specs=[pl.BlockSpec((TM, d), lambda i: (i, 0)), pl.BlockSpec((1, d), lambda i: (0, 0))],
        out_shape=[jax.ShapeDtypeStruct((s_len, d), f32), jax.ShapeDtypeStruct((1, d), f32)],
        compiler_params=_cparams(("arbitrary",)),
    )(*args)


def _mlp_fwd(h1, g, w_up, w_down, layer, name, tf=1024):
    s_len, d = h1.shape
    ff = w_up.shape[2]

    def body(h_ref, g_ref, wu_ref, wd_ref, h2_ref, a_ref, hn_ref):
        x = h_ref[...]
        r = lax.rsqrt(jnp.mean(x * x, axis=-1, keepdims=True) + EPS)
        hn = ((x * r) * g_ref[...]).astype(hn_ref.dtype)
        hn_ref[...] = hn
        acc = x
        for j in range(ff // tf):
            a = _dot(hn, wu_ref[:, j * tf:(j + 1) * tf])
            a_ref[:, j * tf:(j + 1) * tf] = a.astype(a_ref.dtype)
            relu = jnp.maximum(a, 0.0)
            acc = acc + _dot(_mx(relu * relu), wd_ref[j * tf:(j + 1) * tf, :])
        h2_ref[...] = acc

    row = lambda i: (i, 0)
    return pl.pallas_call(
        body, name=name, grid=(s_len // TM,),
        in_specs=[pl.BlockSpec((TM, d), row), pl.BlockSpec((1, d), lambda i: (0, 0)),
                  _resident((None, d, ff), lambda i: (layer, 0, 0)), _resident((None, ff, d), lambda i: (layer, 0, 0))],
        out_specs=[pl.BlockSpec((TM, d), row), pl.BlockSpec((TM, ff), row), pl.BlockSpec((TM, d), row)],
        out_shape=[jax.ShapeDtypeStruct((s_len, d), f32), jax.ShapeDtypeStruct((s_len, ff), MXU_DTYPE),
                   jax.ShapeDtypeStruct((s_len, d), MXU_DTYPE)],
        compiler_params=_cparams(("parallel",)),
    )(h1, g, w_up, w_down)


def _mlp_bwd(dh2, w_down, w_up, layer, a, h1, g, name, tf=1024):
    s_len, d = dh2.shape
    ff = a.shape[1]

    def body(dh_ref, wd_ref, wu_ref, a_ref, h_ref, g_ref, dh1_ref, dg_ref, da_ref):
        i = pl.program_id(0)
        dh = dh_ref[...]
        dhb = _mx(dh)
        acc = None
        for j in range(ff // tf):
            cols = slice(j * tf, (j + 1) * tf)
            dact = _dot_nt(dhb, wd_ref[cols, :])
            da = (dact * (2.0 * jnp.maximum(a_ref[:, cols].astype(f32), 0.0))).astype(da_ref.dtype)
            da_ref[:, cols] = da
            part = _dot_nt(da, wu_ref[:, cols])
            acc = part if acc is None else acc + part
        dx, dgrow = _rmsnorm_bwd(acc, h_ref[...], g_ref[...])
        dh1_ref[...] = dh + dx
        dgsum = jnp.sum(dgrow, axis=0, keepdims=True)

        @pl.when(i == 0)
        def _():
            dg_ref[...] = dgsum

        @pl.when(i > 0)
        def _():
            dg_ref[...] += dgsum

    row = lambda i: (i, 0)
    return pl.pallas_call(
        body, name=name, grid=(s_len // TM,),
        in_specs=[pl.BlockSpec((TM, d), row), _resident((None, ff, d), lambda i: (layer, 0, 0)),
                  _resident((None, d, ff), lambda i: (layer, 0, 0)), pl.BlockSpec((TM, ff), row),
                  pl.BlockSpec((TM, d), row), pl.BlockSpec((1, d), lambda i: (0, 0))],
        out_specs=[pl.BlockSpec((TM, d), row), pl.BlockSpec((1, d), lambda i: (0, 0)), pl.BlockSpec((TM, ff), row)],
        out_shape=[jax.ShapeDtypeStruct((s_len, d), f32), jax.ShapeDtypeStruct((1, d), f32),
                   jax.ShapeDtypeStruct((s_len, ff), MXU_DTYPE)],
        compiler_params=_cparams(("arbitrary",), vmem=VMEM_LIMIT_LARGE),
    )(dh2, w_down, w_up, a, h1, g)


def _weight_grad(a, b, name, act=False):
    s_len, k_dim = a.shape
    n = b.shape[1]
    tka = min(k_dim, 2048)
    tnb = n if n <= 1024 else (2048 if n % 2048 == 0 else 640)
    ns = s_len // TM_WGRAD

    def body(a_ref, b_ref, o_ref, acc_ref):
        s = pl.program_id(2)
        x = a_ref[...]
        if act:
            relu = jnp.maximum(x.astype(f32), 0.0)
            x = relu * relu
        part = _dot_tn(_mx(x), _mx(b_ref[...]))

        @pl.when(s == 0)
        def _():
            acc_ref[...] = part

        @pl.when(s > 0)
        def _():
            acc_ref[...] += part

        @pl.when(s == ns - 1)
        def _():
            o_ref[...] = acc_ref[...].astype(o_ref.dtype)

    return pl.pallas_call(
        body, name=name, grid=(k_dim // tka, n // tnb, ns),
        in_specs=[pl.BlockSpec((TM_WGRAD, tka), lambda i, j, s: (s, i)),
                  pl.BlockSpec((TM_WGRAD, tnb), lambda i, j, s: (s, j))],
        out_specs=pl.BlockSpec((tka, tnb), lambda i, j, s: (i, j)),
        out_shape=jax.ShapeDtypeStruct((k_dim, n), COMM_DTYPE),
        scratch_shapes=[pltpu.VMEM((tka, tnb), f32)],
        compiler_params=_cparams(("parallel", "parallel", "arbitrary")),
    )(a, b)


def _group_select(lane, x2, x4, x8, x16):
    grp = lane // POOL_GC
    return jnp.where(grp == 0, x2, jnp.where(grp == 1, x4, jnp.where(grp == 2, x8, x16)))


def _pool_window(lane):
    grp = lane // POOL_GC
    return jnp.where(grp == 0, 2, jnp.where(grp == 1, 4, jnp.where(grp == 2, 8, 16)))


def _pool_y(u, halo, i):
    xs = jnp.concatenate([jnp.where(i > 0, halo, 0.0), u], axis=0)
    s2 = xs + pltpu.roll(xs, 1, axis=0)
    s4 = s2 + pltpu.roll(s2, 2, axis=0)
    s8 = s4 + pltpu.roll(s4, 4, axis=0)
    s16 = s8 + pltpu.roll(s8, 8, axis=0)
    lane = lax.broadcasted_iota(jnp.int32, xs.shape, 1)
    sel = _group_select(lane, s2, s4, s8, s16)[HALO:, :]
    t = i * TM + lax.broadcasted_iota(jnp.int32, u.shape, 0)
    cnt = jnp.minimum(_pool_window(lax.broadcasted_iota(jnp.int32, u.shape, 1)), t + 1).astype(f32)
    return sel / cnt - u


def _group_weights(l0, l1, l2):
    mx = jnp.maximum(jnp.maximum(l0, l1), l2)
    e0, e1, e2 = jnp.exp(l0 - mx), jnp.exp(l1 - mx), jnp.exp(l2 - mx)
    den = e0 + e1 + e2
    return e0 / den, e1 / den, e2 / den


def _mixer_merge(z, wbd, scale, outs, lses, name):
    s_len = z.shape[0]

    def body(u_ref, halo_ref, wbd_ref, sc_ref, o0, o1, o2, l0, l1, l2, m_ref):
        i = pl.program_id(0)
        y = _pool_y(u_ref[...], halo_ref[...], i)
        pool = _dot(_mx(y), wbd_ref[...]) * sc_ref[...]
        w0, w1, w2 = _group_weights(l0[...], l1[...], l2[...])
        m_ref[...] = jnp.concatenate([pool, o0[...] * w0, o1[...] * w1, o2[...] * w2], axis=1).astype(m_ref.dtype)

    row = lambda i: (i, 0)
    blk = pl.BlockSpec((TM, 256), row)
    grp = [pl.BlockSpec((TM, 256), lambda i, g=g: (i, g)) for g in range(3)]
    return pl.pallas_call(
        body, name=name, grid=(s_len // TM,),
        in_specs=[blk, pl.BlockSpec((HALO, 256), lambda i: (jnp.maximum(i * (TM // HALO) - 1, 0), 0)),
                  pl.BlockSpec((256, 256), lambda i: (0, 0)), pl.BlockSpec((1, 256), lambda i: (0, 0))] + grp + grp,
        out_specs=pl.BlockSpec((TM, D_MODEL), row),
        out_shape=jax.ShapeDtypeStruct((s_len, D_MODEL), MXU_DTYPE),
        compiler_params=_cparams(("parallel",)),
    )(z, z, wbd, scale, outs, outs, outs, lses, lses, lses)


def _head_sums(x):
    r = lax.broadcasted_iota(jnp.int32, (256, 256), 0) // HEAD_DIM
    c = lax.broadcasted_iota(jnp.int32, (256, 256), 1) // HEAD_DIM
    ones = jnp.where(r == c, 1.0, 0.0).astype(jnp.bfloat16)
    hi = x.astype(jnp.bfloat16)
    lo = (x - hi.astype(f32)).astype(jnp.bfloat16)
    return _dot(hi, ones) + _dot(lo, ones)


def _out_combine_bwd(dh1, w_out, layer, outs, lses, name, after=None):
    s_len, d = dh1.shape

    def body(dh_ref, w_ref, o0, o1, o2, l0, l1, l2, *rest):
        dp_ref, do_ref, dl_ref = rest[-3:]
        dm = _dot_nt(_mx(dh_ref[...]), w_ref[...])
        dp_ref[...] = dm[:, :POOL_WIDTH]
        w = _group_weights(l0[...], l1[...], l2[...])
        da = [dm[:, POOL_WIDTH + 256 * g:POOL_WIDTH + 256 * (g + 1)] for g in range(3)]
        o = (o0[...], o1[...], o2[...])
        dw = [_head_sums(da[g] * o[g]) for g in range(3)]
        t = w[0] * dw[0] + w[1] * dw[1] + w[2] * dw[2]
        do_ref[...] = jnp.concatenate([da[g] * w[g] for g in range(3)], axis=1)
        dl_ref[...] = jnp.concatenate([w[g] * t for g in range(3)], axis=1)

    grp = [pl.BlockSpec((TM, 256), lambda i, g=g: (i, g)) for g in range(3)]
    in_specs = [pl.BlockSpec((TM, d), lambda i: (i, 0)), _resident((None, d, d), lambda i: (layer, 0, 0))] + grp + grp
    args = [dh1, w_out, outs, outs, outs, lses, lses, lses]
    if after is not None:
        in_specs.append(pl.BlockSpec(memory_space=pl.ANY))
        args.append(after)
    return pl.pallas_call(
        body, name=name, grid=(s_len // TM,), in_specs=in_specs,
        out_specs=[pl.BlockSpec((TM, POOL_WIDTH), lambda i: (i, 0))] + [pl.BlockSpec((TM, ATTN_WIDTH), lambda i: (i, 0))] * 2,
        out_shape=[jax.ShapeDtypeStruct((s_len, POOL_WIDTH), f32)] + [jax.ShapeDtypeStruct((s_len, ATTN_WIDTH), f32)] * 2,
        compiler_params=_cparams(("parallel",)),
    )(*args)


def _pool_bwd(z, dm, wbd, scale, name):
    s_len = z.shape[0]
    n_halo = s_len // HALO

    def body(u_ref, uh_ref, d_ref, dh_ref, wbd_ref, sc_ref, du_ref, dw_ref, dsc_ref):
        i = pl.program_id(0)
        last = pl.num_programs(0) - 1
        y = _pool_y(u_ref[...], uh_ref[...], i)
        yb = _mx(y)
        dpo = d_ref[...]
        sc = sc_ref[...]
        dsc = jnp.sum(dpo * _dot(yb, wbd_ref[...]), axis=0, keepdims=True)
        dwp = _dot_tn(yb, _mx(dpo * sc))

        @pl.when(i == 0)
        def _():
            dsc_ref[...] = dsc
            dw_ref[...] = dwp

        @pl.when(i > 0)
        def _():
            dsc_ref[...] += dsc
            dw_ref[...] += dwp

        ext = jnp.concatenate([dpo, jnp.where(i < last, dh_ref[...], 0.0)], axis=0)
        dy = _dot_nt(_mx(ext * sc), wbd_ref[...])
        t = i * TM + lax.broadcasted_iota(jnp.int32, ext.shape, 0)
        lane = lax.broadcasted_iota(jnp.int32, ext.shape, 1)
        e = dy / jnp.minimum(_pool_window(lane), t + 1).astype(f32)
        rows = ext.shape[0]
        f2 = e + pltpu.roll(e, rows - 1, axis=0)
        f4 = f2 + pltpu.roll(f2, rows - 2, axis=0)
        f8 = f4 + pltpu.roll(f4, rows - 4, axis=0)
        f16 = f8 + pltpu.roll(f8, rows - 8, axis=0)
        du_ref[...] = (_group_select(lane, f2, f4, f8, f16) - dy)[:TM, :].astype(du_ref.dtype)

    row = lambda i: (i, 0)
    blk = pl.BlockSpec((TM, 256), row)
    return pl.pallas_call(
        body, name=name, grid=(s_len // TM,),
        in_specs=[blk, pl.BlockSpec((HALO, 256), lambda i: (jnp.maximum(i * (TM // HALO) - 1, 0), 0)),
                  blk, pl.BlockSpec((HALO, 256), lambda i: (jnp.minimum((i + 1) * (TM // HALO), n_halo - 1), 0)),
                  pl.BlockSpec((256, 256), lambda i: (0, 0)), pl.BlockSpec((1, 256), lambda i: (0, 0))],
        out_specs=[blk, pl.BlockSpec((256, 256), lambda i: (0, 0)), pl.BlockSpec((1, 256), lambda i: (0, 0))],
        out_shape=[jax.ShapeDtypeStruct((s_len, N_IN), MXU_DTYPE), jax.ShapeDtypeStruct((256, 256), f32),
                   jax.ShapeDtypeStruct((1, 256), f32)],
        compiler_params=_cparams(("arbitrary",)),
    )(z, z, dm, dm, wbd, scale)


def _tri_masks():
    qi = lax.broadcasted_iota(jnp.int32, (BLK, BLK), 0)
    ki = lax.broadcasted_iota(jnp.int32, (BLK, BLK), 1)
    return qi >= ki, ki >= qi


ATTN_SUPER_PER_STEP = (8, 2, 1)
Q_COL, K_COL, V_COL = POOL_WIDTH // 128, (POOL_WIDTH + ATTN_WIDTH) // 128, (POOL_WIDTH + 2 * ATTN_WIDTH) // 128


def _rows(ref, start, dil):
    if dil == 1:
        return ref[pl.ds(start, BLK), :]
    return ref[pl.ds(start, BLK, stride=dil), :]


RESIDUE_UNROLL = 4


def _for_residues(dil, fn, loop=True):
    if dil <= RESIDUE_UNROLL or not loop:
        for r in range(dil):
            fn(r, 0)
    else:
        lax.fori_loop(0, dil, fn, 0, unroll=RESIDUE_UNROLL)


def _set_rows(ref, start, dil, val):
    if dil == 1:
        ref[pl.ds(start, BLK), :] = val
    else:
        ref[pl.ds(start, BLK, stride=dil), :] = val


def _attn_fwd(z, g, prev, name):
    s_len = z.shape[0]
    dil, m = DILATIONS[g], ATTN_SUPER_PER_STEP[g]
    sbr = BLK * dil
    rows = sbr * m

    def body(*refs):
        q_ref, kc_ref, kp_ref, vc_ref, vp_ref = refs[:5]
        o_ref, l_ref = refs[-2:]
        st = pl.program_id(0)
        low, up = _tri_masks()
        head0 = lax.broadcasted_iota(jnp.int32, (BLK, 128), 1) < HEAD_DIM
        for sb in range(m):
            valid = jnp.concatenate([up & (st > 0) if sb == 0 else up, low], axis=1)

            def one_residue(r, carry, sb=sb, valid=valid):
                base = sb * sbr + r
                q = _rows(q_ref, base, dil)
                kc, vc = _rows(kc_ref, base, dil), _rows(vc_ref, base, dil)
                if sb == 0:
                    kp, vp = _rows(kp_ref, r, dil), _rows(vp_ref, r, dil)
                else:
                    kp, vp = _rows(kc_ref, base - sbr, dil), _rows(vc_ref, base - sbr, dil)
                k2 = jnp.concatenate([_mx(kp), _mx(kc)], axis=0)
                v2 = jnp.concatenate([_mx(vp), _mx(vc)], axis=0)
                qs = q * ATTN_SCALE
                outs, lses = [], []
                for hh in range(2):
                    s = jnp.where(valid, _dot_nt(_mx(jnp.where(head0 == (hh == 0), qs, 0.0)), k2), NEG_BIG)
                    mx = jnp.max(s, axis=-1, keepdims=True)
                    e = jnp.exp(s - mx)
                    l = jnp.sum(e, axis=-1, keepdims=True)
                    outs.append(_dot(_mx(e / l), v2))
                    lses.append(jnp.broadcast_to(mx + jnp.log(l), (BLK, 128)))
                _set_rows(o_ref, base, dil, jnp.where(head0, outs[0], outs[1]))
                _set_rows(l_ref, base, dil, jnp.where(head0, lses[0], lses[1]))
                return carry

            _for_residues(dil, one_residue, loop=False)

    def cur(col):
        return pl.BlockSpec((rows, 128), lambda st, hp: (st, col + 2 * g + hp))

    def before(col):
        return pl.BlockSpec((sbr, 128), lambda st, hp: (jnp.maximum(st * m - 1, 0), col + 2 * g + hp))

    in_specs = [cur(Q_COL), cur(K_COL), before(K_COL), cur(V_COL), before(V_COL)]
    args = [z, z, z, z, z]
    aliases = {}
    if prev is not None:
        in_specs += [pl.BlockSpec(memory_space=pl.ANY)] * 2
        args += list(prev)
        aliases = {5: 0, 6: 1}
    return pl.pallas_call(
        body, name=name, grid=(s_len // rows, 2), in_specs=in_specs, out_specs=[cur(0), cur(0)],
        out_shape=[jax.ShapeDtypeStruct((s_len, ATTN_WIDTH), f32)] * 2, input_output_aliases=aliases,
        compiler_params=_cparams(("parallel", "parallel")),
    )(*args)


def _stack_heads(x, head0):
    return jnp.concatenate([_mx(jnp.where(head0, x, 0.0)), _mx(jnp.where(head0, 0.0, x))], axis=0)


def _head_rows(x):
    xt = x.T
    return jnp.concatenate([jnp.broadcast_to(xt[0:1, :], (BLK, BLK)),
                            jnp.broadcast_to(xt[HEAD_DIM:HEAD_DIM + 1, :], (BLK, BLK))], axis=0)


def _attn_bwd(z, do, lse, dlt, tabs, dz, g, name):
    s_len = z.shape[0]
    dil, m = DILATIONS[g], ATTN_SUPER_PER_STEP[g]
    sbr = BLK * dil
    rows = sbr * m
    nsteps = s_len // rows

    def body(q_ref, qn_ref, kc_ref, kp_ref, vc_ref, vp_ref, do_ref, don_ref, l_ref, ln_ref, d_ref, dn_ref,
             c_ref, s1_ref, s2_ref, dz_in, dz_ref, dq_buf, dk_buf, dv_buf, out_buf, sems):
        del dz_in
        st, hp = pl.program_id(0), pl.program_id(1)
        head0 = lax.broadcasted_iota(jnp.int32, (BLK, 128), 1) < HEAD_DIM
        key_i = lax.broadcasted_iota(jnp.int32, (2 * BLK, BLK), 0) & (BLK - 1)
        query_i = lax.broadcasted_iota(jnp.int32, (2 * BLK, BLK), 1)
        same_t, cross_t = query_i >= key_i, key_i >= query_i
        for sb in range(m):
            prev_t = cross_t & (st > 0) if sb == 0 else cross_t
            next_t = cross_t & (st < nsteps - 1) if sb == m - 1 else cross_t

            def one_residue(r, carry, sb=sb, prev_t=prev_t, next_t=next_t):
                base = sb * sbr + r
                q, k, v = _rows(q_ref, base, dil), _rows(kc_ref, base, dil), _rows(vc_ref, base, dil)
                do_c, l_c, d_c = _rows(do_ref, base, dil), _rows(l_ref, base, dil), _rows(d_ref, base, dil)
                if sb == 0:
                    kp, vp = _rows(kp_ref, r, dil), _rows(vp_ref, r, dil)
                else:
                    kp, vp = _rows(kc_ref, base - sbr, dil), _rows(vc_ref, base - sbr, dil)
                if sb == m - 1:
                    qn, do_n = _rows(qn_ref, r, dil), _rows(don_ref, r, dil)
                    l_n, d_n = _rows(ln_ref, r, dil), _rows(dn_ref, r, dil)
                else:
                    qn, do_n = _rows(q_ref, base + sbr, dil), _rows(do_ref, base + sbr, dil)
                    l_n, d_n = _rows(l_ref, base + sbr, dil), _rows(d_ref, base + sbr, dil)
                k2, kp2, v2, vp2 = _stack_heads(k, head0), _stack_heads(kp, head0), _stack_heads(v, head0), _stack_heads(vp, head0)
                qb, qnb, dob, donb = _mx(q), _mx(qn), _mx(do_c), _mx(do_n)
                lse2, dlt2, lsen2, dltn2 = _head_rows(l_c), _head_rows(d_c), _head_rows(l_n), _head_rows(d_n)

                def pair(keys, vals, qs, dos, lse_rows, dlt_rows, valid):
                    p = jnp.where(valid, jnp.exp(_dot_nt(keys, qs) * ATTN_SCALE - lse_rows), 0.0)
                    ds = _mx(p * (_dot_nt(vals, dos) - dlt_rows) * ATTN_SCALE)
                    return _mx(p), ds

                p_a, ds_a = pair(k2, v2, qb, dob, lse2, dlt2, same_t)
                _, ds_b = pair(kp2, vp2, qb, dob, lse2, dlt2, prev_t)
                p_c, ds_c = pair(k2, v2, qnb, donb, lsen2, dltn2, next_t)
                dq = _dot_tn(ds_a, k2) + _dot_tn(ds_b, kp2)
                dk2 = _dot(ds_a, qb) + _dot(ds_c, qnb)
                dv2 = _dot(p_a, dob) + _dot(p_c, donb)
                c, s1, s2 = _rows(c_ref, base, dil), _rows(s1_ref, base, dil), _rows(s2_ref, base, dil)
                _set_rows(dq_buf, base, dil, _rope_transpose(dq, c, s1, s2, 128))
                _set_rows(dk_buf, base, dil, _rope_transpose(jnp.where(head0, dk2[:BLK], dk2[BLK:]), c, s1, s2, 128))
                _set_rows(dv_buf, base, dil, jnp.where(head0, dv2[:BLK], dv2[BLK:]))
                return carry

            _for_residues(dil, one_residue)
        copies = []
        for t, (buf, col) in enumerate(((dq_buf, Q_COL), (dk_buf, K_COL), (dv_buf, V_COL))):
            out_buf[t] = buf[...].astype(out_buf.dtype)
            lane0 = pl.multiple_of((col + 2 * g + hp) * 128, 128)
            dst = dz_ref.at[pl.ds(pl.multiple_of(st * rows, rows), rows), pl.ds(lane0, 128)]
            cp = pltpu.make_async_copy(out_buf.at[t], dst, sems.at[t])
            cp.start()
            copies.append(cp)
        for cp in copies:
            cp.wait()

    def cur(col):
        return pl.BlockSpec((rows, 128), lambda st, hp: (st, col + 2 * g + hp))

    def before(col):
        return pl.BlockSpec((sbr, 128), lambda st, hp: (jnp.maximum(st * m - 1, 0), col + 2 * g + hp))

    def after(col):
        return pl.BlockSpec((sbr, 128), lambda st, hp: (jnp.minimum((st + 1) * m, s_len // sbr - 1), col + 2 * g + hp))

    tab = pl.BlockSpec((rows, 128), lambda st, hp: (st, 0))
    return pl.pallas_call(
        body, name=name, grid=(nsteps, 2),
        in_specs=[cur(Q_COL), after(Q_COL), cur(K_COL), before(K_COL), cur(V_COL), before(V_COL),
                  cur(0), after(0), cur(0), after(0), cur(0), after(0), tab, tab, tab,
                  pl.BlockSpec(memory_space=pl.ANY)],
        out_specs=pl.BlockSpec(memory_space=pl.ANY),
        out_shape=jax.ShapeDtypeStruct(dz.shape, dz.dtype), input_output_aliases={15: 0},
        scratch_shapes=[pltpu.VMEM((rows, 128), f32)] * 3 + [pltpu.VMEM((3, rows, 128), dz.dtype),
                                                            pltpu.SemaphoreType.DMA((3,))],
        compiler_params=_cparams(("arbitrary", "arbitrary")),
    )(z, z, z, z, z, z, do, do, lse, lse, dlt, dlt, *tabs, dz)


def _loss_head(h, g, target, name):
    s_len, d = h.shape

    def body(h_ref, g_ref, t_ref, loss_ref, dh_ref, dg_ref):
        i = pl.program_id(0)
        x = h_ref[...]
        gv = g_ref[...]
        r = lax.rsqrt(jnp.mean(x * x, axis=-1, keepdims=True) + EPS)
        xh = x * r
        diff = xh * gv - t_ref[...]
        part = 0.5 * jnp.sum(jnp.mean(diff * diff, axis=-1, keepdims=True), axis=0, keepdims=True)
        dy = diff * (1.0 / d)
        dxh = dy * gv
        dh_ref[...] = r * (dxh - xh * jnp.mean(dxh * xh, axis=-1, keepdims=True))
        dgsum = jnp.sum(dy * xh, axis=0, keepdims=True)
        lossb = jnp.broadcast_to(part, (8, 128))

        @pl.when(i == 0)
        def _():
            loss_ref[...] = lossb
            dg_ref[...] = dgsum

        @pl.when(i > 0)
        def _():
            loss_ref[...] += lossb
            dg_ref[...] += dgsum

    row = lambda i: (i, 0)
    return pl.pallas_call(
        body, name=name, grid=(s_len // TM,),
        in_specs=[pl.BlockSpec((TM, d), row), pl.BlockSpec((1, d), lambda i: (0, 0)), pl.BlockSpec((TM, d), row)],
        out_specs=[pl.BlockSpec((8, 128), lambda i: (0, 0)), pl.BlockSpec((TM, d), row),
                   pl.BlockSpec((1, d), lambda i: (0, 0))],
        out_shape=[jax.ShapeDtypeStruct((8, 128), f32), jax.ShapeDtypeStruct((s_len, d), f32),
                   jax.ShapeDtypeStruct((1, d), f32)],
        compiler_params=_cparams(("arbitrary",)),
    )(h, g, target)


def _rope_tables(positions):
    inv_freq = ROPE_THETA ** (-jnp.arange(0, ROT_DIM, 2, dtype=f32) / ROT_DIM)
    ang = positions.astype(f32)[:, None] * inv_freq
    cos, sin = jnp.cos(ang), jnp.sin(ang)
    s_len = positions.shape[0]
    zero8, rest = jnp.zeros((s_len, 8), f32), jnp.zeros((s_len, HEAD_DIM - ROT_DIM), f32)
    c = jnp.concatenate([cos, cos, jnp.ones((s_len, HEAD_DIM - ROT_DIM), f32)], axis=1)
    s1 = jnp.concatenate([-sin, zero8, rest], axis=1)
    s2 = jnp.concatenate([zero8, sin, rest], axis=1)
    return c, s1, s2


def _block_diag(pool_w):
    out = jnp.zeros((POOL_WIDTH, POOL_WIDTH), pool_w.dtype)
    for g in range(4):
        out = lax.dynamic_update_slice(out, pool_w[g], (g * POOL_GC, g * POOL_GC))
    return out


def _layer_fwd(h, p_l, wsrc, small, layer, tabs):
    nm = f"l{layer}_"
    wts, wl = wsrc.take(layer, ("w_in",), (h,) if layer else tuple(tabs))
    z, hn1 = _norm_matmul(h, small["norm1"][layer][None], wts["w_in"], wl, 256, nm + "in_proj", rope=tabs)
    ol = None
    for g in range(3):
        ol = _attn_fwd(z, g, ol, nm + f"attn_fwd{g}")
    outs, lses = ol
    wbd = _mx(_block_diag(small["pool_w"][layer]))
    scale = small["pool_scale"][layer][None]
    m = _mixer_merge(z, wbd, scale, outs, lses, nm + "mixer_merge")
    wts.update(wsrc.take(layer, ("w_out",), (m,))[0])
    h1 = _matmul_residual(m, wts["w_out"], wl, h, nm + "out_proj")
    wts.update(wsrc.take(layer, ("w_up", "w_down"), (h1,))[0])
    h2, a, hn2 = _mlp_fwd(h1, small["norm2"][layer][None], wts["w_up"], wts["w_down"], wl, nm + "mlp")
    wts.update(wsrc.take(layer, ("w_gate", "w_ple"), (h2,))[0])
    h3, gl, hn3 = _gate_ple_fwd(h2, small["norm3"][layer][None], wts["w_gate"], wts["w_ple"], wl, p_l, nm + "gate_ple")
    saved = dict(h=h, z=z, hn1=hn1, outs=outs, lses=lses, wbd=wbd, scale=scale, m=m, h1=h1, a=a, hn2=hn2, h2=h2,
                 gl=gl, hn3=hn3, wts=wts, wl=wl)
    return h3, saved


def _layer_bwd(dh3, sv, p_l, small, layer, tabs128, reducer):
    nm = f"l{layer}_"
    wts, wl = sv["wts"], sv["wl"]
    dh2, dg3, de, dgl = _gate_bwd(dh3, sv["gl"], p_l, wts["w_ple"], wts["w_gate"], wl, sv["h2"],
                                  small["norm3"][layer][None], nm + "gate_bwd")
    reducer.add("w_gate", layer, _weight_grad(sv["hn3"], dgl, nm + "dw_gate"))
    reducer.add("w_ple", layer, _weight_grad(p_l, de, nm + "dw_ple"))
    dh1, dg2, da = _mlp_bwd(dh2, wts["w_down"], wts["w_up"], wl, sv["a"], sv["h1"], small["norm2"][layer][None],
                            nm + "mlp_bwd")
    reducer.add("w_down", layer, _weight_grad(sv["a"], dh2, nm + "dw_down", act=True))
    started = reducer.add("w_up", layer, _weight_grad(sv["hn2"], da, nm + "dw_up"))
    dpool, do, dlt = _out_combine_bwd(dh1, wts["w_out"], wl, sv["outs"], sv["lses"], nm + "out_bwd", after=started)
    reducer.add("w_out", layer, _weight_grad(sv["m"], dh1, nm + "dw_out"))
    dz, dwbd, dscale = _pool_bwd(sv["z"], dpool, sv["wbd"], sv["scale"], nm + "pool_bwd")
    for g in range(3):
        dz = _attn_bwd(sv["z"], do, sv["lses"], dlt, tabs128, dz, g, nm + f"attn_bwd{g}")
    started = reducer.add("w_in", layer, _weight_grad(sv["hn1"], dz, nm + "dw_in"))
    dh0, dg1 = _matmul_nt_norm_bwd(dz, wts["w_in"], wl, sv["h"], small["norm1"][layer][None], dh1, nm + "in_bwd",
                                   tk=512, after=started)
    dpool_w = jnp.stack([dwbd[g * POOL_GC:(g + 1) * POOL_GC, g * POOL_GC:(g + 1) * POOL_GC] for g in range(4)])
    sg = dict(norm1=dg1[0], norm2=dg2[0], norm3=dg3[0], pool_w=dpool_w, pool_scale=dscale[0])
    return dh0, sg


def _local_step(x, p, positions, wsrc, small, target, reducer):
    tabs128 = tuple(jnp.tile(t, (1, 2)) for t in _rope_tables(positions))
    h = x
    saved = []
    for layer in range(2):
        h, sv = _layer_fwd(h, p[layer], wsrc, small, layer, tabs128)
        saved.append(sv)
    loss, dh, dgf = _loss_head(h, small["final_norm"][None], target, "loss_head")
    sgs = [None, None]
    for layer in (1, 0):
        dh, sgs[layer] = _layer_bwd(dh, saved[layer], p[layer], small, layer, tabs128, reducer)
    small_grads = {k: jnp.stack([sgs[0][k], sgs[1][k]]) for k in sgs[0]}
    small_grads["final_norm"] = dgf[0]
    return loss, dh, small_grads


HBM = pl.BlockSpec(memory_space=pltpu.HBM)


def _my_place():
    return lax.axis_index("x"), lax.axis_index("y"), lax.axis_index("c")


def _other_chips(x, y):
    return [(1 - x, y), (x, 1 - y), (1 - x, 1 - y)]


def _window(ref, name, chip):
    k, n = _shard_shape(name)
    if COL_SHARDED[name]:
        return ref.at[:, pl.ds(pl.multiple_of(chip * n, 128), n)]
    return ref.at[pl.ds(pl.multiple_of(chip * k, 128), k), :]


def _chip_index():
    return jnp.reshape(2 * lax.axis_index("x") + lax.axis_index("y"), (1,)).astype(jnp.int32)


def _shard_block(name, tr):
    ks, ns = _shard_shape(name)
    if COL_SHARDED[name]:
        return (tr, ns), lambda i, me: (i, me[0])
    return (tr, ns), lambda i, me: (me[0] * (ks // tr) + i, 0)


def _place_shard(w, name, layer):
    ks, ns = _shard_shape(name)
    tr = min(ks, 256)
    shape, index = _shard_block(name, tr)

    def body(me_ref, w_ref, o_ref):
        o_ref[...] = w_ref[...].astype(o_ref.dtype)

    return pl.pallas_call(
        body, name=f"place_{name}{layer}",
        grid_spec=pltpu.PrefetchScalarGridSpec(
            num_scalar_prefetch=1, grid=(ks // tr,),
            in_specs=[pl.BlockSpec((None, tr, ns), lambda i, me: (layer, i, 0))],
            out_specs=pl.BlockSpec((None,) + shape, lambda i, me: (0,) + index(i, me))),
        out_shape=jax.ShapeDtypeStruct((1,) + FULL_SHAPE[name], MXU_DTYPE),
        compiler_params=_cparams(("parallel",)),
    )(_chip_index(), w)


GATHER_ORDER = [("w_in", 0), ("w_out", 0), ("w_up", 0), ("w_down", 0), ("w_gate", 0), ("w_ple", 0),
                ("w_in", 1), ("w_out", 1), ("w_up", 1), ("w_down", 1), ("w_gate", 1), ("w_ple", 1)]
SEM = pl.BlockSpec(memory_space=pltpu.SEMAPHORE)
EFFECT = pltpu.SideEffectType.DATAFLOW_SIDE_EFFECTING


def _gather_copy(src_ref, dst_ref, name, idx, j, chip, send_sems, recv_sems, c):
    cx, cy = chip
    return pltpu.make_async_remote_copy(
        src_ref=src_ref, dst_ref=dst_ref, send_sem=send_sems.at[3 * idx + j], recv_sem=recv_sems.at[3 * idx + j],
        device_id=(cx, cy, c), device_id_type=MESH)


def _gather_start(placed, order, tag, after=None):
    n = len(order)
    extra = [] if after is None else [after]

    def body(*refs):
        ins = refs[:n]
        k = n + len(extra)
        send_sems, recv_sems = refs[k], refs[k + 1]
        outs = refs[k + 2:k + 2 + n]
        token = refs[-1]
        x, y, c = _my_place()
        me = 2 * x + y
        for idx, (name, _) in enumerate(order):
            for j, chip in enumerate(_other_chips(x, y)):
                _gather_copy(_window(ins[idx].at[0], name, me), _window(outs[idx].at[0], name, me), name, idx, j, chip,
                             send_sems, recv_sems, c).start()
        token[...] = jnp.zeros_like(token)

    res = pl.pallas_call(
        body, name="gather_start" + tag,
        out_shape=(pltpu.SemaphoreType.DMA((3 * n,)), pltpu.SemaphoreType.DMA((3 * n,)))
        + tuple(pltpu.HBM(a.shape, a.dtype) for a in placed) + (jax.ShapeDtypeStruct((8, 128), f32),),
        in_specs=[HBM] * n + [pl.BlockSpec(memory_space=pl.ANY)] * len(extra),
        out_specs=(SEM, SEM) + (HBM,) * n + (pl.BlockSpec(memory_space=pltpu.VMEM),),
        input_output_aliases={i: i + 2 for i in range(n)},
        compiler_params=pltpu.CompilerParams(has_side_effects=EFFECT),
    )(*[pltpu.with_memory_space_constraint(a, pltpu.HBM) for a in placed], *extra)
    return res[0], res[1], list(res[2:2 + n]), res[-1]


def _gather_wait(send_sems, recv_sems, arrays, order, idxs, after, name):
    n = len(idxs)

    def body(*refs):
        ins = refs[:n]
        send_ref, recv_ref = refs[n], refs[n + 1]
        x, y, c = _my_place()
        me = 2 * x + y
        for k, idx in enumerate(idxs):
            wname = order[idx][0]
            for j, chip in enumerate(_other_chips(x, y)):
                cx, cy = chip
                mine = _window(ins[k].at[0], wname, me)
                land = _window(ins[k].at[0], wname, 2 * cx + cy)
                _gather_copy(mine, mine, wname, idx, j, chip, send_ref, recv_ref, c).wait_send()
                _gather_copy(land, land, wname, idx, j, chip, send_ref, recv_ref, c).wait_recv()

    operands = list(arrays) + [send_sems, recv_sems] + list(after)
    in_specs = [HBM] * n + [SEM, SEM] + [pl.BlockSpec(memory_space=pl.ANY)] * len(after)
    res = pl.pallas_call(
        body, name=name, out_shape=tuple(pltpu.HBM(a.shape, a.dtype) for a in arrays),
        in_specs=in_specs, out_specs=(HBM,) * n, input_output_aliases={i: i for i in range(n)},
        compiler_params=pltpu.CompilerParams(has_side_effects=EFFECT),
    )(*operands)
    return list(res)


class _GatheredWeights:
    def __init__(self, shards):
        self.starts = []
        token = None
        for tag, order in (("_first", GATHER_ORDER[:1]), ("_rest", GATHER_ORDER[1:])):
            placed = [_place_shard(shards[name], name, layer) for name, layer in order]
            self.starts.append((order,) + _gather_start(placed, order, tag, token))
            token = self.starts[-1][-1]

    def take(self, layer, names, after):
        order, send, recv, arrays, _ = next(s for s in self.starts if (names[0], layer) in s[0])
        after = list(after)
        if order is self.starts[0][0]:
            after.append(self.starts[-1][-1])
        idxs = [order.index((n, layer)) for n in names]
        got = _gather_wait(send, recv, [arrays[i] for i in idxs], order, idxs, after, f"gather_wait{layer}_{names[0]}")
        return dict(zip(names, got)), 0


N_DEV = 8


def _reduce_copies(dws, lands, names, layer, send_sems, recv_sems):
    x, y, c = _my_place()
    me, my_dev = 2 * x + y, 4 * x + 2 * y + c
    out = []
    for t, name in enumerate(names):
        for j, (cx, cy) in enumerate(_other_chips(x, y)):
            out.append((pltpu.make_async_remote_copy(
                src_ref=_window(dws[t], name, 2 * cx + cy), dst_ref=lands[t].at[my_dev],
                send_sem=send_sems.at[4 * t + j], recv_sem=recv_sems.at[N_DEV * t + my_dev],
                device_id=(cx, cy, layer), device_id_type=MESH), False))
        out.append((pltpu.make_async_remote_copy(
            src_ref=_window(dws[t], name, me), dst_ref=lands[t].at[my_dev],
            send_sem=send_sems.at[4 * t + 3], recv_sem=recv_sems.at[N_DEV * t + my_dev],
            device_id=(x, y, layer), device_id_type=MESH), True))
    return out


def _reduce_start(dws, names, layer, tag):
    n = len(names)
    lands = [lax.empty((N_DEV,) + _shard_shape(nm), dws[0].dtype) for nm in names]

    def body(*refs):
        ins = refs[:n]
        send_sems, recv_sems = refs[2 * n], refs[2 * n + 1]
        land_out = refs[3 * n + 2:4 * n + 2]
        token = refs[-1]
        c = lax.axis_index("c")
        for cp, non_owner_only in _reduce_copies(ins, land_out, names, layer, send_sems, recv_sems):
            if non_owner_only:
                @pl.when(c != layer)
                def _():
                    cp.start()
            else:
                cp.start()
        token[...] = jnp.zeros_like(token)

    res = pl.pallas_call(
        body, name="reduce_start" + tag,
        out_shape=(pltpu.SemaphoreType.DMA((4 * n,)), pltpu.SemaphoreType.DMA((N_DEV * n,)))
        + tuple(pltpu.HBM(a.shape, a.dtype) for a in dws) + tuple(pltpu.HBM(a.shape, a.dtype) for a in lands)
        + (jax.ShapeDtypeStruct((8, 128), f32),),
        in_specs=[HBM] * (2 * n),
        out_specs=(SEM, SEM) + (HBM,) * (2 * n) + (pl.BlockSpec(memory_space=pltpu.VMEM),),
        input_output_aliases={i: i + 2 for i in range(2 * n)},
        compiler_params=pltpu.CompilerParams(has_side_effects=EFFECT),
    )(*[pltpu.with_memory_space_constraint(a, pltpu.HBM) for a in list(dws) + lands])
    return res[0], res[1], list(res[2:2 + n]), list(res[2 + n:2 + 2 * n]), res[-1]


def _reduce_wait(send_sems, recv_sems, dws, lands, names, layer, after, tag):
    n = len(names)

    def body(*refs):
        ins, land_in = refs[:n], refs[n:2 * n]
        send_ref, recv_ref = refs[2 * n], refs[2 * n + 1]
        x, y, c = _my_place()
        for cp, non_owner_only in _reduce_copies(ins, land_in, names, layer, send_ref, recv_ref):
            if non_owner_only:
                @pl.when(c != layer)
                def _():
                    cp.wait_send()
            else:
                cp.wait_send()

        @pl.when(c == layer)
        def _():
            for t in range(n):
                for k in range(1, N_DEV):
                    px, py, pc = x ^ ((k >> 2) & 1), y ^ ((k >> 1) & 1), c ^ (k & 1)
                    dev = 4 * px + 2 * py + pc
                    land = land_in[t].at[dev]
                    pltpu.make_async_remote_copy(
                        src_ref=land, dst_ref=land, send_sem=send_ref.at[4 * t], recv_sem=recv_ref.at[N_DEV * t + dev],
                        device_id=(px, py, pc), device_id_type=MESH).wait_recv()

    res = pl.pallas_call(
        body, name="reduce_wait" + tag,
        out_shape=tuple(pltpu.HBM(a.shape, a.dtype) for a in list(dws) + list(lands)),
        in_specs=[HBM] * (2 * n) + [SEM, SEM, pl.BlockSpec(memory_space=pl.ANY)], out_specs=(HBM,) * (2 * n),
        input_output_aliases={i: i for i in range(2 * n)},
        compiler_params=pltpu.CompilerParams(has_side_effects=EFFECT),
    )(*dws, *lands, send_sems, recv_sems, after)
    return list(res[:n]), list(res[n:])


def _sum_devices(land, own, name, layer, prev):
    ks, ns = _shard_shape(name)
    tr = min(ks, 256)
    shape, index = _shard_block(name, tr)

    def body(me_ref, dev_ref, *refs):
        s_ref, own_ref, out_ref = refs[0], refs[1], refs[-1]
        dev = dev_ref[0]
        acc = None
        for s in range(N_DEV):
            term = jnp.where(dev == s, own_ref[...], s_ref[s]).astype(f32)
            acc = term if acc is None else acc + term
        out_ref[...] = acc

    def mine(i, dev):
        return i * jnp.where((dev[0] & 1) == layer, 1, 0)

    in_specs = [pl.BlockSpec((N_DEV, tr, ns), lambda i, me, dev: (0, mine(i, dev), 0)),
                pl.BlockSpec(shape, lambda i, me, dev: index(mine(i, dev), me))]
    args = [land, own]
    aliases = {}
    if prev is not None:
        in_specs.append(pl.BlockSpec(memory_space=pl.ANY))
        args.append(prev)
        aliases = {4: 0}
    x, y, c = _my_place()
    return pl.pallas_call(
        body, name=f"sum_devices_{name}{layer}",
        grid_spec=pltpu.PrefetchScalarGridSpec(
            num_scalar_prefetch=2, grid=(ks // tr,), in_specs=in_specs,
            out_specs=pl.BlockSpec((None, tr, ns), lambda i, me, dev: (layer, mine(i, dev), 0))),
        out_shape=jax.ShapeDtypeStruct((2, ks, ns), f32), input_output_aliases=aliases,
        compiler_params=_cparams(("arbitrary",)),
    )(_chip_index(), jnp.reshape(4 * x + 2 * y + c, (1,)).astype(jnp.int32), *args)


class _GradReducer:
    GROUPS = (("1", 1, ("w_gate", "w_ple", "w_down", "w_up", "w_out", "w_in")),
              ("0a", 0, ("w_gate", "w_ple", "w_down", "w_up")),
              ("0b", 0, ("w_out", "w_in")))

    def __init__(self):
        self.grads = {}
        self.started = {}

    def add(self, name, layer, dw):
        self.grads[(name, layer)] = dw
        token = None
        for tag, glayer, names in self.GROUPS:
            if tag not in self.started and all((nm, glayer) in self.grads for nm in names):
                *self.started[tag], token = _reduce_start([self.grads[(nm, glayer)] for nm in names], names, glayer, tag)
        return token

    def finish(self, after):
        mine = {}
        for tag, layer, names in self.GROUPS:
            send, recv, dws, lands = self.started[tag]
            dws, lands = _reduce_wait(send, recv, dws, lands, names, layer, after, tag)
            for nm, dw, land in zip(names, dws, lands):
                mine[nm] = _sum_devices(land, dw, nm, layer, mine.get(nm))
        return _pair_layers(mine)


def _pair_layers(mine):
    names = list(BIG)

    def body(*refs):
        ins = refs[:len(names)]
        outs = refs[len(names):2 * len(names)]
        send_sems, recv_sems = refs[2 * len(names):]
        x, y, c = _my_place()
        sibling = (x, y, 1 - c)
        cps = []
        for t in range(len(names)):
            cp = pltpu.make_async_remote_copy(
                src_ref=ins[t].at[c], dst_ref=outs[t].at[c], send_sem=send_sems.at[t], recv_sem=recv_sems.at[t],
                device_id=sibling, device_id_type=MESH)
            cp.start()
            cps.append(cp)
        for t in range(len(names)):
            cps[t].wait_send()
            land = outs[t].at[1 - c]
            pltpu.make_async_remote_copy(
                src_ref=land, dst_ref=land, send_sem=send_sems.at[t], recv_sem=recv_sems.at[t],
                device_id=sibling, device_id_type=MESH).wait_recv()

    outs = pl.pallas_call(
        body, name="pair_layers", in_specs=[HBM] * len(names), out_specs=[HBM] * len(names),
        out_shape=[jax.ShapeDtypeStruct((2,) + _shard_shape(n), f32) for n in names],
        input_output_aliases={t: t for t in range(len(names))},
        scratch_shapes=[pltpu.SemaphoreType.DMA((len(names),)), pltpu.SemaphoreType.DMA((len(names),))],
    )(*[mine[n] for n in names])
    return dict(zip(names, outs))


SMALL_ROWS = 320


def _small_copies(vec_ref, land_ref, send_sems, recv_sems):
    x, y, c = _my_place()
    me = 4 * x + 2 * y + c
    out = []
    for k in range(1, N_DEV):
        peer = (x ^ ((k >> 2) & 1), y ^ ((k >> 1) & 1), c ^ (k & 1))
        src_dev = 4 * peer[0] + 2 * peer[1] + peer[2]
        send = pltpu.make_async_remote_copy(
            src_ref=vec_ref, dst_ref=land_ref.at[me], send_sem=send_sems.at[k - 1], recv_sem=recv_sems.at[k - 1],
            device_id=peer, device_id_type=MESH)
        arrival = pltpu.make_async_remote_copy(
            src_ref=land_ref.at[src_dev], dst_ref=land_ref.at[src_dev], send_sem=send_sems.at[k - 1],
            recv_sem=recv_sems.at[k - 1], device_id=peer, device_id_type=MESH)
        out.append((send, arrival))
    return out


def _small_start(vec):
    land = lax.empty((N_DEV,) + vec.shape, vec.dtype)

    def body(v_ref, land_in, send_sems, recv_sems, v_out, land_out):
        del land_in, v_out
        for send, _ in _small_copies(v_ref, land_out, send_sems, recv_sems):
            send.start()

    return pl.pallas_call(
        body, name="small_start",
        out_shape=(pltpu.SemaphoreType.DMA((N_DEV - 1,)), pltpu.SemaphoreType.DMA((N_DEV - 1,)),
                   pltpu.HBM(vec.shape, vec.dtype), pltpu.HBM(land.shape, land.dtype)),
        in_specs=[HBM, HBM], out_specs=(SEM, SEM, HBM, HBM), input_output_aliases={0: 2, 1: 3},
        compiler_params=pltpu.CompilerParams(has_side_effects=EFFECT),
    )(pltpu.with_memory_space_constraint(vec, pltpu.HBM), pltpu.with_memory_space_constraint(land, pltpu.HBM))


def _small_wait(send_sems, recv_sems, vec, land, after):
    def body(v_ref, land_ref, send_ref, recv_ref, after_ref, v_out, land_out):
        del after_ref, v_out, land_out
        for send, arrival in _small_copies(v_ref, land_ref, send_ref, recv_ref):
            send.wait_send()
            arrival.wait_recv()

    return pl.pallas_call(
        body, name="small_wait", out_shape=(pltpu.HBM(vec.shape, vec.dtype), pltpu.HBM(land.shape, land.dtype)),
        in_specs=[HBM, HBM, SEM, SEM, pl.BlockSpec(memory_space=pl.ANY)], out_specs=(HBM, HBM),
        input_output_aliases={0: 0, 1: 1}, compiler_params=pltpu.CompilerParams(has_side_effects=EFFECT),
    )(vec, land, send_sems, recv_sems, after)


def _small_sum(vec, land):
    x, y, c = _my_place()

    def body(dev_ref, v_ref, land_ref, out_ref):
        acc = None
        for s in range(N_DEV):
            term = jnp.where(dev_ref[0] == s, v_ref[...], land_ref[s])
            acc = term if acc is None else acc + term
        out_ref[...] = acc

    return pl.pallas_call(
        body, name="small_sum",
        grid_spec=pltpu.PrefetchScalarGridSpec(
            num_scalar_prefetch=1, grid=(1,),
            in_specs=[pl.BlockSpec(vec.shape, lambda i, dev: (0, 0)), pl.BlockSpec(land.shape, lambda i, dev: (0, 0, 0))],
            out_specs=pl.BlockSpec(vec.shape, lambda i, dev: (0, 0))),
        out_shape=jax.ShapeDtypeStruct(vec.shape, vec.dtype),
        compiler_params=_cparams(("arbitrary",)),
    )(jnp.reshape(4 * x + 2 * y + c, (1,)).astype(jnp.int32), vec, land)


def _adamw(w, g, m, v, name):
    rows, cols = w.shape
    tr = rows
    for cand in (512, 256, 128, 64, 32, 16, 8):
        if rows % cand == 0 and cand * cols * 4 <= 2 * 1024 * 1024:
            tr = cand
            break
    c1 = np.float32(1.0 - ADAM_B1 ** ADAM_STEP)
    c2 = np.float32(1.0 - ADAM_B2 ** ADAM_STEP)

    def body(w_ref, g_ref, m_ref, v_ref, go_ref, d_ref, mo_ref, vo_ref):
        gv = g_ref[...]
        go_ref[...] = gv
        mn = ADAM_B1 * m_ref[...] + (1.0 - ADAM_B1) * gv
        vn = ADAM_B2 * v_ref[...] + (1.0 - ADAM_B2) * (gv * gv)
        mo_ref[...] = mn
        vo_ref[...] = vn
        d_ref[...] = -ADAM_LR * ((mn / c1) / (jnp.sqrt(vn / c2) + ADAM_EPS) + ADAM_WD * w_ref[...])

    blk = pl.BlockSpec((tr, cols), lambda i: (i, 0))
    return pl.pallas_call(
        body, name="adamw_" + name, grid=(rows // tr,), in_specs=[blk] * 4, out_specs=[blk] * 4,
        out_shape=[jax.ShapeDtypeStruct((rows, cols), f32)] * 4,
        compiler_params=_cparams(("parallel",)),
    )(w, g, m, v)


SMALL = ("norm1", "pool_w", "pool_scale", "norm2", "norm3", "final_norm")
ORDER = ("norm1", "w_in", "pool_w", "pool_scale", "w_out", "norm2", "w_up", "w_down", "norm3", "w_gate", "w_ple",
         "final_norm")


def _pack_small(tree, extra=None):
    parts = [tree[n].reshape(-1) for n in SMALL]
    if extra is not None:
        parts.append(extra.reshape(-1))
    flat = jnp.concatenate(parts)
    return jnp.pad(flat, (0, SMALL_ROWS * 128 - flat.shape[0])).reshape(SMALL_ROWS, 128)


def _unpack_small(packed, like):
    flat = packed.reshape(-1)
    out, off = {}, 0
    for n in SMALL:
        size = int(np.prod(like[n].shape))
        out[n] = flat[off:off + size].reshape(like[n].shape)
        off += size
    return out, flat[off]


def kernel(x, p, positions, norm1, w_in, pool_w, pool_scale, w_out, norm2, w_up, w_down, norm3, w_gate, w_ple, final_norm, loss_target, m_norm1, m_w_in, m_pool_w, m_pool_scale, m_w_out, m_norm2, m_w_up, m_w_down, m_norm3, m_w_gate, m_w_ple, m_final_norm, v_norm1, v_w_in, v_pool_w, v_pool_scale, v_w_out, v_norm2, v_w_up, v_w_down, v_norm3, v_w_gate, v_w_ple, v_final_norm):
    w = dict(norm1=norm1, w_in=w_in, pool_w=pool_w, pool_scale=pool_scale, w_out=w_out, norm2=norm2, w_up=w_up,
             w_down=w_down, norm3=norm3, w_gate=w_gate, w_ple=w_ple, final_norm=final_norm)
    m = dict(norm1=m_norm1, w_in=m_w_in, pool_w=m_pool_w, pool_scale=m_pool_scale, w_out=m_w_out, norm2=m_norm2,
             w_up=m_w_up, w_down=m_w_down, norm3=m_norm3, w_gate=m_w_gate, w_ple=m_w_ple, final_norm=m_final_norm)
    v = dict(norm1=v_norm1, w_in=v_w_in, pool_w=v_pool_w, pool_scale=v_pool_scale, w_out=v_w_out, norm2=v_norm2,
             w_up=v_w_up, w_down=v_w_down, norm3=v_norm3, w_gate=v_w_gate, w_ple=v_w_ple, final_norm=v_final_norm)
    small = {n: w[n] for n in SMALL}

    wsrc = _GatheredWeights({n: w[n] for n in BIG})
    reducer = _GradReducer()
    loss8, dx, small_grads = _local_step(x[0], p.reshape(2, x.shape[1], PLE_DIM), positions[0], wsrc, small, loss_target[0], reducer)
    s_send, s_recv, s_vec, s_land = _small_start(_pack_small(small_grads, loss8[0, 0]))
    gsh = reducer.finish(s_vec)

    g_out, d_out, m_out, v_out = {}, {}, {}, {}
    for n in BIG:
        shp = w[n].shape
        two = lambda a: a.reshape(shp[0] * shp[1], shp[2])
        g2, d2, m2, v2 = _adamw(two(w[n]), two(gsh[n]), two(m[n]), two(v[n]), n)
        g_out[n], d_out[n], m_out[n], v_out[n] = g2.reshape(shp), d2.reshape(shp), m2.reshape(shp), v2.reshape(shp)
    red = _small_sum(*_small_wait(s_send, s_recv, s_vec, s_land, d2))
    g_small, loss = _unpack_small(red, small)
    _, d2, m2, v2 = _adamw(_pack_small(small), red, _pack_small({n: m[n] for n in SMALL}),
                           _pack_small({n: v[n] for n in SMALL}), "small")
    for tree, packed in ((d_out, d2), (m_out, m2), (v_out, v2)):
        tree.update(_unpack_small(packed, small)[0])
    g_out.update(g_small)

    return (loss, dx[None], *[g_out[n] for n in ORDER], *[d_out[n] for n in ORDER], *[m_out[n] for n in ORDER],
            *[v_out[n] for n in ORDER])
```

```python
import jax
import jax.numpy as jnp
import numpy as np
from jax import lax
from jax.experimental import pallas as pl
from jax.experimental.pallas import tpu as pltpu

f32 = jnp.float32
MXU_DTYPE = jnp.bfloat16
COMM_DTYPE = jnp.bfloat16

D_MODEL = 1024
POOL_WIDTH = 256
POOL_GC = 64
ATTN_WIDTH = 768
HEAD_DIM = 64
N_IN = POOL_WIDTH + 3 * ATTN_WIDTH
D_FF = 4096
PLE_DIM = 256
BLK = 128
DILATIONS = (1, 4, 16)
ROT_DIM = 16
ROPE_THETA = 500000.0
EPS = 1e-6
ATTN_SCALE = HEAD_DIM ** -0.5
NEG_BIG = -1e30

ADAM_LR, ADAM_B1, ADAM_B2, ADAM_EPS, ADAM_WD, ADAM_STEP = 0.001, 0.9, 0.999, 1e-08, 0.01, 10

TM = 512
TM_WGRAD = 1024
HALO = 16
VMEM_LIMIT = 48 * 1024 * 1024
VMEM_LIMIT_LARGE = 58 * 1024 * 1024
N_CHIPS = 4
MESH = pl.DeviceIdType.MESH

BIG = ("w_in", "w_out", "w_up", "w_down", "w_gate", "w_ple")
FULL_SHAPE = {"w_in": (D_MODEL, N_IN), "w_out": (D_MODEL, D_MODEL), "w_up": (D_MODEL, D_FF),
              "w_down": (D_FF, D_MODEL), "w_gate": (D_MODEL, D_MODEL), "w_ple": (PLE_DIM, D_MODEL)}
COL_SHARDED = {"w_in": True, "w_out": False, "w_up": True, "w_down": False, "w_gate": False, "w_ple": True}


def _shard_shape(name):
    k, n = FULL_SHAPE[name]
    return (k, n // N_CHIPS) if COL_SHARDED[name] else (k // N_CHIPS, n)


def _cparams(sem=None, vmem=VMEM_LIMIT):
    return pltpu.CompilerParams(dimension_semantics=sem, vmem_limit_bytes=vmem)


def _resident(block_shape, index_map):
    return pl.BlockSpec(block_shape, index_map, pipeline_mode=pl.Buffered(1))


def _mx(x):
    return x.astype(MXU_DTYPE)


def _dot(a, b):
    return jnp.dot(a, b, preferred_element_type=f32)


def _dot_nt(a, b):
    return lax.dot_general(a, b, (((1,), (1,)), ((), ())), preferred_element_type=f32)


def _dot_tn(a, b):
    return lax.dot_general(a, b, (((0,), (0,)), ((), ())), preferred_element_type=f32)


def _sigmoid(x):
    return 1.0 / (1.0 + jnp.exp(-x))


def _rope_apply(y, c, s1, s2, width):
    return y * c + pltpu.roll(y, width - 8, axis=1) * s1 + pltpu.roll(y, 8, axis=1) * s2


def _rope_transpose(dy, c, s1, s2, width):
    return dy * c + pltpu.roll(dy * s1, 8, axis=1) + pltpu.roll(dy * s2, width - 8, axis=1)


def _norm_matmul(h, g, w, layer, tn, name, rope=None):
    s_len, d = h.shape
    n = w.shape[2]

    def body(*refs):
        if rope is None:
            h_ref, g_ref, w_ref, y_ref, hn_ref = refs
        else:
            h_ref, g_ref, w_ref, c_ref, s1_ref, s2_ref, y_ref, hn_ref = refs
            reps = tn // 128
            c = jnp.concatenate([c_ref[...]] * reps, axis=1)
            s1 = jnp.concatenate([s1_ref[...]] * reps, axis=1)
            s2 = jnp.concatenate([s2_ref[...]] * reps, axis=1)
        x = h_ref[...]
        r = lax.rsqrt(jnp.mean(x * x, axis=-1, keepdims=True) + EPS)
        hn = ((x * r) * g_ref[...]).astype(hn_ref.dtype)
        hn_ref[...] = hn
        for j in range(n // tn):
            y = _dot(hn, w_ref[:, j * tn:(j + 1) * tn])
            if rope is not None and POOL_WIDTH <= j * tn < POOL_WIDTH + 2 * ATTN_WIDTH:
                y = _rope_apply(y, c, s1, s2, tn)
            y_ref[:, j * tn:(j + 1) * tn] = y

    in_specs = [pl.BlockSpec((TM, d), lambda i: (i, 0)),
                pl.BlockSpec((1, d), lambda i: (0, 0)),
                _resident((None, d, n), lambda i: (layer, 0, 0))]
    args = [h, g, w]
    if rope is not None:
        assert POOL_WIDTH % tn == 0 and (2 * ATTN_WIDTH) % tn == 0
        in_specs += [pl.BlockSpec((TM, 128), lambda i: (i, 0))] * 3
        args += list(rope)
    return pl.pallas_call(
        body, name=name, grid=(s_len // TM,), in_specs=in_specs,
        out_specs=[pl.BlockSpec((TM, n), lambda i: (i, 0)), pl.BlockSpec((TM, d), lambda i: (i, 0))],
        out_shape=[jax.ShapeDtypeStruct((s_len, n), f32), jax.ShapeDtypeStruct((s_len, d), MXU_DTYPE)],
        compiler_params=_cparams(("parallel",)),
    )(*args)


def _gate_ple_fwd(h2, g, w_gate, w_ple, layer, p, name, head=None):
    s_len, d = h2.shape

    def body(h_ref, g_ref, wg_ref, p_ref, wp_ref, *rest):
        gl_ref, hn_ref = rest[-2:]
        x = h_ref[...]
        r = lax.rsqrt(jnp.mean(x * x, axis=-1, keepdims=True) + EPS)
        hn = ((x * r) * g_ref[...]).astype(hn_ref.dtype)
        hn_ref[...] = hn
        gl = _dot(hn, wg_ref[...])
        gl_ref[...] = gl.astype(gl_ref.dtype)
        h3 = x + _sigmoid(gl) * _dot(_mx(p_ref[...]), wp_ref[...])
        if head is None:
            rest[0][...] = h3
            return
        gf_ref, t_ref, loss_ref, dh_ref, dgf_ref = rest[:5]
        i = pl.program_id(0)
        gv = gf_ref[...]
        r3 = lax.rsqrt(jnp.mean(h3 * h3, axis=-1, keepdims=True) + EPS)
        xh = h3 * r3
        diff = xh * gv - t_ref[...]
        part = 0.5 * jnp.sum(jnp.mean(diff * diff, axis=-1, keepdims=True), axis=0, keepdims=True)
        dy = diff * (1.0 / d)
        dxh = dy * gv
        dh_ref[...] = r3 * (dxh - xh * jnp.mean(dxh * xh, axis=-1, keepdims=True))
        dgsum = jnp.sum(dy * xh, axis=0, keepdims=True)
        lossb = jnp.broadcast_to(part, (8, 128))

        @pl.when(i == 0)
        def _():
            loss_ref[...] = lossb
            dgf_ref[...] = dgsum

        @pl.when(i > 0)
        def _():
            loss_ref[...] += lossb
            dgf_ref[...] += dgsum

    row = lambda i: (i, 0)
    one = lambda i: (0, 0)
    in_specs = [pl.BlockSpec((TM, d), row), pl.BlockSpec((1, d), one),
                pl.BlockSpec((None, d, d), lambda i: (layer, 0, 0)), pl.BlockSpec((TM, PLE_DIM), row),
                pl.BlockSpec((None, PLE_DIM, d), lambda i: (layer, 0, 0))]
    args = [h2, g, w_gate, p, w_ple]
    saved = [jax.ShapeDtypeStruct((s_len, d), MXU_DTYPE)] * 2
    if head is None:
        out_specs = [pl.BlockSpec((TM, d), row)] * 3
        out_shape = [jax.ShapeDtypeStruct((s_len, d), f32)] + saved
    else:
        in_specs += [pl.BlockSpec((1, d), one), pl.BlockSpec((TM, d), row)]
        args += list(head)
        out_specs = [pl.BlockSpec((8, 128), one), pl.BlockSpec((TM, d), row), pl.BlockSpec((1, d), one)] \
            + [pl.BlockSpec((TM, d), row)] * 2
        out_shape = [jax.ShapeDtypeStruct((8, 128), f32), jax.ShapeDtypeStruct((s_len, d), f32),
                     jax.ShapeDtypeStruct((1, d), f32)] + saved
    return pl.pallas_call(
        body, name=name, grid=(s_len // TM,), in_specs=in_specs, out_specs=out_specs, out_shape=out_shape,
        compiler_params=_cparams(("arbitrary",)),
    )(*args)


def _gate_bwd(dh3, gl, p, w_ple, w_gate, layer, h2, g, name):
    s_len, d = dh3.shape

    def body(dh_ref, gl_ref, p_ref, wp_ref, wg_ref, h_ref, g_ref, dh2_ref, dg_ref, de_ref, dgl_ref):
        i = pl.program_id(0)
        dh = dh_ref[...]
        gate = _sigmoid(gl_ref[...].astype(f32))
        e = _dot(_mx(p_ref[...]), wp_ref[...])
        de_ref[...] = (dh * gate).astype(de_ref.dtype)
        dgl = ((dh * e) * (gate * (1.0 - gate))).astype(dgl_ref.dtype)
        dgl_ref[...] = dgl
        dx, dgrow = _rmsnorm_bwd(_dot_nt(dgl, wg_ref[...]), h_ref[...], g_ref[...])
        dh2_ref[...] = dh + dx
        dgsum = jnp.sum(dgrow, axis=0, keepdims=True)

        @pl.when(i == 0)
        def _():
            dg_ref[...] = dgsum

        @pl.when(i > 0)
        def _():
            dg_ref[...] += dgsum

    row = lambda i: (i, 0)
    blk = pl.BlockSpec((TM, d), row)
    return pl.pallas_call(
        body, name=name, grid=(s_len // TM,),
        in_specs=[blk, blk, pl.BlockSpec((TM, PLE_DIM), row), _resident((None, PLE_DIM, d), lambda i: (layer, 0, 0)),
                  _resident((None, d, d), lambda i: (layer, 0, 0)), blk, pl.BlockSpec((1, d), lambda i: (0, 0))],
        out_specs=[blk, pl.BlockSpec((1, d), lambda i: (0, 0)), blk, blk],
        out_shape=[jax.ShapeDtypeStruct((s_len, d), f32), jax.ShapeDtypeStruct((1, d), f32),
                   jax.ShapeDtypeStruct((s_len, d), MXU_DTYPE), jax.ShapeDtypeStruct((s_len, d), MXU_DTYPE)],
        compiler_params=_cparams(("arbitrary",)),
    )(dh3, gl, p, w_ple, w_gate, h2, g)


def _rmsnorm_bwd(dhn, x, g):
    r = lax.rsqrt(jnp.mean(x * x, axis=-1, keepdims=True) + EPS)
    xh = x * r
    dxh = dhn * g
    dx = r * (dxh - xh * jnp.mean(dxh * xh, axis=-1, keepdims=True))
    return dx, dhn * xh


def _matmul_nt_norm_bwd(dy, w, layer, h_prev, g, dres, name, tk=1024, after=None):
    s_len, k_dim = dy.shape
    d = h_prev.shape[1]

    def body(dy_ref, w_ref, h_ref, g_ref, dres_ref, *rest):
        dh_ref, dg_ref = rest[-2:]
        i = pl.program_id(0)
        acc = None
        for k in range(k_dim // tk):
            part = _dot_nt(_mx(dy_ref[:, k * tk:(k + 1) * tk]), w_ref[:, k * tk:(k + 1) * tk])
            acc = part if acc is None else acc + part
        dx, dgrow = _rmsnorm_bwd(acc, h_ref[...], g_ref[...])
        dh_ref[...] = dres_ref[...] + dx
        dgsum = jnp.sum(dgrow, axis=0, keepdims=True)

        @pl.when(i == 0)
        def _():
            dg_ref[...] = dgsum

        @pl.when(i > 0)
        def _():
            dg_ref[...] += dgsum

    in_specs = [pl.BlockSpec((TM, k_dim), lambda i: (i, 0)),
                _resident((None, d, k_dim), lambda i: (layer, 0, 0)),
                pl.BlockSpec((TM, d), lambda i: (i, 0)),
                pl.BlockSpec((1, d), lambda i: (0, 0)),
                pl.BlockSpec((TM, d), lambda i: (i, 0))]
    args = [dy, w, h_prev, g, dres]
    if after is not None:
        in_specs.append(pl.BlockSpec(memory_space=pl.ANY))
        args.append(after)
    return pl.pallas_call(
        body, name=name, grid=(s_len // TM,), in_specs=in_specs,
        out_specs=[pl.BlockSpec((TM, d), lambda i: (i, 0)), pl.BlockSpec((1, d), lambda i: (0, 0))],
        out_shape=[jax.ShapeDtypeStruct((s_len, d), f32), jax.ShapeDtypeStruct((1, d), f32)],
        compiler_params=_cparams(("arbitrary",)),
    )(*args)


def _mlp_fwd(h1, g, w_up, w_down, layer, name, tf=1024):
    s_len, d = h1.shape
    ff = w_up.shape[2]

    def body(h_ref, g_ref, wu_ref, wd_ref, h2_ref, a_ref, hn_ref):
        x = h_ref[...]
        r = lax.rsqrt(jnp.mean(x * x, axis=-1, keepdims=True) + EPS)
        hn = ((x * r) * g_ref[...]).astype(hn_ref.dtype)
        hn_ref[...] = hn
        acc = x
        for j in range(ff // tf):
            a = _dot(hn, wu_ref[:, j * tf:(j + 1) * tf])
            a_ref[:, j * tf:(j + 1) * tf] = a.astype(a_ref.dtype)
            relu = jnp.maximum(a, 0.0)
            acc = acc + _dot(_mx(relu * relu), wd_ref[j * tf:(j + 1) * tf, :])
        h2_ref[...] = acc

    row = lambda i: (i, 0)
    return pl.pallas_call(
        body, name=name, grid=(s_len // TM,),
        in_specs=[pl.BlockSpec((TM, d), row), pl.BlockSpec((1, d), lambda i: (0, 0)),
                  _resident((None, d, ff), lambda i: (layer, 0, 0)), _resident((None, ff, d), lambda i: (layer, 0, 0))],
        out_specs=[pl.BlockSpec((TM, d), row), pl.BlockSpec((TM, ff), row), pl.BlockSpec((TM, d), row)],
        out_shape=[jax.ShapeDtypeStruct((s_len, d), f32), jax.ShapeDtypeStruct((s_len, ff), MXU_DTYPE),
                   jax.ShapeDtypeStruct((s_len, d), MXU_DTYPE)],
        compiler_params=_cparams(("parallel",)),
    )(h1, g, w_up, w_down)


def _mlp_bwd(dh2, w_down, w_up, layer, a, h1, g, name, tf=1024):
    s_len, d = dh2.shape
    ff = a.shape[1]

    def body(dh_ref, wd_ref, wu_ref, a_ref, h_ref, g_ref, dh1_ref, dg_ref, da_ref):
        i = pl.program_id(0)
        dh = dh_ref[...]
        dhb = _mx(dh)
        acc = None
        for j in range(ff // tf):
            cols = slice(j * tf, (j + 1) * tf)
            dact = _dot_nt(dhb, wd_ref[cols, :])
            da = (dact * (2.0 * jnp.maximum(a_ref[:, cols].astype(f32), 0.0))).astype(da_ref.dtype)
            da_ref[:, cols] = da
            part = _dot_nt(da, wu_ref[:, cols])
            acc = part if acc is None else acc + part
        dx, dgrow = _rmsnorm_bwd(acc, h_ref[...], g_ref[...])
        dh1_ref[...] = dh + dx
        dgsum = jnp.sum(dgrow, axis=0, keepdims=True)

        @pl.when(i == 0)
        def _():
            dg_ref[...] = dgsum

        @pl.when(i > 0)
        def _():
            dg_ref[...] += dgsum

    row = lambda i: (i, 0)
    return pl.pallas_call(
        body, name=name, grid=(s_len // TM,),
        in_specs=[pl.BlockSpec((TM, d), row), _resident((None, ff, d), lambda i: (layer, 0, 0)),
                  _resident((None, d, ff), lambda i: (layer, 0, 0)), pl.BlockSpec((TM, ff), row),
                  pl.BlockSpec((TM, d), row), pl.BlockSpec((1, d), lambda i: (0, 0))],
        out_specs=[pl.BlockSpec((TM, d), row), pl.BlockSpec((1, d), lambda i: (0, 0)), pl.BlockSpec((TM, ff), row)],
        out_shape=[jax.ShapeDtypeStruct((s_len, d), f32), jax.ShapeDtypeStruct((1, d), f32),
                   jax.ShapeDtypeStruct((s_len, ff), MXU_DTYPE)],
        compiler_params=_cparams(("arbitrary",), vmem=VMEM_LIMIT_LARGE),
    )(dh2, w_down, w_up, a, h1, g)


def _weight_grad(a, b, name, act=False):
    s_len, k_dim = a.shape
    n = b.shape[1]
    tka = min(k_dim, 2048)
    tnb = n if n <= 1024 else (2048 if n % 2048 == 0 else 640)
    ns = s_len // TM_WGRAD

    def body(a_ref, b_ref, o_ref, acc_ref):
        s = pl.program_id(2)
        x = a_ref[...]
        if act:
            relu = jnp.maximum(x.astype(f32), 0.0)
            x = relu * relu
        part = _dot_tn(_mx(x), _mx(b_ref[...]))

        @pl.when(s == 0)
        def _():
            acc_ref[...] = part

        @pl.when(s > 0)
        def _():
            acc_ref[...] += part

        @pl.when(s == ns - 1)
        def _():
            o_ref[...] = acc_ref[...].astype(o_ref.dtype)

    return pl.pallas_call(
        body, name=name, grid=(k_dim // tka, n // tnb, ns),
        in_specs=[pl.BlockSpec((TM_WGRAD, tka), lambda i, j, s: (s, i)),
                  pl.BlockSpec((TM_WGRAD, tnb), lambda i, j, s: (s, j))],
        out_specs=pl.BlockSpec((tka, tnb), lambda i, j, s: (i, j)),
        out_shape=jax.ShapeDtypeStruct((k_dim, n), COMM_DTYPE),
        scratch_shapes=[pltpu.VMEM((tka, tnb), f32)],
        compiler_params=_cparams(("parallel", "parallel", "arbitrary")),
    )(a, b)


def _group_select(lane, x2, x4, x8, x16):
    grp = lane // POOL_GC
    return jnp.where(grp == 0, x2, jnp.where(grp == 1, x4, jnp.where(grp == 2, x8, x16)))


def _pool_window(lane):
    grp = lane // POOL_GC
    return jnp.where(grp == 0, 2, jnp.where(grp == 1, 4, jnp.where(grp == 2, 8, 16)))


def _pool_y(u, halo, i):
    xs = jnp.concatenate([jnp.where(i > 0, halo, 0.0), u], axis=0)
    s2 = xs + pltpu.roll(xs, 1, axis=0)
    s4 = s2 + pltpu.roll(s2, 2, axis=0)
    s8 = s4 + pltpu.roll(s4, 4, axis=0)
    s16 = s8 + pltpu.roll(s8, 8, axis=0)
    lane = lax.broadcasted_iota(jnp.int32, xs.shape, 1)
    sel = _group_select(lane, s2, s4, s8, s16)[HALO:, :]
    t = i * TM + lax.broadcasted_iota(jnp.int32, u.shape, 0)
    cnt = jnp.minimum(_pool_window(lax.broadcasted_iota(jnp.int32, u.shape, 1)), t + 1).astype(f32)
    return sel / cnt - u


def _group_weights(l0, l1, l2):
    mx = jnp.maximum(jnp.maximum(l0, l1), l2)
    e0, e1, e2 = jnp.exp(l0 - mx), jnp.exp(l1 - mx), jnp.exp(l2 - mx)
    den = e0 + e1 + e2
    return e0 / den, e1 / den, e2 / den


def _mixer_out_proj(z, wbd, scale, outs, lses, w_out, layer, h, name):
    s_len, d = h.shape

    def body(u_ref, halo_ref, wbd_ref, sc_ref, o0, o1, o2, l0, l1, l2, wo_ref, h_ref, m_ref, h1_ref):
        i = pl.program_id(0)
        y = _pool_y(u_ref[...], halo_ref[...], i)
        pool = _dot(_mx(y), wbd_ref[...]) * sc_ref[...]
        w0, w1, w2 = _group_weights(l0[...], l1[...], l2[...])
        m = jnp.concatenate([pool, o0[...] * w0, o1[...] * w1, o2[...] * w2], axis=1).astype(m_ref.dtype)
        m_ref[...] = m
        h1_ref[...] = h_ref[...] + _dot(m, wo_ref[...])

    row = lambda i: (i, 0)
    blk = pl.BlockSpec((TM, 256), row)
    grp = [pl.BlockSpec((TM, 256), lambda i, g=g: (i, g)) for g in range(3)]
    return pl.pallas_call(
        body, name=name, grid=(s_len // TM,),
        in_specs=[blk, pl.BlockSpec((HALO, 256), lambda i: (jnp.maximum(i * (TM // HALO) - 1, 0), 0)),
                  pl.BlockSpec((256, 256), lambda i: (0, 0)), pl.BlockSpec((1, 256), lambda i: (0, 0))] + grp + grp
        + [_resident((None, d, d), lambda i: (layer, 0, 0)), pl.BlockSpec((TM, d), row)],
        out_specs=[pl.BlockSpec((TM, d), row)] * 2,
        out_shape=[jax.ShapeDtypeStruct((s_len, d), MXU_DTYPE), jax.ShapeDtypeStruct((s_len, d), f32)],
        compiler_params=_cparams(("parallel",)),
    )(z, z, wbd, scale, outs, outs, outs, lses, lses, lses, w_out, h)


def _head_sums(x):
    r = lax.broadcasted_iota(jnp.int32, (256, 256), 0) // HEAD_DIM
    c = lax.broadcasted_iota(jnp.int32, (256, 256), 1) // HEAD_DIM
    ones = jnp.where(r == c, 1.0, 0.0).astype(jnp.bfloat16)
    hi = x.astype(jnp.bfloat16)
    lo = (x - hi.astype(f32)).astype(jnp.bfloat16)
    return _dot(hi, ones) + _dot(lo, ones)


def _out_combine_bwd(dh1, w_out, layer, outs, lses, name, after=None):
    s_len, d = dh1.shape

    def body(dh_ref, w_ref, o0, o1, o2, l0, l1, l2, *rest):
        dp_ref, do_ref, dl_ref = rest[-3:]
        dm = _dot_nt(_mx(dh_ref[...]), w_ref[...])
        dp_ref[...] = dm[:, :POOL_WIDTH]
        w = _group_weights(l0[...], l1[...], l2[...])
        da = [dm[:, POOL_WIDTH + 256 * g:POOL_WIDTH + 256 * (g + 1)] for g in range(3)]
        o = (o0[...], o1[...], o2[...])
        dw = [_head_sums(da[g] * o[g]) for g in range(3)]
        t = w[0] * dw[0] + w[1] * dw[1] + w[2] * dw[2]
        do_ref[...] = jnp.concatenate([da[g] * w[g] for g in range(3)], axis=1)
        dl_ref[...] = jnp.concatenate([w[g] * t for g in range(3)], axis=1)

    grp = [pl.BlockSpec((TM, 256), lambda i, g=g: (i, g)) for g in range(3)]
    in_specs = [pl.BlockSpec((TM, d), lambda i: (i, 0)), _resident((None, d, d), lambda i: (layer, 0, 0))] + grp + grp
    args = [dh1, w_out, outs, outs, outs, lses, lses, lses]
    if after is not None:
        in_specs.append(pl.BlockSpec(memory_space=pl.ANY))
        args.append(after)
    return pl.pallas_call(
        body, name=name, grid=(s_len // TM,), in_specs=in_specs,
        out_specs=[pl.BlockSpec((TM, POOL_WIDTH), lambda i: (i, 0))] + [pl.BlockSpec((TM, ATTN_WIDTH), lambda i: (i, 0))] * 2,
        out_shape=[jax.ShapeDtypeStruct((s_len, POOL_WIDTH), f32)] + [jax.ShapeDtypeStruct((s_len, ATTN_WIDTH), f32)] * 2,
        compiler_params=_cparams(("parallel",)),
    )(*args)


def _pool_bwd(z, dm, wbd, scale, name, after=None):
    s_len = z.shape[0]
    n_halo = s_len // HALO

    def body(u_ref, uh_ref, d_ref, dh_ref, wbd_ref, sc_ref, *rest):
        du_ref, dw_ref, dsc_ref = rest[-3:]
        i = pl.program_id(0)
        last = pl.num_programs(0) - 1
        y = _pool_y(u_ref[...], uh_ref[...], i)
        yb = _mx(y)
        dpo = d_ref[...]
        sc = sc_ref[...]
        dsc = jnp.sum(dpo * _dot(yb, wbd_ref[...]), axis=0, keepdims=True)
        dwp = _dot_tn(yb, _mx(dpo * sc))

        @pl.when(i == 0)
        def _():
            dsc_ref[...] = dsc
            dw_ref[...] = dwp

        @pl.when(i > 0)
        def _():
            dsc_ref[...] += dsc
            dw_ref[...] += dwp

        ext = jnp.concatenate([dpo, jnp.where(i < last, dh_ref[...], 0.0)], axis=0)
        dy = _dot_nt(_mx(ext * sc), wbd_ref[...])
        t = i * TM + lax.broadcasted_iota(jnp.int32, ext.shape, 0)
        lane = lax.broadcasted_iota(jnp.int32, ext.shape, 1)
        e = dy / jnp.minimum(_pool_window(lane), t + 1).astype(f32)
        rows = ext.shape[0]
        f2 = e + pltpu.roll(e, rows - 1, axis=0)
        f4 = f2 + pltpu.roll(f2, rows - 2, axis=0)
        f8 = f4 + pltpu.roll(f4, rows - 4, axis=0)
        f16 = f8 + pltpu.roll(f8, rows - 8, axis=0)
        du_ref[...] = (_group_select(lane, f2, f4, f8, f16) - dy)[:TM, :].astype(du_ref.dtype)

    row = lambda i: (i, 0)
    blk = pl.BlockSpec((TM, 256), row)
    extra = [] if after is None else [after]
    return pl.pallas_call(
        body, name=name, grid=(s_len // TM,),
        in_specs=[blk, pl.BlockSpec((HALO, 256), lambda i: (jnp.maximum(i * (TM // HALO) - 1, 0), 0)),
                  blk, pl.BlockSpec((HALO, 256), lambda i: (jnp.minimum((i + 1) * (TM // HALO), n_halo - 1), 0)),
                  pl.BlockSpec((256, 256), lambda i: (0, 0)), pl.BlockSpec((1, 256), lambda i: (0, 0))]
        + [pl.BlockSpec(memory_space=pl.ANY)] * len(extra),
        out_specs=[blk, pl.BlockSpec((256, 256), lambda i: (0, 0)), pl.BlockSpec((1, 256), lambda i: (0, 0))],
        out_shape=[jax.ShapeDtypeStruct((s_len, N_IN), MXU_DTYPE), jax.ShapeDtypeStruct((256, 256), f32),
                   jax.ShapeDtypeStruct((1, 256), f32)],
        compiler_params=_cparams(("arbitrary",)),
    )(z, z, dm, dm, wbd, scale, *extra)


def _tri_masks():
    qi = lax.broadcasted_iota(jnp.int32, (BLK, BLK), 0)
    ki = lax.broadcasted_iota(jnp.int32, (BLK, BLK), 1)
    return qi >= ki, ki >= qi


ATTN_SUPER_PER_STEP = (8, 2, 1)
Q_COL, K_COL, V_COL = POOL_WIDTH // 128, (POOL_WIDTH + ATTN_WIDTH) // 128, (POOL_WIDTH + 2 * ATTN_WIDTH) // 128


def _rows(ref, start, dil):
    if dil == 1:
        return ref[pl.ds(start, BLK), :]
    return ref[pl.ds(start, BLK, stride=dil), :]


RESIDUE_UNROLL = 4


def _for_residues(dil, fn, loop=True):
    if dil <= RESIDUE_UNROLL or not loop:
        for r in range(dil):
            fn(r, 0)
    else:
        lax.fori_loop(0, dil, fn, 0, unroll=RESIDUE_UNROLL)


def _set_rows(ref, start, dil, val):
    if dil == 1:
        ref[pl.ds(start, BLK), :] = val
    else:
        ref[pl.ds(start, BLK, stride=dil), :] = val


def _attn_fwd(z, g, prev, name):
    s_len = z.shape[0]
    dil, m = DILATIONS[g], ATTN_SUPER_PER_STEP[g]
    sbr = BLK * dil
    rows = sbr * m

    def body(*refs):
        q_ref, kc_ref, kp_ref, vc_ref, vp_ref = refs[:5]
        o_ref, l_ref = refs[-2:]
        st = pl.program_id(0)
        low, up = _tri_masks()
        head0 = lax.broadcasted_iota(jnp.int32, (BLK, 128), 1) < HEAD_DIM
        for sb in range(m):
            valid = jnp.concatenate([up & (st > 0) if sb == 0 else up, low], axis=1)

            def one_residue(r, carry, sb=sb, valid=valid):
                base = sb * sbr + r
                q = _rows(q_ref, base, dil)
                kc, vc = _rows(kc_ref, base, dil), _rows(vc_ref, base, dil)
                if sb == 0:
                    kp, vp = _rows(kp_ref, r, dil), _rows(vp_ref, r, dil)
                else:
                    kp, vp = _rows(kc_ref, base - sbr, dil), _rows(vc_ref, base - sbr, dil)
                k2 = jnp.concatenate([_mx(kp), _mx(kc)], axis=0)
                v2 = jnp.concatenate([_mx(vp), _mx(vc)], axis=0)
                qs = q * ATTN_SCALE
                outs, lses = [], []
                for hh in range(2):
                    s = jnp.where(valid, _dot_nt(_mx(jnp.where(head0 == (hh == 0), qs, 0.0)), k2), NEG_BIG)
                    mx = jnp.max(s, axis=-1, keepdims=True)
                    e = jnp.exp(s - mx)
                    l = jnp.sum(e, axis=-1, keepdims=True)
                    outs.append(_dot(_mx(e / l), v2))
                    lses.append(jnp.broadcast_to(mx + jnp.log(l), (BLK, 128)))
                _set_rows(o_ref, base, dil, jnp.where(head0, outs[0], outs[1]))
                _set_rows(l_ref, base, dil, jnp.where(head0, lses[0], lses[1]))
                return carry

            _for_residues(dil, one_residue, loop=False)

    def cur(col):
        return pl.BlockSpec((rows, 128), lambda st, hp: (st, col + 2 * g + hp))

    def before(col):
        return pl.BlockSpec((sbr, 128), lambda st, hp: (jnp.maximum(st * m - 1, 0), col + 2 * g + hp))

    in_specs = [cur(Q_COL), cur(K_COL), before(K_COL), cur(V_COL), before(V_COL)]
    args = [z, z, z, z, z]
    aliases = {}
    if prev is not None:
        in_specs += [pl.BlockSpec(memory_space=pl.ANY)] * 2
        args += list(prev)
        aliases = {5: 0, 6: 1}
    return pl.pallas_call(
        body, name=name, grid=(s_len // rows, 2), in_specs=in_specs, out_specs=[cur(0), cur(0)],
        out_shape=[jax.ShapeDtypeStruct((s_len, ATTN_WIDTH), f32)] * 2, input_output_aliases=aliases,
        compiler_params=_cparams(("parallel", "parallel")),
    )(*args)


def _stack_heads(x, head0):
    return jnp.concatenate([_mx(jnp.where(head0, x, 0.0)), _mx(jnp.where(head0, 0.0, x))], axis=0)


def _head_rows(x):
    xt = x.T
    return jnp.concatenate([jnp.broadcast_to(xt[0:1, :], (BLK, BLK)),
                            jnp.broadcast_to(xt[HEAD_DIM:HEAD_DIM + 1, :], (BLK, BLK))], axis=0)


def _attn_bwd(z, do, lse, dlt, tabs, dz, g, name):
    s_len = z.shape[0]
    dil, m = DILATIONS[g], ATTN_SUPER_PER_STEP[g]
    sbr = BLK * dil
    rows = sbr * m
    nsteps = s_len // rows

    def body(q_ref, qn_ref, kc_ref, kp_ref, vc_ref, vp_ref, do_ref, don_ref, l_ref, ln_ref, d_ref, dn_ref,
             c_ref, s1_ref, s2_ref, dz_in, dz_ref, dq_buf, dk_buf, dv_buf, out_buf, sems):
        del dz_in
        st, hp = pl.program_id(0), pl.program_id(1)
        head0 = lax.broadcasted_iota(jnp.int32, (BLK, 128), 1) < HEAD_DIM
        key_i = lax.broadcasted_iota(jnp.int32, (2 * BLK, BLK), 0) & (BLK - 1)
        query_i = lax.broadcasted_iota(jnp.int32, (2 * BLK, BLK), 1)
        same_t, cross_t = query_i >= key_i, key_i >= query_i
        for sb in range(m):
            prev_t = cross_t & (st > 0) if sb == 0 else cross_t
            next_t = cross_t & (st < nsteps - 1) if sb == m - 1 else cross_t

            def one_residue(r, carry, sb=sb, prev_t=prev_t, next_t=next_t):
                base = sb * sbr + r
                q, k, v = _rows(q_ref, base, dil), _rows(kc_ref, base, dil), _rows(vc_ref, base, dil)
                do_c, l_c, d_c = _rows(do_ref, base, dil), _rows(l_ref, base, dil), _rows(d_ref, base, dil)
                if sb == 0:
                    kp, vp = _rows(kp_ref, r, dil), _rows(vp_ref, r, dil)
                else:
                    kp, vp = _rows(kc_ref, base - sbr, dil), _rows(vc_ref, base - sbr, dil)
                if sb == m - 1:
                    qn, do_n = _rows(qn_ref, r, dil), _rows(don_ref, r, dil)
                    l_n, d_n = _rows(ln_ref, r, dil), _rows(dn_ref, r, dil)
                else:
                    qn, do_n = _rows(q_ref, base + sbr, dil), _rows(do_ref, base + sbr, dil)
                    l_n, d_n = _rows(l_ref, base + sbr, dil), _rows(d_ref, base + sbr, dil)
                k2, kp2, v2, vp2 = _stack_heads(k, head0), _stack_heads(kp, head0), _stack_heads(v, head0), _stack_heads(vp, head0)
                qb, qnb, dob, donb = _mx(q), _mx(qn), _mx(do_c), _mx(do_n)
                lse2, dlt2, lsen2, dltn2 = _head_rows(l_c), _head_rows(d_c), _head_rows(l_n), _head_rows(d_n)

                def pair(keys, vals, qs, dos, lse_rows, dlt_rows, valid):
                    p = jnp.where(valid, jnp.exp(_dot_nt(keys, qs) * ATTN_SCALE - lse_rows), 0.0)
                    ds = _mx(p * (_dot_nt(vals, dos) - dlt_rows) * ATTN_SCALE)
                    return _mx(p), ds

                p_a, ds_a = pair(k2, v2, qb, dob, lse2, dlt2, same_t)
                _, ds_b = pair(kp2, vp2, qb, dob, lse2, dlt2, prev_t)
                p_c, ds_c = pair(k2, v2, qnb, donb, lsen2, dltn2, next_t)
                dq = _dot_tn(ds_a, k2) + _dot_tn(ds_b, kp2)
                dk2 = _dot(ds_a, qb) + _dot(ds_c, qnb)
                dv2 = _dot(p_a, dob) + _dot(p_c, donb)
                c, s1, s2 = _rows(c_ref, base, dil), _rows(s1_ref, base, dil), _rows(s2_ref, base, dil)
                _set_rows(dq_buf, base, dil, _rope_transpose(dq, c, s1, s2, 128))
                _set_rows(dk_buf, base, dil, _rope_transpose(jnp.where(head0, dk2[:BLK], dk2[BLK:]), c, s1, s2, 128))
                _set_rows(dv_buf, base, dil, jnp.where(head0, dv2[:BLK], dv2[BLK:]))
                return carry

            _for_residues(dil, one_residue)
        copies = []
        for t, (buf, col) in enumerate(((dq_buf, Q_COL), (dk_buf, K_COL), (dv_buf, V_COL))):
            out_buf[t] = buf[...].astype(out_buf.dtype)
            lane0 = pl.multiple_of((col + 2 * g + hp) * 128, 128)
            dst = dz_ref.at[pl.ds(pl.multiple_of(st * rows, rows), rows), pl.ds(lane0, 128)]
            cp = pltpu.make_async_copy(out_buf.at[t], dst, sems.at[t])
            cp.start()
            copies.append(cp)
        for cp in copies:
            cp.wait()

    def cur(col):
        return pl.BlockSpec((rows, 128), lambda st, hp: (st, col + 2 * g + hp))

    def before(col):
        return pl.BlockSpec((sbr, 128), lambda st, hp: (jnp.maximum(st * m - 1, 0), col + 2 * g + hp))

    def after(col):
        return pl.BlockSpec((sbr, 128), lambda st, hp: (jnp.minimum((st + 1) * m, s_len // sbr - 1), col + 2 * g + hp))

    tab = pl.BlockSpec((rows, 128), lambda st, hp: (st, 0))
    return pl.pallas_call(
        body, name=name, grid=(nsteps, 2),
        in_specs=[cur(Q_COL), after(Q_COL), cur(K_COL), before(K_COL), cur(V_COL), before(V_COL),
                  cur(0), after(0), cur(0), after(0), cur(0), after(0), tab, tab, tab,
                  pl.BlockSpec(memory_space=pl.ANY)],
        out_specs=pl.BlockSpec(memory_space=pl.ANY),
        out_shape=jax.ShapeDtypeStruct(dz.shape, dz.dtype), input_output_aliases={15: 0},
        scratch_shapes=[pltpu.VMEM((rows, 128), f32)] * 3 + [pltpu.VMEM((3, rows, 128), dz.dtype),
                                                            pltpu.SemaphoreType.DMA((3,))],
        compiler_params=_cparams(("arbitrary", "arbitrary")),
    )(z, z, z, z, z, z, do, do, lse, lse, dlt, dlt, *tabs, dz)


def _rope_tables(positions):
    inv_freq = ROPE_THETA ** (-jnp.arange(0, ROT_DIM, 2, dtype=f32) / ROT_DIM)
    ang = positions.astype(f32)[:, None] * inv_freq
    cos, sin = jnp.cos(ang), jnp.sin(ang)
    s_len = positions.shape[0]
    zero8, rest = jnp.zeros((s_len, 8), f32), jnp.zeros((s_len, HEAD_DIM - ROT_DIM), f32)
    c = jnp.concatenate([cos, cos, jnp.ones((s_len, HEAD_DIM - ROT_DIM), f32)], axis=1)
    s1 = jnp.concatenate([-sin, zero8, rest], axis=1)
    s2 = jnp.concatenate([zero8, sin, rest], axis=1)
    return c, s1, s2


def _block_diag(pool_w):
    out = jnp.zeros((POOL_WIDTH, POOL_WIDTH), pool_w.dtype)
    for g in range(4):
        out = lax.dynamic_update_slice(out, pool_w[g], (g * POOL_GC, g * POOL_GC))
    return out


def _layer_fwd(h, p_l, wsrc, small, layer, tabs, head=None):
    nm = f"l{layer}_"
    wts, wl = wsrc.take(layer, ("w_in",), (h,) if layer else tuple(tabs))
    z, hn1 = _norm_matmul(h, small["norm1"][layer][None], wts["w_in"], wl, 256, nm + "in_proj", rope=tabs)
    ol = None
    for g in range(3):
        ol = _attn_fwd(z, g, ol, nm + f"attn_fwd{g}")
    outs, lses = ol
    wbd = _mx(_block_diag(small["pool_w"][layer]))
    scale = small["pool_scale"][layer][None]
    wts.update(wsrc.take(layer, ("w_out",), (outs,))[0])
    m, h1 = _mixer_out_proj(z, wbd, scale, outs, lses, wts["w_out"], wl, h, nm + "mixer_out")
    wts.update(wsrc.take(layer, ("w_up", "w_down"), (h1,))[0])
    h2, a, hn2 = _mlp_fwd(h1, small["norm2"][layer][None], wts["w_up"], wts["w_down"], wl, nm + "mlp")
    wts.update(wsrc.take(layer, ("w_gate", "w_ple"), (h2,))[0])
    *h3, gl, hn3 = _gate_ple_fwd(h2, small["norm3"][layer][None], wts["w_gate"], wts["w_ple"], wl, p_l,
                                 nm + "gate_ple", head=head)
    saved = dict(h=h, z=z, hn1=hn1, outs=outs, lses=lses, wbd=wbd, scale=scale, m=m, h1=h1, a=a, hn2=hn2, h2=h2,
                 gl=gl, hn3=hn3, wts=wts, wl=wl)
    return h3, saved


def _layer_bwd(dh3, sv, p_l, small, layer, tabs128, reducer):
    nm = f"l{layer}_"
    wts, wl = sv["wts"], sv["wl"]
    dh2, dg3, de, dgl = _gate_bwd(dh3, sv["gl"], p_l, wts["w_ple"], wts["w_gate"], wl, sv["h2"],
                                  small["norm3"][layer][None], nm + "gate_bwd")
    reducer.add("w_gate", layer, _weight_grad(sv["hn3"], dgl, nm + "dw_gate"))
    reducer.add("w_ple", layer, _weight_grad(p_l, de, nm + "dw_ple"))
    dh1, dg2, da = _mlp_bwd(dh2, wts["w_down"], wts["w_up"], wl, sv["a"], sv["h1"], small["norm2"][layer][None],
                            nm + "mlp_bwd")
    reducer.add("w_down", layer, _weight_grad(sv["a"], dh2, nm + "dw_down", act=True))
    started = reducer.add("w_up", layer, _weight_grad(sv["hn2"], da, nm + "dw_up"))
    dpool, do, dlt = _out_combine_bwd(dh1, wts["w_out"], wl, sv["outs"], sv["lses"], nm + "out_bwd", after=started)
    started = reducer.add("w_out", layer, _weight_grad(sv["m"], dh1, nm + "dw_out"))
    dz, dwbd, dscale = _pool_bwd(sv["z"], dpool, sv["wbd"], sv["scale"], nm + "pool_bwd", after=started)
    for g in range(3):
        dz = _attn_bwd(sv["z"], do, sv["lses"], dlt, tabs128, dz, g, nm + f"attn_bwd{g}")
    started = reducer.add("w_in", layer, _weight_grad(sv["hn1"], dz, nm + "dw_in"))
    dh0, dg1 = _matmul_nt_norm_bwd(dz, wts["w_in"], wl, sv["h"], small["norm1"][layer][None], dh1, nm + "in_bwd",
                                   tk=512, after=started)
    dpool_w = jnp.stack([dwbd[g * POOL_GC:(g + 1) * POOL_GC, g * POOL_GC:(g + 1) * POOL_GC] for g in range(4)])
    sg = dict(norm1=dg1[0], norm2=dg2[0], norm3=dg3[0], pool_w=dpool_w, pool_scale=dscale[0])
    return dh0, sg


def _local_step(x, p, positions, wsrc, small, target, reducer):
    tabs128 = tuple(jnp.tile(t, (1, 2)) for t in _rope_tables(positions))
    (h,), sv0 = _layer_fwd(x, p[0], wsrc, small, 0, tabs128)
    (loss, dh, dgf), sv1 = _layer_fwd(h, p[1], wsrc, small, 1, tabs128, head=(small["final_norm"][None], target))
    saved = [sv0, sv1]
    sgs = [None, None]
    for layer in (1, 0):
        dh, sgs[layer] = _layer_bwd(dh, saved[layer], p[layer], small, layer, tabs128, reducer)
    small_grads = {k: jnp.stack([sgs[0][k], sgs[1][k]]) for k in sgs[0]}
    small_grads["final_norm"] = dgf[0]
    return loss, dh, small_grads


HBM = pl.BlockSpec(memory_space=pltpu.HBM)


def _my_place():
    return lax.axis_index("x"), lax.axis_index("y"), lax.axis_index("c")


def _other_chips(x, y):
    return [(1 - x, y), (x, 1 - y), (1 - x, 1 - y)]


def _window(ref, name, chip):
    k, n = _shard_shape(name)
    if COL_SHARDED[name]:
        return ref.at[:, pl.ds(pl.multiple_of(chip * n, 128), n)]
    return ref.at[pl.ds(pl.multiple_of(chip * k, 128), k), :]


def _chip_index():
    return jnp.reshape(2 * lax.axis_index("x") + lax.axis_index("y"), (1,)).astype(jnp.int32)


def _shard_block(name, tr):
    ks, ns = _shard_shape(name)
    if COL_SHARDED[name]:
        return (tr, ns), lambda i, me: (i, me[0])
    return (tr, ns), lambda i, me: (me[0] * (ks // tr) + i, 0)


def _place_shard(w, name, layer):
    ks, ns = _shard_shape(name)
    tr = min(ks, 256)
    shape, index = _shard_block(name, tr)

    def body(me_ref, w_ref, o_ref):
        o_ref[...] = w_ref[...].astype(o_ref.dtype)

    return pl.pallas_call(
        body, name=f"place_{name}{layer}",
        grid_spec=pltpu.PrefetchScalarGridSpec(
            num_scalar_prefetch=1, grid=(ks // tr,),
            in_specs=[pl.BlockSpec((None, tr, ns), lambda i, me: (layer, i, 0))],
            out_specs=pl.BlockSpec((None,) + shape, lambda i, me: (0,) + index(i, me))),
        out_shape=jax.ShapeDtypeStruct((1,) + FULL_SHAPE[name], MXU_DTYPE),
        compiler_params=_cparams(("parallel",)),
    )(_chip_index(), w)


GATHER_ORDER = [("w_in", 0), ("w_out", 0), ("w_up", 0), ("w_down", 0), ("w_gate", 0), ("w_ple", 0),
                ("w_in", 1), ("w_out", 1), ("w_up", 1), ("w_down", 1), ("w_gate", 1), ("w_ple", 1)]
SEM = pl.BlockSpec(memory_space=pltpu.SEMAPHORE)
EFFECT = pltpu.SideEffectType.DATAFLOW_SIDE_EFFECTING


def _gather_copy(src_ref, dst_ref, name, idx, j, chip, send_sems, recv_sems, c):
    cx, cy = chip
    return pltpu.make_async_remote_copy(
        src_ref=src_ref, dst_ref=dst_ref, send_sem=send_sems.at[3 * idx + j], recv_sem=recv_sems.at[3 * idx + j],
        device_id=(cx, cy, c), device_id_type=MESH)


def _gather_start(placed, order, tag, after=None):
    n = len(order)
    extra = [] if after is None else [after]

    def body(*refs):
        ins = refs[:n]
        k = n + len(extra)
        send_sems, recv_sems = refs[k], refs[k + 1]
        outs = refs[k + 2:k + 2 + n]
        token = refs[-1]
        x, y, c = _my_place()
        me = 2 * x + y
        for idx, (name, _) in enumerate(order):
            for j, chip in enumerate(_other_chips(x, y)):
                _gather_copy(_window(ins[idx].at[0], name, me), _window(outs[idx].at[0], name, me), name, idx, j, chip,
                             send_sems, recv_sems, c).start()
        token[...] = jnp.zeros_like(token)

    res = pl.pallas_call(
        body, name="gather_start" + tag,
        out_shape=(pltpu.SemaphoreType.DMA((3 * n,)), pltpu.SemaphoreType.DMA((3 * n,)))
        + tuple(pltpu.HBM(a.shape, a.dtype) for a in placed) + (jax.ShapeDtypeStruct((8, 128), f32),),
        in_specs=[HBM] * n + [pl.BlockSpec(memory_space=pl.ANY)] * len(extra),
        out_specs=(SEM, SEM) + (HBM,) * n + (pl.BlockSpec(memory_space=pltpu.VMEM),),
        input_output_aliases={i: i + 2 for i in range(n)},
        compiler_params=pltpu.CompilerParams(has_side_effects=EFFECT),
    )(*[pltpu.with_memory_space_constraint(a, pltpu.HBM) for a in placed], *extra)
    return res[0], res[1], list(res[2:2 + n]), res[-1]


def _gather_wait(send_sems, recv_sems, arrays, order, idxs, after, name):
    n = len(idxs)

    def body(*refs):
        ins = refs[:n]
        send_ref, recv_ref = refs[n], refs[n + 1]
        x, y, c = _my_place()
        me = 2 * x + y
        for k, idx in enumerate(idxs):
            wname = order[idx][0]
            for j, chip in enumerate(_other_chips(x, y)):
                cx, cy = chip
                mine = _window(ins[k].at[0], wname, me)
                land = _window(ins[k].at[0], wname, 2 * cx + cy)
                _gather_copy(mine, mine, wname, idx, j, chip, send_ref, recv_ref, c).wait_send()
                _gather_copy(land, land, wname, idx, j, chip, send_ref, recv_ref, c).wait_recv()

    operands = list(arrays) + [send_sems, recv_sems] + list(after)
    in_specs = [HBM] * n + [SEM, SEM] + [pl.BlockSpec(memory_space=pl.ANY)] * len(after)
    res = pl.pallas_call(
        body, name=name, out_shape=tuple(pltpu.HBM(a.shape, a.dtype) for a in arrays),
        in_specs=in_specs, out_specs=(HBM,) * n, input_output_aliases={i: i for i in range(n)},
        compiler_params=pltpu.CompilerParams(has_side_effects=EFFECT),
    )(*operands)
    return list(res)


class _GatheredWeights:
    def __init__(self, shards):
        self.starts = []
        token = None
        for tag, order in (("_first", GATHER_ORDER[:1]), ("_rest", GATHER_ORDER[1:])):
            placed = [_place_shard(shards[name], name, layer) for name, layer in order]
            self.starts.append((order,) + _gather_start(placed, order, tag, token))
            token = self.starts[-1][-1]

    def take(self, layer, names, after):
        order, send, recv, arrays, _ = next(s for s in self.starts if (names[0], layer) in s[0])
        after = list(after)
        if order is self.starts[0][0]:
            after.append(self.starts[-1][-1])
        idxs = [order.index((n, layer)) for n in names]
        got = _gather_wait(send, recv, [arrays[i] for i in idxs], order, idxs, after, f"gather_wait{layer}_{names[0]}")
        return dict(zip(names, got)), 0


N_DEV = 8


def _reduce_copies(dws, lands, names, layer, send_sems, recv_sems):
    x, y, c = _my_place()
    me, my_dev = 2 * x + y, 4 * x + 2 * y + c
    out = []
    for t, name in enumerate(names):
        for j, (cx, cy) in enumerate(_other_chips(x, y)):
            out.append((pltpu.make_async_remote_copy(
                src_ref=_window(dws[t], name, 2 * cx + cy), dst_ref=lands[t].at[my_dev],
                send_sem=send_sems.at[4 * t + j], recv_sem=recv_sems.at[N_DEV * t + my_dev],
                device_id=(cx, cy, layer), device_id_type=MESH), False))
        out.append((pltpu.make_async_remote_copy(
            src_ref=_window(dws[t], name, me), dst_ref=lands[t].at[my_dev],
            send_sem=send_sems.at[4 * t + 3], recv_sem=recv_sems.at[N_DEV * t + my_dev],
            device_id=(x, y, layer), device_id_type=MESH), True))
    return out


def _reduce_start(dws, names, layer, tag):
    n = len(names)
    lands = [lax.empty((N_DEV,) + _shard_shape(nm), dws[0].dtype) for nm in names]

    def body(*refs):
        ins = refs[:n]
        send_sems, recv_sems = refs[2 * n], refs[2 * n + 1]
        land_out = refs[3 * n + 2:4 * n + 2]
        token = refs[-1]
        c = lax.axis_index("c")
        for cp, non_owner_only in _reduce_copies(ins, land_out, names, layer, send_sems, recv_sems):
            if non_owner_only:
                @pl.when(c != layer)
                def _():
                    cp.start()
            else:
                cp.start()
        token[...] = jnp.zeros_like(token)

    res = pl.pallas_call(
        body, name="reduce_start" + tag,
        out_shape=(pltpu.SemaphoreType.DMA((4 * n,)), pltpu.SemaphoreType.DMA((N_DEV * n,)))
        + tuple(pltpu.HBM(a.shape, a.dtype) for a in dws) + tuple(pltpu.HBM(a.shape, a.dtype) for a in lands)
        + (jax.ShapeDtypeStruct((8, 128), f32),),
        in_specs=[HBM] * (2 * n),
        out_specs=(SEM, SEM) + (HBM,) * (2 * n) + (pl.BlockSpec(memory_space=pltpu.VMEM),),
        input_output_aliases={i: i + 2 for i in range(2 * n)},
        compiler_params=pltpu.CompilerParams(has_side_effects=EFFECT),
    )(*[pltpu.with_memory_space_constraint(a, pltpu.HBM) for a in list(dws) + lands])
    return res[0], res[1], list(res[2:2 + n]), list(res[2 + n:2 + 2 * n]), res[-1]


def _reduce_wait(send_sems, recv_sems, dws, lands, names, layer, after, tag):
    n = len(names)

    def body(*refs):
        ins, land_in = refs[:n], refs[n:2 * n]
        send_ref, recv_ref = refs[2 * n], refs[2 * n + 1]
        x, y, c = _my_place()
        for cp, non_owner_only in _reduce_copies(ins, land_in, names, layer, send_ref, recv_ref):
            if non_owner_only:
                @pl.when(c != layer)
                def _():
                    cp.wait_send()
            else:
                cp.wait_send()

        @pl.when(c == layer)
        def _():
            for t in range(n):
                for k in range(1, N_DEV):
                    px, py, pc = x ^ ((k >> 2) & 1), y ^ ((k >> 1) & 1), c ^ (k & 1)
                    dev = 4 * px + 2 * py + pc
                    land = land_in[t].at[dev]
                    pltpu.make_async_remote_copy(
                        src_ref=land, dst_ref=land, send_sem=send_ref.at[4 * t], recv_sem=recv_ref.at[N_DEV * t + dev],
                        device_id=(px, py, pc), device_id_type=MESH).wait_recv()

    res = pl.pallas_call(
        body, name="reduce_wait" + tag,
        out_shape=tuple(pltpu.HBM(a.shape, a.dtype) for a in list(dws) + list(lands)),
        in_specs=[HBM] * (2 * n) + [SEM, SEM, pl.BlockSpec(memory_space=pl.ANY)], out_specs=(HBM,) * (2 * n),
        input_output_aliases={i: i for i in range(2 * n)},
        compiler_params=pltpu.CompilerParams(has_side_effects=EFFECT),
    )(*dws, *lands, send_sems, recv_sems, after)
    return list(res[:n]), list(res[n:])


def _sum_devices(land, own, name, layer, prev):
    ks, ns = _shard_shape(name)
    tr = min(ks, 256)
    shape, index = _shard_block(name, tr)

    def body(me_ref, dev_ref, *refs):
        s_ref, own_ref, out_ref = refs[0], refs[1], refs[-1]
        dev = dev_ref[0]
        acc = None
        for s in range(N_DEV):
            term = jnp.where(dev == s, own_ref[...], s_ref[s]).astype(f32)
            acc = term if acc is None else acc + term
        out_ref[...] = acc

    def mine(i, dev):
        return i * jnp.where((dev[0] & 1) == layer, 1, 0)

    in_specs = [pl.BlockSpec((N_DEV, tr, ns), lambda i, me, dev: (0, mine(i, dev), 0)),
                pl.BlockSpec(shape, lambda i, me, dev: index(mine(i, dev), me))]
    args = [land, own]
    aliases = {}
    if prev is not None:
        in_specs.append(pl.BlockSpec(memory_space=pl.ANY))
        args.append(prev)
        aliases = {4: 0}
    x, y, c = _my_place()
    return pl.pallas_call(
        body, name=f"sum_devices_{name}{layer}",
        grid_spec=pltpu.PrefetchScalarGridSpec(
            num_scalar_prefetch=2, grid=(ks // tr,), in_specs=in_specs,
            out_specs=pl.BlockSpec((None, tr, ns), lambda i, me, dev: (layer, mine(i, dev), 0))),
        out_shape=jax.ShapeDtypeStruct((2, ks, ns), f32), input_output_aliases=aliases,
        compiler_params=_cparams(("arbitrary",)),
    )(_chip_index(), jnp.reshape(4 * x + 2 * y + c, (1,)).astype(jnp.int32), *args)


class _GradReducer:
    GROUPS = (("1", 1, ("w_gate", "w_ple", "w_down", "w_up", "w_out", "w_in")),
              ("0a", 0, ("w_gate", "w_ple", "w_down", "w_up")),
              ("0b", 0, ("w_out",)),
              ("0c", 0, ("w_in",)))

    def __init__(self):
        self.grads = {}
        self.started = {}

    def add(self, name, layer, dw):
        self.grads[(name, layer)] = dw
        token = None
        for tag, glayer, names in self.GROUPS:
            if tag not in self.started and all((nm, glayer) in self.grads for nm in names):
                *self.started[tag], token = _reduce_start([self.grads[(nm, glayer)] for nm in names], names, glayer, tag)
        return token

    def finish(self, after):
        mine = {}
        for tag, layer, names in self.GROUPS:
            send, recv, dws, lands = self.started[tag]
            dws, lands = _reduce_wait(send, recv, dws, lands, names, layer, after, tag)
            for nm, dw, land in zip(names, dws, lands):
                mine[nm] = _sum_devices(land, dw, nm, layer, mine.get(nm))
        return _pair_layers(mine)


def _pair_layers(mine):
    names = list(BIG)

    def body(*refs):
        ins = refs[:len(names)]
        outs = refs[len(names):2 * len(names)]
        send_sems, recv_sems = refs[2 * len(names):]
        x, y, c = _my_place()
        sibling = (x, y, 1 - c)
        cps = []
        for t in range(len(names)):
            cp = pltpu.make_async_remote_copy(
                src_ref=ins[t].at[c], dst_ref=outs[t].at[c], send_sem=send_sems.at[t], recv_sem=recv_sems.at[t],
                device_id=sibling, device_id_type=MESH)
            cp.start()
            cps.append(cp)
        for t in range(len(names)):
            cps[t].wait_send()
            land = outs[t].at[1 - c]
            pltpu.make_async_remote_copy(
                src_ref=land, dst_ref=land, send_sem=send_sems.at[t], recv_sem=recv_sems.at[t],
                device_id=sibling, device_id_type=MESH).wait_recv()

    outs = pl.pallas_call(
        body, name="pair_layers", in_specs=[HBM] * len(names), out_specs=[HBM] * len(names),
        out_shape=[jax.ShapeDtypeStruct((2,) + _shard_shape(n), f32) for n in names],
        input_output_aliases={t: t for t in range(len(names))},
        scratch_shapes=[pltpu.SemaphoreType.DMA((len(names),)), pltpu.SemaphoreType.DMA((len(names),))],
    )(*[mine[n] for n in names])
    return dict(zip(names, outs))


SMALL_ROWS = 320


def _small_copies(vec_ref, land_ref, send_sems, recv_sems):
    x, y, c = _my_place()
    me = 4 * x + 2 * y + c
    out = []
    for k in range(1, N_DEV):
        peer = (x ^ ((k >> 2) & 1), y ^ ((k >> 1) & 1), c ^ (k & 1))
        src_dev = 4 * peer[0] + 2 * peer[1] + peer[2]
        send = pltpu.make_async_remote_copy(
            src_ref=vec_ref, dst_ref=land_ref.at[me], send_sem=send_sems.at[k - 1], recv_sem=recv_sems.at[k - 1],
            device_id=peer, device_id_type=MESH)
        arrival = pltpu.make_async_remote_copy(
            src_ref=land_ref.at[src_dev], dst_ref=land_ref.at[src_dev], send_sem=send_sems.at[k - 1],
            recv_sem=recv_sems.at[k - 1], device_id=peer, device_id_type=MESH)
        out.append((send, arrival))
    return out


def _small_start(vec):
    land = lax.empty((N_DEV,) + vec.shape, vec.dtype)

    def body(v_ref, land_in, send_sems, recv_sems, v_out, land_out):
        del land_in, v_out
        for send, _ in _small_copies(v_ref, land_out, send_sems, recv_sems):
            send.start()

    return pl.pallas_call(
        body, name="small_start",
        out_shape=(pltpu.SemaphoreType.DMA((N_DEV - 1,)), pltpu.SemaphoreType.DMA((N_DEV - 1,)),
                   pltpu.HBM(vec.shape, vec.dtype), pltpu.HBM(land.shape, land.dtype)),
        in_specs=[HBM, HBM], out_specs=(SEM, SEM, HBM, HBM), input_output_aliases={0: 2, 1: 3},
        compiler_params=pltpu.CompilerParams(has_side_effects=EFFECT),
    )(pltpu.with_memory_space_constraint(vec, pltpu.HBM), pltpu.with_memory_space_constraint(land, pltpu.HBM))


def _small_wait(send_sems, recv_sems, vec, land, after):
    def body(v_ref, land_ref, send_ref, recv_ref, after_ref, v_out, land_out):
        del after_ref, v_out, land_out
        for send, arrival in _small_copies(v_ref, land_ref, send_ref, recv_ref):
            send.wait_send()
            arrival.wait_recv()

    return pl.pallas_call(
        body, name="small_wait", out_shape=(pltpu.HBM(vec.shape, vec.dtype), pltpu.HBM(land.shape, land.dtype)),
        in_specs=[HBM, HBM, SEM, SEM, pl.BlockSpec(memory_space=pl.ANY)], out_specs=(HBM, HBM),
        input_output_aliases={0: 0, 1: 1}, compiler_params=pltpu.CompilerParams(has_side_effects=EFFECT),
    )(vec, land, send_sems, recv_sems, after)


def _small_sum(vec, land):
    x, y, c = _my_place()

    def body(dev_ref, v_ref, land_ref, out_ref):
        acc = None
        for s in range(N_DEV):
            term = jnp.where(dev_ref[0] == s, v_ref[...], land_ref[s])
            acc = term if acc is None else acc + term
        out_ref[...] = acc

    return pl.pallas_call(
        body, name="small_sum",
        grid_spec=pltpu.PrefetchScalarGridSpec(
            num_scalar_prefetch=1, grid=(1,),
            in_specs=[pl.BlockSpec(vec.shape, lambda i, dev: (0, 0)), pl.BlockSpec(land.shape, lambda i, dev: (0, 0, 0))],
            out_specs=pl.BlockSpec(vec.shape, lambda i, dev: (0, 0))),
        out_shape=jax.ShapeDtypeStruct(vec.shape, vec.dtype),
        compiler_params=_cparams(("arbitrary",)),
    )(jnp.reshape(4 * x + 2 * y + c, (1,)).astype(jnp.int32), vec, land)


def _adamw(w, g, m, v, name):
    rows, cols = w.shape
    tr = rows
    for cand in (512, 256, 128, 64, 32, 16, 8):
        if rows % cand == 0 and cand * cols * 4 <= 2 * 1024 * 1024:
            tr = cand
            break
    c1 = np.float32(1.0 - ADAM_B1 ** ADAM_STEP)
    c2 = np.float32(1.0 - ADAM_B2 ** ADAM_STEP)

    def body(w_ref, g_ref, m_ref, v_ref, go_ref, d_ref, mo_ref, vo_ref):
        gv = g_ref[...]
        go_ref[...] = gv
        mn = ADAM_B1 * m_ref[...] + (1.0 - ADAM_B1) * gv
        vn = ADAM_B2 * v_ref[...] + (1.0 - ADAM_B2) * (gv * gv)
        mo_ref[...] = mn
        vo_ref[...] = vn
        d_ref[...] = -ADAM_LR * ((mn / c1) / (jnp.sqrt(vn / c2) + ADAM_EPS) + ADAM_WD * w_ref[...])

    blk = pl.BlockSpec((tr, cols), lambda i: (i, 0))
    return pl.pallas_call(
        body, name="adamw_" + name, grid=(rows // tr,), in_specs=[blk] * 4, out_specs=[blk] * 4,
        out_shape=[jax.ShapeDtypeStruct((rows, cols), f32)] * 4,
        compiler_params=_cparams(("parallel",)),
    )(w, g, m, v)


SMALL = ("norm1", "pool_w", "pool_scale", "norm2", "norm3", "final_norm")
ORDER = ("norm1", "w_in", "pool_w", "pool_scale", "w_out", "norm2", "w_up", "w_down", "norm3", "w_gate", "w_ple",
         "final_norm")


def _pack_small(tree, extra=None):
    parts = [tree[n].reshape(-1) for n in SMALL]
    if extra is not None:
        parts.append(extra.reshape(-1))
    flat = jnp.concatenate(parts)
    return jnp.pad(flat, (0, SMALL_ROWS * 128 - flat.shape[0])).reshape(SMALL_ROWS, 128)


def _unpack_small(packed, like):
    flat = packed.reshape(-1)
    out, off = {}, 0
    for n in SMALL:
        size = int(np.prod(like[n].shape))
        out[n] = flat[off:off + size].reshape(like[n].shape)
        off += size
    return out, flat[off]


def kernel(x, p, positions, norm1, w_in, pool_w, pool_scale, w_out, norm2, w_up, w_down, norm3, w_gate, w_ple, final_norm, loss_target, m_norm1, m_w_in, m_pool_w, m_pool_scale, m_w_out, m_norm2, m_w_up, m_w_down, m_norm3, m_w_gate, m_w_ple, m_final_norm, v_norm1, v_w_in, v_pool_w, v_pool_scale, v_w_out, v_norm2, v_w_up, v_w_down, v_norm3, v_w_gate, v_w_ple, v_final_norm):
    w = dict(norm1=norm1, w_in=w_in, pool_w=pool_w, pool_scale=pool_scale, w_out=w_out, norm2=norm2, w_up=w_up,
             w_down=w_down, norm3=norm3, w_gate=w_gate, w_ple=w_ple, final_norm=final_norm)
    m = dict(norm1=m_norm1, w_in=m_w_in, pool_w=m_pool_w, pool_scale=m_pool_scale, w_out=m_w_out, norm2=m_norm2,
             w_up=m_w_up, w_down=m_w_down, norm3=m_norm3, w_gate=m_w_gate, w_ple=m_w_ple, final_norm=m_final_norm)
    v = dict(norm1=v_norm1, w_in=v_w_in, pool_w=v_pool_w, pool_scale=v_pool_scale, w_out=v_w_out, norm2=v_norm2,
             w_up=v_w_up, w_down=v_w_down, norm3=v_norm3, w_gate=v_w_gate, w_ple=v_w_ple, final_norm=v_final_norm)
    small = {n: w[n] for n in SMALL}

    wsrc = _GatheredWeights({n: w[n] for n in BIG})
    reducer = _GradReducer()
    loss8, dx, small_grads = _local_step(x[0], p.reshape(2, x.shape[1], PLE_DIM), positions[0], wsrc, small, loss_target[0], reducer)
    s_send, s_recv, s_vec, s_land = _small_start(_pack_small(small_grads, loss8[0, 0]))
    gsh = reducer.finish(s_vec)

    g_out, d_out, m_out, v_out = {}, {}, {}, {}
    for n in BIG:
        shp = w[n].shape
        two = lambda a: a.reshape(shp[0] * shp[1], shp[2])
        g2, d2, m2, v2 = _adamw(two(w[n]), two(gsh[n]), two(m[n]), two(v[n]), n)
        g_out[n], d_out[n], m_out[n], v_out[n] = g2.reshape(shp), d2.reshape(shp), m2.reshape(shp), v2.reshape(shp)
    red = _small_sum(*_small_wait(s_send, s_recv, s_vec, s_land, d2))
    g_small, loss = _unpack_small(red, small)
    _, d2, m2, v2 = _adamw(_pack_small(small), red, _pack_small({n: m[n] for n in SMALL}),
                           _pack_small({n: v[n] for n in SMALL}), "small")
    for tree, packed in ((d_out, d2), (m_out, m2), (v_out, v2)):
        tree.update(_unpack_small(packed, small)[0])
    g_out.update(g_small)

    return (loss, dx[None], *[g_out[n] for n in ORDER], *[d_out[n] for n in ORDER], *[m_out[n] for n in ORDER],
            *[v_out[n] for n in ORDER])
```

```python
import jax
import jax.numpy as jnp
import numpy as np
from jax import lax
from jax.experimental import pallas as pl
from jax.experimental.pallas import tpu as pltpu

f32 = jnp.float32
MXU_DTYPE = jnp.bfloat16
COMM_DTYPE = jnp.bfloat16

D_MODEL = 1024
POOL_WIDTH = 256
POOL_GC = 64
ATTN_WIDTH = 768
HEAD_DIM = 64
N_IN = POOL_WIDTH + 3 * ATTN_WIDTH
D_FF = 4096
PLE_DIM = 256
BLK = 128
DILATIONS = (1, 4, 16)
ROT_DIM = 16
ROPE_THETA = 500000.0
EPS = 1e-6
ATTN_SCALE = HEAD_DIM ** -0.5
NEG_BIG = -1e30

ADAM_LR, ADAM_B1, ADAM_B2, ADAM_EPS, ADAM_WD, ADAM_STEP = 0.001, 0.9, 0.999, 1e-08, 0.01, 10

TM = 512
TM_WGRAD = 1024
HALO = 16
VMEM_LIMIT = 48 * 1024 * 1024
VMEM_LIMIT_LARGE = 58 * 1024 * 1024
N_CHIPS = 4
MESH = pl.DeviceIdType.MESH

BIG = ("w_in", "w_out", "w_up", "w_down", "w_gate", "w_ple")
FULL_SHAPE = {"w_in": (D_MODEL, N_IN), "w_out": (D_MODEL, D_MODEL), "w_up": (D_MODEL, D_FF),
              "w_down": (D_FF, D_MODEL), "w_gate": (D_MODEL, D_MODEL), "w_ple": (PLE_DIM, D_MODEL)}
COL_SHARDED = {"w_in": True, "w_out": False, "w_up": True, "w_down": False, "w_gate": False, "w_ple": True}


def _shard_shape(name):
    k, n = FULL_SHAPE[name]
    return (k, n // N_CHIPS) if COL_SHARDED[name] else (k // N_CHIPS, n)


def _cparams(sem=None, vmem=VMEM_LIMIT):
    return pltpu.CompilerParams(dimension_semantics=sem, vmem_limit_bytes=vmem)


def _resident(block_shape, index_map):
    return pl.BlockSpec(block_shape, index_map, pipeline_mode=pl.Buffered(1))


def _mx(x):
    return x.astype(MXU_DTYPE)


def _dot(a, b):
    return jnp.dot(a, b, preferred_element_type=f32)


def _dot_nt(a, b):
    return lax.dot_general(a, b, (((1,), (1,)), ((), ())), preferred_element_type=f32)


def _dot_tn(a, b):
    return lax.dot_general(a, b, (((0,), (0,)), ((), ())), preferred_element_type=f32)


def _sigmoid(x):
    return 1.0 / (1.0 + jnp.exp(-x))


def _rope_apply(y, c, s1, s2, width):
    return y * c + pltpu.roll(y, width - 8, axis=1) * s1 + pltpu.roll(y, 8, axis=1) * s2


def _rope_transpose(dy, c, s1, s2, width):
    return dy * c + pltpu.roll(dy * s1, 8, axis=1) + pltpu.roll(dy * s2, width - 8, axis=1)


def _norm_matmul(h, g, w, layer, tn, name, rope=None):
    s_len, d = h.shape
    n = w.shape[2]

    def body(*refs):
        if rope is None:
            h_ref, g_ref, w_ref, y_ref, hn_ref = refs
        else:
            h_ref, g_ref, w_ref, c_ref, s1_ref, s2_ref, y_ref, hn_ref = refs
            reps = tn // 128
            c = jnp.concatenate([c_ref[...]] * reps, axis=1)
            s1 = jnp.concatenate([s1_ref[...]] * reps, axis=1)
            s2 = jnp.concatenate([s2_ref[...]] * reps, axis=1)
        x = h_ref[...]
        r = lax.rsqrt(jnp.mean(x * x, axis=-1, keepdims=True) + EPS)
        hn = ((x * r) * g_ref[...]).astype(hn_ref.dtype)
        hn_ref[...] = hn
        for j in range(n // tn):
            y = _dot(hn, w_ref[:, j * tn:(j + 1) * tn])
            if rope is not None and POOL_WIDTH <= j * tn < POOL_WIDTH + 2 * ATTN_WIDTH:
                y = _rope_apply(y, c, s1, s2, tn)
            y_ref[:, j * tn:(j + 1) * tn] = y

    in_specs = [pl.BlockSpec((TM, d), lambda i: (i, 0)),
                pl.BlockSpec((1, d), lambda i: (0, 0)),
                _resident((None, d, n), lambda i: (layer, 0, 0))]
    args = [h, g, w]
    if rope is not None:
        assert POOL_WIDTH % tn == 0 and (2 * ATTN_WIDTH) % tn == 0
        in_specs += [pl.BlockSpec((TM, 128), lambda i: (i, 0))] * 3
        args += list(rope)
    return pl.pallas_call(
        body, name=name, grid=(s_len // TM,), in_specs=in_specs,
        out_specs=[pl.BlockSpec((TM, n), lambda i: (i, 0)), pl.BlockSpec((TM, d), lambda i: (i, 0))],
        out_shape=[jax.ShapeDtypeStruct((s_len, n), f32), jax.ShapeDtypeStruct((s_len, d), MXU_DTYPE)],
        compiler_params=_cparams(("parallel",)),
    )(*args)


def _gate_ple_fwd(h2, g, w_gate, w_ple, layer, p, name, head=None):
    s_len, d = h2.shape

    def body(h_ref, g_ref, wg_ref, p_ref, wp_ref, *rest):
        gl_ref, hn_ref = rest[-2:]
        x = h_ref[...]
        r = lax.rsqrt(jnp.mean(x * x, axis=-1, keepdims=True) + EPS)
        hn = ((x * r) * g_ref[...]).astype(hn_ref.dtype)
        hn_ref[...] = hn
        gl = _dot(hn, wg_ref[...])
        gl_ref[...] = gl.astype(gl_ref.dtype)
        h3 = x + _sigmoid(gl) * _dot(_mx(p_ref[...]), wp_ref[...])
        if head is None:
            rest[0][...] = h3
            return
        gf_ref, t_ref, loss_ref, dh_ref, dgf_ref = rest[:5]
        i = pl.program_id(0)
        gv = gf_ref[...]
        r3 = lax.rsqrt(jnp.mean(h3 * h3, axis=-1, keepdims=True) + EPS)
        xh = h3 * r3
        diff = xh * gv - t_ref[...]
        part = 0.5 * jnp.sum(jnp.mean(diff * diff, axis=-1, keepdims=True), axis=0, keepdims=True)
        dy = diff * (1.0 / d)
        dxh = dy * gv
        dh_ref[...] = r3 * (dxh - xh * jnp.mean(dxh * xh, axis=-1, keepdims=True))
        dgsum = jnp.sum(dy * xh, axis=0, keepdims=True)
        lossb = jnp.broadcast_to(part, (8, 128))

        @pl.when(i == 0)
        def _():
            loss_ref[...] = lossb
            dgf_ref[...] = dgsum

        @pl.when(i > 0)
        def _():
            loss_ref[...] += lossb
            dgf_ref[...] += dgsum

    row = lambda i: (i, 0)
    one = lambda i: (0, 0)
    in_specs = [pl.BlockSpec((TM, d), row), pl.BlockSpec((1, d), one),
                pl.BlockSpec((None, d, d), lambda i: (layer, 0, 0)), pl.BlockSpec((TM, PLE_DIM), row),
                pl.BlockSpec((None, PLE_DIM, d), lambda i: (layer, 0, 0))]
    args = [h2, g, w_gate, p, w_ple]
    saved = [jax.ShapeDtypeStruct((s_len, d), MXU_DTYPE)] * 2
    if head is None:
        out_specs = [pl.BlockSpec((TM, d), row)] * 3
        out_shape = [jax.ShapeDtypeStruct((s_len, d), f32)] + saved
    else:
        in_specs += [pl.BlockSpec((1, d), one), pl.BlockSpec((TM, d), row)]
        args += list(head)
        out_specs = [pl.BlockSpec((8, 128), one), pl.BlockSpec((TM, d), row), pl.BlockSpec((1, d), one)] \
            + [pl.BlockSpec((TM, d), row)] * 2
        out_shape = [jax.ShapeDtypeStruct((8, 128), f32), jax.ShapeDtypeStruct((s_len, d), f32),
                     jax.ShapeDtypeStruct((1, d), f32)] + saved
    return pl.pallas_call(
        body, name=name, grid=(s_len // TM,), in_specs=in_specs, out_specs=out_specs, out_shape=out_shape,
        compiler_params=_cparams(("arbitrary",)),
    )(*args)


def _gate_bwd(dh3, gl, p, w_ple, w_gate, layer, h2, g, name):
    s_len, d = dh3.shape

    def body(dh_ref, gl_ref, p_ref, wp_ref, wg_ref, h_ref, g_ref, dh2_ref, dg_ref, de_ref, dgl_ref):
        i = pl.program_id(0)
        dh = dh_ref[...]
        gate = _sigmoid(gl_ref[...].astype(f32))
        e = _dot(_mx(p_ref[...]), wp_ref[...])
        de_ref[...] = (dh * gate).astype(de_ref.dtype)
        dgl = ((dh * e) * (gate * (1.0 - gate))).astype(dgl_ref.dtype)
        dgl_ref[...] = dgl
        dx, dgrow = _rmsnorm_bwd(_dot_nt(dgl, wg_ref[...]), h_ref[...], g_ref[...])
        dh2_ref[...] = dh + dx
        dgsum = jnp.sum(dgrow, axis=0, keepdims=True)

        @pl.when(i == 0)
        def _():
            dg_ref[...] = dgsum

        @pl.when(i > 0)
        def _():
            dg_ref[...] += dgsum

    row = lambda i: (i, 0)
    blk = pl.BlockSpec((TM, d), row)
    return pl.pallas_call(
        body, name=name, grid=(s_len // TM,),
        in_specs=[blk, blk, pl.BlockSpec((TM, PLE_DIM), row), _resident((None, PLE_DIM, d), lambda i: (layer, 0, 0)),
                  _resident((None, d, d), lambda i: (layer, 0, 0)), blk, pl.BlockSpec((1, d), lambda i: (0, 0))],
        out_specs=[blk, pl.BlockSpec((1, d), lambda i: (0, 0)), blk, blk],
        out_shape=[jax.ShapeDtypeStruct((s_len, d), f32), jax.ShapeDtypeStruct((1, d), f32),
                   jax.ShapeDtypeStruct((s_len, d), MXU_DTYPE), jax.ShapeDtypeStruct((s_len, d), MXU_DTYPE)],
        compiler_params=_cparams(("arbitrary",)),
    )(dh3, gl, p, w_ple, w_gate, h2, g)


def _rmsnorm_bwd(dhn, x, g):
    r = lax.rsqrt(jnp.mean(x * x, axis=-1, keepdims=True) + EPS)
    xh = x * r
    dxh = dhn * g
    dx = r * (dxh - xh * jnp.mean(dxh * xh, axis=-1, keepdims=True))
    return dx, dhn * xh


def _matmul_nt_norm_bwd(dy, w, layer, h_prev, g, dres, name, tk=1024, after=None):
    s_len, k_dim = dy.shape
    d = h_prev.shape[1]

    def body(dy_ref, w_ref, h_ref, g_ref, dres_ref, *rest):
        dh_ref, dg_ref = rest[-2:]
        i = pl.program_id(0)
        acc = None
        for k in range(k_dim // tk):
            part = _dot_nt(_mx(dy_ref[:, k * tk:(k + 1) * tk]), w_ref[:, k * tk:(k + 1) * tk])
            acc = part if acc is None else acc + part
        dx, dgrow = _rmsnorm_bwd(acc, h_ref[...], g_ref[...])
        dh_ref[...] = dres_ref[...] + dx
        dgsum = jnp.sum(dgrow, axis=0, keepdims=True)

        @pl.when(i == 0)
        def _():
            dg_ref[...] = dgsum

        @pl.when(i > 0)
        def _():
            dg_ref[...] += dgsum

    in_specs = [pl.BlockSpec((TM, k_dim), lambda i: (i, 0)),
                _resident((None, d, k_dim), lambda i: (layer, 0, 0)),
                pl.BlockSpec((TM, d), lambda i: (i, 0)),
                pl.BlockSpec((1, d), lambda i: (0, 0)),
                pl.BlockSpec((TM, d), lambda i: (i, 0))]
    args = [dy, w, h_prev, g, dres]
    if after is not None:
        in_specs.append(pl.BlockSpec(memory_space=pl.ANY))
        args.append(after)
    return pl.pallas_call(
        body, name=name, grid=(s_len // TM,), in_specs=in_specs,
        out_specs=[pl.BlockSpec((TM, d), lambda i: (i, 0)), pl.BlockSpec((1, d), lambda i: (0, 0))],
        out_shape=[jax.ShapeDtypeStruct((s_len, d), f32), jax.ShapeDtypeStruct((1, d), f32)],
        compiler_params=_cparams(("arbitrary",)),
    )(*args)


def _mlp_fwd(h1, g, w_up, w_down, layer, name, tf=1024):
    s_len, d = h1.shape
    ff = w_up.shape[2]

    def body(h_ref, g_ref, wu_ref, wd_ref, h2_ref, a_ref, hn_ref):
        x = h_ref[...]
        r = lax.rsqrt(jnp.mean(x * x, axis=-1, keepdims=True) + EPS)
        hn = ((x * r) * g_ref[...]).astype(hn_ref.dtype)
        hn_ref[...] = hn
        acc = x
        for j in range(ff // tf):
            a = _dot(hn, wu_ref[:, j * tf:(j + 1) * tf])
            a_ref[:, j * tf:(j + 1) * tf] = a.astype(a_ref.dtype)
            relu = jnp.maximum(a, 0.0)
            acc = acc + _dot(_mx(relu * relu), wd_ref[j * tf:(j + 1) * tf, :])
        h2_ref[...] = acc

    row = lambda i: (i, 0)
    return pl.pallas_call(
        body, name=name, grid=(s_len // TM,),
        in_specs=[pl.BlockSpec((TM, d), row), pl.BlockSpec((1, d), lambda i: (0, 0)),
                  _resident((None, d, ff), lambda i: (layer, 0, 0)), _resident((None, ff, d), lambda i: (layer, 0, 0))],
        out_specs=[pl.BlockSpec((TM, d), row), pl.BlockSpec((TM, ff), row), pl.BlockSpec((TM, d), row)],
        out_shape=[jax.ShapeDtypeStruct((s_len, d), f32), jax.ShapeDtypeStruct((s_len, ff), MXU_DTYPE),
                   jax.ShapeDtypeStruct((s_len, d), MXU_DTYPE)],
        compiler_params=_cparams(("parallel",)),
    )(h1, g, w_up, w_down)


def _mlp_bwd(dh2, w_down, w_up, layer, a, h1, g, name, tf=1024):
    s_len, d = dh2.shape
    ff = a.shape[1]

    def body(dh_ref, wd_ref, wu_ref, a_ref, h_ref, g_ref, dh1_ref, dg_ref, da_ref):
        i = pl.program_id(0)
        dh = dh_ref[...]
        dhb = _mx(dh)
        acc = None
        for j in range(ff // tf):
            cols = slice(j * tf, (j + 1) * tf)
            dact = _dot_nt(dhb, wd_ref[cols, :])
            da = (dact * (2.0 * jnp.maximum(a_ref[:, cols].astype(f32), 0.0))).astype(da_ref.dtype)
            da_ref[:, cols] = da
            part = _dot_nt(da, wu_ref[:, cols])
            acc = part if acc is None else acc + part
        dx, dgrow = _rmsnorm_bwd(acc, h_ref[...], g_ref[...])
        dh1_ref[...] = dh + dx
        dgsum = jnp.sum(dgrow, axis=0, keepdims=True)

        @pl.when(i == 0)
        def _():
            dg_ref[...] = dgsum

        @pl.when(i > 0)
        def _():
            dg_ref[...] += dgsum

    row = lambda i: (i, 0)
    return pl.pallas_call(
        body, name=name, grid=(s_len // TM,),
        in_specs=[pl.BlockSpec((TM, d), row), _resident((None, ff, d), lambda i: (layer, 0, 0)),
                  _resident((None, d, ff), lambda i: (layer, 0, 0)), pl.BlockSpec((TM, ff), row),
                  pl.BlockSpec((TM, d), row), pl.BlockSpec((1, d), lambda i: (0, 0))],
        out_specs=[pl.BlockSpec((TM, d), row), pl.BlockSpec((1, d), lambda i: (0, 0)), pl.BlockSpec((TM, ff), row)],
        out_shape=[jax.ShapeDtypeStruct((s_len, d), f32), jax.ShapeDtypeStruct((1, d), f32),
                   jax.ShapeDtypeStruct((s_len, ff), MXU_DTYPE)],
        compiler_params=_cparams(("arbitrary",), vmem=VMEM_LIMIT_LARGE),
    )(dh2, w_down, w_up, a, h1, g)


def _weight_grad(a, b, name, act=False):
    s_len, k_dim = a.shape
    n = b.shape[1]
    tka = min(k_dim, 2048)
    tnb = n if n <= 1024 else (2048 if n % 2048 == 0 else 640)
    ns = s_len // TM_WGRAD

    def body(a_ref, b_ref, o_ref, acc_ref):
        s = pl.program_id(2)
        x = a_ref[...]
        if act:
            relu = jnp.maximum(x.astype(f32), 0.0)
            x = relu * relu
        part = _dot_tn(_mx(x), _mx(b_ref[...]))

        @pl.when(s == 0)
        def _():
            acc_ref[...] = part

        @pl.when(s > 0)
        def _():
            acc_ref[...] += part

        @pl.when(s == ns - 1)
        def _():
            o_ref[...] = acc_ref[...].astype(o_ref.dtype)

    return pl.pallas_call(
        body, name=name, grid=(k_dim // tka, n // tnb, ns),
        in_specs=[pl.BlockSpec((TM_WGRAD, tka), lambda i, j, s: (s, i)),
                  pl.BlockSpec((TM_WGRAD, tnb), lambda i, j, s: (s, j))],
        out_specs=pl.BlockSpec((tka, tnb), lambda i, j, s: (i, j)),
        out_shape=jax.ShapeDtypeStruct((k_dim, n), COMM_DTYPE),
        scratch_shapes=[pltpu.VMEM((tka, tnb), f32)],
        compiler_params=_cparams(("parallel", "parallel", "arbitrary")),
    )(a, b)


def _group_select(lane, x2, x4, x8, x16):
    grp = lane // POOL_GC
    return jnp.where(grp == 0, x2, jnp.where(grp == 1, x4, jnp.where(grp == 2, x8, x16)))


def _pool_window(lane):
    grp = lane // POOL_GC
    return jnp.where(grp == 0, 2, jnp.where(grp == 1, 4, jnp.where(grp == 2, 8, 16)))


def _pool_y(u, halo, i):
    xs = jnp.concatenate([jnp.where(i > 0, halo, 0.0), u], axis=0)
    s2 = xs + pltpu.roll(xs, 1, axis=0)
    s4 = s2 + pltpu.roll(s2, 2, axis=0)
    s8 = s4 + pltpu.roll(s4, 4, axis=0)
    s16 = s8 + pltpu.roll(s8, 8, axis=0)
    lane = lax.broadcasted_iota(jnp.int32, xs.shape, 1)
    sel = _group_select(lane, s2, s4, s8, s16)[HALO:, :]
    t = i * TM + lax.broadcasted_iota(jnp.int32, u.shape, 0)
    cnt = jnp.minimum(_pool_window(lax.broadcasted_iota(jnp.int32, u.shape, 1)), t + 1).astype(f32)
    return sel / cnt - u


def _group_weights(l0, l1, l2):
    mx = jnp.maximum(jnp.maximum(l0, l1), l2)
    e0, e1, e2 = jnp.exp(l0 - mx), jnp.exp(l1 - mx), jnp.exp(l2 - mx)
    den = e0 + e1 + e2
    return e0 / den, e1 / den, e2 / den


def _mixer_out_proj(z, wbd, scale, outs, lses, w_out, layer, h, name):
    s_len, d = h.shape

    def body(u_ref, halo_ref, wbd_ref, sc_ref, o0, o1, o2, l0, l1, l2, wo_ref, h_ref, m_ref, h1_ref):
        i = pl.program_id(0)
        y = _pool_y(u_ref[...], halo_ref[...], i)
        pool = _dot(_mx(y), wbd_ref[...]) * sc_ref[...]
        w0, w1, w2 = _group_weights(l0[...], l1[...], l2[...])
        m = jnp.concatenate([pool, o0[...] * w0, o1[...] * w1, o2[...] * w2], axis=1).astype(m_ref.dtype)
        m_ref[...] = m
        h1_ref[...] = h_ref[...] + _dot(m, wo_ref[...])

    row = lambda i: (i, 0)
    blk = pl.BlockSpec((TM, 256), row)
    grp = [pl.BlockSpec((TM, 256), lambda i, g=g: (i, g)) for g in range(3)]
    return pl.pallas_call(
        body, name=name, grid=(s_len // TM,),
        in_specs=[blk, pl.BlockSpec((HALO, 256), lambda i: (jnp.maximum(i * (TM // HALO) - 1, 0), 0)),
                  pl.BlockSpec((256, 256), lambda i: (0, 0)), pl.BlockSpec((1, 256), lambda i: (0, 0))] + grp + grp
        + [_resident((None, d, d), lambda i: (layer, 0, 0)), pl.BlockSpec((TM, d), row)],
        out_specs=[pl.BlockSpec((TM, d), row)] * 2,
        out_shape=[jax.ShapeDtypeStruct((s_len, d), MXU_DTYPE), jax.ShapeDtypeStruct((s_len, d), f32)],
        compiler_params=_cparams(("parallel",)),
    )(z, z, wbd, scale, outs, outs, outs, lses, lses, lses, w_out, h)


def _head_sums(x):
    r = lax.broadcasted_iota(jnp.int32, (256, 256), 0) // HEAD_DIM
    c = lax.broadcasted_iota(jnp.int32, (256, 256), 1) // HEAD_DIM
    ones = jnp.where(r == c, 1.0, 0.0).astype(jnp.bfloat16)
    hi = x.astype(jnp.bfloat16)
    lo = (x - hi.astype(f32)).astype(jnp.bfloat16)
    return _dot(hi, ones) + _dot(lo, ones)


def _out_combine_bwd(dh1, w_out, layer, outs, lses, name, after=None):
    s_len, d = dh1.shape

    def body(dh_ref, w_ref, o0, o1, o2, l0, l1, l2, *rest):
        dp_ref, do_ref, dl_ref = rest[-3:]
        dm = _dot_nt(_mx(dh_ref[...]), w_ref[...])
        dp_ref[...] = dm[:, :POOL_WIDTH]
        w = _group_weights(l0[...], l1[...], l2[...])
        da = [dm[:, POOL_WIDTH + 256 * g:POOL_WIDTH + 256 * (g + 1)] for g in range(3)]
        o = (o0[...], o1[...], o2[...])
        dw = [_head_sums(da[g] * o[g]) for g in range(3)]
        t = w[0] * dw[0] + w[1] * dw[1] + w[2] * dw[2]
        do_ref[...] = jnp.concatenate([da[g] * w[g] for g in range(3)], axis=1)
        dl_ref[...] = jnp.concatenate([w[g] * t for g in range(3)], axis=1)

    grp = [pl.BlockSpec((TM, 256), lambda i, g=g: (i, g)) for g in range(3)]
    in_specs = [pl.BlockSpec((TM, d), lambda i: (i, 0)), _resident((None, d, d), lambda i: (layer, 0, 0))] + grp + grp
    args = [dh1, w_out, outs, outs, outs, lses, lses, lses]
    if after is not None:
        in_specs.append(pl.BlockSpec(memory_space=pl.ANY))
        args.append(after)
    return pl.pallas_call(
        body, name=name, grid=(s_len // TM,), in_specs=in_specs,
        out_specs=[pl.BlockSpec((TM, POOL_WIDTH), lambda i: (i, 0))] + [pl.BlockSpec((TM, ATTN_WIDTH), lambda i: (i, 0))] * 2,
        out_shape=[jax.ShapeDtypeStruct((s_len, POOL_WIDTH), f32)] + [jax.ShapeDtypeStruct((s_len, ATTN_WIDTH), f32)] * 2,
        compiler_params=_cparams(("parallel",)),
    )(*args)


def _pool_bwd(z, dm, wbd, scale, name, after=None):
    s_len = z.shape[0]
    n_halo = s_len // HALO

    def body(u_ref, uh_ref, d_ref, dh_ref, wbd_ref, sc_ref, *rest):
        du_ref, dw_ref, dsc_ref = rest[-3:]
        i = pl.program_id(0)
        last = pl.num_programs(0) - 1
        y = _pool_y(u_ref[...], uh_ref[...], i)
        yb = _mx(y)
        dpo = d_ref[...]
        sc = sc_ref[...]
        dsc = jnp.sum(dpo * _dot(yb, wbd_ref[...]), axis=0, keepdims=True)
        dwp = _dot_tn(yb, _mx(dpo * sc))

        @pl.when(i == 0)
        def _():
            dsc_ref[...] = dsc
            dw_ref[...] = dwp

        @pl.when(i > 0)
        def _():
            dsc_ref[...] += dsc
            dw_ref[...] += dwp

        ext = jnp.concatenate([dpo, jnp.where(i < last, dh_ref[...], 0.0)], axis=0)
        dy = _dot_nt(_mx(ext * sc), wbd_ref[...])
        t = i * TM + lax.broadcasted_iota(jnp.int32, ext.shape, 0)
        lane = lax.broadcasted_iota(jnp.int32, ext.shape, 1)
        e = dy / jnp.minimum(_pool_window(lane), t + 1).astype(f32)
        rows = ext.shape[0]
        f2 = e + pltpu.roll(e, rows - 1, axis=0)
        f4 = f2 + pltpu.roll(f2, rows - 2, axis=0)
        f8 = f4 + pltpu.roll(f4, rows - 4, axis=0)
        f16 = f8 + pltpu.roll(f8, rows - 8, axis=0)
        du_ref[...] = (_group_select(lane, f2, f4, f8, f16) - dy)[:TM, :].astype(du_ref.dtype)

    row = lambda i: (i, 0)
    blk = pl.BlockSpec((TM, 256), row)
    extra = [] if after is None else [after]
    return pl.pallas_call(
        body, name=name, grid=(s_len // TM,),
        in_specs=[blk, pl.BlockSpec((HALO, 256), lambda i: (jnp.maximum(i * (TM // HALO) - 1, 0), 0)),
                  blk, pl.BlockSpec((HALO, 256), lambda i: (jnp.minimum((i + 1) * (TM // HALO), n_halo - 1), 0)),
                  pl.BlockSpec((256, 256), lambda i: (0, 0)), pl.BlockSpec((1, 256), lambda i: (0, 0))]
        + [pl.BlockSpec(memory_space=pl.ANY)] * len(extra),
        out_specs=[blk, pl.BlockSpec((256, 256), lambda i: (0, 0)), pl.BlockSpec((1, 256), lambda i: (0, 0))],
        out_shape=[jax.ShapeDtypeStruct((s_len, N_IN), MXU_DTYPE), jax.ShapeDtypeStruct((256, 256), f32),
                   jax.ShapeDtypeStruct((1, 256), f32)],
        compiler_params=_cparams(("arbitrary",)),
    )(z, z, dm, dm, wbd, scale, *extra)


def _tri_masks():
    qi = lax.broadcasted_iota(jnp.int32, (BLK, BLK), 0)
    ki = lax.broadcasted_iota(jnp.int32, (BLK, BLK), 1)
    return qi >= ki, ki >= qi


ATTN_SUPER_PER_STEP = (8, 4, 1)
Q_COL, K_COL, V_COL = POOL_WIDTH // 128, (POOL_WIDTH + ATTN_WIDTH) // 128, (POOL_WIDTH + 2 * ATTN_WIDTH) // 128


def _rows(ref, start, dil):
    if dil == 1:
        return ref[pl.ds(start, BLK), :]
    return ref[pl.ds(start, BLK, stride=dil), :]


ATTN_BLOCKS_TOGETHER = 8


def _set_rows(ref, start, dil, val):
    if dil == 1:
        ref[pl.ds(start, BLK), :] = val
    else:
        ref[pl.ds(start, BLK, stride=dil), :] = val


def _attn_fwd(z, g, prev, name):
    s_len = z.shape[0]
    dil, m = DILATIONS[g], ATTN_SUPER_PER_STEP[g]
    sbr = BLK * dil
    rows = sbr * m

    def body(*refs):
        q_ref, kc_ref, kp_ref, vc_ref, vp_ref = refs[:5]
        o_ref, l_ref = refs[-2:]
        st = pl.program_id(0)
        low, up = _tri_masks()
        head0 = lax.broadcasted_iota(jnp.int32, (BLK, 128), 1) < HEAD_DIM
        blocks = [(sb, r) for sb in range(m) for r in range(dil)]
        for g0 in range(0, len(blocks), ATTN_BLOCKS_TOGETHER):
            grp = blocks[g0:g0 + ATTN_BLOCKS_TOGETHER]
            loaded = []
            for sb, r in grp:
                base = sb * sbr + r
                if sb == 0:
                    kp, vp = _rows(kp_ref, r, dil), _rows(vp_ref, r, dil)
                else:
                    kp, vp = _rows(kc_ref, base - sbr, dil), _rows(vc_ref, base - sbr, dil)
                qs = _rows(q_ref, base, dil) * ATTN_SCALE
                loaded.append((_mx(jnp.where(head0, qs, 0.0)), _mx(jnp.where(head0, 0.0, qs)),
                               jnp.concatenate([_mx(kp), _mx(_rows(kc_ref, base, dil))], axis=0),
                               jnp.concatenate([_mx(vp), _mx(_rows(vc_ref, base, dil))], axis=0)))
            scores = [(_dot_nt(q0, k2), _dot_nt(q1, k2)) for q0, q1, k2, _ in loaded]
            soft = []
            for (sb, _), pair in zip(grp, scores):
                valid = jnp.concatenate([up & (st > 0) if sb == 0 else up, low], axis=1)
                heads = []
                for s in pair:
                    s = jnp.where(valid, s, NEG_BIG)
                    mx = jnp.max(s, axis=-1, keepdims=True)
                    e = jnp.exp(s - mx)
                    l = jnp.sum(e, axis=-1, keepdims=True)
                    heads.append((_mx(e / l), jnp.broadcast_to(mx + jnp.log(l), (BLK, 128))))
                soft.append(heads)
            for (sb, r), heads, (_, _, _, v2) in zip(grp, soft, loaded):
                base = sb * sbr + r
                _set_rows(o_ref, base, dil, jnp.where(head0, _dot(heads[0][0], v2), _dot(heads[1][0], v2)))
                _set_rows(l_ref, base, dil, jnp.where(head0, heads[0][1], heads[1][1]))

    def cur(col):
        return pl.BlockSpec((rows, 128), lambda st, hp: (st, col + 2 * g + hp))

    def before(col):
        return pl.BlockSpec((sbr, 128), lambda st, hp: (jnp.maximum(st * m - 1, 0), col + 2 * g + hp))

    in_specs = [cur(Q_COL), cur(K_COL), before(K_COL), cur(V_COL), before(V_COL)]
    args = [z, z, z, z, z]
    aliases = {}
    if prev is not None:
        in_specs += [pl.BlockSpec(memory_space=pl.ANY)] * 2
        args += list(prev)
        aliases = {5: 0, 6: 1}
    return pl.pallas_call(
        body, name=name, grid=(s_len // rows, 2), in_specs=in_specs, out_specs=[cur(0), cur(0)],
        out_shape=[jax.ShapeDtypeStruct((s_len, ATTN_WIDTH), f32)] * 2, input_output_aliases=aliases,
        compiler_params=_cparams(("parallel", "parallel")),
    )(*args)


def _stack_heads(x, head0):
    return jnp.concatenate([_mx(jnp.where(head0, x, 0.0)), _mx(jnp.where(head0, 0.0, x))], axis=0)


def _head_rows(x):
    xt = x.T
    return jnp.concatenate([jnp.broadcast_to(xt[0:1, :], (BLK, BLK)),
                            jnp.broadcast_to(xt[HEAD_DIM:HEAD_DIM + 1, :], (BLK, BLK))], axis=0)


def _attn_bwd(z, do, lse, dlt, tabs, dz, g, name):
    s_len = z.shape[0]
    dil, m = DILATIONS[g], ATTN_SUPER_PER_STEP[g]
    sbr = BLK * dil
    rows = sbr * m
    nsteps = s_len // rows

    def body(q_ref, qn_ref, kc_ref, kp_ref, vc_ref, vp_ref, do_ref, don_ref, l_ref, ln_ref, d_ref, dn_ref,
             c_ref, s1_ref, s2_ref, dz_in, dz_ref, dq_buf, dk_buf, dv_buf, out_buf, sems):
        del dz_in
        st, hp = pl.program_id(0), pl.program_id(1)
        head0 = lax.broadcasted_iota(jnp.int32, (BLK, 128), 1) < HEAD_DIM
        key_i = lax.broadcasted_iota(jnp.int32, (2 * BLK, BLK), 0) & (BLK - 1)
        query_i = lax.broadcasted_iota(jnp.int32, (2 * BLK, BLK), 1)
        same_t, cross_t = query_i >= key_i, key_i >= query_i
        def load(r):
            keys, vals = [_stack_heads(_rows(kp_ref, r, dil), head0)], [_stack_heads(_rows(vp_ref, r, dil), head0)]
            qs, dos, lses, dlts = [], [], [], []
            for sb in range(m):
                base = sb * sbr + r
                keys.append(_stack_heads(_rows(kc_ref, base, dil), head0))
                vals.append(_stack_heads(_rows(vc_ref, base, dil), head0))
                qs.append(_mx(_rows(q_ref, base, dil)))
                dos.append(_mx(_rows(do_ref, base, dil)))
                lses.append(_head_rows(_rows(l_ref, base, dil)))
                dlts.append(_head_rows(_rows(d_ref, base, dil)))
            qs.append(_mx(_rows(qn_ref, r, dil)))
            dos.append(_mx(_rows(don_ref, r, dil)))
            lses.append(_head_rows(_rows(ln_ref, r, dil)))
            dlts.append(_head_rows(_rows(dn_ref, r, dil)))
            return keys, vals, qs, dos, lses, dlts

        def products(data):
            keys, vals, qs, dos, _, _ = data
            return ([(_dot_nt(keys[j + 1], qs[j]), _dot_nt(vals[j + 1], dos[j])) for j in range(m)],
                    [(_dot_nt(keys[j], qs[j]), _dot_nt(vals[j], dos[j])) for j in range(m + 1)])

        def finish(data, raw):
            lses, dlts = data[4], data[5]

            def one(pair, j, valid):
                p = jnp.where(valid, jnp.exp(pair[0] * ATTN_SCALE - lses[j]), 0.0)
                return _mx(p), _mx(p * (pair[1] - dlts[j]) * ATTN_SCALE)

            same = [one(raw[0][j], j, same_t) for j in range(m)]
            cross = [one(raw[1][j], j, cross_t & (st > 0) if j == 0 else
                         (cross_t & (st < nsteps - 1) if j == m else cross_t)) for j in range(m + 1)]
            return same, cross

        def gradients(r, data, fin):
            keys, _, qs, dos, _, _ = data
            same, cross = fin
            for sb in range(m):
                base = sb * sbr + r
                (p_a, ds_a), (_, ds_x), (p_n, ds_n) = same[sb], cross[sb], cross[sb + 1]
                dq = _dot_tn(ds_a, keys[sb + 1]) + _dot_tn(ds_x, keys[sb])
                dk2 = _dot(ds_a, qs[sb]) + _dot(ds_n, qs[sb + 1])
                dv2 = _dot(p_a, dos[sb]) + _dot(p_n, dos[sb + 1])
                c, s1, s2 = _rows(c_ref, base, dil), _rows(s1_ref, base, dil), _rows(s2_ref, base, dil)
                _set_rows(dq_buf, base, dil, _rope_transpose(dq, c, s1, s2, 128))
                _set_rows(dk_buf, base, dil, _rope_transpose(jnp.where(head0, dk2[:BLK], dk2[BLK:]), c, s1, s2, 128))
                _set_rows(dv_buf, base, dil, jnp.where(head0, dv2[:BLK], dv2[BLK:]))

        def residue_group(rg, carry):
            rs = [rg * group + i for i in range(group)]
            data = [load(r) for r in rs]
            raws = [products(d) for d in data]
            fins = [finish(d, raw) for d, raw in zip(data, raws)]
            for r, d, fin in zip(rs, data, fins):
                gradients(r, d, fin)
            return carry

        group = max(1, min(dil, ATTN_BLOCKS_TOGETHER // m))
        if dil // group <= 2:
            for rg in range(dil // group):
                residue_group(rg, 0)
        else:
            lax.fori_loop(0, dil // group, residue_group, 0)
        copies = []
        for t, (buf, col) in enumerate(((dq_buf, Q_COL), (dk_buf, K_COL), (dv_buf, V_COL))):
            out_buf[t] = buf[...].astype(out_buf.dtype)
            lane0 = pl.multiple_of((col + 2 * g + hp) * 128, 128)
            dst = dz_ref.at[pl.ds(pl.multiple_of(st * rows, rows), rows), pl.ds(lane0, 128)]
            cp = pltpu.make_async_copy(out_buf.at[t], dst, sems.at[t])
            cp.start()
            copies.append(cp)
        for cp in copies:
            cp.wait()

    def cur(col):
        return pl.BlockSpec((rows, 128), lambda st, hp: (st, col + 2 * g + hp))

    def before(col):
        return pl.BlockSpec((sbr, 128), lambda st, hp: (jnp.maximum(st * m - 1, 0), col + 2 * g + hp))

    def after(col):
        return pl.BlockSpec((sbr, 128), lambda st, hp: (jnp.minimum((st + 1) * m, s_len // sbr - 1), col + 2 * g + hp))

    tab = pl.BlockSpec((rows, 128), lambda st, hp: (st, 0))
    return pl.pallas_call(
        body, name=name, grid=(nsteps, 2),
        in_specs=[cur(Q_COL), after(Q_COL), cur(K_COL), before(K_COL), cur(V_COL), before(V_COL),
                  cur(0), after(0), cur(0), after(0), cur(0), after(0), tab, tab, tab,
                  pl.BlockSpec(memory_space=pl.ANY)],
        out_specs=pl.BlockSpec(memory_space=pl.ANY),
        out_shape=jax.ShapeDtypeStruct(dz.shape, dz.dtype), input_output_aliases={15: 0},
        scratch_shapes=[pltpu.VMEM((rows, 128), f32)] * 3 + [pltpu.VMEM((3, rows, 128), dz.dtype),
                                                            pltpu.SemaphoreType.DMA((3,))],
        compiler_params=_cparams(("arbitrary", "arbitrary")),
    )(z, z, z, z, z, z, do, do, lse, lse, dlt, dlt, *tabs, dz)


def _rope_tables(positions):
    inv_freq = ROPE_THETA ** (-jnp.arange(0, ROT_DIM, 2, dtype=f32) / ROT_DIM)
    ang = positions.astype(f32)[:, None] * inv_freq
    cos, sin = jnp.cos(ang), jnp.sin(ang)
    s_len = positions.shape[0]
    zero8, rest = jnp.zeros((s_len, 8), f32), jnp.zeros((s_len, HEAD_DIM - ROT_DIM), f32)
    c = jnp.concatenate([cos, cos, jnp.ones((s_len, HEAD_DIM - ROT_DIM), f32)], axis=1)
    s1 = jnp.concatenate([-sin, zero8, rest], axis=1)
    s2 = jnp.concatenate([zero8, sin, rest], axis=1)
    return c, s1, s2


def _block_diag(pool_w):
    out = jnp.zeros((POOL_WIDTH, POOL_WIDTH), pool_w.dtype)
    for g in range(4):
        out = lax.dynamic_update_slice(out, pool_w[g], (g * POOL_GC, g * POOL_GC))
    return out


def _layer_fwd(h, p_l, wsrc, small, layer, tabs, head=None):
    nm = f"l{layer}_"
    wts, wl = wsrc.take(layer, ("w_in",), (h,) if layer else tuple(tabs))
    z, hn1 = _norm_matmul(h, small["norm1"][layer][None], wts["w_in"], wl, 256, nm + "in_proj", rope=tabs)
    ol = None
    for g in range(3):
        ol = _attn_fwd(z, g, ol, nm + f"attn_fwd{g}")
    outs, lses = ol
    wbd = _mx(_block_diag(small["pool_w"][layer]))
    scale = small["pool_scale"][layer][None]
    wts.update(wsrc.take(layer, ("w_out",), (outs,))[0])
    m, h1 = _mixer_out_proj(z, wbd, scale, outs, lses, wts["w_out"], wl, h, nm + "mixer_out")
    wts.update(wsrc.take(layer, ("w_up", "w_down"), (h1,))[0])
    h2, a, hn2 = _mlp_fwd(h1, small["norm2"][layer][None], wts["w_up"], wts["w_down"], wl, nm + "mlp")
    wts.update(wsrc.take(layer, ("w_gate", "w_ple"), (h2,))[0])
    *h3, gl, hn3 = _gate_ple_fwd(h2, small["norm3"][layer][None], wts["w_gate"], wts["w_ple"], wl, p_l,
                                 nm + "gate_ple", head=head)
    saved = dict(h=h, z=z, hn1=hn1, outs=outs, lses=lses, wbd=wbd, scale=scale, m=m, h1=h1, a=a, hn2=hn2, h2=h2,
                 gl=gl, hn3=hn3, wts=wts, wl=wl)
    return h3, saved


def _layer_bwd(dh3, sv, p_l, small, layer, tabs128, reducer):
    nm = f"l{layer}_"
    wts, wl = sv["wts"], sv["wl"]
    dh2, dg3, de, dgl = _gate_bwd(dh3, sv["gl"], p_l, wts["w_ple"], wts["w_gate"], wl, sv["h2"],
                                  small["norm3"][layer][None], nm + "gate_bwd")
    reducer.add("w_gate", layer, _weight_grad(sv["hn3"], dgl, nm + "dw_gate"))
    reducer.add("w_ple", layer, _weight_grad(p_l, de, nm + "dw_ple"))
    dh1, dg2, da = _mlp_bwd(dh2, wts["w_down"], wts["w_up"], wl, sv["a"], sv["h1"], small["norm2"][layer][None],
                            nm + "mlp_bwd")
    reducer.add("w_down", layer, _weight_grad(sv["a"], dh2, nm + "dw_down", act=True))
    started = reducer.add("w_up", layer, _weight_grad(sv["hn2"], da, nm + "dw_up"))
    dpool, do, dlt = _out_combine_bwd(dh1, wts["w_out"], wl, sv["outs"], sv["lses"], nm + "out_bwd", after=started)
    started = reducer.add("w_out", layer, _weight_grad(sv["m"], dh1, nm + "dw_out"))
    dz, dwbd, dscale = _pool_bwd(sv["z"], dpool, sv["wbd"], sv["scale"], nm + "pool_bwd", after=started)
    for g in range(3):
        dz = _attn_bwd(sv["z"], do, sv["lses"], dlt, tabs128, dz, g, nm + f"attn_bwd{g}")
    started = reducer.add("w_in", layer, _weight_grad(sv["hn1"], dz, nm + "dw_in"))
    dh0, dg1 = _matmul_nt_norm_bwd(dz, wts["w_in"], wl, sv["h"], small["norm1"][layer][None], dh1, nm + "in_bwd",
                                   tk=512, after=started)
    dpool_w = jnp.stack([dwbd[g * POOL_GC:(g + 1) * POOL_GC, g * POOL_GC:(g + 1) * POOL_GC] for g in range(4)])
    sg = dict(norm1=dg1[0], norm2=dg2[0], norm3=dg3[0], pool_w=dpool_w, pool_scale=dscale[0])
    return dh0, sg


def _local_step(x, p, positions, wsrc, small, target, reducer):
    tabs128 = tuple(jnp.tile(t, (1, 2)) for t in _rope_tables(positions))
    (h,), sv0 = _layer_fwd(x, p[0], wsrc, small, 0, tabs128)
    (loss, dh, dgf), sv1 = _layer_fwd(h, p[1], wsrc, small, 1, tabs128, head=(small["final_norm"][None], target))
    saved = [sv0, sv1]
    sgs = [None, None]
    for layer in (1, 0):
        dh, sgs[layer] = _layer_bwd(dh, saved[layer], p[layer], small, layer, tabs128, reducer)
    small_grads = {k: jnp.stack([sgs[0][k], sgs[1][k]]) for k in sgs[0]}
    small_grads["final_norm"] = dgf[0]
    return loss, dh, small_grads


HBM = pl.BlockSpec(memory_space=pltpu.HBM)


def _my_place():
    return lax.axis_index("x"), lax.axis_index("y"), lax.axis_index("c")


def _other_chips(x, y):
    return [(1 - x, y), (x, 1 - y), (1 - x, 1 - y)]


def _window(ref, name, chip):
    k, n = _shard_shape(name)
    if COL_SHARDED[name]:
        return ref.at[:, pl.ds(pl.multiple_of(chip * n, 128), n)]
    return ref.at[pl.ds(pl.multiple_of(chip * k, 128), k), :]


def _chip_index():
    return jnp.reshape(2 * lax.axis_index("x") + lax.axis_index("y"), (1,)).astype(jnp.int32)


def _shard_block(name, tr):
    ks, ns = _shard_shape(name)
    if COL_SHARDED[name]:
        return (tr, ns), lambda i, me: (i, me[0])
    return (tr, ns), lambda i, me: (me[0] * (ks // tr) + i, 0)


def _place_shard(w, name, layer):
    ks, ns = _shard_shape(name)
    tr = min(ks, 256)
    shape, index = _shard_block(name, tr)

    def body(me_ref, w_ref, o_ref):
        o_ref[...] = w_ref[...].astype(o_ref.dtype)

    return pl.pallas_call(
        body, name=f"place_{name}{layer}",
        grid_spec=pltpu.PrefetchScalarGridSpec(
            num_scalar_prefetch=1, grid=(ks // tr,),
            in_specs=[pl.BlockSpec((None, tr, ns), lambda i, me: (layer, i, 0))],
            out_specs=pl.BlockSpec((None,) + shape, lambda i, me: (0,) + index(i, me))),
        out_shape=jax.ShapeDtypeStruct((1,) + FULL_SHAPE[name], MXU_DTYPE),
        compiler_params=_cparams(("parallel",)),
    )(_chip_index(), w)


GATHER_ORDER = [("w_in", 0), ("w_out", 0), ("w_up", 0), ("w_down", 0), ("w_gate", 0), ("w_ple", 0),
                ("w_in", 1), ("w_out", 1), ("w_up", 1), ("w_down", 1), ("w_gate", 1), ("w_ple", 1)]
SEM = pl.BlockSpec(memory_space=pltpu.SEMAPHORE)
EFFECT = pltpu.SideEffectType.DATAFLOW_SIDE_EFFECTING


def _gather_copy(src_ref, dst_ref, name, idx, j, chip, send_sems, recv_sems, c):
    cx, cy = chip
    return pltpu.make_async_remote_copy(
        src_ref=src_ref, dst_ref=dst_ref, send_sem=send_sems.at[3 * idx + j], recv_sem=recv_sems.at[3 * idx + j],
        device_id=(cx, cy, c), device_id_type=MESH)


def _gather_start(placed, order, tag, after=None):
    n = len(order)
    extra = [] if after is None else [after]

    def body(*refs):
        ins = refs[:n]
        k = n + len(extra)
        send_sems, recv_sems = refs[k], refs[k + 1]
        outs = refs[k + 2:k + 2 + n]
        token = refs[-1]
        x, y, c = _my_place()
        me = 2 * x + y
        for idx, (name, _) in enumerate(order):
            for j, chip in enumerate(_other_chips(x, y)):
                _gather_copy(_window(ins[idx].at[0], name, me), _window(outs[idx].at[0], name, me), name, idx, j, chip,
                             send_sems, recv_sems, c).start()
        token[...] = jnp.zeros_like(token)

    res = pl.pallas_call(
        body, name="gather_start" + tag,
        out_shape=(pltpu.SemaphoreType.DMA((3 * n,)), pltpu.SemaphoreType.DMA((3 * n,)))
        + tuple(pltpu.HBM(a.shape, a.dtype) for a in placed) + (jax.ShapeDtypeStruct((8, 128), f32),),
        in_specs=[HBM] * n + [pl.BlockSpec(memory_space=pl.ANY)] * len(extra),
        out_specs=(SEM, SEM) + (HBM,) * n + (pl.BlockSpec(memory_space=pltpu.VMEM),),
        input_output_aliases={i: i + 2 for i in range(n)},
        compiler_params=pltpu.CompilerParams(has_side_effects=EFFECT),
    )(*[pltpu.with_memory_space_constraint(a, pltpu.HBM) for a in placed], *extra)
    return res[0], res[1], list(res[2:2 + n]), res[-1]


def _gather_wait(send_sems, recv_sems, arrays, order, idxs, after, name):
    n = len(idxs)

    def body(*refs):
        ins = refs[:n]
        send_ref, recv_ref = refs[n], refs[n + 1]
        x, y, c = _my_place()
        me = 2 * x + y
        for k, idx in enumerate(idxs):
            wname = order[idx][0]
            for j, chip in enumerate(_other_chips(x, y)):
                cx, cy = chip
                mine = _window(ins[k].at[0], wname, me)
                land = _window(ins[k].at[0], wname, 2 * cx + cy)
                _gather_copy(mine, mine, wname, idx, j, chip, send_ref, recv_ref, c).wait_send()
                _gather_copy(land, land, wname, idx, j, chip, send_ref, recv_ref, c).wait_recv()

    operands = list(arrays) + [send_sems, recv_sems] + list(after)
    in_specs = [HBM] * n + [SEM, SEM] + [pl.BlockSpec(memory_space=pl.ANY)] * len(after)
    res = pl.pallas_call(
        body, name=name, out_shape=tuple(pltpu.HBM(a.shape, a.dtype) for a in arrays),
        in_specs=in_specs, out_specs=(HBM,) * n, input_output_aliases={i: i for i in range(n)},
        compiler_params=pltpu.CompilerParams(has_side_effects=EFFECT),
    )(*operands)
    return list(res)


class _GatheredWeights:
    def __init__(self, shards):
        self.starts = []
        token = None
        for tag, order in (("_first", GATHER_ORDER[:1]), ("_rest", GATHER_ORDER[1:])):
            placed = [_place_shard(shards[name], name, layer) for name, layer in order]
            self.starts.append((order,) + _gather_start(placed, order, tag, token))
            token = self.starts[-1][-1]

    def take(self, layer, names, after):
        order, send, recv, arrays, _ = next(s for s in self.starts if (names[0], layer) in s[0])
        after = list(after)
        if order is self.starts[0][0]:
            after.append(self.starts[-1][-1])
        idxs = [order.index((n, layer)) for n in names]
        got = _gather_wait(send, recv, [arrays[i] for i in idxs], order, idxs, after, f"gather_wait{layer}_{names[0]}")
        return dict(zip(names, got)), 0


N_DEV = 8


def _reduce_copies(dws, lands, names, layer, send_sems, recv_sems):
    x, y, c = _my_place()
    me, my_dev = 2 * x + y, 4 * x + 2 * y + c
    out = []
    for t, name in enumerate(names):
        for j, (cx, cy) in enumerate(_other_chips(x, y)):
            out.append((pltpu.make_async_remote_copy(
                src_ref=_window(dws[t], name, 2 * cx + cy), dst_ref=lands[t].at[my_dev],
                send_sem=send_sems.at[4 * t + j], recv_sem=recv_sems.at[N_DEV * t + my_dev],
                device_id=(cx, cy, layer), device_id_type=MESH), False))
        out.append((pltpu.make_async_remote_copy(
            src_ref=_window(dws[t], name, me), dst_ref=lands[t].at[my_dev],
            send_sem=send_sems.at[4 * t + 3], recv_sem=recv_sems.at[N_DEV * t + my_dev],
            device_id=(x, y, layer), device_id_type=MESH), True))
    return out


def _reduce_start(dws, names, layer, tag):
    n = len(names)
    lands = [lax.empty((N_DEV,) + _shard_shape(nm), dws[0].dtype) for nm in names]

    def body(*refs):
        ins = refs[:n]
        send_sems, recv_sems = refs[2 * n], refs[2 * n + 1]
        land_out = refs[3 * n + 2:4 * n + 2]
        token = refs[-1]
        c = lax.axis_index("c")
        for cp, non_owner_only in _reduce_copies(ins, land_out, names, layer, send_sems, recv_sems):
            if non_owner_only:
                @pl.when(c != layer)
                def _():
                    cp.start()
            else:
                cp.start()
        token[...] = jnp.zeros_like(token)

    res = pl.pallas_call(
        body, name="reduce_start" + tag,
        out_shape=(pltpu.SemaphoreType.DMA((4 * n,)), pltpu.SemaphoreType.DMA((N_DEV * n,)))
        + tuple(pltpu.HBM(a.shape, a.dtype) for a in dws) + tuple(pltpu.HBM(a.shape, a.dtype) for a in lands)
        + (jax.ShapeDtypeStruct((8, 128), f32),),
        in_specs=[HBM] * (2 * n),
        out_specs=(SEM, SEM) + (HBM,) * (2 * n) + (pl.BlockSpec(memory_space=pltpu.VMEM),),
        input_output_aliases={i: i + 2 for i in range(2 * n)},
        compiler_params=pltpu.CompilerParams(has_side_effects=EFFECT),
    )(*[pltpu.with_memory_space_constraint(a, pltpu.HBM) for a in list(dws) + lands])
    return res[0], res[1], list(res[2:2 + n]), list(res[2 + n:2 + 2 * n]), res[-1]


def _reduce_wait(send_sems, recv_sems, dws, lands, names, layer, after, tag):
    n = len(names)

    def body(*refs):
        ins, land_in = refs[:n], refs[n:2 * n]
        send_ref, recv_ref = refs[2 * n], refs[2 * n + 1]
        x, y, c = _my_place()
        for cp, non_owner_only in _reduce_copies(ins, land_in, names, layer, send_ref, recv_ref):
            if non_owner_only:
                @pl.when(c != layer)
                def _():
                    cp.wait_send()
            else:
                cp.wait_send()

        @pl.when(c == layer)
        def _():
            for t in range(n):
                for k in range(1, N_DEV):
                    px, py, pc = x ^ ((k >> 2) & 1), y ^ ((k >> 1) & 1), c ^ (k & 1)
                    dev = 4 * px + 2 * py + pc
                    land = land_in[t].at[dev]
                    pltpu.make_async_remote_copy(
                        src_ref=land, dst_ref=land, send_sem=send_ref.at[4 * t], recv_sem=recv_ref.at[N_DEV * t + dev],
                        device_id=(px, py, pc), device_id_type=MESH).wait_recv()

    res = pl.pallas_call(
        body, name="reduce_wait" + tag,
        out_shape=tuple(pltpu.HBM(a.shape, a.dtype) for a in list(dws) + list(lands)),
        in_specs=[HBM] * (2 * n) + [SEM, SEM, pl.BlockSpec(memory_space=pl.ANY)], out_specs=(HBM,) * (2 * n),
        input_output_aliases={i: i for i in range(2 * n)},
        compiler_params=pltpu.CompilerParams(has_side_effects=EFFECT),
    )(*dws, *lands, send_sems, recv_sems, after)
    return list(res[:n]), list(res[n:])


def _sum_devices(land, own, name, layer, prev):
    ks, ns = _shard_shape(name)
    tr = min(ks, 256)
    shape, index = _shard_block(name, tr)

    def body(me_ref, dev_ref, *refs):
        s_ref, own_ref, out_ref = refs[0], refs[1], refs[-1]
        dev = dev_ref[0]
        acc = None
        for s in range(N_DEV):
            term = jnp.where(dev == s, own_ref[...], s_ref[s]).astype(f32)
            acc = term if acc is None else acc + term
        out_ref[...] = acc

    def mine(i, dev):
        return i * jnp.where((dev[0] & 1) == layer, 1, 0)

    in_specs = [pl.BlockSpec((N_DEV, tr, ns), lambda i, me, dev: (0, mine(i, dev), 0)),
                pl.BlockSpec(shape, lambda i, me, dev: index(mine(i, dev), me))]
    args = [land, own]
    aliases = {}
    if prev is not None:
        in_specs.append(pl.BlockSpec(memory_space=pl.ANY))
        args.append(prev)
        aliases = {4: 0}
    x, y, c = _my_place()
    return pl.pallas_call(
        body, name=f"sum_devices_{name}{layer}",
        grid_spec=pltpu.PrefetchScalarGridSpec(
            num_scalar_prefetch=2, grid=(ks // tr,), in_specs=in_specs,
            out_specs=pl.BlockSpec((None, tr, ns), lambda i, me, dev: (layer, mine(i, dev), 0))),
        out_shape=jax.ShapeDtypeStruct((2, ks, ns), f32), input_output_aliases=aliases,
        compiler_params=_cparams(("arbitrary",)),
    )(_chip_index(), jnp.reshape(4 * x + 2 * y + c, (1,)).astype(jnp.int32), *args)


class _GradReducer:
    GROUPS = (("1", 1, ("w_gate", "w_ple", "w_down", "w_up", "w_out", "w_in")),
              ("0a", 0, ("w_gate", "w_ple", "w_down", "w_up")),
              ("0b", 0, ("w_out",)),
              ("0c", 0, ("w_in",)))

    def __init__(self):
        self.grads = {}
        self.started = {}

    def add(self, name, layer, dw):
        self.grads[(name, layer)] = dw
        token = None
        for tag, glayer, names in self.GROUPS:
            if tag not in self.started and all((nm, glayer) in self.grads for nm in names):
                *self.started[tag], token = _reduce_start([self.grads[(nm, glayer)] for nm in names], names, glayer, tag)
        return token

    def finish(self, after):
        mine = {}
        for tag, layer, names in self.GROUPS:
            send, recv, dws, lands = self.started[tag]
            dws, lands = _reduce_wait(send, recv, dws, lands, names, layer, after, tag)
            for nm, dw, land in zip(names, dws, lands):
                mine[nm] = _sum_devices(land, dw, nm, layer, mine.get(nm))
        return _pair_layers(mine)


def _pair_layers(mine):
    names = list(BIG)

    def body(*refs):
        ins = refs[:len(names)]
        outs = refs[len(names):2 * len(names)]
        send_sems, recv_sems = refs[2 * len(names):]
        x, y, c = _my_place()
        sibling = (x, y, 1 - c)
        cps = []
        for t in range(len(names)):
            cp = pltpu.make_async_remote_copy(
                src_ref=ins[t].at[c], dst_ref=outs[t].at[c], send_sem=send_sems.at[t], recv_sem=recv_sems.at[t],
                device_id=sibling, device_id_type=MESH)
            cp.start()
            cps.append(cp)
        for t in range(len(names)):
            cps[t].wait_send()
            land = outs[t].at[1 - c]
            pltpu.make_async_remote_copy(
                src_ref=land, dst_ref=land, send_sem=send_sems.at[t], recv_sem=recv_sems.at[t],
                device_id=sibling, device_id_type=MESH).wait_recv()

    outs = pl.pallas_call(
        body, name="pair_layers", in_specs=[HBM] * len(names), out_specs=[HBM] * len(names),
        out_shape=[jax.ShapeDtypeStruct((2,) + _shard_shape(n), f32) for n in names],
        input_output_aliases={t: t for t in range(len(names))},
        scratch_shapes=[pltpu.SemaphoreType.DMA((len(names),)), pltpu.SemaphoreType.DMA((len(names),))],
    )(*[mine[n] for n in names])
    return dict(zip(names, outs))


SMALL_ROWS = 320


def _small_copies(vec_ref, land_ref, send_sems, recv_sems):
    x, y, c = _my_place()
    me = 4 * x + 2 * y + c
    out = []
    for k in range(1, N_DEV):
        peer = (x ^ ((k >> 2) & 1), y ^ ((k >> 1) & 1), c ^ (k & 1))
        src_dev = 4 * peer[0] + 2 * peer[1] + peer[2]
        send = pltpu.make_async_remote_copy(
            src_ref=vec_ref, dst_ref=land_ref.at[me], send_sem=send_sems.at[k - 1], recv_sem=recv_sems.at[k - 1],
            device_id=peer, device_id_type=MESH)
        arrival = pltpu.make_async_remote_copy(
            src_ref=land_ref.at[src_dev], dst_ref=land_ref.at[src_dev], send_sem=send_sems.at[k - 1],
            recv_sem=recv_sems.at[k - 1], device_id=peer, device_id_type=MESH)
        out.append((send, arrival))
    return out


def _small_start(vec):
    land = lax.empty((N_DEV,) + vec.shape, vec.dtype)

    def body(v_ref, land_in, send_sems, recv_sems, v_out, land_out):
        del land_in, v_out
        for send, _ in _small_copies(v_ref, land_out, send_sems, recv_sems):
            send.start()

    return pl.pallas_call(
        body, name="small_start",
        out_shape=(pltpu.SemaphoreType.DMA((N_DEV - 1,)), pltpu.SemaphoreType.DMA((N_DEV - 1,)),
                   pltpu.HBM(vec.shape, vec.dtype), pltpu.HBM(land.shape, land.dtype)),
        in_specs=[HBM, HBM], out_specs=(SEM, SEM, HBM, HBM), input_output_aliases={0: 2, 1: 3},
        compiler_params=pltpu.CompilerParams(has_side_effects=EFFECT),
    )(pltpu.with_memory_space_constraint(vec, pltpu.HBM), pltpu.with_memory_space_constraint(land, pltpu.HBM))


def _small_wait(send_sems, recv_sems, vec, land, after):
    def body(v_ref, land_ref, send_ref, recv_ref, after_ref, v_out, land_out):
        del after_ref, v_out, land_out
        for send, arrival in _small_copies(v_ref, land_ref, send_ref, recv_ref):
            send.wait_send()
            arrival.wait_recv()

    return pl.pallas_call(
        body, name="small_wait", out_shape=(pltpu.HBM(vec.shape, vec.dtype), pltpu.HBM(land.shape, land.dtype)),
        in_specs=[HBM, HBM, SEM, SEM, pl.BlockSpec(memory_space=pl.ANY)], out_specs=(HBM, HBM),
        input_output_aliases={0: 0, 1: 1}, compiler_params=pltpu.CompilerParams(has_side_effects=EFFECT),
    )(vec, land, send_sems, recv_sems, after)


def _small_sum(vec, land):
    x, y, c = _my_place()

    def body(dev_ref, v_ref, land_ref, out_ref):
        acc = None
        for s in range(N_DEV):
            term = jnp.where(dev_ref[0] == s, v_ref[...], land_ref[s])
            acc = term if acc is None else acc + term
        out_ref[...] = acc

    return pl.pallas_call(
        body, name="small_sum",
        grid_spec=pltpu.PrefetchScalarGridSpec(
            num_scalar_prefetch=1, grid=(1,),
            in_specs=[pl.BlockSpec(vec.shape, lambda i, dev: (0, 0)), pl.BlockSpec(land.shape, lambda i, dev: (0, 0, 0))],
            out_specs=pl.BlockSpec(vec.shape, lambda i, dev: (0, 0))),
        out_shape=jax.ShapeDtypeStruct(vec.shape, vec.dtype),
        compiler_params=_cparams(("arbitrary",)),
    )(jnp.reshape(4 * x + 2 * y + c, (1,)).astype(jnp.int32), vec, land)


def _adamw(w, g, m, v, name):
    rows, cols = w.shape
    tr = rows
    for cand in (512, 256, 128, 64, 32, 16, 8):
        if rows % cand == 0 and cand * cols * 4 <= 2 * 1024 * 1024:
            tr = cand
            break
    c1 = np.float32(1.0 - ADAM_B1 ** ADAM_STEP)
    c2 = np.float32(1.0 - ADAM_B2 ** ADAM_STEP)

    def body(w_ref, g_ref, m_ref, v_ref, go_ref, d_ref, mo_ref, vo_ref):
        gv = g_ref[...]
        go_ref[...] = gv
        mn = ADAM_B1 * m_ref[...] + (1.0 - ADAM_B1) * gv
        vn = ADAM_B2 * v_ref[...] + (1.0 - ADAM_B2) * (gv * gv)
        mo_ref[...] = mn
        vo_ref[...] = vn
        d_ref[...] = -ADAM_LR * ((mn / c1) / (jnp.sqrt(vn / c2) + ADAM_EPS) + ADAM_WD * w_ref[...])

    blk = pl.BlockSpec((tr, cols), lambda i: (i, 0))
    return pl.pallas_call(
        body, name="adamw_" + name, grid=(rows // tr,), in_specs=[blk] * 4, out_specs=[blk] * 4,
        out_shape=[jax.ShapeDtypeStruct((rows, cols), f32)] * 4,
        compiler_params=_cparams(("parallel",)),
    )(w, g, m, v)


SMALL = ("norm1", "pool_w", "pool_scale", "norm2", "norm3", "final_norm")
ORDER = ("norm1", "w_in", "pool_w", "pool_scale", "w_out", "norm2", "w_up", "w_down", "norm3", "w_gate", "w_ple",
         "final_norm")


def _pack_small(tree, extra=None):
    parts = [tree[n].reshape(-1) for n in SMALL]
    if extra is not None:
        parts.append(extra.reshape(-1))
    flat = jnp.concatenate(parts)
    return jnp.pad(flat, (0, SMALL_ROWS * 128 - flat.shape[0])).reshape(SMALL_ROWS, 128)


def _unpack_small(packed, like):
    flat = packed.reshape(-1)
    out, off = {}, 0
    for n in SMALL:
        size = int(np.prod(like[n].shape))
        out[n] = flat[off:off + size].reshape(like[n].shape)
        off += size
    return out, flat[off]


def kernel(x, p, positions, norm1, w_in, pool_w, pool_scale, w_out, norm2, w_up, w_down, norm3, w_gate, w_ple, final_norm, loss_target, m_norm1, m_w_in, m_pool_w, m_pool_scale, m_w_out, m_norm2, m_w_up, m_w_down, m_norm3, m_w_gate, m_w_ple, m_final_norm, v_norm1, v_w_in, v_pool_w, v_pool_scale, v_w_out, v_norm2, v_w_up, v_w_down, v_norm3, v_w_gate, v_w_ple, v_final_norm):
    w = dict(norm1=norm1, w_in=w_in, pool_w=pool_w, pool_scale=pool_scale, w_out=w_out, norm2=norm2, w_up=w_up,
             w_down=w_down, norm3=norm3, w_gate=w_gate, w_ple=w_ple, final_norm=final_norm)
    m = dict(norm1=m_norm1, w_in=m_w_in, pool_w=m_pool_w, pool_scale=m_pool_scale, w_out=m_w_out, norm2=m_norm2,
             w_up=m_w_up, w_down=m_w_down, norm3=m_norm3, w_gate=m_w_gate, w_ple=m_w_ple, final_norm=m_final_norm)
    v = dict(norm1=v_norm1, w_in=v_w_in, pool_w=v_pool_w, pool_scale=v_pool_scale, w_out=v_w_out, norm2=v_norm2,
             w_up=v_w_up, w_down=v_w_down, norm3=v_norm3, w_gate=v_w_gate, w_ple=v_w_ple, final_norm=v_final_norm)
    small = {n: w[n] for n in SMALL}

    wsrc = _GatheredWeights({n: w[n] for n in BIG})
    reducer = _GradReducer()
    loss8, dx, small_grads = _local_step(x[0], p.reshape(2, x.shape[1], PLE_DIM), positions[0], wsrc, small, loss_target[0], reducer)
    s_send, s_recv, s_vec, s_land = _small_start(_pack_small(small_grads, loss8[0, 0]))
    gsh = reducer.finish(s_vec)

    g_out, d_out, m_out, v_out = {}, {}, {}, {}
    for n in BIG:
        shp = w[n].shape
        two = lambda a: a.reshape(shp[0] * shp[1], shp[2])
        g2, d2, m2, v2 = _adamw(two(w[n]), two(gsh[n]), two(m[n]), two(v[n]), n)
        g_out[n], d_out[n], m_out[n], v_out[n] = g2.reshape(shp), d2.reshape(shp), m2.reshape(shp), v2.reshape(shp)
    red = _small_sum(*_small_wait(s_send, s_recv, s_vec, s_land, d2))
    g_small, loss = _unpack_small(red, small)
    _, d2, m2, v2 = _adamw(_pack_small(small), red, _pack_small({n: m[n] for n in SMALL}),
                           _pack_small({n: v[n] for n in SMALL}), "small")
    for tree, packed in ((d_out, d2), (m_out, m2), (v_out, v2)):
        tree.update(_unpack_small(packed, small)[0])
    g_out.update(g_small)

    return (loss, dx[None], *[g_out[n] for n in ORDER], *[d_out[n] for n in ORDER], *[m_out[n] for n in ORDER],
            *[v_out[n] for n in ORDER])
```

```python
import jax
import jax.numpy as jnp
import numpy as np
from jax import lax
from jax.experimental import pallas as pl
from jax.experimental.pallas import tpu as pltpu

f32 = jnp.float32
MXU_DTYPE = jnp.bfloat16
COMM_DTYPE = jnp.bfloat16

D_MODEL = 1024
POOL_WIDTH = 256
POOL_GC = 64
ATTN_WIDTH = 768
HEAD_DIM = 64
N_IN = POOL_WIDTH + 3 * ATTN_WIDTH
D_FF = 4096
PLE_DIM = 256
BLK = 128
DILATIONS = (1, 4, 16)
ROT_DIM = 16
ROPE_THETA = 500000.0
EPS = 1e-6
ATTN_SCALE = HEAD_DIM ** -0.5
NEG_BIG = -1e30

ADAM_LR, ADAM_B1, ADAM_B2, ADAM_EPS, ADAM_WD, ADAM_STEP = 0.001, 0.9, 0.999, 1e-08, 0.01, 10

TM = 512
TM_WGRAD = 1024
HALO = 16
VMEM_LIMIT = 48 * 1024 * 1024
VMEM_LIMIT_LARGE = 58 * 1024 * 1024
N_CHIPS = 4
MESH = pl.DeviceIdType.MESH

BIG = ("w_in", "w_out", "w_up", "w_down", "w_gate", "w_ple")
FULL_SHAPE = {"w_in": (D_MODEL, N_IN), "w_out": (D_MODEL, D_MODEL), "w_up": (D_MODEL, D_FF),
              "w_down": (D_FF, D_MODEL), "w_gate": (D_MODEL, D_MODEL), "w_ple": (PLE_DIM, D_MODEL)}
COL_SHARDED = {"w_in": True, "w_out": False, "w_up": True, "w_down": False, "w_gate": False, "w_ple": True}


def _shard_shape(name):
    k, n = FULL_SHAPE[name]
    return (k, n // N_CHIPS) if COL_SHARDED[name] else (k // N_CHIPS, n)


def _cparams(sem=None, vmem=VMEM_LIMIT):
    return pltpu.CompilerParams(dimension_semantics=sem, vmem_limit_bytes=vmem)


def _resident(block_shape, index_map):
    return pl.BlockSpec(block_shape, index_map, pipeline_mode=pl.Buffered(1))


def _mx(x):
    return x.astype(MXU_DTYPE)


def _dot(a, b):
    return jnp.dot(a, b, preferred_element_type=f32)


def _dot_nt(a, b):
    return lax.dot_general(a, b, (((1,), (1,)), ((), ())), preferred_element_type=f32)


def _dot_tn(a, b):
    return lax.dot_general(a, b, (((0,), (0,)), ((), ())), preferred_element_type=f32)


def _sigmoid(x):
    return 1.0 / (1.0 + jnp.exp(-x))


def _rope_apply(y, c, s1, s2, width):
    return y * c + pltpu.roll(y, width - 8, axis=1) * s1 + pltpu.roll(y, 8, axis=1) * s2


def _rope_transpose(dy, c, s1, s2, width):
    return dy * c + pltpu.roll(dy * s1, 8, axis=1) + pltpu.roll(dy * s2, width - 8, axis=1)


def _norm_matmul(h, g, w, layer, tn, name, rope=None):
    s_len, d = h.shape
    n = w.shape[2]

    def body(*refs):
        if rope is None:
            h_ref, g_ref, w_ref, y_ref, hn_ref = refs
        else:
            h_ref, g_ref, w_ref, c_ref, s1_ref, s2_ref, y_ref, hn_ref = refs
            reps = tn // 128
            c = jnp.concatenate([c_ref[...]] * reps, axis=1)
            s1 = jnp.concatenate([s1_ref[...]] * reps, axis=1)
            s2 = jnp.concatenate([s2_ref[...]] * reps, axis=1)
        x = h_ref[...]
        r = lax.rsqrt(jnp.mean(x * x, axis=-1, keepdims=True) + EPS)
        hn = ((x * r) * g_ref[...]).astype(hn_ref.dtype)
        hn_ref[...] = hn
        for j in range(n // tn):
            y = _dot(hn, w_ref[:, j * tn:(j + 1) * tn])
            if rope is not None and POOL_WIDTH <= j * tn < POOL_WIDTH + 2 * ATTN_WIDTH:
                y = _rope_apply(y, c, s1, s2, tn)
            y_ref[:, j * tn:(j + 1) * tn] = y

    in_specs = [pl.BlockSpec((TM, d), lambda i: (i, 0)),
                pl.BlockSpec((1, d), lambda i: (0, 0)),
                _resident((None, d, n), lambda i: (layer, 0, 0))]
    args = [h, g, w]
    if rope is not None:
        assert POOL_WIDTH % tn == 0 and (2 * ATTN_WIDTH) % tn == 0
        in_specs += [pl.BlockSpec((TM, 128), lambda i: (i, 0))] * 3
        args += list(rope)
    return pl.pallas_call(
        body, name=name, grid=(s_len // TM,), in_specs=in_specs,
        out_specs=[pl.BlockSpec((TM, n), lambda i: (i, 0)), pl.BlockSpec((TM, d), lambda i: (i, 0))],
        out_shape=[jax.ShapeDtypeStruct((s_len, n), f32), jax.ShapeDtypeStruct((s_len, d), MXU_DTYPE)],
        compiler_params=_cparams(("parallel",)),
    )(*args)


def _gate_ple_fwd(h2, g, w_gate, w_ple, layer, p, name, head=None):
    s_len, d = h2.shape

    def body(h_ref, g_ref, wg_ref, p_ref, wp_ref, *rest):
        gl_ref, hn_ref = rest[-2:]
        x = h_ref[...]
        r = lax.rsqrt(jnp.mean(x * x, axis=-1, keepdims=True) + EPS)
        hn = ((x * r) * g_ref[...]).astype(hn_ref.dtype)
        hn_ref[...] = hn
        gl = _dot(hn, wg_ref[...])
        gl_ref[...] = gl.astype(gl_ref.dtype)
        h3 = x + _sigmoid(gl) * _dot(_mx(p_ref[...]), wp_ref[...])
        if head is None:
            rest[0][...] = h3
            return
        gf_ref, t_ref, loss_ref, dh_ref, dgf_ref = rest[:5]
        i = pl.program_id(0)
        gv = gf_ref[...]
        r3 = lax.rsqrt(jnp.mean(h3 * h3, axis=-1, keepdims=True) + EPS)
        xh = h3 * r3
        diff = xh * gv - t_ref[...]
        part = 0.5 * jnp.sum(jnp.mean(diff * diff, axis=-1, keepdims=True), axis=0, keepdims=True)
        dy = diff * (1.0 / d)
        dxh = dy * gv
        dh_ref[...] = r3 * (dxh - xh * jnp.mean(dxh * xh, axis=-1, keepdims=True))
        dgsum = jnp.sum(dy * xh, axis=0, keepdims=True)
        lossb = jnp.broadcast_to(part, (8, 128))

        @pl.when(i == 0)
        def _():
            loss_ref[...] = lossb
            dgf_ref[...] = dgsum

        @pl.when(i > 0)
        def _():
            loss_ref[...] += lossb
            dgf_ref[...] += dgsum

    row = lambda i: (i, 0)
    one = lambda i: (0, 0)
    in_specs = [pl.BlockSpec((TM, d), row), pl.BlockSpec((1, d), one),
                pl.BlockSpec((None, d, d), lambda i: (layer, 0, 0)), pl.BlockSpec((TM, PLE_DIM), row),
                pl.BlockSpec((None, PLE_DIM, d), lambda i: (layer, 0, 0))]
    args = [h2, g, w_gate, p, w_ple]
    saved = [jax.ShapeDtypeStruct((s_len, d), MXU_DTYPE)] * 2
    if head is None:
        out_specs = [pl.BlockSpec((TM, d), row)] * 3
        out_shape = [jax.ShapeDtypeStruct((s_len, d), f32)] + saved
    else:
        in_specs += [pl.BlockSpec((1, d), one), pl.BlockSpec((TM, d), row)]
        args += list(head)
        out_specs = [pl.BlockSpec((8, 128), one), pl.BlockSpec((TM, d), row), pl.BlockSpec((1, d), one)] \
            + [pl.BlockSpec((TM, d), row)] * 2
        out_shape = [jax.ShapeDtypeStruct((8, 128), f32), jax.ShapeDtypeStruct((s_len, d), f32),
                     jax.ShapeDtypeStruct((1, d), f32)] + saved
    return pl.pallas_call(
        body, name=name, grid=(s_len // TM,), in_specs=in_specs, out_specs=out_specs, out_shape=out_shape,
        compiler_params=_cparams(("arbitrary",)),
    )(*args)


def _gate_bwd(dh3, gl, p, w_ple, w_gate, layer, h2, g, name):
    s_len, d = dh3.shape

    def body(dh_ref, gl_ref, p_ref, wp_ref, wg_ref, h_ref, g_ref, dh2_ref, dg_ref, de_ref, dgl_ref):
        i = pl.program_id(0)
        dh = dh_ref[...]
        gate = _sigmoid(gl_ref[...].astype(f32))
        e = _dot(_mx(p_ref[...]), wp_ref[...])
        de_ref[...] = (dh * gate).astype(de_ref.dtype)
        dgl = ((dh * e) * (gate * (1.0 - gate))).astype(dgl_ref.dtype)
        dgl_ref[...] = dgl
        dx, dgrow = _rmsnorm_bwd(_dot_nt(dgl, wg_ref[...]), h_ref[...], g_ref[...])
        dh2_ref[...] = dh + dx
        dgsum = jnp.sum(dgrow, axis=0, keepdims=True)

        @pl.when(i == 0)
        def _():
            dg_ref[...] = dgsum

        @pl.when(i > 0)
        def _():
            dg_ref[...] += dgsum

    row = lambda i: (i, 0)
    blk = pl.BlockSpec((TM, d), row)
    return pl.pallas_call(
        body, name=name, grid=(s_len // TM,),
        in_specs=[blk, blk, pl.BlockSpec((TM, PLE_DIM), row), _resident((None, PLE_DIM, d), lambda i: (layer, 0, 0)),
                  _resident((None, d, d), lambda i: (layer, 0, 0)), blk, pl.BlockSpec((1, d), lambda i: (0, 0))],
        out_specs=[blk, pl.BlockSpec((1, d), lambda i: (0, 0)), blk, blk],
        out_shape=[jax.ShapeDtypeStruct((s_len, d), f32), jax.ShapeDtypeStruct((1, d), f32),
                   jax.ShapeDtypeStruct((s_len, d), MXU_DTYPE), jax.ShapeDtypeStruct((s_len, d), MXU_DTYPE)],
        compiler_params=_cparams(("arbitrary",)),
    )(dh3, gl, p, w_ple, w_gate, h2, g)


def _rmsnorm_bwd(dhn, x, g):
    r = lax.rsqrt(jnp.mean(x * x, axis=-1, keepdims=True) + EPS)
    xh = x * r
    dxh = dhn * g
    dx = r * (dxh - xh * jnp.mean(dxh * xh, axis=-1, keepdims=True))
    return dx, dhn * xh


def _matmul_nt_norm_bwd(dy, w, layer, h_prev, g, dres, name, tk=1024, after=None):
    s_len, k_dim = dy.shape
    d = h_prev.shape[1]

    def body(dy_ref, w_ref, h_ref, g_ref, dres_ref, *rest):
        dh_ref, dg_ref = rest[-2:]
        i = pl.program_id(0)
        acc = None
        for k in range(k_dim // tk):
            part = _dot_nt(_mx(dy_ref[:, k * tk:(k + 1) * tk]), w_ref[:, k * tk:(k + 1) * tk])
            acc = part if acc is None else acc + part
        dx, dgrow = _rmsnorm_bwd(acc, h_ref[...], g_ref[...])
        dh_ref[...] = dres_ref[...] + dx
        dgsum = jnp.sum(dgrow, axis=0, keepdims=True)

        @pl.when(i == 0)
        def _():
            dg_ref[...] = dgsum

        @pl.when(i > 0)
        def _():
            dg_ref[...] += dgsum

    in_specs = [pl.BlockSpec((TM, k_dim), lambda i: (i, 0)),
                _resident((None, d, k_dim), lambda i: (layer, 0, 0)),
                pl.BlockSpec((TM, d), lambda i: (i, 0)),
                pl.BlockSpec((1, d), lambda i: (0, 0)),
                pl.BlockSpec((TM, d), lambda i: (i, 0))]
    args = [dy, w, h_prev, g, dres]
    if after is not None:
        in_specs.append(pl.BlockSpec(memory_space=pl.ANY))
        args.append(after)
    return pl.pallas_call(
        body, name=name, grid=(s_len // TM,), in_specs=in_specs,
        out_specs=[pl.BlockSpec((TM, d), lambda i: (i, 0)), pl.BlockSpec((1, d), lambda i: (0, 0))],
        out_shape=[jax.ShapeDtypeStruct((s_len, d), f32), jax.ShapeDtypeStruct((1, d), f32)],
        compiler_params=_cparams(("arbitrary",)),
    )(*args)


def _mlp_fwd(h1, g, w_up, w_down, layer, name, tf=1024):
    s_len, d = h1.shape
    ff = w_up.shape[2]

    def body(h_ref, g_ref, wu_ref, wd_ref, h2_ref, a_ref, hn_ref):
        x = h_ref[...]
        r = lax.rsqrt(jnp.mean(x * x, axis=-1, keepdims=True) + EPS)
        hn = ((x * r) * g_ref[...]).astype(hn_ref.dtype)
        hn_ref[...] = hn
        acc = x
        for j in range(ff // tf):
            a = _dot(hn, wu_ref[:, j * tf:(j + 1) * tf])
            a_ref[:, j * tf:(j + 1) * tf] = a.astype(a_ref.dtype)
            relu = jnp.maximum(a, 0.0)
            acc = acc + _dot(_mx(relu * relu), wd_ref[j * tf:(j + 1) * tf, :])
        h2_ref[...] = acc

    row = lambda i: (i, 0)
    return pl.pallas_call(
        body, name=name, grid=(s_len // TM,),
        in_specs=[pl.BlockSpec((TM, d), row), pl.BlockSpec((1, d), lambda i: (0, 0)),
                  _resident((None, d, ff), lambda i: (layer, 0, 0)), _resident((None, ff, d), lambda i: (layer, 0, 0))],
        out_specs=[pl.BlockSpec((TM, d), row), pl.BlockSpec((TM, ff), row), pl.BlockSpec((TM, d), row)],
        out_shape=[jax.ShapeDtypeStruct((s_len, d), f32), jax.ShapeDtypeStruct((s_len, ff), MXU_DTYPE),
                   jax.ShapeDtypeStruct((s_len, d), MXU_DTYPE)],
        compiler_params=_cparams(("parallel",)),
    )(h1, g, w_up, w_down)


def _mlp_bwd(dh2, w_down, w_up, layer, a, h1, g, name, tf=1024):
    s_len, d = dh2.shape
    ff = a.shape[1]

    def body(dh_ref, wd_ref, wu_ref, a_ref, h_ref, g_ref, dh1_ref, dg_ref, da_ref):
        i = pl.program_id(0)
        dh = dh_ref[...]
        dhb = _mx(dh)
        acc = None
        for j in range(ff // tf):
            cols = slice(j * tf, (j + 1) * tf)
            dact = _dot_nt(dhb, wd_ref[cols, :])
            da = (dact * (2.0 * jnp.maximum(a_ref[:, cols].astype(f32), 0.0))).astype(da_ref.dtype)
            da_ref[:, cols] = da
            part = _dot_nt(da, wu_ref[:, cols])
            acc = part if acc is None else acc + part
        dx, dgrow = _rmsnorm_bwd(acc, h_ref[...], g_ref[...])
        dh1_ref[...] = dh + dx
        dgsum = jnp.sum(dgrow, axis=0, keepdims=True)

        @pl.when(i == 0)
        def _():
            dg_ref[...] = dgsum

        @pl.when(i > 0)
        def _():
            dg_ref[...] += dgsum

    row = lambda i: (i, 0)
    return pl.pallas_call(
        body, name=name, grid=(s_len // TM,),
        in_specs=[pl.BlockSpec((TM, d), row), _resident((None, ff, d), lambda i: (layer, 0, 0)),
                  _resident((None, d, ff), lambda i: (layer, 0, 0)), pl.BlockSpec((TM, ff), row),
                  pl.BlockSpec((TM, d), row), pl.BlockSpec((1, d), lambda i: (0, 0))],
        out_specs=[pl.BlockSpec((TM, d), row), pl.BlockSpec((1, d), lambda i: (0, 0)), pl.BlockSpec((TM, ff), row)],
        out_shape=[jax.ShapeDtypeStruct((s_len, d), f32), jax.ShapeDtypeStruct((1, d), f32),
                   jax.ShapeDtypeStruct((s_len, ff), MXU_DTYPE)],
        compiler_params=_cparams(("arbitrary",), vmem=VMEM_LIMIT_LARGE),
    )(dh2, w_down, w_up, a, h1, g)


def _weight_grad(a, b, name, act=False):
    s_len, k_dim = a.shape
    n = b.shape[1]
    tka = min(k_dim, 2048)
    tnb = n if n <= 1024 else (2048 if n % 2048 == 0 else 640)
    ns = s_len // TM_WGRAD

    def body(a_ref, b_ref, o_ref, acc_ref):
        s = pl.program_id(2)
        x = a_ref[...]
        if act:
            relu = jnp.maximum(x, 0.0)
            x = relu * relu
        @pl.when(s == 0)
        def _():
            acc_ref[...] = jnp.zeros_like(acc_ref)

        acc_ref[...] += _dot_tn(_mx(x), _mx(b_ref[...]))

        @pl.when(s == ns - 1)
        def _():
            o_ref[...] = acc_ref[...].astype(o_ref.dtype)

    return pl.pallas_call(
        body, name=name, grid=(k_dim // tka, n // tnb, ns),
        in_specs=[pl.BlockSpec((TM_WGRAD, tka), lambda i, j, s: (s, i)),
                  pl.BlockSpec((TM_WGRAD, tnb), lambda i, j, s: (s, j))],
        out_specs=pl.BlockSpec((tka, tnb), lambda i, j, s: (i, j)),
        out_shape=jax.ShapeDtypeStruct((k_dim, n), COMM_DTYPE),
        scratch_shapes=[pltpu.VMEM((tka, tnb), f32)],
        compiler_params=_cparams(("parallel", "parallel", "arbitrary")),
    )(a, b)


def _group_select(lane, x2, x4, x8, x16):
    grp = lane // POOL_GC
    return jnp.where(grp == 0, x2, jnp.where(grp == 1, x4, jnp.where(grp == 2, x8, x16)))


def _pool_window(lane):
    grp = lane // POOL_GC
    return jnp.where(grp == 0, 2, jnp.where(grp == 1, 4, jnp.where(grp == 2, 8, 16)))


def _pool_y(u, halo, i):
    xs = jnp.concatenate([jnp.where(i > 0, halo, 0.0), u], axis=0)
    s2 = xs + pltpu.roll(xs, 1, axis=0)
    s4 = s2 + pltpu.roll(s2, 2, axis=0)
    s8 = s4 + pltpu.roll(s4, 4, axis=0)
    s16 = s8 + pltpu.roll(s8, 8, axis=0)
    lane = lax.broadcasted_iota(jnp.int32, xs.shape, 1)
    sel = _group_select(lane, s2, s4, s8, s16)[HALO:, :]
    t = i * TM + lax.broadcasted_iota(jnp.int32, u.shape, 0)
    cnt = jnp.minimum(_pool_window(lax.broadcasted_iota(jnp.int32, u.shape, 1)), t + 1).astype(f32)
    return sel / cnt - u


def _group_weights(l0, l1, l2):
    mx = jnp.maximum(jnp.maximum(l0, l1), l2)
    e0, e1, e2 = jnp.exp(l0 - mx), jnp.exp(l1 - mx), jnp.exp(l2 - mx)
    den = e0 + e1 + e2
    return e0 / den, e1 / den, e2 / den


def _mixer_out_proj(z, wbd, scale, outs, lses, w_out, layer, h, name):
    s_len, d = h.shape

    def body(u_ref, halo_ref, wbd_ref, sc_ref, o0, o1, o2, l0, l1, l2, wo_ref, h_ref, m_ref, h1_ref):
        i = pl.program_id(0)
        y = _pool_y(u_ref[...], halo_ref[...], i)
        pool = _dot(_mx(y), wbd_ref[...]) * sc_ref[...]
        w0, w1, w2 = _group_weights(l0[...], l1[...], l2[...])
        m = jnp.concatenate([pool, o0[...] * w0, o1[...] * w1, o2[...] * w2], axis=1).astype(m_ref.dtype)
        m_ref[...] = m
        h1_ref[...] = h_ref[...] + _dot(m, wo_ref[...])

    row = lambda i: (i, 0)
    blk = pl.BlockSpec((TM, 256), row)
    grp = [pl.BlockSpec((TM, 256), lambda i, g=g: (i, g)) for g in range(3)]
    return pl.pallas_call(
        body, name=name, grid=(s_len // TM,),
        in_specs=[blk, pl.BlockSpec((HALO, 256), lambda i: (jnp.maximum(i * (TM // HALO) - 1, 0), 0)),
                  pl.BlockSpec((256, 256), lambda i: (0, 0)), pl.BlockSpec((1, 256), lambda i: (0, 0))] + grp + grp
        + [_resident((None, d, d), lambda i: (layer, 0, 0)), pl.BlockSpec((TM, d), row)],
        out_specs=[pl.BlockSpec((TM, d), row)] * 2,
        out_shape=[jax.ShapeDtypeStruct((s_len, d), MXU_DTYPE), jax.ShapeDtypeStruct((s_len, d), f32)],
        compiler_params=_cparams(("parallel",)),
    )(z, z, wbd, scale, outs, outs, outs, lses, lses, lses, w_out, h)


def _head_sums(x):
    r = lax.broadcasted_iota(jnp.int32, (256, 256), 0) // HEAD_DIM
    c = lax.broadcasted_iota(jnp.int32, (256, 256), 1) // HEAD_DIM
    ones = jnp.where(r == c, 1.0, 0.0).astype(jnp.bfloat16)
    hi = x.astype(jnp.bfloat16)
    lo = (x - hi.astype(f32)).astype(jnp.bfloat16)
    return _dot(hi, ones) + _dot(lo, ones)


def _out_combine_bwd(dh1, w_out, layer, outs, lses, name, after=None):
    s_len, d = dh1.shape

    def body(dh_ref, w_ref, o0, o1, o2, l0, l1, l2, *rest):
        dp_ref, do_ref, dl_ref = rest[-3:]
        dm = _dot_nt(_mx(dh_ref[...]), w_ref[...])
        dp_ref[...] = dm[:, :POOL_WIDTH]
        w = _group_weights(l0[...], l1[...], l2[...])
        da = [dm[:, POOL_WIDTH + 256 * g:POOL_WIDTH + 256 * (g + 1)] for g in range(3)]
        o = (o0[...], o1[...], o2[...])
        dw = [_head_sums(da[g] * o[g]) for g in range(3)]
        t = w[0] * dw[0] + w[1] * dw[1] + w[2] * dw[2]
        do_ref[...] = jnp.concatenate([da[g] * w[g] for g in range(3)], axis=1)
        dl_ref[...] = jnp.concatenate([w[g] * t for g in range(3)], axis=1)

    grp = [pl.BlockSpec((TM, 256), lambda i, g=g: (i, g)) for g in range(3)]
    in_specs = [pl.BlockSpec((TM, d), lambda i: (i, 0)), _resident((None, d, d), lambda i: (layer, 0, 0))] + grp + grp
    args = [dh1, w_out, outs, outs, outs, lses, lses, lses]
    if after is not None:
        in_specs.append(pl.BlockSpec(memory_space=pl.ANY))
        args.append(after)
    return pl.pallas_call(
        body, name=name, grid=(s_len // TM,), in_specs=in_specs,
        out_specs=[pl.BlockSpec((TM, POOL_WIDTH), lambda i: (i, 0))] + [pl.BlockSpec((TM, ATTN_WIDTH), lambda i: (i, 0))] * 2,
        out_shape=[jax.ShapeDtypeStruct((s_len, POOL_WIDTH), f32)] + [jax.ShapeDtypeStruct((s_len, ATTN_WIDTH), f32)] * 2,
        compiler_params=_cparams(("parallel",)),
    )(*args)


def _pool_bwd(z, dm, wbd, scale, name, after=None):
    s_len = z.shape[0]
    n_halo = s_len // HALO

    def body(u_ref, uh_ref, d_ref, dh_ref, wbd_ref, sc_ref, *rest):
        du_ref, dw_ref, dsc_ref = rest[-3:]
        i = pl.program_id(0)
        last = pl.num_programs(0) - 1
        y = _pool_y(u_ref[...], uh_ref[...], i)
        yb = _mx(y)
        dpo = d_ref[...]
        sc = sc_ref[...]
        dsc = jnp.sum(dpo * _dot(yb, wbd_ref[...]), axis=0, keepdims=True)
        dwp = _dot_tn(yb, _mx(dpo * sc))

        @pl.when(i == 0)
        def _():
            dsc_ref[...] = dsc
            dw_ref[...] = dwp

        @pl.when(i > 0)
        def _():
            dsc_ref[...] += dsc
            dw_ref[...] += dwp

        ext = jnp.concatenate([dpo, jnp.where(i < last, dh_ref[...], 0.0)], axis=0)
        dy = _dot_nt(_mx(ext * sc), wbd_ref[...])
        t = i * TM + lax.broadcasted_iota(jnp.int32, ext.shape, 0)
        lane = lax.broadcasted_iota(jnp.int32, ext.shape, 1)
        e = dy / jnp.minimum(_pool_window(lane), t + 1).astype(f32)
        rows = ext.shape[0]
        f2 = e + pltpu.roll(e, rows - 1, axis=0)
        f4 = f2 + pltpu.roll(f2, rows - 2, axis=0)
        f8 = f4 + pltpu.roll(f4, rows - 4, axis=0)
        f16 = f8 + pltpu.roll(f8, rows - 8, axis=0)
        du_ref[...] = (_group_select(lane, f2, f4, f8, f16) - dy)[:TM, :].astype(du_ref.dtype)

    row = lambda i: (i, 0)
    blk = pl.BlockSpec((TM, 256), row)
    extra = [] if after is None else [after]
    return pl.pallas_call(
        body, name=name, grid=(s_len // TM,),
        in_specs=[blk, pl.BlockSpec((HALO, 256), lambda i: (jnp.maximum(i * (TM // HALO) - 1, 0), 0)),
                  blk, pl.BlockSpec((HALO, 256), lambda i: (jnp.minimum((i + 1) * (TM // HALO), n_halo - 1), 0)),
                  pl.BlockSpec((256, 256), lambda i: (0, 0)), pl.BlockSpec((1, 256), lambda i: (0, 0))]
        + [pl.BlockSpec(memory_space=pl.ANY)] * len(extra),
        out_specs=[blk, pl.BlockSpec((256, 256), lambda i: (0, 0)), pl.BlockSpec((1, 256), lambda i: (0, 0))],
        out_shape=[jax.ShapeDtypeStruct((s_len, N_IN), MXU_DTYPE), jax.ShapeDtypeStruct((256, 256), f32),
                   jax.ShapeDtypeStruct((1, 256), f32)],
        compiler_params=_cparams(("arbitrary",)),
    )(z, z, dm, dm, wbd, scale, *extra)


def _tri_masks():
    qi = lax.broadcasted_iota(jnp.int32, (BLK, BLK), 0)
    ki = lax.broadcasted_iota(jnp.int32, (BLK, BLK), 1)
    return qi >= ki, ki >= qi


ATTN_SUPER_PER_STEP = (8, 4, 1)
Q_COL, K_COL, V_COL = POOL_WIDTH // 128, (POOL_WIDTH + ATTN_WIDTH) // 128, (POOL_WIDTH + 2 * ATTN_WIDTH) // 128


def _rows(ref, start, dil):
    if dil == 1:
        return ref[pl.ds(start, BLK), :]
    return ref[pl.ds(start, BLK, stride=dil), :]


ATTN_BLOCKS_TOGETHER = 8


def _set_rows(ref, start, dil, val):
    if dil == 1:
        ref[pl.ds(start, BLK), :] = val
    else:
        ref[pl.ds(start, BLK, stride=dil), :] = val


def _attn_fwd(z, g, prev, name):
    s_len = z.shape[0]
    dil, m = DILATIONS[g], ATTN_SUPER_PER_STEP[g]
    sbr = BLK * dil
    rows = sbr * m

    def body(*refs):
        q_ref, kc_ref, kp_ref, vc_ref, vp_ref = refs[:5]
        o_ref, l_ref = refs[-2:]
        st = pl.program_id(0)
        low, up = _tri_masks()
        head0 = lax.broadcasted_iota(jnp.int32, (BLK, 128), 1) < HEAD_DIM
        blocks = [(sb, r) for sb in range(m) for r in range(dil)]
        for g0 in range(0, len(blocks), ATTN_BLOCKS_TOGETHER):
            grp = blocks[g0:g0 + ATTN_BLOCKS_TOGETHER]
            loaded = []
            for sb, r in grp:
                base = sb * sbr + r
                if sb == 0:
                    kp, vp = _rows(kp_ref, r, dil), _rows(vp_ref, r, dil)
                else:
                    kp, vp = _rows(kc_ref, base - sbr, dil), _rows(vc_ref, base - sbr, dil)
                qs = _rows(q_ref, base, dil) * ATTN_SCALE
                loaded.append((_mx(jnp.where(head0, qs, 0.0)), _mx(jnp.where(head0, 0.0, qs)),
                               jnp.concatenate([_mx(kp), _mx(_rows(kc_ref, base, dil))], axis=0),
                               jnp.concatenate([_mx(vp), _mx(_rows(vc_ref, base, dil))], axis=0)))
            scores = [(_dot_nt(q0, k2), _dot_nt(q1, k2)) for q0, q1, k2, _ in loaded]
            soft = []
            for (sb, _), pair in zip(grp, scores):
                valid = jnp.concatenate([up & (st > 0) if sb == 0 else up, low], axis=1)
                heads = []
                for s in pair:
                    s = jnp.where(valid, s, NEG_BIG)
                    mx = jnp.max(s, axis=-1, keepdims=True)
                    e = jnp.exp(s - mx)
                    l = jnp.sum(e, axis=-1, keepdims=True)
                    heads.append((_mx(e / l), jnp.broadcast_to(mx + jnp.log(l), (BLK, 128))))
                soft.append(heads)
            for (sb, r), heads, (_, _, _, v2) in zip(grp, soft, loaded):
                base = sb * sbr + r
                _set_rows(o_ref, base, dil, jnp.where(head0, _dot(heads[0][0], v2), _dot(heads[1][0], v2)))
                _set_rows(l_ref, base, dil, jnp.where(head0, heads[0][1], heads[1][1]))

    def cur(col):
        return pl.BlockSpec((rows, 128), lambda st, hp: (st, col + 2 * g + hp))

    def before(col):
        return pl.BlockSpec((sbr, 128), lambda st, hp: (jnp.maximum(st * m - 1, 0), col + 2 * g + hp))

    in_specs = [cur(Q_COL), cur(K_COL), before(K_COL), cur(V_COL), before(V_COL)]
    args = [z, z, z, z, z]
    aliases = {}
    if prev is not None:
        in_specs += [pl.BlockSpec(memory_space=pl.ANY)] * 2
        args += list(prev)
        aliases = {5: 0, 6: 1}
    return pl.pallas_call(
        body, name=name, grid=(s_len // rows, 2), in_specs=in_specs, out_specs=[cur(0), cur(0)],
        out_shape=[jax.ShapeDtypeStruct((s_len, ATTN_WIDTH), f32)] * 2, input_output_aliases=aliases,
        compiler_params=_cparams(("parallel", "parallel")),
    )(*args)


def _stack_heads(x, head0):
    return jnp.concatenate([_mx(jnp.where(head0, x, 0.0)), _mx(jnp.where(head0, 0.0, x))], axis=0)


def _head_rows(x):
    xt = x.T
    return jnp.concatenate([jnp.broadcast_to(xt[0:1, :], (BLK, BLK)),
                            jnp.broadcast_to(xt[HEAD_DIM:HEAD_DIM + 1, :], (BLK, BLK))], axis=0)


def _attn_bwd(z, do, lse, dlt, tabs, dz, g, name):
    s_len = z.shape[0]
    dil, m = DILATIONS[g], ATTN_SUPER_PER_STEP[g]
    sbr = BLK * dil
    rows = sbr * m
    nsteps = s_len // rows

    def body(q_ref, qn_ref, kc_ref, kp_ref, vc_ref, vp_ref, do_ref, don_ref, l_ref, ln_ref, d_ref, dn_ref,
             c_ref, s1_ref, s2_ref, dz_in, dz_ref, dq_buf, dk_buf, dv_buf, out_buf, sems):
        del dz_in
        st, hp = pl.program_id(0), pl.program_id(1)
        head0 = lax.broadcasted_iota(jnp.int32, (BLK, 128), 1) < HEAD_DIM
        key_i = lax.broadcasted_iota(jnp.int32, (2 * BLK, BLK), 0) & (BLK - 1)
        query_i = lax.broadcasted_iota(jnp.int32, (2 * BLK, BLK), 1)
        same_t, cross_t = query_i >= key_i, key_i >= query_i
        def load(r):
            keys, vals = [_stack_heads(_rows(kp_ref, r, dil), head0)], [_stack_heads(_rows(vp_ref, r, dil), head0)]
            qs, dos, lses, dlts = [], [], [], []
            for sb in range(m):
                base = sb * sbr + r
                keys.append(_stack_heads(_rows(kc_ref, base, dil), head0))
                vals.append(_stack_heads(_rows(vc_ref, base, dil), head0))
                qs.append(_mx(_rows(q_ref, base, dil)))
                dos.append(_mx(_rows(do_ref, base, dil)))
                lses.append(_head_rows(_rows(l_ref, base, dil)))
                dlts.append(_head_rows(_rows(d_ref, base, dil)))
            qs.append(_mx(_rows(qn_ref, r, dil)))
            dos.append(_mx(_rows(don_ref, r, dil)))
            lses.append(_head_rows(_rows(ln_ref, r, dil)))
            dlts.append(_head_rows(_rows(dn_ref, r, dil)))
            return keys, vals, qs, dos, lses, dlts

        def products(data):
            keys, vals, qs, dos, _, _ = data
            return ([(_dot_nt(keys[j + 1], qs[j]), _dot_nt(vals[j + 1], dos[j])) for j in range(m)],
                    [(_dot_nt(keys[j], qs[j]), _dot_nt(vals[j], dos[j])) for j in range(m + 1)])

        def finish(data, raw):
            lses, dlts = data[4], data[5]

            def one(pair, j, valid):
                p = jnp.where(valid, jnp.exp(pair[0] * ATTN_SCALE - lses[j]), 0.0)
                return _mx(p), _mx(p * (pair[1] - dlts[j]) * ATTN_SCALE)

            same = [one(raw[0][j], j, same_t) for j in range(m)]
            cross = [one(raw[1][j], j, cross_t & (st > 0) if j == 0 else
                         (cross_t & (st < nsteps - 1) if j == m else cross_t)) for j in range(m + 1)]
            return same, cross

        def gradients(r, data, fin):
            keys, _, qs, dos, _, _ = data
            same, cross = fin
            for sb in range(m):
                base = sb * sbr + r
                (p_a, ds_a), (_, ds_x), (p_n, ds_n) = same[sb], cross[sb], cross[sb + 1]
                dq = _dot_tn(ds_a, keys[sb + 1]) + _dot_tn(ds_x, keys[sb])
                dk2 = _dot(ds_a, qs[sb]) + _dot(ds_n, qs[sb + 1])
                dv2 = _dot(p_a, dos[sb]) + _dot(p_n, dos[sb + 1])
                _set_rows(dq_buf, base, dil, dq)
                _set_rows(dk_buf, base, dil, jnp.where(head0, dk2[:BLK], dk2[BLK:]))
                _set_rows(dv_buf, base, dil, jnp.where(head0, dv2[:BLK], dv2[BLK:]))

        def residue_group(rg, carry):
            rs = [rg * group + i for i in range(group)]
            data = [load(r) for r in rs]
            raws = [products(d) for d in data]
            fins = [finish(d, raw) for d, raw in zip(data, raws)]
            for r, d, fin in zip(rs, data, fins):
                gradients(r, d, fin)
            return carry

        group = max(1, min(dil, ATTN_BLOCKS_TOGETHER // m))
        if dil // group <= 2:
            for rg in range(dil // group):
                residue_group(rg, 0)
        else:
            lax.fori_loop(0, dil // group, residue_group, 0)
        copies = []
        for t, (buf, col) in enumerate(((dq_buf, Q_COL), (dk_buf, K_COL), (dv_buf, V_COL))):
            val = buf[...]
            if t < 2:
                val = _rope_transpose(val, c_ref[...], s1_ref[...], s2_ref[...], 128)
            out_buf[t] = val.astype(out_buf.dtype)
            lane0 = pl.multiple_of((col + 2 * g + hp) * 128, 128)
            dst = dz_ref.at[pl.ds(pl.multiple_of(st * rows, rows), rows), pl.ds(lane0, 128)]
            cp = pltpu.make_async_copy(out_buf.at[t], dst, sems.at[t])
            cp.start()
            copies.append(cp)
        for cp in copies:
            cp.wait()

    def cur(col):
        return pl.BlockSpec((rows, 128), lambda st, hp: (st, col + 2 * g + hp))

    def before(col):
        return pl.BlockSpec((sbr, 128), lambda st, hp: (jnp.maximum(st * m - 1, 0), col + 2 * g + hp))

    def after(col):
        return pl.BlockSpec((sbr, 128), lambda st, hp: (jnp.minimum((st + 1) * m, s_len // sbr - 1), col + 2 * g + hp))

    tab = pl.BlockSpec((rows, 128), lambda st, hp: (st, 0))
    return pl.pallas_call(
        body, name=name, grid=(nsteps, 2),
        in_specs=[cur(Q_COL), after(Q_COL), cur(K_COL), before(K_COL), cur(V_COL), before(V_COL),
                  cur(0), after(0), cur(0), after(0), cur(0), after(0), tab, tab, tab,
                  pl.BlockSpec(memory_space=pl.ANY)],
        out_specs=pl.BlockSpec(memory_space=pl.ANY),
        out_shape=jax.ShapeDtypeStruct(dz.shape, dz.dtype), input_output_aliases={15: 0},
        scratch_shapes=[pltpu.VMEM((rows, 128), f32)] * 3 + [pltpu.VMEM((3, rows, 128), dz.dtype),
                                                            pltpu.SemaphoreType.DMA((3,))],
        compiler_params=_cparams(("arbitrary", "arbitrary")),
    )(z, z, z, z, z, z, do, do, lse, lse, dlt, dlt, *tabs, dz)


def _rope_tables(positions):
    inv_freq = ROPE_THETA ** (-jnp.arange(0, ROT_DIM, 2, dtype=f32) / ROT_DIM)
    ang = positions.astype(f32)[:, None] * inv_freq
    cos, sin = jnp.cos(ang), jnp.sin(ang)
    s_len = positions.shape[0]
    zero8, rest = jnp.zeros((s_len, 8), f32), jnp.zeros((s_len, HEAD_DIM - ROT_DIM), f32)
    c = jnp.concatenate([cos, cos, jnp.ones((s_len, HEAD_DIM - ROT_DIM), f32)], axis=1)
    s1 = jnp.concatenate([-sin, zero8, rest], axis=1)
    s2 = jnp.concatenate([zero8, sin, rest], axis=1)
    return c, s1, s2


def _block_diag(pool_w):
    out = jnp.zeros((POOL_WIDTH, POOL_WIDTH), pool_w.dtype)
    for g in range(4):
        out = lax.dynamic_update_slice(out, pool_w[g], (g * POOL_GC, g * POOL_GC))
    return out


def _layer_fwd(h, p_l, wsrc, small, layer, tabs, head=None):
    nm = f"l{layer}_"
    wts, wl = wsrc.take(layer, ("w_in",), (h,) if layer else tuple(tabs))
    z, hn1 = _norm_matmul(h, small["norm1"][layer][None], wts["w_in"], wl, 256, nm + "in_proj", rope=tabs)
    ol = None
    for g in range(3):
        ol = _attn_fwd(z, g, ol, nm + f"attn_fwd{g}")
    outs, lses = ol
    wbd = _mx(_block_diag(small["pool_w"][layer]))
    scale = small["pool_scale"][layer][None]
    wts.update(wsrc.take(layer, ("w_out",), (outs,))[0])
    m, h1 = _mixer_out_proj(z, wbd, scale, outs, lses, wts["w_out"], wl, h, nm + "mixer_out")
    wts.update(wsrc.take(layer, ("w_up", "w_down"), (h1,))[0])
    h2, a, hn2 = _mlp_fwd(h1, small["norm2"][layer][None], wts["w_up"], wts["w_down"], wl, nm + "mlp")
    wts.update(wsrc.take(layer, ("w_gate", "w_ple"), (h2,))[0])
    *h3, gl, hn3 = _gate_ple_fwd(h2, small["norm3"][layer][None], wts["w_gate"], wts["w_ple"], wl, p_l,
                                 nm + "gate_ple", head=head)
    saved = dict(h=h, z=z, hn1=hn1, outs=outs, lses=lses, wbd=wbd, scale=scale, m=m, h1=h1, a=a, hn2=hn2, h2=h2,
                 gl=gl, hn3=hn3, wts=wts, wl=wl)
    return h3, saved


def _layer_bwd(dh3, sv, p_l, small, layer, tabs128, reducer):
    nm = f"l{layer}_"
    wts, wl = sv["wts"], sv["wl"]
    dh2, dg3, de, dgl = _gate_bwd(dh3, sv["gl"], p_l, wts["w_ple"], wts["w_gate"], wl, sv["h2"],
                                  small["norm3"][layer][None], nm + "gate_bwd")
    reducer.add("w_gate", layer, _weight_grad(sv["hn3"], dgl, nm + "dw_gate"))
    reducer.add("w_ple", layer, _weight_grad(p_l, de, nm + "dw_ple"))
    dh1, dg2, da = _mlp_bwd(dh2, wts["w_down"], wts["w_up"], wl, sv["a"], sv["h1"], small["norm2"][layer][None],
                            nm + "mlp_bwd")
    reducer.add("w_down", layer, _weight_grad(sv["a"], dh2, nm + "dw_down", act=True))
    started = reducer.add("w_up", layer, _weight_grad(sv["hn2"], da, nm + "dw_up"))
    dpool, do, dlt = _out_combine_bwd(dh1, wts["w_out"], wl, sv["outs"], sv["lses"], nm + "out_bwd", after=started)
    started = reducer.add("w_out", layer, _weight_grad(sv["m"], dh1, nm + "dw_out"))
    dz, dwbd, dscale = _pool_bwd(sv["z"], dpool, sv["wbd"], sv["scale"], nm + "pool_bwd", after=started)
    for g in range(3):
        dz = _attn_bwd(sv["z"], do, sv["lses"], dlt, tabs128, dz, g, nm + f"attn_bwd{g}")
    started = reducer.add("w_in", layer, _weight_grad(sv["hn1"], dz, nm + "dw_in"))
    dh0, dg1 = _matmul_nt_norm_bwd(dz, wts["w_in"], wl, sv["h"], small["norm1"][layer][None], dh1, nm + "in_bwd",
                                   tk=512, after=started)
    dpool_w = jnp.stack([dwbd[g * POOL_GC:(g + 1) * POOL_GC, g * POOL_GC:(g + 1) * POOL_GC] for g in range(4)])
    sg = dict(norm1=dg1[0], norm2=dg2[0], norm3=dg3[0], pool_w=dpool_w, pool_scale=dscale[0])
    return dh0, sg


def _local_step(x, p, positions, wsrc, small, target, reducer):
    tabs128 = tuple(jnp.tile(t, (1, 2)) for t in _rope_tables(positions))
    (h,), sv0 = _layer_fwd(x, p[0], wsrc, small, 0, tabs128)
    (loss, dh, dgf), sv1 = _layer_fwd(h, p[1], wsrc, small, 1, tabs128, head=(small["final_norm"][None], target))
    saved = [sv0, sv1]
    sgs = [None, None]
    for layer in (1, 0):
        dh, sgs[layer] = _layer_bwd(dh, saved[layer], p[layer], small, layer, tabs128, reducer)
    small_grads = {k: jnp.stack([sgs[0][k], sgs[1][k]]) for k in sgs[0]}
    small_grads["final_norm"] = dgf[0]
    return loss, dh, small_grads


HBM = pl.BlockSpec(memory_space=pltpu.HBM)


def _my_place():
    return lax.axis_index("x"), lax.axis_index("y"), lax.axis_index("c")


def _other_chips(x, y):
    return [(1 - x, y), (x, 1 - y), (1 - x, 1 - y)]


def _window(ref, name, chip):
    k, n = _shard_shape(name)
    if COL_SHARDED[name]:
        return ref.at[:, pl.ds(pl.multiple_of(chip * n, 128), n)]
    return ref.at[pl.ds(pl.multiple_of(chip * k, 128), k), :]


def _chip_index():
    return jnp.reshape(2 * lax.axis_index("x") + lax.axis_index("y"), (1,)).astype(jnp.int32)


def _shard_block(name, tr):
    ks, ns = _shard_shape(name)
    if COL_SHARDED[name]:
        return (tr, ns), lambda i, me: (i, me[0])
    return (tr, ns), lambda i, me: (me[0] * (ks // tr) + i, 0)


def _place_shard(w, name, layer):
    ks, ns = _shard_shape(name)
    tr = min(ks, 256)
    shape, index = _shard_block(name, tr)

    def body(me_ref, w_ref, o_ref):
        o_ref[...] = w_ref[...].astype(o_ref.dtype)

    return pl.pallas_call(
        body, name=f"place_{name}{layer}",
        grid_spec=pltpu.PrefetchScalarGridSpec(
            num_scalar_prefetch=1, grid=(ks // tr,),
            in_specs=[pl.BlockSpec((None, tr, ns), lambda i, me: (layer, i, 0))],
            out_specs=pl.BlockSpec((None,) + shape, lambda i, me: (0,) + index(i, me))),
        out_shape=jax.ShapeDtypeStruct((1,) + FULL_SHAPE[name], MXU_DTYPE),
        compiler_params=_cparams(("parallel",)),
    )(_chip_index(), w)


GATHER_ORDER = [("w_in", 0), ("w_out", 0), ("w_up", 0), ("w_down", 0), ("w_gate", 0), ("w_ple", 0),
                ("w_in", 1), ("w_out", 1), ("w_up", 1), ("w_down", 1), ("w_gate", 1), ("w_ple", 1)]
SEM = pl.BlockSpec(memory_space=pltpu.SEMAPHORE)
EFFECT = pltpu.SideEffectType.DATAFLOW_SIDE_EFFECTING


def _gather_copy(src_ref, dst_ref, name, idx, j, chip, send_sems, recv_sems, c):
    cx, cy = chip
    return pltpu.make_async_remote_copy(
        src_ref=src_ref, dst_ref=dst_ref, send_sem=send_sems.at[3 * idx + j], recv_sem=recv_sems.at[3 * idx + j],
        device_id=(cx, cy, c), device_id_type=MESH)


def _gather_start(placed, order, tag, after=None):
    n = len(order)
    extra = [] if after is None else [after]

    def body(*refs):
        ins = refs[:n]
        k = n + len(extra)
        send_sems, recv_sems = refs[k], refs[k + 1]
        outs = refs[k + 2:k + 2 + n]
        token = refs[-1]
        x, y, c = _my_place()
        me = 2 * x + y
        for idx, (name, _) in enumerate(order):
            for j, chip in enumerate(_other_chips(x, y)):
                _gather_copy(_window(ins[idx].at[0], name, me), _window(outs[idx].at[0], name, me), name, idx, j, chip,
                             send_sems, recv_sems, c).start()
        token[...] = jnp.zeros_like(token)

    res = pl.pallas_call(
        body, name="gather_start" + tag,
        out_shape=(pltpu.SemaphoreType.DMA((3 * n,)), pltpu.SemaphoreType.DMA((3 * n,)))
        + tuple(pltpu.HBM(a.shape, a.dtype) for a in placed) + (jax.ShapeDtypeStruct((8, 128), f32),),
        in_specs=[HBM] * n + [pl.BlockSpec(memory_space=pl.ANY)] * len(extra),
        out_specs=(SEM, SEM) + (HBM,) * n + (pl.BlockSpec(memory_space=pltpu.VMEM),),
        input_output_aliases={i: i + 2 for i in range(n)},
        compiler_params=pltpu.CompilerParams(has_side_effects=EFFECT),
    )(*[pltpu.with_memory_space_constraint(a, pltpu.HBM) for a in placed], *extra)
    return res[0], res[1], list(res[2:2 + n]), res[-1]


def _gather_wait(send_sems, recv_sems, arrays, order, idxs, after, name):
    n = len(idxs)

    def body(*refs):
        ins = refs[:n]
        send_ref, recv_ref = refs[n], refs[n + 1]
        x, y, c = _my_place()
        me = 2 * x + y
        for k, idx in enumerate(idxs):
            wname = order[idx][0]
            for j, chip in enumerate(_other_chips(x, y)):
                cx, cy = chip
                mine = _window(ins[k].at[0], wname, me)
                land = _window(ins[k].at[0], wname, 2 * cx + cy)
                _gather_copy(mine, mine, wname, idx, j, chip, send_ref, recv_ref, c).wait_send()
                _gather_copy(land, land, wname, idx, j, chip, send_ref, recv_ref, c).wait_recv()

    operands = list(arrays) + [send_sems, recv_sems] + list(after)
    in_specs = [HBM] * n + [SEM, SEM] + [pl.BlockSpec(memory_space=pl.ANY)] * len(after)
    res = pl.pallas_call(
        body, name=name, out_shape=tuple(pltpu.HBM(a.shape, a.dtype) for a in arrays),
        in_specs=in_specs, out_specs=(HBM,) * n, input_output_aliases={i: i for i in range(n)},
        compiler_params=pltpu.CompilerParams(has_side_effects=EFFECT),
    )(*operands)
    return list(res)


class _GatheredWeights:
    def __init__(self, shards):
        self.starts = []
        token = None
        for tag, order in (("_first", GATHER_ORDER[:1]), ("_rest", GATHER_ORDER[1:])):
            placed = [_place_shard(shards[name], name, layer) for name, layer in order]
            self.starts.append((order,) + _gather_start(placed, order, tag, token))
            token = self.starts[-1][-1]

    def take(self, layer, names, after):
        order, send, recv, arrays, _ = next(s for s in self.starts if (names[0], layer) in s[0])
        after = list(after)
        if order is self.starts[0][0]:
            after.append(self.starts[-1][-1])
        idxs = [order.index((n, layer)) for n in names]
        got = _gather_wait(send, recv, [arrays[i] for i in idxs], order, idxs, after, f"gather_wait{layer}_{names[0]}")
        return dict(zip(names, got)), 0


N_DEV = 8


def _reduce_copies(dws, lands, names, layer, send_sems, recv_sems):
    x, y, c = _my_place()
    me, my_dev = 2 * x + y, 4 * x + 2 * y + c
    out = []
    for t, name in enumerate(names):
        for j, (cx, cy) in enumerate(_other_chips(x, y)):
            out.append((pltpu.make_async_remote_copy(
                src_ref=_window(dws[t], name, 2 * cx + cy), dst_ref=lands[t].at[my_dev],
                send_sem=send_sems.at[4 * t + j], recv_sem=recv_sems.at[N_DEV * t + my_dev],
                device_id=(cx, cy, layer), device_id_type=MESH), False))
        out.append((pltpu.make_async_remote_copy(
            src_ref=_window(dws[t], name, me), dst_ref=lands[t].at[my_dev],
            send_sem=send_sems.at[4 * t + 3], recv_sem=recv_sems.at[N_DEV * t + my_dev],
            device_id=(x, y, layer), device_id_type=MESH), True))
    return out


def _reduce_start(dws, names, layer, tag):
    n = len(names)
    lands = [lax.empty((N_DEV,) + _shard_shape(nm), dws[0].dtype) for nm in names]

    def body(*refs):
        ins = refs[:n]
        send_sems, recv_sems = refs[2 * n], refs[2 * n + 1]
        land_out = refs[3 * n + 2:4 * n + 2]
        token = refs[-1]
        c = lax.axis_index("c")
        for cp, non_owner_only in _reduce_copies(ins, land_out, names, layer, send_sems, recv_sems):
            if non_owner_only:
                @pl.when(c != layer)
                def _():
                    cp.start()
            else:
                cp.start()
        token[...] = jnp.zeros_like(token)

    res = pl.pallas_call(
        body, name="reduce_start" + tag,
        out_shape=(pltpu.SemaphoreType.DMA((4 * n,)), pltpu.SemaphoreType.DMA((N_DEV * n,)))
        + tuple(pltpu.HBM(a.shape, a.dtype) for a in dws) + tuple(pltpu.HBM(a.shape, a.dtype) for a in lands)
        + (jax.ShapeDtypeStruct((8, 128), f32),),
        in_specs=[HBM] * (2 * n),
        out_specs=(SEM, SEM) + (HBM,) * (2 * n) + (pl.BlockSpec(memory_space=pltpu.VMEM),),
        input_output_aliases={i: i + 2 for i in range(2 * n)},
        compiler_params=pltpu.CompilerParams(has_side_effects=EFFECT),
    )(*[pltpu.with_memory_space_constraint(a, pltpu.HBM) for a in list(dws) + lands])
    return res[0], res[1], list(res[2:2 + n]), list(res[2 + n:2 + 2 * n]), res[-1]


def _reduce_wait(send_sems, recv_sems, dws, lands, names, layer, after, tag):
    n = len(names)

    def body(*refs):
        ins, land_in = refs[:n], refs[n:2 * n]
        send_ref, recv_ref = refs[2 * n], refs[2 * n + 1]
        x, y, c = _my_place()
        for cp, non_owner_only in _reduce_copies(ins, land_in, names, layer, send_ref, recv_ref):
            if non_owner_only:
                @pl.when(c != layer)
                def _():
                    cp.wait_send()
            else:
                cp.wait_send()

        @pl.when(c == layer)
        def _():
            for t in range(n):
                for k in range(1, N_DEV):
                    px, py, pc = x ^ ((k >> 2) & 1), y ^ ((k >> 1) & 1), c ^ (k & 1)
                    dev = 4 * px + 2 * py + pc
                    land = land_in[t].at[dev]
                    pltpu.make_async_remote_copy(
                        src_ref=land, dst_ref=land, send_sem=send_ref.at[4 * t], recv_sem=recv_ref.at[N_DEV * t + dev],
                        device_id=(px, py, pc), device_id_type=MESH).wait_recv()

    res = pl.pallas_call(
        body, name="reduce_wait" + tag,
        out_shape=tuple(pltpu.HBM(a.shape, a.dtype) for a in list(dws) + list(lands)),
        in_specs=[HBM] * (2 * n) + [SEM, SEM, pl.BlockSpec(memory_space=pl.ANY)], out_specs=(HBM,) * (2 * n),
        input_output_aliases={i: i for i in range(2 * n)},
        compiler_params=pltpu.CompilerParams(has_side_effects=EFFECT),
    )(*dws, *lands, send_sems, recv_sems, after)
    return list(res[:n]), list(res[n:])


def _sum_devices(land, own, name, layer, prev):
    ks, ns = _shard_shape(name)
    tr = min(ks, 256)
    shape, index = _shard_block(name, tr)

    def body(me_ref, dev_ref, *refs):
        s_ref, own_ref, out_ref = refs[0], refs[1], refs[-1]
        dev = dev_ref[0]
        acc = None
        for s in range(N_DEV):
            term = jnp.where(dev == s, own_ref[...], s_ref[s]).astype(f32)
            acc = term if acc is None else acc + term
        out_ref[...] = acc

    def mine(i, dev):
        return i * jnp.where((dev[0] & 1) == layer, 1, 0)

    in_specs = [pl.BlockSpec((N_DEV, tr, ns), lambda i, me, dev: (0, mine(i, dev), 0)),
                pl.BlockSpec(shape, lambda i, me, dev: index(mine(i, dev), me))]
    args = [land, own]
    aliases = {}
    if prev is not None:
        in_specs.append(pl.BlockSpec(memory_space=pl.ANY))
        args.append(prev)
        aliases = {4: 0}
    x, y, c = _my_place()
    return pl.pallas_call(
        body, name=f"sum_devices_{name}{layer}",
        grid_spec=pltpu.PrefetchScalarGridSpec(
            num_scalar_prefetch=2, grid=(ks // tr,), in_specs=in_specs,
            out_specs=pl.BlockSpec((None, tr, ns), lambda i, me, dev: (layer, mine(i, dev), 0))),
        out_shape=jax.ShapeDtypeStruct((2, ks, ns), f32), input_output_aliases=aliases,
        compiler_params=_cparams(("arbitrary",)),
    )(_chip_index(), jnp.reshape(4 * x + 2 * y + c, (1,)).astype(jnp.int32), *args)


class _GradReducer:
    GROUPS = (("1", 1, ("w_gate", "w_ple", "w_down", "w_up", "w_out", "w_in")),
              ("0a", 0, ("w_gate", "w_ple", "w_down", "w_up")),
              ("0b", 0, ("w_out",)),
              ("0c", 0, ("w_in",)))

    def __init__(self):
        self.grads = {}
        self.started = {}

    def add(self, name, layer, dw):
        self.grads[(name, layer)] = dw
        token = None
        for tag, glayer, names in self.GROUPS:
            if tag not in self.started and all((nm, glayer) in self.grads for nm in names):
                *self.started[tag], token = _reduce_start([self.grads[(nm, glayer)] for nm in names], names, glayer, tag)
        return token

    def finish(self, after):
        mine = {}
        for tag, layer, names in self.GROUPS:
            send, recv, dws, lands = self.started[tag]
            dws, lands = _reduce_wait(send, recv, dws, lands, names, layer, after, tag)
            for nm, dw, land in zip(names, dws, lands):
                mine[nm] = _sum_devices(land, dw, nm, layer, mine.get(nm))
        return _pair_layers(mine)


def _pair_layers(mine):
    names = list(BIG)

    def body(*refs):
        ins = refs[:len(names)]
        outs = refs[len(names):2 * len(names)]
        send_sems, recv_sems = refs[2 * len(names):]
        x, y, c = _my_place()
        sibling = (x, y, 1 - c)
        cps = []
        for t in range(len(names)):
            cp = pltpu.make_async_remote_copy(
                src_ref=ins[t].at[c], dst_ref=outs[t].at[c], send_sem=send_sems.at[t], recv_sem=recv_sems.at[t],
                device_id=sibling, device_id_type=MESH)
            cp.start()
            cps.append(cp)
        for t in range(len(names)):
            cps[t].wait_send()
            land = outs[t].at[1 - c]
            pltpu.make_async_remote_copy(
                src_ref=land, dst_ref=land, send_sem=send_sems.at[t], recv_sem=recv_sems.at[t],
                device_id=sibling, device_id_type=MESH).wait_recv()

    outs = pl.pallas_call(
        body, name="pair_layers", in_specs=[HBM] * len(names), out_specs=[HBM] * len(names),
        out_shape=[jax.ShapeDtypeStruct((2,) + _shard_shape(n), f32) for n in names],
        input_output_aliases={t: t for t in range(len(names))},
        scratch_shapes=[pltpu.SemaphoreType.DMA((len(names),)), pltpu.SemaphoreType.DMA((len(names),))],
    )(*[mine[n] for n in names])
    return dict(zip(names, outs))


SMALL_ROWS = 320


def _small_copies(vec_ref, land_ref, send_sems, recv_sems):
    x, y, c = _my_place()
    me = 4 * x + 2 * y + c
    out = []
    for k in range(1, N_DEV):
        peer = (x ^ ((k >> 2) & 1), y ^ ((k >> 1) & 1), c ^ (k & 1))
        src_dev = 4 * peer[0] + 2 * peer[1] + peer[2]
        send = pltpu.make_async_remote_copy(
            src_ref=vec_ref, dst_ref=land_ref.at[me], send_sem=send_sems.at[k - 1], recv_sem=recv_sems.at[k - 1],
            device_id=peer, device_id_type=MESH)
        arrival = pltpu.make_async_remote_copy(
            src_ref=land_ref.at[src_dev], dst_ref=land_ref.at[src_dev], send_sem=send_sems.at[k - 1],
            recv_sem=recv_sems.at[k - 1], device_id=peer, device_id_type=MESH)
        out.append((send, arrival))
    return out


def _small_start(vec):
    land = lax.empty((N_DEV,) + vec.shape, vec.dtype)

    def body(v_ref, land_in, send_sems, recv_sems, v_out, land_out):
        del land_in, v_out
        for send, _ in _small_copies(v_ref, land_out, send_sems, recv_sems):
            send.start()

    return pl.pallas_call(
        body, name="small_start",
        out_shape=(pltpu.SemaphoreType.DMA((N_DEV - 1,)), pltpu.SemaphoreType.DMA((N_DEV - 1,)),
                   pltpu.HBM(vec.shape, vec.dtype), pltpu.HBM(land.shape, land.dtype)),
        in_specs=[HBM, HBM], out_specs=(SEM, SEM, HBM, HBM), input_output_aliases={0: 2, 1: 3},
        compiler_params=pltpu.CompilerParams(has_side_effects=EFFECT),
    )(pltpu.with_memory_space_constraint(vec, pltpu.HBM), pltpu.with_memory_space_constraint(land, pltpu.HBM))


def _small_wait(send_sems, recv_sems, vec, land, after):
    def body(v_ref, land_ref, send_ref, recv_ref, after_ref, v_out, land_out):
        del after_ref, v_out, land_out
        for send, arrival in _small_copies(v_ref, land_ref, send_ref, recv_ref):
            send.wait_send()
            arrival.wait_recv()

    return pl.pallas_call(
        body, name="small_wait", out_shape=(pltpu.HBM(vec.shape, vec.dtype), pltpu.HBM(land.shape, land.dtype)),
        in_specs=[HBM, HBM, SEM, SEM, pl.BlockSpec(memory_space=pl.ANY)], out_specs=(HBM, HBM),
        input_output_aliases={0: 0, 1: 1}, compiler_params=pltpu.CompilerParams(has_side_effects=EFFECT),
    )(vec, land, send_sems, recv_sems, after)


def _small_sum(vec, land):
    x, y, c = _my_place()

    def body(dev_ref, v_ref, land_ref, out_ref):
        acc = None
        for s in range(N_DEV):
            term = jnp.where(dev_ref[0] == s, v_ref[...], land_ref[s])
            acc = term if acc is None else acc + term
        out_ref[...] = acc

    return pl.pallas_call(
        body, name="small_sum",
        grid_spec=pltpu.PrefetchScalarGridSpec(
            num_scalar_prefetch=1, grid=(1,),
            in_specs=[pl.BlockSpec(vec.shape, lambda i, dev: (0, 0)), pl.BlockSpec(land.shape, lambda i, dev: (0, 0, 0))],
            out_specs=pl.BlockSpec(vec.shape, lambda i, dev: (0, 0))),
        out_shape=jax.ShapeDtypeStruct(vec.shape, vec.dtype),
        compiler_params=_cparams(("arbitrary",)),
    )(jnp.reshape(4 * x + 2 * y + c, (1,)).astype(jnp.int32), vec, land)


def _adamw(w, g, m, v, name):
    rows, cols = w.shape
    tr = rows
    for cand in (512, 256, 128, 64, 32, 16, 8):
        if rows % cand == 0 and cand * cols * 4 <= 2 * 1024 * 1024:
            tr = cand
            break
    c1 = np.float32(1.0 - ADAM_B1 ** ADAM_STEP)
    c2 = np.float32(1.0 - ADAM_B2 ** ADAM_STEP)

    def body(w_ref, g_ref, m_ref, v_ref, go_ref, d_ref, mo_ref, vo_ref):
        gv = g_ref[...]
        go_ref[...] = gv
        mn = ADAM_B1 * m_ref[...] + (1.0 - ADAM_B1) * gv
        vn = ADAM_B2 * v_ref[...] + (1.0 - ADAM_B2) * (gv * gv)
        mo_ref[...] = mn
        vo_ref[...] = vn
        d_ref[...] = -ADAM_LR * ((mn / c1) / (jnp.sqrt(vn / c2) + ADAM_EPS) + ADAM_WD * w_ref[...])

    blk = pl.BlockSpec((tr, cols), lambda i: (i, 0))
    return pl.pallas_call(
        body, name="adamw_" + name, grid=(rows // tr,), in_specs=[blk] * 4, out_specs=[blk] * 4,
        out_shape=[jax.ShapeDtypeStruct((rows, cols), f32)] * 4,
        compiler_params=_cparams(("parallel",)),
    )(w, g, m, v)


SMALL = ("norm1", "pool_w", "pool_scale", "norm2", "norm3", "final_norm")
ORDER = ("norm1", "w_in", "pool_w", "pool_scale", "w_out", "norm2", "w_up", "w_down", "norm3", "w_gate", "w_ple",
         "final_norm")


def _pack_small(tree, extra=None):
    parts = [tree[n].reshape(-1) for n in SMALL]
    if extra is not None:
        parts.append(extra.reshape(-1))
    flat = jnp.concatenate(parts)
    return jnp.pad(flat, (0, SMALL_ROWS * 128 - flat.shape[0])).reshape(SMALL_ROWS, 128)


def _unpack_small(packed, like):
    flat = packed.reshape(-1)
    out, off = {}, 0
    for n in SMALL:
        size = int(np.prod(like[n].shape))
        out[n] = flat[off:off + size].reshape(like[n].shape)
        off += size
    return out, flat[off]


def kernel(x, p, positions, norm1, w_in, pool_w, pool_scale, w_out, norm2, w_up, w_down, norm3, w_gate, w_ple, final_norm, loss_target, m_norm1, m_w_in, m_pool_w, m_pool_scale, m_w_out, m_norm2, m_w_up, m_w_down, m_norm3, m_w_gate, m_w_ple, m_final_norm, v_norm1, v_w_in, v_pool_w, v_pool_scale, v_w_out, v_norm2, v_w_up, v_w_down, v_norm3, v_w_gate, v_w_ple, v_final_norm):
    w = dict(norm1=norm1, w_in=w_in, pool_w=pool_w, pool_scale=pool_scale, w_out=w_out, norm2=norm2, w_up=w_up,
             w_down=w_down, norm3=norm3, w_gate=w_gate, w_ple=w_ple, final_norm=final_norm)
    m = dict(norm1=m_norm1, w_in=m_w_in, pool_w=m_pool_w, pool_scale=m_pool_scale, w_out=m_w_out, norm2=m_norm2,
             w_up=m_w_up, w_down=m_w_down, norm3=m_norm3, w_gate=m_w_gate, w_ple=m_w_ple, final_norm=m_final_norm)
    v = dict(norm1=v_norm1, w_in=v_w_in, pool_w=v_pool_w, pool_scale=v_pool_scale, w_out=v_w_out, norm2=v_norm2,
             w_up=v_w_up, w_down=v_w_down, norm3=v_norm3, w_gate=v_w_gate, w_ple=v_w_ple, final_norm=v_final_norm)
    small = {n: w[n] for n in SMALL}

    wsrc = _GatheredWeights({n: w[n] for n in BIG})
    reducer = _GradReducer()
    loss8, dx, small_grads = _local_step(x[0], p.reshape(2, x.shape[1], PLE_DIM), positions[0], wsrc, small, loss_target[0], reducer)
    s_send, s_recv, s_vec, s_land = _small_start(_pack_small(small_grads, loss8[0, 0]))
    gsh = reducer.finish(s_vec)

    g_out, d_out, m_out, v_out = {}, {}, {}, {}
    for n in BIG:
        shp = w[n].shape
        two = lambda a: a.reshape(shp[0] * shp[1], shp[2])
        g2, d2, m2, v2 = _adamw(two(w[n]), two(gsh[n]), two(m[n]), two(v[n]), n)
        g_out[n], d_out[n], m_out[n], v_out[n] = g2.reshape(shp), d2.reshape(shp), m2.reshape(shp), v2.reshape(shp)
    red = _small_sum(*_small_wait(s_send, s_recv, s_vec, s_land, d2))
    g_small, loss = _unpack_small(red, small)
    _, d2, m2, v2 = _adamw(_pack_small(small), red, _pack_small({n: m[n] for n in SMALL}),
                           _pack_small({n: v[n] for n in SMALL}), "small")
    for tree, packed in ((d_out, d2), (m_out, m2), (v_out, v2)):
        tree.update(_unpack_small(packed, small)[0])
    g_out.update(g_small)

    return (loss, dx[None], *[g_out[n] for n in ORDER], *[d_out[n] for n in ORDER], *[m_out[n] for n in ORDER],
            *[v_out[n] for n in ORDER])
```

```python
import jax
import jax.numpy as jnp
import numpy as np
from jax import lax
from jax.experimental import pallas as pl
from jax.experimental.pallas import tpu as pltpu

f32 = jnp.float32
MXU_DTYPE = jnp.bfloat16
COMM_DTYPE = jnp.bfloat16

D_MODEL = 1024
POOL_WIDTH = 256
POOL_GC = 64
ATTN_WIDTH = 768
HEAD_DIM = 64
N_IN = POOL_WIDTH + 3 * ATTN_WIDTH
D_FF = 4096
PLE_DIM = 256
BLK = 128
DILATIONS = (1, 4, 16)
ROT_DIM = 16
ROPE_THETA = 500000.0
EPS = 1e-6
ATTN_SCALE = HEAD_DIM ** -0.5
NEG_BIG = -1e30

ADAM_LR, ADAM_B1, ADAM_B2, ADAM_EPS, ADAM_WD, ADAM_STEP = 0.001, 0.9, 0.999, 1e-08, 0.01, 10

TM = 512
TM_WGRAD = 1024
HALO = 16
VMEM_LIMIT = 48 * 1024 * 1024
VMEM_LIMIT_LARGE = 58 * 1024 * 1024
N_CHIPS = 4
MESH = pl.DeviceIdType.MESH

BIG = ("w_in", "w_out", "w_up", "w_down", "w_gate", "w_ple")
FULL_SHAPE = {"w_in": (D_MODEL, N_IN), "w_out": (D_MODEL, D_MODEL), "w_up": (D_MODEL, D_FF),
              "w_down": (D_FF, D_MODEL), "w_gate": (D_MODEL, D_MODEL), "w_ple": (PLE_DIM, D_MODEL)}
COL_SHARDED = {"w_in": True, "w_out": False, "w_up": True, "w_down": False, "w_gate": False, "w_ple": True}


def _shard_shape(name):
    k, n = FULL_SHAPE[name]
    return (k, n // N_CHIPS) if COL_SHARDED[name] else (k // N_CHIPS, n)


def _cparams(sem=None, vmem=VMEM_LIMIT):
    return pltpu.CompilerParams(dimension_semantics=sem, vmem_limit_bytes=vmem)


def _resident(block_shape, index_map):
    return pl.BlockSpec(block_shape, index_map, pipeline_mode=pl.Buffered(1))


def _mx(x):
    return x.astype(MXU_DTYPE)


def _dot(a, b):
    return jnp.dot(a, b, preferred_element_type=f32)


def _dot_nt(a, b):
    return lax.dot_general(a, b, (((1,), (1,)), ((), ())), preferred_element_type=f32)


def _dot_tn(a, b):
    return lax.dot_general(a, b, (((0,), (0,)), ((), ())), preferred_element_type=f32)


def _sigmoid(x):
    return 1.0 / (1.0 + jnp.exp(-x))


def _rope_apply(y, c, s1, s2, width):
    return y * c + pltpu.roll(y, width - 8, axis=1) * s1 + pltpu.roll(y, 8, axis=1) * s2


def _rope_transpose(dy, c, s1, s2, width):
    return dy * c + pltpu.roll(dy * s1, 8, axis=1) + pltpu.roll(dy * s2, width - 8, axis=1)


def _norm_matmul(h, g, w, layer, tn, name, rope=None):
    s_len, d = h.shape
    n = w.shape[2]

    def body(*refs):
        if rope is None:
            h_ref, g_ref, w_ref, y_ref, hn_ref = refs
        else:
            h_ref, g_ref, w_ref, c_ref, s1_ref, s2_ref, y_ref, hn_ref = refs
            reps = tn // 128
            c = jnp.concatenate([c_ref[...]] * reps, axis=1)
            s1 = jnp.concatenate([s1_ref[...]] * reps, axis=1)
            s2 = jnp.concatenate([s2_ref[...]] * reps, axis=1)
        x = h_ref[...]
        r = lax.rsqrt(jnp.mean(x * x, axis=-1, keepdims=True) + EPS)
        hn = ((x * r) * g_ref[...]).astype(hn_ref.dtype)
        hn_ref[...] = hn
        for j in range(n // tn):
            y = _dot(hn, w_ref[:, j * tn:(j + 1) * tn])
            if rope is not None and POOL_WIDTH <= j * tn < POOL_WIDTH + 2 * ATTN_WIDTH:
                y = _rope_apply(y, c, s1, s2, tn)
            y_ref[:, j * tn:(j + 1) * tn] = y

    in_specs = [pl.BlockSpec((TM, d), lambda i: (i, 0)),
                pl.BlockSpec((1, d), lambda i: (0, 0)),
                _resident((None, d, n), lambda i: (layer, 0, 0))]
    args = [h, g, w]
    if rope is not None:
        assert POOL_WIDTH % tn == 0 and (2 * ATTN_WIDTH) % tn == 0
        in_specs += [pl.BlockSpec((TM, 128), lambda i: (i, 0))] * 3
        args += list(rope)
    return pl.pallas_call(
        body, name=name, grid=(s_len // TM,), in_specs=in_specs,
        out_specs=[pl.BlockSpec((TM, n), lambda i: (i, 0)), pl.BlockSpec((TM, d), lambda i: (i, 0))],
        out_shape=[jax.ShapeDtypeStruct((s_len, n), f32), jax.ShapeDtypeStruct((s_len, d), MXU_DTYPE)],
        compiler_params=_cparams(("parallel",)),
    )(*args)


def _gate_ple_fwd(h2, g, w_gate, w_ple, layer, p, p_layer, name, head=None):
    s_len, d = h2.shape

    def body(h_ref, g_ref, wg_ref, p_ref, wp_ref, *rest):
        gl_ref, hn_ref = rest[-2:]
        x = h_ref[...]
        r = lax.rsqrt(jnp.mean(x * x, axis=-1, keepdims=True) + EPS)
        hn = ((x * r) * g_ref[...]).astype(hn_ref.dtype)
        hn_ref[...] = hn
        gl = _dot(hn, wg_ref[...])
        gl_ref[...] = gl.astype(gl_ref.dtype)
        h3 = x + _sigmoid(gl) * _dot(_mx(p_ref[...]), wp_ref[...])
        if head is None:
            rest[0][...] = h3
            return
        gf_ref, t_ref, loss_ref, dh_ref, dgf_ref = rest[:5]
        i = pl.program_id(0)
        gv = gf_ref[...]
        r3 = lax.rsqrt(jnp.mean(h3 * h3, axis=-1, keepdims=True) + EPS)
        xh = h3 * r3
        diff = xh * gv - t_ref[...]
        part = 0.5 * jnp.sum(jnp.mean(diff * diff, axis=-1, keepdims=True), axis=0, keepdims=True)
        dy = diff * (1.0 / d)
        dxh = dy * gv
        dh_ref[...] = r3 * (dxh - xh * jnp.mean(dxh * xh, axis=-1, keepdims=True))
        dgsum = jnp.sum(dy * xh, axis=0, keepdims=True)
        lossb = jnp.broadcast_to(part, (8, 128))

        @pl.when(i == 0)
        def _():
            loss_ref[...] = lossb
            dgf_ref[...] = dgsum

        @pl.when(i > 0)
        def _():
            loss_ref[...] += lossb
            dgf_ref[...] += dgsum

    row = lambda i: (i, 0)
    one = lambda i: (0, 0)
    in_specs = [pl.BlockSpec((TM, d), row), pl.BlockSpec((1, d), one),
                pl.BlockSpec((None, d, d), lambda i: (layer, 0, 0)),
                pl.BlockSpec((None, TM, PLE_DIM), lambda i: (p_layer, i, 0)),
                pl.BlockSpec((None, PLE_DIM, d), lambda i: (layer, 0, 0))]
    args = [h2, g, w_gate, p, w_ple]
    saved = [jax.ShapeDtypeStruct((s_len, d), MXU_DTYPE)] * 2
    if head is None:
        out_specs = [pl.BlockSpec((TM, d), row)] * 3
        out_shape = [jax.ShapeDtypeStruct((s_len, d), f32)] + saved
    else:
        in_specs += [pl.BlockSpec((1, d), one), pl.BlockSpec((TM, d), row)]
        args += list(head)
        out_specs = [pl.BlockSpec((8, 128), one), pl.BlockSpec((TM, d), row), pl.BlockSpec((1, d), one)] \
            + [pl.BlockSpec((TM, d), row)] * 2
        out_shape = [jax.ShapeDtypeStruct((8, 128), f32), jax.ShapeDtypeStruct((s_len, d), f32),
                     jax.ShapeDtypeStruct((1, d), f32)] + saved
    return pl.pallas_call(
        body, name=name, grid=(s_len // TM,), in_specs=in_specs, out_specs=out_specs, out_shape=out_shape,
        compiler_params=_cparams(("arbitrary",)),
    )(*args)


def _gate_bwd(dh3, gl, p, p_layer, w_ple, w_gate, layer, h2, g, name):
    s_len, d = dh3.shape

    def body(dh_ref, gl_ref, p_ref, wp_ref, wg_ref, h_ref, g_ref, dh2_ref, dg_ref, de_ref, dgl_ref):
        i = pl.program_id(0)
        dh = dh_ref[...]
        gate = _sigmoid(gl_ref[...].astype(f32))
        e = _dot(_mx(p_ref[...]), wp_ref[...])
        de_ref[...] = (dh * gate).astype(de_ref.dtype)
        dgl = ((dh * e) * (gate * (1.0 - gate))).astype(dgl_ref.dtype)
        dgl_ref[...] = dgl
        dx, dgrow = _rmsnorm_bwd(_dot_nt(dgl, wg_ref[...]), h_ref[...], g_ref[...])
        dh2_ref[...] = dh + dx
        dgsum = jnp.sum(dgrow, axis=0, keepdims=True)

        @pl.when(i == 0)
        def _():
            dg_ref[...] = dgsum

        @pl.when(i > 0)
        def _():
            dg_ref[...] += dgsum

    row = lambda i: (i, 0)
    blk = pl.BlockSpec((TM, d), row)
    return pl.pallas_call(
        body, name=name, grid=(s_len // TM,),
        in_specs=[blk, blk, pl.BlockSpec((None, TM, PLE_DIM), lambda i: (p_layer, i, 0)),
                  _resident((None, PLE_DIM, d), lambda i: (layer, 0, 0)),
                  _resident((None, d, d), lambda i: (layer, 0, 0)), blk, pl.BlockSpec((1, d), lambda i: (0, 0))],
        out_specs=[blk, pl.BlockSpec((1, d), lambda i: (0, 0)), blk, blk],
        out_shape=[jax.ShapeDtypeStruct((s_len, d), f32), jax.ShapeDtypeStruct((1, d), f32),
                   jax.ShapeDtypeStruct((s_len, d), MXU_DTYPE), jax.ShapeDtypeStruct((s_len, d), MXU_DTYPE)],
        compiler_params=_cparams(("arbitrary",)),
    )(dh3, gl, p, w_ple, w_gate, h2, g)


def _rmsnorm_bwd(dhn, x, g):
    r = lax.rsqrt(jnp.mean(x * x, axis=-1, keepdims=True) + EPS)
    xh = x * r
    dxh = dhn * g
    dx = r * (dxh - xh * jnp.mean(dxh * xh, axis=-1, keepdims=True))
    return dx, dhn * xh


def _matmul_nt_norm_bwd(dy, w, layer, h_prev, g, dres, name, tk=1024, after=None):
    s_len, k_dim = dy.shape
    d = h_prev.shape[1]

    def body(dy_ref, w_ref, h_ref, g_ref, dres_ref, *rest):
        dh_ref, dg_ref = rest[-2:]
        i = pl.program_id(0)
        acc = None
        for k in range(k_dim // tk):
            part = _dot_nt(_mx(dy_ref[:, k * tk:(k + 1) * tk]), w_ref[:, k * tk:(k + 1) * tk])
            acc = part if acc is None else acc + part
        dx, dgrow = _rmsnorm_bwd(acc, h_ref[...], g_ref[...])
        dh_ref[...] = dres_ref[...] + dx
        dgsum = jnp.sum(dgrow, axis=0, keepdims=True)

        @pl.when(i == 0)
        def _():
            dg_ref[...] = dgsum

        @pl.when(i > 0)
        def _():
            dg_ref[...] += dgsum

    in_specs = [pl.BlockSpec((TM, k_dim), lambda i: (i, 0)),
                _resident((None, d, k_dim), lambda i: (layer, 0, 0)),
                pl.BlockSpec((TM, d), lambda i: (i, 0)),
                pl.BlockSpec((1, d), lambda i: (0, 0)),
                pl.BlockSpec((TM, d), lambda i: (i, 0))]
    args = [dy, w, h_prev, g, dres]
    if after is not None:
        in_specs.append(pl.BlockSpec(memory_space=pl.ANY))
        args.append(after)
    return pl.pallas_call(
        body, name=name, grid=(s_len // TM,), in_specs=in_specs,
        out_specs=[pl.BlockSpec((TM, d), lambda i: (i, 0)), pl.BlockSpec((1, d), lambda i: (0, 0))],
        out_shape=[jax.ShapeDtypeStruct((s_len, d), f32), jax.ShapeDtypeStruct((1, d), f32)],
        compiler_params=_cparams(("arbitrary",)),
    )(*args)


def _mlp_fwd(h1, g, w_up, w_down, layer, name, tf=1024):
    s_len, d = h1.shape
    ff = w_up.shape[2]

    def body(h_ref, g_ref, wu_ref, wd_ref, h2_ref, a_ref, hn_ref):
        x = h_ref[...]
        r = lax.rsqrt(jnp.mean(x * x, axis=-1, keepdims=True) + EPS)
        hn = ((x * r) * g_ref[...]).astype(hn_ref.dtype)
        hn_ref[...] = hn
        acc = x
        for j in range(ff // tf):
            a = _dot(hn, wu_ref[:, j * tf:(j + 1) * tf])
            a_ref[:, j * tf:(j + 1) * tf] = a.astype(a_ref.dtype)
            relu = jnp.maximum(a, 0.0)
            acc = acc + _dot(_mx(relu * relu), wd_ref[j * tf:(j + 1) * tf, :])
        h2_ref[...] = acc

    row = lambda i: (i, 0)
    return pl.pallas_call(
        body, name=name, grid=(s_len // TM,),
        in_specs=[pl.BlockSpec((TM, d), row), pl.BlockSpec((1, d), lambda i: (0, 0)),
                  _resident((None, d, ff), lambda i: (layer, 0, 0)), _resident((None, ff, d), lambda i: (layer, 0, 0))],
        out_specs=[pl.BlockSpec((TM, d), row), pl.BlockSpec((TM, ff), row), pl.BlockSpec((TM, d), row)],
        out_shape=[jax.ShapeDtypeStruct((s_len, d), f32), jax.ShapeDtypeStruct((s_len, ff), MXU_DTYPE),
                   jax.ShapeDtypeStruct((s_len, d), MXU_DTYPE)],
        compiler_params=_cparams(("parallel",)),
    )(h1, g, w_up, w_down)


def _mlp_bwd(dh2, w_down, w_up, layer, a, h1, g, name, tf=1024):
    s_len, d = dh2.shape
    ff = a.shape[1]

    def body(dh_ref, wd_ref, wu_ref, a_ref, h_ref, g_ref, dh1_ref, dg_ref, da_ref):
        i = pl.program_id(0)
        dh = dh_ref[...]
        dhb = _mx(dh)
        acc = None
        for j in range(ff // tf):
            cols = slice(j * tf, (j + 1) * tf)
            dact = _dot_nt(dhb, wd_ref[cols, :])
            da = (dact * (2.0 * jnp.maximum(a_ref[:, cols].astype(f32), 0.0))).astype(da_ref.dtype)
            da_ref[:, cols] = da
            part = _dot_nt(da, wu_ref[:, cols])
            acc = part if acc is None else acc + part
        dx, dgrow = _rmsnorm_bwd(acc, h_ref[...], g_ref[...])
        dh1_ref[...] = dh + dx
        dgsum = jnp.sum(dgrow, axis=0, keepdims=True)

        @pl.when(i == 0)
        def _():
            dg_ref[...] = dgsum

        @pl.when(i > 0)
        def _():
            dg_ref[...] += dgsum

    row = lambda i: (i, 0)
    return pl.pallas_call(
        body, name=name, grid=(s_len // TM,),
        in_specs=[pl.BlockSpec((TM, d), row), _resident((None, ff, d), lambda i: (layer, 0, 0)),
                  _resident((None, d, ff), lambda i: (layer, 0, 0)), pl.BlockSpec((TM, ff), row),
                  pl.BlockSpec((TM, d), row), pl.BlockSpec((1, d), lambda i: (0, 0))],
        out_specs=[pl.BlockSpec((TM, d), row), pl.BlockSpec((1, d), lambda i: (0, 0)), pl.BlockSpec((TM, ff), row)],
        out_shape=[jax.ShapeDtypeStruct((s_len, d), f32), jax.ShapeDtypeStruct((1, d), f32),
                   jax.ShapeDtypeStruct((s_len, ff), MXU_DTYPE)],
        compiler_params=_cparams(("arbitrary",), vmem=VMEM_LIMIT_LARGE),
    )(dh2, w_down, w_up, a, h1, g)


def _weight_grad(a, b, name, act=False, lead=None):
    s_len, k_dim = a.shape[-2:]
    n = b.shape[1]
    tka = min(k_dim, 2048)
    tnb = n if n <= 1024 else (2048 if n % 2048 == 0 else 640)
    ns = s_len // TM_WGRAD

    def body(a_ref, b_ref, o_ref, acc_ref):
        s = pl.program_id(2)
        x = a_ref[...]
        if act:
            relu = jnp.maximum(x, 0.0)
            x = relu * relu

        @pl.when(s == 0)
        def _():
            acc_ref[...] = jnp.zeros_like(acc_ref)

        acc_ref[...] += _dot_tn(_mx(x), _mx(b_ref[...]))

        @pl.when(s == ns - 1)
        def _():
            o_ref[...] = acc_ref[...].astype(o_ref.dtype)

    if lead is None:
        a_spec = pl.BlockSpec((TM_WGRAD, tka), lambda i, j, s: (s, i))
    else:
        a_spec = pl.BlockSpec((None, TM_WGRAD, tka), lambda i, j, s: (lead, s, i))
    return pl.pallas_call(
        body, name=name, grid=(k_dim // tka, n // tnb, ns),
        in_specs=[a_spec, pl.BlockSpec((TM_WGRAD, tnb), lambda i, j, s: (s, j))],
        out_specs=pl.BlockSpec((tka, tnb), lambda i, j, s: (i, j)),
        out_shape=jax.ShapeDtypeStruct((k_dim, n), COMM_DTYPE),
        scratch_shapes=[pltpu.VMEM((tka, tnb), f32)],
        compiler_params=_cparams(("parallel", "parallel", "arbitrary")),
    )(a, b)


def _group_select(lane, x2, x4, x8, x16):
    grp = lane // POOL_GC
    return jnp.where(grp == 0, x2, jnp.where(grp == 1, x4, jnp.where(grp == 2, x8, x16)))


def _pool_window(lane):
    grp = lane // POOL_GC
    return jnp.where(grp == 0, 2, jnp.where(grp == 1, 4, jnp.where(grp == 2, 8, 16)))


def _pool_y(u, halo, i):
    xs = jnp.concatenate([jnp.where(i > 0, halo, 0.0), u], axis=0)
    s2 = xs + pltpu.roll(xs, 1, axis=0)
    s4 = s2 + pltpu.roll(s2, 2, axis=0)
    s8 = s4 + pltpu.roll(s4, 4, axis=0)
    s16 = s8 + pltpu.roll(s8, 8, axis=0)
    lane = lax.broadcasted_iota(jnp.int32, xs.shape, 1)
    sel = _group_select(lane, s2, s4, s8, s16)[HALO:, :]
    t = i * TM + lax.broadcasted_iota(jnp.int32, u.shape, 0)
    cnt = jnp.minimum(_pool_window(lax.broadcasted_iota(jnp.int32, u.shape, 1)), t + 1).astype(f32)
    return sel / cnt - u


def _group_weights(l0, l1, l2):
    mx = jnp.maximum(jnp.maximum(l0, l1), l2)
    e0, e1, e2 = jnp.exp(l0 - mx), jnp.exp(l1 - mx), jnp.exp(l2 - mx)
    den = e0 + e1 + e2
    return e0 / den, e1 / den, e2 / den


def _mixer_out_proj(z, wbd, scale, outs, lses, w_out, layer, h, name):
    s_len, d = h.shape

    def body(u_ref, halo_ref, wbd_ref, sc_ref, o0, o1, o2, l0, l1, l2, wo_ref, h_ref, m_ref, h1_ref):
        i = pl.program_id(0)
        y = _pool_y(u_ref[...], halo_ref[...], i)
        pool = _dot(_mx(y), wbd_ref[...]) * sc_ref[...]
        w0, w1, w2 = _group_weights(l0[...], l1[...], l2[...])
        m = jnp.concatenate([pool, o0[...] * w0, o1[...] * w1, o2[...] * w2], axis=1).astype(m_ref.dtype)
        m_ref[...] = m
        h1_ref[...] = h_ref[...] + _dot(m, wo_ref[...])

    row = lambda i: (i, 0)
    blk = pl.BlockSpec((TM, 256), row)
    grp = [pl.BlockSpec((TM, 256), lambda i, g=g: (i, g)) for g in range(3)]
    return pl.pallas_call(
        body, name=name, grid=(s_len // TM,),
        in_specs=[blk, pl.BlockSpec((HALO, 256), lambda i: (jnp.maximum(i * (TM // HALO) - 1, 0), 0)),
                  pl.BlockSpec((256, 256), lambda i: (0, 0)), pl.BlockSpec((1, 256), lambda i: (0, 0))] + grp + grp
        + [_resident((None, d, d), lambda i: (layer, 0, 0)), pl.BlockSpec((TM, d), row)],
        out_specs=[pl.BlockSpec((TM, d), row)] * 2,
        out_shape=[jax.ShapeDtypeStruct((s_len, d), MXU_DTYPE), jax.ShapeDtypeStruct((s_len, d), f32)],
        compiler_params=_cparams(("parallel",)),
    )(z, z, wbd, scale, outs, outs, outs, lses, lses, lses, w_out, h)


def _head_sums(x):
    r = lax.broadcasted_iota(jnp.int32, (256, 256), 0) // HEAD_DIM
    c = lax.broadcasted_iota(jnp.int32, (256, 256), 1) // HEAD_DIM
    ones = jnp.where(r == c, 1.0, 0.0).astype(jnp.bfloat16)
    hi = x.astype(jnp.bfloat16)
    lo = (x - hi.astype(f32)).astype(jnp.bfloat16)
    return _dot(hi, ones) + _dot(lo, ones)


def _out_combine_bwd(dh1, w_out, layer, outs, lses, name, after=None):
    s_len, d = dh1.shape

    def body(dh_ref, w_ref, o0, o1, o2, l0, l1, l2, *rest):
        dp_ref, do_ref, dl_ref = rest[-3:]
        dm = _dot_nt(_mx(dh_ref[...]), w_ref[...])
        dp_ref[...] = dm[:, :POOL_WIDTH]
        w = _group_weights(l0[...], l1[...], l2[...])
        da = [dm[:, POOL_WIDTH + 256 * g:POOL_WIDTH + 256 * (g + 1)] for g in range(3)]
        o = (o0[...], o1[...], o2[...])
        dw = [_head_sums(da[g] * o[g]) for g in range(3)]
        t = w[0] * dw[0] + w[1] * dw[1] + w[2] * dw[2]
        do_ref[...] = jnp.concatenate([da[g] * w[g] for g in range(3)], axis=1)
        dl_ref[...] = jnp.concatenate([w[g] * t for g in range(3)], axis=1)

    grp = [pl.BlockSpec((TM, 256), lambda i, g=g: (i, g)) for g in range(3)]
    in_specs = [pl.BlockSpec((TM, d), lambda i: (i, 0)), _resident((None, d, d), lambda i: (layer, 0, 0))] + grp + grp
    args = [dh1, w_out, outs, outs, outs, lses, lses, lses]
    if after is not None:
        in_specs.append(pl.BlockSpec(memory_space=pl.ANY))
        args.append(after)
    return pl.pallas_call(
        body, name=name, grid=(s_len // TM,), in_specs=in_specs,
        out_specs=[pl.BlockSpec((TM, POOL_WIDTH), lambda i: (i, 0))] + [pl.BlockSpec((TM, ATTN_WIDTH), lambda i: (i, 0))] * 2,
        out_shape=[jax.ShapeDtypeStruct((s_len, POOL_WIDTH), f32)] + [jax.ShapeDtypeStruct((s_len, ATTN_WIDTH), f32)] * 2,
        compiler_params=_cparams(("parallel",)),
    )(*args)


def _pool_bwd(z, dm, wbd, scale, name, after=None):
    s_len = z.shape[0]
    n_halo = s_len // HALO

    def body(u_ref, uh_ref, d_ref, dh_ref, wbd_ref, sc_ref, *rest):
        du_ref, dw_ref, dsc_ref = rest[-3:]
        i = pl.program_id(0)
        last = pl.num_programs(0) - 1
        y = _pool_y(u_ref[...], uh_ref[...], i)
        yb = _mx(y)
        dpo = d_ref[...]
        sc = sc_ref[...]
        dsc = jnp.sum(dpo * _dot(yb, wbd_ref[...]), axis=0, keepdims=True)
        dwp = _dot_tn(yb, _mx(dpo * sc))

        @pl.when(i == 0)
        def _():
            dsc_ref[...] = dsc
            dw_ref[...] = dwp

        @pl.when(i > 0)
        def _():
            dsc_ref[...] += dsc
            dw_ref[...] += dwp

        ext = jnp.concatenate([dpo, jnp.where(i < last, dh_ref[...], 0.0)], axis=0)
        dy = _dot_nt(_mx(ext * sc), wbd_ref[...])
        t = i * TM + lax.broadcasted_iota(jnp.int32, ext.shape, 0)
        lane = lax.broadcasted_iota(jnp.int32, ext.shape, 1)
        e = dy / jnp.minimum(_pool_window(lane), t + 1).astype(f32)
        rows = ext.shape[0]
        f2 = e + pltpu.roll(e, rows - 1, axis=0)
        f4 = f2 + pltpu.roll(f2, rows - 2, axis=0)
        f8 = f4 + pltpu.roll(f4, rows - 4, axis=0)
        f16 = f8 + pltpu.roll(f8, rows - 8, axis=0)
        du_ref[...] = (_group_select(lane, f2, f4, f8, f16) - dy)[:TM, :].astype(du_ref.dtype)

    row = lambda i: (i, 0)
    blk = pl.BlockSpec((TM, 256), row)
    extra = [] if after is None else [after]
    return pl.pallas_call(
        body, name=name, grid=(s_len // TM,),
        in_specs=[blk, pl.BlockSpec((HALO, 256), lambda i: (jnp.maximum(i * (TM // HALO) - 1, 0), 0)),
                  blk, pl.BlockSpec((HALO, 256), lambda i: (jnp.minimum((i + 1) * (TM // HALO), n_halo - 1), 0)),
                  pl.BlockSpec((256, 256), lambda i: (0, 0)), pl.BlockSpec((1, 256), lambda i: (0, 0))]
        + [pl.BlockSpec(memory_space=pl.ANY)] * len(extra),
        out_specs=[blk, pl.BlockSpec((256, 256), lambda i: (0, 0)), pl.BlockSpec((1, 256), lambda i: (0, 0))],
        out_shape=[jax.ShapeDtypeStruct((s_len, N_IN), MXU_DTYPE), jax.ShapeDtypeStruct((256, 256), f32),
                   jax.ShapeDtypeStruct((1, 256), f32)],
        compiler_params=_cparams(("arbitrary",)),
    )(z, z, dm, dm, wbd, scale, *extra)


def _tri_masks():
    qi = lax.broadcasted_iota(jnp.int32, (BLK, BLK), 0)
    ki = lax.broadcasted_iota(jnp.int32, (BLK, BLK), 1)
    return qi >= ki, ki >= qi


ATTN_SUPER_PER_STEP = (8, 4, 1)
Q_COL, K_COL, V_COL = POOL_WIDTH // 128, (POOL_WIDTH + ATTN_WIDTH) // 128, (POOL_WIDTH + 2 * ATTN_WIDTH) // 128


def _rows(ref, start, dil):
    if dil == 1:
        return ref[pl.ds(start, BLK), :]
    return ref[pl.ds(start, BLK, stride=dil), :]


ATTN_BLOCKS_TOGETHER = 8


def _set_rows(ref, start, dil, val):
    if dil == 1:
        ref[pl.ds(start, BLK), :] = val
    else:
        ref[pl.ds(start, BLK, stride=dil), :] = val


def _attn_fwd(z, g, prev, name):
    s_len = z.shape[0]
    dil, m = DILATIONS[g], ATTN_SUPER_PER_STEP[g]
    sbr = BLK * dil
    rows = sbr * m

    def body(*refs):
        q_ref, kc_ref, kp_ref, vc_ref, vp_ref = refs[:5]
        o_ref, l_ref = refs[-2:]
        st = pl.program_id(0)
        low, up = _tri_masks()
        head0 = lax.broadcasted_iota(jnp.int32, (BLK, 128), 1) < HEAD_DIM
        blocks = [(sb, r) for sb in range(m) for r in range(dil)]
        for g0 in range(0, len(blocks), ATTN_BLOCKS_TOGETHER):
            grp = blocks[g0:g0 + ATTN_BLOCKS_TOGETHER]
            loaded = []
            for sb, r in grp:
                base = sb * sbr + r
                if sb == 0:
                    kp, vp = _rows(kp_ref, r, dil), _rows(vp_ref, r, dil)
                else:
                    kp, vp = _rows(kc_ref, base - sbr, dil), _rows(vc_ref, base - sbr, dil)
                qs = _rows(q_ref, base, dil) * ATTN_SCALE
                loaded.append((_mx(jnp.where(head0, qs, 0.0)), _mx(jnp.where(head0, 0.0, qs)),
                               jnp.concatenate([_mx(kp), _mx(_rows(kc_ref, base, dil))], axis=0),
                               jnp.concatenate([_mx(vp), _mx(_rows(vc_ref, base, dil))], axis=0)))
            scores = [(_dot_nt(q0, k2), _dot_nt(q1, k2)) for q0, q1, k2, _ in loaded]
            soft = []
            for (sb, _), pair in zip(grp, scores):
                valid = jnp.concatenate([up & (st > 0) if sb == 0 else up, low], axis=1)
                heads = []
                for s in pair:
                    s = jnp.where(valid, s, NEG_BIG)
                    mx = jnp.max(s, axis=-1, keepdims=True)
                    e = jnp.exp(s - mx)
                    l = jnp.sum(e, axis=-1, keepdims=True)
                    heads.append((_mx(e / l), jnp.broadcast_to(mx + jnp.log(l), (BLK, 128))))
                soft.append(heads)
            for (sb, r), heads, (_, _, _, v2) in zip(grp, soft, loaded):
                base = sb * sbr + r
                _set_rows(o_ref, base, dil, jnp.where(head0, _dot(heads[0][0], v2), _dot(heads[1][0], v2)))
                _set_rows(l_ref, base, dil, jnp.where(head0, heads[0][1], heads[1][1]))

    def cur(col):
        return pl.BlockSpec((rows, 128), lambda st, hp: (st, col + 2 * g + hp))

    def before(col):
        return pl.BlockSpec((sbr, 128), lambda st, hp: (jnp.maximum(st * m - 1, 0), col + 2 * g + hp))

    in_specs = [cur(Q_COL), cur(K_COL), before(K_COL), cur(V_COL), before(V_COL)]
    args = [z, z, z, z, z]
    aliases = {}
    if prev is not None:
        in_specs += [pl.BlockSpec(memory_space=pl.ANY)] * 2
        args += list(prev)
        aliases = {5: 0, 6: 1}
    return pl.pallas_call(
        body, name=name, grid=(s_len // rows, 2), in_specs=in_specs, out_specs=[cur(0), cur(0)],
        out_shape=[jax.ShapeDtypeStruct((s_len, ATTN_WIDTH), f32)] * 2, input_output_aliases=aliases,
        compiler_params=_cparams(("parallel", "parallel")),
    )(*args)


def _stack_heads(x, head0):
    return jnp.concatenate([_mx(jnp.where(head0, x, 0.0)), _mx(jnp.where(head0, 0.0, x))], axis=0)


def _head_rows(x):
    xt = x.T
    return jnp.concatenate([jnp.broadcast_to(xt[0:1, :], (BLK, BLK)),
                            jnp.broadcast_to(xt[HEAD_DIM:HEAD_DIM + 1, :], (BLK, BLK))], axis=0)


def _attn_bwd(z, do, lse, dlt, tabs, dz, g, name):
    s_len = z.shape[0]
    dil, m = DILATIONS[g], ATTN_SUPER_PER_STEP[g]
    sbr = BLK * dil
    rows = sbr * m
    nsteps = s_len // rows

    def body(q_ref, qn_ref, kc_ref, kp_ref, vc_ref, vp_ref, do_ref, don_ref, l_ref, ln_ref, d_ref, dn_ref,
             c_ref, s1_ref, s2_ref, dz_in, dz_ref, dq_buf, dk_buf, dv_buf, out_buf, sems):
        del dz_in
        st, hp = pl.program_id(0), pl.program_id(1)
        head0 = lax.broadcasted_iota(jnp.int32, (BLK, 128), 1) < HEAD_DIM
        key_i = lax.broadcasted_iota(jnp.int32, (2 * BLK, BLK), 0) & (BLK - 1)
        query_i = lax.broadcasted_iota(jnp.int32, (2 * BLK, BLK), 1)
        same_t, cross_t = query_i >= key_i, key_i >= query_i
        def load(r):
            keys, vals = [_stack_heads(_rows(kp_ref, r, dil), head0)], [_stack_heads(_rows(vp_ref, r, dil), head0)]
            qs, dos, lses, dlts = [], [], [], []
            for sb in range(m):
                base = sb * sbr + r
                keys.append(_stack_heads(_rows(kc_ref, base, dil), head0))
                vals.append(_stack_heads(_rows(vc_ref, base, dil), head0))
                qs.append(_mx(_rows(q_ref, base, dil)))
                dos.append(_mx(_rows(do_ref, base, dil)))
                lses.append(_head_rows(_rows(l_ref, base, dil)))
                dlts.append(_head_rows(_rows(d_ref, base, dil)))
            qs.append(_mx(_rows(qn_ref, r, dil)))
            dos.append(_mx(_rows(don_ref, r, dil)))
            lses.append(_head_rows(_rows(ln_ref, r, dil)))
            dlts.append(_head_rows(_rows(dn_ref, r, dil)))
            return keys, vals, qs, dos, lses, dlts

        def products(data):
            keys, vals, qs, dos, _, _ = data
            return ([(_dot_nt(keys[j + 1], qs[j]), _dot_nt(vals[j + 1], dos[j])) for j in range(m)],
                    [(_dot_nt(keys[j], qs[j]), _dot_nt(vals[j], dos[j])) for j in range(m + 1)])

        def finish(data, raw):
            lses, dlts = data[4], data[5]

            def one(pair, j, valid):
                p = jnp.where(valid, jnp.exp(pair[0] * ATTN_SCALE - lses[j]), 0.0)
                return _mx(p), _mx(p * (pair[1] - dlts[j]) * ATTN_SCALE)

            same = [one(raw[0][j], j, same_t) for j in range(m)]
            cross = [one(raw[1][j], j, cross_t & (st > 0) if j == 0 else
                         (cross_t & (st < nsteps - 1) if j == m else cross_t)) for j in range(m + 1)]
            return same, cross

        def gradients(r, data, fin):
            keys, _, qs, dos, _, _ = data
            same, cross = fin
            for sb in range(m):
                base = sb * sbr + r
                (p_a, ds_a), (_, ds_x), (p_n, ds_n) = same[sb], cross[sb], cross[sb + 1]
                dq = _dot_tn(ds_a, keys[sb + 1]) + _dot_tn(ds_x, keys[sb])
                dk2 = _dot(ds_a, qs[sb]) + _dot(ds_n, qs[sb + 1])
                dv2 = _dot(p_a, dos[sb]) + _dot(p_n, dos[sb + 1])
                _set_rows(dq_buf, base, dil, dq)
                _set_rows(dk_buf, base, dil, jnp.where(head0, dk2[:BLK], dk2[BLK:]))
                _set_rows(dv_buf, base, dil, jnp.where(head0, dv2[:BLK], dv2[BLK:]))

        def residue_group(rg, carry):
            rs = [rg * group + i for i in range(group)]
            data = [load(r) for r in rs]
            raws = [products(d) for d in data]
            fins = [finish(d, raw) for d, raw in zip(data, raws)]
            for r, d, fin in zip(rs, data, fins):
                gradients(r, d, fin)
            return carry

        group = max(1, min(dil, ATTN_BLOCKS_TOGETHER // m))
        if dil // group <= 2:
            for rg in range(dil // group):
                residue_group(rg, 0)
        else:
            lax.fori_loop(0, dil // group, residue_group, 0)
        copies = []
        for t, (buf, col) in enumerate(((dq_buf, Q_COL), (dk_buf, K_COL), (dv_buf, V_COL))):
            val = buf[...]
            if t < 2:
                val = _rope_transpose(val, c_ref[...], s1_ref[...], s2_ref[...], 128)
            out_buf[t] = val.astype(out_buf.dtype)
            lane0 = pl.multiple_of((col + 2 * g + hp) * 128, 128)
            dst = dz_ref.at[pl.ds(pl.multiple_of(st * rows, rows), rows), pl.ds(lane0, 128)]
            cp = pltpu.make_async_copy(out_buf.at[t], dst, sems.at[t])
            cp.start()
            copies.append(cp)
        for cp in copies:
            cp.wait()

    def cur(col):
        return pl.BlockSpec((rows, 128), lambda st, hp: (st, col + 2 * g + hp))

    def before(col):
        return pl.BlockSpec((sbr, 128), lambda st, hp: (jnp.maximum(st * m - 1, 0), col + 2 * g + hp))

    def after(col):
        return pl.BlockSpec((sbr, 128), lambda st, hp: (jnp.minimum((st + 1) * m, s_len // sbr - 1), col + 2 * g + hp))

    tab = pl.BlockSpec((rows, 128), lambda st, hp: (st, 0))
    return pl.pallas_call(
        body, name=name, grid=(nsteps, 2),
        in_specs=[cur(Q_COL), after(Q_COL), cur(K_COL), before(K_COL), cur(V_COL), before(V_COL),
                  cur(0), after(0), cur(0), after(0), cur(0), after(0), tab, tab, tab,
                  pl.BlockSpec(memory_space=pl.ANY)],
        out_specs=pl.BlockSpec(memory_space=pl.ANY),
        out_shape=jax.ShapeDtypeStruct(dz.shape, dz.dtype), input_output_aliases={15: 0},
        scratch_shapes=[pltpu.VMEM((rows, 128), f32)] * 3 + [pltpu.VMEM((3, rows, 128), dz.dtype),
                                                            pltpu.SemaphoreType.DMA((3,))],
        compiler_params=_cparams(("arbitrary", "arbitrary")),
    )(z, z, z, z, z, z, do, do, lse, lse, dlt, dlt, *tabs, dz)


def _rope_tables(positions):
    inv_freq = ROPE_THETA ** (-jnp.arange(0, ROT_DIM, 2, dtype=f32) / ROT_DIM)
    ang = positions.astype(f32)[:, None] * inv_freq
    cos, sin = jnp.cos(ang), jnp.sin(ang)
    s_len = positions.shape[0]
    zero8, rest = jnp.zeros((s_len, 8), f32), jnp.zeros((s_len, HEAD_DIM - ROT_DIM), f32)
    ones = jnp.ones((s_len, HEAD_DIM - ROT_DIM), f32)
    c = jnp.concatenate([cos, cos, ones] * 2, axis=1)
    s1 = jnp.concatenate([-sin, zero8, rest] * 2, axis=1)
    s2 = jnp.concatenate([zero8, sin, rest] * 2, axis=1)
    return c, s1, s2


def _block_diag(pool_w):
    out = jnp.zeros((POOL_WIDTH, POOL_WIDTH), pool_w.dtype)
    for g in range(4):
        out = lax.dynamic_update_slice(out, pool_w[g], (g * POOL_GC, g * POOL_GC))
    return out


def _layer_fwd(h, p, wsrc, small, layer, tabs, head=None):
    nm = f"l{layer}_"
    wts, wl = wsrc.take(layer, ("w_in",), (h,) if layer else tuple(tabs))
    z, hn1 = _norm_matmul(h, small["norm1"][layer][None], wts["w_in"], wl, 256, nm + "in_proj", rope=tabs)
    ol = None
    for g in range(3):
        ol = _attn_fwd(z, g, ol, nm + f"attn_fwd{g}")
    outs, lses = ol
    wbd = _mx(_block_diag(small["pool_w"][layer]))
    scale = small["pool_scale"][layer][None]
    wts.update(wsrc.take(layer, ("w_out",), (outs,))[0])
    m, h1 = _mixer_out_proj(z, wbd, scale, outs, lses, wts["w_out"], wl, h, nm + "mixer_out")
    wts.update(wsrc.take(layer, ("w_up", "w_down"), (h1,))[0])
    h2, a, hn2 = _mlp_fwd(h1, small["norm2"][layer][None], wts["w_up"], wts["w_down"], wl, nm + "mlp")
    wts.update(wsrc.take(layer, ("w_gate", "w_ple"), (h2,))[0])
    *h3, gl, hn3 = _gate_ple_fwd(h2, small["norm3"][layer][None], wts["w_gate"], wts["w_ple"], wl, p, layer,
                                 nm + "gate_ple", head=head)
    saved = dict(h=h, z=z, hn1=hn1, outs=outs, lses=lses, wbd=wbd, scale=scale, m=m, h1=h1, a=a, hn2=hn2, h2=h2,
                 gl=gl, hn3=hn3, wts=wts, wl=wl)
    return h3, saved


def _layer_bwd(dh3, sv, p, small, layer, tabs128, reducer):
    nm = f"l{layer}_"
    wts, wl = sv["wts"], sv["wl"]
    dh2, dg3, de, dgl = _gate_bwd(dh3, sv["gl"], p, layer, wts["w_ple"], wts["w_gate"], wl, sv["h2"],
                                  small["norm3"][layer][None], nm + "gate_bwd")
    reducer.add("w_gate", layer, _weight_grad(sv["hn3"], dgl, nm + "dw_gate"))
    reducer.add("w_ple", layer, _weight_grad(p, de, nm + "dw_ple", lead=layer))
    dh1, dg2, da = _mlp_bwd(dh2, wts["w_down"], wts["w_up"], wl, sv["a"], sv["h1"], small["norm2"][layer][None],
                            nm + "mlp_bwd")
    reducer.add("w_down", layer, _weight_grad(sv["a"], dh2, nm + "dw_down", act=True))
    started = reducer.add("w_up", layer, _weight_grad(sv["hn2"], da, nm + "dw_up"))
    dpool, do, dlt = _out_combine_bwd(dh1, wts["w_out"], wl, sv["outs"], sv["lses"], nm + "out_bwd", after=started)
    started = reducer.add("w_out", layer, _weight_grad(sv["m"], dh1, nm + "dw_out"))
    dz, dwbd, dscale = _pool_bwd(sv["z"], dpool, sv["wbd"], sv["scale"], nm + "pool_bwd", after=started)
    for g in range(3):
        dz = _attn_bwd(sv["z"], do, sv["lses"], dlt, tabs128, dz, g, nm + f"attn_bwd{g}")
    started = reducer.add("w_in", layer, _weight_grad(sv["hn1"], dz, nm + "dw_in"))
    dh0, dg1 = _matmul_nt_norm_bwd(dz, wts["w_in"], wl, sv["h"], small["norm1"][layer][None], dh1, nm + "in_bwd",
                                   tk=512, after=started)
    dpool_w = jnp.stack([dwbd[g * POOL_GC:(g + 1) * POOL_GC, g * POOL_GC:(g + 1) * POOL_GC] for g in range(4)])
    sg = dict(norm1=dg1[0], norm2=dg2[0], norm3=dg3[0], pool_w=dpool_w, pool_scale=dscale[0])
    return dh0, sg


def _local_step(x, p, positions, wsrc, small, target, reducer):
    tabs128 = _rope_tables(positions)
    (h,), sv0 = _layer_fwd(x, p, wsrc, small, 0, tabs128)
    (loss, dh, dgf), sv1 = _layer_fwd(h, p, wsrc, small, 1, tabs128, head=(small["final_norm"][None], target))
    saved = [sv0, sv1]
    sgs = [None, None]
    for layer in (1, 0):
        dh, sgs[layer] = _layer_bwd(dh, saved[layer], p, small, layer, tabs128, reducer)
    small_grads = {k: jnp.stack([sgs[0][k], sgs[1][k]]) for k in sgs[0]}
    small_grads["final_norm"] = dgf[0]
    return loss, dh, small_grads


HBM = pl.BlockSpec(memory_space=pltpu.HBM)


def _my_place():
    return lax.axis_index("x"), lax.axis_index("y"), lax.axis_index("c")


def _other_chips(x, y):
    return [(1 - x, y), (x, 1 - y), (1 - x, 1 - y)]


def _window(ref, name, chip):
    k, n = _shard_shape(name)
    if COL_SHARDED[name]:
        return ref.at[:, pl.ds(pl.multiple_of(chip * n, 128), n)]
    return ref.at[pl.ds(pl.multiple_of(chip * k, 128), k), :]


def _chip_index():
    return jnp.reshape(2 * lax.axis_index("x") + lax.axis_index("y"), (1,)).astype(jnp.int32)


def _shard_block(name, tr):
    ks, ns = _shard_shape(name)
    if COL_SHARDED[name]:
        return (tr, ns), lambda i, me: (i, me[0])
    return (tr, ns), lambda i, me: (me[0] * (ks // tr) + i, 0)


def _place_shard(w, name, layer):
    ks, ns = _shard_shape(name)
    tr = min(ks, 256)
    shape, index = _shard_block(name, tr)

    def body(me_ref, w_ref, o_ref):
        o_ref[...] = w_ref[...].astype(o_ref.dtype)

    return pl.pallas_call(
        body, name=f"place_{name}{layer}",
        grid_spec=pltpu.PrefetchScalarGridSpec(
            num_scalar_prefetch=1, grid=(ks // tr,),
            in_specs=[pl.BlockSpec((None, tr, ns), lambda i, me: (layer, i, 0))],
            out_specs=pl.BlockSpec((None,) + shape, lambda i, me: (0,) + index(i, me))),
        out_shape=jax.ShapeDtypeStruct((1,) + FULL_SHAPE[name], MXU_DTYPE),
        compiler_params=_cparams(("parallel",)),
    )(_chip_index(), w)


GATHER_ORDER = [("w_in", 0), ("w_out", 0), ("w_up", 0), ("w_down", 0), ("w_gate", 0), ("w_ple", 0),
                ("w_in", 1), ("w_out", 1), ("w_up", 1), ("w_down", 1), ("w_gate", 1), ("w_ple", 1)]
SEM = pl.BlockSpec(memory_space=pltpu.SEMAPHORE)
EFFECT = pltpu.SideEffectType.DATAFLOW_SIDE_EFFECTING


def _gather_copy(src_ref, dst_ref, name, idx, j, chip, send_sems, recv_sems, c):
    cx, cy = chip
    return pltpu.make_async_remote_copy(
        src_ref=src_ref, dst_ref=dst_ref, send_sem=send_sems.at[3 * idx + j], recv_sem=recv_sems.at[3 * idx + j],
        device_id=(cx, cy, c), device_id_type=MESH)


def _gather_start(placed, order, tag, after=None):
    n = len(order)
    extra = [] if after is None else [after]

    def body(*refs):
        ins = refs[:n]
        k = n + len(extra)
        send_sems, recv_sems = refs[k], refs[k + 1]
        outs = refs[k + 2:k + 2 + n]
        token = refs[-1]
        x, y, c = _my_place()
        me = 2 * x + y
        for idx, (name, _) in enumerate(order):
            for j, chip in enumerate(_other_chips(x, y)):
                _gather_copy(_window(ins[idx].at[0], name, me), _window(outs[idx].at[0], name, me), name, idx, j, chip,
                             send_sems, recv_sems, c).start()
        token[...] = jnp.zeros_like(token)

    res = pl.pallas_call(
        body, name="gather_start" + tag,
        out_shape=(pltpu.SemaphoreType.DMA((3 * n,)), pltpu.SemaphoreType.DMA((3 * n,)))
        + tuple(pltpu.HBM(a.shape, a.dtype) for a in placed) + (jax.ShapeDtypeStruct((8, 128), f32),),
        in_specs=[HBM] * n + [pl.BlockSpec(memory_space=pl.ANY)] * len(extra),
        out_specs=(SEM, SEM) + (HBM,) * n + (pl.BlockSpec(memory_space=pltpu.VMEM),),
        input_output_aliases={i: i + 2 for i in range(n)},
        compiler_params=pltpu.CompilerParams(has_side_effects=EFFECT),
    )(*[pltpu.with_memory_space_constraint(a, pltpu.HBM) for a in placed], *extra)
    return res[0], res[1], list(res[2:2 + n]), res[-1]


def _gather_wait(send_sems, recv_sems, arrays, order, idxs, after, name):
    n = len(idxs)

    def body(*refs):
        ins = refs[:n]
        send_ref, recv_ref = refs[n], refs[n + 1]
        x, y, c = _my_place()
        me = 2 * x + y
        for k, idx in enumerate(idxs):
            wname = order[idx][0]
            for j, chip in enumerate(_other_chips(x, y)):
                cx, cy = chip
                mine = _window(ins[k].at[0], wname, me)
                land = _window(ins[k].at[0], wname, 2 * cx + cy)
                _gather_copy(mine, mine, wname, idx, j, chip, send_ref, recv_ref, c).wait_send()
                _gather_copy(land, land, wname, idx, j, chip, send_ref, recv_ref, c).wait_recv()

    operands = list(arrays) + [send_sems, recv_sems] + list(after)
    in_specs = [HBM] * n + [SEM, SEM] + [pl.BlockSpec(memory_space=pl.ANY)] * len(after)
    res = pl.pallas_call(
        body, name=name, out_shape=tuple(pltpu.HBM(a.shape, a.dtype) for a in arrays),
        in_specs=in_specs, out_specs=(HBM,) * n, input_output_aliases={i: i for i in range(n)},
        compiler_params=pltpu.CompilerParams(has_side_effects=EFFECT),
    )(*operands)
    return list(res)


class _GatheredWeights:
    def __init__(self, shards):
        self.starts = []
        token = None
        for tag, order in (("_first", GATHER_ORDER[:1]), ("_rest", GATHER_ORDER[1:])):
            placed = [_place_shard(shards[name], name, layer) for name, layer in order]
            self.starts.append((order,) + _gather_start(placed, order, tag, token))
            token = self.starts[-1][-1]

    def take(self, layer, names, after):
        order, send, recv, arrays, _ = next(s for s in self.starts if (names[0], layer) in s[0])
        after = list(after)
        if order is self.starts[0][0]:
            after.append(self.starts[-1][-1])
        idxs = [order.index((n, layer)) for n in names]
        got = _gather_wait(send, recv, [arrays[i] for i in idxs], order, idxs, after, f"gather_wait{layer}_{names[0]}")
        return dict(zip(names, got)), 0


N_DEV = 8


def _reduce_copies(dws, lands, names, layer, send_sems, recv_sems):
    x, y, c = _my_place()
    me, my_dev = 2 * x + y, 4 * x + 2 * y + c
    out = []
    for t, name in enumerate(names):
        for j, (cx, cy) in enumerate(_other_chips(x, y)):
            out.append((pltpu.make_async_remote_copy(
                src_ref=_window(dws[t], name, 2 * cx + cy), dst_ref=lands[t].at[my_dev],
                send_sem=send_sems.at[4 * t + j], recv_sem=recv_sems.at[N_DEV * t + my_dev],
                device_id=(cx, cy, layer), device_id_type=MESH), False))
        out.append((pltpu.make_async_remote_copy(
            src_ref=_window(dws[t], name, me), dst_ref=lands[t].at[my_dev],
            send_sem=send_sems.at[4 * t + 3], recv_sem=recv_sems.at[N_DEV * t + my_dev],
            device_id=(x, y, layer), device_id_type=MESH), True))
    return out


def _reduce_start(dws, names, layer, tag):
    n = len(names)
    lands = [lax.empty((N_DEV,) + _shard_shape(nm), dws[0].dtype) for nm in names]

    def body(*refs):
        ins = refs[:n]
        send_sems, recv_sems = refs[2 * n], refs[2 * n + 1]
        land_out = refs[3 * n + 2:4 * n + 2]
        token = refs[-1]
        c = lax.axis_index("c")
        for cp, non_owner_only in _reduce_copies(ins, land_out, names, layer, send_sems, recv_sems):
            if non_owner_only:
                @pl.when(c != layer)
                def _():
                    cp.start()
            else:
                cp.start()
        token[...] = jnp.zeros_like(token)

    res = pl.pallas_call(
        body, name="reduce_start" + tag,
        out_shape=(pltpu.SemaphoreType.DMA((4 * n,)), pltpu.SemaphoreType.DMA((N_DEV * n,)))
        + tuple(pltpu.HBM(a.shape, a.dtype) for a in dws) + tuple(pltpu.HBM(a.shape, a.dtype) for a in lands)
        + (jax.ShapeDtypeStruct((8, 128), f32),),
        in_specs=[HBM] * (2 * n),
        out_specs=(SEM, SEM) + (HBM,) * (2 * n) + (pl.BlockSpec(memory_space=pltpu.VMEM),),
        input_output_aliases={i: i + 2 for i in range(2 * n)},
        compiler_params=pltpu.CompilerParams(has_side_effects=EFFECT),
    )(*[pltpu.with_memory_space_constraint(a, pltpu.HBM) for a in list(dws) + lands])
    return res[0], res[1], list(res[2:2 + n]), list(res[2 + n:2 + 2 * n]), res[-1]


def _reduce_wait(send_sems, recv_sems, dws, lands, names, layer, after, tag):
    n = len(names)

    def body(*refs):
        ins, land_in = refs[:n], refs[n:2 * n]
        send_ref, recv_ref = refs[2 * n], refs[2 * n + 1]
        x, y, c = _my_place()
        for cp, non_owner_only in _reduce_copies(ins, land_in, names, layer, send_ref, recv_ref):
            if non_owner_only:
                @pl.when(c != layer)
                def _():
                    cp.wait_send()
            else:
                cp.wait_send()

        @pl.when(c == layer)
        def _():
            for t in range(n):
                for k in range(1, N_DEV):
                    px, py, pc = x ^ ((k >> 2) & 1), y ^ ((k >> 1) & 1), c ^ (k & 1)
                    dev = 4 * px + 2 * py + pc
                    land = land_in[t].at[dev]
                    pltpu.make_async_remote_copy(
                        src_ref=land, dst_ref=land, send_sem=send_ref.at[4 * t], recv_sem=recv_ref.at[N_DEV * t + dev],
                        device_id=(px, py, pc), device_id_type=MESH).wait_recv()

    res = pl.pallas_call(
        body, name="reduce_wait" + tag,
        out_shape=tuple(pltpu.HBM(a.shape, a.dtype) for a in list(dws) + list(lands)),
        in_specs=[HBM] * (2 * n) + [SEM, SEM, pl.BlockSpec(memory_space=pl.ANY)], out_specs=(HBM,) * (2 * n),
        input_output_aliases={i: i for i in range(2 * n)},
        compiler_params=pltpu.CompilerParams(has_side_effects=EFFECT),
    )(*dws, *lands, send_sems, recv_sems, after)
    return list(res[:n]), list(res[n:])


def _sum_devices(land, own, name, layer, prev):
    ks, ns = _shard_shape(name)
    tr = min(ks, 256)
    shape, index = _shard_block(name, tr)

    def body(me_ref, dev_ref, *refs):
        s_ref, own_ref, out_ref = refs[0], refs[1], refs[-1]
        dev = dev_ref[0]
        acc = None
        for s in range(N_DEV):
            term = jnp.where(dev == s, own_ref[...], s_ref[s]).astype(f32)
            acc = term if acc is None else acc + term
        out_ref[...] = acc

    def mine(i, dev):
        return i * jnp.where((dev[0] & 1) == layer, 1, 0)

    in_specs = [pl.BlockSpec((N_DEV, tr, ns), lambda i, me, dev: (0, mine(i, dev), 0)),
                pl.BlockSpec(shape, lambda i, me, dev: index(mine(i, dev), me))]
    args = [land, own]
    aliases = {}
    if prev is not None:
        in_specs.append(pl.BlockSpec(memory_space=pl.ANY))
        args.append(prev)
        aliases = {4: 0}
    x, y, c = _my_place()
    return pl.pallas_call(
        body, name=f"sum_devices_{name}{layer}",
        grid_spec=pltpu.PrefetchScalarGridSpec(
            num_scalar_prefetch=2, grid=(ks // tr,), in_specs=in_specs,
            out_specs=pl.BlockSpec((None, tr, ns), lambda i, me, dev: (layer, mine(i, dev), 0))),
        out_shape=jax.ShapeDtypeStruct((2, ks, ns), f32), input_output_aliases=aliases,
        compiler_params=_cparams(("arbitrary",)),
    )(_chip_index(), jnp.reshape(4 * x + 2 * y + c, (1,)).astype(jnp.int32), *args)


class _GradReducer:
    GROUPS = (("1", 1, ("w_gate", "w_ple", "w_down", "w_up", "w_out", "w_in")),
              ("0a", 0, ("w_gate", "w_ple", "w_down", "w_up")),
              ("0b", 0, ("w_out",)),
              ("0c", 0, ("w_in",)))

    def __init__(self):
        self.grads = {}
        self.started = {}

    def add(self, name, layer, dw):
        self.grads[(name, layer)] = dw
        token = None
        for tag, glayer, names in self.GROUPS:
            if tag not in self.started and all((nm, glayer) in self.grads for nm in names):
                *self.started[tag], token = _reduce_start([self.grads[(nm, glayer)] for nm in names], names, glayer, tag)
        return token

    def finish(self, after):
        mine = {}
        for tag, layer, names in self.GROUPS:
            send, recv, dws, lands = self.started[tag]
            dws, lands = _reduce_wait(send, recv, dws, lands, names, layer, after, tag)
            for nm, dw, land in zip(names, dws, lands):
                mine[nm] = _sum_devices(land, dw, nm, layer, mine.get(nm))
        return _pair_layers(mine)


def _pair_layers(mine):
    names = list(BIG)

    def body(*refs):
        ins = refs[:len(names)]
        outs = refs[len(names):2 * len(names)]
        send_sems, recv_sems = refs[2 * len(names):]
        x, y, c = _my_place()
        sibling = (x, y, 1 - c)
        cps = []
        for t in range(len(names)):
            cp = pltpu.make_async_remote_copy(
                src_ref=ins[t].at[c], dst_ref=outs[t].at[c], send_sem=send_sems.at[t], recv_sem=recv_sems.at[t],
                device_id=sibling, device_id_type=MESH)
            cp.start()
            cps.append(cp)
        for t in range(len(names)):
            cps[t].wait_send()
            land = outs[t].at[1 - c]
            pltpu.make_async_remote_copy(
                src_ref=land, dst_ref=land, send_sem=send_sems.at[t], recv_sem=recv_sems.at[t],
                device_id=sibling, device_id_type=MESH).wait_recv()

    outs = pl.pallas_call(
        body, name="pair_layers", in_specs=[HBM] * len(names), out_specs=[HBM] * len(names),
        out_shape=[jax.ShapeDtypeStruct((2,) + _shard_shape(n), f32) for n in names],
        input_output_aliases={t: t for t in range(len(names))},
        scratch_shapes=[pltpu.SemaphoreType.DMA((len(names),)), pltpu.SemaphoreType.DMA((len(names),))],
    )(*[mine[n] for n in names])
    return dict(zip(names, outs))


SMALL_ROWS = 320


def _small_copies(vec_ref, land_ref, send_sems, recv_sems):
    x, y, c = _my_place()
    me = 4 * x + 2 * y + c
    out = []
    for k in range(1, N_DEV):
        peer = (x ^ ((k >> 2) & 1), y ^ ((k >> 1) & 1), c ^ (k & 1))
        src_dev = 4 * peer[0] + 2 * peer[1] + peer[2]
        send = pltpu.make_async_remote_copy(
            src_ref=vec_ref, dst_ref=land_ref.at[me], send_sem=send_sems.at[k - 1], recv_sem=recv_sems.at[k - 1],
            device_id=peer, device_id_type=MESH)
        arrival = pltpu.make_async_remote_copy(
            src_ref=land_ref.at[src_dev], dst_ref=land_ref.at[src_dev], send_sem=send_sems.at[k - 1],
            recv_sem=recv_sems.at[k - 1], device_id=peer, device_id_type=MESH)
        out.append((send, arrival))
    return out


def _small_start(vec):
    land = lax.empty((N_DEV,) + vec.shape, vec.dtype)

    def body(v_ref, land_in, send_sems, recv_sems, v_out, land_out):
        del land_in, v_out
        for send, _ in _small_copies(v_ref, land_out, send_sems, recv_sems):
            send.start()

    return pl.pallas_call(
        body, name="small_start",
        out_shape=(pltpu.SemaphoreType.DMA((N_DEV - 1,)), pltpu.SemaphoreType.DMA((N_DEV - 1,)),
                   pltpu.HBM(vec.shape, vec.dtype), pltpu.HBM(land.shape, land.dtype)),
        in_specs=[HBM, HBM], out_specs=(SEM, SEM, HBM, HBM), input_output_aliases={0: 2, 1: 3},
        compiler_params=pltpu.CompilerParams(has_side_effects=EFFECT),
    )(pltpu.with_memory_space_constraint(vec, pltpu.HBM), pltpu.with_memory_space_constraint(land, pltpu.HBM))


def _small_wait(send_sems, recv_sems, vec, land, after):
    def body(v_ref, land_ref, send_ref, recv_ref, after_ref, v_out, land_out):
        del after_ref, v_out, land_out
        for send, arrival in _small_copies(v_ref, land_ref, send_ref, recv_ref):
            send.wait_send()
            arrival.wait_recv()

    return pl.pallas_call(
        body, name="small_wait", out_shape=(pltpu.HBM(vec.shape, vec.dtype), pltpu.HBM(land.shape, land.dtype)),
        in_specs=[HBM, HBM, SEM, SEM, pl.BlockSpec(memory_space=pl.ANY)], out_specs=(HBM, HBM),
        input_output_aliases={0: 0, 1: 1}, compiler_params=pltpu.CompilerParams(has_side_effects=EFFECT),
    )(vec, land, send_sems, recv_sems, after)


def _small_sum(vec, land):
    x, y, c = _my_place()

    def body(dev_ref, v_ref, land_ref, out_ref):
        acc = None
        for s in range(N_DEV):
            term = jnp.where(dev_ref[0] == s, v_ref[...], land_ref[s])
            acc = term if acc is None else acc + term
        out_ref[...] = acc

    return pl.pallas_call(
        body, name="small_sum",
        grid_spec=pltpu.PrefetchScalarGridSpec(
            num_scalar_prefetch=1, grid=(1,),
            in_specs=[pl.BlockSpec(vec.shape, lambda i, dev: (0, 0)), pl.BlockSpec(land.shape, lambda i, dev: (0, 0, 0))],
            out_specs=pl.BlockSpec(vec.shape, lambda i, dev: (0, 0))),
        out_shape=jax.ShapeDtypeStruct(vec.shape, vec.dtype),
        compiler_params=_cparams(("arbitrary",)),
    )(jnp.reshape(4 * x + 2 * y + c, (1,)).astype(jnp.int32), vec, land)


def _adamw(w, g, m, v, name):
    rows, cols = w.shape
    tr = rows
    for cand in (512, 256, 128, 64, 32, 16, 8):
        if rows % cand == 0 and cand * cols * 4 <= 2 * 1024 * 1024:
            tr = cand
            break
    c1 = np.float32(1.0 - ADAM_B1 ** ADAM_STEP)
    c2 = np.float32(1.0 - ADAM_B2 ** ADAM_STEP)

    def body(w_ref, g_ref, m_ref, v_ref, go_ref, d_ref, mo_ref, vo_ref):
        gv = g_ref[...]
        go_ref[...] = gv
        mn = ADAM_B1 * m_ref[...] + (1.0 - ADAM_B1) * gv
        vn = ADAM_B2 * v_ref[...] + (1.0 - ADAM_B2) * (gv * gv)
        mo_ref[...] = mn
        vo_ref[...] = vn
        d_ref[...] = -ADAM_LR * ((mn / c1) / (jnp.sqrt(vn / c2) + ADAM_EPS) + ADAM_WD * w_ref[...])

    blk = pl.BlockSpec((tr, cols), lambda i: (i, 0))
    return pl.pallas_call(
        body, name="adamw_" + name, grid=(rows // tr,), in_specs=[blk] * 4, out_specs=[blk] * 4,
        out_shape=[jax.ShapeDtypeStruct((rows, cols), f32)] * 4,
        compiler_params=_cparams(("parallel",)),
    )(w, g, m, v)


SMALL = ("norm1", "pool_w", "pool_scale", "norm2", "norm3", "final_norm")
ORDER = ("norm1", "w_in", "pool_w", "pool_scale", "w_out", "norm2", "w_up", "w_down", "norm3", "w_gate", "w_ple",
         "final_norm")


def _pack_small(tree, extra=None):
    parts = [tree[n].reshape(-1) for n in SMALL]
    if extra is not None:
        parts.append(extra.reshape(-1))
    flat = jnp.concatenate(parts)
    return jnp.pad(flat, (0, SMALL_ROWS * 128 - flat.shape[0])).reshape(SMALL_ROWS, 128)


def _unpack_small(packed, like):
    flat = packed.reshape(-1)
    out, off = {}, 0
    for n in SMALL:
        size = int(np.prod(like[n].shape))
        out[n] = flat[off:off + size].reshape(like[n].shape)
        off += size
    return out, flat[off]


def kernel(x, p, positions, norm1, w_in, pool_w, pool_scale, w_out, norm2, w_up, w_down, norm3, w_gate, w_ple, final_norm, loss_target, m_norm1, m_w_in, m_pool_w, m_pool_scale, m_w_out, m_norm2, m_w_up, m_w_down, m_norm3, m_w_gate, m_w_ple, m_final_norm, v_norm1, v_w_in, v_pool_w, v_pool_scale, v_w_out, v_norm2, v_w_up, v_w_down, v_norm3, v_w_gate, v_w_ple, v_final_norm):
    w = dict(norm1=norm1, w_in=w_in, pool_w=pool_w, pool_scale=pool_scale, w_out=w_out, norm2=norm2, w_up=w_up,
             w_down=w_down, norm3=norm3, w_gate=w_gate, w_ple=w_ple, final_norm=final_norm)
    m = dict(norm1=m_norm1, w_in=m_w_in, pool_w=m_pool_w, pool_scale=m_pool_scale, w_out=m_w_out, norm2=m_norm2,
             w_up=m_w_up, w_down=m_w_down, norm3=m_norm3, w_gate=m_w_gate, w_ple=m_w_ple, final_norm=m_final_norm)
    v = dict(norm1=v_norm1, w_in=v_w_in, pool_w=v_pool_w, pool_scale=v_pool_scale, w_out=v_w_out, norm2=v_norm2,
             w_up=v_w_up, w_down=v_w_down, norm3=v_norm3, w_gate=v_w_gate, w_ple=v_w_ple, final_norm=v_final_norm)
    small = {n: w[n] for n in SMALL}

    wsrc = _GatheredWeights({n: w[n] for n in BIG})
    reducer = _GradReducer()
    loss8, dx, small_grads = _local_step(x[0], p.reshape(2, x.shape[1], PLE_DIM), positions[0], wsrc, small, loss_target[0], reducer)
    s_send, s_recv, s_vec, s_land = _small_start(_pack_small(small_grads, loss8[0, 0]))
    gsh = reducer.finish(s_vec)

    g_out, d_out, m_out, v_out = {}, {}, {}, {}
    for n in BIG:
        shp = w[n].shape
        two = lambda a: a.reshape(shp[0] * shp[1], shp[2])
        g2, d2, m2, v2 = _adamw(two(w[n]), two(gsh[n]), two(m[n]), two(v[n]), n)
        g_out[n], d_out[n], m_out[n], v_out[n] = g2.reshape(shp), d2.reshape(shp), m2.reshape(shp), v2.reshape(shp)
    red = _small_sum(*_small_wait(s_send, s_recv, s_vec, s_land, d2))
    g_small, loss = _unpack_small(red, small)
    _, d2, m2, v2 = _adamw(_pack_small(small), red, _pack_small({n: m[n] for n in SMALL}),
                           _pack_small({n: v[n] for n in SMALL}), "small")
    for tree, packed in ((d_out, d2), (m_out, m2), (v_out, v2)):
        tree.update(_unpack_small(packed, small)[0])
    g_out.update(g_small)

    return (loss, dx[None], *[g_out[n] for n in ORDER], *[d_out[n] for n in ORDER], *[m_out[n] for n in ORDER],
            *[v_out[n] for n in ORDER])
```

```python
import jax
import jax.numpy as jnp
import numpy as np
from jax import lax
from jax.experimental import pallas as pl
from jax.experimental.pallas import tpu as pltpu

f32 = jnp.float32
MXU_DTYPE = jnp.bfloat16
COMM_DTYPE = jnp.bfloat16

D_MODEL = 1024
POOL_WIDTH = 256
POOL_GC = 64
ATTN_WIDTH = 768
HEAD_DIM = 64
N_IN = POOL_WIDTH + 3 * ATTN_WIDTH
D_FF = 4096
PLE_DIM = 256
BLK = 128
DILATIONS = (1, 4, 16)
ROT_DIM = 16
ROPE_THETA = 500000.0
EPS = 1e-6
ATTN_SCALE = HEAD_DIM ** -0.5
NEG_BIG = -1e30

ADAM_LR, ADAM_B1, ADAM_B2, ADAM_EPS, ADAM_WD, ADAM_STEP = 0.001, 0.9, 0.999, 1e-08, 0.01, 10

TM = 512
TM_WGRAD = 1024
HALO = 16
VMEM_LIMIT = 48 * 1024 * 1024
VMEM_LIMIT_LARGE = 58 * 1024 * 1024
N_CHIPS = 4
MESH = pl.DeviceIdType.MESH

BIG = ("w_in", "w_out", "w_up", "w_down", "w_gate", "w_ple")
FULL_SHAPE = {"w_in": (D_MODEL, N_IN), "w_out": (D_MODEL, D_MODEL), "w_up": (D_MODEL, D_FF),
              "w_down": (D_FF, D_MODEL), "w_gate": (D_MODEL, D_MODEL), "w_ple": (PLE_DIM, D_MODEL)}
COL_SHARDED = {"w_in": True, "w_out": False, "w_up": True, "w_down": False, "w_gate": False, "w_ple": True}


def _shard_shape(name):
    k, n = FULL_SHAPE[name]
    return (k, n // N_CHIPS) if COL_SHARDED[name] else (k // N_CHIPS, n)


def _cparams(sem=None, vmem=VMEM_LIMIT):
    return pltpu.CompilerParams(dimension_semantics=sem, vmem_limit_bytes=vmem)


def _resident(block_shape, index_map):
    return pl.BlockSpec(block_shape, index_map, pipeline_mode=pl.Buffered(1))


def _mx(x):
    return x.astype(MXU_DTYPE)


def _dot(a, b):
    return jnp.dot(a, b, preferred_element_type=f32)


def _dot_nt(a, b):
    return lax.dot_general(a, b, (((1,), (1,)), ((), ())), preferred_element_type=f32)


def _dot_tn(a, b):
    return lax.dot_general(a, b, (((0,), (0,)), ((), ())), preferred_element_type=f32)


def _sigmoid(x):
    return 1.0 / (1.0 + jnp.exp(-x))


def _rope_apply(y, c, s1, s2, width):
    return y * c + pltpu.roll(y, width - 8, axis=1) * s1 + pltpu.roll(y, 8, axis=1) * s2


def _rope_transpose(dy, c, s1, s2, width):
    return dy * c + pltpu.roll(dy * s1, 8, axis=1) + pltpu.roll(dy * s2, width - 8, axis=1)


def _norm_matmul(h, g, w, layer, tn, name, rope=None):
    s_len, d = h.shape
    n = w.shape[2]

    def body(*refs):
        if rope is None:
            h_ref, g_ref, w_ref, y_ref, hn_ref = refs
        else:
            h_ref, g_ref, w_ref, c_ref, s1_ref, s2_ref, y_ref, hn_ref = refs
            reps = tn // 128
            c = jnp.concatenate([c_ref[...]] * reps, axis=1)
            s1 = jnp.concatenate([s1_ref[...]] * reps, axis=1)
            s2 = jnp.concatenate([s2_ref[...]] * reps, axis=1)
        x = h_ref[...]
        r = lax.rsqrt(jnp.mean(x * x, axis=-1, keepdims=True) + EPS)
        hn = ((x * r) * g_ref[...]).astype(hn_ref.dtype)
        hn_ref[...] = hn
        for j in range(n // tn):
            y = _dot(hn, w_ref[:, j * tn:(j + 1) * tn])
            if rope is not None and POOL_WIDTH <= j * tn < POOL_WIDTH + 2 * ATTN_WIDTH:
                y = _rope_apply(y, c, s1, s2, tn)
            y_ref[:, j * tn:(j + 1) * tn] = y

    in_specs = [pl.BlockSpec((TM, d), lambda i: (i, 0)),
                pl.BlockSpec((1, d), lambda i: (0, 0)),
                _resident((None, d, n), lambda i: (layer, 0, 0))]
    args = [h, g, w]
    if rope is not None:
        assert POOL_WIDTH % tn == 0 and (2 * ATTN_WIDTH) % tn == 0
        in_specs += [pl.BlockSpec((TM, 128), lambda i: (i, 0))] * 3
        args += list(rope)
    return pl.pallas_call(
        body, name=name, grid=(s_len // TM,), in_specs=in_specs,
        out_specs=[pl.BlockSpec((TM, n), lambda i: (i, 0)), pl.BlockSpec((TM, d), lambda i: (i, 0))],
        out_shape=[jax.ShapeDtypeStruct((s_len, n), f32), jax.ShapeDtypeStruct((s_len, d), MXU_DTYPE)],
        compiler_params=_cparams(("parallel",)),
    )(*args)


def _gate_ple_fwd(h2, g, w_gate, w_ple, layer, p, p_layer, name, head=None):
    s_len, d = h2.shape

    def body(h_ref, g_ref, wg_ref, p_ref, wp_ref, *rest):
        gl_ref, hn_ref = rest[-2:]
        x = h_ref[...]
        r = lax.rsqrt(jnp.mean(x * x, axis=-1, keepdims=True) + EPS)
        hn = ((x * r) * g_ref[...]).astype(hn_ref.dtype)
        hn_ref[...] = hn
        gl = _dot(hn, wg_ref[...])
        gl_ref[...] = gl.astype(gl_ref.dtype)
        h3 = x + _sigmoid(gl) * _dot(_mx(p_ref[...]), wp_ref[...])
        if head is None:
            rest[0][...] = h3
            return
        gf_ref, t_ref, loss_ref, dh_ref, dgf_ref = rest[:5]
        i = pl.program_id(0)
        gv = gf_ref[...]
        r3 = lax.rsqrt(jnp.mean(h3 * h3, axis=-1, keepdims=True) + EPS)
        xh = h3 * r3
        diff = xh * gv - t_ref[...]
        part = 0.5 * jnp.sum(jnp.mean(diff * diff, axis=-1, keepdims=True), axis=0, keepdims=True)
        dy = diff * (1.0 / d)
        dxh = dy * gv
        dh_ref[...] = r3 * (dxh - xh * jnp.mean(dxh * xh, axis=-1, keepdims=True))
        dgsum = jnp.sum(dy * xh, axis=0, keepdims=True)
        lossb = jnp.broadcast_to(part, (8, 128))

        @pl.when(i == 0)
        def _():
            loss_ref[...] = lossb
            dgf_ref[...] = dgsum

        @pl.when(i > 0)
        def _():
            loss_ref[...] += lossb
            dgf_ref[...] += dgsum

    row = lambda i: (i, 0)
    one = lambda i: (0, 0)
    in_specs = [pl.BlockSpec((TM, d), row), pl.BlockSpec((1, d), one),
                pl.BlockSpec((None, d, d), lambda i: (layer, 0, 0)),
                pl.BlockSpec((None, TM, PLE_DIM), lambda i: (p_layer, i, 0)),
                pl.BlockSpec((None, PLE_DIM, d), lambda i: (layer, 0, 0))]
    args = [h2, g, w_gate, p, w_ple]
    saved = [jax.ShapeDtypeStruct((s_len, d), MXU_DTYPE)] * 2
    if head is None:
        out_specs = [pl.BlockSpec((TM, d), row)] * 3
        out_shape = [jax.ShapeDtypeStruct((s_len, d), f32)] + saved
    else:
        in_specs += [pl.BlockSpec((1, d), one), pl.BlockSpec((TM, d), row)]
        args += list(head)
        out_specs = [pl.BlockSpec((8, 128), one), pl.BlockSpec((TM, d), row), pl.BlockSpec((1, d), one)] \
            + [pl.BlockSpec((TM, d), row)] * 2
        out_shape = [jax.ShapeDtypeStruct((8, 128), f32), jax.ShapeDtypeStruct((s_len, d), f32),
                     jax.ShapeDtypeStruct((1, d), f32)] + saved
    return pl.pallas_call(
        body, name=name, grid=(s_len // TM,), in_specs=in_specs, out_specs=out_specs, out_shape=out_shape,
        compiler_params=_cparams(("arbitrary",)),
    )(*args)


def _gate_bwd(dh3, gl, p, p_layer, w_ple, w_gate, layer, h2, g, name):
    s_len, d = dh3.shape

    def body(dh_ref, gl_ref, p_ref, wp_ref, wg_ref, h_ref, g_ref, dh2_ref, dg_ref, de_ref, dgl_ref):
        i = pl.program_id(0)
        dh = dh_ref[...]
        gate = _sigmoid(gl_ref[...].astype(f32))
        e = _dot(_mx(p_ref[...]), wp_ref[...])
        de_ref[...] = (dh * gate).astype(de_ref.dtype)
        dgl = ((dh * e) * (gate * (1.0 - gate))).astype(dgl_ref.dtype)
        dgl_ref[...] = dgl
        dx, dgrow = _rmsnorm_bwd(_dot_nt(dgl, wg_ref[...]), h_ref[...], g_ref[...])
        dh2_ref[...] = dh + dx
        dgsum = jnp.sum(dgrow, axis=0, keepdims=True)

        @pl.when(i == 0)
        def _():
            dg_ref[...] = dgsum

        @pl.when(i > 0)
        def _():
            dg_ref[...] += dgsum

    row = lambda i: (i, 0)
    blk = pl.BlockSpec((TM, d), row)
    return pl.pallas_call(
        body, name=name, grid=(s_len // TM,),
        in_specs=[blk, blk, pl.BlockSpec((None, TM, PLE_DIM), lambda i: (p_layer, i, 0)),
                  _resident((None, PLE_DIM, d), lambda i: (layer, 0, 0)),
                  _resident((None, d, d), lambda i: (layer, 0, 0)), blk, pl.BlockSpec((1, d), lambda i: (0, 0))],
        out_specs=[blk, pl.BlockSpec((1, d), lambda i: (0, 0)), blk, blk],
        out_shape=[jax.ShapeDtypeStruct((s_len, d), f32), jax.ShapeDtypeStruct((1, d), f32),
                   jax.ShapeDtypeStruct((s_len, d), MXU_DTYPE), jax.ShapeDtypeStruct((s_len, d), MXU_DTYPE)],
        compiler_params=_cparams(("arbitrary",)),
    )(dh3, gl, p, w_ple, w_gate, h2, g)


def _rmsnorm_bwd(dhn, x, g):
    r = lax.rsqrt(jnp.mean(x * x, axis=-1, keepdims=True) + EPS)
    xh = x * r
    dxh = dhn * g
    dx = r * (dxh - xh * jnp.mean(dxh * xh, axis=-1, keepdims=True))
    return dx, dhn * xh


def _matmul_nt_norm_bwd(dy, w, layer, h_prev, g, dres, name, tk=1024, after=None):
    s_len, k_dim = dy.shape
    d = h_prev.shape[1]

    def body(dy_ref, w_ref, h_ref, g_ref, dres_ref, *rest):
        dh_ref, dg_ref = rest[-2:]
        i = pl.program_id(0)
        acc = None
        for k in range(k_dim // tk):
            part = _dot_nt(_mx(dy_ref[:, k * tk:(k + 1) * tk]), w_ref[:, k * tk:(k + 1) * tk])
            acc = part if acc is None else acc + part
        dx, dgrow = _rmsnorm_bwd(acc, h_ref[...], g_ref[...])
        dh_ref[...] = dres_ref[...] + dx
        dgsum = jnp.sum(dgrow, axis=0, keepdims=True)

        @pl.when(i == 0)
        def _():
            dg_ref[...] = dgsum

        @pl.when(i > 0)
        def _():
            dg_ref[...] += dgsum

    in_specs = [pl.BlockSpec((TM, k_dim), lambda i: (i, 0)),
                _resident((None, d, k_dim), lambda i: (layer, 0, 0)),
                pl.BlockSpec((TM, d), lambda i: (i, 0)),
                pl.BlockSpec((1, d), lambda i: (0, 0)),
                pl.BlockSpec((TM, d), lambda i: (i, 0))]
    args = [dy, w, h_prev, g, dres]
    if after is not None:
        in_specs.append(pl.BlockSpec(memory_space=pl.ANY))
        args.append(after)
    return pl.pallas_call(
        body, name=name, grid=(s_len // TM,), in_specs=in_specs,
        out_specs=[pl.BlockSpec((TM, d), lambda i: (i, 0)), pl.BlockSpec((1, d), lambda i: (0, 0))],
        out_shape=[jax.ShapeDtypeStruct((s_len, d), f32), jax.ShapeDtypeStruct((1, d), f32)],
        compiler_params=_cparams(("arbitrary",)),
    )(*args)


def _mlp_fwd(h1, g, w_up, w_down, layer, name, tf=1024):
    s_len, d = h1.shape
    ff = w_up.shape[2]

    def body(h_ref, g_ref, wu_ref, wd_ref, h2_ref, a_ref, hn_ref):
        x = h_ref[...]
        r = lax.rsqrt(jnp.mean(x * x, axis=-1, keepdims=True) + EPS)
        hn = ((x * r) * g_ref[...]).astype(hn_ref.dtype)
        hn_ref[...] = hn
        acc = x
        for j in range(ff // tf):
            a = _dot(hn, wu_ref[:, j * tf:(j + 1) * tf])
            a_ref[:, j * tf:(j + 1) * tf] = a.astype(a_ref.dtype)
            relu = jnp.maximum(a, 0.0)
            acc = acc + _dot(_mx(relu * relu), wd_ref[j * tf:(j + 1) * tf, :])
        h2_ref[...] = acc

    row = lambda i: (i, 0)
    return pl.pallas_call(
        body, name=name, grid=(s_len // TM,),
        in_specs=[pl.BlockSpec((TM, d), row), pl.BlockSpec((1, d), lambda i: (0, 0)),
                  _resident((None, d, ff), lambda i: (layer, 0, 0)), _resident((None, ff, d), lambda i: (layer, 0, 0))],
        out_specs=[pl.BlockSpec((TM, d), row), pl.BlockSpec((TM, ff), row), pl.BlockSpec((TM, d), row)],
        out_shape=[jax.ShapeDtypeStruct((s_len, d), f32), jax.ShapeDtypeStruct((s_len, ff), MXU_DTYPE),
                   jax.ShapeDtypeStruct((s_len, d), MXU_DTYPE)],
        compiler_params=_cparams(("parallel",)),
    )(h1, g, w_up, w_down)


def _mlp_bwd(dh2, w_down, w_up, layer, a, h1, g, name, tf=1024):
    s_len, d = dh2.shape
    ff = a.shape[1]

    def body(dh_ref, wd_ref, wu_ref, a_ref, h_ref, g_ref, dh1_ref, dg_ref, da_ref):
        i = pl.program_id(0)
        dh = dh_ref[...]
        dhb = _mx(dh)
        acc = None
        for j in range(ff // tf):
            cols = slice(j * tf, (j + 1) * tf)
            dact = _dot_nt(dhb, wd_ref[cols, :])
            da = (dact * (2.0 * jnp.maximum(a_ref[:, cols].astype(f32), 0.0))).astype(da_ref.dtype)
            da_ref[:, cols] = da
            part = _dot_nt(da, wu_ref[:, cols])
            acc = part if acc is None else acc + part
        dx, dgrow = _rmsnorm_bwd(acc, h_ref[...], g_ref[...])
        dh1_ref[...] = dh + dx
        dgsum = jnp.sum(dgrow, axis=0, keepdims=True)

        @pl.when(i == 0)
        def _():
            dg_ref[...] = dgsum

        @pl.when(i > 0)
        def _():
            dg_ref[...] += dgsum

    row = lambda i: (i, 0)
    return pl.pallas_call(
        body, name=name, grid=(s_len // TM,),
        in_specs=[pl.BlockSpec((TM, d), row), _resident((None, ff, d), lambda i: (layer, 0, 0)),
                  _resident((None, d, ff), lambda i: (layer, 0, 0)), pl.BlockSpec((TM, ff), row),
                  pl.BlockSpec((TM, d), row), pl.BlockSpec((1, d), lambda i: (0, 0))],
        out_specs=[pl.BlockSpec((TM, d), row), pl.BlockSpec((1, d), lambda i: (0, 0)), pl.BlockSpec((TM, ff), row)],
        out_shape=[jax.ShapeDtypeStruct((s_len, d), f32), jax.ShapeDtypeStruct((1, d), f32),
                   jax.ShapeDtypeStruct((s_len, ff), MXU_DTYPE)],
        compiler_params=_cparams(("arbitrary",), vmem=VMEM_LIMIT_LARGE),
    )(dh2, w_down, w_up, a, h1, g)


def _weight_grad(a, b, name, act=False, lead=None):
    s_len, k_dim = a.shape[-2:]
    n = b.shape[1]
    tka = min(k_dim, 2048)
    tnb = n if n <= 1024 else (2048 if n % 2048 == 0 else 640)
    ns = s_len // TM_WGRAD

    def body(a_ref, b_ref, o_ref, acc_ref):
        s = pl.program_id(2)
        x = a_ref[...]
        if act:
            relu = jnp.maximum(x, 0.0)
            x = relu * relu

        @pl.when(s == 0)
        def _():
            acc_ref[...] = jnp.zeros_like(acc_ref)

        acc_ref[...] += _dot_tn(_mx(x), _mx(b_ref[...]))

        @pl.when(s == ns - 1)
        def _():
            o_ref[...] = acc_ref[...].astype(o_ref.dtype)

    if lead is None:
        a_spec = pl.BlockSpec((TM_WGRAD, tka), lambda i, j, s: (s, i))
    else:
        a_spec = pl.BlockSpec((None, TM_WGRAD, tka), lambda i, j, s: (lead, s, i))
    return pl.pallas_call(
        body, name=name, grid=(k_dim // tka, n // tnb, ns),
        in_specs=[a_spec, pl.BlockSpec((TM_WGRAD, tnb), lambda i, j, s: (s, j))],
        out_specs=pl.BlockSpec((tka, tnb), lambda i, j, s: (i, j)),
        out_shape=jax.ShapeDtypeStruct((k_dim, n), COMM_DTYPE),
        scratch_shapes=[pltpu.VMEM((tka, tnb), f32)],
        compiler_params=_cparams(("parallel", "parallel", "arbitrary")),
    )(a, b)


def _group_select(lane, x2, x4, x8, x16):
    grp = lane // POOL_GC
    return jnp.where(grp == 0, x2, jnp.where(grp == 1, x4, jnp.where(grp == 2, x8, x16)))


def _pool_window(lane):
    grp = lane // POOL_GC
    return jnp.where(grp == 0, 2, jnp.where(grp == 1, 4, jnp.where(grp == 2, 8, 16)))


def _pool_y(u, halo, i):
    xs = jnp.concatenate([jnp.where(i > 0, halo, 0.0), u], axis=0)
    s2 = xs + pltpu.roll(xs, 1, axis=0)
    s4 = s2 + pltpu.roll(s2, 2, axis=0)
    s8 = s4 + pltpu.roll(s4, 4, axis=0)
    s16 = s8 + pltpu.roll(s8, 8, axis=0)
    lane = lax.broadcasted_iota(jnp.int32, xs.shape, 1)
    sel = _group_select(lane, s2, s4, s8, s16)[HALO:, :]
    t = i * TM + lax.broadcasted_iota(jnp.int32, u.shape, 0)
    cnt = jnp.minimum(_pool_window(lax.broadcasted_iota(jnp.int32, u.shape, 1)), t + 1).astype(f32)
    return sel / cnt - u


def _group_weights(l0, l1, l2):
    mx = jnp.maximum(jnp.maximum(l0, l1), l2)
    e0, e1, e2 = jnp.exp(l0 - mx), jnp.exp(l1 - mx), jnp.exp(l2 - mx)
    den = e0 + e1 + e2
    return e0 / den, e1 / den, e2 / den


def _mixer_out_proj(z, wbd, scale, outs, lses, w_out, layer, h, name):
    s_len, d = h.shape

    def body(u_ref, halo_ref, wbd_ref, sc_ref, o0, o1, o2, l0, l1, l2, wo_ref, h_ref, m_ref, h1_ref):
        i = pl.program_id(0)
        y = _pool_y(u_ref[...], halo_ref[...], i)
        pool = _dot(_mx(y), wbd_ref[...]) * sc_ref[...]
        w0, w1, w2 = _group_weights(l0[...], l1[...], l2[...])
        m = jnp.concatenate([pool, o0[...] * w0, o1[...] * w1, o2[...] * w2], axis=1).astype(m_ref.dtype)
        m_ref[...] = m
        h1_ref[...] = h_ref[...] + _dot(m, wo_ref[...])

    row = lambda i: (i, 0)
    blk = pl.BlockSpec((TM, 256), row)
    grp = [pl.BlockSpec((TM, 256), lambda i, g=g: (i, g)) for g in range(3)]
    return pl.pallas_call(
        body, name=name, grid=(s_len // TM,),
        in_specs=[blk, pl.BlockSpec((HALO, 256), lambda i: (jnp.maximum(i * (TM // HALO) - 1, 0), 0)),
                  pl.BlockSpec((256, 256), lambda i: (0, 0)), pl.BlockSpec((1, 256), lambda i: (0, 0))] + grp + grp
        + [_resident((None, d, d), lambda i: (layer, 0, 0)), pl.BlockSpec((TM, d), row)],
        out_specs=[pl.BlockSpec((TM, d), row)] * 2,
        out_shape=[jax.ShapeDtypeStruct((s_len, d), MXU_DTYPE), jax.ShapeDtypeStruct((s_len, d), f32)],
        compiler_params=_cparams(("parallel",)),
    )(z, z, wbd, scale, outs, outs, outs, lses, lses, lses, w_out, h)


def _head_sums(x):
    r = lax.broadcasted_iota(jnp.int32, (256, 256), 0) // HEAD_DIM
    c = lax.broadcasted_iota(jnp.int32, (256, 256), 1) // HEAD_DIM
    ones = jnp.where(r == c, 1.0, 0.0).astype(jnp.bfloat16)
    hi = x.astype(jnp.bfloat16)
    lo = (x - hi.astype(f32)).astype(jnp.bfloat16)
    return _dot(hi, ones) + _dot(lo, ones)


def _out_combine_bwd(dh1, w_out, layer, outs, lses, name, after=None):
    s_len, d = dh1.shape

    def body(dh_ref, w_ref, o0, o1, o2, l0, l1, l2, *rest):
        dp_ref, do_ref, dl_ref = rest[-3:]
        dm = _dot_nt(_mx(dh_ref[...]), w_ref[...])
        dp_ref[...] = dm[:, :POOL_WIDTH]
        w = _group_weights(l0[...], l1[...], l2[...])
        da = [dm[:, POOL_WIDTH + 256 * g:POOL_WIDTH + 256 * (g + 1)] for g in range(3)]
        o = (o0[...], o1[...], o2[...])
        dw = [_head_sums(da[g] * o[g]) for g in range(3)]
        t = w[0] * dw[0] + w[1] * dw[1] + w[2] * dw[2]
        do_ref[...] = jnp.concatenate([da[g] * w[g] for g in range(3)], axis=1)
        dl_ref[...] = jnp.concatenate([w[g] * t for g in range(3)], axis=1)

    grp = [pl.BlockSpec((TM, 256), lambda i, g=g: (i, g)) for g in range(3)]
    in_specs = [pl.BlockSpec((TM, d), lambda i: (i, 0)), _resident((None, d, d), lambda i: (layer, 0, 0))] + grp + grp
    args = [dh1, w_out, outs, outs, outs, lses, lses, lses]
    if after is not None:
        in_specs.append(pl.BlockSpec(memory_space=pl.ANY))
        args.append(after)
    return pl.pallas_call(
        body, name=name, grid=(s_len // TM,), in_specs=in_specs,
        out_specs=[pl.BlockSpec((TM, POOL_WIDTH), lambda i: (i, 0))] + [pl.BlockSpec((TM, ATTN_WIDTH), lambda i: (i, 0))] * 2,
        out_shape=[jax.ShapeDtypeStruct((s_len, POOL_WIDTH), f32)] + [jax.ShapeDtypeStruct((s_len, ATTN_WIDTH), f32)] * 2,
        compiler_params=_cparams(("parallel",)),
    )(*args)


def _pool_bwd(z, dm, wbd, scale, name, after=None):
    s_len = z.shape[0]
    n_halo = s_len // HALO

    def body(u_ref, uh_ref, d_ref, dh_ref, wbd_ref, sc_ref, *rest):
        du_ref, dw_ref, dsc_ref = rest[-3:]
        i = pl.program_id(0)
        last = pl.num_programs(0) - 1
        y = _pool_y(u_ref[...], uh_ref[...], i)
        yb = _mx(y)
        dpo = d_ref[...]
        sc = sc_ref[...]
        dsc = jnp.sum(dpo * _dot(yb, wbd_ref[...]), axis=0, keepdims=True)
        dwp = _dot_tn(yb, _mx(dpo * sc))

        @pl.when(i == 0)
        def _():
            dsc_ref[...] = dsc
            dw_ref[...] = dwp

        @pl.when(i > 0)
        def _():
            dsc_ref[...] += dsc
            dw_ref[...] += dwp

        ext = jnp.concatenate([dpo, jnp.where(i < last, dh_ref[...], 0.0)], axis=0)
        dy = _dot_nt(_mx(ext * sc), wbd_ref[...])
        t = i * TM + lax.broadcasted_iota(jnp.int32, ext.shape, 0)
        lane = lax.broadcasted_iota(jnp.int32, ext.shape, 1)
        e = dy / jnp.minimum(_pool_window(lane), t + 1).astype(f32)
        rows = ext.shape[0]
        f2 = e + pltpu.roll(e, rows - 1, axis=0)
        f4 = f2 + pltpu.roll(f2, rows - 2, axis=0)
        f8 = f4 + pltpu.roll(f4, rows - 4, axis=0)
        f16 = f8 + pltpu.roll(f8, rows - 8, axis=0)
        du_ref[...] = (_group_select(lane, f2, f4, f8, f16) - dy)[:TM, :].astype(du_ref.dtype)

    row = lambda i: (i, 0)
    blk = pl.BlockSpec((TM, 256), row)
    extra = [] if after is None else [after]
    return pl.pallas_call(
        body, name=name, grid=(s_len // TM,),
        in_specs=[blk, pl.BlockSpec((HALO, 256), lambda i: (jnp.maximum(i * (TM // HALO) - 1, 0), 0)),
                  blk, pl.BlockSpec((HALO, 256), lambda i: (jnp.minimum((i + 1) * (TM // HALO), n_halo - 1), 0)),
                  pl.BlockSpec((256, 256), lambda i: (0, 0)), pl.BlockSpec((1, 256), lambda i: (0, 0))]
        + [pl.BlockSpec(memory_space=pl.ANY)] * len(extra),
        out_specs=[blk, pl.BlockSpec((256, 256), lambda i: (0, 0)), pl.BlockSpec((1, 256), lambda i: (0, 0))],
        out_shape=[jax.ShapeDtypeStruct((s_len, N_IN), MXU_DTYPE), jax.ShapeDtypeStruct((256, 256), f32),
                   jax.ShapeDtypeStruct((1, 256), f32)],
        compiler_params=_cparams(("arbitrary",)),
    )(z, z, dm, dm, wbd, scale, *extra)


def _tri_masks():
    qi = lax.broadcasted_iota(jnp.int32, (BLK, BLK), 0)
    ki = lax.broadcasted_iota(jnp.int32, (BLK, BLK), 1)
    return qi >= ki, ki >= qi


ATTN_SUPER_PER_STEP = (8, 4, 1)
Q_COL, K_COL, V_COL = POOL_WIDTH // 128, (POOL_WIDTH + ATTN_WIDTH) // 128, (POOL_WIDTH + 2 * ATTN_WIDTH) // 128


def _rows(ref, start, dil):
    if dil == 1:
        return ref[pl.ds(start, BLK), :]
    return ref[pl.ds(start, BLK, stride=dil), :]


ATTN_BLOCKS_TOGETHER = 8


def _set_rows(ref, start, dil, val):
    if dil == 1:
        ref[pl.ds(start, BLK), :] = val
    else:
        ref[pl.ds(start, BLK, stride=dil), :] = val


def _attn_fwd(z, g, prev, name):
    s_len = z.shape[0]
    dil, m = DILATIONS[g], ATTN_SUPER_PER_STEP[g]
    sbr = BLK * dil
    rows = sbr * m

    def body(*refs):
        q_ref, kc_ref, kp_ref, vc_ref, vp_ref = refs[:5]
        o_ref, l_ref = refs[-2:]
        st = pl.program_id(0)
        low, up = _tri_masks()
        head0 = lax.broadcasted_iota(jnp.int32, (BLK, 128), 1) < HEAD_DIM
        blocks = [(sb, r) for sb in range(m) for r in range(dil)]
        for g0 in range(0, len(blocks), ATTN_BLOCKS_TOGETHER):
            grp = blocks[g0:g0 + ATTN_BLOCKS_TOGETHER]
            loaded = []
            for sb, r in grp:
                base = sb * sbr + r
                if sb == 0:
                    kp, vp = _rows(kp_ref, r, dil), _rows(vp_ref, r, dil)
                else:
                    kp, vp = _rows(kc_ref, base - sbr, dil), _rows(vc_ref, base - sbr, dil)
                qs = _rows(q_ref, base, dil) * ATTN_SCALE
                loaded.append((_mx(jnp.where(head0, qs, 0.0)), _mx(jnp.where(head0, 0.0, qs)),
                               jnp.concatenate([_mx(kp), _mx(_rows(kc_ref, base, dil))], axis=0),
                               jnp.concatenate([_mx(vp), _mx(_rows(vc_ref, base, dil))], axis=0)))
            scores = [(_dot_nt(q0, k2), _dot_nt(q1, k2)) for q0, q1, k2, _ in loaded]
            soft = []
            for (sb, _), pair in zip(grp, scores):
                valid = jnp.concatenate([up & (st > 0) if sb == 0 else up, low], axis=1)
                heads = []
                for s in pair:
                    s = jnp.where(valid, s, NEG_BIG)
                    mx = jnp.max(s, axis=-1, keepdims=True)
                    e = jnp.exp(s - mx)
                    l = jnp.sum(e, axis=-1, keepdims=True)
                    heads.append((_mx(e / l), jnp.broadcast_to(mx + jnp.log(l), (BLK, 128))))
                soft.append(heads)
            for (sb, r), heads, (_, _, _, v2) in zip(grp, soft, loaded):
                base = sb * sbr + r
                _set_rows(o_ref, base, dil, jnp.where(head0, _dot(heads[0][0], v2), _dot(heads[1][0], v2)))
                _set_rows(l_ref, base, dil, jnp.where(head0, heads[0][1], heads[1][1]))

    def cur(col):
        return pl.BlockSpec((rows, 128), lambda st, hp: (st, col + 2 * g + hp))

    def before(col):
        return pl.BlockSpec((sbr, 128), lambda st, hp: (jnp.maximum(st * m - 1, 0), col + 2 * g + hp))

    in_specs = [cur(Q_COL), cur(K_COL), before(K_COL), cur(V_COL), before(V_COL)]
    args = [z, z, z, z, z]
    aliases = {}
    if prev is not None:
        in_specs += [pl.BlockSpec(memory_space=pl.ANY)] * 2
        args += list(prev)
        aliases = {5: 0, 6: 1}
    return pl.pallas_call(
        body, name=name, grid=(s_len // rows, 2), in_specs=in_specs, out_specs=[cur(0), cur(0)],
        out_shape=[jax.ShapeDtypeStruct((s_len, ATTN_WIDTH), f32)] * 2, input_output_aliases=aliases,
        compiler_params=_cparams(("parallel", "parallel")),
    )(*args)


def _stack_heads(x, head0):
    return jnp.concatenate([_mx(jnp.where(head0, x, 0.0)), _mx(jnp.where(head0, 0.0, x))], axis=0)


def _head_rows(x):
    xt = x.T
    return jnp.concatenate([jnp.broadcast_to(xt[0:1, :], (BLK, BLK)),
                            jnp.broadcast_to(xt[HEAD_DIM:HEAD_DIM + 1, :], (BLK, BLK))], axis=0)


def _attn_bwd(z, do, lse, dlt, tabs, dz, g, name):
    s_len = z.shape[0]
    dil, m = DILATIONS[g], ATTN_SUPER_PER_STEP[g]
    sbr = BLK * dil
    rows = sbr * m
    nsteps = s_len // rows

    def body(q_ref, qn_ref, kc_ref, kp_ref, vc_ref, vp_ref, do_ref, don_ref, l_ref, ln_ref, d_ref, dn_ref,
             c_ref, s1_ref, s2_ref, dz_in, dz_ref, dq_buf, dk_buf, dv_buf, out_buf, sems):
        del dz_in
        st, hp = pl.program_id(0), pl.program_id(1)
        head0 = lax.broadcasted_iota(jnp.int32, (BLK, 128), 1) < HEAD_DIM
        key_i = lax.broadcasted_iota(jnp.int32, (2 * BLK, BLK), 0) & (BLK - 1)
        query_i = lax.broadcasted_iota(jnp.int32, (2 * BLK, BLK), 1)
        same_t, cross_t = query_i >= key_i, key_i >= query_i
        def load(r):
            keys, vals = [_stack_heads(_rows(kp_ref, r, dil), head0)], [_stack_heads(_rows(vp_ref, r, dil), head0)]
            qs, dos, lses, dlts = [], [], [], []
            for sb in range(m):
                base = sb * sbr + r
                keys.append(_stack_heads(_rows(kc_ref, base, dil), head0))
                vals.append(_stack_heads(_rows(vc_ref, base, dil), head0))
                qs.append(_mx(_rows(q_ref, base, dil)))
                dos.append(_mx(_rows(do_ref, base, dil)))
                lses.append(_head_rows(_rows(l_ref, base, dil)))
                dlts.append(_head_rows(_rows(d_ref, base, dil)))
            qs.append(_mx(_rows(qn_ref, r, dil)))
            dos.append(_mx(_rows(don_ref, r, dil)))
            lses.append(_head_rows(_rows(ln_ref, r, dil)))
            dlts.append(_head_rows(_rows(dn_ref, r, dil)))
            return keys, vals, qs, dos, lses, dlts

        def products(data):
            keys, vals, qs, dos, _, _ = data
            return ([(_dot_nt(keys[j + 1], qs[j]), _dot_nt(vals[j + 1], dos[j])) for j in range(m)],
                    [(_dot_nt(keys[j], qs[j]), _dot_nt(vals[j], dos[j])) for j in range(m + 1)])

        def finish(data, raw):
            lses, dlts = data[4], data[5]

            def one(pair, j, valid):
                p = jnp.where(valid, jnp.exp(pair[0] * ATTN_SCALE - lses[j]), 0.0)
                return _mx(p), _mx(p * (pair[1] - dlts[j]) * ATTN_SCALE)

            same = [one(raw[0][j], j, same_t) for j in range(m)]
            cross = [one(raw[1][j], j, cross_t & (st > 0) if j == 0 else
                         (cross_t & (st < nsteps - 1) if j == m else cross_t)) for j in range(m + 1)]
            return same, cross

        def gradients(r, data, fin):
            keys, _, qs, dos, _, _ = data
            same, cross = fin
            for sb in range(m):
                base = sb * sbr + r
                (p_a, ds_a), (_, ds_x), (p_n, ds_n) = same[sb], cross[sb], cross[sb + 1]
                dq = _dot_tn(ds_a, keys[sb + 1]) + _dot_tn(ds_x, keys[sb])
                dk2 = _dot(ds_a, qs[sb]) + _dot(ds_n, qs[sb + 1])
                dv2 = _dot(p_a, dos[sb]) + _dot(p_n, dos[sb + 1])
                _set_rows(dq_buf, base, dil, dq)
                _set_rows(dk_buf, base, dil, jnp.where(head0, dk2[:BLK], dk2[BLK:]))
                _set_rows(dv_buf, base, dil, jnp.where(head0, dv2[:BLK], dv2[BLK:]))

        def residue_group(rg, carry):
            rs = [rg * group + i for i in range(group)]
            data = [load(r) for r in rs]
            raws = [products(d) for d in data]
            fins = [finish(d, raw) for d, raw in zip(data, raws)]
            for r, d, fin in zip(rs, data, fins):
                gradients(r, d, fin)
            return carry

        group = max(1, min(dil, ATTN_BLOCKS_TOGETHER // m))
        if dil // group <= 2:
            for rg in range(dil // group):
                residue_group(rg, 0)
        else:
            lax.fori_loop(0, dil // group, residue_group, 0)
        copies = []
        for t, (buf, col) in enumerate(((dq_buf, Q_COL), (dk_buf, K_COL), (dv_buf, V_COL))):
            val = buf[...]
            if t < 2:
                val = _rope_transpose(val, c_ref[...], s1_ref[...], s2_ref[...], 128)
            out_buf[t] = val.astype(out_buf.dtype)
            lane0 = pl.multiple_of((col + 2 * g + hp) * 128, 128)
            dst = dz_ref.at[pl.ds(pl.multiple_of(st * rows, rows), rows), pl.ds(lane0, 128)]
            cp = pltpu.make_async_copy(out_buf.at[t], dst, sems.at[t])
            cp.start()
            copies.append(cp)
        for cp in copies:
            cp.wait()

    def cur(col):
        return pl.BlockSpec((rows, 128), lambda st, hp: (st, col + 2 * g + hp))

    def before(col):
        return pl.BlockSpec((sbr, 128), lambda st, hp: (jnp.maximum(st * m - 1, 0), col + 2 * g + hp))

    def after(col):
        return pl.BlockSpec((sbr, 128), lambda st, hp: (jnp.minimum((st + 1) * m, s_len // sbr - 1), col + 2 * g + hp))

    tab = pl.BlockSpec((rows, 128), lambda st, hp: (st, 0))
    return pl.pallas_call(
        body, name=name, grid=(nsteps, 2),
        in_specs=[cur(Q_COL), after(Q_COL), cur(K_COL), before(K_COL), cur(V_COL), before(V_COL),
                  cur(0), after(0), cur(0), after(0), cur(0), after(0), tab, tab, tab,
                  pl.BlockSpec(memory_space=pl.ANY)],
        out_specs=pl.BlockSpec(memory_space=pl.ANY),
        out_shape=jax.ShapeDtypeStruct(dz.shape, dz.dtype), input_output_aliases={15: 0},
        scratch_shapes=[pltpu.VMEM((rows, 128), f32)] * 3 + [pltpu.VMEM((3, rows, 128), dz.dtype),
                                                            pltpu.SemaphoreType.DMA((3,))],
        compiler_params=_cparams(("arbitrary", "arbitrary")),
    )(z, z, z, z, z, z, do, do, lse, lse, dlt, dlt, *tabs, dz)


def _rope_tables(positions):
    inv_freq = ROPE_THETA ** (-jnp.arange(0, ROT_DIM, 2, dtype=f32) / ROT_DIM)
    ang = positions.astype(f32)[:, None] * inv_freq
    cos, sin = jnp.cos(ang), jnp.sin(ang)
    s_len = positions.shape[0]
    zero8, rest = jnp.zeros((s_len, 8), f32), jnp.zeros((s_len, HEAD_DIM - ROT_DIM), f32)
    c = jnp.concatenate([cos, cos, jnp.ones((s_len, HEAD_DIM - ROT_DIM), f32)], axis=1)
    s1 = jnp.concatenate([-sin, zero8, rest], axis=1)
    s2 = jnp.concatenate([zero8, sin, rest], axis=1)
    return tuple(jnp.tile(t, (1, 2)) for t in (c, s1, s2))


def _block_diag(pool_w):
    out = jnp.zeros((POOL_WIDTH, POOL_WIDTH), pool_w.dtype)
    for g in range(4):
        out = lax.dynamic_update_slice(out, pool_w[g], (g * POOL_GC, g * POOL_GC))
    return out


def _layer_fwd(h, p, wsrc, small, layer, tabs, head=None):
    nm = f"l{layer}_"
    wts, wl = wsrc.take(layer, ("w_in",), (h,) if layer else tuple(tabs))
    z, hn1 = _norm_matmul(h, small["norm1"][layer][None], wts["w_in"], wl, 256, nm + "in_proj", rope=tabs)
    ol = None
    for g in range(3):
        ol = _attn_fwd(z, g, ol, nm + f"attn_fwd{g}")
    outs, lses = ol
    wbd = _mx(_block_diag(small["pool_w"][layer]))
    scale = small["pool_scale"][layer][None]
    wts.update(wsrc.take(layer, ("w_out",), (outs,))[0])
    m, h1 = _mixer_out_proj(z, wbd, scale, outs, lses, wts["w_out"], wl, h, nm + "mixer_out")
    wts.update(wsrc.take(layer, ("w_up", "w_down"), (h1,))[0])
    h2, a, hn2 = _mlp_fwd(h1, small["norm2"][layer][None], wts["w_up"], wts["w_down"], wl, nm + "mlp")
    wts.update(wsrc.take(layer, ("w_gate", "w_ple"), (h2,))[0])
    *h3, gl, hn3 = _gate_ple_fwd(h2, small["norm3"][layer][None], wts["w_gate"], wts["w_ple"], wl, p, layer,
                                 nm + "gate_ple", head=head)
    saved = dict(h=h, z=z, hn1=hn1, outs=outs, lses=lses, wbd=wbd, scale=scale, m=m, h1=h1, a=a, hn2=hn2, h2=h2,
                 gl=gl, hn3=hn3, wts=wts, wl=wl)
    return h3, saved


def _layer_bwd(dh3, sv, p, small, layer, tabs128, reducer):
    nm = f"l{layer}_"
    wts, wl = sv["wts"], sv["wl"]
    dh2, dg3, de, dgl = _gate_bwd(dh3, sv["gl"], p, layer, wts["w_ple"], wts["w_gate"], wl, sv["h2"],
                                  small["norm3"][layer][None], nm + "gate_bwd")
    reducer.add("w_gate", layer, _weight_grad(sv["hn3"], dgl, nm + "dw_gate"))
    reducer.add("w_ple", layer, _weight_grad(p, de, nm + "dw_ple", lead=layer))
    dh1, dg2, da = _mlp_bwd(dh2, wts["w_down"], wts["w_up"], wl, sv["a"], sv["h1"], small["norm2"][layer][None],
                            nm + "mlp_bwd")
    reducer.add("w_down", layer, _weight_grad(sv["a"], dh2, nm + "dw_down", act=True))
    started = reducer.add("w_up", layer, _weight_grad(sv["hn2"], da, nm + "dw_up"))
    dpool, do, dlt = _out_combine_bwd(dh1, wts["w_out"], wl, sv["outs"], sv["lses"], nm + "out_bwd", after=started)
    started = reducer.add("w_out", layer, _weight_grad(sv["m"], dh1, nm + "dw_out"))
    dz, dwbd, dscale = _pool_bwd(sv["z"], dpool, sv["wbd"], sv["scale"], nm + "pool_bwd", after=started)
    for g in range(3):
        dz = _attn_bwd(sv["z"], do, sv["lses"], dlt, tabs128, dz, g, nm + f"attn_bwd{g}")
    started = reducer.add("w_in", layer, _weight_grad(sv["hn1"], dz, nm + "dw_in"))
    dh0, dg1 = _matmul_nt_norm_bwd(dz, wts["w_in"], wl, sv["h"], small["norm1"][layer][None], dh1, nm + "in_bwd",
                                   tk=512, after=started)
    dpool_w = jnp.stack([dwbd[g * POOL_GC:(g + 1) * POOL_GC, g * POOL_GC:(g + 1) * POOL_GC] for g in range(4)])
    sg = dict(norm1=dg1[0], norm2=dg2[0], norm3=dg3[0], pool_w=dpool_w, pool_scale=dscale[0])
    return dh0, sg


def _local_step(x, p, positions, wsrc, small, target, reducer):
    tabs128 = _rope_tables(positions)
    (h,), sv0 = _layer_fwd(x, p, wsrc, small, 0, tabs128)
    (loss, dh, dgf), sv1 = _layer_fwd(h, p, wsrc, small, 1, tabs128, head=(small["final_norm"][None], target))
    saved = [sv0, sv1]
    sgs = [None, None]
    for layer in (1, 0):
        dh, sgs[layer] = _layer_bwd(dh, saved[layer], p, small, layer, tabs128, reducer)
    small_grads = {k: jnp.stack([sgs[0][k], sgs[1][k]]) for k in sgs[0]}
    small_grads["final_norm"] = dgf[0]
    return loss, dh, small_grads


HBM = pl.BlockSpec(memory_space=pltpu.HBM)


def _my_place():
    return lax.axis_index("x"), lax.axis_index("y"), lax.axis_index("c")


def _other_chips(x, y):
    return [(1 - x, y), (x, 1 - y), (1 - x, 1 - y)]


def _window(ref, name, chip):
    k, n = _shard_shape(name)
    if COL_SHARDED[name]:
        return ref.at[:, pl.ds(pl.multiple_of(chip * n, 128), n)]
    return ref.at[pl.ds(pl.multiple_of(chip * k, 128), k), :]


def _chip_index():
    return jnp.reshape(2 * lax.axis_index("x") + lax.axis_index("y"), (1,)).astype(jnp.int32)


def _shard_block(name, tr):
    ks, ns = _shard_shape(name)
    if COL_SHARDED[name]:
        return (tr, ns), lambda i, me: (i, me[0])
    return (tr, ns), lambda i, me: (me[0] * (ks // tr) + i, 0)


def _place_shard(w, name, layer):
    ks, ns = _shard_shape(name)
    tr = min(ks, 256)
    shape, index = _shard_block(name, tr)

    def body(me_ref, w_ref, o_ref):
        o_ref[...] = w_ref[...].astype(o_ref.dtype)

    return pl.pallas_call(
        body, name=f"place_{name}{layer}",
        grid_spec=pltpu.PrefetchScalarGridSpec(
            num_scalar_prefetch=1, grid=(ks // tr,),
            in_specs=[pl.BlockSpec((None, tr, ns), lambda i, me: (layer, i, 0))],
            out_specs=pl.BlockSpec((None,) + shape, lambda i, me: (0,) + index(i, me))),
        out_shape=jax.ShapeDtypeStruct((1,) + FULL_SHAPE[name], MXU_DTYPE),
        compiler_params=_cparams(("parallel",)),
    )(_chip_index(), w)


GATHER_ORDER = [("w_in", 0), ("w_out", 0), ("w_up", 0), ("w_down", 0), ("w_gate", 0), ("w_ple", 0),
                ("w_in", 1), ("w_out", 1), ("w_up", 1), ("w_down", 1), ("w_gate", 1), ("w_ple", 1)]
SEM = pl.BlockSpec(memory_space=pltpu.SEMAPHORE)
EFFECT = pltpu.SideEffectType.DATAFLOW_SIDE_EFFECTING


def _gather_copy(src_ref, dst_ref, name, idx, j, chip, send_sems, recv_sems, c):
    cx, cy = chip
    return pltpu.make_async_remote_copy(
        src_ref=src_ref, dst_ref=dst_ref, send_sem=send_sems.at[3 * idx + j], recv_sem=recv_sems.at[3 * idx + j],
        device_id=(cx, cy, c), device_id_type=MESH)


def _gather_start(placed, order, tag, after=None):
    n = len(order)
    extra = [] if after is None else [after]

    def body(*refs):
        ins = refs[:n]
        k = n + len(extra)
        send_sems, recv_sems = refs[k], refs[k + 1]
        outs = refs[k + 2:k + 2 + n]
        token = refs[-1]
        x, y, c = _my_place()
        me = 2 * x + y
        for idx, (name, _) in enumerate(order):
            for j, chip in enumerate(_other_chips(x, y)):
                _gather_copy(_window(ins[idx].at[0], name, me), _window(outs[idx].at[0], name, me), name, idx, j, chip,
                             send_sems, recv_sems, c).start()
        token[...] = jnp.zeros_like(token)

    res = pl.pallas_call(
        body, name="gather_start" + tag,
        out_shape=(pltpu.SemaphoreType.DMA((3 * n,)), pltpu.SemaphoreType.DMA((3 * n,)))
        + tuple(pltpu.HBM(a.shape, a.dtype) for a in placed) + (jax.ShapeDtypeStruct((8, 128), f32),),
        in_specs=[HBM] * n + [pl.BlockSpec(memory_space=pl.ANY)] * len(extra),
        out_specs=(SEM, SEM) + (HBM,) * n + (pl.BlockSpec(memory_space=pltpu.VMEM),),
        input_output_aliases={i: i + 2 for i in range(n)},
        compiler_params=pltpu.CompilerParams(has_side_effects=EFFECT),
    )(*[pltpu.with_memory_space_constraint(a, pltpu.HBM) for a in placed], *extra)
    return res[0], res[1], list(res[2:2 + n]), res[-1]


def _gather_wait(send_sems, recv_sems, arrays, order, idxs, after, name):
    n = len(idxs)

    def body(*refs):
        ins = refs[:n]
        send_ref, recv_ref = refs[n], refs[n + 1]
        x, y, c = _my_place()
        me = 2 * x + y
        for k, idx in enumerate(idxs):
            wname = order[idx][0]
            for j, chip in enumerate(_other_chips(x, y)):
                cx, cy = chip
                mine = _window(ins[k].at[0], wname, me)
                land = _window(ins[k].at[0], wname, 2 * cx + cy)
                _gather_copy(mine, mine, wname, idx, j, chip, send_ref, recv_ref, c).wait_send()
                _gather_copy(land, land, wname, idx, j, chip, send_ref, recv_ref, c).wait_recv()

    operands = list(arrays) + [send_sems, recv_sems] + list(after)
    in_specs = [HBM] * n + [SEM, SEM] + [pl.BlockSpec(memory_space=pl.ANY)] * len(after)
    res = pl.pallas_call(
        body, name=name, out_shape=tuple(pltpu.HBM(a.shape, a.dtype) for a in arrays),
        in_specs=in_specs, out_specs=(HBM,) * n, input_output_aliases={i: i for i in range(n)},
        compiler_params=pltpu.CompilerParams(has_side_effects=EFFECT),
    )(*operands)
    return list(res)


class _GatheredWeights:
    def __init__(self, shards):
        self.starts = []
        token = None
        for tag, order in (("_first", GATHER_ORDER[:1]), ("_rest", GATHER_ORDER[1:])):
            placed = [_place_shard(shards[name], name, layer) for name, layer in order]
            self.starts.append((order,) + _gather_start(placed, order, tag, token))
            token = self.starts[-1][-1]

    def take(self, layer, names, after):
        order, send, recv, arrays, _ = next(s for s in self.starts if (names[0], layer) in s[0])
        after = list(after)
        if order is self.starts[0][0]:
            after.append(self.starts[-1][-1])
        idxs = [order.index((n, layer)) for n in names]
        got = _gather_wait(send, recv, [arrays[i] for i in idxs], order, idxs, after, f"gather_wait{layer}_{names[0]}")
        return dict(zip(names, got)), 0


N_DEV = 8


def _reduce_copies(dws, lands, names, layer, send_sems, recv_sems):
    x, y, c = _my_place()
    me, my_dev = 2 * x + y, 4 * x + 2 * y + c
    out = []
    for t, name in enumerate(names):
        for j, (cx, cy) in enumerate(_other_chips(x, y)):
            out.append((pltpu.make_async_remote_copy(
                src_ref=_window(dws[t], name, 2 * cx + cy), dst_ref=lands[t].at[my_dev],
                send_sem=send_sems.at[4 * t + j], recv_sem=recv_sems.at[N_DEV * t + my_dev],
                device_id=(cx, cy, layer), device_id_type=MESH), False))
        out.append((pltpu.make_async_remote_copy(
            src_ref=_window(dws[t], name, me), dst_ref=lands[t].at[my_dev],
            send_sem=send_sems.at[4 * t + 3], recv_sem=recv_sems.at[N_DEV * t + my_dev],
            device_id=(x, y, layer), device_id_type=MESH), True))
    return out


def _reduce_start(dws, names, layer, tag):
    n = len(names)
    lands = [lax.empty((N_DEV,) + _shard_shape(nm), dws[0].dtype) for nm in names]

    def body(*refs):
        ins = refs[:n]
        send_sems, recv_sems = refs[2 * n], refs[2 * n + 1]
        land_out = refs[3 * n + 2:4 * n + 2]
        token = refs[-1]
        c = lax.axis_index("c")
        for cp, non_owner_only in _reduce_copies(ins, land_out, names, layer, send_sems, recv_sems):
            if non_owner_only:
                @pl.when(c != layer)
                def _():
                    cp.start()
            else:
                cp.start()
        token[...] = jnp.zeros_like(token)

    res = pl.pallas_call(
        body, name="reduce_start" + tag,
        out_shape=(pltpu.SemaphoreType.DMA((4 * n,)), pltpu.SemaphoreType.DMA((N_DEV * n,)))
        + tuple(pltpu.HBM(a.shape, a.dtype) for a in dws) + tuple(pltpu.HBM(a.shape, a.dtype) for a in lands)
        + (jax.ShapeDtypeStruct((8, 128), f32),),
        in_specs=[HBM] * (2 * n),
        out_specs=(SEM, SEM) + (HBM,) * (2 * n) + (pl.BlockSpec(memory_space=pltpu.VMEM),),
        input_output_aliases={i: i + 2 for i in range(2 * n)},
        compiler_params=pltpu.CompilerParams(has_side_effects=EFFECT),
    )(*[pltpu.with_memory_space_constraint(a, pltpu.HBM) for a in list(dws) + lands])
    return res[0], res[1], list(res[2:2 + n]), list(res[2 + n:2 + 2 * n]), res[-1]


def _reduce_wait(send_sems, recv_sems, dws, lands, names, layer, after, tag):
    n = len(names)

    def body(*refs):
        ins, land_in = refs[:n], refs[n:2 * n]
        send_ref, recv_ref = refs[2 * n], refs[2 * n + 1]
        x, y, c = _my_place()
        for cp, non_owner_only in _reduce_copies(ins, land_in, names, layer, send_ref, recv_ref):
            if non_owner_only:
                @pl.when(c != layer)
                def _():
                    cp.wait_send()
            else:
                cp.wait_send()

        @pl.when(c == layer)
        def _():
            for t in range(n):
                for k in range(1, N_DEV):
                    px, py, pc = x ^ ((k >> 2) & 1), y ^ ((k >> 1) & 1), c ^ (k & 1)
                    dev = 4 * px + 2 * py + pc
                    land = land_in[t].at[dev]
                    pltpu.make_async_remote_copy(
                        src_ref=land, dst_ref=land, send_sem=send_ref.at[4 * t], recv_sem=recv_ref.at[N_DEV * t + dev],
                        device_id=(px, py, pc), device_id_type=MESH).wait_recv()

    res = pl.pallas_call(
        body, name="reduce_wait" + tag,
        out_shape=tuple(pltpu.HBM(a.shape, a.dtype) for a in list(dws) + list(lands)),
        in_specs=[HBM] * (2 * n) + [SEM, SEM, pl.BlockSpec(memory_space=pl.ANY)], out_specs=(HBM,) * (2 * n),
        input_output_aliases={i: i for i in range(2 * n)},
        compiler_params=pltpu.CompilerParams(has_side_effects=EFFECT),
    )(*dws, *lands, send_sems, recv_sems, after)
    return list(res[:n]), list(res[n:])


def _sum_devices(land, own, name, layer, prev):
    ks, ns = _shard_shape(name)
    tr = min(ks, 256)
    shape, index = _shard_block(name, tr)

    def body(me_ref, dev_ref, *refs):
        s_ref, own_ref, out_ref = refs[0], refs[1], refs[-1]
        dev = dev_ref[0]
        acc = None
        for s in range(N_DEV):
            term = jnp.where(dev == s, own_ref[...], s_ref[s]).astype(f32)
            acc = term if acc is None else acc + term
        out_ref[...] = acc

    def mine(i, dev):
        return i * jnp.where((dev[0] & 1) == layer, 1, 0)

    in_specs = [pl.BlockSpec((N_DEV, tr, ns), lambda i, me, dev: (0, mine(i, dev), 0)),
                pl.BlockSpec(shape, lambda i, me, dev: index(mine(i, dev), me))]
    args = [land, own]
    aliases = {}
    if prev is not None:
        in_specs.append(pl.BlockSpec(memory_space=pl.ANY))
        args.append(prev)
        aliases = {4: 0}
    x, y, c = _my_place()
    return pl.pallas_call(
        body, name=f"sum_devices_{name}{layer}",
        grid_spec=pltpu.PrefetchScalarGridSpec(
            num_scalar_prefetch=2, grid=(ks // tr,), in_specs=in_specs,
            out_specs=pl.BlockSpec((None, tr, ns), lambda i, me, dev: (layer, mine(i, dev), 0))),
        out_shape=jax.ShapeDtypeStruct((2, ks, ns), f32), input_output_aliases=aliases,
        compiler_params=_cparams(("arbitrary",)),
    )(_chip_index(), jnp.reshape(4 * x + 2 * y + c, (1,)).astype(jnp.int32), *args)


class _GradReducer:
    GROUPS = (("1", 1, ("w_gate", "w_ple", "w_down", "w_up", "w_out", "w_in")),
              ("0a", 0, ("w_gate", "w_ple", "w_down", "w_up")),
              ("0b", 0, ("w_out",)),
              ("0c", 0, ("w_in",)))

    def __init__(self):
        self.grads = {}
        self.started = {}

    def add(self, name, layer, dw):
        self.grads[(name, layer)] = dw
        token = None
        for tag, glayer, names in self.GROUPS:
            if tag not in self.started and all((nm, glayer) in self.grads for nm in names):
                *self.started[tag], token = _reduce_start([self.grads[(nm, glayer)] for nm in names], names, glayer, tag)
        return token

    def finish(self, after):
        mine = {}
        for tag, layer, names in self.GROUPS:
            send, recv, dws, lands = self.started[tag]
            dws, lands = _reduce_wait(send, recv, dws, lands, names, layer, after, tag)
            for nm, dw, land in zip(names, dws, lands):
                mine[nm] = _sum_devices(land, dw, nm, layer, mine.get(nm))
        return _pair_layers(mine)


def _pair_layers(mine):
    names = list(BIG)

    def body(*refs):
        ins = refs[:len(names)]
        outs = refs[len(names):2 * len(names)]
        send_sems, recv_sems = refs[2 * len(names):]
        x, y, c = _my_place()
        sibling = (x, y, 1 - c)
        cps = []
        for t in range(len(names)):
            cp = pltpu.make_async_remote_copy(
                src_ref=ins[t].at[c], dst_ref=outs[t].at[c], send_sem=send_sems.at[t], recv_sem=recv_sems.at[t],
                device_id=sibling, device_id_type=MESH)
            cp.start()
            cps.append(cp)
        for t in range(len(names)):
            cps[t].wait_send()
            land = outs[t].at[1 - c]
            pltpu.make_async_remote_copy(
                src_ref=land, dst_ref=land, send_sem=send_sems.at[t], recv_sem=recv_sems.at[t],
                device_id=sibling, device_id_type=MESH).wait_recv()

    outs = pl.pallas_call(
        body, name="pair_layers", in_specs=[HBM] * len(names), out_specs=[HBM] * len(names),
        out_shape=[jax.ShapeDtypeStruct((2,) + _shard_shape(n), f32) for n in names],
        input_output_aliases={t: t for t in range(len(names))},
        scratch_shapes=[pltpu.SemaphoreType.DMA((len(names),)), pltpu.SemaphoreType.DMA((len(names),))],
    )(*[mine[n] for n in names])
    return dict(zip(names, outs))


SMALL_ROWS = 320


def _small_copies(vec_ref, land_ref, send_sems, recv_sems):
    x, y, c = _my_place()
    me = 4 * x + 2 * y + c
    out = []
    for k in range(1, N_DEV):
        peer = (x ^ ((k >> 2) & 1), y ^ ((k >> 1) & 1), c ^ (k & 1))
        src_dev = 4 * peer[0] + 2 * peer[1] + peer[2]
        send = pltpu.make_async_remote_copy(
            src_ref=vec_ref, dst_ref=land_ref.at[me], send_sem=send_sems.at[k - 1], recv_sem=recv_sems.at[k - 1],
            device_id=peer, device_id_type=MESH)
        arrival = pltpu.make_async_remote_copy(
            src_ref=land_ref.at[src_dev], dst_ref=land_ref.at[src_dev], send_sem=send_sems.at[k - 1],
            recv_sem=recv_sems.at[k - 1], device_id=peer, device_id_type=MESH)
        out.append((send, arrival))
    return out


def _small_start(vec):
    land = lax.empty((N_DEV,) + vec.shape, vec.dtype)

    def body(v_ref, land_in, send_sems, recv_sems, v_out, land_out):
        del land_in, v_out
        for send, _ in _small_copies(v_ref, land_out, send_sems, recv_sems):
            send.start()

    return pl.pallas_call(
        body, name="small_start",
        out_shape=(pltpu.SemaphoreType.DMA((N_DEV - 1,)), pltpu.SemaphoreType.DMA((N_DEV - 1,)),
                   pltpu.HBM(vec.shape, vec.dtype), pltpu.HBM(land.shape, land.dtype)),
        in_specs=[HBM, HBM], out_specs=(SEM, SEM, HBM, HBM), input_output_aliases={0: 2, 1: 3},
        compiler_params=pltpu.CompilerParams(has_side_effects=EFFECT),
    )(pltpu.with_memory_space_constraint(vec, pltpu.HBM), pltpu.with_memory_space_constraint(land, pltpu.HBM))


def _small_wait(send_sems, recv_sems, vec, land, after):
    def body(v_ref, land_ref, send_ref, recv_ref, after_ref, v_out, land_out):
        del after_ref, v_out, land_out
        for send, arrival in _small_copies(v_ref, land_ref, send_ref, recv_ref):
            send.wait_send()
            arrival.wait_recv()

    return pl.pallas_call(
        body, name="small_wait", out_shape=(pltpu.HBM(vec.shape, vec.dtype), pltpu.HBM(land.shape, land.dtype)),
        in_specs=[HBM, HBM, SEM, SEM, pl.BlockSpec(memory_space=pl.ANY)], out_specs=(HBM, HBM),
        input_output_aliases={0: 0, 1: 1}, compiler_params=pltpu.CompilerParams(has_side_effects=EFFECT),
    )(vec, land, send_sems, recv_sems, after)


def _small_sum(vec, land):
    x, y, c = _my_place()

    def body(dev_ref, v_ref, land_ref, out_ref):
        acc = None
        for s in range(N_DEV):
            term = jnp.where(dev_ref[0] == s, v_ref[...], land_ref[s])
            acc = term if acc is None else acc + term
        out_ref[...] = acc

    return pl.pallas_call(
        body, name="small_sum",
        grid_spec=pltpu.PrefetchScalarGridSpec(
            num_scalar_prefetch=1, grid=(1,),
            in_specs=[pl.BlockSpec(vec.shape, lambda i, dev: (0, 0)), pl.BlockSpec(land.shape, lambda i, dev: (0, 0, 0))],
            out_specs=pl.BlockSpec(vec.shape, lambda i, dev: (0, 0))),
        out_shape=jax.ShapeDtypeStruct(vec.shape, vec.dtype),
        compiler_params=_cparams(("arbitrary",)),
    )(jnp.reshape(4 * x + 2 * y + c, (1,)).astype(jnp.int32), vec, land)


def _adamw(w, g, m, v, name):
    rows, cols = w.shape
    tr = rows
    for cand in (512, 256, 128, 64, 32, 16, 8):
        if rows % cand == 0 and cand * cols * 4 <= 2 * 1024 * 1024:
            tr = cand
            break
    c1 = np.float32(1.0 - ADAM_B1 ** ADAM_STEP)
    c2 = np.float32(1.0 - ADAM_B2 ** ADAM_STEP)

    def body(w_ref, g_ref, m_ref, v_ref, go_ref, d_ref, mo_ref, vo_ref):
        gv = g_ref[...]
        go_ref[...] = gv
        mn = ADAM_B1 * m_ref[...] + (1.0 - ADAM_B1) * gv
        vn = ADAM_B2 * v_ref[...] + (1.0 - ADAM_B2) * (gv * gv)
        mo_ref[...] = mn
        vo_ref[...] = vn
        d_ref[...] = -ADAM_LR * ((mn / c1) / (jnp.sqrt(vn / c2) + ADAM_EPS) + ADAM_WD * w_ref[...])

    blk = pl.BlockSpec((tr, cols), lambda i: (i, 0))
    return pl.pallas_call(
        body, name="adamw_" + name, grid=(rows // tr,), in_specs=[blk] * 4, out_specs=[blk] * 4,
        out_shape=[jax.ShapeDtypeStruct((rows, cols), f32)] * 4,
        compiler_params=_cparams(("parallel",)),
    )(w, g, m, v)


SMALL = ("norm1", "pool_w", "pool_scale", "norm2", "norm3", "final_norm")
ORDER = ("norm1", "w_in", "pool_w", "pool_scale", "w_out", "norm2", "w_up", "w_down", "norm3", "w_gate", "w_ple",
         "final_norm")


def _pack_small(tree, extra=None):
    parts = [tree[n].reshape(-1) for n in SMALL]
    if extra is not None:
        parts.append(extra.reshape(-1))
    flat = jnp.concatenate(parts)
    return jnp.pad(flat, (0, SMALL_ROWS * 128 - flat.shape[0])).reshape(SMALL_ROWS, 128)


def _unpack_small(packed, like):
    flat = packed.reshape(-1)
    out, off = {}, 0
    for n in SMALL:
        size = int(np.prod(like[n].shape))
        out[n] = flat[off:off + size].reshape(like[n].shape)
        off += size
    return out, flat[off]


def kernel(x, p, positions, norm1, w_in, pool_w, pool_scale, w_out, norm2, w_up, w_down, norm3, w_gate, w_ple, final_norm, loss_target, m_norm1, m_w_in, m_pool_w, m_pool_scale, m_w_out, m_norm2, m_w_up, m_w_down, m_norm3, m_w_gate, m_w_ple, m_final_norm, v_norm1, v_w_in, v_pool_w, v_pool_scale, v_w_out, v_norm2, v_w_up, v_w_down, v_norm3, v_w_gate, v_w_ple, v_final_norm):
    w = dict(norm1=norm1, w_in=w_in, pool_w=pool_w, pool_scale=pool_scale, w_out=w_out, norm2=norm2, w_up=w_up,
             w_down=w_down, norm3=norm3, w_gate=w_gate, w_ple=w_ple, final_norm=final_norm)
    m = dict(norm1=m_norm1, w_in=m_w_in, pool_w=m_pool_w, pool_scale=m_pool_scale, w_out=m_w_out, norm2=m_norm2,
             w_up=m_w_up, w_down=m_w_down, norm3=m_norm3, w_gate=m_w_gate, w_ple=m_w_ple, final_norm=m_final_norm)
    v = dict(norm1=v_norm1, w_in=v_w_in, pool_w=v_pool_w, pool_scale=v_pool_scale, w_out=v_w_out, norm2=v_norm2,
             w_up=v_w_up, w_down=v_w_down, norm3=v_norm3, w_gate=v_w_gate, w_ple=v_w_ple, final_norm=v_final_norm)
    small = {n: w[n] for n in SMALL}

    wsrc = _GatheredWeights({n: w[n] for n in BIG})
    reducer = _GradReducer()
    loss8, dx, small_grads = _local_step(x[0], p.reshape(2, x.shape[1], PLE_DIM), positions[0], wsrc, small, loss_target[0], reducer)
    s_send, s_recv, s_vec, s_land = _small_start(_pack_small(small_grads, loss8[0, 0]))
    gsh = reducer.finish(s_vec)

    g_out, d_out, m_out, v_out = {}, {}, {}, {}
    for n in BIG:
        shp = w[n].shape
        two = lambda a: a.reshape(shp[0] * shp[1], shp[2])
        g2, d2, m2, v2 = _adamw(two(w[n]), two(gsh[n]), two(m[n]), two(v[n]), n)
        g_out[n], d_out[n], m_out[n], v_out[n] = g2.reshape(shp), d2.reshape(shp), m2.reshape(shp), v2.reshape(shp)
    red = _small_sum(*_small_wait(s_send, s_recv, s_vec, s_land, d2))
    g_small, loss = _unpack_small(red, small)
    _, d2, m2, v2 = _adamw(_pack_small(small), red, _pack_small({n: m[n] for n in SMALL}),
                           _pack_small({n: v[n] for n in SMALL}), "small")
    for tree, packed in ((d_out, d2), (m_out, m2), (v_out, v2)):
        tree.update(_unpack_small(packed, small)[0])
    g_out.update(g_small)

    return (loss, dx[None], *[g_out[n] for n in ORDER], *[d_out[n] for n in ORDER], *[m_out[n] for n in ORDER],
            *[v_out[n] for n in ORDER])
```

```python
import jax
import jax.numpy as jnp
import numpy as np
from jax import lax
from jax.experimental import pallas as pl
from jax.experimental.pallas import tpu as pltpu

f32 = jnp.float32
MXU_DTYPE = jnp.bfloat16
COMM_DTYPE = jnp.bfloat16

D_MODEL = 1024
POOL_WIDTH = 256
POOL_GC = 64
ATTN_WIDTH = 768
HEAD_DIM = 64
N_IN = POOL_WIDTH + 3 * ATTN_WIDTH
D_FF = 4096
PLE_DIM = 256
BLK = 128
DILATIONS = (1, 4, 16)
ROT_DIM = 16
ROPE_THETA = 500000.0
EPS = 1e-6
ATTN_SCALE = HEAD_DIM ** -0.5
NEG_BIG = -1e30

ADAM_LR, ADAM_B1, ADAM_B2, ADAM_EPS, ADAM_WD, ADAM_STEP = 0.001, 0.9, 0.999, 1e-08, 0.01, 10

TM = 512
TM_WGRAD = 1024
HALO = 16
VMEM_LIMIT = 48 * 1024 * 1024
VMEM_LIMIT_LARGE = 58 * 1024 * 1024
N_CHIPS = 4
MESH = pl.DeviceIdType.MESH

BIG = ("w_in", "w_out", "w_up", "w_down", "w_gate", "w_ple")
FULL_SHAPE = {"w_in": (D_MODEL, N_IN), "w_out": (D_MODEL, D_MODEL), "w_up": (D_MODEL, D_FF),
              "w_down": (D_FF, D_MODEL), "w_gate": (D_MODEL, D_MODEL), "w_ple": (PLE_DIM, D_MODEL)}
COL_SHARDED = {"w_in": True, "w_out": False, "w_up": True, "w_down": False, "w_gate": False, "w_ple": True}


def _shard_shape(name):
    k, n = FULL_SHAPE[name]
    return (k, n // N_CHIPS) if COL_SHARDED[name] else (k // N_CHIPS, n)


def _cparams(sem=None, vmem=VMEM_LIMIT):
    return pltpu.CompilerParams(dimension_semantics=sem, vmem_limit_bytes=vmem)


def _resident(block_shape, index_map):
    return pl.BlockSpec(block_shape, index_map, pipeline_mode=pl.Buffered(1))


def _mx(x):
    return x.astype(MXU_DTYPE)


def _dot(a, b):
    return jnp.dot(a, b, preferred_element_type=f32)


def _dot_nt(a, b):
    return lax.dot_general(a, b, (((1,), (1,)), ((), ())), preferred_element_type=f32)


def _dot_tn(a, b):
    return lax.dot_general(a, b, (((0,), (0,)), ((), ())), preferred_element_type=f32)


def _sigmoid(x):
    return 1.0 / (1.0 + jnp.exp(-x))


def _rope_apply(y, c, s1, s2, width):
    return y * c + pltpu.roll(y, width - 8, axis=1) * s1 + pltpu.roll(y, 8, axis=1) * s2


def _rope_transpose(dy, c, s1, s2, width):
    return dy * c + pltpu.roll(dy * s1, 8, axis=1) + pltpu.roll(dy * s2, width - 8, axis=1)


def _norm_matmul(h, g, w, layer, tn, name, rope=None):
    s_len, d = h.shape
    n = w.shape[2]

    def body(*refs):
        if rope is None:
            h_ref, g_ref, w_ref, y_ref, hn_ref = refs
        else:
            h_ref, g_ref, w_ref, c_ref, s1_ref, s2_ref, y_ref, hn_ref = refs
            reps = tn // 128
            c = jnp.concatenate([c_ref[...]] * reps, axis=1)
            s1 = jnp.concatenate([s1_ref[...]] * reps, axis=1)
            s2 = jnp.concatenate([s2_ref[...]] * reps, axis=1)
        x = h_ref[...]
        r = lax.rsqrt(jnp.mean(x * x, axis=-1, keepdims=True) + EPS)
        hn = ((x * r) * g_ref[...]).astype(hn_ref.dtype)
        hn_ref[...] = hn
        for j in range(n // tn):
            y = _dot(hn, w_ref[:, j * tn:(j + 1) * tn])
            if rope is not None and POOL_WIDTH <= j * tn < POOL_WIDTH + 2 * ATTN_WIDTH:
                y = _rope_apply(y, c, s1, s2, tn)
            y_ref[:, j * tn:(j + 1) * tn] = y

    in_specs = [pl.BlockSpec((TM, d), lambda i: (i, 0)),
                pl.BlockSpec((1, d), lambda i: (0, 0)),
                _resident((None, d, n), lambda i: (layer, 0, 0))]
    args = [h, g, w]
    if rope is not None:
        assert POOL_WIDTH % tn == 0 and (2 * ATTN_WIDTH) % tn == 0
        in_specs += [pl.BlockSpec((TM, 128), lambda i: (i, 0))] * 3
        args += list(rope)
    return pl.pallas_call(
        body, name=name, grid=(s_len // TM,), in_specs=in_specs,
        out_specs=[pl.BlockSpec((TM, n), lambda i: (i, 0)), pl.BlockSpec((TM, d), lambda i: (i, 0))],
        out_shape=[jax.ShapeDtypeStruct((s_len, n), f32), jax.ShapeDtypeStruct((s_len, d), MXU_DTYPE)],
        compiler_params=_cparams(("parallel",)),
    )(*args)


def _gate_ple_fwd(h2, g, w_gate, w_ple, layer, p, p_layer, name, head=None):
    s_len, d = h2.shape

    def body(h_ref, g_ref, wg_ref, p_ref, wp_ref, *rest):
        gl_ref, hn_ref = rest[-2:]
        x = h_ref[...]
        r = lax.rsqrt(jnp.mean(x * x, axis=-1, keepdims=True) + EPS)
        hn = ((x * r) * g_ref[...]).astype(hn_ref.dtype)
        hn_ref[...] = hn
        gl = _dot(hn, wg_ref[...])
        gl_ref[...] = gl.astype(gl_ref.dtype)
        h3 = x + _sigmoid(gl) * _dot(_mx(p_ref[...]), wp_ref[...])
        if head is None:
            rest[0][...] = h3
            return
        gf_ref, t_ref, loss_ref, dh_ref, dgf_ref = rest[:5]
        i = pl.program_id(0)
        gv = gf_ref[...]
        r3 = lax.rsqrt(jnp.mean(h3 * h3, axis=-1, keepdims=True) + EPS)
        xh = h3 * r3
        diff = xh * gv - t_ref[...]
        part = 0.5 * jnp.sum(jnp.mean(diff * diff, axis=-1, keepdims=True), axis=0, keepdims=True)
        dy = diff * (1.0 / d)
        dxh = dy * gv
        dh_ref[...] = r3 * (dxh - xh * jnp.mean(dxh * xh, axis=-1, keepdims=True))
        dgsum = jnp.sum(dy * xh, axis=0, keepdims=True)
        lossb = jnp.broadcast_to(part, (8, 128))

        @pl.when(i == 0)
        def _():
            loss_ref[...] = lossb
            dgf_ref[...] = dgsum

        @pl.when(i > 0)
        def _():
            loss_ref[...] += lossb
            dgf_ref[...] += dgsum

    row = lambda i: (i, 0)
    one = lambda i: (0, 0)
    in_specs = [pl.BlockSpec((TM, d), row), pl.BlockSpec((1, d), one),
                pl.BlockSpec((None, d, d), lambda i: (layer, 0, 0)),
                pl.BlockSpec((None, TM, PLE_DIM), lambda i: (p_layer, i, 0)),
                pl.BlockSpec((None, PLE_DIM, d), lambda i: (layer, 0, 0))]
    args = [h2, g, w_gate, p, w_ple]
    saved = [jax.ShapeDtypeStruct((s_len, d), MXU_DTYPE)] * 2
    if head is None:
        out_specs = [pl.BlockSpec((TM, d), row)] * 3
        out_shape = [jax.ShapeDtypeStruct((s_len, d), f32)] + saved
    else:
        in_specs += [pl.BlockSpec((1, d), one), pl.BlockSpec((TM, d), row)]
        args += list(head)
        out_specs = [pl.BlockSpec((8, 128), one), pl.BlockSpec((TM, d), row), pl.BlockSpec((1, d), one)] \
            + [pl.BlockSpec((TM, d), row)] * 2
        out_shape = [jax.ShapeDtypeStruct((8, 128), f32), jax.ShapeDtypeStruct((s_len, d), f32),
                     jax.ShapeDtypeStruct((1, d), f32)] + saved
    return pl.pallas_call(
        body, name=name, grid=(s_len // TM,), in_specs=in_specs, out_specs=out_specs, out_shape=out_shape,
        compiler_params=_cparams(("arbitrary",)),
    )(*args)


def _gate_bwd(dh3, gl, p, p_layer, w_ple, w_gate, layer, h2, g, name):
    s_len, d = dh3.shape

    def body(dh_ref, gl_ref, p_ref, wp_ref, wg_ref, h_ref, g_ref, dh2_ref, dg_ref, de_ref, dgl_ref):
        i = pl.program_id(0)
        dh = dh_ref[...]
        gate = _sigmoid(gl_ref[...].astype(f32))
        e = _dot(_mx(p_ref[...]), wp_ref[...])
        de_ref[...] = (dh * gate).astype(de_ref.dtype)
        dgl = ((dh * e) * (gate * (1.0 - gate))).astype(dgl_ref.dtype)
        dgl_ref[...] = dgl
        dx, dgrow = _rmsnorm_bwd(_dot_nt(dgl, wg_ref[...]), h_ref[...], g_ref[...])
        dh2_ref[...] = dh + dx
        dgsum = jnp.sum(dgrow, axis=0, keepdims=True)

        @pl.when(i == 0)
        def _():
            dg_ref[...] = dgsum

        @pl.when(i > 0)
        def _():
            dg_ref[...] += dgsum

    row = lambda i: (i, 0)
    blk = pl.BlockSpec((TM, d), row)
    return pl.pallas_call(
        body, name=name, grid=(s_len // TM,),
        in_specs=[blk, blk, pl.BlockSpec((None, TM, PLE_DIM), lambda i: (p_layer, i, 0)),
                  _resident((None, PLE_DIM, d), lambda i: (layer, 0, 0)),
                  _resident((None, d, d), lambda i: (layer, 0, 0)), blk, pl.BlockSpec((1, d), lambda i: (0, 0))],
        out_specs=[blk, pl.BlockSpec((1, d), lambda i: (0, 0)), blk, blk],
        out_shape=[jax.ShapeDtypeStruct((s_len, d), f32), jax.ShapeDtypeStruct((1, d), f32),
                   jax.ShapeDtypeStruct((s_len, d), MXU_DTYPE), jax.ShapeDtypeStruct((s_len, d), MXU_DTYPE)],
        compiler_params=_cparams(("arbitrary",)),
    )(dh3, gl, p, w_ple, w_gate, h2, g)


def _rmsnorm_bwd(dhn, x, g):
    r = lax.rsqrt(jnp.mean(x * x, axis=-1, keepdims=True) + EPS)
    xh = x * r
    dxh = dhn * g
    dx = r * (dxh - xh * jnp.mean(dxh * xh, axis=-1, keepdims=True))
    return dx, dhn * xh


def _matmul_nt_norm_bwd(dy, w, layer, h_prev, g, dres, name, tk=1024, after=None):
    s_len, k_dim = dy.shape
    d = h_prev.shape[1]

    def body(dy_ref, w_ref, h_ref, g_ref, dres_ref, *rest):
        dh_ref, dg_ref = rest[-2:]
        i = pl.program_id(0)
        acc = None
        for k in range(k_dim // tk):
            part = _dot_nt(_mx(dy_ref[:, k * tk:(k + 1) * tk]), w_ref[:, k * tk:(k + 1) * tk])
            acc = part if acc is None else acc + part
        dx, dgrow = _rmsnorm_bwd(acc, h_ref[...], g_ref[...])
        dh_ref[...] = dres_ref[...] + dx
        dgsum = jnp.sum(dgrow, axis=0, keepdims=True)

        @pl.when(i == 0)
        def _():
            dg_ref[...] = dgsum

        @pl.when(i > 0)
        def _():
            dg_ref[...] += dgsum

    in_specs = [pl.BlockSpec((TM, k_dim), lambda i: (i, 0)),
                _resident((None, d, k_dim), lambda i: (layer, 0, 0)),
                pl.BlockSpec((TM, d), lambda i: (i, 0)),
                pl.BlockSpec((1, d), lambda i: (0, 0)),
                pl.BlockSpec((TM, d), lambda i: (i, 0))]
    args = [dy, w, h_prev, g, dres]
    if after is not None:
        in_specs.append(pl.BlockSpec(memory_space=pl.ANY))
        args.append(after)
    return pl.pallas_call(
        body, name=name, grid=(s_len // TM,), in_specs=in_specs,
        out_specs=[pl.BlockSpec((TM, d), lambda i: (i, 0)), pl.BlockSpec((1, d), lambda i: (0, 0))],
        out_shape=[jax.ShapeDtypeStruct((s_len, d), f32), jax.ShapeDtypeStruct((1, d), f32)],
        compiler_params=_cparams(("arbitrary",)),
    )(*args)


def _mlp_fwd(h1, g, w_up, w_down, layer, name, tf=1024):
    s_len, d = h1.shape
    ff = w_up.shape[2]

    def body(h_ref, g_ref, wu_ref, wd_ref, h2_ref, a_ref, hn_ref):
        x = h_ref[...]
        r = lax.rsqrt(jnp.mean(x * x, axis=-1, keepdims=True) + EPS)
        hn = ((x * r) * g_ref[...]).astype(hn_ref.dtype)
        hn_ref[...] = hn
        acc = x
        for j in range(ff // tf):
            a = _dot(hn, wu_ref[:, j * tf:(j + 1) * tf])
            a_ref[:, j * tf:(j + 1) * tf] = a.astype(a_ref.dtype)
            relu = jnp.maximum(a, 0.0)
            acc = acc + _dot(_mx(relu * relu), wd_ref[j * tf:(j + 1) * tf, :])
        h2_ref[...] = acc

    row = lambda i: (i, 0)
    return pl.pallas_call(
        body, name=name, grid=(s_len // TM,),
        in_specs=[pl.BlockSpec((TM, d), row), pl.BlockSpec((1, d), lambda i: (0, 0)),
                  _resident((None, d, ff), lambda i: (layer, 0, 0)), _resident((None, ff, d), lambda i: (layer, 0, 0))],
        out_specs=[pl.BlockSpec((TM, d), row), pl.BlockSpec((TM, ff), row), pl.BlockSpec((TM, d), row)],
        out_shape=[jax.ShapeDtypeStruct((s_len, d), f32), jax.ShapeDtypeStruct((s_len, ff), MXU_DTYPE),
                   jax.ShapeDtypeStruct((s_len, d), MXU_DTYPE)],
        compiler_params=_cparams(("parallel",)),
    )(h1, g, w_up, w_down)


def _mlp_bwd(dh2, w_down, w_up, layer, a, h1, g, name, tf=1024):
    s_len, d = dh2.shape
    ff = a.shape[1]

    def body(dh_ref, wd_ref, wu_ref, a_ref, h_ref, g_ref, dh1_ref, dg_ref, da_ref):
        i = pl.program_id(0)
        dh = dh_ref[...]
        dhb = _mx(dh)
        acc = None
        for j in range(ff // tf):
            cols = slice(j * tf, (j + 1) * tf)
            dact = _dot_nt(dhb, wd_ref[cols, :])
            da = (dact * (2.0 * jnp.maximum(a_ref[:, cols].astype(f32), 0.0))).astype(da_ref.dtype)
            da_ref[:, cols] = da
            part = _dot_nt(da, wu_ref[:, cols])
            acc = part if acc is None else acc + part
        dx, dgrow = _rmsnorm_bwd(acc, h_ref[...], g_ref[...])
        dh1_ref[...] = dh + dx
        dgsum = jnp.sum(dgrow, axis=0, keepdims=True)

        @pl.when(i == 0)
        def _():
            dg_ref[...] = dgsum

        @pl.when(i > 0)
        def _():
            dg_ref[...] += dgsum

    row = lambda i: (i, 0)
    return pl.pallas_call(
        body, name=name, grid=(s_len // TM,),
        in_specs=[pl.BlockSpec((TM, d), row), _resident((None, ff, d), lambda i: (layer, 0, 0)),
                  _resident((None, d, ff), lambda i: (layer, 0, 0)), pl.BlockSpec((TM, ff), row),
                  pl.BlockSpec((TM, d), row), pl.BlockSpec((1, d), lambda i: (0, 0))],
        out_specs=[pl.BlockSpec((TM, d), row), pl.BlockSpec((1, d), lambda i: (0, 0)), pl.BlockSpec((TM, ff), row)],
        out_shape=[jax.ShapeDtypeStruct((s_len, d), f32), jax.ShapeDtypeStruct((1, d), f32),
                   jax.ShapeDtypeStruct((s_len, ff), MXU_DTYPE)],
        compiler_params=_cparams(("arbitrary",), vmem=VMEM_LIMIT_LARGE),
    )(dh2, w_down, w_up, a, h1, g)


def _weight_grad(a, b, name, act=False, lead=None):
    s_len, k_dim = a.shape[-2:]
    n = b.shape[1]
    tka = min(k_dim, 2048)
    tnb = n if n <= 1024 else (2048 if n % 2048 == 0 else 640)
    tm = TM_WGRAD * (2 if tka * tnb <= 1024 * 1024 and s_len % (2 * TM_WGRAD) == 0 else 1)
    ns = s_len // tm

    def body(a_ref, b_ref, o_ref, acc_ref):
        s = pl.program_id(2)
        x = a_ref[...]
        if act:
            relu = jnp.maximum(x, 0.0)
            x = relu * relu

        @pl.when(s == 0)
        def _():
            acc_ref[...] = jnp.zeros_like(acc_ref)

        acc_ref[...] += _dot_tn(_mx(x), _mx(b_ref[...]))

        @pl.when(s == ns - 1)
        def _():
            o_ref[...] = acc_ref[...].astype(o_ref.dtype)

    if lead is None:
        a_spec = pl.BlockSpec((tm, tka), lambda i, j, s: (s, i))
    else:
        a_spec = pl.BlockSpec((None, tm, tka), lambda i, j, s: (lead, s, i))
    return pl.pallas_call(
        body, name=name, grid=(k_dim // tka, n // tnb, ns),
        in_specs=[a_spec, pl.BlockSpec((tm, tnb), lambda i, j, s: (s, j))],
        out_specs=pl.BlockSpec((tka, tnb), lambda i, j, s: (i, j)),
        out_shape=jax.ShapeDtypeStruct((k_dim, n), COMM_DTYPE),
        scratch_shapes=[pltpu.VMEM((tka, tnb), f32)],
        compiler_params=_cparams(("parallel", "parallel", "arbitrary")),
    )(a, b)


def _group_select(lane, x2, x4, x8, x16):
    grp = lane // POOL_GC
    return jnp.where(grp == 0, x2, jnp.where(grp == 1, x4, jnp.where(grp == 2, x8, x16)))


def _pool_window(lane):
    grp = lane // POOL_GC
    return jnp.where(grp == 0, 2, jnp.where(grp == 1, 4, jnp.where(grp == 2, 8, 16)))


def _pool_y(u, halo, i):
    xs = jnp.concatenate([jnp.where(i > 0, halo, 0.0), u], axis=0)
    s2 = xs + pltpu.roll(xs, 1, axis=0)
    s4 = s2 + pltpu.roll(s2, 2, axis=0)
    s8 = s4 + pltpu.roll(s4, 4, axis=0)
    s16 = s8 + pltpu.roll(s8, 8, axis=0)
    lane = lax.broadcasted_iota(jnp.int32, xs.shape, 1)
    sel = _group_select(lane, s2, s4, s8, s16)[HALO:, :]
    t = i * TM + lax.broadcasted_iota(jnp.int32, u.shape, 0)
    cnt = jnp.minimum(_pool_window(lax.broadcasted_iota(jnp.int32, u.shape, 1)), t + 1).astype(f32)
    return sel / cnt - u


def _group_weights(l0, l1, l2):
    mx = jnp.maximum(jnp.maximum(l0, l1), l2)
    e0, e1, e2 = jnp.exp(l0 - mx), jnp.exp(l1 - mx), jnp.exp(l2 - mx)
    den = e0 + e1 + e2
    return e0 / den, e1 / den, e2 / den


def _mixer_out_proj(z, wbd, scale, outs, lses, w_out, layer, h, name):
    s_len, d = h.shape

    def body(u_ref, halo_ref, wbd_ref, sc_ref, o0, o1, o2, l0, l1, l2, wo_ref, h_ref, m_ref, h1_ref):
        i = pl.program_id(0)
        y = _pool_y(u_ref[...], halo_ref[...], i)
        pool = _dot(_mx(y), wbd_ref[...]) * sc_ref[...]
        w0, w1, w2 = _group_weights(l0[...], l1[...], l2[...])
        m = jnp.concatenate([pool, o0[...] * w0, o1[...] * w1, o2[...] * w2], axis=1).astype(m_ref.dtype)
        m_ref[...] = m
        h1_ref[...] = h_ref[...] + _dot(m, wo_ref[...])

    row = lambda i: (i, 0)
    blk = pl.BlockSpec((TM, 256), row)
    grp = [pl.BlockSpec((TM, 256), lambda i, g=g: (i, g)) for g in range(3)]
    return pl.pallas_call(
        body, name=name, grid=(s_len // TM,),
        in_specs=[blk, pl.BlockSpec((HALO, 256), lambda i: (jnp.maximum(i * (TM // HALO) - 1, 0), 0)),
                  pl.BlockSpec((256, 256), lambda i: (0, 0)), pl.BlockSpec((1, 256), lambda i: (0, 0))] + grp + grp
        + [_resident((None, d, d), lambda i: (layer, 0, 0)), pl.BlockSpec((TM, d), row)],
        out_specs=[pl.BlockSpec((TM, d), row)] * 2,
        out_shape=[jax.ShapeDtypeStruct((s_len, d), MXU_DTYPE), jax.ShapeDtypeStruct((s_len, d), f32)],
        compiler_params=_cparams(("parallel",)),
    )(z, z, wbd, scale, outs, outs, outs, lses, lses, lses, w_out, h)


def _head_sums(x):
    r = lax.broadcasted_iota(jnp.int32, (256, 256), 0) // HEAD_DIM
    c = lax.broadcasted_iota(jnp.int32, (256, 256), 1) // HEAD_DIM
    ones = jnp.where(r == c, 1.0, 0.0).astype(jnp.bfloat16)
    hi = x.astype(jnp.bfloat16)
    lo = (x - hi.astype(f32)).astype(jnp.bfloat16)
    return _dot(hi, ones) + _dot(lo, ones)


def _out_combine_bwd(dh1, w_out, layer, outs, lses, name, after=None):
    s_len, d = dh1.shape

    def body(dh_ref, w_ref, o0, o1, o2, l0, l1, l2, *rest):
        dp_ref, do_ref, dl_ref = rest[-3:]
        dm = _dot_nt(_mx(dh_ref[...]), w_ref[...])
        dp_ref[...] = dm[:, :POOL_WIDTH]
        w = _group_weights(l0[...], l1[...], l2[...])
        da = [dm[:, POOL_WIDTH + 256 * g:POOL_WIDTH + 256 * (g + 1)] for g in range(3)]
        o = (o0[...], o1[...], o2[...])
        dw = [_head_sums(da[g] * o[g]) for g in range(3)]
        t = w[0] * dw[0] + w[1] * dw[1] + w[2] * dw[2]
        do_ref[...] = jnp.concatenate([da[g] * w[g] for g in range(3)], axis=1)
        dl_ref[...] = jnp.concatenate([w[g] * t for g in range(3)], axis=1)

    grp = [pl.BlockSpec((TM, 256), lambda i, g=g: (i, g)) for g in range(3)]
    in_specs = [pl.BlockSpec((TM, d), lambda i: (i, 0)), _resident((None, d, d), lambda i: (layer, 0, 0))] + grp + grp
    args = [dh1, w_out, outs, outs, outs, lses, lses, lses]
    if after is not None:
        in_specs.append(pl.BlockSpec(memory_space=pl.ANY))
        args.append(after)
    return pl.pallas_call(
        body, name=name, grid=(s_len // TM,), in_specs=in_specs,
        out_specs=[pl.BlockSpec((TM, POOL_WIDTH), lambda i: (i, 0))] + [pl.BlockSpec((TM, ATTN_WIDTH), lambda i: (i, 0))] * 2,
        out_shape=[jax.ShapeDtypeStruct((s_len, POOL_WIDTH), f32)] + [jax.ShapeDtypeStruct((s_len, ATTN_WIDTH), f32)] * 2,
        compiler_params=_cparams(("parallel",)),
    )(*args)


def _pool_bwd(z, dm, wbd, scale, name, after=None):
    s_len = z.shape[0]
    n_halo = s_len // HALO

    def body(u_ref, uh_ref, d_ref, dh_ref, wbd_ref, sc_ref, *rest):
        du_ref, dw_ref, dsc_ref = rest[-3:]
        i = pl.program_id(0)
        last = pl.num_programs(0) - 1
        y = _pool_y(u_ref[...], uh_ref[...], i)
        yb = _mx(y)
        dpo = d_ref[...]
        sc = sc_ref[...]
        dsc = jnp.sum(dpo * _dot(yb, wbd_ref[...]), axis=0, keepdims=True)
        dwp = _dot_tn(yb, _mx(dpo * sc))

        @pl.when(i == 0)
        def _():
            dsc_ref[...] = dsc
            dw_ref[...] = dwp

        @pl.when(i > 0)
        def _():
            dsc_ref[...] += dsc
            dw_ref[...] += dwp

        ext = jnp.concatenate([dpo, jnp.where(i < last, dh_ref[...], 0.0)], axis=0)
        dy = _dot_nt(_mx(ext * sc), wbd_ref[...])
        t = i * TM + lax.broadcasted_iota(jnp.int32, ext.shape, 0)
        lane = lax.broadcasted_iota(jnp.int32, ext.shape, 1)
        e = dy / jnp.minimum(_pool_window(lane), t + 1).astype(f32)
        rows = ext.shape[0]
        f2 = e + pltpu.roll(e, rows - 1, axis=0)
        f4 = f2 + pltpu.roll(f2, rows - 2, axis=0)
        f8 = f4 + pltpu.roll(f4, rows - 4, axis=0)
        f16 = f8 + pltpu.roll(f8, rows - 8, axis=0)
        du_ref[...] = (_group_select(lane, f2, f4, f8, f16) - dy)[:TM, :].astype(du_ref.dtype)

    row = lambda i: (i, 0)
    blk = pl.BlockSpec((TM, 256), row)
    extra = [] if after is None else [after]
    return pl.pallas_call(
        body, name=name, grid=(s_len // TM,),
        in_specs=[blk, pl.BlockSpec((HALO, 256), lambda i: (jnp.maximum(i * (TM // HALO) - 1, 0), 0)),
                  blk, pl.BlockSpec((HALO, 256), lambda i: (jnp.minimum((i + 1) * (TM // HALO), n_halo - 1), 0)),
                  pl.BlockSpec((256, 256), lambda i: (0, 0)), pl.BlockSpec((1, 256), lambda i: (0, 0))]
        + [pl.BlockSpec(memory_space=pl.ANY)] * len(extra),
        out_specs=[blk, pl.BlockSpec((256, 256), lambda i: (0, 0)), pl.BlockSpec((1, 256), lambda i: (0, 0))],
        out_shape=[jax.ShapeDtypeStruct((s_len, N_IN), MXU_DTYPE), jax.ShapeDtypeStruct((256, 256), f32),
                   jax.ShapeDtypeStruct((1, 256), f32)],
        compiler_params=_cparams(("arbitrary",)),
    )(z, z, dm, dm, wbd, scale, *extra)


def _tri_masks():
    qi = lax.broadcasted_iota(jnp.int32, (BLK, BLK), 0)
    ki = lax.broadcasted_iota(jnp.int32, (BLK, BLK), 1)
    return qi >= ki, ki >= qi


ATTN_SUPER_PER_STEP = (8, 4, 1)
Q_COL, K_COL, V_COL = POOL_WIDTH // 128, (POOL_WIDTH + ATTN_WIDTH) // 128, (POOL_WIDTH + 2 * ATTN_WIDTH) // 128


def _rows(ref, start, dil):
    if dil == 1:
        return ref[pl.ds(start, BLK), :]
    return ref[pl.ds(start, BLK, stride=dil), :]


ATTN_BLOCKS_TOGETHER = 8


def _set_rows(ref, start, dil, val):
    if dil == 1:
        ref[pl.ds(start, BLK), :] = val
    else:
        ref[pl.ds(start, BLK, stride=dil), :] = val


def _attn_fwd(z, g, prev, name):
    s_len = z.shape[0]
    dil, m = DILATIONS[g], ATTN_SUPER_PER_STEP[g]
    sbr = BLK * dil
    rows = sbr * m

    def body(*refs):
        q_ref, kc_ref, kp_ref, vc_ref, vp_ref = refs[:5]
        o_ref, l_ref = refs[-2:]
        st = pl.program_id(0)
        low, up = _tri_masks()
        head0 = lax.broadcasted_iota(jnp.int32, (BLK, 128), 1) < HEAD_DIM
        blocks = [(sb, r) for sb in range(m) for r in range(dil)]
        for g0 in range(0, len(blocks), ATTN_BLOCKS_TOGETHER):
            grp = blocks[g0:g0 + ATTN_BLOCKS_TOGETHER]
            loaded = []
            for sb, r in grp:
                base = sb * sbr + r
                if sb == 0:
                    kp, vp = _rows(kp_ref, r, dil), _rows(vp_ref, r, dil)
                else:
                    kp, vp = _rows(kc_ref, base - sbr, dil), _rows(vc_ref, base - sbr, dil)
                qs = _rows(q_ref, base, dil) * ATTN_SCALE
                loaded.append((_mx(jnp.where(head0, qs, 0.0)), _mx(jnp.where(head0, 0.0, qs)),
                               jnp.concatenate([_mx(kp), _mx(_rows(kc_ref, base, dil))], axis=0),
                               jnp.concatenate([_mx(vp), _mx(_rows(vc_ref, base, dil))], axis=0)))
            scores = [(_dot_nt(q0, k2), _dot_nt(q1, k2)) for q0, q1, k2, _ in loaded]
            soft = []
            for (sb, _), pair in zip(grp, scores):
                valid = jnp.concatenate([up & (st > 0) if sb == 0 else up, low], axis=1)
                heads = []
                for s in pair:
                    s = jnp.where(valid, s, NEG_BIG)
                    mx = jnp.max(s, axis=-1, keepdims=True)
                    e = jnp.exp(s - mx)
                    l = jnp.sum(e, axis=-1, keepdims=True)
                    heads.append((_mx(e / l), jnp.broadcast_to(mx + jnp.log(l), (BLK, 128))))
                soft.append(heads)
            for (sb, r), heads, (_, _, _, v2) in zip(grp, soft, loaded):
                base = sb * sbr + r
                _set_rows(o_ref, base, dil, jnp.where(head0, _dot(heads[0][0], v2), _dot(heads[1][0], v2)))
                _set_rows(l_ref, base, dil, jnp.where(head0, heads[0][1], heads[1][1]))

    def cur(col):
        return pl.BlockSpec((rows, 128), lambda st, hp: (st, col + 2 * g + hp))

    def before(col):
        return pl.BlockSpec((sbr, 128), lambda st, hp: (jnp.maximum(st * m - 1, 0), col + 2 * g + hp))

    in_specs = [cur(Q_COL), cur(K_COL), before(K_COL), cur(V_COL), before(V_COL)]
    args = [z, z, z, z, z]
    aliases = {}
    if prev is not None:
        in_specs += [pl.BlockSpec(memory_space=pl.ANY)] * 2
        args += list(prev)
        aliases = {5: 0, 6: 1}
    return pl.pallas_call(
        body, name=name, grid=(s_len // rows, 2), in_specs=in_specs, out_specs=[cur(0), cur(0)],
        out_shape=[jax.ShapeDtypeStruct((s_len, ATTN_WIDTH), f32)] * 2, input_output_aliases=aliases,
        compiler_params=_cparams(("parallel", "parallel")),
    )(*args)


def _stack_heads(x, head0):
    return jnp.concatenate([_mx(jnp.where(head0, x, 0.0)), _mx(jnp.where(head0, 0.0, x))], axis=0)


def _head_rows(x):
    xt = x.T
    return jnp.concatenate([jnp.broadcast_to(xt[0:1, :], (BLK, BLK)),
                            jnp.broadcast_to(xt[HEAD_DIM:HEAD_DIM + 1, :], (BLK, BLK))], axis=0)


def _attn_bwd(z, do, lse, dlt, tabs, dz, g, name):
    s_len = z.shape[0]
    dil, m = DILATIONS[g], ATTN_SUPER_PER_STEP[g]
    sbr = BLK * dil
    rows = sbr * m
    nsteps = s_len // rows

    def body(q_ref, qn_ref, kc_ref, kp_ref, vc_ref, vp_ref, do_ref, don_ref, l_ref, ln_ref, d_ref, dn_ref,
             c_ref, s1_ref, s2_ref, dz_in, dz_ref, dq_buf, dk_buf, dv_buf, out_buf, sems):
        del dz_in
        st, hp = pl.program_id(0), pl.program_id(1)
        head0 = lax.broadcasted_iota(jnp.int32, (BLK, 128), 1) < HEAD_DIM
        key_i = lax.broadcasted_iota(jnp.int32, (2 * BLK, BLK), 0) & (BLK - 1)
        query_i = lax.broadcasted_iota(jnp.int32, (2 * BLK, BLK), 1)
        same_t, cross_t = query_i >= key_i, key_i >= query_i
        def load(r):
            keys, vals = [_stack_heads(_rows(kp_ref, r, dil), head0)], [_stack_heads(_rows(vp_ref, r, dil), head0)]
            qs, dos, lses, dlts = [], [], [], []
            for sb in range(m):
                base = sb * sbr + r
                keys.append(_stack_heads(_rows(kc_ref, base, dil), head0))
                vals.append(_stack_heads(_rows(vc_ref, base, dil), head0))
                qs.append(_mx(_rows(q_ref, base, dil)))
                dos.append(_mx(_rows(do_ref, base, dil)))
                lses.append(_head_rows(_rows(l_ref, base, dil)))
                dlts.append(_head_rows(_rows(d_ref, base, dil)))
            qs.append(_mx(_rows(qn_ref, r, dil)))
            dos.append(_mx(_rows(don_ref, r, dil)))
            lses.append(_head_rows(_rows(ln_ref, r, dil)))
            dlts.append(_head_rows(_rows(dn_ref, r, dil)))
            return keys, vals, qs, dos, lses, dlts

        def products(data):
            keys, vals, qs, dos, _, _ = data
            return ([(_dot_nt(keys[j + 1], qs[j]), _dot_nt(vals[j + 1], dos[j])) for j in range(m)],
                    [(_dot_nt(keys[j], qs[j]), _dot_nt(vals[j], dos[j])) for j in range(m + 1)])

        def finish(data, raw):
            lses, dlts = data[4], data[5]

            def one(pair, j, valid):
                p = jnp.where(valid, jnp.exp(pair[0] * ATTN_SCALE - lses[j]), 0.0)
                return _mx(p), _mx(p * (pair[1] - dlts[j]) * ATTN_SCALE)

            same = [one(raw[0][j], j, same_t) for j in range(m)]
            cross = [one(raw[1][j], j, cross_t & (st > 0) if j == 0 else
                         (cross_t & (st < nsteps - 1) if j == m else cross_t)) for j in range(m + 1)]
            return same, cross

        def gradients(r, data, fin):
            keys, _, qs, dos, _, _ = data
            same, cross = fin
            for sb in range(m):
                base = sb * sbr + r
                (p_a, ds_a), (_, ds_x), (p_n, ds_n) = same[sb], cross[sb], cross[sb + 1]
                dq = _dot_tn(ds_a, keys[sb + 1]) + _dot_tn(ds_x, keys[sb])
                dk2 = _dot(ds_a, qs[sb]) + _dot(ds_n, qs[sb + 1])
                dv2 = _dot(p_a, dos[sb]) + _dot(p_n, dos[sb + 1])
                _set_rows(dq_buf, base, dil, dq)
                _set_rows(dk_buf, base, dil, jnp.where(head0, dk2[:BLK], dk2[BLK:]))
                _set_rows(dv_buf, base, dil, jnp.where(head0, dv2[:BLK], dv2[BLK:]))

        def residue_group(rg, carry):
            rs = [rg * group + i for i in range(group)]
            data = [load(r) for r in rs]
            raws = [products(d) for d in data]
            fins = [finish(d, raw) for d, raw in zip(data, raws)]
            for r, d, fin in zip(rs, data, fins):
                gradients(r, d, fin)
            return carry

        group = max(1, min(dil, ATTN_BLOCKS_TOGETHER // m))
        if dil // group <= 2:
            for rg in range(dil // group):
                residue_group(rg, 0)
        else:
            lax.fori_loop(0, dil // group, residue_group, 0)
        copies = []
        for t, (buf, col) in enumerate(((dq_buf, Q_COL), (dk_buf, K_COL), (dv_buf, V_COL))):
            val = buf[...]
            if t < 2:
                val = _rope_transpose(val, c_ref[...], s1_ref[...], s2_ref[...], 128)
            out_buf[t] = val.astype(out_buf.dtype)
            lane0 = pl.multiple_of((col + 2 * g + hp) * 128, 128)
            dst = dz_ref.at[pl.ds(pl.multiple_of(st * rows, rows), rows), pl.ds(lane0, 128)]
            cp = pltpu.make_async_copy(out_buf.at[t], dst, sems.at[t])
            cp.start()
            copies.append(cp)
        for cp in copies:
            cp.wait()

    def cur(col):
        return pl.BlockSpec((rows, 128), lambda st, hp: (st, col + 2 * g + hp))

    def before(col):
        return pl.BlockSpec((sbr, 128), lambda st, hp: (jnp.maximum(st * m - 1, 0), col + 2 * g + hp))

    def after(col):
        return pl.BlockSpec((sbr, 128), lambda st, hp: (jnp.minimum((st + 1) * m, s_len // sbr - 1), col + 2 * g + hp))

    tab = pl.BlockSpec((rows, 128), lambda st, hp: (st, 0))
    return pl.pallas_call(
        body, name=name, grid=(nsteps, 2),
        in_specs=[cur(Q_COL), after(Q_COL), cur(K_COL), before(K_COL), cur(V_COL), before(V_COL),
                  cur(0), after(0), cur(0), after(0), cur(0), after(0), tab, tab, tab,
                  pl.BlockSpec(memory_space=pl.ANY)],
        out_specs=pl.BlockSpec(memory_space=pl.ANY),
        out_shape=jax.ShapeDtypeStruct(dz.shape, dz.dtype), input_output_aliases={15: 0},
        scratch_shapes=[pltpu.VMEM((rows, 128), f32)] * 3 + [pltpu.VMEM((3, rows, 128), dz.dtype),
                                                            pltpu.SemaphoreType.DMA((3,))],
        compiler_params=_cparams(("arbitrary", "arbitrary")),
    )(z, z, z, z, z, z, do, do, lse, lse, dlt, dlt, *tabs, dz)


def _rope_tables(positions):
    inv_freq = ROPE_THETA ** (-jnp.arange(0, ROT_DIM, 2, dtype=f32) / ROT_DIM)
    ang = positions.astype(f32)[:, None] * inv_freq
    cos, sin = jnp.cos(ang), jnp.sin(ang)
    s_len = positions.shape[0]
    zero8, rest = jnp.zeros((s_len, 8), f32), jnp.zeros((s_len, HEAD_DIM - ROT_DIM), f32)
    c = jnp.concatenate([cos, cos, jnp.ones((s_len, HEAD_DIM - ROT_DIM), f32)], axis=1)
    s1 = jnp.concatenate([-sin, zero8, rest], axis=1)
    s2 = jnp.concatenate([zero8, sin, rest], axis=1)
    return tuple(jnp.tile(t, (1, 2)) for t in (c, s1, s2))


def _block_diag(pool_w):
    out = jnp.zeros((POOL_WIDTH, POOL_WIDTH), pool_w.dtype)
    for g in range(4):
        out = lax.dynamic_update_slice(out, pool_w[g], (g * POOL_GC, g * POOL_GC))
    return out


def _layer_fwd(h, p, wsrc, small, layer, tabs, head=None):
    nm = f"l{layer}_"
    wts, wl = wsrc.take(layer, ("w_in",), (h,) if layer else tuple(tabs))
    z, hn1 = _norm_matmul(h, small["norm1"][layer][None], wts["w_in"], wl, 256, nm + "in_proj", rope=tabs)
    ol = None
    for g in range(3):
        ol = _attn_fwd(z, g, ol, nm + f"attn_fwd{g}")
    outs, lses = ol
    wbd = _mx(_block_diag(small["pool_w"][layer]))
    scale = small["pool_scale"][layer][None]
    wts.update(wsrc.take(layer, ("w_out",), (outs,))[0])
    m, h1 = _mixer_out_proj(z, wbd, scale, outs, lses, wts["w_out"], wl, h, nm + "mixer_out")
    wts.update(wsrc.take(layer, ("w_up", "w_down"), (h1,))[0])
    h2, a, hn2 = _mlp_fwd(h1, small["norm2"][layer][None], wts["w_up"], wts["w_down"], wl, nm + "mlp")
    wts.update(wsrc.take(layer, ("w_gate", "w_ple"), (h2,))[0])
    *h3, gl, hn3 = _gate_ple_fwd(h2, small["norm3"][layer][None], wts["w_gate"], wts["w_ple"], wl, p, layer,
                                 nm + "gate_ple", head=head)
    saved = dict(h=h, z=z, hn1=hn1, outs=outs, lses=lses, wbd=wbd, scale=scale, m=m, h1=h1, a=a, hn2=hn2, h2=h2,
                 gl=gl, hn3=hn3, wts=wts, wl=wl)
    return h3, saved


def _layer_bwd(dh3, sv, p, small, layer, tabs128, reducer):
    nm = f"l{layer}_"
    wts, wl = sv["wts"], sv["wl"]
    dh2, dg3, de, dgl = _gate_bwd(dh3, sv["gl"], p, layer, wts["w_ple"], wts["w_gate"], wl, sv["h2"],
                                  small["norm3"][layer][None], nm + "gate_bwd")
    reducer.add("w_gate", layer, _weight_grad(sv["hn3"], dgl, nm + "dw_gate"))
    reducer.add("w_ple", layer, _weight_grad(p, de, nm + "dw_ple", lead=layer))
    dh1, dg2, da = _mlp_bwd(dh2, wts["w_down"], wts["w_up"], wl, sv["a"], sv["h1"], small["norm2"][layer][None],
                            nm + "mlp_bwd")
    reducer.add("w_down", layer, _weight_grad(sv["a"], dh2, nm + "dw_down", act=True))
    started = reducer.add("w_up", layer, _weight_grad(sv["hn2"], da, nm + "dw_up"))
    dpool, do, dlt = _out_combine_bwd(dh1, wts["w_out"], wl, sv["outs"], sv["lses"], nm + "out_bwd", after=started)
    started = reducer.add("w_out", layer, _weight_grad(sv["m"], dh1, nm + "dw_out"))
    dz, dwbd, dscale = _pool_bwd(sv["z"], dpool, sv["wbd"], sv["scale"], nm + "pool_bwd", after=started)
    for g in range(3):
        dz = _attn_bwd(sv["z"], do, sv["lses"], dlt, tabs128, dz, g, nm + f"attn_bwd{g}")
    started = reducer.add("w_in", layer, _weight_grad(sv["hn1"], dz, nm + "dw_in"))
    dh0, dg1 = _matmul_nt_norm_bwd(dz, wts["w_in"], wl, sv["h"], small["norm1"][layer][None], dh1, nm + "in_bwd",
                                   tk=512, after=started)
    dpool_w = jnp.stack([dwbd[g * POOL_GC:(g + 1) * POOL_GC, g * POOL_GC:(g + 1) * POOL_GC] for g in range(4)])
    sg = dict(norm1=dg1[0], norm2=dg2[0], norm3=dg3[0], pool_w=dpool_w, pool_scale=dscale[0])
    return dh0, sg


def _local_step(x, p, positions, wsrc, small, target, reducer):
    tabs128 = _rope_tables(positions)
    (h,), sv0 = _layer_fwd(x, p, wsrc, small, 0, tabs128)
    (loss, dh, dgf), sv1 = _layer_fwd(h, p, wsrc, small, 1, tabs128, head=(small["final_norm"][None], target))
    saved = [sv0, sv1]
    sgs = [None, None]
    for layer in (1, 0):
        dh, sgs[layer] = _layer_bwd(dh, saved[layer], p, small, layer, tabs128, reducer)
    small_grads = {k: jnp.stack([sgs[0][k], sgs[1][k]]) for k in sgs[0]}
    small_grads["final_norm"] = dgf[0]
    return loss, dh, small_grads


HBM = pl.BlockSpec(memory_space=pltpu.HBM)


def _my_place():
    return lax.axis_index("x"), lax.axis_index("y"), lax.axis_index("c")


def _other_chips(x, y):
    return [(1 - x, y), (x, 1 - y), (1 - x, 1 - y)]


def _window(ref, name, chip):
    k, n = _shard_shape(name)
    if COL_SHARDED[name]:
        return ref.at[:, pl.ds(pl.multiple_of(chip * n, 128), n)]
    return ref.at[pl.ds(pl.multiple_of(chip * k, 128), k), :]


def _chip_index():
    return jnp.reshape(2 * lax.axis_index("x") + lax.axis_index("y"), (1,)).astype(jnp.int32)


def _shard_block(name, tr):
    ks, ns = _shard_shape(name)
    if COL_SHARDED[name]:
        return (tr, ns), lambda i, me: (i, me[0])
    return (tr, ns), lambda i, me: (me[0] * (ks // tr) + i, 0)


def _place_shard(w, name, layer):
    ks, ns = _shard_shape(name)
    tr = min(ks, 256)
    shape, index = _shard_block(name, tr)

    def body(me_ref, w_ref, o_ref):
        o_ref[...] = w_ref[...].astype(o_ref.dtype)

    return pl.pallas_call(
        body, name=f"place_{name}{layer}",
        grid_spec=pltpu.PrefetchScalarGridSpec(
            num_scalar_prefetch=1, grid=(ks // tr,),
            in_specs=[pl.BlockSpec((None, tr, ns), lambda i, me: (layer, i, 0))],
            out_specs=pl.BlockSpec((None,) + shape, lambda i, me: (0,) + index(i, me))),
        out_shape=jax.ShapeDtypeStruct((1,) + FULL_SHAPE[name], MXU_DTYPE),
        compiler_params=_cparams(("parallel",)),
    )(_chip_index(), w)


GATHER_ORDER = [("w_in", 0), ("w_out", 0), ("w_up", 0), ("w_down", 0), ("w_gate", 0), ("w_ple", 0),
                ("w_in", 1), ("w_out", 1), ("w_up", 1), ("w_down", 1), ("w_gate", 1), ("w_ple", 1)]
SEM = pl.BlockSpec(memory_space=pltpu.SEMAPHORE)
EFFECT = pltpu.SideEffectType.DATAFLOW_SIDE_EFFECTING


def _gather_copy(src_ref, dst_ref, name, idx, j, chip, send_sems, recv_sems, c):
    cx, cy = chip
    return pltpu.make_async_remote_copy(
        src_ref=src_ref, dst_ref=dst_ref, send_sem=send_sems.at[3 * idx + j], recv_sem=recv_sems.at[3 * idx + j],
        device_id=(cx, cy, c), device_id_type=MESH)


def _gather_start(placed, order, tag, after=None):
    n = len(order)
    extra = [] if after is None else [after]

    def body(*refs):
        ins = refs[:n]
        k = n + len(extra)
        send_sems, recv_sems = refs[k], refs[k + 1]
        outs = refs[k + 2:k + 2 + n]
        token = refs[-1]
        x, y, c = _my_place()
        me = 2 * x + y
        for idx, (name, _) in enumerate(order):
            for j, chip in enumerate(_other_chips(x, y)):
                _gather_copy(_window(ins[idx].at[0], name, me), _window(outs[idx].at[0], name, me), name, idx, j, chip,
                             send_sems, recv_sems, c).start()
        token[...] = jnp.zeros_like(token)

    res = pl.pallas_call(
        body, name="gather_start" + tag,
        out_shape=(pltpu.SemaphoreType.DMA((3 * n,)), pltpu.SemaphoreType.DMA((3 * n,)))
        + tuple(pltpu.HBM(a.shape, a.dtype) for a in placed) + (jax.ShapeDtypeStruct((8, 128), f32),),
        in_specs=[HBM] * n + [pl.BlockSpec(memory_space=pl.ANY)] * len(extra),
        out_specs=(SEM, SEM) + (HBM,) * n + (pl.BlockSpec(memory_space=pltpu.VMEM),),
        input_output_aliases={i: i + 2 for i in range(n)},
        compiler_params=pltpu.CompilerParams(has_side_effects=EFFECT),
    )(*[pltpu.with_memory_space_constraint(a, pltpu.HBM) for a in placed], *extra)
    return res[0], res[1], list(res[2:2 + n]), res[-1]


def _gather_wait(send_sems, recv_sems, arrays, order, idxs, after, name):
    n = len(idxs)

    def body(*refs):
        ins = refs[:n]
        send_ref, recv_ref = refs[n], refs[n + 1]
        x, y, c = _my_place()
        me = 2 * x + y
        for k, idx in enumerate(idxs):
            wname = order[idx][0]
            for j, chip in enumerate(_other_chips(x, y)):
                cx, cy = chip
                mine = _window(ins[k].at[0], wname, me)
                land = _window(ins[k].at[0], wname, 2 * cx + cy)
                _gather_copy(mine, mine, wname, idx, j, chip, send_ref, recv_ref, c).wait_send()
                _gather_copy(land, land, wname, idx, j, chip, send_ref, recv_ref, c).wait_recv()

    operands = list(arrays) + [send_sems, recv_sems] + list(after)
    in_specs = [HBM] * n + [SEM, SEM] + [pl.BlockSpec(memory_space=pl.ANY)] * len(after)
    res = pl.pallas_call(
        body, name=name, out_shape=tuple(pltpu.HBM(a.shape, a.dtype) for a in arrays),
        in_specs=in_specs, out_specs=(HBM,) * n, input_output_aliases={i: i for i in range(n)},
        compiler_params=pltpu.CompilerParams(has_side_effects=EFFECT),
    )(*operands)
    return list(res)


class _GatheredWeights:
    def __init__(self, shards):
        self.starts = []
        token = None
        for tag, order in (("_first", GATHER_ORDER[:1]), ("_rest", GATHER_ORDER[1:])):
            placed = [_place_shard(shards[name], name, layer) for name, layer in order]
            self.starts.append((order,) + _gather_start(placed, order, tag, token))
            token = self.starts[-1][-1]

    def take(self, layer, names, after):
        order, send, recv, arrays, _ = next(s for s in self.starts if (names[0], layer) in s[0])
        after = list(after)
        if order is self.starts[0][0]:
            after.append(self.starts[-1][-1])
        idxs = [order.index((n, layer)) for n in names]
        got = _gather_wait(send, recv, [arrays[i] for i in idxs], order, idxs, after, f"gather_wait{layer}_{names[0]}")
        return dict(zip(names, got)), 0


N_DEV = 8


def _reduce_copies(dws, lands, names, layer, send_sems, recv_sems):
    x, y, c = _my_place()
    me, my_dev = 2 * x + y, 4 * x + 2 * y + c
    out = []
    for t, name in enumerate(names):
        for j, (cx, cy) in enumerate(_other_chips(x, y)):
            out.append((pltpu.make_async_remote_copy(
                src_ref=_window(dws[t], name, 2 * cx + cy), dst_ref=lands[t].at[my_dev],
                send_sem=send_sems.at[4 * t + j], recv_sem=recv_sems.at[N_DEV * t + my_dev],
                device_id=(cx, cy, layer), device_id_type=MESH), False))
        out.append((pltpu.make_async_remote_copy(
            src_ref=_window(dws[t], name, me), dst_ref=lands[t].at[my_dev],
            send_sem=send_sems.at[4 * t + 3], recv_sem=recv_sems.at[N_DEV * t + my_dev],
            device_id=(x, y, layer), device_id_type=MESH), True))
    return out


def _reduce_start(dws, names, layer, tag):
    n = len(names)
    lands = [lax.empty((N_DEV,) + _shard_shape(nm), dws[0].dtype) for nm in names]

    def body(*refs):
        ins = refs[:n]
        send_sems, recv_sems = refs[2 * n], refs[2 * n + 1]
        land_out = refs[3 * n + 2:4 * n + 2]
        token = refs[-1]
        c = lax.axis_index("c")
        for cp, non_owner_only in _reduce_copies(ins, land_out, names, layer, send_sems, recv_sems):
            if non_owner_only:
                @pl.when(c != layer)
                def _():
                    cp.start()
            else:
                cp.start()
        token[...] = jnp.zeros_like(token)

    res = pl.pallas_call(
        body, name="reduce_start" + tag,
        out_shape=(pltpu.SemaphoreType.DMA((4 * n,)), pltpu.SemaphoreType.DMA((N_DEV * n,)))
        + tuple(pltpu.HBM(a.shape, a.dtype) for a in dws) + tuple(pltpu.HBM(a.shape, a.dtype) for a in lands)
        + (jax.ShapeDtypeStruct((8, 128), f32),),
        in_specs=[HBM] * (2 * n),
        out_specs=(SEM, SEM) + (HBM,) * (2 * n) + (pl.BlockSpec(memory_space=pltpu.VMEM),),
        input_output_aliases={i: i + 2 for i in range(2 * n)},
        compiler_params=pltpu.CompilerParams(has_side_effects=EFFECT),
    )(*[pltpu.with_memory_space_constraint(a, pltpu.HBM) for a in list(dws) + lands])
    return res[0], res[1], list(res[2:2 + n]), list(res[2 + n:2 + 2 * n]), res[-1]


def _reduce_wait(send_sems, recv_sems, dws, lands, names, layer, after, tag):
    n = len(names)

    def body(*refs):
        ins, land_in = refs[:n], refs[n:2 * n]
        send_ref, recv_ref = refs[2 * n], refs[2 * n + 1]
        x, y, c = _my_place()
        for cp, non_owner_only in _reduce_copies(ins, land_in, names, layer, send_ref, recv_ref):
            if non_owner_only:
                @pl.when(c != layer)
                def _():
                    cp.wait_send()
            else:
                cp.wait_send()

        @pl.when(c == layer)
        def _():
            for t in range(n):
                for k in range(1, N_DEV):
                    px, py, pc = x ^ ((k >> 2) & 1), y ^ ((k >> 1) & 1), c ^ (k & 1)
                    dev = 4 * px + 2 * py + pc
                    land = land_in[t].at[dev]
                    pltpu.make_async_remote_copy(
                        src_ref=land, dst_ref=land, send_sem=send_ref.at[4 * t], recv_sem=recv_ref.at[N_DEV * t + dev],
                        device_id=(px, py, pc), device_id_type=MESH).wait_recv()

    res = pl.pallas_call(
        body, name="reduce_wait" + tag,
        out_shape=tuple(pltpu.HBM(a.shape, a.dtype) for a in list(dws) + list(lands)),
        in_specs=[HBM] * (2 * n) + [SEM, SEM, pl.BlockSpec(memory_space=pl.ANY)], out_specs=(HBM,) * (2 * n),
        input_output_aliases={i: i for i in range(2 * n)},
        compiler_params=pltpu.CompilerParams(has_side_effects=EFFECT),
    )(*dws, *lands, send_sems, recv_sems, after)
    return list(res[:n]), list(res[n:])


def _sum_devices(land, own, name, layer, prev):
    ks, ns = _shard_shape(name)
    tr = min(ks, 256)
    shape, index = _shard_block(name, tr)

    def body(me_ref, dev_ref, *refs):
        s_ref, own_ref, out_ref = refs[0], refs[1], refs[-1]
        dev = dev_ref[0]
        acc = None
        for s in range(N_DEV):
            term = jnp.where(dev == s, own_ref[...], s_ref[s]).astype(f32)
            acc = term if acc is None else acc + term
        out_ref[...] = acc

    def mine(i, dev):
        return i * jnp.where((dev[0] & 1) == layer, 1, 0)

    in_specs = [pl.BlockSpec((N_DEV, tr, ns), lambda i, me, dev: (0, mine(i, dev), 0)),
                pl.BlockSpec(shape, lambda i, me, dev: index(mine(i, dev), me))]
    args = [land, own]
    aliases = {}
    if prev is not None:
        in_specs.append(pl.BlockSpec(memory_space=pl.ANY))
        args.append(prev)
        aliases = {4: 0}
    x, y, c = _my_place()
    return pl.pallas_call(
        body, name=f"sum_devices_{name}{layer}",
        grid_spec=pltpu.PrefetchScalarGridSpec(
            num_scalar_prefetch=2, grid=(ks // tr,), in_specs=in_specs,
            out_specs=pl.BlockSpec((None, tr, ns), lambda i, me, dev: (layer, mine(i, dev), 0))),
        out_shape=jax.ShapeDtypeStruct((2, ks, ns), f32), input_output_aliases=aliases,
        compiler_params=_cparams(("arbitrary",)),
    )(_chip_index(), jnp.reshape(4 * x + 2 * y + c, (1,)).astype(jnp.int32), *args)


class _GradReducer:
    GROUPS = (("1", 1, ("w_gate", "w_ple", "w_down", "w_up", "w_out", "w_in")),
              ("0a", 0, ("w_gate", "w_ple", "w_down", "w_up")),
              ("0b", 0, ("w_out",)),
              ("0c", 0, ("w_in",)))

    def __init__(self):
        self.grads = {}
        self.started = {}

    def add(self, name, layer, dw):
        self.grads[(name, layer)] = dw
        token = None
        for tag, glayer, names in self.GROUPS:
            if tag not in self.started and all((nm, glayer) in self.grads for nm in names):
                *self.started[tag], token = _reduce_start([self.grads[(nm, glayer)] for nm in names], names, glayer, tag)
        return token

    def finish(self, after):
        mine = {}
        for tag, layer, names in self.GROUPS:
            send, recv, dws, lands = self.started[tag]
            dws, lands = _reduce_wait(send, recv, dws, lands, names, layer, after, tag)
            for nm, dw, land in zip(names, dws, lands):
                mine[nm] = _sum_devices(land, dw, nm, layer, mine.get(nm))
        return _pair_layers(mine)


def _pair_layers(mine):
    names = list(BIG)

    def body(*refs):
        ins = refs[:len(names)]
        outs = refs[len(names):2 * len(names)]
        send_sems, recv_sems = refs[2 * len(names):]
        x, y, c = _my_place()
        sibling = (x, y, 1 - c)
        cps = []
        for t in range(len(names)):
            cp = pltpu.make_async_remote_copy(
                src_ref=ins[t].at[c], dst_ref=outs[t].at[c], send_sem=send_sems.at[t], recv_sem=recv_sems.at[t],
                device_id=sibling, device_id_type=MESH)
            cp.start()
            cps.append(cp)
        for t in range(len(names)):
            cps[t].wait_send()
            land = outs[t].at[1 - c]
            pltpu.make_async_remote_copy(
                src_ref=land, dst_ref=land, send_sem=send_sems.at[t], recv_sem=recv_sems.at[t],
                device_id=sibling, device_id_type=MESH).wait_recv()

    outs = pl.pallas_call(
        body, name="pair_layers", in_specs=[HBM] * len(names), out_specs=[HBM] * len(names),
        out_shape=[jax.ShapeDtypeStruct((2,) + _shard_shape(n), f32) for n in names],
        input_output_aliases={t: t for t in range(len(names))},
        scratch_shapes=[pltpu.SemaphoreType.DMA((len(names),)), pltpu.SemaphoreType.DMA((len(names),))],
    )(*[mine[n] for n in names])
    return dict(zip(names, outs))


SMALL_ROWS = 320


def _small_copies(vec_ref, land_ref, send_sems, recv_sems):
    x, y, c = _my_place()
    me = 4 * x + 2 * y + c
    out = []
    for k in range(1, N_DEV):
        peer = (x ^ ((k >> 2) & 1), y ^ ((k >> 1) & 1), c ^ (k & 1))
        src_dev = 4 * peer[0] + 2 * peer[1] + peer[2]
        send = pltpu.make_async_remote_copy(
            src_ref=vec_ref, dst_ref=land_ref.at[me], send_sem=send_sems.at[k - 1], recv_sem=recv_sems.at[k - 1],
            device_id=peer, device_id_type=MESH)
        arrival = pltpu.make_async_remote_copy(
            src_ref=land_ref.at[src_dev], dst_ref=land_ref.at[src_dev], send_sem=send_sems.at[k - 1],
            recv_sem=recv_sems.at[k - 1], device_id=peer, device_id_type=MESH)
        out.append((send, arrival))
    return out


def _small_start(vec):
    land = lax.empty((N_DEV,) + vec.shape, vec.dtype)

    def body(v_ref, land_in, send_sems, recv_sems, v_out, land_out):
        del land_in, v_out
        for send, _ in _small_copies(v_ref, land_out, send_sems, recv_sems):
            send.start()

    return pl.pallas_call(
        body, name="small_start",
        out_shape=(pltpu.SemaphoreType.DMA((N_DEV - 1,)), pltpu.SemaphoreType.DMA((N_DEV - 1,)),
                   pltpu.HBM(vec.shape, vec.dtype), pltpu.HBM(land.shape, land.dtype)),
        in_specs=[HBM, HBM], out_specs=(SEM, SEM, HBM, HBM), input_output_aliases={0: 2, 1: 3},
        compiler_params=pltpu.CompilerParams(has_side_effects=EFFECT),
    )(pltpu.with_memory_space_constraint(vec, pltpu.HBM), pltpu.with_memory_space_constraint(land, pltpu.HBM))


def _small_wait(send_sems, recv_sems, vec, land, after):
    def body(v_ref, land_ref, send_ref, recv_ref, after_ref, v_out, land_out):
        del after_ref, v_out, land_out
        for send, arrival in _small_copies(v_ref, land_ref, send_ref, recv_ref):
            send.wait_send()
            arrival.wait_recv()

    return pl.pallas_call(
        body, name="small_wait", out_shape=(pltpu.HBM(vec.shape, vec.dtype), pltpu.HBM(land.shape, land.dtype)),
        in_specs=[HBM, HBM, SEM, SEM, pl.BlockSpec(memory_space=pl.ANY)], out_specs=(HBM, HBM),
        input_output_aliases={0: 0, 1: 1}, compiler_params=pltpu.CompilerParams(has_side_effects=EFFECT),
    )(vec, land, send_sems, recv_sems, after)


def _small_sum(vec, land):
    x, y, c = _my_place()

    def body(dev_ref, v_ref, land_ref, out_ref):
        acc = None
        for s in range(N_DEV):
            term = jnp.where(dev_ref[0] == s, v_ref[...], land_ref[s])
            acc = term if acc is None else acc + term
        out_ref[...] = acc

    return pl.pallas_call(
        body, name="small_sum",
        grid_spec=pltpu.PrefetchScalarGridSpec(
            num_scalar_prefetch=1, grid=(1,),
            in_specs=[pl.BlockSpec(vec.shape, lambda i, dev: (0, 0)), pl.BlockSpec(land.shape, lambda i, dev: (0, 0, 0))],
            out_specs=pl.BlockSpec(vec.shape, lambda i, dev: (0, 0))),
        out_shape=jax.ShapeDtypeStruct(vec.shape, vec.dtype),
        compiler_params=_cparams(("arbitrary",)),
    )(jnp.reshape(4 * x + 2 * y + c, (1,)).astype(jnp.int32), vec, land)


def _adamw(w, g, m, v, name):
    rows, cols = w.shape
    tr = rows
    for cand in (512, 256, 128, 64, 32, 16, 8):
        if rows % cand == 0 and cand * cols * 4 <= 2 * 1024 * 1024:
            tr = cand
            break
    c1 = np.float32(1.0 - ADAM_B1 ** ADAM_STEP)
    c2 = np.float32(1.0 - ADAM_B2 ** ADAM_STEP)

    def body(w_ref, g_ref, m_ref, v_ref, go_ref, d_ref, mo_ref, vo_ref):
        gv = g_ref[...]
        go_ref[...] = gv
        mn = ADAM_B1 * m_ref[...] + (1.0 - ADAM_B1) * gv
        vn = ADAM_B2 * v_ref[...] + (1.0 - ADAM_B2) * (gv * gv)
        mo_ref[...] = mn
        vo_ref[...] = vn
        d_ref[...] = -ADAM_LR * ((mn / c1) / (jnp.sqrt(vn / c2) + ADAM_EPS) + ADAM_WD * w_ref[...])

    blk = pl.BlockSpec((tr, cols), lambda i: (i, 0))
    return pl.pallas_call(
        body, name="adamw_" + name, grid=(rows // tr,), in_specs=[blk] * 4, out_specs=[blk] * 4,
        out_shape=[jax.ShapeDtypeStruct((rows, cols), f32)] * 4,
        compiler_params=_cparams(("parallel",)),
    )(w, g, m, v)


SMALL = ("norm1", "pool_w", "pool_scale", "norm2", "norm3", "final_norm")
ORDER = ("norm1", "w_in", "pool_w", "pool_scale", "w_out", "norm2", "w_up", "w_down", "norm3", "w_gate", "w_ple",
         "final_norm")


def _pack_small(tree, extra=None):
    parts = [tree[n].reshape(-1) for n in SMALL]
    if extra is not None:
        parts.append(extra.reshape(-1))
    flat = jnp.concatenate(parts)
    return jnp.pad(flat, (0, SMALL_ROWS * 128 - flat.shape[0])).reshape(SMALL_ROWS, 128)


def _unpack_small(packed, like):
    flat = packed.reshape(-1)
    out, off = {}, 0
    for n in SMALL:
        size = int(np.prod(like[n].shape))
        out[n] = flat[off:off + size].reshape(like[n].shape)
        off += size
    return out, flat[off]


def kernel(x, p, positions, norm1, w_in, pool_w, pool_scale, w_out, norm2, w_up, w_down, norm3, w_gate, w_ple, final_norm, loss_target, m_norm1, m_w_in, m_pool_w, m_pool_scale, m_w_out, m_norm2, m_w_up, m_w_down, m_norm3, m_w_gate, m_w_ple, m_final_norm, v_norm1, v_w_in, v_pool_w, v_pool_scale, v_w_out, v_norm2, v_w_up, v_w_down, v_norm3, v_w_gate, v_w_ple, v_final_norm):
    w = dict(norm1=norm1, w_in=w_in, pool_w=pool_w, pool_scale=pool_scale, w_out=w_out, norm2=norm2, w_up=w_up,
             w_down=w_down, norm3=norm3, w_gate=w_gate, w_ple=w_ple, final_norm=final_norm)
    m = dict(norm1=m_norm1, w_in=m_w_in, pool_w=m_pool_w, pool_scale=m_pool_scale, w_out=m_w_out, norm2=m_norm2,
             w_up=m_w_up, w_down=m_w_down, norm3=m_norm3, w_gate=m_w_gate, w_ple=m_w_ple, final_norm=m_final_norm)
    v = dict(norm1=v_norm1, w_in=v_w_in, pool_w=v_pool_w, pool_scale=v_pool_scale, w_out=v_w_out, norm2=v_norm2,
             w_up=v_w_up, w_down=v_w_down, norm3=v_norm3, w_gate=v_w_gate, w_ple=v_w_ple, final_norm=v_final_norm)
    small = {n: w[n] for n in SMALL}

    wsrc = _GatheredWeights({n: w[n] for n in BIG})
    reducer = _GradReducer()
    loss8, dx, small_grads = _local_step(x[0], p.reshape(2, x.shape[1], PLE_DIM), positions[0], wsrc, small, loss_target[0], reducer)
    s_send, s_recv, s_vec, s_land = _small_start(_pack_small(small_grads, loss8[0, 0]))
    gsh = reducer.finish(s_vec)

    g_out, d_out, m_out, v_out = {}, {}, {}, {}
    for n in BIG:
        shp = w[n].shape
        two = lambda a: a.reshape(shp[0] * shp[1], shp[2])
        g2, d2, m2, v2 = _adamw(two(w[n]), two(gsh[n]), two(m[n]), two(v[n]), n)
        g_out[n], d_out[n], m_out[n], v_out[n] = g2.reshape(shp), d2.reshape(shp), m2.reshape(shp), v2.reshape(shp)
    red = _small_sum(*_small_wait(s_send, s_recv, s_vec, s_land, d2))
    g_small, loss = _unpack_small(red, small)
    _, d2, m2, v2 = _adamw(_pack_small(small), red, _pack_small({n: m[n] for n in SMALL}),
                           _pack_small({n: v[n] for n in SMALL}), "small")
    for tree, packed in ((d_out, d2), (m_out, m2), (v_out, v2)):
        tree.update(_unpack_small(packed, small)[0])
    g_out.update(g_small)

    return (loss, dx[None], *[g_out[n] for n in ORDER], *[d_out[n] for n in ORDER], *[m_out[n] for n in ORDER],
            *[v_out[n] for n in ORDER])
```

```python
import jax
import jax.numpy as jnp
import numpy as np
from jax import lax
from jax.experimental import pallas as pl
from jax.experimental.pallas import tpu as pltpu

f32 = jnp.float32
MXU_DTYPE = jnp.bfloat16
COMM_DTYPE = jnp.bfloat16

D_MODEL = 1024
POOL_WIDTH = 256
POOL_GC = 64
ATTN_WIDTH = 768
HEAD_DIM = 64
N_IN = POOL_WIDTH + 3 * ATTN_WIDTH
D_FF = 4096
PLE_DIM = 256
BLK = 128
DILATIONS = (1, 4, 16)
ROT_DIM = 16
ROPE_THETA = 500000.0
EPS = 1e-6
ATTN_SCALE = HEAD_DIM ** -0.5
NEG_BIG = -1e30

ADAM_LR, ADAM_B1, ADAM_B2, ADAM_EPS, ADAM_WD, ADAM_STEP = 0.001, 0.9, 0.999, 1e-08, 0.01, 10

TM = 512
TM_WGRAD = 1024
HALO = 16
VMEM_LIMIT = 48 * 1024 * 1024
VMEM_LIMIT_LARGE = 58 * 1024 * 1024
VMEM_COMPILER_RESERVE = 6 * 1024 * 1024
N_CHIPS = 4
MESH = pl.DeviceIdType.MESH

BIG = ("w_in", "w_out", "w_up", "w_down", "w_gate", "w_ple")
FULL_SHAPE = {"w_in": (D_MODEL, N_IN), "w_out": (D_MODEL, D_MODEL), "w_up": (D_MODEL, D_FF),
              "w_down": (D_FF, D_MODEL), "w_gate": (D_MODEL, D_MODEL), "w_ple": (PLE_DIM, D_MODEL)}
COL_SHARDED = {"w_in": True, "w_out": False, "w_up": True, "w_down": False, "w_gate": False, "w_ple": True}


def _shard_shape(name):
    k, n = FULL_SHAPE[name]
    return (k, n // N_CHIPS) if COL_SHARDED[name] else (k // N_CHIPS, n)


def _cparams(sem=None, vmem=VMEM_LIMIT):
    return pltpu.CompilerParams(dimension_semantics=sem, vmem_limit_bytes=vmem)


def _resident(block_shape, index_map):
    return pl.BlockSpec(block_shape, index_map, pipeline_mode=pl.Buffered(1))


def _mx(x):
    return x.astype(MXU_DTYPE)


def _dot(a, b):
    return jnp.dot(a, b, preferred_element_type=f32)


def _dot_nt(a, b):
    return lax.dot_general(a, b, (((1,), (1,)), ((), ())), preferred_element_type=f32)


def _dot_tn(a, b):
    return lax.dot_general(a, b, (((0,), (0,)), ((), ())), preferred_element_type=f32)


def _sigmoid(x):
    return 1.0 / (1.0 + jnp.exp(-x))


def _rope_apply(y, c, s1, s2, width):
    return y * c + pltpu.roll(y, width - 8, axis=1) * s1 + pltpu.roll(y, 8, axis=1) * s2


def _rope_transpose(dy, c, s1, s2, width):
    return dy * c + pltpu.roll(dy * s1, 8, axis=1) + pltpu.roll(dy * s2, width - 8, axis=1)


def _norm_matmul(h, g, w, layer, tn, name, rope=None):
    s_len, d = h.shape
    n = w.shape[2]

    def body(*refs):
        if rope is None:
            h_ref, g_ref, w_ref, y_ref, hn_ref = refs
        else:
            h_ref, g_ref, w_ref, c_ref, s1_ref, s2_ref, y_ref, hn_ref = refs
            reps = tn // 128
            c = jnp.concatenate([c_ref[...]] * reps, axis=1)
            s1 = jnp.concatenate([s1_ref[...]] * reps, axis=1)
            s2 = jnp.concatenate([s2_ref[...]] * reps, axis=1)
        x = h_ref[...]
        r = lax.rsqrt(jnp.mean(x * x, axis=-1, keepdims=True) + EPS)
        hn = ((x * r) * g_ref[...]).astype(hn_ref.dtype)
        hn_ref[...] = hn
        for j in range(n // tn):
            y = _dot(hn, w_ref[:, j * tn:(j + 1) * tn])
            if rope is not None and POOL_WIDTH <= j * tn < POOL_WIDTH + 2 * ATTN_WIDTH:
                y = _rope_apply(y, c, s1, s2, tn)
            y_ref[:, j * tn:(j + 1) * tn] = y

    in_specs = [pl.BlockSpec((TM, d), lambda i: (i, 0)),
                pl.BlockSpec((1, d), lambda i: (0, 0)),
                _resident((None, d, n), lambda i: (layer, 0, 0))]
    args = [h, g, w]
    if rope is not None:
        assert POOL_WIDTH % tn == 0 and (2 * ATTN_WIDTH) % tn == 0
        in_specs += [pl.BlockSpec((TM, 128), lambda i: (i, 0))] * 3
        args += list(rope)
    return pl.pallas_call(
        body, name=name, grid=(s_len // TM,), in_specs=in_specs,
        out_specs=[pl.BlockSpec((TM, n), lambda i: (i, 0)), pl.BlockSpec((TM, d), lambda i: (i, 0))],
        out_shape=[jax.ShapeDtypeStruct((s_len, n), f32), jax.ShapeDtypeStruct((s_len, d), MXU_DTYPE)],
        compiler_params=_cparams(("parallel",)),
    )(*args)


def _gate_ple_fwd(h2, g, w_gate, w_ple, layer, p, p_layer, name, head=None):
    s_len, d = h2.shape

    def body(h_ref, g_ref, wg_ref, p_ref, wp_ref, *rest):
        gl_ref, hn_ref = rest[-2:]
        x = h_ref[...]
        r = lax.rsqrt(jnp.mean(x * x, axis=-1, keepdims=True) + EPS)
        hn = ((x * r) * g_ref[...]).astype(hn_ref.dtype)
        hn_ref[...] = hn
        gl = _dot(hn, wg_ref[...])
        gl_ref[...] = gl.astype(gl_ref.dtype)
        h3 = x + _sigmoid(gl) * _dot(_mx(p_ref[...]), wp_ref[...])
        if head is None:
            rest[0][...] = h3
            return
        gf_ref, t_ref, loss_ref, dh_ref, dgf_ref = rest[:5]
        i = pl.program_id(0)
        gv = gf_ref[...]
        r3 = lax.rsqrt(jnp.mean(h3 * h3, axis=-1, keepdims=True) + EPS)
        xh = h3 * r3
        diff = xh * gv - t_ref[...]
        part = 0.5 * jnp.sum(jnp.mean(diff * diff, axis=-1, keepdims=True), axis=0, keepdims=True)
        dy = diff * (1.0 / d)
        dxh = dy * gv
        dh_ref[...] = r3 * (dxh - xh * jnp.mean(dxh * xh, axis=-1, keepdims=True))
        dgsum = jnp.sum(dy * xh, axis=0, keepdims=True)
        lossb = jnp.broadcast_to(part, (8, 128))

        @pl.when(i == 0)
        def _():
            loss_ref[...] = lossb
            dgf_ref[...] = dgsum

        @pl.when(i > 0)
        def _():
            loss_ref[...] += lossb
            dgf_ref[...] += dgsum

    row = lambda i: (i, 0)
    one = lambda i: (0, 0)
    in_specs = [pl.BlockSpec((TM, d), row), pl.BlockSpec((1, d), one),
                pl.BlockSpec((None, d, d), lambda i: (layer, 0, 0)),
                pl.BlockSpec((None, TM, PLE_DIM), lambda i: (p_layer, i, 0)),
                pl.BlockSpec((None, PLE_DIM, d), lambda i: (layer, 0, 0))]
    args = [h2, g, w_gate, p, w_ple]
    saved = [jax.ShapeDtypeStruct((s_len, d), MXU_DTYPE)] * 2
    if head is None:
        out_specs = [pl.BlockSpec((TM, d), row)] * 3
        out_shape = [jax.ShapeDtypeStruct((s_len, d), f32)] + saved
    else:
        in_specs += [pl.BlockSpec((1, d), one), pl.BlockSpec((TM, d), row)]
        args += list(head)
        out_specs = [pl.BlockSpec((8, 128), one), pl.BlockSpec((TM, d), row), pl.BlockSpec((1, d), one)] \
            + [pl.BlockSpec((TM, d), row)] * 2
        out_shape = [jax.ShapeDtypeStruct((8, 128), f32), jax.ShapeDtypeStruct((s_len, d), f32),
                     jax.ShapeDtypeStruct((1, d), f32)] + saved
    return pl.pallas_call(
        body, name=name, grid=(s_len // TM,), in_specs=in_specs, out_specs=out_specs, out_shape=out_shape,
        compiler_params=_cparams(("arbitrary",)),
    )(*args)


def _gate_bwd(dh3, gl, p, p_layer, w_ple, w_gate, layer, h2, g, name):
    s_len, d = dh3.shape

    def body(dh_ref, gl_ref, p_ref, wp_ref, wg_ref, h_ref, g_ref, dh2_ref, dg_ref, de_ref, dgl_ref):
        i = pl.program_id(0)
        dh = dh_ref[...]
        gate = _sigmoid(gl_ref[...].astype(f32))
        e = _dot(_mx(p_ref[...]), wp_ref[...])
        de_ref[...] = (dh * gate).astype(de_ref.dtype)
        dgl = ((dh * e) * (gate * (1.0 - gate))).astype(dgl_ref.dtype)
        dgl_ref[...] = dgl
        dx, dgrow = _rmsnorm_bwd(_dot_nt(dgl, wg_ref[...]), h_ref[...], g_ref[...])
        dh2_ref[...] = dh + dx
        dgsum = jnp.sum(dgrow, axis=0, keepdims=True)

        @pl.when(i == 0)
        def _():
            dg_ref[...] = dgsum

        @pl.when(i > 0)
        def _():
            dg_ref[...] += dgsum

    row = lambda i: (i, 0)
    blk = pl.BlockSpec((TM, d), row)
    return pl.pallas_call(
        body, name=name, grid=(s_len // TM,),
        in_specs=[blk, blk, pl.BlockSpec((None, TM, PLE_DIM), lambda i: (p_layer, i, 0)),
                  _resident((None, PLE_DIM, d), lambda i: (layer, 0, 0)),
                  _resident((None, d, d), lambda i: (layer, 0, 0)), blk, pl.BlockSpec((1, d), lambda i: (0, 0))],
        out_specs=[blk, pl.BlockSpec((1, d), lambda i: (0, 0)), blk, blk],
        out_shape=[jax.ShapeDtypeStruct((s_len, d), f32), jax.ShapeDtypeStruct((1, d), f32),
                   jax.ShapeDtypeStruct((s_len, d), MXU_DTYPE), jax.ShapeDtypeStruct((s_len, d), MXU_DTYPE)],
        compiler_params=_cparams(("arbitrary",)),
    )(dh3, gl, p, w_ple, w_gate, h2, g)


def _rmsnorm_bwd(dhn, x, g):
    r = lax.rsqrt(jnp.mean(x * x, axis=-1, keepdims=True) + EPS)
    xh = x * r
    dxh = dhn * g
    dx = r * (dxh - xh * jnp.mean(dxh * xh, axis=-1, keepdims=True))
    return dx, dhn * xh


def _matmul_nt_norm_bwd(dy, w, layer, h_prev, g, dres, name, tk=1024, after=None):
    s_len, k_dim = dy.shape
    d = h_prev.shape[1]

    def body(dy_ref, w_ref, h_ref, g_ref, dres_ref, *rest):
        dh_ref, dg_ref = rest[-2:]
        i = pl.program_id(0)
        acc = None
        for k in range(k_dim // tk):
            part = _dot_nt(_mx(dy_ref[:, k * tk:(k + 1) * tk]), w_ref[:, k * tk:(k + 1) * tk])
            acc = part if acc is None else acc + part
        dx, dgrow = _rmsnorm_bwd(acc, h_ref[...], g_ref[...])
        dh_ref[...] = dres_ref[...] + dx
        dgsum = jnp.sum(dgrow, axis=0, keepdims=True)

        @pl.when(i == 0)
        def _():
            dg_ref[...] = dgsum

        @pl.when(i > 0)
        def _():
            dg_ref[...] += dgsum

    in_specs = [pl.BlockSpec((TM, k_dim), lambda i: (i, 0)),
                _resident((None, d, k_dim), lambda i: (layer, 0, 0)),
                pl.BlockSpec((TM, d), lambda i: (i, 0)),
                pl.BlockSpec((1, d), lambda i: (0, 0)),
                pl.BlockSpec((TM, d), lambda i: (i, 0))]
    args = [dy, w, h_prev, g, dres]
    if after is not None:
        in_specs.append(pl.BlockSpec(memory_space=pl.ANY))
        args.append(after)
    return pl.pallas_call(
        body, name=name, grid=(s_len // TM,), in_specs=in_specs,
        out_specs=[pl.BlockSpec((TM, d), lambda i: (i, 0)), pl.BlockSpec((1, d), lambda i: (0, 0))],
        out_shape=[jax.ShapeDtypeStruct((s_len, d), f32), jax.ShapeDtypeStruct((1, d), f32)],
        compiler_params=_cparams(("arbitrary",)),
    )(*args)


def _mlp_fwd(h1, g, w_up, w_down, layer, name, tf=1024):
    s_len, d = h1.shape
    ff = w_up.shape[2]

    def body(h_ref, g_ref, wu_ref, wd_ref, h2_ref, a_ref, hn_ref):
        x = h_ref[...]
        r = lax.rsqrt(jnp.mean(x * x, axis=-1, keepdims=True) + EPS)
        hn = ((x * r) * g_ref[...]).astype(hn_ref.dtype)
        hn_ref[...] = hn
        acc = x
        for j in range(ff // tf):
            a = _dot(hn, wu_ref[:, j * tf:(j + 1) * tf])
            a_ref[:, j * tf:(j + 1) * tf] = a.astype(a_ref.dtype)
            relu = jnp.maximum(a, 0.0)
            acc = acc + _dot(_mx(relu * relu), wd_ref[j * tf:(j + 1) * tf, :])
        h2_ref[...] = acc

    row = lambda i: (i, 0)
    return pl.pallas_call(
        body, name=name, grid=(s_len // TM,),
        in_specs=[pl.BlockSpec((TM, d), row), pl.BlockSpec((1, d), lambda i: (0, 0)),
                  _resident((None, d, ff), lambda i: (layer, 0, 0)), _resident((None, ff, d), lambda i: (layer, 0, 0))],
        out_specs=[pl.BlockSpec((TM, d), row), pl.BlockSpec((TM, ff), row), pl.BlockSpec((TM, d), row)],
        out_shape=[jax.ShapeDtypeStruct((s_len, d), f32), jax.ShapeDtypeStruct((s_len, ff), MXU_DTYPE),
                   jax.ShapeDtypeStruct((s_len, d), MXU_DTYPE)],
        compiler_params=_cparams(("parallel",)),
    )(h1, g, w_up, w_down)


def _mlp_bwd(dh2, w_down, w_up, layer, a, h1, g, name, tf=1024):
    s_len, d = dh2.shape
    ff = a.shape[1]

    def body(dh_ref, wd_ref, wu_ref, a_ref, h_ref, g_ref, dh1_ref, dg_ref, da_ref):
        i = pl.program_id(0)
        dh = dh_ref[...]
        dhb = _mx(dh)
        acc = None
        for j in range(ff // tf):
            cols = slice(j * tf, (j + 1) * tf)
            dact = _dot_nt(dhb, wd_ref[cols, :])
            da = (dact * (2.0 * jnp.maximum(a_ref[:, cols].astype(f32), 0.0))).astype(da_ref.dtype)
            da_ref[:, cols] = da
            part = _dot_nt(da, wu_ref[:, cols])
            acc = part if acc is None else acc + part
        dx, dgrow = _rmsnorm_bwd(acc, h_ref[...], g_ref[...])
        dh1_ref[...] = dh + dx
        dgsum = jnp.sum(dgrow, axis=0, keepdims=True)

        @pl.when(i == 0)
        def _():
            dg_ref[...] = dgsum

        @pl.when(i > 0)
        def _():
            dg_ref[...] += dgsum

    row = lambda i: (i, 0)
    return pl.pallas_call(
        body, name=name, grid=(s_len // TM,),
        in_specs=[pl.BlockSpec((TM, d), row), _resident((None, ff, d), lambda i: (layer, 0, 0)),
                  _resident((None, d, ff), lambda i: (layer, 0, 0)), pl.BlockSpec((TM, ff), row),
                  pl.BlockSpec((TM, d), row), pl.BlockSpec((1, d), lambda i: (0, 0))],
        out_specs=[pl.BlockSpec((TM, d), row), pl.BlockSpec((1, d), lambda i: (0, 0)), pl.BlockSpec((TM, ff), row)],
        out_shape=[jax.ShapeDtypeStruct((s_len, d), f32), jax.ShapeDtypeStruct((1, d), f32),
                   jax.ShapeDtypeStruct((s_len, ff), MXU_DTYPE)],
        compiler_params=_cparams(("arbitrary",), vmem=VMEM_LIMIT_LARGE),
    )(dh2, w_down, w_up, a, h1, g)


def _weight_grad(a, b, name, act=False, lead=None):
    s_len, k_dim = a.shape[-2:]
    n = b.shape[1]
    tka = min(k_dim, 2048)
    tnb = n if n <= 1024 else (2048 if n % 2048 == 0 else 640)
    tm = 2 * TM_WGRAD
    vmem = 2 * tm * (tka * a.dtype.itemsize + tnb * b.dtype.itemsize) + tka * tnb * (4 + 2 * jnp.dtype(COMM_DTYPE).itemsize)
    if vmem > VMEM_LIMIT - VMEM_COMPILER_RESERVE or s_len % tm:
        tm = TM_WGRAD
    ns = s_len // tm

    def body(a_ref, b_ref, o_ref, acc_ref):
        s = pl.program_id(2)
        x = a_ref[...]
        if act:
            relu = jnp.maximum(x, 0.0)
            x = relu * relu

        @pl.when(s == 0)
        def _():
            acc_ref[...] = jnp.zeros_like(acc_ref)

        acc_ref[...] += _dot_tn(_mx(x), _mx(b_ref[...]))

        @pl.when(s == ns - 1)
        def _():
            o_ref[...] = acc_ref[...].astype(o_ref.dtype)

    if lead is None:
        a_spec = pl.BlockSpec((tm, tka), lambda i, j, s: (s, i))
    else:
        a_spec = pl.BlockSpec((None, tm, tka), lambda i, j, s: (lead, s, i))
    return pl.pallas_call(
        body, name=name, grid=(k_dim // tka, n // tnb, ns),
        in_specs=[a_spec, pl.BlockSpec((tm, tnb), lambda i, j, s: (s, j))],
        out_specs=pl.BlockSpec((tka, tnb), lambda i, j, s: (i, j)),
        out_shape=jax.ShapeDtypeStruct((k_dim, n), COMM_DTYPE),
        scratch_shapes=[pltpu.VMEM((tka, tnb), f32)],
        compiler_params=_cparams(("parallel", "parallel", "arbitrary")),
    )(a, b)


def _group_select(lane, x2, x4, x8, x16):
    grp = lane // POOL_GC
    return jnp.where(grp == 0, x2, jnp.where(grp == 1, x4, jnp.where(grp == 2, x8, x16)))


def _pool_window(lane):
    grp = lane // POOL_GC
    return jnp.where(grp == 0, 2, jnp.where(grp == 1, 4, jnp.where(grp == 2, 8, 16)))


def _pool_y(u, halo, i):
    xs = jnp.concatenate([jnp.where(i > 0, halo, 0.0), u], axis=0)
    s2 = xs + pltpu.roll(xs, 1, axis=0)
    s4 = s2 + pltpu.roll(s2, 2, axis=0)
    s8 = s4 + pltpu.roll(s4, 4, axis=0)
    s16 = s8 + pltpu.roll(s8, 8, axis=0)
    lane = lax.broadcasted_iota(jnp.int32, xs.shape, 1)
    sel = _group_select(lane, s2, s4, s8, s16)[HALO:, :]
    t = i * TM + lax.broadcasted_iota(jnp.int32, u.shape, 0)
    cnt = jnp.minimum(_pool_window(lax.broadcasted_iota(jnp.int32, u.shape, 1)), t + 1).astype(f32)
    return sel / cnt - u


def _group_weights(l0, l1, l2):
    mx = jnp.maximum(jnp.maximum(l0, l1), l2)
    e0, e1, e2 = jnp.exp(l0 - mx), jnp.exp(l1 - mx), jnp.exp(l2 - mx)
    den = e0 + e1 + e2
    return e0 / den, e1 / den, e2 / den


def _mixer_out_proj(z, wbd, scale, outs, lses, w_out, layer, h, name):
    s_len, d = h.shape

    def body(u_ref, halo_ref, wbd_ref, sc_ref, o0, o1, o2, l0, l1, l2, wo_ref, h_ref, m_ref, h1_ref):
        i = pl.program_id(0)
        y = _pool_y(u_ref[...], halo_ref[...], i)
        pool = _dot(_mx(y), wbd_ref[...]) * sc_ref[...]
        w0, w1, w2 = _group_weights(l0[...], l1[...], l2[...])
        m = jnp.concatenate([pool, o0[...] * w0, o1[...] * w1, o2[...] * w2], axis=1).astype(m_ref.dtype)
        m_ref[...] = m
        h1_ref[...] = h_ref[...] + _dot(m, wo_ref[...])

    row = lambda i: (i, 0)
    blk = pl.BlockSpec((TM, 256), row)
    grp = [pl.BlockSpec((TM, 256), lambda i, g=g: (i, g)) for g in range(3)]
    return pl.pallas_call(
        body, name=name, grid=(s_len // TM,),
        in_specs=[blk, pl.BlockSpec((HALO, 256), lambda i: (jnp.maximum(i * (TM // HALO) - 1, 0), 0)),
                  pl.BlockSpec((256, 256), lambda i: (0, 0)), pl.BlockSpec((1, 256), lambda i: (0, 0))] + grp + grp
        + [_resident((None, d, d), lambda i: (layer, 0, 0)), pl.BlockSpec((TM, d), row)],
        out_specs=[pl.BlockSpec((TM, d), row)] * 2,
        out_shape=[jax.ShapeDtypeStruct((s_len, d), MXU_DTYPE), jax.ShapeDtypeStruct((s_len, d), f32)],
        compiler_params=_cparams(("parallel",)),
    )(z, z, wbd, scale, outs, outs, outs, lses, lses, lses, w_out, h)


def _head_sums(x):
    r = lax.broadcasted_iota(jnp.int32, (256, 256), 0) // HEAD_DIM
    c = lax.broadcasted_iota(jnp.int32, (256, 256), 1) // HEAD_DIM
    ones = jnp.where(r == c, 1.0, 0.0).astype(jnp.bfloat16)
    hi = x.astype(jnp.bfloat16)
    lo = (x - hi.astype(f32)).astype(jnp.bfloat16)
    return _dot(hi, ones) + _dot(lo, ones)


def _out_combine_bwd(dh1, w_out, layer, outs, lses, name, after=None):
    s_len, d = dh1.shape

    def body(dh_ref, w_ref, o0, o1, o2, l0, l1, l2, *rest):
        dp_ref, do_ref, dl_ref = rest[-3:]
        dm = _dot_nt(_mx(dh_ref[...]), w_ref[...])
        dp_ref[...] = dm[:, :POOL_WIDTH]
        w = _group_weights(l0[...], l1[...], l2[...])
        da = [dm[:, POOL_WIDTH + 256 * g:POOL_WIDTH + 256 * (g + 1)] for g in range(3)]
        o = (o0[...], o1[...], o2[...])
        dw = [_head_sums(da[g] * o[g]) for g in range(3)]
        t = w[0] * dw[0] + w[1] * dw[1] + w[2] * dw[2]
        do_ref[...] = jnp.concatenate([da[g] * w[g] for g in range(3)], axis=1)
        dl_ref[...] = jnp.concatenate([w[g] * t for g in range(3)], axis=1)

    grp = [pl.BlockSpec((TM, 256), lambda i, g=g: (i, g)) for g in range(3)]
    in_specs = [pl.BlockSpec((TM, d), lambda i: (i, 0)), _resident((None, d, d), lambda i: (layer, 0, 0))] + grp + grp
    args = [dh1, w_out, outs, outs, outs, lses, lses, lses]
    if after is not None:
        in_specs.append(pl.BlockSpec(memory_space=pl.ANY))
        args.append(after)
    return pl.pallas_call(
        body, name=name, grid=(s_len // TM,), in_specs=in_specs,
        out_specs=[pl.BlockSpec((TM, POOL_WIDTH), lambda i: (i, 0))] + [pl.BlockSpec((TM, ATTN_WIDTH), lambda i: (i, 0))] * 2,
        out_shape=[jax.ShapeDtypeStruct((s_len, POOL_WIDTH), f32)] + [jax.ShapeDtypeStruct((s_len, ATTN_WIDTH), f32)] * 2,
        compiler_params=_cparams(("parallel",)),
    )(*args)


def _pool_bwd(z, dm, wbd, scale, name, after=None):
    s_len = z.shape[0]
    n_halo = s_len // HALO

    def body(u_ref, uh_ref, d_ref, dh_ref, wbd_ref, sc_ref, *rest):
        du_ref, dw_ref, dsc_ref = rest[-3:]
        i = pl.program_id(0)
        last = pl.num_programs(0) - 1
        y = _pool_y(u_ref[...], uh_ref[...], i)
        yb = _mx(y)
        dpo = d_ref[...]
        sc = sc_ref[...]
        dsc = jnp.sum(dpo * _dot(yb, wbd_ref[...]), axis=0, keepdims=True)
        dwp = _dot_tn(yb, _mx(dpo * sc))

        @pl.when(i == 0)
        def _():
            dsc_ref[...] = dsc
            dw_ref[...] = dwp

        @pl.when(i > 0)
        def _():
            dsc_ref[...] += dsc
            dw_ref[...] += dwp

        ext = jnp.concatenate([dpo, jnp.where(i < last, dh_ref[...], 0.0)], axis=0)
        dy = _dot_nt(_mx(ext * sc), wbd_ref[...])
        t = i * TM + lax.broadcasted_iota(jnp.int32, ext.shape, 0)
        lane = lax.broadcasted_iota(jnp.int32, ext.shape, 1)
        e = dy / jnp.minimum(_pool_window(lane), t + 1).astype(f32)
        rows = ext.shape[0]
        f2 = e + pltpu.roll(e, rows - 1, axis=0)
        f4 = f2 + pltpu.roll(f2, rows - 2, axis=0)
        f8 = f4 + pltpu.roll(f4, rows - 4, axis=0)
        f16 = f8 + pltpu.roll(f8, rows - 8, axis=0)
        du_ref[...] = (_group_select(lane, f2, f4, f8, f16) - dy)[:TM, :].astype(du_ref.dtype)

    row = lambda i: (i, 0)
    blk = pl.BlockSpec((TM, 256), row)
    extra = [] if after is None else [after]
    return pl.pallas_call(
        body, name=name, grid=(s_len // TM,),
        in_specs=[blk, pl.BlockSpec((HALO, 256), lambda i: (jnp.maximum(i * (TM // HALO) - 1, 0), 0)),
                  blk, pl.BlockSpec((HALO, 256), lambda i: (jnp.minimum((i + 1) * (TM // HALO), n_halo - 1), 0)),
                  pl.BlockSpec((256, 256), lambda i: (0, 0)), pl.BlockSpec((1, 256), lambda i: (0, 0))]
        + [pl.BlockSpec(memory_space=pl.ANY)] * len(extra),
        out_specs=[blk, pl.BlockSpec((256, 256), lambda i: (0, 0)), pl.BlockSpec((1, 256), lambda i: (0, 0))],
        out_shape=[jax.ShapeDtypeStruct((s_len, N_IN), MXU_DTYPE), jax.ShapeDtypeStruct((256, 256), f32),
                   jax.ShapeDtypeStruct((1, 256), f32)],
        compiler_params=_cparams(("arbitrary",)),
    )(z, z, dm, dm, wbd, scale, *extra)


def _tri_masks():
    qi = lax.broadcasted_iota(jnp.int32, (BLK, BLK), 0)
    ki = lax.broadcasted_iota(jnp.int32, (BLK, BLK), 1)
    return qi >= ki, ki >= qi


ATTN_SUPER_PER_STEP = (8, 4, 1)
Q_COL, K_COL, V_COL = POOL_WIDTH // 128, (POOL_WIDTH + ATTN_WIDTH) // 128, (POOL_WIDTH + 2 * ATTN_WIDTH) // 128


def _rows(ref, start, dil):
    if dil == 1:
        return ref[pl.ds(start, BLK), :]
    return ref[pl.ds(start, BLK, stride=dil), :]


ATTN_BLOCKS_TOGETHER = 8


def _set_rows(ref, start, dil, val):
    if dil == 1:
        ref[pl.ds(start, BLK), :] = val
    else:
        ref[pl.ds(start, BLK, stride=dil), :] = val


def _attn_fwd(z, g, prev, name):
    s_len = z.shape[0]
    dil, m = DILATIONS[g], ATTN_SUPER_PER_STEP[g]
    sbr = BLK * dil
    rows = sbr * m

    def body(*refs):
        q_ref, kc_ref, kp_ref, vc_ref, vp_ref = refs[:5]
        o_ref, l_ref = refs[-2:]
        st = pl.program_id(0)
        low, up = _tri_masks()
        head0 = lax.broadcasted_iota(jnp.int32, (BLK, 128), 1) < HEAD_DIM
        blocks = [(sb, r) for sb in range(m) for r in range(dil)]
        for g0 in range(0, len(blocks), ATTN_BLOCKS_TOGETHER):
            grp = blocks[g0:g0 + ATTN_BLOCKS_TOGETHER]
            loaded = []
            for sb, r in grp:
                base = sb * sbr + r
                if sb == 0:
                    kp, vp = _rows(kp_ref, r, dil), _rows(vp_ref, r, dil)
                else:
                    kp, vp = _rows(kc_ref, base - sbr, dil), _rows(vc_ref, base - sbr, dil)
                qs = _rows(q_ref, base, dil) * ATTN_SCALE
                loaded.append((_mx(jnp.where(head0, qs, 0.0)), _mx(jnp.where(head0, 0.0, qs)),
                               jnp.concatenate([_mx(kp), _mx(_rows(kc_ref, base, dil))], axis=0),
                               jnp.concatenate([_mx(vp), _mx(_rows(vc_ref, base, dil))], axis=0)))
            scores = [(_dot_nt(q0, k2), _dot_nt(q1, k2)) for q0, q1, k2, _ in loaded]
            soft = []
            for (sb, _), pair in zip(grp, scores):
                valid = jnp.concatenate([up & (st > 0) if sb == 0 else up, low], axis=1)
                heads = []
                for s in pair:
                    s = jnp.where(valid, s, NEG_BIG)
                    mx = jnp.max(s, axis=-1, keepdims=True)
                    e = jnp.exp(s - mx)
                    l = jnp.sum(e, axis=-1, keepdims=True)
                    heads.append((_mx(e / l), jnp.broadcast_to(mx + jnp.log(l), (BLK, 128))))
                soft.append(heads)
            for (sb, r), heads, (_, _, _, v2) in zip(grp, soft, loaded):
                base = sb * sbr + r
                _set_rows(o_ref, base, dil, jnp.where(head0, _dot(heads[0][0], v2), _dot(heads[1][0], v2)))
                _set_rows(l_ref, base, dil, jnp.where(head0, heads[0][1], heads[1][1]))

    def cur(col):
        return pl.BlockSpec((rows, 128), lambda st, hp: (st, col + 2 * g + hp))

    def before(col):
        return pl.BlockSpec((sbr, 128), lambda st, hp: (jnp.maximum(st * m - 1, 0), col + 2 * g + hp))

    in_specs = [cur(Q_COL), cur(K_COL), before(K_COL), cur(V_COL), before(V_COL)]
    args = [z, z, z, z, z]
    aliases = {}
    if prev is not None:
        in_specs += [pl.BlockSpec(memory_space=pl.ANY)] * 2
        args += list(prev)
        aliases = {5: 0, 6: 1}
    return pl.pallas_call(
        body, name=name, grid=(s_len // rows, 2), in_specs=in_specs, out_specs=[cur(0), cur(0)],
        out_shape=[jax.ShapeDtypeStruct((s_len, ATTN_WIDTH), f32)] * 2, input_output_aliases=aliases,
        compiler_params=_cparams(("parallel", "parallel")),
    )(*args)


def _stack_heads(x, head0):
    return jnp.concatenate([_mx(jnp.where(head0, x, 0.0)), _mx(jnp.where(head0, 0.0, x))], axis=0)


def _head_rows(x):
    xt = x.T
    return jnp.concatenate([jnp.broadcast_to(xt[0:1, :], (BLK, BLK)),
                            jnp.broadcast_to(xt[HEAD_DIM:HEAD_DIM + 1, :], (BLK, BLK))], axis=0)


def _attn_bwd(z, do, lse, dlt, tabs, dz, g, name):
    s_len = z.shape[0]
    dil, m = DILATIONS[g], ATTN_SUPER_PER_STEP[g]
    sbr = BLK * dil
    rows = sbr * m
    nsteps = s_len // rows

    def body(q_ref, qn_ref, kc_ref, kp_ref, vc_ref, vp_ref, do_ref, don_ref, l_ref, ln_ref, d_ref, dn_ref,
             c_ref, s1_ref, s2_ref, dz_in, dz_ref, dq_buf, dk_buf, dv_buf, out_buf, sems):
        del dz_in
        st, hp = pl.program_id(0), pl.program_id(1)
        head0 = lax.broadcasted_iota(jnp.int32, (BLK, 128), 1) < HEAD_DIM
        key_i = lax.broadcasted_iota(jnp.int32, (2 * BLK, BLK), 0) & (BLK - 1)
        query_i = lax.broadcasted_iota(jnp.int32, (2 * BLK, BLK), 1)
        same_t, cross_t = query_i >= key_i, key_i >= query_i
        def load(r):
            keys, vals = [_stack_heads(_rows(kp_ref, r, dil), head0)], [_stack_heads(_rows(vp_ref, r, dil), head0)]
            qs, dos, lses, dlts = [], [], [], []
            for sb in range(m):
                base = sb * sbr + r
                keys.append(_stack_heads(_rows(kc_ref, base, dil), head0))
                vals.append(_stack_heads(_rows(vc_ref, base, dil), head0))
                qs.append(_mx(_rows(q_ref, base, dil)))
                dos.append(_mx(_rows(do_ref, base, dil)))
                lses.append(_head_rows(_rows(l_ref, base, dil)))
                dlts.append(_head_rows(_rows(d_ref, base, dil)))
            qs.append(_mx(_rows(qn_ref, r, dil)))
            dos.append(_mx(_rows(don_ref, r, dil)))
            lses.append(_head_rows(_rows(ln_ref, r, dil)))
            dlts.append(_head_rows(_rows(dn_ref, r, dil)))
            return keys, vals, qs, dos, lses, dlts

        def products(data):
            keys, vals, qs, dos, _, _ = data
            return ([(_dot_nt(keys[j + 1], qs[j]), _dot_nt(vals[j + 1], dos[j])) for j in range(m)],
                    [(_dot_nt(keys[j], qs[j]), _dot_nt(vals[j], dos[j])) for j in range(m + 1)])

        def finish(data, raw):
            lses, dlts = data[4], data[5]

            def one(pair, j, valid):
                p = jnp.where(valid, jnp.exp(pair[0] * ATTN_SCALE - lses[j]), 0.0)
                return _mx(p), _mx(p * (pair[1] - dlts[j]) * ATTN_SCALE)

            same = [one(raw[0][j], j, same_t) for j in range(m)]
            cross = [one(raw[1][j], j, cross_t & (st > 0) if j == 0 else
                         (cross_t & (st < nsteps - 1) if j == m else cross_t)) for j in range(m + 1)]
            return same, cross

        def gradients(r, data, fin):
            keys, _, qs, dos, _, _ = data
            same, cross = fin
            for sb in range(m):
                base = sb * sbr + r
                (p_a, ds_a), (_, ds_x), (p_n, ds_n) = same[sb], cross[sb], cross[sb + 1]
                dq = _dot_tn(ds_a, keys[sb + 1]) + _dot_tn(ds_x, keys[sb])
                dk2 = _dot(ds_a, qs[sb]) + _dot(ds_n, qs[sb + 1])
                dv2 = _dot(p_a, dos[sb]) + _dot(p_n, dos[sb + 1])
                _set_rows(dq_buf, base, dil, dq)
                _set_rows(dk_buf, base, dil, jnp.where(head0, dk2[:BLK], dk2[BLK:]))
                _set_rows(dv_buf, base, dil, jnp.where(head0, dv2[:BLK], dv2[BLK:]))

        def residue_group(rg, carry):
            rs = [rg * group + i for i in range(group)]
            data = [load(r) for r in rs]
            raws = [products(d) for d in data]
            fins = [finish(d, raw) for d, raw in zip(data, raws)]
            for r, d, fin in zip(rs, data, fins):
                gradients(r, d, fin)
            return carry

        group = max(1, min(dil, ATTN_BLOCKS_TOGETHER // m))
        if dil // group <= 2:
            for rg in range(dil // group):
                residue_group(rg, 0)
        else:
            lax.fori_loop(0, dil // group, residue_group, 0)
        copies = []
        for t, (buf, col) in enumerate(((dq_buf, Q_COL), (dk_buf, K_COL), (dv_buf, V_COL))):
            val = buf[...]
            if t < 2:
                val = _rope_transpose(val, c_ref[...], s1_ref[...], s2_ref[...], 128)
            out_buf[t] = val.astype(out_buf.dtype)
            lane0 = pl.multiple_of((col + 2 * g + hp) * 128, 128)
            dst = dz_ref.at[pl.ds(pl.multiple_of(st * rows, rows), rows), pl.ds(lane0, 128)]
            cp = pltpu.make_async_copy(out_buf.at[t], dst, sems.at[t])
            cp.start()
            copies.append(cp)
        for cp in copies:
            cp.wait()

    def cur(col):
        return pl.BlockSpec((rows, 128), lambda st, hp: (st, col + 2 * g + hp))

    def before(col):
        return pl.BlockSpec((sbr, 128), lambda st, hp: (jnp.maximum(st * m - 1, 0), col + 2 * g + hp))

    def after(col):
        return pl.BlockSpec((sbr, 128), lambda st, hp: (jnp.minimum((st + 1) * m, s_len // sbr - 1), col + 2 * g + hp))

    tab = pl.BlockSpec((rows, 128), lambda st, hp: (st, 0))
    return pl.pallas_call(
        body, name=name, grid=(nsteps, 2),
        in_specs=[cur(Q_COL), after(Q_COL), cur(K_COL), before(K_COL), cur(V_COL), before(V_COL),
                  cur(0), after(0), cur(0), after(0), cur(0), after(0), tab, tab, tab,
                  pl.BlockSpec(memory_space=pl.ANY)],
        out_specs=pl.BlockSpec(memory_space=pl.ANY),
        out_shape=jax.ShapeDtypeStruct(dz.shape, dz.dtype), input_output_aliases={15: 0},
        scratch_shapes=[pltpu.VMEM((rows, 128), f32)] * 3 + [pltpu.VMEM((3, rows, 128), dz.dtype),
                                                            pltpu.SemaphoreType.DMA((3,))],
        compiler_params=_cparams(("arbitrary", "arbitrary")),
    )(z, z, z, z, z, z, do, do, lse, lse, dlt, dlt, *tabs, dz)


def _rope_tables(positions):
    inv_freq = ROPE_THETA ** (-jnp.arange(0, ROT_DIM, 2, dtype=f32) / ROT_DIM)
    ang = positions.astype(f32)[:, None] * inv_freq
    cos, sin = jnp.cos(ang), jnp.sin(ang)
    s_len = positions.shape[0]
    zero8, rest = jnp.zeros((s_len, 8), f32), jnp.zeros((s_len, HEAD_DIM - ROT_DIM), f32)
    c = jnp.concatenate([cos, cos, jnp.ones((s_len, HEAD_DIM - ROT_DIM), f32)], axis=1)
    s1 = jnp.concatenate([-sin, zero8, rest], axis=1)
    s2 = jnp.concatenate([zero8, sin, rest], axis=1)
    return tuple(jnp.tile(t, (1, 2)) for t in (c, s1, s2))


def _block_diag(pool_w):
    out = jnp.zeros((POOL_WIDTH, POOL_WIDTH), pool_w.dtype)
    for g in range(4):
        out = lax.dynamic_update_slice(out, pool_w[g], (g * POOL_GC, g * POOL_GC))
    return out


def _layer_fwd(h, p, wsrc, small, layer, tabs, head=None):
    nm = f"l{layer}_"
    wts, wl = wsrc.take(layer, ("w_in",), (h,) if layer else tuple(tabs))
    z, hn1 = _norm_matmul(h, small["norm1"][layer][None], wts["w_in"], wl, 256, nm + "in_proj", rope=tabs)
    ol = None
    for g in range(3):
        ol = _attn_fwd(z, g, ol, nm + f"attn_fwd{g}")
    outs, lses = ol
    wbd = _mx(_block_diag(small["pool_w"][layer]))
    scale = small["pool_scale"][layer][None]
    wts.update(wsrc.take(layer, ("w_out",), (outs,))[0])
    m, h1 = _mixer_out_proj(z, wbd, scale, outs, lses, wts["w_out"], wl, h, nm + "mixer_out")
    wts.update(wsrc.take(layer, ("w_up", "w_down"), (h1,))[0])
    h2, a, hn2 = _mlp_fwd(h1, small["norm2"][layer][None], wts["w_up"], wts["w_down"], wl, nm + "mlp")
    wts.update(wsrc.take(layer, ("w_gate", "w_ple"), (h2,))[0])
    *h3, gl, hn3 = _gate_ple_fwd(h2, small["norm3"][layer][None], wts["w_gate"], wts["w_ple"], wl, p, layer,
                                 nm + "gate_ple", head=head)
    saved = dict(h=h, z=z, hn1=hn1, outs=outs, lses=lses, wbd=wbd, scale=scale, m=m, h1=h1, a=a, hn2=hn2, h2=h2,
                 gl=gl, hn3=hn3, wts=wts, wl=wl)
    return h3, saved


def _layer_bwd(dh3, sv, p, small, layer, tabs128, reducer):
    nm = f"l{layer}_"
    wts, wl = sv["wts"], sv["wl"]
    dh2, dg3, de, dgl = _gate_bwd(dh3, sv["gl"], p, layer, wts["w_ple"], wts["w_gate"], wl, sv["h2"],
                                  small["norm3"][layer][None], nm + "gate_bwd")
    reducer.add("w_gate", layer, _weight_grad(sv["hn3"], dgl, nm + "dw_gate"))
    reducer.add("w_ple", layer, _weight_grad(p, de, nm + "dw_ple", lead=layer))
    dh1, dg2, da = _mlp_bwd(dh2, wts["w_down"], wts["w_up"], wl, sv["a"], sv["h1"], small["norm2"][layer][None],
                            nm + "mlp_bwd")
    reducer.add("w_down", layer, _weight_grad(sv["a"], dh2, nm + "dw_down", act=True))
    started = reducer.add("w_up", layer, _weight_grad(sv["hn2"], da, nm + "dw_up"))
    dpool, do, dlt = _out_combine_bwd(dh1, wts["w_out"], wl, sv["outs"], sv["lses"], nm + "out_bwd", after=started)
    started = reducer.add("w_out", layer, _weight_grad(sv["m"], dh1, nm + "dw_out"))
    dz, dwbd, dscale = _pool_bwd(sv["z"], dpool, sv["wbd"], sv["scale"], nm + "pool_bwd", after=started)
    for g in range(3):
        dz = _attn_bwd(sv["z"], do, sv["lses"], dlt, tabs128, dz, g, nm + f"attn_bwd{g}")
    started = reducer.add("w_in", layer, _weight_grad(sv["hn1"], dz, nm + "dw_in"))
    dh0, dg1 = _matmul_nt_norm_bwd(dz, wts["w_in"], wl, sv["h"], small["norm1"][layer][None], dh1, nm + "in_bwd",
                                   tk=512, after=started)
    dpool_w = jnp.stack([dwbd[g * POOL_GC:(g + 1) * POOL_GC, g * POOL_GC:(g + 1) * POOL_GC] for g in range(4)])
    sg = dict(norm1=dg1[0], norm2=dg2[0], norm3=dg3[0], pool_w=dpool_w, pool_scale=dscale[0])
    return dh0, sg


def _local_step(x, p, positions, wsrc, small, target, reducer):
    tabs128 = _rope_tables(positions)
    (h,), sv0 = _layer_fwd(x, p, wsrc, small, 0, tabs128)
    (loss, dh, dgf), sv1 = _layer_fwd(h, p, wsrc, small, 1, tabs128, head=(small["final_norm"][None], target))
    saved = [sv0, sv1]
    sgs = [None, None]
    for layer in (1, 0):
        dh, sgs[layer] = _layer_bwd(dh, saved[layer], p, small, layer, tabs128, reducer)
    small_grads = {k: jnp.stack([sgs[0][k], sgs[1][k]]) for k in sgs[0]}
    small_grads["final_norm"] = dgf[0]
    return loss, dh, small_grads


HBM = pl.BlockSpec(memory_space=pltpu.HBM)


def _my_place():
    return lax.axis_index("x"), lax.axis_index("y"), lax.axis_index("c")


def _other_chips(x, y):
    return [(1 - x, y), (x, 1 - y), (1 - x, 1 - y)]


def _window(ref, name, chip):
    k, n = _shard_shape(name)
    if COL_SHARDED[name]:
        return ref.at[:, pl.ds(pl.multiple_of(chip * n, 128), n)]
    return ref.at[pl.ds(pl.multiple_of(chip * k, 128), k), :]


def _chip_index():
    return jnp.reshape(2 * lax.axis_index("x") + lax.axis_index("y"), (1,)).astype(jnp.int32)


def _shard_block(name, tr):
    ks, ns = _shard_shape(name)
    if COL_SHARDED[name]:
        return (tr, ns), lambda i, me: (i, me[0])
    return (tr, ns), lambda i, me: (me[0] * (ks // tr) + i, 0)


def _place_shard(w, name, layer):
    ks, ns = _shard_shape(name)
    tr = min(ks, 256)
    shape, index = _shard_block(name, tr)

    def body(me_ref, w_ref, o_ref):
        o_ref[...] = w_ref[...].astype(o_ref.dtype)

    return pl.pallas_call(
        body, name=f"place_{name}{layer}",
        grid_spec=pltpu.PrefetchScalarGridSpec(
            num_scalar_prefetch=1, grid=(ks // tr,),
            in_specs=[pl.BlockSpec((None, tr, ns), lambda i, me: (layer, i, 0))],
            out_specs=pl.BlockSpec((None,) + shape, lambda i, me: (0,) + index(i, me))),
        out_shape=jax.ShapeDtypeStruct((1,) + FULL_SHAPE[name], MXU_DTYPE),
        compiler_params=_cparams(("parallel",)),
    )(_chip_index(), w)


GATHER_ORDER = [("w_in", 0), ("w_out", 0), ("w_up", 0), ("w_down", 0), ("w_gate", 0), ("w_ple", 0),
                ("w_in", 1), ("w_out", 1), ("w_up", 1), ("w_down", 1), ("w_gate", 1), ("w_ple", 1)]
SEM = pl.BlockSpec(memory_space=pltpu.SEMAPHORE)
EFFECT = pltpu.SideEffectType.DATAFLOW_SIDE_EFFECTING


def _gather_copy(src_ref, dst_ref, name, idx, j, chip, send_sems, recv_sems, c):
    cx, cy = chip
    return pltpu.make_async_remote_copy(
        src_ref=src_ref, dst_ref=dst_ref, send_sem=send_sems.at[3 * idx + j], recv_sem=recv_sems.at[3 * idx + j],
        device_id=(cx, cy, c), device_id_type=MESH)


def _gather_start(placed, order, tag, after=None):
    n = len(order)
    extra = [] if after is None else [after]

    def body(*refs):
        ins = refs[:n]
        k = n + len(extra)
        send_sems, recv_sems = refs[k], refs[k + 1]
        outs = refs[k + 2:k + 2 + n]
        token = refs[-1]
        x, y, c = _my_place()
        me = 2 * x + y
        for idx, (name, _) in enumerate(order):
            for j, chip in enumerate(_other_chips(x, y)):
                _gather_copy(_window(ins[idx].at[0], name, me), _window(outs[idx].at[0], name, me), name, idx, j, chip,
                             send_sems, recv_sems, c).start()
        token[...] = jnp.zeros_like(token)

    res = pl.pallas_call(
        body, name="gather_start" + tag,
        out_shape=(pltpu.SemaphoreType.DMA((3 * n,)), pltpu.SemaphoreType.DMA((3 * n,)))
        + tuple(pltpu.HBM(a.shape, a.dtype) for a in placed) + (jax.ShapeDtypeStruct((8, 128), f32),),
        in_specs=[HBM] * n + [pl.BlockSpec(memory_space=pl.ANY)] * len(extra),
        out_specs=(SEM, SEM) + (HBM,) * n + (pl.BlockSpec(memory_space=pltpu.VMEM),),
        input_output_aliases={i: i + 2 for i in range(n)},
        compiler_params=pltpu.CompilerParams(has_side_effects=EFFECT),
    )(*[pltpu.with_memory_space_constraint(a, pltpu.HBM) for a in placed], *extra)
    return res[0], res[1], list(res[2:2 + n]), res[-1]


def _gather_wait(send_sems, recv_sems, arrays, order, idxs, after, name):
    n = len(idxs)

    def body(*refs):
        ins = refs[:n]
        send_ref, recv_ref = refs[n], refs[n + 1]
        x, y, c = _my_place()
        me = 2 * x + y
        for k, idx in enumerate(idxs):
            wname = order[idx][0]
            for j, chip in enumerate(_other_chips(x, y)):
                cx, cy = chip
                mine = _window(ins[k].at[0], wname, me)
                land = _window(ins[k].at[0], wname, 2 * cx + cy)
                _gather_copy(mine, mine, wname, idx, j, chip, send_ref, recv_ref, c).wait_send()
                _gather_copy(land, land, wname, idx, j, chip, send_ref, recv_ref, c).wait_recv()

    operands = list(arrays) + [send_sems, recv_sems] + list(after)
    in_specs = [HBM] * n + [SEM, SEM] + [pl.BlockSpec(memory_space=pl.ANY)] * len(after)
    res = pl.pallas_call(
        body, name=name, out_shape=tuple(pltpu.HBM(a.shape, a.dtype) for a in arrays),
        in_specs=in_specs, out_specs=(HBM,) * n, input_output_aliases={i: i for i in range(n)},
        compiler_params=pltpu.CompilerParams(has_side_effects=EFFECT),
    )(*operands)
    return list(res)


class _GatheredWeights:
    def __init__(self, shards):
        self.starts = []
        token = None
        for tag, order in (("_first", GATHER_ORDER[:1]), ("_rest", GATHER_ORDER[1:])):
            placed = [_place_shard(shards[name], name, layer) for name, layer in order]
            self.starts.append((order,) + _gather_start(placed, order, tag, token))
            token = self.starts[-1][-1]

    def take(self, layer, names, after):
        order, send, recv, arrays, _ = next(s for s in self.starts if (names[0], layer) in s[0])
        after = list(after)
        if order is self.starts[0][0]:
            after.append(self.starts[-1][-1])
        idxs = [order.index((n, layer)) for n in names]
        got = _gather_wait(send, recv, [arrays[i] for i in idxs], order, idxs, after, f"gather_wait{layer}_{names[0]}")
        return dict(zip(names, got)), 0


N_DEV = 8


def _reduce_copies(dws, lands, names, layer, send_sems, recv_sems):
    x, y, c = _my_place()
    me, my_dev = 2 * x + y, 4 * x + 2 * y + c
    out = []
    for t, name in enumerate(names):
        for j, (cx, cy) in enumerate(_other_chips(x, y)):
            out.append((pltpu.make_async_remote_copy(
                src_ref=_window(dws[t], name, 2 * cx + cy), dst_ref=lands[t].at[my_dev],
                send_sem=send_sems.at[4 * t + j], recv_sem=recv_sems.at[N_DEV * t + my_dev],
                device_id=(cx, cy, layer), device_id_type=MESH), False))
        out.append((pltpu.make_async_remote_copy(
            src_ref=_window(dws[t], name, me), dst_ref=lands[t].at[my_dev],
            send_sem=send_sems.at[4 * t + 3], recv_sem=recv_sems.at[N_DEV * t + my_dev],
            device_id=(x, y, layer), device_id_type=MESH), True))
    return out


def _reduce_start(dws, names, layer, tag):
    n = len(names)
    lands = [lax.empty((N_DEV,) + _shard_shape(nm), dws[0].dtype) for nm in names]

    def body(*refs):
        ins = refs[:n]
        send_sems, recv_sems = refs[2 * n], refs[2 * n + 1]
        land_out = refs[3 * n + 2:4 * n + 2]
        token = refs[-1]
        c = lax.axis_index("c")
        for cp, non_owner_only in _reduce_copies(ins, land_out, names, layer, send_sems, recv_sems):
            if non_owner_only:
                @pl.when(c != layer)
                def _():
                    cp.start()
            else:
                cp.start()
        token[...] = jnp.zeros_like(token)

    res = pl.pallas_call(
        body, name="reduce_start" + tag,
        out_shape=(pltpu.SemaphoreType.DMA((4 * n,)), pltpu.SemaphoreType.DMA((N_DEV * n,)))
        + tuple(pltpu.HBM(a.shape, a.dtype) for a in dws) + tuple(pltpu.HBM(a.shape, a.dtype) for a in lands)
        + (jax.ShapeDtypeStruct((8, 128), f32),),
        in_specs=[HBM] * (2 * n),
        out_specs=(SEM, SEM) + (HBM,) * (2 * n) + (pl.BlockSpec(memory_space=pltpu.VMEM),),
        input_output_aliases={i: i + 2 for i in range(2 * n)},
        compiler_params=pltpu.CompilerParams(has_side_effects=EFFECT),
    )(*[pltpu.with_memory_space_constraint(a, pltpu.HBM) for a in list(dws) + lands])
    return res[0], res[1], list(res[2:2 + n]), list(res[2 + n:2 + 2 * n]), res[-1]


def _reduce_wait(send_sems, recv_sems, dws, lands, names, layer, after, tag):
    n = len(names)

    def body(*refs):
        ins, land_in = refs[:n], refs[n:2 * n]
        send_ref, recv_ref = refs[2 * n], refs[2 * n + 1]
        x, y, c = _my_place()
        for cp, non_owner_only in _reduce_copies(ins, land_in, names, layer, send_ref, recv_ref):
            if non_owner_only:
                @pl.when(c != layer)
                def _():
                    cp.wait_send()
            else:
                cp.wait_send()

        @pl.when(c == layer)
        def _():
            for t in range(n):
                for k in range(1, N_DEV):
                    px, py, pc = x ^ ((k >> 2) & 1), y ^ ((k >> 1) & 1), c ^ (k & 1)
                    dev = 4 * px + 2 * py + pc
                    land = land_in[t].at[dev]
                    pltpu.make_async_remote_copy(
                        src_ref=land, dst_ref=land, send_sem=send_ref.at[4 * t], recv_sem=recv_ref.at[N_DEV * t + dev],
                        device_id=(px, py, pc), device_id_type=MESH).wait_recv()

    res = pl.pallas_call(
        body, name="reduce_wait" + tag,
        out_shape=tuple(pltpu.HBM(a.shape, a.dtype) for a in list(dws) + list(lands)),
        in_specs=[HBM] * (2 * n) + [SEM, SEM, pl.BlockSpec(memory_space=pl.ANY)], out_specs=(HBM,) * (2 * n),
        input_output_aliases={i: i for i in range(2 * n)},
        compiler_params=pltpu.CompilerParams(has_side_effects=EFFECT),
    )(*dws, *lands, send_sems, recv_sems, after)
    return list(res[:n]), list(res[n:])


def _sum_devices(land, own, name, layer, prev):
    ks, ns = _shard_shape(name)
    tr = min(ks, 256)
    shape, index = _shard_block(name, tr)

    def body(me_ref, dev_ref, *refs):
        s_ref, own_ref, out_ref = refs[0], refs[1], refs[-1]
        dev = dev_ref[0]
        acc = None
        for s in range(N_DEV):
            term = jnp.where(dev == s, own_ref[...], s_ref[s]).astype(f32)
            acc = term if acc is None else acc + term
        out_ref[...] = acc

    def mine(i, dev):
        return i * jnp.where((dev[0] & 1) == layer, 1, 0)

    in_specs = [pl.BlockSpec((N_DEV, tr, ns), lambda i, me, dev: (0, mine(i, dev), 0)),
                pl.BlockSpec(shape, lambda i, me, dev: index(mine(i, dev), me))]
    args = [land, own]
    aliases = {}
    if prev is not None:
        in_specs.append(pl.BlockSpec(memory_space=pl.ANY))
        args.append(prev)
        aliases = {4: 0}
    x, y, c = _my_place()
    return pl.pallas_call(
        body, name=f"sum_devices_{name}{layer}",
        grid_spec=pltpu.PrefetchScalarGridSpec(
            num_scalar_prefetch=2, grid=(ks // tr,), in_specs=in_specs,
            out_specs=pl.BlockSpec((None, tr, ns), lambda i, me, dev: (layer, mine(i, dev), 0))),
        out_shape=jax.ShapeDtypeStruct((2, ks, ns), f32), input_output_aliases=aliases,
        compiler_params=_cparams(("arbitrary",)),
    )(_chip_index(), jnp.reshape(4 * x + 2 * y + c, (1,)).astype(jnp.int32), *args)


class _GradReducer:
    GROUPS = (("1", 1, ("w_gate", "w_ple", "w_down", "w_up", "w_out", "w_in")),
              ("0a", 0, ("w_gate", "w_ple", "w_down", "w_up")),
              ("0b", 0, ("w_out",)),
              ("0c", 0, ("w_in",)))

    def __init__(self):
        self.grads = {}
        self.started = {}

    def add(self, name, layer, dw):
        self.grads[(name, layer)] = dw
        token = None
        for tag, glayer, names in self.GROUPS:
            if tag not in self.started and all((nm, glayer) in self.grads for nm in names):
                *self.started[tag], token = _reduce_start([self.grads[(nm, glayer)] for nm in names], names, glayer, tag)
        return token

    def finish(self, after):
        mine = {}
        for tag, layer, names in self.GROUPS:
            send, recv, dws, lands = self.started[tag]
            dws, lands = _reduce_wait(send, recv, dws, lands, names, layer, after, tag)
            for nm, dw, land in zip(names, dws, lands):
                mine[nm] = _sum_devices(land, dw, nm, layer, mine.get(nm))
        return _pair_layers(mine)


def _pair_layers(mine):
    names = list(BIG)

    def body(*refs):
        ins = refs[:len(names)]
        outs = refs[len(names):2 * len(names)]
        send_sems, recv_sems = refs[2 * len(names):]
        x, y, c = _my_place()
        sibling = (x, y, 1 - c)
        cps = []
        for t in range(len(names)):
            cp = pltpu.make_async_remote_copy(
                src_ref=ins[t].at[c], dst_ref=outs[t].at[c], send_sem=send_sems.at[t], recv_sem=recv_sems.at[t],
                device_id=sibling, device_id_type=MESH)
            cp.start()
            cps.append(cp)
        for t in range(len(names)):
            cps[t].wait_send()
            land = outs[t].at[1 - c]
            pltpu.make_async_remote_copy(
                src_ref=land, dst_ref=land, send_sem=send_sems.at[t], recv_sem=recv_sems.at[t],
                device_id=sibling, device_id_type=MESH).wait_recv()

    outs = pl.pallas_call(
        body, name="pair_layers", in_specs=[HBM] * len(names), out_specs=[HBM] * len(names),
        out_shape=[jax.ShapeDtypeStruct((2,) + _shard_shape(n), f32) for n in names],
        input_output_aliases={t: t for t in range(len(names))},
        scratch_shapes=[pltpu.SemaphoreType.DMA((len(names),)), pltpu.SemaphoreType.DMA((len(names),))],
    )(*[mine[n] for n in names])
    return dict(zip(names, outs))


SMALL_ROWS = 320


def _small_copies(vec_ref, land_ref, send_sems, recv_sems):
    x, y, c = _my_place()
    me = 4 * x + 2 * y + c
    out = []
    for k in range(1, N_DEV):
        peer = (x ^ ((k >> 2) & 1), y ^ ((k >> 1) & 1), c ^ (k & 1))
        src_dev = 4 * peer[0] + 2 * peer[1] + peer[2]
        send = pltpu.make_async_remote_copy(
            src_ref=vec_ref, dst_ref=land_ref.at[me], send_sem=send_sems.at[k - 1], recv_sem=recv_sems.at[k - 1],
            device_id=peer, device_id_type=MESH)
        arrival = pltpu.make_async_remote_copy(
            src_ref=land_ref.at[src_dev], dst_ref=land_ref.at[src_dev], send_sem=send_sems.at[k - 1],
            recv_sem=recv_sems.at[k - 1], device_id=peer, device_id_type=MESH)
        out.append((send, arrival))
    return out


def _small_start(vec):
    land = lax.empty((N_DEV,) + vec.shape, vec.dtype)

    def body(v_ref, land_in, send_sems, recv_sems, v_out, land_out):
        del land_in, v_out
        for send, _ in _small_copies(v_ref, land_out, send_sems, recv_sems):
            send.start()

    return pl.pallas_call(
        body, name="small_start",
        out_shape=(pltpu.SemaphoreType.DMA((N_DEV - 1,)), pltpu.SemaphoreType.DMA((N_DEV - 1,)),
                   pltpu.HBM(vec.shape, vec.dtype), pltpu.HBM(land.shape, land.dtype)),
        in_specs=[HBM, HBM], out_specs=(SEM, SEM, HBM, HBM), input_output_aliases={0: 2, 1: 3},
        compiler_params=pltpu.CompilerParams(has_side_effects=EFFECT),
    )(pltpu.with_memory_space_constraint(vec, pltpu.HBM), pltpu.with_memory_space_constraint(land, pltpu.HBM))


def _small_wait(send_sems, recv_sems, vec, land, after):
    def body(v_ref, land_ref, send_ref, recv_ref, after_ref, v_out, land_out):
        del after_ref, v_out, land_out
        for send, arrival in _small_copies(v_ref, land_ref, send_ref, recv_ref):
            send.wait_send()
            arrival.wait_recv()

    return pl.pallas_call(
        body, name="small_wait", out_shape=(pltpu.HBM(vec.shape, vec.dtype), pltpu.HBM(land.shape, land.dtype)),
        in_specs=[HBM, HBM, SEM, SEM, pl.BlockSpec(memory_space=pl.ANY)], out_specs=(HBM, HBM),
        input_output_aliases={0: 0, 1: 1}, compiler_params=pltpu.CompilerParams(has_side_effects=EFFECT),
    )(vec, land, send_sems, recv_sems, after)


def _small_sum(vec, land):
    x, y, c = _my_place()

    def body(dev_ref, v_ref, land_ref, out_ref):
        acc = None
        for s in range(N_DEV):
            term = jnp.where(dev_ref[0] == s, v_ref[...], land_ref[s])
            acc = term if acc is None else acc + term
        out_ref[...] = acc

    return pl.pallas_call(
        body, name="small_sum",
        grid_spec=pltpu.PrefetchScalarGridSpec(
            num_scalar_prefetch=1, grid=(1,),
            in_specs=[pl.BlockSpec(vec.shape, lambda i, dev: (0, 0)), pl.BlockSpec(land.shape, lambda i, dev: (0, 0, 0))],
            out_specs=pl.BlockSpec(vec.shape, lambda i, dev: (0, 0))),
        out_shape=jax.ShapeDtypeStruct(vec.shape, vec.dtype),
        compiler_params=_cparams(("arbitrary",)),
    )(jnp.reshape(4 * x + 2 * y + c, (1,)).astype(jnp.int32), vec, land)


def _adamw(w, g, m, v, name):
    rows, cols = w.shape
    tr = rows
    for cand in (512, 256, 128, 64, 32, 16, 8):
        if rows % cand == 0 and cand * cols * 4 <= 2 * 1024 * 1024:
            tr = cand
            break
    c1 = np.float32(1.0 - ADAM_B1 ** ADAM_STEP)
    c2 = np.float32(1.0 - ADAM_B2 ** ADAM_STEP)

    def body(w_ref, g_ref, m_ref, v_ref, go_ref, d_ref, mo_ref, vo_ref):
        gv = g_ref[...]
        go_ref[...] = gv
        mn = ADAM_B1 * m_ref[...] + (1.0 - ADAM_B1) * gv
        vn = ADAM_B2 * v_ref[...] + (1.0 - ADAM_B2) * (gv * gv)
        mo_ref[...] = mn
        vo_ref[...] = vn
        d_ref[...] = -ADAM_LR * ((mn / c1) / (jnp.sqrt(vn / c2) + ADAM_EPS) + ADAM_WD * w_ref[...])

    blk = pl.BlockSpec((tr, cols), lambda i: (i, 0))
    return pl.pallas_call(
        body, name="adamw_" + name, grid=(rows // tr,), in_specs=[blk] * 4, out_specs=[blk] * 4,
        out_shape=[jax.ShapeDtypeStruct((rows, cols), f32)] * 4,
        compiler_params=_cparams(("parallel",)),
    )(w, g, m, v)


SMALL = ("norm1", "pool_w", "pool_scale", "norm2", "norm3", "final_norm")
ORDER = ("norm1", "w_in", "pool_w", "pool_scale", "w_out", "norm2", "w_up", "w_down", "norm3", "w_gate", "w_ple",
         "final_norm")


def _pack_small(tree, extra=None):
    parts = [tree[n].reshape(-1) for n in SMALL]
    if extra is not None:
        parts.append(extra.reshape(-1))
    flat = jnp.concatenate(parts)
    return jnp.pad(flat, (0, SMALL_ROWS * 128 - flat.shape[0])).reshape(SMALL_ROWS, 128)


def _unpack_small(packed, like):
    flat = packed.reshape(-1)
    out, off = {}, 0
    for n in SMALL:
        size = int(np.prod(like[n].shape))
        out[n] = flat[off:off + size].reshape(like[n].shape)
        off += size
    return out, flat[off]


def kernel(x, p, positions, norm1, w_in, pool_w, pool_scale, w_out, norm2, w_up, w_down, norm3, w_gate, w_ple, final_norm, loss_target, m_norm1, m_w_in, m_pool_w, m_pool_scale, m_w_out, m_norm2, m_w_up, m_w_down, m_norm3, m_w_gate, m_w_ple, m_final_norm, v_norm1, v_w_in, v_pool_w, v_pool_scale, v_w_out, v_norm2, v_w_up, v_w_down, v_norm3, v_w_gate, v_w_ple, v_final_norm):
    w = dict(norm1=norm1, w_in=w_in, pool_w=pool_w, pool_scale=pool_scale, w_out=w_out, norm2=norm2, w_up=w_up,
             w_down=w_down, norm3=norm3, w_gate=w_gate, w_ple=w_ple, final_norm=final_norm)
    m = dict(norm1=m_norm1, w_in=m_w_in, pool_w=m_pool_w, pool_scale=m_pool_scale, w_out=m_w_out, norm2=m_norm2,
             w_up=m_w_up, w_down=m_w_down, norm3=m_norm3, w_gate=m_w_gate, w_ple=m_w_ple, final_norm=m_final_norm)
    v = dict(norm1=v_norm1, w_in=v_w_in, pool_w=v_pool_w, pool_scale=v_pool_scale, w_out=v_w_out, norm2=v_norm2,
             w_up=v_w_up, w_down=v_w_down, norm3=v_norm3, w_gate=v_w_gate, w_ple=v_w_ple, final_norm=v_final_norm)
    small = {n: w[n] for n in SMALL}

    wsrc = _GatheredWeights({n: w[n] for n in BIG})
    reducer = _GradReducer()
    loss8, dx, small_grads = _local_step(x[0], p.reshape(2, x.shape[1], PLE_DIM), positions[0], wsrc, small, loss_target[0], reducer)
    s_send, s_recv, s_vec, s_land = _small_start(_pack_small(small_grads, loss8[0, 0]))
    gsh = reducer.finish(s_vec)

    g_out, d_out, m_out, v_out = {}, {}, {}, {}
    for n in BIG:
        shp = w[n].shape
        two = lambda a: a.reshape(shp[0] * shp[1], shp[2])
        g2, d2, m2, v2 = _adamw(two(w[n]), two(gsh[n]), two(m[n]), two(v[n]), n)
        g_out[n], d_out[n], m_out[n], v_out[n] = g2.reshape(shp), d2.reshape(shp), m2.reshape(shp), v2.reshape(shp)
    red = _small_sum(*_small_wait(s_send, s_recv, s_vec, s_land, d2))
    g_small, loss = _unpack_small(red, small)
    _, d2, m2, v2 = _adamw(_pack_small(small), red, _pack_small({n: m[n] for n in SMALL}),
                           _pack_small({n: v[n] for n in SMALL}), "small")
    for tree, packed in ((d_out, d2), (m_out, m2), (v_out, v2)):
        tree.update(_unpack_small(packed, small)[0])
    g_out.update(g_small)

    return (loss, dx[None], *[g_out[n] for n in ORDER], *[d_out[n] for n in ORDER], *[m_out[n] for n in ORDER],
            *[v_out[n] for n in ORDER])
```

```python
import jax
import jax.numpy as jnp
import numpy as np
from jax import lax
from jax.experimental import pallas as pl
from jax.experimental.pallas import tpu as pltpu

f32 = jnp.float32
MXU_DTYPE = jnp.bfloat16
COMM_DTYPE = jnp.bfloat16

D_MODEL = 1024
POOL_WIDTH = 256
POOL_GC = 64
ATTN_WIDTH = 768
HEAD_DIM = 64
N_IN = POOL_WIDTH + 3 * ATTN_WIDTH
D_FF = 4096
PLE_DIM = 256
BLK = 128
DILATIONS = (1, 4, 16)
ROT_DIM = 16
ROPE_THETA = 500000.0
EPS = 1e-6
ATTN_SCALE = HEAD_DIM ** -0.5
NEG_BIG = -1e30

ADAM_LR, ADAM_B1, ADAM_B2, ADAM_EPS, ADAM_WD, ADAM_STEP = 0.001, 0.9, 0.999, 1e-08, 0.01, 10

TM = 512
TM_WGRAD = 1024
HALO = 16
VMEM_LIMIT = 48 * 1024 * 1024
VMEM_LIMIT_LARGE = 58 * 1024 * 1024
VMEM_COMPILER_RESERVE = 6 * 1024 * 1024
N_CHIPS = 4
MESH = pl.DeviceIdType.MESH

BIG = ("w_in", "w_out", "w_up", "w_down", "w_gate", "w_ple")
FULL_SHAPE = {"w_in": (D_MODEL, N_IN), "w_out": (D_MODEL, D_MODEL), "w_up": (D_MODEL, D_FF),
              "w_down": (D_FF, D_MODEL), "w_gate": (D_MODEL, D_MODEL), "w_ple": (PLE_DIM, D_MODEL)}
COL_SHARDED = {"w_in": True, "w_out": False, "w_up": True, "w_down": False, "w_gate": False, "w_ple": True}


def _shard_shape(name):
    k, n = FULL_SHAPE[name]
    return (k, n // N_CHIPS) if COL_SHARDED[name] else (k // N_CHIPS, n)


def _cparams(sem=None, vmem=VMEM_LIMIT):
    return pltpu.CompilerParams(dimension_semantics=sem, vmem_limit_bytes=vmem)


def _resident(block_shape, index_map):
    return pl.BlockSpec(block_shape, index_map, pipeline_mode=pl.Buffered(1))


def _mx(x):
    return x.astype(MXU_DTYPE)


def _dot(a, b):
    return jnp.dot(a, b, preferred_element_type=f32)


def _dot_nt(a, b):
    return lax.dot_general(a, b, (((1,), (1,)), ((), ())), preferred_element_type=f32)


def _dot_tn(a, b):
    return lax.dot_general(a, b, (((0,), (0,)), ((), ())), preferred_element_type=f32)


def _sigmoid(x):
    return 1.0 / (1.0 + jnp.exp(-x))


def _rope_apply(y, c, s1, s2, width):
    return y * c + pltpu.roll(y, width - 8, axis=1) * s1 + pltpu.roll(y, 8, axis=1) * s2


def _rope_transpose(dy, c, s1, s2, width):
    return dy * c + pltpu.roll(dy * s1, 8, axis=1) + pltpu.roll(dy * s2, width - 8, axis=1)


def _norm_matmul(h, g, w, layer, tn, name, rope=None):
    s_len, d = h.shape
    n = w.shape[2]

    def body(*refs):
        if rope is None:
            h_ref, g_ref, w_ref, y_ref, hn_ref = refs
        else:
            h_ref, g_ref, w_ref, c_ref, s1_ref, s2_ref, y_ref, hn_ref = refs
            reps = tn // 128
            c = jnp.concatenate([c_ref[...]] * reps, axis=1)
            s1 = jnp.concatenate([s1_ref[...]] * reps, axis=1)
            s2 = jnp.concatenate([s2_ref[...]] * reps, axis=1)
        x = h_ref[...]
        r = lax.rsqrt(jnp.mean(x * x, axis=-1, keepdims=True) + EPS)
        hn = ((x * r) * g_ref[...]).astype(hn_ref.dtype)
        hn_ref[...] = hn
        for j in range(n // tn):
            y = _dot(hn, w_ref[:, j * tn:(j + 1) * tn])
            if rope is not None and POOL_WIDTH <= j * tn < POOL_WIDTH + 2 * ATTN_WIDTH:
                y = _rope_apply(y, c, s1, s2, tn)
            y_ref[:, j * tn:(j + 1) * tn] = y

    in_specs = [pl.BlockSpec((TM, d), lambda i: (i, 0)),
                pl.BlockSpec((1, d), lambda i: (0, 0)),
                _resident((None, d, n), lambda i: (layer, 0, 0))]
    args = [h, g, w]
    if rope is not None:
        assert POOL_WIDTH % tn == 0 and (2 * ATTN_WIDTH) % tn == 0
        in_specs += [pl.BlockSpec((TM, 128), lambda i: (i, 0))] * 3
        args += list(rope)
    return pl.pallas_call(
        body, name=name, grid=(s_len // TM,), in_specs=in_specs,
        out_specs=[pl.BlockSpec((TM, n), lambda i: (i, 0)), pl.BlockSpec((TM, d), lambda i: (i, 0))],
        out_shape=[jax.ShapeDtypeStruct((s_len, n), f32), jax.ShapeDtypeStruct((s_len, d), MXU_DTYPE)],
        compiler_params=_cparams(("parallel",)),
    )(*args)


def _gate_ple_fwd(h2, g, w_gate, w_ple, layer, p, p_layer, name, head=None):
    s_len, d = h2.shape

    def body(h_ref, g_ref, wg_ref, p_ref, wp_ref, *rest):
        gl_ref, hn_ref = rest[-2:]
        x = h_ref[...]
        r = lax.rsqrt(jnp.mean(x * x, axis=-1, keepdims=True) + EPS)
        hn = ((x * r) * g_ref[...]).astype(hn_ref.dtype)
        hn_ref[...] = hn
        gl = _dot(hn, wg_ref[...])
        gl_ref[...] = gl.astype(gl_ref.dtype)
        h3 = x + _sigmoid(gl) * _dot(_mx(p_ref[...]), wp_ref[...])
        if head is None:
            rest[0][...] = h3
            return
        gf_ref, t_ref, loss_ref, dh_ref, dgf_ref = rest[:5]
        i = pl.program_id(0)
        gv = gf_ref[...]
        r3 = lax.rsqrt(jnp.mean(h3 * h3, axis=-1, keepdims=True) + EPS)
        xh = h3 * r3
        diff = xh * gv - t_ref[...]
        part = 0.5 * jnp.sum(jnp.mean(diff * diff, axis=-1, keepdims=True), axis=0, keepdims=True)
        dy = diff * (1.0 / d)
        dxh = dy * gv
        dh_ref[...] = r3 * (dxh - xh * jnp.mean(dxh * xh, axis=-1, keepdims=True))
        dgsum = jnp.sum(dy * xh, axis=0, keepdims=True)
        lossb = jnp.broadcast_to(part, (8, 128))

        @pl.when(i == 0)
        def _():
            loss_ref[...] = lossb
            dgf_ref[...] = dgsum

        @pl.when(i > 0)
        def _():
            loss_ref[...] += lossb
            dgf_ref[...] += dgsum

    row = lambda i: (i, 0)
    one = lambda i: (0, 0)
    in_specs = [pl.BlockSpec((TM, d), row), pl.BlockSpec((1, d), one),
                pl.BlockSpec((None, d, d), lambda i: (layer, 0, 0)),
                pl.BlockSpec((None, TM, PLE_DIM), lambda i: (p_layer, i, 0)),
                pl.BlockSpec((None, PLE_DIM, d), lambda i: (layer, 0, 0))]
    args = [h2, g, w_gate, p, w_ple]
    saved = [jax.ShapeDtypeStruct((s_len, d), MXU_DTYPE)] * 2
    if head is None:
        out_specs = [pl.BlockSpec((TM, d), row)] * 3
        out_shape = [jax.ShapeDtypeStruct((s_len, d), f32)] + saved
    else:
        in_specs += [pl.BlockSpec((1, d), one), pl.BlockSpec((TM, d), row)]
        args += list(head)
        out_specs = [pl.BlockSpec((8, 128), one), pl.BlockSpec((TM, d), row), pl.BlockSpec((1, d), one)] \
            + [pl.BlockSpec((TM, d), row)] * 2
        out_shape = [jax.ShapeDtypeStruct((8, 128), f32), jax.ShapeDtypeStruct((s_len, d), f32),
                     jax.ShapeDtypeStruct((1, d), f32)] + saved
    return pl.pallas_call(
        body, name=name, grid=(s_len // TM,), in_specs=in_specs, out_specs=out_specs, out_shape=out_shape,
        compiler_params=_cparams(("arbitrary",)),
    )(*args)


def _gate_bwd(dh3, gl, p, p_layer, w_ple, w_gate, layer, h2, g, name):
    s_len, d = dh3.shape

    def body(dh_ref, gl_ref, p_ref, wp_ref, wg_ref, h_ref, g_ref, dh2_ref, dg_ref, de_ref, dgl_ref):
        i = pl.program_id(0)
        dh = dh_ref[...]
        gate = _sigmoid(gl_ref[...].astype(f32))
        e = _dot(_mx(p_ref[...]), wp_ref[...])
        de_ref[...] = (dh * gate).astype(de_ref.dtype)
        dgl = ((dh * e) * (gate * (1.0 - gate))).astype(dgl_ref.dtype)
        dgl_ref[...] = dgl
        dx, dgrow = _rmsnorm_bwd(_dot_nt(dgl, wg_ref[...]), h_ref[...], g_ref[...])
        dh2_ref[...] = dh + dx
        dgsum = jnp.sum(dgrow, axis=0, keepdims=True)

        @pl.when(i == 0)
        def _():
            dg_ref[...] = dgsum

        @pl.when(i > 0)
        def _():
            dg_ref[...] += dgsum

    row = lambda i: (i, 0)
    blk = pl.BlockSpec((TM, d), row)
    return pl.pallas_call(
        body, name=name, grid=(s_len // TM,),
        in_specs=[blk, blk, pl.BlockSpec((None, TM, PLE_DIM), lambda i: (p_layer, i, 0)),
                  _resident((None, PLE_DIM, d), lambda i: (layer, 0, 0)),
                  _resident((None, d, d), lambda i: (layer, 0, 0)), blk, pl.BlockSpec((1, d), lambda i: (0, 0))],
        out_specs=[blk, pl.BlockSpec((1, d), lambda i: (0, 0)), blk, blk],
        out_shape=[jax.ShapeDtypeStruct((s_len, d), f32), jax.ShapeDtypeStruct((1, d), f32),
                   jax.ShapeDtypeStruct((s_len, d), MXU_DTYPE), jax.ShapeDtypeStruct((s_len, d), MXU_DTYPE)],
        compiler_params=_cparams(("arbitrary",)),
    )(dh3, gl, p, w_ple, w_gate, h2, g)


def _rmsnorm_bwd(dhn, x, g):
    r = lax.rsqrt(jnp.mean(x * x, axis=-1, keepdims=True) + EPS)
    xh = x * r
    dxh = dhn * g
    dx = r * (dxh - xh * jnp.mean(dxh * xh, axis=-1, keepdims=True))
    return dx, dhn * xh


def _matmul_nt_norm_bwd(dy, w, layer, h_prev, g, dres, name, tk=1024, after=None):
    s_len, k_dim = dy.shape
    d = h_prev.shape[1]

    def body(dy_ref, w_ref, h_ref, g_ref, dres_ref, *rest):
        dh_ref, dg_ref = rest[-2:]
        i = pl.program_id(0)
        acc = None
        for k in range(k_dim // tk):
            part = _dot_nt(_mx(dy_ref[:, k * tk:(k + 1) * tk]), w_ref[:, k * tk:(k + 1) * tk])
            acc = part if acc is None else acc + part
        dx, dgrow = _rmsnorm_bwd(acc, h_ref[...], g_ref[...])
        dh_ref[...] = dres_ref[...] + dx
        dgsum = jnp.sum(dgrow, axis=0, keepdims=True)

        @pl.when(i == 0)
        def _():
            dg_ref[...] = dgsum

        @pl.when(i > 0)
        def _():
            dg_ref[...] += dgsum

    in_specs = [pl.BlockSpec((TM, k_dim), lambda i: (i, 0)),
                _resident((None, d, k_dim), lambda i: (layer, 0, 0)),
                pl.BlockSpec((TM, d), lambda i: (i, 0)),
                pl.BlockSpec((1, d), lambda i: (0, 0)),
                pl.BlockSpec((TM, d), lambda i: (i, 0))]
    args = [dy, w, h_prev, g, dres]
    if after is not None:
        in_specs.append(pl.BlockSpec(memory_space=pl.ANY))
        args.append(after)
    return pl.pallas_call(
        body, name=name, grid=(s_len // TM,), in_specs=in_specs,
        out_specs=[pl.BlockSpec((TM, d), lambda i: (i, 0)), pl.BlockSpec((1, d), lambda i: (0, 0))],
        out_shape=[jax.ShapeDtypeStruct((s_len, d), f32), jax.ShapeDtypeStruct((1, d), f32)],
        compiler_params=_cparams(("arbitrary",)),
    )(*args)


def _mlp_fwd(h1, g, w_up, w_down, layer, name, tf=1024):
    s_len, d = h1.shape
    ff = w_up.shape[2]

    def body(h_ref, g_ref, wu_ref, wd_ref, h2_ref, a_ref, hn_ref):
        x = h_ref[...]
        r = lax.rsqrt(jnp.mean(x * x, axis=-1, keepdims=True) + EPS)
        hn = ((x * r) * g_ref[...]).astype(hn_ref.dtype)
        hn_ref[...] = hn
        acc = x
        for j in range(ff // tf):
            a = _dot(hn, wu_ref[:, j * tf:(j + 1) * tf])
            a_ref[:, j * tf:(j + 1) * tf] = a.astype(a_ref.dtype)
            relu = jnp.maximum(a, 0.0)
            acc = acc + _dot(_mx(relu * relu), wd_ref[j * tf:(j + 1) * tf, :])
        h2_ref[...] = acc

    row = lambda i: (i, 0)
    return pl.pallas_call(
        body, name=name, grid=(s_len // TM,),
        in_specs=[pl.BlockSpec((TM, d), row), pl.BlockSpec((1, d), lambda i: (0, 0)),
                  _resident((None, d, ff), lambda i: (layer, 0, 0)), _resident((None, ff, d), lambda i: (layer, 0, 0))],
        out_specs=[pl.BlockSpec((TM, d), row), pl.BlockSpec((TM, ff), row), pl.BlockSpec((TM, d), row)],
        out_shape=[jax.ShapeDtypeStruct((s_len, d), f32), jax.ShapeDtypeStruct((s_len, ff), MXU_DTYPE),
                   jax.ShapeDtypeStruct((s_len, d), MXU_DTYPE)],
        compiler_params=_cparams(("parallel",)),
    )(h1, g, w_up, w_down)


def _mlp_bwd(dh2, w_down, w_up, layer, a, h1, g, name, tf=1024):
    s_len, d = dh2.shape
    ff = a.shape[1]

    def body(dh_ref, wd_ref, wu_ref, a_ref, h_ref, g_ref, dh1_ref, dg_ref, da_ref):
        i = pl.program_id(0)
        dh = dh_ref[...]
        dhb = _mx(dh)
        acc = None
        for j in range(ff // tf):
            cols = slice(j * tf, (j + 1) * tf)
            dact = _dot_nt(dhb, wd_ref[cols, :])
            da = (dact * (2.0 * jnp.maximum(a_ref[:, cols].astype(f32), 0.0))).astype(da_ref.dtype)
            da_ref[:, cols] = da
            part = _dot_nt(da, wu_ref[:, cols])
            acc = part if acc is None else acc + part
        dx, dgrow = _rmsnorm_bwd(acc, h_ref[...], g_ref[...])
        dh1_ref[...] = dh + dx
        dgsum = jnp.sum(dgrow, axis=0, keepdims=True)

        @pl.when(i == 0)
        def _():
            dg_ref[...] = dgsum

        @pl.when(i > 0)
        def _():
            dg_ref[...] += dgsum

    row = lambda i: (i, 0)
    return pl.pallas_call(
        body, name=name, grid=(s_len // TM,),
        in_specs=[pl.BlockSpec((TM, d), row), _resident((None, ff, d), lambda i: (layer, 0, 0)),
                  _resident((None, d, ff), lambda i: (layer, 0, 0)), pl.BlockSpec((TM, ff), row),
                  pl.BlockSpec((TM, d), row), pl.BlockSpec((1, d), lambda i: (0, 0))],
        out_specs=[pl.BlockSpec((TM, d), row), pl.BlockSpec((1, d), lambda i: (0, 0)), pl.BlockSpec((TM, ff), row)],
        out_shape=[jax.ShapeDtypeStruct((s_len, d), f32), jax.ShapeDtypeStruct((1, d), f32),
                   jax.ShapeDtypeStruct((s_len, ff), MXU_DTYPE)],
        compiler_params=_cparams(("arbitrary",), vmem=VMEM_LIMIT_LARGE),
    )(dh2, w_down, w_up, a, h1, g)


def _weight_grad(a, b, name, act=False, lead=None):
    s_len, k_dim = a.shape[-2:]
    n = b.shape[1]
    tka = min(k_dim, 2048)
    tnb = n if n <= 1024 else (2048 if n % 2048 == 0 else 640)
    tm = 2 * TM_WGRAD
    vmem = 2 * tm * (tka * a.dtype.itemsize + tnb * b.dtype.itemsize) + tka * tnb * (4 + 2 * jnp.dtype(COMM_DTYPE).itemsize)
    if vmem > VMEM_LIMIT - VMEM_COMPILER_RESERVE or s_len % tm:
        tm = TM_WGRAD
    ns = s_len // tm

    def body(a_ref, b_ref, o_ref, acc_ref):
        s = pl.program_id(2)
        x = a_ref[...]
        if act:
            relu = jnp.maximum(x, 0.0)
            x = relu * relu

        @pl.when(s == 0)
        def _():
            acc_ref[...] = jnp.zeros_like(acc_ref)

        acc_ref[...] += _dot_tn(_mx(x), _mx(b_ref[...]))

        @pl.when(s == ns - 1)
        def _():
            o_ref[...] = acc_ref[...].astype(o_ref.dtype)

    if lead is None:
        a_spec = pl.BlockSpec((tm, tka), lambda i, j, s: (s, i))
    else:
        a_spec = pl.BlockSpec((None, tm, tka), lambda i, j, s: (lead, s, i))
    return pl.pallas_call(
        body, name=name, grid=(k_dim // tka, n // tnb, ns),
        in_specs=[a_spec, pl.BlockSpec((tm, tnb), lambda i, j, s: (s, j))],
        out_specs=pl.BlockSpec((tka, tnb), lambda i, j, s: (i, j)),
        out_shape=jax.ShapeDtypeStruct((k_dim, n), COMM_DTYPE),
        scratch_shapes=[pltpu.VMEM((tka, tnb), f32)],
        compiler_params=_cparams(("parallel", "parallel", "arbitrary")),
    )(a, b)


def _group_select(lane, x2, x4, x8, x16):
    grp = lane // POOL_GC
    return jnp.where(grp == 0, x2, jnp.where(grp == 1, x4, jnp.where(grp == 2, x8, x16)))


def _pool_window(lane):
    grp = lane // POOL_GC
    return jnp.where(grp == 0, 2, jnp.where(grp == 1, 4, jnp.where(grp == 2, 8, 16)))


def _pool_y(u, halo, i):
    xs = jnp.concatenate([jnp.where(i > 0, halo, 0.0), u], axis=0)
    s2 = xs + pltpu.roll(xs, 1, axis=0)
    s4 = s2 + pltpu.roll(s2, 2, axis=0)
    s8 = s4 + pltpu.roll(s4, 4, axis=0)
    s16 = s8 + pltpu.roll(s8, 8, axis=0)
    lane = lax.broadcasted_iota(jnp.int32, xs.shape, 1)
    sel = _group_select(lane, s2, s4, s8, s16)[HALO:, :]
    t = i * TM + lax.broadcasted_iota(jnp.int32, u.shape, 0)
    cnt = jnp.minimum(_pool_window(lax.broadcasted_iota(jnp.int32, u.shape, 1)), t + 1).astype(f32)
    return sel / cnt - u


def _group_weights(l0, l1, l2):
    mx = jnp.maximum(jnp.maximum(l0, l1), l2)
    e0, e1, e2 = jnp.exp(l0 - mx), jnp.exp(l1 - mx), jnp.exp(l2 - mx)
    den = e0 + e1 + e2
    return e0 / den, e1 / den, e2 / den


def _mixer_out_proj(z, wbd, scale, outs, lses, w_out, layer, h, name):
    s_len, d = h.shape

    def body(u_ref, halo_ref, wbd_ref, sc_ref, o0, o1, o2, l0, l1, l2, wo_ref, h_ref, m_ref, h1_ref):
        i = pl.program_id(0)
        y = _pool_y(u_ref[...], halo_ref[...], i)
        pool = _dot(_mx(y), wbd_ref[...]) * sc_ref[...]
        w0, w1, w2 = _group_weights(l0[...], l1[...], l2[...])
        m = jnp.concatenate([pool, o0[...] * w0, o1[...] * w1, o2[...] * w2], axis=1).astype(m_ref.dtype)
        m_ref[...] = m
        h1_ref[...] = h_ref[...] + _dot(m, wo_ref[...])

    row = lambda i: (i, 0)
    blk = pl.BlockSpec((TM, 256), row)
    grp = [pl.BlockSpec((TM, 256), lambda i, g=g: (i, g)) for g in range(3)]
    return pl.pallas_call(
        body, name=name, grid=(s_len // TM,),
        in_specs=[blk, pl.BlockSpec((HALO, 256), lambda i: (jnp.maximum(i * (TM // HALO) - 1, 0), 0)),
                  pl.BlockSpec((256, 256), lambda i: (0, 0)), pl.BlockSpec((1, 256), lambda i: (0, 0))] + grp + grp
        + [_resident((None, d, d), lambda i: (layer, 0, 0)), pl.BlockSpec((TM, d), row)],
        out_specs=[pl.BlockSpec((TM, d), row)] * 2,
        out_shape=[jax.ShapeDtypeStruct((s_len, d), MXU_DTYPE), jax.ShapeDtypeStruct((s_len, d), f32)],
        compiler_params=_cparams(("parallel",)),
    )(z, z, wbd, scale, outs, outs, outs, lses, lses, lses, w_out, h)


def _head_sums(x):
    r = lax.broadcasted_iota(jnp.int32, (256, 256), 0) // HEAD_DIM
    c = lax.broadcasted_iota(jnp.int32, (256, 256), 1) // HEAD_DIM
    ones = jnp.where(r == c, 1.0, 0.0).astype(jnp.bfloat16)
    hi = x.astype(jnp.bfloat16)
    lo = (x - hi.astype(f32)).astype(jnp.bfloat16)
    return _dot(hi, ones) + _dot(lo, ones)


def _out_combine_bwd(dh1, w_out, layer, outs, lses, name, after=None):
    s_len, d = dh1.shape

    def body(dh_ref, w_ref, o0, o1, o2, l0, l1, l2, *rest):
        dp_ref, do_ref, dl_ref = rest[-3:]
        dm = _dot_nt(_mx(dh_ref[...]), w_ref[...])
        dp_ref[...] = dm[:, :POOL_WIDTH]
        w = _group_weights(l0[...], l1[...], l2[...])
        da = [dm[:, POOL_WIDTH + 256 * g:POOL_WIDTH + 256 * (g + 1)] for g in range(3)]
        o = (o0[...], o1[...], o2[...])
        dw = [_head_sums(da[g] * o[g]) for g in range(3)]
        t = w[0] * dw[0] + w[1] * dw[1] + w[2] * dw[2]
        do_ref[...] = jnp.concatenate([da[g] * w[g] for g in range(3)], axis=1)
        dl_ref[...] = jnp.concatenate([w[g] * t for g in range(3)], axis=1)

    grp = [pl.BlockSpec((TM, 256), lambda i, g=g: (i, g)) for g in range(3)]
    in_specs = [pl.BlockSpec((TM, d), lambda i: (i, 0)), _resident((None, d, d), lambda i: (layer, 0, 0))] + grp + grp
    args = [dh1, w_out, outs, outs, outs, lses, lses, lses]
    if after is not None:
        in_specs.append(pl.BlockSpec(memory_space=pl.ANY))
        args.append(after)
    return pl.pallas_call(
        body, name=name, grid=(s_len // TM,), in_specs=in_specs,
        out_specs=[pl.BlockSpec((TM, POOL_WIDTH), lambda i: (i, 0))] + [pl.BlockSpec((TM, ATTN_WIDTH), lambda i: (i, 0))] * 2,
        out_shape=[jax.ShapeDtypeStruct((s_len, POOL_WIDTH), f32)] + [jax.ShapeDtypeStruct((s_len, ATTN_WIDTH), f32)] * 2,
        compiler_params=_cparams(("parallel",)),
    )(*args)


def _pool_bwd(z, dm, wbd, scale, name, after=None):
    s_len = z.shape[0]
    n_halo = s_len // HALO

    def body(u_ref, uh_ref, d_ref, dh_ref, wbd_ref, sc_ref, *rest):
        du_ref, dw_ref, dsc_ref = rest[-3:]
        i = pl.program_id(0)
        last = pl.num_programs(0) - 1
        y = _pool_y(u_ref[...], uh_ref[...], i)
        yb = _mx(y)
        dpo = d_ref[...]
        sc = sc_ref[...]
        dsc = jnp.sum(dpo * _dot(yb, wbd_ref[...]), axis=0, keepdims=True)
        dwp = _dot_tn(yb, _mx(dpo * sc))

        @pl.when(i == 0)
        def _():
            dsc_ref[...] = dsc
            dw_ref[...] = dwp

        @pl.when(i > 0)
        def _():
            dsc_ref[...] += dsc
            dw_ref[...] += dwp

        ext = jnp.concatenate([dpo, jnp.where(i < last, dh_ref[...], 0.0)], axis=0)
        dy = _dot_nt(_mx(ext * sc), wbd_ref[...])
        t = i * TM + lax.broadcasted_iota(jnp.int32, ext.shape, 0)
        lane = lax.broadcasted_iota(jnp.int32, ext.shape, 1)
        e = dy / jnp.minimum(_pool_window(lane), t + 1).astype(f32)
        rows = ext.shape[0]
        f2 = e + pltpu.roll(e, rows - 1, axis=0)
        f4 = f2 + pltpu.roll(f2, rows - 2, axis=0)
        f8 = f4 + pltpu.roll(f4, rows - 4, axis=0)
        f16 = f8 + pltpu.roll(f8, rows - 8, axis=0)
        du_ref[...] = (_group_select(lane, f2, f4, f8, f16) - dy)[:TM, :].astype(du_ref.dtype)

    row = lambda i: (i, 0)
    blk = pl.BlockSpec((TM, 256), row)
    extra = [] if after is None else [after]
    return pl.pallas_call(
        body, name=name, grid=(s_len // TM,),
        in_specs=[blk, pl.BlockSpec((HALO, 256), lambda i: (jnp.maximum(i * (TM // HALO) - 1, 0), 0)),
                  blk, pl.BlockSpec((HALO, 256), lambda i: (jnp.minimum((i + 1) * (TM // HALO), n_halo - 1), 0)),
                  pl.BlockSpec((256, 256), lambda i: (0, 0)), pl.BlockSpec((1, 256), lambda i: (0, 0))]
        + [pl.BlockSpec(memory_space=pl.ANY)] * len(extra),
        out_specs=[blk, pl.BlockSpec((256, 256), lambda i: (0, 0)), pl.BlockSpec((1, 256), lambda i: (0, 0))],
        out_shape=[jax.ShapeDtypeStruct((s_len, N_IN), MXU_DTYPE), jax.ShapeDtypeStruct((256, 256), f32),
                   jax.ShapeDtypeStruct((1, 256), f32)],
        compiler_params=_cparams(("arbitrary",)),
    )(z, z, dm, dm, wbd, scale, *extra)


def _tri_masks():
    qi = lax.broadcasted_iota(jnp.int32, (BLK, BLK), 0)
    ki = lax.broadcasted_iota(jnp.int32, (BLK, BLK), 1)
    return qi >= ki, ki >= qi


ATTN_SUPER_PER_STEP = (8, 4, 1)
Q_COL, K_COL, V_COL = POOL_WIDTH // 128, (POOL_WIDTH + ATTN_WIDTH) // 128, (POOL_WIDTH + 2 * ATTN_WIDTH) // 128


def _rows(ref, start, dil):
    if dil == 1:
        return ref[pl.ds(start, BLK), :]
    return ref[pl.ds(start, BLK, stride=dil), :]


ATTN_BLOCKS_TOGETHER = 8


def _set_rows(ref, start, dil, val):
    if dil == 1:
        ref[pl.ds(start, BLK), :] = val
    else:
        ref[pl.ds(start, BLK, stride=dil), :] = val


def _attn_fwd(z, g, prev, name):
    s_len = z.shape[0]
    dil, m = DILATIONS[g], ATTN_SUPER_PER_STEP[g]
    sbr = BLK * dil
    rows = sbr * m

    def body(*refs):
        q_ref, kc_ref, kp_ref, vc_ref, vp_ref = refs[:5]
        o_ref, l_ref = refs[-2:]
        st = pl.program_id(0)
        low, up = _tri_masks()
        head0 = lax.broadcasted_iota(jnp.int32, (BLK, 128), 1) < HEAD_DIM
        blocks = [(sb, r) for sb in range(m) for r in range(dil)]
        for g0 in range(0, len(blocks), ATTN_BLOCKS_TOGETHER):
            grp = blocks[g0:g0 + ATTN_BLOCKS_TOGETHER]
            loaded = []
            for sb, r in grp:
                base = sb * sbr + r
                if sb == 0:
                    kp, vp = _rows(kp_ref, r, dil), _rows(vp_ref, r, dil)
                else:
                    kp, vp = _rows(kc_ref, base - sbr, dil), _rows(vc_ref, base - sbr, dil)
                qs = _rows(q_ref, base, dil) * ATTN_SCALE
                loaded.append((_mx(jnp.where(head0, qs, 0.0)), _mx(jnp.where(head0, 0.0, qs)),
                               jnp.concatenate([_mx(kp), _mx(_rows(kc_ref, base, dil))], axis=0),
                               jnp.concatenate([_mx(vp), _mx(_rows(vc_ref, base, dil))], axis=0)))
            scores = [(_dot_nt(q0, k2), _dot_nt(q1, k2)) for q0, q1, k2, _ in loaded]
            soft = []
            for (sb, _), pair in zip(grp, scores):
                valid = jnp.concatenate([up & (st > 0) if sb == 0 else up, low], axis=1)
                heads = []
                for s in pair:
                    s = jnp.where(valid, s, NEG_BIG)
                    mx = jnp.max(s, axis=-1, keepdims=True)
                    e = jnp.exp(s - mx)
                    l = jnp.sum(e, axis=-1, keepdims=True)
                    heads.append((_mx(e / l), jnp.broadcast_to(mx + jnp.log(l), (BLK, 128))))
                soft.append(heads)
            for (sb, r), heads, (_, _, _, v2) in zip(grp, soft, loaded):
                base = sb * sbr + r
                _set_rows(o_ref, base, dil, jnp.where(head0, _dot(heads[0][0], v2), _dot(heads[1][0], v2)))
                _set_rows(l_ref, base, dil, jnp.where(head0, heads[0][1], heads[1][1]))

    def cur(col):
        return pl.BlockSpec((rows, 128), lambda st, hp: (st, col + 2 * g + hp))

    def before(col):
        return pl.BlockSpec((sbr, 128), lambda st, hp: (jnp.maximum(st * m - 1, 0), col + 2 * g + hp))

    in_specs = [cur(Q_COL), cur(K_COL), before(K_COL), cur(V_COL), before(V_COL)]
    args = [z, z, z, z, z]
    aliases = {}
    if prev is not None:
        in_specs += [pl.BlockSpec(memory_space=pl.ANY)] * 2
        args += list(prev)
        aliases = {5: 0, 6: 1}
    return pl.pallas_call(
        body, name=name, grid=(s_len // rows, 2), in_specs=in_specs, out_specs=[cur(0), cur(0)],
        out_shape=[jax.ShapeDtypeStruct((s_len, ATTN_WIDTH), f32)] * 2, input_output_aliases=aliases,
        compiler_params=_cparams(("parallel", "parallel")),
    )(*args)


def _stack_heads(x, head0):
    return jnp.concatenate([_mx(jnp.where(head0, x, 0.0)), _mx(jnp.where(head0, 0.0, x))], axis=0)


def _head_rows(x):
    xt = x.T
    return jnp.concatenate([jnp.broadcast_to(xt[0:1, :], (BLK, BLK)),
                            jnp.broadcast_to(xt[HEAD_DIM:HEAD_DIM + 1, :], (BLK, BLK))], axis=0)


def _attn_bwd(z, do, lse, dlt, tabs, dz, g, name):
    s_len = z.shape[0]
    dil, m = DILATIONS[g], ATTN_SUPER_PER_STEP[g]
    sbr = BLK * dil
    rows = sbr * m
    nsteps = s_len // rows

    def body(q_ref, qn_ref, kc_ref, kp_ref, vc_ref, vp_ref, do_ref, don_ref, l_ref, ln_ref, d_ref, dn_ref,
             c_ref, s1_ref, s2_ref, dz_in, dz_ref, dq_buf, dk_buf, dv_buf, out_buf, sems):
        del dz_in
        st, hp = pl.program_id(0), pl.program_id(1)
        head0 = lax.broadcasted_iota(jnp.int32, (BLK, 128), 1) < HEAD_DIM
        key_i = lax.broadcasted_iota(jnp.int32, (2 * BLK, BLK), 0) & (BLK - 1)
        query_i = lax.broadcasted_iota(jnp.int32, (2 * BLK, BLK), 1)
        same_t, cross_t = query_i >= key_i, key_i >= query_i
        def load(r):
            keys, vals = [_stack_heads(_rows(kp_ref, r, dil), head0)], [_stack_heads(_rows(vp_ref, r, dil), head0)]
            qs, dos, lses, dlts = [], [], [], []
            for sb in range(m):
                base = sb * sbr + r
                keys.append(_stack_heads(_rows(kc_ref, base, dil), head0))
                vals.append(_stack_heads(_rows(vc_ref, base, dil), head0))
                qs.append(_mx(_rows(q_ref, base, dil)))
                dos.append(_mx(_rows(do_ref, base, dil)))
                lses.append(_head_rows(_rows(l_ref, base, dil)))
                dlts.append(_head_rows(_rows(d_ref, base, dil)))
            qs.append(_mx(_rows(qn_ref, r, dil)))
            dos.append(_mx(_rows(don_ref, r, dil)))
            lses.append(_head_rows(_rows(ln_ref, r, dil)))
            dlts.append(_head_rows(_rows(dn_ref, r, dil)))
            return keys, vals, qs, dos, lses, dlts

        def products(data):
            keys, vals, qs, dos, _, _ = data
            return ([(_dot_nt(keys[j + 1], qs[j]), _dot_nt(vals[j + 1], dos[j])) for j in range(m)],
                    [(_dot_nt(keys[j], qs[j]), _dot_nt(vals[j], dos[j])) for j in range(m + 1)])

        def finish(data, raw):
            lses, dlts = data[4], data[5]

            def one(pair, j, valid):
                p = jnp.where(valid, jnp.exp(pair[0] * ATTN_SCALE - lses[j]), 0.0)
                return _mx(p), _mx(p * (pair[1] - dlts[j]) * ATTN_SCALE)

            same = [one(raw[0][j], j, same_t) for j in range(m)]
            cross = [one(raw[1][j], j, cross_t & (st > 0) if j == 0 else
                         (cross_t & (st < nsteps - 1) if j == m else cross_t)) for j in range(m + 1)]
            return same, cross

        def gradients(r, data, fin):
            keys, _, qs, dos, _, _ = data
            same, cross = fin
            for sb in range(m):
                base = sb * sbr + r
                (p_a, ds_a), (_, ds_x), (p_n, ds_n) = same[sb], cross[sb], cross[sb + 1]
                dq = _dot_tn(ds_a, keys[sb + 1]) + _dot_tn(ds_x, keys[sb])
                dk2 = _dot(ds_a, qs[sb]) + _dot(ds_n, qs[sb + 1])
                dv2 = _dot(p_a, dos[sb]) + _dot(p_n, dos[sb + 1])
                c, s1, s2 = _rows(c_ref, base, dil), _rows(s1_ref, base, dil), _rows(s2_ref, base, dil)
                _set_rows(dq_buf, base, dil, _rope_transpose(dq, c, s1, s2, 128))
                _set_rows(dk_buf, base, dil, _rope_transpose(jnp.where(head0, dk2[:BLK], dk2[BLK:]), c, s1, s2, 128))
                _set_rows(dv_buf, base, dil, jnp.where(head0, dv2[:BLK], dv2[BLK:]))

        def residue_group(rg, carry):
            rs = [rg * group + i for i in range(group)]
            data = [load(r) for r in rs]
            raws = [products(d) for d in data]
            fins = [finish(d, raw) for d, raw in zip(data, raws)]
            for r, d, fin in zip(rs, data, fins):
                gradients(r, d, fin)
            return carry

        group = max(1, min(dil, ATTN_BLOCKS_TOGETHER // m))
        if dil // group <= 2:
            for rg in range(dil // group):
                residue_group(rg, 0)
        else:
            lax.fori_loop(0, dil // group, residue_group, 0)
        copies = []
        for t, (buf, col) in enumerate(((dq_buf, Q_COL), (dk_buf, K_COL), (dv_buf, V_COL))):
            out_buf[t] = buf[...].astype(out_buf.dtype)
            lane0 = pl.multiple_of((col + 2 * g + hp) * 128, 128)
            dst = dz_ref.at[pl.ds(pl.multiple_of(st * rows, rows), rows), pl.ds(lane0, 128)]
            cp = pltpu.make_async_copy(out_buf.at[t], dst, sems.at[t])
            cp.start()
            copies.append(cp)
        for cp in copies:
            cp.wait()

    def cur(col):
        return pl.BlockSpec((rows, 128), lambda st, hp: (st, col + 2 * g + hp))

    def before(col):
        return pl.BlockSpec((sbr, 128), lambda st, hp: (jnp.maximum(st * m - 1, 0), col + 2 * g + hp))

    def after(col):
        return pl.BlockSpec((sbr, 128), lambda st, hp: (jnp.minimum((st + 1) * m, s_len // sbr - 1), col + 2 * g + hp))

    tab = pl.BlockSpec((rows, 128), lambda st, hp: (st, 0))
    return pl.pallas_call(
        body, name=name, grid=(nsteps, 2),
        in_specs=[cur(Q_COL), after(Q_COL), cur(K_COL), before(K_COL), cur(V_COL), before(V_COL),
                  cur(0), after(0), cur(0), after(0), cur(0), after(0), tab, tab, tab,
                  pl.BlockSpec(memory_space=pl.ANY)],
        out_specs=pl.BlockSpec(memory_space=pl.ANY),
        out_shape=jax.ShapeDtypeStruct(dz.shape, dz.dtype), input_output_aliases={15: 0},
        scratch_shapes=[pltpu.VMEM((rows, 128), f32)] * 3 + [pltpu.VMEM((3, rows, 128), dz.dtype),
                                                            pltpu.SemaphoreType.DMA((3,))],
        compiler_params=_cparams(("arbitrary", "arbitrary")),
    )(z, z, z, z, z, z, do, do, lse, lse, dlt, dlt, *tabs, dz)


def _rope_tables(positions):
    inv_freq = ROPE_THETA ** (-jnp.arange(0, ROT_DIM, 2, dtype=f32) / ROT_DIM)
    ang = positions.astype(f32)[:, None] * inv_freq
    cos, sin = jnp.cos(ang), jnp.sin(ang)
    s_len = positions.shape[0]
    zero8, rest = jnp.zeros((s_len, 8), f32), jnp.zeros((s_len, HEAD_DIM - ROT_DIM), f32)
    c = jnp.concatenate([cos, cos, jnp.ones((s_len, HEAD_DIM - ROT_DIM), f32)], axis=1)
    s1 = jnp.concatenate([-sin, zero8, rest], axis=1)
    s2 = jnp.concatenate([zero8, sin, rest], axis=1)
    return tuple(jnp.tile(t, (1, 2)) for t in (c, s1, s2))


def _block_diag(pool_w):
    out = jnp.zeros((POOL_WIDTH, POOL_WIDTH), pool_w.dtype)
    for g in range(4):
        out = lax.dynamic_update_slice(out, pool_w[g], (g * POOL_GC, g * POOL_GC))
    return out


def _layer_fwd(h, p, wsrc, small, layer, tabs, head=None):
    nm = f"l{layer}_"
    wts, wl = wsrc.take(layer, ("w_in",), (h,) if layer else tuple(tabs))
    z, hn1 = _norm_matmul(h, small["norm1"][layer][None], wts["w_in"], wl, 256, nm + "in_proj", rope=tabs)
    ol = None
    for g in range(3):
        ol = _attn_fwd(z, g, ol, nm + f"attn_fwd{g}")
    outs, lses = ol
    wbd = _mx(_block_diag(small["pool_w"][layer]))
    scale = small["pool_scale"][layer][None]
    wts.update(wsrc.take(layer, ("w_out",), (outs,))[0])
    m, h1 = _mixer_out_proj(z, wbd, scale, outs, lses, wts["w_out"], wl, h, nm + "mixer_out")
    wts.update(wsrc.take(layer, ("w_up", "w_down"), (h1,))[0])
    h2, a, hn2 = _mlp_fwd(h1, small["norm2"][layer][None], wts["w_up"], wts["w_down"], wl, nm + "mlp")
    wts.update(wsrc.take(layer, ("w_gate", "w_ple"), (h2,))[0])
    *h3, gl, hn3 = _gate_ple_fwd(h2, small["norm3"][layer][None], wts["w_gate"], wts["w_ple"], wl, p, layer,
                                 nm + "gate_ple", head=head)
    saved = dict(h=h, z=z, hn1=hn1, outs=outs, lses=lses, wbd=wbd, scale=scale, m=m, h1=h1, a=a, hn2=hn2, h2=h2,
                 gl=gl, hn3=hn3, wts=wts, wl=wl)
    return h3, saved


def _layer_bwd(dh3, sv, p, small, layer, tabs128, reducer):
    nm = f"l{layer}_"
    wts, wl = sv["wts"], sv["wl"]
    dh2, dg3, de, dgl = _gate_bwd(dh3, sv["gl"], p, layer, wts["w_ple"], wts["w_gate"], wl, sv["h2"],
                                  small["norm3"][layer][None], nm + "gate_bwd")
    reducer.add("w_gate", layer, _weight_grad(sv["hn3"], dgl, nm + "dw_gate"))
    reducer.add("w_ple", layer, _weight_grad(p, de, nm + "dw_ple", lead=layer))
    dh1, dg2, da = _mlp_bwd(dh2, wts["w_down"], wts["w_up"], wl, sv["a"], sv["h1"], small["norm2"][layer][None],
                            nm + "mlp_bwd")
    reducer.add("w_down", layer, _weight_grad(sv["a"], dh2, nm + "dw_down", act=True))
    started = reducer.add("w_up", layer, _weight_grad(sv["hn2"], da, nm + "dw_up"))
    dpool, do, dlt = _out_combine_bwd(dh1, wts["w_out"], wl, sv["outs"], sv["lses"], nm + "out_bwd", after=started)
    started = reducer.add("w_out", layer, _weight_grad(sv["m"], dh1, nm + "dw_out"))
    dz, dwbd, dscale = _pool_bwd(sv["z"], dpool, sv["wbd"], sv["scale"], nm + "pool_bwd", after=started)
    for g in range(3):
        dz = _attn_bwd(sv["z"], do, sv["lses"], dlt, tabs128, dz, g, nm + f"attn_bwd{g}")
    started = reducer.add("w_in", layer, _weight_grad(sv["hn1"], dz, nm + "dw_in"))
    dh0, dg1 = _matmul_nt_norm_bwd(dz, wts["w_in"], wl, sv["h"], small["norm1"][layer][None], dh1, nm + "in_bwd",
                                   tk=512, after=started)
    dpool_w = jnp.stack([dwbd[g * POOL_GC:(g + 1) * POOL_GC, g * POOL_GC:(g + 1) * POOL_GC] for g in range(4)])
    sg = dict(norm1=dg1[0], norm2=dg2[0], norm3=dg3[0], pool_w=dpool_w, pool_scale=dscale[0])
    return dh0, sg


def _local_step(x, p, positions, wsrc, small, target, reducer):
    tabs128 = _rope_tables(positions)
    (h,), sv0 = _layer_fwd(x, p, wsrc, small, 0, tabs128)
    (loss, dh, dgf), sv1 = _layer_fwd(h, p, wsrc, small, 1, tabs128, head=(small["final_norm"][None], target))
    saved = [sv0, sv1]
    sgs = [None, None]
    for layer in (1, 0):
        dh, sgs[layer] = _layer_bwd(dh, saved[layer], p, small, layer, tabs128, reducer)
    small_grads = {k: jnp.stack([sgs[0][k], sgs[1][k]]) for k in sgs[0]}
    small_grads["final_norm"] = dgf[0]
    return loss, dh, small_grads


HBM = pl.BlockSpec(memory_space=pltpu.HBM)


def _my_place():
    return lax.axis_index("x"), lax.axis_index("y"), lax.axis_index("c")


def _other_chips(x, y):
    return [(1 - x, y), (x, 1 - y), (1 - x, 1 - y)]


def _window(ref, name, chip):
    k, n = _shard_shape(name)
    if COL_SHARDED[name]:
        return ref.at[:, pl.ds(pl.multiple_of(chip * n, 128), n)]
    return ref.at[pl.ds(pl.multiple_of(chip * k, 128), k), :]


def _chip_index():
    return jnp.reshape(2 * lax.axis_index("x") + lax.axis_index("y"), (1,)).astype(jnp.int32)


def _shard_block(name, tr):
    ks, ns = _shard_shape(name)
    if COL_SHARDED[name]:
        return (tr, ns), lambda i, me: (i, me[0])
    return (tr, ns), lambda i, me: (me[0] * (ks // tr) + i, 0)


def _place_shard(w, name, layer):
    ks, ns = _shard_shape(name)
    tr = min(ks, 256)
    shape, index = _shard_block(name, tr)

    def body(me_ref, w_ref, o_ref):
        o_ref[...] = w_ref[...].astype(o_ref.dtype)

    return pl.pallas_call(
        body, name=f"place_{name}{layer}",
        grid_spec=pltpu.PrefetchScalarGridSpec(
            num_scalar_prefetch=1, grid=(ks // tr,),
            in_specs=[pl.BlockSpec((None, tr, ns), lambda i, me: (layer, i, 0))],
            out_specs=pl.BlockSpec((None,) + shape, lambda i, me: (0,) + index(i, me))),
        out_shape=jax.ShapeDtypeStruct((1,) + FULL_SHAPE[name], MXU_DTYPE),
        compiler_params=_cparams(("parallel",)),
    )(_chip_index(), w)


GATHER_ORDER = [("w_in", 0), ("w_out", 0), ("w_up", 0), ("w_down", 0), ("w_gate", 0), ("w_ple", 0),
                ("w_in", 1), ("w_out", 1), ("w_up", 1), ("w_down", 1), ("w_gate", 1), ("w_ple", 1)]
SEM = pl.BlockSpec(memory_space=pltpu.SEMAPHORE)
EFFECT = pltpu.SideEffectType.DATAFLOW_SIDE_EFFECTING


def _gather_copy(src_ref, dst_ref, name, idx, j, chip, send_sems, recv_sems, c):
    cx, cy = chip
    return pltpu.make_async_remote_copy(
        src_ref=src_ref, dst_ref=dst_ref, send_sem=send_sems.at[3 * idx + j], recv_sem=recv_sems.at[3 * idx + j],
        device_id=(cx, cy, c), device_id_type=MESH)


def _gather_start(placed, order, tag, after=None):
    n = len(order)
    extra = [] if after is None else [after]

    def body(*refs):
        ins = refs[:n]
        k = n + len(extra)
        send_sems, recv_sems = refs[k], refs[k + 1]
        outs = refs[k + 2:k + 2 + n]
        token = refs[-1]
        x, y, c = _my_place()
        me = 2 * x + y
        for idx, (name, _) in enumerate(order):
            for j, chip in enumerate(_other_chips(x, y)):
                _gather_copy(_window(ins[idx].at[0], name, me), _window(outs[idx].at[0], name, me), name, idx, j, chip,
                             send_sems, recv_sems, c).start()
        token[...] = jnp.zeros_like(token)

    res = pl.pallas_call(
        body, name="gather_start" + tag,
        out_shape=(pltpu.SemaphoreType.DMA((3 * n,)), pltpu.SemaphoreType.DMA((3 * n,)))
        + tuple(pltpu.HBM(a.shape, a.dtype) for a in placed) + (jax.ShapeDtypeStruct((8, 128), f32),),
        in_specs=[HBM] * n + [pl.BlockSpec(memory_space=pl.ANY)] * len(extra),
        out_specs=(SEM, SEM) + (HBM,) * n + (pl.BlockSpec(memory_space=pltpu.VMEM),),
        input_output_aliases={i: i + 2 for i in range(n)},
        compiler_params=pltpu.CompilerParams(has_side_effects=EFFECT),
    )(*[pltpu.with_memory_space_constraint(a, pltpu.HBM) for a in placed], *extra)
    return res[0], res[1], list(res[2:2 + n]), res[-1]


def _gather_wait(send_sems, recv_sems, arrays, order, idxs, after, name):
    n = len(idxs)

    def body(*refs):
        ins = refs[:n]
        send_ref, recv_ref = refs[n], refs[n + 1]
        x, y, c = _my_place()
        me = 2 * x + y
        for k, idx in enumerate(idxs):
            wname = order[idx][0]
            for j, chip in enumerate(_other_chips(x, y)):
                cx, cy = chip
                mine = _window(ins[k].at[0], wname, me)
                land = _window(ins[k].at[0], wname, 2 * cx + cy)
                _gather_copy(mine, mine, wname, idx, j, chip, send_ref, recv_ref, c).wait_send()
                _gather_copy(land, land, wname, idx, j, chip, send_ref, recv_ref, c).wait_recv()

    operands = list(arrays) + [send_sems, recv_sems] + list(after)
    in_specs = [HBM] * n + [SEM, SEM] + [pl.BlockSpec(memory_space=pl.ANY)] * len(after)
    res = pl.pallas_call(
        body, name=name, out_shape=tuple(pltpu.HBM(a.shape, a.dtype) for a in arrays),
        in_specs=in_specs, out_specs=(HBM,) * n, input_output_aliases={i: i for i in range(n)},
        compiler_params=pltpu.CompilerParams(has_side_effects=EFFECT),
    )(*operands)
    return list(res)


class _GatheredWeights:
    def __init__(self, shards):
        self.starts = []
        token = None
        for tag, order in (("_first", GATHER_ORDER[:1]), ("_rest", GATHER_ORDER[1:])):
            placed = [_place_shard(shards[name], name, layer) for name, layer in order]
            self.starts.append((order,) + _gather_start(placed, order, tag, token))
            token = self.starts[-1][-1]

    def take(self, layer, names, after):
        order, send, recv, arrays, _ = next(s for s in self.starts if (names[0], layer) in s[0])
        after = list(after)
        if order is self.starts[0][0]:
            after.append(self.starts[-1][-1])
        idxs = [order.index((n, layer)) for n in names]
        got = _gather_wait(send, recv, [arrays[i] for i in idxs], order, idxs, after, f"gather_wait{layer}_{names[0]}")
        return dict(zip(names, got)), 0


N_DEV = 8


def _reduce_copies(dws, lands, names, layer, send_sems, recv_sems):
    x, y, c = _my_place()
    me, my_dev = 2 * x + y, 4 * x + 2 * y + c
    out = []
    for t, name in enumerate(names):
        for j, (cx, cy) in enumerate(_other_chips(x, y)):
            out.append((pltpu.make_async_remote_copy(
                src_ref=_window(dws[t], name, 2 * cx + cy), dst_ref=lands[t].at[my_dev],
                send_sem=send_sems.at[4 * t + j], recv_sem=recv_sems.at[N_DEV * t + my_dev],
                device_id=(cx, cy, layer), device_id_type=MESH), False))
        out.append((pltpu.make_async_remote_copy(
            src_ref=_window(dws[t], name, me), dst_ref=lands[t].at[my_dev],
            send_sem=send_sems.at[4 * t + 3], recv_sem=recv_sems.at[N_DEV * t + my_dev],
            device_id=(x, y, layer), device_id_type=MESH), True))
    return out


def _reduce_start(dws, names, layer, tag):
    n = len(names)
    lands = [lax.empty((N_DEV,) + _shard_shape(nm), dws[0].dtype) for nm in names]

    def body(*refs):
        ins = refs[:n]
        send_sems, recv_sems = refs[2 * n], refs[2 * n + 1]
        land_out = refs[3 * n + 2:4 * n + 2]
        token = refs[-1]
        c = lax.axis_index("c")
        for cp, non_owner_only in _reduce_copies(ins, land_out, names, layer, send_sems, recv_sems):
            if non_owner_only:
                @pl.when(c != layer)
                def _():
                    cp.start()
            else:
                cp.start()
        token[...] = jnp.zeros_like(token)

    res = pl.pallas_call(
        body, name="reduce_start" + tag,
        out_shape=(pltpu.SemaphoreType.DMA((4 * n,)), pltpu.SemaphoreType.DMA((N_DEV * n,)))
        + tuple(pltpu.HBM(a.shape, a.dtype) for a in dws) + tuple(pltpu.HBM(a.shape, a.dtype) for a in lands)
        + (jax.ShapeDtypeStruct((8, 128), f32),),
        in_specs=[HBM] * (2 * n),
        out_specs=(SEM, SEM) + (HBM,) * (2 * n) + (pl.BlockSpec(memory_space=pltpu.VMEM),),
        input_output_aliases={i: i + 2 for i in range(2 * n)},
        compiler_params=pltpu.CompilerParams(has_side_effects=EFFECT),
    )(*[pltpu.with_memory_space_constraint(a, pltpu.HBM) for a in list(dws) + lands])
    return res[0], res[1], list(res[2:2 + n]), list(res[2 + n:2 + 2 * n]), res[-1]


def _reduce_wait(send_sems, recv_sems, dws, lands, names, layer, after, tag):
    n = len(names)

    def body(*refs):
        ins, land_in = refs[:n], refs[n:2 * n]
        send_ref, recv_ref = refs[2 * n], refs[2 * n + 1]
        x, y, c = _my_place()
        for cp, non_owner_only in _reduce_copies(ins, land_in, names, layer, send_ref, recv_ref):
            if non_owner_only:
                @pl.when(c != layer)
                def _():
                    cp.wait_send()
            else:
                cp.wait_send()

        @pl.when(c == layer)
        def _():
            for t in range(n):
                for k in range(1, N_DEV):
                    px, py, pc = x ^ ((k >> 2) & 1), y ^ ((k >> 1) & 1), c ^ (k & 1)
                    dev = 4 * px + 2 * py + pc
                    land = land_in[t].at[dev]
                    pltpu.make_async_remote_copy(
                        src_ref=land, dst_ref=land, send_sem=send_ref.at[4 * t], recv_sem=recv_ref.at[N_DEV * t + dev],
                        device_id=(px, py, pc), device_id_type=MESH).wait_recv()

    res = pl.pallas_call(
        body, name="reduce_wait" + tag,
        out_shape=tuple(pltpu.HBM(a.shape, a.dtype) for a in list(dws) + list(lands)),
        in_specs=[HBM] * (2 * n) + [SEM, SEM, pl.BlockSpec(memory_space=pl.ANY)], out_specs=(HBM,) * (2 * n),
        input_output_aliases={i: i for i in range(2 * n)},
        compiler_params=pltpu.CompilerParams(has_side_effects=EFFECT),
    )(*dws, *lands, send_sems, recv_sems, after)
    return list(res[:n]), list(res[n:])


def _sum_devices(land, own, name, layer, prev):
    ks, ns = _shard_shape(name)
    tr = min(ks, 256)
    shape, index = _shard_block(name, tr)

    def body(me_ref, dev_ref, *refs):
        s_ref, own_ref, out_ref = refs[0], refs[1], refs[-1]
        dev = dev_ref[0]
        acc = None
        for s in range(N_DEV):
            term = jnp.where(dev == s, own_ref[...], s_ref[s]).astype(f32)
            acc = term if acc is None else acc + term
        out_ref[...] = acc

    def mine(i, dev):
        return i * jnp.where((dev[0] & 1) == layer, 1, 0)

    in_specs = [pl.BlockSpec((N_DEV, tr, ns), lambda i, me, dev: (0, mine(i, dev), 0)),
                pl.BlockSpec(shape, lambda i, me, dev: index(mine(i, dev), me))]
    args = [land, own]
    aliases = {}
    if prev is not None:
        in_specs.append(pl.BlockSpec(memory_space=pl.ANY))
        args.append(prev)
        aliases = {4: 0}
    x, y, c = _my_place()
    return pl.pallas_call(
        body, name=f"sum_devices_{name}{layer}",
        grid_spec=pltpu.PrefetchScalarGridSpec(
            num_scalar_prefetch=2, grid=(ks // tr,), in_specs=in_specs,
            out_specs=pl.BlockSpec((None, tr, ns), lambda i, me, dev: (layer, mine(i, dev), 0))),
        out_shape=jax.ShapeDtypeStruct((2, ks, ns), f32), input_output_aliases=aliases,
        compiler_params=_cparams(("arbitrary",)),
    )(_chip_index(), jnp.reshape(4 * x + 2 * y + c, (1,)).astype(jnp.int32), *args)


class _GradReducer:
    GROUPS = (("1", 1, ("w_gate", "w_ple", "w_down", "w_up", "w_out", "w_in")),
              ("0a", 0, ("w_gate", "w_ple", "w_down", "w_up")),
              ("0b", 0, ("w_out",)),
              ("0c", 0, ("w_in",)))

    def __init__(self):
        self.grads = {}
        self.started = {}

    def add(self, name, layer, dw):
        self.grads[(name, layer)] = dw
        token = None
        for tag, glayer, names in self.GROUPS:
            if tag not in self.started and all((nm, glayer) in self.grads for nm in names):
                *self.started[tag], token = _reduce_start([self.grads[(nm, glayer)] for nm in names], names, glayer, tag)
        return token

    def finish(self, after):
        mine = {}
        for tag, layer, names in self.GROUPS:
            send, recv, dws, lands = self.started[tag]
            dws, lands = _reduce_wait(send, recv, dws, lands, names, layer, after, tag)
            for nm, dw, land in zip(names, dws, lands):
                mine[nm] = _sum_devices(land, dw, nm, layer, mine.get(nm))
        return _pair_layers(mine)


def _pair_layers(mine):
    names = list(BIG)

    def body(*refs):
        ins = refs[:len(names)]
        outs = refs[len(names):2 * len(names)]
        send_sems, recv_sems = refs[2 * len(names):]
        x, y, c = _my_place()
        sibling = (x, y, 1 - c)
        cps = []
        for t in range(len(names)):
            cp = pltpu.make_async_remote_copy(
                src_ref=ins[t].at[c], dst_ref=outs[t].at[c], send_sem=send_sems.at[t], recv_sem=recv_sems.at[t],
                device_id=sibling, device_id_type=MESH)
            cp.start()
            cps.append(cp)
        for t in range(len(names)):
            cps[t].wait_send()
            land = outs[t].at[1 - c]
            pltpu.make_async_remote_copy(
                src_ref=land, dst_ref=land, send_sem=send_sems.at[t], recv_sem=recv_sems.at[t],
                device_id=sibling, device_id_type=MESH).wait_recv()

    outs = pl.pallas_call(
        body, name="pair_layers", in_specs=[HBM] * len(names), out_specs=[HBM] * len(names),
        out_shape=[jax.ShapeDtypeStruct((2,) + _shard_shape(n), f32) for n in names],
        input_output_aliases={t: t for t in range(len(names))},
        scratch_shapes=[pltpu.SemaphoreType.DMA((len(names),)), pltpu.SemaphoreType.DMA((len(names),))],
    )(*[mine[n] for n in names])
    return dict(zip(names, outs))


SMALL_ROWS = 320


def _small_copies(vec_ref, land_ref, send_sems, recv_sems):
    x, y, c = _my_place()
    me = 4 * x + 2 * y + c
    out = []
    for k in range(1, N_DEV):
        peer = (x ^ ((k >> 2) & 1), y ^ ((k >> 1) & 1), c ^ (k & 1))
        src_dev = 4 * peer[0] + 2 * peer[1] + peer[2]
        send = pltpu.make_async_remote_copy(
            src_ref=vec_ref, dst_ref=land_ref.at[me], send_sem=send_sems.at[k - 1], recv_sem=recv_sems.at[k - 1],
            device_id=peer, device_id_type=MESH)
        arrival = pltpu.make_async_remote_copy(
            src_ref=land_ref.at[src_dev], dst_ref=land_ref.at[src_dev], send_sem=send_sems.at[k - 1],
            recv_sem=recv_sems.at[k - 1], device_id=peer, device_id_type=MESH)
        out.append((send, arrival))
    return out


def _small_start(vec):
    land = lax.empty((N_DEV,) + vec.shape, vec.dtype)

    def body(v_ref, land_in, send_sems, recv_sems, v_out, land_out):
        del land_in, v_out
        for send, _ in _small_copies(v_ref, land_out, send_sems, recv_sems):
            send.start()

    return pl.pallas_call(
        body, name="small_start",
        out_shape=(pltpu.SemaphoreType.DMA((N_DEV - 1,)), pltpu.SemaphoreType.DMA((N_DEV - 1,)),
                   pltpu.HBM(vec.shape, vec.dtype), pltpu.HBM(land.shape, land.dtype)),
        in_specs=[HBM, HBM], out_specs=(SEM, SEM, HBM, HBM), input_output_aliases={0: 2, 1: 3},
        compiler_params=pltpu.CompilerParams(has_side_effects=EFFECT),
    )(pltpu.with_memory_space_constraint(vec, pltpu.HBM), pltpu.with_memory_space_constraint(land, pltpu.HBM))


def _small_wait(send_sems, recv_sems, vec, land, after):
    def body(v_ref, land_ref, send_ref, recv_ref, after_ref, v_out, land_out):
        del after_ref, v_out, land_out
        for send, arrival in _small_copies(v_ref, land_ref, send_ref, recv_ref):
            send.wait_send()
            arrival.wait_recv()

    return pl.pallas_call(
        body, name="small_wait", out_shape=(pltpu.HBM(vec.shape, vec.dtype), pltpu.HBM(land.shape, land.dtype)),
        in_specs=[HBM, HBM, SEM, SEM, pl.BlockSpec(memory_space=pl.ANY)], out_specs=(HBM, HBM),
        input_output_aliases={0: 0, 1: 1}, compiler_params=pltpu.CompilerParams(has_side_effects=EFFECT),
    )(vec, land, send_sems, recv_sems, after)


def _small_sum(vec, land):
    x, y, c = _my_place()

    def body(dev_ref, v_ref, land_ref, out_ref):
        acc = None
        for s in range(N_DEV):
            term = jnp.where(dev_ref[0] == s, v_ref[...], land_ref[s])
            acc = term if acc is None else acc + term
        out_ref[...] = acc

    return pl.pallas_call(
        body, name="small_sum",
        grid_spec=pltpu.PrefetchScalarGridSpec(
            num_scalar_prefetch=1, grid=(1,),
            in_specs=[pl.BlockSpec(vec.shape, lambda i, dev: (0, 0)), pl.BlockSpec(land.shape, lambda i, dev: (0, 0, 0))],
            out_specs=pl.BlockSpec(vec.shape, lambda i, dev: (0, 0))),
        out_shape=jax.ShapeDtypeStruct(vec.shape, vec.dtype),
        compiler_params=_cparams(("arbitrary",)),
    )(jnp.reshape(4 * x + 2 * y + c, (1,)).astype(jnp.int32), vec, land)


def _adamw(w, g, m, v, name):
    rows, cols = w.shape
    tr = rows
    for cand in (512, 256, 128, 64, 32, 16, 8):
        if rows % cand == 0 and cand * cols * 4 <= 2 * 1024 * 1024:
            tr = cand
            break
    c1 = np.float32(1.0 - ADAM_B1 ** ADAM_STEP)
    c2 = np.float32(1.0 - ADAM_B2 ** ADAM_STEP)

    def body(w_ref, g_ref, m_ref, v_ref, go_ref, d_ref, mo_ref, vo_ref):
        gv = g_ref[...]
        go_ref[...] = gv
        mn = ADAM_B1 * m_ref[...] + (1.0 - ADAM_B1) * gv
        vn = ADAM_B2 * v_ref[...] + (1.0 - ADAM_B2) * (gv * gv)
        mo_ref[...] = mn
        vo_ref[...] = vn
        d_ref[...] = -ADAM_LR * ((mn / c1) / (jnp.sqrt(vn / c2) + ADAM_EPS) + ADAM_WD * w_ref[...])

    blk = pl.BlockSpec((tr, cols), lambda i: (i, 0))
    return pl.pallas_call(
        body, name="adamw_" + name, grid=(rows // tr,), in_specs=[blk] * 4, out_specs=[blk] * 4,
        out_shape=[jax.ShapeDtypeStruct((rows, cols), f32)] * 4,
        compiler_params=_cparams(("parallel",)),
    )(w, g, m, v)


SMALL = ("norm1", "pool_w", "pool_scale", "norm2", "norm3", "final_norm")
ORDER = ("norm1", "w_in", "pool_w", "pool_scale", "w_out", "norm2", "w_up", "w_down", "norm3", "w_gate", "w_ple",
         "final_norm")


def _pack_small(tree, extra=None):
    parts = [tree[n].reshape(-1) for n in SMALL]
    if extra is not None:
        parts.append(extra.reshape(-1))
    flat = jnp.concatenate(parts)
    return jnp.pad(flat, (0, SMALL_ROWS * 128 - flat.shape[0])).reshape(SMALL_ROWS, 128)


def _unpack_small(packed, like):
    flat = packed.reshape(-1)
    out, off = {}, 0
    for n in SMALL:
        size = int(np.prod(like[n].shape))
        out[n] = flat[off:off + size].reshape(like[n].shape)
        off += size
    return out, flat[off]


def kernel(x, p, positions, norm1, w_in, pool_w, pool_scale, w_out, norm2, w_up, w_down, norm3, w_gate, w_ple, final_norm, loss_target, m_norm1, m_w_in, m_pool_w, m_pool_scale, m_w_out, m_norm2, m_w_up, m_w_down, m_norm3, m_w_gate, m_w_ple, m_final_norm, v_norm1, v_w_in, v_pool_w, v_pool_scale, v_w_out, v_norm2, v_w_up, v_w_down, v_norm3, v_w_gate, v_w_ple, v_final_norm):
    w = dict(norm1=norm1, w_in=w_in, pool_w=pool_w, pool_scale=pool_scale, w_out=w_out, norm2=norm2, w_up=w_up,
             w_down=w_down, norm3=norm3, w_gate=w_gate, w_ple=w_ple, final_norm=final_norm)
    m = dict(norm1=m_norm1, w_in=m_w_in, pool_w=m_pool_w, pool_scale=m_pool_scale, w_out=m_w_out, norm2=m_norm2,
             w_up=m_w_up, w_down=m_w_down, norm3=m_norm3, w_gate=m_w_gate, w_ple=m_w_ple, final_norm=m_final_norm)
    v = dict(norm1=v_norm1, w_in=v_w_in, pool_w=v_pool_w, pool_scale=v_pool_scale, w_out=v_w_out, norm2=v_norm2,
             w_up=v_w_up, w_down=v_w_down, norm3=v_norm3, w_gate=v_w_gate, w_ple=v_w_ple, final_norm=v_final_norm)
    small = {n: w[n] for n in SMALL}

    wsrc = _GatheredWeights({n: w[n] for n in BIG})
    reducer = _GradReducer()
    loss8, dx, small_grads = _local_step(x[0], p.reshape(2, x.shape[1], PLE_DIM), positions[0], wsrc, small, loss_target[0], reducer)
    s_send, s_recv, s_vec, s_land = _small_start(_pack_small(small_grads, loss8[0, 0]))
    gsh = reducer.finish(s_vec)

    g_out, d_out, m_out, v_out = {}, {}, {}, {}
    for n in BIG:
        shp = w[n].shape
        two = lambda a: a.reshape(shp[0] * shp[1], shp[2])
        g2, d2, m2, v2 = _adamw(two(w[n]), two(gsh[n]), two(m[n]), two(v[n]), n)
        g_out[n], d_out[n], m_out[n], v_out[n] = g2.reshape(shp), d2.reshape(shp), m2.reshape(shp), v2.reshape(shp)
    red = _small_sum(*_small_wait(s_send, s_recv, s_vec, s_land, d2))
    g_small, loss = _unpack_small(red, small)
    _, d2, m2, v2 = _adamw(_pack_small(small), red, _pack_small({n: m[n] for n in SMALL}),
                           _pack_small({n: v[n] for n in SMALL}), "small")
    for tree, packed in ((d_out, d2), (m_out, m2), (v_out, v2)):
        tree.update(_unpack_small(packed, small)[0])
    g_out.update(g_small)

    return (loss, dx[None], *[g_out[n] for n in ORDER], *[d_out[n] for n in ORDER], *[m_out[n] for n in ORDER],
            *[v_out[n] for n in ORDER])
```

```python
import jax
import jax.numpy as jnp
import numpy as np
from jax import lax
from jax.experimental import pallas as pl
from jax.experimental.pallas import tpu as pltpu

f32 = jnp.float32
MXU_DTYPE = jnp.bfloat16
COMM_DTYPE = jnp.bfloat16

D_MODEL = 1024
POOL_WIDTH = 256
POOL_GC = 64
ATTN_WIDTH = 768
HEAD_DIM = 64
N_IN = POOL_WIDTH + 3 * ATTN_WIDTH
D_FF = 4096
PLE_DIM = 256
BLK = 128
DILATIONS = (1, 4, 16)
ROT_DIM = 16
ROPE_THETA = 500000.0
EPS = 1e-6
ATTN_SCALE = HEAD_DIM ** -0.5
NEG_BIG = -1e30

ADAM_LR, ADAM_B1, ADAM_B2, ADAM_EPS, ADAM_WD, ADAM_STEP = 0.001, 0.9, 0.999, 1e-08, 0.01, 10

TM = 512
TM_WGRAD = 1024
HALO = 16
VMEM_LIMIT = 48 * 1024 * 1024
VMEM_LIMIT_LARGE = 58 * 1024 * 1024
VMEM_COMPILER_RESERVE = 6 * 1024 * 1024
N_CHIPS = 4
MESH = pl.DeviceIdType.MESH

BIG = ("w_in", "w_out", "w_up", "w_down", "w_gate", "w_ple")
FULL_SHAPE = {"w_in": (D_MODEL, N_IN), "w_out": (D_MODEL, D_MODEL), "w_up": (D_MODEL, D_FF),
              "w_down": (D_FF, D_MODEL), "w_gate": (D_MODEL, D_MODEL), "w_ple": (PLE_DIM, D_MODEL)}
COL_SHARDED = {"w_in": True, "w_out": False, "w_up": True, "w_down": False, "w_gate": False, "w_ple": True}


def _shard_shape(name):
    k, n = FULL_SHAPE[name]
    return (k, n // N_CHIPS) if COL_SHARDED[name] else (k // N_CHIPS, n)


def _cparams(sem=None, vmem=VMEM_LIMIT):
    return pltpu.CompilerParams(dimension_semantics=sem, vmem_limit_bytes=vmem)


def _resident(block_shape, index_map):
    return pl.BlockSpec(block_shape, index_map, pipeline_mode=pl.Buffered(1))


def _mx(x):
    return x.astype(MXU_DTYPE)


def _dot(a, b):
    return jnp.dot(a, b, preferred_element_type=f32)


def _dot_nt(a, b):
    return lax.dot_general(a, b, (((1,), (1,)), ((), ())), preferred_element_type=f32)


def _dot_tn(a, b):
    return lax.dot_general(a, b, (((0,), (0,)), ((), ())), preferred_element_type=f32)


def _sigmoid(x):
    return 1.0 / (1.0 + jnp.exp(-x))


def _rope_apply(y, c, s1, s2, width):
    return y * c + pltpu.roll(y, width - 8, axis=1) * s1 + pltpu.roll(y, 8, axis=1) * s2


def _rope_transpose(dy, c, s1, s2, width):
    return dy * c + pltpu.roll(dy * s1, 8, axis=1) + pltpu.roll(dy * s2, width - 8, axis=1)


def _norm_matmul(h, g, w, layer, tn, name, rope=None):
    s_len, d = h.shape
    n = w.shape[2]

    def body(*refs):
        if rope is None:
            h_ref, g_ref, w_ref, y_ref, hn_ref = refs
        else:
            h_ref, g_ref, w_ref, c_ref, s1_ref, s2_ref, y_ref, hn_ref = refs
            reps = tn // 128
            c = jnp.concatenate([c_ref[...]] * reps, axis=1)
            s1 = jnp.concatenate([s1_ref[...]] * reps, axis=1)
            s2 = jnp.concatenate([s2_ref[...]] * reps, axis=1)
        x = h_ref[...]
        r = lax.rsqrt(jnp.mean(x * x, axis=-1, keepdims=True) + EPS)
        hn = ((x * r) * g_ref[...]).astype(hn_ref.dtype)
        hn_ref[...] = hn
        for j in range(n // tn):
            y = _dot(hn, w_ref[:, j * tn:(j + 1) * tn])
            if rope is not None and POOL_WIDTH <= j * tn < POOL_WIDTH + 2 * ATTN_WIDTH:
                y = _rope_apply(y, c, s1, s2, tn)
            y_ref[:, j * tn:(j + 1) * tn] = y

    in_specs = [pl.BlockSpec((TM, d), lambda i: (i, 0)),
                pl.BlockSpec((1, d), lambda i: (0, 0)),
                _resident((None, d, n), lambda i: (layer, 0, 0))]
    args = [h, g, w]
    if rope is not None:
        assert POOL_WIDTH % tn == 0 and (2 * ATTN_WIDTH) % tn == 0
        in_specs += [pl.BlockSpec((TM, 128), lambda i: (i, 0))] * 3
        args += list(rope)
    return pl.pallas_call(
        body, name=name, grid=(s_len // TM,), in_specs=in_specs,
        out_specs=[pl.BlockSpec((TM, n), lambda i: (i, 0)), pl.BlockSpec((TM, d), lambda i: (i, 0))],
        out_shape=[jax.ShapeDtypeStruct((s_len, n), f32), jax.ShapeDtypeStruct((s_len, d), MXU_DTYPE)],
        compiler_params=_cparams(("parallel",)),
    )(*args)


def _gate_ple_fwd(h2, g, w_gate, w_ple, layer, p, p_layer, name, head=None):
    s_len, d = h2.shape

    def body(h_ref, g_ref, wg_ref, p_ref, wp_ref, *rest):
        gl_ref, hn_ref = rest[-2:]
        x = h_ref[...]
        r = lax.rsqrt(jnp.mean(x * x, axis=-1, keepdims=True) + EPS)
        hn = ((x * r) * g_ref[...]).astype(hn_ref.dtype)
        hn_ref[...] = hn
        gl = _dot(hn, wg_ref[...])
        gl_ref[...] = gl.astype(gl_ref.dtype)
        h3 = x + _sigmoid(gl) * _dot(_mx(p_ref[...]), wp_ref[...])
        if head is None:
            rest[0][...] = h3
            return
        gf_ref, t_ref, loss_ref, dh_ref, dgf_ref = rest[:5]
        i = pl.program_id(0)
        gv = gf_ref[...]
        r3 = lax.rsqrt(jnp.mean(h3 * h3, axis=-1, keepdims=True) + EPS)
        xh = h3 * r3
        diff = xh * gv - t_ref[...]
        part = 0.5 * jnp.sum(jnp.mean(diff * diff, axis=-1, keepdims=True), axis=0, keepdims=True)
        dy = diff * (1.0 / d)
        dxh = dy * gv
        dh_ref[...] = r3 * (dxh - xh * jnp.mean(dxh * xh, axis=-1, keepdims=True))
        dgsum = jnp.sum(dy * xh, axis=0, keepdims=True)
        lossb = jnp.broadcast_to(part, (8, 128))

        @pl.when(i == 0)
        def _():
            loss_ref[...] = lossb
            dgf_ref[...] = dgsum

        @pl.when(i > 0)
        def _():
            loss_ref[...] += lossb
            dgf_ref[...] += dgsum

    row = lambda i: (i, 0)
    one = lambda i: (0, 0)
    in_specs = [pl.BlockSpec((TM, d), row), pl.BlockSpec((1, d), one),
                pl.BlockSpec((None, d, d), lambda i: (layer, 0, 0)),
                pl.BlockSpec((None, TM, PLE_DIM), lambda i: (p_layer, i, 0)),
                pl.BlockSpec((None, PLE_DIM, d), lambda i: (layer, 0, 0))]
    args = [h2, g, w_gate, p, w_ple]
    saved = [jax.ShapeDtypeStruct((s_len, d), MXU_DTYPE)] * 2
    if head is None:
        out_specs = [pl.BlockSpec((TM, d), row)] * 3
        out_shape = [jax.ShapeDtypeStruct((s_len, d), f32)] + saved
    else:
        in_specs += [pl.BlockSpec((1, d), one), pl.BlockSpec((TM, d), row)]
        args += list(head)
        out_specs = [pl.BlockSpec((8, 128), one), pl.BlockSpec((TM, d), row), pl.BlockSpec((1, d), one)] \
            + [pl.BlockSpec((TM, d), row)] * 2
        out_shape = [jax.ShapeDtypeStruct((8, 128), f32), jax.ShapeDtypeStruct((s_len, d), f32),
                     jax.ShapeDtypeStruct((1, d), f32)] + saved
    return pl.pallas_call(
        body, name=name, grid=(s_len // TM,), in_specs=in_specs, out_specs=out_specs, out_shape=out_shape,
        compiler_params=_cparams(("arbitrary",)),
    )(*args)


def _gate_bwd(dh3, gl, p, p_layer, w_ple, w_gate, layer, h2, g, name):
    s_len, d = dh3.shape

    def body(dh_ref, gl_ref, p_ref, wp_ref, wg_ref, h_ref, g_ref, dh2_ref, dg_ref, de_ref, dgl_ref):
        i = pl.program_id(0)
        dh = dh_ref[...]
        gate = _sigmoid(gl_ref[...].astype(f32))
        e = _dot(_mx(p_ref[...]), wp_ref[...])
        de_ref[...] = (dh * gate).astype(de_ref.dtype)
        dgl = ((dh * e) * (gate * (1.0 - gate))).astype(dgl_ref.dtype)
        dgl_ref[...] = dgl
        dx, dgrow = _rmsnorm_bwd(_dot_nt(dgl, wg_ref[...]), h_ref[...], g_ref[...])
        dh2_ref[...] = dh + dx
        dgsum = jnp.sum(dgrow, axis=0, keepdims=True)

        @pl.when(i == 0)
        def _():
            dg_ref[...] = dgsum

        @pl.when(i > 0)
        def _():
            dg_ref[...] += dgsum

    row = lambda i: (i, 0)
    blk = pl.BlockSpec((TM, d), row)
    return pl.pallas_call(
        body, name=name, grid=(s_len // TM,),
        in_specs=[blk, blk, pl.BlockSpec((None, TM, PLE_DIM), lambda i: (p_layer, i, 0)),
                  _resident((None, PLE_DIM, d), lambda i: (layer, 0, 0)),
                  _resident((None, d, d), lambda i: (layer, 0, 0)), blk, pl.BlockSpec((1, d), lambda i: (0, 0))],
        out_specs=[blk, pl.BlockSpec((1, d), lambda i: (0, 0)), blk, blk],
        out_shape=[jax.ShapeDtypeStruct((s_len, d), f32), jax.ShapeDtypeStruct((1, d), f32),
                   jax.ShapeDtypeStruct((s_len, d), MXU_DTYPE), jax.ShapeDtypeStruct((s_len, d), MXU_DTYPE)],
        compiler_params=_cparams(("arbitrary",)),
    )(dh3, gl, p, w_ple, w_gate, h2, g)


def _rmsnorm_bwd(dhn, x, g):
    r = lax.rsqrt(jnp.mean(x * x, axis=-1, keepdims=True) + EPS)
    xh = x * r
    dxh = dhn * g
    dx = r * (dxh - xh * jnp.mean(dxh * xh, axis=-1, keepdims=True))
    return dx, dhn * xh


def _matmul_nt_norm_bwd(dy, w, layer, h_prev, g, dres, name, tk=1024, after=None):
    s_len, k_dim = dy.shape
    d = h_prev.shape[1]

    def body(dy_ref, w_ref, h_ref, g_ref, dres_ref, *rest):
        dh_ref, dg_ref = rest[-2:]
        i = pl.program_id(0)
        acc = None
        for k in range(k_dim // tk):
            part = _dot_nt(_mx(dy_ref[:, k * tk:(k + 1) * tk]), w_ref[:, k * tk:(k + 1) * tk])
            acc = part if acc is None else acc + part
        dx, dgrow = _rmsnorm_bwd(acc, h_ref[...], g_ref[...])
        dh_ref[...] = dres_ref[...] + dx
        dgsum = jnp.sum(dgrow, axis=0, keepdims=True)

        @pl.when(i == 0)
        def _():
            dg_ref[...] = dgsum

        @pl.when(i > 0)
        def _():
            dg_ref[...] += dgsum

    in_specs = [pl.BlockSpec((TM, k_dim), lambda i: (i, 0)),
                _resident((None, d, k_dim), lambda i: (layer, 0, 0)),
                pl.BlockSpec((TM, d), lambda i: (i, 0)),
                pl.BlockSpec((1, d), lambda i: (0, 0)),
                pl.BlockSpec((TM, d), lambda i: (i, 0))]
    args = [dy, w, h_prev, g, dres]
    if after is not None:
        in_specs.append(pl.BlockSpec(memory_space=pl.ANY))
        args.append(after)
    return pl.pallas_call(
        body, name=name, grid=(s_len // TM,), in_specs=in_specs,
        out_specs=[pl.BlockSpec((TM, d), lambda i: (i, 0)), pl.BlockSpec((1, d), lambda i: (0, 0))],
        out_shape=[jax.ShapeDtypeStruct((s_len, d), f32), jax.ShapeDtypeStruct((1, d), f32)],
        compiler_params=_cparams(("arbitrary",)),
    )(*args)


def _mlp_fwd(h1, g, w_up, w_down, layer, name, tf=1024):
    s_len, d = h1.shape
    ff = w_up.shape[2]

    def body(h_ref, g_ref, wu_ref, wd_ref, h2_ref, a_ref, hn_ref):
        x = h_ref[...]
        r = lax.rsqrt(jnp.mean(x * x, axis=-1, keepdims=True) + EPS)
        hn = ((x * r) * g_ref[...]).astype(hn_ref.dtype)
        hn_ref[...] = hn
        acc = x
        for j in range(ff // tf):
            a = _dot(hn, wu_ref[:, j * tf:(j + 1) * tf])
            a_ref[:, j * tf:(j + 1) * tf] = a.astype(a_ref.dtype)
            relu = jnp.maximum(a, 0.0)
            acc = acc + _dot(_mx(relu * relu), wd_ref[j * tf:(j + 1) * tf, :])
        h2_ref[...] = acc

    row = lambda i: (i, 0)
    return pl.pallas_call(
        body, name=name, grid=(s_len // TM,),
        in_specs=[pl.BlockSpec((TM, d), row), pl.BlockSpec((1, d), lambda i: (0, 0)),
                  _resident((None, d, ff), lambda i: (layer, 0, 0)), _resident((None, ff, d), lambda i: (layer, 0, 0))],
        out_specs=[pl.BlockSpec((TM, d), row), pl.BlockSpec((TM, ff), row), pl.BlockSpec((TM, d), row)],
        out_shape=[jax.ShapeDtypeStruct((s_len, d), f32), jax.ShapeDtypeStruct((s_len, ff), MXU_DTYPE),
                   jax.ShapeDtypeStruct((s_len, d), MXU_DTYPE)],
        compiler_params=_cparams(("parallel",)),
    )(h1, g, w_up, w_down)


def _mlp_bwd(dh2, w_down, w_up, layer, a, h1, g, name, tf=1024):
    s_len, d = dh2.shape
    ff = a.shape[1]

    def body(dh_ref, wd_ref, wu_ref, a_ref, h_ref, g_ref, dh1_ref, dg_ref, da_ref):
        i = pl.program_id(0)
        dh = dh_ref[...]
        dhb = _mx(dh)
        acc = None
        for j in range(ff // tf):
            cols = slice(j * tf, (j + 1) * tf)
            dact = _dot_nt(dhb, wd_ref[cols, :])
            da = (dact * (2.0 * jnp.maximum(a_ref[:, cols].astype(f32), 0.0))).astype(da_ref.dtype)
            da_ref[:, cols] = da
            part = _dot_nt(da, wu_ref[:, cols])
            acc = part if acc is None else acc + part
        dx, dgrow = _rmsnorm_bwd(acc, h_ref[...], g_ref[...])
        dh1_ref[...] = dh + dx
        dgsum = jnp.sum(dgrow, axis=0, keepdims=True)

        @pl.when(i == 0)
        def _():
            dg_ref[...] = dgsum

        @pl.when(i > 0)
        def _():
            dg_ref[...] += dgsum

    row = lambda i: (i, 0)
    return pl.pallas_call(
        body, name=name, grid=(s_len // TM,),
        in_specs=[pl.BlockSpec((TM, d), row), _resident((None, ff, d), lambda i: (layer, 0, 0)),
                  _resident((None, d, ff), lambda i: (layer, 0, 0)), pl.BlockSpec((TM, ff), row),
                  pl.BlockSpec((TM, d), row), pl.BlockSpec((1, d), lambda i: (0, 0))],
        out_specs=[pl.BlockSpec((TM, d), row), pl.BlockSpec((1, d), lambda i: (0, 0)), pl.BlockSpec((TM, ff), row)],
        out_shape=[jax.ShapeDtypeStruct((s_len, d), f32), jax.ShapeDtypeStruct((1, d), f32),
                   jax.ShapeDtypeStruct((s_len, ff), MXU_DTYPE)],
        compiler_params=_cparams(("arbitrary",), vmem=VMEM_LIMIT_LARGE),
    )(dh2, w_down, w_up, a, h1, g)


def _weight_grad(a, b, name, act=False, lead=None):
    s_len, k_dim = a.shape[-2:]
    n = b.shape[1]
    tka = min(k_dim, 2048)
    tnb = n if n <= 1024 else (2048 if n % 2048 == 0 else 640)
    tm = 2 * TM_WGRAD
    vmem = 2 * tm * (tka * a.dtype.itemsize + tnb * b.dtype.itemsize) + tka * tnb * (4 + 2 * jnp.dtype(COMM_DTYPE).itemsize)
    if vmem > VMEM_LIMIT - VMEM_COMPILER_RESERVE or s_len % tm:
        tm = TM_WGRAD
    ns = s_len // tm

    def body(a_ref, b_ref, o_ref, acc_ref):
        s = pl.program_id(2)
        x = a_ref[...]
        if act:
            relu = jnp.maximum(x, 0.0)
            x = relu * relu

        @pl.when(s == 0)
        def _():
            acc_ref[...] = jnp.zeros_like(acc_ref)

        acc_ref[...] += _dot_tn(_mx(x), _mx(b_ref[...]))

        @pl.when(s == ns - 1)
        def _():
            o_ref[...] = acc_ref[...].astype(o_ref.dtype)

    if lead is None:
        a_spec = pl.BlockSpec((tm, tka), lambda i, j, s: (s, i))
    else:
        a_spec = pl.BlockSpec((None, tm, tka), lambda i, j, s: (lead, s, i))
    return pl.pallas_call(
        body, name=name, grid=(k_dim // tka, n // tnb, ns),
        in_specs=[a_spec, pl.BlockSpec((tm, tnb), lambda i, j, s: (s, j))],
        out_specs=pl.BlockSpec((tka, tnb), lambda i, j, s: (i, j)),
        out_shape=jax.ShapeDtypeStruct((k_dim, n), COMM_DTYPE),
        scratch_shapes=[pltpu.VMEM((tka, tnb), f32)],
        compiler_params=_cparams(("parallel", "parallel", "arbitrary")),
    )(a, b)


def _group_select(lane, x2, x4, x8, x16):
    grp = lane // POOL_GC
    return jnp.where(grp == 0, x2, jnp.where(grp == 1, x4, jnp.where(grp == 2, x8, x16)))


def _pool_window(lane):
    grp = lane // POOL_GC
    return jnp.where(grp == 0, 2, jnp.where(grp == 1, 4, jnp.where(grp == 2, 8, 16)))


def _pool_y(u, halo, i):
    xs = jnp.concatenate([jnp.where(i > 0, halo, 0.0), u], axis=0)
    s2 = xs + pltpu.roll(xs, 1, axis=0)
    s4 = s2 + pltpu.roll(s2, 2, axis=0)
    s8 = s4 + pltpu.roll(s4, 4, axis=0)
    s16 = s8 + pltpu.roll(s8, 8, axis=0)
    lane = lax.broadcasted_iota(jnp.int32, xs.shape, 1)
    sel = _group_select(lane, s2, s4, s8, s16)[HALO:, :]
    t = i * TM + lax.broadcasted_iota(jnp.int32, u.shape, 0)
    cnt = jnp.minimum(_pool_window(lax.broadcasted_iota(jnp.int32, u.shape, 1)), t + 1).astype(f32)
    return sel / cnt - u


def _group_weights(l0, l1, l2):
    mx = jnp.maximum(jnp.maximum(l0, l1), l2)
    e0, e1, e2 = jnp.exp(l0 - mx), jnp.exp(l1 - mx), jnp.exp(l2 - mx)
    den = e0 + e1 + e2
    return e0 / den, e1 / den, e2 / den


def _mixer_out_proj(z, wbd, scale, outs, lses, w_out, layer, h, name):
    s_len, d = h.shape

    def body(u_ref, halo_ref, wbd_ref, sc_ref, o0, o1, o2, l0, l1, l2, wo_ref, h_ref, m_ref, h1_ref):
        i = pl.program_id(0)
        y = _pool_y(u_ref[...], halo_ref[...], i)
        pool = _dot(_mx(y), wbd_ref[...]) * sc_ref[...]
        w0, w1, w2 = _group_weights(l0[...], l1[...], l2[...])
        m = jnp.concatenate([pool, o0[...] * w0, o1[...] * w1, o2[...] * w2], axis=1).astype(m_ref.dtype)
        m_ref[...] = m
        h1_ref[...] = h_ref[...] + _dot(m, wo_ref[...])

    row = lambda i: (i, 0)
    blk = pl.BlockSpec((TM, 256), row)
    grp = [pl.BlockSpec((TM, 256), lambda i, g=g: (i, g)) for g in range(3)]
    return pl.pallas_call(
        body, name=name, grid=(s_len // TM,),
        in_specs=[blk, pl.BlockSpec((HALO, 256), lambda i: (jnp.maximum(i * (TM // HALO) - 1, 0), 0)),
                  pl.BlockSpec((256, 256), lambda i: (0, 0)), pl.BlockSpec((1, 256), lambda i: (0, 0))] + grp + grp
        + [_resident((None, d, d), lambda i: (layer, 0, 0)), pl.BlockSpec((TM, d), row)],
        out_specs=[pl.BlockSpec((TM, d), row)] * 2,
        out_shape=[jax.ShapeDtypeStruct((s_len, d), MXU_DTYPE), jax.ShapeDtypeStruct((s_len, d), f32)],
        compiler_params=_cparams(("parallel",)),
    )(z, z, wbd, scale, outs, outs, outs, lses, lses, lses, w_out, h)


def _head_sums(x):
    r = lax.broadcasted_iota(jnp.int32, (256, 256), 0) // HEAD_DIM
    c = lax.broadcasted_iota(jnp.int32, (256, 256), 1) // HEAD_DIM
    ones = jnp.where(r == c, 1.0, 0.0).astype(jnp.bfloat16)
    hi = x.astype(jnp.bfloat16)
    lo = (x - hi.astype(f32)).astype(jnp.bfloat16)
    return _dot(hi, ones) + _dot(lo, ones)


def _out_combine_bwd(dh1, w_out, layer, outs, lses, name, after=None):
    s_len, d = dh1.shape

    def body(dh_ref, w_ref, o0, o1, o2, l0, l1, l2, *rest):
        dp_ref, do_ref, dl_ref = rest[-3:]
        dm = _dot_nt(_mx(dh_ref[...]), w_ref[...])
        dp_ref[...] = dm[:, :POOL_WIDTH]
        w = _group_weights(l0[...], l1[...], l2[...])
        da = [dm[:, POOL_WIDTH + 256 * g:POOL_WIDTH + 256 * (g + 1)] for g in range(3)]
        o = (o0[...], o1[...], o2[...])
        dw = [_head_sums(da[g] * o[g]) for g in range(3)]
        t = w[0] * dw[0] + w[1] * dw[1] + w[2] * dw[2]
        do_ref[...] = jnp.concatenate([da[g] * w[g] for g in range(3)], axis=1)
        dl_ref[...] = jnp.concatenate([w[g] * t for g in range(3)], axis=1)

    grp = [pl.BlockSpec((TM, 256), lambda i, g=g: (i, g)) for g in range(3)]
    in_specs = [pl.BlockSpec((TM, d), lambda i: (i, 0)), _resident((None, d, d), lambda i: (layer, 0, 0))] + grp + grp
    args = [dh1, w_out, outs, outs, outs, lses, lses, lses]
    if after is not None:
        in_specs.append(pl.BlockSpec(memory_space=pl.ANY))
        args.append(after)
    return pl.pallas_call(
        body, name=name, grid=(s_len // TM,), in_specs=in_specs,
        out_specs=[pl.BlockSpec((TM, POOL_WIDTH), lambda i: (i, 0))] + [pl.BlockSpec((TM, ATTN_WIDTH), lambda i: (i, 0))] * 2,
        out_shape=[jax.ShapeDtypeStruct((s_len, POOL_WIDTH), f32)] + [jax.ShapeDtypeStruct((s_len, ATTN_WIDTH), f32)] * 2,
        compiler_params=_cparams(("parallel",)),
    )(*args)


def _pool_bwd(z, dm, wbd, scale, name, after=None):
    s_len = z.shape[0]
    n_halo = s_len // HALO

    def body(u_ref, uh_ref, d_ref, dh_ref, wbd_ref, sc_ref, *rest):
        du_ref, dw_ref, dsc_ref = rest[-3:]
        i = pl.program_id(0)
        last = pl.num_programs(0) - 1
        y = _pool_y(u_ref[...], uh_ref[...], i)
        yb = _mx(y)
        dpo = d_ref[...]
        sc = sc_ref[...]
        dsc = jnp.sum(dpo * _dot(yb, wbd_ref[...]), axis=0, keepdims=True)
        dwp = _dot_tn(yb, _mx(dpo * sc))

        @pl.when(i == 0)
        def _():
            dsc_ref[...] = dsc
            dw_ref[...] = dwp

        @pl.when(i > 0)
        def _():
            dsc_ref[...] += dsc
            dw_ref[...] += dwp

        ext = jnp.concatenate([dpo, jnp.where(i < last, dh_ref[...], 0.0)], axis=0)
        dy = _dot_nt(_mx(ext * sc), wbd_ref[...])
        t = i * TM + lax.broadcasted_iota(jnp.int32, ext.shape, 0)
        lane = lax.broadcasted_iota(jnp.int32, ext.shape, 1)
        e = dy / jnp.minimum(_pool_window(lane), t + 1).astype(f32)
        rows = ext.shape[0]
        f2 = e + pltpu.roll(e, rows - 1, axis=0)
        f4 = f2 + pltpu.roll(f2, rows - 2, axis=0)
        f8 = f4 + pltpu.roll(f4, rows - 4, axis=0)
        f16 = f8 + pltpu.roll(f8, rows - 8, axis=0)
        du_ref[...] = (_group_select(lane, f2, f4, f8, f16) - dy)[:TM, :].astype(du_ref.dtype)

    row = lambda i: (i, 0)
    blk = pl.BlockSpec((TM, 256), row)
    extra = [] if after is None else [after]
    return pl.pallas_call(
        body, name=name, grid=(s_len // TM,),
        in_specs=[blk, pl.BlockSpec((HALO, 256), lambda i: (jnp.maximum(i * (TM // HALO) - 1, 0), 0)),
                  blk, pl.BlockSpec((HALO, 256), lambda i: (jnp.minimum((i + 1) * (TM // HALO), n_halo - 1), 0)),
                  pl.BlockSpec((256, 256), lambda i: (0, 0)), pl.BlockSpec((1, 256), lambda i: (0, 0))]
        + [pl.BlockSpec(memory_space=pl.ANY)] * len(extra),
        out_specs=[blk, pl.BlockSpec((256, 256), lambda i: (0, 0)), pl.BlockSpec((1, 256), lambda i: (0, 0))],
        out_shape=[jax.ShapeDtypeStruct((s_len, N_IN), MXU_DTYPE), jax.ShapeDtypeStruct((256, 256), f32),
                   jax.ShapeDtypeStruct((1, 256), f32)],
        compiler_params=_cparams(("arbitrary",)),
    )(z, z, dm, dm, wbd, scale, *extra)


def _tri_masks():
    qi = lax.broadcasted_iota(jnp.int32, (BLK, BLK), 0)
    ki = lax.broadcasted_iota(jnp.int32, (BLK, BLK), 1)
    return qi >= ki, ki >= qi


ATTN_SUPER_PER_STEP = (8, 4, 1)
Q_COL, K_COL, V_COL = POOL_WIDTH // 128, (POOL_WIDTH + ATTN_WIDTH) // 128, (POOL_WIDTH + 2 * ATTN_WIDTH) // 128


def _rows(ref, start, dil):
    if dil == 1:
        return ref[pl.ds(start, BLK), :]
    return ref[pl.ds(start, BLK, stride=dil), :]


ATTN_BLOCKS_TOGETHER = 8


def _set_rows(ref, start, dil, val):
    if dil == 1:
        ref[pl.ds(start, BLK), :] = val
    else:
        ref[pl.ds(start, BLK, stride=dil), :] = val


def _attn_fwd(z, g, prev, name):
    s_len = z.shape[0]
    dil, m = DILATIONS[g], ATTN_SUPER_PER_STEP[g]
    sbr = BLK * dil
    rows = sbr * m

    def body(*refs):
        q_ref, kc_ref, kp_ref, vc_ref, vp_ref = refs[:5]
        o_ref, l_ref = refs[-2:]
        st = pl.program_id(0)
        low, up = _tri_masks()
        head0 = lax.broadcasted_iota(jnp.int32, (BLK, 128), 1) < HEAD_DIM
        blocks = [(sb, r) for sb in range(m) for r in range(dil)]
        for g0 in range(0, len(blocks), ATTN_BLOCKS_TOGETHER):
            grp = blocks[g0:g0 + ATTN_BLOCKS_TOGETHER]
            loaded = []
            for sb, r in grp:
                base = sb * sbr + r
                if sb == 0:
                    kp, vp = _rows(kp_ref, r, dil), _rows(vp_ref, r, dil)
                else:
                    kp, vp = _rows(kc_ref, base - sbr, dil), _rows(vc_ref, base - sbr, dil)
                qs = _rows(q_ref, base, dil) * ATTN_SCALE
                loaded.append((_mx(jnp.where(head0, qs, 0.0)), _mx(jnp.where(head0, 0.0, qs)),
                               jnp.concatenate([_mx(kp), _mx(_rows(kc_ref, base, dil))], axis=0),
                               jnp.concatenate([_mx(vp), _mx(_rows(vc_ref, base, dil))], axis=0)))
            scores = [(_dot_nt(q0, k2), _dot_nt(q1, k2)) for q0, q1, k2, _ in loaded]
            soft = []
            for (sb, _), pair in zip(grp, scores):
                valid = jnp.concatenate([up & (st > 0) if sb == 0 else up, low], axis=1)
                heads = []
                for s in pair:
                    s = jnp.where(valid, s, NEG_BIG)
                    mx = jnp.max(s, axis=-1, keepdims=True)
                    e = jnp.exp(s - mx)
                    l = jnp.sum(e, axis=-1, keepdims=True)
                    heads.append((_mx(e), 1.0 / l, jnp.broadcast_to(mx + jnp.log(l), (BLK, 128))))
                soft.append(heads)
            for (sb, r), heads, (_, _, _, v2) in zip(grp, soft, loaded):
                base = sb * sbr + r
                _set_rows(o_ref, base, dil, jnp.where(head0, _dot(heads[0][0], v2) * heads[0][1],
                                                      _dot(heads[1][0], v2) * heads[1][1]))
                _set_rows(l_ref, base, dil, jnp.where(head0, heads[0][2], heads[1][2]))

    def cur(col):
        return pl.BlockSpec((rows, 128), lambda st, hp: (st, col + 2 * g + hp))

    def before(col):
        return pl.BlockSpec((sbr, 128), lambda st, hp: (jnp.maximum(st * m - 1, 0), col + 2 * g + hp))

    in_specs = [cur(Q_COL), cur(K_COL), before(K_COL), cur(V_COL), before(V_COL)]
    args = [z, z, z, z, z]
    aliases = {}
    if prev is not None:
        in_specs += [pl.BlockSpec(memory_space=pl.ANY)] * 2
        args += list(prev)
        aliases = {5: 0, 6: 1}
    return pl.pallas_call(
        body, name=name, grid=(s_len // rows, 2), in_specs=in_specs, out_specs=[cur(0), cur(0)],
        out_shape=[jax.ShapeDtypeStruct((s_len, ATTN_WIDTH), f32)] * 2, input_output_aliases=aliases,
        compiler_params=_cparams(("parallel", "parallel")),
    )(*args)


def _stack_heads(x, head0):
    return jnp.concatenate([_mx(jnp.where(head0, x, 0.0)), _mx(jnp.where(head0, 0.0, x))], axis=0)


def _head_rows(x):
    xt = x.T
    return jnp.concatenate([jnp.broadcast_to(xt[0:1, :], (BLK, BLK)),
                            jnp.broadcast_to(xt[HEAD_DIM:HEAD_DIM + 1, :], (BLK, BLK))], axis=0)


def _attn_bwd(z, do, lse, dlt, tabs, dz, g, name):
    s_len = z.shape[0]
    dil, m = DILATIONS[g], ATTN_SUPER_PER_STEP[g]
    sbr = BLK * dil
    rows = sbr * m
    nsteps = s_len // rows

    def body(q_ref, qn_ref, kc_ref, kp_ref, vc_ref, vp_ref, do_ref, don_ref, l_ref, ln_ref, d_ref, dn_ref,
             c_ref, s1_ref, s2_ref, dz_in, dz_ref, dq_buf, dk_buf, dv_buf, out_buf, sems):
        del dz_in
        st, hp = pl.program_id(0), pl.program_id(1)
        head0 = lax.broadcasted_iota(jnp.int32, (BLK, 128), 1) < HEAD_DIM
        key_i = lax.broadcasted_iota(jnp.int32, (2 * BLK, BLK), 0) & (BLK - 1)
        query_i = lax.broadcasted_iota(jnp.int32, (2 * BLK, BLK), 1)
        same_t, cross_t = query_i >= key_i, key_i >= query_i
        def load(r):
            keys, vals = [_stack_heads(_rows(kp_ref, r, dil), head0)], [_stack_heads(_rows(vp_ref, r, dil), head0)]
            qs, dos, lses, dlts = [], [], [], []
            for sb in range(m):
                base = sb * sbr + r
                keys.append(_stack_heads(_rows(kc_ref, base, dil), head0))
                vals.append(_stack_heads(_rows(vc_ref, base, dil), head0))
                qs.append(_mx(_rows(q_ref, base, dil)))
                dos.append(_mx(_rows(do_ref, base, dil)))
                lses.append(_head_rows(_rows(l_ref, base, dil)))
                dlts.append(_head_rows(_rows(d_ref, base, dil)))
            qs.append(_mx(_rows(qn_ref, r, dil)))
            dos.append(_mx(_rows(don_ref, r, dil)))
            lses.append(_head_rows(_rows(ln_ref, r, dil)))
            dlts.append(_head_rows(_rows(dn_ref, r, dil)))
            return keys, vals, qs, dos, lses, dlts

        def products(data):
            keys, vals, qs, dos, _, _ = data
            return ([(_dot_nt(keys[j + 1], qs[j]), _dot_nt(vals[j + 1], dos[j])) for j in range(m)],
                    [(_dot_nt(keys[j], qs[j]), _dot_nt(vals[j], dos[j])) for j in range(m + 1)])

        def finish(data, raw):
            lses, dlts = data[4], data[5]

            def one(pair, j, valid):
                p = jnp.where(valid, jnp.exp(pair[0] * ATTN_SCALE - lses[j]), 0.0)
                return _mx(p), _mx(p * (pair[1] - dlts[j]) * ATTN_SCALE)

            same = [one(raw[0][j], j, same_t) for j in range(m)]
            cross = [one(raw[1][j], j, cross_t & (st > 0) if j == 0 else
                         (cross_t & (st < nsteps - 1) if j == m else cross_t)) for j in range(m + 1)]
            return same, cross

        def gradients(r, data, fin):
            keys, _, qs, dos, _, _ = data
            same, cross = fin
            for sb in range(m):
                base = sb * sbr + r
                (p_a, ds_a), (_, ds_x), (p_n, ds_n) = same[sb], cross[sb], cross[sb + 1]
                dq = _dot_tn(ds_a, keys[sb + 1]) + _dot_tn(ds_x, keys[sb])
                dk2 = _dot(ds_a, qs[sb]) + _dot(ds_n, qs[sb + 1])
                dv2 = _dot(p_a, dos[sb]) + _dot(p_n, dos[sb + 1])
                _set_rows(dq_buf, base, dil, dq)
                _set_rows(dk_buf, base, dil, jnp.where(head0, dk2[:BLK], dk2[BLK:]))
                _set_rows(dv_buf, base, dil, jnp.where(head0, dv2[:BLK], dv2[BLK:]))

        def residue_group(rg, carry):
            rs = [rg * group + i for i in range(group)]
            data = [load(r) for r in rs]
            raws = [products(d) for d in data]
            fins = [finish(d, raw) for d, raw in zip(data, raws)]
            for r, d, fin in zip(rs, data, fins):
                gradients(r, d, fin)
            return carry

        group = max(1, min(dil, ATTN_BLOCKS_TOGETHER // m))
        if dil // group <= 2:
            for rg in range(dil // group):
                residue_group(rg, 0)
        else:
            lax.fori_loop(0, dil // group, residue_group, 0)
        copies = []
        for t, (buf, col) in enumerate(((dq_buf, Q_COL), (dk_buf, K_COL), (dv_buf, V_COL))):
            val = buf[...]
            if t < 2:
                val = _rope_transpose(val, c_ref[...], s1_ref[...], s2_ref[...], 128)
            out_buf[t] = val.astype(out_buf.dtype)
            lane0 = pl.multiple_of((col + 2 * g + hp) * 128, 128)
            dst = dz_ref.at[pl.ds(pl.multiple_of(st * rows, rows), rows), pl.ds(lane0, 128)]
            cp = pltpu.make_async_copy(out_buf.at[t], dst, sems.at[t])
            cp.start()
            copies.append(cp)
        for cp in copies:
            cp.wait()

    def cur(col):
        return pl.BlockSpec((rows, 128), lambda st, hp: (st, col + 2 * g + hp))

    def before(col):
        return pl.BlockSpec((sbr, 128), lambda st, hp: (jnp.maximum(st * m - 1, 0), col + 2 * g + hp))

    def after(col):
        return pl.BlockSpec((sbr, 128), lambda st, hp: (jnp.minimum((st + 1) * m, s_len // sbr - 1), col + 2 * g + hp))

    tab = pl.BlockSpec((rows, 128), lambda st, hp: (st, 0))
    return pl.pallas_call(
        body, name=name, grid=(nsteps, 2),
        in_specs=[cur(Q_COL), after(Q_COL), cur(K_COL), before(K_COL), cur(V_COL), before(V_COL),
                  cur(0), after(0), cur(0), after(0), cur(0), after(0), tab, tab, tab,
                  pl.BlockSpec(memory_space=pl.ANY)],
        out_specs=pl.BlockSpec(memory_space=pl.ANY),
        out_shape=jax.ShapeDtypeStruct(dz.shape, dz.dtype), input_output_aliases={15: 0},
        scratch_shapes=[pltpu.VMEM((rows, 128), f32)] * 3 + [pltpu.VMEM((3, rows, 128), dz.dtype),
                                                            pltpu.SemaphoreType.DMA((3,))],
        compiler_params=_cparams(("arbitrary", "arbitrary")),
    )(z, z, z, z, z, z, do, do, lse, lse, dlt, dlt, *tabs, dz)


def _rope_tables(positions):
    inv_freq = ROPE_THETA ** (-jnp.arange(0, ROT_DIM, 2, dtype=f32) / ROT_DIM)
    ang = positions.astype(f32)[:, None] * inv_freq
    cos, sin = jnp.cos(ang), jnp.sin(ang)
    s_len = positions.shape[0]
    zero8, rest = jnp.zeros((s_len, 8), f32), jnp.zeros((s_len, HEAD_DIM - ROT_DIM), f32)
    c = jnp.concatenate([cos, cos, jnp.ones((s_len, HEAD_DIM - ROT_DIM), f32)], axis=1)
    s1 = jnp.concatenate([-sin, zero8, rest], axis=1)
    s2 = jnp.concatenate([zero8, sin, rest], axis=1)
    return tuple(jnp.tile(t, (1, 2)) for t in (c, s1, s2))


def _block_diag(pool_w):
    out = jnp.zeros((POOL_WIDTH, POOL_WIDTH), pool_w.dtype)
    for g in range(4):
        out = lax.dynamic_update_slice(out, pool_w[g], (g * POOL_GC, g * POOL_GC))
    return out


def _layer_fwd(h, p, wsrc, small, layer, tabs, head=None):
    nm = f"l{layer}_"
    wts, wl = wsrc.take(layer, ("w_in",), (h,) if layer else tuple(tabs))
    z, hn1 = _norm_matmul(h, small["norm1"][layer][None], wts["w_in"], wl, 256, nm + "in_proj", rope=tabs)
    ol = None
    for g in range(3):
        ol = _attn_fwd(z, g, ol, nm + f"attn_fwd{g}")
    outs, lses = ol
    wbd = _mx(_block_diag(small["pool_w"][layer]))
    scale = small["pool_scale"][layer][None]
    wts.update(wsrc.take(layer, ("w_out",), (outs,))[0])
    m, h1 = _mixer_out_proj(z, wbd, scale, outs, lses, wts["w_out"], wl, h, nm + "mixer_out")
    wts.update(wsrc.take(layer, ("w_up", "w_down"), (h1,))[0])
    h2, a, hn2 = _mlp_fwd(h1, small["norm2"][layer][None], wts["w_up"], wts["w_down"], wl, nm + "mlp")
    wts.update(wsrc.take(layer, ("w_gate", "w_ple"), (h2,))[0])
    *h3, gl, hn3 = _gate_ple_fwd(h2, small["norm3"][layer][None], wts["w_gate"], wts["w_ple"], wl, p, layer,
                                 nm + "gate_ple", head=head)
    saved = dict(h=h, z=z, hn1=hn1, outs=outs, lses=lses, wbd=wbd, scale=scale, m=m, h1=h1, a=a, hn2=hn2, h2=h2,
                 gl=gl, hn3=hn3, wts=wts, wl=wl)
    return h3, saved


def _layer_bwd(dh3, sv, p, small, layer, tabs128, reducer):
    nm = f"l{layer}_"
    wts, wl = sv["wts"], sv["wl"]
    dh2, dg3, de, dgl = _gate_bwd(dh3, sv["gl"], p, layer, wts["w_ple"], wts["w_gate"], wl, sv["h2"],
                                  small["norm3"][layer][None], nm + "gate_bwd")
    reducer.add("w_gate", layer, _weight_grad(sv["hn3"], dgl, nm + "dw_gate"))
    reducer.add("w_ple", layer, _weight_grad(p, de, nm + "dw_ple", lead=layer))
    dh1, dg2, da = _mlp_bwd(dh2, wts["w_down"], wts["w_up"], wl, sv["a"], sv["h1"], small["norm2"][layer][None],
                            nm + "mlp_bwd")
    reducer.add("w_down", layer, _weight_grad(sv["a"], dh2, nm + "dw_down", act=True))
    started = reducer.add("w_up", layer, _weight_grad(sv["hn2"], da, nm + "dw_up"))
    dpool, do, dlt = _out_combine_bwd(dh1, wts["w_out"], wl, sv["outs"], sv["lses"], nm + "out_bwd", after=started)
    started = reducer.add("w_out", layer, _weight_grad(sv["m"], dh1, nm + "dw_out"))
    dz, dwbd, dscale = _pool_bwd(sv["z"], dpool, sv["wbd"], sv["scale"], nm + "pool_bwd", after=started)
    for g in range(3):
        dz = _attn_bwd(sv["z"], do, sv["lses"], dlt, tabs128, dz, g, nm + f"attn_bwd{g}")
    started = reducer.add("w_in", layer, _weight_grad(sv["hn1"], dz, nm + "dw_in"))
    dh0, dg1 = _matmul_nt_norm_bwd(dz, wts["w_in"], wl, sv["h"], small["norm1"][layer][None], dh1, nm + "in_bwd",
                                   tk=512, after=started)
    dpool_w = jnp.stack([dwbd[g * POOL_GC:(g + 1) * POOL_GC, g * POOL_GC:(g + 1) * POOL_GC] for g in range(4)])
    sg = dict(norm1=dg1[0], norm2=dg2[0], norm3=dg3[0], pool_w=dpool_w, pool_scale=dscale[0])
    return dh0, sg


def _local_step(x, p, positions, wsrc, small, target, reducer):
    tabs128 = _rope_tables(positions)
    (h,), sv0 = _layer_fwd(x, p, wsrc, small, 0, tabs128)
    (loss, dh, dgf), sv1 = _layer_fwd(h, p, wsrc, small, 1, tabs128, head=(small["final_norm"][None], target))
    saved = [sv0, sv1]
    sgs = [None, None]
    for layer in (1, 0):
        dh, sgs[layer] = _layer_bwd(dh, saved[layer], p, small, layer, tabs128, reducer)
    small_grads = {k: jnp.stack([sgs[0][k], sgs[1][k]]) for k in sgs[0]}
    small_grads["final_norm"] = dgf[0]
    return loss, dh, small_grads


HBM = pl.BlockSpec(memory_space=pltpu.HBM)


def _my_place():
    return lax.axis_index("x"), lax.axis_index("y"), lax.axis_index("c")


def _other_chips(x, y):
    return [(1 - x, y), (x, 1 - y), (1 - x, 1 - y)]


def _window(ref, name, chip):
    k, n = _shard_shape(name)
    if COL_SHARDED[name]:
        return ref.at[:, pl.ds(pl.multiple_of(chip * n, 128), n)]
    return ref.at[pl.ds(pl.multiple_of(chip * k, 128), k), :]


def _chip_index():
    return jnp.reshape(2 * lax.axis_index("x") + lax.axis_index("y"), (1,)).astype(jnp.int32)


def _shard_block(name, tr):
    ks, ns = _shard_shape(name)
    if COL_SHARDED[name]:
        return (tr, ns), lambda i, me: (i, me[0])
    return (tr, ns), lambda i, me: (me[0] * (ks // tr) + i, 0)


def _place_shard(w, name, layer):
    ks, ns = _shard_shape(name)
    tr = min(ks, 256)
    shape, index = _shard_block(name, tr)

    def body(me_ref, w_ref, o_ref):
        o_ref[...] = w_ref[...].astype(o_ref.dtype)

    return pl.pallas_call(
        body, name=f"place_{name}{layer}",
        grid_spec=pltpu.PrefetchScalarGridSpec(
            num_scalar_prefetch=1, grid=(ks // tr,),
            in_specs=[pl.BlockSpec((None, tr, ns), lambda i, me: (layer, i, 0))],
            out_specs=pl.BlockSpec((None,) + shape, lambda i, me: (0,) + index(i, me))),
        out_shape=jax.ShapeDtypeStruct((1,) + FULL_SHAPE[name], MXU_DTYPE),
        compiler_params=_cparams(("parallel",)),
    )(_chip_index(), w)


GATHER_ORDER = [("w_in", 0), ("w_out", 0), ("w_up", 0), ("w_down", 0), ("w_gate", 0), ("w_ple", 0),
                ("w_in", 1), ("w_out", 1), ("w_up", 1), ("w_down", 1), ("w_gate", 1), ("w_ple", 1)]
SEM = pl.BlockSpec(memory_space=pltpu.SEMAPHORE)
EFFECT = pltpu.SideEffectType.DATAFLOW_SIDE_EFFECTING


def _gather_copy(src_ref, dst_ref, name, idx, j, chip, send_sems, recv_sems, c):
    cx, cy = chip
    return pltpu.make_async_remote_copy(
        src_ref=src_ref, dst_ref=dst_ref, send_sem=send_sems.at[3 * idx + j], recv_sem=recv_sems.at[3 * idx + j],
        device_id=(cx, cy, c), device_id_type=MESH)


def _gather_start(placed, order, tag, after=None):
    n = len(order)
    extra = [] if after is None else [after]

    def body(*refs):
        ins = refs[:n]
        k = n + len(extra)
        send_sems, recv_sems = refs[k], refs[k + 1]
        outs = refs[k + 2:k + 2 + n]
        token = refs[-1]
        x, y, c = _my_place()
        me = 2 * x + y
        for idx, (name, _) in enumerate(order):
            for j, chip in enumerate(_other_chips(x, y)):
                _gather_copy(_window(ins[idx].at[0], name, me), _window(outs[idx].at[0], name, me), name, idx, j, chip,
                             send_sems, recv_sems, c).start()
        token[...] = jnp.zeros_like(token)

    res = pl.pallas_call(
        body, name="gather_start" + tag,
        out_shape=(pltpu.SemaphoreType.DMA((3 * n,)), pltpu.SemaphoreType.DMA((3 * n,)))
        + tuple(pltpu.HBM(a.shape, a.dtype) for a in placed) + (jax.ShapeDtypeStruct((8, 128), f32),),
        in_specs=[HBM] * n + [pl.BlockSpec(memory_space=pl.ANY)] * len(extra),
        out_specs=(SEM, SEM) + (HBM,) * n + (pl.BlockSpec(memory_space=pltpu.VMEM),),
        input_output_aliases={i: i + 2 for i in range(n)},
        compiler_params=pltpu.CompilerParams(has_side_effects=EFFECT),
    )(*[pltpu.with_memory_space_constraint(a, pltpu.HBM) for a in placed], *extra)
    return res[0], res[1], list(res[2:2 + n]), res[-1]


def _gather_wait(send_sems, recv_sems, arrays, order, idxs, after, name):
    n = len(idxs)

    def body(*refs):
        ins = refs[:n]
        send_ref, recv_ref = refs[n], refs[n + 1]
        x, y, c = _my_place()
        me = 2 * x + y
        for k, idx in enumerate(idxs):
            wname = order[idx][0]
            for j, chip in enumerate(_other_chips(x, y)):
                cx, cy = chip
                mine = _window(ins[k].at[0], wname, me)
                land = _window(ins[k].at[0], wname, 2 * cx + cy)
                _gather_copy(mine, mine, wname, idx, j, chip, send_ref, recv_ref, c).wait_send()
                _gather_copy(land, land, wname, idx, j, chip, send_ref, recv_ref, c).wait_recv()

    operands = list(arrays) + [send_sems, recv_sems] + list(after)
    in_specs = [HBM] * n + [SEM, SEM] + [pl.BlockSpec(memory_space=pl.ANY)] * len(after)
    res = pl.pallas_call(
        body, name=name, out_shape=tuple(pltpu.HBM(a.shape, a.dtype) for a in arrays),
        in_specs=in_specs, out_specs=(HBM,) * n, input_output_aliases={i: i for i in range(n)},
        compiler_params=pltpu.CompilerParams(has_side_effects=EFFECT),
    )(*operands)
    return list(res)


class _GatheredWeights:
    def __init__(self, shards):
        self.starts = []
        token = None
        for tag, order in (("_first", GATHER_ORDER[:1]), ("_rest", GATHER_ORDER[1:])):
            placed = [_place_shard(shards[name], name, layer) for name, layer in order]
            self.starts.append((order,) + _gather_start(placed, order, tag, token))
            token = self.starts[-1][-1]

    def take(self, layer, names, after):
        order, send, recv, arrays, _ = next(s for s in self.starts if (names[0], layer) in s[0])
        after = list(after)
        if order is self.starts[0][0]:
            after.append(self.starts[-1][-1])
        idxs = [order.index((n, layer)) for n in names]
        got = _gather_wait(send, recv, [arrays[i] for i in idxs], order, idxs, after, f"gather_wait{layer}_{names[0]}")
        return dict(zip(names, got)), 0


N_DEV = 8


def _reduce_copies(dws, lands, names, layer, send_sems, recv_sems):
    x, y, c = _my_place()
    me, my_dev = 2 * x + y, 4 * x + 2 * y + c
    out = []
    for t, name in enumerate(names):
        for j, (cx, cy) in enumerate(_other_chips(x, y)):
            out.append((pltpu.make_async_remote_copy(
                src_ref=_window(dws[t], name, 2 * cx + cy), dst_ref=lands[t].at[my_dev],
                send_sem=send_sems.at[4 * t + j], recv_sem=recv_sems.at[N_DEV * t + my_dev],
                device_id=(cx, cy, layer), device_id_type=MESH), False))
        out.append((pltpu.make_async_remote_copy(
            src_ref=_window(dws[t], name, me), dst_ref=lands[t].at[my_dev],
            send_sem=send_sems.at[4 * t + 3], recv_sem=recv_sems.at[N_DEV * t + my_dev],
            device_id=(x, y, layer), device_id_type=MESH), True))
    return out


def _reduce_start(dws, names, layer, tag):
    n = len(names)
    lands = [lax.empty((N_DEV,) + _shard_shape(nm), dws[0].dtype) for nm in names]

    def body(*refs):
        ins = refs[:n]
        send_sems, recv_sems = refs[2 * n], refs[2 * n + 1]
        land_out = refs[3 * n + 2:4 * n + 2]
        token = refs[-1]
        c = lax.axis_index("c")
        for cp, non_owner_only in _reduce_copies(ins, land_out, names, layer, send_sems, recv_sems):
            if non_owner_only:
                @pl.when(c != layer)
                def _():
                    cp.start()
            else:
                cp.start()
        token[...] = jnp.zeros_like(token)

    res = pl.pallas_call(
        body, name="reduce_start" + tag,
        out_shape=(pltpu.SemaphoreType.DMA((4 * n,)), pltpu.SemaphoreType.DMA((N_DEV * n,)))
        + tuple(pltpu.HBM(a.shape, a.dtype) for a in dws) + tuple(pltpu.HBM(a.shape, a.dtype) for a in lands)
        + (jax.ShapeDtypeStruct((8, 128), f32),),
        in_specs=[HBM] * (2 * n),
        out_specs=(SEM, SEM) + (HBM,) * (2 * n) + (pl.BlockSpec(memory_space=pltpu.VMEM),),
        input_output_aliases={i: i + 2 for i in range(2 * n)},
        compiler_params=pltpu.CompilerParams(has_side_effects=EFFECT),
    )(*[pltpu.with_memory_space_constraint(a, pltpu.HBM) for a in list(dws) + lands])
    return res[0], res[1], list(res[2:2 + n]), list(res[2 + n:2 + 2 * n]), res[-1]


def _reduce_wait(send_sems, recv_sems, dws, lands, names, layer, after, tag):
    n = len(names)

    def body(*refs):
        ins, land_in = refs[:n], refs[n:2 * n]
        send_ref, recv_ref = refs[2 * n], refs[2 * n + 1]
        x, y, c = _my_place()
        for cp, non_owner_only in _reduce_copies(ins, land_in, names, layer, send_ref, recv_ref):
            if non_owner_only:
                @pl.when(c != layer)
                def _():
                    cp.wait_send()
            else:
                cp.wait_send()

        @pl.when(c == layer)
        def _():
            for t in range(n):
                for k in range(1, N_DEV):
                    px, py, pc = x ^ ((k >> 2) & 1), y ^ ((k >> 1) & 1), c ^ (k & 1)
                    dev = 4 * px + 2 * py + pc
                    land = land_in[t].at[dev]
                    pltpu.make_async_remote_copy(
                        src_ref=land, dst_ref=land, send_sem=send_ref.at[4 * t], recv_sem=recv_ref.at[N_DEV * t + dev],
                        device_id=(px, py, pc), device_id_type=MESH).wait_recv()

    res = pl.pallas_call(
        body, name="reduce_wait" + tag,
        out_shape=tuple(pltpu.HBM(a.shape, a.dtype) for a in list(dws) + list(lands)),
        in_specs=[HBM] * (2 * n) + [SEM, SEM, pl.BlockSpec(memory_space=pl.ANY)], out_specs=(HBM,) * (2 * n),
        input_output_aliases={i: i for i in range(2 * n)},
        compiler_params=pltpu.CompilerParams(has_side_effects=EFFECT),
    )(*dws, *lands, send_sems, recv_sems, after)
    return list(res[:n]), list(res[n:])


def _sum_devices(land, own, name, layer, prev):
    ks, ns = _shard_shape(name)
    tr = min(ks, 256)
    shape, index = _shard_block(name, tr)

    def body(me_ref, dev_ref, *refs):
        s_ref, own_ref, out_ref = refs[0], refs[1], refs[-1]
        dev = dev_ref[0]
        acc = None
        for s in range(N_DEV):
            term = jnp.where(dev == s, own_ref[...], s_ref[s]).astype(f32)
            acc = term if acc is None else acc + term
        out_ref[...] = acc

    def mine(i, dev):
        return i * jnp.where((dev[0] & 1) == layer, 1, 0)

    in_specs = [pl.BlockSpec((N_DEV, tr, ns), lambda i, me, dev: (0, mine(i, dev), 0)),
                pl.BlockSpec(shape, lambda i, me, dev: index(mine(i, dev), me))]
    args = [land, own]
    aliases = {}
    if prev is not None:
        in_specs.append(pl.BlockSpec(memory_space=pl.ANY))
        args.append(prev)
        aliases = {4: 0}
    x, y, c = _my_place()
    return pl.pallas_call(
        body, name=f"sum_devices_{name}{layer}",
        grid_spec=pltpu.PrefetchScalarGridSpec(
            num_scalar_prefetch=2, grid=(ks // tr,), in_specs=in_specs,
            out_specs=pl.BlockSpec((None, tr, ns), lambda i, me, dev: (layer, mine(i, dev), 0))),
        out_shape=jax.ShapeDtypeStruct((2, ks, ns), f32), input_output_aliases=aliases,
        compiler_params=_cparams(("arbitrary",)),
    )(_chip_index(), jnp.reshape(4 * x + 2 * y + c, (1,)).astype(jnp.int32), *args)


class _GradReducer:
    GROUPS = (("1", 1, ("w_gate", "w_ple", "w_down", "w_up", "w_out", "w_in")),
              ("0a", 0, ("w_gate", "w_ple", "w_down", "w_up")),
              ("0b", 0, ("w_out",)),
              ("0c", 0, ("w_in",)))

    def __init__(self):
        self.grads = {}
        self.started = {}

    def add(self, name, layer, dw):
        self.grads[(name, layer)] = dw
        token = None
        for tag, glayer, names in self.GROUPS:
            if tag not in self.started and all((nm, glayer) in self.grads for nm in names):
                *self.started[tag], token = _reduce_start([self.grads[(nm, glayer)] for nm in names], names, glayer, tag)
        return token

    def finish(self, after):
        mine = {}
        for tag, layer, names in self.GROUPS:
            send, recv, dws, lands = self.started[tag]
            dws, lands = _reduce_wait(send, recv, dws, lands, names, layer, after, tag)
            for nm, dw, land in zip(names, dws, lands):
                mine[nm] = _sum_devices(land, dw, nm, layer, mine.get(nm))
        return _pair_layers(mine)


def _pair_layers(mine):
    names = list(BIG)

    def body(*refs):
        ins = refs[:len(names)]
        outs = refs[len(names):2 * len(names)]
        send_sems, recv_sems = refs[2 * len(names):]
        x, y, c = _my_place()
        sibling = (x, y, 1 - c)
        cps = []
        for t in range(len(names)):
            cp = pltpu.make_async_remote_copy(
                src_ref=ins[t].at[c], dst_ref=outs[t].at[c], send_sem=send_sems.at[t], recv_sem=recv_sems.at[t],
                device_id=sibling, device_id_type=MESH)
            cp.start()
            cps.append(cp)
        for t in range(len(names)):
            cps[t].wait_send()
            land = outs[t].at[1 - c]
            pltpu.make_async_remote_copy(
                src_ref=land, dst_ref=land, send_sem=send_sems.at[t], recv_sem=recv_sems.at[t],
                device_id=sibling, device_id_type=MESH).wait_recv()

    outs = pl.pallas_call(
        body, name="pair_layers", in_specs=[HBM] * len(names), out_specs=[HBM] * len(names),
        out_shape=[jax.ShapeDtypeStruct((2,) + _shard_shape(n), f32) for n in names],
        input_output_aliases={t: t for t in range(len(names))},
        scratch_shapes=[pltpu.SemaphoreType.DMA((len(names),)), pltpu.SemaphoreType.DMA((len(names),))],
    )(*[mine[n] for n in names])
    return dict(zip(names, outs))


SMALL_ROWS = 320


def _small_copies(vec_ref, land_ref, send_sems, recv_sems):
    x, y, c = _my_place()
    me = 4 * x + 2 * y + c
    out = []
    for k in range(1, N_DEV):
        peer = (x ^ ((k >> 2) & 1), y ^ ((k >> 1) & 1), c ^ (k & 1))
        src_dev = 4 * peer[0] + 2 * peer[1] + peer[2]
        send = pltpu.make_async_remote_copy(
            src_ref=vec_ref, dst_ref=land_ref.at[me], send_sem=send_sems.at[k - 1], recv_sem=recv_sems.at[k - 1],
            device_id=peer, device_id_type=MESH)
        arrival = pltpu.make_async_remote_copy(
            src_ref=land_ref.at[src_dev], dst_ref=land_ref.at[src_dev], send_sem=send_sems.at[k - 1],
            recv_sem=recv_sems.at[k - 1], device_id=peer, device_id_type=MESH)
        out.append((send, arrival))
    return out


def _small_start(vec):
    land = lax.empty((N_DEV,) + vec.shape, vec.dtype)

    def body(v_ref, land_in, send_sems, recv_sems, v_out, land_out):
        del land_in, v_out
        for send, _ in _small_copies(v_ref, land_out, send_sems, recv_sems):
            send.start()

    return pl.pallas_call(
        body, name="small_start",
        out_shape=(pltpu.SemaphoreType.DMA((N_DEV - 1,)), pltpu.SemaphoreType.DMA((N_DEV - 1,)),
                   pltpu.HBM(vec.shape, vec.dtype), pltpu.HBM(land.shape, land.dtype)),
        in_specs=[HBM, HBM], out_specs=(SEM, SEM, HBM, HBM), input_output_aliases={0: 2, 1: 3},
        compiler_params=pltpu.CompilerParams(has_side_effects=EFFECT),
    )(pltpu.with_memory_space_constraint(vec, pltpu.HBM), pltpu.with_memory_space_constraint(land, pltpu.HBM))


def _small_wait(send_sems, recv_sems, vec, land, after):
    def body(v_ref, land_ref, send_ref, recv_ref, after_ref, v_out, land_out):
        del after_ref, v_out, land_out
        for send, arrival in _small_copies(v_ref, land_ref, send_ref, recv_ref):
            send.wait_send()
            arrival.wait_recv()

    return pl.pallas_call(
        body, name="small_wait", out_shape=(pltpu.HBM(vec.shape, vec.dtype), pltpu.HBM(land.shape, land.dtype)),
        in_specs=[HBM, HBM, SEM, SEM, pl.BlockSpec(memory_space=pl.ANY)], out_specs=(HBM, HBM),
        input_output_aliases={0: 0, 1: 1}, compiler_params=pltpu.CompilerParams(has_side_effects=EFFECT),
    )(vec, land, send_sems, recv_sems, after)


def _small_sum(vec, land):
    x, y, c = _my_place()

    def body(dev_ref, v_ref, land_ref, out_ref):
        acc = None
        for s in range(N_DEV):
            term = jnp.where(dev_ref[0] == s, v_ref[...], land_ref[s])
            acc = term if acc is None else acc + term
        out_ref[...] = acc

    return pl.pallas_call(
        body, name="small_sum",
        grid_spec=pltpu.PrefetchScalarGridSpec(
            num_scalar_prefetch=1, grid=(1,),
            in_specs=[pl.BlockSpec(vec.shape, lambda i, dev: (0, 0)), pl.BlockSpec(land.shape, lambda i, dev: (0, 0, 0))],
            out_specs=pl.BlockSpec(vec.shape, lambda i, dev: (0, 0))),
        out_shape=jax.ShapeDtypeStruct(vec.shape, vec.dtype),
        compiler_params=_cparams(("arbitrary",)),
    )(jnp.reshape(4 * x + 2 * y + c, (1,)).astype(jnp.int32), vec, land)


def _adamw(w, g, m, v, name):
    rows, cols = w.shape
    tr = rows
    for cand in (512, 256, 128, 64, 32, 16, 8):
        if rows % cand == 0 and cand * cols * 4 <= 2 * 1024 * 1024:
            tr = cand
            break
    c1 = np.float32(1.0 - ADAM_B1 ** ADAM_STEP)
    c2 = np.float32(1.0 - ADAM_B2 ** ADAM_STEP)

    def body(w_ref, g_ref, m_ref, v_ref, go_ref, d_ref, mo_ref, vo_ref):
        gv = g_ref[...]
        go_ref[...] = gv
        mn = ADAM_B1 * m_ref[...] + (1.0 - ADAM_B1) * gv
        vn = ADAM_B2 * v_ref[...] + (1.0 - ADAM_B2) * (gv * gv)
        mo_ref[...] = mn
        vo_ref[...] = vn
        d_ref[...] = -ADAM_LR * ((mn / c1) / (jnp.sqrt(vn / c2) + ADAM_EPS) + ADAM_WD * w_ref[...])

    blk = pl.BlockSpec((tr, cols), lambda i: (i, 0))
    return pl.pallas_call(
        body, name="adamw_" + name, grid=(rows // tr,), in_specs=[blk] * 4, out_specs=[blk] * 4,
        out_shape=[jax.ShapeDtypeStruct((rows, cols), f32)] * 4,
        compiler_params=_cparams(("parallel",)),
    )(w, g, m, v)


SMALL = ("norm1", "pool_w", "pool_scale", "norm2", "norm3", "final_norm")
ORDER = ("norm1", "w_in", "pool_w", "pool_scale", "w_out", "norm2", "w_up", "w_down", "norm3", "w_gate", "w_ple",
         "final_norm")


def _pack_small(tree, extra=None):
    parts = [tree[n].reshape(-1) for n in SMALL]
    if extra is not None:
        parts.append(extra.reshape(-1))
    flat = jnp.concatenate(parts)
    return jnp.pad(flat, (0, SMALL_ROWS * 128 - flat.shape[0])).reshape(SMALL_ROWS, 128)


def _unpack_small(packed, like):
    flat = packed.reshape(-1)
    out, off = {}, 0
    for n in SMALL:
        size = int(np.prod(like[n].shape))
        out[n] = flat[off:off + size].reshape(like[n].shape)
        off += size
    return out, flat[off]


def kernel(x, p, positions, norm1, w_in, pool_w, pool_scale, w_out, norm2, w_up, w_down, norm3, w_gate, w_ple, final_norm, loss_target, m_norm1, m_w_in, m_pool_w, m_pool_scale, m_w_out, m_norm2, m_w_up, m_w_down, m_norm3, m_w_gate, m_w_ple, m_final_norm, v_norm1, v_w_in, v_pool_w, v_pool_scale, v_w_out, v_norm2, v_w_up, v_w_down, v_norm3, v_w_gate, v_w_ple, v_final_norm):
    w = dict(norm1=norm1, w_in=w_in, pool_w=pool_w, pool_scale=pool_scale, w_out=w_out, norm2=norm2, w_up=w_up,
             w_down=w_down, norm3=norm3, w_gate=w_gate, w_ple=w_ple, final_norm=final_norm)
    m = dict(norm1=m_norm1, w_in=m_w_in, pool_w=m_pool_w, pool_scale=m_pool_scale, w_out=m_w_out, norm2=m_norm2,
             w_up=m_w_up, w_down=m_w_down, norm3=m_norm3, w_gate=m_w_gate, w_ple=m_w_ple, final_norm=m_final_norm)
    v = dict(norm1=v_norm1, w_in=v_w_in, pool_w=v_pool_w, pool_scale=v_pool_scale, w_out=v_w_out, norm2=v_norm2,
             w_up=v_w_up, w_down=v_w_down, norm3=v_norm3, w_gate=v_w_gate, w_ple=v_w_ple, final_norm=v_final_norm)
    small = {n: w[n] for n in SMALL}

    wsrc = _GatheredWeights({n: w[n] for n in BIG})
    reducer = _GradReducer()
    loss8, dx, small_grads = _local_step(x[0], p.reshape(2, x.shape[1], PLE_DIM), positions[0], wsrc, small, loss_target[0], reducer)
    s_send, s_recv, s_vec, s_land = _small_start(_pack_small(small_grads, loss8[0, 0]))
    gsh = reducer.finish(s_vec)

    g_out, d_out, m_out, v_out = {}, {}, {}, {}
    for n in BIG:
        shp = w[n].shape
        two = lambda a: a.reshape(shp[0] * shp[1], shp[2])
        g2, d2, m2, v2 = _adamw(two(w[n]), two(gsh[n]), two(m[n]), two(v[n]), n)
        g_out[n], d_out[n], m_out[n], v_out[n] = g2.reshape(shp), d2.reshape(shp), m2.reshape(shp), v2.reshape(shp)
    red = _small_sum(*_small_wait(s_send, s_recv, s_vec, s_land, d2))
    g_small, loss = _unpack_small(red, small)
    _, d2, m2, v2 = _adamw(_pack_small(small), red, _pack_small({n: m[n] for n in SMALL}),
                           _pack_small({n: v[n] for n in SMALL}), "small")
    for tree, packed in ((d_out, d2), (m_out, m2), (v_out, v2)):
        tree.update(_unpack_small(packed, small)[0])
    g_out.update(g_small)

    return (loss, dx[None], *[g_out[n] for n in ORDER], *[d_out[n] for n in ORDER], *[m_out[n] for n in ORDER],
            *[v_out[n] for n in ORDER])
```

```python
import jax
import jax.numpy as jnp
import numpy as np
from jax import lax
from jax.experimental import pallas as pl
from jax.experimental.pallas import tpu as pltpu

f32 = jnp.float32
MXU_DTYPE = jnp.bfloat16
COMM_DTYPE = jnp.bfloat16

D_MODEL = 1024
POOL_WIDTH = 256
POOL_GC = 64
ATTN_WIDTH = 768
HEAD_DIM = 64
N_IN = POOL_WIDTH + 3 * ATTN_WIDTH
D_FF = 4096
PLE_DIM = 256
BLK = 128
DILATIONS = (1, 4, 16)
ROT_DIM = 16
ROPE_THETA = 500000.0
EPS = 1e-6
ATTN_SCALE = HEAD_DIM ** -0.5
NEG_BIG = -1e30

ADAM_LR, ADAM_B1, ADAM_B2, ADAM_EPS, ADAM_WD, ADAM_STEP = 0.001, 0.9, 0.999, 1e-08, 0.01, 10

TM = 512
TM_WGRAD = 1024
HALO = 16
VMEM_LIMIT = 48 * 1024 * 1024
VMEM_LIMIT_LARGE = 58 * 1024 * 1024
VMEM_COMPILER_RESERVE = 6 * 1024 * 1024
N_CHIPS = 4
MESH = pl.DeviceIdType.MESH

BIG = ("w_in", "w_out", "w_up", "w_down", "w_gate", "w_ple")
FULL_SHAPE = {"w_in": (D_MODEL, N_IN), "w_out": (D_MODEL, D_MODEL), "w_up": (D_MODEL, D_FF),
              "w_down": (D_FF, D_MODEL), "w_gate": (D_MODEL, D_MODEL), "w_ple": (PLE_DIM, D_MODEL)}
COL_SHARDED = {"w_in": True, "w_out": False, "w_up": True, "w_down": False, "w_gate": False, "w_ple": True}


def _shard_shape(name):
    k, n = FULL_SHAPE[name]
    return (k, n // N_CHIPS) if COL_SHARDED[name] else (k // N_CHIPS, n)


def _cparams(sem=None, vmem=VMEM_LIMIT):
    return pltpu.CompilerParams(dimension_semantics=sem, vmem_limit_bytes=vmem)


def _resident(block_shape, index_map):
    return pl.BlockSpec(block_shape, index_map, pipeline_mode=pl.Buffered(1))


def _mx(x):
    return x.astype(MXU_DTYPE)


def _dot(a, b):
    return jnp.dot(a, b, preferred_element_type=f32)


def _dot_nt(a, b):
    return lax.dot_general(a, b, (((1,), (1,)), ((), ())), preferred_element_type=f32)


def _dot_tn(a, b):
    return lax.dot_general(a, b, (((0,), (0,)), ((), ())), preferred_element_type=f32)


def _sigmoid(x):
    return 1.0 / (1.0 + jnp.exp(-x))


def _rope_apply(y, c, s1, s2, width):
    return y * c + pltpu.roll(y, width - 8, axis=1) * s1 + pltpu.roll(y, 8, axis=1) * s2


def _rope_transpose(dy, c, s1, s2, width):
    return dy * c + pltpu.roll(dy * s1, 8, axis=1) + pltpu.roll(dy * s2, width - 8, axis=1)


def _norm_matmul(h, g, w, layer, tn, name, rope=None):
    s_len, d = h.shape
    n = w.shape[2]

    def body(*refs):
        if rope is None:
            h_ref, g_ref, w_ref, y_ref, hn_ref = refs
        else:
            h_ref, g_ref, w_ref, c_ref, s1_ref, s2_ref, y_ref, hn_ref = refs
            reps = tn // 128
            c = jnp.concatenate([c_ref[...]] * reps, axis=1)
            s1 = jnp.concatenate([s1_ref[...]] * reps, axis=1)
            s2 = jnp.concatenate([s2_ref[...]] * reps, axis=1)
        x = h_ref[...]
        r = lax.rsqrt(jnp.mean(x * x, axis=-1, keepdims=True) + EPS)
        hn = ((x * r) * g_ref[...]).astype(hn_ref.dtype)
        hn_ref[...] = hn
        for j in range(n // tn):
            y = _dot(hn, w_ref[:, j * tn:(j + 1) * tn])
            if rope is not None and POOL_WIDTH <= j * tn < POOL_WIDTH + 2 * ATTN_WIDTH:
                y = _rope_apply(y, c, s1, s2, tn)
            y_ref[:, j * tn:(j + 1) * tn] = y

    in_specs = [pl.BlockSpec((TM, d), lambda i: (i, 0)),
                pl.BlockSpec((1, d), lambda i: (0, 0)),
                _resident((None, d, n), lambda i: (layer, 0, 0))]
    args = [h, g, w]
    if rope is not None:
        assert POOL_WIDTH % tn == 0 and (2 * ATTN_WIDTH) % tn == 0
        in_specs += [pl.BlockSpec((TM, 128), lambda i: (i, 0))] * 3
        args += list(rope)
    return pl.pallas_call(
        body, name=name, grid=(s_len // TM,), in_specs=in_specs,
        out_specs=[pl.BlockSpec((TM, n), lambda i: (i, 0)), pl.BlockSpec((TM, d), lambda i: (i, 0))],
        out_shape=[jax.ShapeDtypeStruct((s_len, n), f32), jax.ShapeDtypeStruct((s_len, d), MXU_DTYPE)],
        compiler_params=_cparams(("parallel",)),
    )(*args)


def _gate_ple_fwd(h2, g, w_gate, w_ple, layer, p, p_layer, name, head=None):
    s_len, d = h2.shape

    def body(h_ref, g_ref, wg_ref, p_ref, wp_ref, *rest):
        gl_ref, hn_ref = rest[-2:]
        x = h_ref[...]
        r = lax.rsqrt(jnp.mean(x * x, axis=-1, keepdims=True) + EPS)
        hn = ((x * r) * g_ref[...]).astype(hn_ref.dtype)
        hn_ref[...] = hn
        gl = _dot(hn, wg_ref[...])
        gl_ref[...] = gl.astype(gl_ref.dtype)
        h3 = x + _sigmoid(gl) * _dot(_mx(p_ref[...]), wp_ref[...])
        if head is None:
            rest[0][...] = h3
            return
        gf_ref, t_ref, loss_ref, dh_ref, dgf_ref = rest[:5]
        i = pl.program_id(0)
        gv = gf_ref[...]
        r3 = lax.rsqrt(jnp.mean(h3 * h3, axis=-1, keepdims=True) + EPS)
        xh = h3 * r3
        diff = xh * gv - t_ref[...]
        part = 0.5 * jnp.sum(jnp.mean(diff * diff, axis=-1, keepdims=True), axis=0, keepdims=True)
        dy = diff * (1.0 / d)
        dxh = dy * gv
        dh_ref[...] = r3 * (dxh - xh * jnp.mean(dxh * xh, axis=-1, keepdims=True))
        dgsum = jnp.sum(dy * xh, axis=0, keepdims=True)
        lossb = jnp.broadcast_to(part, (8, 128))

        @pl.when(i == 0)
        def _():
            loss_ref[...] = lossb
            dgf_ref[...] = dgsum

        @pl.when(i > 0)
        def _():
            loss_ref[...] += lossb
            dgf_ref[...] += dgsum

    row = lambda i: (i, 0)
    one = lambda i: (0, 0)
    in_specs = [pl.BlockSpec((TM, d), row), pl.BlockSpec((1, d), one),
                pl.BlockSpec((None, d, d), lambda i: (layer, 0, 0)),
                pl.BlockSpec((None, TM, PLE_DIM), lambda i: (p_layer, i, 0)),
                pl.BlockSpec((None, PLE_DIM, d), lambda i: (layer, 0, 0))]
    args = [h2, g, w_gate, p, w_ple]
    saved = [jax.ShapeDtypeStruct((s_len, d), MXU_DTYPE)] * 2
    if head is None:
        out_specs = [pl.BlockSpec((TM, d), row)] * 3
        out_shape = [jax.ShapeDtypeStruct((s_len, d), f32)] + saved
    else:
        in_specs += [pl.BlockSpec((1, d), one), pl.BlockSpec((TM, d), row)]
        args += list(head)
        out_specs = [pl.BlockSpec((8, 128), one), pl.BlockSpec((TM, d), row), pl.BlockSpec((1, d), one)] \
            + [pl.BlockSpec((TM, d), row)] * 2
        out_shape = [jax.ShapeDtypeStruct((8, 128), f32), jax.ShapeDtypeStruct((s_len, d), f32),
                     jax.ShapeDtypeStruct((1, d), f32)] + saved
    return pl.pallas_call(
        body, name=name, grid=(s_len // TM,), in_specs=in_specs, out_specs=out_specs, out_shape=out_shape,
        compiler_params=_cparams(("arbitrary",)),
    )(*args)


def _gate_bwd(dh3, gl, p, p_layer, w_ple, w_gate, layer, h2, g, name):
    s_len, d = dh3.shape

    def body(dh_ref, gl_ref, p_ref, wp_ref, wg_ref, h_ref, g_ref, dh2_ref, dg_ref, de_ref, dgl_ref):
        i = pl.program_id(0)
        dh = dh_ref[...]
        gate = _sigmoid(gl_ref[...].astype(f32))
        e = _dot(_mx(p_ref[...]), wp_ref[...])
        de_ref[...] = (dh * gate).astype(de_ref.dtype)
        dgl = ((dh * e) * (gate * (1.0 - gate))).astype(dgl_ref.dtype)
        dgl_ref[...] = dgl
        dx, dgrow = _rmsnorm_bwd(_dot_nt(dgl, wg_ref[...]), h_ref[...], g_ref[...])
        dh2_ref[...] = dh + dx
        dgsum = jnp.sum(dgrow, axis=0, keepdims=True)

        @pl.when(i == 0)
        def _():
            dg_ref[...] = dgsum

        @pl.when(i > 0)
        def _():
            dg_ref[...] += dgsum

    row = lambda i: (i, 0)
    blk = pl.BlockSpec((TM, d), row)
    return pl.pallas_call(
        body, name=name, grid=(s_len // TM,),
        in_specs=[blk, blk, pl.BlockSpec((None, TM, PLE_DIM), lambda i: (p_layer, i, 0)),
                  _resident((None, PLE_DIM, d), lambda i: (layer, 0, 0)),
                  _resident((None, d, d), lambda i: (layer, 0, 0)), blk, pl.BlockSpec((1, d), lambda i: (0, 0))],
        out_specs=[blk, pl.BlockSpec((1, d), lambda i: (0, 0)), blk, blk],
        out_shape=[jax.ShapeDtypeStruct((s_len, d), f32), jax.ShapeDtypeStruct((1, d), f32),
                   jax.ShapeDtypeStruct((s_len, d), MXU_DTYPE), jax.ShapeDtypeStruct((s_len, d), MXU_DTYPE)],
        compiler_params=_cparams(("arbitrary",)),
    )(dh3, gl, p, w_ple, w_gate, h2, g)


def _rmsnorm_bwd(dhn, x, g):
    r = lax.rsqrt(jnp.mean(x * x, axis=-1, keepdims=True) + EPS)
    xh = x * r
    dxh = dhn * g
    dx = r * (dxh - xh * jnp.mean(dxh * xh, axis=-1, keepdims=True))
    return dx, dhn * xh


def _matmul_nt_norm_bwd(dy, w, layer, h_prev, g, dres, name, tk=1024, after=None):
    s_len, k_dim = dy.shape
    d = h_prev.shape[1]

    def body(dy_ref, w_ref, h_ref, g_ref, dres_ref, *rest):
        dh_ref, dg_ref = rest[-2:]
        i = pl.program_id(0)
        acc = None
        for k in range(k_dim // tk):
            part = _dot_nt(_mx(dy_ref[:, k * tk:(k + 1) * tk]), w_ref[:, k * tk:(k + 1) * tk])
            acc = part if acc is None else acc + part
        dx, dgrow = _rmsnorm_bwd(acc, h_ref[...], g_ref[...])
        dh_ref[...] = dres_ref[...] + dx
        dgsum = jnp.sum(dgrow, axis=0, keepdims=True)

        @pl.when(i == 0)
        def _():
            dg_ref[...] = dgsum

        @pl.when(i > 0)
        def _():
            dg_ref[...] += dgsum

    in_specs = [pl.BlockSpec((TM, k_dim), lambda i: (i, 0)),
                _resident((None, d, k_dim), lambda i: (layer, 0, 0)),
                pl.BlockSpec((TM, d), lambda i: (i, 0)),
                pl.BlockSpec((1, d), lambda i: (0, 0)),
                pl.BlockSpec((TM, d), lambda i: (i, 0))]
    args = [dy, w, h_prev, g, dres]
    if after is not None:
        in_specs.append(pl.BlockSpec(memory_space=pl.ANY))
        args.append(after)
    return pl.pallas_call(
        body, name=name, grid=(s_len // TM,), in_specs=in_specs,
        out_specs=[pl.BlockSpec((TM, d), lambda i: (i, 0)), pl.BlockSpec((1, d), lambda i: (0, 0))],
        out_shape=[jax.ShapeDtypeStruct((s_len, d), f32), jax.ShapeDtypeStruct((1, d), f32)],
        compiler_params=_cparams(("arbitrary",)),
    )(*args)


def _mlp_fwd(h1, g, w_up, w_down, layer, name, tf=1024):
    s_len, d = h1.shape
    ff = w_up.shape[2]

    def body(h_ref, g_ref, wu_ref, wd_ref, h2_ref, a_ref, hn_ref):
        x = h_ref[...]
        r = lax.rsqrt(jnp.mean(x * x, axis=-1, keepdims=True) + EPS)
        hn = ((x * r) * g_ref[...]).astype(hn_ref.dtype)
        hn_ref[...] = hn
        acc = x
        for j in range(ff // tf):
            a = _dot(hn, wu_ref[:, j * tf:(j + 1) * tf])
            a_ref[:, j * tf:(j + 1) * tf] = a.astype(a_ref.dtype)
            relu = jnp.maximum(a, 0.0)
            acc = acc + _dot(_mx(relu * relu), wd_ref[j * tf:(j + 1) * tf, :])
        h2_ref[...] = acc

    row = lambda i: (i, 0)
    return pl.pallas_call(
        body, name=name, grid=(s_len // TM,),
        in_specs=[pl.BlockSpec((TM, d), row), pl.BlockSpec((1, d), lambda i: (0, 0)),
                  _resident((None, d, ff), lambda i: (layer, 0, 0)), _resident((None, ff, d), lambda i: (layer, 0, 0))],
        out_specs=[pl.BlockSpec((TM, d), row), pl.BlockSpec((TM, ff), row), pl.BlockSpec((TM, d), row)],
        out_shape=[jax.ShapeDtypeStruct((s_len, d), f32), jax.ShapeDtypeStruct((s_len, ff), MXU_DTYPE),
                   jax.ShapeDtypeStruct((s_len, d), MXU_DTYPE)],
        compiler_params=_cparams(("parallel",)),
    )(h1, g, w_up, w_down)


def _mlp_bwd(dh2, w_down, w_up, layer, a, h1, g, name, tf=1024):
    s_len, d = dh2.shape
    ff = a.shape[1]

    def body(dh_ref, wd_ref, wu_ref, a_ref, h_ref, g_ref, dh1_ref, dg_ref, da_ref):
        i = pl.program_id(0)
        dh = dh_ref[...]
        dhb = _mx(dh)
        acc = None
        for j in range(ff // tf):
            cols = slice(j * tf, (j + 1) * tf)
            dact = _dot_nt(dhb, wd_ref[cols, :])
            da = (dact * (2.0 * jnp.maximum(a_ref[:, cols].astype(f32), 0.0))).astype(da_ref.dtype)
            da_ref[:, cols] = da
            part = _dot_nt(da, wu_ref[:, cols])
            acc = part if acc is None else acc + part
        dx, dgrow = _rmsnorm_bwd(acc, h_ref[...], g_ref[...])
        dh1_ref[...] = dh + dx
        dgsum = jnp.sum(dgrow, axis=0, keepdims=True)

        @pl.when(i == 0)
        def _():
            dg_ref[...] = dgsum

        @pl.when(i > 0)
        def _():
            dg_ref[...] += dgsum

    row = lambda i: (i, 0)
    return pl.pallas_call(
        body, name=name, grid=(s_len // TM,),
        in_specs=[pl.BlockSpec((TM, d), row), _resident((None, ff, d), lambda i: (layer, 0, 0)),
                  _resident((None, d, ff), lambda i: (layer, 0, 0)), pl.BlockSpec((TM, ff), row),
                  pl.BlockSpec((TM, d), row), pl.BlockSpec((1, d), lambda i: (0, 0))],
        out_specs=[pl.BlockSpec((TM, d), row), pl.BlockSpec((1, d), lambda i: (0, 0)), pl.BlockSpec((TM, ff), row)],
        out_shape=[jax.ShapeDtypeStruct((s_len, d), f32), jax.ShapeDtypeStruct((1, d), f32),
                   jax.ShapeDtypeStruct((s_len, ff), MXU_DTYPE)],
        compiler_params=_cparams(("arbitrary",), vmem=VMEM_LIMIT_LARGE),
    )(dh2, w_down, w_up, a, h1, g)


def _weight_grad(a, b, name, act=False, lead=None):
    s_len, k_dim = a.shape[-2:]
    n = b.shape[1]
    tka = min(k_dim, 2048)
    tnb = n if n <= 1024 else (2048 if n % 2048 == 0 else 640)
    tm = 2 * TM_WGRAD
    vmem = 2 * tm * (tka * a.dtype.itemsize + tnb * b.dtype.itemsize) + tka * tnb * (4 + 2 * jnp.dtype(COMM_DTYPE).itemsize)
    if vmem > VMEM_LIMIT - VMEM_COMPILER_RESERVE or s_len % tm:
        tm = TM_WGRAD
    ns = s_len // tm

    def body(a_ref, b_ref, o_ref, acc_ref):
        s = pl.program_id(2)
        x = a_ref[...]
        if act:
            relu = jnp.maximum(x, 0.0)
            x = relu * relu

        @pl.when(s == 0)
        def _():
            acc_ref[...] = jnp.zeros_like(acc_ref)

        acc_ref[...] += _dot_tn(_mx(x), _mx(b_ref[...]))

        @pl.when(s == ns - 1)
        def _():
            o_ref[...] = acc_ref[...].astype(o_ref.dtype)

    if lead is None:
        a_spec = pl.BlockSpec((tm, tka), lambda i, j, s: (s, i))
    else:
        a_spec = pl.BlockSpec((None, tm, tka), lambda i, j, s: (lead, s, i))
    return pl.pallas_call(
        body, name=name, grid=(k_dim // tka, n // tnb, ns),
        in_specs=[a_spec, pl.BlockSpec((tm, tnb), lambda i, j, s: (s, j))],
        out_specs=pl.BlockSpec((tka, tnb), lambda i, j, s: (i, j)),
        out_shape=jax.ShapeDtypeStruct((k_dim, n), COMM_DTYPE),
        scratch_shapes=[pltpu.VMEM((tka, tnb), f32)],
        compiler_params=_cparams(("parallel", "parallel", "arbitrary")),
    )(a, b)


def _group_select(lane, x2, x4, x8, x16):
    grp = lane // POOL_GC
    return jnp.where(grp == 0, x2, jnp.where(grp == 1, x4, jnp.where(grp == 2, x8, x16)))


def _pool_window(lane):
    grp = lane // POOL_GC
    return jnp.where(grp == 0, 2, jnp.where(grp == 1, 4, jnp.where(grp == 2, 8, 16)))


def _pool_y(u, halo, i):
    xs = jnp.concatenate([jnp.where(i > 0, halo, 0.0), u], axis=0)
    s2 = xs + pltpu.roll(xs, 1, axis=0)
    s4 = s2 + pltpu.roll(s2, 2, axis=0)
    s8 = s4 + pltpu.roll(s4, 4, axis=0)
    s16 = s8 + pltpu.roll(s8, 8, axis=0)
    lane = lax.broadcasted_iota(jnp.int32, xs.shape, 1)
    sel = _group_select(lane, s2, s4, s8, s16)[HALO:, :]
    t = i * TM + lax.broadcasted_iota(jnp.int32, u.shape, 0)
    cnt = jnp.minimum(_pool_window(lax.broadcasted_iota(jnp.int32, u.shape, 1)), t + 1).astype(f32)
    return sel / cnt - u


def _group_weights(l0, l1, l2):
    mx = jnp.maximum(jnp.maximum(l0, l1), l2)
    e0, e1, e2 = jnp.exp(l0 - mx), jnp.exp(l1 - mx), jnp.exp(l2 - mx)
    den = e0 + e1 + e2
    return e0 / den, e1 / den, e2 / den


def _mixer_out_proj(z, wbd, scale, outs, lses, w_out, layer, h, name):
    s_len, d = h.shape

    def body(u_ref, halo_ref, wbd_ref, sc_ref, o0, o1, o2, l0, l1, l2, wo_ref, h_ref, m_ref, h1_ref):
        i = pl.program_id(0)
        y = _pool_y(u_ref[...], halo_ref[...], i)
        pool = _dot(_mx(y), wbd_ref[...]) * sc_ref[...]
        w0, w1, w2 = _group_weights(l0[...], l1[...], l2[...])
        m = jnp.concatenate([pool, o0[...] * w0, o1[...] * w1, o2[...] * w2], axis=1).astype(m_ref.dtype)
        m_ref[...] = m
        h1_ref[...] = h_ref[...] + _dot(m, wo_ref[...])

    row = lambda i: (i, 0)
    blk = pl.BlockSpec((TM, 256), row)
    grp = [pl.BlockSpec((TM, 256), lambda i, g=g: (i, g)) for g in range(3)]
    return pl.pallas_call(
        body, name=name, grid=(s_len // TM,),
        in_specs=[blk, pl.BlockSpec((HALO, 256), lambda i: (jnp.maximum(i * (TM // HALO) - 1, 0), 0)),
                  pl.BlockSpec((256, 256), lambda i: (0, 0)), pl.BlockSpec((1, 256), lambda i: (0, 0))] + grp + grp
        + [_resident((None, d, d), lambda i: (layer, 0, 0)), pl.BlockSpec((TM, d), row)],
        out_specs=[pl.BlockSpec((TM, d), row)] * 2,
        out_shape=[jax.ShapeDtypeStruct((s_len, d), MXU_DTYPE), jax.ShapeDtypeStruct((s_len, d), f32)],
        compiler_params=_cparams(("parallel",)),
    )(z, z, wbd, scale, outs, outs, outs, lses, lses, lses, w_out, h)


def _head_sums(x):
    r = lax.broadcasted_iota(jnp.int32, (256, 256), 0) // HEAD_DIM
    c = lax.broadcasted_iota(jnp.int32, (256, 256), 1) // HEAD_DIM
    ones = jnp.where(r == c, 1.0, 0.0).astype(jnp.bfloat16)
    hi = x.astype(jnp.bfloat16)
    lo = (x - hi.astype(f32)).astype(jnp.bfloat16)
    return _dot(hi, ones) + _dot(lo, ones)


def _out_combine_bwd(dh1, w_out, layer, outs, lses, name, after=None):
    s_len, d = dh1.shape

    def body(dh_ref, w_ref, o0, o1, o2, l0, l1, l2, *rest):
        dp_ref, do_ref, dl_ref = rest[-3:]
        dm = _dot_nt(_mx(dh_ref[...]), w_ref[...])
        dp_ref[...] = dm[:, :POOL_WIDTH]
        w = _group_weights(l0[...], l1[...], l2[...])
        da = [dm[:, POOL_WIDTH + 256 * g:POOL_WIDTH + 256 * (g + 1)] for g in range(3)]
        o = (o0[...], o1[...], o2[...])
        dw = [_head_sums(da[g] * o[g]) for g in range(3)]
        t = w[0] * dw[0] + w[1] * dw[1] + w[2] * dw[2]
        do_ref[...] = jnp.concatenate([da[g] * w[g] for g in range(3)], axis=1)
        dl_ref[...] = jnp.concatenate([w[g] * t for g in range(3)], axis=1)

    grp = [pl.BlockSpec((TM, 256), lambda i, g=g: (i, g)) for g in range(3)]
    in_specs = [pl.BlockSpec((TM, d), lambda i: (i, 0)), _resident((None, d, d), lambda i: (layer, 0, 0))] + grp + grp
    args = [dh1, w_out, outs, outs, outs, lses, lses, lses]
    if after is not None:
        in_specs.append(pl.BlockSpec(memory_space=pl.ANY))
        args.append(after)
    return pl.pallas_call(
        body, name=name, grid=(s_len // TM,), in_specs=in_specs,
        out_specs=[pl.BlockSpec((TM, POOL_WIDTH), lambda i: (i, 0))] + [pl.BlockSpec((TM, ATTN_WIDTH), lambda i: (i, 0))] * 2,
        out_shape=[jax.ShapeDtypeStruct((s_len, POOL_WIDTH), f32)] + [jax.ShapeDtypeStruct((s_len, ATTN_WIDTH), f32)] * 2,
        compiler_params=_cparams(("parallel",)),
    )(*args)


def _pool_bwd(z, dm, wbd, scale, name, after=None):
    s_len = z.shape[0]
    n_halo = s_len // HALO

    def body(u_ref, uh_ref, d_ref, dh_ref, wbd_ref, sc_ref, *rest):
        du_ref, dw_ref, dsc_ref = rest[-3:]
        i = pl.program_id(0)
        last = pl.num_programs(0) - 1
        y = _pool_y(u_ref[...], uh_ref[...], i)
        yb = _mx(y)
        dpo = d_ref[...]
        sc = sc_ref[...]
        dsc = jnp.sum(dpo * _dot(yb, wbd_ref[...]), axis=0, keepdims=True)
        dwp = _dot_tn(yb, _mx(dpo * sc))

        @pl.when(i == 0)
        def _():
            dsc_ref[...] = dsc
            dw_ref[...] = dwp

        @pl.when(i > 0)
        def _():
            dsc_ref[...] += dsc
            dw_ref[...] += dwp

        ext = jnp.concatenate([dpo, jnp.where(i < last, dh_ref[...], 0.0)], axis=0)
        dy = _dot_nt(_mx(ext * sc), wbd_ref[...])
        t = i * TM + lax.broadcasted_iota(jnp.int32, ext.shape, 0)
        lane = lax.broadcasted_iota(jnp.int32, ext.shape, 1)
        e = dy / jnp.minimum(_pool_window(lane), t + 1).astype(f32)
        rows = ext.shape[0]
        f2 = e + pltpu.roll(e, rows - 1, axis=0)
        f4 = f2 + pltpu.roll(f2, rows - 2, axis=0)
        f8 = f4 + pltpu.roll(f4, rows - 4, axis=0)
        f16 = f8 + pltpu.roll(f8, rows - 8, axis=0)
        du_ref[...] = (_group_select(lane, f2, f4, f8, f16) - dy)[:TM, :].astype(du_ref.dtype)

    row = lambda i: (i, 0)
    blk = pl.BlockSpec((TM, 256), row)
    extra = [] if after is None else [after]
    return pl.pallas_call(
        body, name=name, grid=(s_len // TM,),
        in_specs=[blk, pl.BlockSpec((HALO, 256), lambda i: (jnp.maximum(i * (TM // HALO) - 1, 0), 0)),
                  blk, pl.BlockSpec((HALO, 256), lambda i: (jnp.minimum((i + 1) * (TM // HALO), n_halo - 1), 0)),
                  pl.BlockSpec((256, 256), lambda i: (0, 0)), pl.BlockSpec((1, 256), lambda i: (0, 0))]
        + [pl.BlockSpec(memory_space=pl.ANY)] * len(extra),
        out_specs=[blk, pl.BlockSpec((256, 256), lambda i: (0, 0)), pl.BlockSpec((1, 256), lambda i: (0, 0))],
        out_shape=[jax.ShapeDtypeStruct((s_len, N_IN), MXU_DTYPE), jax.ShapeDtypeStruct((256, 256), f32),
                   jax.ShapeDtypeStruct((1, 256), f32)],
        compiler_params=_cparams(("arbitrary",)),
    )(z, z, dm, dm, wbd, scale, *extra)


def _tri_masks():
    qi = lax.broadcasted_iota(jnp.int32, (BLK, BLK), 0)
    ki = lax.broadcasted_iota(jnp.int32, (BLK, BLK), 1)
    return qi >= ki, ki >= qi


ATTN_SUPER_PER_STEP = (8, 4, 1)
Q_COL, K_COL, V_COL = POOL_WIDTH // 128, (POOL_WIDTH + ATTN_WIDTH) // 128, (POOL_WIDTH + 2 * ATTN_WIDTH) // 128


def _rows(ref, start, dil):
    if dil == 1:
        return ref[pl.ds(start, BLK), :]
    return ref[pl.ds(start, BLK, stride=dil), :]


ATTN_BLOCKS_TOGETHER = 8


def _set_rows(ref, start, dil, val):
    if dil == 1:
        ref[pl.ds(start, BLK), :] = val
    else:
        ref[pl.ds(start, BLK, stride=dil), :] = val


def _attn_fwd(z, g, prev, name):
    s_len = z.shape[0]
    dil, m = DILATIONS[g], ATTN_SUPER_PER_STEP[g]
    sbr = BLK * dil
    rows = sbr * m

    def body(*refs):
        q_ref, kc_ref, kp_ref, vc_ref, vp_ref = refs[:5]
        o_ref, l_ref = refs[-2:]
        st = pl.program_id(0)
        low, up = _tri_masks()
        head0 = lax.broadcasted_iota(jnp.int32, (BLK, 128), 1) < HEAD_DIM
        blocks = [(sb, r) for sb in range(m) for r in range(dil)]
        together = ATTN_BLOCKS_TOGETHER // 2 if dil == DILATIONS[-1] else ATTN_BLOCKS_TOGETHER
        for g0 in range(0, len(blocks), together):
            grp = blocks[g0:g0 + together]
            loaded = []
            for sb, r in grp:
                base = sb * sbr + r
                if sb == 0:
                    kp, vp = _rows(kp_ref, r, dil), _rows(vp_ref, r, dil)
                else:
                    kp, vp = _rows(kc_ref, base - sbr, dil), _rows(vc_ref, base - sbr, dil)
                qs = _rows(q_ref, base, dil) * ATTN_SCALE
                loaded.append((_mx(jnp.where(head0, qs, 0.0)), _mx(jnp.where(head0, 0.0, qs)),
                               jnp.concatenate([_mx(kp), _mx(_rows(kc_ref, base, dil))], axis=0),
                               jnp.concatenate([_mx(vp), _mx(_rows(vc_ref, base, dil))], axis=0)))
            scores = [(_dot_nt(q0, k2), _dot_nt(q1, k2)) for q0, q1, k2, _ in loaded]
            soft = []
            for (sb, _), pair in zip(grp, scores):
                valid = jnp.concatenate([up & (st > 0) if sb == 0 else up, low], axis=1)
                heads = []
                for s in pair:
                    s = jnp.where(valid, s, NEG_BIG)
                    mx = jnp.max(s, axis=-1, keepdims=True)
                    e = jnp.exp(s - mx)
                    l = jnp.sum(e, axis=-1, keepdims=True)
                    heads.append((_mx(e), 1.0 / l, jnp.broadcast_to(mx + jnp.log(l), (BLK, 128))))
                soft.append(heads)
            for (sb, r), heads, (_, _, _, v2) in zip(grp, soft, loaded):
                base = sb * sbr + r
                _set_rows(o_ref, base, dil, jnp.where(head0, _dot(heads[0][0], v2) * heads[0][1],
                                                      _dot(heads[1][0], v2) * heads[1][1]))
                _set_rows(l_ref, base, dil, jnp.where(head0, heads[0][2], heads[1][2]))

    def cur(col):
        return pl.BlockSpec((rows, 128), lambda st, hp: (st, col + 2 * g + hp))

    def before(col):
        return pl.BlockSpec((sbr, 128), lambda st, hp: (jnp.maximum(st * m - 1, 0), col + 2 * g + hp))

    in_specs = [cur(Q_COL), cur(K_COL), before(K_COL), cur(V_COL), before(V_COL)]
    args = [z, z, z, z, z]
    aliases = {}
    if prev is not None:
        in_specs += [pl.BlockSpec(memory_space=pl.ANY)] * 2
        args += list(prev)
        aliases = {5: 0, 6: 1}
    return pl.pallas_call(
        body, name=name, grid=(s_len // rows, 2), in_specs=in_specs, out_specs=[cur(0), cur(0)],
        out_shape=[jax.ShapeDtypeStruct((s_len, ATTN_WIDTH), f32)] * 2, input_output_aliases=aliases,
        compiler_params=_cparams(("parallel", "parallel")),
    )(*args)


def _stack_heads(x, head0):
    return jnp.concatenate([_mx(jnp.where(head0, x, 0.0)), _mx(jnp.where(head0, 0.0, x))], axis=0)


def _head_rows(x):
    xt = x.T
    return jnp.concatenate([jnp.broadcast_to(xt[0:1, :], (BLK, BLK)),
                            jnp.broadcast_to(xt[HEAD_DIM:HEAD_DIM + 1, :], (BLK, BLK))], axis=0)


def _attn_bwd(z, do, lse, dlt, tabs, dz, g, name):
    s_len = z.shape[0]
    dil, m = DILATIONS[g], ATTN_SUPER_PER_STEP[g]
    sbr = BLK * dil
    rows = sbr * m
    nsteps = s_len // rows

    def body(q_ref, qn_ref, kc_ref, kp_ref, vc_ref, vp_ref, do_ref, don_ref, l_ref, ln_ref, d_ref, dn_ref,
             c_ref, s1_ref, s2_ref, dz_in, dz_ref, dq_buf, dk_buf, dv_buf, out_buf, sems):
        del dz_in
        st, hp = pl.program_id(0), pl.program_id(1)
        head0 = lax.broadcasted_iota(jnp.int32, (BLK, 128), 1) < HEAD_DIM
        key_i = lax.broadcasted_iota(jnp.int32, (2 * BLK, BLK), 0) & (BLK - 1)
        query_i = lax.broadcasted_iota(jnp.int32, (2 * BLK, BLK), 1)
        same_t, cross_t = query_i >= key_i, key_i >= query_i
        def load(r):
            keys, vals = [_stack_heads(_rows(kp_ref, r, dil), head0)], [_stack_heads(_rows(vp_ref, r, dil), head0)]
            qs, dos, lses, dlts = [], [], [], []
            for sb in range(m):
                base = sb * sbr + r
                keys.append(_stack_heads(_rows(kc_ref, base, dil), head0))
                vals.append(_stack_heads(_rows(vc_ref, base, dil), head0))
                qs.append(_mx(_rows(q_ref, base, dil)))
                dos.append(_mx(_rows(do_ref, base, dil)))
                lses.append(_head_rows(_rows(l_ref, base, dil)))
                dlts.append(_head_rows(_rows(d_ref, base, dil)))
            qs.append(_mx(_rows(qn_ref, r, dil)))
            dos.append(_mx(_rows(don_ref, r, dil)))
            lses.append(_head_rows(_rows(ln_ref, r, dil)))
            dlts.append(_head_rows(_rows(dn_ref, r, dil)))
            return keys, vals, qs, dos, lses, dlts

        def products(data):
            keys, vals, qs, dos, _, _ = data
            return ([(_dot_nt(keys[j + 1], qs[j]), _dot_nt(vals[j + 1], dos[j])) for j in range(m)],
                    [(_dot_nt(keys[j], qs[j]), _dot_nt(vals[j], dos[j])) for j in range(m + 1)])

        def finish(data, raw):
            lses, dlts = data[4], data[5]

            def one(pair, j, valid):
                p = jnp.where(valid, jnp.exp(pair[0] * ATTN_SCALE - lses[j]), 0.0)
                return _mx(p), _mx(p * (pair[1] - dlts[j]) * ATTN_SCALE)

            same = [one(raw[0][j], j, same_t) for j in range(m)]
            cross = [one(raw[1][j], j, cross_t & (st > 0) if j == 0 else
                         (cross_t & (st < nsteps - 1) if j == m else cross_t)) for j in range(m + 1)]
            return same, cross

        def gradients(r, data, fin):
            keys, _, qs, dos, _, _ = data
            same, cross = fin
            for sb in range(m):
                base = sb * sbr + r
                (p_a, ds_a), (_, ds_x), (p_n, ds_n) = same[sb], cross[sb], cross[sb + 1]
                dq = _dot_tn(ds_a, keys[sb + 1]) + _dot_tn(ds_x, keys[sb])
                dk2 = _dot(ds_a, qs[sb]) + _dot(ds_n, qs[sb + 1])
                dv2 = _dot(p_a, dos[sb]) + _dot(p_n, dos[sb + 1])
                _set_rows(dq_buf, base, dil, dq)
                _set_rows(dk_buf, base, dil, jnp.where(head0, dk2[:BLK], dk2[BLK:]))
                _set_rows(dv_buf, base, dil, jnp.where(head0, dv2[:BLK], dv2[BLK:]))

        def residue_group(rg, carry):
            rs = [rg * group + i for i in range(group)]
            data = [load(r) for r in rs]
            raws = [products(d) for d in data]
            fins = [finish(d, raw) for d, raw in zip(data, raws)]
            for r, d, fin in zip(rs, data, fins):
                gradients(r, d, fin)
            return carry

        group = max(1, min(dil, ATTN_BLOCKS_TOGETHER // m))
        if dil // group <= 2:
            for rg in range(dil // group):
                residue_group(rg, 0)
        else:
            lax.fori_loop(0, dil // group, residue_group, 0)
        copies = []
        for t, (buf, col) in enumerate(((dq_buf, Q_COL), (dk_buf, K_COL), (dv_buf, V_COL))):
            val = buf[...]
            if t < 2:
                val = _rope_transpose(val, c_ref[...], s1_ref[...], s2_ref[...], 128)
            out_buf[t] = val.astype(out_buf.dtype)
            lane0 = pl.multiple_of((col + 2 * g + hp) * 128, 128)
            dst = dz_ref.at[pl.ds(pl.multiple_of(st * rows, rows), rows), pl.ds(lane0, 128)]
            cp = pltpu.make_async_copy(out_buf.at[t], dst, sems.at[t])
            cp.start()
            copies.append(cp)
        for cp in copies:
            cp.wait()

    def cur(col):
        return pl.BlockSpec((rows, 128), lambda st, hp: (st, col + 2 * g + hp))

    def before(col):
        return pl.BlockSpec((sbr, 128), lambda st, hp: (jnp.maximum(st * m - 1, 0), col + 2 * g + hp))

    def after(col):
        return pl.BlockSpec((sbr, 128), lambda st, hp: (jnp.minimum((st + 1) * m, s_len // sbr - 1), col + 2 * g + hp))

    tab = pl.BlockSpec((rows, 128), lambda st, hp: (st, 0))
    return pl.pallas_call(
        body, name=name, grid=(nsteps, 2),
        in_specs=[cur(Q_COL), after(Q_COL), cur(K_COL), before(K_COL), cur(V_COL), before(V_COL),
                  cur(0), after(0), cur(0), after(0), cur(0), after(0), tab, tab, tab,
                  pl.BlockSpec(memory_space=pl.ANY)],
        out_specs=pl.BlockSpec(memory_space=pl.ANY),
        out_shape=jax.ShapeDtypeStruct(dz.shape, dz.dtype), input_output_aliases={15: 0},
        scratch_shapes=[pltpu.VMEM((rows, 128), f32)] * 3 + [pltpu.VMEM((3, rows, 128), dz.dtype),
                                                            pltpu.SemaphoreType.DMA((3,))],
        compiler_params=_cparams(("arbitrary", "arbitrary")),
    )(z, z, z, z, z, z, do, do, lse, lse, dlt, dlt, *tabs, dz)


def _rope_tables(positions):
    inv_freq = ROPE_THETA ** (-jnp.arange(0, ROT_DIM, 2, dtype=f32) / ROT_DIM)
    ang = positions.astype(f32)[:, None] * inv_freq
    cos, sin = jnp.cos(ang), jnp.sin(ang)
    s_len = positions.shape[0]
    zero8, rest = jnp.zeros((s_len, 8), f32), jnp.zeros((s_len, HEAD_DIM - ROT_DIM), f32)
    c = jnp.concatenate([cos, cos, jnp.ones((s_len, HEAD_DIM - ROT_DIM), f32)], axis=1)
    s1 = jnp.concatenate([-sin, zero8, rest], axis=1)
    s2 = jnp.concatenate([zero8, sin, rest], axis=1)
    return tuple(jnp.tile(t, (1, 2)) for t in (c, s1, s2))


def _block_diag(pool_w):
    out = jnp.zeros((POOL_WIDTH, POOL_WIDTH), pool_w.dtype)
    for g in range(4):
        out = lax.dynamic_update_slice(out, pool_w[g], (g * POOL_GC, g * POOL_GC))
    return out


def _layer_fwd(h, p, wsrc, small, layer, tabs, head=None):
    nm = f"l{layer}_"
    wts, wl = wsrc.take(layer, ("w_in",), (h,) if layer else tuple(tabs))
    z, hn1 = _norm_matmul(h, small["norm1"][layer][None], wts["w_in"], wl, 256, nm + "in_proj", rope=tabs)
    ol = None
    for g in range(3):
        ol = _attn_fwd(z, g, ol, nm + f"attn_fwd{g}")
    outs, lses = ol
    wbd = _mx(_block_diag(small["pool_w"][layer]))
    scale = small["pool_scale"][layer][None]
    wts.update(wsrc.take(layer, ("w_out",), (outs,))[0])
    m, h1 = _mixer_out_proj(z, wbd, scale, outs, lses, wts["w_out"], wl, h, nm + "mixer_out")
    wts.update(wsrc.take(layer, ("w_up", "w_down"), (h1,))[0])
    h2, a, hn2 = _mlp_fwd(h1, small["norm2"][layer][None], wts["w_up"], wts["w_down"], wl, nm + "mlp")
    wts.update(wsrc.take(layer, ("w_gate", "w_ple"), (h2,))[0])
    *h3, gl, hn3 = _gate_ple_fwd(h2, small["norm3"][layer][None], wts["w_gate"], wts["w_ple"], wl, p, layer,
                                 nm + "gate_ple", head=head)
    saved = dict(h=h, z=z, hn1=hn1, outs=outs, lses=lses, wbd=wbd, scale=scale, m=m, h1=h1, a=a, hn2=hn2, h2=h2,
                 gl=gl, hn3=hn3, wts=wts, wl=wl)
    return h3, saved


def _layer_bwd(dh3, sv, p, small, layer, tabs128, reducer):
    nm = f"l{layer}_"
    wts, wl = sv["wts"], sv["wl"]
    dh2, dg3, de, dgl = _gate_bwd(dh3, sv["gl"], p, layer, wts["w_ple"], wts["w_gate"], wl, sv["h2"],
                                  small["norm3"][layer][None], nm + "gate_bwd")
    reducer.add("w_gate", layer, _weight_grad(sv["hn3"], dgl, nm + "dw_gate"))
    reducer.add("w_ple", layer, _weight_grad(p, de, nm + "dw_ple", lead=layer))
    dh1, dg2, da = _mlp_bwd(dh2, wts["w_down"], wts["w_up"], wl, sv["a"], sv["h1"], small["norm2"][layer][None],
                            nm + "mlp_bwd")
    reducer.add("w_down", layer, _weight_grad(sv["a"], dh2, nm + "dw_down", act=True))
    started = reducer.add("w_up", layer, _weight_grad(sv["hn2"], da, nm + "dw_up"))
    dpool, do, dlt = _out_combine_bwd(dh1, wts["w_out"], wl, sv["outs"], sv["lses"], nm + "out_bwd", after=started)
    started = reducer.add("w_out", layer, _weight_grad(sv["m"], dh1, nm + "dw_out"))
    dz, dwbd, dscale = _pool_bwd(sv["z"], dpool, sv["wbd"], sv["scale"], nm + "pool_bwd", after=started)
    for g in range(3):
        dz = _attn_bwd(sv["z"], do, sv["lses"], dlt, tabs128, dz, g, nm + f"attn_bwd{g}")
    started = reducer.add("w_in", layer, _weight_grad(sv["hn1"], dz, nm + "dw_in"))
    dh0, dg1 = _matmul_nt_norm_bwd(dz, wts["w_in"], wl, sv["h"], small["norm1"][layer][None], dh1, nm + "in_bwd",
                                   tk=512, after=started)
    dpool_w = jnp.stack([dwbd[g * POOL_GC:(g + 1) * POOL_GC, g * POOL_GC:(g + 1) * POOL_GC] for g in range(4)])
    sg = dict(norm1=dg1[0], norm2=dg2[0], norm3=dg3[0], pool_w=dpool_w, pool_scale=dscale[0])
    return dh0, sg


def _local_step(x, p, positions, wsrc, small, target, reducer):
    tabs128 = _rope_tables(positions)
    (h,), sv0 = _layer_fwd(x, p, wsrc, small, 0, tabs128)
    (loss, dh, dgf), sv1 = _layer_fwd(h, p, wsrc, small, 1, tabs128, head=(small["final_norm"][None], target))
    saved = [sv0, sv1]
    sgs = [None, None]
    for layer in (1, 0):
        dh, sgs[layer] = _layer_bwd(dh, saved[layer], p, small, layer, tabs128, reducer)
    small_grads = {k: jnp.stack([sgs[0][k], sgs[1][k]]) for k in sgs[0]}
    small_grads["final_norm"] = dgf[0]
    return loss, dh, small_grads


HBM = pl.BlockSpec(memory_space=pltpu.HBM)


def _my_place():
    return lax.axis_index("x"), lax.axis_index("y"), lax.axis_index("c")


def _other_chips(x, y):
    return [(1 - x, y), (x, 1 - y), (1 - x, 1 - y)]


def _window(ref, name, chip):
    k, n = _shard_shape(name)
    if COL_SHARDED[name]:
        return ref.at[:, pl.ds(pl.multiple_of(chip * n, 128), n)]
    return ref.at[pl.ds(pl.multiple_of(chip * k, 128), k), :]


def _chip_index():
    return jnp.reshape(2 * lax.axis_index("x") + lax.axis_index("y"), (1,)).astype(jnp.int32)


def _shard_block(name, tr):
    ks, ns = _shard_shape(name)
    if COL_SHARDED[name]:
        return (tr, ns), lambda i, me: (i, me[0])
    return (tr, ns), lambda i, me: (me[0] * (ks // tr) + i, 0)


def _place_shard(w, name, layer):
    ks, ns = _shard_shape(name)
    tr = min(ks, 256)
    shape, index = _shard_block(name, tr)

    def body(me_ref, w_ref, o_ref):
        o_ref[...] = w_ref[...].astype(o_ref.dtype)

    return pl.pallas_call(
        body, name=f"place_{name}{layer}",
        grid_spec=pltpu.PrefetchScalarGridSpec(
            num_scalar_prefetch=1, grid=(ks // tr,),
            in_specs=[pl.BlockSpec((None, tr, ns), lambda i, me: (layer, i, 0))],
            out_specs=pl.BlockSpec((None,) + shape, lambda i, me: (0,) + index(i, me))),
        out_shape=jax.ShapeDtypeStruct((1,) + FULL_SHAPE[name], MXU_DTYPE),
        compiler_params=_cparams(("parallel",)),
    )(_chip_index(), w)


GATHER_ORDER = [("w_in", 0), ("w_out", 0), ("w_up", 0), ("w_down", 0), ("w_gate", 0), ("w_ple", 0),
                ("w_in", 1), ("w_out", 1), ("w_up", 1), ("w_down", 1), ("w_gate", 1), ("w_ple", 1)]
SEM = pl.BlockSpec(memory_space=pltpu.SEMAPHORE)
EFFECT = pltpu.SideEffectType.DATAFLOW_SIDE_EFFECTING


def _gather_copy(src_ref, dst_ref, name, idx, j, chip, send_sems, recv_sems, c):
    cx, cy = chip
    return pltpu.make_async_remote_copy(
        src_ref=src_ref, dst_ref=dst_ref, send_sem=send_sems.at[3 * idx + j], recv_sem=recv_sems.at[3 * idx + j],
        device_id=(cx, cy, c), device_id_type=MESH)


def _gather_start(placed, order, tag, after=None):
    n = len(order)
    extra = [] if after is None else [after]

    def body(*refs):
        ins = refs[:n]
        k = n + len(extra)
        send_sems, recv_sems = refs[k], refs[k + 1]
        outs = refs[k + 2:k + 2 + n]
        token = refs[-1]
        x, y, c = _my_place()
        me = 2 * x + y
        for idx, (name, _) in enumerate(order):
            for j, chip in enumerate(_other_chips(x, y)):
                _gather_copy(_window(ins[idx].at[0], name, me), _window(outs[idx].at[0], name, me), name, idx, j, chip,
                             send_sems, recv_sems, c).start()
        token[...] = jnp.zeros_like(token)

    res = pl.pallas_call(
        body, name="gather_start" + tag,
        out_shape=(pltpu.SemaphoreType.DMA((3 * n,)), pltpu.SemaphoreType.DMA((3 * n,)))
        + tuple(pltpu.HBM(a.shape, a.dtype) for a in placed) + (jax.ShapeDtypeStruct((8, 128), f32),),
        in_specs=[HBM] * n + [pl.BlockSpec(memory_space=pl.ANY)] * len(extra),
        out_specs=(SEM, SEM) + (HBM,) * n + (pl.BlockSpec(memory_space=pltpu.VMEM),),
        input_output_aliases={i: i + 2 for i in range(n)},
        compiler_params=pltpu.CompilerParams(has_side_effects=EFFECT),
    )(*[pltpu.with_memory_space_constraint(a, pltpu.HBM) for a in placed], *extra)
    return res[0], res[1], list(res[2:2 + n]), res[-1]


def _gather_wait(send_sems, recv_sems, arrays, order, idxs, after, name):
    n = len(idxs)

    def body(*refs):
        ins = refs[:n]
        send_ref, recv_ref = refs[n], refs[n + 1]
        x, y, c = _my_place()
        me = 2 * x + y
        for k, idx in enumerate(idxs):
            wname = order[idx][0]
            for j, chip in enumerate(_other_chips(x, y)):
                cx, cy = chip
                mine = _window(ins[k].at[0], wname, me)
                land = _window(ins[k].at[0], wname, 2 * cx + cy)
                _gather_copy(mine, mine, wname, idx, j, chip, send_ref, recv_ref, c).wait_send()
                _gather_copy(land, land, wname, idx, j, chip, send_ref, recv_ref, c).wait_recv()

    operands = list(arrays) + [send_sems, recv_sems] + list(after)
    in_specs = [HBM] * n + [SEM, SEM] + [pl.BlockSpec(memory_space=pl.ANY)] * len(after)
    res = pl.pallas_call(
        body, name=name, out_shape=tuple(pltpu.HBM(a.shape, a.dtype) for a in arrays),
        in_specs=in_specs, out_specs=(HBM,) * n, input_output_aliases={i: i for i in range(n)},
        compiler_params=pltpu.CompilerParams(has_side_effects=EFFECT),
    )(*operands)
    return list(res)


class _GatheredWeights:
    def __init__(self, shards):
        self.starts = []
        token = None
        for tag, order in (("_first", GATHER_ORDER[:1]), ("_rest", GATHER_ORDER[1:])):
            placed = [_place_shard(shards[name], name, layer) for name, layer in order]
            self.starts.append((order,) + _gather_start(placed, order, tag, token))
            token = self.starts[-1][-1]

    def take(self, layer, names, after):
        order, send, recv, arrays, _ = next(s for s in self.starts if (names[0], layer) in s[0])
        after = list(after)
        if order is self.starts[0][0]:
            after.append(self.starts[-1][-1])
        idxs = [order.index((n, layer)) for n in names]
        got = _gather_wait(send, recv, [arrays[i] for i in idxs], order, idxs, after, f"gather_wait{layer}_{names[0]}")
        return dict(zip(names, got)), 0


N_DEV = 8


def _reduce_copies(dws, lands, names, layer, send_sems, recv_sems):
    x, y, c = _my_place()
    me, my_dev = 2 * x + y, 4 * x + 2 * y + c
    out = []
    for t, name in enumerate(names):
        for j, (cx, cy) in enumerate(_other_chips(x, y)):
            out.append((pltpu.make_async_remote_copy(
                src_ref=_window(dws[t], name, 2 * cx + cy), dst_ref=lands[t].at[my_dev],
                send_sem=send_sems.at[4 * t + j], recv_sem=recv_sems.at[N_DEV * t + my_dev],
                device_id=(cx, cy, layer), device_id_type=MESH), False))
        out.append((pltpu.make_async_remote_copy(
            src_ref=_window(dws[t], name, me), dst_ref=lands[t].at[my_dev],
            send_sem=send_sems.at[4 * t + 3], recv_sem=recv_sems.at[N_DEV * t + my_dev],
            device_id=(x, y, layer), device_id_type=MESH), True))
    return out


def _reduce_start(dws, names, layer, tag):
    n = len(names)
    lands = [lax.empty((N_DEV,) + _shard_shape(nm), dws[0].dtype) for nm in names]

    def body(*refs):
        ins = refs[:n]
        send_sems, recv_sems = refs[2 * n], refs[2 * n + 1]
        land_out = refs[3 * n + 2:4 * n + 2]
        token = refs[-1]
        c = lax.axis_index("c")
        for cp, non_owner_only in _reduce_copies(ins, land_out, names, layer, send_sems, recv_sems):
            if non_owner_only:
                @pl.when(c != layer)
                def _():
                    cp.start()
            else:
                cp.start()
        token[...] = jnp.zeros_like(token)

    res = pl.pallas_call(
        body, name="reduce_start" + tag,
        out_shape=(pltpu.SemaphoreType.DMA((4 * n,)), pltpu.SemaphoreType.DMA((N_DEV * n,)))
        + tuple(pltpu.HBM(a.shape, a.dtype) for a in dws) + tuple(pltpu.HBM(a.shape, a.dtype) for a in lands)
        + (jax.ShapeDtypeStruct((8, 128), f32),),
        in_specs=[HBM] * (2 * n),
        out_specs=(SEM, SEM) + (HBM,) * (2 * n) + (pl.BlockSpec(memory_space=pltpu.VMEM),),
        input_output_aliases={i: i + 2 for i in range(2 * n)},
        compiler_params=pltpu.CompilerParams(has_side_effects=EFFECT),
    )(*[pltpu.with_memory_space_constraint(a, pltpu.HBM) for a in list(dws) + lands])
    return res[0], res[1], list(res[2:2 + n]), list(res[2 + n:2 + 2 * n]), res[-1]


def _reduce_wait(send_sems, recv_sems, dws, lands, names, layer, after, tag):
    n = len(names)

    def body(*refs):
        ins, land_in = refs[:n], refs[n:2 * n]
        send_ref, recv_ref = refs[2 * n], refs[2 * n + 1]
        x, y, c = _my_place()
        for cp, non_owner_only in _reduce_copies(ins, land_in, names, layer, send_ref, recv_ref):
            if non_owner_only:
                @pl.when(c != layer)
                def _():
                    cp.wait_send()
            else:
                cp.wait_send()

        @pl.when(c == layer)
        def _():
            for t in range(n):
                for k in range(1, N_DEV):
                    px, py, pc = x ^ ((k >> 2) & 1), y ^ ((k >> 1) & 1), c ^ (k & 1)
                    dev = 4 * px + 2 * py + pc
                    land = land_in[t].at[dev]
                    pltpu.make_async_remote_copy(
                        src_ref=land, dst_ref=land, send_sem=send_ref.at[4 * t], recv_sem=recv_ref.at[N_DEV * t + dev],
                        device_id=(px, py, pc), device_id_type=MESH).wait_recv()

    res = pl.pallas_call(
        body, name="reduce_wait" + tag,
        out_shape=tuple(pltpu.HBM(a.shape, a.dtype) for a in list(dws) + list(lands)),
        in_specs=[HBM] * (2 * n) + [SEM, SEM, pl.BlockSpec(memory_space=pl.ANY)], out_specs=(HBM,) * (2 * n),
        input_output_aliases={i: i for i in range(2 * n)},
        compiler_params=pltpu.CompilerParams(has_side_effects=EFFECT),
    )(*dws, *lands, send_sems, recv_sems, after)
    return list(res[:n]), list(res[n:])


def _sum_devices(land, own, name, layer, prev):
    ks, ns = _shard_shape(name)
    tr = min(ks, 256)
    shape, index = _shard_block(name, tr)

    def body(me_ref, dev_ref, *refs):
        s_ref, own_ref, out_ref = refs[0], refs[1], refs[-1]
        dev = dev_ref[0]
        acc = None
        for s in range(N_DEV):
            term = jnp.where(dev == s, own_ref[...], s_ref[s]).astype(f32)
            acc = term if acc is None else acc + term
        out_ref[...] = acc

    def mine(i, dev):
        return i * jnp.where((dev[0] & 1) == layer, 1, 0)

    in_specs = [pl.BlockSpec((N_DEV, tr, ns), lambda i, me, dev: (0, mine(i, dev), 0)),
                pl.BlockSpec(shape, lambda i, me, dev: index(mine(i, dev), me))]
    args = [land, own]
    aliases = {}
    if prev is not None:
        in_specs.append(pl.BlockSpec(memory_space=pl.ANY))
        args.append(prev)
        aliases = {4: 0}
    x, y, c = _my_place()
    return pl.pallas_call(
        body, name=f"sum_devices_{name}{layer}",
        grid_spec=pltpu.PrefetchScalarGridSpec(
            num_scalar_prefetch=2, grid=(ks // tr,), in_specs=in_specs,
            out_specs=pl.BlockSpec((None, tr, ns), lambda i, me, dev: (layer, mine(i, dev), 0))),
        out_shape=jax.ShapeDtypeStruct((2, ks, ns), f32), input_output_aliases=aliases,
        compiler_params=_cparams(("arbitrary",)),
    )(_chip_index(), jnp.reshape(4 * x + 2 * y + c, (1,)).astype(jnp.int32), *args)


class _GradReducer:
    GROUPS = (("1", 1, ("w_gate", "w_ple", "w_down", "w_up", "w_out", "w_in")),
              ("0a", 0, ("w_gate", "w_ple", "w_down", "w_up")),
              ("0b", 0, ("w_out",)),
              ("0c", 0, ("w_in",)))

    def __init__(self):
        self.grads = {}
        self.started = {}

    def add(self, name, layer, dw):
        self.grads[(name, layer)] = dw
        token = None
        for tag, glayer, names in self.GROUPS:
            if tag not in self.started and all((nm, glayer) in self.grads for nm in names):
                *self.started[tag], token = _reduce_start([self.grads[(nm, glayer)] for nm in names], names, glayer, tag)
        return token

    def finish(self, after):
        mine = {}
        for tag, layer, names in self.GROUPS:
            send, recv, dws, lands = self.started[tag]
            dws, lands = _reduce_wait(send, recv, dws, lands, names, layer, after, tag)
            for nm, dw, land in zip(names, dws, lands):
                mine[nm] = _sum_devices(land, dw, nm, layer, mine.get(nm))
        return _pair_layers(mine)


def _pair_layers(mine):
    names = list(BIG)

    def body(*refs):
        ins = refs[:len(names)]
        outs = refs[len(names):2 * len(names)]
        send_sems, recv_sems = refs[2 * len(names):]
        x, y, c = _my_place()
        sibling = (x, y, 1 - c)
        cps = []
        for t in range(len(names)):
            cp = pltpu.make_async_remote_copy(
                src_ref=ins[t].at[c], dst_ref=outs[t].at[c], send_sem=send_sems.at[t], recv_sem=recv_sems.at[t],
                device_id=sibling, device_id_type=MESH)
            cp.start()
            cps.append(cp)
        for t in range(len(names)):
            cps[t].wait_send()
            land = outs[t].at[1 - c]
            pltpu.make_async_remote_copy(
                src_ref=land, dst_ref=land, send_sem=send_sems.at[t], recv_sem=recv_sems.at[t],
                device_id=sibling, device_id_type=MESH).wait_recv()

    outs = pl.pallas_call(
        body, name="pair_layers", in_specs=[HBM] * len(names), out_specs=[HBM] * len(names),
        out_shape=[jax.ShapeDtypeStruct((2,) + _shard_shape(n), f32) for n in names],
        input_output_aliases={t: t for t in range(len(names))},
        scratch_shapes=[pltpu.SemaphoreType.DMA((len(names),)), pltpu.SemaphoreType.DMA((len(names),))],
    )(*[mine[n] for n in names])
    return dict(zip(names, outs))


SMALL_ROWS = 320


def _small_copies(vec_ref, land_ref, send_sems, recv_sems):
    x, y, c = _my_place()
    me = 4 * x + 2 * y + c
    out = []
    for k in range(1, N_DEV):
        peer = (x ^ ((k >> 2) & 1), y ^ ((k >> 1) & 1), c ^ (k & 1))
        src_dev = 4 * peer[0] + 2 * peer[1] + peer[2]
        send = pltpu.make_async_remote_copy(
            src_ref=vec_ref, dst_ref=land_ref.at[me], send_sem=send_sems.at[k - 1], recv_sem=recv_sems.at[k - 1],
            device_id=peer, device_id_type=MESH)
        arrival = pltpu.make_async_remote_copy(
            src_ref=land_ref.at[src_dev], dst_ref=land_ref.at[src_dev], send_sem=send_sems.at[k - 1],
            recv_sem=recv_sems.at[k - 1], device_id=peer, device_id_type=MESH)
        out.append((send, arrival))
    return out


def _small_start(vec):
    land = lax.empty((N_DEV,) + vec.shape, vec.dtype)

    def body(v_ref, land_in, send_sems, recv_sems, v_out, land_out):
        del land_in, v_out
        for send, _ in _small_copies(v_ref, land_out, send_sems, recv_sems):
            send.start()

    return pl.pallas_call(
        body, name="small_start",
        out_shape=(pltpu.SemaphoreType.DMA((N_DEV - 1,)), pltpu.SemaphoreType.DMA((N_DEV - 1,)),
                   pltpu.HBM(vec.shape, vec.dtype), pltpu.HBM(land.shape, land.dtype)),
        in_specs=[HBM, HBM], out_specs=(SEM, SEM, HBM, HBM), input_output_aliases={0: 2, 1: 3},
        compiler_params=pltpu.CompilerParams(has_side_effects=EFFECT),
    )(pltpu.with_memory_space_constraint(vec, pltpu.HBM), pltpu.with_memory_space_constraint(land, pltpu.HBM))


def _small_wait(send_sems, recv_sems, vec, land, after):
    def body(v_ref, land_ref, send_ref, recv_ref, after_ref, v_out, land_out):
        del after_ref, v_out, land_out
        for send, arrival in _small_copies(v_ref, land_ref, send_ref, recv_ref):
            send.wait_send()
            arrival.wait_recv()

    return pl.pallas_call(
        body, name="small_wait", out_shape=(pltpu.HBM(vec.shape, vec.dtype), pltpu.HBM(land.shape, land.dtype)),
        in_specs=[HBM, HBM, SEM, SEM, pl.BlockSpec(memory_space=pl.ANY)], out_specs=(HBM, HBM),
        input_output_aliases={0: 0, 1: 1}, compiler_params=pltpu.CompilerParams(has_side_effects=EFFECT),
    )(vec, land, send_sems, recv_sems, after)


def _small_sum(vec, land):
    x, y, c = _my_place()

    def body(dev_ref, v_ref, land_ref, out_ref):
        acc = None
        for s in range(N_DEV):
            term = jnp.where(dev_ref[0] == s, v_ref[...], land_ref[s])
            acc = term if acc is None else acc + term
        out_ref[...] = acc

    return pl.pallas_call(
        body, name="small_sum",
        grid_spec=pltpu.PrefetchScalarGridSpec(
            num_scalar_prefetch=1, grid=(1,),
            in_specs=[pl.BlockSpec(vec.shape, lambda i, dev: (0, 0)), pl.BlockSpec(land.shape, lambda i, dev: (0, 0, 0))],
            out_specs=pl.BlockSpec(vec.shape, lambda i, dev: (0, 0))),
        out_shape=jax.ShapeDtypeStruct(vec.shape, vec.dtype),
        compiler_params=_cparams(("arbitrary",)),
    )(jnp.reshape(4 * x + 2 * y + c, (1,)).astype(jnp.int32), vec, land)


def _adamw(w, g, m, v, name):
    rows, cols = w.shape
    tr = rows
    for cand in (512, 256, 128, 64, 32, 16, 8):
        if rows % cand == 0 and cand * cols * 4 <= 2 * 1024 * 1024:
            tr = cand
            break
    c1 = np.float32(1.0 - ADAM_B1 ** ADAM_STEP)
    c2 = np.float32(1.0 - ADAM_B2 ** ADAM_STEP)

    def body(w_ref, g_ref, m_ref, v_ref, go_ref, d_ref, mo_ref, vo_ref):
        gv = g_ref[...]
        go_ref[...] = gv
        mn = ADAM_B1 * m_ref[...] + (1.0 - ADAM_B1) * gv
        vn = ADAM_B2 * v_ref[...] + (1.0 - ADAM_B2) * (gv * gv)
        mo_ref[...] = mn
        vo_ref[...] = vn
        d_ref[...] = -ADAM_LR * ((mn / c1) / (jnp.sqrt(vn / c2) + ADAM_EPS) + ADAM_WD * w_ref[...])

    blk = pl.BlockSpec((tr, cols), lambda i: (i, 0))
    return pl.pallas_call(
        body, name="adamw_" + name, grid=(rows // tr,), in_specs=[blk] * 4, out_specs=[blk] * 4,
        out_shape=[jax.ShapeDtypeStruct((rows, cols), f32)] * 4,
        compiler_params=_cparams(("parallel",)),
    )(w, g, m, v)


SMALL = ("norm1", "pool_w", "pool_scale", "norm2", "norm3", "final_norm")
ORDER = ("norm1", "w_in", "pool_w", "pool_scale", "w_out", "norm2", "w_up", "w_down", "norm3", "w_gate", "w_ple",
         "final_norm")


def _pack_small(tree, extra=None):
    parts = [tree[n].reshape(-1) for n in SMALL]
    if extra is not None:
        parts.append(extra.reshape(-1))
    flat = jnp.concatenate(parts)
    return jnp.pad(flat, (0, SMALL_ROWS * 128 - flat.shape[0])).reshape(SMALL_ROWS, 128)


def _unpack_small(packed, like):
    flat = packed.reshape(-1)
    out, off = {}, 0
    for n in SMALL:
        size = int(np.prod(like[n].shape))
        out[n] = flat[off:off + size].reshape(like[n].shape)
        off += size
    return out, flat[off]


def kernel(x, p, positions, norm1, w_in, pool_w, pool_scale, w_out, norm2, w_up, w_down, norm3, w_gate, w_ple, final_norm, loss_target, m_norm1, m_w_in, m_pool_w, m_pool_scale, m_w_out, m_norm2, m_w_up, m_w_down, m_norm3, m_w_gate, m_w_ple, m_final_norm, v_norm1, v_w_in, v_pool_w, v_pool_scale, v_w_out, v_norm2, v_w_up, v_w_down, v_norm3, v_w_gate, v_w_ple, v_final_norm):
    w = dict(norm1=norm1, w_in=w_in, pool_w=pool_w, pool_scale=pool_scale, w_out=w_out, norm2=norm2, w_up=w_up,
             w_down=w_down, norm3=norm3, w_gate=w_gate, w_ple=w_ple, final_norm=final_norm)
    m = dict(norm1=m_norm1, w_in=m_w_in, pool_w=m_pool_w, pool_scale=m_pool_scale, w_out=m_w_out, norm2=m_norm2,
             w_up=m_w_up, w_down=m_w_down, norm3=m_norm3, w_gate=m_w_gate, w_ple=m_w_ple, final_norm=m_final_norm)
    v = dict(norm1=v_norm1, w_in=v_w_in, pool_w=v_pool_w, pool_scale=v_pool_scale, w_out=v_w_out, norm2=v_norm2,
             w_up=v_w_up, w_down=v_w_down, norm3=v_norm3, w_gate=v_w_gate, w_ple=v_w_ple, final_norm=v_final_norm)
    small = {n: w[n] for n in SMALL}

    wsrc = _GatheredWeights({n: w[n] for n in BIG})
    reducer = _GradReducer()
    loss8, dx, small_grads = _local_step(x[0], p.reshape(2, x.shape[1], PLE_DIM), positions[0], wsrc, small, loss_target[0], reducer)
    s_send, s_recv, s_vec, s_land = _small_start(_pack_small(small_grads, loss8[0, 0]))
    gsh = reducer.finish(s_vec)

    g_out, d_out, m_out, v_out = {}, {}, {}, {}
    for n in BIG:
        shp = w[n].shape
        two = lambda a: a.reshape(shp[0] * shp[1], shp[2])
        g2, d2, m2, v2 = _adamw(two(w[n]), two(gsh[n]), two(m[n]), two(v[n]), n)
        g_out[n], d_out[n], m_out[n], v_out[n] = g2.reshape(shp), d2.reshape(shp), m2.reshape(shp), v2.reshape(shp)
    red = _small_sum(*_small_wait(s_send, s_recv, s_vec, s_land, d2))
    g_small, loss = _unpack_small(red, small)
    _, d2, m2, v2 = _adamw(_pack_small(small), red, _pack_small({n: m[n] for n in SMALL}),
                           _pack_small({n: v[n] for n in SMALL}), "small")
    for tree, packed in ((d_out, d2), (m_out, m2), (v_out, v2)):
        tree.update(_unpack_small(packed, small)[0])
    g_out.update(g_small)

    return (loss, dx[None], *[g_out[n] for n in ORDER], *[d_out[n] for n in ORDER], *[m_out[n] for n in ORDER],
            *[v_out[n] for n in ORDER])
```
